```python
import jax, jax.numpy as jnp
from jax import lax
import numpy as np

D_MODEL = 2048
BATCH = 8
SEQ = 2048
DEPTH = 2

CHUNK = 64
N_MIXERS = 2
EPS = 1e-6

GLA_HEADS = 4
GLA_DK = D_MODEL // 2
GLA_DV = D_MODEL
GLA_DK_HEAD = GLA_DK // GLA_HEADS
GLA_DV_HEAD = GLA_DV // GLA_HEADS
GLA_GATE_RANK = 16
GLA_TAU = 16.0
GLA_IN = 2 * GLA_DK + 2 * GLA_DV + GLA_GATE_RANK

SGU_WIDTH = D_MODEL
SGU_BLOCK = 128
SGU_GROUPS = 8
SGU_GROUP_DIM = SGU_WIDTH // SGU_GROUPS
SGU_IN = 3 * SGU_WIDTH

N_GLA_LAYERS = (DEPTH + 1) // 2
N_SGU_LAYERS = DEPTH // 2

kernel_name = "hybrid_gla_sgu_sandwich_trunk"


def rmsnorm(x, gain):
    xf = x.astype(jnp.float32)
    y = xf * lax.rsqrt(jnp.mean(xf * xf, axis=-1, keepdims=True) + EPS)
    return (y * gain.astype(jnp.float32)).astype(x.dtype)


def gla_mixer(h, w_in, w_gate2, b_gate, o_gain, w_out):
    B, S, _ = h.shape
    nc = S // CHUNK
    proj = h @ w_in
    q, k, v, g, glr = jnp.split(
        proj, [GLA_DK, 2 * GLA_DK, 2 * GLA_DK + GLA_DV, 2 * GLA_DK + 2 * GLA_DV], axis=-1)
    log_a = jax.nn.log_sigmoid((glr @ w_gate2 + b_gate).astype(jnp.float32)) / GLA_TAU

    def to_chunks(t, dh):
        return t.astype(jnp.float32).reshape(B, nc, CHUNK, GLA_HEADS, dh).transpose(1, 0, 3, 2, 4)

    qc = to_chunks(q, GLA_DK_HEAD) * (GLA_DK_HEAD ** -0.5)
    kc = to_chunks(k, GLA_DK_HEAD)
    vc = to_chunks(v, GLA_DV_HEAD)
    la = to_chunks(log_a, GLA_DK_HEAD)
    bcum = jnp.cumsum(la, axis=3)
    b_end = bcum[:, :, :, -1:, :]
    k_dec = kc * jnp.exp(b_end - bcum)
    decay = jnp.exp(b_end[:, :, :, 0, :])

    def step(state, xs):
        q_i, k_i, v_i, d_i = xs
        state = state * d_i[..., None] + jnp.einsum('bhck,bhcv->bhkv', k_i, v_i)
        return state, jnp.einsum('bhck,bhkv->bhcv', q_i, state)

    s0 = jnp.zeros((B, GLA_HEADS, GLA_DK_HEAD, GLA_DV_HEAD), jnp.float32)
    _, o = lax.scan(step, s0, (qc, k_dec, vc, decay))
    o = o.transpose(1, 0, 3, 2, 4).reshape(B, S, GLA_HEADS, GLA_DV_HEAD)
    o = o * lax.rsqrt(jnp.mean(o * o, axis=-1, keepdims=True) + EPS)
    o = o.reshape(B, S, GLA_DV) * o_gain.astype(jnp.float32)
    o = o.astype(h.dtype) * jax.nn.silu(g)
    return o @ w_out


def sgu_mixer(h, w_in, ln_gain, ln_bias, w_spatial, b_spatial, w_out):
    B, S, _ = h.shape
    nb = S // SGU_BLOCK
    proj = h @ w_in
    u, v, g = jnp.split(proj, 3, axis=-1)
    u = jax.nn.gelu(u)
    vf = jax.nn.gelu(v).astype(jnp.float32)
    mu = jnp.mean(vf, axis=-1, keepdims=True)
    var = jnp.mean(jnp.square(vf - mu), axis=-1, keepdims=True)
    vn = (vf - mu) * lax.rsqrt(var + EPS) * ln_gain.astype(jnp.float32) + ln_bias.astype(jnp.float32)
    vn = vn.reshape(B, nb, SGU_BLOCK, SGU_GROUPS, SGU_GROUP_DIM)
    pos_chunk = jnp.arange(SGU_BLOCK) // CHUNK
    mask = pos_chunk[:, None] >= pos_chunk[None, :]
    ws = jnp.where(mask[None], w_spatial, 0).astype(jnp.float32)
    vs = jnp.einsum('gij,bnjgd->bnigd', ws, vn) \
        + b_spatial.astype(jnp.float32).T[None, None, :, :, None]
    vs = vs.reshape(B, S, SGU_WIDTH).astype(h.dtype)
    return (u * vs * jax.nn.silu(g)) @ w_out


def _fwd_setup_inputs(seed: int = 0) -> dict:
    key = jax.random.key(seed)
    ks = jax.random.split(key, 15)
    f32 = jnp.float32
    nrm = lambda k, shape, scale: jax.random.normal(k, shape, f32) * scale
    return {
        "x": nrm(ks[0], (BATCH, SEQ, D_MODEL), 1.0),
        "norm_pre": 1.0 + nrm(ks[1], (DEPTH, D_MODEL), 0.02),
        "norm_post": 1.0 + nrm(ks[2], (DEPTH, D_MODEL), 0.02),
        "gla_w_in": nrm(ks[3], (N_GLA_LAYERS, D_MODEL, GLA_IN), D_MODEL ** -0.5),
        "gla_w_gate2": nrm(ks[4], (N_GLA_LAYERS, GLA_GATE_RANK, GLA_DK), GLA_GATE_RANK ** -0.5),
        "gla_b_gate": nrm(ks[5], (N_GLA_LAYERS, GLA_DK), 0.1),
        "gla_o_gain": 1.0 + nrm(ks[6], (N_GLA_LAYERS, GLA_DV), 0.02),
        "gla_w_out": nrm(ks[7], (N_GLA_LAYERS, GLA_DV, D_MODEL), GLA_DV ** -0.5),
        "sgu_w_in": nrm(ks[8], (N_SGU_LAYERS, D_MODEL, SGU_IN), D_MODEL ** -0.5),
        "sgu_ln_gain": 1.0 + nrm(ks[9], (N_SGU_LAYERS, SGU_WIDTH), 0.02),
        "sgu_ln_bias": nrm(ks[10], (N_SGU_LAYERS, SGU_WIDTH), 0.02),
        "sgu_w_spatial": nrm(ks[11], (N_SGU_LAYERS, SGU_GROUPS, SGU_BLOCK, SGU_BLOCK), SGU_BLOCK ** -0.5),
        "sgu_b_spatial": 1.0 + nrm(ks[12], (N_SGU_LAYERS, SGU_GROUPS, SGU_BLOCK), 0.1),
        "sgu_w_out": nrm(ks[13], (N_SGU_LAYERS, SGU_WIDTH, D_MODEL), SGU_WIDTH ** -0.5),
    }


def _fwd_reference(x, norm_pre, norm_post, gla_w_in, gla_w_gate2, gla_b_gate, gla_o_gain,
              gla_w_out, sgu_w_in, sgu_ln_gain, sgu_ln_bias, sgu_w_spatial, sgu_b_spatial,
              sgu_w_out):
    for i in range(DEPTH):
        h = rmsnorm(x, norm_pre[i])
        j = i // N_MIXERS
        if i % N_MIXERS == 0:
            y = gla_mixer(h, gla_w_in[j], gla_w_gate2[j], gla_b_gate[j], gla_o_gain[j], gla_w_out[j])
        else:
            y = sgu_mixer(h, sgu_w_in[j], sgu_ln_gain[j], sgu_ln_bias[j], sgu_w_spatial[j],
                          sgu_b_spatial[j], sgu_w_out[j])
        x = x + rmsnorm(y, norm_post[i])
    return x


import jax as _jax
import jax.numpy as _jnp

TWIN_FORMAT = 'train_step'
FWD_PARAMS = ['x', 'norm_pre', 'norm_post', 'gla_w_in', 'gla_w_gate2', 'gla_b_gate', 'gla_o_gain', 'gla_w_out', 'sgu_w_in', 'sgu_ln_gain', 'sgu_ln_bias', 'sgu_w_spatial', 'sgu_b_spatial', 'sgu_w_out']
TWIN_WEIGHTS = ['norm_pre', 'norm_post', 'gla_w_in', 'gla_w_gate2', 'gla_b_gate', 'gla_o_gain', 'gla_w_out', 'sgu_w_in', 'sgu_ln_gain', 'sgu_ln_bias', 'sgu_w_spatial', 'sgu_b_spatial', 'sgu_w_out']
TWIN_DIFF_INPUT = 'x'
TWIN_INPUTS = ['x', 'norm_pre', 'norm_post', 'gla_w_in', 'gla_w_gate2', 'gla_b_gate', 'gla_o_gain', 'gla_w_out', 'sgu_w_in', 'sgu_ln_gain', 'sgu_ln_bias', 'sgu_w_spatial', 'sgu_b_spatial', 'sgu_w_out', 'loss_target', 'm_norm_pre', 'm_norm_post', 'm_gla_w_in', 'm_gla_w_gate2', 'm_gla_b_gate', 'm_gla_o_gain', 'm_gla_w_out', 'm_sgu_w_in', 'm_sgu_ln_gain', 'm_sgu_ln_bias', 'm_sgu_w_spatial', 'm_sgu_b_spatial', 'm_sgu_w_out', 'v_norm_pre', 'v_norm_post', 'v_gla_w_in', 'v_gla_w_gate2', 'v_gla_b_gate', 'v_gla_o_gain', 'v_gla_w_out', 'v_sgu_w_in', 'v_sgu_ln_gain', 'v_sgu_ln_bias', 'v_sgu_w_spatial', 'v_sgu_b_spatial', 'v_sgu_w_out']
TWIN_OUTPUTS = ['loss', 'grad_x', 'grad_norm_pre', 'grad_norm_post', 'grad_gla_w_in', 'grad_gla_w_gate2', 'grad_gla_b_gate', 'grad_gla_o_gain', 'grad_gla_w_out', 'grad_sgu_w_in', 'grad_sgu_ln_gain', 'grad_sgu_ln_bias', 'grad_sgu_w_spatial', 'grad_sgu_b_spatial', 'grad_sgu_w_out', 'delta_norm_pre', 'delta_norm_post', 'delta_gla_w_in', 'delta_gla_w_gate2', 'delta_gla_b_gate', 'delta_gla_o_gain', 'delta_gla_w_out', 'delta_sgu_w_in', 'delta_sgu_ln_gain', 'delta_sgu_ln_bias', 'delta_sgu_w_spatial', 'delta_sgu_b_spatial', 'delta_sgu_w_out', 'new_m_norm_pre', 'new_m_norm_post', 'new_m_gla_w_in', 'new_m_gla_w_gate2', 'new_m_gla_b_gate', 'new_m_gla_o_gain', 'new_m_gla_w_out', 'new_m_sgu_w_in', 'new_m_sgu_ln_gain', 'new_m_sgu_ln_bias', 'new_m_sgu_w_spatial', 'new_m_sgu_b_spatial', 'new_m_sgu_w_out', 'new_v_norm_pre', 'new_v_norm_post', 'new_v_gla_w_in', 'new_v_gla_w_gate2', 'new_v_gla_b_gate', 'new_v_gla_o_gain', 'new_v_gla_w_out', 'new_v_sgu_w_in', 'new_v_sgu_ln_gain', 'new_v_sgu_ln_bias', 'new_v_sgu_w_spatial', 'new_v_sgu_b_spatial', 'new_v_sgu_w_out']
TWIN_LEAF_KINDS = {'loss': 'loss', 'grad_x': 'grad_x', 'grad_norm_pre': 'grad_w', 'grad_norm_post': 'grad_w', 'grad_gla_w_in': 'grad_w', 'grad_gla_w_gate2': 'grad_w', 'grad_gla_b_gate': 'grad_w', 'grad_gla_o_gain': 'grad_w', 'grad_gla_w_out': 'grad_w', 'grad_sgu_w_in': 'grad_w', 'grad_sgu_ln_gain': 'grad_w', 'grad_sgu_ln_bias': 'grad_w', 'grad_sgu_w_spatial': 'grad_w', 'grad_sgu_b_spatial': 'grad_w', 'grad_sgu_w_out': 'grad_w', 'delta_norm_pre': 'delta_w', 'delta_norm_post': 'delta_w', 'delta_gla_w_in': 'delta_w', 'delta_gla_w_gate2': 'delta_w', 'delta_gla_b_gate': 'delta_w', 'delta_gla_o_gain': 'delta_w', 'delta_gla_w_out': 'delta_w', 'delta_sgu_w_in': 'delta_w', 'delta_sgu_ln_gain': 'delta_w', 'delta_sgu_ln_bias': 'delta_w', 'delta_sgu_w_spatial': 'delta_w', 'delta_sgu_b_spatial': 'delta_w', 'delta_sgu_w_out': 'delta_w', 'new_m_norm_pre': 'new_m', 'new_m_norm_post': 'new_m', 'new_m_gla_w_in': 'new_m', 'new_m_gla_w_gate2': 'new_m', 'new_m_gla_b_gate': 'new_m', 'new_m_gla_o_gain': 'new_m', 'new_m_gla_w_out': 'new_m', 'new_m_sgu_w_in': 'new_m', 'new_m_sgu_ln_gain': 'new_m', 'new_m_sgu_ln_bias': 'new_m', 'new_m_sgu_w_spatial': 'new_m', 'new_m_sgu_b_spatial': 'new_m', 'new_m_sgu_w_out': 'new_m', 'new_v_norm_pre': 'new_v', 'new_v_norm_post': 'new_v', 'new_v_gla_w_in': 'new_v', 'new_v_gla_w_gate2': 'new_v', 'new_v_gla_b_gate': 'new_v', 'new_v_gla_o_gain': 'new_v', 'new_v_gla_w_out': 'new_v', 'new_v_sgu_w_in': 'new_v', 'new_v_sgu_ln_gain': 'new_v', 'new_v_sgu_ln_bias': 'new_v', 'new_v_sgu_w_spatial': 'new_v', 'new_v_sgu_b_spatial': 'new_v', 'new_v_sgu_w_out': 'new_v'}


def _forward(args):
    return _fwd_reference(*[args[k] for k in FWD_PARAMS])


def _output_shape():
    out = _jax.eval_shape(lambda: _forward(_fwd_setup_inputs(0)))
    return out.shape, out.dtype

N_MICROBATCH = 1
ADAM_LR = 0.001
ADAM_B1 = 0.9
ADAM_B2 = 0.999
ADAM_EPS = 1e-08
ADAM_WD = 0.01
ADAM_STEP = 10
PER_EXAMPLE_BATCH_AXIS = {'x': 0, 'loss_target': 0}
SHARED_INPUTS = []
_WEIGHT_DTYPES = {'norm_pre': _jnp.float32, 'norm_post': _jnp.float32, 'gla_w_in': _jnp.float32, 'gla_w_gate2': _jnp.float32, 'gla_b_gate': _jnp.float32, 'gla_o_gain': _jnp.float32, 'gla_w_out': _jnp.float32, 'sgu_w_in': _jnp.float32, 'sgu_ln_gain': _jnp.float32, 'sgu_ln_bias': _jnp.float32, 'sgu_w_spatial': _jnp.float32, 'sgu_b_spatial': _jnp.float32, 'sgu_w_out': _jnp.float32}
MOMENT_SCALE = {'norm_pre': 2.821316e-01, 'norm_post': 8.003148e+00, 'gla_w_in': 1.970057e-01, 'gla_w_gate2': 2.725120e-02, 'gla_b_gate': 1.050431e-01, 'gla_o_gain': 1.740812e-01, 'gla_w_out': 1.717593e-01, 'sgu_w_in': 1.097645e-01, 'sgu_ln_gain': 6.913450e-02, 'sgu_ln_bias': 6.962039e-02, 'sgu_w_spatial': 9.880116e-02, 'sgu_b_spatial': 1.127290e-01, 'sgu_w_out': 1.270575e-01}


def _to_microbatches(a, axis):
    t = _jnp.moveaxis(a, axis, 0)
    t = t.reshape((N_MICROBATCH, t.shape[0] // N_MICROBATCH) + t.shape[1:])
    return _jnp.moveaxis(t, 1, axis + 1)


def setup_inputs(seed: int = 0) -> dict:
    inp = _fwd_setup_inputs(seed)
    key = _jax.random.fold_in(_jax.random.key(seed), 7919)
    shape, _ = _output_shape()
    out = dict(inp)
    out["loss_target"] = _jax.random.normal(_jax.random.fold_in(key, 0), shape, _jnp.float32)
    for i, name in enumerate(TWIN_WEIGHTS):
        w = inp[name].astype(_jnp.float32)
        if MOMENT_SCALE is None:
            s = _jnp.sqrt(_jnp.mean(_jnp.square(w)) + 1e-30)
        else:
            s = MOMENT_SCALE[name]
        km, kv = _jax.random.split(_jax.random.fold_in(key, i + 1))
        out[name] = w
        out["m_" + name] = s * _jax.random.normal(km, w.shape, _jnp.float32)
        out["v_" + name] = (s * s) * _jax.random.uniform(kv, w.shape, _jnp.float32, 0.5, 1.5)
    if N_MICROBATCH > 1:
        for name, axis in PER_EXAMPLE_BATCH_AXIS.items():
            out[name] = _to_microbatches(out[name], axis)
    return {'x': out['x'], 'norm_pre': out['norm_pre'], 'norm_post': out['norm_post'], 'gla_w_in': out['gla_w_in'], 'gla_w_gate2': out['gla_w_gate2'], 'gla_b_gate': out['gla_b_gate'], 'gla_o_gain': out['gla_o_gain'], 'gla_w_out': out['gla_w_out'], 'sgu_w_in': out['sgu_w_in'], 'sgu_ln_gain': out['sgu_ln_gain'], 'sgu_ln_bias': out['sgu_ln_bias'], 'sgu_w_spatial': out['sgu_w_spatial'], 'sgu_b_spatial': out['sgu_b_spatial'], 'sgu_w_out': out['sgu_w_out'], 'loss_target': out['loss_target'], 'm_norm_pre': out['m_norm_pre'], 'm_norm_post': out['m_norm_post'], 'm_gla_w_in': out['m_gla_w_in'], 'm_gla_w_gate2': out['m_gla_w_gate2'], 'm_gla_b_gate': out['m_gla_b_gate'], 'm_gla_o_gain': out['m_gla_o_gain'], 'm_gla_w_out': out['m_gla_w_out'], 'm_sgu_w_in': out['m_sgu_w_in'], 'm_sgu_ln_gain': out['m_sgu_ln_gain'], 'm_sgu_ln_bias': out['m_sgu_ln_bias'], 'm_sgu_w_spatial': out['m_sgu_w_spatial'], 'm_sgu_b_spatial': out['m_sgu_b_spatial'], 'm_sgu_w_out': out['m_sgu_w_out'], 'v_norm_pre': out['v_norm_pre'], 'v_norm_post': out['v_norm_post'], 'v_gla_w_in': out['v_gla_w_in'], 'v_gla_w_gate2': out['v_gla_w_gate2'], 'v_gla_b_gate': out['v_gla_b_gate'], 'v_gla_o_gain': out['v_gla_o_gain'], 'v_gla_w_out': out['v_gla_w_out'], 'v_sgu_w_in': out['v_sgu_w_in'], 'v_sgu_ln_gain': out['v_sgu_ln_gain'], 'v_sgu_ln_bias': out['v_sgu_ln_bias'], 'v_sgu_w_spatial': out['v_sgu_w_spatial'], 'v_sgu_b_spatial': out['v_sgu_b_spatial'], 'v_sgu_w_out': out['v_sgu_w_out']}


def _loss(weights, diff, rest, loss_target):
    with _jax.named_scope("forward"):
        args = {**rest, TWIN_DIFF_INPUT: diff, **{k: w.astype(_WEIGHT_DTYPES[k]) for k, w in weights.items()}}
        y = _forward(args)
    with _jax.named_scope("loss_head"):
        err = _jnp.square(y.astype(_jnp.float32) - loss_target)
        return 0.5 * _jnp.sum(_jnp.mean(err, axis=-1)) if err.ndim else 0.5 * err


def _adamw(w, g, m, v):
    m = ADAM_B1 * m + (1.0 - ADAM_B1) * g
    v = ADAM_B2 * v + (1.0 - ADAM_B2) * _jnp.square(g)
    m_hat = m / (1.0 - ADAM_B1 ** ADAM_STEP)
    v_hat = v / (1.0 - ADAM_B2 ** ADAM_STEP)
    delta = -ADAM_LR * (m_hat / (_jnp.sqrt(v_hat) + ADAM_EPS) + ADAM_WD * w)
    return delta, m, v


def reference(x, norm_pre, norm_post, gla_w_in, gla_w_gate2, gla_b_gate, gla_o_gain, gla_w_out, sgu_w_in, sgu_ln_gain, sgu_ln_bias, sgu_w_spatial, sgu_b_spatial, sgu_w_out, loss_target, m_norm_pre, m_norm_post, m_gla_w_in, m_gla_w_gate2, m_gla_b_gate, m_gla_o_gain, m_gla_w_out, m_sgu_w_in, m_sgu_ln_gain, m_sgu_ln_bias, m_sgu_w_spatial, m_sgu_b_spatial, m_sgu_w_out, v_norm_pre, v_norm_post, v_gla_w_in, v_gla_w_gate2, v_gla_b_gate, v_gla_o_gain, v_gla_w_out, v_sgu_w_in, v_sgu_ln_gain, v_sgu_ln_bias, v_sgu_w_spatial, v_sgu_b_spatial, v_sgu_w_out):
    given = dict(x=x, norm_pre=norm_pre, norm_post=norm_post, gla_w_in=gla_w_in, gla_w_gate2=gla_w_gate2, gla_b_gate=gla_b_gate, gla_o_gain=gla_o_gain, gla_w_out=gla_w_out, sgu_w_in=sgu_w_in, sgu_ln_gain=sgu_ln_gain, sgu_ln_bias=sgu_ln_bias, sgu_w_spatial=sgu_w_spatial, sgu_b_spatial=sgu_b_spatial, sgu_w_out=sgu_w_out, loss_target=loss_target, m_norm_pre=m_norm_pre, m_norm_post=m_norm_post, m_gla_w_in=m_gla_w_in, m_gla_w_gate2=m_gla_w_gate2, m_gla_b_gate=m_gla_b_gate, m_gla_o_gain=m_gla_o_gain, m_gla_w_out=m_gla_w_out, m_sgu_w_in=m_sgu_w_in, m_sgu_ln_gain=m_sgu_ln_gain, m_sgu_ln_bias=m_sgu_ln_bias, m_sgu_w_spatial=m_sgu_w_spatial, m_sgu_b_spatial=m_sgu_b_spatial, m_sgu_w_out=m_sgu_w_out, v_norm_pre=v_norm_pre, v_norm_post=v_norm_post, v_gla_w_in=v_gla_w_in, v_gla_w_gate2=v_gla_w_gate2, v_gla_b_gate=v_gla_b_gate, v_gla_o_gain=v_gla_o_gain, v_gla_w_out=v_gla_w_out, v_sgu_w_in=v_sgu_w_in, v_sgu_ln_gain=v_sgu_ln_gain, v_sgu_ln_bias=v_sgu_ln_bias, v_sgu_w_spatial=v_sgu_w_spatial, v_sgu_b_spatial=v_sgu_b_spatial, v_sgu_w_out=v_sgu_w_out)
    weights = {n: given[n] for n in TWIN_WEIGHTS}
    shared = {n: given[n] for n in SHARED_INPUTS}
    per_example = {n: given[n] for n in ['x']}
    grad_fn = _jax.value_and_grad(_loss, argnums=(0, 1))

    def one_microbatch(ex, loss_target):
        ex = dict(ex)
        diff = ex.pop(TWIN_DIFF_INPUT)
        return grad_fn(weights, diff, {**shared, **ex}, loss_target)

    if N_MICROBATCH == 1:
        loss, (grad_w, grad_x) = one_microbatch(per_example, given["loss_target"])
    else:
        def body(carry, xs):
            loss_sum, grad_sum = carry
            l_k, (gw_k, gx_k) = one_microbatch(xs[0], xs[1])
            with _jax.named_scope("update"):
                return (loss_sum + l_k, _jax.tree.map(_jnp.add, grad_sum, gw_k)), gx_k

        init = (_jnp.zeros((), _jnp.float32), _jax.tree.map(_jnp.zeros_like, weights))
        (loss, grad_w), grad_x = _jax.lax.scan(body, init, (per_example, given["loss_target"]))
    with _jax.named_scope("update"):
        delta_w, new_m, new_v = {}, {}, {}
        for n in TWIN_WEIGHTS:
            delta_w[n], new_m[n], new_v[n] = _adamw(weights[n], grad_w[n], given["m_" + n], given["v_" + n])
    return (loss, grad_x, *[grad_w[n] for n in TWIN_WEIGHTS], *[delta_w[n] for n in TWIN_WEIGHTS],
            *[new_m[n] for n in TWIN_WEIGHTS], *[new_v[n] for n in TWIN_WEIGHTS])
```

```python
import functools
import math

import jax
import jax.numpy as jnp
from jax import lax
from jax.experimental import pallas as pl
from jax.experimental.pallas import tpu as pltpu

F32 = jnp.float32
BF16 = jnp.bfloat16
MESH = pl.DeviceIdType.MESH

EPS = 1e-6
CHUNK = 64
GLA_HEADS = 4
GLA_GATE_RANK = 16
GLA_TAU = 16.0
SGU_BLOCK = 128
SGU_GROUPS = 8
N_CHIPS = 4
N_DEV = 8
LANES = 128

ADAM_LR = 0.001
ADAM_B1 = 0.9
ADAM_B2 = 0.999
ADAM_EPS = 1e-08
ADAM_WD = 0.01
ADAM_STEP = 10

VMEM_LIMIT = 56 * 1024 * 1024


def _cparams(sem=None):
    return pltpu.CompilerParams(dimension_semantics=sem, vmem_limit_bytes=VMEM_LIMIT)


def _pick(n, cap, unit=LANES):
    best = None
    for t in range(unit, min(n, cap) + 1, unit):
        if n % t == 0:
            best = t
    assert best is not None, (n, cap, unit)
    return best


def _dot(a, b, dims):
    return lax.dot_general(a, b, (dims, ((), ())), preferred_element_type=F32)


def _dot_nn(a, b):
    return _dot(a, b, ((1,), (0,)))


def _dot_nt(a, b):
    return _dot(a, b, ((1,), (1,)))


def _dot_tn(a, b):
    return _dot(a, b, ((0,), (0,)))


def _matmul(a, b, *, mode, out_dtype, name, tm=1024, tn=512, tk=2048, b_shards=False, out_shards=False):
    if mode == "tn":
        K, M = a.shape
    else:
        M, K = a.shape
    if b_shards:
        ns, br, bc = b.shape
        if mode == "nt":
            N, Kb = br, ns * bc
        else:
            Kb, N = br, ns * bc
    else:
        if mode == "nt":
            N, Kb = b.shape
        else:
            Kb, N = b.shape
    assert K == Kb, (a.shape, b.shape, mode)
    tm = _pick(M, tm)
    tk = _pick(K, tk)
    if b_shards and mode != "nt":
        tn = _pick(bc, tn)
    elif out_shards:
        tn = _pick(N // N_CHIPS, tn)
    else:
        tn = _pick(N, tn)
    if b_shards and mode == "nt":
        tk = _pick(bc, tk)
    nk = K // tk
    grid = (M // tm, N // tn, nk)

    if mode == "tn":
        a_spec = pl.BlockSpec((tk, tm), lambda i, j, k: (k, i))
    else:
        a_spec = pl.BlockSpec((tm, tk), lambda i, j, k: (i, k))
    if b_shards:
        if mode == "nt":
            per = bc // tk
            b_spec = pl.BlockSpec((None, tn, tk), lambda i, j, k: (k // per, j, k % per))
        else:
            per = bc // tn
            b_spec = pl.BlockSpec((None, tk, tn), lambda i, j, k: (j // per, k, j % per))
    elif mode == "nt":
        b_spec = pl.BlockSpec((tn, tk), lambda i, j, k: (j, k))
    else:
        b_spec = pl.BlockSpec((tk, tn), lambda i, j, k: (k, j))
    if out_shards:
        per_o = (N // N_CHIPS) // tn
        out_spec = pl.BlockSpec((None, tm, tn), lambda i, j, k: (j // per_o, i, j % per_o))
        out_shape = jax.ShapeDtypeStruct((N_CHIPS, M, N // N_CHIPS), out_dtype)
    else:
        out_spec = pl.BlockSpec((tm, tn), lambda i, j, k: (i, j))
        out_shape = jax.ShapeDtypeStruct((M, N), out_dtype)

    dims = {"nn": ((1,), (0,)), "nt": ((1,), (1,)), "tn": ((0,), (0,))}[mode]

    def body(a_ref, b_ref, o_ref, *scratch):
        part = _dot(a_ref[...].astype(BF16), b_ref[...].astype(BF16), dims)
        if nk == 1:
            o_ref[...] = part.astype(out_dtype)
        else:
            acc_ref, = scratch
            k = pl.program_id(2)

            @pl.when(k == 0)
            def _():
                acc_ref[...] = part

            @pl.when(k > 0)
            def _():
                acc_ref[...] += part

            @pl.when(k == nk - 1)
            def _():
                o_ref[...] = acc_ref[...].astype(out_dtype)

    return pl.pallas_call(
        body, name=name, grid=grid, in_specs=[a_spec, b_spec], out_specs=out_spec, out_shape=out_shape,
        scratch_shapes=[] if nk == 1 else [pltpu.VMEM((tm, tn), F32)],
        compiler_params=_cparams(("parallel", "parallel", "arbitrary")),
    )(a, b)


def _rstd(x):
    return lax.rsqrt(jnp.mean(x * x, axis=-1, keepdims=True) + EPS)


def _row_spec(tr, d):
    return pl.BlockSpec((tr, d), lambda i: (i, 0))


def _vec_spec(d):
    return pl.BlockSpec((1, d), lambda i: (0, 0))


def _acc_rows(ref, i, val, cols=slice(None)):
    @pl.when(i == 0)
    def _():
        ref[:, cols] = val

    @pl.when(i > 0)
    def _():
        ref[:, cols] += val


def _norm_pre(x, gain, *, name, tr=256):
    t, d = x.shape
    tr = _pick(t, tr, 8)

    def body(x_ref, g_ref, h_ref):
        xv = x_ref[...]
        h_ref[...] = (xv * _rstd(xv) * g_ref[...]).astype(BF16)

    return pl.pallas_call(
        body, name=name, grid=(t // tr,), in_specs=[_row_spec(tr, d), _vec_spec(d)], out_specs=_row_spec(tr, d),
        out_shape=jax.ShapeDtypeStruct((t, d), BF16), compiler_params=_cparams(("parallel",)),
    )(x, gain)


def _post_then_pre(x, y, post_gain, pre_gain, *, name, tr=256):
    t, d = x.shape
    tr = _pick(t, tr, 8)

    def body(x_ref, y_ref, pg_ref, ng_ref, xn_ref, h_ref):
        yv = y_ref[...]
        xn = x_ref[...] + yv * _rstd(yv) * pg_ref[...]
        xn_ref[...] = xn
        h_ref[...] = (xn * _rstd(xn) * ng_ref[...]).astype(BF16)

    return pl.pallas_call(
        body, name=name, grid=(t // tr,),
        in_specs=[_row_spec(tr, d), _row_spec(tr, d), _vec_spec(d), _vec_spec(d)],
        out_specs=[_row_spec(tr, d), _row_spec(tr, d)],
        out_shape=[jax.ShapeDtypeStruct((t, d), F32), jax.ShapeDtypeStruct((t, d), BF16)],
        compiler_params=_cparams(("parallel",)),
    )(x, y, post_gain, pre_gain)


def _norm_bwd(dy, n, r, gain):
    dn = dy * gain
    return r * (dn - n * jnp.mean(dn * n, axis=-1, keepdims=True))


def _loss_head(x, y, post_gain, target, *, name, tr=256):
    t, d = x.shape
    tr = _pick(t, tr, 8)

    def body(x_ref, y_ref, pg_ref, t_ref, loss_ref, dx_ref, dy_ref, dpg_ref):
        i = pl.program_id(0)
        yv = y_ref[...]
        r = _rstd(yv)
        n = yv * r
        err = x_ref[...] + n * pg_ref[...] - t_ref[...]
        dx = err * (1.0 / d)
        dx_ref[...] = dx
        part = 0.5 * jnp.sum(jnp.mean(err * err, axis=-1, keepdims=True), axis=0, keepdims=True)
        _acc_rows(loss_ref, i, jnp.broadcast_to(part, (1, LANES)))
        _acc_rows(dpg_ref, i, jnp.sum(dx * n, axis=0, keepdims=True))
        dy_ref[...] = _norm_bwd(dx, n, r, pg_ref[...]).astype(BF16)

    return pl.pallas_call(
        body, name=name, grid=(t // tr,),
        in_specs=[_row_spec(tr, d), _row_spec(tr, d), _vec_spec(d), _row_spec(tr, d)],
        out_specs=[_vec_spec(LANES), _row_spec(tr, d), _row_spec(tr, d), _vec_spec(d)],
        out_shape=[jax.ShapeDtypeStruct((1, LANES), F32), jax.ShapeDtypeStruct((t, d), F32),
                   jax.ShapeDtypeStruct((t, d), BF16), jax.ShapeDtypeStruct((1, d), F32)],
        compiler_params=_cparams(("arbitrary",)),
    )(x, y, post_gain, target)


def _mid_bwd(dx_out, dh, x, pre_gain, y_prev, post_gain_prev, *, name, tr=256):
    t, d = x.shape
    tr = _pick(t, tr, 8)

    def body(dxo_ref, dh_ref, x_ref, ng_ref, y_ref, pg_ref, dx_ref, dy_ref, dng_ref, dpg_ref):
        i = pl.program_id(0)
        xv = x_ref[...]
        r = _rstd(xv)
        xh = xv * r
        dhv = dh_ref[...]
        _acc_rows(dng_ref, i, jnp.sum(dhv * xh, axis=0, keepdims=True))
        dx = dxo_ref[...] + _norm_bwd(dhv, xh, r, ng_ref[...])
        dx_ref[...] = dx
        yv = y_ref[...]
        ry = _rstd(yv)
        n = yv * ry
        _acc_rows(dpg_ref, i, jnp.sum(dx * n, axis=0, keepdims=True))
        dy_ref[...] = _norm_bwd(dx, n, ry, pg_ref[...]).astype(BF16)

    return pl.pallas_call(
        body, name=name, grid=(t // tr,),
        in_specs=[_row_spec(tr, d), _row_spec(tr, d), _row_spec(tr, d), _vec_spec(d), _row_spec(tr, d), _vec_spec(d)],
        out_specs=[_row_spec(tr, d), _row_spec(tr, d), _vec_spec(d), _vec_spec(d)],
        out_shape=[jax.ShapeDtypeStruct((t, d), F32), jax.ShapeDtypeStruct((t, d), BF16),
                   jax.ShapeDtypeStruct((1, d), F32), jax.ShapeDtypeStruct((1, d), F32)],
        compiler_params=_cparams(("arbitrary",)),
    )(dx_out, dh, x, pre_gain, y_prev, post_gain_prev)


def _first_bwd(dx_out, dh, x, pre_gain, *, name, tr=256):
    t, d = x.shape
    tr = _pick(t, tr, 8)

    def body(dxo_ref, dh_ref, x_ref, ng_ref, dx_ref, dng_ref):
        i = pl.program_id(0)
        xv = x_ref[...]
        r = _rstd(xv)
        xh = xv * r
        dhv = dh_ref[...]
        _acc_rows(dng_ref, i, jnp.sum(dhv * xh, axis=0, keepdims=True))
        dx_ref[...] = dxo_ref[...] + _norm_bwd(dhv, xh, r, ng_ref[...])

    return pl.pallas_call(
        body, name=name, grid=(t // tr,),
        in_specs=[_row_spec(tr, d), _row_spec(tr, d), _row_spec(tr, d), _vec_spec(d)],
        out_specs=[_row_spec(tr, d), _vec_spec(d)],
        out_shape=[jax.ShapeDtypeStruct((t, d), F32), jax.ShapeDtypeStruct((1, d), F32)],
        compiler_params=_cparams(("arbitrary",)),
    )(dx_out, dh, x, pre_gain)


def _sigmoid(x):
    return 1.0 / (1.0 + jnp.exp(-x))


def _log_sigmoid(x):
    return jnp.minimum(x, 0.0) - jnp.log(1.0 + jnp.exp(-jnp.abs(x)))


_GELU_C = math.sqrt(2.0 / math.pi)


def _gelu_parts(x):
    x2 = x * x
    th = jnp.tanh(_GELU_C * (x + 0.044715 * x * x2))
    val = 0.5 * x * (1.0 + th)
    grad = 0.5 * (1.0 + th) + 0.5 * x * (1.0 - th * th) * (_GELU_C * (1.0 + 3.0 * 0.044715 * x2))
    return val, grad


def _split3(x):
    hi = x.astype(BF16)
    r1 = x - hi.astype(F32)
    mid = r1.astype(BF16)
    lo = (r1 - mid.astype(F32)).astype(BF16)
    return hi, mid, lo


def _tri_matmul(tri_bf16, x):
    hi, mid, lo = _split3(x)
    return _dot_nn(tri_bf16, hi) + _dot_nn(tri_bf16, mid) + _dot_nn(tri_bf16, lo)


def _gla_dims(d):
    dk, dv = d // 2, d
    return dk, dv, dk // GLA_HEADS, dv // GLA_HEADS


def _gla_gates(glr, k, w2_ref, b_ref):
    z = _dot_nn(glr.astype(BF16), w2_ref[...].astype(BF16)) + b_ref[...]
    la = _log_sigmoid(z) * (1.0 / GLA_TAU)
    row = lax.broadcasted_iota(jnp.int32, (CHUNK, CHUNK), 0)
    col = lax.broadcasted_iota(jnp.int32, (CHUNK, CHUNK), 1)
    incl = (row >= col).astype(BF16)
    bcum = _tri_matmul(incl, la)
    b_end = bcum[CHUNK - 1:CHUNK, :]
    e_rest = jnp.exp(b_end - bcum)
    return z, e_rest, k * e_rest, jnp.exp(b_end)


def _gla_fwd(proj, w2p, b_gate, o_gain, *, name):
    t, wcols = proj.shape
    d = o_gain.shape[1]
    dk, dv, dkh, dvh = _gla_dims(d)
    nc = t // CHUNK
    c_k, c_v, c_g, c_r = dk, 2 * dk, 2 * dk + dv, 2 * dk + 2 * dv
    scale = dkh ** -0.5

    def body(p_ref, w2_ref, b_ref, og_ref, o_ref, a_ref, sb_ref, sfin_ref, s_ref):
        i = pl.program_id(0)

        @pl.when(i == 0)
        def _():
            s_ref[...] = jnp.zeros_like(s_ref)

        q = p_ref[:, 0:dk] * scale
        k = p_ref[:, c_k:c_k + dk]
        glr = p_ref[:, c_r:c_r + LANES]
        _, _, kdec, decay = _gla_gates(glr, k, w2_ref, b_ref)
        for h in range(GLA_HEADS):
            ks = slice(h * dkh, (h + 1) * dkh)
            vs = slice(h * dvh, (h + 1) * dvh)
            v_h = p_ref[:, c_v + h * dvh:c_v + (h + 1) * dvh]
            g_h = p_ref[:, c_g + h * dvh:c_g + (h + 1) * dvh]
            s_old = s_ref[h]
            sb_ref[0, h] = s_old
            s_new = s_old * decay[:, ks] + _dot_tn(v_h.astype(BF16), kdec[:, ks].astype(BF16))
            s_ref[h] = s_new
            o_h = _dot_nt(q[:, ks].astype(BF16), s_new.astype(BF16))
            o_ref[:, vs] = o_h
            on = o_h * _rstd(o_h)
            a_ref[:, vs] = (on * og_ref[:, vs] * (g_h * _sigmoid(g_h))).astype(BF16)

        @pl.when(i == nc - 1)
        def _():
            sfin_ref[...] = s_ref[...]

    full = lambda *shape: pl.BlockSpec(shape, lambda i: (0,) * len(shape))
    return pl.pallas_call(
        body, name=name, grid=(nc,),
        in_specs=[pl.BlockSpec((CHUNK, wcols), lambda i: (i, 0)), full(LANES, dk), full(1, dk), full(1, dv)],
        out_specs=[pl.BlockSpec((CHUNK, dv), lambda i: (i, 0)), pl.BlockSpec((CHUNK, dv), lambda i: (i, 0)),
                   pl.BlockSpec((1, GLA_HEADS, dvh, dkh), lambda i: (i, 0, 0, 0)), full(GLA_HEADS, dvh, dkh)],
        out_shape=[jax.ShapeDtypeStruct((t, dv), F32), jax.ShapeDtypeStruct((t, dv), BF16),
                   jax.ShapeDtypeStruct((nc, GLA_HEADS, dvh, dkh), F32),
                   jax.ShapeDtypeStruct((GLA_HEADS, dvh, dkh), F32)],
        scratch_shapes=[pltpu.VMEM((GLA_HEADS, dvh, dkh), F32)],
        compiler_params=_cparams(("arbitrary",)),
    )(proj, w2p, b_gate, o_gain)


def _gla_bwd(da, o, proj, w2p, b_gate, o_gain, s_before, s_final, *, name):
    t, wcols = proj.shape
    d = o_gain.shape[1]
    dk, dv, dkh, dvh = _gla_dims(d)
    nc = t // CHUNK
    c_k, c_v, c_g, c_r = dk, 2 * dk, 2 * dk + dv, 2 * dk + 2 * dv
    scale = dkh ** -0.5

    def body(da_ref, o_ref, p_ref, w2_ref, b_ref, og_ref, sb_ref, sfin_ref,
             dp_ref, dog_ref, db_ref, dw2_ref, s_ref, gc_ref, dkd_ref):
        i = pl.program_id(0)

        @pl.when(i == 0)
        def _():
            s_ref[...] = sfin_ref[...]
            gc_ref[...] = jnp.zeros_like(gc_ref)

        q = p_ref[:, 0:dk] * scale
        k = p_ref[:, c_k:c_k + dk]
        glr = p_ref[:, c_r:c_r + LANES]
        z, e_rest, kdec, decay = _gla_gates(glr, k, w2_ref, b_ref)
        ddecay = []
        for h in range(GLA_HEADS):
            ks = slice(h * dkh, (h + 1) * dkh)
            vs = slice(h * dvh, (h + 1) * dvh)
            v_h = p_ref[:, c_v + h * dvh:c_v + (h + 1) * dvh]
            g_h = p_ref[:, c_g + h * dvh:c_g + (h + 1) * dvh]
            da_h = da_ref[:, vs]
            o_h = o_ref[:, vs]
            og_h = og_ref[:, vs]
            r = _rstd(o_h)
            on = o_h * r
            sg = _sigmoid(g_h)
            silu = g_h * sg
            _acc_rows(dog_ref, i, jnp.sum(da_h * silu * on, axis=0, keepdims=True), vs)
            dp_ref[:, c_g + h * dvh:c_g + (h + 1) * dvh] = (
                da_h * (on * og_h) * (sg * (1.0 + g_h * (1.0 - sg)))).astype(BF16)
            don = da_h * silu * og_h
            do_h = (r * (don - on * jnp.mean(don * on, axis=-1, keepdims=True))).astype(BF16)
            s_cur = s_ref[h]
            dp_ref[:, ks] = (_dot_nn(do_h, s_cur.astype(BF16)) * scale).astype(BF16)
            g_tot = gc_ref[h] + _dot_tn(do_h, q[:, ks].astype(BF16))
            g_bf = g_tot.astype(BF16)
            dkd_ref[:, ks] = _dot_nn(v_h.astype(BF16), g_bf)
            dp_ref[:, c_v + h * dvh:c_v + (h + 1) * dvh] = _dot_nt(kdec[:, ks].astype(BF16), g_bf).astype(BF16)
            s_prev = sb_ref[0, h]
            ddecay.append(jnp.sum(g_tot * s_prev, axis=0, keepdims=True))
            gc_ref[h] = g_tot * decay[:, ks]
            s_ref[h] = s_prev
        dkdec = dkd_ref[...]
        dp_ref[:, c_k:c_k + dk] = (dkdec * e_rest).astype(BF16)
        d_e = dkdec * kdec
        row = lax.broadcasted_iota(jnp.int32, (CHUNK, CHUNK), 0)
        col = lax.broadcasted_iota(jnp.int32, (CHUNK, CHUNK), 1)
        excl = (row > col).astype(BF16)
        dla = jnp.concatenate(ddecay, axis=1) * decay + _tri_matmul(excl, d_e)
        dz = dla * (1.0 / GLA_TAU) * (1.0 - _sigmoid(z))
        _acc_rows(db_ref, i, jnp.sum(dz, axis=0, keepdims=True))
        dz_bf = dz.astype(BF16)
        dw2 = _dot_tn(glr.astype(BF16), dz_bf)

        @pl.when(i == 0)
        def _():
            dw2_ref[...] = dw2

        @pl.when(i > 0)
        def _():
            dw2_ref[...] += dw2

        dp_ref[:, c_r:c_r + LANES] = _dot_nt(dz_bf, w2_ref[...].astype(BF16)).astype(BF16)

    rev = lambda i: (nc - 1 - i, 0)
    full = lambda *shape: pl.BlockSpec(shape, lambda i: (0,) * len(shape))
    return pl.pallas_call(
        body, name=name, grid=(nc,),
        in_specs=[pl.BlockSpec((CHUNK, dv), rev), pl.BlockSpec((CHUNK, dv), rev), pl.BlockSpec((CHUNK, wcols), rev),
                  full(LANES, dk), full(1, dk), full(1, dv),
                  pl.BlockSpec((1, GLA_HEADS, dvh, dkh), lambda i: (nc - 1 - i, 0, 0, 0)), full(GLA_HEADS, dvh, dkh)],
        out_specs=[pl.BlockSpec((CHUNK, wcols), rev), full(1, dv), full(1, dk), full(LANES, dk)],
        out_shape=[jax.ShapeDtypeStruct((t, wcols), BF16), jax.ShapeDtypeStruct((1, dv), F32),
                   jax.ShapeDtypeStruct((1, dk), F32), jax.ShapeDtypeStruct((LANES, dk), F32)],
        scratch_shapes=[pltpu.VMEM((GLA_HEADS, dvh, dkh), F32), pltpu.VMEM((GLA_HEADS, dvh, dkh), F32),
                        pltpu.VMEM((CHUNK, dk), F32)],
        compiler_params=_cparams(("arbitrary",)),
    )(da, o, proj, w2p, b_gate, o_gain, s_before, s_final)


def _sgu_mid(p_ref, lg_ref, lb_ref, ws_ref, bst_ref, w):
    gd = w // SGU_GROUPS
    u_act, du_fac = _gelu_parts(p_ref[:, 0:w])
    vf, dv_fac = _gelu_parts(p_ref[:, w:2 * w])
    mu = jnp.mean(vf, axis=-1, keepdims=True)
    cen = vf - mu
    rstd = lax.rsqrt(jnp.mean(cen * cen, axis=-1, keepdims=True) + EPS)
    xh = cen * rstd
    vn = (xh * lg_ref[...] + lb_ref[...]).astype(BF16)
    vs = [_dot_nn(ws_ref[g].astype(BF16), vn[:, g * gd:(g + 1) * gd]) + bst_ref[:, g:g + 1]
          for g in range(SGU_GROUPS)]
    return u_act, du_fac, dv_fac, rstd, xh, vn, vs


def _sgu_fwd(proj, ln_gain, ln_bias, ws_masked, bs_t, *, name):
    t, w3 = proj.shape
    w = w3 // 3
    gd = w // SGU_GROUPS
    nb = t // SGU_BLOCK

    def body(p_ref, lg_ref, lb_ref, ws_ref, bst_ref, a_ref):
        u_act, _, _, _, _, _, vs = _sgu_mid(p_ref, lg_ref, lb_ref, ws_ref, bst_ref, w)
        for g in range(SGU_GROUPS):
            cs = slice(g * gd, (g + 1) * gd)
            gate = p_ref[:, 2 * w + g * gd:2 * w + (g + 1) * gd]
            a_ref[:, cs] = (u_act[:, cs] * vs[g] * (gate * _sigmoid(gate))).astype(BF16)

    full = lambda *shape: pl.BlockSpec(shape, lambda i: (0,) * len(shape))
    return pl.pallas_call(
        body, name=name, grid=(nb,),
        in_specs=[pl.BlockSpec((SGU_BLOCK, w3), lambda i: (i, 0)), full(1, w), full(1, w),
                  full(SGU_GROUPS, SGU_BLOCK, SGU_BLOCK), full(SGU_BLOCK, SGU_GROUPS)],
        out_specs=pl.BlockSpec((SGU_BLOCK, w), lambda i: (i, 0)),
        out_shape=jax.ShapeDtypeStruct((t, w), BF16),
        compiler_params=_cparams(("parallel",)),
    )(proj, ln_gain, ln_bias, ws_masked, bs_t)


def _sgu_bwd(da, proj, ln_gain, ln_bias, ws_masked, ws_masked_t, bs_t, *, name):
    t, w3 = proj.shape
    w = w3 // 3
    gd = w // SGU_GROUPS
    nb = t // SGU_BLOCK

    def body(da_ref, p_ref, lg_ref, lb_ref, ws_ref, wst_ref, bst_ref, dp_ref, dws_ref, dbst_ref, dlg_ref, dlb_ref,
             dvn_ref):
        i = pl.program_id(0)
        u_act, du_fac, dv_fac, rstd, xh, vn, vs = _sgu_mid(p_ref, lg_ref, lb_ref, ws_ref, bst_ref, w)
        for g in range(SGU_GROUPS):
            cs = slice(g * gd, (g + 1) * gd)
            gate = p_ref[:, 2 * w + g * gd:2 * w + (g + 1) * gd]
            sg = _sigmoid(gate)
            silu = gate * sg
            da_g = da_ref[:, cs]
            ua_g = u_act[:, cs]
            dp_ref[:, cs] = (da_g * vs[g] * silu * du_fac[:, cs]).astype(BF16)
            dp_ref[:, 2 * w + g * gd:2 * w + (g + 1) * gd] = (
                da_g * ua_g * vs[g] * (sg * (1.0 + gate * (1.0 - sg)))).astype(BF16)
            dvs = da_g * ua_g * silu
            dvs_bf = dvs.astype(BF16)
            dvn_ref[:, cs] = _dot_nn(wst_ref[g].astype(BF16), dvs_bf)
            dws = _dot_nt(dvs_bf, vn[:, cs])
            dbs = jnp.sum(dvs, axis=1, keepdims=True)

            @pl.when(i == 0)
            def _():
                dws_ref[g] = dws
                dbst_ref[:, g:g + 1] = dbs

            @pl.when(i > 0)
            def _():
                dws_ref[g] += dws
                dbst_ref[:, g:g + 1] += dbs

        dvn = dvn_ref[...]
        _acc_rows(dlg_ref, i, jnp.sum(dvn * xh, axis=0, keepdims=True))
        _acc_rows(dlb_ref, i, jnp.sum(dvn, axis=0, keepdims=True))
        dxh = dvn * lg_ref[...]
        dvf = rstd * (dxh - jnp.mean(dxh, axis=-1, keepdims=True)
                      - xh * jnp.mean(dxh * xh, axis=-1, keepdims=True))
        dp_ref[:, w:2 * w] = (dvf * dv_fac).astype(BF16)

    full = lambda *shape: pl.BlockSpec(shape, lambda i: (0,) * len(shape))
    return pl.pallas_call(
        body, name=name, grid=(nb,),
        in_specs=[pl.BlockSpec((SGU_BLOCK, w), lambda i: (i, 0)), pl.BlockSpec((SGU_BLOCK, w3), lambda i: (i, 0)),
                  full(1, w), full(1, w), full(SGU_GROUPS, SGU_BLOCK, SGU_BLOCK),
                  full(SGU_GROUPS, SGU_BLOCK, SGU_BLOCK), full(SGU_BLOCK, SGU_GROUPS)],
        out_specs=[pl.BlockSpec((SGU_BLOCK, w3), lambda i: (i, 0)), full(SGU_GROUPS, SGU_BLOCK, SGU_BLOCK),
                   full(SGU_BLOCK, SGU_GROUPS), full(1, w), full(1, w)],
        out_shape=[jax.ShapeDtypeStruct((t, w3), BF16), jax.ShapeDtypeStruct((SGU_GROUPS, SGU_BLOCK, SGU_BLOCK), F32),
                   jax.ShapeDtypeStruct((SGU_BLOCK, SGU_GROUPS), F32), jax.ShapeDtypeStruct((1, w), F32),
                   jax.ShapeDtypeStruct((1, w), F32)],
        scratch_shapes=[pltpu.VMEM((SGU_BLOCK, w), F32)],
        compiler_params=_cparams(("arbitrary",)),
    )(da, proj, ln_gain, ln_bias, ws_masked, ws_masked_t, bs_t)


def _adamw(w, g, m, v, *, name, block_bytes=1 << 20):
    rows, cols = w.shape
    tr = _pick(rows, max(8, block_bytes // (4 * cols)), 8) if rows % 8 == 0 else rows

    def body(w_ref, g_ref, m_ref, v_ref, d_ref, mo_ref, vo_ref):
        gv = g_ref[...]
        mn = ADAM_B1 * m_ref[...] + (1.0 - ADAM_B1) * gv
        vn = ADAM_B2 * v_ref[...] + (1.0 - ADAM_B2) * (gv * gv)
        m_hat = mn / (1.0 - ADAM_B1 ** ADAM_STEP)
        v_hat = vn / (1.0 - ADAM_B2 ** ADAM_STEP)
        d_ref[...] = -ADAM_LR * (m_hat / (jnp.sqrt(v_hat) + ADAM_EPS) + ADAM_WD * w_ref[...])
        mo_ref[...] = mn
        vo_ref[...] = vn

    spec = pl.BlockSpec((tr, cols), lambda i: (i, 0))
    return pl.pallas_call(
        body, name=name, grid=(rows // tr,), in_specs=[spec] * 4, out_specs=[spec] * 3,
        out_shape=[jax.ShapeDtypeStruct((rows, cols), F32)] * 3,
        compiler_params=_cparams(("parallel",)),
    )(w, g, m, v)


def _pair_sum_bf16(own, half_idx, peer, *, name, block_bytes=1 << 20):
    s, r, c = own.shape
    hr = r // 2
    tr = _pick(hr, max(8, block_bytes // (4 * c)), 16)
    nrb = hr // tr

    def body(h_ref, a_ref, b_ref, o_ref):
        o_ref[...] = (a_ref[...] + b_ref[...]).astype(BF16)

    grid_spec = pltpu.PrefetchScalarGridSpec(
        num_scalar_prefetch=1, grid=(s, nrb),
        in_specs=[pl.BlockSpec((None, tr, c), lambda j, i, h: (j, h[0] * nrb + i, 0)),
                  pl.BlockSpec((None, tr, c), lambda j, i, h: (j, i, 0))],
        out_specs=pl.BlockSpec((None, tr, c), lambda j, i, h: (j, i, 0)))
    return pl.pallas_call(
        body, name=name, grid_spec=grid_spec, out_shape=jax.ShapeDtypeStruct((s, hr, c), BF16),
        compiler_params=_cparams(("parallel", "parallel")),
    )(half_idx, own, peer)


def _stack_sum(x, *, name, out_dtype=F32, block_bytes=1 << 20):
    s, r, c = x.shape
    tr = _pick(r, max(8, block_bytes // (4 * c)), 16) if r % 16 == 0 else r

    def body(x_ref, o_ref):
        acc = x_ref[0].astype(F32)
        for j in range(1, s):
            acc = acc + x_ref[j].astype(F32)
        o_ref[...] = acc.astype(out_dtype)

    return pl.pallas_call(
        body, name=name, grid=(r // tr,),
        in_specs=[pl.BlockSpec((s, tr, c), lambda i: (0, i, 0))], out_specs=pl.BlockSpec((tr, c), lambda i: (i, 0)),
        out_shape=jax.ShapeDtypeStruct((r, c), out_dtype), compiler_params=_cparams(("parallel",)),
    )(x)


HBM = pl.BlockSpec(memory_space=pltpu.HBM)


def _place():
    x, y, c = lax.axis_index("x"), lax.axis_index("y"), lax.axis_index("c")
    other_chips = [(1 - x, y), (x, 1 - y), (1 - x, 1 - y)]
    return x, y, c, other_chips


def _all_gather_chip(shards, *, name):
    n = len(shards)

    def body(*refs):
        ins, outs = refs[:n], refs[n:2 * n]
        send_sems, recv_sems, local_sems = refs[2 * n:]
        x, y, c, chips = _place()
        me = 2 * x + y

        def half(a, which):
            hr = ins[a].shape[0] // 2
            return pl.ds(which * hr, hr)

        def landing(a, chip_idx, which):
            return outs[a].at[chip_idx, half(a, which), :]

        def copy(a, k, src, dst, to):
            return pltpu.make_async_remote_copy(
                src_ref=src, dst_ref=dst, send_sem=send_sems.at[a, k], recv_sem=recv_sems.at[a, k],
                device_id=to, device_id_type=MESH)

        local, sends = [], []
        for a in range(n):
            cp = pltpu.make_async_copy(ins[a], outs[a].at[me], local_sems.at[a])
            cp.start()
            local.append(cp)
            for k, (cx, cy) in enumerate(chips):
                s = copy(a, k, ins[a].at[half(a, c), :], landing(a, me, c), (cx, cy, c))
                s.start()
                sends.append(s)
        for a in range(n):
            for k, (cx, cy) in enumerate(chips):
                idx = 2 * cx + cy
                copy(a, k, landing(a, idx, c), landing(a, idx, c), (x, y, c)).wait_recv()
                f = copy(a, 3 + k, landing(a, idx, c), landing(a, idx, c), (x, y, 1 - c))
                f.start()
                sends.append(f)
        for a in range(n):
            for k, (cx, cy) in enumerate(chips):
                idx = 2 * cx + cy
                copy(a, 3 + k, landing(a, idx, 1 - c), landing(a, idx, 1 - c), (x, y, c)).wait_recv()
        for s in sends:
            s.wait_send()
        for cp in local:
            cp.wait()

    return pl.pallas_call(
        body, name=name, in_specs=[HBM] * n, out_specs=[HBM] * n,
        out_shape=[jax.ShapeDtypeStruct((N_CHIPS,) + s.shape, s.dtype) for s in shards],
        scratch_shapes=[pltpu.SemaphoreType.DMA((n, 6)), pltpu.SemaphoreType.DMA((n, 6)), pltpu.SemaphoreType.DMA((n,))],
    )(*shards)


def _all_gather_dev(x, *, name):
    r, ccols = x.shape

    def body(x_ref, out_ref, send_sems, recv_sems, local_sem):
        px, py, c, chips = _place()
        me, sibling = (px, py, c), (px, py, 1 - c)

        def slot(qx, qy, qc):
            return out_ref.at[4 * qx + 2 * qy + qc]

        def copy(k, block, to, src=None):
            return pltpu.make_async_remote_copy(
                src_ref=slot(*block) if src is None else src, dst_ref=slot(*block),
                send_sem=send_sems.at[k], recv_sem=recv_sems.at[k], device_id=to, device_id_type=MESH)

        mine = pltpu.make_async_copy(x_ref, slot(*me), local_sem)
        mine.start()
        first = [copy(0, me, sibling, src=x_ref)]
        first += [copy(1 + j, me, (*chip, c), src=x_ref) for j, chip in enumerate(chips)]
        for cp in first:
            cp.start()
        passed = [copy(4 + j, (*chip, c), sibling) for j, chip in enumerate(chips)]
        for j, chip in enumerate(chips):
            copy(1 + j, (*chip, c), me).wait_recv()
            passed[j].start()
        copy(0, sibling, me).wait_recv()
        for j, chip in enumerate(chips):
            copy(4 + j, (*chip, 1 - c), me).wait_recv()
        for cp in first + passed:
            cp.wait_send()
        mine.wait()

    return pl.pallas_call(
        body, name=name, in_specs=[HBM], out_specs=HBM, out_shape=jax.ShapeDtypeStruct((N_DEV, r, ccols), x.dtype),
        scratch_shapes=[pltpu.SemaphoreType.DMA((7,)), pltpu.SemaphoreType.DMA((7,)), pltpu.SemaphoreType.DMA],
    )(x)


def _sibling_swap_halves(grads, *, name):
    n = len(grads)

    def body(*refs):
        ins, outs = refs[:n], refs[n:2 * n]
        send_sems, recv_sems = refs[2 * n:]
        x, y, c, _ = _place()
        copies = []
        for a in range(n):
            hr = ins[a].shape[1] // 2
            cp = pltpu.make_async_remote_copy(
                src_ref=ins[a].at[:, pl.ds((1 - c) * hr, hr), :], dst_ref=outs[a],
                send_sem=send_sems.at[a], recv_sem=recv_sems.at[a], device_id=(x, y, 1 - c), device_id_type=MESH)
            cp.start()
            copies.append(cp)
        for cp in copies:
            cp.wait()

    return pl.pallas_call(
        body, name=name, in_specs=[HBM] * n, out_specs=[HBM] * n,
        out_shape=[jax.ShapeDtypeStruct((g.shape[0], g.shape[1] // 2, g.shape[2]), g.dtype) for g in grads],
        scratch_shapes=[pltpu.SemaphoreType.DMA((n,)), pltpu.SemaphoreType.DMA((n,))],
    )(*grads)


def _chip_scatter(parts, *, name):
    n = len(parts)

    def body(*refs):
        ins, outs = refs[:n], refs[n:2 * n]
        send_sems, recv_sems, local_sems = refs[2 * n:]
        x, y, c, chips = _place()
        me = 2 * x + y
        copies, locals_ = [], []
        for a in range(n):
            mine = pltpu.make_async_copy(ins[a].at[me], outs[a].at[me], local_sems.at[a])
            mine.start()
            locals_.append(mine)
            for k, (cx, cy) in enumerate(chips):
                cp = pltpu.make_async_remote_copy(
                    src_ref=ins[a].at[2 * cx + cy], dst_ref=outs[a].at[me],
                    send_sem=send_sems.at[a, k], recv_sem=recv_sems.at[a, k], device_id=(cx, cy, c),
                    device_id_type=MESH)
                cp.start()
                copies.append(cp)
        for a in range(n):
            for k, (cx, cy) in enumerate(chips):
                idx = 2 * cx + cy
                pltpu.make_async_remote_copy(
                    src_ref=ins[a].at[idx], dst_ref=outs[a].at[idx], send_sem=send_sems.at[a, k],
                    recv_sem=recv_sems.at[a, k], device_id=(x, y, c), device_id_type=MESH).wait_recv()
        for cp in copies:
            cp.wait_send()
        for mine in locals_:
            mine.wait()

    return pl.pallas_call(
        body, name=name, in_specs=[HBM] * n, out_specs=[HBM] * n,
        out_shape=[jax.ShapeDtypeStruct(p.shape, p.dtype) for p in parts],
        scratch_shapes=[pltpu.SemaphoreType.DMA((n, 3)), pltpu.SemaphoreType.DMA((n, 3)), pltpu.SemaphoreType.DMA((n,))],
    )(*parts)


def _sibling_share_halves(halves, *, name):
    n = len(halves)

    def body(*refs):
        ins, outs = refs[:n], refs[n:2 * n]
        send_sems, recv_sems, local_sems = refs[2 * n:]
        x, y, c, _ = _place()
        copies, locals_ = [], []
        for a in range(n):
            hr = ins[a].shape[0]
            mine = pltpu.make_async_copy(ins[a], outs[a].at[pl.ds(c * hr, hr), :], local_sems.at[a])
            mine.start()
            locals_.append(mine)
            cp = pltpu.make_async_remote_copy(
                src_ref=ins[a], dst_ref=outs[a].at[pl.ds(c * hr, hr), :],
                send_sem=send_sems.at[a], recv_sem=recv_sems.at[a], device_id=(x, y, 1 - c), device_id_type=MESH)
            cp.start()
            copies.append(cp)
        for a in range(n):
            hr = ins[a].shape[0]
            pltpu.make_async_remote_copy(
                src_ref=ins[a], dst_ref=outs[a].at[pl.ds((1 - c) * hr, hr), :], send_sem=send_sems.at[a],
                recv_sem=recv_sems.at[a], device_id=(x, y, c), device_id_type=MESH).wait_recv()
        for cp in copies:
            cp.wait_send()
        for mine in locals_:
            mine.wait()

    return pl.pallas_call(
        body, name=name, in_specs=[HBM] * n, out_specs=[HBM] * n,
        out_shape=[jax.ShapeDtypeStruct((2 * h.shape[0], h.shape[1]), h.dtype) for h in halves],
        scratch_shapes=[pltpu.SemaphoreType.DMA((n,)), pltpu.SemaphoreType.DMA((n,)), pltpu.SemaphoreType.DMA((n,))],
    )(*halves)


def _pack(arrays, rows_multiple=16):
    flat = jnp.concatenate([a.astype(F32).reshape(-1) for a in arrays])
    total = flat.shape[0]
    rows = -(-total // LANES)
    rows = -(-rows // rows_multiple) * rows_multiple
    return jnp.pad(flat, (0, rows * LANES - total)).reshape(rows, LANES)


def _unpack(buf, shapes):
    flat = buf.reshape(-1)
    out, off = [], 0
    for s in shapes:
        n = math.prod(s)
        out.append(flat[off:off + n].reshape(s))
        off += n
    return out


def kernel(x, norm_pre, norm_post, gla_w_in, gla_w_gate2, gla_b_gate, gla_o_gain, gla_w_out, sgu_w_in, sgu_ln_gain, sgu_ln_bias, sgu_w_spatial, sgu_b_spatial, sgu_w_out, loss_target, m_norm_pre, m_norm_post, m_gla_w_in, m_gla_w_gate2, m_gla_b_gate, m_gla_o_gain, m_gla_w_out, m_sgu_w_in, m_sgu_ln_gain, m_sgu_ln_bias, m_sgu_w_spatial, m_sgu_b_spatial, m_sgu_w_out, v_norm_pre, v_norm_post, v_gla_w_in, v_gla_w_gate2, v_gla_b_gate, v_gla_o_gain, v_gla_w_out, v_sgu_w_in, v_sgu_ln_gain, v_sgu_ln_bias, v_sgu_w_spatial, v_sgu_b_spatial, v_sgu_w_out):
    _, t, d = x.shape
    dk = d // 2
    gla_cols = gla_w_in.shape[2] * N_CHIPS
    gla_main = gla_cols - GLA_GATE_RANK
    gla_pad = gla_main + LANES
    chip = 2 * lax.axis_index("x") + lax.axis_index("y")
    core = lax.axis_index("c")
    half_idx = core.astype(jnp.int32).reshape(1)

    x0 = x[0]
    target = loss_target[0]

    small_shard = _pack([gla_w_gate2[0], sgu_ln_gain[0], sgu_ln_bias[0]])
    g_wi_g, g_wo_g, g_wi_s, g_wo_s, g_small = _all_gather_chip(
        [gla_w_in[0].astype(BF16), gla_w_out[0].astype(BF16), sgu_w_in[0].astype(BF16), sgu_w_out[0].astype(BF16),
         small_shard], name="gather_weights")
    w_in_g = jnp.pad(g_wi_g.transpose(1, 0, 2).reshape(d, gla_cols), ((0, 0), (0, gla_pad - gla_cols)))
    w_out_g = g_wo_g.reshape(d, d)
    w_out_s = g_wo_s.reshape(d, d)
    shard_shapes = [gla_w_gate2.shape[1:], sgu_ln_gain.shape[1:], sgu_ln_bias.shape[1:]]
    per_chip = [_unpack(g_small[j], shard_shapes) for j in range(N_CHIPS)]
    w2_full = jnp.concatenate([p[0] for p in per_chip], axis=1)
    ln_gain = jnp.concatenate([p[1] for p in per_chip], axis=0)[None, :]
    ln_bias = jnp.concatenate([p[2] for p in per_chip], axis=0)[None, :]
    w2p = jnp.pad(w2_full, ((0, LANES - GLA_GATE_RANK), (0, 0)))

    pos_chunk = jnp.arange(SGU_BLOCK) // CHUNK
    mask = pos_chunk[:, None] >= pos_chunk[None, :]
    ws_masked = jnp.where(mask[None], sgu_w_spatial[0], 0.0)
    ws_masked_t = ws_masked.transpose(0, 2, 1)
    bs_t = sgu_b_spatial[0].T

    h0 = _norm_pre(x0, norm_pre[0:1], name="pre0")
    proj0 = _matmul(h0, w_in_g, mode="nn", out_dtype=F32, name="gla_in", tn=896)
    o0, a0, s_before, s_final = _gla_fwd(proj0, w2p, gla_b_gate, gla_o_gain, name="gla_scan")
    y0 = _matmul(a0, w_out_g, mode="nn", out_dtype=F32, name="gla_out")
    x1, h1 = _post_then_pre(x0, y0, norm_post[0:1], norm_pre[1:2], name="post0_pre1")
    proj1 = _matmul(h1, g_wi_s, mode="nn", out_dtype=F32, name="sgu_in", b_shards=True)
    a1 = _sgu_fwd(proj1, ln_gain, ln_bias, ws_masked, bs_t, name="sgu_gate")
    y1 = _matmul(a1, w_out_s, mode="nn", out_dtype=F32, name="sgu_out")
    loss_part, dx2, dy1, d_post1 = _loss_head(x1, y1, norm_post[1:2], target, name="loss_head")

    dw_out_s = _matmul(a1, dy1, mode="tn", out_dtype=F32, name="d_sgu_w_out")
    da1 = _matmul(dy1, w_out_s, mode="nt", out_dtype=F32, name="d_sgu_act")
    dproj1, d_ws, d_bs_t, d_lg, d_lb = _sgu_bwd(da1, proj1, ln_gain, ln_bias, ws_masked, ws_masked_t, bs_t,
                                                name="sgu_gate_bwd")
    dw_in_s = _matmul(h1, dproj1, mode="tn", out_dtype=F32, name="d_sgu_w_in", out_shards=True)
    dh1 = _matmul(dproj1, g_wi_s, mode="nt", out_dtype=F32, name="d_sgu_h", b_shards=True, tk=1536)
    dx1, dy0, d_pre1, d_post0 = _mid_bwd(dx2, dh1, x1, norm_pre[1:2], y0, norm_post[0:1], name="pre1_post0_bwd")
    dw_out_g = _matmul(a0, dy0, mode="tn", out_dtype=F32, name="d_gla_w_out")
    da0 = _matmul(dy0, w_out_g, mode="nt", out_dtype=F32, name="d_gla_act")
    dproj0, d_og, d_bg, d_w2p = _gla_bwd(da0, o0, proj0, w2p, gla_b_gate, gla_o_gain, s_before, s_final,
                                         name="gla_scan_bwd")
    dw_in_g_full = _matmul(h0, dproj0, mode="tn", out_dtype=F32, name="d_gla_w_in", tn=896)
    dh0 = _matmul(dproj0, w_in_g, mode="nt", out_dtype=F32, name="d_gla_h", tk=896)
    grad_x, d_pre0 = _first_bwd(dx1, dh0, x0, norm_pre[0:1], name="pre0_bwd")

    dw_in_g = dw_in_g_full[:, :gla_cols].reshape(d, N_CHIPS, gla_cols // N_CHIPS).transpose(1, 0, 2)
    big = [dw_in_g, dw_out_g.reshape(N_CHIPS, d // N_CHIPS, d), dw_in_s, dw_out_s.reshape(N_CHIPS, d // N_CHIPS, d)]
    peer = _sibling_swap_halves(big, name="grads_to_sibling")
    pair = [_pair_sum_bf16(b, half_idx, p, name=f"pair_sum_{i}") for i, (b, p) in enumerate(zip(big, peer))]
    landed = _chip_scatter(pair, name="grads_to_chips")
    mine = [_stack_sum(q, name=f"chip_sum_{i}") for i, q in enumerate(landed)]
    g_wi_gla, g_wo_gla, g_wi_sgu, g_wo_sgu = _sibling_share_halves(mine, name="grads_from_sibling")

    small_shapes = [norm_pre.shape, norm_post.shape, gla_b_gate.shape, gla_o_gain.shape, sgu_w_spatial.shape,
                    sgu_b_spatial.shape, (1, GLA_GATE_RANK, dk), (1, d), (1, d), (1, LANES)]
    d_pre = jnp.concatenate([d_pre0, d_pre1], axis=0)
    d_post = jnp.concatenate([d_post0, d_post1], axis=0)
    d_wsp = jnp.where(mask[None], d_ws, 0.0)[None]
    small_part = _pack([d_pre, d_post, d_bg, d_og, d_wsp, d_bs_t.T[None], d_w2p[:GLA_GATE_RANK][None], d_lg, d_lb,
                        loss_part])
    small_all = _all_gather_dev(small_part, name="gather_small_grads")
    small_sum = _stack_sum(small_all, name="small_sum")
    (g_pre, g_post, g_bg, g_og, g_wsp, g_bsp, g_w2_full, g_lg_full, g_lb_full, loss_vec) = _unpack(small_sum, small_shapes)
    loss = loss_vec[0, 0]
    g_w2 = lax.dynamic_slice_in_dim(g_w2_full, chip * (dk // N_CHIPS), dk // N_CHIPS, axis=2)
    g_lg = lax.dynamic_slice_in_dim(g_lg_full, chip * (d // N_CHIPS), d // N_CHIPS, axis=1)
    g_lb = lax.dynamic_slice_in_dim(g_lb_full, chip * (d // N_CHIPS), d // N_CHIPS, axis=1)

    def big_update(w, g, m, v, name):
        dl, mn, vn = _adamw(w[0], g, m[0], v[0], name=name)
        return g[None], dl[None], mn[None], vn[None]

    u_wi_gla = big_update(gla_w_in, g_wi_gla, m_gla_w_in, v_gla_w_in, "adamw_gla_w_in")
    u_wo_gla = big_update(gla_w_out, g_wo_gla, m_gla_w_out, v_gla_w_out, "adamw_gla_w_out")
    u_wi_sgu = big_update(sgu_w_in, g_wi_sgu, m_sgu_w_in, v_sgu_w_in, "adamw_sgu_w_in")
    u_wo_sgu = big_update(sgu_w_out, g_wo_sgu, m_sgu_w_out, v_sgu_w_out, "adamw_sgu_w_out")

    small_w = [norm_pre, norm_post, gla_b_gate, gla_o_gain, sgu_w_spatial, sgu_b_spatial, gla_w_gate2, sgu_ln_gain,
               sgu_ln_bias]
    small_g = [g_pre, g_post, g_bg, g_og, g_wsp, g_bsp, g_w2, g_lg, g_lb]
    small_m = [m_norm_pre, m_norm_post, m_gla_b_gate, m_gla_o_gain, m_sgu_w_spatial, m_sgu_b_spatial, m_gla_w_gate2,
               m_sgu_ln_gain, m_sgu_ln_bias]
    small_v = [v_norm_pre, v_norm_post, v_gla_b_gate, v_gla_o_gain, v_sgu_w_spatial, v_sgu_b_spatial, v_gla_w_gate2,
               v_sgu_ln_gain, v_sgu_ln_bias]
    own_shapes = [w.shape for w in small_w]
    s_dl, s_m, s_v = _adamw(_pack(small_w), _pack(small_g), _pack(small_m), _pack(small_v), name="adamw_small")
    dl_s, m_s, v_s = _unpack(s_dl, own_shapes), _unpack(s_m, own_shapes), _unpack(s_v, own_shapes)

    def ordered(small, kind):
        pre, post, bg, og, wsp, bsp, w2, lg, lb = small
        return [pre, post, u_wi_gla[kind], w2, bg, og, u_wo_gla[kind], u_wi_sgu[kind], lg, lb, wsp, bsp, u_wo_sgu[kind]]

    return (loss, grad_x[None], *ordered(small_g, 0), *ordered(dl_s, 1), *ordered(m_s, 2), *ordered(v_s, 3))
```

```python
import functools
import math

import jax
import jax.numpy as jnp
from jax import lax
from jax.experimental import pallas as pl
from jax.experimental.pallas import tpu as pltpu

F32 = jnp.float32
BF16 = jnp.bfloat16
MESH = pl.DeviceIdType.MESH

EPS = 1e-6
CHUNK = 64
GLA_HEADS = 4
GLA_GATE_RANK = 16
GLA_TAU = 16.0
SGU_BLOCK = 128
SGU_GROUPS = 8
N_CHIPS = 4
N_DEV = 8
LANES = 128

ADAM_LR = 0.001
ADAM_B1 = 0.9
ADAM_B2 = 0.999
ADAM_EPS = 1e-08
ADAM_WD = 0.01
ADAM_STEP = 10

VMEM_LIMIT = 56 * 1024 * 1024


def _cparams(sem=None):
    return pltpu.CompilerParams(dimension_semantics=sem, vmem_limit_bytes=VMEM_LIMIT)


def _pick(n, cap, unit=LANES):
    best = None
    for t in range(unit, min(n, cap) + 1, unit):
        if n % t == 0:
            best = t
    assert best is not None, (n, cap, unit)
    return best


def _dot(a, b, dims):
    return lax.dot_general(a, b, (dims, ((), ())), preferred_element_type=F32)


def _dot_nn(a, b):
    return _dot(a, b, ((1,), (0,)))


def _dot_nt(a, b):
    return _dot(a, b, ((1,), (1,)))


def _dot_tn(a, b):
    return _dot(a, b, ((0,), (0,)))


def _matmul(a, b, *, mode, out_dtype, name, tm=1024, tn=512, tk=2048, b_shards=False, out_shards=False):
    if mode == "tn":
        K, M = a.shape
    else:
        M, K = a.shape
    if b_shards:
        ns, br, bc = b.shape
        if mode == "nt":
            N, Kb = br, ns * bc
        else:
            Kb, N = br, ns * bc
    else:
        if mode == "nt":
            N, Kb = b.shape
        else:
            Kb, N = b.shape
    assert K == Kb, (a.shape, b.shape, mode)
    tm = _pick(M, tm)
    tk = _pick(K, tk)
    if b_shards and mode != "nt":
        tn = _pick(bc, tn)
    elif out_shards:
        tn = _pick(N // N_CHIPS, tn)
    else:
        tn = _pick(N, tn)
    if b_shards and mode == "nt":
        tk = _pick(bc, tk)
    nk = K // tk
    grid = (M // tm, N // tn, nk)

    if mode == "tn":
        a_spec = pl.BlockSpec((tk, tm), lambda i, j, k: (k, i))
    else:
        a_spec = pl.BlockSpec((tm, tk), lambda i, j, k: (i, k))
    if b_shards:
        if mode == "nt":
            per = bc // tk
            b_spec = pl.BlockSpec((None, tn, tk), lambda i, j, k: (k // per, j, k % per))
        else:
            per = bc // tn
            b_spec = pl.BlockSpec((None, tk, tn), lambda i, j, k: (j // per, k, j % per))
    elif mode == "nt":
        b_spec = pl.BlockSpec((tn, tk), lambda i, j, k: (j, k))
    else:
        b_spec = pl.BlockSpec((tk, tn), lambda i, j, k: (k, j))
    if out_shards:
        per_o = (N // N_CHIPS) // tn
        out_spec = pl.BlockSpec((None, tm, tn), lambda i, j, k: (j // per_o, i, j % per_o))
        out_shape = jax.ShapeDtypeStruct((N_CHIPS, M, N // N_CHIPS), out_dtype)
    else:
        out_spec = pl.BlockSpec((tm, tn), lambda i, j, k: (i, j))
        out_shape = jax.ShapeDtypeStruct((M, N), out_dtype)

    dims = {"nn": ((1,), (0,)), "nt": ((1,), (1,)), "tn": ((0,), (0,))}[mode]

    def body(a_ref, b_ref, o_ref, *scratch):
        part = _dot(a_ref[...].astype(BF16), b_ref[...].astype(BF16), dims)
        if nk == 1:
            o_ref[...] = part.astype(out_dtype)
        else:
            acc_ref, = scratch
            k = pl.program_id(2)

            @pl.when(k == 0)
            def _():
                acc_ref[...] = part

            @pl.when(k > 0)
            def _():
                acc_ref[...] += part

            @pl.when(k == nk - 1)
            def _():
                o_ref[...] = acc_ref[...].astype(out_dtype)

    return pl.pallas_call(
        body, name=name, grid=grid, in_specs=[a_spec, b_spec], out_specs=out_spec, out_shape=out_shape,
        scratch_shapes=[] if nk == 1 else [pltpu.VMEM((tm, tn), F32)],
        compiler_params=_cparams(("parallel", "parallel", "arbitrary")),
    )(a, b)


def _rstd(x):
    return lax.rsqrt(jnp.mean(x * x, axis=-1, keepdims=True) + EPS)


def _row_spec(tr, d):
    return pl.BlockSpec((tr, d), lambda i: (i, 0))


def _vec_spec(d):
    return pl.BlockSpec((1, d), lambda i: (0, 0))


def _acc_rows(ref, i, val, cols=slice(None)):
    @pl.when(i == 0)
    def _():
        ref[:, cols] = val

    @pl.when(i > 0)
    def _():
        ref[:, cols] += val


def _norm_pre(x, gain, *, name, tr=256):
    t, d = x.shape
    tr = _pick(t, tr, 8)

    def body(x_ref, g_ref, h_ref):
        xv = x_ref[...]
        h_ref[...] = (xv * _rstd(xv) * g_ref[...]).astype(BF16)

    return pl.pallas_call(
        body, name=name, grid=(t // tr,), in_specs=[_row_spec(tr, d), _vec_spec(d)], out_specs=_row_spec(tr, d),
        out_shape=jax.ShapeDtypeStruct((t, d), BF16), compiler_params=_cparams(("parallel",)),
    )(x, gain)


def _post_then_pre(x, y, post_gain, pre_gain, *, name, tr=256):
    t, d = x.shape
    tr = _pick(t, tr, 8)

    def body(x_ref, y_ref, pg_ref, ng_ref, xn_ref, h_ref):
        yv = y_ref[...]
        xn = x_ref[...] + yv * _rstd(yv) * pg_ref[...]
        xn_ref[...] = xn
        h_ref[...] = (xn * _rstd(xn) * ng_ref[...]).astype(BF16)

    return pl.pallas_call(
        body, name=name, grid=(t // tr,),
        in_specs=[_row_spec(tr, d), _row_spec(tr, d), _vec_spec(d), _vec_spec(d)],
        out_specs=[_row_spec(tr, d), _row_spec(tr, d)],
        out_shape=[jax.ShapeDtypeStruct((t, d), F32), jax.ShapeDtypeStruct((t, d), BF16)],
        compiler_params=_cparams(("parallel",)),
    )(x, y, post_gain, pre_gain)


def _norm_bwd(dy, n, r, gain):
    dn = dy * gain
    return r * (dn - n * jnp.mean(dn * n, axis=-1, keepdims=True))


def _loss_head(x, y, post_gain, target, *, name, tr=256):
    t, d = x.shape
    tr = _pick(t, tr, 8)

    def body(x_ref, y_ref, pg_ref, t_ref, loss_ref, dx_ref, dy_ref, dpg_ref):
        i = pl.program_id(0)
        yv = y_ref[...]
        r = _rstd(yv)
        n = yv * r
        err = x_ref[...] + n * pg_ref[...] - t_ref[...]
        dx = err * (1.0 / d)
        dx_ref[...] = dx
        part = 0.5 * jnp.sum(jnp.mean(err * err, axis=-1, keepdims=True), axis=0, keepdims=True)
        _acc_rows(loss_ref, i, jnp.broadcast_to(part, (1, LANES)))
        _acc_rows(dpg_ref, i, jnp.sum(dx * n, axis=0, keepdims=True))
        dy_ref[...] = _norm_bwd(dx, n, r, pg_ref[...]).astype(BF16)

    return pl.pallas_call(
        body, name=name, grid=(t // tr,),
        in_specs=[_row_spec(tr, d), _row_spec(tr, d), _vec_spec(d), _row_spec(tr, d)],
        out_specs=[_vec_spec(LANES), _row_spec(tr, d), _row_spec(tr, d), _vec_spec(d)],
        out_shape=[jax.ShapeDtypeStruct((1, LANES), F32), jax.ShapeDtypeStruct((t, d), F32),
                   jax.ShapeDtypeStruct((t, d), BF16), jax.ShapeDtypeStruct((1, d), F32)],
        compiler_params=_cparams(("arbitrary",)),
    )(x, y, post_gain, target)


def _mid_bwd(dx_out, dh, x, pre_gain, y_prev, post_gain_prev, *, name, tr=256):
    t, d = x.shape
    tr = _pick(t, tr, 8)

    def body(dxo_ref, dh_ref, x_ref, ng_ref, y_ref, pg_ref, dx_ref, dy_ref, dng_ref, dpg_ref):
        i = pl.program_id(0)
        xv = x_ref[...]
        r = _rstd(xv)
        xh = xv * r
        dhv = dh_ref[...]
        _acc_rows(dng_ref, i, jnp.sum(dhv * xh, axis=0, keepdims=True))
        dx = dxo_ref[...] + _norm_bwd(dhv, xh, r, ng_ref[...])
        dx_ref[...] = dx
        yv = y_ref[...]
        ry = _rstd(yv)
        n = yv * ry
        _acc_rows(dpg_ref, i, jnp.sum(dx * n, axis=0, keepdims=True))
        dy_ref[...] = _norm_bwd(dx, n, ry, pg_ref[...]).astype(BF16)

    return pl.pallas_call(
        body, name=name, grid=(t // tr,),
        in_specs=[_row_spec(tr, d), _row_spec(tr, d), _row_spec(tr, d), _vec_spec(d), _row_spec(tr, d), _vec_spec(d)],
        out_specs=[_row_spec(tr, d), _row_spec(tr, d), _vec_spec(d), _vec_spec(d)],
        out_shape=[jax.ShapeDtypeStruct((t, d), F32), jax.ShapeDtypeStruct((t, d), BF16),
                   jax.ShapeDtypeStruct((1, d), F32), jax.ShapeDtypeStruct((1, d), F32)],
        compiler_params=_cparams(("arbitrary",)),
    )(dx_out, dh, x, pre_gain, y_prev, post_gain_prev)


def _first_bwd(dx_out, dh, x, pre_gain, *, name, tr=256):
    t, d = x.shape
    tr = _pick(t, tr, 8)

    def body(dxo_ref, dh_ref, x_ref, ng_ref, dx_ref, dng_ref):
        i = pl.program_id(0)
        xv = x_ref[...]
        r = _rstd(xv)
        xh = xv * r
        dhv = dh_ref[...]
        _acc_rows(dng_ref, i, jnp.sum(dhv * xh, axis=0, keepdims=True))
        dx_ref[...] = dxo_ref[...] + _norm_bwd(dhv, xh, r, ng_ref[...])

    return pl.pallas_call(
        body, name=name, grid=(t // tr,),
        in_specs=[_row_spec(tr, d), _row_spec(tr, d), _row_spec(tr, d), _vec_spec(d)],
        out_specs=[_row_spec(tr, d), _vec_spec(d)],
        out_shape=[jax.ShapeDtypeStruct((t, d), F32), jax.ShapeDtypeStruct((1, d), F32)],
        compiler_params=_cparams(("arbitrary",)),
    )(dx_out, dh, x, pre_gain)


def _sigmoid(x):
    return 1.0 / (1.0 + jnp.exp(-x))


def _log_sigmoid(x):
    return jnp.minimum(x, 0.0) - jnp.log(1.0 + jnp.exp(-jnp.abs(x)))


_GELU_C = math.sqrt(2.0 / math.pi)


def _gelu_parts(x):
    x2 = x * x
    th = jnp.tanh(_GELU_C * (x + 0.044715 * x * x2))
    val = 0.5 * x * (1.0 + th)
    grad = 0.5 * (1.0 + th) + 0.5 * x * (1.0 - th * th) * (_GELU_C * (1.0 + 3.0 * 0.044715 * x2))
    return val, grad


def _split3(x):
    hi = x.astype(BF16)
    r1 = x - hi.astype(F32)
    mid = r1.astype(BF16)
    lo = (r1 - mid.astype(F32)).astype(BF16)
    return hi, mid, lo


def _tri_matmul(tri_bf16, x):
    hi, mid, lo = _split3(x)
    return _dot_nn(tri_bf16, hi) + _dot_nn(tri_bf16, mid) + _dot_nn(tri_bf16, lo)


def _gla_dims(d):
    dk, dv = d // 2, d
    return dk, dv, dk // GLA_HEADS, dv // GLA_HEADS


def _gla_gates(glr, k, w2_ref, b_ref):
    z = _dot_nn(glr.astype(BF16), w2_ref[...].astype(BF16)) + b_ref[...]
    la = _log_sigmoid(z) * (1.0 / GLA_TAU)
    row = lax.broadcasted_iota(jnp.int32, (CHUNK, CHUNK), 0)
    col = lax.broadcasted_iota(jnp.int32, (CHUNK, CHUNK), 1)
    incl = (row >= col).astype(BF16)
    bcum = _tri_matmul(incl, la)
    b_end = bcum[CHUNK - 1:CHUNK, :]
    e_rest = jnp.exp(b_end - bcum)
    return z, e_rest, k * e_rest, jnp.exp(b_end)


def _gla_fwd(proj, w2p, b_gate, o_gain, *, name):
    t, wcols = proj.shape
    d = o_gain.shape[1]
    dk, dv, dkh, dvh = _gla_dims(d)
    nc = t // CHUNK
    c_k, c_v, c_g, c_r = dk, 2 * dk, 2 * dk + dv, 2 * dk + 2 * dv
    scale = dkh ** -0.5

    def body(p_ref, w2_ref, b_ref, og_ref, o_ref, a_ref, sb_ref, sfin_ref, s_ref):
        i = pl.program_id(0)

        @pl.when(i == 0)
        def _():
            s_ref[...] = jnp.zeros_like(s_ref)

        q = p_ref[:, 0:dk] * scale
        k = p_ref[:, c_k:c_k + dk]
        glr = p_ref[:, c_r:c_r + LANES]
        _, _, kdec, decay = _gla_gates(glr, k, w2_ref, b_ref)
        for h in range(GLA_HEADS):
            ks = slice(h * dkh, (h + 1) * dkh)
            vs = slice(h * dvh, (h + 1) * dvh)
            v_h = p_ref[:, c_v + h * dvh:c_v + (h + 1) * dvh]
            g_h = p_ref[:, c_g + h * dvh:c_g + (h + 1) * dvh]
            s_old = s_ref[h]
            sb_ref[0, h] = s_old
            s_new = s_old * decay[:, ks] + _dot_tn(v_h.astype(BF16), kdec[:, ks].astype(BF16))
            s_ref[h] = s_new
            o_h = _dot_nt(q[:, ks].astype(BF16), s_new.astype(BF16))
            o_ref[:, vs] = o_h
            on = o_h * _rstd(o_h)
            a_ref[:, vs] = (on * og_ref[:, vs] * (g_h * _sigmoid(g_h))).astype(BF16)

        @pl.when(i == nc - 1)
        def _():
            sfin_ref[...] = s_ref[...]

    full = lambda *shape: pl.BlockSpec(shape, lambda i: (0,) * len(shape))
    return pl.pallas_call(
        body, name=name, grid=(nc,),
        in_specs=[pl.BlockSpec((CHUNK, wcols), lambda i: (i, 0)), full(LANES, dk), full(1, dk), full(1, dv)],
        out_specs=[pl.BlockSpec((CHUNK, dv), lambda i: (i, 0)), pl.BlockSpec((CHUNK, dv), lambda i: (i, 0)),
                   pl.BlockSpec((1, GLA_HEADS, dvh, dkh), lambda i: (i, 0, 0, 0)), full(GLA_HEADS, dvh, dkh)],
        out_shape=[jax.ShapeDtypeStruct((t, dv), F32), jax.ShapeDtypeStruct((t, dv), BF16),
                   jax.ShapeDtypeStruct((nc, GLA_HEADS, dvh, dkh), F32),
                   jax.ShapeDtypeStruct((GLA_HEADS, dvh, dkh), F32)],
        scratch_shapes=[pltpu.VMEM((GLA_HEADS, dvh, dkh), F32)],
        compiler_params=_cparams(("arbitrary",)),
    )(proj, w2p, b_gate, o_gain)


def _gla_bwd(da, o, proj, w2p, b_gate, o_gain, s_before, s_final, *, name):
    t, wcols = proj.shape
    d = o_gain.shape[1]
    dk, dv, dkh, dvh = _gla_dims(d)
    nc = t // CHUNK
    c_k, c_v, c_g, c_r = dk, 2 * dk, 2 * dk + dv, 2 * dk + 2 * dv
    scale = dkh ** -0.5

    def body(da_ref, o_ref, p_ref, w2_ref, b_ref, og_ref, sb_ref, sfin_ref,
             dp_ref, dog_ref, db_ref, dw2_ref, s_ref, gc_ref, dkd_ref):
        i = pl.program_id(0)

        @pl.when(i == 0)
        def _():
            s_ref[...] = sfin_ref[...]
            gc_ref[...] = jnp.zeros_like(gc_ref)

        q = p_ref[:, 0:dk] * scale
        k = p_ref[:, c_k:c_k + dk]
        glr = p_ref[:, c_r:c_r + LANES]
        z, e_rest, kdec, decay = _gla_gates(glr, k, w2_ref, b_ref)
        ddecay = []
        for h in range(GLA_HEADS):
            ks = slice(h * dkh, (h + 1) * dkh)
            vs = slice(h * dvh, (h + 1) * dvh)
            v_h = p_ref[:, c_v + h * dvh:c_v + (h + 1) * dvh]
            g_h = p_ref[:, c_g + h * dvh:c_g + (h + 1) * dvh]
            da_h = da_ref[:, vs]
            o_h = o_ref[:, vs]
            og_h = og_ref[:, vs]
            r = _rstd(o_h)
            on = o_h * r
            sg = _sigmoid(g_h)
            silu = g_h * sg
            _acc_rows(dog_ref, i, jnp.sum(da_h * silu * on, axis=0, keepdims=True), vs)
            dp_ref[:, c_g + h * dvh:c_g + (h + 1) * dvh] = (
                da_h * (on * og_h) * (sg * (1.0 + g_h * (1.0 - sg)))).astype(BF16)
            don = da_h * silu * og_h
            do_h = (r * (don - on * jnp.mean(don * on, axis=-1, keepdims=True))).astype(BF16)
            s_cur = s_ref[h]
            dp_ref[:, ks] = (_dot_nn(do_h, s_cur.astype(BF16)) * scale).astype(BF16)
            g_tot = gc_ref[h] + _dot_tn(do_h, q[:, ks].astype(BF16))
            g_bf = g_tot.astype(BF16)
            dkd_ref[:, ks] = _dot_nn(v_h.astype(BF16), g_bf)
            dp_ref[:, c_v + h * dvh:c_v + (h + 1) * dvh] = _dot_nt(kdec[:, ks].astype(BF16), g_bf).astype(BF16)
            s_prev = sb_ref[0, h]
            ddecay.append(jnp.sum(g_tot * s_prev, axis=0, keepdims=True))
            gc_ref[h] = g_tot * decay[:, ks]
            s_ref[h] = s_prev
        dkdec = dkd_ref[...]
        dp_ref[:, c_k:c_k + dk] = (dkdec * e_rest).astype(BF16)
        d_e = dkdec * kdec
        row = lax.broadcasted_iota(jnp.int32, (CHUNK, CHUNK), 0)
        col = lax.broadcasted_iota(jnp.int32, (CHUNK, CHUNK), 1)
        excl = (row > col).astype(BF16)
        dla = jnp.concatenate(ddecay, axis=1) * decay + _tri_matmul(excl, d_e)
        dz = dla * (1.0 / GLA_TAU) * (1.0 - _sigmoid(z))
        _acc_rows(db_ref, i, jnp.sum(dz, axis=0, keepdims=True))
        dz_bf = dz.astype(BF16)
        dw2 = _dot_tn(glr.astype(BF16), dz_bf)

        @pl.when(i == 0)
        def _():
            dw2_ref[...] = dw2

        @pl.when(i > 0)
        def _():
            dw2_ref[...] += dw2

        dp_ref[:, c_r:c_r + LANES] = _dot_nt(dz_bf, w2_ref[...].astype(BF16)).astype(BF16)

    rev = lambda i: (nc - 1 - i, 0)
    full = lambda *shape: pl.BlockSpec(shape, lambda i: (0,) * len(shape))
    return pl.pallas_call(
        body, name=name, grid=(nc,),
        in_specs=[pl.BlockSpec((CHUNK, dv), rev), pl.BlockSpec((CHUNK, dv), rev), pl.BlockSpec((CHUNK, wcols), rev),
                  full(LANES, dk), full(1, dk), full(1, dv),
                  pl.BlockSpec((1, GLA_HEADS, dvh, dkh), lambda i: (nc - 1 - i, 0, 0, 0)), full(GLA_HEADS, dvh, dkh)],
        out_specs=[pl.BlockSpec((CHUNK, wcols), rev), full(1, dv), full(1, dk), full(LANES, dk)],
        out_shape=[jax.ShapeDtypeStruct((t, wcols), BF16), jax.ShapeDtypeStruct((1, dv), F32),
                   jax.ShapeDtypeStruct((1, dk), F32), jax.ShapeDtypeStruct((LANES, dk), F32)],
        scratch_shapes=[pltpu.VMEM((GLA_HEADS, dvh, dkh), F32), pltpu.VMEM((GLA_HEADS, dvh, dkh), F32),
                        pltpu.VMEM((CHUNK, dk), F32)],
        compiler_params=_cparams(("arbitrary",)),
    )(da, o, proj, w2p, b_gate, o_gain, s_before, s_final)


def _sgu_mid(p_ref, lg_ref, lb_ref, ws_ref, bst_ref, w):
    gd = w // SGU_GROUPS
    u_act, du_fac = _gelu_parts(p_ref[:, 0:w])
    vf, dv_fac = _gelu_parts(p_ref[:, w:2 * w])
    mu = jnp.mean(vf, axis=-1, keepdims=True)
    cen = vf - mu
    rstd = lax.rsqrt(jnp.mean(cen * cen, axis=-1, keepdims=True) + EPS)
    xh = cen * rstd
    vn = (xh * lg_ref[...] + lb_ref[...]).astype(BF16)
    vs = [_dot_nn(ws_ref[g].astype(BF16), vn[:, g * gd:(g + 1) * gd]) + bst_ref[:, g:g + 1]
          for g in range(SGU_GROUPS)]
    return u_act, du_fac, dv_fac, rstd, xh, vn, vs


def _sgu_fwd(proj, ln_gain, ln_bias, ws_masked, bs_t, *, name):
    t, w3 = proj.shape
    w = w3 // 3
    gd = w // SGU_GROUPS
    nb = t // SGU_BLOCK

    def body(p_ref, lg_ref, lb_ref, ws_ref, bst_ref, a_ref):
        u_act, _, _, _, _, _, vs = _sgu_mid(p_ref, lg_ref, lb_ref, ws_ref, bst_ref, w)
        for g in range(SGU_GROUPS):
            cs = slice(g * gd, (g + 1) * gd)
            gate = p_ref[:, 2 * w + g * gd:2 * w + (g + 1) * gd]
            a_ref[:, cs] = (u_act[:, cs] * vs[g] * (gate * _sigmoid(gate))).astype(BF16)

    full = lambda *shape: pl.BlockSpec(shape, lambda i: (0,) * len(shape))
    return pl.pallas_call(
        body, name=name, grid=(nb,),
        in_specs=[pl.BlockSpec((SGU_BLOCK, w3), lambda i: (i, 0)), full(1, w), full(1, w),
                  full(SGU_GROUPS, SGU_BLOCK, SGU_BLOCK), full(SGU_BLOCK, SGU_GROUPS)],
        out_specs=pl.BlockSpec((SGU_BLOCK, w), lambda i: (i, 0)),
        out_shape=jax.ShapeDtypeStruct((t, w), BF16),
        compiler_params=_cparams(("parallel",)),
    )(proj, ln_gain, ln_bias, ws_masked, bs_t)


def _sgu_bwd(da, proj, ln_gain, ln_bias, ws_masked, ws_masked_t, bs_t, *, name):
    t, w3 = proj.shape
    w = w3 // 3
    gd = w // SGU_GROUPS
    nb = t // SGU_BLOCK

    def body(da_ref, p_ref, lg_ref, lb_ref, ws_ref, wst_ref, bst_ref, dp_ref, dws_ref, dbst_ref, dlg_ref, dlb_ref,
             dvn_ref):
        i = pl.program_id(0)
        u_act, du_fac, dv_fac, rstd, xh, vn, vs = _sgu_mid(p_ref, lg_ref, lb_ref, ws_ref, bst_ref, w)
        for g in range(SGU_GROUPS):
            cs = slice(g * gd, (g + 1) * gd)
            gate = p_ref[:, 2 * w + g * gd:2 * w + (g + 1) * gd]
            sg = _sigmoid(gate)
            silu = gate * sg
            da_g = da_ref[:, cs]
            ua_g = u_act[:, cs]
            dp_ref[:, cs] = (da_g * vs[g] * silu * du_fac[:, cs]).astype(BF16)
            dp_ref[:, 2 * w + g * gd:2 * w + (g + 1) * gd] = (
                da_g * ua_g * vs[g] * (sg * (1.0 + gate * (1.0 - sg)))).astype(BF16)
            dvs = da_g * ua_g * silu
            dvs_bf = dvs.astype(BF16)
            dvn_ref[:, cs] = _dot_nn(wst_ref[g].astype(BF16), dvs_bf)
            dws = _dot_nt(dvs_bf, vn[:, cs])
            dbs = jnp.sum(dvs, axis=1, keepdims=True)

            @pl.when(i == 0)
            def _():
                dws_ref[g] = dws
                dbst_ref[:, g:g + 1] = dbs

            @pl.when(i > 0)
            def _():
                dws_ref[g] += dws
                dbst_ref[:, g:g + 1] += dbs

        dvn = dvn_ref[...]
        _acc_rows(dlg_ref, i, jnp.sum(dvn * xh, axis=0, keepdims=True))
        _acc_rows(dlb_ref, i, jnp.sum(dvn, axis=0, keepdims=True))
        dxh = dvn * lg_ref[...]
        dvf = rstd * (dxh - jnp.mean(dxh, axis=-1, keepdims=True)
                      - xh * jnp.mean(dxh * xh, axis=-1, keepdims=True))
        dp_ref[:, w:2 * w] = (dvf * dv_fac).astype(BF16)

    full = lambda *shape: pl.BlockSpec(shape, lambda i: (0,) * len(shape))
    return pl.pallas_call(
        body, name=name, grid=(nb,),
        in_specs=[pl.BlockSpec((SGU_BLOCK, w), lambda i: (i, 0)), pl.BlockSpec((SGU_BLOCK, w3), lambda i: (i, 0)),
                  full(1, w), full(1, w), full(SGU_GROUPS, SGU_BLOCK, SGU_BLOCK),
                  full(SGU_GROUPS, SGU_BLOCK, SGU_BLOCK), full(SGU_BLOCK, SGU_GROUPS)],
        out_specs=[pl.BlockSpec((SGU_BLOCK, w3), lambda i: (i, 0)), full(SGU_GROUPS, SGU_BLOCK, SGU_BLOCK),
                   full(SGU_BLOCK, SGU_GROUPS), full(1, w), full(1, w)],
        out_shape=[jax.ShapeDtypeStruct((t, w3), BF16), jax.ShapeDtypeStruct((SGU_GROUPS, SGU_BLOCK, SGU_BLOCK), F32),
                   jax.ShapeDtypeStruct((SGU_BLOCK, SGU_GROUPS), F32), jax.ShapeDtypeStruct((1, w), F32),
                   jax.ShapeDtypeStruct((1, w), F32)],
        scratch_shapes=[pltpu.VMEM((SGU_BLOCK, w), F32)],
        compiler_params=_cparams(("arbitrary",)),
    )(da, proj, ln_gain, ln_bias, ws_masked, ws_masked_t, bs_t)


def _tile2d(rows, cols, block_bytes, row_unit):
    if rows % row_unit == 0:
        return _pick(rows, max(row_unit, block_bytes // (4 * cols)), row_unit), cols
    return rows, _pick(cols, max(LANES, block_bytes // (4 * rows)))


def _adamw(w, g, m, v, *, name, block_bytes=1 << 20):
    rows, cols = w.shape
    tr, tc = _tile2d(rows, cols, block_bytes, 8)

    def body(w_ref, g_ref, m_ref, v_ref, go_ref, d_ref, mo_ref, vo_ref):
        gv = g_ref[...]
        go_ref[...] = gv
        mn = ADAM_B1 * m_ref[...] + (1.0 - ADAM_B1) * gv
        vn = ADAM_B2 * v_ref[...] + (1.0 - ADAM_B2) * (gv * gv)
        m_hat = mn / (1.0 - ADAM_B1 ** ADAM_STEP)
        v_hat = vn / (1.0 - ADAM_B2 ** ADAM_STEP)
        d_ref[...] = -ADAM_LR * (m_hat / (jnp.sqrt(v_hat) + ADAM_EPS) + ADAM_WD * w_ref[...])
        mo_ref[...] = mn
        vo_ref[...] = vn

    spec = pl.BlockSpec((tr, tc), lambda i, j: (i, j))
    return pl.pallas_call(
        body, name=name, grid=(rows // tr, cols // tc), in_specs=[spec] * 4, out_specs=[spec] * 4,
        out_shape=[jax.ShapeDtypeStruct((rows, cols), F32)] * 4,
        compiler_params=_cparams(("parallel", "parallel")),
    )(w, g, m, v)


def _pair_sum_bf16(own, core_idx, peer, *, name, block_bytes=1 << 20):
    s, r, c = own.shape
    hc = c // 2
    tr, tc = _tile2d(r, hc, block_bytes, 16)
    ncb = hc // tc

    def body(h_ref, a_ref, b_ref, o_ref):
        o_ref[...] = (a_ref[...] + b_ref[...]).astype(BF16)

    grid_spec = pltpu.PrefetchScalarGridSpec(
        num_scalar_prefetch=1, grid=(s, r // tr, ncb),
        in_specs=[pl.BlockSpec((None, tr, tc), lambda j, i, k, h: (j, i, h[0] * ncb + k)),
                  pl.BlockSpec((None, tr, tc), lambda j, i, k, h: (j, i, k))],
        out_specs=pl.BlockSpec((None, tr, tc), lambda j, i, k, h: (j, i, k)))
    return pl.pallas_call(
        body, name=name, grid_spec=grid_spec, out_shape=jax.ShapeDtypeStruct((s, r, hc), BF16),
        compiler_params=_cparams(("parallel", "parallel", "parallel")),
    )(core_idx, own, peer)


def _chip_sum(pair, landed, slots, *, name, block_bytes=1 << 20):
    _, r, hc = pair.shape
    tr, tc = _tile2d(r, hc, block_bytes, 16)
    ncb = hc // tc

    def body(s_ref, own_ref, l0_ref, l1_ref, l2_ref, o_ref):
        o_ref[...] = ((own_ref[...].astype(F32) + l0_ref[...].astype(F32)) + l1_ref[...].astype(F32)
                      ) + l2_ref[...].astype(F32)

    def slab(which):
        return pl.BlockSpec((None, tr, tc), lambda i, k, s: (s[which], i, k))

    grid_spec = pltpu.PrefetchScalarGridSpec(
        num_scalar_prefetch=1, grid=(r // tr, ncb),
        in_specs=[slab(0), slab(1), slab(2), slab(3)],
        out_specs=pl.BlockSpec((tr, tc), lambda i, k, s: (i, s[4] * ncb + k)))
    return pl.pallas_call(
        body, name=name, grid_spec=grid_spec, out_shape=jax.ShapeDtypeStruct((r, 2 * hc), F32),
        compiler_params=_cparams(("parallel", "parallel")),
    )(slots, pair, landed, landed, landed)


def _stack_sum(x, *, name, out_dtype=F32, block_bytes=1 << 20):
    s, r, c = x.shape
    tr = _pick(r, max(8, block_bytes // (4 * c)), 16) if r % 16 == 0 else r

    def body(x_ref, o_ref):
        acc = x_ref[0].astype(F32)
        for j in range(1, s):
            acc = acc + x_ref[j].astype(F32)
        o_ref[...] = acc.astype(out_dtype)

    return pl.pallas_call(
        body, name=name, grid=(r // tr,),
        in_specs=[pl.BlockSpec((s, tr, c), lambda i: (0, i, 0))], out_specs=pl.BlockSpec((tr, c), lambda i: (i, 0)),
        out_shape=jax.ShapeDtypeStruct((r, c), out_dtype), compiler_params=_cparams(("parallel",)),
    )(x)


HBM = pl.BlockSpec(memory_space=pltpu.HBM)


def _place():
    x, y, c = lax.axis_index("x"), lax.axis_index("y"), lax.axis_index("c")
    other_chips = [(1 - x, y), (x, 1 - y), (1 - x, 1 - y)]
    return x, y, c, other_chips


def _half_cols(cols, which):
    hc = cols // 2
    return pl.ds(pl.multiple_of(which * hc, LANES), hc)


def _all_gather_chip(shards, *, name):
    n = len(shards)

    def body(*refs):
        ins, outs = refs[:n], refs[n:2 * n]
        send_sems, recv_sems = refs[2 * n:]
        x, y, c, chips = _place()
        me = 2 * x + y

        def landing(a, chip_idx, which):
            return outs[a].at[chip_idx, :, _half_cols(ins[a].shape[1], which)]

        def copy(a, k, src, dst, to):
            return pltpu.make_async_remote_copy(
                src_ref=src, dst_ref=dst, send_sem=send_sems.at[a, k], recv_sem=recv_sems.at[a, k],
                device_id=to, device_id_type=MESH)

        sends = []
        for a in range(n):
            for k, (cx, cy) in enumerate(chips):
                s = copy(a, k, ins[a].at[:, _half_cols(ins[a].shape[1], c)], landing(a, me, c), (cx, cy, c))
                s.start()
                sends.append(s)
        for a in range(n):
            for k, (cx, cy) in enumerate(chips):
                idx = 2 * cx + cy
                copy(a, k, landing(a, idx, c), landing(a, idx, c), (x, y, c)).wait_recv()
                f = copy(a, 3 + k, landing(a, idx, c), landing(a, idx, c), (x, y, 1 - c))
                f.start()
                sends.append(f)
        for a in range(n):
            for k, (cx, cy) in enumerate(chips):
                idx = 2 * cx + cy
                copy(a, 3 + k, landing(a, idx, 1 - c), landing(a, idx, 1 - c), (x, y, c)).wait_recv()
        for s in sends:
            s.wait_send()

    return pl.pallas_call(
        body, name=name, in_specs=[HBM] * n, out_specs=[HBM] * n,
        out_shape=[jax.ShapeDtypeStruct((N_CHIPS,) + s.shape, s.dtype) for s in shards],
        scratch_shapes=[pltpu.SemaphoreType.DMA((n, 6)), pltpu.SemaphoreType.DMA((n, 6))],
    )(*shards)


def _all_gather_dev(x, *, name):
    r, ccols = x.shape

    def body(x_ref, out_ref, send_sems, recv_sems):
        px, py, c, chips = _place()
        me, sibling = (px, py, c), (px, py, 1 - c)

        def slot(qx, qy, qc):
            return out_ref.at[4 * qx + 2 * qy + qc]

        def copy(k, block, to, src=None):
            return pltpu.make_async_remote_copy(
                src_ref=slot(*block) if src is None else src, dst_ref=slot(*block),
                send_sem=send_sems.at[k], recv_sem=recv_sems.at[k], device_id=to, device_id_type=MESH)

        out_ref[4 * px + 2 * py + c] = x_ref[...]
        first = [copy(0, me, sibling, src=x_ref)]
        first += [copy(1 + j, me, (*chip, c), src=x_ref) for j, chip in enumerate(chips)]
        for cp in first:
            cp.start()
        passed = [copy(4 + j, (*chip, c), sibling) for j, chip in enumerate(chips)]
        for j, chip in enumerate(chips):
            copy(1 + j, (*chip, c), me).wait_recv()
            passed[j].start()
        copy(0, sibling, me).wait_recv()
        for j, chip in enumerate(chips):
            copy(4 + j, (*chip, 1 - c), me).wait_recv()
        for cp in first + passed:
            cp.wait_send()

    vmem = pl.BlockSpec(memory_space=pltpu.VMEM)
    return pl.pallas_call(
        body, name=name, in_specs=[vmem], out_specs=vmem, out_shape=jax.ShapeDtypeStruct((N_DEV, r, ccols), x.dtype),
        scratch_shapes=[pltpu.SemaphoreType.DMA((7,)), pltpu.SemaphoreType.DMA((7,))],
        compiler_params=_cparams(),
    )(x)


def _sibling_swap_halves(grads, *, name):
    n = len(grads)

    def body(*refs):
        ins, outs = refs[:n], refs[n:2 * n]
        send_sems, recv_sems = refs[2 * n:]
        x, y, c, _ = _place()
        copies = []
        for a in range(n):
            cp = pltpu.make_async_remote_copy(
                src_ref=ins[a].at[:, :, _half_cols(ins[a].shape[2], 1 - c)], dst_ref=outs[a],
                send_sem=send_sems.at[a], recv_sem=recv_sems.at[a], device_id=(x, y, 1 - c), device_id_type=MESH)
            cp.start()
            copies.append(cp)
        for cp in copies:
            cp.wait()

    return pl.pallas_call(
        body, name=name, in_specs=[HBM] * n, out_specs=[HBM] * n,
        out_shape=[jax.ShapeDtypeStruct((g.shape[0], g.shape[1], g.shape[2] // 2), g.dtype) for g in grads],
        scratch_shapes=[pltpu.SemaphoreType.DMA((n,)), pltpu.SemaphoreType.DMA((n,))],
    )(*grads)


def _chip_scatter(parts, *, name):
    n = len(parts)

    def body(*refs):
        ins, outs = refs[:n], refs[n:2 * n]
        send_sems, recv_sems = refs[2 * n:]
        x, y, c, chips = _place()
        me = 2 * x + y
        copies = []
        for a in range(n):
            for k, (cx, cy) in enumerate(chips):
                cp = pltpu.make_async_remote_copy(
                    src_ref=ins[a].at[2 * cx + cy], dst_ref=outs[a].at[me],
                    send_sem=send_sems.at[a, k], recv_sem=recv_sems.at[a, k], device_id=(cx, cy, c),
                    device_id_type=MESH)
                cp.start()
                copies.append(cp)
        for a in range(n):
            for k, (cx, cy) in enumerate(chips):
                idx = 2 * cx + cy
                pltpu.make_async_remote_copy(
                    src_ref=ins[a].at[idx], dst_ref=outs[a].at[idx], send_sem=send_sems.at[a, k],
                    recv_sem=recv_sems.at[a, k], device_id=(x, y, c), device_id_type=MESH).wait_recv()
        for cp in copies:
            cp.wait_send()

    return pl.pallas_call(
        body, name=name, in_specs=[HBM] * n, out_specs=[HBM] * n,
        out_shape=[jax.ShapeDtypeStruct(p.shape, p.dtype) for p in parts],
        scratch_shapes=[pltpu.SemaphoreType.DMA((n, 3)), pltpu.SemaphoreType.DMA((n, 3))],
    )(*parts)


def _sibling_share_halves(arrays, *, name):
    n = len(arrays)

    def body(*refs):
        bufs = refs[n:2 * n]
        send_sems, recv_sems = refs[2 * n:]
        x, y, c, _ = _place()
        copies = []
        for a in range(n):
            mine = bufs[a].at[:, _half_cols(bufs[a].shape[1], c)]
            cp = pltpu.make_async_remote_copy(
                src_ref=mine, dst_ref=mine, send_sem=send_sems.at[a], recv_sem=recv_sems.at[a],
                device_id=(x, y, 1 - c), device_id_type=MESH)
            cp.start()
            copies.append(cp)
        for a in range(n):
            theirs = bufs[a].at[:, _half_cols(bufs[a].shape[1], 1 - c)]
            pltpu.make_async_remote_copy(
                src_ref=theirs, dst_ref=theirs, send_sem=send_sems.at[a], recv_sem=recv_sems.at[a],
                device_id=(x, y, c), device_id_type=MESH).wait_recv()
        for cp in copies:
            cp.wait_send()

    return pl.pallas_call(
        body, name=name, in_specs=[HBM] * n, out_specs=[HBM] * n,
        out_shape=[jax.ShapeDtypeStruct(h.shape, h.dtype) for h in arrays],
        input_output_aliases={a: a for a in range(n)},
        scratch_shapes=[pltpu.SemaphoreType.DMA((n,)), pltpu.SemaphoreType.DMA((n,))],
    )(*arrays)


def _pack(arrays, rows_multiple=16, width=LANES):
    flat = jnp.concatenate([a.astype(F32).reshape(-1) for a in arrays])
    total = flat.shape[0]
    rows = -(-total // width)
    rows = -(-rows // rows_multiple) * rows_multiple
    return jnp.pad(flat, (0, rows * width - total)).reshape(rows, width)


def _unpack(buf, shapes):
    flat = buf.reshape(-1)
    out, off = [], 0
    for s in shapes:
        n = math.prod(s)
        out.append(flat[off:off + n].reshape(s))
        off += n
    return out


def kernel(x, norm_pre, norm_post, gla_w_in, gla_w_gate2, gla_b_gate, gla_o_gain, gla_w_out, sgu_w_in, sgu_ln_gain, sgu_ln_bias, sgu_w_spatial, sgu_b_spatial, sgu_w_out, loss_target, m_norm_pre, m_norm_post, m_gla_w_in, m_gla_w_gate2, m_gla_b_gate, m_gla_o_gain, m_gla_w_out, m_sgu_w_in, m_sgu_ln_gain, m_sgu_ln_bias, m_sgu_w_spatial, m_sgu_b_spatial, m_sgu_w_out, v_norm_pre, v_norm_post, v_gla_w_in, v_gla_w_gate2, v_gla_b_gate, v_gla_o_gain, v_gla_w_out, v_sgu_w_in, v_sgu_ln_gain, v_sgu_ln_bias, v_sgu_w_spatial, v_sgu_b_spatial, v_sgu_w_out):
    _, t, d = x.shape
    dk = d // 2
    gla_cols = gla_w_in.shape[2] * N_CHIPS
    gla_main = gla_cols - GLA_GATE_RANK
    gla_pad = gla_main + LANES
    chip = 2 * lax.axis_index("x") + lax.axis_index("y")
    core = lax.axis_index("c")
    core_idx = core.astype(jnp.int32).reshape(1)
    others = jnp.arange(N_CHIPS - 1, dtype=jnp.int32)
    others = others + (others >= chip).astype(jnp.int32)
    slots = jnp.concatenate([chip.astype(jnp.int32).reshape(1), others, core_idx])

    x0 = x[0]
    target = loss_target[0]

    wt_in_g, mt_in_g, vt_in_g = gla_w_in[0].T, m_gla_w_in[0].T, v_gla_w_in[0].T

    small_shard = _pack([gla_w_gate2[0], sgu_ln_gain[0], sgu_ln_bias[0]], rows_multiple=8, width=2 * LANES)
    own = [wt_in_g.astype(BF16), gla_w_out[0].astype(BF16), sgu_w_in[0].astype(BF16), sgu_w_out[0].astype(BF16),
           small_shard]
    gathered = _all_gather_chip(own, name="gather_weights")
    g_wi_g, g_wo_g, g_wi_s, g_wo_s, g_small = [
        lax.dynamic_update_slice(g, o[None], (chip, 0, 0)) for g, o in zip(gathered, own)]
    wt_g = jnp.pad(g_wi_g.reshape(gla_cols, d), ((0, gla_pad - gla_cols), (0, 0)))
    w_out_g = g_wo_g.reshape(d, d)
    w_out_s = g_wo_s.reshape(d, d)
    shard_shapes = [gla_w_gate2.shape[1:], sgu_ln_gain.shape[1:], sgu_ln_bias.shape[1:]]
    per_chip = [_unpack(g_small[j], shard_shapes) for j in range(N_CHIPS)]
    w2_full = jnp.concatenate([p[0] for p in per_chip], axis=1)
    ln_gain = jnp.concatenate([p[1] for p in per_chip], axis=0)[None, :]
    ln_bias = jnp.concatenate([p[2] for p in per_chip], axis=0)[None, :]
    w2p = jnp.pad(w2_full, ((0, LANES - GLA_GATE_RANK), (0, 0)))

    pos_chunk = jnp.arange(SGU_BLOCK) // CHUNK
    mask = pos_chunk[:, None] >= pos_chunk[None, :]
    ws_masked = jnp.where(mask[None], sgu_w_spatial[0], 0.0)
    ws_masked_t = ws_masked.transpose(0, 2, 1)
    bs_t = sgu_b_spatial[0].T

    h0 = _norm_pre(x0, norm_pre[0:1], name="pre0")
    proj0 = _matmul(h0, wt_g, mode="nt", out_dtype=F32, name="gla_in", tn=896)
    o0, a0, s_before, s_final = _gla_fwd(proj0, w2p, gla_b_gate, gla_o_gain, name="gla_scan")
    y0 = _matmul(a0, w_out_g, mode="nn", out_dtype=F32, name="gla_out")
    x1, h1 = _post_then_pre(x0, y0, norm_post[0:1], norm_pre[1:2], name="post0_pre1")
    proj1 = _matmul(h1, g_wi_s, mode="nn", out_dtype=F32, name="sgu_in", b_shards=True)
    a1 = _sgu_fwd(proj1, ln_gain, ln_bias, ws_masked, bs_t, name="sgu_gate")
    y1 = _matmul(a1, w_out_s, mode="nn", out_dtype=F32, name="sgu_out")
    loss_part, dx2, dy1, d_post1 = _loss_head(x1, y1, norm_post[1:2], target, name="loss_head")

    dw_out_s = _matmul(a1, dy1, mode="tn", out_dtype=F32, name="d_sgu_w_out")
    da1 = _matmul(dy1, w_out_s, mode="nt", out_dtype=F32, name="d_sgu_act")
    dproj1, d_ws, d_bs_t, d_lg, d_lb = _sgu_bwd(da1, proj1, ln_gain, ln_bias, ws_masked, ws_masked_t, bs_t,
                                                name="sgu_gate_bwd")
    dw_in_s = _matmul(h1, dproj1, mode="tn", out_dtype=F32, name="d_sgu_w_in", out_shards=True)
    dh1 = _matmul(dproj1, g_wi_s, mode="nt", out_dtype=F32, name="d_sgu_h", b_shards=True, tk=1536)
    dx1, dy0, d_pre1, d_post0 = _mid_bwd(dx2, dh1, x1, norm_pre[1:2], y0, norm_post[0:1], name="pre1_post0_bwd")
    dw_out_g = _matmul(a0, dy0, mode="tn", out_dtype=F32, name="d_gla_w_out")
    da0 = _matmul(dy0, w_out_g, mode="nt", out_dtype=F32, name="d_gla_act")
    dproj0, d_og, d_bg, d_w2p = _gla_bwd(da0, o0, proj0, w2p, gla_b_gate, gla_o_gain, s_before, s_final,
                                         name="gla_scan_bwd")
    dwt_in_g = _matmul(dproj0, h0, mode="tn", out_dtype=F32, name="d_gla_w_in", tm=896)
    dh0 = _matmul(dproj0, wt_g, mode="nn", out_dtype=F32, name="d_gla_h", tk=896)
    grad_x, d_pre0 = _first_bwd(dx1, dh0, x0, norm_pre[0:1], name="pre0_bwd")

    big = [dwt_in_g[:gla_cols].reshape(N_CHIPS, gla_cols // N_CHIPS, d), dw_out_g.reshape(N_CHIPS, d // N_CHIPS, d),
           dw_in_s, dw_out_s.reshape(N_CHIPS, d // N_CHIPS, d)]
    peer = _sibling_swap_halves(big, name="grads_to_sibling")
    pair = [_pair_sum_bf16(b, core_idx, p, name=f"pair_sum_{i}") for i, (b, p) in enumerate(zip(big, peer))]
    landed = _chip_scatter(pair, name="grads_to_chips")
    mine = [_chip_sum(p, q, slots, name=f"chip_sum_{i}") for i, (p, q) in enumerate(zip(pair, landed))]
    gt_wi_gla, g_wo_gla, g_wi_sgu, g_wo_sgu = _sibling_share_halves(mine, name="grads_from_sibling")

    small_shapes = [norm_pre.shape, norm_post.shape, gla_b_gate.shape, gla_o_gain.shape, sgu_w_spatial.shape,
                    sgu_b_spatial.shape, (1, GLA_GATE_RANK, dk), (1, d), (1, d), (1, LANES)]
    d_pre = jnp.concatenate([d_pre0, d_pre1], axis=0)
    d_post = jnp.concatenate([d_post0, d_post1], axis=0)
    d_wsp = jnp.where(mask[None], d_ws, 0.0)[None]
    small_part = _pack([d_pre, d_post, d_bg, d_og, d_wsp, d_bs_t.T[None], d_w2p[:GLA_GATE_RANK][None], d_lg, d_lb,
                        loss_part])
    small_all = _all_gather_dev(small_part, name="gather_small_grads")
    small_sum = _stack_sum(small_all, name="small_sum")
    (g_pre, g_post, g_bg, g_og, g_wsp, g_bsp, g_w2_full, g_lg_full, g_lb_full, loss_vec) = _unpack(small_sum, small_shapes)
    loss = loss_vec[0, 0]
    g_w2 = lax.dynamic_slice_in_dim(g_w2_full, chip * (dk // N_CHIPS), dk // N_CHIPS, axis=2)
    g_lg = lax.dynamic_slice_in_dim(g_lg_full, chip * (d // N_CHIPS), d // N_CHIPS, axis=1)
    g_lb = lax.dynamic_slice_in_dim(g_lb_full, chip * (d // N_CHIPS), d // N_CHIPS, axis=1)

    def big_update(w, g, m, v, name):
        return [u[None] for u in _adamw(w[0], g, m[0], v[0], name=name)]

    u_wi_gla = [u.T[None] for u in _adamw(wt_in_g, gt_wi_gla, mt_in_g, vt_in_g, name="adamw_gla_w_in")]
    u_wo_gla = big_update(gla_w_out, g_wo_gla, m_gla_w_out, v_gla_w_out, "adamw_gla_w_out")
    u_wi_sgu = big_update(sgu_w_in, g_wi_sgu, m_sgu_w_in, v_sgu_w_in, "adamw_sgu_w_in")
    u_wo_sgu = big_update(sgu_w_out, g_wo_sgu, m_sgu_w_out, v_sgu_w_out, "adamw_sgu_w_out")

    small_w = [norm_pre, norm_post, gla_b_gate, gla_o_gain, sgu_w_spatial, sgu_b_spatial, gla_w_gate2, sgu_ln_gain,
               sgu_ln_bias]
    small_g = [g_pre, g_post, g_bg, g_og, g_wsp, g_bsp, g_w2, g_lg, g_lb]
    small_m = [m_norm_pre, m_norm_post, m_gla_b_gate, m_gla_o_gain, m_sgu_w_spatial, m_sgu_b_spatial, m_gla_w_gate2,
               m_sgu_ln_gain, m_sgu_ln_bias]
    small_v = [v_norm_pre, v_norm_post, v_gla_b_gate, v_gla_o_gain, v_sgu_w_spatial, v_sgu_b_spatial, v_gla_w_gate2,
               v_sgu_ln_gain, v_sgu_ln_bias]
    own_shapes = [w.shape for w in small_w]
    _, s_dl, s_m, s_v = _adamw(_pack(small_w), _pack(small_g), _pack(small_m), _pack(small_v), name="adamw_small")
    dl_s, m_s, v_s = _unpack(s_dl, own_shapes), _unpack(s_m, own_shapes), _unpack(s_v, own_shapes)

    def ordered(small, kind):
        pre, post, bg, og, wsp, bsp, w2, lg, lb = small
        return [pre, post, u_wi_gla[kind], w2, bg, og, u_wo_gla[kind], u_wi_sgu[kind], lg, lb, wsp, bsp, u_wo_sgu[kind]]

    return (loss, grad_x[None], *ordered(small_g, 0), *ordered(dl_s, 1), *ordered(m_s, 2), *ordered(v_s, 3))
```

```python
import functools
import math

import jax
import jax.numpy as jnp
from jax import lax
from jax.experimental import pallas as pl
from jax.experimental.pallas import tpu as pltpu

F32 = jnp.float32
BF16 = jnp.bfloat16
MESH = pl.DeviceIdType.MESH

EPS = 1e-6
CHUNK = 64
GLA_HEADS = 4
GLA_GATE_RANK = 16
GLA_TAU = 16.0
SGU_BLOCK = 128
SGU_GROUPS = 8
N_CHIPS = 4
N_DEV = 8
LANES = 128

ADAM_LR = 0.001
ADAM_B1 = 0.9
ADAM_B2 = 0.999
ADAM_EPS = 1e-08
ADAM_WD = 0.01
ADAM_STEP = 10

VMEM_LIMIT = 56 * 1024 * 1024


def _cparams(sem=None):
    return pltpu.CompilerParams(dimension_semantics=sem, vmem_limit_bytes=VMEM_LIMIT)


def _pick(n, cap, unit=LANES):
    best = None
    for t in range(unit, min(n, cap) + 1, unit):
        if n % t == 0:
            best = t
    assert best is not None, (n, cap, unit)
    return best


def _dot(a, b, dims):
    return lax.dot_general(a, b, (dims, ((), ())), preferred_element_type=F32)


def _dot_nn(a, b):
    return _dot(a, b, ((1,), (0,)))


def _dot_nt(a, b):
    return _dot(a, b, ((1,), (1,)))


def _dot_tn(a, b):
    return _dot(a, b, ((0,), (0,)))


def _matmul(a, b, *, mode, out_dtype, name, tm=1024, tn=512, tk=2048, b_shards=False, out_shards=False, after=None):
    if mode == "tn":
        K, M = a.shape
    else:
        M, K = a.shape
    if b_shards:
        ns, br, bc = b.shape
        if mode == "nt":
            N, Kb = br, ns * bc
        else:
            Kb, N = br, ns * bc
    else:
        if mode == "nt":
            N, Kb = b.shape
        else:
            Kb, N = b.shape
    assert K == Kb, (a.shape, b.shape, mode)
    tm = _pick(M, tm)
    tk = _pick(K, tk)
    if b_shards and mode != "nt":
        tn = _pick(bc, tn)
    elif out_shards:
        tn = _pick(N // N_CHIPS, tn)
    else:
        tn = _pick(N, tn)
    if b_shards and mode == "nt":
        tk = _pick(bc, tk)
    nk = K // tk
    grid = (M // tm, N // tn, nk)

    if mode == "tn":
        a_spec = pl.BlockSpec((tk, tm), lambda i, j, k: (k, i))
    else:
        a_spec = pl.BlockSpec((tm, tk), lambda i, j, k: (i, k))
    if b_shards:
        if mode == "nt":
            per = bc // tk
            b_spec = pl.BlockSpec((None, tn, tk), lambda i, j, k: (k // per, j, k % per))
        else:
            per = bc // tn
            b_spec = pl.BlockSpec((None, tk, tn), lambda i, j, k: (j // per, k, j % per))
    elif mode == "nt":
        b_spec = pl.BlockSpec((tn, tk), lambda i, j, k: (j, k))
    else:
        b_spec = pl.BlockSpec((tk, tn), lambda i, j, k: (k, j))
    if out_shards:
        per_o = (N // N_CHIPS) // tn
        out_spec = pl.BlockSpec((None, tm, tn), lambda i, j, k: (j // per_o, i, j % per_o))
        out_shape = jax.ShapeDtypeStruct((N_CHIPS, M, N // N_CHIPS), out_dtype)
    else:
        out_spec = pl.BlockSpec((tm, tn), lambda i, j, k: (i, j))
        out_shape = jax.ShapeDtypeStruct((M, N), out_dtype)

    dims = {"nn": ((1,), (0,)), "nt": ((1,), (1,)), "tn": ((0,), (0,))}[mode]

    def body(a_ref, b_ref, *rest):
        o_ref, scratch = (rest[1], rest[2:]) if after is not None else (rest[0], rest[1:])
        part = _dot(a_ref[...].astype(BF16), b_ref[...].astype(BF16), dims)
        if nk == 1:
            o_ref[...] = part.astype(out_dtype)
        else:
            acc_ref, = scratch
            k = pl.program_id(2)

            @pl.when(k == 0)
            def _():
                acc_ref[...] = part

            @pl.when(k > 0)
            def _():
                acc_ref[...] += part

            @pl.when(k == nk - 1)
            def _():
                o_ref[...] = acc_ref[...].astype(out_dtype)

    extra_specs, extra_args = ([], []) if after is None else ([pl.BlockSpec(memory_space=pl.ANY)], [after])
    return pl.pallas_call(
        body, name=name, grid=grid, in_specs=[a_spec, b_spec] + extra_specs, out_specs=out_spec, out_shape=out_shape,
        scratch_shapes=[] if nk == 1 else [pltpu.VMEM((tm, tn), F32)],
        compiler_params=_cparams(("parallel", "parallel", "arbitrary")),
    )(a, b, *extra_args)


def _rstd(x):
    return lax.rsqrt(jnp.mean(x * x, axis=-1, keepdims=True) + EPS)


def _row_spec(tr, d):
    return pl.BlockSpec((tr, d), lambda i: (i, 0))


def _vec_spec(d):
    return pl.BlockSpec((1, d), lambda i: (0, 0))


def _acc_rows(ref, i, val, cols=slice(None)):
    @pl.when(i == 0)
    def _():
        ref[:, cols] = val

    @pl.when(i > 0)
    def _():
        ref[:, cols] += val


def _norm_pre(x, gain, *, name, tr=256):
    t, d = x.shape
    tr = _pick(t, tr, 8)

    def body(x_ref, g_ref, h_ref):
        xv = x_ref[...]
        h_ref[...] = (xv * _rstd(xv) * g_ref[...]).astype(BF16)

    return pl.pallas_call(
        body, name=name, grid=(t // tr,), in_specs=[_row_spec(tr, d), _vec_spec(d)], out_specs=_row_spec(tr, d),
        out_shape=jax.ShapeDtypeStruct((t, d), BF16), compiler_params=_cparams(("parallel",)),
    )(x, gain)


def _post_then_pre(x, y, post_gain, pre_gain, *, name, tr=256):
    t, d = x.shape
    tr = _pick(t, tr, 8)

    def body(x_ref, y_ref, pg_ref, ng_ref, xn_ref, h_ref):
        yv = y_ref[...]
        xn = x_ref[...] + yv * _rstd(yv) * pg_ref[...]
        xn_ref[...] = xn
        h_ref[...] = (xn * _rstd(xn) * ng_ref[...]).astype(BF16)

    return pl.pallas_call(
        body, name=name, grid=(t // tr,),
        in_specs=[_row_spec(tr, d), _row_spec(tr, d), _vec_spec(d), _vec_spec(d)],
        out_specs=[_row_spec(tr, d), _row_spec(tr, d)],
        out_shape=[jax.ShapeDtypeStruct((t, d), F32), jax.ShapeDtypeStruct((t, d), BF16)],
        compiler_params=_cparams(("parallel",)),
    )(x, y, post_gain, pre_gain)


def _norm_bwd(dy, n, r, gain):
    dn = dy * gain
    return r * (dn - n * jnp.mean(dn * n, axis=-1, keepdims=True))


def _loss_head(x, y, post_gain, target, *, name, tr=256):
    t, d = x.shape
    tr = _pick(t, tr, 8)

    def body(x_ref, y_ref, pg_ref, t_ref, loss_ref, dx_ref, dy_ref, dpg_ref):
        i = pl.program_id(0)
        yv = y_ref[...]
        r = _rstd(yv)
        n = yv * r
        err = x_ref[...] + n * pg_ref[...] - t_ref[...]
        dx = err * (1.0 / d)
        dx_ref[...] = dx
        part = 0.5 * jnp.sum(jnp.mean(err * err, axis=-1, keepdims=True), axis=0, keepdims=True)
        _acc_rows(loss_ref, i, jnp.broadcast_to(part, (1, LANES)))
        _acc_rows(dpg_ref, i, jnp.sum(dx * n, axis=0, keepdims=True))
        dy_ref[...] = _norm_bwd(dx, n, r, pg_ref[...]).astype(BF16)

    return pl.pallas_call(
        body, name=name, grid=(t // tr,),
        in_specs=[_row_spec(tr, d), _row_spec(tr, d), _vec_spec(d), _row_spec(tr, d)],
        out_specs=[_vec_spec(LANES), _row_spec(tr, d), _row_spec(tr, d), _vec_spec(d)],
        out_shape=[jax.ShapeDtypeStruct((1, LANES), F32), jax.ShapeDtypeStruct((t, d), F32),
                   jax.ShapeDtypeStruct((t, d), BF16), jax.ShapeDtypeStruct((1, d), F32)],
        compiler_params=_cparams(("arbitrary",)),
    )(x, y, post_gain, target)


def _mid_bwd(dx_out, dh, x, pre_gain, y_prev, post_gain_prev, *, name, tr=256):
    t, d = x.shape
    tr = _pick(t, tr, 8)

    def body(dxo_ref, dh_ref, x_ref, ng_ref, y_ref, pg_ref, dx_ref, dy_ref, dng_ref, dpg_ref):
        i = pl.program_id(0)
        xv = x_ref[...]
        r = _rstd(xv)
        xh = xv * r
        dhv = dh_ref[...]
        _acc_rows(dng_ref, i, jnp.sum(dhv * xh, axis=0, keepdims=True))
        dx = dxo_ref[...] + _norm_bwd(dhv, xh, r, ng_ref[...])
        dx_ref[...] = dx
        yv = y_ref[...]
        ry = _rstd(yv)
        n = yv * ry
        _acc_rows(dpg_ref, i, jnp.sum(dx * n, axis=0, keepdims=True))
        dy_ref[...] = _norm_bwd(dx, n, ry, pg_ref[...]).astype(BF16)

    return pl.pallas_call(
        body, name=name, grid=(t // tr,),
        in_specs=[_row_spec(tr, d), _row_spec(tr, d), _row_spec(tr, d), _vec_spec(d), _row_spec(tr, d), _vec_spec(d)],
        out_specs=[_row_spec(tr, d), _row_spec(tr, d), _vec_spec(d), _vec_spec(d)],
        out_shape=[jax.ShapeDtypeStruct((t, d), F32), jax.ShapeDtypeStruct((t, d), BF16),
                   jax.ShapeDtypeStruct((1, d), F32), jax.ShapeDtypeStruct((1, d), F32)],
        compiler_params=_cparams(("arbitrary",)),
    )(dx_out, dh, x, pre_gain, y_prev, post_gain_prev)


def _first_bwd(dx_out, dh, x, pre_gain, *, name, tr=256):
    t, d = x.shape
    tr = _pick(t, tr, 8)

    def body(dxo_ref, dh_ref, x_ref, ng_ref, dx_ref, dng_ref):
        i = pl.program_id(0)
        xv = x_ref[...]
        r = _rstd(xv)
        xh = xv * r
        dhv = dh_ref[...]
        _acc_rows(dng_ref, i, jnp.sum(dhv * xh, axis=0, keepdims=True))
        dx_ref[...] = dxo_ref[...] + _norm_bwd(dhv, xh, r, ng_ref[...])

    return pl.pallas_call(
        body, name=name, grid=(t // tr,),
        in_specs=[_row_spec(tr, d), _row_spec(tr, d), _row_spec(tr, d), _vec_spec(d)],
        out_specs=[_row_spec(tr, d), _vec_spec(d)],
        out_shape=[jax.ShapeDtypeStruct((t, d), F32), jax.ShapeDtypeStruct((1, d), F32)],
        compiler_params=_cparams(("arbitrary",)),
    )(dx_out, dh, x, pre_gain)


def _sigmoid(x):
    return 1.0 / (1.0 + jnp.exp(-x))


def _log_sigmoid(x):
    return jnp.minimum(x, 0.0) - jnp.log(1.0 + jnp.exp(-jnp.abs(x)))


_GELU_C = math.sqrt(2.0 / math.pi)


def _gelu_parts(x):
    x2 = x * x
    th = jnp.tanh(_GELU_C * (x + 0.044715 * x * x2))
    val = 0.5 * x * (1.0 + th)
    grad = 0.5 * (1.0 + th) + 0.5 * x * (1.0 - th * th) * (_GELU_C * (1.0 + 3.0 * 0.044715 * x2))
    return val, grad


def _split3(x):
    hi = x.astype(BF16)
    r1 = x - hi.astype(F32)
    mid = r1.astype(BF16)
    lo = (r1 - mid.astype(F32)).astype(BF16)
    return hi, mid, lo


def _tri_matmul(tri_bf16, x):
    hi, mid, lo = _split3(x)
    return _dot_nn(tri_bf16, hi) + _dot_nn(tri_bf16, mid) + _dot_nn(tri_bf16, lo)


def _gla_dims(d):
    dk, dv = d // 2, d
    return dk, dv, dk // GLA_HEADS, dv // GLA_HEADS


def _gla_gates(glr, k, w2_ref, b_ref):
    z = _dot_nn(glr.astype(BF16), w2_ref[...].astype(BF16)) + b_ref[...]
    la = _log_sigmoid(z) * (1.0 / GLA_TAU)
    row = lax.broadcasted_iota(jnp.int32, (CHUNK, CHUNK), 0)
    col = lax.broadcasted_iota(jnp.int32, (CHUNK, CHUNK), 1)
    incl = (row >= col).astype(BF16)
    bcum = _tri_matmul(incl, la)
    b_end = bcum[CHUNK - 1:CHUNK, :]
    e_rest = jnp.exp(b_end - bcum)
    return z, e_rest, k * e_rest, jnp.exp(b_end)


def _gla_fwd(proj, w2p, b_gate, o_gain, *, name):
    t, wcols = proj.shape
    d = o_gain.shape[1]
    dk, dv, dkh, dvh = _gla_dims(d)
    nc = t // CHUNK
    c_k, c_v, c_g, c_r = dk, 2 * dk, 2 * dk + dv, 2 * dk + 2 * dv
    scale = dkh ** -0.5

    def body(p_ref, w2_ref, b_ref, og_ref, o_ref, a_ref, sb_ref, sfin_ref, s_ref):
        i = pl.program_id(0)

        @pl.when(i == 0)
        def _():
            s_ref[...] = jnp.zeros_like(s_ref)

        q = p_ref[:, 0:dk] * scale
        k = p_ref[:, c_k:c_k + dk]
        glr = p_ref[:, c_r:c_r + LANES]
        _, _, kdec, decay = _gla_gates(glr, k, w2_ref, b_ref)
        for h in range(GLA_HEADS):
            ks = slice(h * dkh, (h + 1) * dkh)
            vs = slice(h * dvh, (h + 1) * dvh)
            v_h = p_ref[:, c_v + h * dvh:c_v + (h + 1) * dvh]
            g_h = p_ref[:, c_g + h * dvh:c_g + (h + 1) * dvh]
            s_old = s_ref[h]
            sb_ref[0, h] = s_old
            s_new = s_old * decay[:, ks] + _dot_tn(v_h.astype(BF16), kdec[:, ks].astype(BF16))
            s_ref[h] = s_new
            o_h = _dot_nt(q[:, ks].astype(BF16), s_new.astype(BF16))
            o_ref[:, vs] = o_h
            on = o_h * _rstd(o_h)
            a_ref[:, vs] = (on * og_ref[:, vs] * (g_h * _sigmoid(g_h))).astype(BF16)

        @pl.when(i == nc - 1)
        def _():
            sfin_ref[...] = s_ref[...]

    full = lambda *shape: pl.BlockSpec(shape, lambda i: (0,) * len(shape))
    return pl.pallas_call(
        body, name=name, grid=(nc,),
        in_specs=[pl.BlockSpec((CHUNK, wcols), lambda i: (i, 0)), full(LANES, dk), full(1, dk), full(1, dv)],
        out_specs=[pl.BlockSpec((CHUNK, dv), lambda i: (i, 0)), pl.BlockSpec((CHUNK, dv), lambda i: (i, 0)),
                   pl.BlockSpec((1, GLA_HEADS, dvh, dkh), lambda i: (i, 0, 0, 0)), full(GLA_HEADS, dvh, dkh)],
        out_shape=[jax.ShapeDtypeStruct((t, dv), F32), jax.ShapeDtypeStruct((t, dv), BF16),
                   jax.ShapeDtypeStruct((nc, GLA_HEADS, dvh, dkh), F32),
                   jax.ShapeDtypeStruct((GLA_HEADS, dvh, dkh), F32)],
        scratch_shapes=[pltpu.VMEM((GLA_HEADS, dvh, dkh), F32)],
        compiler_params=_cparams(("arbitrary",)),
    )(proj, w2p, b_gate, o_gain)


def _gla_bwd(da, o, proj, w2p, b_gate, o_gain, s_before, s_final, *, name):
    t, wcols = proj.shape
    d = o_gain.shape[1]
    dk, dv, dkh, dvh = _gla_dims(d)
    nc = t // CHUNK
    c_k, c_v, c_g, c_r = dk, 2 * dk, 2 * dk + dv, 2 * dk + 2 * dv
    scale = dkh ** -0.5

    def body(da_ref, o_ref, p_ref, w2_ref, b_ref, og_ref, sb_ref, sfin_ref,
             dp_ref, dog_ref, db_ref, dw2_ref, s_ref, gc_ref, dkd_ref):
        i = pl.program_id(0)

        @pl.when(i == 0)
        def _():
            s_ref[...] = sfin_ref[...]
            gc_ref[...] = jnp.zeros_like(gc_ref)

        q = p_ref[:, 0:dk] * scale
        k = p_ref[:, c_k:c_k + dk]
        glr = p_ref[:, c_r:c_r + LANES]
        z, e_rest, kdec, decay = _gla_gates(glr, k, w2_ref, b_ref)
        ddecay = []
        for h in range(GLA_HEADS):
            ks = slice(h * dkh, (h + 1) * dkh)
            vs = slice(h * dvh, (h + 1) * dvh)
            v_h = p_ref[:, c_v + h * dvh:c_v + (h + 1) * dvh]
            g_h = p_ref[:, c_g + h * dvh:c_g + (h + 1) * dvh]
            da_h = da_ref[:, vs]
            o_h = o_ref[:, vs]
            og_h = og_ref[:, vs]
            r = _rstd(o_h)
            on = o_h * r
            sg = _sigmoid(g_h)
            silu = g_h * sg
            _acc_rows(dog_ref, i, jnp.sum(da_h * silu * on, axis=0, keepdims=True), vs)
            dp_ref[:, c_g + h * dvh:c_g + (h + 1) * dvh] = (
                da_h * (on * og_h) * (sg * (1.0 + g_h * (1.0 - sg)))).astype(BF16)
            don = da_h * silu * og_h
            do_h = (r * (don - on * jnp.mean(don * on, axis=-1, keepdims=True))).astype(BF16)
            s_cur = s_ref[h]
            dp_ref[:, ks] = (_dot_nn(do_h, s_cur.astype(BF16)) * scale).astype(BF16)
            g_tot = gc_ref[h] + _dot_tn(do_h, q[:, ks].astype(BF16))
            g_bf = g_tot.astype(BF16)
            dkd_ref[:, ks] = _dot_nn(v_h.astype(BF16), g_bf)
            dp_ref[:, c_v + h * dvh:c_v + (h + 1) * dvh] = _dot_nt(kdec[:, ks].astype(BF16), g_bf).astype(BF16)
            s_prev = sb_ref[0, h]
            ddecay.append(jnp.sum(g_tot * s_prev, axis=0, keepdims=True))
            gc_ref[h] = g_tot * decay[:, ks]
            s_ref[h] = s_prev
        dkdec = dkd_ref[...]
        dp_ref[:, c_k:c_k + dk] = (dkdec * e_rest).astype(BF16)
        d_e = dkdec * kdec
        row = lax.broadcasted_iota(jnp.int32, (CHUNK, CHUNK), 0)
        col = lax.broadcasted_iota(jnp.int32, (CHUNK, CHUNK), 1)
        excl = (row > col).astype(BF16)
        dla = jnp.concatenate(ddecay, axis=1) * decay + _tri_matmul(excl, d_e)
        dz = dla * (1.0 / GLA_TAU) * (1.0 - _sigmoid(z))
        _acc_rows(db_ref, i, jnp.sum(dz, axis=0, keepdims=True))
        dz_bf = dz.astype(BF16)
        dw2 = _dot_tn(glr.astype(BF16), dz_bf)

        @pl.when(i == 0)
        def _():
            dw2_ref[...] = dw2

        @pl.when(i > 0)
        def _():
            dw2_ref[...] += dw2

        dp_ref[:, c_r:c_r + LANES] = _dot_nt(dz_bf, w2_ref[...].astype(BF16)).astype(BF16)

    rev = lambda i: (nc - 1 - i, 0)
    full = lambda *shape: pl.BlockSpec(shape, lambda i: (0,) * len(shape))
    return pl.pallas_call(
        body, name=name, grid=(nc,),
        in_specs=[pl.BlockSpec((CHUNK, dv), rev), pl.BlockSpec((CHUNK, dv), rev), pl.BlockSpec((CHUNK, wcols), rev),
                  full(LANES, dk), full(1, dk), full(1, dv),
                  pl.BlockSpec((1, GLA_HEADS, dvh, dkh), lambda i: (nc - 1 - i, 0, 0, 0)), full(GLA_HEADS, dvh, dkh)],
        out_specs=[pl.BlockSpec((CHUNK, wcols), rev), full(1, dv), full(1, dk), full(LANES, dk)],
        out_shape=[jax.ShapeDtypeStruct((t, wcols), BF16), jax.ShapeDtypeStruct((1, dv), F32),
                   jax.ShapeDtypeStruct((1, dk), F32), jax.ShapeDtypeStruct((LANES, dk), F32)],
        scratch_shapes=[pltpu.VMEM((GLA_HEADS, dvh, dkh), F32), pltpu.VMEM((GLA_HEADS, dvh, dkh), F32),
                        pltpu.VMEM((CHUNK, dk), F32)],
        compiler_params=_cparams(("arbitrary",)),
    )(da, o, proj, w2p, b_gate, o_gain, s_before, s_final)


def _sgu_mid(p_ref, lg_ref, lb_ref, ws_ref, bst_ref, w):
    gd = w // SGU_GROUPS
    u_act, du_fac = _gelu_parts(p_ref[:, 0:w])
    vf, dv_fac = _gelu_parts(p_ref[:, w:2 * w])
    mu = jnp.mean(vf, axis=-1, keepdims=True)
    cen = vf - mu
    rstd = lax.rsqrt(jnp.mean(cen * cen, axis=-1, keepdims=True) + EPS)
    xh = cen * rstd
    vn = (xh * lg_ref[...] + lb_ref[...]).astype(BF16)
    vs = [_dot_nn(ws_ref[g].astype(BF16), vn[:, g * gd:(g + 1) * gd]) + bst_ref[:, g:g + 1]
          for g in range(SGU_GROUPS)]
    return u_act, du_fac, dv_fac, rstd, xh, vn, vs


def _sgu_fwd(proj, ln_gain, ln_bias, ws_masked, bs_t, *, name):
    t, w3 = proj.shape
    w = w3 // 3
    gd = w // SGU_GROUPS
    nb = t // SGU_BLOCK

    def body(p_ref, lg_ref, lb_ref, ws_ref, bst_ref, a_ref):
        u_act, _, _, _, _, _, vs = _sgu_mid(p_ref, lg_ref, lb_ref, ws_ref, bst_ref, w)
        for g in range(SGU_GROUPS):
            cs = slice(g * gd, (g + 1) * gd)
            gate = p_ref[:, 2 * w + g * gd:2 * w + (g + 1) * gd]
            a_ref[:, cs] = (u_act[:, cs] * vs[g] * (gate * _sigmoid(gate))).astype(BF16)

    full = lambda *shape: pl.BlockSpec(shape, lambda i: (0,) * len(shape))
    return pl.pallas_call(
        body, name=name, grid=(nb,),
        in_specs=[pl.BlockSpec((SGU_BLOCK, w3), lambda i: (i, 0)), full(1, w), full(1, w),
                  full(SGU_GROUPS, SGU_BLOCK, SGU_BLOCK), full(SGU_BLOCK, SGU_GROUPS)],
        out_specs=pl.BlockSpec((SGU_BLOCK, w), lambda i: (i, 0)),
        out_shape=jax.ShapeDtypeStruct((t, w), BF16),
        compiler_params=_cparams(("parallel",)),
    )(proj, ln_gain, ln_bias, ws_masked, bs_t)


def _sgu_bwd(da, proj, ln_gain, ln_bias, ws_masked, ws_masked_t, bs_t, *, name):
    t, w3 = proj.shape
    w = w3 // 3
    gd = w // SGU_GROUPS
    nb = t // SGU_BLOCK

    def body(da_ref, p_ref, lg_ref, lb_ref, ws_ref, wst_ref, bst_ref, dp_ref, dws_ref, dbst_ref, dlg_ref, dlb_ref,
             dvn_ref):
        i = pl.program_id(0)
        u_act, du_fac, dv_fac, rstd, xh, vn, vs = _sgu_mid(p_ref, lg_ref, lb_ref, ws_ref, bst_ref, w)
        for g in range(SGU_GROUPS):
            cs = slice(g * gd, (g + 1) * gd)
            gate = p_ref[:, 2 * w + g * gd:2 * w + (g + 1) * gd]
            sg = _sigmoid(gate)
            silu = gate * sg
            da_g = da_ref[:, cs]
            ua_g = u_act[:, cs]
            dp_ref[:, cs] = (da_g * vs[g] * silu * du_fac[:, cs]).astype(BF16)
            dp_ref[:, 2 * w + g * gd:2 * w + (g + 1) * gd] = (
                da_g * ua_g * vs[g] * (sg * (1.0 + gate * (1.0 - sg)))).astype(BF16)
            dvs = da_g * ua_g * silu
            dvs_bf = dvs.astype(BF16)
            dvn_ref[:, cs] = _dot_nn(wst_ref[g].astype(BF16), dvs_bf)
            dws = _dot_nt(dvs_bf, vn[:, cs])
            dbs = jnp.sum(dvs, axis=1, keepdims=True)

            @pl.when(i == 0)
            def _():
                dws_ref[g] = dws
                dbst_ref[:, g:g + 1] = dbs

            @pl.when(i > 0)
            def _():
                dws_ref[g] += dws
                dbst_ref[:, g:g + 1] += dbs

        dvn = dvn_ref[...]
        _acc_rows(dlg_ref, i, jnp.sum(dvn * xh, axis=0, keepdims=True))
        _acc_rows(dlb_ref, i, jnp.sum(dvn, axis=0, keepdims=True))
        dxh = dvn * lg_ref[...]
        dvf = rstd * (dxh - jnp.mean(dxh, axis=-1, keepdims=True)
                      - xh * jnp.mean(dxh * xh, axis=-1, keepdims=True))
        dp_ref[:, w:2 * w] = (dvf * dv_fac).astype(BF16)

    full = lambda *shape: pl.BlockSpec(shape, lambda i: (0,) * len(shape))
    return pl.pallas_call(
        body, name=name, grid=(nb,),
        in_specs=[pl.BlockSpec((SGU_BLOCK, w), lambda i: (i, 0)), pl.BlockSpec((SGU_BLOCK, w3), lambda i: (i, 0)),
                  full(1, w), full(1, w), full(SGU_GROUPS, SGU_BLOCK, SGU_BLOCK),
                  full(SGU_GROUPS, SGU_BLOCK, SGU_BLOCK), full(SGU_BLOCK, SGU_GROUPS)],
        out_specs=[pl.BlockSpec((SGU_BLOCK, w3), lambda i: (i, 0)), full(SGU_GROUPS, SGU_BLOCK, SGU_BLOCK),
                   full(SGU_BLOCK, SGU_GROUPS), full(1, w), full(1, w)],
        out_shape=[jax.ShapeDtypeStruct((t, w3), BF16), jax.ShapeDtypeStruct((SGU_GROUPS, SGU_BLOCK, SGU_BLOCK), F32),
                   jax.ShapeDtypeStruct((SGU_BLOCK, SGU_GROUPS), F32), jax.ShapeDtypeStruct((1, w), F32),
                   jax.ShapeDtypeStruct((1, w), F32)],
        scratch_shapes=[pltpu.VMEM((SGU_BLOCK, w), F32)],
        compiler_params=_cparams(("arbitrary",)),
    )(da, proj, ln_gain, ln_bias, ws_masked, ws_masked_t, bs_t)


def _tile2d(rows, cols, block_bytes, row_unit):
    if rows % row_unit == 0:
        return _pick(rows, max(row_unit, block_bytes // (4 * cols)), row_unit), cols
    return rows, _pick(cols, max(LANES, block_bytes // (4 * rows)))


def _adamw(w, g, m, v, *, name, block_bytes=1 << 20):
    rows, cols = w.shape
    tr, tc = _tile2d(rows, cols, block_bytes, 8)

    def body(w_ref, g_ref, m_ref, v_ref, go_ref, d_ref, mo_ref, vo_ref):
        gv = g_ref[...]
        go_ref[...] = gv
        mn = ADAM_B1 * m_ref[...] + (1.0 - ADAM_B1) * gv
        vn = ADAM_B2 * v_ref[...] + (1.0 - ADAM_B2) * (gv * gv)
        m_hat = mn / (1.0 - ADAM_B1 ** ADAM_STEP)
        v_hat = vn / (1.0 - ADAM_B2 ** ADAM_STEP)
        d_ref[...] = -ADAM_LR * (m_hat / (jnp.sqrt(v_hat) + ADAM_EPS) + ADAM_WD * w_ref[...])
        mo_ref[...] = mn
        vo_ref[...] = vn

    spec = pl.BlockSpec((tr, tc), lambda i, j: (i, j))
    return pl.pallas_call(
        body, name=name, grid=(rows // tr, cols // tc), in_specs=[spec] * 4, out_specs=[spec] * 4,
        out_shape=[jax.ShapeDtypeStruct((rows, cols), F32)] * 4,
        compiler_params=_cparams(("parallel", "parallel")),
    )(w, g, m, v)


def _pair_sum_bf16(own, core_idx, peer, *, name, block_bytes=1 << 20):
    s, r, c = own.shape
    hc = c // 2
    tr, tc = _tile2d(r, hc, block_bytes, 16)
    ncb = hc // tc

    def body(h_ref, a_ref, b_ref, o_ref):
        o_ref[...] = (a_ref[...] + b_ref[...]).astype(BF16)

    grid_spec = pltpu.PrefetchScalarGridSpec(
        num_scalar_prefetch=1, grid=(s, r // tr, ncb),
        in_specs=[pl.BlockSpec((None, tr, tc), lambda j, i, k, h: (j, i, h[0] * ncb + k)),
                  pl.BlockSpec((None, tr, tc), lambda j, i, k, h: (j, i, k))],
        out_specs=pl.BlockSpec((None, tr, tc), lambda j, i, k, h: (j, i, k)))
    return pl.pallas_call(
        body, name=name, grid_spec=grid_spec, out_shape=jax.ShapeDtypeStruct((s, r, hc), BF16),
        compiler_params=_cparams(("parallel", "parallel", "parallel")),
    )(core_idx, own, peer)


def _chip_sum(pair, landed, slots, *, name, block_bytes=1 << 20):
    _, r, hc = pair.shape
    tr, tc = _tile2d(r, hc, block_bytes, 16)
    ncb = hc // tc

    def body(s_ref, own_ref, l0_ref, l1_ref, l2_ref, o_ref):
        o_ref[...] = ((own_ref[...].astype(F32) + l0_ref[...].astype(F32)) + l1_ref[...].astype(F32)
                      ) + l2_ref[...].astype(F32)

    def slab(which):
        return pl.BlockSpec((None, tr, tc), lambda i, k, s: (s[which], i, k))

    grid_spec = pltpu.PrefetchScalarGridSpec(
        num_scalar_prefetch=1, grid=(r // tr, ncb),
        in_specs=[slab(0), slab(1), slab(2), slab(3)],
        out_specs=pl.BlockSpec((tr, tc), lambda i, k, s: (i, s[4] * ncb + k)))
    return pl.pallas_call(
        body, name=name, grid_spec=grid_spec, out_shape=jax.ShapeDtypeStruct((r, 2 * hc), F32),
        compiler_params=_cparams(("parallel", "parallel")),
    )(slots, pair, landed, landed, landed)


def _stack_sum(x, *, name, out_dtype=F32, block_bytes=1 << 20):
    s, r, c = x.shape
    tr = _pick(r, max(8, block_bytes // (4 * c)), 16) if r % 16 == 0 else r

    def body(x_ref, o_ref):
        acc = x_ref[0].astype(F32)
        for j in range(1, s):
            acc = acc + x_ref[j].astype(F32)
        o_ref[...] = acc.astype(out_dtype)

    return pl.pallas_call(
        body, name=name, grid=(r // tr,),
        in_specs=[pl.BlockSpec((s, tr, c), lambda i: (0, i, 0))], out_specs=pl.BlockSpec((tr, c), lambda i: (i, 0)),
        out_shape=jax.ShapeDtypeStruct((r, c), out_dtype), compiler_params=_cparams(("parallel",)),
    )(x)


HBM = pl.BlockSpec(memory_space=pltpu.HBM)


def _place():
    x, y, c = lax.axis_index("x"), lax.axis_index("y"), lax.axis_index("c")
    other_chips = [(1 - x, y), (x, 1 - y), (1 - x, 1 - y)]
    return x, y, c, other_chips


def _half_cols(cols, which):
    hc = cols // 2
    return pl.ds(pl.multiple_of(which * hc, LANES), hc)


SEM = pl.BlockSpec(memory_space=pltpu.SEMAPHORE)
ANY = pl.BlockSpec(memory_space=pl.ANY)
SIDE_EFFECT = pltpu.SideEffectType.DATAFLOW_SIDE_EFFECTING
TOKEN_SHAPE = (8, LANES)


def _hbm(shape, dtype):
    return pltpu.HBM(shape, dtype)


def _in_hbm(a):
    return pltpu.with_memory_space_constraint(a, pltpu.HBM)


def _gather_copy(src_ref, land_ref, ssem, rsem, k, chip_of_block, to, c):
    cols = src_ref.shape[1]
    return pltpu.make_async_remote_copy(
        src_ref=src_ref.at[:, _half_cols(cols, c)], dst_ref=land_ref.at[chip_of_block, :, _half_cols(cols, c)],
        send_sem=ssem.at[k], recv_sem=rsem.at[k], device_id=to, device_id_type=MESH)


def _gather_start(shards, *, name):
    n = len(shards)

    def body(*refs):
        srcs, lands = refs[:n], refs[n:2 * n]
        outs = refs[2 * n:]
        token = outs[-1]
        x, y, c, chips = _place()
        me = 2 * x + y
        for a in range(n):
            ssem, rsem = outs[4 * a], outs[4 * a + 1]
            for k, (cx, cy) in enumerate(chips):
                _gather_copy(srcs[a], lands[a], ssem, rsem, k, me, (cx, cy, c), c).start()
        token[...] = jnp.zeros_like(token)

    out_shape, out_specs, aliases = [], [], {}
    for a, s in enumerate(shards):
        out_shape += [pltpu.SemaphoreType.DMA((3,)), pltpu.SemaphoreType.DMA((3,)), _hbm(s.shape, s.dtype),
                      _hbm((N_CHIPS,) + s.shape, s.dtype)]
        out_specs += [SEM, SEM, HBM, HBM]
        aliases[a] = 4 * a + 2
        aliases[n + a] = 4 * a + 3
    out_shape.append(jax.ShapeDtypeStruct(TOKEN_SHAPE, F32))
    out_specs.append(pl.BlockSpec(memory_space=pltpu.VMEM))
    lands = [_in_hbm(lax.empty((N_CHIPS,) + s.shape, s.dtype)) for s in shards]
    res = pl.pallas_call(
        body, name=name, in_specs=[HBM] * (2 * n), out_specs=out_specs, out_shape=out_shape,
        input_output_aliases=aliases, compiler_params=pltpu.CompilerParams(has_side_effects=SIDE_EFFECT),
    )(*[_in_hbm(s) for s in shards], *lands)
    return [tuple(res[4 * a:4 * a + 4]) for a in range(n)], res[-1]


def _gather_wait(parts, after, *, name):
    ssem, rsem, src, land = parts

    def body(src_ref, land_ref, ssem_ref, rsem_ref, after_ref, src_out, land_out):
        x, y, c, chips = _place()
        for k, (cx, cy) in enumerate(chips):
            cp = _gather_copy(src_ref, land_ref, ssem_ref, rsem_ref, k, 2 * cx + cy, (x, y, c), c)
            cp.wait_send()
            cp.wait_recv()

    return pl.pallas_call(
        body, name=name, in_specs=[HBM, HBM, SEM, SEM, ANY], out_specs=[HBM, HBM],
        out_shape=[_hbm(src.shape, src.dtype), _hbm(land.shape, land.dtype)], input_output_aliases={0: 0, 1: 1},
        compiler_params=pltpu.CompilerParams(has_side_effects=SIDE_EFFECT),
    )(src, land, ssem, rsem, after)[1]


def _sibling_forward(land, *, name):
    def body(_, buf, send_sems, recv_sems):
        x, y, c, chips = _place()
        cols = buf.shape[2]
        copies = []
        for k, (cx, cy) in enumerate(chips):
            mine = buf.at[2 * cx + cy, :, _half_cols(cols, c)]
            cp = pltpu.make_async_remote_copy(
                src_ref=mine, dst_ref=mine, send_sem=send_sems.at[k], recv_sem=recv_sems.at[k],
                device_id=(x, y, 1 - c), device_id_type=MESH)
            cp.start()
            copies.append(cp)
        for k, (cx, cy) in enumerate(chips):
            theirs = buf.at[2 * cx + cy, :, _half_cols(cols, 1 - c)]
            pltpu.make_async_remote_copy(
                src_ref=theirs, dst_ref=theirs, send_sem=send_sems.at[k], recv_sem=recv_sems.at[k],
                device_id=(x, y, c), device_id_type=MESH).wait_recv()
        for cp in copies:
            cp.wait_send()

    return pl.pallas_call(
        body, name=name, in_specs=[HBM], out_specs=HBM, out_shape=jax.ShapeDtypeStruct(land.shape, land.dtype),
        input_output_aliases={0: 0},
        scratch_shapes=[pltpu.SemaphoreType.DMA((3,)), pltpu.SemaphoreType.DMA((3,))],
    )(land)


def _scatter_copy(src_ref, land_ref, ssem, rsem, k, src_slab, dst_slab, to):
    return pltpu.make_async_remote_copy(
        src_ref=src_ref.at[src_slab], dst_ref=land_ref.at[dst_slab], send_sem=ssem.at[k], recv_sem=rsem.at[k],
        device_id=to, device_id_type=MESH)


def _scatter_start(part, *, name):
    def body(src_ref, land_ref, ssem, rsem, src_out, land_out, token):
        x, y, c, chips = _place()
        me = 2 * x + y
        for k, (cx, cy) in enumerate(chips):
            _scatter_copy(src_ref, land_ref, ssem, rsem, k, 2 * cx + cy, me, (cx, cy, c)).start()
        token[...] = jnp.zeros_like(token)

    res = pl.pallas_call(
        body, name=name, in_specs=[HBM, HBM], out_specs=[SEM, SEM, HBM, HBM, pl.BlockSpec(memory_space=pltpu.VMEM)],
        out_shape=[pltpu.SemaphoreType.DMA((3,)), pltpu.SemaphoreType.DMA((3,)), _hbm(part.shape, part.dtype),
                   _hbm(part.shape, part.dtype), jax.ShapeDtypeStruct(TOKEN_SHAPE, F32)],
        input_output_aliases={0: 2, 1: 3}, compiler_params=pltpu.CompilerParams(has_side_effects=SIDE_EFFECT),
    )(_in_hbm(part), _in_hbm(lax.empty(part.shape, part.dtype)))
    return tuple(res[:4]), res[4]


def _scatter_wait(parts, after, *, name):
    ssem, rsem, src, land = parts

    def body(src_ref, land_ref, ssem_ref, rsem_ref, after_ref, src_out, land_out):
        x, y, c, chips = _place()
        for k, (cx, cy) in enumerate(chips):
            idx = 2 * cx + cy
            cp = _scatter_copy(src_ref, land_ref, ssem_ref, rsem_ref, k, idx, idx, (x, y, c))
            cp.wait_send()
            cp.wait_recv()

    return pl.pallas_call(
        body, name=name, in_specs=[HBM, HBM, SEM, SEM, ANY], out_specs=[HBM, HBM],
        out_shape=[_hbm(src.shape, src.dtype), _hbm(land.shape, land.dtype)], input_output_aliases={0: 0, 1: 1},
        compiler_params=pltpu.CompilerParams(has_side_effects=SIDE_EFFECT),
    )(src, land, ssem, rsem, after)


def _all_gather_dev(x, *, name):
    r, ccols = x.shape

    def body(x_ref, out_ref, send_sems, recv_sems):
        px, py, c, chips = _place()
        me, sibling = (px, py, c), (px, py, 1 - c)

        def slot(qx, qy, qc):
            return out_ref.at[4 * qx + 2 * qy + qc]

        def copy(k, block, to, src=None):
            return pltpu.make_async_remote_copy(
                src_ref=slot(*block) if src is None else src, dst_ref=slot(*block),
                send_sem=send_sems.at[k], recv_sem=recv_sems.at[k], device_id=to, device_id_type=MESH)

        out_ref[4 * px + 2 * py + c] = x_ref[...]
        first = [copy(0, me, sibling, src=x_ref)]
        first += [copy(1 + j, me, (*chip, c), src=x_ref) for j, chip in enumerate(chips)]
        for cp in first:
            cp.start()
        passed = [copy(4 + j, (*chip, c), sibling) for j, chip in enumerate(chips)]
        for j, chip in enumerate(chips):
            copy(1 + j, (*chip, c), me).wait_recv()
            passed[j].start()
        copy(0, sibling, me).wait_recv()
        for j, chip in enumerate(chips):
            copy(4 + j, (*chip, 1 - c), me).wait_recv()
        for cp in first + passed:
            cp.wait_send()

    vmem = pl.BlockSpec(memory_space=pltpu.VMEM)
    return pl.pallas_call(
        body, name=name, in_specs=[vmem], out_specs=vmem, out_shape=jax.ShapeDtypeStruct((N_DEV, r, ccols), x.dtype),
        scratch_shapes=[pltpu.SemaphoreType.DMA((7,)), pltpu.SemaphoreType.DMA((7,))],
        compiler_params=_cparams(),
    )(x)


def _sibling_swap_halves(grads, *, name):
    n = len(grads)

    def body(*refs):
        ins, outs = refs[:n], refs[n:2 * n]
        send_sems, recv_sems = refs[2 * n:]
        x, y, c, _ = _place()
        copies = []
        for a in range(n):
            cp = pltpu.make_async_remote_copy(
                src_ref=ins[a].at[:, :, _half_cols(ins[a].shape[2], 1 - c)], dst_ref=outs[a],
                send_sem=send_sems.at[a], recv_sem=recv_sems.at[a], device_id=(x, y, 1 - c), device_id_type=MESH)
            cp.start()
            copies.append(cp)
        for cp in copies:
            cp.wait()

    return pl.pallas_call(
        body, name=name, in_specs=[HBM] * n, out_specs=[HBM] * n,
        out_shape=[jax.ShapeDtypeStruct((g.shape[0], g.shape[1], g.shape[2] // 2), g.dtype) for g in grads],
        scratch_shapes=[pltpu.SemaphoreType.DMA((n,)), pltpu.SemaphoreType.DMA((n,))],
    )(*grads)


def _sibling_share_halves(arrays, *, name):
    n = len(arrays)

    def body(*refs):
        bufs = refs[n:2 * n]
        send_sems, recv_sems = refs[2 * n:]
        x, y, c, _ = _place()
        copies = []
        for a in range(n):
            mine = bufs[a].at[:, _half_cols(bufs[a].shape[1], c)]
            cp = pltpu.make_async_remote_copy(
                src_ref=mine, dst_ref=mine, send_sem=send_sems.at[a], recv_sem=recv_sems.at[a],
                device_id=(x, y, 1 - c), device_id_type=MESH)
            cp.start()
            copies.append(cp)
        for a in range(n):
            theirs = bufs[a].at[:, _half_cols(bufs[a].shape[1], 1 - c)]
            pltpu.make_async_remote_copy(
                src_ref=theirs, dst_ref=theirs, send_sem=send_sems.at[a], recv_sem=recv_sems.at[a],
                device_id=(x, y, c), device_id_type=MESH).wait_recv()
        for cp in copies:
            cp.wait_send()

    return pl.pallas_call(
        body, name=name, in_specs=[HBM] * n, out_specs=[HBM] * n,
        out_shape=[jax.ShapeDtypeStruct(h.shape, h.dtype) for h in arrays],
        input_output_aliases={a: a for a in range(n)},
        scratch_shapes=[pltpu.SemaphoreType.DMA((n,)), pltpu.SemaphoreType.DMA((n,))],
    )(*arrays)


def _pack(arrays, rows_multiple=16, width=LANES):
    flat = jnp.concatenate([a.astype(F32).reshape(-1) for a in arrays])
    total = flat.shape[0]
    rows = -(-total // width)
    rows = -(-rows // rows_multiple) * rows_multiple
    return jnp.pad(flat, (0, rows * width - total)).reshape(rows, width)


def _unpack(buf, shapes):
    flat = buf.reshape(-1)
    out, off = [], 0
    for s in shapes:
        n = math.prod(s)
        out.append(flat[off:off + n].reshape(s))
        off += n
    return out


def kernel(x, norm_pre, norm_post, gla_w_in, gla_w_gate2, gla_b_gate, gla_o_gain, gla_w_out, sgu_w_in, sgu_ln_gain, sgu_ln_bias, sgu_w_spatial, sgu_b_spatial, sgu_w_out, loss_target, m_norm_pre, m_norm_post, m_gla_w_in, m_gla_w_gate2, m_gla_b_gate, m_gla_o_gain, m_gla_w_out, m_sgu_w_in, m_sgu_ln_gain, m_sgu_ln_bias, m_sgu_w_spatial, m_sgu_b_spatial, m_sgu_w_out, v_norm_pre, v_norm_post, v_gla_w_in, v_gla_w_gate2, v_gla_b_gate, v_gla_o_gain, v_gla_w_out, v_sgu_w_in, v_sgu_ln_gain, v_sgu_ln_bias, v_sgu_w_spatial, v_sgu_b_spatial, v_sgu_w_out):
    _, t, d = x.shape
    dk = d // 2
    gla_cols = gla_w_in.shape[2] * N_CHIPS
    gla_main = gla_cols - GLA_GATE_RANK
    gla_pad = gla_main + LANES
    chip = 2 * lax.axis_index("x") + lax.axis_index("y")
    core = lax.axis_index("c")
    core_idx = core.astype(jnp.int32).reshape(1)
    others = jnp.arange(N_CHIPS - 1, dtype=jnp.int32)
    others = others + (others >= chip).astype(jnp.int32)
    slots = jnp.concatenate([chip.astype(jnp.int32).reshape(1), others, core_idx])

    x0 = x[0]
    target = loss_target[0]

    wt_in_g, mt_in_g, vt_in_g = gla_w_in[0].T, m_gla_w_in[0].T, v_gla_w_in[0].T

    small_shard = _pack([gla_w_gate2[0], sgu_ln_gain[0], sgu_ln_bias[0]], rows_multiple=8, width=2 * LANES)
    own = [small_shard, wt_in_g.astype(BF16), gla_w_out[0].astype(BF16), sgu_w_in[0].astype(BF16),
           sgu_w_out[0].astype(BF16)]
    in_flight, token = _gather_start(own, name="gather_start")

    def arrived(i, after, name):
        land = _gather_wait(in_flight[i], after, name=name + "_wait")
        land = _sibling_forward(land, name=name + "_share")
        return lax.dynamic_update_slice(land, own[i][None], (chip, 0, 0))

    h0 = _norm_pre(x0, norm_pre[0:1] + token[0:1, 0:1], name="pre0")
    g_small = arrived(0, h0, "w_small")
    g_wi_g = arrived(1, g_small, "w_gla_in")
    wt_g = jnp.pad(g_wi_g.reshape(gla_cols, d), ((0, gla_pad - gla_cols), (0, 0)))
    shard_shapes = [gla_w_gate2.shape[1:], sgu_ln_gain.shape[1:], sgu_ln_bias.shape[1:]]
    per_chip = [_unpack(g_small[j], shard_shapes) for j in range(N_CHIPS)]
    w2_full = jnp.concatenate([p[0] for p in per_chip], axis=1)
    ln_gain = jnp.concatenate([p[1] for p in per_chip], axis=0)[None, :]
    ln_bias = jnp.concatenate([p[2] for p in per_chip], axis=0)[None, :]
    w2p = jnp.pad(w2_full, ((0, LANES - GLA_GATE_RANK), (0, 0)))

    pos_chunk = jnp.arange(SGU_BLOCK) // CHUNK
    mask = pos_chunk[:, None] >= pos_chunk[None, :]
    ws_masked = jnp.where(mask[None], sgu_w_spatial[0], 0.0)
    ws_masked_t = ws_masked.transpose(0, 2, 1)
    bs_t = sgu_b_spatial[0].T

    proj0 = _matmul(h0, wt_g, mode="nt", out_dtype=F32, name="gla_in", tn=896)
    o0, a0, s_before, s_final = _gla_fwd(proj0, w2p, gla_b_gate, gla_o_gain, name="gla_scan")
    w_out_g = arrived(2, a0, "w_gla_out").reshape(d, d)
    y0 = _matmul(a0, w_out_g, mode="nn", out_dtype=F32, name="gla_out")
    x1, h1 = _post_then_pre(x0, y0, norm_post[0:1], norm_pre[1:2], name="post0_pre1")
    g_wi_s = arrived(3, h1, "w_sgu_in")
    proj1 = _matmul(h1, g_wi_s, mode="nn", out_dtype=F32, name="sgu_in", b_shards=True)
    a1 = _sgu_fwd(proj1, ln_gain, ln_bias, ws_masked, bs_t, name="sgu_gate")
    w_out_s = arrived(4, a1, "w_sgu_out").reshape(d, d)
    y1 = _matmul(a1, w_out_s, mode="nn", out_dtype=F32, name="sgu_out")
    loss_part, dx2, dy1, d_post1 = _loss_head(x1, y1, norm_post[1:2], target, name="loss_head")

    def send_off(grad, name):
        peer, = _sibling_swap_halves([grad], name=name + "_swap")
        pair = _pair_sum_bf16(grad, core_idx, peer, name=name + "_pair")
        return _scatter_start(pair, name=name + "_start")

    def reduced(flight, after, name):
        pair, landed = _scatter_wait(flight, after, name=name + "_wait")
        return _chip_sum(pair, landed, slots, name=name + "_sum")

    dw_out_s = _matmul(a1, dy1, mode="tn", out_dtype=F32, name="d_sgu_w_out")
    fl_wo_s, tok = send_off(dw_out_s.reshape(N_CHIPS, d // N_CHIPS, d), "g_sgu_out")
    da1 = _matmul(dy1, w_out_s, mode="nt", out_dtype=F32, name="d_sgu_act", after=tok)
    dproj1, d_ws, d_bs_t, d_lg, d_lb = _sgu_bwd(da1, proj1, ln_gain, ln_bias, ws_masked, ws_masked_t, bs_t,
                                                name="sgu_gate_bwd")
    dw_in_s = _matmul(h1, dproj1, mode="tn", out_dtype=F32, name="d_sgu_w_in", out_shards=True)
    fl_wi_s, tok = send_off(dw_in_s, "g_sgu_in")
    dh1 = _matmul(dproj1, g_wi_s, mode="nt", out_dtype=F32, name="d_sgu_h", b_shards=True, tk=1536, after=tok)
    dx1, dy0, d_pre1, d_post0 = _mid_bwd(dx2, dh1, x1, norm_pre[1:2], y0, norm_post[0:1], name="pre1_post0_bwd")
    dw_out_g = _matmul(a0, dy0, mode="tn", out_dtype=F32, name="d_gla_w_out")
    fl_wo_g, tok = send_off(dw_out_g.reshape(N_CHIPS, d // N_CHIPS, d), "g_gla_out")
    da0 = _matmul(dy0, w_out_g, mode="nt", out_dtype=F32, name="d_gla_act", after=tok)
    dproj0, d_og, d_bg, d_w2p = _gla_bwd(da0, o0, proj0, w2p, gla_b_gate, gla_o_gain, s_before, s_final,
                                         name="gla_scan_bwd")
    dwt_in_g = _matmul(dproj0, h0, mode="tn", out_dtype=F32, name="d_gla_w_in", tm=896)
    fl_wi_g, tok = send_off(dwt_in_g[:gla_cols].reshape(N_CHIPS, gla_cols // N_CHIPS, d), "g_gla_in")
    dh0 = _matmul(dproj0, wt_g, mode="nn", out_dtype=F32, name="d_gla_h", tk=896, after=tok)
    grad_x, d_pre0 = _first_bwd(dx1, dh0, x0, norm_pre[0:1], name="pre0_bwd")

    small_shapes = [norm_pre.shape, norm_post.shape, gla_b_gate.shape, gla_o_gain.shape, sgu_w_spatial.shape,
                    sgu_b_spatial.shape, (1, GLA_GATE_RANK, dk), (1, d), (1, d), (1, LANES)]
    d_pre = jnp.concatenate([d_pre0, d_pre1], axis=0)
    d_post = jnp.concatenate([d_post0, d_post1], axis=0)
    d_wsp = jnp.where(mask[None], d_ws, 0.0)[None]
    small_part = _pack([d_pre, d_post, d_bg, d_og, d_wsp, d_bs_t.T[None], d_w2p[:GLA_GATE_RANK][None], d_lg, d_lb,
                        loss_part])
    small_all = _all_gather_dev(small_part, name="gather_small_grads")
    small_sum = _stack_sum(small_all, name="small_sum")
    (g_pre, g_post, g_bg, g_og, g_wsp, g_bsp, g_w2_full, g_lg_full, g_lb_full, loss_vec) = _unpack(small_sum, small_shapes)
    loss = loss_vec[0, 0]
    g_w2 = lax.dynamic_slice_in_dim(g_w2_full, chip * (dk // N_CHIPS), dk // N_CHIPS, axis=2)
    g_lg = lax.dynamic_slice_in_dim(g_lg_full, chip * (d // N_CHIPS), d // N_CHIPS, axis=1)
    g_lb = lax.dynamic_slice_in_dim(g_lb_full, chip * (d // N_CHIPS), d // N_CHIPS, axis=1)

    def big_update(w, g, m, v, name):
        return [u[None] for u in _adamw(w[0], g, m[0], v[0], name=name)]

    r_wo_s = reduced(fl_wo_s, small_sum, "g_sgu_out")
    r_wi_s = reduced(fl_wi_s, r_wo_s, "g_sgu_in")
    r_wo_g = reduced(fl_wo_g, r_wi_s, "g_gla_out")
    g_wo_sgu, g_wi_sgu, g_wo_gla = _sibling_share_halves([r_wo_s, r_wi_s, r_wo_g], name="grads_share_a")
    u_wo_sgu = big_update(sgu_w_out, g_wo_sgu, m_sgu_w_out, v_sgu_w_out, "adamw_sgu_w_out")
    u_wi_sgu = big_update(sgu_w_in, g_wi_sgu, m_sgu_w_in, v_sgu_w_in, "adamw_sgu_w_in")
    u_wo_gla = big_update(gla_w_out, g_wo_gla, m_gla_w_out, v_gla_w_out, "adamw_gla_w_out")
    r_wi_g = reduced(fl_wi_g, u_wo_gla[1], "g_gla_in")
    gt_wi_gla, = _sibling_share_halves([r_wi_g], name="grads_share_b")
    u_wi_gla = [u.T[None] for u in _adamw(wt_in_g, gt_wi_gla, mt_in_g, vt_in_g, name="adamw_gla_w_in")]

    small_w = [norm_pre, norm_post, gla_b_gate, gla_o_gain, sgu_w_spatial, sgu_b_spatial, gla_w_gate2, sgu_ln_gain,
               sgu_ln_bias]
    small_g = [g_pre, g_post, g_bg, g_og, g_wsp, g_bsp, g_w2, g_lg, g_lb]
    small_m = [m_norm_pre, m_norm_post, m_gla_b_gate, m_gla_o_gain, m_sgu_w_spatial, m_sgu_b_spatial, m_gla_w_gate2,
               m_sgu_ln_gain, m_sgu_ln_bias]
    small_v = [v_norm_pre, v_norm_post, v_gla_b_gate, v_gla_o_gain, v_sgu_w_spatial, v_sgu_b_spatial, v_gla_w_gate2,
               v_sgu_ln_gain, v_sgu_ln_bias]
    own_shapes = [w.shape for w in small_w]
    _, s_dl, s_m, s_v = _adamw(_pack(small_w), _pack(small_g), _pack(small_m), _pack(small_v), name="adamw_small")
    dl_s, m_s, v_s = _unpack(s_dl, own_shapes), _unpack(s_m, own_shapes), _unpack(s_v, own_shapes)

    def ordered(small, kind):
        pre, post, bg, og, wsp, bsp, w2, lg, lb = small
        return [pre, post, u_wi_gla[kind], w2, bg, og, u_wo_gla[kind], u_wi_sgu[kind], lg, lb, wsp, bsp, u_wo_sgu[kind]]

    return (loss, grad_x[None], *ordered(small_g, 0), *ordered(dl_s, 1), *ordered(m_s, 2), *ordered(v_s, 3))
```

```python
import functools
import math

import jax
import jax.numpy as jnp
from jax import lax
from jax.experimental import pallas as pl
from jax.experimental.pallas import tpu as pltpu

F32 = jnp.float32
BF16 = jnp.bfloat16
MESH = pl.DeviceIdType.MESH

EPS = 1e-6
CHUNK = 64
GLA_HEADS = 4
GLA_GATE_RANK = 16
GLA_TAU = 16.0
SGU_BLOCK = 128
SGU_GROUPS = 8
N_CHIPS = 4
N_DEV = 8
LANES = 128

ADAM_LR = 0.001
ADAM_B1 = 0.9
ADAM_B2 = 0.999
ADAM_EPS = 1e-08
ADAM_WD = 0.01
ADAM_STEP = 10

VMEM_LIMIT = 56 * 1024 * 1024


def _cparams(sem=None):
    return pltpu.CompilerParams(dimension_semantics=sem, vmem_limit_bytes=VMEM_LIMIT)


def _pick(n, cap, unit=LANES):
    best = None
    for t in range(unit, min(n, cap) + 1, unit):
        if n % t == 0:
            best = t
    assert best is not None, (n, cap, unit)
    return best


def _dot(a, b, dims):
    return lax.dot_general(a, b, (dims, ((), ())), preferred_element_type=F32)


def _dot_nn(a, b):
    return _dot(a, b, ((1,), (0,)))


def _dot_nt(a, b):
    return _dot(a, b, ((1,), (1,)))


def _dot_tn(a, b):
    return _dot(a, b, ((0,), (0,)))


def _matmul(a, b, *, mode, out_dtype, name, tm=1024, tn=512, tk=2048, b_shards=False, out_shards=False, after=None,
            out_rows=None):
    if mode == "tn":
        K, M = a.shape
    else:
        M, K = a.shape
    if b_shards:
        ns, br, bc = b.shape
        if mode == "nt":
            N, Kb = br, ns * bc
        else:
            Kb, N = br, ns * bc
    else:
        if mode == "nt":
            N, Kb = b.shape
        else:
            Kb, N = b.shape
    assert K == Kb, (a.shape, b.shape, mode)
    tm = _pick(M, tm)
    tk = _pick(K, tk)
    if b_shards and mode != "nt":
        tn = _pick(bc, tn)
    elif out_shards:
        tn = _pick(N // N_CHIPS, tn)
    else:
        tn = _pick(N, tn)
    if b_shards and mode == "nt":
        tk = _pick(bc, tk)
    nk = K // tk
    grid = (M // tm, N // tn, nk)

    if mode == "tn":
        a_spec = pl.BlockSpec((tk, tm), lambda i, j, k: (k, i))
    else:
        a_spec = pl.BlockSpec((tm, tk), lambda i, j, k: (i, k))
    if b_shards:
        if mode == "nt":
            per = bc // tk
            b_spec = pl.BlockSpec((None, tn, tk), lambda i, j, k: (k // per, j, k % per))
        else:
            per = bc // tn
            b_spec = pl.BlockSpec((None, tk, tn), lambda i, j, k: (j // per, k, j % per))
    elif mode == "nt":
        b_spec = pl.BlockSpec((tn, tk), lambda i, j, k: (j, k))
    else:
        b_spec = pl.BlockSpec((tk, tn), lambda i, j, k: (k, j))
    if out_shards:
        per_o = (N // N_CHIPS) // tn
        out_spec = pl.BlockSpec((None, tm, tn), lambda i, j, k: (j // per_o, i, j % per_o))
        out_shape = jax.ShapeDtypeStruct((N_CHIPS, M, N // N_CHIPS), out_dtype)
    else:
        out_spec = pl.BlockSpec((tm, tn), lambda i, j, k: (i, j))
        out_shape = jax.ShapeDtypeStruct((M if out_rows is None else out_rows, N), out_dtype)

    dims = {"nn": ((1,), (0,)), "nt": ((1,), (1,)), "tn": ((0,), (0,))}[mode]

    def body(a_ref, b_ref, *rest):
        o_ref, scratch = (rest[1], rest[2:]) if after is not None else (rest[0], rest[1:])
        part = _dot(a_ref[...].astype(BF16), b_ref[...].astype(BF16), dims)
        if nk == 1:
            o_ref[...] = part.astype(out_dtype)
        else:
            acc_ref, = scratch
            k = pl.program_id(2)

            @pl.when(k == 0)
            def _():
                acc_ref[...] = part

            @pl.when(k > 0)
            def _():
                acc_ref[...] += part

            @pl.when(k == nk - 1)
            def _():
                o_ref[...] = acc_ref[...].astype(out_dtype)

    extra_specs, extra_args = ([], []) if after is None else ([pl.BlockSpec(memory_space=pl.ANY)], [after])
    return pl.pallas_call(
        body, name=name, grid=grid, in_specs=[a_spec, b_spec] + extra_specs, out_specs=out_spec, out_shape=out_shape,
        scratch_shapes=[] if nk == 1 else [pltpu.VMEM((tm, tn), F32)],
        compiler_params=_cparams(("parallel", "parallel", "arbitrary")),
    )(a, b, *extra_args)


def _matmul_nt_shards(a, b, *, out_dtype, name, tm=1024, tn=512, after=None):
    M, K = a.shape
    ns, N, kc = b.shape
    assert K == ns * kc
    tm, tn = _pick(M, tm), _pick(N, tn)

    def body(a_ref, *rest):
        b_refs, o_ref = rest[:ns], rest[ns + (after is not None)]
        acc = _dot_nt(a_ref[:, 0:kc], b_refs[0][...])
        for j in range(1, ns):
            acc += _dot_nt(a_ref[:, j * kc:(j + 1) * kc], b_refs[j][...])
        o_ref[...] = acc.astype(out_dtype)

    def shard(j):
        return pl.BlockSpec((None, tn, kc), lambda i, n: (j, n, 0))

    extra_specs, extra_args = ([], []) if after is None else ([pl.BlockSpec(memory_space=pl.ANY)], [after])
    return pl.pallas_call(
        body, name=name, grid=(M // tm, N // tn),
        in_specs=[pl.BlockSpec((tm, K), lambda i, n: (i, 0))] + [shard(j) for j in range(ns)] + extra_specs,
        out_specs=pl.BlockSpec((tm, tn), lambda i, n: (i, n)), out_shape=jax.ShapeDtypeStruct((M, N), out_dtype),
        compiler_params=_cparams(("parallel", "parallel")),
    )(a, *([b] * ns), *extra_args)


def _rstd(x):
    return lax.rsqrt(jnp.mean(x * x, axis=-1, keepdims=True) + EPS)


def _row_spec(tr, d):
    return pl.BlockSpec((tr, d), lambda i: (i, 0))


def _vec_spec(d):
    return pl.BlockSpec((1, d), lambda i: (0, 0))


def _acc_rows(ref, i, val, cols=slice(None)):
    @pl.when(i == 0)
    def _():
        ref[:, cols] = val

    @pl.when(i > 0)
    def _():
        ref[:, cols] += val


def _norm_pre(x, gain, *, name, tr=256):
    t, d = x.shape
    tr = _pick(t, tr, 8)

    def body(x_ref, g_ref, h_ref):
        xv = x_ref[...]
        h_ref[...] = (xv * _rstd(xv) * g_ref[...]).astype(BF16)

    return pl.pallas_call(
        body, name=name, grid=(t // tr,), in_specs=[_row_spec(tr, d), _vec_spec(d)], out_specs=_row_spec(tr, d),
        out_shape=jax.ShapeDtypeStruct((t, d), BF16), compiler_params=_cparams(("parallel",)),
    )(x, gain)


def _post_then_pre(x, y, post_gain, pre_gain, *, name, tr=256):
    t, d = x.shape
    tr = _pick(t, tr, 8)

    def body(x_ref, y_ref, pg_ref, ng_ref, xn_ref, h_ref):
        yv = y_ref[...]
        xn = x_ref[...] + yv * _rstd(yv) * pg_ref[...]
        xn_ref[...] = xn
        h_ref[...] = (xn * _rstd(xn) * ng_ref[...]).astype(BF16)

    return pl.pallas_call(
        body, name=name, grid=(t // tr,),
        in_specs=[_row_spec(tr, d), _row_spec(tr, d), _vec_spec(d), _vec_spec(d)],
        out_specs=[_row_spec(tr, d), _row_spec(tr, d)],
        out_shape=[jax.ShapeDtypeStruct((t, d), F32), jax.ShapeDtypeStruct((t, d), BF16)],
        compiler_params=_cparams(("parallel",)),
    )(x, y, post_gain, pre_gain)


def _norm_bwd(dy, n, r, gain):
    dn = dy * gain
    return r * (dn - n * jnp.mean(dn * n, axis=-1, keepdims=True))


def _loss_head(x, y, post_gain, target, *, name, tr=256):
    t, d = x.shape
    tr = _pick(t, tr, 8)

    def body(x_ref, y_ref, pg_ref, t_ref, loss_ref, dx_ref, dy_ref, dpg_ref):
        i = pl.program_id(0)
        yv = y_ref[...]
        r = _rstd(yv)
        n = yv * r
        err = x_ref[...] + n * pg_ref[...] - t_ref[...]
        dx = err * (1.0 / d)
        dx_ref[...] = dx
        part = 0.5 * jnp.sum(jnp.mean(err * err, axis=-1, keepdims=True), axis=0, keepdims=True)
        _acc_rows(loss_ref, i, jnp.broadcast_to(part, (1, LANES)))
        _acc_rows(dpg_ref, i, jnp.sum(dx * n, axis=0, keepdims=True))
        dy_ref[...] = _norm_bwd(dx, n, r, pg_ref[...]).astype(BF16)

    return pl.pallas_call(
        body, name=name, grid=(t // tr,),
        in_specs=[_row_spec(tr, d), _row_spec(tr, d), _vec_spec(d), _row_spec(tr, d)],
        out_specs=[_vec_spec(LANES), _row_spec(tr, d), _row_spec(tr, d), _vec_spec(d)],
        out_shape=[jax.ShapeDtypeStruct((1, LANES), F32), jax.ShapeDtypeStruct((t, d), F32),
                   jax.ShapeDtypeStruct((t, d), BF16), jax.ShapeDtypeStruct((1, d), F32)],
        compiler_params=_cparams(("arbitrary",)),
    )(x, y, post_gain, target)


def _mid_bwd(dx_out, dh, x, pre_gain, y_prev, post_gain_prev, *, name, tr=256):
    t, d = x.shape
    tr = _pick(t, tr, 8)

    def body(dxo_ref, dh_ref, x_ref, ng_ref, y_ref, pg_ref, dx_ref, dy_ref, dng_ref, dpg_ref):
        i = pl.program_id(0)
        xv = x_ref[...]
        r = _rstd(xv)
        xh = xv * r
        dhv = dh_ref[...]
        _acc_rows(dng_ref, i, jnp.sum(dhv * xh, axis=0, keepdims=True))
        dx = dxo_ref[...] + _norm_bwd(dhv, xh, r, ng_ref[...])
        dx_ref[...] = dx
        yv = y_ref[...]
        ry = _rstd(yv)
        n = yv * ry
        _acc_rows(dpg_ref, i, jnp.sum(dx * n, axis=0, keepdims=True))
        dy_ref[...] = _norm_bwd(dx, n, ry, pg_ref[...]).astype(BF16)

    return pl.pallas_call(
        body, name=name, grid=(t // tr,),
        in_specs=[_row_spec(tr, d), _row_spec(tr, d), _row_spec(tr, d), _vec_spec(d), _row_spec(tr, d), _vec_spec(d)],
        out_specs=[_row_spec(tr, d), _row_spec(tr, d), _vec_spec(d), _vec_spec(d)],
        out_shape=[jax.ShapeDtypeStruct((t, d), F32), jax.ShapeDtypeStruct((t, d), BF16),
                   jax.ShapeDtypeStruct((1, d), F32), jax.ShapeDtypeStruct((1, d), F32)],
        compiler_params=_cparams(("arbitrary",)),
    )(dx_out, dh, x, pre_gain, y_prev, post_gain_prev)


def _first_bwd(dx_out, dh, x, pre_gain, *, name, tr=256):
    t, d = x.shape
    tr = _pick(t, tr, 8)

    def body(dxo_ref, dh_ref, x_ref, ng_ref, dx_ref, dng_ref):
        i = pl.program_id(0)
        xv = x_ref[...]
        r = _rstd(xv)
        xh = xv * r
        dhv = dh_ref[...]
        _acc_rows(dng_ref, i, jnp.sum(dhv * xh, axis=0, keepdims=True))
        dx_ref[...] = dxo_ref[...] + _norm_bwd(dhv, xh, r, ng_ref[...])

    return pl.pallas_call(
        body, name=name, grid=(t // tr,),
        in_specs=[_row_spec(tr, d), _row_spec(tr, d), _row_spec(tr, d), _vec_spec(d)],
        out_specs=[_row_spec(tr, d), _vec_spec(d)],
        out_shape=[jax.ShapeDtypeStruct((t, d), F32), jax.ShapeDtypeStruct((1, d), F32)],
        compiler_params=_cparams(("arbitrary",)),
    )(dx_out, dh, x, pre_gain)


def _sigmoid(x):
    return 1.0 / (1.0 + jnp.exp(-x))


def _log_sigmoid(x):
    return jnp.minimum(x, 0.0) - jnp.log(1.0 + jnp.exp(-jnp.abs(x)))


_GELU_C = math.sqrt(2.0 / math.pi)


def _gelu_parts(x):
    x2 = x * x
    th = jnp.tanh(_GELU_C * (x + 0.044715 * x * x2))
    val = 0.5 * x * (1.0 + th)
    grad = 0.5 * (1.0 + th) + 0.5 * x * (1.0 - th * th) * (_GELU_C * (1.0 + 3.0 * 0.044715 * x2))
    return val, grad


def _split3(x):
    hi = x.astype(BF16)
    r1 = x - hi.astype(F32)
    mid = r1.astype(BF16)
    lo = (r1 - mid.astype(F32)).astype(BF16)
    return hi, mid, lo


def _tri_matmul(tri_bf16, x):
    hi, mid, lo = _split3(x)
    return _dot_nn(tri_bf16, hi) + _dot_nn(tri_bf16, mid) + _dot_nn(tri_bf16, lo)


def _gla_dims(d):
    dk, dv = d // 2, d
    return dk, dv, dk // GLA_HEADS, dv // GLA_HEADS


def _gla_gates(glr, k, w2_ref, b_ref):
    z = _dot_nn(glr.astype(BF16), w2_ref[...].astype(BF16)) + b_ref[...]
    la = _log_sigmoid(z) * (1.0 / GLA_TAU)
    row = lax.broadcasted_iota(jnp.int32, (CHUNK, CHUNK), 0)
    col = lax.broadcasted_iota(jnp.int32, (CHUNK, CHUNK), 1)
    incl = (row >= col).astype(BF16)
    bcum = _tri_matmul(incl, la)
    b_end = bcum[CHUNK - 1:CHUNK, :]
    e_rest = jnp.exp(b_end - bcum)
    return z, e_rest, k * e_rest, jnp.exp(b_end)


def _gla_fwd(proj, w2p, b_gate, o_gain, *, name):
    t, wcols = proj.shape
    d = o_gain.shape[1]
    dk, dv, dkh, dvh = _gla_dims(d)
    nc = t // CHUNK
    c_k, c_v, c_g, c_r = dk, 2 * dk, 2 * dk + dv, 2 * dk + 2 * dv
    scale = dkh ** -0.5

    def body(p_ref, w2_ref, b_ref, og_ref, o_ref, a_ref, sb_ref, sfin_ref, s_ref):
        i = pl.program_id(0)

        @pl.when(i == 0)
        def _():
            s_ref[...] = jnp.zeros_like(s_ref)

        q = p_ref[:, 0:dk] * scale
        k = p_ref[:, c_k:c_k + dk]
        glr = p_ref[:, c_r:c_r + LANES]
        _, _, kdec, decay = _gla_gates(glr, k, w2_ref, b_ref)
        for h in range(GLA_HEADS):
            ks = slice(h * dkh, (h + 1) * dkh)
            vs = slice(h * dvh, (h + 1) * dvh)
            v_h = p_ref[:, c_v + h * dvh:c_v + (h + 1) * dvh]
            g_h = p_ref[:, c_g + h * dvh:c_g + (h + 1) * dvh]
            s_old = s_ref[h]
            sb_ref[0, h] = s_old
            s_new = s_old * decay[:, ks] + _dot_tn(v_h.astype(BF16), kdec[:, ks].astype(BF16))
            s_ref[h] = s_new
            o_h = _dot_nt(q[:, ks].astype(BF16), s_new.astype(BF16))
            o_ref[:, vs] = o_h
            on = o_h * _rstd(o_h)
            a_ref[:, vs] = (on * og_ref[:, vs] * (g_h * _sigmoid(g_h))).astype(BF16)

        @pl.when(i == nc - 1)
        def _():
            sfin_ref[...] = s_ref[...]

    full = lambda *shape: pl.BlockSpec(shape, lambda i: (0,) * len(shape))
    return pl.pallas_call(
        body, name=name, grid=(nc,),
        in_specs=[pl.BlockSpec((CHUNK, wcols), lambda i: (i, 0)), full(LANES, dk), full(1, dk), full(1, dv)],
        out_specs=[pl.BlockSpec((CHUNK, dv), lambda i: (i, 0)), pl.BlockSpec((CHUNK, dv), lambda i: (i, 0)),
                   pl.BlockSpec((1, GLA_HEADS, dvh, dkh), lambda i: (i, 0, 0, 0)), full(GLA_HEADS, dvh, dkh)],
        out_shape=[jax.ShapeDtypeStruct((t, dv), F32), jax.ShapeDtypeStruct((t, dv), BF16),
                   jax.ShapeDtypeStruct((nc, GLA_HEADS, dvh, dkh), F32),
                   jax.ShapeDtypeStruct((GLA_HEADS, dvh, dkh), F32)],
        scratch_shapes=[pltpu.VMEM((GLA_HEADS, dvh, dkh), F32)],
        compiler_params=_cparams(("arbitrary",)),
    )(proj, w2p, b_gate, o_gain)


def _gla_bwd(da, o, proj, w2p, b_gate, o_gain, s_before, s_final, *, name):
    t, wcols = proj.shape
    d = o_gain.shape[1]
    dk, dv, dkh, dvh = _gla_dims(d)
    nc = t // CHUNK
    c_k, c_v, c_g, c_r = dk, 2 * dk, 2 * dk + dv, 2 * dk + 2 * dv
    scale = dkh ** -0.5

    def body(da_ref, o_ref, p_ref, w2_ref, b_ref, og_ref, sb_ref, sfin_ref,
             dp_ref, dog_ref, db_ref, dw2_ref, s_ref, gc_ref, dkd_ref):
        i = pl.program_id(0)

        @pl.when(i == 0)
        def _():
            s_ref[...] = sfin_ref[...]
            gc_ref[...] = jnp.zeros_like(gc_ref)

        q = p_ref[:, 0:dk] * scale
        k = p_ref[:, c_k:c_k + dk]
        glr = p_ref[:, c_r:c_r + LANES]
        z, e_rest, kdec, decay = _gla_gates(glr, k, w2_ref, b_ref)
        ddecay = []
        for h in range(GLA_HEADS):
            ks = slice(h * dkh, (h + 1) * dkh)
            vs = slice(h * dvh, (h + 1) * dvh)
            v_h = p_ref[:, c_v + h * dvh:c_v + (h + 1) * dvh]
            g_h = p_ref[:, c_g + h * dvh:c_g + (h + 1) * dvh]
            da_h = da_ref[:, vs]
            o_h = o_ref[:, vs]
            og_h = og_ref[:, vs]
            r = _rstd(o_h)
            on = o_h * r
            sg = _sigmoid(g_h)
            silu = g_h * sg
            _acc_rows(dog_ref, i, jnp.sum(da_h * silu * on, axis=0, keepdims=True), vs)
            dp_ref[:, c_g + h * dvh:c_g + (h + 1) * dvh] = (
                da_h * (on * og_h) * (sg * (1.0 + g_h * (1.0 - sg)))).astype(BF16)
            don = da_h * silu * og_h
            do_h = (r * (don - on * jnp.mean(don * on, axis=-1, keepdims=True))).astype(BF16)
            s_cur = s_ref[h]
            dp_ref[:, ks] = (_dot_nn(do_h, s_cur.astype(BF16)) * scale).astype(BF16)
            g_tot = gc_ref[h] + _dot_tn(do_h, q[:, ks].astype(BF16))
            g_bf = g_tot.astype(BF16)
            dkd_ref[:, ks] = _dot_nn(v_h.astype(BF16), g_bf)
            dp_ref[:, c_v + h * dvh:c_v + (h + 1) * dvh] = _dot_nt(kdec[:, ks].astype(BF16), g_bf).astype(BF16)
            s_prev = sb_ref[0, h]
            ddecay.append(jnp.sum(g_tot * s_prev, axis=0, keepdims=True))
            gc_ref[h] = g_tot * decay[:, ks]
            s_ref[h] = s_prev
        dkdec = dkd_ref[...]
        dp_ref[:, c_k:c_k + dk] = (dkdec * e_rest).astype(BF16)
        d_e = dkdec * kdec
        row = lax.broadcasted_iota(jnp.int32, (CHUNK, CHUNK), 0)
        col = lax.broadcasted_iota(jnp.int32, (CHUNK, CHUNK), 1)
        excl = (row > col).astype(BF16)
        dla = jnp.concatenate(ddecay, axis=1) * decay + _tri_matmul(excl, d_e)
        dz = dla * (1.0 / GLA_TAU) * (1.0 - _sigmoid(z))
        _acc_rows(db_ref, i, jnp.sum(dz, axis=0, keepdims=True))
        dz_bf = dz.astype(BF16)
        dw2 = _dot_tn(glr.astype(BF16), dz_bf)

        @pl.when(i == 0)
        def _():
            dw2_ref[...] = dw2

        @pl.when(i > 0)
        def _():
            dw2_ref[...] += dw2

        dp_ref[:, c_r:c_r + LANES] = _dot_nt(dz_bf, w2_ref[...].astype(BF16)).astype(BF16)

    rev = lambda i: (nc - 1 - i, 0)
    full = lambda *shape: pl.BlockSpec(shape, lambda i: (0,) * len(shape))
    return pl.pallas_call(
        body, name=name, grid=(nc,),
        in_specs=[pl.BlockSpec((CHUNK, dv), rev), pl.BlockSpec((CHUNK, dv), rev), pl.BlockSpec((CHUNK, wcols), rev),
                  full(LANES, dk), full(1, dk), full(1, dv),
                  pl.BlockSpec((1, GLA_HEADS, dvh, dkh), lambda i: (nc - 1 - i, 0, 0, 0)), full(GLA_HEADS, dvh, dkh)],
        out_specs=[pl.BlockSpec((CHUNK, wcols), rev), full(1, dv), full(1, dk), full(LANES, dk)],
        out_shape=[jax.ShapeDtypeStruct((t, wcols), BF16), jax.ShapeDtypeStruct((1, dv), F32),
                   jax.ShapeDtypeStruct((1, dk), F32), jax.ShapeDtypeStruct((LANES, dk), F32)],
        scratch_shapes=[pltpu.VMEM((GLA_HEADS, dvh, dkh), F32), pltpu.VMEM((GLA_HEADS, dvh, dkh), F32),
                        pltpu.VMEM((CHUNK, dk), F32)],
        compiler_params=_cparams(("arbitrary",)),
    )(da, o, proj, w2p, b_gate, o_gain, s_before, s_final)


def _sgu_mid(p_ref, lg_ref, lb_ref, ws_ref, bst_ref, w):
    gd = w // SGU_GROUPS
    u_act, du_fac = _gelu_parts(p_ref[:, 0:w])
    vf, dv_fac = _gelu_parts(p_ref[:, w:2 * w])
    mu = jnp.mean(vf, axis=-1, keepdims=True)
    cen = vf - mu
    rstd = lax.rsqrt(jnp.mean(cen * cen, axis=-1, keepdims=True) + EPS)
    xh = cen * rstd
    vn = (xh * lg_ref[...] + lb_ref[...]).astype(BF16)
    vs = [_dot_nn(ws_ref[g].astype(BF16), vn[:, g * gd:(g + 1) * gd]) + bst_ref[:, g:g + 1]
          for g in range(SGU_GROUPS)]
    return u_act, du_fac, dv_fac, rstd, xh, vn, vs


def _sgu_fwd(proj, ln_gain, ln_bias, ws_masked, bs_t, *, name):
    t, w3 = proj.shape
    w = w3 // 3
    gd = w // SGU_GROUPS
    nb = t // SGU_BLOCK

    def body(p_ref, lg_ref, lb_ref, ws_ref, bst_ref, a_ref):
        u_act, _, _, _, _, _, vs = _sgu_mid(p_ref, lg_ref, lb_ref, ws_ref, bst_ref, w)
        for g in range(SGU_GROUPS):
            cs = slice(g * gd, (g + 1) * gd)
            gate = p_ref[:, 2 * w + g * gd:2 * w + (g + 1) * gd]
            a_ref[:, cs] = (u_act[:, cs] * vs[g] * (gate * _sigmoid(gate))).astype(BF16)

    full = lambda *shape: pl.BlockSpec(shape, lambda i: (0,) * len(shape))
    return pl.pallas_call(
        body, name=name, grid=(nb,),
        in_specs=[pl.BlockSpec((SGU_BLOCK, w3), lambda i: (i, 0)), full(1, w), full(1, w),
                  full(SGU_GROUPS, SGU_BLOCK, SGU_BLOCK), full(SGU_BLOCK, SGU_GROUPS)],
        out_specs=pl.BlockSpec((SGU_BLOCK, w), lambda i: (i, 0)),
        out_shape=jax.ShapeDtypeStruct((t, w), BF16),
        compiler_params=_cparams(("parallel",)),
    )(proj, ln_gain, ln_bias, ws_masked, bs_t)


def _sgu_bwd(da, proj, ln_gain, ln_bias, ws_masked, ws_masked_t, bs_t, *, name):
    t, w3 = proj.shape
    w = w3 // 3
    gd = w // SGU_GROUPS
    nb = t // SGU_BLOCK

    def body(da_ref, p_ref, lg_ref, lb_ref, ws_ref, wst_ref, bst_ref, dp_ref, dws_ref, dbst_ref, dlg_ref, dlb_ref,
             dvn_ref):
        i = pl.program_id(0)
        u_act, du_fac, dv_fac, rstd, xh, vn, vs = _sgu_mid(p_ref, lg_ref, lb_ref, ws_ref, bst_ref, w)
        for g in range(SGU_GROUPS):
            cs = slice(g * gd, (g + 1) * gd)
            gate = p_ref[:, 2 * w + g * gd:2 * w + (g + 1) * gd]
            sg = _sigmoid(gate)
            silu = gate * sg
            da_g = da_ref[:, cs]
            ua_g = u_act[:, cs]
            dp_ref[:, cs] = (da_g * vs[g] * silu * du_fac[:, cs]).astype(BF16)
            dp_ref[:, 2 * w + g * gd:2 * w + (g + 1) * gd] = (
                da_g * ua_g * vs[g] * (sg * (1.0 + gate * (1.0 - sg)))).astype(BF16)
            dvs = da_g * ua_g * silu
            dvs_bf = dvs.astype(BF16)
            dvn_ref[:, cs] = _dot_nn(wst_ref[g].astype(BF16), dvs_bf)
            dws = _dot_nt(dvs_bf, vn[:, cs])
            dbs = jnp.sum(dvs, axis=1, keepdims=True)

            @pl.when(i == 0)
            def _():
                dws_ref[g] = dws
                dbst_ref[:, g:g + 1] = dbs

            @pl.when(i > 0)
            def _():
                dws_ref[g] += dws
                dbst_ref[:, g:g + 1] += dbs

        dvn = dvn_ref[...]
        _acc_rows(dlg_ref, i, jnp.sum(dvn * xh, axis=0, keepdims=True))
        _acc_rows(dlb_ref, i, jnp.sum(dvn, axis=0, keepdims=True))
        dxh = dvn * lg_ref[...]
        dvf = rstd * (dxh - jnp.mean(dxh, axis=-1, keepdims=True)
                      - xh * jnp.mean(dxh * xh, axis=-1, keepdims=True))
        dp_ref[:, w:2 * w] = (dvf * dv_fac).astype(BF16)

    full = lambda *shape: pl.BlockSpec(shape, lambda i: (0,) * len(shape))
    return pl.pallas_call(
        body, name=name, grid=(nb,),
        in_specs=[pl.BlockSpec((SGU_BLOCK, w), lambda i: (i, 0)), pl.BlockSpec((SGU_BLOCK, w3), lambda i: (i, 0)),
                  full(1, w), full(1, w), full(SGU_GROUPS, SGU_BLOCK, SGU_BLOCK),
                  full(SGU_GROUPS, SGU_BLOCK, SGU_BLOCK), full(SGU_BLOCK, SGU_GROUPS)],
        out_specs=[pl.BlockSpec((SGU_BLOCK, w3), lambda i: (i, 0)), full(SGU_GROUPS, SGU_BLOCK, SGU_BLOCK),
                   full(SGU_BLOCK, SGU_GROUPS), full(1, w), full(1, w)],
        out_shape=[jax.ShapeDtypeStruct((t, w3), BF16), jax.ShapeDtypeStruct((SGU_GROUPS, SGU_BLOCK, SGU_BLOCK), F32),
                   jax.ShapeDtypeStruct((SGU_BLOCK, SGU_GROUPS), F32), jax.ShapeDtypeStruct((1, w), F32),
                   jax.ShapeDtypeStruct((1, w), F32)],
        scratch_shapes=[pltpu.VMEM((SGU_BLOCK, w), F32)],
        compiler_params=_cparams(("arbitrary",)),
    )(da, proj, ln_gain, ln_bias, ws_masked, ws_masked_t, bs_t)


def _tile2d(rows, cols, block_bytes, row_unit):
    if rows % row_unit == 0:
        return _pick(rows, max(row_unit, block_bytes // (4 * cols)), row_unit), cols
    return rows, _pick(cols, max(LANES, block_bytes // (4 * rows)))


def _adamw(w, g, m, v, *, name, block_bytes=1 << 20):
    rows, cols = w.shape
    tr, tc = _tile2d(rows, cols, block_bytes, 8)

    def body(w_ref, g_ref, m_ref, v_ref, go_ref, d_ref, mo_ref, vo_ref):
        gv = g_ref[...]
        go_ref[...] = gv
        mn = ADAM_B1 * m_ref[...] + (1.0 - ADAM_B1) * gv
        vn = ADAM_B2 * v_ref[...] + (1.0 - ADAM_B2) * (gv * gv)
        m_hat = mn / (1.0 - ADAM_B1 ** ADAM_STEP)
        v_hat = vn / (1.0 - ADAM_B2 ** ADAM_STEP)
        d_ref[...] = -ADAM_LR * (m_hat / (jnp.sqrt(v_hat) + ADAM_EPS) + ADAM_WD * w_ref[...])
        mo_ref[...] = mn
        vo_ref[...] = vn

    spec = pl.BlockSpec((tr, tc), lambda i, j: (i, j))
    return pl.pallas_call(
        body, name=name, grid=(rows // tr, cols // tc), in_specs=[spec] * 4, out_specs=[spec] * 4,
        out_shape=[jax.ShapeDtypeStruct((rows, cols), F32)] * 4,
        compiler_params=_cparams(("parallel", "parallel")),
    )(w, g, m, v)


def _pair_sum_bf16(own, core_idx, peer, *, name, block_bytes=1 << 20):
    s, r, c = own.shape
    hc = c // 2
    tr, tc = _tile2d(r, hc, block_bytes, 16)
    ncb = hc // tc

    def body(h_ref, a_ref, b_ref, o_ref):
        o_ref[...] = (a_ref[...] + b_ref[...]).astype(BF16)

    grid_spec = pltpu.PrefetchScalarGridSpec(
        num_scalar_prefetch=1, grid=(s, r // tr, ncb),
        in_specs=[pl.BlockSpec((None, tr, tc), lambda j, i, k, h: (j, i, h[0] * ncb + k)),
                  pl.BlockSpec((None, tr, tc), lambda j, i, k, h: (j, i, k))],
        out_specs=pl.BlockSpec((None, tr, tc), lambda j, i, k, h: (j, i, k)))
    return pl.pallas_call(
        body, name=name, grid_spec=grid_spec, out_shape=jax.ShapeDtypeStruct((s, r, hc), BF16),
        compiler_params=_cparams(("parallel", "parallel", "parallel")),
    )(core_idx, own, peer)


def _chip_sum(pair, landed, slots, *, name, block_bytes=1 << 20):
    _, r, hc = pair.shape
    tr, tc = _tile2d(r, hc, block_bytes, 16)
    ncb = hc // tc

    def body(s_ref, own_ref, l0_ref, l1_ref, l2_ref, o_ref):
        o_ref[...] = ((own_ref[...].astype(F32) + l0_ref[...].astype(F32)) + l1_ref[...].astype(F32)
                      ) + l2_ref[...].astype(F32)

    def slab(which):
        return pl.BlockSpec((None, tr, tc), lambda i, k, s: (s[which], i, k))

    grid_spec = pltpu.PrefetchScalarGridSpec(
        num_scalar_prefetch=1, grid=(r // tr, ncb),
        in_specs=[slab(0), slab(1), slab(2), slab(3)],
        out_specs=pl.BlockSpec((tr, tc), lambda i, k, s: (i, s[4] * ncb + k)))
    return pl.pallas_call(
        body, name=name, grid_spec=grid_spec, out_shape=jax.ShapeDtypeStruct((r, 2 * hc), F32),
        compiler_params=_cparams(("parallel", "parallel")),
    )(slots, pair, landed, landed, landed)


def _stack_sum(x, *, name, out_dtype=F32, block_bytes=1 << 20):
    s, r, c = x.shape
    tr = _pick(r, max(8, block_bytes // (4 * c)), 16) if r % 16 == 0 else r

    def body(x_ref, o_ref):
        acc = x_ref[0].astype(F32)
        for j in range(1, s):
            acc = acc + x_ref[j].astype(F32)
        o_ref[...] = acc.astype(out_dtype)

    return pl.pallas_call(
        body, name=name, grid=(r // tr,),
        in_specs=[pl.BlockSpec((s, tr, c), lambda i: (0, i, 0))], out_specs=pl.BlockSpec((tr, c), lambda i: (i, 0)),
        out_shape=jax.ShapeDtypeStruct((r, c), out_dtype), compiler_params=_cparams(("parallel",)),
    )(x)


HBM = pl.BlockSpec(memory_space=pltpu.HBM)


def _place():
    x, y, c = lax.axis_index("x"), lax.axis_index("y"), lax.axis_index("c")
    other_chips = [(1 - x, y), (x, 1 - y), (1 - x, 1 - y)]
    return x, y, c, other_chips


def _half_cols(cols, which):
    hc = cols // 2
    return pl.ds(pl.multiple_of(which * hc, LANES), hc)


SEM = pl.BlockSpec(memory_space=pltpu.SEMAPHORE)
ANY = pl.BlockSpec(memory_space=pl.ANY)
SIDE_EFFECT = pltpu.SideEffectType.DATAFLOW_SIDE_EFFECTING
TOKEN_SHAPE = (8, LANES)


def _hbm(shape, dtype):
    return pltpu.HBM(shape, dtype)


def _in_hbm(a):
    return pltpu.with_memory_space_constraint(a, pltpu.HBM)


def _gather_copy(src_ref, land_ref, ssem, rsem, k, chip_of_block, to, c):
    cols = src_ref.shape[1]
    return pltpu.make_async_remote_copy(
        src_ref=src_ref.at[:, _half_cols(cols, c)], dst_ref=land_ref.at[chip_of_block, :, _half_cols(cols, c)],
        send_sem=ssem.at[k], recv_sem=rsem.at[k], device_id=to, device_id_type=MESH)


def _gather_start(shards, *, name):
    n = len(shards)

    def body(*refs):
        srcs, lands = refs[:n], refs[n:2 * n]
        outs = refs[2 * n:]
        token = outs[-1]
        x, y, c, chips = _place()
        me = 2 * x + y
        for a in range(n):
            ssem, rsem = outs[4 * a], outs[4 * a + 1]
            for k, (cx, cy) in enumerate(chips):
                _gather_copy(srcs[a], lands[a], ssem, rsem, k, me, (cx, cy, c), c).start()
        token[...] = jnp.zeros_like(token)

    out_shape, out_specs, aliases = [], [], {}
    for a, s in enumerate(shards):
        out_shape += [pltpu.SemaphoreType.DMA((3,)), pltpu.SemaphoreType.DMA((3,)), _hbm(s.shape, s.dtype),
                      _hbm((N_CHIPS,) + s.shape, s.dtype)]
        out_specs += [SEM, SEM, HBM, HBM]
        aliases[a] = 4 * a + 2
        aliases[n + a] = 4 * a + 3
    out_shape.append(jax.ShapeDtypeStruct(TOKEN_SHAPE, F32))
    out_specs.append(pl.BlockSpec(memory_space=pltpu.VMEM))
    lands = [_in_hbm(lax.empty((N_CHIPS,) + s.shape, s.dtype)) for s in shards]
    res = pl.pallas_call(
        body, name=name, in_specs=[HBM] * (2 * n), out_specs=out_specs, out_shape=out_shape,
        input_output_aliases=aliases, compiler_params=pltpu.CompilerParams(has_side_effects=SIDE_EFFECT),
    )(*[_in_hbm(s) for s in shards], *lands)
    return [tuple(res[4 * a:4 * a + 4]) for a in range(n)], res[-1]


def _wait_call(wait_fn, parts, after, *, name):
    ssem, rsem, src, land = parts
    after = list(after) if isinstance(after, (list, tuple)) else [after]

    def body(src_ref, land_ref, ssem_ref, rsem_ref, *rest):
        wait_fn(src_ref, land_ref, ssem_ref, rsem_ref)

    return pl.pallas_call(
        body, name=name, in_specs=[HBM, HBM, SEM, SEM] + [ANY] * len(after), out_specs=[HBM, HBM],
        out_shape=[_hbm(src.shape, src.dtype), _hbm(land.shape, land.dtype)], input_output_aliases={0: 0, 1: 1},
        compiler_params=pltpu.CompilerParams(has_side_effects=SIDE_EFFECT),
    )(src, land, ssem, rsem, *after)


def _gather_wait(parts, after, *, name):
    def wait(src_ref, land_ref, ssem_ref, rsem_ref):
        x, y, c, chips = _place()
        for k, (cx, cy) in enumerate(chips):
            cp = _gather_copy(src_ref, land_ref, ssem_ref, rsem_ref, k, 2 * cx + cy, (x, y, c), c)
            cp.wait_send()
            cp.wait_recv()

    return _wait_call(wait, parts, after, name=name)[1]


def _sibling_forward(land, *, name):
    def body(_, buf, send_sems, recv_sems):
        x, y, c, chips = _place()
        cols = buf.shape[2]
        copies = []
        for k, (cx, cy) in enumerate(chips):
            mine = buf.at[2 * cx + cy, :, _half_cols(cols, c)]
            cp = pltpu.make_async_remote_copy(
                src_ref=mine, dst_ref=mine, send_sem=send_sems.at[k], recv_sem=recv_sems.at[k],
                device_id=(x, y, 1 - c), device_id_type=MESH)
            cp.start()
            copies.append(cp)
        for k, (cx, cy) in enumerate(chips):
            theirs = buf.at[2 * cx + cy, :, _half_cols(cols, 1 - c)]
            pltpu.make_async_remote_copy(
                src_ref=theirs, dst_ref=theirs, send_sem=send_sems.at[k], recv_sem=recv_sems.at[k],
                device_id=(x, y, c), device_id_type=MESH).wait_recv()
        for cp in copies:
            cp.wait_send()

    return pl.pallas_call(
        body, name=name, in_specs=[HBM], out_specs=HBM, out_shape=jax.ShapeDtypeStruct(land.shape, land.dtype),
        input_output_aliases={0: 0},
        scratch_shapes=[pltpu.SemaphoreType.DMA((3,)), pltpu.SemaphoreType.DMA((3,))],
    )(land)


def _scatter_copy(src_ref, land_ref, ssem, rsem, k, src_slab, dst_slab, to):
    return pltpu.make_async_remote_copy(
        src_ref=src_ref.at[src_slab], dst_ref=land_ref.at[dst_slab], send_sem=ssem.at[k], recv_sem=rsem.at[k],
        device_id=to, device_id_type=MESH)


def _scatter_start(part, *, name):
    def start(src_ref, land_ref, ssem, rsem):
        x, y, c, chips = _place()
        me = 2 * x + y
        for k, (cx, cy) in enumerate(chips):
            _scatter_copy(src_ref, land_ref, ssem, rsem, k, 2 * cx + cy, me, (cx, cy, c)).start()

    return _split_start(start, part, part.shape, N_CHIPS - 1, name=name)


def _scatter_wait(parts, after, *, name):
    def wait(src_ref, land_ref, ssem_ref, rsem_ref):
        x, y, c, chips = _place()
        for k, (cx, cy) in enumerate(chips):
            idx = 2 * cx + cy
            cp = _scatter_copy(src_ref, land_ref, ssem_ref, rsem_ref, k, idx, idx, (x, y, c))
            cp.wait_send()
            cp.wait_recv()

    return _wait_call(wait, parts, after, name=name)


def _split_start(start_fn, src, land_shape, n_sems, *, name):
    def body(src_ref, land_ref, ssem, rsem, src_out, land_out, token):
        start_fn(src_ref, land_ref, ssem, rsem)
        token[...] = jnp.zeros_like(token)

    res = pl.pallas_call(
        body, name=name, in_specs=[HBM, HBM], out_specs=[SEM, SEM, HBM, HBM, pl.BlockSpec(memory_space=pltpu.VMEM)],
        out_shape=[pltpu.SemaphoreType.DMA((n_sems,)), pltpu.SemaphoreType.DMA((n_sems,)), _hbm(src.shape, src.dtype),
                   _hbm(land_shape, src.dtype), jax.ShapeDtypeStruct(TOKEN_SHAPE, F32)],
        input_output_aliases={0: 2, 1: 3}, compiler_params=pltpu.CompilerParams(has_side_effects=SIDE_EFFECT),
    )(_in_hbm(src), _in_hbm(lax.empty(land_shape, src.dtype)))
    return tuple(res[:4]), res[4]


def _swap_copy(src_ref, land_ref, ssem, rsem, which, to):
    return pltpu.make_async_remote_copy(
        src_ref=src_ref.at[:, :, _half_cols(src_ref.shape[2], which)], dst_ref=land_ref,
        send_sem=ssem.at[0], recv_sem=rsem.at[0], device_id=to, device_id_type=MESH)


def _swap_start(grad, *, name):
    def start(src_ref, land_ref, ssem, rsem):
        x, y, c, _ = _place()
        _swap_copy(src_ref, land_ref, ssem, rsem, 1 - c, (x, y, 1 - c)).start()

    s, r, cols = grad.shape
    return _split_start(start, grad, (s, r, cols // 2), 1, name=name)


def _swap_wait(parts, after, *, name):
    def wait(src_ref, land_ref, ssem_ref, rsem_ref):
        x, y, c, _ = _place()
        cp = _swap_copy(src_ref, land_ref, ssem_ref, rsem_ref, 1 - c, (x, y, c))
        cp.wait_send()
        cp.wait_recv()

    return _wait_call(wait, parts, after, name=name)


def _dev_peers(x, y, c, chips):
    return [(x, y, 1 - c)] + [(cx, cy, c) for cx, cy in chips] + [(cx, cy, 1 - c) for cx, cy in chips]


def _dev_gather_start(part, *, name):
    def start(src_ref, land_ref, ssem, rsem):
        x, y, c, chips = _place()
        for k, to in enumerate(_dev_peers(x, y, c, chips)):
            pltpu.make_async_remote_copy(
                src_ref=src_ref, dst_ref=land_ref.at[4 * x + 2 * y + c], send_sem=ssem.at[k], recv_sem=rsem.at[k],
                device_id=to, device_id_type=MESH).start()

    return _split_start(start, part, (N_DEV,) + part.shape, N_DEV - 1, name=name)


def _dev_gather_wait(parts, after, *, name):
    def wait(src_ref, land_ref, ssem_ref, rsem_ref):
        x, y, c, chips = _place()
        for k, (px, py, pc) in enumerate(_dev_peers(x, y, c, chips)):
            cp = pltpu.make_async_remote_copy(
                src_ref=src_ref, dst_ref=land_ref.at[4 * px + 2 * py + pc], send_sem=ssem_ref.at[k],
                recv_sem=rsem_ref.at[k], device_id=(x, y, c), device_id_type=MESH)
            cp.wait_send()
            cp.wait_recv()

    return _wait_call(wait, parts, after, name=name)[1]


def _sibling_share_halves(arrays, *, name):
    n = len(arrays)

    def body(*refs):
        bufs = refs[n:2 * n]
        send_sems, recv_sems = refs[2 * n:]
        x, y, c, _ = _place()
        copies = []
        for a in range(n):
            mine = bufs[a].at[:, _half_cols(bufs[a].shape[1], c)]
            cp = pltpu.make_async_remote_copy(
                src_ref=mine, dst_ref=mine, send_sem=send_sems.at[a], recv_sem=recv_sems.at[a],
                device_id=(x, y, 1 - c), device_id_type=MESH)
            cp.start()
            copies.append(cp)
        for a in range(n):
            theirs = bufs[a].at[:, _half_cols(bufs[a].shape[1], 1 - c)]
            pltpu.make_async_remote_copy(
                src_ref=theirs, dst_ref=theirs, send_sem=send_sems.at[a], recv_sem=recv_sems.at[a],
                device_id=(x, y, c), device_id_type=MESH).wait_recv()
        for cp in copies:
            cp.wait_send()

    return pl.pallas_call(
        body, name=name, in_specs=[HBM] * n, out_specs=[HBM] * n,
        out_shape=[jax.ShapeDtypeStruct(h.shape, h.dtype) for h in arrays],
        input_output_aliases={a: a for a in range(n)},
        scratch_shapes=[pltpu.SemaphoreType.DMA((n,)), pltpu.SemaphoreType.DMA((n,))],
    )(*arrays)


def _pack(arrays, rows_multiple=16, width=LANES):
    flat = jnp.concatenate([a.astype(F32).reshape(-1) for a in arrays])
    total = flat.shape[0]
    rows = -(-total // width)
    rows = -(-rows // rows_multiple) * rows_multiple
    return jnp.pad(flat, (0, rows * width - total)).reshape(rows, width)


def _unpack(buf, shapes):
    flat = buf.reshape(-1)
    out, off = [], 0
    for s in shapes:
        n = math.prod(s)
        out.append(flat[off:off + n].reshape(s))
        off += n
    return out


def kernel(x, norm_pre, norm_post, gla_w_in, gla_w_gate2, gla_b_gate, gla_o_gain, gla_w_out, sgu_w_in, sgu_ln_gain, sgu_ln_bias, sgu_w_spatial, sgu_b_spatial, sgu_w_out, loss_target, m_norm_pre, m_norm_post, m_gla_w_in, m_gla_w_gate2, m_gla_b_gate, m_gla_o_gain, m_gla_w_out, m_sgu_w_in, m_sgu_ln_gain, m_sgu_ln_bias, m_sgu_w_spatial, m_sgu_b_spatial, m_sgu_w_out, v_norm_pre, v_norm_post, v_gla_w_in, v_gla_w_gate2, v_gla_b_gate, v_gla_o_gain, v_gla_w_out, v_sgu_w_in, v_sgu_ln_gain, v_sgu_ln_bias, v_sgu_w_spatial, v_sgu_b_spatial, v_sgu_w_out):
    _, t, d = x.shape
    dk = d // 2
    gla_cols = gla_w_in.shape[2] * N_CHIPS
    gla_main = gla_cols - GLA_GATE_RANK
    gla_pad = gla_main + LANES
    chip = 2 * lax.axis_index("x") + lax.axis_index("y")
    core = lax.axis_index("c")
    core_idx = core.astype(jnp.int32).reshape(1)
    others = jnp.arange(N_CHIPS - 1, dtype=jnp.int32)
    others = others + (others >= chip).astype(jnp.int32)
    slots = jnp.concatenate([chip.astype(jnp.int32).reshape(1), others, core_idx])

    x0 = x[0]
    target = loss_target[0]

    wt_in_g, mt_in_g, vt_in_g = gla_w_in[0].T, m_gla_w_in[0].T, v_gla_w_in[0].T

    small_shard = _pack([gla_w_gate2[0], sgu_ln_gain[0], sgu_ln_bias[0]], rows_multiple=8, width=2 * LANES)
    own = [small_shard, wt_in_g.astype(BF16), gla_w_out[0].astype(BF16), sgu_w_in[0].astype(BF16),
           sgu_w_out[0].astype(BF16)]
    in_flight, token = _gather_start(own, name="gather_start")

    def arrived(i, after, name):
        land = _gather_wait(in_flight[i], after, name=name + "_wait")
        land = _sibling_forward(land, name=name + "_share")
        return lax.dynamic_update_slice(land, own[i][None], (chip, 0, 0))

    h0 = _norm_pre(x0, norm_pre[0:1] + token[0:1, 0:1], name="pre0")
    g_small = arrived(0, h0, "w_small")
    g_wi_g = arrived(1, [g_small, wt_in_g, mt_in_g, vt_in_g], "w_gla_in")
    wt_g = jnp.pad(g_wi_g.reshape(gla_cols, d), ((0, gla_pad - gla_cols), (0, 0)))
    shard_shapes = [gla_w_gate2.shape[1:], sgu_ln_gain.shape[1:], sgu_ln_bias.shape[1:]]
    per_chip = [_unpack(g_small[j], shard_shapes) for j in range(N_CHIPS)]
    w2_full = jnp.concatenate([p[0] for p in per_chip], axis=1)
    ln_gain = jnp.concatenate([p[1] for p in per_chip], axis=0)[None, :]
    ln_bias = jnp.concatenate([p[2] for p in per_chip], axis=0)[None, :]
    w2p = jnp.pad(w2_full, ((0, LANES - GLA_GATE_RANK), (0, 0)))

    pos_chunk = jnp.arange(SGU_BLOCK) // CHUNK
    mask = pos_chunk[:, None] >= pos_chunk[None, :]
    ws_masked = jnp.where(mask[None], sgu_w_spatial[0], 0.0)
    ws_masked_t = ws_masked.transpose(0, 2, 1)
    bs_t = sgu_b_spatial[0].T

    proj0 = _matmul(h0, wt_g, mode="nt", out_dtype=F32, name="gla_in", tn=896)
    o0, a0, s_before, s_final = _gla_fwd(proj0, w2p, gla_b_gate, gla_o_gain, name="gla_scan")
    w_out_g = arrived(2, a0, "w_gla_out").reshape(d, d)
    y0 = _matmul(a0, w_out_g, mode="nn", out_dtype=F32, name="gla_out")
    x1, h1 = _post_then_pre(x0, y0, norm_post[0:1], norm_pre[1:2], name="post0_pre1")
    g_wi_s = arrived(3, h1, "w_sgu_in")
    proj1 = _matmul(h1, g_wi_s, mode="nn", out_dtype=F32, name="sgu_in", b_shards=True)
    a1 = _sgu_fwd(proj1, ln_gain, ln_bias, ws_masked, bs_t, name="sgu_gate")
    w_out_s = arrived(4, a1, "w_sgu_out").reshape(d, d)
    y1 = _matmul(a1, w_out_s, mode="nn", out_dtype=F32, name="sgu_out")
    loss_part, dx2, dy1, d_post1 = _loss_head(x1, y1, norm_post[1:2], target, name="loss_head")

    def behind(small, token):
        return small + token[0:1, 0:1]

    def pair_and_scatter(swap, after, name):
        grad, peer = _swap_wait(swap, after, name=name + "_swap_wait")
        pair = _pair_sum_bf16(grad, core_idx, peer, name=name + "_pair")
        return _scatter_start(pair, name=name + "_start")

    def reduced(flight, after, name):
        pair, landed = _scatter_wait(flight, after, name=name + "_wait")
        return _chip_sum(pair, landed, slots, name=name + "_sum")

    dw_out_s = _matmul(a1, dy1, mode="tn", out_dtype=F32, name="d_sgu_w_out")
    swap, tok = _swap_start(dw_out_s.reshape(N_CHIPS, d // N_CHIPS, d), name="g_sgu_out_swap")
    da1 = _matmul(dy1, w_out_s, mode="nt", out_dtype=F32, name="d_sgu_act", after=tok)
    fl_wo_s, tok = pair_and_scatter(swap, da1, "g_sgu_out")
    dproj1, d_ws, d_bs_t, d_lg, d_lb = _sgu_bwd(da1, proj1, ln_gain, behind(ln_bias, tok), ws_masked, ws_masked_t,
                                                bs_t, name="sgu_gate_bwd")
    dw_in_s = _matmul(h1, dproj1, mode="tn", out_dtype=F32, name="d_sgu_w_in", out_shards=True)
    swap, tok = _swap_start(dw_in_s, name="g_sgu_in_swap")
    dh1 = _matmul_nt_shards(dproj1, g_wi_s, out_dtype=F32, name="d_sgu_h", after=tok)
    fl_wi_s, tok = pair_and_scatter(swap, dh1, "g_sgu_in")
    dx1, dy0, d_pre1, d_post0 = _mid_bwd(dx2, dh1, x1, behind(norm_pre[1:2], tok), y0, norm_post[0:1],
                                         name="pre1_post0_bwd")
    dw_out_g = _matmul(a0, dy0, mode="tn", out_dtype=F32, name="d_gla_w_out")
    swap, tok = _swap_start(dw_out_g.reshape(N_CHIPS, d // N_CHIPS, d), name="g_gla_out_swap")
    da0 = _matmul(dy0, w_out_g, mode="nt", out_dtype=F32, name="d_gla_act", after=tok)
    fl_wo_g, tok = pair_and_scatter(swap, da0, "g_gla_out")
    dproj0, d_og, d_bg, d_w2p = _gla_bwd(da0, o0, proj0, w2p, behind(gla_b_gate, tok), gla_o_gain, s_before, s_final,
                                         name="gla_scan_bwd")
    dwt_in_g = _matmul(dproj0, h0, mode="tn", out_dtype=F32, name="d_gla_w_in", tm=896, out_rows=gla_cols)
    swap, tok = _swap_start(dwt_in_g.reshape(N_CHIPS, gla_cols // N_CHIPS, d), name="g_gla_in_swap")
    dh0 = _matmul(dproj0, wt_g, mode="nn", out_dtype=F32, name="d_gla_h", tk=gla_pad, after=tok)
    fl_wi_g, tok = pair_and_scatter(swap, dh0, "g_gla_in")
    grad_x, d_pre0 = _first_bwd(dx1, dh0, x0, behind(norm_pre[0:1], tok), name="pre0_bwd")

    small_shapes = [norm_pre.shape, norm_post.shape, gla_b_gate.shape, gla_o_gain.shape, sgu_w_spatial.shape,
                    sgu_b_spatial.shape, (1, GLA_GATE_RANK, dk), (1, d), (1, d), (1, LANES)]
    d_pre = jnp.concatenate([d_pre0, d_pre1], axis=0)
    d_post = jnp.concatenate([d_post0, d_post1], axis=0)
    d_wsp = jnp.where(mask[None], d_ws, 0.0)[None]
    small_part = _pack([d_pre, d_post, d_bg, d_og, d_wsp, d_bs_t.T[None], d_w2p[:GLA_GATE_RANK][None], d_lg, d_lb,
                        loss_part])
    small_flight, tok = _dev_gather_start(small_part, name="small_grads_start")

    def big_update(w, g, m, v, name):
        return [u[None] for u in _adamw(w[0], g, m[0], v[0], name=name)]

    r_wo_s = reduced(fl_wo_s, [grad_x, tok], "g_sgu_out")
    r_wi_s = reduced(fl_wi_s, r_wo_s, "g_sgu_in")
    r_wo_g = reduced(fl_wo_g, r_wi_s, "g_gla_out")
    g_wo_sgu, g_wi_sgu, g_wo_gla = _sibling_share_halves([r_wo_s, r_wi_s, r_wo_g], name="grads_share_a")
    u_wo_sgu = big_update(sgu_w_out, g_wo_sgu, m_sgu_w_out, v_sgu_w_out, "adamw_sgu_w_out")
    u_wi_sgu = big_update(sgu_w_in, g_wi_sgu, m_sgu_w_in, v_sgu_w_in, "adamw_sgu_w_in")
    u_wo_gla = big_update(gla_w_out, g_wo_gla, m_gla_w_out, v_gla_w_out, "adamw_gla_w_out")
    r_wi_g = reduced(fl_wi_g, [u_wo_gla[1], u_wi_sgu[1], u_wo_sgu[1]], "g_gla_in")
    gt_wi_gla, = _sibling_share_halves([r_wi_g], name="grads_share_b")
    u_wi_gla = [u.T[None] for u in _adamw(wt_in_g, gt_wi_gla, mt_in_g, vt_in_g, name="adamw_gla_w_in")]

    small_land = _dev_gather_wait(small_flight, u_wi_gla[1], name="small_grads_wait")
    small_all = lax.dynamic_update_slice(small_land, small_part[None], (2 * chip + core, 0, 0))
    small_sum = _stack_sum(small_all, name="small_sum")
    (g_pre, g_post, g_bg, g_og, g_wsp, g_bsp, g_w2_full, g_lg_full, g_lb_full, loss_vec) = _unpack(small_sum, small_shapes)
    loss = loss_vec[0, 0]
    g_w2 = lax.dynamic_slice_in_dim(g_w2_full, chip * (dk // N_CHIPS), dk // N_CHIPS, axis=2)
    g_lg = lax.dynamic_slice_in_dim(g_lg_full, chip * (d // N_CHIPS), d // N_CHIPS, axis=1)
    g_lb = lax.dynamic_slice_in_dim(g_lb_full, chip * (d // N_CHIPS), d // N_CHIPS, axis=1)

    small_w = [norm_pre, norm_post, gla_b_gate, gla_o_gain, sgu_w_spatial, sgu_b_spatial, gla_w_gate2, sgu_ln_gain,
               sgu_ln_bias]
    small_g = [g_pre, g_post, g_bg, g_og, g_wsp, g_bsp, g_w2, g_lg, g_lb]
    small_m = [m_norm_pre, m_norm_post, m_gla_b_gate, m_gla_o_gain, m_sgu_w_spatial, m_sgu_b_spatial, m_gla_w_gate2,
               m_sgu_ln_gain, m_sgu_ln_bias]
    small_v = [v_norm_pre, v_norm_post, v_gla_b_gate, v_gla_o_gain, v_sgu_w_spatial, v_sgu_b_spatial, v_gla_w_gate2,
               v_sgu_ln_gain, v_sgu_ln_bias]
    own_shapes = [w.shape for w in small_w]
    _, s_dl, s_m, s_v = _adamw(_pack(small_w), _pack(small_g), _pack(small_m), _pack(small_v), name="adamw_small")
    dl_s, m_s, v_s = _unpack(s_dl, own_shapes), _unpack(s_m, own_shapes), _unpack(s_v, own_shapes)

    def ordered(small, kind):
        pre, post, bg, og, wsp, bsp, w2, lg, lb = small
        return [pre, post, u_wi_gla[kind], w2, bg, og, u_wo_gla[kind], u_wi_sgu[kind], lg, lb, wsp, bsp, u_wo_sgu[kind]]

    return (loss, grad_x[None], *ordered(small_g, 0), *ordered(dl_s, 1), *ordered(m_s, 2), *ordered(v_s, 3))
```

```python
import functools
import math

import jax
import jax.numpy as jnp
from jax import lax
from jax.experimental import pallas as pl
from jax.experimental.pallas import tpu as pltpu

F32 = jnp.float32
BF16 = jnp.bfloat16
MESH = pl.DeviceIdType.MESH

EPS = 1e-6
CHUNK = 64
GLA_HEADS = 4
GLA_GATE_RANK = 16
GLA_TAU = 16.0
SGU_BLOCK = 128
SGU_GROUPS = 8
N_CHIPS = 4
N_DEV = 8
LANES = 128

ADAM_LR = 0.001
ADAM_B1 = 0.9
ADAM_B2 = 0.999
ADAM_EPS = 1e-08
ADAM_WD = 0.01
ADAM_STEP = 10

VMEM_LIMIT = 56 * 1024 * 1024


def _cparams(sem=None):
    return pltpu.CompilerParams(dimension_semantics=sem, vmem_limit_bytes=VMEM_LIMIT)


def _pick(n, cap, unit=LANES):
    best = None
    for t in range(unit, min(n, cap) + 1, unit):
        if n % t == 0:
            best = t
    assert best is not None, (n, cap, unit)
    return best


def _dot(a, b, dims):
    return lax.dot_general(a, b, (dims, ((), ())), preferred_element_type=F32)


def _dot_nn(a, b):
    return _dot(a, b, ((1,), (0,)))


def _dot_nt(a, b):
    return _dot(a, b, ((1,), (1,)))


def _dot_tn(a, b):
    return _dot(a, b, ((0,), (0,)))


def _matmul(a, b, *, mode, out_dtype, name, tm=1024, tn=512, tk=2048, b_shards=False, out_shards=False, after=None,
            out_rows=None):
    if mode == "tn":
        K, M = a.shape
    else:
        M, K = a.shape
    if b_shards:
        ns, br, bc = b.shape
        if mode == "nt":
            N, Kb = br, ns * bc
        else:
            Kb, N = br, ns * bc
    else:
        if mode == "nt":
            N, Kb = b.shape
        else:
            Kb, N = b.shape
    assert K == Kb, (a.shape, b.shape, mode)
    tm = _pick(M, tm)
    tk = _pick(K, tk)
    if b_shards and mode != "nt":
        tn = _pick(bc, tn)
    elif out_shards:
        tn = _pick(N // N_CHIPS, tn)
    else:
        tn = _pick(N, tn)
    if b_shards and mode == "nt":
        tk = _pick(bc, tk)
    nk = K // tk
    grid = (M // tm, N // tn, nk)

    if mode == "tn":
        a_spec = pl.BlockSpec((tk, tm), lambda i, j, k: (k, i))
    else:
        a_spec = pl.BlockSpec((tm, tk), lambda i, j, k: (i, k))
    if b_shards:
        if mode == "nt":
            per = bc // tk
            b_spec = pl.BlockSpec((None, tn, tk), lambda i, j, k: (k // per, j, k % per))
        else:
            per = bc // tn
            b_spec = pl.BlockSpec((None, tk, tn), lambda i, j, k: (j // per, k, j % per))
    elif mode == "nt":
        b_spec = pl.BlockSpec((tn, tk), lambda i, j, k: (j, k))
    else:
        b_spec = pl.BlockSpec((tk, tn), lambda i, j, k: (k, j))
    if out_shards:
        per_o = (N // N_CHIPS) // tn
        out_spec = pl.BlockSpec((None, tm, tn), lambda i, j, k: (j // per_o, i, j % per_o))
        out_shape = jax.ShapeDtypeStruct((N_CHIPS, M, N // N_CHIPS), out_dtype)
    else:
        out_spec = pl.BlockSpec((tm, tn), lambda i, j, k: (i, j))
        out_shape = jax.ShapeDtypeStruct((M if out_rows is None else out_rows, N), out_dtype)

    dims = {"nn": ((1,), (0,)), "nt": ((1,), (1,)), "tn": ((0,), (0,))}[mode]

    def body(a_ref, b_ref, *rest):
        o_ref, scratch = (rest[1], rest[2:]) if after is not None else (rest[0], rest[1:])
        part = _dot(a_ref[...].astype(BF16), b_ref[...].astype(BF16), dims)
        if nk == 1:
            o_ref[...] = part.astype(out_dtype)
        else:
            acc_ref, = scratch
            k = pl.program_id(2)

            @pl.when(k == 0)
            def _():
                acc_ref[...] = part

            @pl.when(k > 0)
            def _():
                acc_ref[...] += part

            @pl.when(k == nk - 1)
            def _():
                o_ref[...] = acc_ref[...].astype(out_dtype)

    extra_specs, extra_args = ([], []) if after is None else ([pl.BlockSpec(memory_space=pl.ANY)], [after])
    return pl.pallas_call(
        body, name=name, grid=grid, in_specs=[a_spec, b_spec] + extra_specs, out_specs=out_spec, out_shape=out_shape,
        scratch_shapes=[] if nk == 1 else [pltpu.VMEM((tm, tn), F32)],
        compiler_params=_cparams(("parallel", "parallel", "arbitrary")),
    )(a, b, *extra_args)


def _matmul_nt_shards(a, b, *, out_dtype, name, tm=1024, tn=512, after=None):
    M, K = a.shape
    ns, N, kc = b.shape
    assert K == ns * kc
    tm, tn = _pick(M, tm), _pick(N, tn)

    def body(a_ref, *rest):
        b_refs, o_ref = rest[:ns], rest[ns + (after is not None)]
        acc = _dot_nt(a_ref[:, 0:kc], b_refs[0][...])
        for j in range(1, ns):
            acc += _dot_nt(a_ref[:, j * kc:(j + 1) * kc], b_refs[j][...])
        o_ref[...] = acc.astype(out_dtype)

    def shard(j):
        return pl.BlockSpec((None, tn, kc), lambda i, n: (j, n, 0))

    extra_specs, extra_args = ([], []) if after is None else ([pl.BlockSpec(memory_space=pl.ANY)], [after])
    return pl.pallas_call(
        body, name=name, grid=(M // tm, N // tn),
        in_specs=[pl.BlockSpec((tm, K), lambda i, n: (i, 0))] + [shard(j) for j in range(ns)] + extra_specs,
        out_specs=pl.BlockSpec((tm, tn), lambda i, n: (i, n)), out_shape=jax.ShapeDtypeStruct((M, N), out_dtype),
        compiler_params=_cparams(("parallel", "parallel")),
    )(a, *([b] * ns), *extra_args)


def _rstd(x):
    return lax.rsqrt(jnp.mean(x * x, axis=-1, keepdims=True) + EPS)


def _row_spec(tr, d):
    return pl.BlockSpec((tr, d), lambda i: (i, 0))


def _vec_spec(d):
    return pl.BlockSpec((1, d), lambda i: (0, 0))


def _acc_rows(ref, i, val, cols=slice(None)):
    @pl.when(i == 0)
    def _():
        ref[:, cols] = val

    @pl.when(i > 0)
    def _():
        ref[:, cols] += val


def _norm_pre(x, gain, *, name, tr=256):
    t, d = x.shape
    tr = _pick(t, tr, 8)

    def body(x_ref, g_ref, h_ref):
        xv = x_ref[...]
        h_ref[...] = (xv * _rstd(xv) * g_ref[...]).astype(BF16)

    return pl.pallas_call(
        body, name=name, grid=(t // tr,), in_specs=[_row_spec(tr, d), _vec_spec(d)], out_specs=_row_spec(tr, d),
        out_shape=jax.ShapeDtypeStruct((t, d), BF16), compiler_params=_cparams(("parallel",)),
    )(x, gain)


def _post_then_pre(x, y, post_gain, pre_gain, *, name, tr=256):
    t, d = x.shape
    tr = _pick(t, tr, 8)

    def body(x_ref, y_ref, pg_ref, ng_ref, xn_ref, h_ref):
        yv = y_ref[...]
        xn = x_ref[...] + yv * _rstd(yv) * pg_ref[...]
        xn_ref[...] = xn
        h_ref[...] = (xn * _rstd(xn) * ng_ref[...]).astype(BF16)

    return pl.pallas_call(
        body, name=name, grid=(t // tr,),
        in_specs=[_row_spec(tr, d), _row_spec(tr, d), _vec_spec(d), _vec_spec(d)],
        out_specs=[_row_spec(tr, d), _row_spec(tr, d)],
        out_shape=[jax.ShapeDtypeStruct((t, d), F32), jax.ShapeDtypeStruct((t, d), BF16)],
        compiler_params=_cparams(("parallel",)),
    )(x, y, post_gain, pre_gain)


def _norm_bwd(dy, n, r, gain):
    dn = dy * gain
    return r * (dn - n * jnp.mean(dn * n, axis=-1, keepdims=True))


def _loss_head(x, y, post_gain, target, *, name, tr=256):
    t, d = x.shape
    tr = _pick(t, tr, 8)

    def body(x_ref, y_ref, pg_ref, t_ref, loss_ref, dx_ref, dy_ref, dpg_ref):
        i = pl.program_id(0)
        yv = y_ref[...]
        r = _rstd(yv)
        n = yv * r
        err = x_ref[...] + n * pg_ref[...] - t_ref[...]
        dx = err * (1.0 / d)
        dx_ref[...] = dx
        part = 0.5 * jnp.sum(jnp.mean(err * err, axis=-1, keepdims=True), axis=0, keepdims=True)
        _acc_rows(loss_ref, i, jnp.broadcast_to(part, (1, LANES)))
        _acc_rows(dpg_ref, i, jnp.sum(dx * n, axis=0, keepdims=True))
        dy_ref[...] = _norm_bwd(dx, n, r, pg_ref[...]).astype(BF16)

    return pl.pallas_call(
        body, name=name, grid=(t // tr,),
        in_specs=[_row_spec(tr, d), _row_spec(tr, d), _vec_spec(d), _row_spec(tr, d)],
        out_specs=[_vec_spec(LANES), _row_spec(tr, d), _row_spec(tr, d), _vec_spec(d)],
        out_shape=[jax.ShapeDtypeStruct((1, LANES), F32), jax.ShapeDtypeStruct((t, d), F32),
                   jax.ShapeDtypeStruct((t, d), BF16), jax.ShapeDtypeStruct((1, d), F32)],
        compiler_params=_cparams(("arbitrary",)),
    )(x, y, post_gain, target)


def _mid_bwd(dx_out, dh, x, pre_gain, y_prev, post_gain_prev, *, name, tr=256):
    t, d = x.shape
    tr = _pick(t, tr, 8)

    def body(dxo_ref, dh_ref, x_ref, ng_ref, y_ref, pg_ref, dx_ref, dy_ref, dng_ref, dpg_ref):
        i = pl.program_id(0)
        xv = x_ref[...]
        r = _rstd(xv)
        xh = xv * r
        dhv = dh_ref[...]
        _acc_rows(dng_ref, i, jnp.sum(dhv * xh, axis=0, keepdims=True))
        dx = dxo_ref[...] + _norm_bwd(dhv, xh, r, ng_ref[...])
        dx_ref[...] = dx
        yv = y_ref[...]
        ry = _rstd(yv)
        n = yv * ry
        _acc_rows(dpg_ref, i, jnp.sum(dx * n, axis=0, keepdims=True))
        dy_ref[...] = _norm_bwd(dx, n, ry, pg_ref[...]).astype(BF16)

    return pl.pallas_call(
        body, name=name, grid=(t // tr,),
        in_specs=[_row_spec(tr, d), _row_spec(tr, d), _row_spec(tr, d), _vec_spec(d), _row_spec(tr, d), _vec_spec(d)],
        out_specs=[_row_spec(tr, d), _row_spec(tr, d), _vec_spec(d), _vec_spec(d)],
        out_shape=[jax.ShapeDtypeStruct((t, d), F32), jax.ShapeDtypeStruct((t, d), BF16),
                   jax.ShapeDtypeStruct((1, d), F32), jax.ShapeDtypeStruct((1, d), F32)],
        compiler_params=_cparams(("arbitrary",)),
    )(dx_out, dh, x, pre_gain, y_prev, post_gain_prev)


def _first_bwd(dx_out, dh, x, pre_gain, *, name, tr=256):
    t, d = x.shape
    tr = _pick(t, tr, 8)

    def body(dxo_ref, dh_ref, x_ref, ng_ref, dx_ref, dng_ref):
        i = pl.program_id(0)
        xv = x_ref[...]
        r = _rstd(xv)
        xh = xv * r
        dhv = dh_ref[...]
        _acc_rows(dng_ref, i, jnp.sum(dhv * xh, axis=0, keepdims=True))
        dx_ref[...] = dxo_ref[...] + _norm_bwd(dhv, xh, r, ng_ref[...])

    return pl.pallas_call(
        body, name=name, grid=(t // tr,),
        in_specs=[_row_spec(tr, d), _row_spec(tr, d), _row_spec(tr, d), _vec_spec(d)],
        out_specs=[_row_spec(tr, d), _vec_spec(d)],
        out_shape=[jax.ShapeDtypeStruct((t, d), F32), jax.ShapeDtypeStruct((1, d), F32)],
        compiler_params=_cparams(("arbitrary",)),
    )(dx_out, dh, x, pre_gain)


def _sigmoid(x):
    return 1.0 / (1.0 + jnp.exp(-x))


def _log_sigmoid(x):
    return jnp.minimum(x, 0.0) - jnp.log(1.0 + jnp.exp(-jnp.abs(x)))


_GELU_C = math.sqrt(2.0 / math.pi)


def _gelu_parts(x):
    x2 = x * x
    th = jnp.tanh(_GELU_C * (x + 0.044715 * x * x2))
    val = 0.5 * x * (1.0 + th)
    grad = 0.5 * (1.0 + th) + 0.5 * x * (1.0 - th * th) * (_GELU_C * (1.0 + 3.0 * 0.044715 * x2))
    return val, grad


def _split3(x):
    hi = x.astype(BF16)
    r1 = x - hi.astype(F32)
    mid = r1.astype(BF16)
    lo = (r1 - mid.astype(F32)).astype(BF16)
    return hi, mid, lo


def _tri_matmul(tri_bf16, x):
    hi, mid, lo = _split3(x)
    return _dot_nn(tri_bf16, hi) + _dot_nn(tri_bf16, mid) + _dot_nn(tri_bf16, lo)


def _gla_dims(d):
    dk, dv = d // 2, d
    return dk, dv, dk // GLA_HEADS, dv // GLA_HEADS


def _col_pieces(a, b, lay):
    ws, wp = lay
    out = []
    while a < b:
        j = a // ws
        end = min(b, (j + 1) * ws)
        out.append((j * wp + a - j * ws, end - a))
        a = end
    return out


def _load_cols(ref, a, b, lay):
    parts = [ref[:, s:s + n] for s, n in _col_pieces(a, b, lay)]
    return parts[0] if len(parts) == 1 else jnp.concatenate(parts, axis=1)


def _store_cols(ref, a, val, lay):
    off = 0
    for s, n in _col_pieces(a, a + val.shape[1], lay):
        ref[:, s:s + n] = val[:, off:off + n]
        off += n


def _gate_window(c_r, lay):
    (start, _), = _col_pieces(c_r, c_r + GLA_GATE_RANK, lay)
    assert (start % lay[1]) + LANES <= lay[1]
    return slice(start, start + LANES)


def _gla_gates(glr, k, w2_ref, b_ref):
    z = _dot_nn(glr.astype(BF16), w2_ref[...].astype(BF16)) + b_ref[...]
    la = _log_sigmoid(z) * (1.0 / GLA_TAU)
    row = lax.broadcasted_iota(jnp.int32, (CHUNK, CHUNK), 0)
    col = lax.broadcasted_iota(jnp.int32, (CHUNK, CHUNK), 1)
    incl = (row >= col).astype(BF16)
    bcum = _tri_matmul(incl, la)
    b_end = bcum[CHUNK - 1:CHUNK, :]
    e_rest = jnp.exp(b_end - bcum)
    return z, e_rest, k * e_rest, jnp.exp(b_end)


def _gla_fwd(proj, w2p, b_gate, o_gain, lay, *, name):
    t, wcols = proj.shape
    d = o_gain.shape[1]
    dk, dv, dkh, dvh = _gla_dims(d)
    nc = t // CHUNK
    c_k, c_v, c_g, c_r = dk, 2 * dk, 2 * dk + dv, 2 * dk + 2 * dv
    scale = dkh ** -0.5

    def body(p_ref, w2_ref, b_ref, og_ref, o_ref, a_ref, sb_ref, sfin_ref, s_ref):
        i = pl.program_id(0)

        @pl.when(i == 0)
        def _():
            s_ref[...] = jnp.zeros_like(s_ref)

        q = _load_cols(p_ref, 0, dk, lay) * scale
        k = _load_cols(p_ref, c_k, c_k + dk, lay)
        glr = p_ref[:, _gate_window(c_r, lay)]
        _, _, kdec, decay = _gla_gates(glr, k, w2_ref, b_ref)
        for h in range(GLA_HEADS):
            ks = slice(h * dkh, (h + 1) * dkh)
            vs = slice(h * dvh, (h + 1) * dvh)
            v_h = _load_cols(p_ref, c_v + h * dvh, c_v + (h + 1) * dvh, lay)
            g_h = _load_cols(p_ref, c_g + h * dvh, c_g + (h + 1) * dvh, lay)
            s_old = s_ref[h]
            sb_ref[0, h] = s_old
            s_new = s_old * decay[:, ks] + _dot_tn(v_h.astype(BF16), kdec[:, ks].astype(BF16))
            s_ref[h] = s_new
            o_h = _dot_nt(q[:, ks].astype(BF16), s_new.astype(BF16))
            o_ref[:, vs] = o_h
            on = o_h * _rstd(o_h)
            a_ref[:, vs] = (on * og_ref[:, vs] * (g_h * _sigmoid(g_h))).astype(BF16)

        @pl.when(i == nc - 1)
        def _():
            sfin_ref[...] = s_ref[...]

    full = lambda *shape: pl.BlockSpec(shape, lambda i: (0,) * len(shape))
    return pl.pallas_call(
        body, name=name, grid=(nc,),
        in_specs=[pl.BlockSpec((CHUNK, wcols), lambda i: (i, 0)), full(LANES, dk), full(1, dk), full(1, dv)],
        out_specs=[pl.BlockSpec((CHUNK, dv), lambda i: (i, 0)), pl.BlockSpec((CHUNK, dv), lambda i: (i, 0)),
                   pl.BlockSpec((1, GLA_HEADS, dvh, dkh), lambda i: (i, 0, 0, 0)), full(GLA_HEADS, dvh, dkh)],
        out_shape=[jax.ShapeDtypeStruct((t, dv), F32), jax.ShapeDtypeStruct((t, dv), BF16),
                   jax.ShapeDtypeStruct((nc, GLA_HEADS, dvh, dkh), F32),
                   jax.ShapeDtypeStruct((GLA_HEADS, dvh, dkh), F32)],
        scratch_shapes=[pltpu.VMEM((GLA_HEADS, dvh, dkh), F32)],
        compiler_params=_cparams(("arbitrary",)),
    )(proj, w2p, b_gate, o_gain)


def _gla_bwd(da, o, proj, w2p, b_gate, o_gain, s_before, s_final, lay, *, name):
    t, wcols = proj.shape
    d = o_gain.shape[1]
    dk, dv, dkh, dvh = _gla_dims(d)
    nc = t // CHUNK
    c_k, c_v, c_g, c_r = dk, 2 * dk, 2 * dk + dv, 2 * dk + 2 * dv
    scale = dkh ** -0.5

    def body(da_ref, o_ref, p_ref, w2_ref, b_ref, og_ref, sb_ref, sfin_ref,
             dp_ref, dog_ref, db_ref, dw2_ref, s_ref, gc_ref, dkd_ref):
        i = pl.program_id(0)

        @pl.when(i == 0)
        def _():
            s_ref[...] = sfin_ref[...]
            gc_ref[...] = jnp.zeros_like(gc_ref)

        ws, wp = lay
        for j in range(N_CHIPS):
            dp_ref[:, j * wp + ws:(j + 1) * wp] = jnp.zeros((CHUNK, wp - ws), BF16)
        q = _load_cols(p_ref, 0, dk, lay) * scale
        k = _load_cols(p_ref, c_k, c_k + dk, lay)
        glr = p_ref[:, _gate_window(c_r, lay)]
        z, e_rest, kdec, decay = _gla_gates(glr, k, w2_ref, b_ref)
        ddecay = []
        for h in range(GLA_HEADS):
            ks = slice(h * dkh, (h + 1) * dkh)
            vs = slice(h * dvh, (h + 1) * dvh)
            v_h = _load_cols(p_ref, c_v + h * dvh, c_v + (h + 1) * dvh, lay)
            g_h = _load_cols(p_ref, c_g + h * dvh, c_g + (h + 1) * dvh, lay)
            da_h = da_ref[:, vs]
            o_h = o_ref[:, vs]
            og_h = og_ref[:, vs]
            r = _rstd(o_h)
            on = o_h * r
            sg = _sigmoid(g_h)
            silu = g_h * sg
            _acc_rows(dog_ref, i, jnp.sum(da_h * silu * on, axis=0, keepdims=True), vs)
            _store_cols(dp_ref, c_g + h * dvh, (da_h * (on * og_h) * (sg * (1.0 + g_h * (1.0 - sg)))).astype(BF16),
                        lay)
            don = da_h * silu * og_h
            do_h = (r * (don - on * jnp.mean(don * on, axis=-1, keepdims=True))).astype(BF16)
            s_cur = s_ref[h]
            _store_cols(dp_ref, h * dkh, (_dot_nn(do_h, s_cur.astype(BF16)) * scale).astype(BF16), lay)
            g_tot = gc_ref[h] + _dot_tn(do_h, q[:, ks].astype(BF16))
            g_bf = g_tot.astype(BF16)
            dkd_ref[:, ks] = _dot_nn(v_h.astype(BF16), g_bf)
            _store_cols(dp_ref, c_v + h * dvh, _dot_nt(kdec[:, ks].astype(BF16), g_bf).astype(BF16), lay)
            s_prev = sb_ref[0, h]
            ddecay.append(jnp.sum(g_tot * s_prev, axis=0, keepdims=True))
            gc_ref[h] = g_tot * decay[:, ks]
            s_ref[h] = s_prev
        dkdec = dkd_ref[...]
        _store_cols(dp_ref, c_k, (dkdec * e_rest).astype(BF16), lay)
        d_e = dkdec * kdec
        row = lax.broadcasted_iota(jnp.int32, (CHUNK, CHUNK), 0)
        col = lax.broadcasted_iota(jnp.int32, (CHUNK, CHUNK), 1)
        excl = (row > col).astype(BF16)
        dla = jnp.concatenate(ddecay, axis=1) * decay + _tri_matmul(excl, d_e)
        dz = dla * (1.0 / GLA_TAU) * (1.0 - _sigmoid(z))
        _acc_rows(db_ref, i, jnp.sum(dz, axis=0, keepdims=True))
        dz_bf = dz.astype(BF16)
        dw2 = _dot_tn(glr.astype(BF16), dz_bf)

        @pl.when(i == 0)
        def _():
            dw2_ref[...] = dw2

        @pl.when(i > 0)
        def _():
            dw2_ref[...] += dw2

        dp_ref[:, _gate_window(c_r, lay)] = _dot_nt(dz_bf, w2_ref[...].astype(BF16)).astype(BF16)

    rev = lambda i: (nc - 1 - i, 0)
    full = lambda *shape: pl.BlockSpec(shape, lambda i: (0,) * len(shape))
    return pl.pallas_call(
        body, name=name, grid=(nc,),
        in_specs=[pl.BlockSpec((CHUNK, dv), rev), pl.BlockSpec((CHUNK, dv), rev), pl.BlockSpec((CHUNK, wcols), rev),
                  full(LANES, dk), full(1, dk), full(1, dv),
                  pl.BlockSpec((1, GLA_HEADS, dvh, dkh), lambda i: (nc - 1 - i, 0, 0, 0)), full(GLA_HEADS, dvh, dkh)],
        out_specs=[pl.BlockSpec((CHUNK, wcols), rev), full(1, dv), full(1, dk), full(LANES, dk)],
        out_shape=[jax.ShapeDtypeStruct((t, wcols), BF16), jax.ShapeDtypeStruct((1, dv), F32),
                   jax.ShapeDtypeStruct((1, dk), F32), jax.ShapeDtypeStruct((LANES, dk), F32)],
        scratch_shapes=[pltpu.VMEM((GLA_HEADS, dvh, dkh), F32), pltpu.VMEM((GLA_HEADS, dvh, dkh), F32),
                        pltpu.VMEM((CHUNK, dk), F32)],
        compiler_params=_cparams(("arbitrary",)),
    )(da, o, proj, w2p, b_gate, o_gain, s_before, s_final)


def _sgu_mid(p_ref, lg_ref, lb_ref, ws_ref, bst_ref, w):
    gd = w // SGU_GROUPS
    u_act, du_fac = _gelu_parts(p_ref[:, 0:w])
    vf, dv_fac = _gelu_parts(p_ref[:, w:2 * w])
    mu = jnp.mean(vf, axis=-1, keepdims=True)
    cen = vf - mu
    rstd = lax.rsqrt(jnp.mean(cen * cen, axis=-1, keepdims=True) + EPS)
    xh = cen * rstd
    vn = (xh * lg_ref[...] + lb_ref[...]).astype(BF16)
    vs = [_dot_nn(ws_ref[g].astype(BF16), vn[:, g * gd:(g + 1) * gd]) + bst_ref[:, g:g + 1]
          for g in range(SGU_GROUPS)]
    return u_act, du_fac, dv_fac, rstd, xh, vn, vs


def _sgu_fwd(proj, ln_gain, ln_bias, ws_masked, bs_t, *, name):
    t, w3 = proj.shape
    w = w3 // 3
    gd = w // SGU_GROUPS
    nb = t // SGU_BLOCK

    def body(p_ref, lg_ref, lb_ref, ws_ref, bst_ref, a_ref):
        u_act, _, _, _, _, _, vs = _sgu_mid(p_ref, lg_ref, lb_ref, ws_ref, bst_ref, w)
        for g in range(SGU_GROUPS):
            cs = slice(g * gd, (g + 1) * gd)
            gate = p_ref[:, 2 * w + g * gd:2 * w + (g + 1) * gd]
            a_ref[:, cs] = (u_act[:, cs] * vs[g] * (gate * _sigmoid(gate))).astype(BF16)

    full = lambda *shape: pl.BlockSpec(shape, lambda i: (0,) * len(shape))
    return pl.pallas_call(
        body, name=name, grid=(nb,),
        in_specs=[pl.BlockSpec((SGU_BLOCK, w3), lambda i: (i, 0)), full(1, w), full(1, w),
                  full(SGU_GROUPS, SGU_BLOCK, SGU_BLOCK), full(SGU_BLOCK, SGU_GROUPS)],
        out_specs=pl.BlockSpec((SGU_BLOCK, w), lambda i: (i, 0)),
        out_shape=jax.ShapeDtypeStruct((t, w), BF16),
        compiler_params=_cparams(("parallel",)),
    )(proj, ln_gain, ln_bias, ws_masked, bs_t)


def _sgu_bwd(da, proj, ln_gain, ln_bias, ws_masked, ws_masked_t, bs_t, *, name):
    t, w3 = proj.shape
    w = w3 // 3
    gd = w // SGU_GROUPS
    nb = t // SGU_BLOCK

    def body(da_ref, p_ref, lg_ref, lb_ref, ws_ref, wst_ref, bst_ref, dp_ref, dws_ref, dbst_ref, dlg_ref, dlb_ref,
             dvn_ref):
        i = pl.program_id(0)
        u_act, du_fac, dv_fac, rstd, xh, vn, vs = _sgu_mid(p_ref, lg_ref, lb_ref, ws_ref, bst_ref, w)
        for g in range(SGU_GROUPS):
            cs = slice(g * gd, (g + 1) * gd)
            gate = p_ref[:, 2 * w + g * gd:2 * w + (g + 1) * gd]
            sg = _sigmoid(gate)
            silu = gate * sg
            da_g = da_ref[:, cs]
            ua_g = u_act[:, cs]
            dp_ref[:, cs] = (da_g * vs[g] * silu * du_fac[:, cs]).astype(BF16)
            dp_ref[:, 2 * w + g * gd:2 * w + (g + 1) * gd] = (
                da_g * ua_g * vs[g] * (sg * (1.0 + gate * (1.0 - sg)))).astype(BF16)
            dvs = da_g * ua_g * silu
            dvs_bf = dvs.astype(BF16)
            dvn_ref[:, cs] = _dot_nn(wst_ref[g].astype(BF16), dvs_bf)
            dws = _dot_nt(dvs_bf, vn[:, cs])
            dbs = jnp.sum(dvs, axis=1, keepdims=True)

            @pl.when(i == 0)
            def _():
                dws_ref[g] = dws
                dbst_ref[:, g:g + 1] = dbs

            @pl.when(i > 0)
            def _():
                dws_ref[g] += dws
                dbst_ref[:, g:g + 1] += dbs

        dvn = dvn_ref[...]
        _acc_rows(dlg_ref, i, jnp.sum(dvn * xh, axis=0, keepdims=True))
        _acc_rows(dlb_ref, i, jnp.sum(dvn, axis=0, keepdims=True))
        dxh = dvn * lg_ref[...]
        dvf = rstd * (dxh - jnp.mean(dxh, axis=-1, keepdims=True)
                      - xh * jnp.mean(dxh * xh, axis=-1, keepdims=True))
        dp_ref[:, w:2 * w] = (dvf * dv_fac).astype(BF16)

    full = lambda *shape: pl.BlockSpec(shape, lambda i: (0,) * len(shape))
    return pl.pallas_call(
        body, name=name, grid=(nb,),
        in_specs=[pl.BlockSpec((SGU_BLOCK, w), lambda i: (i, 0)), pl.BlockSpec((SGU_BLOCK, w3), lambda i: (i, 0)),
                  full(1, w), full(1, w), full(SGU_GROUPS, SGU_BLOCK, SGU_BLOCK),
                  full(SGU_GROUPS, SGU_BLOCK, SGU_BLOCK), full(SGU_BLOCK, SGU_GROUPS)],
        out_specs=[pl.BlockSpec((SGU_BLOCK, w3), lambda i: (i, 0)), full(SGU_GROUPS, SGU_BLOCK, SGU_BLOCK),
                   full(SGU_BLOCK, SGU_GROUPS), full(1, w), full(1, w)],
        out_shape=[jax.ShapeDtypeStruct((t, w3), BF16), jax.ShapeDtypeStruct((SGU_GROUPS, SGU_BLOCK, SGU_BLOCK), F32),
                   jax.ShapeDtypeStruct((SGU_BLOCK, SGU_GROUPS), F32), jax.ShapeDtypeStruct((1, w), F32),
                   jax.ShapeDtypeStruct((1, w), F32)],
        scratch_shapes=[pltpu.VMEM((SGU_BLOCK, w), F32)],
        compiler_params=_cparams(("arbitrary",)),
    )(da, proj, ln_gain, ln_bias, ws_masked, ws_masked_t, bs_t)


def _tile2d(rows, cols, block_bytes, row_unit):
    if rows % row_unit == 0:
        return _pick(rows, max(row_unit, block_bytes // (4 * cols)), row_unit), cols
    return rows, _pick(cols, max(LANES, block_bytes // (4 * rows)))


def _adamw(w, g, m, v, *, name, block_bytes=1 << 20):
    rows, cols = w.shape
    tr, tc = _tile2d(rows, cols, block_bytes, 8)

    def body(w_ref, g_ref, m_ref, v_ref, go_ref, d_ref, mo_ref, vo_ref):
        gv = g_ref[...]
        go_ref[...] = gv
        mn = ADAM_B1 * m_ref[...] + (1.0 - ADAM_B1) * gv
        vn = ADAM_B2 * v_ref[...] + (1.0 - ADAM_B2) * (gv * gv)
        m_hat = mn / (1.0 - ADAM_B1 ** ADAM_STEP)
        v_hat = vn / (1.0 - ADAM_B2 ** ADAM_STEP)
        d_ref[...] = -ADAM_LR * (m_hat / (jnp.sqrt(v_hat) + ADAM_EPS) + ADAM_WD * w_ref[...])
        mo_ref[...] = mn
        vo_ref[...] = vn

    spec = pl.BlockSpec((tr, tc), lambda i, j: (i, j))
    return pl.pallas_call(
        body, name=name, grid=(rows // tr, cols // tc), in_specs=[spec] * 4, out_specs=[spec] * 4,
        out_shape=[jax.ShapeDtypeStruct((rows, cols), F32)] * 4,
        compiler_params=_cparams(("parallel", "parallel")),
    )(w, g, m, v)


def _pair_sum_bf16(own, core_idx, peer, *, name, block_bytes=1 << 20):
    s, r, c = own.shape
    hc = c // 2
    tr, tc = _tile2d(r, hc, block_bytes, 16)
    ncb = hc // tc

    def body(h_ref, a_ref, b_ref, o_ref):
        o_ref[...] = (a_ref[...] + b_ref[...]).astype(BF16)

    grid_spec = pltpu.PrefetchScalarGridSpec(
        num_scalar_prefetch=1, grid=(s, r // tr, ncb),
        in_specs=[pl.BlockSpec((None, tr, tc), lambda j, i, k, h: (j, i, h[0] * ncb + k)),
                  pl.BlockSpec((None, tr, tc), lambda j, i, k, h: (j, i, k))],
        out_specs=pl.BlockSpec((None, tr, tc), lambda j, i, k, h: (j, i, k)))
    return pl.pallas_call(
        body, name=name, grid_spec=grid_spec, out_shape=jax.ShapeDtypeStruct((s, r, hc), BF16),
        compiler_params=_cparams(("parallel", "parallel", "parallel")),
    )(core_idx, own, peer)


def _chip_sum(pair, landed, slots, *, name, block_bytes=1 << 20):
    _, r, hc = pair.shape
    tr, tc = _tile2d(r, hc, block_bytes, 16)
    ncb = hc // tc

    def body(s_ref, own_ref, l0_ref, l1_ref, l2_ref, o_ref):
        o_ref[...] = ((own_ref[...].astype(F32) + l0_ref[...].astype(F32)) + l1_ref[...].astype(F32)
                      ) + l2_ref[...].astype(F32)

    def slab(which):
        return pl.BlockSpec((None, tr, tc), lambda i, k, s: (s[which], i, k))

    grid_spec = pltpu.PrefetchScalarGridSpec(
        num_scalar_prefetch=1, grid=(r // tr, ncb),
        in_specs=[slab(0), slab(1), slab(2), slab(3)],
        out_specs=pl.BlockSpec((tr, tc), lambda i, k, s: (i, s[4] * ncb + k)))
    return pl.pallas_call(
        body, name=name, grid_spec=grid_spec, out_shape=jax.ShapeDtypeStruct((r, 2 * hc), F32),
        compiler_params=_cparams(("parallel", "parallel")),
    )(slots, pair, landed, landed, landed)


def _stack_sum(x, *, name, out_dtype=F32, block_bytes=1 << 20):
    s, r, c = x.shape
    tr = _pick(r, max(8, block_bytes // (4 * c)), 16) if r % 16 == 0 else r

    def body(x_ref, o_ref):
        acc = x_ref[0].astype(F32)
        for j in range(1, s):
            acc = acc + x_ref[j].astype(F32)
        o_ref[...] = acc.astype(out_dtype)

    return pl.pallas_call(
        body, name=name, grid=(r // tr,),
        in_specs=[pl.BlockSpec((s, tr, c), lambda i: (0, i, 0))], out_specs=pl.BlockSpec((tr, c), lambda i: (i, 0)),
        out_shape=jax.ShapeDtypeStruct((r, c), out_dtype), compiler_params=_cparams(("parallel",)),
    )(x)


HBM = pl.BlockSpec(memory_space=pltpu.HBM)


def _place():
    x, y, c = lax.axis_index("x"), lax.axis_index("y"), lax.axis_index("c")
    other_chips = [(1 - x, y), (x, 1 - y), (1 - x, 1 - y)]
    return x, y, c, other_chips


def _half_cols(cols, which):
    hc = cols // 2
    return pl.ds(pl.multiple_of(which * hc, LANES), hc)


SEM = pl.BlockSpec(memory_space=pltpu.SEMAPHORE)
ANY = pl.BlockSpec(memory_space=pl.ANY)
SIDE_EFFECT = pltpu.SideEffectType.DATAFLOW_SIDE_EFFECTING
TOKEN_SHAPE = (8, LANES)


def _hbm(shape, dtype):
    return pltpu.HBM(shape, dtype)


def _in_hbm(a):
    return pltpu.with_memory_space_constraint(a, pltpu.HBM)


def _gather_copy(src_ref, land_ref, ssem, rsem, k, chip_of_block, to, c):
    cols = src_ref.shape[1]
    return pltpu.make_async_remote_copy(
        src_ref=src_ref.at[:, _half_cols(cols, c)], dst_ref=land_ref.at[chip_of_block, :, _half_cols(cols, c)],
        send_sem=ssem.at[k], recv_sem=rsem.at[k], device_id=to, device_id_type=MESH)


def _gather_start(shards, *, name):
    n = len(shards)

    def body(*refs):
        srcs, lands = refs[:n], refs[n:2 * n]
        outs = refs[2 * n:]
        token = outs[-1]
        x, y, c, chips = _place()
        me = 2 * x + y
        for a in range(n):
            ssem, rsem = outs[4 * a], outs[4 * a + 1]
            for k, (cx, cy) in enumerate(chips):
                _gather_copy(srcs[a], lands[a], ssem, rsem, k, me, (cx, cy, c), c).start()
        token[...] = jnp.zeros_like(token)

    out_shape, out_specs, aliases = [], [], {}
    for a, s in enumerate(shards):
        out_shape += [pltpu.SemaphoreType.DMA((3,)), pltpu.SemaphoreType.DMA((3,)), _hbm(s.shape, s.dtype),
                      _hbm((N_CHIPS,) + s.shape, s.dtype)]
        out_specs += [SEM, SEM, HBM, HBM]
        aliases[a] = 4 * a + 2
        aliases[n + a] = 4 * a + 3
    out_shape.append(jax.ShapeDtypeStruct(TOKEN_SHAPE, F32))
    out_specs.append(pl.BlockSpec(memory_space=pltpu.VMEM))
    lands = [_in_hbm(lax.empty((N_CHIPS,) + s.shape, s.dtype)) for s in shards]
    res = pl.pallas_call(
        body, name=name, in_specs=[HBM] * (2 * n), out_specs=out_specs, out_shape=out_shape,
        input_output_aliases=aliases, compiler_params=pltpu.CompilerParams(has_side_effects=SIDE_EFFECT),
    )(*[_in_hbm(s) for s in shards], *lands)
    return [tuple(res[4 * a:4 * a + 4]) for a in range(n)], res[-1]


def _wait_call(wait_fn, parts, after, *, name):
    ssem, rsem, src, land = parts
    after = list(after) if isinstance(after, (list, tuple)) else [after]

    def body(src_ref, land_ref, ssem_ref, rsem_ref, *rest):
        wait_fn(src_ref, land_ref, ssem_ref, rsem_ref)

    return pl.pallas_call(
        body, name=name, in_specs=[HBM, HBM, SEM, SEM] + [ANY] * len(after), out_specs=[HBM, HBM],
        out_shape=[_hbm(src.shape, src.dtype), _hbm(land.shape, land.dtype)], input_output_aliases={0: 0, 1: 1},
        compiler_params=pltpu.CompilerParams(has_side_effects=SIDE_EFFECT),
    )(src, land, ssem, rsem, *after)


def _gather_wait(parts, after, *, name):
    def wait(src_ref, land_ref, ssem_ref, rsem_ref):
        x, y, c, chips = _place()
        for k, (cx, cy) in enumerate(chips):
            cp = _gather_copy(src_ref, land_ref, ssem_ref, rsem_ref, k, 2 * cx + cy, (x, y, c), c)
            cp.wait_send()
            cp.wait_recv()

    return _wait_call(wait, parts, after, name=name)[1]


def _sibling_forward(land, *, name):
    def body(_, buf, send_sems, recv_sems):
        x, y, c, chips = _place()
        cols = buf.shape[2]
        copies = []
        for k, (cx, cy) in enumerate(chips):
            mine = buf.at[2 * cx + cy, :, _half_cols(cols, c)]
            cp = pltpu.make_async_remote_copy(
                src_ref=mine, dst_ref=mine, send_sem=send_sems.at[k], recv_sem=recv_sems.at[k],
                device_id=(x, y, 1 - c), device_id_type=MESH)
            cp.start()
            copies.append(cp)
        for k, (cx, cy) in enumerate(chips):
            theirs = buf.at[2 * cx + cy, :, _half_cols(cols, 1 - c)]
            pltpu.make_async_remote_copy(
                src_ref=theirs, dst_ref=theirs, send_sem=send_sems.at[k], recv_sem=recv_sems.at[k],
                device_id=(x, y, c), device_id_type=MESH).wait_recv()
        for cp in copies:
            cp.wait_send()

    return pl.pallas_call(
        body, name=name, in_specs=[HBM], out_specs=HBM, out_shape=jax.ShapeDtypeStruct(land.shape, land.dtype),
        input_output_aliases={0: 0},
        scratch_shapes=[pltpu.SemaphoreType.DMA((3,)), pltpu.SemaphoreType.DMA((3,))],
    )(land)


def _scatter_copy(src_ref, land_ref, ssem, rsem, k, src_slab, dst_slab, to):
    return pltpu.make_async_remote_copy(
        src_ref=src_ref.at[src_slab], dst_ref=land_ref.at[dst_slab], send_sem=ssem.at[k], recv_sem=rsem.at[k],
        device_id=to, device_id_type=MESH)


def _scatter_start(part, *, name):
    def start(src_ref, land_ref, ssem, rsem):
        x, y, c, chips = _place()
        me = 2 * x + y
        for k, (cx, cy) in enumerate(chips):
            _scatter_copy(src_ref, land_ref, ssem, rsem, k, 2 * cx + cy, me, (cx, cy, c)).start()

    return _split_start(start, part, part.shape, N_CHIPS - 1, name=name)


def _scatter_wait(parts, after, *, name):
    def wait(src_ref, land_ref, ssem_ref, rsem_ref):
        x, y, c, chips = _place()
        for k, (cx, cy) in enumerate(chips):
            idx = 2 * cx + cy
            cp = _scatter_copy(src_ref, land_ref, ssem_ref, rsem_ref, k, idx, idx, (x, y, c))
            cp.wait_send()
            cp.wait_recv()

    return _wait_call(wait, parts, after, name=name)


def _split_start(start_fn, src, land_shape, n_sems, *, name):
    def body(src_ref, land_ref, ssem, rsem, src_out, land_out, token):
        start_fn(src_ref, land_ref, ssem, rsem)
        token[...] = jnp.zeros_like(token)

    res = pl.pallas_call(
        body, name=name, in_specs=[HBM, HBM], out_specs=[SEM, SEM, HBM, HBM, pl.BlockSpec(memory_space=pltpu.VMEM)],
        out_shape=[pltpu.SemaphoreType.DMA((n_sems,)), pltpu.SemaphoreType.DMA((n_sems,)), _hbm(src.shape, src.dtype),
                   _hbm(land_shape, src.dtype), jax.ShapeDtypeStruct(TOKEN_SHAPE, F32)],
        input_output_aliases={0: 2, 1: 3}, compiler_params=pltpu.CompilerParams(has_side_effects=SIDE_EFFECT),
    )(_in_hbm(src), _in_hbm(lax.empty(land_shape, src.dtype)))
    return tuple(res[:4]), res[4]


def _swap_copy(src_ref, land_ref, ssem, rsem, which, to):
    return pltpu.make_async_remote_copy(
        src_ref=src_ref.at[:, :, _half_cols(src_ref.shape[2], which)], dst_ref=land_ref,
        send_sem=ssem.at[0], recv_sem=rsem.at[0], device_id=to, device_id_type=MESH)


def _swap_start(grad, *, name):
    def start(src_ref, land_ref, ssem, rsem):
        x, y, c, _ = _place()
        _swap_copy(src_ref, land_ref, ssem, rsem, 1 - c, (x, y, 1 - c)).start()

    s, r, cols = grad.shape
    return _split_start(start, grad, (s, r, cols // 2), 1, name=name)


def _swap_wait(parts, after, *, name):
    def wait(src_ref, land_ref, ssem_ref, rsem_ref):
        x, y, c, _ = _place()
        cp = _swap_copy(src_ref, land_ref, ssem_ref, rsem_ref, 1 - c, (x, y, c))
        cp.wait_send()
        cp.wait_recv()

    return _wait_call(wait, parts, after, name=name)


def _dev_peers(x, y, c, chips):
    return [(x, y, 1 - c)] + [(cx, cy, c) for cx, cy in chips] + [(cx, cy, 1 - c) for cx, cy in chips]


def _dev_gather_start(part, *, name):
    def start(src_ref, land_ref, ssem, rsem):
        x, y, c, chips = _place()
        for k, to in enumerate(_dev_peers(x, y, c, chips)):
            pltpu.make_async_remote_copy(
                src_ref=src_ref, dst_ref=land_ref.at[4 * x + 2 * y + c], send_sem=ssem.at[k], recv_sem=rsem.at[k],
                device_id=to, device_id_type=MESH).start()

    return _split_start(start, part, (N_DEV,) + part.shape, N_DEV - 1, name=name)


def _dev_gather_wait(parts, after, *, name):
    def wait(src_ref, land_ref, ssem_ref, rsem_ref):
        x, y, c, chips = _place()
        for k, (px, py, pc) in enumerate(_dev_peers(x, y, c, chips)):
            cp = pltpu.make_async_remote_copy(
                src_ref=src_ref, dst_ref=land_ref.at[4 * px + 2 * py + pc], send_sem=ssem_ref.at[k],
                recv_sem=rsem_ref.at[k], device_id=(x, y, c), device_id_type=MESH)
            cp.wait_send()
            cp.wait_recv()

    return _wait_call(wait, parts, after, name=name)[1]


def _sibling_share_halves(arrays, *, name):
    n = len(arrays)

    def body(*refs):
        bufs = refs[n:2 * n]
        send_sems, recv_sems = refs[2 * n:]
        x, y, c, _ = _place()
        copies = []
        for a in range(n):
            mine = bufs[a].at[:, _half_cols(bufs[a].shape[1], c)]
            cp = pltpu.make_async_remote_copy(
                src_ref=mine, dst_ref=mine, send_sem=send_sems.at[a], recv_sem=recv_sems.at[a],
                device_id=(x, y, 1 - c), device_id_type=MESH)
            cp.start()
            copies.append(cp)
        for a in range(n):
            theirs = bufs[a].at[:, _half_cols(bufs[a].shape[1], 1 - c)]
            pltpu.make_async_remote_copy(
                src_ref=theirs, dst_ref=theirs, send_sem=send_sems.at[a], recv_sem=recv_sems.at[a],
                device_id=(x, y, c), device_id_type=MESH).wait_recv()
        for cp in copies:
            cp.wait_send()

    return pl.pallas_call(
        body, name=name, in_specs=[HBM] * n, out_specs=[HBM] * n,
        out_shape=[jax.ShapeDtypeStruct(h.shape, h.dtype) for h in arrays],
        input_output_aliases={a: a for a in range(n)},
        scratch_shapes=[pltpu.SemaphoreType.DMA((n,)), pltpu.SemaphoreType.DMA((n,))],
    )(*arrays)


def _pack(arrays, rows_multiple=16, width=LANES):
    flat = jnp.concatenate([a.astype(F32).reshape(-1) for a in arrays])
    total = flat.shape[0]
    rows = -(-total // width)
    rows = -(-rows // rows_multiple) * rows_multiple
    return jnp.pad(flat, (0, rows * width - total)).reshape(rows, width)


def _unpack(buf, shapes):
    flat = buf.reshape(-1)
    out, off = [], 0
    for s in shapes:
        n = math.prod(s)
        out.append(flat[off:off + n].reshape(s))
        off += n
    return out


def kernel(x, norm_pre, norm_post, gla_w_in, gla_w_gate2, gla_b_gate, gla_o_gain, gla_w_out, sgu_w_in, sgu_ln_gain, sgu_ln_bias, sgu_w_spatial, sgu_b_spatial, sgu_w_out, loss_target, m_norm_pre, m_norm_post, m_gla_w_in, m_gla_w_gate2, m_gla_b_gate, m_gla_o_gain, m_gla_w_out, m_sgu_w_in, m_sgu_ln_gain, m_sgu_ln_bias, m_sgu_w_spatial, m_sgu_b_spatial, m_sgu_w_out, v_norm_pre, v_norm_post, v_gla_w_in, v_gla_w_gate2, v_gla_b_gate, v_gla_o_gain, v_gla_w_out, v_sgu_w_in, v_sgu_ln_gain, v_sgu_ln_bias, v_sgu_w_spatial, v_sgu_b_spatial, v_sgu_w_out):
    _, t, d = x.shape
    dk = d // 2
    ws = gla_w_in.shape[2]
    wp = -(-ws // LANES) * LANES
    lay = (ws, wp)
    chip =2 * lax.axis_index("x") + lax.axis_index("y")
    core = lax.axis_index("c")
    core_idx = core.astype(jnp.int32).reshape(1)
    others = jnp.arange(N_CHIPS - 1, dtype=jnp.int32)
    others = others + (others >= chip).astype(jnp.int32)
    slots = jnp.concatenate([chip.astype(jnp.int32).reshape(1), others, core_idx])

    x0 = x[0]
    target = loss_target[0]

    wt_in_g, mt_in_g, vt_in_g = [jnp.pad(p[0].T, ((0, wp - ws), (0, 0))) for p in (gla_w_in, m_gla_w_in, v_gla_w_in)]

    small_shard = _pack([gla_w_gate2[0], sgu_ln_gain[0], sgu_ln_bias[0]], rows_multiple=8, width=2 * LANES)
    own = [small_shard, wt_in_g.astype(BF16)]
    in_flight, token = _gather_start(own, name="gather_start_a")
    own_later = [gla_w_out[0].astype(BF16), sgu_w_in[0].astype(BF16), sgu_w_out[0].astype(BF16)]
    in_flight_later, token_later = _gather_start(own_later, name="gather_start_b")
    own, in_flight = own + own_later, in_flight + in_flight_later

    def arrived(i, after, name):
        land = _gather_wait(in_flight[i], after, name=name + "_wait")
        land = _sibling_forward(land, name=name + "_share")
        return lax.dynamic_update_slice(land, own[i][None], (chip, 0, 0))

    h0 = _norm_pre(x0, norm_pre[0:1] + token[0:1, 0:1] + token_later[0:1, 0:1], name="pre0")
    g_small = arrived(0, h0, "w_small")
    g_wi_g = arrived(1, [g_small, mt_in_g, vt_in_g], "w_gla_in")
    wt_g = g_wi_g.reshape(N_CHIPS * wp, d)
    shard_shapes = [gla_w_gate2.shape[1:], sgu_ln_gain.shape[1:], sgu_ln_bias.shape[1:]]
    per_chip = [_unpack(g_small[j], shard_shapes) for j in range(N_CHIPS)]
    w2_full = jnp.concatenate([p[0] for p in per_chip], axis=1)
    ln_gain = jnp.concatenate([p[1] for p in per_chip], axis=0)[None, :]
    ln_bias = jnp.concatenate([p[2] for p in per_chip], axis=0)[None, :]
    w2p = jnp.pad(w2_full, ((0, LANES - GLA_GATE_RANK), (0, 0)))

    pos_chunk = jnp.arange(SGU_BLOCK) // CHUNK
    mask = pos_chunk[:, None] >= pos_chunk[None, :]
    ws_masked = jnp.where(mask[None], sgu_w_spatial[0], 0.0)
    ws_masked_t = ws_masked.transpose(0, 2, 1)
    bs_t = sgu_b_spatial[0].T

    proj0 = _matmul(h0, wt_g, mode="nt", out_dtype=F32, name="gla_in", tn=wp)
    o0, a0, s_before, s_final = _gla_fwd(proj0, w2p, gla_b_gate, gla_o_gain, lay, name="gla_scan")
    w_out_g = arrived(2, a0, "w_gla_out").reshape(d, d)
    y0 = _matmul(a0, w_out_g, mode="nn", out_dtype=F32, name="gla_out")
    x1, h1 = _post_then_pre(x0, y0, norm_post[0:1], norm_pre[1:2], name="post0_pre1")
    g_wi_s = arrived(3, h1, "w_sgu_in")
    proj1 = _matmul(h1, g_wi_s, mode="nn", out_dtype=F32, name="sgu_in", b_shards=True)
    a1 = _sgu_fwd(proj1, ln_gain, ln_bias, ws_masked, bs_t, name="sgu_gate")
    w_out_s = arrived(4, a1, "w_sgu_out").reshape(d, d)
    y1 = _matmul(a1, w_out_s, mode="nn", out_dtype=F32, name="sgu_out")
    loss_part, dx2, dy1, d_post1 = _loss_head(x1, y1, norm_post[1:2], target, name="loss_head")

    def behind(small, token):
        return small + token[0:1, 0:1]

    def pair_and_scatter(swap, after, name):
        grad, peer = _swap_wait(swap, after, name=name + "_swap_wait")
        pair = _pair_sum_bf16(grad, core_idx, peer, name=name + "_pair")
        return _scatter_start(pair, name=name + "_start")

    def reduced(flight, after, name):
        pair, landed = _scatter_wait(flight, after, name=name + "_wait")
        return _chip_sum(pair, landed, slots, name=name + "_sum")

    dw_out_s = _matmul(a1, dy1, mode="tn", out_dtype=F32, name="d_sgu_w_out")
    swap, tok = _swap_start(dw_out_s.reshape(N_CHIPS, d // N_CHIPS, d), name="g_sgu_out_swap")
    da1 = _matmul(dy1, w_out_s, mode="nt", out_dtype=F32, name="d_sgu_act", after=tok)
    fl_wo_s, tok = pair_and_scatter(swap, da1, "g_sgu_out")
    dproj1, d_ws, d_bs_t, d_lg, d_lb = _sgu_bwd(da1, proj1, ln_gain, behind(ln_bias, tok), ws_masked, ws_masked_t,
                                                bs_t, name="sgu_gate_bwd")
    dw_in_s = _matmul(h1, dproj1, mode="tn", out_dtype=F32, name="d_sgu_w_in", out_shards=True)
    swap, tok = _swap_start(dw_in_s, name="g_sgu_in_swap")
    dh1 = _matmul_nt_shards(dproj1, g_wi_s, out_dtype=F32, name="d_sgu_h", after=tok)
    fl_wi_s, tok = pair_and_scatter(swap, dh1, "g_sgu_in")
    dx1, dy0, d_pre1, d_post0 = _mid_bwd(dx2, dh1, x1, behind(norm_pre[1:2], tok), y0, norm_post[0:1],
                                         name="pre1_post0_bwd")
    dw_out_g = _matmul(a0, dy0, mode="tn", out_dtype=F32, name="d_gla_w_out")
    swap, tok = _swap_start(dw_out_g.reshape(N_CHIPS, d // N_CHIPS, d), name="g_gla_out_swap")
    da0 = _matmul(dy0, w_out_g, mode="nt", out_dtype=F32, name="d_gla_act", after=tok)
    fl_wo_g, tok = pair_and_scatter(swap, da0, "g_gla_out")
    dproj0, d_og, d_bg, d_w2p = _gla_bwd(da0, o0, proj0, w2p, behind(gla_b_gate, tok), gla_o_gain, s_before, s_final,
                                         lay, name="gla_scan_bwd")
    dwt_in_g = _matmul(dproj0, h0, mode="tn", out_dtype=F32, name="d_gla_w_in", tm=wp)
    swap, tok = _swap_start(dwt_in_g.reshape(N_CHIPS, wp, d), name="g_gla_in_swap")
    dh0 = _matmul(dproj0, wt_g, mode="nn", out_dtype=F32, name="d_gla_h", tk=N_CHIPS * wp, after=tok)
    fl_wi_g, tok = pair_and_scatter(swap, dh0, "g_gla_in")
    grad_x, d_pre0 = _first_bwd(dx1, dh0, x0, behind(norm_pre[0:1], tok), name="pre0_bwd")

    small_shapes = [norm_pre.shape, norm_post.shape, gla_b_gate.shape, gla_o_gain.shape, sgu_w_spatial.shape,
                    sgu_b_spatial.shape, (1, GLA_GATE_RANK, dk), (1, d), (1, d), (1, LANES)]
    d_pre = jnp.concatenate([d_pre0, d_pre1], axis=0)
    d_post = jnp.concatenate([d_post0, d_post1], axis=0)
    d_wsp = jnp.where(mask[None], d_ws, 0.0)[None]
    small_part = _pack([d_pre, d_post, d_bg, d_og, d_wsp, d_bs_t.T[None], d_w2p[:GLA_GATE_RANK][None], d_lg, d_lb,
                        loss_part])
    small_flight, tok = _dev_gather_start(small_part, name="small_grads_start")

    def big_update(w, g, m, v, name):
        return [u[None] for u in _adamw(w[0], g, m[0], v[0], name=name)]

    r_wo_s = reduced(fl_wo_s, [grad_x, tok], "g_sgu_out")
    r_wi_s = reduced(fl_wi_s, r_wo_s, "g_sgu_in")
    r_wo_g = reduced(fl_wo_g, r_wi_s, "g_gla_out")
    g_wo_sgu, g_wi_sgu, g_wo_gla = _sibling_share_halves([r_wo_s, r_wi_s, r_wo_g], name="grads_share_a")
    u_wo_sgu = big_update(sgu_w_out, g_wo_sgu, m_sgu_w_out, v_sgu_w_out, "adamw_sgu_w_out")
    u_wi_sgu = big_update(sgu_w_in, g_wi_sgu, m_sgu_w_in, v_sgu_w_in, "adamw_sgu_w_in")
    u_wo_gla = big_update(gla_w_out, g_wo_gla, m_gla_w_out, v_gla_w_out, "adamw_gla_w_out")
    r_wi_g = reduced(fl_wi_g, [u_wo_gla[1], u_wi_sgu[1], u_wo_sgu[1]], "g_gla_in")
    gt_wi_gla, = _sibling_share_halves([r_wi_g], name="grads_share_b")
    u_wi_gla = [u[:ws].T[None] for u in _adamw(wt_in_g, gt_wi_gla, mt_in_g, vt_in_g, name="adamw_gla_w_in")]

    small_land = _dev_gather_wait(small_flight, u_wi_gla[1], name="small_grads_wait")
    small_all = lax.dynamic_update_slice(small_land, small_part[None], (2 * chip + core, 0, 0))
    small_sum = _stack_sum(small_all, name="small_sum")
    (g_pre, g_post, g_bg, g_og, g_wsp, g_bsp, g_w2_full, g_lg_full, g_lb_full, loss_vec) = _unpack(small_sum, small_shapes)
    loss = loss_vec[0, 0]
    g_w2 = lax.dynamic_slice_in_dim(g_w2_full, chip * (dk // N_CHIPS), dk // N_CHIPS, axis=2)
    g_lg = lax.dynamic_slice_in_dim(g_lg_full, chip * (d // N_CHIPS), d // N_CHIPS, axis=1)
    g_lb = lax.dynamic_slice_in_dim(g_lb_full, chip * (d // N_CHIPS), d // N_CHIPS, axis=1)

    small_w = [norm_pre, norm_post, gla_b_gate, gla_o_gain, sgu_w_spatial, sgu_b_spatial, gla_w_gate2, sgu_ln_gain,
               sgu_ln_bias]
    small_g = [g_pre, g_post, g_bg, g_og, g_wsp, g_bsp, g_w2, g_lg, g_lb]
    small_m = [m_norm_pre, m_norm_post, m_gla_b_gate, m_gla_o_gain, m_sgu_w_spatial, m_sgu_b_spatial, m_gla_w_gate2,
               m_sgu_ln_gain, m_sgu_ln_bias]
    small_v = [v_norm_pre, v_norm_post, v_gla_b_gate, v_gla_o_gain, v_sgu_w_spatial, v_sgu_b_spatial, v_gla_w_gate2,
               v_sgu_ln_gain, v_sgu_ln_bias]
    own_shapes = [w.shape for w in small_w]
    _, s_dl, s_m, s_v = _adamw(_pack(small_w), _pack(small_g), _pack(small_m), _pack(small_v), name="adamw_small")
    dl_s, m_s, v_s = _unpack(s_dl, own_shapes), _unpack(s_m, own_shapes), _unpack(s_v, own_shapes)

    def ordered(small, kind):
        pre, post, bg, og, wsp, bsp, w2, lg, lb = small
        return [pre, post, u_wi_gla[kind], w2, bg, og, u_wo_gla[kind], u_wi_sgu[kind], lg, lb, wsp, bsp, u_wo_sgu[kind]]

    return (loss, grad_x[None], *ordered(small_g, 0), *ordered(dl_s, 1), *ordered(m_s, 2), *ordered(v_s, 3))
```

```python
import functools
import math

import jax
import jax.numpy as jnp
from jax import lax
from jax.experimental import pallas as pl
from jax.experimental.pallas import tpu as pltpu

F32 = jnp.float32
BF16 = jnp.bfloat16
MESH = pl.DeviceIdType.MESH

EPS = 1e-6
CHUNK = 64
GLA_HEADS = 4
GLA_GATE_RANK = 16
GLA_TAU = 16.0
SGU_BLOCK = 128
SGU_GROUPS = 8
N_CHIPS = 4
N_DEV = 8
LANES = 128

ADAM_LR = 0.001
ADAM_B1 = 0.9
ADAM_B2 = 0.999
ADAM_EPS = 1e-08
ADAM_WD = 0.01
ADAM_STEP = 10

VMEM_LIMIT = 56 * 1024 * 1024


def _cparams(sem=None):
    return pltpu.CompilerParams(dimension_semantics=sem, vmem_limit_bytes=VMEM_LIMIT)


def _pick(n, cap, unit=LANES):
    best = None
    for t in range(unit, min(n, cap) + 1, unit):
        if n % t == 0:
            best = t
    assert best is not None, (n, cap, unit)
    return best


def _dot(a, b, dims):
    return lax.dot_general(a, b, (dims, ((), ())), preferred_element_type=F32)


def _dot_nn(a, b):
    return _dot(a, b, ((1,), (0,)))


def _dot_nt(a, b):
    return _dot(a, b, ((1,), (1,)))


def _dot_tn(a, b):
    return _dot(a, b, ((0,), (0,)))


def _matmul(a, b, *, mode, out_dtype, name, tm=1024, tn=512, tk=2048, b_shards=False, out_shards=False, after=None,
            out_rows=None):
    if mode == "tn":
        K, M = a.shape
    else:
        M, K = a.shape
    if b_shards:
        ns, br, bc = b.shape
        if mode == "nt":
            N, Kb = br, ns * bc
        else:
            Kb, N = br, ns * bc
    else:
        if mode == "nt":
            N, Kb = b.shape
        else:
            Kb, N = b.shape
    assert K == Kb, (a.shape, b.shape, mode)
    tm = _pick(M, tm)
    tk = _pick(K, tk)
    if b_shards and mode != "nt":
        tn = _pick(bc, tn)
    elif out_shards:
        tn = _pick(N // N_CHIPS, tn)
    else:
        tn = _pick(N, tn)
    if b_shards and mode == "nt":
        tk = _pick(bc, tk)
    nk = K // tk
    grid = (M // tm, N // tn, nk)

    if mode == "tn":
        a_spec = pl.BlockSpec((tk, tm), lambda i, j, k: (k, i))
    else:
        a_spec = pl.BlockSpec((tm, tk), lambda i, j, k: (i, k))
    if b_shards:
        if mode == "nt":
            per = bc // tk
            b_spec = pl.BlockSpec((None, tn, tk), lambda i, j, k: (k // per, j, k % per))
        else:
            per = bc // tn
            b_spec = pl.BlockSpec((None, tk, tn), lambda i, j, k: (j // per, k, j % per))
    elif mode == "nt":
        b_spec = pl.BlockSpec((tn, tk), lambda i, j, k: (j, k))
    else:
        b_spec = pl.BlockSpec((tk, tn), lambda i, j, k: (k, j))
    if out_shards:
        per_o = (N // N_CHIPS) // tn
        out_spec = pl.BlockSpec((None, tm, tn), lambda i, j, k: (j // per_o, i, j % per_o))
        out_shape = jax.ShapeDtypeStruct((N_CHIPS, M, N // N_CHIPS), out_dtype)
    else:
        out_spec = pl.BlockSpec((tm, tn), lambda i, j, k: (i, j))
        out_shape = jax.ShapeDtypeStruct((M if out_rows is None else out_rows, N), out_dtype)

    dims = {"nn": ((1,), (0,)), "nt": ((1,), (1,)), "tn": ((0,), (0,))}[mode]

    def body(a_ref, b_ref, *rest):
        o_ref, scratch = (rest[1], rest[2:]) if after is not None else (rest[0], rest[1:])
        part = _dot(a_ref[...].astype(BF16), b_ref[...].astype(BF16), dims)
        if nk == 1:
            o_ref[...] = part.astype(out_dtype)
        else:
            acc_ref, = scratch
            k = pl.program_id(2)

            @pl.when(k == 0)
            def _():
                acc_ref[...] = part

            @pl.when(k > 0)
            def _():
                acc_ref[...] += part

            @pl.when(k == nk - 1)
            def _():
                o_ref[...] = acc_ref[...].astype(out_dtype)

    extra_specs, extra_args = ([], []) if after is None else ([pl.BlockSpec(memory_space=pl.ANY)], [after])
    return pl.pallas_call(
        body, name=name, grid=grid, in_specs=[a_spec, b_spec] + extra_specs, out_specs=out_spec, out_shape=out_shape,
        scratch_shapes=[] if nk == 1 else [pltpu.VMEM((tm, tn), F32)],
        compiler_params=_cparams(("parallel", "parallel", "arbitrary")),
    )(a, b, *extra_args)


def _matmul_nt_shards(a, b, *, out_dtype, name, tm=1024, tn=512, after=None):
    M, K = a.shape
    ns, N, kc = b.shape
    assert K == ns * kc
    tm, tn = _pick(M, tm), _pick(N, tn)

    def body(a_ref, *rest):
        b_refs, o_ref = rest[:ns], rest[ns + (after is not None)]
        acc = _dot_nt(a_ref[:, 0:kc], b_refs[0][...])
        for j in range(1, ns):
            acc += _dot_nt(a_ref[:, j * kc:(j + 1) * kc], b_refs[j][...])
        o_ref[...] = acc.astype(out_dtype)

    def shard(j):
        return pl.BlockSpec((None, tn, kc), lambda i, n: (j, n, 0))

    extra_specs, extra_args = ([], []) if after is None else ([pl.BlockSpec(memory_space=pl.ANY)], [after])
    return pl.pallas_call(
        body, name=name, grid=(M // tm, N // tn),
        in_specs=[pl.BlockSpec((tm, K), lambda i, n: (i, 0))] + [shard(j) for j in range(ns)] + extra_specs,
        out_specs=pl.BlockSpec((tm, tn), lambda i, n: (i, n)), out_shape=jax.ShapeDtypeStruct((M, N), out_dtype),
        compiler_params=_cparams(("parallel", "parallel")),
    )(a, *([b] * ns), *extra_args)


def _rstd(x):
    return lax.rsqrt(jnp.mean(x * x, axis=-1, keepdims=True) + EPS)


def _row_spec(tr, d):
    return pl.BlockSpec((tr, d), lambda i: (i, 0))


def _vec_spec(d):
    return pl.BlockSpec((1, d), lambda i: (0, 0))


def _acc_rows(ref, i, val, cols=slice(None)):
    @pl.when(i == 0)
    def _():
        ref[:, cols] = val

    @pl.when(i > 0)
    def _():
        ref[:, cols] += val


def _norm_pre(x, gain, *, name, tr=256):
    t, d = x.shape
    tr = _pick(t, tr, 8)

    def body(x_ref, g_ref, h_ref):
        xv = x_ref[...]
        h_ref[...] = (xv * _rstd(xv) * g_ref[...]).astype(BF16)

    return pl.pallas_call(
        body, name=name, grid=(t // tr,), in_specs=[_row_spec(tr, d), _vec_spec(d)], out_specs=_row_spec(tr, d),
        out_shape=jax.ShapeDtypeStruct((t, d), BF16), compiler_params=_cparams(("parallel",)),
    )(x, gain)


def _post_then_pre(x, y, post_gain, pre_gain, *, name, tr=256):
    t, d = x.shape
    tr = _pick(t, tr, 8)

    def body(x_ref, y_ref, pg_ref, ng_ref, xn_ref, h_ref):
        yv = y_ref[...]
        xn = x_ref[...] + yv * _rstd(yv) * pg_ref[...]
        xn_ref[...] = xn
        h_ref[...] = (xn * _rstd(xn) * ng_ref[...]).astype(BF16)

    return pl.pallas_call(
        body, name=name, grid=(t // tr,),
        in_specs=[_row_spec(tr, d), _row_spec(tr, d), _vec_spec(d), _vec_spec(d)],
        out_specs=[_row_spec(tr, d), _row_spec(tr, d)],
        out_shape=[jax.ShapeDtypeStruct((t, d), F32), jax.ShapeDtypeStruct((t, d), BF16)],
        compiler_params=_cparams(("parallel",)),
    )(x, y, post_gain, pre_gain)


def _norm_bwd(dy, n, r, gain):
    dn = dy * gain
    return r * (dn - n * jnp.mean(dn * n, axis=-1, keepdims=True))


def _loss_head(x, y, post_gain, target, *, name, tr=256):
    t, d = x.shape
    tr = _pick(t, tr, 8)

    def body(x_ref, y_ref, pg_ref, t_ref, loss_ref, dx_ref, dy_ref, dpg_ref):
        i = pl.program_id(0)
        yv = y_ref[...]
        r = _rstd(yv)
        n = yv * r
        err = x_ref[...] + n * pg_ref[...] - t_ref[...]
        dx = err * (1.0 / d)
        dx_ref[...] = dx
        part = 0.5 * jnp.sum(jnp.mean(err * err, axis=-1, keepdims=True), axis=0, keepdims=True)
        _acc_rows(loss_ref, i, jnp.broadcast_to(part, (1, LANES)))
        _acc_rows(dpg_ref, i, jnp.sum(dx * n, axis=0, keepdims=True))
        dy_ref[...] = _norm_bwd(dx, n, r, pg_ref[...]).astype(BF16)

    return pl.pallas_call(
        body, name=name, grid=(t // tr,),
        in_specs=[_row_spec(tr, d), _row_spec(tr, d), _vec_spec(d), _row_spec(tr, d)],
        out_specs=[_vec_spec(LANES), _row_spec(tr, d), _row_spec(tr, d), _vec_spec(d)],
        out_shape=[jax.ShapeDtypeStruct((1, LANES), F32), jax.ShapeDtypeStruct((t, d), F32),
                   jax.ShapeDtypeStruct((t, d), BF16), jax.ShapeDtypeStruct((1, d), F32)],
        compiler_params=_cparams(("arbitrary",)),
    )(x, y, post_gain, target)


def _mid_bwd(dx_out, dh, x, pre_gain, y_prev, post_gain_prev, *, name, tr=256):
    t, d = x.shape
    tr = _pick(t, tr, 8)

    def body(dxo_ref, dh_ref, x_ref, ng_ref, y_ref, pg_ref, dx_ref, dy_ref, dng_ref, dpg_ref):
        i = pl.program_id(0)
        xv = x_ref[...]
        r = _rstd(xv)
        xh = xv * r
        dhv = dh_ref[...]
        _acc_rows(dng_ref, i, jnp.sum(dhv * xh, axis=0, keepdims=True))
        dx = dxo_ref[...] + _norm_bwd(dhv, xh, r, ng_ref[...])
        dx_ref[...] = dx
        yv = y_ref[...]
        ry = _rstd(yv)
        n = yv * ry
        _acc_rows(dpg_ref, i, jnp.sum(dx * n, axis=0, keepdims=True))
        dy_ref[...] = _norm_bwd(dx, n, ry, pg_ref[...]).astype(BF16)

    return pl.pallas_call(
        body, name=name, grid=(t // tr,),
        in_specs=[_row_spec(tr, d), _row_spec(tr, d), _row_spec(tr, d), _vec_spec(d), _row_spec(tr, d), _vec_spec(d)],
        out_specs=[_row_spec(tr, d), _row_spec(tr, d), _vec_spec(d), _vec_spec(d)],
        out_shape=[jax.ShapeDtypeStruct((t, d), F32), jax.ShapeDtypeStruct((t, d), BF16),
                   jax.ShapeDtypeStruct((1, d), F32), jax.ShapeDtypeStruct((1, d), F32)],
        compiler_params=_cparams(("arbitrary",)),
    )(dx_out, dh, x, pre_gain, y_prev, post_gain_prev)


def _first_bwd(dx_out, dh, x, pre_gain, *, name, tr=256):
    t, d = x.shape
    tr = _pick(t, tr, 8)

    def body(dxo_ref, dh_ref, x_ref, ng_ref, dx_ref, dng_ref):
        i = pl.program_id(0)
        xv = x_ref[...]
        r = _rstd(xv)
        xh = xv * r
        dhv = dh_ref[...]
        _acc_rows(dng_ref, i, jnp.sum(dhv * xh, axis=0, keepdims=True))
        dx_ref[...] = dxo_ref[...] + _norm_bwd(dhv, xh, r, ng_ref[...])

    return pl.pallas_call(
        body, name=name, grid=(t // tr,),
        in_specs=[_row_spec(tr, d), _row_spec(tr, d), _row_spec(tr, d), _vec_spec(d)],
        out_specs=[_row_spec(tr, d), _vec_spec(d)],
        out_shape=[jax.ShapeDtypeStruct((t, d), F32), jax.ShapeDtypeStruct((1, d), F32)],
        compiler_params=_cparams(("arbitrary",)),
    )(dx_out, dh, x, pre_gain)


def _sigmoid(x):
    return 1.0 / (1.0 + jnp.exp(-x))


def _log_sigmoid(x):
    return jnp.minimum(x, 0.0) - jnp.log(1.0 + jnp.exp(-jnp.abs(x)))


_GELU_C = math.sqrt(2.0 / math.pi)


def _gelu_parts(x):
    x2 = x * x
    th = jnp.tanh(_GELU_C * (x + 0.044715 * x * x2))
    val = 0.5 * x * (1.0 + th)
    grad = 0.5 * (1.0 + th) + 0.5 * x * (1.0 - th * th) * (_GELU_C * (1.0 + 3.0 * 0.044715 * x2))
    return val, grad


def _split3(x):
    hi = x.astype(BF16)
    r1 = x - hi.astype(F32)
    mid = r1.astype(BF16)
    lo = (r1 - mid.astype(F32)).astype(BF16)
    return hi, mid, lo


def _tri_matmul(tri_bf16, x):
    hi, mid, lo = _split3(x)
    return _dot_nn(tri_bf16, hi) + _dot_nn(tri_bf16, mid) + _dot_nn(tri_bf16, lo)


def _gla_dims(d):
    dk, dv = d // 2, d
    return dk, dv, dk // GLA_HEADS, dv // GLA_HEADS


def _col_pieces(a, b, lay):
    ws, wp = lay
    out = []
    while a < b:
        j = a // ws
        end = min(b, (j + 1) * ws)
        out.append((j * wp + a - j * ws, end - a))
        a = end
    return out


def _load_cols(ref, a, b, lay):
    parts = [ref[:, s:s + n] for s, n in _col_pieces(a, b, lay)]
    return parts[0] if len(parts) == 1 else jnp.concatenate(parts, axis=1)


def _store_cols(ref, a, val, lay):
    off = 0
    for s, n in _col_pieces(a, a + val.shape[1], lay):
        ref[:, s:s + n] = val[:, off:off + n]
        off += n


def _gate_window(c_r, lay):
    (start, _), = _col_pieces(c_r, c_r + GLA_GATE_RANK, lay)
    assert (start % lay[1]) + LANES <= lay[1]
    return slice(start, start + LANES)


def _gla_gates(glr, k, w2_ref, b_ref):
    z = _dot_nn(glr.astype(BF16), w2_ref[...].astype(BF16)) + b_ref[...]
    la = _log_sigmoid(z) * (1.0 / GLA_TAU)
    row = lax.broadcasted_iota(jnp.int32, (CHUNK, CHUNK), 0)
    col = lax.broadcasted_iota(jnp.int32, (CHUNK, CHUNK), 1)
    incl = (row >= col).astype(BF16)
    bcum = _tri_matmul(incl, la)
    b_end = bcum[CHUNK - 1:CHUNK, :]
    e_rest = jnp.exp(b_end - bcum)
    return z, e_rest, k * e_rest, jnp.exp(b_end)


def _gla_fwd(proj, w2p, b_gate, o_gain, lay, *, name):
    t, wcols = proj.shape
    d = o_gain.shape[1]
    dk, dv, dkh, dvh = _gla_dims(d)
    nc = t // CHUNK
    c_k, c_v, c_g, c_r = dk, 2 * dk, 2 * dk + dv, 2 * dk + 2 * dv
    scale = dkh ** -0.5

    def body(p_ref, w2_ref, b_ref, og_ref, o_ref, a_ref, sb_ref, sfin_ref, s_ref):
        i = pl.program_id(0)

        @pl.when(i == 0)
        def _():
            s_ref[...] = jnp.zeros_like(s_ref)

        q = _load_cols(p_ref, 0, dk, lay) * scale
        k = _load_cols(p_ref, c_k, c_k + dk, lay)
        glr = p_ref[:, _gate_window(c_r, lay)]
        _, _, kdec, decay = _gla_gates(glr, k, w2_ref, b_ref)
        for h in range(GLA_HEADS):
            ks = slice(h * dkh, (h + 1) * dkh)
            vs = slice(h * dvh, (h + 1) * dvh)
            v_h = _load_cols(p_ref, c_v + h * dvh, c_v + (h + 1) * dvh, lay)
            g_h = _load_cols(p_ref, c_g + h * dvh, c_g + (h + 1) * dvh, lay)
            s_old = s_ref[h]
            sb_ref[0, h] = s_old
            s_new = s_old * decay[:, ks] + _dot_tn(v_h.astype(BF16), kdec[:, ks].astype(BF16))
            s_ref[h] = s_new
            o_h = _dot_nt(q[:, ks].astype(BF16), s_new.astype(BF16))
            o_ref[:, vs] = o_h
            on = o_h * _rstd(o_h)
            a_ref[:, vs] = (on * og_ref[:, vs] * (g_h * _sigmoid(g_h))).astype(BF16)

        @pl.when(i == nc - 1)
        def _():
            sfin_ref[...] = s_ref[...]

    full = lambda *shape: pl.BlockSpec(shape, lambda i: (0,) * len(shape))
    return pl.pallas_call(
        body, name=name, grid=(nc,),
        in_specs=[pl.BlockSpec((CHUNK, wcols), lambda i: (i, 0)), full(LANES, dk), full(1, dk), full(1, dv)],
        out_specs=[pl.BlockSpec((CHUNK, dv), lambda i: (i, 0)), pl.BlockSpec((CHUNK, dv), lambda i: (i, 0)),
                   pl.BlockSpec((1, GLA_HEADS, dvh, dkh), lambda i: (i, 0, 0, 0)), full(GLA_HEADS, dvh, dkh)],
        out_shape=[jax.ShapeDtypeStruct((t, dv), F32), jax.ShapeDtypeStruct((t, dv), BF16),
                   jax.ShapeDtypeStruct((nc, GLA_HEADS, dvh, dkh), F32),
                   jax.ShapeDtypeStruct((GLA_HEADS, dvh, dkh), F32)],
        scratch_shapes=[pltpu.VMEM((GLA_HEADS, dvh, dkh), F32)],
        compiler_params=_cparams(("arbitrary",)),
    )(proj, w2p, b_gate, o_gain)


def _gla_bwd(da, o, proj, w2p, b_gate, o_gain, s_before, s_final, lay, *, name):
    t, wcols = proj.shape
    d = o_gain.shape[1]
    dk, dv, dkh, dvh = _gla_dims(d)
    nc = t // CHUNK
    c_k, c_v, c_g, c_r = dk, 2 * dk, 2 * dk + dv, 2 * dk + 2 * dv
    scale = dkh ** -0.5

    def body(da_ref, o_ref, p_ref, w2_ref, b_ref, og_ref, sb_ref, sfin_ref,
             dp_ref, dog_ref, db_ref, dw2_ref, s_ref, gc_ref, dkd_ref):
        i = pl.program_id(0)

        @pl.when(i == 0)
        def _():
            s_ref[...] = sfin_ref[...]
            gc_ref[...] = jnp.zeros_like(gc_ref)

        ws, wp = lay
        for j in range(N_CHIPS):
            dp_ref[:, j * wp + ws:(j + 1) * wp] = jnp.zeros((CHUNK, wp - ws), BF16)
        q = _load_cols(p_ref, 0, dk, lay) * scale
        k = _load_cols(p_ref, c_k, c_k + dk, lay)
        glr = p_ref[:, _gate_window(c_r, lay)]
        z, e_rest, kdec, decay = _gla_gates(glr, k, w2_ref, b_ref)
        ddecay = []
        for h in range(GLA_HEADS):
            ks = slice(h * dkh, (h + 1) * dkh)
            vs = slice(h * dvh, (h + 1) * dvh)
            v_h = _load_cols(p_ref, c_v + h * dvh, c_v + (h + 1) * dvh, lay)
            g_h = _load_cols(p_ref, c_g + h * dvh, c_g + (h + 1) * dvh, lay)
            da_h = da_ref[:, vs]
            o_h = o_ref[:, vs]
            og_h = og_ref[:, vs]
            r = _rstd(o_h)
            on = o_h * r
            sg = _sigmoid(g_h)
            silu = g_h * sg
            _acc_rows(dog_ref, i, jnp.sum(da_h * silu * on, axis=0, keepdims=True), vs)
            _store_cols(dp_ref, c_g + h * dvh, (da_h * (on * og_h) * (sg * (1.0 + g_h * (1.0 - sg)))).astype(BF16),
                        lay)
            don = da_h * silu * og_h
            do_h = (r * (don - on * jnp.mean(don * on, axis=-1, keepdims=True))).astype(BF16)
            s_cur = s_ref[h]
            _store_cols(dp_ref, h * dkh, (_dot_nn(do_h, s_cur.astype(BF16)) * scale).astype(BF16), lay)
            g_tot = gc_ref[h] + _dot_tn(do_h, q[:, ks].astype(BF16))
            g_bf = g_tot.astype(BF16)
            dkd_ref[:, ks] = _dot_nn(v_h.astype(BF16), g_bf)
            _store_cols(dp_ref, c_v + h * dvh, _dot_nt(kdec[:, ks].astype(BF16), g_bf).astype(BF16), lay)
            s_prev = sb_ref[0, h]
            ddecay.append(jnp.sum(g_tot * s_prev, axis=0, keepdims=True))
            gc_ref[h] = g_tot * decay[:, ks]
            s_ref[h] = s_prev
        dkdec = dkd_ref[...]
        _store_cols(dp_ref, c_k, (dkdec * e_rest).astype(BF16), lay)
        d_e = dkdec * kdec
        row = lax.broadcasted_iota(jnp.int32, (CHUNK, CHUNK), 0)
        col = lax.broadcasted_iota(jnp.int32, (CHUNK, CHUNK), 1)
        excl = (row > col).astype(BF16)
        dla = jnp.concatenate(ddecay, axis=1) * decay + _tri_matmul(excl, d_e)
        dz = dla * (1.0 / GLA_TAU) * (1.0 - _sigmoid(z))
        _acc_rows(db_ref, i, jnp.sum(dz, axis=0, keepdims=True))
        dz_bf = dz.astype(BF16)
        dw2 = _dot_tn(glr.astype(BF16), dz_bf)

        @pl.when(i == 0)
        def _():
            dw2_ref[...] = dw2

        @pl.when(i > 0)
        def _():
            dw2_ref[...] += dw2

        dp_ref[:, _gate_window(c_r, lay)] = _dot_nt(dz_bf, w2_ref[...].astype(BF16)).astype(BF16)

    rev = lambda i: (nc - 1 - i, 0)
    full = lambda *shape: pl.BlockSpec(shape, lambda i: (0,) * len(shape))
    return pl.pallas_call(
        body, name=name, grid=(nc,),
        in_specs=[pl.BlockSpec((CHUNK, dv), rev), pl.BlockSpec((CHUNK, dv), rev), pl.BlockSpec((CHUNK, wcols), rev),
                  full(LANES, dk), full(1, dk), full(1, dv),
                  pl.BlockSpec((1, GLA_HEADS, dvh, dkh), lambda i: (nc - 1 - i, 0, 0, 0)), full(GLA_HEADS, dvh, dkh)],
        out_specs=[pl.BlockSpec((CHUNK, wcols), rev), full(1, dv), full(1, dk), full(LANES, dk)],
        out_shape=[jax.ShapeDtypeStruct((t, wcols), BF16), jax.ShapeDtypeStruct((1, dv), F32),
                   jax.ShapeDtypeStruct((1, dk), F32), jax.ShapeDtypeStruct((LANES, dk), F32)],
        scratch_shapes=[pltpu.VMEM((GLA_HEADS, dvh, dkh), F32), pltpu.VMEM((GLA_HEADS, dvh, dkh), F32),
                        pltpu.VMEM((CHUNK, dk), F32)],
        compiler_params=_cparams(("arbitrary",)),
    )(da, o, proj, w2p, b_gate, o_gain, s_before, s_final)


def _sgu_mid(p_ref, lg_ref, lb_ref, ws_ref, bst_ref, w):
    gd = w // SGU_GROUPS
    u_act, du_fac = _gelu_parts(p_ref[:, 0:w])
    vf, dv_fac = _gelu_parts(p_ref[:, w:2 * w])
    mu = jnp.mean(vf, axis=-1, keepdims=True)
    cen = vf - mu
    rstd = lax.rsqrt(jnp.mean(cen * cen, axis=-1, keepdims=True) + EPS)
    xh = cen * rstd
    vn = (xh * lg_ref[...] + lb_ref[...]).astype(BF16)
    vs = [_dot_nn(ws_ref[g].astype(BF16), vn[:, g * gd:(g + 1) * gd]) + bst_ref[:, g:g + 1]
          for g in range(SGU_GROUPS)]
    return u_act, du_fac, dv_fac, rstd, xh, vn, vs


def _sgu_fwd(proj, ln_gain, ln_bias, ws_masked, bs_t, *, name):
    t, w3 = proj.shape
    w = w3 // 3
    gd = w // SGU_GROUPS
    nb = t // SGU_BLOCK

    def body(p_ref, lg_ref, lb_ref, ws_ref, bst_ref, a_ref):
        u_act, _, _, _, _, _, vs = _sgu_mid(p_ref, lg_ref, lb_ref, ws_ref, bst_ref, w)
        for g in range(SGU_GROUPS):
            cs = slice(g * gd, (g + 1) * gd)
            gate = p_ref[:, 2 * w + g * gd:2 * w + (g + 1) * gd]
            a_ref[:, cs] = (u_act[:, cs] * vs[g] * (gate * _sigmoid(gate))).astype(BF16)

    full = lambda *shape: pl.BlockSpec(shape, lambda i: (0,) * len(shape))
    return pl.pallas_call(
        body, name=name, grid=(nb,),
        in_specs=[pl.BlockSpec((SGU_BLOCK, w3), lambda i: (i, 0)), full(1, w), full(1, w),
                  full(SGU_GROUPS, SGU_BLOCK, SGU_BLOCK), full(SGU_BLOCK, SGU_GROUPS)],
        out_specs=pl.BlockSpec((SGU_BLOCK, w), lambda i: (i, 0)),
        out_shape=jax.ShapeDtypeStruct((t, w), BF16),
        compiler_params=_cparams(("parallel",)),
    )(proj, ln_gain, ln_bias, ws_masked, bs_t)


def _sgu_bwd(da, proj, ln_gain, ln_bias, ws_masked, ws_masked_t, bs_t, *, name):
    t, w3 = proj.shape
    w = w3 // 3
    gd = w // SGU_GROUPS
    nb = t // SGU_BLOCK

    def body(da_ref, p_ref, lg_ref, lb_ref, ws_ref, wst_ref, bst_ref, dp_ref, dws_ref, dbst_ref, dlg_ref, dlb_ref,
             dvn_ref):
        i = pl.program_id(0)
        u_act, du_fac, dv_fac, rstd, xh, vn, vs = _sgu_mid(p_ref, lg_ref, lb_ref, ws_ref, bst_ref, w)
        for g in range(SGU_GROUPS):
            cs = slice(g * gd, (g + 1) * gd)
            gate = p_ref[:, 2 * w + g * gd:2 * w + (g + 1) * gd]
            sg = _sigmoid(gate)
            silu = gate * sg
            da_g = da_ref[:, cs]
            ua_g = u_act[:, cs]
            dp_ref[:, cs] = (da_g * vs[g] * silu * du_fac[:, cs]).astype(BF16)
            dp_ref[:, 2 * w + g * gd:2 * w + (g + 1) * gd] = (
                da_g * ua_g * vs[g] * (sg * (1.0 + gate * (1.0 - sg)))).astype(BF16)
            dvs = da_g * ua_g * silu
            dvs_bf = dvs.astype(BF16)
            dvn_ref[:, cs] = _dot_nn(wst_ref[g].astype(BF16), dvs_bf)
            dws = _dot_nt(dvs_bf, vn[:, cs])
            dbs = jnp.sum(dvs, axis=1, keepdims=True)

            @pl.when(i == 0)
            def _():
                dws_ref[g] = dws
                dbst_ref[:, g:g + 1] = dbs

            @pl.when(i > 0)
            def _():
                dws_ref[g] += dws
                dbst_ref[:, g:g + 1] += dbs

        dvn = dvn_ref[...]
        _acc_rows(dlg_ref, i, jnp.sum(dvn * xh, axis=0, keepdims=True))
        _acc_rows(dlb_ref, i, jnp.sum(dvn, axis=0, keepdims=True))
        dxh = dvn * lg_ref[...]
        dvf = rstd * (dxh - jnp.mean(dxh, axis=-1, keepdims=True)
                      - xh * jnp.mean(dxh * xh, axis=-1, keepdims=True))
        dp_ref[:, w:2 * w] = (dvf * dv_fac).astype(BF16)

    full = lambda *shape: pl.BlockSpec(shape, lambda i: (0,) * len(shape))
    return pl.pallas_call(
        body, name=name, grid=(nb,),
        in_specs=[pl.BlockSpec((SGU_BLOCK, w), lambda i: (i, 0)), pl.BlockSpec((SGU_BLOCK, w3), lambda i: (i, 0)),
                  full(1, w), full(1, w), full(SGU_GROUPS, SGU_BLOCK, SGU_BLOCK),
                  full(SGU_GROUPS, SGU_BLOCK, SGU_BLOCK), full(SGU_BLOCK, SGU_GROUPS)],
        out_specs=[pl.BlockSpec((SGU_BLOCK, w3), lambda i: (i, 0)), full(SGU_GROUPS, SGU_BLOCK, SGU_BLOCK),
                   full(SGU_BLOCK, SGU_GROUPS), full(1, w), full(1, w)],
        out_shape=[jax.ShapeDtypeStruct((t, w3), BF16), jax.ShapeDtypeStruct((SGU_GROUPS, SGU_BLOCK, SGU_BLOCK), F32),
                   jax.ShapeDtypeStruct((SGU_BLOCK, SGU_GROUPS), F32), jax.ShapeDtypeStruct((1, w), F32),
                   jax.ShapeDtypeStruct((1, w), F32)],
        scratch_shapes=[pltpu.VMEM((SGU_BLOCK, w), F32)],
        compiler_params=_cparams(("arbitrary",)),
    )(da, proj, ln_gain, ln_bias, ws_masked, ws_masked_t, bs_t)


def _tile2d(rows, cols, block_bytes, row_unit):
    if rows % row_unit == 0:
        return _pick(rows, max(row_unit, block_bytes // (4 * cols)), row_unit), cols
    return rows, _pick(cols, max(LANES, block_bytes // (4 * rows)))


def _adamw(w, g, m, v, *, name, block_bytes=1 << 20):
    rows, cols = w.shape
    tr, tc = _tile2d(rows, cols, block_bytes, 8)
    g_rows = g.shape[0]
    assert g_rows == rows or tr == rows

    def body(w_ref, g_ref, m_ref, v_ref, go_ref, d_ref, mo_ref, vo_ref):
        gv = g_ref[0:tr, :]
        go_ref[...] = gv
        mn = ADAM_B1 * m_ref[...] + (1.0 - ADAM_B1) * gv
        vn = ADAM_B2 * v_ref[...] + (1.0 - ADAM_B2) * (gv * gv)
        m_hat = mn / (1.0 - ADAM_B1 ** ADAM_STEP)
        v_hat = vn / (1.0 - ADAM_B2 ** ADAM_STEP)
        d_ref[...] = -ADAM_LR * (m_hat / (jnp.sqrt(v_hat) + ADAM_EPS) + ADAM_WD * w_ref[...])
        mo_ref[...] = mn
        vo_ref[...] = vn

    spec = pl.BlockSpec((tr, tc), lambda i, j: (i, j))
    g_spec = spec if g_rows == rows else pl.BlockSpec((g_rows, tc), lambda i, j: (0, j))
    return pl.pallas_call(
        body, name=name, grid=(rows // tr, cols // tc), in_specs=[spec, g_spec, spec, spec], out_specs=[spec] * 4,
        out_shape=[jax.ShapeDtypeStruct((rows, cols), F32)] * 4,
        compiler_params=_cparams(("parallel", "parallel")),
    )(w, g, m, v)


def _pair_sum_bf16(own, core_idx, peer, *, name, block_bytes=1 << 20):
    s, r, c = own.shape
    hc = c // 2
    tr, tc = _tile2d(r, hc, block_bytes, 16)
    ncb = hc // tc

    def body(h_ref, a_ref, b_ref, o_ref):
        o_ref[...] = (a_ref[...] + b_ref[...]).astype(BF16)

    grid_spec = pltpu.PrefetchScalarGridSpec(
        num_scalar_prefetch=1, grid=(s, r // tr, ncb),
        in_specs=[pl.BlockSpec((None, tr, tc), lambda j, i, k, h: (j, i, h[0] * ncb + k)),
                  pl.BlockSpec((None, tr, tc), lambda j, i, k, h: (j, i, k))],
        out_specs=pl.BlockSpec((None, tr, tc), lambda j, i, k, h: (j, i, k)))
    return pl.pallas_call(
        body, name=name, grid_spec=grid_spec, out_shape=jax.ShapeDtypeStruct((s, r, hc), BF16),
        compiler_params=_cparams(("parallel", "parallel", "parallel")),
    )(core_idx, own, peer)


def _chip_sum(pair, landed, slots, *, name, block_bytes=1 << 20):
    _, r, hc = pair.shape
    tr, tc = _tile2d(r, hc, block_bytes, 16)
    ncb = hc // tc

    def body(s_ref, own_ref, l0_ref, l1_ref, l2_ref, o_ref):
        o_ref[...] = ((own_ref[...].astype(F32) + l0_ref[...].astype(F32)) + l1_ref[...].astype(F32)
                      ) + l2_ref[...].astype(F32)

    def slab(which):
        return pl.BlockSpec((None, tr, tc), lambda i, k, s: (s[which], i, k))

    grid_spec = pltpu.PrefetchScalarGridSpec(
        num_scalar_prefetch=1, grid=(r // tr, ncb),
        in_specs=[slab(0), slab(1), slab(2), slab(3)],
        out_specs=pl.BlockSpec((tr, tc), lambda i, k, s: (i, s[4] * ncb + k)))
    return pl.pallas_call(
        body, name=name, grid_spec=grid_spec, out_shape=jax.ShapeDtypeStruct((r, 2 * hc), F32),
        compiler_params=_cparams(("parallel", "parallel")),
    )(slots, pair, landed, landed, landed)


def _stack_sum(x, *, name, out_dtype=F32, block_bytes=1 << 20):
    s, r, c = x.shape
    tr = _pick(r, max(8, block_bytes // (4 * c)), 16) if r % 16 == 0 else r

    def body(x_ref, o_ref):
        acc = x_ref[0].astype(F32)
        for j in range(1, s):
            acc = acc + x_ref[j].astype(F32)
        o_ref[...] = acc.astype(out_dtype)

    return pl.pallas_call(
        body, name=name, grid=(r // tr,),
        in_specs=[pl.BlockSpec((s, tr, c), lambda i: (0, i, 0))], out_specs=pl.BlockSpec((tr, c), lambda i: (i, 0)),
        out_shape=jax.ShapeDtypeStruct((r, c), out_dtype), compiler_params=_cparams(("parallel",)),
    )(x)


HBM = pl.BlockSpec(memory_space=pltpu.HBM)


def _place():
    x, y, c = lax.axis_index("x"), lax.axis_index("y"), lax.axis_index("c")
    other_chips = [(1 - x, y), (x, 1 - y), (1 - x, 1 - y)]
    return x, y, c, other_chips


def _half_cols(cols, which):
    hc = cols // 2
    return pl.ds(pl.multiple_of(which * hc, LANES), hc)


SEM = pl.BlockSpec(memory_space=pltpu.SEMAPHORE)
ANY = pl.BlockSpec(memory_space=pl.ANY)
SIDE_EFFECT = pltpu.SideEffectType.DATAFLOW_SIDE_EFFECTING
TOKEN_SHAPE = (8, LANES)


def _hbm(shape, dtype):
    return pltpu.HBM(shape, dtype)


def _in_hbm(a):
    return pltpu.with_memory_space_constraint(a, pltpu.HBM)


def _gather_copy(src_ref, land_ref, ssem, rsem, k, chip_of_block, to, c):
    cols = src_ref.shape[1]
    return pltpu.make_async_remote_copy(
        src_ref=src_ref.at[:, _half_cols(cols, c)], dst_ref=land_ref.at[chip_of_block, :, _half_cols(cols, c)],
        send_sem=ssem.at[k], recv_sem=rsem.at[k], device_id=to, device_id_type=MESH)


def _gather_start(shards, *, name, after=()):
    n = len(shards)
    after = list(after)

    def body(*refs):
        srcs, lands = refs[:n], refs[n:2 * n]
        outs = refs[2 * n + len(after):]
        token = outs[-1]
        x, y, c, chips = _place()
        me = 2 * x + y
        for a in range(n):
            ssem, rsem = outs[4 * a], outs[4 * a + 1]
            for k, (cx, cy) in enumerate(chips):
                _gather_copy(srcs[a], lands[a], ssem, rsem, k, me, (cx, cy, c), c).start()
        token[...] = jnp.zeros_like(token)

    out_shape, out_specs, aliases = [], [], {}
    for a, s in enumerate(shards):
        out_shape += [pltpu.SemaphoreType.DMA((3,)), pltpu.SemaphoreType.DMA((3,)), _hbm(s.shape, s.dtype),
                      _hbm((N_CHIPS,) + s.shape, s.dtype)]
        out_specs += [SEM, SEM, HBM, HBM]
        aliases[a] = 4 * a + 2
        aliases[n + a] = 4 * a + 3
    out_shape.append(jax.ShapeDtypeStruct(TOKEN_SHAPE, F32))
    out_specs.append(pl.BlockSpec(memory_space=pltpu.VMEM))
    lands = [_in_hbm(lax.empty((N_CHIPS,) + s.shape, s.dtype)) for s in shards]
    res = pl.pallas_call(
        body, name=name, in_specs=[HBM] * (2 * n) + [ANY] * len(after), out_specs=out_specs, out_shape=out_shape,
        input_output_aliases=aliases, compiler_params=pltpu.CompilerParams(has_side_effects=SIDE_EFFECT),
    )(*[_in_hbm(s) for s in shards], *lands, *after)
    return [tuple(res[4 * a:4 * a + 4]) for a in range(n)], res[-1]


def _wait_call(wait_fn, parts, after, *, name):
    ssem, rsem, src, land = parts
    after = list(after) if isinstance(after, (list, tuple)) else [after]

    def body(src_ref, land_ref, ssem_ref, rsem_ref, *rest):
        wait_fn(src_ref, land_ref, ssem_ref, rsem_ref)

    return pl.pallas_call(
        body, name=name, in_specs=[HBM, HBM, SEM, SEM] + [ANY] * len(after), out_specs=[HBM, HBM],
        out_shape=[_hbm(src.shape, src.dtype), _hbm(land.shape, land.dtype)], input_output_aliases={0: 0, 1: 1},
        compiler_params=pltpu.CompilerParams(has_side_effects=SIDE_EFFECT),
    )(src, land, ssem, rsem, *after)


def _gather_wait(parts, after, *, name):
    def wait(src_ref, land_ref, ssem_ref, rsem_ref):
        x, y, c, chips = _place()
        for k, (cx, cy) in enumerate(chips):
            cp = _gather_copy(src_ref, land_ref, ssem_ref, rsem_ref, k, 2 * cx + cy, (x, y, c), c)
            cp.wait_send()
            cp.wait_recv()

    return _wait_call(wait, parts, after, name=name)[1]


def _sibling_forward(land, *, name):
    def body(_, buf, send_sems, recv_sems):
        x, y, c, chips = _place()
        cols = buf.shape[2]
        copies = []
        for k, (cx, cy) in enumerate(chips):
            mine = buf.at[2 * cx + cy, :, _half_cols(cols, c)]
            cp = pltpu.make_async_remote_copy(
                src_ref=mine, dst_ref=mine, send_sem=send_sems.at[k], recv_sem=recv_sems.at[k],
                device_id=(x, y, 1 - c), device_id_type=MESH)
            cp.start()
            copies.append(cp)
        for k, (cx, cy) in enumerate(chips):
            theirs = buf.at[2 * cx + cy, :, _half_cols(cols, 1 - c)]
            pltpu.make_async_remote_copy(
                src_ref=theirs, dst_ref=theirs, send_sem=send_sems.at[k], recv_sem=recv_sems.at[k],
                device_id=(x, y, c), device_id_type=MESH).wait_recv()
        for cp in copies:
            cp.wait_send()

    return pl.pallas_call(
        body, name=name, in_specs=[HBM], out_specs=HBM, out_shape=jax.ShapeDtypeStruct(land.shape, land.dtype),
        input_output_aliases={0: 0},
        scratch_shapes=[pltpu.SemaphoreType.DMA((3,)), pltpu.SemaphoreType.DMA((3,))],
    )(land)


def _scatter_copy(src_ref, land_ref, ssem, rsem, k, src_slab, dst_slab, to):
    return pltpu.make_async_remote_copy(
        src_ref=src_ref.at[src_slab], dst_ref=land_ref.at[dst_slab], send_sem=ssem.at[k], recv_sem=rsem.at[k],
        device_id=to, device_id_type=MESH)


def _scatter_start(part, *, name):
    def start(src_ref, land_ref, ssem, rsem):
        x, y, c, chips = _place()
        me = 2 * x + y
        for k, (cx, cy) in enumerate(chips):
            _scatter_copy(src_ref, land_ref, ssem, rsem, k, 2 * cx + cy, me, (cx, cy, c)).start()

    return _split_start(start, part, part.shape, N_CHIPS - 1, name=name)


def _scatter_wait(parts, after, *, name):
    def wait(src_ref, land_ref, ssem_ref, rsem_ref):
        x, y, c, chips = _place()
        for k, (cx, cy) in enumerate(chips):
            idx = 2 * cx + cy
            cp = _scatter_copy(src_ref, land_ref, ssem_ref, rsem_ref, k, idx, idx, (x, y, c))
            cp.wait_send()
            cp.wait_recv()

    return _wait_call(wait, parts, after, name=name)


def _split_start(start_fn, src, land_shape, n_sems, *, name):
    def body(src_ref, land_ref, ssem, rsem, src_out, land_out, token):
        start_fn(src_ref, land_ref, ssem, rsem)
        token[...] = jnp.zeros_like(token)

    res = pl.pallas_call(
        body, name=name, in_specs=[HBM, HBM], out_specs=[SEM, SEM, HBM, HBM, pl.BlockSpec(memory_space=pltpu.VMEM)],
        out_shape=[pltpu.SemaphoreType.DMA((n_sems,)), pltpu.SemaphoreType.DMA((n_sems,)), _hbm(src.shape, src.dtype),
                   _hbm(land_shape, src.dtype), jax.ShapeDtypeStruct(TOKEN_SHAPE, F32)],
        input_output_aliases={0: 2, 1: 3}, compiler_params=pltpu.CompilerParams(has_side_effects=SIDE_EFFECT),
    )(_in_hbm(src), _in_hbm(lax.empty(land_shape, src.dtype)))
    return tuple(res[:4]), res[4]


def _swap_copy(src_ref, land_ref, ssem, rsem, which, to):
    return pltpu.make_async_remote_copy(
        src_ref=src_ref.at[:, :, _half_cols(src_ref.shape[2], which)], dst_ref=land_ref,
        send_sem=ssem.at[0], recv_sem=rsem.at[0], device_id=to, device_id_type=MESH)


def _swap_start(grad, *, name):
    def start(src_ref, land_ref, ssem, rsem):
        x, y, c, _ = _place()
        _swap_copy(src_ref, land_ref, ssem, rsem, 1 - c, (x, y, 1 - c)).start()

    s, r, cols = grad.shape
    return _split_start(start, grad, (s, r, cols // 2), 1, name=name)


def _swap_wait(parts, after, *, name):
    def wait(src_ref, land_ref, ssem_ref, rsem_ref):
        x, y, c, _ = _place()
        cp = _swap_copy(src_ref, land_ref, ssem_ref, rsem_ref, 1 - c, (x, y, c))
        cp.wait_send()
        cp.wait_recv()

    return _wait_call(wait, parts, after, name=name)


def _dev_peers(x, y, c, chips):
    return [(x, y, 1 - c)] + [(cx, cy, c) for cx, cy in chips] + [(cx, cy, 1 - c) for cx, cy in chips]


def _dev_gather_start(part, *, name):
    def start(src_ref, land_ref, ssem, rsem):
        x, y, c, chips = _place()
        for k, to in enumerate(_dev_peers(x, y, c, chips)):
            pltpu.make_async_remote_copy(
                src_ref=src_ref, dst_ref=land_ref.at[4 * x + 2 * y + c], send_sem=ssem.at[k], recv_sem=rsem.at[k],
                device_id=to, device_id_type=MESH).start()

    return _split_start(start, part, (N_DEV,) + part.shape, N_DEV - 1, name=name)


def _dev_gather_wait(parts, after, *, name):
    def wait(src_ref, land_ref, ssem_ref, rsem_ref):
        x, y, c, chips = _place()
        for k, (px, py, pc) in enumerate(_dev_peers(x, y, c, chips)):
            cp = pltpu.make_async_remote_copy(
                src_ref=src_ref, dst_ref=land_ref.at[4 * px + 2 * py + pc], send_sem=ssem_ref.at[k],
                recv_sem=rsem_ref.at[k], device_id=(x, y, c), device_id_type=MESH)
            cp.wait_send()
            cp.wait_recv()

    return _wait_call(wait, parts, after, name=name)[1]


def _sibling_share_halves(arrays, *, name):
    n = len(arrays)

    def body(*refs):
        bufs = refs[n:2 * n]
        send_sems, recv_sems = refs[2 * n:]
        x, y, c, _ = _place()
        copies = []
        for a in range(n):
            mine = bufs[a].at[:, _half_cols(bufs[a].shape[1], c)]
            cp = pltpu.make_async_remote_copy(
                src_ref=mine, dst_ref=mine, send_sem=send_sems.at[a], recv_sem=recv_sems.at[a],
                device_id=(x, y, 1 - c), device_id_type=MESH)
            cp.start()
            copies.append(cp)
        for a in range(n):
            theirs = bufs[a].at[:, _half_cols(bufs[a].shape[1], 1 - c)]
            pltpu.make_async_remote_copy(
                src_ref=theirs, dst_ref=theirs, send_sem=send_sems.at[a], recv_sem=recv_sems.at[a],
                device_id=(x, y, c), device_id_type=MESH).wait_recv()
        for cp in copies:
            cp.wait_send()

    return pl.pallas_call(
        body, name=name, in_specs=[HBM] * n, out_specs=[HBM] * n,
        out_shape=[jax.ShapeDtypeStruct(h.shape, h.dtype) for h in arrays],
        input_output_aliases={a: a for a in range(n)},
        scratch_shapes=[pltpu.SemaphoreType.DMA((n,)), pltpu.SemaphoreType.DMA((n,))],
    )(*arrays)


def _pack(arrays, rows_multiple=16, width=LANES):
    flat = jnp.concatenate([a.astype(F32).reshape(-1) for a in arrays])
    total = flat.shape[0]
    rows = -(-total // width)
    rows = -(-rows // rows_multiple) * rows_multiple
    return jnp.pad(flat, (0, rows * width - total)).reshape(rows, width)


def _unpack(buf, shapes):
    flat = buf.reshape(-1)
    out, off = [], 0
    for s in shapes:
        n = math.prod(s)
        out.append(flat[off:off + n].reshape(s))
        off += n
    return out


def kernel(x, norm_pre, norm_post, gla_w_in, gla_w_gate2, gla_b_gate, gla_o_gain, gla_w_out, sgu_w_in, sgu_ln_gain, sgu_ln_bias, sgu_w_spatial, sgu_b_spatial, sgu_w_out, loss_target, m_norm_pre, m_norm_post, m_gla_w_in, m_gla_w_gate2, m_gla_b_gate, m_gla_o_gain, m_gla_w_out, m_sgu_w_in, m_sgu_ln_gain, m_sgu_ln_bias, m_sgu_w_spatial, m_sgu_b_spatial, m_sgu_w_out, v_norm_pre, v_norm_post, v_gla_w_in, v_gla_w_gate2, v_gla_b_gate, v_gla_o_gain, v_gla_w_out, v_sgu_w_in, v_sgu_ln_gain, v_sgu_ln_bias, v_sgu_w_spatial, v_sgu_b_spatial, v_sgu_w_out):
    _, t, d = x.shape
    dk = d // 2
    ws = gla_w_in.shape[2]
    wp = -(-ws // LANES) * LANES
    lay = (ws, wp)
    chip =2 * lax.axis_index("x") + lax.axis_index("y")
    core = lax.axis_index("c")
    core_idx = core.astype(jnp.int32).reshape(1)
    others = jnp.arange(N_CHIPS - 1, dtype=jnp.int32)
    others = others + (others >= chip).astype(jnp.int32)
    slots = jnp.concatenate([chip.astype(jnp.int32).reshape(1), others, core_idx])

    x0 = x[0]
    target = loss_target[0]

    wt_in_g, mt_in_g, vt_in_g = gla_w_in[0].T, m_gla_w_in[0].T, v_gla_w_in[0].T

    small_shard = _pack([gla_w_gate2[0], sgu_ln_gain[0], sgu_ln_bias[0]], rows_multiple=8, width=2 * LANES)
    own = [small_shard, jnp.pad(wt_in_g.astype(BF16), ((0, wp - ws), (0, 0)))]
    in_flight, token = _gather_start(own, name="gather_start_a")
    own_later = [gla_w_out[0].astype(BF16), sgu_w_in[0].astype(BF16), sgu_w_out[0].astype(BF16)]
    in_flight_later, token_later = _gather_start(own_later, name="gather_start_b", after=[token])
    own, in_flight = own + own_later, in_flight + in_flight_later

    def arrived(i, after, name):
        land = _gather_wait(in_flight[i], after, name=name + "_wait")
        land = _sibling_forward(land, name=name + "_share")
        return lax.dynamic_update_slice(land, own[i][None], (chip, 0, 0))

    h0 = _norm_pre(x0, norm_pre[0:1] + token[0:1, 0:1] + token_later[0:1, 0:1], name="pre0")
    g_small = arrived(0, h0, "w_small")
    g_wi_g = arrived(1, [g_small, wt_in_g, mt_in_g, vt_in_g], "w_gla_in")
    wt_g = g_wi_g.reshape(N_CHIPS * wp, d)
    shard_shapes = [gla_w_gate2.shape[1:], sgu_ln_gain.shape[1:], sgu_ln_bias.shape[1:]]
    per_chip = [_unpack(g_small[j], shard_shapes) for j in range(N_CHIPS)]
    w2_full = jnp.concatenate([p[0] for p in per_chip], axis=1)
    ln_gain = jnp.concatenate([p[1] for p in per_chip], axis=0)[None, :]
    ln_bias = jnp.concatenate([p[2] for p in per_chip], axis=0)[None, :]
    w2p = jnp.pad(w2_full, ((0, LANES - GLA_GATE_RANK), (0, 0)))

    pos_chunk = jnp.arange(SGU_BLOCK) // CHUNK
    mask = pos_chunk[:, None] >= pos_chunk[None, :]
    ws_masked = jnp.where(mask[None], sgu_w_spatial[0], 0.0)
    ws_masked_t = ws_masked.transpose(0, 2, 1)
    bs_t = sgu_b_spatial[0].T

    proj0 = _matmul(h0, wt_g, mode="nt", out_dtype=F32, name="gla_in", tn=wp)
    o0, a0, s_before, s_final = _gla_fwd(proj0, w2p, gla_b_gate, gla_o_gain, lay, name="gla_scan")
    w_out_g = arrived(2, a0, "w_gla_out").reshape(d, d)
    y0 = _matmul(a0, w_out_g, mode="nn", out_dtype=F32, name="gla_out")
    x1, h1 = _post_then_pre(x0, y0, norm_post[0:1], norm_pre[1:2], name="post0_pre1")
    g_wi_s = arrived(3, h1, "w_sgu_in")
    proj1 = _matmul(h1, g_wi_s, mode="nn", out_dtype=F32, name="sgu_in", b_shards=True)
    a1 = _sgu_fwd(proj1, ln_gain, ln_bias, ws_masked, bs_t, name="sgu_gate")
    w_out_s = arrived(4, a1, "w_sgu_out").reshape(d, d)
    y1 = _matmul(a1, w_out_s, mode="nn", out_dtype=F32, name="sgu_out")
    loss_part, dx2, dy1, d_post1 = _loss_head(x1, y1, norm_post[1:2], target, name="loss_head")

    def behind(small, token):
        return small + token[0:1, 0:1]

    def pair_and_scatter(swap, after, name):
        grad, peer = _swap_wait(swap, after, name=name + "_swap_wait")
        pair = _pair_sum_bf16(grad, core_idx, peer, name=name + "_pair")
        return _scatter_start(pair, name=name + "_start")

    def reduced(flight, after, name):
        pair, landed = _scatter_wait(flight, after, name=name + "_wait")
        return _chip_sum(pair, landed, slots, name=name + "_sum")

    dw_out_s = _matmul(a1, dy1, mode="tn", out_dtype=F32, name="d_sgu_w_out")
    swap, tok = _swap_start(dw_out_s.reshape(N_CHIPS, d // N_CHIPS, d), name="g_sgu_out_swap")
    da1 = _matmul(dy1, w_out_s, mode="nt", out_dtype=F32, name="d_sgu_act", after=tok)
    fl_wo_s, tok = pair_and_scatter(swap, da1, "g_sgu_out")
    dproj1, d_ws, d_bs_t, d_lg, d_lb = _sgu_bwd(da1, proj1, ln_gain, behind(ln_bias, tok), ws_masked, ws_masked_t,
                                                bs_t, name="sgu_gate_bwd")
    dw_in_s = _matmul(h1, dproj1, mode="tn", out_dtype=F32, name="d_sgu_w_in", out_shards=True)
    swap, tok = _swap_start(dw_in_s, name="g_sgu_in_swap")
    dh1 = _matmul_nt_shards(dproj1, g_wi_s, out_dtype=F32, name="d_sgu_h", after=tok)
    fl_wi_s, tok = pair_and_scatter(swap, dh1, "g_sgu_in")
    dx1, dy0, d_pre1, d_post0 = _mid_bwd(dx2, dh1, x1, behind(norm_pre[1:2], tok), y0, norm_post[0:1],
                                         name="pre1_post0_bwd")
    dw_out_g = _matmul(a0, dy0, mode="tn", out_dtype=F32, name="d_gla_w_out")
    swap, tok = _swap_start(dw_out_g.reshape(N_CHIPS, d // N_CHIPS, d), name="g_gla_out_swap")
    da0 = _matmul(dy0, w_out_g, mode="nt", out_dtype=F32, name="d_gla_act", after=tok)
    fl_wo_g, tok = pair_and_scatter(swap, da0, "g_gla_out")
    dproj0, d_og, d_bg, d_w2p = _gla_bwd(da0, o0, proj0, w2p, behind(gla_b_gate, tok), gla_o_gain, s_before, s_final,
                                         lay, name="gla_scan_bwd")
    dwt_in_g = _matmul(dproj0, h0, mode="tn", out_dtype=F32, name="d_gla_w_in", tm=wp)
    swap, tok = _swap_start(dwt_in_g.reshape(N_CHIPS, wp, d), name="g_gla_in_swap")
    dh0 = _matmul(dproj0, wt_g, mode="nn", out_dtype=F32, name="d_gla_h", tk=N_CHIPS * wp, after=tok)
    fl_wi_g, tok = pair_and_scatter(swap, dh0, "g_gla_in")
    grad_x, d_pre0 = _first_bwd(dx1, dh0, x0, behind(norm_pre[0:1], tok), name="pre0_bwd")

    small_shapes = [norm_pre.shape, norm_post.shape, gla_b_gate.shape, gla_o_gain.shape, sgu_w_spatial.shape,
                    sgu_b_spatial.shape, (1, GLA_GATE_RANK, dk), (1, d), (1, d), (1, LANES)]
    d_pre = jnp.concatenate([d_pre0, d_pre1], axis=0)
    d_post = jnp.concatenate([d_post0, d_post1], axis=0)
    d_wsp = jnp.where(mask[None], d_ws, 0.0)[None]
    small_part = _pack([d_pre, d_post, d_bg, d_og, d_wsp, d_bs_t.T[None], d_w2p[:GLA_GATE_RANK][None], d_lg, d_lb,
                        loss_part])
    small_flight, tok = _dev_gather_start(small_part, name="small_grads_start")

    def big_update(w, g, m, v, name):
        return [u[None] for u in _adamw(w[0], g, m[0], v[0], name=name)]

    r_wo_s = reduced(fl_wo_s, [grad_x, tok], "g_sgu_out")
    r_wi_s = reduced(fl_wi_s, r_wo_s, "g_sgu_in")
    r_wo_g = reduced(fl_wo_g, r_wi_s, "g_gla_out")
    g_wo_sgu, g_wi_sgu, g_wo_gla = _sibling_share_halves([r_wo_s, r_wi_s, r_wo_g], name="grads_share_a")
    u_wo_sgu = big_update(sgu_w_out, g_wo_sgu, m_sgu_w_out, v_sgu_w_out, "adamw_sgu_w_out")
    u_wi_sgu = big_update(sgu_w_in, g_wi_sgu, m_sgu_w_in, v_sgu_w_in, "adamw_sgu_w_in")
    u_wo_gla = big_update(gla_w_out, g_wo_gla, m_gla_w_out, v_gla_w_out, "adamw_gla_w_out")
    r_wi_g = reduced(fl_wi_g, [u_wo_gla[1], u_wi_sgu[1], u_wo_sgu[1]], "g_gla_in")
    gt_wi_gla, = _sibling_share_halves([r_wi_g], name="grads_share_b")
    u_wi_gla = [u.T[None] for u in _adamw(wt_in_g, gt_wi_gla, mt_in_g, vt_in_g, name="adamw_gla_w_in")]

    small_land = _dev_gather_wait(small_flight, u_wi_gla[1], name="small_grads_wait")
    small_all = lax.dynamic_update_slice(small_land, small_part[None], (2 * chip + core, 0, 0))
    small_sum = _stack_sum(small_all, name="small_sum")
    (g_pre, g_post, g_bg, g_og, g_wsp, g_bsp, g_w2_full, g_lg_full, g_lb_full, loss_vec) = _unpack(small_sum, small_shapes)
    loss = loss_vec[0, 0]
    g_w2 = lax.dynamic_slice_in_dim(g_w2_full, chip * (dk // N_CHIPS), dk // N_CHIPS, axis=2)
    g_lg = lax.dynamic_slice_in_dim(g_lg_full, chip * (d // N_CHIPS), d // N_CHIPS, axis=1)
    g_lb = lax.dynamic_slice_in_dim(g_lb_full, chip * (d // N_CHIPS), d // N_CHIPS, axis=1)

    small_w = [norm_pre, norm_post, gla_b_gate, gla_o_gain, sgu_w_spatial, sgu_b_spatial, gla_w_gate2, sgu_ln_gain,
               sgu_ln_bias]
    small_g = [g_pre, g_post, g_bg, g_og, g_wsp, g_bsp, g_w2, g_lg, g_lb]
    small_m = [m_norm_pre, m_norm_post, m_gla_b_gate, m_gla_o_gain, m_sgu_w_spatial, m_sgu_b_spatial, m_gla_w_gate2,
               m_sgu_ln_gain, m_sgu_ln_bias]
    small_v = [v_norm_pre, v_norm_post, v_gla_b_gate, v_gla_o_gain, v_sgu_w_spatial, v_sgu_b_spatial, v_gla_w_gate2,
               v_sgu_ln_gain, v_sgu_ln_bias]
    own_shapes = [w.shape for w in small_w]
    _, s_dl, s_m, s_v = _adamw(_pack(small_w), _pack(small_g), _pack(small_m), _pack(small_v), name="adamw_small")
    dl_s, m_s, v_s = _unpack(s_dl, own_shapes), _unpack(s_m, own_shapes), _unpack(s_v, own_shapes)

    def ordered(small, kind):
        pre, post, bg, og, wsp, bsp, w2, lg, lb = small
        return [pre, post, u_wi_gla[kind], w2, bg, og, u_wo_gla[kind], u_wi_sgu[kind], lg, lb, wsp, bsp, u_wo_sgu[kind]]

    return (loss, grad_x[None], *ordered(small_g, 0), *ordered(dl_s, 1), *ordered(m_s, 2), *ordered(v_s, 3))
```

```python
import functools
import math

import jax
import jax.numpy as jnp
from jax import lax
from jax.experimental import pallas as pl
from jax.experimental.pallas import tpu as pltpu

F32 = jnp.float32
BF16 = jnp.bfloat16
MESH = pl.DeviceIdType.MESH

EPS = 1e-6
CHUNK = 64
GLA_HEADS = 4
GLA_GATE_RANK = 16
GLA_TAU = 16.0
SGU_BLOCK = 128
SGU_GROUPS = 8
N_CHIPS = 4
N_DEV = 8
LANES = 128

ADAM_LR = 0.001
ADAM_B1 = 0.9
ADAM_B2 = 0.999
ADAM_EPS = 1e-08
ADAM_WD = 0.01
ADAM_STEP = 10

VMEM_LIMIT = 56 * 1024 * 1024


def _cparams(sem=None):
    return pltpu.CompilerParams(dimension_semantics=sem, vmem_limit_bytes=VMEM_LIMIT)


def _pick(n, cap, unit=LANES):
    best = None
    for t in range(unit, min(n, cap) + 1, unit):
        if n % t == 0:
            best = t
    assert best is not None, (n, cap, unit)
    return best


def _dot(a, b, dims):
    return lax.dot_general(a, b, (dims, ((), ())), preferred_element_type=F32)


def _dot_nn(a, b):
    return _dot(a, b, ((1,), (0,)))


def _dot_nt(a, b):
    return _dot(a, b, ((1,), (1,)))


def _dot_tn(a, b):
    return _dot(a, b, ((0,), (0,)))


def _matmul(a, b, *, mode, out_dtype, name, tm=1024, tn=512, tk=2048, b_shards=False, out_shards=False, after=None,
            out_rows=None):
    if mode == "tn":
        K, M = a.shape
    else:
        M, K = a.shape
    if b_shards:
        ns, br, bc = b.shape
        if mode == "nt":
            N, Kb = br, ns * bc
        else:
            Kb, N = br, ns * bc
    else:
        if mode == "nt":
            N, Kb = b.shape
        else:
            Kb, N = b.shape
    assert K == Kb, (a.shape, b.shape, mode)
    tm = _pick(M, tm)
    tk = _pick(K, tk)
    if b_shards and mode != "nt":
        tn = _pick(bc, tn)
    elif out_shards:
        tn = _pick(N // N_CHIPS, tn)
    else:
        tn = _pick(N, tn)
    if b_shards and mode == "nt":
        tk = _pick(bc, tk)
    nk = K // tk
    grid = (M // tm, N // tn, nk)

    if mode == "tn":
        a_spec = pl.BlockSpec((tk, tm), lambda i, j, k: (k, i))
    else:
        a_spec = pl.BlockSpec((tm, tk), lambda i, j, k: (i, k))
    if b_shards:
        if mode == "nt":
            per = bc // tk
            b_spec = pl.BlockSpec((None, tn, tk), lambda i, j, k: (k // per, j, k % per))
        else:
            per = bc // tn
            b_spec = pl.BlockSpec((None, tk, tn), lambda i, j, k: (j // per, k, j % per))
    elif mode == "nt":
        b_spec = pl.BlockSpec((tn, tk), lambda i, j, k: (j, k))
    else:
        b_spec = pl.BlockSpec((tk, tn), lambda i, j, k: (k, j))
    if out_shards:
        per_o = (N // N_CHIPS) // tn
        out_spec = pl.BlockSpec((None, tm, tn), lambda i, j, k: (j // per_o, i, j % per_o))
        out_shape = jax.ShapeDtypeStruct((N_CHIPS, M, N // N_CHIPS), out_dtype)
    else:
        out_spec = pl.BlockSpec((tm, tn), lambda i, j, k: (i, j))
        out_shape = jax.ShapeDtypeStruct((M if out_rows is None else out_rows, N), out_dtype)

    dims = {"nn": ((1,), (0,)), "nt": ((1,), (1,)), "tn": ((0,), (0,))}[mode]

    def body(a_ref, b_ref, *rest):
        o_ref, scratch = (rest[1], rest[2:]) if after is not None else (rest[0], rest[1:])
        part = _dot(a_ref[...].astype(BF16), b_ref[...].astype(BF16), dims)
        if nk == 1:
            o_ref[...] = part.astype(out_dtype)
        else:
            acc_ref, = scratch
            k = pl.program_id(2)

            @pl.when(k == 0)
            def _():
                acc_ref[...] = part

            @pl.when(k > 0)
            def _():
                acc_ref[...] += part

            @pl.when(k == nk - 1)
            def _():
                o_ref[...] = acc_ref[...].astype(out_dtype)

    extra_specs, extra_args = ([], []) if after is None else ([pl.BlockSpec(memory_space=pl.ANY)], [after])
    return pl.pallas_call(
        body, name=name, grid=grid, in_specs=[a_spec, b_spec] + extra_specs, out_specs=out_spec, out_shape=out_shape,
        scratch_shapes=[] if nk == 1 else [pltpu.VMEM((tm, tn), F32)],
        compiler_params=_cparams(("parallel", "parallel", "arbitrary")),
    )(a, b, *extra_args)


def _matmul_into_cols(a, w, which, buf, *, name, tm=1024):
    M, K = a.shape
    _, N, _ = w.shape
    tm = _pick(M, tm)

    def body(which_ref, a_ref, w_ref, buf_ref, o_ref):
        o_ref[...] = _dot_nt(a_ref[...], w_ref[...])

    grid_spec = pltpu.PrefetchScalarGridSpec(
        num_scalar_prefetch=1, grid=(M // tm,),
        in_specs=[pl.BlockSpec((tm, K), lambda i, s: (i, 0)), pl.BlockSpec((None, N, K), lambda i, s: (s[1], 0, 0)),
                  pl.BlockSpec(memory_space=pl.ANY)],
        out_specs=pl.BlockSpec((tm, N), lambda i, s: (i, s[0])))
    return pl.pallas_call(
        body, name=name, grid_spec=grid_spec, out_shape=jax.ShapeDtypeStruct(buf.shape, buf.dtype),
        input_output_aliases={3: 0}, compiler_params=_cparams(("parallel",)),
    )(which, a, w, buf)


def _matmul_nt_shards(a, b, *, out_dtype, name, tm=1024, tn=512, after=None):
    M, K = a.shape
    ns, N, kc = b.shape
    assert K == ns * kc
    tm, tn = _pick(M, tm), _pick(N, tn)

    def body(a_ref, *rest):
        b_refs, o_ref = rest[:ns], rest[ns + (after is not None)]
        acc = _dot_nt(a_ref[:, 0:kc], b_refs[0][...])
        for j in range(1, ns):
            acc += _dot_nt(a_ref[:, j * kc:(j + 1) * kc], b_refs[j][...])
        o_ref[...] = acc.astype(out_dtype)

    def shard(j):
        return pl.BlockSpec((None, tn, kc), lambda i, n: (j, n, 0))

    extra_specs, extra_args = ([], []) if after is None else ([pl.BlockSpec(memory_space=pl.ANY)], [after])
    return pl.pallas_call(
        body, name=name, grid=(M // tm, N // tn),
        in_specs=[pl.BlockSpec((tm, K), lambda i, n: (i, 0))] + [shard(j) for j in range(ns)] + extra_specs,
        out_specs=pl.BlockSpec((tm, tn), lambda i, n: (i, n)), out_shape=jax.ShapeDtypeStruct((M, N), out_dtype),
        compiler_params=_cparams(("parallel", "parallel")),
    )(a, *([b] * ns), *extra_args)


def _rstd(x):
    return lax.rsqrt(jnp.mean(x * x, axis=-1, keepdims=True) + EPS)


def _row_spec(tr, d):
    return pl.BlockSpec((tr, d), lambda i: (i, 0))


def _vec_spec(d):
    return pl.BlockSpec((1, d), lambda i: (0, 0))


def _acc_rows(ref, i, val, cols=slice(None)):
    @pl.when(i == 0)
    def _():
        ref[:, cols] = val

    @pl.when(i > 0)
    def _():
        ref[:, cols] += val


def _norm_pre(x, gain, *, name, tr=256):
    t, d = x.shape
    tr = _pick(t, tr, 8)

    def body(x_ref, g_ref, h_ref):
        xv = x_ref[...]
        h_ref[...] = (xv * _rstd(xv) * g_ref[...]).astype(BF16)

    return pl.pallas_call(
        body, name=name, grid=(t // tr,), in_specs=[_row_spec(tr, d), _vec_spec(d)], out_specs=_row_spec(tr, d),
        out_shape=jax.ShapeDtypeStruct((t, d), BF16), compiler_params=_cparams(("parallel",)),
    )(x, gain)


def _post_then_pre(x, y, post_gain, pre_gain, *, name, tr=256):
    t, d = x.shape
    tr = _pick(t, tr, 8)

    def body(x_ref, y_ref, pg_ref, ng_ref, xn_ref, h_ref):
        yv = y_ref[...]
        xn = x_ref[...] + yv * _rstd(yv) * pg_ref[...]
        xn_ref[...] = xn
        h_ref[...] = (xn * _rstd(xn) * ng_ref[...]).astype(BF16)

    return pl.pallas_call(
        body, name=name, grid=(t // tr,),
        in_specs=[_row_spec(tr, d), _row_spec(tr, d), _vec_spec(d), _vec_spec(d)],
        out_specs=[_row_spec(tr, d), _row_spec(tr, d)],
        out_shape=[jax.ShapeDtypeStruct((t, d), F32), jax.ShapeDtypeStruct((t, d), BF16)],
        compiler_params=_cparams(("parallel",)),
    )(x, y, post_gain, pre_gain)


def _norm_bwd(dy, n, r, gain):
    dn = dy * gain
    return r * (dn - n * jnp.mean(dn * n, axis=-1, keepdims=True))


def _loss_head(x, y, post_gain, target, *, name, tr=256):
    t, d = x.shape
    tr = _pick(t, tr, 8)

    def body(x_ref, y_ref, pg_ref, t_ref, loss_ref, dx_ref, dy_ref, dpg_ref):
        i = pl.program_id(0)
        yv = y_ref[...]
        r = _rstd(yv)
        n = yv * r
        err = x_ref[...] + n * pg_ref[...] - t_ref[...]
        dx = err * (1.0 / d)
        dx_ref[...] = dx
        part = 0.5 * jnp.sum(jnp.mean(err * err, axis=-1, keepdims=True), axis=0, keepdims=True)
        _acc_rows(loss_ref, i, jnp.broadcast_to(part, (1, LANES)))
        _acc_rows(dpg_ref, i, jnp.sum(dx * n, axis=0, keepdims=True))
        dy_ref[...] = _norm_bwd(dx, n, r, pg_ref[...]).astype(BF16)

    return pl.pallas_call(
        body, name=name, grid=(t // tr,),
        in_specs=[_row_spec(tr, d), _row_spec(tr, d), _vec_spec(d), _row_spec(tr, d)],
        out_specs=[_vec_spec(LANES), _row_spec(tr, d), _row_spec(tr, d), _vec_spec(d)],
        out_shape=[jax.ShapeDtypeStruct((1, LANES), F32), jax.ShapeDtypeStruct((t, d), F32),
                   jax.ShapeDtypeStruct((t, d), BF16), jax.ShapeDtypeStruct((1, d), F32)],
        compiler_params=_cparams(("arbitrary",)),
    )(x, y, post_gain, target)


def _mid_bwd(dx_out, dh, x, pre_gain, y_prev, post_gain_prev, *, name, tr=256):
    t, d = x.shape
    tr = _pick(t, tr, 8)

    def body(dxo_ref, dh_ref, x_ref, ng_ref, y_ref, pg_ref, dx_ref, dy_ref, dng_ref, dpg_ref):
        i = pl.program_id(0)
        xv = x_ref[...]
        r = _rstd(xv)
        xh = xv * r
        dhv = dh_ref[...]
        _acc_rows(dng_ref, i, jnp.sum(dhv * xh, axis=0, keepdims=True))
        dx = dxo_ref[...] + _norm_bwd(dhv, xh, r, ng_ref[...])
        dx_ref[...] = dx
        yv = y_ref[...]
        ry = _rstd(yv)
        n = yv * ry
        _acc_rows(dpg_ref, i, jnp.sum(dx * n, axis=0, keepdims=True))
        dy_ref[...] = _norm_bwd(dx, n, ry, pg_ref[...]).astype(BF16)

    return pl.pallas_call(
        body, name=name, grid=(t // tr,),
        in_specs=[_row_spec(tr, d), _row_spec(tr, d), _row_spec(tr, d), _vec_spec(d), _row_spec(tr, d), _vec_spec(d)],
        out_specs=[_row_spec(tr, d), _row_spec(tr, d), _vec_spec(d), _vec_spec(d)],
        out_shape=[jax.ShapeDtypeStruct((t, d), F32), jax.ShapeDtypeStruct((t, d), BF16),
                   jax.ShapeDtypeStruct((1, d), F32), jax.ShapeDtypeStruct((1, d), F32)],
        compiler_params=_cparams(("arbitrary",)),
    )(dx_out, dh, x, pre_gain, y_prev, post_gain_prev)


def _first_bwd(dx_out, dh, x, pre_gain, *, name, tr=256):
    t, d = x.shape
    tr = _pick(t, tr, 8)

    def body(dxo_ref, dh_ref, x_ref, ng_ref, dx_ref, dng_ref):
        i = pl.program_id(0)
        xv = x_ref[...]
        r = _rstd(xv)
        xh = xv * r
        dhv = dh_ref[...]
        _acc_rows(dng_ref, i, jnp.sum(dhv * xh, axis=0, keepdims=True))
        dx_ref[...] = dxo_ref[...] + _norm_bwd(dhv, xh, r, ng_ref[...])

    return pl.pallas_call(
        body, name=name, grid=(t // tr,),
        in_specs=[_row_spec(tr, d), _row_spec(tr, d), _row_spec(tr, d), _vec_spec(d)],
        out_specs=[_row_spec(tr, d), _vec_spec(d)],
        out_shape=[jax.ShapeDtypeStruct((t, d), F32), jax.ShapeDtypeStruct((1, d), F32)],
        compiler_params=_cparams(("arbitrary",)),
    )(dx_out, dh, x, pre_gain)


def _sigmoid(x):
    return 1.0 / (1.0 + jnp.exp(-x))


def _log_sigmoid(x):
    return jnp.minimum(x, 0.0) - jnp.log(1.0 + jnp.exp(-jnp.abs(x)))


_GELU_C = math.sqrt(2.0 / math.pi)


def _gelu_parts(x):
    x2 = x * x
    th = jnp.tanh(_GELU_C * (x + 0.044715 * x * x2))
    val = 0.5 * x * (1.0 + th)
    grad = 0.5 * (1.0 + th) + 0.5 * x * (1.0 - th * th) * (_GELU_C * (1.0 + 3.0 * 0.044715 * x2))
    return val, grad


def _split3(x):
    hi = x.astype(BF16)
    r1 = x - hi.astype(F32)
    mid = r1.astype(BF16)
    lo = (r1 - mid.astype(F32)).astype(BF16)
    return hi, mid, lo


def _tri_matmul(tri_bf16, x):
    hi, mid, lo = _split3(x)
    return _dot_nn(tri_bf16, hi) + _dot_nn(tri_bf16, mid) + _dot_nn(tri_bf16, lo)


def _gla_dims(d):
    dk, dv = d // 2, d
    return dk, dv, dk // GLA_HEADS, dv // GLA_HEADS


def _col_pieces(a, b, lay):
    ws, wp = lay
    out = []
    while a < b:
        j = a // ws
        end = min(b, (j + 1) * ws)
        out.append((j * wp + a - j * ws, end - a))
        a = end
    return out


def _load_cols(ref, a, b, lay):
    parts = [ref[:, s:s + n] for s, n in _col_pieces(a, b, lay)]
    return parts[0] if len(parts) == 1 else jnp.concatenate(parts, axis=1)


def _store_cols(ref, a, val, lay):
    off = 0
    for s, n in _col_pieces(a, a + val.shape[1], lay):
        ref[:, s:s + n] = val[:, off:off + n]
        off += n


def _gate_window(c_r, lay):
    (start, _), = _col_pieces(c_r, c_r + GLA_GATE_RANK, lay)
    assert (start % lay[1]) + LANES <= lay[1]
    return slice(start, start + LANES)


def _gla_gates(glr, k, w2_ref, b_ref):
    z = _dot_nn(glr.astype(BF16), w2_ref[...].astype(BF16)) + b_ref[...]
    la = _log_sigmoid(z) * (1.0 / GLA_TAU)
    row = lax.broadcasted_iota(jnp.int32, (CHUNK, CHUNK), 0)
    col = lax.broadcasted_iota(jnp.int32, (CHUNK, CHUNK), 1)
    incl = (row >= col).astype(BF16)
    bcum = _tri_matmul(incl, la)
    b_end = bcum[CHUNK - 1:CHUNK, :]
    e_rest = jnp.exp(b_end - bcum)
    return z, e_rest, k * e_rest, jnp.exp(b_end)


def _gla_fwd(proj, w2p, b_gate, o_gain, lay, *, name):
    t, wcols = proj.shape
    d = o_gain.shape[1]
    dk, dv, dkh, dvh = _gla_dims(d)
    nc = t // CHUNK
    c_k, c_v, c_g, c_r = dk, 2 * dk, 2 * dk + dv, 2 * dk + 2 * dv
    scale = dkh ** -0.5

    def body(p_ref, w2_ref, b_ref, og_ref, o_ref, a_ref, sb_ref, sfin_ref, s_ref):
        i = pl.program_id(0)

        @pl.when(i == 0)
        def _():
            s_ref[...] = jnp.zeros_like(s_ref)

        q = _load_cols(p_ref, 0, dk, lay) * scale
        k = _load_cols(p_ref, c_k, c_k + dk, lay)
        glr = p_ref[:, _gate_window(c_r, lay)]
        _, _, kdec, decay = _gla_gates(glr, k, w2_ref, b_ref)
        for h in range(GLA_HEADS):
            ks = slice(h * dkh, (h + 1) * dkh)
            vs = slice(h * dvh, (h + 1) * dvh)
            v_h = _load_cols(p_ref, c_v + h * dvh, c_v + (h + 1) * dvh, lay)
            g_h = _load_cols(p_ref, c_g + h * dvh, c_g + (h + 1) * dvh, lay)
            s_old = s_ref[h]
            sb_ref[0, h] = s_old
            s_new = s_old * decay[:, ks] + _dot_tn(v_h.astype(BF16), kdec[:, ks].astype(BF16))
            s_ref[h] = s_new
            o_h = _dot_nt(q[:, ks].astype(BF16), s_new.astype(BF16))
            o_ref[:, vs] = o_h
            on = o_h * _rstd(o_h)
            a_ref[:, vs] = (on * og_ref[:, vs] * (g_h * _sigmoid(g_h))).astype(BF16)

        @pl.when(i == nc - 1)
        def _():
            sfin_ref[...] = s_ref[...]

    full = lambda *shape: pl.BlockSpec(shape, lambda i: (0,) * len(shape))
    return pl.pallas_call(
        body, name=name, grid=(nc,),
        in_specs=[pl.BlockSpec((CHUNK, wcols), lambda i: (i, 0)), full(LANES, dk), full(1, dk), full(1, dv)],
        out_specs=[pl.BlockSpec((CHUNK, dv), lambda i: (i, 0)), pl.BlockSpec((CHUNK, dv), lambda i: (i, 0)),
                   pl.BlockSpec((1, GLA_HEADS, dvh, dkh), lambda i: (i, 0, 0, 0)), full(GLA_HEADS, dvh, dkh)],
        out_shape=[jax.ShapeDtypeStruct((t, dv), F32), jax.ShapeDtypeStruct((t, dv), BF16),
                   jax.ShapeDtypeStruct((nc, GLA_HEADS, dvh, dkh), F32),
                   jax.ShapeDtypeStruct((GLA_HEADS, dvh, dkh), F32)],
        scratch_shapes=[pltpu.VMEM((GLA_HEADS, dvh, dkh), F32)],
        compiler_params=_cparams(("arbitrary",)),
    )(proj, w2p, b_gate, o_gain)


def _gla_bwd(da, o, proj, w2p, b_gate, o_gain, s_before, s_final, lay, *, name):
    t, wcols = proj.shape
    d = o_gain.shape[1]
    dk, dv, dkh, dvh = _gla_dims(d)
    nc = t // CHUNK
    c_k, c_v, c_g, c_r = dk, 2 * dk, 2 * dk + dv, 2 * dk + 2 * dv
    scale = dkh ** -0.5

    def body(da_ref, o_ref, p_ref, w2_ref, b_ref, og_ref, sb_ref, sfin_ref,
             dp_ref, dog_ref, db_ref, dw2_ref, s_ref, gc_ref, dkd_ref):
        i = pl.program_id(0)

        @pl.when(i == 0)
        def _():
            s_ref[...] = sfin_ref[...]
            gc_ref[...] = jnp.zeros_like(gc_ref)

        ws, wp = lay
        for j in range(N_CHIPS):
            dp_ref[:, j * wp + ws:(j + 1) * wp] = jnp.zeros((CHUNK, wp - ws), BF16)
        q = _load_cols(p_ref, 0, dk, lay) * scale
        k = _load_cols(p_ref, c_k, c_k + dk, lay)
        glr = p_ref[:, _gate_window(c_r, lay)]
        z, e_rest, kdec, decay = _gla_gates(glr, k, w2_ref, b_ref)
        ddecay = []
        for h in range(GLA_HEADS):
            ks = slice(h * dkh, (h + 1) * dkh)
            vs = slice(h * dvh, (h + 1) * dvh)
            v_h = _load_cols(p_ref, c_v + h * dvh, c_v + (h + 1) * dvh, lay)
            g_h = _load_cols(p_ref, c_g + h * dvh, c_g + (h + 1) * dvh, lay)
            da_h = da_ref[:, vs]
            o_h = o_ref[:, vs]
            og_h = og_ref[:, vs]
            r = _rstd(o_h)
            on = o_h * r
            sg = _sigmoid(g_h)
            silu = g_h * sg
            _acc_rows(dog_ref, i, jnp.sum(da_h * silu * on, axis=0, keepdims=True), vs)
            _store_cols(dp_ref, c_g + h * dvh, (da_h * (on * og_h) * (sg * (1.0 + g_h * (1.0 - sg)))).astype(BF16),
                        lay)
            don = da_h * silu * og_h
            do_h = (r * (don - on * jnp.mean(don * on, axis=-1, keepdims=True))).astype(BF16)
            s_cur = s_ref[h]
            _store_cols(dp_ref, h * dkh, (_dot_nn(do_h, s_cur.astype(BF16)) * scale).astype(BF16), lay)
            g_tot = gc_ref[h] + _dot_tn(do_h, q[:, ks].astype(BF16))
            g_bf = g_tot.astype(BF16)
            dkd_ref[:, ks] = _dot_nn(v_h.astype(BF16), g_bf)
            _store_cols(dp_ref, c_v + h * dvh, _dot_nt(kdec[:, ks].astype(BF16), g_bf).astype(BF16), lay)
            s_prev = sb_ref[0, h]
            ddecay.append(jnp.sum(g_tot * s_prev, axis=0, keepdims=True))
            gc_ref[h] = g_tot * decay[:, ks]
            s_ref[h] = s_prev
        dkdec = dkd_ref[...]
        _store_cols(dp_ref, c_k, (dkdec * e_rest).astype(BF16), lay)
        d_e = dkdec * kdec
        row = lax.broadcasted_iota(jnp.int32, (CHUNK, CHUNK), 0)
        col = lax.broadcasted_iota(jnp.int32, (CHUNK, CHUNK), 1)
        excl = (row > col).astype(BF16)
        dla = jnp.concatenate(ddecay, axis=1) * decay + _tri_matmul(excl, d_e)
        dz = dla * (1.0 / GLA_TAU) * (1.0 - _sigmoid(z))
        _acc_rows(db_ref, i, jnp.sum(dz, axis=0, keepdims=True))
        dz_bf = dz.astype(BF16)
        dw2 = _dot_tn(glr.astype(BF16), dz_bf)

        @pl.when(i == 0)
        def _():
            dw2_ref[...] = dw2

        @pl.when(i > 0)
        def _():
            dw2_ref[...] += dw2

        dp_ref[:, _gate_window(c_r, lay)] = _dot_nt(dz_bf, w2_ref[...].astype(BF16)).astype(BF16)

    rev = lambda i: (nc - 1 - i, 0)
    full = lambda *shape: pl.BlockSpec(shape, lambda i: (0,) * len(shape))
    return pl.pallas_call(
        body, name=name, grid=(nc,),
        in_specs=[pl.BlockSpec((CHUNK, dv), rev), pl.BlockSpec((CHUNK, dv), rev), pl.BlockSpec((CHUNK, wcols), rev),
                  full(LANES, dk), full(1, dk), full(1, dv),
                  pl.BlockSpec((1, GLA_HEADS, dvh, dkh), lambda i: (nc - 1 - i, 0, 0, 0)), full(GLA_HEADS, dvh, dkh)],
        out_specs=[pl.BlockSpec((CHUNK, wcols), rev), full(1, dv), full(1, dk), full(LANES, dk)],
        out_shape=[jax.ShapeDtypeStruct((t, wcols), BF16), jax.ShapeDtypeStruct((1, dv), F32),
                   jax.ShapeDtypeStruct((1, dk), F32), jax.ShapeDtypeStruct((LANES, dk), F32)],
        scratch_shapes=[pltpu.VMEM((GLA_HEADS, dvh, dkh), F32), pltpu.VMEM((GLA_HEADS, dvh, dkh), F32),
                        pltpu.VMEM((CHUNK, dk), F32)],
        compiler_params=_cparams(("arbitrary",)),
    )(da, o, proj, w2p, b_gate, o_gain, s_before, s_final)


def _sgu_mid(p_ref, lg_ref, lb_ref, ws_ref, bst_ref, w):
    gd = w // SGU_GROUPS
    u_act, du_fac = _gelu_parts(p_ref[:, 0:w])
    vf, dv_fac = _gelu_parts(p_ref[:, w:2 * w])
    mu = jnp.mean(vf, axis=-1, keepdims=True)
    cen = vf - mu
    rstd = lax.rsqrt(jnp.mean(cen * cen, axis=-1, keepdims=True) + EPS)
    xh = cen * rstd
    vn = (xh * lg_ref[...] + lb_ref[...]).astype(BF16)
    vs = [_dot_nn(ws_ref[g].astype(BF16), vn[:, g * gd:(g + 1) * gd]) + bst_ref[:, g:g + 1]
          for g in range(SGU_GROUPS)]
    return u_act, du_fac, dv_fac, rstd, xh, vn, vs


def _sgu_fwd(proj, ln_gain, ln_bias, ws_masked, bs_t, *, name):
    t, w3 = proj.shape
    w = w3 // 3
    gd = w // SGU_GROUPS
    nb = t // SGU_BLOCK

    def body(p_ref, lg_ref, lb_ref, ws_ref, bst_ref, a_ref):
        u_act, _, _, _, _, _, vs = _sgu_mid(p_ref, lg_ref, lb_ref, ws_ref, bst_ref, w)
        for g in range(SGU_GROUPS):
            cs = slice(g * gd, (g + 1) * gd)
            gate = p_ref[:, 2 * w + g * gd:2 * w + (g + 1) * gd]
            a_ref[:, cs] = (u_act[:, cs] * vs[g] * (gate * _sigmoid(gate))).astype(BF16)

    full = lambda *shape: pl.BlockSpec(shape, lambda i: (0,) * len(shape))
    return pl.pallas_call(
        body, name=name, grid=(nb,),
        in_specs=[pl.BlockSpec((SGU_BLOCK, w3), lambda i: (i, 0)), full(1, w), full(1, w),
                  full(SGU_GROUPS, SGU_BLOCK, SGU_BLOCK), full(SGU_BLOCK, SGU_GROUPS)],
        out_specs=pl.BlockSpec((SGU_BLOCK, w), lambda i: (i, 0)),
        out_shape=jax.ShapeDtypeStruct((t, w), BF16),
        compiler_params=_cparams(("parallel",)),
    )(proj, ln_gain, ln_bias, ws_masked, bs_t)


def _sgu_bwd(da, proj, ln_gain, ln_bias, ws_masked, ws_masked_t, bs_t, *, name):
    t, w3 = proj.shape
    w = w3 // 3
    gd = w // SGU_GROUPS
    nb = t // SGU_BLOCK

    def body(da_ref, p_ref, lg_ref, lb_ref, ws_ref, wst_ref, bst_ref, dp_ref, dws_ref, dbst_ref, dlg_ref, dlb_ref,
             dvn_ref):
        i = pl.program_id(0)
        u_act, du_fac, dv_fac, rstd, xh, vn, vs = _sgu_mid(p_ref, lg_ref, lb_ref, ws_ref, bst_ref, w)
        for g in range(SGU_GROUPS):
            cs = slice(g * gd, (g + 1) * gd)
            gate = p_ref[:, 2 * w + g * gd:2 * w + (g + 1) * gd]
            sg = _sigmoid(gate)
            silu = gate * sg
            da_g = da_ref[:, cs]
            ua_g = u_act[:, cs]
            dp_ref[:, cs] = (da_g * vs[g] * silu * du_fac[:, cs]).astype(BF16)
            dp_ref[:, 2 * w + g * gd:2 * w + (g + 1) * gd] = (
                da_g * ua_g * vs[g] * (sg * (1.0 + gate * (1.0 - sg)))).astype(BF16)
            dvs = da_g * ua_g * silu
            dvs_bf = dvs.astype(BF16)
            dvn_ref[:, cs] = _dot_nn(wst_ref[g].astype(BF16), dvs_bf)
            dws = _dot_nt(dvs_bf, vn[:, cs])
            dbs = jnp.sum(dvs, axis=1, keepdims=True)

            @pl.when(i == 0)
            def _():
                dws_ref[g] = dws
                dbst_ref[:, g:g + 1] = dbs

            @pl.when(i > 0)
            def _():
                dws_ref[g] += dws
                dbst_ref[:, g:g + 1] += dbs

        dvn = dvn_ref[...]
        _acc_rows(dlg_ref, i, jnp.sum(dvn * xh, axis=0, keepdims=True))
        _acc_rows(dlb_ref, i, jnp.sum(dvn, axis=0, keepdims=True))
        dxh = dvn * lg_ref[...]
        dvf = rstd * (dxh - jnp.mean(dxh, axis=-1, keepdims=True)
                      - xh * jnp.mean(dxh * xh, axis=-1, keepdims=True))
        dp_ref[:, w:2 * w] = (dvf * dv_fac).astype(BF16)

    full = lambda *shape: pl.BlockSpec(shape, lambda i: (0,) * len(shape))
    return pl.pallas_call(
        body, name=name, grid=(nb,),
        in_specs=[pl.BlockSpec((SGU_BLOCK, w), lambda i: (i, 0)), pl.BlockSpec((SGU_BLOCK, w3), lambda i: (i, 0)),
                  full(1, w), full(1, w), full(SGU_GROUPS, SGU_BLOCK, SGU_BLOCK),
                  full(SGU_GROUPS, SGU_BLOCK, SGU_BLOCK), full(SGU_BLOCK, SGU_GROUPS)],
        out_specs=[pl.BlockSpec((SGU_BLOCK, w3), lambda i: (i, 0)), full(SGU_GROUPS, SGU_BLOCK, SGU_BLOCK),
                   full(SGU_BLOCK, SGU_GROUPS), full(1, w), full(1, w)],
        out_shape=[jax.ShapeDtypeStruct((t, w3), BF16), jax.ShapeDtypeStruct((SGU_GROUPS, SGU_BLOCK, SGU_BLOCK), F32),
                   jax.ShapeDtypeStruct((SGU_BLOCK, SGU_GROUPS), F32), jax.ShapeDtypeStruct((1, w), F32),
                   jax.ShapeDtypeStruct((1, w), F32)],
        scratch_shapes=[pltpu.VMEM((SGU_BLOCK, w), F32)],
        compiler_params=_cparams(("arbitrary",)),
    )(da, proj, ln_gain, ln_bias, ws_masked, ws_masked_t, bs_t)


def _tile2d(rows, cols, block_bytes, row_unit):
    if rows % row_unit == 0:
        return _pick(rows, max(row_unit, block_bytes // (4 * cols)), row_unit), cols
    return rows, _pick(cols, max(LANES, block_bytes // (4 * rows)))


def _adamw(w, g, m, v, *, name, block_bytes=1 << 20, after=None):
    rows, cols = w.shape
    tr, tc = _tile2d(rows, cols, block_bytes, 8)
    g_rows = g.shape[0]
    assert g_rows == rows or tr == rows
    extra_specs, extra_args = ([], []) if after is None else ([pl.BlockSpec(memory_space=pl.ANY)], [after])

    def body(w_ref, g_ref, m_ref, v_ref, *rest):
        go_ref, d_ref, mo_ref, vo_ref = rest[len(extra_args):]
        gv = g_ref[0:tr, :]
        go_ref[...] = gv
        mn = ADAM_B1 * m_ref[...] + (1.0 - ADAM_B1) * gv
        vn = ADAM_B2 * v_ref[...] + (1.0 - ADAM_B2) * (gv * gv)
        m_hat = mn / (1.0 - ADAM_B1 ** ADAM_STEP)
        v_hat = vn / (1.0 - ADAM_B2 ** ADAM_STEP)
        d_ref[...] = -ADAM_LR * (m_hat / (jnp.sqrt(v_hat) + ADAM_EPS) + ADAM_WD * w_ref[...])
        mo_ref[...] = mn
        vo_ref[...] = vn

    spec = pl.BlockSpec((tr, tc), lambda i, j: (i, j))
    g_spec = spec if g_rows == rows else pl.BlockSpec((g_rows, tc), lambda i, j: (0, j))
    return pl.pallas_call(
        body, name=name, grid=(rows // tr, cols // tc), in_specs=[spec, g_spec, spec, spec] + extra_specs,
        out_specs=[spec] * 4, out_shape=[jax.ShapeDtypeStruct((rows, cols), F32)] * 4,
        compiler_params=_cparams(("parallel", "parallel")),
    )(w, g, m, v, *extra_args)


def _pair_sum_bf16(own, core_idx, peer, *, name, block_bytes=1 << 20):
    s, r, c = own.shape
    hc = c // 2
    tr, tc = _tile2d(r, hc, block_bytes, 16)
    ncb = hc // tc

    def body(h_ref, a_ref, b_ref, o_ref):
        o_ref[...] = (a_ref[...] + b_ref[...]).astype(BF16)

    grid_spec = pltpu.PrefetchScalarGridSpec(
        num_scalar_prefetch=1, grid=(s, r // tr, ncb),
        in_specs=[pl.BlockSpec((None, tr, tc), lambda j, i, k, h: (j, i, h[0] * ncb + k)),
                  pl.BlockSpec((None, tr, tc), lambda j, i, k, h: (j, i, k))],
        out_specs=pl.BlockSpec((None, tr, tc), lambda j, i, k, h: (j, i, k)))
    return pl.pallas_call(
        body, name=name, grid_spec=grid_spec, out_shape=jax.ShapeDtypeStruct((s, r, hc), BF16),
        compiler_params=_cparams(("parallel", "parallel", "parallel")),
    )(core_idx, own, peer)


def _chip_sum(pair, landed, slots, *, name, block_bytes=1 << 20):
    _, r, hc = pair.shape
    tr, tc = _tile2d(r, hc, block_bytes, 16)
    ncb = hc // tc

    def body(s_ref, own_ref, l0_ref, l1_ref, l2_ref, o_ref):
        o_ref[...] = ((own_ref[...].astype(F32) + l0_ref[...].astype(F32)) + l1_ref[...].astype(F32)
                      ) + l2_ref[...].astype(F32)

    def slab(which):
        return pl.BlockSpec((None, tr, tc), lambda i, k, s: (s[which], i, k))

    grid_spec = pltpu.PrefetchScalarGridSpec(
        num_scalar_prefetch=1, grid=(r // tr, ncb),
        in_specs=[slab(0), slab(1), slab(2), slab(3)],
        out_specs=pl.BlockSpec((tr, tc), lambda i, k, s: (i, s[4] * ncb + k)))
    return pl.pallas_call(
        body, name=name, grid_spec=grid_spec, out_shape=jax.ShapeDtypeStruct((r, 2 * hc), F32),
        compiler_params=_cparams(("parallel", "parallel")),
    )(slots, pair, landed, landed, landed)


def _stack_sum(x, *, name, out_dtype=F32, block_bytes=1 << 20):
    s, r, c = x.shape
    tr = _pick(r, max(8, block_bytes // (4 * c)), 16) if r % 16 == 0 else r

    def body(x_ref, o_ref):
        acc = x_ref[0].astype(F32)
        for j in range(1, s):
            acc = acc + x_ref[j].astype(F32)
        o_ref[...] = acc.astype(out_dtype)

    return pl.pallas_call(
        body, name=name, grid=(r // tr,),
        in_specs=[pl.BlockSpec((s, tr, c), lambda i: (0, i, 0))], out_specs=pl.BlockSpec((tr, c), lambda i: (i, 0)),
        out_shape=jax.ShapeDtypeStruct((r, c), out_dtype), compiler_params=_cparams(("parallel",)),
    )(x)


HBM = pl.BlockSpec(memory_space=pltpu.HBM)


def _place():
    x, y, c = lax.axis_index("x"), lax.axis_index("y"), lax.axis_index("c")
    other_chips = [(1 - x, y), (x, 1 - y), (1 - x, 1 - y)]
    return x, y, c, other_chips


def _half_cols(cols, which):
    hc = cols // 2
    return pl.ds(pl.multiple_of(which * hc, LANES), hc)


SEM = pl.BlockSpec(memory_space=pltpu.SEMAPHORE)
ANY = pl.BlockSpec(memory_space=pl.ANY)
SIDE_EFFECT = pltpu.SideEffectType.DATAFLOW_SIDE_EFFECTING
TOKEN_SHAPE = (8, LANES)


def _hbm(shape, dtype):
    return pltpu.HBM(shape, dtype)


def _in_hbm(a):
    return pltpu.with_memory_space_constraint(a, pltpu.HBM)


def _gather_copy(src_ref, land_ref, ssem, rsem, k, chip_of_block, to, c):
    cols = src_ref.shape[1]
    return pltpu.make_async_remote_copy(
        src_ref=src_ref.at[:, _half_cols(cols, c)], dst_ref=land_ref.at[chip_of_block, :, _half_cols(cols, c)],
        send_sem=ssem.at[k], recv_sem=rsem.at[k], device_id=to, device_id_type=MESH)


def _gather_start(shards, *, name, after=()):
    n = len(shards)
    after = list(after)

    def body(*refs):
        srcs, lands = refs[:n], refs[n:2 * n]
        outs = refs[2 * n + len(after):]
        token = outs[-1]
        x, y, c, chips = _place()
        me = 2 * x + y
        for a in range(n):
            ssem, rsem = outs[4 * a], outs[4 * a + 1]
            for k, (cx, cy) in enumerate(chips):
                _gather_copy(srcs[a], lands[a], ssem, rsem, k, me, (cx, cy, c), c).start()
        token[...] = jnp.zeros_like(token)

    out_shape, out_specs, aliases = [], [], {}
    for a, s in enumerate(shards):
        out_shape += [pltpu.SemaphoreType.DMA((3,)), pltpu.SemaphoreType.DMA((3,)), _hbm(s.shape, s.dtype),
                      _hbm((N_CHIPS,) + s.shape, s.dtype)]
        out_specs += [SEM, SEM, HBM, HBM]
        aliases[a] = 4 * a + 2
        aliases[n + a] = 4 * a + 3
    out_shape.append(jax.ShapeDtypeStruct(TOKEN_SHAPE, F32))
    out_specs.append(pl.BlockSpec(memory_space=pltpu.VMEM))
    lands = [_in_hbm(lax.empty((N_CHIPS,) + s.shape, s.dtype)) for s in shards]
    res = pl.pallas_call(
        body, name=name, in_specs=[HBM] * (2 * n) + [ANY] * len(after), out_specs=out_specs, out_shape=out_shape,
        input_output_aliases=aliases, compiler_params=pltpu.CompilerParams(has_side_effects=SIDE_EFFECT),
    )(*[_in_hbm(s) for s in shards], *lands, *after)
    return [tuple(res[4 * a:4 * a + 4]) for a in range(n)], res[-1]


def _wait_call(wait_fn, parts, after, *, name):
    ssem, rsem, src, land = parts
    after = list(after) if isinstance(after, (list, tuple)) else [after]

    def body(src_ref, land_ref, ssem_ref, rsem_ref, *rest):
        wait_fn(src_ref, land_ref, ssem_ref, rsem_ref)

    return pl.pallas_call(
        body, name=name, in_specs=[HBM, HBM, SEM, SEM] + [ANY] * len(after), out_specs=[HBM, HBM],
        out_shape=[_hbm(src.shape, src.dtype), _hbm(land.shape, land.dtype)], input_output_aliases={0: 0, 1: 1},
        compiler_params=pltpu.CompilerParams(has_side_effects=SIDE_EFFECT),
    )(src, land, ssem, rsem, *after)


ALL_CHIPS = (0, 1, 2)


def _gather_wait(parts, after, *, name, ks=ALL_CHIPS):
    def wait(src_ref, land_ref, ssem_ref, rsem_ref):
        x, y, c, chips = _place()
        for k in ks:
            cx, cy = chips[k]
            cp = _gather_copy(src_ref, land_ref, ssem_ref, rsem_ref, k, 2 * cx + cy, (x, y, c), c)
            cp.wait_send()
            cp.wait_recv()

    src, land = _wait_call(wait, parts, after, name=name)
    return (parts[0], parts[1], src, land)


def _forward_copy(buf_ref, ssem, rsem, k, slab, which, to):
    part = buf_ref.at[slab, :, _half_cols(buf_ref.shape[2], which)]
    return pltpu.make_async_remote_copy(
        src_ref=part, dst_ref=part, send_sem=ssem.at[k], recv_sem=rsem.at[k], device_id=to, device_id_type=MESH)


def _sibling_forward(land, *, name, ks=ALL_CHIPS):
    def body(_, buf, send_sems, recv_sems):
        x, y, c, chips = _place()
        copies = []
        for k in ks:
            cx, cy = chips[k]
            cp = _forward_copy(buf, send_sems, recv_sems, k, 2 * cx + cy, c, (x, y, 1 - c))
            cp.start()
            copies.append(cp)
        for k in ks:
            cx, cy = chips[k]
            _forward_copy(buf, send_sems, recv_sems, k, 2 * cx + cy, 1 - c, (x, y, c)).wait_recv()
        for cp in copies:
            cp.wait_send()

    return pl.pallas_call(
        body, name=name, in_specs=[HBM], out_specs=HBM, out_shape=jax.ShapeDtypeStruct(land.shape, land.dtype),
        input_output_aliases={0: 0},
        scratch_shapes=[pltpu.SemaphoreType.DMA((3,)), pltpu.SemaphoreType.DMA((3,))],
    )(land)


def _forward_start(land, *, name, ks=ALL_CHIPS):
    def body(buf_ref, ssem, rsem, buf_out, token):
        x, y, c, chips = _place()
        for k in ks:
            cx, cy = chips[k]
            _forward_copy(buf_ref, ssem, rsem, k, 2 * cx + cy, c, (x, y, 1 - c)).start()
        token[...] = jnp.zeros_like(token)

    res = pl.pallas_call(
        body, name=name, in_specs=[HBM], out_specs=[SEM, SEM, HBM, pl.BlockSpec(memory_space=pltpu.VMEM)],
        out_shape=[pltpu.SemaphoreType.DMA((3,)), pltpu.SemaphoreType.DMA((3,)), _hbm(land.shape, land.dtype),
                   jax.ShapeDtypeStruct(TOKEN_SHAPE, F32)],
        input_output_aliases={0: 2}, compiler_params=pltpu.CompilerParams(has_side_effects=SIDE_EFFECT),
    )(land)
    return tuple(res[:3]), res[3]


def _forward_wait(parts, after, *, name, ks=ALL_CHIPS):
    ssem, rsem, buf = parts
    after = list(after) if isinstance(after, (list, tuple)) else [after]

    def body(buf_ref, ssem_ref, rsem_ref, *rest):
        x, y, c, chips = _place()
        for k in ks:
            cx, cy = chips[k]
            _forward_copy(buf_ref, ssem_ref, rsem_ref, k, 2 * cx + cy, c, (x, y, c)).wait_send()
            _forward_copy(buf_ref, ssem_ref, rsem_ref, k, 2 * cx + cy, 1 - c, (x, y, c)).wait_recv()

    return pl.pallas_call(
        body, name=name, in_specs=[HBM, SEM, SEM] + [ANY] * len(after), out_specs=HBM,
        out_shape=_hbm(buf.shape, buf.dtype), input_output_aliases={0: 0},
        compiler_params=pltpu.CompilerParams(has_side_effects=SIDE_EFFECT),
    )(buf, ssem, rsem, *after)


def _scatter_copy(src_ref, land_ref, ssem, rsem, k, src_slab, dst_slab, to):
    return pltpu.make_async_remote_copy(
        src_ref=src_ref.at[src_slab], dst_ref=land_ref.at[dst_slab], send_sem=ssem.at[k], recv_sem=rsem.at[k],
        device_id=to, device_id_type=MESH)


def _scatter_start(part, *, name):
    def start(src_ref, land_ref, ssem, rsem):
        x, y, c, chips = _place()
        me = 2 * x + y
        for k, (cx, cy) in enumerate(chips):
            _scatter_copy(src_ref, land_ref, ssem, rsem, k, 2 * cx + cy, me, (cx, cy, c)).start()

    return _split_start(start, part, part.shape, N_CHIPS - 1, name=name)


def _scatter_wait(parts, after, *, name):
    def wait(src_ref, land_ref, ssem_ref, rsem_ref):
        x, y, c, chips = _place()
        for k, (cx, cy) in enumerate(chips):
            idx = 2 * cx + cy
            cp = _scatter_copy(src_ref, land_ref, ssem_ref, rsem_ref, k, idx, idx, (x, y, c))
            cp.wait_send()
            cp.wait_recv()

    return _wait_call(wait, parts, after, name=name)


def _split_start(start_fn, src, land_shape, n_sems, *, name):
    def body(src_ref, land_ref, ssem, rsem, src_out, land_out, token):
        start_fn(src_ref, land_ref, ssem, rsem)
        token[...] = jnp.zeros_like(token)

    res = pl.pallas_call(
        body, name=name, in_specs=[HBM, HBM], out_specs=[SEM, SEM, HBM, HBM, pl.BlockSpec(memory_space=pltpu.VMEM)],
        out_shape=[pltpu.SemaphoreType.DMA((n_sems,)), pltpu.SemaphoreType.DMA((n_sems,)), _hbm(src.shape, src.dtype),
                   _hbm(land_shape, src.dtype), jax.ShapeDtypeStruct(TOKEN_SHAPE, F32)],
        input_output_aliases={0: 2, 1: 3}, compiler_params=pltpu.CompilerParams(has_side_effects=SIDE_EFFECT),
    )(_in_hbm(src), _in_hbm(lax.empty(land_shape, src.dtype)))
    return tuple(res[:4]), res[4]


def _swap_copy(src_ref, land_ref, ssem, rsem, which, to):
    return pltpu.make_async_remote_copy(
        src_ref=src_ref.at[:, :, _half_cols(src_ref.shape[2], which)], dst_ref=land_ref,
        send_sem=ssem.at[0], recv_sem=rsem.at[0], device_id=to, device_id_type=MESH)


def _swap_start(grad, *, name):
    def start(src_ref, land_ref, ssem, rsem):
        x, y, c, _ = _place()
        _swap_copy(src_ref, land_ref, ssem, rsem, 1 - c, (x, y, 1 - c)).start()

    s, r, cols = grad.shape
    return _split_start(start, grad, (s, r, cols // 2), 1, name=name)


def _swap_wait(parts, after, *, name):
    def wait(src_ref, land_ref, ssem_ref, rsem_ref):
        x, y, c, _ = _place()
        cp = _swap_copy(src_ref, land_ref, ssem_ref, rsem_ref, 1 - c, (x, y, c))
        cp.wait_send()
        cp.wait_recv()

    return _wait_call(wait, parts, after, name=name)


def _dev_peers(x, y, c, chips):
    return [(x, y, 1 - c)] + [(cx, cy, c) for cx, cy in chips] + [(cx, cy, 1 - c) for cx, cy in chips]


def _dev_gather_start(part, *, name):
    def start(src_ref, land_ref, ssem, rsem):
        x, y, c, chips = _place()
        for k, to in enumerate(_dev_peers(x, y, c, chips)):
            pltpu.make_async_remote_copy(
                src_ref=src_ref, dst_ref=land_ref.at[4 * x + 2 * y + c], send_sem=ssem.at[k], recv_sem=rsem.at[k],
                device_id=to, device_id_type=MESH).start()

    return _split_start(start, part, (N_DEV,) + part.shape, N_DEV - 1, name=name)


def _dev_gather_wait(parts, after, *, name):
    def wait(src_ref, land_ref, ssem_ref, rsem_ref):
        x, y, c, chips = _place()
        for k, (px, py, pc) in enumerate(_dev_peers(x, y, c, chips)):
            cp = pltpu.make_async_remote_copy(
                src_ref=src_ref, dst_ref=land_ref.at[4 * px + 2 * py + pc], send_sem=ssem_ref.at[k],
                recv_sem=rsem_ref.at[k], device_id=(x, y, c), device_id_type=MESH)
            cp.wait_send()
            cp.wait_recv()

    return _wait_call(wait, parts, after, name=name)[1]


def _sibling_share_halves(arrays, *, name):
    n = len(arrays)

    def body(*refs):
        bufs = refs[n:2 * n]
        send_sems, recv_sems = refs[2 * n:]
        x, y, c, _ = _place()
        copies = []
        for a in range(n):
            mine = bufs[a].at[:, _half_cols(bufs[a].shape[1], c)]
            cp = pltpu.make_async_remote_copy(
                src_ref=mine, dst_ref=mine, send_sem=send_sems.at[a], recv_sem=recv_sems.at[a],
                device_id=(x, y, 1 - c), device_id_type=MESH)
            cp.start()
            copies.append(cp)
        for a in range(n):
            theirs = bufs[a].at[:, _half_cols(bufs[a].shape[1], 1 - c)]
            pltpu.make_async_remote_copy(
                src_ref=theirs, dst_ref=theirs, send_sem=send_sems.at[a], recv_sem=recv_sems.at[a],
                device_id=(x, y, c), device_id_type=MESH).wait_recv()
        for cp in copies:
            cp.wait_send()

    return pl.pallas_call(
        body, name=name, in_specs=[HBM] * n, out_specs=[HBM] * n,
        out_shape=[jax.ShapeDtypeStruct(h.shape, h.dtype) for h in arrays],
        input_output_aliases={a: a for a in range(n)},
        scratch_shapes=[pltpu.SemaphoreType.DMA((n,)), pltpu.SemaphoreType.DMA((n,))],
    )(*arrays)


def _pack(arrays, rows_multiple=16, width=LANES):
    flat = jnp.concatenate([a.astype(F32).reshape(-1) for a in arrays])
    total = flat.shape[0]
    rows = -(-total // width)
    rows = -(-rows // rows_multiple) * rows_multiple
    return jnp.pad(flat, (0, rows * width - total)).reshape(rows, width)


def _unpack(buf, shapes):
    flat = buf.reshape(-1)
    out, off = [], 0
    for s in shapes:
        n = math.prod(s)
        out.append(flat[off:off + n].reshape(s))
        off += n
    return out


def kernel(x, norm_pre, norm_post, gla_w_in, gla_w_gate2, gla_b_gate, gla_o_gain, gla_w_out, sgu_w_in, sgu_ln_gain, sgu_ln_bias, sgu_w_spatial, sgu_b_spatial, sgu_w_out, loss_target, m_norm_pre, m_norm_post, m_gla_w_in, m_gla_w_gate2, m_gla_b_gate, m_gla_o_gain, m_gla_w_out, m_sgu_w_in, m_sgu_ln_gain, m_sgu_ln_bias, m_sgu_w_spatial, m_sgu_b_spatial, m_sgu_w_out, v_norm_pre, v_norm_post, v_gla_w_in, v_gla_w_gate2, v_gla_b_gate, v_gla_o_gain, v_gla_w_out, v_sgu_w_in, v_sgu_ln_gain, v_sgu_ln_bias, v_sgu_w_spatial, v_sgu_b_spatial, v_sgu_w_out):
    _, t, d = x.shape
    dk = d // 2
    ws = gla_w_in.shape[2]
    wp = -(-ws // LANES) * LANES
    lay = (ws, wp)
    chip =2 * lax.axis_index("x") + lax.axis_index("y")
    core = lax.axis_index("c")
    core_idx = core.astype(jnp.int32).reshape(1)
    others = jnp.arange(N_CHIPS - 1, dtype=jnp.int32)
    others = others + (others >= chip).astype(jnp.int32)
    slots = jnp.concatenate([chip.astype(jnp.int32).reshape(1), others, core_idx])

    x0 = x[0]
    target = loss_target[0]

    wt_in_g, mt_in_g, vt_in_g = gla_w_in[0].T, m_gla_w_in[0].T, v_gla_w_in[0].T

    small_shard = _pack([gla_w_gate2[0], sgu_ln_gain[0], sgu_ln_bias[0]], rows_multiple=8, width=2 * LANES)
    own = [small_shard, jnp.pad(wt_in_g.astype(BF16), ((0, wp - ws), (0, 0)))]
    in_flight, token = _gather_start(own, name="gather_start_a")
    own_later = [gla_w_out[0].astype(BF16), sgu_w_in[0].astype(BF16), sgu_w_out[0].astype(BF16)]
    in_flight_later, token_later = _gather_start(own_later, name="gather_start_b", after=[token])
    own, in_flight = own + own_later, in_flight + in_flight_later

    def with_own(i, land):
        return lax.dynamic_update_slice(land, own[i][None], (chip, 0, 0))

    def arrived(i, after, name):
        land = _gather_wait(in_flight[i], after, name=name + "_wait")[3]
        return with_own(i, _sibling_forward(land, name=name + "_share"))

    h0 = _norm_pre(x0, norm_pre[0:1] + token[0:1, 0:1] + token_later[0:1, 0:1], name="pre0")
    g_small = arrived(0, h0, "w_small")

    px, py = lax.axis_index("x"), lax.axis_index("y")
    slab = [jnp.stack([s, s]).astype(jnp.int32) for s in (2 * (1 - px) + py, 2 * px + 1 - py, 2 * (1 - px) + 1 - py)]
    flight = _gather_wait(in_flight[1], [g_small, wt_in_g, mt_in_g, vt_in_g], name="w_gla_in_wait_xy", ks=(0, 1))
    fwd, tok = _forward_start(flight[3], name="w_gla_in_share_xy", ks=(0, 1))
    own_slab = jnp.stack([chip, 0 * chip]).astype(jnp.int32) + tok[0, 0].astype(jnp.int32)
    proj0 = _matmul_into_cols(h0, own[1][None], own_slab, lax.empty((t, N_CHIPS * wp), F32), name="gla_in_own")
    land = _forward_wait(fwd, proj0, name="w_gla_in_share_xy_wait", ks=(0, 1))
    proj0 = _matmul_into_cols(h0, land, slab[0], proj0, name="gla_in_x")
    proj0 = _matmul_into_cols(h0, land, slab[1], proj0, name="gla_in_y")
    land = _gather_wait(flight[:3] + (land,), proj0, name="w_gla_in_wait_d", ks=(2,))[3]
    land = _sibling_forward(land, name="w_gla_in_share_d", ks=(2,))
    proj0 = _matmul_into_cols(h0, land, slab[2], proj0, name="gla_in_d")
    wt_g = with_own(1, land).reshape(N_CHIPS * wp, d)
    shard_shapes = [gla_w_gate2.shape[1:], sgu_ln_gain.shape[1:], sgu_ln_bias.shape[1:]]
    per_chip = [_unpack(g_small[j], shard_shapes) for j in range(N_CHIPS)]
    w2_full = jnp.concatenate([p[0] for p in per_chip], axis=1)
    ln_gain = jnp.concatenate([p[1] for p in per_chip], axis=0)[None, :]
    ln_bias = jnp.concatenate([p[2] for p in per_chip], axis=0)[None, :]
    w2p = jnp.pad(w2_full, ((0, LANES - GLA_GATE_RANK), (0, 0)))

    pos_chunk = jnp.arange(SGU_BLOCK) // CHUNK
    mask = pos_chunk[:, None] >= pos_chunk[None, :]
    ws_masked = jnp.where(mask[None], sgu_w_spatial[0], 0.0)
    ws_masked_t = ws_masked.transpose(0, 2, 1)
    bs_t = sgu_b_spatial[0].T

    o0, a0, s_before, s_final = _gla_fwd(proj0, w2p, gla_b_gate, gla_o_gain, lay, name="gla_scan")
    w_out_g = arrived(2, a0, "w_gla_out").reshape(d, d)
    y0 = _matmul(a0, w_out_g, mode="nn", out_dtype=F32, name="gla_out")
    x1, h1 = _post_then_pre(x0, y0, norm_post[0:1], norm_pre[1:2], name="post0_pre1")
    g_wi_s = arrived(3, h1, "w_sgu_in")
    proj1 = _matmul(h1, g_wi_s, mode="nn", out_dtype=F32, name="sgu_in", b_shards=True)
    a1 = _sgu_fwd(proj1, ln_gain, ln_bias, ws_masked, bs_t, name="sgu_gate")
    w_out_s = arrived(4, a1, "w_sgu_out").reshape(d, d)
    y1 = _matmul(a1, w_out_s, mode="nn", out_dtype=F32, name="sgu_out")
    loss_part, dx2, dy1, d_post1 = _loss_head(x1, y1, norm_post[1:2], target, name="loss_head")

    def behind(small, token):
        return small + token[0:1, 0:1]

    def pair_and_scatter(swap, after, name):
        grad, peer = _swap_wait(swap, after, name=name + "_swap_wait")
        pair = _pair_sum_bf16(grad, core_idx, peer, name=name + "_pair")
        return _scatter_start(pair, name=name + "_start")

    def reduced(flight, after, name):
        pair, landed = _scatter_wait(flight, after, name=name + "_wait")
        return _chip_sum(pair, landed, slots, name=name + "_sum")

    dw_out_s = _matmul(a1, dy1, mode="tn", out_dtype=F32, name="d_sgu_w_out")
    swap, tok = _swap_start(dw_out_s.reshape(N_CHIPS, d // N_CHIPS, d), name="g_sgu_out_swap")
    da1 = _matmul(dy1, w_out_s, mode="nt", out_dtype=F32, name="d_sgu_act", after=tok)
    fl_wo_s, tok = pair_and_scatter(swap, da1, "g_sgu_out")
    dproj1, d_ws, d_bs_t, d_lg, d_lb = _sgu_bwd(da1, proj1, ln_gain, behind(ln_bias, tok), ws_masked, ws_masked_t,
                                                bs_t, name="sgu_gate_bwd")
    dw_in_s = _matmul(h1, dproj1, mode="tn", out_dtype=F32, name="d_sgu_w_in", out_shards=True)
    swap, tok = _swap_start(dw_in_s, name="g_sgu_in_swap")
    dh1 = _matmul_nt_shards(dproj1, g_wi_s, out_dtype=F32, name="d_sgu_h", after=tok)
    fl_wi_s, tok = pair_and_scatter(swap, dh1, "g_sgu_in")
    dx1, dy0, d_pre1, d_post0 = _mid_bwd(dx2, dh1, x1, behind(norm_pre[1:2], tok), y0, norm_post[0:1],
                                         name="pre1_post0_bwd")
    dw_out_g = _matmul(a0, dy0, mode="tn", out_dtype=F32, name="d_gla_w_out")
    swap, tok = _swap_start(dw_out_g.reshape(N_CHIPS, d // N_CHIPS, d), name="g_gla_out_swap")
    da0 = _matmul(dy0, w_out_g, mode="nt", out_dtype=F32, name="d_gla_act", after=tok)
    fl_wo_g, tok = pair_and_scatter(swap, da0, "g_gla_out")
    dproj0, d_og, d_bg, d_w2p = _gla_bwd(da0, o0, proj0, w2p, behind(gla_b_gate, tok), gla_o_gain, s_before, s_final,
                                         lay, name="gla_scan_bwd")
    dwt_in_g = _matmul(dproj0, h0, mode="tn", out_dtype=F32, name="d_gla_w_in", tm=wp)
    swap, tok = _swap_start(dwt_in_g.reshape(N_CHIPS, wp, d), name="g_gla_in_swap")
    dh0 = _matmul(dproj0, wt_g, mode="nn", out_dtype=F32, name="d_gla_h", tk=N_CHIPS * wp, after=tok)
    fl_wi_g, tok = pair_and_scatter(swap, dh0, "g_gla_in")
    grad_x, d_pre0 = _first_bwd(dx1, dh0, x0, behind(norm_pre[0:1], tok), name="pre0_bwd")

    small_shapes = [norm_pre.shape, norm_post.shape, gla_b_gate.shape, gla_o_gain.shape, sgu_w_spatial.shape,
                    sgu_b_spatial.shape, (1, GLA_GATE_RANK, dk), (1, d), (1, d), (1, LANES)]
    d_pre = jnp.concatenate([d_pre0, d_pre1], axis=0)
    d_post = jnp.concatenate([d_post0, d_post1], axis=0)
    d_wsp = jnp.where(mask[None], d_ws, 0.0)[None]
    small_part = _pack([d_pre, d_post, d_bg, d_og, d_wsp, d_bs_t.T[None], d_w2p[:GLA_GATE_RANK][None], d_lg, d_lb,
                        loss_part])
    small_flight, tok = _dev_gather_start(small_part, name="small_grads_start")

    def big_update(w, g, m, v, name, after=None):
        return [u[None] for u in _adamw(w[0], g, m[0], v[0], name=name, after=after)]

    r_wo_s = reduced(fl_wo_s, [grad_x, tok], "g_sgu_out")
    r_wi_s = reduced(fl_wi_s, r_wo_s, "g_sgu_in")
    r_wo_g = reduced(fl_wo_g, r_wi_s, "g_gla_out")
    g_wo_sgu, g_wi_sgu, g_wo_gla = _sibling_share_halves([r_wo_s, r_wi_s, r_wo_g], name="grads_share_a")
    u_wo_sgu = big_update(sgu_w_out, g_wo_sgu, m_sgu_w_out, v_sgu_w_out, "adamw_sgu_w_out")
    u_wo_gla = big_update(gla_w_out, g_wo_gla, m_gla_w_out, v_gla_w_out, "adamw_gla_w_out", after=u_wo_sgu[1])
    r_wi_g = reduced(fl_wi_g, [u_wo_gla[1], u_wo_sgu[1]], "g_gla_in")
    gt_wi_gla, = _sibling_share_halves([r_wi_g], name="grads_share_b")
    u_wi_gla_t = _adamw(wt_in_g, gt_wi_gla, mt_in_g, vt_in_g, name="adamw_gla_w_in")
    u_wi_gla = [u.T[None] for u in u_wi_gla_t]
    u_wi_sgu = big_update(sgu_w_in, g_wi_sgu, m_sgu_w_in, v_sgu_w_in, "adamw_sgu_w_in", after=u_wi_gla_t[1])

    small_land = _dev_gather_wait(small_flight, u_wi_sgu[1], name="small_grads_wait")
    small_all = lax.dynamic_update_slice(small_land, small_part[None], (2 * chip + core, 0, 0))
    small_sum = _stack_sum(small_all, name="small_sum")
    (g_pre, g_post, g_bg, g_og, g_wsp, g_bsp, g_w2_full, g_lg_full, g_lb_full, loss_vec) = _unpack(small_sum, small_shapes)
    loss = loss_vec[0, 0]
    g_w2 = lax.dynamic_slice_in_dim(g_w2_full, chip * (dk // N_CHIPS), dk // N_CHIPS, axis=2)
    g_lg = lax.dynamic_slice_in_dim(g_lg_full, chip * (d // N_CHIPS), d // N_CHIPS, axis=1)
    g_lb = lax.dynamic_slice_in_dim(g_lb_full, chip * (d // N_CHIPS), d // N_CHIPS, axis=1)

    small_w = [norm_pre, norm_post, gla_b_gate, gla_o_gain, sgu_w_spatial, sgu_b_spatial, gla_w_gate2, sgu_ln_gain,
               sgu_ln_bias]
    small_g = [g_pre, g_post, g_bg, g_og, g_wsp, g_bsp, g_w2, g_lg, g_lb]
    small_m = [m_norm_pre, m_norm_post, m_gla_b_gate, m_gla_o_gain, m_sgu_w_spatial, m_sgu_b_spatial, m_gla_w_gate2,
               m_sgu_ln_gain, m_sgu_ln_bias]
    small_v = [v_norm_pre, v_norm_post, v_gla_b_gate, v_gla_o_gain, v_sgu_w_spatial, v_sgu_b_spatial, v_gla_w_gate2,
               v_sgu_ln_gain, v_sgu_ln_bias]
    own_shapes = [w.shape for w in small_w]
    _, s_dl, s_m, s_v = _adamw(_pack(small_w), _pack(small_g), _pack(small_m), _pack(small_v), name="adamw_small")
    dl_s, m_s, v_s = _unpack(s_dl, own_shapes), _unpack(s_m, own_shapes), _unpack(s_v, own_shapes)

    def ordered(small, kind):
        pre, post, bg, og, wsp, bsp, w2, lg, lb = small
        return [pre, post, u_wi_gla[kind], w2, bg, og, u_wo_gla[kind], u_wi_sgu[kind], lg, lb, wsp, bsp, u_wo_sgu[kind]]

    return (loss, grad_x[None], *ordered(small_g, 0), *ordered(dl_s, 1), *ordered(m_s, 2), *ordered(v_s, 3))
```

```python
import functools
import math

import jax
import jax.numpy as jnp
from jax import lax
from jax.experimental import pallas as pl
from jax.experimental.pallas import tpu as pltpu

F32 = jnp.float32
BF16 = jnp.bfloat16
MESH = pl.DeviceIdType.MESH

EPS = 1e-6
CHUNK = 64
GLA_HEADS = 4
GLA_GATE_RANK = 16
GLA_TAU = 16.0
SGU_BLOCK = 128
SGU_GROUPS = 8
N_CHIPS = 4
N_DEV = 8
LANES = 128

ADAM_LR = 0.001
ADAM_B1 = 0.9
ADAM_B2 = 0.999
ADAM_EPS = 1e-08
ADAM_WD = 0.01
ADAM_STEP = 10

VMEM_LIMIT = 56 * 1024 * 1024


def _cparams(sem=None):
    return pltpu.CompilerParams(dimension_semantics=sem, vmem_limit_bytes=VMEM_LIMIT)


def _pick(n, cap, unit=LANES):
    best = None
    for t in range(unit, min(n, cap) + 1, unit):
        if n % t == 0:
            best = t
    assert best is not None, (n, cap, unit)
    return best


def _dot(a, b, dims):
    return lax.dot_general(a, b, (dims, ((), ())), preferred_element_type=F32)


def _dot_nn(a, b):
    return _dot(a, b, ((1,), (0,)))


def _dot_nt(a, b):
    return _dot(a, b, ((1,), (1,)))


def _dot_tn(a, b):
    return _dot(a, b, ((0,), (0,)))


def _matmul(a, b, *, mode, out_dtype, name, tm=1024, tn=512, tk=2048, b_shards=False, out_shards=False, after=None,
            out_rows=None):
    if mode == "tn":
        K, M = a.shape
    else:
        M, K = a.shape
    if b_shards:
        ns, br, bc = b.shape
        if mode == "nt":
            N, Kb = br, ns * bc
        else:
            Kb, N = br, ns * bc
    else:
        if mode == "nt":
            N, Kb = b.shape
        else:
            Kb, N = b.shape
    assert K == Kb, (a.shape, b.shape, mode)
    tm = _pick(M, tm)
    tk = _pick(K, tk)
    if b_shards and mode != "nt":
        tn = _pick(bc, tn)
    elif out_shards:
        tn = _pick(N // N_CHIPS, tn)
    else:
        tn = _pick(N, tn)
    if b_shards and mode == "nt":
        tk = _pick(bc, tk)
    nk = K // tk
    grid = (M // tm, N // tn, nk)

    if mode == "tn":
        a_spec = pl.BlockSpec((tk, tm), lambda i, j, k: (k, i))
    else:
        a_spec = pl.BlockSpec((tm, tk), lambda i, j, k: (i, k))
    if b_shards:
        if mode == "nt":
            per = bc // tk
            b_spec = pl.BlockSpec((None, tn, tk), lambda i, j, k: (k // per, j, k % per))
        else:
            per = bc // tn
            b_spec = pl.BlockSpec((None, tk, tn), lambda i, j, k: (j // per, k, j % per))
    elif mode == "nt":
        b_spec = pl.BlockSpec((tn, tk), lambda i, j, k: (j, k))
    else:
        b_spec = pl.BlockSpec((tk, tn), lambda i, j, k: (k, j))
    if out_shards:
        per_o = (N // N_CHIPS) // tn
        out_spec = pl.BlockSpec((None, tm, tn), lambda i, j, k: (j // per_o, i, j % per_o))
        out_shape = jax.ShapeDtypeStruct((N_CHIPS, M, N // N_CHIPS), out_dtype)
    else:
        out_spec = pl.BlockSpec((tm, tn), lambda i, j, k: (i, j))
        out_shape = jax.ShapeDtypeStruct((M if out_rows is None else out_rows, N), out_dtype)

    dims = {"nn": ((1,), (0,)), "nt": ((1,), (1,)), "tn": ((0,), (0,))}[mode]

    def body(a_ref, b_ref, *rest):
        o_ref, scratch = (rest[1], rest[2:]) if after is not None else (rest[0], rest[1:])
        part = _dot(a_ref[...].astype(BF16), b_ref[...].astype(BF16), dims)
        if nk == 1:
            o_ref[...] = part.astype(out_dtype)
        else:
            acc_ref, = scratch
            k = pl.program_id(2)

            @pl.when(k == 0)
            def _():
                acc_ref[...] = part

            @pl.when(k > 0)
            def _():
                acc_ref[...] += part

            @pl.when(k == nk - 1)
            def _():
                o_ref[...] = acc_ref[...].astype(out_dtype)

    extra_specs, extra_args = ([], []) if after is None else ([pl.BlockSpec(memory_space=pl.ANY)], [after])
    return pl.pallas_call(
        body, name=name, grid=grid, in_specs=[a_spec, b_spec] + extra_specs, out_specs=out_spec, out_shape=out_shape,
        scratch_shapes=[] if nk == 1 else [pltpu.VMEM((tm, tn), F32)],
        compiler_params=_cparams(("parallel", "parallel", "arbitrary")),
    )(a, b, *extra_args)


def _matmul_into_cols(a, w, which, buf, *, name, tm=1024):
    M, K = a.shape
    _, N, _ = w.shape
    tm = _pick(M, tm)

    def body(which_ref, a_ref, w_ref, buf_ref, o_ref):
        o_ref[...] = _dot_nt(a_ref[...], w_ref[...])

    grid_spec = pltpu.PrefetchScalarGridSpec(
        num_scalar_prefetch=1, grid=(M // tm,),
        in_specs=[pl.BlockSpec((tm, K), lambda i, s: (i, 0)), pl.BlockSpec((None, N, K), lambda i, s: (s[1], 0, 0)),
                  pl.BlockSpec(memory_space=pl.ANY)],
        out_specs=pl.BlockSpec((tm, N), lambda i, s: (i, s[0])))
    return pl.pallas_call(
        body, name=name, grid_spec=grid_spec, out_shape=jax.ShapeDtypeStruct(buf.shape, buf.dtype),
        input_output_aliases={3: 0}, compiler_params=_cparams(("parallel",)),
    )(which, a, w, buf)


def _matmul_nt_shards(a, b, *, out_dtype, name, tm=1024, tn=512, after=None):
    M, K = a.shape
    ns, N, kc = b.shape
    assert K == ns * kc
    tm, tn = _pick(M, tm), _pick(N, tn)

    def body(a_ref, *rest):
        b_refs, o_ref = rest[:ns], rest[ns + (after is not None)]
        acc = _dot_nt(a_ref[:, 0:kc], b_refs[0][...])
        for j in range(1, ns):
            acc += _dot_nt(a_ref[:, j * kc:(j + 1) * kc], b_refs[j][...])
        o_ref[...] = acc.astype(out_dtype)

    def shard(j):
        return pl.BlockSpec((None, tn, kc), lambda i, n: (j, n, 0))

    extra_specs, extra_args = ([], []) if after is None else ([pl.BlockSpec(memory_space=pl.ANY)], [after])
    return pl.pallas_call(
        body, name=name, grid=(M // tm, N // tn),
        in_specs=[pl.BlockSpec((tm, K), lambda i, n: (i, 0))] + [shard(j) for j in range(ns)] + extra_specs,
        out_specs=pl.BlockSpec((tm, tn), lambda i, n: (i, n)), out_shape=jax.ShapeDtypeStruct((M, N), out_dtype),
        compiler_params=_cparams(("parallel", "parallel")),
    )(a, *([b] * ns), *extra_args)


def _rstd(x):
    return lax.rsqrt(jnp.mean(x * x, axis=-1, keepdims=True) + EPS)


def _row_spec(tr, d):
    return pl.BlockSpec((tr, d), lambda i: (i, 0))


def _vec_spec(d):
    return pl.BlockSpec((1, d), lambda i: (0, 0))


def _acc_rows(ref, i, val, cols=slice(None)):
    @pl.when(i == 0)
    def _():
        ref[:, cols] = val

    @pl.when(i > 0)
    def _():
        ref[:, cols] += val


def _norm_pre(x, gain, *, name, tr=256):
    t, d = x.shape
    tr = _pick(t, tr, 8)

    def body(x_ref, g_ref, h_ref):
        xv = x_ref[...]
        h_ref[...] = (xv * _rstd(xv) * g_ref[...]).astype(BF16)

    return pl.pallas_call(
        body, name=name, grid=(t // tr,), in_specs=[_row_spec(tr, d), _vec_spec(d)], out_specs=_row_spec(tr, d),
        out_shape=jax.ShapeDtypeStruct((t, d), BF16), compiler_params=_cparams(("parallel",)),
    )(x, gain)


def _post_then_pre(x, y, post_gain, pre_gain, *, name, tr=256):
    t, d = x.shape
    tr = _pick(t, tr, 8)

    def body(x_ref, y_ref, pg_ref, ng_ref, xn_ref, h_ref):
        yv = y_ref[...]
        xn = x_ref[...] + yv * _rstd(yv) * pg_ref[...]
        xn_ref[...] = xn
        h_ref[...] = (xn * _rstd(xn) * ng_ref[...]).astype(BF16)

    return pl.pallas_call(
        body, name=name, grid=(t // tr,),
        in_specs=[_row_spec(tr, d), _row_spec(tr, d), _vec_spec(d), _vec_spec(d)],
        out_specs=[_row_spec(tr, d), _row_spec(tr, d)],
        out_shape=[jax.ShapeDtypeStruct((t, d), F32), jax.ShapeDtypeStruct((t, d), BF16)],
        compiler_params=_cparams(("parallel",)),
    )(x, y, post_gain, pre_gain)


def _norm_bwd(dy, n, r, gain):
    dn = dy * gain
    return r * (dn - n * jnp.mean(dn * n, axis=-1, keepdims=True))


def _loss_head(x, y, post_gain, target, *, name, tr=256):
    t, d = x.shape
    tr = _pick(t, tr, 8)

    def body(x_ref, y_ref, pg_ref, t_ref, loss_ref, dx_ref, dy_ref, dpg_ref):
        i = pl.program_id(0)
        yv = y_ref[...]
        r = _rstd(yv)
        n = yv * r
        err = x_ref[...] + n * pg_ref[...] - t_ref[...]
        dx = err * (1.0 / d)
        dx_ref[...] = dx
        part = 0.5 * jnp.sum(jnp.mean(err * err, axis=-1, keepdims=True), axis=0, keepdims=True)
        _acc_rows(loss_ref, i, jnp.broadcast_to(part, (1, LANES)))
        _acc_rows(dpg_ref, i, jnp.sum(dx * n, axis=0, keepdims=True))
        dy_ref[...] = _norm_bwd(dx, n, r, pg_ref[...]).astype(BF16)

    return pl.pallas_call(
        body, name=name, grid=(t // tr,),
        in_specs=[_row_spec(tr, d), _row_spec(tr, d), _vec_spec(d), _row_spec(tr, d)],
        out_specs=[_vec_spec(LANES), _row_spec(tr, d), _row_spec(tr, d), _vec_spec(d)],
        out_shape=[jax.ShapeDtypeStruct((1, LANES), F32), jax.ShapeDtypeStruct((t, d), F32),
                   jax.ShapeDtypeStruct((t, d), BF16), jax.ShapeDtypeStruct((1, d), F32)],
        compiler_params=_cparams(("arbitrary",)),
    )(x, y, post_gain, target)


def _mid_bwd(dx_out, dh, x, pre_gain, y_prev, post_gain_prev, *, name, tr=256):
    t, d = x.shape
    tr = _pick(t, tr, 8)

    def body(dxo_ref, dh_ref, x_ref, ng_ref, y_ref, pg_ref, dx_ref, dy_ref, dng_ref, dpg_ref):
        i = pl.program_id(0)
        xv = x_ref[...]
        r = _rstd(xv)
        xh = xv * r
        dhv = dh_ref[...]
        _acc_rows(dng_ref, i, jnp.sum(dhv * xh, axis=0, keepdims=True))
        dx = dxo_ref[...] + _norm_bwd(dhv, xh, r, ng_ref[...])
        dx_ref[...] = dx
        yv = y_ref[...]
        ry = _rstd(yv)
        n = yv * ry
        _acc_rows(dpg_ref, i, jnp.sum(dx * n, axis=0, keepdims=True))
        dy_ref[...] = _norm_bwd(dx, n, ry, pg_ref[...]).astype(BF16)

    return pl.pallas_call(
        body, name=name, grid=(t // tr,),
        in_specs=[_row_spec(tr, d), _row_spec(tr, d), _row_spec(tr, d), _vec_spec(d), _row_spec(tr, d), _vec_spec(d)],
        out_specs=[_row_spec(tr, d), _row_spec(tr, d), _vec_spec(d), _vec_spec(d)],
        out_shape=[jax.ShapeDtypeStruct((t, d), F32), jax.ShapeDtypeStruct((t, d), BF16),
                   jax.ShapeDtypeStruct((1, d), F32), jax.ShapeDtypeStruct((1, d), F32)],
        compiler_params=_cparams(("arbitrary",)),
    )(dx_out, dh, x, pre_gain, y_prev, post_gain_prev)


def _first_bwd(dx_out, dh, x, pre_gain, *, name, tr=256):
    t, d = x.shape
    tr = _pick(t, tr, 8)

    def body(dxo_ref, dh_ref, x_ref, ng_ref, dx_ref, dng_ref):
        i = pl.program_id(0)
        xv = x_ref[...]
        r = _rstd(xv)
        xh = xv * r
        dhv = dh_ref[...]
        _acc_rows(dng_ref, i, jnp.sum(dhv * xh, axis=0, keepdims=True))
        dx_ref[...] = dxo_ref[...] + _norm_bwd(dhv, xh, r, ng_ref[...])

    return pl.pallas_call(
        body, name=name, grid=(t // tr,),
        in_specs=[_row_spec(tr, d), _row_spec(tr, d), _row_spec(tr, d), _vec_spec(d)],
        out_specs=[_row_spec(tr, d), _vec_spec(d)],
        out_shape=[jax.ShapeDtypeStruct((t, d), F32), jax.ShapeDtypeStruct((1, d), F32)],
        compiler_params=_cparams(("arbitrary",)),
    )(dx_out, dh, x, pre_gain)


def _sigmoid(x):
    return 1.0 / (1.0 + jnp.exp(-x))


def _log_sigmoid(x):
    return jnp.minimum(x, 0.0) - jnp.log(1.0 + jnp.exp(-jnp.abs(x)))


_GELU_C = math.sqrt(2.0 / math.pi)


_GELU_A = 0.044715


def _gelu_parts(x, with_grad=True):
    x2 = x * x
    h = 0.5 * jnp.tanh(x * (_GELU_C + (_GELU_C * _GELU_A) * x2)) + 0.5
    val = x * h
    if not with_grad:
        return val, None
    return val, h * (1.0 + (1.0 - h) * (x * (2.0 * _GELU_C + (6.0 * _GELU_C * _GELU_A) * x2)))


def _split3(x):
    hi = x.astype(BF16)
    r1 = x - hi.astype(F32)
    mid = r1.astype(BF16)
    lo = (r1 - mid.astype(F32)).astype(BF16)
    return hi, mid, lo


def _tri_matmul(tri_bf16, x):
    hi, mid, lo = _split3(x)
    return _dot_nn(tri_bf16, hi) + _dot_nn(tri_bf16, mid) + _dot_nn(tri_bf16, lo)


def _gla_dims(d):
    dk, dv = d // 2, d
    return dk, dv, dk // GLA_HEADS, dv // GLA_HEADS


def _col_pieces(a, b, lay):
    ws, wp = lay
    out = []
    while a < b:
        j = a // ws
        end = min(b, (j + 1) * ws)
        out.append((j * wp + a - j * ws, end - a))
        a = end
    return out


def _load_cols(ref, a, b, lay):
    parts = [ref[:, s:s + n] for s, n in _col_pieces(a, b, lay)]
    return parts[0] if len(parts) == 1 else jnp.concatenate(parts, axis=1)


def _store_cols(ref, a, val, lay):
    off = 0
    for s, n in _col_pieces(a, a + val.shape[1], lay):
        ref[:, s:s + n] = val[:, off:off + n]
        off += n


def _gate_window(c_r, lay):
    (start, _), = _col_pieces(c_r, c_r + GLA_GATE_RANK, lay)
    assert (start % lay[1]) + LANES <= lay[1]
    return slice(start, start + LANES)


def _gla_gates(glr, k, w2_ref, b_ref):
    z = _dot_nn(glr.astype(BF16), w2_ref[...].astype(BF16)) + b_ref[...]
    la = _log_sigmoid(z) * (1.0 / GLA_TAU)
    row = lax.broadcasted_iota(jnp.int32, (CHUNK, CHUNK), 0)
    col = lax.broadcasted_iota(jnp.int32, (CHUNK, CHUNK), 1)
    incl = (row >= col).astype(BF16)
    bcum = _tri_matmul(incl, la)
    b_end = bcum[CHUNK - 1:CHUNK, :]
    e_rest = jnp.exp(b_end - bcum)
    return z, e_rest, k * e_rest, jnp.exp(b_end)


def _gla_fwd(proj, w2p, b_gate, o_gain, lay, *, name):
    t, wcols = proj.shape
    d = o_gain.shape[1]
    dk, dv, dkh, dvh = _gla_dims(d)
    nc = t // CHUNK
    c_k, c_v, c_g, c_r = dk, 2 * dk, 2 * dk + dv, 2 * dk + 2 * dv
    scale = dkh ** -0.5

    def body(p_ref, w2_ref, b_ref, og_ref, o_ref, a_ref, sb_ref, sfin_ref, s_ref):
        i = pl.program_id(0)

        @pl.when(i == 0)
        def _():
            s_ref[...] = jnp.zeros_like(s_ref)

        q = _load_cols(p_ref, 0, dk, lay) * scale
        k = _load_cols(p_ref, c_k, c_k + dk, lay)
        glr = p_ref[:, _gate_window(c_r, lay)]
        _, _, kdec, decay = _gla_gates(glr, k, w2_ref, b_ref)
        for h in range(GLA_HEADS):
            ks = slice(h * dkh, (h + 1) * dkh)
            vs = slice(h * dvh, (h + 1) * dvh)
            v_h = _load_cols(p_ref, c_v + h * dvh, c_v + (h + 1) * dvh, lay)
            g_h = _load_cols(p_ref, c_g + h * dvh, c_g + (h + 1) * dvh, lay)
            s_old = s_ref[h]
            sb_ref[0, h] = s_old
            s_new = s_old * decay[:, ks] + _dot_tn(v_h.astype(BF16), kdec[:, ks].astype(BF16))
            s_ref[h] = s_new
            o_h = _dot_nt(q[:, ks].astype(BF16), s_new.astype(BF16))
            o_ref[:, vs] = o_h
            on = o_h * _rstd(o_h)
            a_ref[:, vs] = (on * og_ref[:, vs] * (g_h * _sigmoid(g_h))).astype(BF16)

        @pl.when(i == nc - 1)
        def _():
            sfin_ref[...] = s_ref[...]

    full = lambda *shape: pl.BlockSpec(shape, lambda i: (0,) * len(shape))
    return pl.pallas_call(
        body, name=name, grid=(nc,),
        in_specs=[pl.BlockSpec((CHUNK, wcols), lambda i: (i, 0)), full(LANES, dk), full(1, dk), full(1, dv)],
        out_specs=[pl.BlockSpec((CHUNK, dv), lambda i: (i, 0)), pl.BlockSpec((CHUNK, dv), lambda i: (i, 0)),
                   pl.BlockSpec((1, GLA_HEADS, dvh, dkh), lambda i: (i, 0, 0, 0)), full(GLA_HEADS, dvh, dkh)],
        out_shape=[jax.ShapeDtypeStruct((t, dv), F32), jax.ShapeDtypeStruct((t, dv), BF16),
                   jax.ShapeDtypeStruct((nc, GLA_HEADS, dvh, dkh), F32),
                   jax.ShapeDtypeStruct((GLA_HEADS, dvh, dkh), F32)],
        scratch_shapes=[pltpu.VMEM((GLA_HEADS, dvh, dkh), F32)],
        compiler_params=_cparams(("arbitrary",)),
    )(proj, w2p, b_gate, o_gain)


def _gla_bwd(da, o, proj, w2p, b_gate, o_gain, s_before, s_final, lay, *, name):
    t, wcols = proj.shape
    d = o_gain.shape[1]
    dk, dv, dkh, dvh = _gla_dims(d)
    nc = t // CHUNK
    c_k, c_v, c_g, c_r = dk, 2 * dk, 2 * dk + dv, 2 * dk + 2 * dv
    scale = dkh ** -0.5

    def body(da_ref, o_ref, p_ref, w2_ref, b_ref, og_ref, sb_ref, sfin_ref,
             dp_ref, dog_ref, db_ref, dw2_ref, s_ref, gc_ref, dkd_ref):
        i = pl.program_id(0)

        @pl.when(i == 0)
        def _():
            s_ref[...] = sfin_ref[...]
            gc_ref[...] = jnp.zeros_like(gc_ref)

        ws, wp = lay
        for j in range(N_CHIPS):
            dp_ref[:, j * wp + ws:(j + 1) * wp] = jnp.zeros((CHUNK, wp - ws), BF16)
        q = _load_cols(p_ref, 0, dk, lay) * scale
        k = _load_cols(p_ref, c_k, c_k + dk, lay)
        glr = p_ref[:, _gate_window(c_r, lay)]
        z, e_rest, kdec, decay = _gla_gates(glr, k, w2_ref, b_ref)
        ddecay = []
        for h in range(GLA_HEADS):
            ks = slice(h * dkh, (h + 1) * dkh)
            vs = slice(h * dvh, (h + 1) * dvh)
            v_h = _load_cols(p_ref, c_v + h * dvh, c_v + (h + 1) * dvh, lay)
            g_h = _load_cols(p_ref, c_g + h * dvh, c_g + (h + 1) * dvh, lay)
            da_h = da_ref[:, vs]
            o_h = o_ref[:, vs]
            og_h = og_ref[:, vs]
            r = _rstd(o_h)
            on = o_h * r
            sg = _sigmoid(g_h)
            silu = g_h * sg
            _acc_rows(dog_ref, i, jnp.sum(da_h * silu * on, axis=0, keepdims=True), vs)
            _store_cols(dp_ref, c_g + h * dvh, (da_h * (on * og_h) * (sg * (1.0 + g_h * (1.0 - sg)))).astype(BF16),
                        lay)
            don = da_h * silu * og_h
            do_h = (r * (don - on * jnp.mean(don * on, axis=-1, keepdims=True))).astype(BF16)
            s_cur = s_ref[h]
            _store_cols(dp_ref, h * dkh, (_dot_nn(do_h, s_cur.astype(BF16)) * scale).astype(BF16), lay)
            g_tot = gc_ref[h] + _dot_tn(do_h, q[:, ks].astype(BF16))
            g_bf = g_tot.astype(BF16)
            dkd_ref[:, ks] = _dot_nn(v_h.astype(BF16), g_bf)
            _store_cols(dp_ref, c_v + h * dvh, _dot_nt(kdec[:, ks].astype(BF16), g_bf).astype(BF16), lay)
            s_prev = sb_ref[0, h]
            ddecay.append(jnp.sum(g_tot * s_prev, axis=0, keepdims=True))
            gc_ref[h] = g_tot * decay[:, ks]
            s_ref[h] = s_prev
        dkdec = dkd_ref[...]
        _store_cols(dp_ref, c_k, (dkdec * e_rest).astype(BF16), lay)
        d_e = dkdec * kdec
        row = lax.broadcasted_iota(jnp.int32, (CHUNK, CHUNK), 0)
        col = lax.broadcasted_iota(jnp.int32, (CHUNK, CHUNK), 1)
        excl = (row > col).astype(BF16)
        dla = jnp.concatenate(ddecay, axis=1) * decay + _tri_matmul(excl, d_e)
        dz = dla * (1.0 / GLA_TAU) * (1.0 - _sigmoid(z))
        _acc_rows(db_ref, i, jnp.sum(dz, axis=0, keepdims=True))
        dz_bf = dz.astype(BF16)
        dw2 = _dot_tn(glr.astype(BF16), dz_bf)

        @pl.when(i == 0)
        def _():
            dw2_ref[...] = dw2

        @pl.when(i > 0)
        def _():
            dw2_ref[...] += dw2

        dp_ref[:, _gate_window(c_r, lay)] = _dot_nt(dz_bf, w2_ref[...].astype(BF16)).astype(BF16)

    rev = lambda i: (nc - 1 - i, 0)
    full = lambda *shape: pl.BlockSpec(shape, lambda i: (0,) * len(shape))
    return pl.pallas_call(
        body, name=name, grid=(nc,),
        in_specs=[pl.BlockSpec((CHUNK, dv), rev), pl.BlockSpec((CHUNK, dv), rev), pl.BlockSpec((CHUNK, wcols), rev),
                  full(LANES, dk), full(1, dk), full(1, dv),
                  pl.BlockSpec((1, GLA_HEADS, dvh, dkh), lambda i: (nc - 1 - i, 0, 0, 0)), full(GLA_HEADS, dvh, dkh)],
        out_specs=[pl.BlockSpec((CHUNK, wcols), rev), full(1, dv), full(1, dk), full(LANES, dk)],
        out_shape=[jax.ShapeDtypeStruct((t, wcols), BF16), jax.ShapeDtypeStruct((1, dv), F32),
                   jax.ShapeDtypeStruct((1, dk), F32), jax.ShapeDtypeStruct((LANES, dk), F32)],
        scratch_shapes=[pltpu.VMEM((GLA_HEADS, dvh, dkh), F32), pltpu.VMEM((GLA_HEADS, dvh, dkh), F32),
                        pltpu.VMEM((CHUNK, dk), F32)],
        compiler_params=_cparams(("arbitrary",)),
    )(da, o, proj, w2p, b_gate, o_gain, s_before, s_final)


def _sgu_mid(p_ref, lg_ref, lb_ref, ws_ref, bst_ref, w, with_grad=True):
    gd = w // SGU_GROUPS
    u_act, du_fac = _gelu_parts(p_ref[:, 0:w], with_grad)
    vf, dv_fac = _gelu_parts(p_ref[:, w:2 * w], with_grad)
    mu = jnp.mean(vf, axis=-1, keepdims=True)
    cen = vf - mu
    rstd = lax.rsqrt(jnp.mean(cen * cen, axis=-1, keepdims=True) + EPS)
    xh = cen * rstd
    vn = (xh * lg_ref[...] + lb_ref[...]).astype(BF16)
    vs = [_dot_nn(ws_ref[g].astype(BF16), vn[:, g * gd:(g + 1) * gd]) + bst_ref[:, g:g + 1]
          for g in range(SGU_GROUPS)]
    return u_act, du_fac, dv_fac, rstd, xh, vn, vs


def _sgu_fwd(proj, ln_gain, ln_bias, ws_masked, bs_t, *, name):
    t, w3 = proj.shape
    w = w3 // 3
    gd = w // SGU_GROUPS
    nb = t // SGU_BLOCK

    def body(p_ref, lg_ref, lb_ref, ws_ref, bst_ref, a_ref):
        u_act, _, _, _, _, _, vs = _sgu_mid(p_ref, lg_ref, lb_ref, ws_ref, bst_ref, w, with_grad=False)
        for g in range(SGU_GROUPS):
            cs = slice(g * gd, (g + 1) * gd)
            gate = p_ref[:, 2 * w + g * gd:2 * w + (g + 1) * gd]
            a_ref[:, cs] = (u_act[:, cs] * vs[g] * (gate * _sigmoid(gate))).astype(BF16)

    full = lambda *shape: pl.BlockSpec(shape, lambda i: (0,) * len(shape))
    return pl.pallas_call(
        body, name=name, grid=(nb,),
        in_specs=[pl.BlockSpec((SGU_BLOCK, w3), lambda i: (i, 0)), full(1, w), full(1, w),
                  full(SGU_GROUPS, SGU_BLOCK, SGU_BLOCK), full(SGU_BLOCK, SGU_GROUPS)],
        out_specs=pl.BlockSpec((SGU_BLOCK, w), lambda i: (i, 0)),
        out_shape=jax.ShapeDtypeStruct((t, w), BF16),
        compiler_params=_cparams(("parallel",)),
    )(proj, ln_gain, ln_bias, ws_masked, bs_t)


def _sgu_bwd(da, proj, ln_gain, ln_bias, ws_masked, ws_masked_t, bs_t, *, name):
    t, w3 = proj.shape
    w = w3 // 3
    gd = w // SGU_GROUPS
    nb = t // SGU_BLOCK

    def body(da_ref, p_ref, lg_ref, lb_ref, ws_ref, wst_ref, bst_ref, dp_ref, dws_ref, dbst_ref, dlg_ref, dlb_ref,
             dvn_ref):
        i = pl.program_id(0)
        u_act, du_fac, dv_fac, rstd, xh, vn, vs = _sgu_mid(p_ref, lg_ref, lb_ref, ws_ref, bst_ref, w)
        for g in range(SGU_GROUPS):
            cs = slice(g * gd, (g + 1) * gd)
            gate = p_ref[:, 2 * w + g * gd:2 * w + (g + 1) * gd]
            sg = _sigmoid(gate)
            silu = gate * sg
            da_g = da_ref[:, cs]
            ua_g = u_act[:, cs]
            dp_ref[:, cs] = (da_g * vs[g] * silu * du_fac[:, cs]).astype(BF16)
            dp_ref[:, 2 * w + g * gd:2 * w + (g + 1) * gd] = (
                da_g * ua_g * vs[g] * (sg * (1.0 + gate * (1.0 - sg)))).astype(BF16)
            dvs = da_g * ua_g * silu
            dvs_bf = dvs.astype(BF16)
            dvn_ref[:, cs] = _dot_nn(wst_ref[g].astype(BF16), dvs_bf)
            dws = _dot_nt(dvs_bf, vn[:, cs])
            dbs = jnp.sum(dvs, axis=1, keepdims=True)

            @pl.when(i == 0)
            def _():
                dws_ref[g] = dws
                dbst_ref[:, g:g + 1] = dbs

            @pl.when(i > 0)
            def _():
                dws_ref[g] += dws
                dbst_ref[:, g:g + 1] += dbs

        dvn = dvn_ref[...]
        _acc_rows(dlg_ref, i, jnp.sum(dvn * xh, axis=0, keepdims=True))
        _acc_rows(dlb_ref, i, jnp.sum(dvn, axis=0, keepdims=True))
        dxh = dvn * lg_ref[...]
        dvf = rstd * (dxh - jnp.mean(dxh, axis=-1, keepdims=True)
                      - xh * jnp.mean(dxh * xh, axis=-1, keepdims=True))
        dp_ref[:, w:2 * w] = (dvf * dv_fac).astype(BF16)

    full = lambda *shape: pl.BlockSpec(shape, lambda i: (0,) * len(shape))
    return pl.pallas_call(
        body, name=name, grid=(nb,),
        in_specs=[pl.BlockSpec((SGU_BLOCK, w), lambda i: (i, 0)), pl.BlockSpec((SGU_BLOCK, w3), lambda i: (i, 0)),
                  full(1, w), full(1, w), full(SGU_GROUPS, SGU_BLOCK, SGU_BLOCK),
                  full(SGU_GROUPS, SGU_BLOCK, SGU_BLOCK), full(SGU_BLOCK, SGU_GROUPS)],
        out_specs=[pl.BlockSpec((SGU_BLOCK, w3), lambda i: (i, 0)), full(SGU_GROUPS, SGU_BLOCK, SGU_BLOCK),
                   full(SGU_BLOCK, SGU_GROUPS), full(1, w), full(1, w)],
        out_shape=[jax.ShapeDtypeStruct((t, w3), BF16), jax.ShapeDtypeStruct((SGU_GROUPS, SGU_BLOCK, SGU_BLOCK), F32),
                   jax.ShapeDtypeStruct((SGU_BLOCK, SGU_GROUPS), F32), jax.ShapeDtypeStruct((1, w), F32),
                   jax.ShapeDtypeStruct((1, w), F32)],
        scratch_shapes=[pltpu.VMEM((SGU_BLOCK, w), F32)],
        compiler_params=_cparams(("arbitrary",)),
    )(da, proj, ln_gain, ln_bias, ws_masked, ws_masked_t, bs_t)


def _tile2d(rows, cols, block_bytes, row_unit):
    if rows % row_unit == 0:
        return _pick(rows, max(row_unit, block_bytes // (4 * cols)), row_unit), cols
    return rows, _pick(cols, max(LANES, block_bytes // (4 * rows)))


def _adamw(w, g, m, v, *, name, block_bytes=1 << 20, after=None):
    rows, cols = w.shape
    tr, tc = _tile2d(rows, cols, block_bytes, 8)
    g_rows = g.shape[0]
    assert g_rows == rows or tr == rows
    extra_specs, extra_args = ([], []) if after is None else ([pl.BlockSpec(memory_space=pl.ANY)], [after])

    def body(w_ref, g_ref, m_ref, v_ref, *rest):
        go_ref, d_ref, mo_ref, vo_ref = rest[len(extra_args):]
        gv = g_ref[0:tr, :]
        go_ref[...] = gv
        mn = ADAM_B1 * m_ref[...] + (1.0 - ADAM_B1) * gv
        vn = ADAM_B2 * v_ref[...] + (1.0 - ADAM_B2) * (gv * gv)
        m_hat = mn / (1.0 - ADAM_B1 ** ADAM_STEP)
        v_hat = vn / (1.0 - ADAM_B2 ** ADAM_STEP)
        d_ref[...] = -ADAM_LR * (m_hat / (jnp.sqrt(v_hat) + ADAM_EPS) + ADAM_WD * w_ref[...])
        mo_ref[...] = mn
        vo_ref[...] = vn

    spec = pl.BlockSpec((tr, tc), lambda i, j: (i, j))
    g_spec = spec if g_rows == rows else pl.BlockSpec((g_rows, tc), lambda i, j: (0, j))
    return pl.pallas_call(
        body, name=name, grid=(rows // tr, cols // tc), in_specs=[spec, g_spec, spec, spec] + extra_specs,
        out_specs=[spec] * 4, out_shape=[jax.ShapeDtypeStruct((rows, cols), F32)] * 4,
        compiler_params=_cparams(("parallel", "parallel")),
    )(w, g, m, v, *extra_args)


def _pair_sum_bf16(own, core_idx, peer, *, name, block_bytes=1 << 20):
    s, r, c = own.shape
    hc = c // 2
    tr, tc = _tile2d(r, hc, block_bytes, 16)
    ncb = hc // tc

    def body(h_ref, a_ref, b_ref, o_ref):
        o_ref[...] = (a_ref[...] + b_ref[...]).astype(BF16)

    grid_spec = pltpu.PrefetchScalarGridSpec(
        num_scalar_prefetch=1, grid=(s, r // tr, ncb),
        in_specs=[pl.BlockSpec((None, tr, tc), lambda j, i, k, h: (j, i, h[0] * ncb + k)),
                  pl.BlockSpec((None, tr, tc), lambda j, i, k, h: (j, i, k))],
        out_specs=pl.BlockSpec((None, tr, tc), lambda j, i, k, h: (j, i, k)))
    return pl.pallas_call(
        body, name=name, grid_spec=grid_spec, out_shape=jax.ShapeDtypeStruct((s, r, hc), BF16),
        compiler_params=_cparams(("parallel", "parallel", "parallel")),
    )(core_idx, own, peer)


def _chip_sum(pair, landed, slots, *, name, block_bytes=1 << 20):
    _, r, hc = pair.shape
    tr, tc = _tile2d(r, hc, block_bytes, 16)
    ncb = hc // tc

    def body(s_ref, own_ref, l0_ref, l1_ref, l2_ref, o_ref):
        o_ref[...] = ((own_ref[...].astype(F32) + l0_ref[...].astype(F32)) + l1_ref[...].astype(F32)
                      ) + l2_ref[...].astype(F32)

    def slab(which):
        return pl.BlockSpec((None, tr, tc), lambda i, k, s: (s[which], i, k))

    grid_spec = pltpu.PrefetchScalarGridSpec(
        num_scalar_prefetch=1, grid=(r // tr, ncb),
        in_specs=[slab(0), slab(1), slab(2), slab(3)],
        out_specs=pl.BlockSpec((tr, tc), lambda i, k, s: (i, s[4] * ncb + k)))
    return pl.pallas_call(
        body, name=name, grid_spec=grid_spec, out_shape=jax.ShapeDtypeStruct((r, 2 * hc), F32),
        compiler_params=_cparams(("parallel", "parallel")),
    )(slots, pair, landed, landed, landed)


def _stack_sum(x, *, name, out_dtype=F32, block_bytes=1 << 20):
    s, r, c = x.shape
    tr = _pick(r, max(8, block_bytes // (4 * c)), 16) if r % 16 == 0 else r

    def body(x_ref, o_ref):
        acc = x_ref[0].astype(F32)
        for j in range(1, s):
            acc = acc + x_ref[j].astype(F32)
        o_ref[...] = acc.astype(out_dtype)

    return pl.pallas_call(
        body, name=name, grid=(r // tr,),
        in_specs=[pl.BlockSpec((s, tr, c), lambda i: (0, i, 0))], out_specs=pl.BlockSpec((tr, c), lambda i: (i, 0)),
        out_shape=jax.ShapeDtypeStruct((r, c), out_dtype), compiler_params=_cparams(("parallel",)),
    )(x)


HBM = pl.BlockSpec(memory_space=pltpu.HBM)


def _place():
    x, y, c = lax.axis_index("x"), lax.axis_index("y"), lax.axis_index("c")
    other_chips = [(1 - x, y), (x, 1 - y), (1 - x, 1 - y)]
    return x, y, c, other_chips


def _half_cols(cols, which):
    hc = cols // 2
    return pl.ds(pl.multiple_of(which * hc, LANES), hc)


SEM = pl.BlockSpec(memory_space=pltpu.SEMAPHORE)
ANY = pl.BlockSpec(memory_space=pl.ANY)
SIDE_EFFECT = pltpu.SideEffectType.DATAFLOW_SIDE_EFFECTING
TOKEN_SHAPE = (8, LANES)


def _hbm(shape, dtype):
    return pltpu.HBM(shape, dtype)


def _in_hbm(a):
    return pltpu.with_memory_space_constraint(a, pltpu.HBM)


def _gather_copy(src_ref, land_ref, ssem, rsem, k, chip_of_block, to, c):
    cols = src_ref.shape[1]
    return pltpu.make_async_remote_copy(
        src_ref=src_ref.at[:, _half_cols(cols, c)], dst_ref=land_ref.at[chip_of_block, :, _half_cols(cols, c)],
        send_sem=ssem.at[k], recv_sem=rsem.at[k], device_id=to, device_id_type=MESH)


def _gather_start(shards, *, name, after=()):
    n = len(shards)
    after = list(after)

    def body(*refs):
        srcs, lands = refs[:n], refs[n:2 * n]
        outs = refs[2 * n + len(after):]
        token = outs[-1]
        x, y, c, chips = _place()
        me = 2 * x + y
        for a in range(n):
            ssem, rsem = outs[4 * a], outs[4 * a + 1]
            for k, (cx, cy) in enumerate(chips):
                _gather_copy(srcs[a], lands[a], ssem, rsem, k, me, (cx, cy, c), c).start()
        token[...] = jnp.zeros_like(token)

    out_shape, out_specs, aliases = [], [], {}
    for a, s in enumerate(shards):
        out_shape += [pltpu.SemaphoreType.DMA((3,)), pltpu.SemaphoreType.DMA((3,)), _hbm(s.shape, s.dtype),
                      _hbm((N_CHIPS,) + s.shape, s.dtype)]
        out_specs += [SEM, SEM, HBM, HBM]
        aliases[a] = 4 * a + 2
        aliases[n + a] = 4 * a + 3
    out_shape.append(jax.ShapeDtypeStruct(TOKEN_SHAPE, F32))
    out_specs.append(pl.BlockSpec(memory_space=pltpu.VMEM))
    lands = [_in_hbm(lax.empty((N_CHIPS,) + s.shape, s.dtype)) for s in shards]
    res = pl.pallas_call(
        body, name=name, in_specs=[HBM] * (2 * n) + [ANY] * len(after), out_specs=out_specs, out_shape=out_shape,
        input_output_aliases=aliases, compiler_params=pltpu.CompilerParams(has_side_effects=SIDE_EFFECT),
    )(*[_in_hbm(s) for s in shards], *lands, *after)
    return [tuple(res[4 * a:4 * a + 4]) for a in range(n)], res[-1]


def _wait_call(wait_fn, parts, after, *, name):
    ssem, rsem, src, land = parts
    after = list(after) if isinstance(after, (list, tuple)) else [after]

    def body(src_ref, land_ref, ssem_ref, rsem_ref, *rest):
        wait_fn(src_ref, land_ref, ssem_ref, rsem_ref)

    return pl.pallas_call(
        body, name=name, in_specs=[HBM, HBM, SEM, SEM] + [ANY] * len(after), out_specs=[HBM, HBM],
        out_shape=[_hbm(src.shape, src.dtype), _hbm(land.shape, land.dtype)], input_output_aliases={0: 0, 1: 1},
        compiler_params=pltpu.CompilerParams(has_side_effects=SIDE_EFFECT),
    )(src, land, ssem, rsem, *after)


ALL_CHIPS = (0, 1, 2)


def _gather_wait(parts, after, *, name, ks=ALL_CHIPS):
    def wait(src_ref, land_ref, ssem_ref, rsem_ref):
        x, y, c, chips = _place()
        for k in ks:
            cx, cy = chips[k]
            cp = _gather_copy(src_ref, land_ref, ssem_ref, rsem_ref, k, 2 * cx + cy, (x, y, c), c)
            cp.wait_send()
            cp.wait_recv()

    src, land = _wait_call(wait, parts, after, name=name)
    return (parts[0], parts[1], src, land)


def _forward_copy(buf_ref, ssem, rsem, k, slab, which, to):
    part = buf_ref.at[slab, :, _half_cols(buf_ref.shape[2], which)]
    return pltpu.make_async_remote_copy(
        src_ref=part, dst_ref=part, send_sem=ssem.at[k], recv_sem=rsem.at[k], device_id=to, device_id_type=MESH)


def _sibling_forward(land, *, name, ks=ALL_CHIPS):
    def body(_, buf, send_sems, recv_sems):
        x, y, c, chips = _place()
        copies = []
        for k in ks:
            cx, cy = chips[k]
            cp = _forward_copy(buf, send_sems, recv_sems, k, 2 * cx + cy, c, (x, y, 1 - c))
            cp.start()
            copies.append(cp)
        for k in ks:
            cx, cy = chips[k]
            _forward_copy(buf, send_sems, recv_sems, k, 2 * cx + cy, 1 - c, (x, y, c)).wait_recv()
        for cp in copies:
            cp.wait_send()

    return pl.pallas_call(
        body, name=name, in_specs=[HBM], out_specs=HBM, out_shape=jax.ShapeDtypeStruct(land.shape, land.dtype),
        input_output_aliases={0: 0},
        scratch_shapes=[pltpu.SemaphoreType.DMA((3,)), pltpu.SemaphoreType.DMA((3,))],
    )(land)


def _share_copy(buf_ref, ssem, rsem, a, which, to):
    part = buf_ref.at[:, _half_cols(buf_ref.shape[1], which)]
    return pltpu.make_async_remote_copy(
        src_ref=part, dst_ref=part, send_sem=ssem.at[a], recv_sem=rsem.at[a], device_id=to, device_id_type=MESH)


def _share_start(arrays, *, name):
    n = len(arrays)

    def body(*refs):
        bufs, ssem, rsem, token = refs[:n], refs[n], refs[n + 1], refs[-1]
        x, y, c, _ = _place()
        for a in range(n):
            _share_copy(bufs[a], ssem, rsem, a, c, (x, y, 1 - c)).start()
        token[...] = jnp.zeros_like(token)

    res = pl.pallas_call(
        body, name=name, in_specs=[HBM] * n,
        out_specs=[SEM, SEM] + [HBM] * n + [pl.BlockSpec(memory_space=pltpu.VMEM)],
        out_shape=[pltpu.SemaphoreType.DMA((n,)), pltpu.SemaphoreType.DMA((n,))]
        + [_hbm(b.shape, b.dtype) for b in arrays] + [jax.ShapeDtypeStruct(TOKEN_SHAPE, F32)],
        input_output_aliases={a: 2 + a for a in range(n)},
        compiler_params=pltpu.CompilerParams(has_side_effects=SIDE_EFFECT),
    )(*[_in_hbm(b) for b in arrays])
    return (res[0], res[1], list(res[2:2 + n])), res[-1]


def _share_wait(parts, after, *, name):
    ssem, rsem, bufs = parts
    n = len(bufs)
    after = list(after) if isinstance(after, (list, tuple)) else [after]

    def body(*refs):
        buf_refs, ssem_ref, rsem_ref = refs[:n], refs[n], refs[n + 1]
        x, y, c, _ = _place()
        for a in range(n):
            _share_copy(buf_refs[a], ssem_ref, rsem_ref, a, c, (x, y, c)).wait_send()
            _share_copy(buf_refs[a], ssem_ref, rsem_ref, a, 1 - c, (x, y, c)).wait_recv()

    return pl.pallas_call(
        body, name=name, in_specs=[HBM] * n + [SEM, SEM] + [ANY] * len(after), out_specs=[HBM] * n,
        out_shape=[_hbm(b.shape, b.dtype) for b in bufs], input_output_aliases={a: a for a in range(n)},
        compiler_params=pltpu.CompilerParams(has_side_effects=SIDE_EFFECT),
    )(*bufs, ssem, rsem, *after)


def _scatter_copy(src_ref, land_ref, ssem, rsem, k, src_slab, dst_slab, to):
    return pltpu.make_async_remote_copy(
        src_ref=src_ref.at[src_slab], dst_ref=land_ref.at[dst_slab], send_sem=ssem.at[k], recv_sem=rsem.at[k],
        device_id=to, device_id_type=MESH)


def _scatter_start(part, *, name):
    def start(src_ref, land_ref, ssem, rsem):
        x, y, c, chips = _place()
        me = 2 * x + y
        for k, (cx, cy) in enumerate(chips):
            _scatter_copy(src_ref, land_ref, ssem, rsem, k, 2 * cx + cy, me, (cx, cy, c)).start()

    return _split_start(start, part, part.shape, N_CHIPS - 1, name=name)


def _scatter_wait(parts, after, *, name):
    def wait(src_ref, land_ref, ssem_ref, rsem_ref):
        x, y, c, chips = _place()
        for k, (cx, cy) in enumerate(chips):
            idx = 2 * cx + cy
            cp = _scatter_copy(src_ref, land_ref, ssem_ref, rsem_ref, k, idx, idx, (x, y, c))
            cp.wait_send()
            cp.wait_recv()

    return _wait_call(wait, parts, after, name=name)


def _split_start(start_fn, src, land_shape, n_sems, *, name):
    def body(src_ref, land_ref, ssem, rsem, src_out, land_out, token):
        start_fn(src_ref, land_ref, ssem, rsem)
        token[...] = jnp.zeros_like(token)

    res = pl.pallas_call(
        body, name=name, in_specs=[HBM, HBM], out_specs=[SEM, SEM, HBM, HBM, pl.BlockSpec(memory_space=pltpu.VMEM)],
        out_shape=[pltpu.SemaphoreType.DMA((n_sems,)), pltpu.SemaphoreType.DMA((n_sems,)), _hbm(src.shape, src.dtype),
                   _hbm(land_shape, src.dtype), jax.ShapeDtypeStruct(TOKEN_SHAPE, F32)],
        input_output_aliases={0: 2, 1: 3}, compiler_params=pltpu.CompilerParams(has_side_effects=SIDE_EFFECT),
    )(_in_hbm(src), _in_hbm(lax.empty(land_shape, src.dtype)))
    return tuple(res[:4]), res[4]


def _swap_copy(src_ref, land_ref, ssem, rsem, which, to):
    return pltpu.make_async_remote_copy(
        src_ref=src_ref.at[:, :, _half_cols(src_ref.shape[2], which)], dst_ref=land_ref,
        send_sem=ssem.at[0], recv_sem=rsem.at[0], device_id=to, device_id_type=MESH)


def _swap_start(grad, *, name):
    def start(src_ref, land_ref, ssem, rsem):
        x, y, c, _ = _place()
        _swap_copy(src_ref, land_ref, ssem, rsem, 1 - c, (x, y, 1 - c)).start()

    s, r, cols = grad.shape
    return _split_start(start, grad, (s, r, cols // 2), 1, name=name)


def _swap_wait(parts, after, *, name):
    def wait(src_ref, land_ref, ssem_ref, rsem_ref):
        x, y, c, _ = _place()
        cp = _swap_copy(src_ref, land_ref, ssem_ref, rsem_ref, 1 - c, (x, y, c))
        cp.wait_send()
        cp.wait_recv()

    return _wait_call(wait, parts, after, name=name)


def _dev_peers(x, y, c, chips):
    return [(x, y, 1 - c)] + [(cx, cy, c) for cx, cy in chips] + [(cx, cy, 1 - c) for cx, cy in chips]


def _dev_gather_start(part, *, name):
    def start(src_ref, land_ref, ssem, rsem):
        x, y, c, chips = _place()
        for k, to in enumerate(_dev_peers(x, y, c, chips)):
            pltpu.make_async_remote_copy(
                src_ref=src_ref, dst_ref=land_ref.at[4 * x + 2 * y + c], send_sem=ssem.at[k], recv_sem=rsem.at[k],
                device_id=to, device_id_type=MESH).start()

    return _split_start(start, part, (N_DEV,) + part.shape, N_DEV - 1, name=name)


def _dev_gather_wait(parts, after, *, name):
    def wait(src_ref, land_ref, ssem_ref, rsem_ref):
        x, y, c, chips = _place()
        for k, (px, py, pc) in enumerate(_dev_peers(x, y, c, chips)):
            cp = pltpu.make_async_remote_copy(
                src_ref=src_ref, dst_ref=land_ref.at[4 * px + 2 * py + pc], send_sem=ssem_ref.at[k],
                recv_sem=rsem_ref.at[k], device_id=(x, y, c), device_id_type=MESH)
            cp.wait_send()
            cp.wait_recv()

    return _wait_call(wait, parts, after, name=name)[1]


def _sibling_share_halves(arrays, *, name):
    n = len(arrays)

    def body(*refs):
        bufs = refs[n:2 * n]
        send_sems, recv_sems = refs[2 * n:]
        x, y, c, _ = _place()
        copies = []
        for a in range(n):
            mine = bufs[a].at[:, _half_cols(bufs[a].shape[1], c)]
            cp = pltpu.make_async_remote_copy(
                src_ref=mine, dst_ref=mine, send_sem=send_sems.at[a], recv_sem=recv_sems.at[a],
                device_id=(x, y, 1 - c), device_id_type=MESH)
            cp.start()
            copies.append(cp)
        for a in range(n):
            theirs = bufs[a].at[:, _half_cols(bufs[a].shape[1], 1 - c)]
            pltpu.make_async_remote_copy(
                src_ref=theirs, dst_ref=theirs, send_sem=send_sems.at[a], recv_sem=recv_sems.at[a],
                device_id=(x, y, c), device_id_type=MESH).wait_recv()
        for cp in copies:
            cp.wait_send()

    return pl.pallas_call(
        body, name=name, in_specs=[HBM] * n, out_specs=[HBM] * n,
        out_shape=[jax.ShapeDtypeStruct(h.shape, h.dtype) for h in arrays],
        input_output_aliases={a: a for a in range(n)},
        scratch_shapes=[pltpu.SemaphoreType.DMA((n,)), pltpu.SemaphoreType.DMA((n,))],
    )(*arrays)


def _pack(arrays, rows_multiple=16, width=LANES):
    flat = jnp.concatenate([a.astype(F32).reshape(-1) for a in arrays])
    total = flat.shape[0]
    rows = -(-total // width)
    rows = -(-rows // rows_multiple) * rows_multiple
    return jnp.pad(flat, (0, rows * width - total)).reshape(rows, width)


def _unpack(buf, shapes):
    flat = buf.reshape(-1)
    out, off = [], 0
    for s in shapes:
        n = math.prod(s)
        out.append(flat[off:off + n].reshape(s))
        off += n
    return out


def kernel(x, norm_pre, norm_post, gla_w_in, gla_w_gate2, gla_b_gate, gla_o_gain, gla_w_out, sgu_w_in, sgu_ln_gain, sgu_ln_bias, sgu_w_spatial, sgu_b_spatial, sgu_w_out, loss_target, m_norm_pre, m_norm_post, m_gla_w_in, m_gla_w_gate2, m_gla_b_gate, m_gla_o_gain, m_gla_w_out, m_sgu_w_in, m_sgu_ln_gain, m_sgu_ln_bias, m_sgu_w_spatial, m_sgu_b_spatial, m_sgu_w_out, v_norm_pre, v_norm_post, v_gla_w_in, v_gla_w_gate2, v_gla_b_gate, v_gla_o_gain, v_gla_w_out, v_sgu_w_in, v_sgu_ln_gain, v_sgu_ln_bias, v_sgu_w_spatial, v_sgu_b_spatial, v_sgu_w_out):
    _, t, d = x.shape
    dk = d // 2
    ws = gla_w_in.shape[2]
    wp = -(-ws // LANES) * LANES
    lay = (ws, wp)
    chip =2 * lax.axis_index("x") + lax.axis_index("y")
    core = lax.axis_index("c")
    core_idx = core.astype(jnp.int32).reshape(1)
    others = jnp.arange(N_CHIPS - 1, dtype=jnp.int32)
    others = others + (others >= chip).astype(jnp.int32)
    slots = jnp.concatenate([chip.astype(jnp.int32).reshape(1), others, core_idx])

    x0 = x[0]
    target = loss_target[0]

    wt_in_g, mt_in_g, vt_in_g = gla_w_in[0].T, m_gla_w_in[0].T, v_gla_w_in[0].T

    small_shard = _pack([gla_w_gate2[0], sgu_ln_gain[0], sgu_ln_bias[0]], rows_multiple=8, width=2 * LANES)
    own = [small_shard, jnp.pad(wt_in_g.astype(BF16), ((0, wp - ws), (0, 0)))]
    in_flight, token = _gather_start(own, name="gather_start_a")
    own_later = [gla_w_out[0].astype(BF16), sgu_w_in[0].astype(BF16), sgu_w_out[0].astype(BF16)]
    in_flight_later, token_later = _gather_start(own_later, name="gather_start_b", after=[token])
    own, in_flight = own + own_later, in_flight + in_flight_later

    def with_own(i, land):
        return lax.dynamic_update_slice(land, own[i][None], (chip, 0, 0))

    def arrived(i, after, name):
        land = _gather_wait(in_flight[i], after, name=name + "_wait")[3]
        return with_own(i, _sibling_forward(land, name=name + "_share"))

    h0 = _norm_pre(x0, norm_pre[0:1] + token[0:1, 0:1] + token_later[0:1, 0:1], name="pre0")
    g_small = arrived(0, h0, "w_small")
    wt_g = arrived(1, [g_small, wt_in_g, mt_in_g, vt_in_g], "w_gla_in").reshape(N_CHIPS * wp, d)
    shard_shapes = [gla_w_gate2.shape[1:], sgu_ln_gain.shape[1:], sgu_ln_bias.shape[1:]]
    per_chip = [_unpack(g_small[j], shard_shapes) for j in range(N_CHIPS)]
    w2_full = jnp.concatenate([p[0] for p in per_chip], axis=1)
    ln_gain = jnp.concatenate([p[1] for p in per_chip], axis=0)[None, :]
    ln_bias = jnp.concatenate([p[2] for p in per_chip], axis=0)[None, :]
    w2p = jnp.pad(w2_full, ((0, LANES - GLA_GATE_RANK), (0, 0)))

    pos_chunk = jnp.arange(SGU_BLOCK) // CHUNK
    mask = pos_chunk[:, None] >= pos_chunk[None, :]
    ws_masked = jnp.where(mask[None], sgu_w_spatial[0], 0.0)
    ws_masked_t = ws_masked.transpose(0, 2, 1)
    bs_t = sgu_b_spatial[0].T

    proj0 = _matmul(h0, wt_g, mode="nt", out_dtype=F32, name="gla_in", tn=wp)
    o0, a0, s_before, s_final = _gla_fwd(proj0, w2p, gla_b_gate, gla_o_gain, lay, name="gla_scan")
    w_out_g = arrived(2, a0, "w_gla_out").reshape(d, d)
    y0 = _matmul(a0, w_out_g, mode="nn", out_dtype=F32, name="gla_out")
    x1, h1 = _post_then_pre(x0, y0, norm_post[0:1], norm_pre[1:2], name="post0_pre1")
    g_wi_s = arrived(3, h1, "w_sgu_in")
    proj1 = _matmul(h1, g_wi_s, mode="nn", out_dtype=F32, name="sgu_in", b_shards=True)
    a1 = _sgu_fwd(proj1, ln_gain, ln_bias, ws_masked, bs_t, name="sgu_gate")
    w_out_s = arrived(4, a1, "w_sgu_out").reshape(d, d)
    y1 = _matmul(a1, w_out_s, mode="nn", out_dtype=F32, name="sgu_out")
    loss_part, dx2, dy1, d_post1 = _loss_head(x1, y1, norm_post[1:2], target, name="loss_head")

    def behind(small, token):
        return small + token[0:1, 0:1]

    def pair_and_scatter(swap, after, name):
        grad, peer = _swap_wait(swap, after, name=name + "_swap_wait")
        pair = _pair_sum_bf16(grad, core_idx, peer, name=name + "_pair")
        return _scatter_start(pair, name=name + "_start")

    def reduced(flight, after, name):
        pair, landed = _scatter_wait(flight, after, name=name + "_wait")
        return _chip_sum(pair, landed, slots, name=name + "_sum")

    dw_out_s = _matmul(a1, dy1, mode="tn", out_dtype=F32, name="d_sgu_w_out")
    swap, tok = _swap_start(dw_out_s.reshape(N_CHIPS, d // N_CHIPS, d), name="g_sgu_out_swap")
    da1 = _matmul(dy1, w_out_s, mode="nt", out_dtype=F32, name="d_sgu_act", after=tok)
    fl_wo_s, tok = pair_and_scatter(swap, da1, "g_sgu_out")
    dproj1, d_ws, d_bs_t, d_lg, d_lb = _sgu_bwd(da1, proj1, ln_gain, behind(ln_bias, tok), ws_masked, ws_masked_t,
                                                bs_t, name="sgu_gate_bwd")
    dw_in_s = _matmul(h1, dproj1, mode="tn", out_dtype=F32, name="d_sgu_w_in", out_shards=True)
    swap, tok = _swap_start(dw_in_s, name="g_sgu_in_swap")
    dh1 = _matmul_nt_shards(dproj1, g_wi_s, out_dtype=F32, name="d_sgu_h", after=tok)
    fl_wi_s, tok = pair_and_scatter(swap, dh1, "g_sgu_in")
    dx1, dy0, d_pre1, d_post0 = _mid_bwd(dx2, dh1, x1, behind(norm_pre[1:2], tok), y0, norm_post[0:1],
                                         name="pre1_post0_bwd")
    dw_out_g = _matmul(a0, dy0, mode="tn", out_dtype=F32, name="d_gla_w_out")
    swap, tok = _swap_start(dw_out_g.reshape(N_CHIPS, d // N_CHIPS, d), name="g_gla_out_swap")
    da0 = _matmul(dy0, w_out_g, mode="nt", out_dtype=F32, name="d_gla_act", after=tok)
    fl_wo_g, tok = pair_and_scatter(swap, da0, "g_gla_out")
    dproj0, d_og, d_bg, d_w2p = _gla_bwd(da0, o0, proj0, w2p, behind(gla_b_gate, tok), gla_o_gain, s_before, s_final,
                                         lay, name="gla_scan_bwd")
    r_wo_s = reduced(fl_wo_s, dproj0, "g_sgu_out")
    r_wi_s = reduced(fl_wi_s, r_wo_s, "g_sgu_in")
    r_wo_g = reduced(fl_wo_g, r_wi_s, "g_gla_out")
    sharing, tok = _share_start([r_wo_s, r_wi_s, r_wo_g], name="grads_share_a")
    dwt_in_g = _matmul(dproj0, h0, mode="tn", out_dtype=F32, name="d_gla_w_in", tm=wp, after=tok)
    swap, tok = _swap_start(dwt_in_g.reshape(N_CHIPS, wp, d), name="g_gla_in_swap")
    dh0 = _matmul(dproj0, wt_g, mode="nn", out_dtype=F32, name="d_gla_h", tk=N_CHIPS * wp, after=tok)
    fl_wi_g, tok = pair_and_scatter(swap, dh0, "g_gla_in")
    grad_x, d_pre0 = _first_bwd(dx1, dh0, x0, behind(norm_pre[0:1], tok), name="pre0_bwd")

    small_shapes = [norm_pre.shape, norm_post.shape, gla_b_gate.shape, gla_o_gain.shape, sgu_w_spatial.shape,
                    sgu_b_spatial.shape, (1, GLA_GATE_RANK, dk), (1, d), (1, d), (1, LANES)]
    d_pre = jnp.concatenate([d_pre0, d_pre1], axis=0)
    d_post = jnp.concatenate([d_post0, d_post1], axis=0)
    d_wsp = jnp.where(mask[None], d_ws, 0.0)[None]
    small_part = _pack([d_pre, d_post, d_bg, d_og, d_wsp, d_bs_t.T[None], d_w2p[:GLA_GATE_RANK][None], d_lg, d_lb,
                        loss_part])
    small_flight, tok = _dev_gather_start(small_part, name="small_grads_start")

    def big_update(w, g, m, v, name):
        return [u[None] for u in _adamw(w[0], g, m[0], v[0], name=name)]

    g_wo_sgu, g_wi_sgu, g_wo_gla = _share_wait(sharing, [grad_x, tok], name="grads_share_a_wait")
    u_wo_sgu = big_update(sgu_w_out, g_wo_sgu, m_sgu_w_out, v_sgu_w_out, "adamw_sgu_w_out")
    u_wi_sgu = big_update(sgu_w_in, g_wi_sgu, m_sgu_w_in, v_sgu_w_in, "adamw_sgu_w_in")
    u_wo_gla = big_update(gla_w_out, g_wo_gla, m_gla_w_out, v_gla_w_out, "adamw_gla_w_out")
    r_wi_g = reduced(fl_wi_g, [u_wo_gla[1], u_wi_sgu[1], u_wo_sgu[1]], "g_gla_in")
    gt_wi_gla, = _sibling_share_halves([r_wi_g], name="grads_share_b")
    u_wi_gla = [u.T[None] for u in _adamw(wt_in_g, gt_wi_gla, mt_in_g, vt_in_g, name="adamw_gla_w_in")]

    small_land = _dev_gather_wait(small_flight, u_wi_gla[1], name="small_grads_wait")
    small_all = lax.dynamic_update_slice(small_land, small_part[None], (2 * chip + core, 0, 0))
    small_sum = _stack_sum(small_all, name="small_sum")
    (g_pre, g_post, g_bg, g_og, g_wsp, g_bsp, g_w2_full, g_lg_full, g_lb_full, loss_vec) = _unpack(small_sum, small_shapes)
    loss = loss_vec[0, 0]
    g_w2 = lax.dynamic_slice_in_dim(g_w2_full, chip * (dk // N_CHIPS), dk // N_CHIPS, axis=2)
    g_lg = lax.dynamic_slice_in_dim(g_lg_full, chip * (d // N_CHIPS), d // N_CHIPS, axis=1)
    g_lb = lax.dynamic_slice_in_dim(g_lb_full, chip * (d // N_CHIPS), d // N_CHIPS, axis=1)

    small_w = [norm_pre, norm_post, gla_b_gate, gla_o_gain, sgu_w_spatial, sgu_b_spatial, gla_w_gate2, sgu_ln_gain,
               sgu_ln_bias]
    small_g = [g_pre, g_post, g_bg, g_og, g_wsp, g_bsp, g_w2, g_lg, g_lb]
    small_m = [m_norm_pre, m_norm_post, m_gla_b_gate, m_gla_o_gain, m_sgu_w_spatial, m_sgu_b_spatial, m_gla_w_gate2,
               m_sgu_ln_gain, m_sgu_ln_bias]
    small_v = [v_norm_pre, v_norm_post, v_gla_b_gate, v_gla_o_gain, v_sgu_w_spatial, v_sgu_b_spatial, v_gla_w_gate2,
               v_sgu_ln_gain, v_sgu_ln_bias]
    own_shapes = [w.shape for w in small_w]
    _, s_dl, s_m, s_v = _adamw(_pack(small_w), _pack(small_g), _pack(small_m), _pack(small_v), name="adamw_small")
    dl_s, m_s, v_s = _unpack(s_dl, own_shapes), _unpack(s_m, own_shapes), _unpack(s_v, own_shapes)

    def ordered(small, kind):
        pre, post, bg, og, wsp, bsp, w2, lg, lb = small
        return [pre, post, u_wi_gla[kind], w2, bg, og, u_wo_gla[kind], u_wi_sgu[kind], lg, lb, wsp, bsp, u_wo_sgu[kind]]

    return (loss, grad_x[None], *ordered(small_g, 0), *ordered(dl_s, 1), *ordered(m_s, 2), *ordered(v_s, 3))
```

```python
import functools
import math

import jax
import jax.numpy as jnp
from jax import lax
from jax.experimental import pallas as pl
from jax.experimental.pallas import tpu as pltpu

F32 = jnp.float32
BF16 = jnp.bfloat16
MESH = pl.DeviceIdType.MESH

EPS = 1e-6
CHUNK = 64
GLA_HEADS = 4
GLA_GATE_RANK = 16
GLA_TAU = 16.0
SGU_BLOCK = 128
SGU_GROUPS = 8
N_CHIPS = 4
N_DEV = 8
LANES = 128

ADAM_LR = 0.001
ADAM_B1 = 0.9
ADAM_B2 = 0.999
ADAM_EPS = 1e-08
ADAM_WD = 0.01
ADAM_STEP = 10

VMEM_LIMIT = 56 * 1024 * 1024


def _cparams(sem=None):
    return pltpu.CompilerParams(dimension_semantics=sem, vmem_limit_bytes=VMEM_LIMIT)


def _pick(n, cap, unit=LANES):
    best = None
    for t in range(unit, min(n, cap) + 1, unit):
        if n % t == 0:
            best = t
    assert best is not None, (n, cap, unit)
    return best


def _dot(a, b, dims):
    return lax.dot_general(a, b, (dims, ((), ())), preferred_element_type=F32)


def _dot_nn(a, b):
    return _dot(a, b, ((1,), (0,)))


def _dot_nt(a, b):
    return _dot(a, b, ((1,), (1,)))


def _dot_tn(a, b):
    return _dot(a, b, ((0,), (0,)))


def _matmul(a, b, *, mode, out_dtype, name, tm=1024, tn=512, tk=2048, b_shards=False, out_shards=False, after=None,
            out_rows=None):
    if mode == "tn":
        K, M = a.shape
    else:
        M, K = a.shape
    if b_shards:
        ns, br, bc = b.shape
        if mode == "nt":
            N, Kb = br, ns * bc
        else:
            Kb, N = br, ns * bc
    else:
        if mode == "nt":
            N, Kb = b.shape
        else:
            Kb, N = b.shape
    assert K == Kb, (a.shape, b.shape, mode)
    tm = _pick(M, tm)
    tk = _pick(K, tk)
    if b_shards and mode != "nt":
        tn = _pick(bc, tn)
    elif out_shards:
        tn = _pick(N // N_CHIPS, tn)
    else:
        tn = _pick(N, tn)
    if b_shards and mode == "nt":
        tk = _pick(bc, tk)
    nk = K // tk
    grid = (M // tm, N // tn, nk)

    if mode == "tn":
        a_spec = pl.BlockSpec((tk, tm), lambda i, j, k: (k, i))
    else:
        a_spec = pl.BlockSpec((tm, tk), lambda i, j, k: (i, k))
    if b_shards:
        if mode == "nt":
            per = bc // tk
            b_spec = pl.BlockSpec((None, tn, tk), lambda i, j, k: (k // per, j, k % per))
        else:
            per = bc // tn
            b_spec = pl.BlockSpec((None, tk, tn), lambda i, j, k: (j // per, k, j % per))
    elif mode == "nt":
        b_spec = pl.BlockSpec((tn, tk), lambda i, j, k: (j, k))
    else:
        b_spec = pl.BlockSpec((tk, tn), lambda i, j, k: (k, j))
    if out_shards:
        per_o = (N // N_CHIPS) // tn
        out_spec = pl.BlockSpec((None, tm, tn), lambda i, j, k: (j // per_o, i, j % per_o))
        out_shape = jax.ShapeDtypeStruct((N_CHIPS, M, N // N_CHIPS), out_dtype)
    else:
        out_spec = pl.BlockSpec((tm, tn), lambda i, j, k: (i, j))
        out_shape = jax.ShapeDtypeStruct((M if out_rows is None else out_rows, N), out_dtype)

    dims = {"nn": ((1,), (0,)), "nt": ((1,), (1,)), "tn": ((0,), (0,))}[mode]

    def body(a_ref, b_ref, *rest):
        o_ref, scratch = (rest[1], rest[2:]) if after is not None else (rest[0], rest[1:])
        part = _dot(a_ref[...].astype(BF16), b_ref[...].astype(BF16), dims)
        if nk == 1:
            o_ref[...] = part.astype(out_dtype)
        else:
            acc_ref, = scratch
            k = pl.program_id(2)

            @pl.when(k == 0)
            def _():
                acc_ref[...] = part

            @pl.when(k > 0)
            def _():
                acc_ref[...] += part

            @pl.when(k == nk - 1)
            def _():
                o_ref[...] = acc_ref[...].astype(out_dtype)

    extra_specs, extra_args = ([], []) if after is None else ([pl.BlockSpec(memory_space=pl.ANY)], [after])
    return pl.pallas_call(
        body, name=name, grid=grid, in_specs=[a_spec, b_spec] + extra_specs, out_specs=out_spec, out_shape=out_shape,
        scratch_shapes=[] if nk == 1 else [pltpu.VMEM((tm, tn), F32)],
        compiler_params=_cparams(("parallel", "parallel", "arbitrary")),
    )(a, b, *extra_args)


def _matmul_into_cols(a, w, which, buf, *, name, tm=1024):
    M, K = a.shape
    _, N, _ = w.shape
    tm = _pick(M, tm)

    def body(which_ref, a_ref, w_ref, buf_ref, o_ref):
        o_ref[...] = _dot_nt(a_ref[...], w_ref[...])

    grid_spec = pltpu.PrefetchScalarGridSpec(
        num_scalar_prefetch=1, grid=(M // tm,),
        in_specs=[pl.BlockSpec((tm, K), lambda i, s: (i, 0)), pl.BlockSpec((None, N, K), lambda i, s: (s[1], 0, 0)),
                  pl.BlockSpec(memory_space=pl.ANY)],
        out_specs=pl.BlockSpec((tm, N), lambda i, s: (i, s[0])))
    return pl.pallas_call(
        body, name=name, grid_spec=grid_spec, out_shape=jax.ShapeDtypeStruct(buf.shape, buf.dtype),
        input_output_aliases={3: 0}, compiler_params=_cparams(("parallel",)),
    )(which, a, w, buf)


def _matmul_nt_shards(a, b, *, out_dtype, name, tm=1024, tn=512, after=None):
    M, K = a.shape
    ns, N, kc = b.shape
    assert K == ns * kc
    tm, tn = _pick(M, tm), _pick(N, tn)

    def body(a_ref, *rest):
        b_refs, o_ref = rest[:ns], rest[ns + (after is not None)]
        acc = _dot_nt(a_ref[:, 0:kc], b_refs[0][...])
        for j in range(1, ns):
            acc += _dot_nt(a_ref[:, j * kc:(j + 1) * kc], b_refs[j][...])
        o_ref[...] = acc.astype(out_dtype)

    def shard(j):
        return pl.BlockSpec((None, tn, kc), lambda i, n: (j, n, 0))

    extra_specs, extra_args = ([], []) if after is None else ([pl.BlockSpec(memory_space=pl.ANY)], [after])
    return pl.pallas_call(
        body, name=name, grid=(M // tm, N // tn),
        in_specs=[pl.BlockSpec((tm, K), lambda i, n: (i, 0))] + [shard(j) for j in range(ns)] + extra_specs,
        out_specs=pl.BlockSpec((tm, tn), lambda i, n: (i, n)), out_shape=jax.ShapeDtypeStruct((M, N), out_dtype),
        compiler_params=_cparams(("parallel", "parallel")),
    )(a, *([b] * ns), *extra_args)


def _rstd(x):
    return lax.rsqrt(jnp.mean(x * x, axis=-1, keepdims=True) + EPS)


def _row_spec(tr, d):
    return pl.BlockSpec((tr, d), lambda i: (i, 0))


def _vec_spec(d):
    return pl.BlockSpec((1, d), lambda i: (0, 0))


def _acc_rows(ref, i, val, cols=slice(None)):
    @pl.when(i == 0)
    def _():
        ref[:, cols] = val

    @pl.when(i > 0)
    def _():
        ref[:, cols] += val


def _norm_pre(x, gain, *, name, tr=256):
    t, d = x.shape
    tr = _pick(t, tr, 8)

    def body(x_ref, g_ref, h_ref):
        xv = x_ref[...]
        h_ref[...] = (xv * _rstd(xv) * g_ref[...]).astype(BF16)

    return pl.pallas_call(
        body, name=name, grid=(t // tr,), in_specs=[_row_spec(tr, d), _vec_spec(d)], out_specs=_row_spec(tr, d),
        out_shape=jax.ShapeDtypeStruct((t, d), BF16), compiler_params=_cparams(("parallel",)),
    )(x, gain)


def _post_then_pre(x, y, post_gain, pre_gain, *, name, tr=256):
    t, d = x.shape
    tr = _pick(t, tr, 8)

    def body(x_ref, y_ref, pg_ref, ng_ref, xn_ref, h_ref):
        yv = y_ref[...]
        xn = x_ref[...] + yv * _rstd(yv) * pg_ref[...]
        xn_ref[...] = xn
        h_ref[...] = (xn * _rstd(xn) * ng_ref[...]).astype(BF16)

    return pl.pallas_call(
        body, name=name, grid=(t // tr,),
        in_specs=[_row_spec(tr, d), _row_spec(tr, d), _vec_spec(d), _vec_spec(d)],
        out_specs=[_row_spec(tr, d), _row_spec(tr, d)],
        out_shape=[jax.ShapeDtypeStruct((t, d), F32), jax.ShapeDtypeStruct((t, d), BF16)],
        compiler_params=_cparams(("parallel",)),
    )(x, y, post_gain, pre_gain)


def _norm_bwd(dy, n, r, gain):
    dn = dy * gain
    return r * (dn - n * jnp.mean(dn * n, axis=-1, keepdims=True))


def _loss_head(x, y, post_gain, target, *, name, tr=256):
    t, d = x.shape
    tr = _pick(t, tr, 8)

    def body(x_ref, y_ref, pg_ref, t_ref, loss_ref, dx_ref, dy_ref, dpg_ref):
        i = pl.program_id(0)
        yv = y_ref[...]
        r = _rstd(yv)
        n = yv * r
        err = x_ref[...] + n * pg_ref[...] - t_ref[...]
        dx = err * (1.0 / d)
        dx_ref[...] = dx
        part = 0.5 * jnp.sum(jnp.mean(err * err, axis=-1, keepdims=True), axis=0, keepdims=True)
        _acc_rows(loss_ref, i, jnp.broadcast_to(part, (1, LANES)))
        _acc_rows(dpg_ref, i, jnp.sum(dx * n, axis=0, keepdims=True))
        dy_ref[...] = _norm_bwd(dx, n, r, pg_ref[...]).astype(BF16)

    return pl.pallas_call(
        body, name=name, grid=(t // tr,),
        in_specs=[_row_spec(tr, d), _row_spec(tr, d), _vec_spec(d), _row_spec(tr, d)],
        out_specs=[_vec_spec(LANES), _row_spec(tr, d), _row_spec(tr, d), _vec_spec(d)],
        out_shape=[jax.ShapeDtypeStruct((1, LANES), F32), jax.ShapeDtypeStruct((t, d), F32),
                   jax.ShapeDtypeStruct((t, d), BF16), jax.ShapeDtypeStruct((1, d), F32)],
        compiler_params=_cparams(("arbitrary",)),
    )(x, y, post_gain, target)


def _mid_bwd(dx_out, dh, x, pre_gain, y_prev, post_gain_prev, *, name, tr=256):
    t, d = x.shape
    tr = _pick(t, tr, 8)

    def body(dxo_ref, dh_ref, x_ref, ng_ref, y_ref, pg_ref, dx_ref, dy_ref, dng_ref, dpg_ref):
        i = pl.program_id(0)
        xv = x_ref[...]
        r = _rstd(xv)
        xh = xv * r
        dhv = dh_ref[...]
        _acc_rows(dng_ref, i, jnp.sum(dhv * xh, axis=0, keepdims=True))
        dx = dxo_ref[...] + _norm_bwd(dhv, xh, r, ng_ref[...])
        dx_ref[...] = dx
        yv = y_ref[...]
        ry = _rstd(yv)
        n = yv * ry
        _acc_rows(dpg_ref, i, jnp.sum(dx * n, axis=0, keepdims=True))
        dy_ref[...] = _norm_bwd(dx, n, ry, pg_ref[...]).astype(BF16)

    return pl.pallas_call(
        body, name=name, grid=(t // tr,),
        in_specs=[_row_spec(tr, d), _row_spec(tr, d), _row_spec(tr, d), _vec_spec(d), _row_spec(tr, d), _vec_spec(d)],
        out_specs=[_row_spec(tr, d), _row_spec(tr, d), _vec_spec(d), _vec_spec(d)],
        out_shape=[jax.ShapeDtypeStruct((t, d), F32), jax.ShapeDtypeStruct((t, d), BF16),
                   jax.ShapeDtypeStruct((1, d), F32), jax.ShapeDtypeStruct((1, d), F32)],
        compiler_params=_cparams(("arbitrary",)),
    )(dx_out, dh, x, pre_gain, y_prev, post_gain_prev)


def _first_bwd(dx_out, dh, x, pre_gain, *, name, tr=256):
    t, d = x.shape
    tr = _pick(t, tr, 8)

    def body(dxo_ref, dh_ref, x_ref, ng_ref, dx_ref, dng_ref):
        i = pl.program_id(0)
        xv = x_ref[...]
        r = _rstd(xv)
        xh = xv * r
        dhv = dh_ref[...]
        _acc_rows(dng_ref, i, jnp.sum(dhv * xh, axis=0, keepdims=True))
        dx_ref[...] = dxo_ref[...] + _norm_bwd(dhv, xh, r, ng_ref[...])

    return pl.pallas_call(
        body, name=name, grid=(t // tr,),
        in_specs=[_row_spec(tr, d), _row_spec(tr, d), _row_spec(tr, d), _vec_spec(d)],
        out_specs=[_row_spec(tr, d), _vec_spec(d)],
        out_shape=[jax.ShapeDtypeStruct((t, d), F32), jax.ShapeDtypeStruct((1, d), F32)],
        compiler_params=_cparams(("arbitrary",)),
    )(dx_out, dh, x, pre_gain)


def _sigmoid(x):
    return 1.0 / (1.0 + jnp.exp(-x))


def _log_sigmoid(x):
    return jnp.minimum(x, 0.0) - jnp.log(1.0 + jnp.exp(-jnp.abs(x)))


_GELU_C = math.sqrt(2.0 / math.pi)


_GELU_A = 0.044715


def _gelu_parts(x, with_grad=True):
    x2 = x * x
    h = 0.5 * jnp.tanh(x * (_GELU_C + (_GELU_C * _GELU_A) * x2)) + 0.5
    val = x * h
    if not with_grad:
        return val, None
    return val, h * (1.0 + (1.0 - h) * (x * (2.0 * _GELU_C + (6.0 * _GELU_C * _GELU_A) * x2)))


def _split3(x):
    hi = x.astype(BF16)
    r1 = x - hi.astype(F32)
    mid = r1.astype(BF16)
    lo = (r1 - mid.astype(F32)).astype(BF16)
    return hi, mid, lo


def _tri_matmul(tri_bf16, x):
    hi, mid, lo = _split3(x)
    return _dot_nn(tri_bf16, hi) + _dot_nn(tri_bf16, mid) + _dot_nn(tri_bf16, lo)


def _gla_dims(d):
    dk, dv = d // 2, d
    return dk, dv, dk // GLA_HEADS, dv // GLA_HEADS


def _col_pieces(a, b, lay):
    ws, wp = lay
    out = []
    while a < b:
        j = a // ws
        end = min(b, (j + 1) * ws)
        out.append((j * wp + a - j * ws, end - a))
        a = end
    return out


def _load_cols(ref, a, b, lay):
    parts = [ref[:, s:s + n] for s, n in _col_pieces(a, b, lay)]
    return parts[0] if len(parts) == 1 else jnp.concatenate(parts, axis=1)


def _store_cols(ref, a, val, lay):
    off = 0
    for s, n in _col_pieces(a, a + val.shape[1], lay):
        ref[:, s:s + n] = val[:, off:off + n]
        off += n


def _gate_window(c_r, lay):
    (start, _), = _col_pieces(c_r, c_r + GLA_GATE_RANK, lay)
    assert (start % lay[1]) + LANES <= lay[1]
    return slice(start, start + LANES)


def _gla_gates(glr, k, w2_ref, b_ref):
    z = _dot_nn(glr.astype(BF16), w2_ref[...].astype(BF16)) + b_ref[...]
    la = _log_sigmoid(z) * (1.0 / GLA_TAU)
    row = lax.broadcasted_iota(jnp.int32, (CHUNK, CHUNK), 0)
    col = lax.broadcasted_iota(jnp.int32, (CHUNK, CHUNK), 1)
    incl = (row >= col).astype(BF16)
    bcum = _tri_matmul(incl, la)
    b_end = bcum[CHUNK - 1:CHUNK, :]
    e_rest = jnp.exp(b_end - bcum)
    return z, e_rest, k * e_rest, jnp.exp(b_end)


def _gla_fwd(proj, w2p, b_gate, o_gain, lay, *, name):
    t, wcols = proj.shape
    d = o_gain.shape[1]
    dk, dv, dkh, dvh = _gla_dims(d)
    nc = t // CHUNK
    c_k, c_v, c_g, c_r = dk, 2 * dk, 2 * dk + dv, 2 * dk + 2 * dv
    scale = dkh ** -0.5

    def body(p_ref, w2_ref, b_ref, og_ref, o_ref, a_ref, sb_ref, sfin_ref, s_ref):
        i = pl.program_id(0)

        @pl.when(i == 0)
        def _():
            s_ref[...] = jnp.zeros_like(s_ref)

        q = _load_cols(p_ref, 0, dk, lay) * scale
        k = _load_cols(p_ref, c_k, c_k + dk, lay)
        glr = p_ref[:, _gate_window(c_r, lay)]
        _, _, kdec, decay = _gla_gates(glr, k, w2_ref, b_ref)
        for h in range(GLA_HEADS):
            ks = slice(h * dkh, (h + 1) * dkh)
            vs = slice(h * dvh, (h + 1) * dvh)
            v_h = _load_cols(p_ref, c_v + h * dvh, c_v + (h + 1) * dvh, lay)
            g_h = _load_cols(p_ref, c_g + h * dvh, c_g + (h + 1) * dvh, lay)
            s_old = s_ref[h]
            sb_ref[0, h] = s_old
            s_new = s_old * decay[:, ks] + _dot_tn(v_h.astype(BF16), kdec[:, ks].astype(BF16))
            s_ref[h] = s_new
            o_h = _dot_nt(q[:, ks].astype(BF16), s_new.astype(BF16))
            o_ref[:, vs] = o_h
            on = o_h * _rstd(o_h)
            a_ref[:, vs] = (on * og_ref[:, vs] * (g_h * _sigmoid(g_h))).astype(BF16)

        @pl.when(i == nc - 1)
        def _():
            sfin_ref[...] = s_ref[...]

    full = lambda *shape: pl.BlockSpec(shape, lambda i: (0,) * len(shape))
    return pl.pallas_call(
        body, name=name, grid=(nc,),
        in_specs=[pl.BlockSpec((CHUNK, wcols), lambda i: (i, 0)), full(LANES, dk), full(1, dk), full(1, dv)],
        out_specs=[pl.BlockSpec((CHUNK, dv), lambda i: (i, 0)), pl.BlockSpec((CHUNK, dv), lambda i: (i, 0)),
                   pl.BlockSpec((1, GLA_HEADS, dvh, dkh), lambda i: (i, 0, 0, 0)), full(GLA_HEADS, dvh, dkh)],
        out_shape=[jax.ShapeDtypeStruct((t, dv), F32), jax.ShapeDtypeStruct((t, dv), BF16),
                   jax.ShapeDtypeStruct((nc, GLA_HEADS, dvh, dkh), F32),
                   jax.ShapeDtypeStruct((GLA_HEADS, dvh, dkh), F32)],
        scratch_shapes=[pltpu.VMEM((GLA_HEADS, dvh, dkh), F32)],
        compiler_params=_cparams(("arbitrary",)),
    )(proj, w2p, b_gate, o_gain)


def _gla_bwd(da, o, proj, w2p, b_gate, o_gain, s_before, s_final, lay, *, name):
    t, wcols = proj.shape
    d = o_gain.shape[1]
    dk, dv, dkh, dvh = _gla_dims(d)
    nc = t // CHUNK
    c_k, c_v, c_g, c_r = dk, 2 * dk, 2 * dk + dv, 2 * dk + 2 * dv
    scale = dkh ** -0.5

    def body(da_ref, o_ref, p_ref, w2_ref, b_ref, og_ref, sb_ref, sfin_ref,
             dp_ref, dog_ref, db_ref, dw2_ref, s_ref, gc_ref, dkd_ref):
        i = pl.program_id(0)

        @pl.when(i == 0)
        def _():
            s_ref[...] = sfin_ref[...]
            gc_ref[...] = jnp.zeros_like(gc_ref)

        ws, wp = lay
        for j in range(N_CHIPS):
            dp_ref[:, j * wp + ws:(j + 1) * wp] = jnp.zeros((CHUNK, wp - ws), BF16)
        q = _load_cols(p_ref, 0, dk, lay) * scale
        k = _load_cols(p_ref, c_k, c_k + dk, lay)
        glr = p_ref[:, _gate_window(c_r, lay)]
        z, e_rest, kdec, decay = _gla_gates(glr, k, w2_ref, b_ref)
        ddecay = []
        for h in range(GLA_HEADS):
            ks = slice(h * dkh, (h + 1) * dkh)
            vs = slice(h * dvh, (h + 1) * dvh)
            v_h = _load_cols(p_ref, c_v + h * dvh, c_v + (h + 1) * dvh, lay)
            g_h = _load_cols(p_ref, c_g + h * dvh, c_g + (h + 1) * dvh, lay)
            da_h = da_ref[:, vs]
            o_h = o_ref[:, vs]
            og_h = og_ref[:, vs]
            r = _rstd(o_h)
            on = o_h * r
            sg = _sigmoid(g_h)
            silu = g_h * sg
            _acc_rows(dog_ref, i, jnp.sum(da_h * silu * on, axis=0, keepdims=True), vs)
            _store_cols(dp_ref, c_g + h * dvh, (da_h * (on * og_h) * (sg * (1.0 + g_h * (1.0 - sg)))).astype(BF16),
                        lay)
            don = da_h * silu * og_h
            do_h = (r * (don - on * jnp.mean(don * on, axis=-1, keepdims=True))).astype(BF16)
            s_cur = s_ref[h]
            _store_cols(dp_ref, h * dkh, (_dot_nn(do_h, s_cur.astype(BF16)) * scale).astype(BF16), lay)
            g_tot = gc_ref[h] + _dot_tn(do_h, q[:, ks].astype(BF16))
            g_bf = g_tot.astype(BF16)
            dkd_ref[:, ks] = _dot_nn(v_h.astype(BF16), g_bf)
            _store_cols(dp_ref, c_v + h * dvh, _dot_nt(kdec[:, ks].astype(BF16), g_bf).astype(BF16), lay)
            s_prev = sb_ref[0, h]
            ddecay.append(jnp.sum(g_tot * s_prev, axis=0, keepdims=True))
            gc_ref[h] = g_tot * decay[:, ks]
            s_ref[h] = s_prev
        dkdec = dkd_ref[...]
        _store_cols(dp_ref, c_k, (dkdec * e_rest).astype(BF16), lay)
        d_e = dkdec * kdec
        row = lax.broadcasted_iota(jnp.int32, (CHUNK, CHUNK), 0)
        col = lax.broadcasted_iota(jnp.int32, (CHUNK, CHUNK), 1)
        excl = (row > col).astype(BF16)
        dla = jnp.concatenate(ddecay, axis=1) * decay + _tri_matmul(excl, d_e)
        dz = dla * (1.0 / GLA_TAU) * (1.0 - _sigmoid(z))
        _acc_rows(db_ref, i, jnp.sum(dz, axis=0, keepdims=True))
        dz_bf = dz.astype(BF16)
        dw2 = _dot_tn(glr.astype(BF16), dz_bf)

        @pl.when(i == 0)
        def _():
            dw2_ref[...] = dw2

        @pl.when(i > 0)
        def _():
            dw2_ref[...] += dw2

        dp_ref[:, _gate_window(c_r, lay)] = _dot_nt(dz_bf, w2_ref[...].astype(BF16)).astype(BF16)

    rev = lambda i: (nc - 1 - i, 0)
    full = lambda *shape: pl.BlockSpec(shape, lambda i: (0,) * len(shape))
    return pl.pallas_call(
        body, name=name, grid=(nc,),
        in_specs=[pl.BlockSpec((CHUNK, dv), rev), pl.BlockSpec((CHUNK, dv), rev), pl.BlockSpec((CHUNK, wcols), rev),
                  full(LANES, dk), full(1, dk), full(1, dv),
                  pl.BlockSpec((1, GLA_HEADS, dvh, dkh), lambda i: (nc - 1 - i, 0, 0, 0)), full(GLA_HEADS, dvh, dkh)],
        out_specs=[pl.BlockSpec((CHUNK, wcols), rev), full(1, dv), full(1, dk), full(LANES, dk)],
        out_shape=[jax.ShapeDtypeStruct((t, wcols), BF16), jax.ShapeDtypeStruct((1, dv), F32),
                   jax.ShapeDtypeStruct((1, dk), F32), jax.ShapeDtypeStruct((LANES, dk), F32)],
        scratch_shapes=[pltpu.VMEM((GLA_HEADS, dvh, dkh), F32), pltpu.VMEM((GLA_HEADS, dvh, dkh), F32),
                        pltpu.VMEM((CHUNK, dk), F32)],
        compiler_params=_cparams(("arbitrary",)),
    )(da, o, proj, w2p, b_gate, o_gain, s_before, s_final)


def _sgu_mid(p_ref, lg_ref, lb_ref, ws_ref, bst_ref, w, with_grad=True):
    gd = w // SGU_GROUPS
    u_act, du_fac = _gelu_parts(p_ref[:, 0:w], with_grad)
    vf, dv_fac = _gelu_parts(p_ref[:, w:2 * w], with_grad)
    mu = jnp.mean(vf, axis=-1, keepdims=True)
    cen = vf - mu
    rstd = lax.rsqrt(jnp.mean(cen * cen, axis=-1, keepdims=True) + EPS)
    xh = cen * rstd
    vn = (xh * lg_ref[...] + lb_ref[...]).astype(BF16)
    vs = [_dot_nn(ws_ref[g].astype(BF16), vn[:, g * gd:(g + 1) * gd]) + bst_ref[:, g:g + 1]
          for g in range(SGU_GROUPS)]
    return u_act, du_fac, dv_fac, rstd, xh, vn, vs


def _sgu_fwd(proj, ln_gain, ln_bias, ws_masked, bs_t, *, name):
    t, w3 = proj.shape
    w = w3 // 3
    gd = w // SGU_GROUPS
    nb = t // SGU_BLOCK

    def body(p_ref, lg_ref, lb_ref, ws_ref, bst_ref, a_ref):
        u_act, _, _, _, _, _, vs = _sgu_mid(p_ref, lg_ref, lb_ref, ws_ref, bst_ref, w, with_grad=False)
        for g in range(SGU_GROUPS):
            cs = slice(g * gd, (g + 1) * gd)
            gate = p_ref[:, 2 * w + g * gd:2 * w + (g + 1) * gd]
            a_ref[:, cs] = (u_act[:, cs] * vs[g] * (gate * _sigmoid(gate))).astype(BF16)

    full = lambda *shape: pl.BlockSpec(shape, lambda i: (0,) * len(shape))
    return pl.pallas_call(
        body, name=name, grid=(nb,),
        in_specs=[pl.BlockSpec((SGU_BLOCK, w3), lambda i: (i, 0)), full(1, w), full(1, w),
                  full(SGU_GROUPS, SGU_BLOCK, SGU_BLOCK), full(SGU_BLOCK, SGU_GROUPS)],
        out_specs=pl.BlockSpec((SGU_BLOCK, w), lambda i: (i, 0)),
        out_shape=jax.ShapeDtypeStruct((t, w), BF16),
        compiler_params=_cparams(("parallel",)),
    )(proj, ln_gain, ln_bias, ws_masked, bs_t)


def _sgu_bwd(da, proj, ln_gain, ln_bias, ws_masked, ws_masked_t, bs_t, *, name):
    t, w3 = proj.shape
    w = w3 // 3
    gd = w // SGU_GROUPS
    nb = t // SGU_BLOCK

    def body(da_ref, p_ref, lg_ref, lb_ref, ws_ref, wst_ref, bst_ref, dp_ref, dws_ref, dbst_ref, dlg_ref, dlb_ref,
             dvn_ref):
        i = pl.program_id(0)
        u_act, du_fac, dv_fac, rstd, xh, vn, vs = _sgu_mid(p_ref, lg_ref, lb_ref, ws_ref, bst_ref, w)
        for g in range(SGU_GROUPS):
            cs = slice(g * gd, (g + 1) * gd)
            gate = p_ref[:, 2 * w + g * gd:2 * w + (g + 1) * gd]
            sg = _sigmoid(gate)
            silu = gate * sg
            da_g = da_ref[:, cs]
            ua_g = u_act[:, cs]
            dp_ref[:, cs] = (da_g * vs[g] * silu * du_fac[:, cs]).astype(BF16)
            dp_ref[:, 2 * w + g * gd:2 * w + (g + 1) * gd] = (
                da_g * ua_g * vs[g] * (sg * (1.0 + gate * (1.0 - sg)))).astype(BF16)
            dvs = da_g * ua_g * silu
            dvs_bf = dvs.astype(BF16)
            dvn_ref[:, cs] = _dot_nn(wst_ref[g].astype(BF16), dvs_bf)
            dws = _dot_nt(dvs_bf, vn[:, cs])
            dbs = jnp.sum(dvs, axis=1, keepdims=True)

            @pl.when(i == 0)
            def _():
                dws_ref[g] = dws
                dbst_ref[:, g:g + 1] = dbs

            @pl.when(i > 0)
            def _():
                dws_ref[g] += dws
                dbst_ref[:, g:g + 1] += dbs

        dvn = dvn_ref[...]
        _acc_rows(dlg_ref, i, jnp.sum(dvn * xh, axis=0, keepdims=True))
        _acc_rows(dlb_ref, i, jnp.sum(dvn, axis=0, keepdims=True))
        dxh = dvn * lg_ref[...]
        dvf = rstd * (dxh - jnp.mean(dxh, axis=-1, keepdims=True)
                      - xh * jnp.mean(dxh * xh, axis=-1, keepdims=True))
        dp_ref[:, w:2 * w] = (dvf * dv_fac).astype(BF16)

    full = lambda *shape: pl.BlockSpec(shape, lambda i: (0,) * len(shape))
    return pl.pallas_call(
        body, name=name, grid=(nb,),
        in_specs=[pl.BlockSpec((SGU_BLOCK, w), lambda i: (i, 0)), pl.BlockSpec((SGU_BLOCK, w3), lambda i: (i, 0)),
                  full(1, w), full(1, w), full(SGU_GROUPS, SGU_BLOCK, SGU_BLOCK),
                  full(SGU_GROUPS, SGU_BLOCK, SGU_BLOCK), full(SGU_BLOCK, SGU_GROUPS)],
        out_specs=[pl.BlockSpec((SGU_BLOCK, w3), lambda i: (i, 0)), full(SGU_GROUPS, SGU_BLOCK, SGU_BLOCK),
                   full(SGU_BLOCK, SGU_GROUPS), full(1, w), full(1, w)],
        out_shape=[jax.ShapeDtypeStruct((t, w3), BF16), jax.ShapeDtypeStruct((SGU_GROUPS, SGU_BLOCK, SGU_BLOCK), F32),
                   jax.ShapeDtypeStruct((SGU_BLOCK, SGU_GROUPS), F32), jax.ShapeDtypeStruct((1, w), F32),
                   jax.ShapeDtypeStruct((1, w), F32)],
        scratch_shapes=[pltpu.VMEM((SGU_BLOCK, w), F32)],
        compiler_params=_cparams(("arbitrary",)),
    )(da, proj, ln_gain, ln_bias, ws_masked, ws_masked_t, bs_t)


def _tile2d(rows, cols, block_bytes, row_unit):
    if rows % row_unit == 0:
        return _pick(rows, max(row_unit, block_bytes // (4 * cols)), row_unit), cols
    return rows, _pick(cols, max(LANES, block_bytes // (4 * rows)))


def _adamw(w, g, m, v, *, name, block_bytes=1 << 20, after=None):
    rows, cols = w.shape
    tr, tc = _tile2d(rows, cols, block_bytes, 8)
    g_rows = g.shape[0]
    assert g_rows == rows or tr == rows
    extra_specs, extra_args = ([], []) if after is None else ([pl.BlockSpec(memory_space=pl.ANY)], [after])

    def body(w_ref, g_ref, m_ref, v_ref, *rest):
        go_ref, d_ref, mo_ref, vo_ref = rest[len(extra_args):]
        gv = g_ref[0:tr, :]
        go_ref[...] = gv
        mn = ADAM_B1 * m_ref[...] + (1.0 - ADAM_B1) * gv
        vn = ADAM_B2 * v_ref[...] + (1.0 - ADAM_B2) * (gv * gv)
        m_hat = mn / (1.0 - ADAM_B1 ** ADAM_STEP)
        v_hat = vn / (1.0 - ADAM_B2 ** ADAM_STEP)
        d_ref[...] = -ADAM_LR * (m_hat / (jnp.sqrt(v_hat) + ADAM_EPS) + ADAM_WD * w_ref[...])
        mo_ref[...] = mn
        vo_ref[...] = vn

    spec = pl.BlockSpec((tr, tc), lambda i, j: (i, j))
    g_spec = spec if g_rows == rows else pl.BlockSpec((g_rows, tc), lambda i, j: (0, j))
    return pl.pallas_call(
        body, name=name, grid=(rows // tr, cols // tc), in_specs=[spec, g_spec, spec, spec] + extra_specs,
        out_specs=[spec] * 4, out_shape=[jax.ShapeDtypeStruct((rows, cols), F32)] * 4,
        compiler_params=_cparams(("parallel", "parallel")),
    )(w, g, m, v, *extra_args)


def _pair_sum_bf16(own, core_idx, peer, *, name, block_bytes=1 << 20):
    s, r, c = own.shape
    hc = c // 2
    tr, tc = _tile2d(r, hc, block_bytes, 16)
    ncb = hc // tc

    def body(h_ref, a_ref, b_ref, o_ref):
        o_ref[...] = (a_ref[...] + b_ref[...]).astype(BF16)

    grid_spec = pltpu.PrefetchScalarGridSpec(
        num_scalar_prefetch=1, grid=(s, r // tr, ncb),
        in_specs=[pl.BlockSpec((None, tr, tc), lambda j, i, k, h: (j, i, h[0] * ncb + k)),
                  pl.BlockSpec((None, tr, tc), lambda j, i, k, h: (j, i, k))],
        out_specs=pl.BlockSpec((None, tr, tc), lambda j, i, k, h: (j, i, k)))
    return pl.pallas_call(
        body, name=name, grid_spec=grid_spec, out_shape=jax.ShapeDtypeStruct((s, r, hc), BF16),
        compiler_params=_cparams(("parallel", "parallel", "parallel")),
    )(core_idx, own, peer)


def _chip_sum(pair, landed, slots, *, name, block_bytes=1 << 20):
    _, r, hc = pair.shape
    tr, tc = _tile2d(r, hc, block_bytes, 16)
    ncb = hc // tc

    def body(s_ref, own_ref, l0_ref, l1_ref, l2_ref, o_ref):
        o_ref[...] = ((own_ref[...].astype(F32) + l0_ref[...].astype(F32)) + l1_ref[...].astype(F32)
                      ) + l2_ref[...].astype(F32)

    def slab(which):
        return pl.BlockSpec((None, tr, tc), lambda i, k, s: (s[which], i, k))

    grid_spec = pltpu.PrefetchScalarGridSpec(
        num_scalar_prefetch=1, grid=(r // tr, ncb),
        in_specs=[slab(0), slab(1), slab(2), slab(3)],
        out_specs=pl.BlockSpec((tr, tc), lambda i, k, s: (i, s[4] * ncb + k)))
    return pl.pallas_call(
        body, name=name, grid_spec=grid_spec, out_shape=jax.ShapeDtypeStruct((r, 2 * hc), F32),
        compiler_params=_cparams(("parallel", "parallel")),
    )(slots, pair, landed, landed, landed)


def _stack_sum(x, *, name, out_dtype=F32, block_bytes=1 << 20):
    s, r, c = x.shape
    tr = _pick(r, max(8, block_bytes // (4 * c)), 16) if r % 16 == 0 else r

    def body(x_ref, o_ref):
        acc = x_ref[0].astype(F32)
        for j in range(1, s):
            acc = acc + x_ref[j].astype(F32)
        o_ref[...] = acc.astype(out_dtype)

    return pl.pallas_call(
        body, name=name, grid=(r // tr,),
        in_specs=[pl.BlockSpec((s, tr, c), lambda i: (0, i, 0))], out_specs=pl.BlockSpec((tr, c), lambda i: (i, 0)),
        out_shape=jax.ShapeDtypeStruct((r, c), out_dtype), compiler_params=_cparams(("parallel",)),
    )(x)


HBM = pl.BlockSpec(memory_space=pltpu.HBM)


def _place():
    x, y, c = lax.axis_index("x"), lax.axis_index("y"), lax.axis_index("c")
    other_chips = [(1 - x, y), (x, 1 - y), (1 - x, 1 - y)]
    return x, y, c, other_chips


def _half_cols(cols, which):
    hc = cols // 2
    return pl.ds(pl.multiple_of(which * hc, LANES), hc)


SEM = pl.BlockSpec(memory_space=pltpu.SEMAPHORE)
ANY = pl.BlockSpec(memory_space=pl.ANY)
SIDE_EFFECT = pltpu.SideEffectType.DATAFLOW_SIDE_EFFECTING
TOKEN_SHAPE = (8, LANES)


def _hbm(shape, dtype):
    return pltpu.HBM(shape, dtype)


def _in_hbm(a):
    return pltpu.with_memory_space_constraint(a, pltpu.HBM)


def _gather_copy(src_ref, land_ref, ssem, rsem, k, chip_of_block, to, c):
    cols = src_ref.shape[1]
    return pltpu.make_async_remote_copy(
        src_ref=src_ref.at[:, _half_cols(cols, c)], dst_ref=land_ref.at[chip_of_block, :, _half_cols(cols, c)],
        send_sem=ssem.at[k], recv_sem=rsem.at[k], device_id=to, device_id_type=MESH)


def _gather_start(shards, *, name, after=()):
    n = len(shards)
    after = list(after)

    def body(*refs):
        srcs, lands = refs[:n], refs[n:2 * n]
        outs = refs[2 * n + len(after):]
        token = outs[-1]
        x, y, c, chips = _place()
        me = 2 * x + y
        for a in range(n):
            ssem, rsem = outs[4 * a], outs[4 * a + 1]
            for k, (cx, cy) in enumerate(chips):
                _gather_copy(srcs[a], lands[a], ssem, rsem, k, me, (cx, cy, c), c).start()
        token[...] = jnp.zeros_like(token)

    out_shape, out_specs, aliases = [], [], {}
    for a, s in enumerate(shards):
        out_shape += [pltpu.SemaphoreType.DMA((3,)), pltpu.SemaphoreType.DMA((3,)), _hbm(s.shape, s.dtype),
                      _hbm((N_CHIPS,) + s.shape, s.dtype)]
        out_specs += [SEM, SEM, HBM, HBM]
        aliases[a] = 4 * a + 2
        aliases[n + a] = 4 * a + 3
    out_shape.append(jax.ShapeDtypeStruct(TOKEN_SHAPE, F32))
    out_specs.append(pl.BlockSpec(memory_space=pltpu.VMEM))
    lands = [_in_hbm(lax.empty((N_CHIPS,) + s.shape, s.dtype)) for s in shards]
    res = pl.pallas_call(
        body, name=name, in_specs=[HBM] * (2 * n) + [ANY] * len(after), out_specs=out_specs, out_shape=out_shape,
        input_output_aliases=aliases, compiler_params=pltpu.CompilerParams(has_side_effects=SIDE_EFFECT),
    )(*[_in_hbm(s) for s in shards], *lands, *after)
    return [tuple(res[4 * a:4 * a + 4]) for a in range(n)], res[-1]


def _wait_call(wait_fn, parts, after, *, name):
    ssem, rsem, src, land = parts
    after = list(after) if isinstance(after, (list, tuple)) else [after]

    def body(src_ref, land_ref, ssem_ref, rsem_ref, *rest):
        wait_fn(src_ref, land_ref, ssem_ref, rsem_ref)

    return pl.pallas_call(
        body, name=name, in_specs=[HBM, HBM, SEM, SEM] + [ANY] * len(after), out_specs=[HBM, HBM],
        out_shape=[_hbm(src.shape, src.dtype), _hbm(land.shape, land.dtype)], input_output_aliases={0: 0, 1: 1},
        compiler_params=pltpu.CompilerParams(has_side_effects=SIDE_EFFECT),
    )(src, land, ssem, rsem, *after)


ALL_CHIPS = (0, 1, 2)


def _gather_wait(parts, after, *, name, ks=ALL_CHIPS):
    def wait(src_ref, land_ref, ssem_ref, rsem_ref):
        x, y, c, chips = _place()
        for k in ks:
            cx, cy = chips[k]
            cp = _gather_copy(src_ref, land_ref, ssem_ref, rsem_ref, k, 2 * cx + cy, (x, y, c), c)
            cp.wait_send()
            cp.wait_recv()

    src, land = _wait_call(wait, parts, after, name=name)
    return (parts[0], parts[1], src, land)


def _forward_copy(buf_ref, ssem, rsem, k, slab, which, to):
    part = buf_ref.at[slab, :, _half_cols(buf_ref.shape[2], which)]
    return pltpu.make_async_remote_copy(
        src_ref=part, dst_ref=part, send_sem=ssem.at[k], recv_sem=rsem.at[k], device_id=to, device_id_type=MESH)


def _sibling_forward(land, *, name, ks=ALL_CHIPS):
    def body(_, buf, send_sems, recv_sems):
        x, y, c, chips = _place()
        copies = []
        for k in ks:
            cx, cy = chips[k]
            cp = _forward_copy(buf, send_sems, recv_sems, k, 2 * cx + cy, c, (x, y, 1 - c))
            cp.start()
            copies.append(cp)
        for k in ks:
            cx, cy = chips[k]
            _forward_copy(buf, send_sems, recv_sems, k, 2 * cx + cy, 1 - c, (x, y, c)).wait_recv()
        for cp in copies:
            cp.wait_send()

    return pl.pallas_call(
        body, name=name, in_specs=[HBM], out_specs=HBM, out_shape=jax.ShapeDtypeStruct(land.shape, land.dtype),
        input_output_aliases={0: 0},
        scratch_shapes=[pltpu.SemaphoreType.DMA((3,)), pltpu.SemaphoreType.DMA((3,))],
    )(land)


def _share_copy(buf_ref, ssem, rsem, a, which, to):
    part = buf_ref.at[:, _half_cols(buf_ref.shape[1], which)]
    return pltpu.make_async_remote_copy(
        src_ref=part, dst_ref=part, send_sem=ssem.at[a], recv_sem=rsem.at[a], device_id=to, device_id_type=MESH)


def _share_start(arrays, *, name):
    n = len(arrays)

    def body(*refs):
        bufs, ssem, rsem, token = refs[:n], refs[n], refs[n + 1], refs[-1]
        x, y, c, _ = _place()
        for a in range(n):
            _share_copy(bufs[a], ssem, rsem, a, c, (x, y, 1 - c)).start()
        token[...] = jnp.zeros_like(token)

    res = pl.pallas_call(
        body, name=name, in_specs=[HBM] * n,
        out_specs=[SEM, SEM] + [HBM] * n + [pl.BlockSpec(memory_space=pltpu.VMEM)],
        out_shape=[pltpu.SemaphoreType.DMA((n,)), pltpu.SemaphoreType.DMA((n,))]
        + [_hbm(b.shape, b.dtype) for b in arrays] + [jax.ShapeDtypeStruct(TOKEN_SHAPE, F32)],
        input_output_aliases={a: 2 + a for a in range(n)},
        compiler_params=pltpu.CompilerParams(has_side_effects=SIDE_EFFECT),
    )(*[_in_hbm(b) for b in arrays])
    return (res[0], res[1], list(res[2:2 + n])), res[-1]


def _share_wait(parts, after, *, name):
    ssem, rsem, bufs = parts
    n = len(bufs)
    after = list(after) if isinstance(after, (list, tuple)) else [after]

    def body(*refs):
        buf_refs, ssem_ref, rsem_ref = refs[:n], refs[n], refs[n + 1]
        x, y, c, _ = _place()
        for a in range(n):
            _share_copy(buf_refs[a], ssem_ref, rsem_ref, a, c, (x, y, c)).wait_send()
            _share_copy(buf_refs[a], ssem_ref, rsem_ref, a, 1 - c, (x, y, c)).wait_recv()

    return pl.pallas_call(
        body, name=name, in_specs=[HBM] * n + [SEM, SEM] + [ANY] * len(after), out_specs=[HBM] * n,
        out_shape=[_hbm(b.shape, b.dtype) for b in bufs], input_output_aliases={a: a for a in range(n)},
        compiler_params=pltpu.CompilerParams(has_side_effects=SIDE_EFFECT),
    )(*bufs, ssem, rsem, *after)


def _scatter_copy(src_ref, land_ref, ssem, rsem, k, src_slab, dst_slab, to):
    return pltpu.make_async_remote_copy(
        src_ref=src_ref.at[src_slab], dst_ref=land_ref.at[dst_slab], send_sem=ssem.at[k], recv_sem=rsem.at[k],
        device_id=to, device_id_type=MESH)


def _scatter_start(part, *, name):
    def start(src_ref, land_ref, ssem, rsem):
        x, y, c, chips = _place()
        me = 2 * x + y
        for k, (cx, cy) in enumerate(chips):
            _scatter_copy(src_ref, land_ref, ssem, rsem, k, 2 * cx + cy, me, (cx, cy, c)).start()

    return _split_start(start, part, part.shape, N_CHIPS - 1, name=name)


def _scatter_wait(parts, after, *, name):
    def wait(src_ref, land_ref, ssem_ref, rsem_ref):
        x, y, c, chips = _place()
        for k, (cx, cy) in enumerate(chips):
            idx = 2 * cx + cy
            cp = _scatter_copy(src_ref, land_ref, ssem_ref, rsem_ref, k, idx, idx, (x, y, c))
            cp.wait_send()
            cp.wait_recv()

    return _wait_call(wait, parts, after, name=name)


def _split_start(start_fn, src, land_shape, n_sems, *, name):
    def body(src_ref, land_ref, ssem, rsem, src_out, land_out, token):
        start_fn(src_ref, land_ref, ssem, rsem)
        token[...] = jnp.zeros_like(token)

    res = pl.pallas_call(
        body, name=name, in_specs=[HBM, HBM], out_specs=[SEM, SEM, HBM, HBM, pl.BlockSpec(memory_space=pltpu.VMEM)],
        out_shape=[pltpu.SemaphoreType.DMA((n_sems,)), pltpu.SemaphoreType.DMA((n_sems,)), _hbm(src.shape, src.dtype),
                   _hbm(land_shape, src.dtype), jax.ShapeDtypeStruct(TOKEN_SHAPE, F32)],
        input_output_aliases={0: 2, 1: 3}, compiler_params=pltpu.CompilerParams(has_side_effects=SIDE_EFFECT),
    )(_in_hbm(src), _in_hbm(lax.empty(land_shape, src.dtype)))
    return tuple(res[:4]), res[4]


def _swap_copy(src_ref, land_ref, ssem, rsem, which, to):
    return pltpu.make_async_remote_copy(
        src_ref=src_ref.at[:, :, _half_cols(src_ref.shape[2], which)], dst_ref=land_ref,
        send_sem=ssem.at[0], recv_sem=rsem.at[0], device_id=to, device_id_type=MESH)


def _swap_start(grad, *, name):
    def start(src_ref, land_ref, ssem, rsem):
        x, y, c, _ = _place()
        _swap_copy(src_ref, land_ref, ssem, rsem, 1 - c, (x, y, 1 - c)).start()

    s, r, cols = grad.shape
    return _split_start(start, grad, (s, r, cols // 2), 1, name=name)


def _swap_wait(parts, after, *, name):
    def wait(src_ref, land_ref, ssem_ref, rsem_ref):
        x, y, c, _ = _place()
        cp = _swap_copy(src_ref, land_ref, ssem_ref, rsem_ref, 1 - c, (x, y, c))
        cp.wait_send()
        cp.wait_recv()

    return _wait_call(wait, parts, after, name=name)


def _dev_peers(x, y, c, chips):
    return [(x, y, 1 - c)] + [(cx, cy, c) for cx, cy in chips] + [(cx, cy, 1 - c) for cx, cy in chips]


def _dev_gather_start(part, *, name):
    def start(src_ref, land_ref, ssem, rsem):
        x, y, c, chips = _place()
        for k, to in enumerate(_dev_peers(x, y, c, chips)):
            pltpu.make_async_remote_copy(
                src_ref=src_ref, dst_ref=land_ref.at[4 * x + 2 * y + c], send_sem=ssem.at[k], recv_sem=rsem.at[k],
                device_id=to, device_id_type=MESH).start()

    return _split_start(start, part, (N_DEV,) + part.shape, N_DEV - 1, name=name)


def _dev_gather_wait(parts, after, *, name):
    def wait(src_ref, land_ref, ssem_ref, rsem_ref):
        x, y, c, chips = _place()
        for k, (px, py, pc) in enumerate(_dev_peers(x, y, c, chips)):
            cp = pltpu.make_async_remote_copy(
                src_ref=src_ref, dst_ref=land_ref.at[4 * px + 2 * py + pc], send_sem=ssem_ref.at[k],
                recv_sem=rsem_ref.at[k], device_id=(x, y, c), device_id_type=MESH)
            cp.wait_send()
            cp.wait_recv()

    return _wait_call(wait, parts, after, name=name)[1]


def _sibling_share_halves(arrays, *, name):
    n = len(arrays)

    def body(*refs):
        bufs = refs[n:2 * n]
        send_sems, recv_sems = refs[2 * n:]
        x, y, c, _ = _place()
        copies = []
        for a in range(n):
            mine = bufs[a].at[:, _half_cols(bufs[a].shape[1], c)]
            cp = pltpu.make_async_remote_copy(
                src_ref=mine, dst_ref=mine, send_sem=send_sems.at[a], recv_sem=recv_sems.at[a],
                device_id=(x, y, 1 - c), device_id_type=MESH)
            cp.start()
            copies.append(cp)
        for a in range(n):
            theirs = bufs[a].at[:, _half_cols(bufs[a].shape[1], 1 - c)]
            pltpu.make_async_remote_copy(
                src_ref=theirs, dst_ref=theirs, send_sem=send_sems.at[a], recv_sem=recv_sems.at[a],
                device_id=(x, y, c), device_id_type=MESH).wait_recv()
        for cp in copies:
            cp.wait_send()

    return pl.pallas_call(
        body, name=name, in_specs=[HBM] * n, out_specs=[HBM] * n,
        out_shape=[jax.ShapeDtypeStruct(h.shape, h.dtype) for h in arrays],
        input_output_aliases={a: a for a in range(n)},
        scratch_shapes=[pltpu.SemaphoreType.DMA((n,)), pltpu.SemaphoreType.DMA((n,))],
    )(*arrays)


def _pack(arrays, rows_multiple=16, width=LANES):
    flat = jnp.concatenate([a.astype(F32).reshape(-1) for a in arrays])
    total = flat.shape[0]
    rows = -(-total // width)
    rows = -(-rows // rows_multiple) * rows_multiple
    return jnp.pad(flat, (0, rows * width - total)).reshape(rows, width)


def _unpack(buf, shapes):
    flat = buf.reshape(-1)
    out, off = [], 0
    for s in shapes:
        n = math.prod(s)
        out.append(flat[off:off + n].reshape(s))
        off += n
    return out


def kernel(x, norm_pre, norm_post, gla_w_in, gla_w_gate2, gla_b_gate, gla_o_gain, gla_w_out, sgu_w_in, sgu_ln_gain, sgu_ln_bias, sgu_w_spatial, sgu_b_spatial, sgu_w_out, loss_target, m_norm_pre, m_norm_post, m_gla_w_in, m_gla_w_gate2, m_gla_b_gate, m_gla_o_gain, m_gla_w_out, m_sgu_w_in, m_sgu_ln_gain, m_sgu_ln_bias, m_sgu_w_spatial, m_sgu_b_spatial, m_sgu_w_out, v_norm_pre, v_norm_post, v_gla_w_in, v_gla_w_gate2, v_gla_b_gate, v_gla_o_gain, v_gla_w_out, v_sgu_w_in, v_sgu_ln_gain, v_sgu_ln_bias, v_sgu_w_spatial, v_sgu_b_spatial, v_sgu_w_out):
    _, t, d = x.shape
    dk = d // 2
    ws = gla_w_in.shape[2]
    wp = -(-ws // LANES) * LANES
    lay = (ws, wp)
    chip =2 * lax.axis_index("x") + lax.axis_index("y")
    core = lax.axis_index("c")
    core_idx = core.astype(jnp.int32).reshape(1)
    others = jnp.arange(N_CHIPS - 1, dtype=jnp.int32)
    others = others + (others >= chip).astype(jnp.int32)
    slots = jnp.concatenate([chip.astype(jnp.int32).reshape(1), others, core_idx])

    x0 = x[0]
    target = loss_target[0]

    wt_in_g, mt_in_g, vt_in_g = gla_w_in[0].T, m_gla_w_in[0].T, v_gla_w_in[0].T

    small_shard = _pack([gla_w_gate2[0], sgu_ln_gain[0], sgu_ln_bias[0]], rows_multiple=8, width=2 * LANES)
    own = [small_shard, jnp.pad(wt_in_g.astype(BF16), ((0, wp - ws), (0, 0)))]
    in_flight, token = _gather_start(own, name="gather_start_a")
    own_later = [gla_w_out[0].astype(BF16), sgu_w_in[0].astype(BF16), sgu_w_out[0].astype(BF16)]
    in_flight_later, token_later = _gather_start(own_later, name="gather_start_b", after=[token])
    own, in_flight = own + own_later, in_flight + in_flight_later

    def with_own(i, land):
        return lax.dynamic_update_slice(land, own[i][None], (chip, 0, 0))

    def arrived(i, after, name):
        land = _gather_wait(in_flight[i], after, name=name + "_wait")[3]
        return with_own(i, _sibling_forward(land, name=name + "_share"))

    h0 = _norm_pre(x0, norm_pre[0:1] + token[0:1, 0:1] + token_later[0:1, 0:1], name="pre0")
    g_small = arrived(0, h0, "w_small")
    wt_g = arrived(1, [g_small, wt_in_g, mt_in_g, vt_in_g], "w_gla_in").reshape(N_CHIPS * wp, d)
    shard_shapes = [gla_w_gate2.shape[1:], sgu_ln_gain.shape[1:], sgu_ln_bias.shape[1:]]
    per_chip = [_unpack(g_small[j], shard_shapes) for j in range(N_CHIPS)]
    w2_full = jnp.concatenate([p[0] for p in per_chip], axis=1)
    ln_gain = jnp.concatenate([p[1] for p in per_chip], axis=0)[None, :]
    ln_bias = jnp.concatenate([p[2] for p in per_chip], axis=0)[None, :]
    w2p = jnp.pad(w2_full, ((0, LANES - GLA_GATE_RANK), (0, 0)))

    pos_chunk = jnp.arange(SGU_BLOCK) // CHUNK
    mask = pos_chunk[:, None] >= pos_chunk[None, :]
    ws_masked = jnp.where(mask[None], sgu_w_spatial[0], 0.0)
    ws_masked_t = ws_masked.transpose(0, 2, 1)
    bs_t = sgu_b_spatial[0].T

    proj0 = _matmul(h0, wt_g, mode="nt", out_dtype=F32, name="gla_in", tn=wp)
    o0, a0, s_before, s_final = _gla_fwd(proj0, w2p, gla_b_gate, gla_o_gain, lay, name="gla_scan")
    w_out_g = arrived(2, a0, "w_gla_out").reshape(d, d)
    y0 = _matmul(a0, w_out_g, mode="nn", out_dtype=F32, name="gla_out")
    x1, h1 = _post_then_pre(x0, y0, norm_post[0:1], norm_pre[1:2], name="post0_pre1")
    g_wi_s = arrived(3, h1, "w_sgu_in")
    proj1 = _matmul(h1, g_wi_s, mode="nn", out_dtype=F32, name="sgu_in", b_shards=True)
    a1 = _sgu_fwd(proj1, ln_gain, ln_bias, ws_masked, bs_t, name="sgu_gate")
    w_out_s = arrived(4, a1, "w_sgu_out").reshape(d, d)
    y1 = _matmul(a1, w_out_s, mode="nn", out_dtype=F32, name="sgu_out")
    loss_part, dx2, dy1, d_post1 = _loss_head(x1, y1, norm_post[1:2], target, name="loss_head")

    def behind(small, token):
        return small + token[0:1, 0:1]

    def pair_and_scatter(swap, after, name):
        grad, peer = _swap_wait(swap, after, name=name + "_swap_wait")
        pair = _pair_sum_bf16(grad, core_idx, peer, name=name + "_pair")
        return _scatter_start(pair, name=name + "_start")

    def reduced(flight, after, name):
        pair, landed = _scatter_wait(flight, after, name=name + "_wait")
        return _chip_sum(pair, landed, slots, name=name + "_sum")

    dw_out_s = _matmul(a1, dy1, mode="tn", out_dtype=F32, name="d_sgu_w_out")
    swap, tok = _swap_start(dw_out_s.reshape(N_CHIPS, d // N_CHIPS, d), name="g_sgu_out_swap")
    da1 = _matmul(dy1, w_out_s, mode="nt", out_dtype=F32, name="d_sgu_act", after=tok)
    fl_wo_s, tok = pair_and_scatter(swap, da1, "g_sgu_out")
    dproj1, d_ws, d_bs_t, d_lg, d_lb = _sgu_bwd(da1, proj1, ln_gain, behind(ln_bias, tok), ws_masked, ws_masked_t,
                                                bs_t, name="sgu_gate_bwd")
    dw_in_s = _matmul(h1, dproj1, mode="tn", out_dtype=F32, name="d_sgu_w_in", out_shards=True)
    swap, tok = _swap_start(dw_in_s, name="g_sgu_in_swap")
    dh1 = _matmul_nt_shards(dproj1, g_wi_s, out_dtype=F32, name="d_sgu_h", after=tok)
    fl_wi_s, tok = pair_and_scatter(swap, dh1, "g_sgu_in")
    dx1, dy0, d_pre1, d_post0 = _mid_bwd(dx2, dh1, x1, behind(norm_pre[1:2], tok), y0, norm_post[0:1],
                                         name="pre1_post0_bwd")
    dw_out_g = _matmul(a0, dy0, mode="tn", out_dtype=F32, name="d_gla_w_out")
    swap, tok = _swap_start(dw_out_g.reshape(N_CHIPS, d // N_CHIPS, d), name="g_gla_out_swap")
    da0 = _matmul(dy0, w_out_g, mode="nt", out_dtype=F32, name="d_gla_act", after=tok)
    fl_wo_g, tok = pair_and_scatter(swap, da0, "g_gla_out")
    dproj0, d_og, d_bg, d_w2p = _gla_bwd(da0, o0, proj0, w2p, behind(gla_b_gate, tok), gla_o_gain, s_before, s_final,
                                         lay, name="gla_scan_bwd")
    dwt_in_g = _matmul(dproj0, h0, mode="tn", out_dtype=F32, name="d_gla_w_in", tm=wp)
    swap, tok = _swap_start(dwt_in_g.reshape(N_CHIPS, wp, d), name="g_gla_in_swap")
    r_wo_s = reduced(fl_wo_s, tok, "g_sgu_out")
    r_wi_s = reduced(fl_wi_s, r_wo_s, "g_sgu_in")
    r_wo_g = reduced(fl_wo_g, r_wi_s, "g_gla_out")
    sharing, tok = _share_start([r_wo_s, r_wi_s, r_wo_g], name="grads_share_a")
    fl_wi_g, tok = pair_and_scatter(swap, tok, "g_gla_in")
    dh0 = _matmul(dproj0, wt_g, mode="nn", out_dtype=F32, name="d_gla_h", tk=N_CHIPS * wp, after=tok)
    grad_x, d_pre0 = _first_bwd(dx1, dh0, x0, norm_pre[0:1], name="pre0_bwd")

    small_shapes = [norm_pre.shape, norm_post.shape, gla_b_gate.shape, gla_o_gain.shape, sgu_w_spatial.shape,
                    sgu_b_spatial.shape, (1, GLA_GATE_RANK, dk), (1, d), (1, d), (1, LANES)]
    d_pre = jnp.concatenate([d_pre0, d_pre1], axis=0)
    d_post = jnp.concatenate([d_post0, d_post1], axis=0)
    d_wsp = jnp.where(mask[None], d_ws, 0.0)[None]
    small_part = _pack([d_pre, d_post, d_bg, d_og, d_wsp, d_bs_t.T[None], d_w2p[:GLA_GATE_RANK][None], d_lg, d_lb,
                        loss_part])
    small_flight, tok = _dev_gather_start(small_part, name="small_grads_start")

    def big_update(w, g, m, v, name):
        return [u[None] for u in _adamw(w[0], g, m[0], v[0], name=name)]

    g_wo_sgu, g_wi_sgu, g_wo_gla = _share_wait(sharing, [grad_x, tok], name="grads_share_a_wait")
    u_wo_sgu = big_update(sgu_w_out, g_wo_sgu, m_sgu_w_out, v_sgu_w_out, "adamw_sgu_w_out")
    u_wi_sgu = big_update(sgu_w_in, g_wi_sgu, m_sgu_w_in, v_sgu_w_in, "adamw_sgu_w_in")
    u_wo_gla = big_update(gla_w_out, g_wo_gla, m_gla_w_out, v_gla_w_out, "adamw_gla_w_out")

    small_land = _dev_gather_wait(small_flight, [u_wo_gla[1], u_wi_sgu[1], u_wo_sgu[1]], name="small_grads_wait")
    small_all = lax.dynamic_update_slice(small_land, small_part[None], (2 * chip + core, 0, 0))
    small_sum = _stack_sum(small_all, name="small_sum")
    (g_pre, g_post, g_bg, g_og, g_wsp, g_bsp, g_w2_full, g_lg_full, g_lb_full, loss_vec) = _unpack(small_sum, small_shapes)
    loss = loss_vec[0, 0]
    g_w2 = lax.dynamic_slice_in_dim(g_w2_full, chip * (dk // N_CHIPS), dk // N_CHIPS, axis=2)
    g_lg = lax.dynamic_slice_in_dim(g_lg_full, chip * (d // N_CHIPS), d // N_CHIPS, axis=1)
    g_lb = lax.dynamic_slice_in_dim(g_lb_full, chip * (d // N_CHIPS), d // N_CHIPS, axis=1)

    small_w = [norm_pre, norm_post, gla_b_gate, gla_o_gain, sgu_w_spatial, sgu_b_spatial, gla_w_gate2, sgu_ln_gain,
               sgu_ln_bias]
    small_g = [g_pre, g_post, g_bg, g_og, g_wsp, g_bsp, g_w2, g_lg, g_lb]
    small_m = [m_norm_pre, m_norm_post, m_gla_b_gate, m_gla_o_gain, m_sgu_w_spatial, m_sgu_b_spatial, m_gla_w_gate2,
               m_sgu_ln_gain, m_sgu_ln_bias]
    small_v = [v_norm_pre, v_norm_post, v_gla_b_gate, v_gla_o_gain, v_sgu_w_spatial, v_sgu_b_spatial, v_gla_w_gate2,
               v_sgu_ln_gain, v_sgu_ln_bias]
    own_shapes = [w.shape for w in small_w]
    _, s_dl, s_m, s_v = _adamw(_pack(small_w), _pack(small_g), _pack(small_m), _pack(small_v), name="adamw_small")
    dl_s, m_s, v_s = _unpack(s_dl, own_shapes), _unpack(s_m, own_shapes), _unpack(s_v, own_shapes)

    r_wi_g = reduced(fl_wi_g, s_dl, "g_gla_in")
    gt_wi_gla, = _sibling_share_halves([r_wi_g], name="grads_share_b")
    u_wi_gla = [u.T[None] for u in _adamw(wt_in_g, gt_wi_gla, mt_in_g, vt_in_g, name="adamw_gla_w_in")]

    def ordered(small, kind):
        pre, post, bg, og, wsp, bsp, w2, lg, lb = small
        return [pre, post, u_wi_gla[kind], w2, bg, og, u_wo_gla[kind], u_wi_sgu[kind], lg, lb, wsp, bsp, u_wo_sgu[kind]]

    return (loss, grad_x[None], *ordered(small_g, 0), *ordered(dl_s, 1), *ordered(m_s, 2), *ordered(v_s, 3))
```

```python
import functools
import math

import jax
import jax.numpy as jnp
from jax import lax
from jax.experimental import pallas as pl
from jax.experimental.pallas import tpu as pltpu

F32 = jnp.float32
BF16 = jnp.bfloat16
MESH = pl.DeviceIdType.MESH

EPS = 1e-6
CHUNK = 64
GLA_HEADS = 4
GLA_GATE_RANK = 16
GLA_TAU = 16.0
SGU_BLOCK = 128
SGU_GROUPS = 8
N_CHIPS = 4
N_DEV = 8
LANES = 128

ADAM_LR = 0.001
ADAM_B1 = 0.9
ADAM_B2 = 0.999
ADAM_EPS = 1e-08
ADAM_WD = 0.01
ADAM_STEP = 10

VMEM_LIMIT = 56 * 1024 * 1024


def _cparams(sem=None):
    return pltpu.CompilerParams(dimension_semantics=sem, vmem_limit_bytes=VMEM_LIMIT)


def _pick(n, cap, unit=LANES):
    best = None
    for t in range(unit, min(n, cap) + 1, unit):
        if n % t == 0:
            best = t
    assert best is not None, (n, cap, unit)
    return best


def _dot(a, b, dims):
    return lax.dot_general(a, b, (dims, ((), ())), preferred_element_type=F32)


def _dot_nn(a, b):
    return _dot(a, b, ((1,), (0,)))


def _dot_nt(a, b):
    return _dot(a, b, ((1,), (1,)))


def _dot_tn(a, b):
    return _dot(a, b, ((0,), (0,)))


def _matmul(a, b, *, mode, out_dtype, name, tm=1024, tn=512, tk=2048, b_shards=False, out_shards=False, after=None,
            out_rows=None):
    if mode == "tn":
        K, M = a.shape
    else:
        M, K = a.shape
    if b_shards:
        ns, br, bc = b.shape
        if mode == "nt":
            N, Kb = br, ns * bc
        else:
            Kb, N = br, ns * bc
    else:
        if mode == "nt":
            N, Kb = b.shape
        else:
            Kb, N = b.shape
    assert K == Kb, (a.shape, b.shape, mode)
    tm = _pick(M, tm)
    tk = _pick(K, tk)
    if b_shards and mode != "nt":
        tn = _pick(bc, tn)
    elif out_shards:
        tn = _pick(N // N_CHIPS, tn)
    else:
        tn = _pick(N, tn)
    if b_shards and mode == "nt":
        tk = _pick(bc, tk)
    nk = K // tk
    grid = (M // tm, N // tn, nk)

    if mode == "tn":
        a_spec = pl.BlockSpec((tk, tm), lambda i, j, k: (k, i))
    else:
        a_spec = pl.BlockSpec((tm, tk), lambda i, j, k: (i, k))
    if b_shards:
        if mode == "nt":
            per = bc // tk
            b_spec = pl.BlockSpec((None, tn, tk), lambda i, j, k: (k // per, j, k % per))
        else:
            per = bc // tn
            b_spec = pl.BlockSpec((None, tk, tn), lambda i, j, k: (j // per, k, j % per))
    elif mode == "nt":
        b_spec = pl.BlockSpec((tn, tk), lambda i, j, k: (j, k))
    else:
        b_spec = pl.BlockSpec((tk, tn), lambda i, j, k: (k, j))
    if out_shards:
        per_o = (N // N_CHIPS) // tn
        out_spec = pl.BlockSpec((None, tm, tn), lambda i, j, k: (j // per_o, i, j % per_o))
        out_shape = jax.ShapeDtypeStruct((N_CHIPS, M, N // N_CHIPS), out_dtype)
    else:
        out_spec = pl.BlockSpec((tm, tn), lambda i, j, k: (i, j))
        out_shape = jax.ShapeDtypeStruct((M if out_rows is None else out_rows, N), out_dtype)

    dims = {"nn": ((1,), (0,)), "nt": ((1,), (1,)), "tn": ((0,), (0,))}[mode]

    def body(a_ref, b_ref, *rest):
        o_ref, scratch = (rest[1], rest[2:]) if after is not None else (rest[0], rest[1:])
        part = _dot(a_ref[...].astype(BF16), b_ref[...].astype(BF16), dims)
        if nk == 1:
            o_ref[...] = part.astype(out_dtype)
        else:
            acc_ref, = scratch
            k = pl.program_id(2)

            @pl.when(k == 0)
            def _():
                acc_ref[...] = part

            @pl.when(k > 0)
            def _():
                acc_ref[...] += part

            @pl.when(k == nk - 1)
            def _():
                o_ref[...] = acc_ref[...].astype(out_dtype)

    extra_specs, extra_args = ([], []) if after is None else ([pl.BlockSpec(memory_space=pl.ANY)], [after])
    return pl.pallas_call(
        body, name=name, grid=grid, in_specs=[a_spec, b_spec] + extra_specs, out_specs=out_spec, out_shape=out_shape,
        scratch_shapes=[] if nk == 1 else [pltpu.VMEM((tm, tn), F32)],
        compiler_params=_cparams(("parallel", "parallel", "arbitrary")),
    )(a, b, *extra_args)


def _matmul_into_cols(a, w, which, buf, *, name, tm=1024):
    M, K = a.shape
    _, N, _ = w.shape
    tm = _pick(M, tm)

    def body(which_ref, a_ref, w_ref, buf_ref, o_ref):
        o_ref[...] = _dot_nt(a_ref[...], w_ref[...])

    grid_spec = pltpu.PrefetchScalarGridSpec(
        num_scalar_prefetch=1, grid=(M // tm,),
        in_specs=[pl.BlockSpec((tm, K), lambda i, s: (i, 0)), pl.BlockSpec((None, N, K), lambda i, s: (s[1], 0, 0)),
                  pl.BlockSpec(memory_space=pl.ANY)],
        out_specs=pl.BlockSpec((tm, N), lambda i, s: (i, s[0])))
    return pl.pallas_call(
        body, name=name, grid_spec=grid_spec, out_shape=jax.ShapeDtypeStruct(buf.shape, buf.dtype),
        input_output_aliases={3: 0}, compiler_params=_cparams(("parallel",)),
    )(which, a, w, buf)


def _matmul_nt_shards(a, b, *, out_dtype, name, tm=1024, tn=512, after=None):
    M, K = a.shape
    ns, N, kc = b.shape
    assert K == ns * kc
    tm, tn = _pick(M, tm), _pick(N, tn)

    def body(a_ref, *rest):
        b_refs, o_ref = rest[:ns], rest[ns + (after is not None)]
        acc = _dot_nt(a_ref[:, 0:kc], b_refs[0][...])
        for j in range(1, ns):
            acc += _dot_nt(a_ref[:, j * kc:(j + 1) * kc], b_refs[j][...])
        o_ref[...] = acc.astype(out_dtype)

    def shard(j):
        return pl.BlockSpec((None, tn, kc), lambda i, n: (j, n, 0))

    extra_specs, extra_args = ([], []) if after is None else ([pl.BlockSpec(memory_space=pl.ANY)], [after])
    return pl.pallas_call(
        body, name=name, grid=(M // tm, N // tn),
        in_specs=[pl.BlockSpec((tm, K), lambda i, n: (i, 0))] + [shard(j) for j in range(ns)] + extra_specs,
        out_specs=pl.BlockSpec((tm, tn), lambda i, n: (i, n)), out_shape=jax.ShapeDtypeStruct((M, N), out_dtype),
        compiler_params=_cparams(("parallel", "parallel")),
    )(a, *([b] * ns), *extra_args)


def _rstd(x):
    return lax.rsqrt(jnp.mean(x * x, axis=-1, keepdims=True) + EPS)


def _row_spec(tr, d):
    return pl.BlockSpec((tr, d), lambda i: (i, 0))


def _vec_spec(d):
    return pl.BlockSpec((1, d), lambda i: (0, 0))


def _acc_rows(ref, i, val, cols=slice(None)):
    @pl.when(i == 0)
    def _():
        ref[:, cols] = val

    @pl.when(i > 0)
    def _():
        ref[:, cols] += val


def _norm_pre(x, gain, *, name, tr=256):
    t, d = x.shape
    tr = _pick(t, tr, 8)

    def body(x_ref, g_ref, h_ref):
        xv = x_ref[...]
        h_ref[...] = (xv * _rstd(xv) * g_ref[...]).astype(BF16)

    return pl.pallas_call(
        body, name=name, grid=(t // tr,), in_specs=[_row_spec(tr, d), _vec_spec(d)], out_specs=_row_spec(tr, d),
        out_shape=jax.ShapeDtypeStruct((t, d), BF16), compiler_params=_cparams(("parallel",)),
    )(x, gain)


def _post_then_pre(x, y, post_gain, pre_gain, *, name, tr=256):
    t, d = x.shape
    tr = _pick(t, tr, 8)

    def body(x_ref, y_ref, pg_ref, ng_ref, xn_ref, h_ref):
        yv = y_ref[...]
        xn = x_ref[...] + yv * _rstd(yv) * pg_ref[...]
        xn_ref[...] = xn
        h_ref[...] = (xn * _rstd(xn) * ng_ref[...]).astype(BF16)

    return pl.pallas_call(
        body, name=name, grid=(t // tr,),
        in_specs=[_row_spec(tr, d), _row_spec(tr, d), _vec_spec(d), _vec_spec(d)],
        out_specs=[_row_spec(tr, d), _row_spec(tr, d)],
        out_shape=[jax.ShapeDtypeStruct((t, d), F32), jax.ShapeDtypeStruct((t, d), BF16)],
        compiler_params=_cparams(("parallel",)),
    )(x, y, post_gain, pre_gain)


def _norm_bwd(dy, n, r, gain):
    dn = dy * gain
    return r * (dn - n * jnp.mean(dn * n, axis=-1, keepdims=True))


def _loss_head(x, y, post_gain, target, *, name, tr=256):
    t, d = x.shape
    tr = _pick(t, tr, 8)

    def body(x_ref, y_ref, pg_ref, t_ref, loss_ref, dx_ref, dy_ref, dpg_ref):
        i = pl.program_id(0)
        yv = y_ref[...]
        r = _rstd(yv)
        n = yv * r
        err = x_ref[...] + n * pg_ref[...] - t_ref[...]
        dx = err * (1.0 / d)
        dx_ref[...] = dx
        part = 0.5 * jnp.sum(jnp.mean(err * err, axis=-1, keepdims=True), axis=0, keepdims=True)
        _acc_rows(loss_ref, i, jnp.broadcast_to(part, (1, LANES)))
        _acc_rows(dpg_ref, i, jnp.sum(dx * n, axis=0, keepdims=True))
        dy_ref[...] = _norm_bwd(dx, n, r, pg_ref[...]).astype(BF16)

    return pl.pallas_call(
        body, name=name, grid=(t // tr,),
        in_specs=[_row_spec(tr, d), _row_spec(tr, d), _vec_spec(d), _row_spec(tr, d)],
        out_specs=[_vec_spec(LANES), _row_spec(tr, d), _row_spec(tr, d), _vec_spec(d)],
        out_shape=[jax.ShapeDtypeStruct((1, LANES), F32), jax.ShapeDtypeStruct((t, d), F32),
                   jax.ShapeDtypeStruct((t, d), BF16), jax.ShapeDtypeStruct((1, d), F32)],
        compiler_params=_cparams(("arbitrary",)),
    )(x, y, post_gain, target)


def _mid_bwd(dx_out, dh, x, pre_gain, y_prev, post_gain_prev, *, name, tr=256):
    t, d = x.shape
    tr = _pick(t, tr, 8)

    def body(dxo_ref, dh_ref, x_ref, ng_ref, y_ref, pg_ref, dx_ref, dy_ref, dng_ref, dpg_ref):
        i = pl.program_id(0)
        xv = x_ref[...]
        r = _rstd(xv)
        xh = xv * r
        dhv = dh_ref[...]
        _acc_rows(dng_ref, i, jnp.sum(dhv * xh, axis=0, keepdims=True))
        dx = dxo_ref[...] + _norm_bwd(dhv, xh, r, ng_ref[...])
        dx_ref[...] = dx
        yv = y_ref[...]
        ry = _rstd(yv)
        n = yv * ry
        _acc_rows(dpg_ref, i, jnp.sum(dx * n, axis=0, keepdims=True))
        dy_ref[...] = _norm_bwd(dx, n, ry, pg_ref[...]).astype(BF16)

    return pl.pallas_call(
        body, name=name, grid=(t // tr,),
        in_specs=[_row_spec(tr, d), _row_spec(tr, d), _row_spec(tr, d), _vec_spec(d), _row_spec(tr, d), _vec_spec(d)],
        out_specs=[_row_spec(tr, d), _row_spec(tr, d), _vec_spec(d), _vec_spec(d)],
        out_shape=[jax.ShapeDtypeStruct((t, d), F32), jax.ShapeDtypeStruct((t, d), BF16),
                   jax.ShapeDtypeStruct((1, d), F32), jax.ShapeDtypeStruct((1, d), F32)],
        compiler_params=_cparams(("arbitrary",)),
    )(dx_out, dh, x, pre_gain, y_prev, post_gain_prev)


def _first_bwd(dx_out, dh, x, pre_gain, *, name, tr=256):
    t, d = x.shape
    tr = _pick(t, tr, 8)

    def body(dxo_ref, dh_ref, x_ref, ng_ref, dx_ref, dng_ref):
        i = pl.program_id(0)
        xv = x_ref[...]
        r = _rstd(xv)
        xh = xv * r
        dhv = dh_ref[...]
        _acc_rows(dng_ref, i, jnp.sum(dhv * xh, axis=0, keepdims=True))
        dx_ref[...] = dxo_ref[...] + _norm_bwd(dhv, xh, r, ng_ref[...])

    return pl.pallas_call(
        body, name=name, grid=(t // tr,),
        in_specs=[_row_spec(tr, d), _row_spec(tr, d), _row_spec(tr, d), _vec_spec(d)],
        out_specs=[_row_spec(tr, d), _vec_spec(d)],
        out_shape=[jax.ShapeDtypeStruct((t, d), F32), jax.ShapeDtypeStruct((1, d), F32)],
        compiler_params=_cparams(("arbitrary",)),
    )(dx_out, dh, x, pre_gain)


def _sigmoid(x):
    return 1.0 / (1.0 + jnp.exp(-x))


def _log_sigmoid(x):
    return jnp.minimum(x, 0.0) - jnp.log(1.0 + jnp.exp(-jnp.abs(x)))


_GELU_C = math.sqrt(2.0 / math.pi)


_GELU_A = 0.044715


def _gelu_parts(x, with_grad=True):
    x2 = x * x
    h = 0.5 * jnp.tanh(x * (_GELU_C + (_GELU_C * _GELU_A) * x2)) + 0.5
    val = x * h
    if not with_grad:
        return val, None
    return val, h * (1.0 + (1.0 - h) * (x * (2.0 * _GELU_C + (6.0 * _GELU_C * _GELU_A) * x2)))


def _split3(x):
    hi = x.astype(BF16)
    r1 = x - hi.astype(F32)
    mid = r1.astype(BF16)
    lo = (r1 - mid.astype(F32)).astype(BF16)
    return hi, mid, lo


def _tri_matmul(tri_bf16, x):
    hi, mid, lo = _split3(x)
    return _dot_nn(tri_bf16, hi) + _dot_nn(tri_bf16, mid) + _dot_nn(tri_bf16, lo)


def _gla_dims(d):
    dk, dv = d // 2, d
    return dk, dv, dk // GLA_HEADS, dv // GLA_HEADS


def _col_pieces(a, b, lay):
    ws, wp = lay
    out = []
    while a < b:
        j = a // ws
        end = min(b, (j + 1) * ws)
        out.append((j * wp + a - j * ws, end - a))
        a = end
    return out


def _load_cols(ref, a, b, lay):
    parts = [ref[:, s:s + n] for s, n in _col_pieces(a, b, lay)]
    return parts[0] if len(parts) == 1 else jnp.concatenate(parts, axis=1)


def _store_cols(ref, a, val, lay):
    off = 0
    for s, n in _col_pieces(a, a + val.shape[1], lay):
        ref[:, s:s + n] = val[:, off:off + n]
        off += n


def _gate_window(c_r, lay):
    (start, _), = _col_pieces(c_r, c_r + GLA_GATE_RANK, lay)
    assert (start % lay[1]) + LANES <= lay[1]
    return slice(start, start + LANES)


def _gla_gates(glr, k, w2_ref, b_ref):
    z = _dot_nn(glr.astype(BF16), w2_ref[...].astype(BF16)) + b_ref[...]
    la = _log_sigmoid(z) * (1.0 / GLA_TAU)
    row = lax.broadcasted_iota(jnp.int32, (CHUNK, CHUNK), 0)
    col = lax.broadcasted_iota(jnp.int32, (CHUNK, CHUNK), 1)
    incl = (row >= col).astype(BF16)
    bcum = _tri_matmul(incl, la)
    b_end = bcum[CHUNK - 1:CHUNK, :]
    e_rest = jnp.exp(b_end - bcum)
    return z, e_rest, k * e_rest, jnp.exp(b_end)


def _gla_fwd(proj, w2p, b_gate, o_gain, lay, *, name):
    t, wcols = proj.shape
    d = o_gain.shape[1]
    dk, dv, dkh, dvh = _gla_dims(d)
    nc = t // CHUNK
    c_k, c_v, c_g, c_r = dk, 2 * dk, 2 * dk + dv, 2 * dk + 2 * dv
    scale = dkh ** -0.5

    def body(p_ref, w2_ref, b_ref, og_ref, o_ref, a_ref, sb_ref, sfin_ref, s_ref):
        i = pl.program_id(0)

        @pl.when(i == 0)
        def _():
            s_ref[...] = jnp.zeros_like(s_ref)

        q = _load_cols(p_ref, 0, dk, lay) * scale
        k = _load_cols(p_ref, c_k, c_k + dk, lay)
        glr = p_ref[:, _gate_window(c_r, lay)]
        _, _, kdec, decay = _gla_gates(glr, k, w2_ref, b_ref)
        for h in range(GLA_HEADS):
            ks = slice(h * dkh, (h + 1) * dkh)
            vs = slice(h * dvh, (h + 1) * dvh)
            v_h = _load_cols(p_ref, c_v + h * dvh, c_v + (h + 1) * dvh, lay)
            g_h = _load_cols(p_ref, c_g + h * dvh, c_g + (h + 1) * dvh, lay)
            s_old = s_ref[h]
            sb_ref[0, h] = s_old
            s_new = s_old * decay[:, ks] + _dot_tn(v_h.astype(BF16), kdec[:, ks].astype(BF16))
            s_ref[h] = s_new
            o_h = _dot_nt(q[:, ks].astype(BF16), s_new.astype(BF16))
            o_ref[:, vs] = o_h
            on = o_h * _rstd(o_h)
            a_ref[:, vs] = (on * og_ref[:, vs] * (g_h * _sigmoid(g_h))).astype(BF16)

        @pl.when(i == nc - 1)
        def _():
            sfin_ref[...] = s_ref[...]

    full = lambda *shape: pl.BlockSpec(shape, lambda i: (0,) * len(shape))
    return pl.pallas_call(
        body, name=name, grid=(nc,),
        in_specs=[pl.BlockSpec((CHUNK, wcols), lambda i: (i, 0)), full(LANES, dk), full(1, dk), full(1, dv)],
        out_specs=[pl.BlockSpec((CHUNK, dv), lambda i: (i, 0)), pl.BlockSpec((CHUNK, dv), lambda i: (i, 0)),
                   pl.BlockSpec((1, GLA_HEADS, dvh, dkh), lambda i: (i, 0, 0, 0)), full(GLA_HEADS, dvh, dkh)],
        out_shape=[jax.ShapeDtypeStruct((t, dv), F32), jax.ShapeDtypeStruct((t, dv), BF16),
                   jax.ShapeDtypeStruct((nc, GLA_HEADS, dvh, dkh), F32),
                   jax.ShapeDtypeStruct((GLA_HEADS, dvh, dkh), F32)],
        scratch_shapes=[pltpu.VMEM((GLA_HEADS, dvh, dkh), F32)],
        compiler_params=_cparams(("arbitrary",)),
    )(proj, w2p, b_gate, o_gain)


def _gla_bwd(da, o, proj, w2p, b_gate, o_gain, s_before, s_final, lay, *, name):
    t, wcols = proj.shape
    d = o_gain.shape[1]
    dk, dv, dkh, dvh = _gla_dims(d)
    nc = t // CHUNK
    c_k, c_v, c_g, c_r = dk, 2 * dk, 2 * dk + dv, 2 * dk + 2 * dv
    scale = dkh ** -0.5

    def body(da_ref, o_ref, p_ref, w2_ref, b_ref, og_ref, sb_ref, sfin_ref,
             dp_ref, dog_ref, db_ref, dw2_ref, s_ref, gc_ref, dkd_ref):
        i = pl.program_id(0)

        @pl.when(i == 0)
        def _():
            s_ref[...] = sfin_ref[...]
            gc_ref[...] = jnp.zeros_like(gc_ref)

        ws, wp = lay
        for j in range(N_CHIPS):
            dp_ref[:, j * wp + ws:(j + 1) * wp] = jnp.zeros((CHUNK, wp - ws), BF16)
        q = _load_cols(p_ref, 0, dk, lay) * scale
        k = _load_cols(p_ref, c_k, c_k + dk, lay)
        glr = p_ref[:, _gate_window(c_r, lay)]
        z, e_rest, kdec, decay = _gla_gates(glr, k, w2_ref, b_ref)
        ddecay = []
        for h in range(GLA_HEADS):
            ks = slice(h * dkh, (h + 1) * dkh)
            vs = slice(h * dvh, (h + 1) * dvh)
            v_h = _load_cols(p_ref, c_v + h * dvh, c_v + (h + 1) * dvh, lay)
            g_h = _load_cols(p_ref, c_g + h * dvh, c_g + (h + 1) * dvh, lay)
            da_h = da_ref[:, vs]
            o_h = o_ref[:, vs]
            og_h = og_ref[:, vs]
            r = _rstd(o_h)
            on = o_h * r
            sg = _sigmoid(g_h)
            silu = g_h * sg
            _acc_rows(dog_ref, i, jnp.sum(da_h * silu * on, axis=0, keepdims=True), vs)
            _store_cols(dp_ref, c_g + h * dvh, (da_h * (on * og_h) * (sg * (1.0 + g_h * (1.0 - sg)))).astype(BF16),
                        lay)
            don = da_h * silu * og_h
            do_h = (r * (don - on * jnp.mean(don * on, axis=-1, keepdims=True))).astype(BF16)
            s_cur = s_ref[h]
            _store_cols(dp_ref, h * dkh, (_dot_nn(do_h, s_cur.astype(BF16)) * scale).astype(BF16), lay)
            g_tot = gc_ref[h] + _dot_tn(do_h, q[:, ks].astype(BF16))
            g_bf = g_tot.astype(BF16)
            dkd_ref[:, ks] = _dot_nn(v_h.astype(BF16), g_bf)
            _store_cols(dp_ref, c_v + h * dvh, _dot_nt(kdec[:, ks].astype(BF16), g_bf).astype(BF16), lay)
            s_prev = sb_ref[0, h]
            ddecay.append(jnp.sum(g_tot * s_prev, axis=0, keepdims=True))
            gc_ref[h] = g_tot * decay[:, ks]
            s_ref[h] = s_prev
        dkdec = dkd_ref[...]
        _store_cols(dp_ref, c_k, (dkdec * e_rest).astype(BF16), lay)
        d_e = dkdec * kdec
        row = lax.broadcasted_iota(jnp.int32, (CHUNK, CHUNK), 0)
        col = lax.broadcasted_iota(jnp.int32, (CHUNK, CHUNK), 1)
        excl = (row > col).astype(BF16)
        dla = jnp.concatenate(ddecay, axis=1) * decay + _tri_matmul(excl, d_e)
        dz = dla * (1.0 / GLA_TAU) * (1.0 - _sigmoid(z))
        _acc_rows(db_ref, i, jnp.sum(dz, axis=0, keepdims=True))
        dz_bf = dz.astype(BF16)
        dw2 = _dot_tn(glr.astype(BF16), dz_bf)

        @pl.when(i == 0)
        def _():
            dw2_ref[...] = dw2

        @pl.when(i > 0)
        def _():
            dw2_ref[...] += dw2

        dp_ref[:, _gate_window(c_r, lay)] = _dot_nt(dz_bf, w2_ref[...].astype(BF16)).astype(BF16)

    rev = lambda i: (nc - 1 - i, 0)
    full = lambda *shape: pl.BlockSpec(shape, lambda i: (0,) * len(shape))
    return pl.pallas_call(
        body, name=name, grid=(nc,),
        in_specs=[pl.BlockSpec((CHUNK, dv), rev), pl.BlockSpec((CHUNK, dv), rev), pl.BlockSpec((CHUNK, wcols), rev),
                  full(LANES, dk), full(1, dk), full(1, dv),
                  pl.BlockSpec((1, GLA_HEADS, dvh, dkh), lambda i: (nc - 1 - i, 0, 0, 0)), full(GLA_HEADS, dvh, dkh)],
        out_specs=[pl.BlockSpec((CHUNK, wcols), rev), full(1, dv), full(1, dk), full(LANES, dk)],
        out_shape=[jax.ShapeDtypeStruct((t, wcols), BF16), jax.ShapeDtypeStruct((1, dv), F32),
                   jax.ShapeDtypeStruct((1, dk), F32), jax.ShapeDtypeStruct((LANES, dk), F32)],
        scratch_shapes=[pltpu.VMEM((GLA_HEADS, dvh, dkh), F32), pltpu.VMEM((GLA_HEADS, dvh, dkh), F32),
                        pltpu.VMEM((CHUNK, dk), F32)],
        compiler_params=_cparams(("arbitrary",)),
    )(da, o, proj, w2p, b_gate, o_gain, s_before, s_final)


def _sgu_mid(p_ref, lg_ref, lb_ref, ws_ref, bst_ref, w, with_grad=True):
    gd = w // SGU_GROUPS
    u_act, du_fac = _gelu_parts(p_ref[:, 0:w], with_grad)
    vf, dv_fac = _gelu_parts(p_ref[:, w:2 * w], with_grad)
    mu = jnp.mean(vf, axis=-1, keepdims=True)
    cen = vf - mu
    rstd = lax.rsqrt(jnp.mean(cen * cen, axis=-1, keepdims=True) + EPS)
    xh = cen * rstd
    vn = (xh * lg_ref[...] + lb_ref[...]).astype(BF16)
    vs = [_dot_nn(ws_ref[g].astype(BF16), vn[:, g * gd:(g + 1) * gd]) + bst_ref[:, g:g + 1]
          for g in range(SGU_GROUPS)]
    return u_act, du_fac, dv_fac, rstd, xh, vn, vs


def _sgu_fwd(proj, ln_gain, ln_bias, ws_masked, bs_t, *, name):
    t, w3 = proj.shape
    w = w3 // 3
    gd = w // SGU_GROUPS
    nb = t // SGU_BLOCK

    def body(p_ref, lg_ref, lb_ref, ws_ref, bst_ref, a_ref):
        u_act, _, _, _, _, _, vs = _sgu_mid(p_ref, lg_ref, lb_ref, ws_ref, bst_ref, w, with_grad=False)
        for g in range(SGU_GROUPS):
            cs = slice(g * gd, (g + 1) * gd)
            gate = p_ref[:, 2 * w + g * gd:2 * w + (g + 1) * gd]
            a_ref[:, cs] = (u_act[:, cs] * vs[g] * (gate * _sigmoid(gate))).astype(BF16)

    full = lambda *shape: pl.BlockSpec(shape, lambda i: (0,) * len(shape))
    return pl.pallas_call(
        body, name=name, grid=(nb,),
        in_specs=[pl.BlockSpec((SGU_BLOCK, w3), lambda i: (i, 0)), full(1, w), full(1, w),
                  full(SGU_GROUPS, SGU_BLOCK, SGU_BLOCK), full(SGU_BLOCK, SGU_GROUPS)],
        out_specs=pl.BlockSpec((SGU_BLOCK, w), lambda i: (i, 0)),
        out_shape=jax.ShapeDtypeStruct((t, w), BF16),
        compiler_params=_cparams(("parallel",)),
    )(proj, ln_gain, ln_bias, ws_masked, bs_t)


def _sgu_bwd(da, proj, ln_gain, ln_bias, ws_masked, ws_masked_t, bs_t, *, name):
    t, w3 = proj.shape
    w = w3 // 3
    gd = w // SGU_GROUPS
    nb = t // SGU_BLOCK

    def body(da_ref, p_ref, lg_ref, lb_ref, ws_ref, wst_ref, bst_ref, dp_ref, dws_ref, dbst_ref, dlg_ref, dlb_ref,
             dvn_ref):
        i = pl.program_id(0)
        u_act, du_fac, dv_fac, rstd, xh, vn, vs = _sgu_mid(p_ref, lg_ref, lb_ref, ws_ref, bst_ref, w)
        for g in range(SGU_GROUPS):
            cs = slice(g * gd, (g + 1) * gd)
            gate = p_ref[:, 2 * w + g * gd:2 * w + (g + 1) * gd]
            sg = _sigmoid(gate)
            silu = gate * sg
            da_g = da_ref[:, cs]
            ua_g = u_act[:, cs]
            dp_ref[:, cs] = (da_g * vs[g] * silu * du_fac[:, cs]).astype(BF16)
            dp_ref[:, 2 * w + g * gd:2 * w + (g + 1) * gd] = (
                da_g * ua_g * vs[g] * (sg * (1.0 + gate * (1.0 - sg)))).astype(BF16)
            dvs = da_g * ua_g * silu
            dvs_bf = dvs.astype(BF16)
            dvn_ref[:, cs] = _dot_nn(wst_ref[g].astype(BF16), dvs_bf)
            dws = _dot_nt(dvs_bf, vn[:, cs])
            dbs = jnp.sum(dvs, axis=1, keepdims=True)

            @pl.when(i == 0)
            def _():
                dws_ref[g] = dws
                dbst_ref[:, g:g + 1] = dbs

            @pl.when(i > 0)
            def _():
                dws_ref[g] += dws
                dbst_ref[:, g:g + 1] += dbs

        dvn = dvn_ref[...]
        _acc_rows(dlg_ref, i, jnp.sum(dvn * xh, axis=0, keepdims=True))
        _acc_rows(dlb_ref, i, jnp.sum(dvn, axis=0, keepdims=True))
        dxh = dvn * lg_ref[...]
        dvf = rstd * (dxh - jnp.mean(dxh, axis=-1, keepdims=True)
                      - xh * jnp.mean(dxh * xh, axis=-1, keepdims=True))
        dp_ref[:, w:2 * w] = (dvf * dv_fac).astype(BF16)

    full = lambda *shape: pl.BlockSpec(shape, lambda i: (0,) * len(shape))
    return pl.pallas_call(
        body, name=name, grid=(nb,),
        in_specs=[pl.BlockSpec((SGU_BLOCK, w), lambda i: (i, 0)), pl.BlockSpec((SGU_BLOCK, w3), lambda i: (i, 0)),
                  full(1, w), full(1, w), full(SGU_GROUPS, SGU_BLOCK, SGU_BLOCK),
                  full(SGU_GROUPS, SGU_BLOCK, SGU_BLOCK), full(SGU_BLOCK, SGU_GROUPS)],
        out_specs=[pl.BlockSpec((SGU_BLOCK, w3), lambda i: (i, 0)), full(SGU_GROUPS, SGU_BLOCK, SGU_BLOCK),
                   full(SGU_BLOCK, SGU_GROUPS), full(1, w), full(1, w)],
        out_shape=[jax.ShapeDtypeStruct((t, w3), BF16), jax.ShapeDtypeStruct((SGU_GROUPS, SGU_BLOCK, SGU_BLOCK), F32),
                   jax.ShapeDtypeStruct((SGU_BLOCK, SGU_GROUPS), F32), jax.ShapeDtypeStruct((1, w), F32),
                   jax.ShapeDtypeStruct((1, w), F32)],
        scratch_shapes=[pltpu.VMEM((SGU_BLOCK, w), F32)],
        compiler_params=_cparams(("arbitrary",)),
    )(da, proj, ln_gain, ln_bias, ws_masked, ws_masked_t, bs_t)


def _tile2d(rows, cols, block_bytes, row_unit):
    if rows % row_unit == 0:
        return _pick(rows, max(row_unit, block_bytes // (4 * cols)), row_unit), cols
    return rows, _pick(cols, max(LANES, block_bytes // (4 * rows)))


def _adamw(w, g, m, v, *, name, block_bytes=1 << 20, after=None):
    rows, cols = w.shape
    tr, tc = _tile2d(rows, cols, block_bytes, 8)
    g_rows = g.shape[0]
    assert g_rows == rows or tr == rows
    extra_specs, extra_args = ([], []) if after is None else ([pl.BlockSpec(memory_space=pl.ANY)], [after])

    def body(w_ref, g_ref, m_ref, v_ref, *rest):
        go_ref, d_ref, mo_ref, vo_ref = rest[len(extra_args):]
        gv = g_ref[0:tr, :]
        go_ref[...] = gv
        mn = ADAM_B1 * m_ref[...] + (1.0 - ADAM_B1) * gv
        vn = ADAM_B2 * v_ref[...] + (1.0 - ADAM_B2) * (gv * gv)
        m_hat = mn / (1.0 - ADAM_B1 ** ADAM_STEP)
        v_hat = vn / (1.0 - ADAM_B2 ** ADAM_STEP)
        d_ref[...] = -ADAM_LR * (m_hat / (jnp.sqrt(v_hat) + ADAM_EPS) + ADAM_WD * w_ref[...])
        mo_ref[...] = mn
        vo_ref[...] = vn

    spec = pl.BlockSpec((tr, tc), lambda i, j: (i, j))
    g_spec = spec if g_rows == rows else pl.BlockSpec((g_rows, tc), lambda i, j: (0, j))
    return pl.pallas_call(
        body, name=name, grid=(rows // tr, cols // tc), in_specs=[spec, g_spec, spec, spec] + extra_specs,
        out_specs=[spec] * 4, out_shape=[jax.ShapeDtypeStruct((rows, cols), F32)] * 4,
        compiler_params=_cparams(("parallel", "parallel")),
    )(w, g, m, v, *extra_args)


def _pair_sum_bf16(own, core_idx, peer, *, name, block_bytes=1 << 20):
    s, r, c = own.shape
    hc = c // 2
    tr, tc = _tile2d(r, hc, block_bytes, 16)
    ncb = hc // tc

    def body(h_ref, a_ref, b_ref, o_ref):
        o_ref[...] = (a_ref[...] + b_ref[...]).astype(BF16)

    grid_spec = pltpu.PrefetchScalarGridSpec(
        num_scalar_prefetch=1, grid=(s, r // tr, ncb),
        in_specs=[pl.BlockSpec((None, tr, tc), lambda j, i, k, h: (j, i, h[0] * ncb + k)),
                  pl.BlockSpec((None, tr, tc), lambda j, i, k, h: (j, i, k))],
        out_specs=pl.BlockSpec((None, tr, tc), lambda j, i, k, h: (j, i, k)))
    return pl.pallas_call(
        body, name=name, grid_spec=grid_spec, out_shape=jax.ShapeDtypeStruct((s, r, hc), BF16),
        compiler_params=_cparams(("parallel", "parallel", "parallel")),
    )(core_idx, own, peer)


def _chip_sum(pair, landed, slots, *, name, block_bytes=1 << 20):
    _, r, hc = pair.shape
    tr, tc = _tile2d(r, hc, block_bytes, 16)
    ncb = hc // tc

    def body(s_ref, own_ref, l0_ref, l1_ref, l2_ref, o_ref):
        o_ref[...] = ((own_ref[...].astype(F32) + l0_ref[...].astype(F32)) + l1_ref[...].astype(F32)
                      ) + l2_ref[...].astype(F32)

    def slab(which):
        return pl.BlockSpec((None, tr, tc), lambda i, k, s: (s[which], i, k))

    grid_spec = pltpu.PrefetchScalarGridSpec(
        num_scalar_prefetch=1, grid=(r // tr, ncb),
        in_specs=[slab(0), slab(1), slab(2), slab(3)],
        out_specs=pl.BlockSpec((tr, tc), lambda i, k, s: (i, s[4] * ncb + k)))
    return pl.pallas_call(
        body, name=name, grid_spec=grid_spec, out_shape=jax.ShapeDtypeStruct((r, 2 * hc), F32),
        compiler_params=_cparams(("parallel", "parallel")),
    )(slots, pair, landed, landed, landed)


def _stack_sum(x, *, name, out_dtype=F32, block_bytes=1 << 20):
    s, r, c = x.shape
    tr = _pick(r, max(8, block_bytes // (4 * c)), 16) if r % 16 == 0 else r

    def body(x_ref, o_ref):
        acc = x_ref[0].astype(F32)
        for j in range(1, s):
            acc = acc + x_ref[j].astype(F32)
        o_ref[...] = acc.astype(out_dtype)

    return pl.pallas_call(
        body, name=name, grid=(r // tr,),
        in_specs=[pl.BlockSpec((s, tr, c), lambda i: (0, i, 0))], out_specs=pl.BlockSpec((tr, c), lambda i: (i, 0)),
        out_shape=jax.ShapeDtypeStruct((r, c), out_dtype), compiler_params=_cparams(("parallel",)),
    )(x)


HBM = pl.BlockSpec(memory_space=pltpu.HBM)


def _place():
    x, y, c = lax.axis_index("x"), lax.axis_index("y"), lax.axis_index("c")
    other_chips = [(1 - x, y), (x, 1 - y), (1 - x, 1 - y)]
    return x, y, c, other_chips


def _half_cols(cols, which):
    hc = cols // 2
    return pl.ds(pl.multiple_of(which * hc, LANES), hc)


SEM = pl.BlockSpec(memory_space=pltpu.SEMAPHORE)
ANY = pl.BlockSpec(memory_space=pl.ANY)
SIDE_EFFECT = pltpu.SideEffectType.DATAFLOW_SIDE_EFFECTING
TOKEN_SHAPE = (8, LANES)


def _hbm(shape, dtype):
    return pltpu.HBM(shape, dtype)


def _in_hbm(a):
    return pltpu.with_memory_space_constraint(a, pltpu.HBM)


def _gather_copy(src_ref, land_ref, ssem, rsem, k, chip_of_block, to, c):
    cols = src_ref.shape[1]
    return pltpu.make_async_remote_copy(
        src_ref=src_ref.at[:, _half_cols(cols, c)], dst_ref=land_ref.at[chip_of_block, :, _half_cols(cols, c)],
        send_sem=ssem.at[k], recv_sem=rsem.at[k], device_id=to, device_id_type=MESH)


def _gather_start(shards, *, name, after=()):
    n = len(shards)
    after = list(after)

    def body(*refs):
        srcs, lands = refs[:n], refs[n:2 * n]
        outs = refs[2 * n + len(after):]
        token = outs[-1]
        x, y, c, chips = _place()
        me = 2 * x + y
        for a in range(n):
            ssem, rsem = outs[4 * a], outs[4 * a + 1]
            for k, (cx, cy) in enumerate(chips):
                _gather_copy(srcs[a], lands[a], ssem, rsem, k, me, (cx, cy, c), c).start()
        token[...] = jnp.zeros_like(token)

    out_shape, out_specs, aliases = [], [], {}
    for a, s in enumerate(shards):
        out_shape += [pltpu.SemaphoreType.DMA((3,)), pltpu.SemaphoreType.DMA((3,)), _hbm(s.shape, s.dtype),
                      _hbm((N_CHIPS,) + s.shape, s.dtype)]
        out_specs += [SEM, SEM, HBM, HBM]
        aliases[a] = 4 * a + 2
        aliases[n + a] = 4 * a + 3
    out_shape.append(jax.ShapeDtypeStruct(TOKEN_SHAPE, F32))
    out_specs.append(pl.BlockSpec(memory_space=pltpu.VMEM))
    lands = [_in_hbm(lax.empty((N_CHIPS,) + s.shape, s.dtype)) for s in shards]
    res = pl.pallas_call(
        body, name=name, in_specs=[HBM] * (2 * n) + [ANY] * len(after), out_specs=out_specs, out_shape=out_shape,
        input_output_aliases=aliases, compiler_params=pltpu.CompilerParams(has_side_effects=SIDE_EFFECT),
    )(*[_in_hbm(s) for s in shards], *lands, *after)
    return [tuple(res[4 * a:4 * a + 4]) for a in range(n)], res[-1]


def _wait_call(wait_fn, parts, after, *, name):
    ssem, rsem, src, land = parts
    after = list(after) if isinstance(after, (list, tuple)) else [after]

    def body(src_ref, land_ref, ssem_ref, rsem_ref, *rest):
        wait_fn(src_ref, land_ref, ssem_ref, rsem_ref)

    return pl.pallas_call(
        body, name=name, in_specs=[HBM, HBM, SEM, SEM] + [ANY] * len(after), out_specs=[HBM, HBM],
        out_shape=[_hbm(src.shape, src.dtype), _hbm(land.shape, land.dtype)], input_output_aliases={0: 0, 1: 1},
        compiler_params=pltpu.CompilerParams(has_side_effects=SIDE_EFFECT),
    )(src, land, ssem, rsem, *after)


ALL_CHIPS = (0, 1, 2)


def _gather_wait(parts, after, *, name, ks=ALL_CHIPS):
    def wait(src_ref, land_ref, ssem_ref, rsem_ref):
        x, y, c, chips = _place()
        for k in ks:
            cx, cy = chips[k]
            cp = _gather_copy(src_ref, land_ref, ssem_ref, rsem_ref, k, 2 * cx + cy, (x, y, c), c)
            cp.wait_send()
            cp.wait_recv()

    src, land = _wait_call(wait, parts, after, name=name)
    return (parts[0], parts[1], src, land)


def _forward_copy(buf_ref, ssem, rsem, k, slab, which, to):
    part = buf_ref.at[slab, :, _half_cols(buf_ref.shape[2], which)]
    return pltpu.make_async_remote_copy(
        src_ref=part, dst_ref=part, send_sem=ssem.at[k], recv_sem=rsem.at[k], device_id=to, device_id_type=MESH)


def _sibling_forward(land, *, name, ks=ALL_CHIPS):
    def body(_, buf, send_sems, recv_sems):
        x, y, c, chips = _place()
        copies = []
        for k in ks:
            cx, cy = chips[k]
            cp = _forward_copy(buf, send_sems, recv_sems, k, 2 * cx + cy, c, (x, y, 1 - c))
            cp.start()
            copies.append(cp)
        for k in ks:
            cx, cy = chips[k]
            _forward_copy(buf, send_sems, recv_sems, k, 2 * cx + cy, 1 - c, (x, y, c)).wait_recv()
        for cp in copies:
            cp.wait_send()

    return pl.pallas_call(
        body, name=name, in_specs=[HBM], out_specs=HBM, out_shape=jax.ShapeDtypeStruct(land.shape, land.dtype),
        input_output_aliases={0: 0},
        scratch_shapes=[pltpu.SemaphoreType.DMA((3,)), pltpu.SemaphoreType.DMA((3,))],
    )(land)


def _share_copy(buf_ref, ssem, rsem, a, which, to):
    part = buf_ref.at[:, _half_cols(buf_ref.shape[1], which)]
    return pltpu.make_async_remote_copy(
        src_ref=part, dst_ref=part, send_sem=ssem.at[a], recv_sem=rsem.at[a], device_id=to, device_id_type=MESH)


def _share_start(arrays, *, name):
    n = len(arrays)

    def body(*refs):
        bufs, ssem, rsem, token = refs[:n], refs[n], refs[n + 1], refs[-1]
        x, y, c, _ = _place()
        for a in range(n):
            _share_copy(bufs[a], ssem, rsem, a, c, (x, y, 1 - c)).start()
        token[...] = jnp.zeros_like(token)

    res = pl.pallas_call(
        body, name=name, in_specs=[HBM] * n,
        out_specs=[SEM, SEM] + [HBM] * n + [pl.BlockSpec(memory_space=pltpu.VMEM)],
        out_shape=[pltpu.SemaphoreType.DMA((n,)), pltpu.SemaphoreType.DMA((n,))]
        + [_hbm(b.shape, b.dtype) for b in arrays] + [jax.ShapeDtypeStruct(TOKEN_SHAPE, F32)],
        input_output_aliases={a: 2 + a for a in range(n)},
        compiler_params=pltpu.CompilerParams(has_side_effects=SIDE_EFFECT),
    )(*[_in_hbm(b) for b in arrays])
    return (res[0], res[1], list(res[2:2 + n])), res[-1]


def _share_wait(parts, after, *, name):
    ssem, rsem, bufs = parts
    n = len(bufs)
    after = list(after) if isinstance(after, (list, tuple)) else [after]

    def body(*refs):
        buf_refs, ssem_ref, rsem_ref = refs[:n], refs[n], refs[n + 1]
        x, y, c, _ = _place()
        for a in range(n):
            _share_copy(buf_refs[a], ssem_ref, rsem_ref, a, c, (x, y, c)).wait_send()
            _share_copy(buf_refs[a], ssem_ref, rsem_ref, a, 1 - c, (x, y, c)).wait_recv()

    return pl.pallas_call(
        body, name=name, in_specs=[HBM] * n + [SEM, SEM] + [ANY] * len(after), out_specs=[HBM] * n,
        out_shape=[_hbm(b.shape, b.dtype) for b in bufs], input_output_aliases={a: a for a in range(n)},
        compiler_params=pltpu.CompilerParams(has_side_effects=SIDE_EFFECT),
    )(*bufs, ssem, rsem, *after)


def _scatter_copy(src_ref, land_ref, ssem, rsem, k, src_slab, dst_slab, to):
    return pltpu.make_async_remote_copy(
        src_ref=src_ref.at[src_slab], dst_ref=land_ref.at[dst_slab], send_sem=ssem.at[k], recv_sem=rsem.at[k],
        device_id=to, device_id_type=MESH)


def _scatter_start(part, *, name):
    def start(src_ref, land_ref, ssem, rsem):
        x, y, c, chips = _place()
        me = 2 * x + y
        for k, (cx, cy) in enumerate(chips):
            _scatter_copy(src_ref, land_ref, ssem, rsem, k, 2 * cx + cy, me, (cx, cy, c)).start()

    return _split_start(start, part, part.shape, N_CHIPS - 1, name=name)


def _scatter_wait(parts, after, *, name):
    def wait(src_ref, land_ref, ssem_ref, rsem_ref):
        x, y, c, chips = _place()
        for k, (cx, cy) in enumerate(chips):
            idx = 2 * cx + cy
            cp = _scatter_copy(src_ref, land_ref, ssem_ref, rsem_ref, k, idx, idx, (x, y, c))
            cp.wait_send()
            cp.wait_recv()

    return _wait_call(wait, parts, after, name=name)


def _split_start(start_fn, src, land_shape, n_sems, *, name):
    def body(src_ref, land_ref, ssem, rsem, src_out, land_out, token):
        start_fn(src_ref, land_ref, ssem, rsem)
        token[...] = jnp.zeros_like(token)

    res = pl.pallas_call(
        body, name=name, in_specs=[HBM, HBM], out_specs=[SEM, SEM, HBM, HBM, pl.BlockSpec(memory_space=pltpu.VMEM)],
        out_shape=[pltpu.SemaphoreType.DMA((n_sems,)), pltpu.SemaphoreType.DMA((n_sems,)), _hbm(src.shape, src.dtype),
                   _hbm(land_shape, src.dtype), jax.ShapeDtypeStruct(TOKEN_SHAPE, F32)],
        input_output_aliases={0: 2, 1: 3}, compiler_params=pltpu.CompilerParams(has_side_effects=SIDE_EFFECT),
    )(_in_hbm(src), _in_hbm(lax.empty(land_shape, src.dtype)))
    return tuple(res[:4]), res[4]


def _swap_copy(src_ref, land_ref, ssem, rsem, which, to):
    return pltpu.make_async_remote_copy(
        src_ref=src_ref.at[:, :, _half_cols(src_ref.shape[2], which)], dst_ref=land_ref,
        send_sem=ssem.at[0], recv_sem=rsem.at[0], device_id=to, device_id_type=MESH)


def _swap_start(grad, *, name):
    def start(src_ref, land_ref, ssem, rsem):
        x, y, c, _ = _place()
        _swap_copy(src_ref, land_ref, ssem, rsem, 1 - c, (x, y, 1 - c)).start()

    s, r, cols = grad.shape
    return _split_start(start, grad, (s, r, cols // 2), 1, name=name)


def _swap_wait(parts, after, *, name):
    def wait(src_ref, land_ref, ssem_ref, rsem_ref):
        x, y, c, _ = _place()
        cp = _swap_copy(src_ref, land_ref, ssem_ref, rsem_ref, 1 - c, (x, y, c))
        cp.wait_send()
        cp.wait_recv()

    return _wait_call(wait, parts, after, name=name)


def _dev_peers(x, y, c, chips):
    return [(x, y, 1 - c)] + [(cx, cy, c) for cx, cy in chips] + [(cx, cy, 1 - c) for cx, cy in chips]


def _dev_gather_start(part, *, name):
    def start(src_ref, land_ref, ssem, rsem):
        x, y, c, chips = _place()
        for k, to in enumerate(_dev_peers(x, y, c, chips)):
            pltpu.make_async_remote_copy(
                src_ref=src_ref, dst_ref=land_ref.at[4 * x + 2 * y + c], send_sem=ssem.at[k], recv_sem=rsem.at[k],
                device_id=to, device_id_type=MESH).start()

    return _split_start(start, part, (N_DEV,) + part.shape, N_DEV - 1, name=name)


def _dev_gather_wait(parts, after, *, name):
    def wait(src_ref, land_ref, ssem_ref, rsem_ref):
        x, y, c, chips = _place()
        for k, (px, py, pc) in enumerate(_dev_peers(x, y, c, chips)):
            cp = pltpu.make_async_remote_copy(
                src_ref=src_ref, dst_ref=land_ref.at[4 * px + 2 * py + pc], send_sem=ssem_ref.at[k],
                recv_sem=rsem_ref.at[k], device_id=(x, y, c), device_id_type=MESH)
            cp.wait_send()
            cp.wait_recv()

    return _wait_call(wait, parts, after, name=name)[1]


def _sibling_share_halves(arrays, *, name):
    n = len(arrays)

    def body(*refs):
        bufs = refs[n:2 * n]
        send_sems, recv_sems = refs[2 * n:]
        x, y, c, _ = _place()
        copies = []
        for a in range(n):
            mine = bufs[a].at[:, _half_cols(bufs[a].shape[1], c)]
            cp = pltpu.make_async_remote_copy(
                src_ref=mine, dst_ref=mine, send_sem=send_sems.at[a], recv_sem=recv_sems.at[a],
                device_id=(x, y, 1 - c), device_id_type=MESH)
            cp.start()
            copies.append(cp)
        for a in range(n):
            theirs = bufs[a].at[:, _half_cols(bufs[a].shape[1], 1 - c)]
            pltpu.make_async_remote_copy(
                src_ref=theirs, dst_ref=theirs, send_sem=send_sems.at[a], recv_sem=recv_sems.at[a],
                device_id=(x, y, c), device_id_type=MESH).wait_recv()
        for cp in copies:
            cp.wait_send()

    return pl.pallas_call(
        body, name=name, in_specs=[HBM] * n, out_specs=[HBM] * n,
        out_shape=[jax.ShapeDtypeStruct(h.shape, h.dtype) for h in arrays],
        input_output_aliases={a: a for a in range(n)},
        scratch_shapes=[pltpu.SemaphoreType.DMA((n,)), pltpu.SemaphoreType.DMA((n,))],
    )(*arrays)


def _pack(arrays, rows_multiple=16, width=LANES):
    flat = jnp.concatenate([a.astype(F32).reshape(-1) for a in arrays])
    total = flat.shape[0]
    rows = -(-total // width)
    rows = -(-rows // rows_multiple) * rows_multiple
    return jnp.pad(flat, (0, rows * width - total)).reshape(rows, width)


def _unpack(buf, shapes):
    flat = buf.reshape(-1)
    out, off = [], 0
    for s in shapes:
        n = math.prod(s)
        out.append(flat[off:off + n].reshape(s))
        off += n
    return out


def kernel(x, norm_pre, norm_post, gla_w_in, gla_w_gate2, gla_b_gate, gla_o_gain, gla_w_out, sgu_w_in, sgu_ln_gain, sgu_ln_bias, sgu_w_spatial, sgu_b_spatial, sgu_w_out, loss_target, m_norm_pre, m_norm_post, m_gla_w_in, m_gla_w_gate2, m_gla_b_gate, m_gla_o_gain, m_gla_w_out, m_sgu_w_in, m_sgu_ln_gain, m_sgu_ln_bias, m_sgu_w_spatial, m_sgu_b_spatial, m_sgu_w_out, v_norm_pre, v_norm_post, v_gla_w_in, v_gla_w_gate2, v_gla_b_gate, v_gla_o_gain, v_gla_w_out, v_sgu_w_in, v_sgu_ln_gain, v_sgu_ln_bias, v_sgu_w_spatial, v_sgu_b_spatial, v_sgu_w_out):
    _, t, d = x.shape
    dk = d // 2
    ws = gla_w_in.shape[2]
    wp = -(-ws // LANES) * LANES
    lay = (ws, wp)
    chip =2 * lax.axis_index("x") + lax.axis_index("y")
    core = lax.axis_index("c")
    core_idx = core.astype(jnp.int32).reshape(1)
    others = jnp.arange(N_CHIPS - 1, dtype=jnp.int32)
    others = others + (others >= chip).astype(jnp.int32)
    slots = jnp.concatenate([chip.astype(jnp.int32).reshape(1), others, core_idx])

    x0 = x[0]
    target = loss_target[0]

    wt_in_g, mt_in_g, vt_in_g = gla_w_in[0].T, m_gla_w_in[0].T, v_gla_w_in[0].T

    small_shard = _pack([gla_w_gate2[0], sgu_ln_gain[0], sgu_ln_bias[0]], rows_multiple=8, width=2 * LANES)
    own = [small_shard, jnp.pad(wt_in_g.astype(BF16), ((0, wp - ws), (0, 0)))]
    in_flight, token = _gather_start(own, name="gather_start_a")
    own_later = [gla_w_out[0].astype(BF16), sgu_w_in[0].astype(BF16), sgu_w_out[0].astype(BF16)]
    in_flight_later, token_later = _gather_start(own_later, name="gather_start_b", after=[token])
    own, in_flight = own + own_later, in_flight + in_flight_later

    def with_own(i, land):
        return lax.dynamic_update_slice(land, own[i][None], (chip, 0, 0))

    def arrived(i, after, name):
        land = _gather_wait(in_flight[i], after, name=name + "_wait")[3]
        return with_own(i, _sibling_forward(land, name=name + "_share"))

    h0 = _norm_pre(x0, norm_pre[0:1] + token[0:1, 0:1] + token_later[0:1, 0:1], name="pre0")
    g_small = arrived(0, h0, "w_small")
    wt_g = arrived(1, [g_small, wt_in_g, mt_in_g, vt_in_g], "w_gla_in").reshape(N_CHIPS * wp, d)
    shard_shapes = [gla_w_gate2.shape[1:], sgu_ln_gain.shape[1:], sgu_ln_bias.shape[1:]]
    per_chip = [_unpack(g_small[j], shard_shapes) for j in range(N_CHIPS)]
    w2_full = jnp.concatenate([p[0] for p in per_chip], axis=1)
    ln_gain = jnp.concatenate([p[1] for p in per_chip], axis=0)[None, :]
    ln_bias = jnp.concatenate([p[2] for p in per_chip], axis=0)[None, :]
    w2p = jnp.pad(w2_full, ((0, LANES - GLA_GATE_RANK), (0, 0)))

    pos_chunk = jnp.arange(SGU_BLOCK) // CHUNK
    mask = pos_chunk[:, None] >= pos_chunk[None, :]
    ws_masked = jnp.where(mask[None], sgu_w_spatial[0], 0.0)
    ws_masked_t = ws_masked.transpose(0, 2, 1)
    bs_t = sgu_b_spatial[0].T

    proj0 = _matmul(h0, wt_g, mode="nt", out_dtype=F32, name="gla_in", tn=wp)
    o0, a0, s_before, s_final = _gla_fwd(proj0, w2p, gla_b_gate, gla_o_gain, lay, name="gla_scan")
    w_out_g = arrived(2, a0, "w_gla_out").reshape(d, d)
    y0 = _matmul(a0, w_out_g, mode="nn", out_dtype=F32, name="gla_out")
    x1, h1 = _post_then_pre(x0, y0, norm_post[0:1], norm_pre[1:2], name="post0_pre1")
    g_wi_s = arrived(3, h1, "w_sgu_in")
    proj1 = _matmul(h1, g_wi_s, mode="nn", out_dtype=F32, name="sgu_in", b_shards=True)
    a1 = _sgu_fwd(proj1, ln_gain, ln_bias, ws_masked, bs_t, name="sgu_gate")
    w_out_s = arrived(4, a1, "w_sgu_out").reshape(d, d)
    y1 = _matmul(a1, w_out_s, mode="nn", out_dtype=F32, name="sgu_out")
    loss_part, dx2, dy1, d_post1 = _loss_head(x1, y1, norm_post[1:2], target, name="loss_head")

    def behind(small, token):
        return small + token[0:1, 0:1]

    def pair_and_scatter(swap, after, name):
        grad, peer = _swap_wait(swap, after, name=name + "_swap_wait")
        pair = _pair_sum_bf16(grad, core_idx, peer, name=name + "_pair")
        return _scatter_start(pair, name=name + "_start")

    def reduced(flight, after, name):
        pair, landed = _scatter_wait(flight, after, name=name + "_wait")
        return _chip_sum(pair, landed, slots, name=name + "_sum")

    dw_out_s = _matmul(a1, dy1, mode="tn", out_dtype=F32, name="d_sgu_w_out")
    swap, tok = _swap_start(dw_out_s.reshape(N_CHIPS, d // N_CHIPS, d), name="g_sgu_out_swap")
    da1 = _matmul(dy1, w_out_s, mode="nt", out_dtype=F32, name="d_sgu_act", after=tok)
    fl_wo_s, tok = pair_and_scatter(swap, da1, "g_sgu_out")
    dproj1, d_ws, d_bs_t, d_lg, d_lb = _sgu_bwd(da1, proj1, ln_gain, behind(ln_bias, tok), ws_masked, ws_masked_t,
                                                bs_t, name="sgu_gate_bwd")
    dw_in_s = _matmul(h1, dproj1, mode="tn", out_dtype=F32, name="d_sgu_w_in", out_shards=True)
    swap, tok = _swap_start(dw_in_s, name="g_sgu_in_swap")
    dh1 = _matmul_nt_shards(dproj1, g_wi_s, out_dtype=F32, name="d_sgu_h", after=tok)
    fl_wi_s, tok = pair_and_scatter(swap, dh1, "g_sgu_in")
    dx1, dy0, d_pre1, d_post0 = _mid_bwd(dx2, dh1, x1, behind(norm_pre[1:2], tok), y0, norm_post[0:1],
                                         name="pre1_post0_bwd")
    dw_out_g = _matmul(a0, dy0, mode="tn", out_dtype=F32, name="d_gla_w_out")
    swap, tok = _swap_start(dw_out_g.reshape(N_CHIPS, d // N_CHIPS, d), name="g_gla_out_swap")
    da0 = _matmul(dy0, w_out_g, mode="nt", out_dtype=F32, name="d_gla_act", after=tok)
    fl_wo_g, tok = pair_and_scatter(swap, da0, "g_gla_out")
    dproj0, d_og, d_bg, d_w2p = _gla_bwd(da0, o0, proj0, w2p, behind(gla_b_gate, tok), gla_o_gain, s_before, s_final,
                                         lay, name="gla_scan_bwd")
    early_shapes = [norm_post.shape, gla_b_gate.shape, gla_o_gain.shape, sgu_w_spatial.shape, sgu_b_spatial.shape,
                    (1, GLA_GATE_RANK, dk), (1, d), (1, d), (1, LANES)]
    early_part = _pack([jnp.concatenate([d_post0, d_post1], axis=0), d_bg, d_og, jnp.where(mask[None], d_ws, 0.0)[None],
                        d_bs_t.T[None], d_w2p[:GLA_GATE_RANK][None], d_lg, d_lb, loss_part])
    early_flight, tok = _dev_gather_start(early_part, name="small_early_start")
    dwt_in_g = _matmul(dproj0, h0, mode="tn", out_dtype=F32, name="d_gla_w_in", tm=wp, after=tok)
    swap, tok = _swap_start(dwt_in_g.reshape(N_CHIPS, wp, d), name="g_gla_in_swap")
    r_wo_s = reduced(fl_wo_s, tok, "g_sgu_out")
    r_wi_s = reduced(fl_wi_s, r_wo_s, "g_sgu_in")
    r_wo_g = reduced(fl_wo_g, r_wi_s, "g_gla_out")
    sharing, tok = _share_start([r_wo_s, r_wi_s, r_wo_g], name="grads_share_a")
    fl_wi_g, tok = pair_and_scatter(swap, tok, "g_gla_in")
    dh0 = _matmul(dproj0, wt_g, mode="nn", out_dtype=F32, name="d_gla_h", tk=N_CHIPS * wp, after=tok)
    grad_x, d_pre0 = _first_bwd(dx1, dh0, x0, norm_pre[0:1], name="pre0_bwd")

    late_part = _pack([jnp.concatenate([d_pre0, d_pre1], axis=0)])
    late_flight, tok = _dev_gather_start(late_part, name="small_late_start")

    def big_update(w, g, m, v, name):
        return [u[None] for u in _adamw(w[0], g, m[0], v[0], name=name)]

    g_wo_sgu, g_wi_sgu, g_wo_gla = _share_wait(sharing, [grad_x, tok], name="grads_share_a_wait")
    u_wo_sgu = big_update(sgu_w_out, g_wo_sgu, m_sgu_w_out, v_sgu_w_out, "adamw_sgu_w_out")
    u_wi_sgu = big_update(sgu_w_in, g_wi_sgu, m_sgu_w_in, v_sgu_w_in, "adamw_sgu_w_in")
    u_wo_gla = big_update(gla_w_out, g_wo_gla, m_gla_w_out, v_gla_w_out, "adamw_gla_w_out")

    def summed_over_devices(part, flight, after, shapes, name):
        land = _dev_gather_wait(flight, after, name=name + "_wait")
        every = lax.dynamic_update_slice(land, part[None], (2 * chip + core, 0, 0))
        return _unpack(_stack_sum(every, name=name + "_sum"), shapes)

    updated = [u_wo_gla[1], u_wi_sgu[1], u_wo_sgu[1]]
    (g_post, g_bg, g_og, g_wsp, g_bsp, g_w2_full, g_lg_full, g_lb_full, loss_vec) = summed_over_devices(
        early_part, early_flight, updated, early_shapes, "small_early")
    g_pre, = summed_over_devices(late_part, late_flight, updated, [norm_pre.shape], "small_late")
    loss = loss_vec[0, 0]
    g_w2 = lax.dynamic_slice_in_dim(g_w2_full, chip * (dk // N_CHIPS), dk // N_CHIPS, axis=2)
    g_lg = lax.dynamic_slice_in_dim(g_lg_full, chip * (d // N_CHIPS), d // N_CHIPS, axis=1)
    g_lb = lax.dynamic_slice_in_dim(g_lb_full, chip * (d // N_CHIPS), d // N_CHIPS, axis=1)

    small_w = [norm_pre, norm_post, gla_b_gate, gla_o_gain, sgu_w_spatial, sgu_b_spatial, gla_w_gate2, sgu_ln_gain,
               sgu_ln_bias]
    small_g = [g_pre, g_post, g_bg, g_og, g_wsp, g_bsp, g_w2, g_lg, g_lb]
    small_m = [m_norm_pre, m_norm_post, m_gla_b_gate, m_gla_o_gain, m_sgu_w_spatial, m_sgu_b_spatial, m_gla_w_gate2,
               m_sgu_ln_gain, m_sgu_ln_bias]
    small_v = [v_norm_pre, v_norm_post, v_gla_b_gate, v_gla_o_gain, v_sgu_w_spatial, v_sgu_b_spatial, v_gla_w_gate2,
               v_sgu_ln_gain, v_sgu_ln_bias]
    own_shapes = [w.shape for w in small_w]
    _, s_dl, s_m, s_v = _adamw(_pack(small_w), _pack(small_g), _pack(small_m), _pack(small_v), name="adamw_small")
    dl_s, m_s, v_s = _unpack(s_dl, own_shapes), _unpack(s_m, own_shapes), _unpack(s_v, own_shapes)

    r_wi_g = reduced(fl_wi_g, s_dl, "g_gla_in")
    gt_wi_gla, = _sibling_share_halves([r_wi_g], name="grads_share_b")
    u_wi_gla = [u.T[None] for u in _adamw(wt_in_g, gt_wi_gla, mt_in_g, vt_in_g, name="adamw_gla_w_in")]

    def ordered(small, kind):
        pre, post, bg, og, wsp, bsp, w2, lg, lb = small
        return [pre, post, u_wi_gla[kind], w2, bg, og, u_wo_gla[kind], u_wi_sgu[kind], lg, lb, wsp, bsp, u_wo_sgu[kind]]

    return (loss, grad_x[None], *ordered(small_g, 0), *ordered(dl_s, 1), *ordered(m_s, 2), *ordered(v_s, 3))
```

```python
import functools
import math

import jax
import jax.numpy as jnp
from jax import lax
from jax.experimental import pallas as pl
from jax.experimental.pallas import tpu as pltpu

F32 = jnp.float32
BF16 = jnp.bfloat16
MESH = pl.DeviceIdType.MESH

EPS = 1e-6
CHUNK = 64
GLA_HEADS = 4
GLA_GATE_RANK = 16
GLA_TAU = 16.0
SGU_BLOCK = 128
SGU_GROUPS = 8
N_CHIPS = 4
N_DEV = 8
LANES = 128

ADAM_LR = 0.001
ADAM_B1 = 0.9
ADAM_B2 = 0.999
ADAM_EPS = 1e-08
ADAM_WD = 0.01
ADAM_STEP = 10

VMEM_LIMIT = 56 * 1024 * 1024


def _cparams(sem=None):
    return pltpu.CompilerParams(dimension_semantics=sem, vmem_limit_bytes=VMEM_LIMIT)


def _pick(n, cap, unit=LANES):
    best = None
    for t in range(unit, min(n, cap) + 1, unit):
        if n % t == 0:
            best = t
    assert best is not None, (n, cap, unit)
    return best


def _dot(a, b, dims):
    return lax.dot_general(a, b, (dims, ((), ())), preferred_element_type=F32)


def _dot_nn(a, b):
    return _dot(a, b, ((1,), (0,)))


def _dot_nt(a, b):
    return _dot(a, b, ((1,), (1,)))


def _dot_tn(a, b):
    return _dot(a, b, ((0,), (0,)))


def _matmul(a, b, *, mode, out_dtype, name, tm=1024, tn=512, tk=2048, b_shards=False, out_shards=False, after=None,
            out_rows=None):
    if mode == "tn":
        K, M = a.shape
    else:
        M, K = a.shape
    if b_shards:
        ns, br, bc = b.shape
        if mode == "nt":
            N, Kb = br, ns * bc
        else:
            Kb, N = br, ns * bc
    else:
        if mode == "nt":
            N, Kb = b.shape
        else:
            Kb, N = b.shape
    assert K == Kb, (a.shape, b.shape, mode)
    tm = _pick(M, tm)
    tk = _pick(K, tk)
    if b_shards and mode != "nt":
        tn = _pick(bc, tn)
    elif out_shards:
        tn = _pick(N // N_CHIPS, tn)
    else:
        tn = _pick(N, tn)
    if b_shards and mode == "nt":
        tk = _pick(bc, tk)
    nk = K // tk
    grid = (M // tm, N // tn, nk)

    if mode == "tn":
        a_spec = pl.BlockSpec((tk, tm), lambda i, j, k: (k, i))
    else:
        a_spec = pl.BlockSpec((tm, tk), lambda i, j, k: (i, k))
    if b_shards:
        if mode == "nt":
            per = bc // tk
            b_spec = pl.BlockSpec((None, tn, tk), lambda i, j, k: (k // per, j, k % per))
        else:
            per = bc // tn
            b_spec = pl.BlockSpec((None, tk, tn), lambda i, j, k: (j // per, k, j % per))
    elif mode == "nt":
        b_spec = pl.BlockSpec((tn, tk), lambda i, j, k: (j, k))
    else:
        b_spec = pl.BlockSpec((tk, tn), lambda i, j, k: (k, j))
    if out_shards:
        per_o = (N // N_CHIPS) // tn
        out_spec = pl.BlockSpec((None, tm, tn), lambda i, j, k: (j // per_o, i, j % per_o))
        out_shape = jax.ShapeDtypeStruct((N_CHIPS, M, N // N_CHIPS), out_dtype)
    else:
        out_spec = pl.BlockSpec((tm, tn), lambda i, j, k: (i, j))
        out_shape = jax.ShapeDtypeStruct((M if out_rows is None else out_rows, N), out_dtype)

    dims = {"nn": ((1,), (0,)), "nt": ((1,), (1,)), "tn": ((0,), (0,))}[mode]

    def body(a_ref, b_ref, *rest):
        o_ref, scratch = (rest[1], rest[2:]) if after is not None else (rest[0], rest[1:])
        part = _dot(a_ref[...].astype(BF16), b_ref[...].astype(BF16), dims)
        if nk == 1:
            o_ref[...] = part.astype(out_dtype)
        else:
            acc_ref, = scratch
            k = pl.program_id(2)

            @pl.when(k == 0)
            def _():
                acc_ref[...] = part

            @pl.when(k > 0)
            def _():
                acc_ref[...] += part

            @pl.when(k == nk - 1)
            def _():
                o_ref[...] = acc_ref[...].astype(out_dtype)

    extra_specs, extra_args = ([], []) if after is None else ([pl.BlockSpec(memory_space=pl.ANY)], [after])
    return pl.pallas_call(
        body, name=name, grid=grid, in_specs=[a_spec, b_spec] + extra_specs, out_specs=out_spec, out_shape=out_shape,
        scratch_shapes=[] if nk == 1 else [pltpu.VMEM((tm, tn), F32)],
        compiler_params=_cparams(("parallel", "parallel", "arbitrary")),
    )(a, b, *extra_args)


def _matmul_into_cols(a, w, which, buf, *, name, tm=1024):
    M, K = a.shape
    _, N, _ = w.shape
    tm = _pick(M, tm)

    def body(which_ref, a_ref, w_ref, buf_ref, o_ref):
        o_ref[...] = _dot_nt(a_ref[...], w_ref[...])

    grid_spec = pltpu.PrefetchScalarGridSpec(
        num_scalar_prefetch=1, grid=(M // tm,),
        in_specs=[pl.BlockSpec((tm, K), lambda i, s: (i, 0)), pl.BlockSpec((None, N, K), lambda i, s: (s[1], 0, 0)),
                  pl.BlockSpec(memory_space=pl.ANY)],
        out_specs=pl.BlockSpec((tm, N), lambda i, s: (i, s[0])))
    return pl.pallas_call(
        body, name=name, grid_spec=grid_spec, out_shape=jax.ShapeDtypeStruct(buf.shape, buf.dtype),
        input_output_aliases={3: 0}, compiler_params=_cparams(("parallel",)),
    )(which, a, w, buf)


def _matmul_nt_shards(a, b, *, out_dtype, name, tm=1024, tn=512, after=None):
    M, K = a.shape
    ns, N, kc = b.shape
    assert K == ns * kc
    tm, tn = _pick(M, tm), _pick(N, tn)

    def body(a_ref, *rest):
        b_refs, o_ref = rest[:ns], rest[ns + (after is not None)]
        acc = _dot_nt(a_ref[:, 0:kc], b_refs[0][...])
        for j in range(1, ns):
            acc += _dot_nt(a_ref[:, j * kc:(j + 1) * kc], b_refs[j][...])
        o_ref[...] = acc.astype(out_dtype)

    def shard(j):
        return pl.BlockSpec((None, tn, kc), lambda i, n: (j, n, 0))

    extra_specs, extra_args = ([], []) if after is None else ([pl.BlockSpec(memory_space=pl.ANY)], [after])
    return pl.pallas_call(
        body, name=name, grid=(M // tm, N // tn),
        in_specs=[pl.BlockSpec((tm, K), lambda i, n: (i, 0))] + [shard(j) for j in range(ns)] + extra_specs,
        out_specs=pl.BlockSpec((tm, tn), lambda i, n: (i, n)), out_shape=jax.ShapeDtypeStruct((M, N), out_dtype),
        compiler_params=_cparams(("parallel", "parallel")),
    )(a, *([b] * ns), *extra_args)


def _rstd(x):
    return lax.rsqrt(jnp.mean(x * x, axis=-1, keepdims=True) + EPS)


def _row_spec(tr, d):
    return pl.BlockSpec((tr, d), lambda i: (i, 0))


def _vec_spec(d):
    return pl.BlockSpec((1, d), lambda i: (0, 0))


def _acc_rows(ref, i, val, cols=slice(None)):
    @pl.when(i == 0)
    def _():
        ref[:, cols] = val

    @pl.when(i > 0)
    def _():
        ref[:, cols] += val


def _norm_pre(x, gain, *, name, tr=256):
    t, d = x.shape
    tr = _pick(t, tr, 8)

    def body(x_ref, g_ref, h_ref):
        xv = x_ref[...]
        h_ref[...] = (xv * _rstd(xv) * g_ref[...]).astype(BF16)

    return pl.pallas_call(
        body, name=name, grid=(t // tr,), in_specs=[_row_spec(tr, d), _vec_spec(d)], out_specs=_row_spec(tr, d),
        out_shape=jax.ShapeDtypeStruct((t, d), BF16), compiler_params=_cparams(("parallel",)),
    )(x, gain)


def _post_then_pre(x, y, post_gain, pre_gain, *, name, tr=256):
    t, d = x.shape
    tr = _pick(t, tr, 8)

    def body(x_ref, y_ref, pg_ref, ng_ref, xn_ref, h_ref):
        yv = y_ref[...]
        xn = x_ref[...] + yv * _rstd(yv) * pg_ref[...]
        xn_ref[...] = xn
        h_ref[...] = (xn * _rstd(xn) * ng_ref[...]).astype(BF16)

    return pl.pallas_call(
        body, name=name, grid=(t // tr,),
        in_specs=[_row_spec(tr, d), _row_spec(tr, d), _vec_spec(d), _vec_spec(d)],
        out_specs=[_row_spec(tr, d), _row_spec(tr, d)],
        out_shape=[jax.ShapeDtypeStruct((t, d), F32), jax.ShapeDtypeStruct((t, d), BF16)],
        compiler_params=_cparams(("parallel",)),
    )(x, y, post_gain, pre_gain)


def _norm_bwd(dy, n, r, gain):
    dn = dy * gain
    return r * (dn - n * jnp.mean(dn * n, axis=-1, keepdims=True))


def _loss_head(x, y, post_gain, target, *, name, tr=256):
    t, d = x.shape
    tr = _pick(t, tr, 8)

    def body(x_ref, y_ref, pg_ref, t_ref, loss_ref, dx_ref, dy_ref, dpg_ref):
        i = pl.program_id(0)
        yv = y_ref[...]
        r = _rstd(yv)
        n = yv * r
        err = x_ref[...] + n * pg_ref[...] - t_ref[...]
        dx = err * (1.0 / d)
        dx_ref[...] = dx
        part = 0.5 * jnp.sum(jnp.mean(err * err, axis=-1, keepdims=True), axis=0, keepdims=True)
        _acc_rows(loss_ref, i, jnp.broadcast_to(part, (1, LANES)))
        _acc_rows(dpg_ref, i, jnp.sum(dx * n, axis=0, keepdims=True))
        dy_ref[...] = _norm_bwd(dx, n, r, pg_ref[...]).astype(BF16)

    return pl.pallas_call(
        body, name=name, grid=(t // tr,),
        in_specs=[_row_spec(tr, d), _row_spec(tr, d), _vec_spec(d), _row_spec(tr, d)],
        out_specs=[_vec_spec(LANES), _row_spec(tr, d), _row_spec(tr, d), _vec_spec(d)],
        out_shape=[jax.ShapeDtypeStruct((1, LANES), F32), jax.ShapeDtypeStruct((t, d), F32),
                   jax.ShapeDtypeStruct((t, d), BF16), jax.ShapeDtypeStruct((1, d), F32)],
        compiler_params=_cparams(("arbitrary",)),
    )(x, y, post_gain, target)


def _mid_bwd(dx_out, dh, x, pre_gain, y_prev, post_gain_prev, *, name, tr=256):
    t, d = x.shape
    tr = _pick(t, tr, 8)

    def body(dxo_ref, dh_ref, x_ref, ng_ref, y_ref, pg_ref, dx_ref, dy_ref, dng_ref, dpg_ref):
        i = pl.program_id(0)
        xv = x_ref[...]
        r = _rstd(xv)
        xh = xv * r
        dhv = dh_ref[...]
        _acc_rows(dng_ref, i, jnp.sum(dhv * xh, axis=0, keepdims=True))
        dx = dxo_ref[...] + _norm_bwd(dhv, xh, r, ng_ref[...])
        dx_ref[...] = dx
        yv = y_ref[...]
        ry = _rstd(yv)
        n = yv * ry
        _acc_rows(dpg_ref, i, jnp.sum(dx * n, axis=0, keepdims=True))
        dy_ref[...] = _norm_bwd(dx, n, ry, pg_ref[...]).astype(BF16)

    return pl.pallas_call(
        body, name=name, grid=(t // tr,),
        in_specs=[_row_spec(tr, d), _row_spec(tr, d), _row_spec(tr, d), _vec_spec(d), _row_spec(tr, d), _vec_spec(d)],
        out_specs=[_row_spec(tr, d), _row_spec(tr, d), _vec_spec(d), _vec_spec(d)],
        out_shape=[jax.ShapeDtypeStruct((t, d), F32), jax.ShapeDtypeStruct((t, d), BF16),
                   jax.ShapeDtypeStruct((1, d), F32), jax.ShapeDtypeStruct((1, d), F32)],
        compiler_params=_cparams(("arbitrary",)),
    )(dx_out, dh, x, pre_gain, y_prev, post_gain_prev)


def _first_bwd(dx_out, dh, x, pre_gain, *, name, tr=256):
    t, d = x.shape
    tr = _pick(t, tr, 8)

    def body(dxo_ref, dh_ref, x_ref, ng_ref, dx_ref, dng_ref):
        i = pl.program_id(0)
        xv = x_ref[...]
        r = _rstd(xv)
        xh = xv * r
        dhv = dh_ref[...]
        _acc_rows(dng_ref, i, jnp.sum(dhv * xh, axis=0, keepdims=True))
        dx_ref[...] = dxo_ref[...] + _norm_bwd(dhv, xh, r, ng_ref[...])

    return pl.pallas_call(
        body, name=name, grid=(t // tr,),
        in_specs=[_row_spec(tr, d), _row_spec(tr, d), _row_spec(tr, d), _vec_spec(d)],
        out_specs=[_row_spec(tr, d), _vec_spec(d)],
        out_shape=[jax.ShapeDtypeStruct((t, d), F32), jax.ShapeDtypeStruct((1, d), F32)],
        compiler_params=_cparams(("arbitrary",)),
    )(dx_out, dh, x, pre_gain)


def _sigmoid(x):
    return 1.0 / (1.0 + jnp.exp(-x))


def _log_sigmoid(x):
    return jnp.minimum(x, 0.0) - jnp.log(1.0 + jnp.exp(-jnp.abs(x)))


_GELU_C = math.sqrt(2.0 / math.pi)


_GELU_A = 0.044715


def _gelu_parts(x, with_grad=True):
    x2 = x * x
    h = 0.5 * jnp.tanh(x * (_GELU_C + (_GELU_C * _GELU_A) * x2)) + 0.5
    val = x * h
    if not with_grad:
        return val, None
    return val, h * (1.0 + (1.0 - h) * (x * (2.0 * _GELU_C + (6.0 * _GELU_C * _GELU_A) * x2)))


def _split3(x):
    hi = x.astype(BF16)
    r1 = x - hi.astype(F32)
    mid = r1.astype(BF16)
    lo = (r1 - mid.astype(F32)).astype(BF16)
    return hi, mid, lo


def _tri_matmul(tri_bf16, x):
    hi, mid, lo = _split3(x)
    return _dot_nn(tri_bf16, hi) + _dot_nn(tri_bf16, mid) + _dot_nn(tri_bf16, lo)


def _gla_dims(d):
    dk, dv = d // 2, d
    return dk, dv, dk // GLA_HEADS, dv // GLA_HEADS


def _col_pieces(a, b, lay):
    ws, wp = lay
    out = []
    while a < b:
        j = a // ws
        end = min(b, (j + 1) * ws)
        out.append((j * wp + a - j * ws, end - a))
        a = end
    return out


def _load_cols(ref, a, b, lay):
    parts = [ref[:, s:s + n] for s, n in _col_pieces(a, b, lay)]
    return parts[0] if len(parts) == 1 else jnp.concatenate(parts, axis=1)


def _store_cols(ref, a, val, lay):
    off = 0
    for s, n in _col_pieces(a, a + val.shape[1], lay):
        ref[:, s:s + n] = val[:, off:off + n]
        off += n


def _gate_window(c_r, lay):
    (start, _), = _col_pieces(c_r, c_r + GLA_GATE_RANK, lay)
    assert (start % lay[1]) + LANES <= lay[1]
    return slice(start, start + LANES)


def _gla_gates(glr, k, w2_ref, b_ref):
    z = _dot_nn(glr.astype(BF16), w2_ref[...].astype(BF16)) + b_ref[...]
    la = _log_sigmoid(z) * (1.0 / GLA_TAU)
    row = lax.broadcasted_iota(jnp.int32, (CHUNK, CHUNK), 0)
    col = lax.broadcasted_iota(jnp.int32, (CHUNK, CHUNK), 1)
    incl = (row >= col).astype(BF16)
    bcum = _tri_matmul(incl, la)
    b_end = bcum[CHUNK - 1:CHUNK, :]
    e_rest = jnp.exp(b_end - bcum)
    return z, e_rest, k * e_rest, jnp.exp(b_end)


def _gla_fwd(proj, w2p, b_gate, o_gain, lay, *, name):
    t, wcols = proj.shape
    d = o_gain.shape[1]
    dk, dv, dkh, dvh = _gla_dims(d)
    nc = t // CHUNK
    c_k, c_v, c_g, c_r = dk, 2 * dk, 2 * dk + dv, 2 * dk + 2 * dv
    scale = dkh ** -0.5

    def body(p_ref, w2_ref, b_ref, og_ref, o_ref, a_ref, sb_ref, sfin_ref, s_ref):
        i = pl.program_id(0)

        @pl.when(i == 0)
        def _():
            s_ref[...] = jnp.zeros_like(s_ref)

        q = _load_cols(p_ref, 0, dk, lay) * scale
        k = _load_cols(p_ref, c_k, c_k + dk, lay)
        glr = p_ref[:, _gate_window(c_r, lay)]
        _, _, kdec, decay = _gla_gates(glr, k, w2_ref, b_ref)
        for h in range(GLA_HEADS):
            ks = slice(h * dkh, (h + 1) * dkh)
            vs = slice(h * dvh, (h + 1) * dvh)
            v_h = _load_cols(p_ref, c_v + h * dvh, c_v + (h + 1) * dvh, lay)
            g_h = _load_cols(p_ref, c_g + h * dvh, c_g + (h + 1) * dvh, lay)
            s_old = s_ref[h]
            sb_ref[0, h] = s_old
            s_new = s_old * decay[:, ks] + _dot_tn(v_h.astype(BF16), kdec[:, ks].astype(BF16))
            s_ref[h] = s_new
            o_h = _dot_nt(q[:, ks].astype(BF16), s_new.astype(BF16))
            o_ref[:, vs] = o_h
            on = o_h * _rstd(o_h)
            a_ref[:, vs] = (on * og_ref[:, vs] * (g_h * _sigmoid(g_h))).astype(BF16)

        @pl.when(i == nc - 1)
        def _():
            sfin_ref[...] = s_ref[...]

    full = lambda *shape: pl.BlockSpec(shape, lambda i: (0,) * len(shape))
    return pl.pallas_call(
        body, name=name, grid=(nc,),
        in_specs=[pl.BlockSpec((CHUNK, wcols), lambda i: (i, 0)), full(LANES, dk), full(1, dk), full(1, dv)],
        out_specs=[pl.BlockSpec((CHUNK, dv), lambda i: (i, 0)), pl.BlockSpec((CHUNK, dv), lambda i: (i, 0)),
                   pl.BlockSpec((1, GLA_HEADS, dvh, dkh), lambda i: (i, 0, 0, 0)), full(GLA_HEADS, dvh, dkh)],
        out_shape=[jax.ShapeDtypeStruct((t, dv), F32), jax.ShapeDtypeStruct((t, dv), BF16),
                   jax.ShapeDtypeStruct((nc, GLA_HEADS, dvh, dkh), F32),
                   jax.ShapeDtypeStruct((GLA_HEADS, dvh, dkh), F32)],
        scratch_shapes=[pltpu.VMEM((GLA_HEADS, dvh, dkh), F32)],
        compiler_params=_cparams(("arbitrary",)),
    )(proj, w2p, b_gate, o_gain)


def _gla_bwd(da, o, proj, w2p, b_gate, o_gain, s_before, s_final, lay, *, name):
    t, wcols = proj.shape
    d = o_gain.shape[1]
    dk, dv, dkh, dvh = _gla_dims(d)
    nc = t // CHUNK
    c_k, c_v, c_g, c_r = dk, 2 * dk, 2 * dk + dv, 2 * dk + 2 * dv
    scale = dkh ** -0.5

    def body(da_ref, o_ref, p_ref, w2_ref, b_ref, og_ref, sb_ref, sfin_ref,
             dp_ref, dog_ref, db_ref, dw2_ref, s_ref, gc_ref, dkd_ref):
        i = pl.program_id(0)

        @pl.when(i == 0)
        def _():
            s_ref[...] = sfin_ref[...]
            gc_ref[...] = jnp.zeros_like(gc_ref)

        ws, wp = lay
        for j in range(N_CHIPS):
            dp_ref[:, j * wp + ws:(j + 1) * wp] = jnp.zeros((CHUNK, wp - ws), BF16)
        q = _load_cols(p_ref, 0, dk, lay) * scale
        k = _load_cols(p_ref, c_k, c_k + dk, lay)
        glr = p_ref[:, _gate_window(c_r, lay)]
        z, e_rest, kdec, decay = _gla_gates(glr, k, w2_ref, b_ref)
        ddecay = []
        for h in range(GLA_HEADS):
            ks = slice(h * dkh, (h + 1) * dkh)
            vs = slice(h * dvh, (h + 1) * dvh)
            v_h = _load_cols(p_ref, c_v + h * dvh, c_v + (h + 1) * dvh, lay)
            g_h = _load_cols(p_ref, c_g + h * dvh, c_g + (h + 1) * dvh, lay)
            da_h = da_ref[:, vs]
            o_h = o_ref[:, vs]
            og_h = og_ref[:, vs]
            r = _rstd(o_h)
            on = o_h * r
            sg = _sigmoid(g_h)
            silu = g_h * sg
            _acc_rows(dog_ref, i, jnp.sum(da_h * silu * on, axis=0, keepdims=True), vs)
            _store_cols(dp_ref, c_g + h * dvh, (da_h * (on * og_h) * (sg * (1.0 + g_h * (1.0 - sg)))).astype(BF16),
                        lay)
            don = da_h * silu * og_h
            do_h = (r * (don - on * jnp.mean(don * on, axis=-1, keepdims=True))).astype(BF16)
            s_cur = s_ref[h]
            _store_cols(dp_ref, h * dkh, (_dot_nn(do_h, s_cur.astype(BF16)) * scale).astype(BF16), lay)
            g_tot = gc_ref[h] + _dot_tn(do_h, q[:, ks].astype(BF16))
            g_bf = g_tot.astype(BF16)
            dkd_ref[:, ks] = _dot_nn(v_h.astype(BF16), g_bf)
            _store_cols(dp_ref, c_v + h * dvh, _dot_nt(kdec[:, ks].astype(BF16), g_bf).astype(BF16), lay)
            s_prev = sb_ref[0, h]
            ddecay.append(jnp.sum(g_tot * s_prev, axis=0, keepdims=True))
            gc_ref[h] = g_tot * decay[:, ks]
            s_ref[h] = s_prev
        dkdec = dkd_ref[...]
        _store_cols(dp_ref, c_k, (dkdec * e_rest).astype(BF16), lay)
        d_e = dkdec * kdec
        row = lax.broadcasted_iota(jnp.int32, (CHUNK, CHUNK), 0)
        col = lax.broadcasted_iota(jnp.int32, (CHUNK, CHUNK), 1)
        excl = (row > col).astype(BF16)
        dla = jnp.concatenate(ddecay, axis=1) * decay + _tri_matmul(excl, d_e)
        dz = dla * (1.0 / GLA_TAU) * (1.0 - _sigmoid(z))
        _acc_rows(db_ref, i, jnp.sum(dz, axis=0, keepdims=True))
        dz_bf = dz.astype(BF16)
        dw2 = _dot_tn(glr.astype(BF16), dz_bf)

        @pl.when(i == 0)
        def _():
            dw2_ref[...] = dw2

        @pl.when(i > 0)
        def _():
            dw2_ref[...] += dw2

        dp_ref[:, _gate_window(c_r, lay)] = _dot_nt(dz_bf, w2_ref[...].astype(BF16)).astype(BF16)

    rev = lambda i: (nc - 1 - i, 0)
    full = lambda *shape: pl.BlockSpec(shape, lambda i: (0,) * len(shape))
    return pl.pallas_call(
        body, name=name, grid=(nc,),
        in_specs=[pl.BlockSpec((CHUNK, dv), rev), pl.BlockSpec((CHUNK, dv), rev), pl.BlockSpec((CHUNK, wcols), rev),
                  full(LANES, dk), full(1, dk), full(1, dv),
                  pl.BlockSpec((1, GLA_HEADS, dvh, dkh), lambda i: (nc - 1 - i, 0, 0, 0)), full(GLA_HEADS, dvh, dkh)],
        out_specs=[pl.BlockSpec((CHUNK, wcols), rev), full(1, dv), full(1, dk), full(LANES, dk)],
        out_shape=[jax.ShapeDtypeStruct((t, wcols), BF16), jax.ShapeDtypeStruct((1, dv), F32),
                   jax.ShapeDtypeStruct((1, dk), F32), jax.ShapeDtypeStruct((LANES, dk), F32)],
        scratch_shapes=[pltpu.VMEM((GLA_HEADS, dvh, dkh), F32), pltpu.VMEM((GLA_HEADS, dvh, dkh), F32),
                        pltpu.VMEM((CHUNK, dk), F32)],
        compiler_params=_cparams(("arbitrary",)),
    )(da, o, proj, w2p, b_gate, o_gain, s_before, s_final)


def _sgu_mid(p_ref, lg_ref, lb_ref, ws_ref, bst_ref, w, with_grad=True):
    gd = w // SGU_GROUPS
    u_act, du_fac = _gelu_parts(p_ref[:, 0:w], with_grad)
    vf, dv_fac = _gelu_parts(p_ref[:, w:2 * w], with_grad)
    mu = jnp.mean(vf, axis=-1, keepdims=True)
    cen = vf - mu
    rstd = lax.rsqrt(jnp.mean(cen * cen, axis=-1, keepdims=True) + EPS)
    xh = cen * rstd
    vn = (xh * lg_ref[...] + lb_ref[...]).astype(BF16)
    vs = [_dot_nn(ws_ref[g].astype(BF16), vn[:, g * gd:(g + 1) * gd]) + bst_ref[:, g:g + 1]
          for g in range(SGU_GROUPS)]
    return u_act, du_fac, dv_fac, rstd, xh, vn, vs


def _sgu_fwd(proj, ln_gain, ln_bias, ws_masked, bs_t, *, name):
    t, w3 = proj.shape
    w = w3 // 3
    gd = w // SGU_GROUPS
    nb = t // SGU_BLOCK

    def body(p_ref, lg_ref, lb_ref, ws_ref, bst_ref, a_ref):
        u_act, _, _, _, _, _, vs = _sgu_mid(p_ref, lg_ref, lb_ref, ws_ref, bst_ref, w, with_grad=False)
        for g in range(SGU_GROUPS):
            cs = slice(g * gd, (g + 1) * gd)
            gate = p_ref[:, 2 * w + g * gd:2 * w + (g + 1) * gd]
            a_ref[:, cs] = (u_act[:, cs] * vs[g] * (gate * _sigmoid(gate))).astype(BF16)

    full = lambda *shape: pl.BlockSpec(shape, lambda i: (0,) * len(shape))
    return pl.pallas_call(
        body, name=name, grid=(nb,),
        in_specs=[pl.BlockSpec((SGU_BLOCK, w3), lambda i: (i, 0)), full(1, w), full(1, w),
                  full(SGU_GROUPS, SGU_BLOCK, SGU_BLOCK), full(SGU_BLOCK, SGU_GROUPS)],
        out_specs=pl.BlockSpec((SGU_BLOCK, w), lambda i: (i, 0)),
        out_shape=jax.ShapeDtypeStruct((t, w), BF16),
        compiler_params=_cparams(("parallel",)),
    )(proj, ln_gain, ln_bias, ws_masked, bs_t)


def _sgu_bwd(da, proj, ln_gain, ln_bias, ws_masked, ws_masked_t, bs_t, *, name):
    t, w3 = proj.shape
    w = w3 // 3
    gd = w // SGU_GROUPS
    nb = t // SGU_BLOCK

    def body(da_ref, p_ref, lg_ref, lb_ref, ws_ref, wst_ref, bst_ref, dp_ref, dws_ref, dbst_ref, dlg_ref, dlb_ref,
             dvn_ref):
        i = pl.program_id(0)
        u_act, du_fac, dv_fac, rstd, xh, vn, vs = _sgu_mid(p_ref, lg_ref, lb_ref, ws_ref, bst_ref, w)
        for g in range(SGU_GROUPS):
            cs = slice(g * gd, (g + 1) * gd)
            gate = p_ref[:, 2 * w + g * gd:2 * w + (g + 1) * gd]
            sg = _sigmoid(gate)
            silu = gate * sg
            da_g = da_ref[:, cs]
            ua_g = u_act[:, cs]
            dp_ref[:, cs] = (da_g * vs[g] * silu * du_fac[:, cs]).astype(BF16)
            dp_ref[:, 2 * w + g * gd:2 * w + (g + 1) * gd] = (
                da_g * ua_g * vs[g] * (sg * (1.0 + gate * (1.0 - sg)))).astype(BF16)
            dvs = da_g * ua_g * silu
            dvs_bf = dvs.astype(BF16)
            dvn_ref[:, cs] = _dot_nn(wst_ref[g].astype(BF16), dvs_bf)
            dws = _dot_nt(dvs_bf, vn[:, cs])
            dbs = jnp.sum(dvs, axis=1, keepdims=True)

            @pl.when(i == 0)
            def _():
                dws_ref[g] = dws
                dbst_ref[:, g:g + 1] = dbs

            @pl.when(i > 0)
            def _():
                dws_ref[g] += dws
                dbst_ref[:, g:g + 1] += dbs

        dvn = dvn_ref[...]
        _acc_rows(dlg_ref, i, jnp.sum(dvn * xh, axis=0, keepdims=True))
        _acc_rows(dlb_ref, i, jnp.sum(dvn, axis=0, keepdims=True))
        dxh = dvn * lg_ref[...]
        dvf = rstd * (dxh - jnp.mean(dxh, axis=-1, keepdims=True)
                      - xh * jnp.mean(dxh * xh, axis=-1, keepdims=True))
        dp_ref[:, w:2 * w] = (dvf * dv_fac).astype(BF16)

    full = lambda *shape: pl.BlockSpec(shape, lambda i: (0,) * len(shape))
    return pl.pallas_call(
        body, name=name, grid=(nb,),
        in_specs=[pl.BlockSpec((SGU_BLOCK, w), lambda i: (i, 0)), pl.BlockSpec((SGU_BLOCK, w3), lambda i: (i, 0)),
                  full(1, w), full(1, w), full(SGU_GROUPS, SGU_BLOCK, SGU_BLOCK),
                  full(SGU_GROUPS, SGU_BLOCK, SGU_BLOCK), full(SGU_BLOCK, SGU_GROUPS)],
        out_specs=[pl.BlockSpec((SGU_BLOCK, w3), lambda i: (i, 0)), full(SGU_GROUPS, SGU_BLOCK, SGU_BLOCK),
                   full(SGU_BLOCK, SGU_GROUPS), full(1, w), full(1, w)],
        out_shape=[jax.ShapeDtypeStruct((t, w3), BF16), jax.ShapeDtypeStruct((SGU_GROUPS, SGU_BLOCK, SGU_BLOCK), F32),
                   jax.ShapeDtypeStruct((SGU_BLOCK, SGU_GROUPS), F32), jax.ShapeDtypeStruct((1, w), F32),
                   jax.ShapeDtypeStruct((1, w), F32)],
        scratch_shapes=[pltpu.VMEM((SGU_BLOCK, w), F32)],
        compiler_params=_cparams(("arbitrary",)),
    )(da, proj, ln_gain, ln_bias, ws_masked, ws_masked_t, bs_t)


def _tile2d(rows, cols, block_bytes, row_unit):
    if rows % row_unit == 0:
        return _pick(rows, max(row_unit, block_bytes // (4 * cols)), row_unit), cols
    return rows, _pick(cols, max(LANES, block_bytes // (4 * rows)))


def _adamw(w, g, m, v, *, name, block_bytes=1 << 20, after=None):
    rows, cols = w.shape
    tr, tc = _tile2d(rows, cols, block_bytes, 8)
    g_rows = g.shape[0]
    assert g_rows == rows or tr == rows
    extra_specs, extra_args = ([], []) if after is None else ([pl.BlockSpec(memory_space=pl.ANY)], [after])

    def body(w_ref, g_ref, m_ref, v_ref, *rest):
        go_ref, d_ref, mo_ref, vo_ref = rest[len(extra_args):]
        gv = g_ref[0:tr, :]
        go_ref[...] = gv
        mn = ADAM_B1 * m_ref[...] + (1.0 - ADAM_B1) * gv
        vn = ADAM_B2 * v_ref[...] + (1.0 - ADAM_B2) * (gv * gv)
        m_hat = mn / (1.0 - ADAM_B1 ** ADAM_STEP)
        v_hat = vn / (1.0 - ADAM_B2 ** ADAM_STEP)
        d_ref[...] = -ADAM_LR * (m_hat / (jnp.sqrt(v_hat) + ADAM_EPS) + ADAM_WD * w_ref[...])
        mo_ref[...] = mn
        vo_ref[...] = vn

    spec = pl.BlockSpec((tr, tc), lambda i, j: (i, j))
    g_spec = spec if g_rows == rows else pl.BlockSpec((g_rows, tc), lambda i, j: (0, j))
    return pl.pallas_call(
        body, name=name, grid=(rows // tr, cols // tc), in_specs=[spec, g_spec, spec, spec] + extra_specs,
        out_specs=[spec] * 4, out_shape=[jax.ShapeDtypeStruct((rows, cols), F32)] * 4,
        compiler_params=_cparams(("parallel", "parallel")),
    )(w, g, m, v, *extra_args)


def _pair_sum_bf16(own, core_idx, peer, *, name, block_bytes=1 << 20):
    s, r, c = own.shape
    hc = c // 2
    tr, tc = _tile2d(r, hc, block_bytes, 16)
    ncb = hc // tc

    def body(h_ref, a_ref, b_ref, o_ref):
        o_ref[...] = (a_ref[...] + b_ref[...]).astype(BF16)

    grid_spec = pltpu.PrefetchScalarGridSpec(
        num_scalar_prefetch=1, grid=(s, r // tr, ncb),
        in_specs=[pl.BlockSpec((None, tr, tc), lambda j, i, k, h: (j, i, h[0] * ncb + k)),
                  pl.BlockSpec((None, tr, tc), lambda j, i, k, h: (j, i, k))],
        out_specs=pl.BlockSpec((None, tr, tc), lambda j, i, k, h: (j, i, k)))
    return pl.pallas_call(
        body, name=name, grid_spec=grid_spec, out_shape=jax.ShapeDtypeStruct((s, r, hc), BF16),
        compiler_params=_cparams(("parallel", "parallel", "parallel")),
    )(core_idx, own, peer)


def _chip_sum(pair, landed, slots, *, name, block_bytes=1 << 20):
    _, r, hc = pair.shape
    tr, tc = _tile2d(r, hc, block_bytes, 16)
    ncb = hc // tc

    def body(s_ref, own_ref, l0_ref, l1_ref, l2_ref, o_ref):
        o_ref[...] = ((own_ref[...].astype(F32) + l0_ref[...].astype(F32)) + l1_ref[...].astype(F32)
                      ) + l2_ref[...].astype(F32)

    def slab(which):
        return pl.BlockSpec((None, tr, tc), lambda i, k, s: (s[which], i, k))

    grid_spec = pltpu.PrefetchScalarGridSpec(
        num_scalar_prefetch=1, grid=(r // tr, ncb),
        in_specs=[slab(0), slab(1), slab(2), slab(3)],
        out_specs=pl.BlockSpec((tr, tc), lambda i, k, s: (i, s[4] * ncb + k)))
    return pl.pallas_call(
        body, name=name, grid_spec=grid_spec, out_shape=jax.ShapeDtypeStruct((r, 2 * hc), F32),
        compiler_params=_cparams(("parallel", "parallel")),
    )(slots, pair, landed, landed, landed)


def _stack_sum(x, *, name, out_dtype=F32, block_bytes=1 << 20):
    s, r, c = x.shape
    tr = _pick(r, max(8, block_bytes // (4 * c)), 16) if r % 16 == 0 else r

    def body(x_ref, o_ref):
        acc = x_ref[0].astype(F32)
        for j in range(1, s):
            acc = acc + x_ref[j].astype(F32)
        o_ref[...] = acc.astype(out_dtype)

    return pl.pallas_call(
        body, name=name, grid=(r // tr,),
        in_specs=[pl.BlockSpec((s, tr, c), lambda i: (0, i, 0))], out_specs=pl.BlockSpec((tr, c), lambda i: (i, 0)),
        out_shape=jax.ShapeDtypeStruct((r, c), out_dtype), compiler_params=_cparams(("parallel",)),
    )(x)


HBM = pl.BlockSpec(memory_space=pltpu.HBM)


def _place():
    x, y, c = lax.axis_index("x"), lax.axis_index("y"), lax.axis_index("c")
    other_chips = [(1 - x, y), (x, 1 - y), (1 - x, 1 - y)]
    return x, y, c, other_chips


def _half_cols(cols, which):
    hc = cols // 2
    return pl.ds(pl.multiple_of(which * hc, LANES), hc)


SEM = pl.BlockSpec(memory_space=pltpu.SEMAPHORE)
ANY = pl.BlockSpec(memory_space=pl.ANY)
SIDE_EFFECT = pltpu.SideEffectType.DATAFLOW_SIDE_EFFECTING
TOKEN_SHAPE = (8, LANES)


def _hbm(shape, dtype):
    return pltpu.HBM(shape, dtype)


def _in_hbm(a):
    return pltpu.with_memory_space_constraint(a, pltpu.HBM)


def _gather_copy(src_ref, land_ref, ssem, rsem, k, chip_of_block, to, c):
    cols = src_ref.shape[1]
    return pltpu.make_async_remote_copy(
        src_ref=src_ref.at[:, _half_cols(cols, c)], dst_ref=land_ref.at[chip_of_block, :, _half_cols(cols, c)],
        send_sem=ssem.at[k], recv_sem=rsem.at[k], device_id=to, device_id_type=MESH)


def _gather_start(shards, *, name, after=()):
    n = len(shards)
    after = list(after)

    def body(*refs):
        srcs, lands = refs[:n], refs[n:2 * n]
        outs = refs[2 * n + len(after):]
        token = outs[-1]
        x, y, c, chips = _place()
        me = 2 * x + y
        for a in range(n):
            ssem, rsem = outs[4 * a], outs[4 * a + 1]
            for k, (cx, cy) in enumerate(chips):
                _gather_copy(srcs[a], lands[a], ssem, rsem, k, me, (cx, cy, c), c).start()
        token[...] = jnp.zeros_like(token)

    out_shape, out_specs, aliases = [], [], {}
    for a, s in enumerate(shards):
        out_shape += [pltpu.SemaphoreType.DMA((3,)), pltpu.SemaphoreType.DMA((3,)), _hbm(s.shape, s.dtype),
                      _hbm((N_CHIPS,) + s.shape, s.dtype)]
        out_specs += [SEM, SEM, HBM, HBM]
        aliases[a] = 4 * a + 2
        aliases[n + a] = 4 * a + 3
    out_shape.append(jax.ShapeDtypeStruct(TOKEN_SHAPE, F32))
    out_specs.append(pl.BlockSpec(memory_space=pltpu.VMEM))
    lands = [_in_hbm(lax.empty((N_CHIPS,) + s.shape, s.dtype)) for s in shards]
    res = pl.pallas_call(
        body, name=name, in_specs=[HBM] * (2 * n) + [ANY] * len(after), out_specs=out_specs, out_shape=out_shape,
        input_output_aliases=aliases, compiler_params=pltpu.CompilerParams(has_side_effects=SIDE_EFFECT),
    )(*[_in_hbm(s) for s in shards], *lands, *after)
    return [tuple(res[4 * a:4 * a + 4]) for a in range(n)], res[-1]


def _wait_call(wait_fn, parts, after, *, name):
    ssem, rsem, src, land = parts
    after = list(after) if isinstance(after, (list, tuple)) else [after]

    def body(src_ref, land_ref, ssem_ref, rsem_ref, *rest):
        wait_fn(src_ref, land_ref, ssem_ref, rsem_ref)

    return pl.pallas_call(
        body, name=name, in_specs=[HBM, HBM, SEM, SEM] + [ANY] * len(after), out_specs=[HBM, HBM],
        out_shape=[_hbm(src.shape, src.dtype), _hbm(land.shape, land.dtype)], input_output_aliases={0: 0, 1: 1},
        compiler_params=pltpu.CompilerParams(has_side_effects=SIDE_EFFECT),
    )(src, land, ssem, rsem, *after)


ALL_CHIPS = (0, 1, 2)


def _gather_wait(parts, after, *, name, ks=ALL_CHIPS):
    def wait(src_ref, land_ref, ssem_ref, rsem_ref):
        x, y, c, chips = _place()
        for k in ks:
            cx, cy = chips[k]
            cp = _gather_copy(src_ref, land_ref, ssem_ref, rsem_ref, k, 2 * cx + cy, (x, y, c), c)
            cp.wait_send()
            cp.wait_recv()

    src, land = _wait_call(wait, parts, after, name=name)
    return (parts[0], parts[1], src, land)


def _forward_copy(buf_ref, ssem, rsem, k, slab, which, to):
    part = buf_ref.at[slab, :, _half_cols(buf_ref.shape[2], which)]
    return pltpu.make_async_remote_copy(
        src_ref=part, dst_ref=part, send_sem=ssem.at[k], recv_sem=rsem.at[k], device_id=to, device_id_type=MESH)


def _sibling_forward(land, *, name, ks=ALL_CHIPS):
    def body(_, buf, send_sems, recv_sems):
        x, y, c, chips = _place()
        copies = []
        for k in ks:
            cx, cy = chips[k]
            cp = _forward_copy(buf, send_sems, recv_sems, k, 2 * cx + cy, c, (x, y, 1 - c))
            cp.start()
            copies.append(cp)
        for k in ks:
            cx, cy = chips[k]
            _forward_copy(buf, send_sems, recv_sems, k, 2 * cx + cy, 1 - c, (x, y, c)).wait_recv()
        for cp in copies:
            cp.wait_send()

    return pl.pallas_call(
        body, name=name, in_specs=[HBM], out_specs=HBM, out_shape=jax.ShapeDtypeStruct(land.shape, land.dtype),
        input_output_aliases={0: 0},
        scratch_shapes=[pltpu.SemaphoreType.DMA((3,)), pltpu.SemaphoreType.DMA((3,))],
    )(land)


def _share_copy(buf_ref, ssem, rsem, a, which, to):
    part = buf_ref.at[:, _half_cols(buf_ref.shape[1], which)]
    return pltpu.make_async_remote_copy(
        src_ref=part, dst_ref=part, send_sem=ssem.at[a], recv_sem=rsem.at[a], device_id=to, device_id_type=MESH)


def _share_start(arrays, *, name):
    n = len(arrays)

    def body(*refs):
        bufs, ssem, rsem, token = refs[:n], refs[n], refs[n + 1], refs[-1]
        x, y, c, _ = _place()
        for a in range(n):
            _share_copy(bufs[a], ssem, rsem, a, c, (x, y, 1 - c)).start()
        token[...] = jnp.zeros_like(token)

    res = pl.pallas_call(
        body, name=name, in_specs=[HBM] * n,
        out_specs=[SEM, SEM] + [HBM] * n + [pl.BlockSpec(memory_space=pltpu.VMEM)],
        out_shape=[pltpu.SemaphoreType.DMA((n,)), pltpu.SemaphoreType.DMA((n,))]
        + [_hbm(b.shape, b.dtype) for b in arrays] + [jax.ShapeDtypeStruct(TOKEN_SHAPE, F32)],
        input_output_aliases={a: 2 + a for a in range(n)},
        compiler_params=pltpu.CompilerParams(has_side_effects=SIDE_EFFECT),
    )(*[_in_hbm(b) for b in arrays])
    return (res[0], res[1], list(res[2:2 + n])), res[-1]


def _share_wait(parts, after, *, name):
    ssem, rsem, bufs = parts
    n = len(bufs)
    after = list(after) if isinstance(after, (list, tuple)) else [after]

    def body(*refs):
        buf_refs, ssem_ref, rsem_ref = refs[:n], refs[n], refs[n + 1]
        x, y, c, _ = _place()
        for a in range(n):
            _share_copy(buf_refs[a], ssem_ref, rsem_ref, a, c, (x, y, c)).wait_send()
            _share_copy(buf_refs[a], ssem_ref, rsem_ref, a, 1 - c, (x, y, c)).wait_recv()

    return pl.pallas_call(
        body, name=name, in_specs=[HBM] * n + [SEM, SEM] + [ANY] * len(after), out_specs=[HBM] * n,
        out_shape=[_hbm(b.shape, b.dtype) for b in bufs], input_output_aliases={a: a for a in range(n)},
        compiler_params=pltpu.CompilerParams(has_side_effects=SIDE_EFFECT),
    )(*bufs, ssem, rsem, *after)


def _scatter_copy(src_ref, land_ref, ssem, rsem, k, src_slab, dst_slab, to):
    return pltpu.make_async_remote_copy(
        src_ref=src_ref.at[src_slab], dst_ref=land_ref.at[dst_slab], send_sem=ssem.at[k], recv_sem=rsem.at[k],
        device_id=to, device_id_type=MESH)


def _scatter_start(part, *, name):
    def start(src_ref, land_ref, ssem, rsem):
        x, y, c, chips = _place()
        me = 2 * x + y
        for k, (cx, cy) in enumerate(chips):
            _scatter_copy(src_ref, land_ref, ssem, rsem, k, 2 * cx + cy, me, (cx, cy, c)).start()

    return _split_start(start, part, part.shape, N_CHIPS - 1, name=name)


def _scatter_wait(parts, after, *, name):
    def wait(src_ref, land_ref, ssem_ref, rsem_ref):
        x, y, c, chips = _place()
        for k, (cx, cy) in enumerate(chips):
            idx = 2 * cx + cy
            cp = _scatter_copy(src_ref, land_ref, ssem_ref, rsem_ref, k, idx, idx, (x, y, c))
            cp.wait_send()
            cp.wait_recv()

    return _wait_call(wait, parts, after, name=name)


def _split_start(start_fn, src, land_shape, n_sems, *, name):
    def body(src_ref, land_ref, ssem, rsem, src_out, land_out, token):
        start_fn(src_ref, land_ref, ssem, rsem)
        token[...] = jnp.zeros_like(token)

    res = pl.pallas_call(
        body, name=name, in_specs=[HBM, HBM], out_specs=[SEM, SEM, HBM, HBM, pl.BlockSpec(memory_space=pltpu.VMEM)],
        out_shape=[pltpu.SemaphoreType.DMA((n_sems,)), pltpu.SemaphoreType.DMA((n_sems,)), _hbm(src.shape, src.dtype),
                   _hbm(land_shape, src.dtype), jax.ShapeDtypeStruct(TOKEN_SHAPE, F32)],
        input_output_aliases={0: 2, 1: 3}, compiler_params=pltpu.CompilerParams(has_side_effects=SIDE_EFFECT),
    )(_in_hbm(src), _in_hbm(lax.empty(land_shape, src.dtype)))
    return tuple(res[:4]), res[4]


def _swap_copy(src_ref, land_ref, ssem, rsem, which, to):
    return pltpu.make_async_remote_copy(
        src_ref=src_ref.at[:, :, _half_cols(src_ref.shape[2], which)], dst_ref=land_ref,
        send_sem=ssem.at[0], recv_sem=rsem.at[0], device_id=to, device_id_type=MESH)


def _swap_start(grad, *, name):
    def start(src_ref, land_ref, ssem, rsem):
        x, y, c, _ = _place()
        _swap_copy(src_ref, land_ref, ssem, rsem, 1 - c, (x, y, 1 - c)).start()

    s, r, cols = grad.shape
    return _split_start(start, grad, (s, r, cols // 2), 1, name=name)


def _swap_wait(parts, after, *, name):
    def wait(src_ref, land_ref, ssem_ref, rsem_ref):
        x, y, c, _ = _place()
        cp = _swap_copy(src_ref, land_ref, ssem_ref, rsem_ref, 1 - c, (x, y, c))
        cp.wait_send()
        cp.wait_recv()

    return _wait_call(wait, parts, after, name=name)


def _dev_peers(x, y, c, chips):
    return [(x, y, 1 - c)] + [(cx, cy, c) for cx, cy in chips] + [(cx, cy, 1 - c) for cx, cy in chips]


def _dev_gather_start(part, *, name):
    def start(src_ref, land_ref, ssem, rsem):
        x, y, c, chips = _place()
        for k, to in enumerate(_dev_peers(x, y, c, chips)):
            pltpu.make_async_remote_copy(
                src_ref=src_ref, dst_ref=land_ref.at[4 * x + 2 * y + c], send_sem=ssem.at[k], recv_sem=rsem.at[k],
                device_id=to, device_id_type=MESH).start()

    return _split_start(start, part, (N_DEV,) + part.shape, N_DEV - 1, name=name)


def _dev_gather_wait(parts, after, *, name):
    def wait(src_ref, land_ref, ssem_ref, rsem_ref):
        x, y, c, chips = _place()
        for k, (px, py, pc) in enumerate(_dev_peers(x, y, c, chips)):
            cp = pltpu.make_async_remote_copy(
                src_ref=src_ref, dst_ref=land_ref.at[4 * px + 2 * py + pc], send_sem=ssem_ref.at[k],
                recv_sem=rsem_ref.at[k], device_id=(x, y, c), device_id_type=MESH)
            cp.wait_send()
            cp.wait_recv()

    return _wait_call(wait, parts, after, name=name)[1]


def _sibling_share_halves(arrays, *, name):
    n = len(arrays)

    def body(*refs):
        bufs = refs[n:2 * n]
        send_sems, recv_sems = refs[2 * n:]
        x, y, c, _ = _place()
        copies = []
        for a in range(n):
            mine = bufs[a].at[:, _half_cols(bufs[a].shape[1], c)]
            cp = pltpu.make_async_remote_copy(
                src_ref=mine, dst_ref=mine, send_sem=send_sems.at[a], recv_sem=recv_sems.at[a],
                device_id=(x, y, 1 - c), device_id_type=MESH)
            cp.start()
            copies.append(cp)
        for a in range(n):
            theirs = bufs[a].at[:, _half_cols(bufs[a].shape[1], 1 - c)]
            pltpu.make_async_remote_copy(
                src_ref=theirs, dst_ref=theirs, send_sem=send_sems.at[a], recv_sem=recv_sems.at[a],
                device_id=(x, y, c), device_id_type=MESH).wait_recv()
        for cp in copies:
            cp.wait_send()

    return pl.pallas_call(
        body, name=name, in_specs=[HBM] * n, out_specs=[HBM] * n,
        out_shape=[jax.ShapeDtypeStruct(h.shape, h.dtype) for h in arrays],
        input_output_aliases={a: a for a in range(n)},
        scratch_shapes=[pltpu.SemaphoreType.DMA((n,)), pltpu.SemaphoreType.DMA((n,))],
    )(*arrays)


def _pack(arrays, rows_multiple=16, width=LANES):
    flat = jnp.concatenate([a.astype(F32).reshape(-1) for a in arrays])
    total = flat.shape[0]
    rows = -(-total // width)
    rows = -(-rows // rows_multiple) * rows_multiple
    return jnp.pad(flat, (0, rows * width - total)).reshape(rows, width)


def _unpack(buf, shapes):
    flat = buf.reshape(-1)
    out, off = [], 0
    for s in shapes:
        n = math.prod(s)
        out.append(flat[off:off + n].reshape(s))
        off += n
    return out


def kernel(x, norm_pre, norm_post, gla_w_in, gla_w_gate2, gla_b_gate, gla_o_gain, gla_w_out, sgu_w_in, sgu_ln_gain, sgu_ln_bias, sgu_w_spatial, sgu_b_spatial, sgu_w_out, loss_target, m_norm_pre, m_norm_post, m_gla_w_in, m_gla_w_gate2, m_gla_b_gate, m_gla_o_gain, m_gla_w_out, m_sgu_w_in, m_sgu_ln_gain, m_sgu_ln_bias, m_sgu_w_spatial, m_sgu_b_spatial, m_sgu_w_out, v_norm_pre, v_norm_post, v_gla_w_in, v_gla_w_gate2, v_gla_b_gate, v_gla_o_gain, v_gla_w_out, v_sgu_w_in, v_sgu_ln_gain, v_sgu_ln_bias, v_sgu_w_spatial, v_sgu_b_spatial, v_sgu_w_out):
    _, t, d = x.shape
    dk = d // 2
    ws = gla_w_in.shape[2]
    wp = -(-ws // LANES) * LANES
    lay = (ws, wp)
    chip =2 * lax.axis_index("x") + lax.axis_index("y")
    core = lax.axis_index("c")
    core_idx = core.astype(jnp.int32).reshape(1)
    others = jnp.arange(N_CHIPS - 1, dtype=jnp.int32)
    others = others + (others >= chip).astype(jnp.int32)
    slots = jnp.concatenate([chip.astype(jnp.int32).reshape(1), others, core_idx])

    x0 = x[0]
    target = loss_target[0]

    wt_in_g, mt_in_g, vt_in_g = gla_w_in[0].T, m_gla_w_in[0].T, v_gla_w_in[0].T

    small_shard = _pack([gla_w_gate2[0], sgu_ln_gain[0], sgu_ln_bias[0]], rows_multiple=8, width=2 * LANES)
    own = [small_shard, jnp.pad(wt_in_g.astype(BF16), ((0, wp - ws), (0, 0)))]
    in_flight, token = _gather_start(own, name="gather_start_a")
    own_later = [gla_w_out[0].astype(BF16), sgu_w_in[0].astype(BF16), sgu_w_out[0].astype(BF16)]
    in_flight_later, token_later = _gather_start(own_later, name="gather_start_b", after=[token])
    own, in_flight = own + own_later, in_flight + in_flight_later

    def with_own(i, land):
        return lax.dynamic_update_slice(land, own[i][None], (chip, 0, 0))

    def arrived(i, after, name):
        land = _gather_wait(in_flight[i], after, name=name + "_wait")[3]
        return with_own(i, _sibling_forward(land, name=name + "_share"))

    h0 = _norm_pre(x0, norm_pre[0:1] + token[0:1, 0:1] + token_later[0:1, 0:1], name="pre0")
    g_small = arrived(0, h0, "w_small")
    wt_g = arrived(1, [g_small, wt_in_g, mt_in_g, vt_in_g], "w_gla_in").reshape(N_CHIPS * wp, d)
    shard_shapes = [gla_w_gate2.shape[1:], sgu_ln_gain.shape[1:], sgu_ln_bias.shape[1:]]
    per_chip = [_unpack(g_small[j], shard_shapes) for j in range(N_CHIPS)]
    w2_full = jnp.concatenate([p[0] for p in per_chip], axis=1)
    ln_gain = jnp.concatenate([p[1] for p in per_chip], axis=0)[None, :]
    ln_bias = jnp.concatenate([p[2] for p in per_chip], axis=0)[None, :]
    w2p = jnp.pad(w2_full, ((0, LANES - GLA_GATE_RANK), (0, 0)))

    pos_chunk = jnp.arange(SGU_BLOCK) // CHUNK
    mask = pos_chunk[:, None] >= pos_chunk[None, :]
    ws_masked = jnp.where(mask[None], sgu_w_spatial[0], 0.0)
    ws_masked_t = ws_masked.transpose(0, 2, 1)
    bs_t = sgu_b_spatial[0].T

    proj0 = _matmul(h0, wt_g, mode="nt", out_dtype=F32, name="gla_in", tn=wp)
    o0, a0, s_before, s_final = _gla_fwd(proj0, w2p, gla_b_gate, gla_o_gain, lay, name="gla_scan")
    w_out_g = arrived(2, a0, "w_gla_out").reshape(d, d)
    y0 = _matmul(a0, w_out_g, mode="nn", out_dtype=F32, name="gla_out")
    x1, h1 = _post_then_pre(x0, y0, norm_post[0:1], norm_pre[1:2], name="post0_pre1")
    g_wi_s = arrived(3, h1, "w_sgu_in")
    proj1 = _matmul(h1, g_wi_s, mode="nn", out_dtype=F32, name="sgu_in", b_shards=True)
    a1 = _sgu_fwd(proj1, ln_gain, ln_bias, ws_masked, bs_t, name="sgu_gate")
    w_out_s = arrived(4, a1, "w_sgu_out").reshape(d, d)
    y1 = _matmul(a1, w_out_s, mode="nn", out_dtype=F32, name="sgu_out")
    loss_part, dx2, dy1, d_post1 = _loss_head(x1, y1, norm_post[1:2], target, name="loss_head")

    def behind(small, token):
        return small + token[0:1, 0:1]

    def pair_and_scatter(swap, after, name):
        grad, peer = _swap_wait(swap, after, name=name + "_swap_wait")
        pair = _pair_sum_bf16(grad, core_idx, peer, name=name + "_pair")
        return _scatter_start(pair, name=name + "_start")

    def reduced(flight, after, name):
        pair, landed = _scatter_wait(flight, after, name=name + "_wait")
        return _chip_sum(pair, landed, slots, name=name + "_sum")

    dw_out_s = _matmul(a1, dy1, mode="tn", out_dtype=F32, name="d_sgu_w_out")
    swap, tok = _swap_start(dw_out_s.reshape(N_CHIPS, d // N_CHIPS, d), name="g_sgu_out_swap")
    da1 = _matmul(dy1, w_out_s, mode="nt", out_dtype=F32, name="d_sgu_act", after=tok)
    fl_wo_s, tok = pair_and_scatter(swap, da1, "g_sgu_out")
    dproj1, d_ws, d_bs_t, d_lg, d_lb = _sgu_bwd(da1, proj1, ln_gain, behind(ln_bias, tok), ws_masked, ws_masked_t,
                                                bs_t, name="sgu_gate_bwd")
    dw_in_s = _matmul(h1, dproj1, mode="tn", out_dtype=F32, name="d_sgu_w_in", out_shards=True)
    swap, tok = _swap_start(dw_in_s, name="g_sgu_in_swap")
    dh1 = _matmul_nt_shards(dproj1, g_wi_s, out_dtype=F32, name="d_sgu_h", after=tok)
    fl_wi_s, tok = pair_and_scatter(swap, dh1, "g_sgu_in")
    dx1, dy0, d_pre1, d_post0 = _mid_bwd(dx2, dh1, x1, behind(norm_pre[1:2], tok), y0, norm_post[0:1],
                                         name="pre1_post0_bwd")
    dw_out_g = _matmul(a0, dy0, mode="tn", out_dtype=F32, name="d_gla_w_out")
    swap, tok = _swap_start(dw_out_g.reshape(N_CHIPS, d // N_CHIPS, d), name="g_gla_out_swap")
    da0 = _matmul(dy0, w_out_g, mode="nt", out_dtype=F32, name="d_gla_act", after=tok)
    fl_wo_g, tok = pair_and_scatter(swap, da0, "g_gla_out")
    dproj0, d_og, d_bg, d_w2p = _gla_bwd(da0, o0, proj0, w2p, behind(gla_b_gate, tok), gla_o_gain, s_before, s_final,
                                         lay, name="gla_scan_bwd")
    early_shapes = [norm_post.shape, gla_b_gate.shape, gla_o_gain.shape, sgu_w_spatial.shape, sgu_b_spatial.shape,
                    (1, GLA_GATE_RANK, dk), (1, d), (1, d), (1, LANES)]
    early_part = _pack([jnp.concatenate([d_post0, d_post1], axis=0), d_bg, d_og, jnp.where(mask[None], d_ws, 0.0)[None],
                        d_bs_t.T[None], d_w2p[:GLA_GATE_RANK][None], d_lg, d_lb, loss_part])
    early_flight, tok = _dev_gather_start(early_part, name="small_early_start")
    dwt_in_g = _matmul(dproj0, h0, mode="tn", out_dtype=F32, name="d_gla_w_in", tm=wp, after=tok)
    swap, tok = _swap_start(dwt_in_g.reshape(N_CHIPS, wp, d), name="g_gla_in_swap")
    r_wo_s = reduced(fl_wo_s, tok, "g_sgu_out")
    r_wi_s = reduced(fl_wi_s, r_wo_s, "g_sgu_in")
    r_wo_g = reduced(fl_wo_g, r_wi_s, "g_gla_out")
    sharing, tok = _share_start([r_wo_s, r_wi_s, r_wo_g], name="grads_share_a")
    fl_wi_g, tok = pair_and_scatter(swap, tok, "g_gla_in")
    dh0 = _matmul(dproj0, wt_g, mode="nn", out_dtype=F32, name="d_gla_h", tk=N_CHIPS * wp, after=tok)
    grad_x, d_pre0 = _first_bwd(dx1, dh0, x0, norm_pre[0:1], name="pre0_bwd")

    late_part = _pack([jnp.concatenate([d_pre0, d_pre1], axis=0)])
    late_flight, tok = _dev_gather_start(late_part, name="small_late_start")

    def big_update(w, g, m, v, name, after=None):
        return [u[None] for u in _adamw(w[0], g, m[0], v[0], name=name, after=after)]

    g_wo_sgu, g_wi_sgu, g_wo_gla = _share_wait(sharing, [grad_x, tok], name="grads_share_a_wait")
    u_wo_sgu = big_update(sgu_w_out, g_wo_sgu, m_sgu_w_out, v_sgu_w_out, "adamw_sgu_w_out")
    u_wo_gla = big_update(gla_w_out, g_wo_gla, m_gla_w_out, v_gla_w_out, "adamw_gla_w_out")

    def summed_over_devices(part, flight, after, shapes, name):
        land = _dev_gather_wait(flight, after, name=name + "_wait")
        every = lax.dynamic_update_slice(land, part[None], (2 * chip + core, 0, 0))
        return _unpack(_stack_sum(every, name=name + "_sum"), shapes)

    updated = [u_wo_gla[1], u_wo_sgu[1]]
    (g_post, g_bg, g_og, g_wsp, g_bsp, g_w2_full, g_lg_full, g_lb_full, loss_vec) = summed_over_devices(
        early_part, early_flight, updated, early_shapes, "small_early")
    g_pre, = summed_over_devices(late_part, late_flight, updated, [norm_pre.shape], "small_late")
    loss = loss_vec[0, 0]
    g_w2 = lax.dynamic_slice_in_dim(g_w2_full, chip * (dk // N_CHIPS), dk // N_CHIPS, axis=2)
    g_lg = lax.dynamic_slice_in_dim(g_lg_full, chip * (d // N_CHIPS), d // N_CHIPS, axis=1)
    g_lb = lax.dynamic_slice_in_dim(g_lb_full, chip * (d // N_CHIPS), d // N_CHIPS, axis=1)

    small_w = [norm_pre, norm_post, gla_b_gate, gla_o_gain, sgu_w_spatial, sgu_b_spatial, gla_w_gate2, sgu_ln_gain,
               sgu_ln_bias]
    small_g = [g_pre, g_post, g_bg, g_og, g_wsp, g_bsp, g_w2, g_lg, g_lb]
    small_m = [m_norm_pre, m_norm_post, m_gla_b_gate, m_gla_o_gain, m_sgu_w_spatial, m_sgu_b_spatial, m_gla_w_gate2,
               m_sgu_ln_gain, m_sgu_ln_bias]
    small_v = [v_norm_pre, v_norm_post, v_gla_b_gate, v_gla_o_gain, v_sgu_w_spatial, v_sgu_b_spatial, v_gla_w_gate2,
               v_sgu_ln_gain, v_sgu_ln_bias]
    own_shapes = [w.shape for w in small_w]
    _, s_dl, s_m, s_v = _adamw(_pack(small_w), _pack(small_g), _pack(small_m), _pack(small_v), name="adamw_small")
    dl_s, m_s, v_s = _unpack(s_dl, own_shapes), _unpack(s_m, own_shapes), _unpack(s_v, own_shapes)

    r_wi_g = reduced(fl_wi_g, s_dl, "g_gla_in")
    gt_wi_gla, = _sibling_share_halves([r_wi_g], name="grads_share_b")
    u_wi_gla_t = _adamw(wt_in_g, gt_wi_gla, mt_in_g, vt_in_g, name="adamw_gla_w_in")
    u_wi_gla = [u.T[None] for u in u_wi_gla_t]
    u_wi_sgu = big_update(sgu_w_in, g_wi_sgu, m_sgu_w_in, v_sgu_w_in, "adamw_sgu_w_in", after=u_wi_gla_t[1])

    def ordered(small, kind):
        pre, post, bg, og, wsp, bsp, w2, lg, lb = small
        return [pre, post, u_wi_gla[kind], w2, bg, og, u_wo_gla[kind], u_wi_sgu[kind], lg, lb, wsp, bsp, u_wo_sgu[kind]]

    return (loss, grad_x[None], *ordered(small_g, 0), *ordered(dl_s, 1), *ordered(m_s, 2), *ordered(v_s, 3))
```

```python
import functools
import math

import jax
import jax.numpy as jnp
from jax import lax
from jax.experimental import pallas as pl
from jax.experimental.pallas import tpu as pltpu

F32 = jnp.float32
BF16 = jnp.bfloat16
MESH = pl.DeviceIdType.MESH

EPS = 1e-6
CHUNK = 64
GLA_HEADS = 4
GLA_GATE_RANK = 16
GLA_TAU = 16.0
SGU_BLOCK = 128
SGU_GROUPS = 8
N_CHIPS = 4
N_DEV = 8
LANES = 128

ADAM_LR = 0.001
ADAM_B1 = 0.9
ADAM_B2 = 0.999
ADAM_EPS = 1e-08
ADAM_WD = 0.01
ADAM_STEP = 10

VMEM_LIMIT = 56 * 1024 * 1024


def _cparams(sem=None):
    return pltpu.CompilerParams(dimension_semantics=sem, vmem_limit_bytes=VMEM_LIMIT)


def _pick(n, cap, unit=LANES):
    best = None
    for t in range(unit, min(n, cap) + 1, unit):
        if n % t == 0:
            best = t
    assert best is not None, (n, cap, unit)
    return best


def _dot(a, b, dims):
    return lax.dot_general(a, b, (dims, ((), ())), preferred_element_type=F32)


def _dot_nn(a, b):
    return _dot(a, b, ((1,), (0,)))


def _dot_nt(a, b):
    return _dot(a, b, ((1,), (1,)))


def _dot_tn(a, b):
    return _dot(a, b, ((0,), (0,)))


def _matmul(a, b, *, mode, out_dtype, name, tm=1024, tn=512, tk=2048, b_shards=False, out_shards=False, after=None,
            out_rows=None):
    if mode == "tn":
        K, M = a.shape
    else:
        M, K = a.shape
    if b_shards:
        ns, br, bc = b.shape
        if mode == "nt":
            N, Kb = br, ns * bc
        else:
            Kb, N = br, ns * bc
    else:
        if mode == "nt":
            N, Kb = b.shape
        else:
            Kb, N = b.shape
    assert K == Kb, (a.shape, b.shape, mode)
    tm = _pick(M, tm)
    tk = _pick(K, tk)
    if b_shards and mode != "nt":
        tn = _pick(bc, tn)
    elif out_shards:
        tn = _pick(N // N_CHIPS, tn)
    else:
        tn = _pick(N, tn)
    if b_shards and mode == "nt":
        tk = _pick(bc, tk)
    nk = K // tk
    grid = (M // tm, N // tn, nk)

    if mode == "tn":
        a_spec = pl.BlockSpec((tk, tm), lambda i, j, k: (k, i))
    else:
        a_spec = pl.BlockSpec((tm, tk), lambda i, j, k: (i, k))
    if b_shards:
        if mode == "nt":
            per = bc // tk
            b_spec = pl.BlockSpec((None, tn, tk), lambda i, j, k: (k // per, j, k % per))
        else:
            per = bc // tn
            b_spec = pl.BlockSpec((None, tk, tn), lambda i, j, k: (j // per, k, j % per))
    elif mode == "nt":
        b_spec = pl.BlockSpec((tn, tk), lambda i, j, k: (j, k))
    else:
        b_spec = pl.BlockSpec((tk, tn), lambda i, j, k: (k, j))
    if out_shards:
        per_o = (N // N_CHIPS) // tn
        out_spec = pl.BlockSpec((None, tm, tn), lambda i, j, k: (j // per_o, i, j % per_o))
        out_shape = jax.ShapeDtypeStruct((N_CHIPS, M, N // N_CHIPS), out_dtype)
    else:
        out_spec = pl.BlockSpec((tm, tn), lambda i, j, k: (i, j))
        out_shape = jax.ShapeDtypeStruct((M if out_rows is None else out_rows, N), out_dtype)

    dims = {"nn": ((1,), (0,)), "nt": ((1,), (1,)), "tn": ((0,), (0,))}[mode]

    def body(a_ref, b_ref, *rest):
        o_ref, scratch = (rest[1], rest[2:]) if after is not None else (rest[0], rest[1:])
        part = _dot(a_ref[...].astype(BF16), b_ref[...].astype(BF16), dims)
        if nk == 1:
            o_ref[...] = part.astype(out_dtype)
        else:
            acc_ref, = scratch
            k = pl.program_id(2)

            @pl.when(k == 0)
            def _():
                acc_ref[...] = part

            @pl.when(k > 0)
            def _():
                acc_ref[...] += part

            @pl.when(k == nk - 1)
            def _():
                o_ref[...] = acc_ref[...].astype(out_dtype)

    extra_specs, extra_args = ([], []) if after is None else ([pl.BlockSpec(memory_space=pl.ANY)], [after])
    return pl.pallas_call(
        body, name=name, grid=grid, in_specs=[a_spec, b_spec] + extra_specs, out_specs=out_spec, out_shape=out_shape,
        scratch_shapes=[] if nk == 1 else [pltpu.VMEM((tm, tn), F32)],
        compiler_params=_cparams(("parallel", "parallel", "arbitrary")),
    )(a, b, *extra_args)


def _matmul_into_cols(a, w, which, buf, *, name, tm=1024):
    M, K = a.shape
    _, N, _ = w.shape
    tm = _pick(M, tm)

    def body(which_ref, a_ref, w_ref, buf_ref, o_ref):
        o_ref[...] = _dot_nt(a_ref[...], w_ref[...])

    grid_spec = pltpu.PrefetchScalarGridSpec(
        num_scalar_prefetch=1, grid=(M // tm,),
        in_specs=[pl.BlockSpec((tm, K), lambda i, s: (i, 0)), pl.BlockSpec((None, N, K), lambda i, s: (s[1], 0, 0)),
                  pl.BlockSpec(memory_space=pl.ANY)],
        out_specs=pl.BlockSpec((tm, N), lambda i, s: (i, s[0])))
    return pl.pallas_call(
        body, name=name, grid_spec=grid_spec, out_shape=jax.ShapeDtypeStruct(buf.shape, buf.dtype),
        input_output_aliases={3: 0}, compiler_params=_cparams(("parallel",)),
    )(which, a, w, buf)


def _matmul_nt_shards(a, b, *, out_dtype, name, tm=1024, tn=512, after=None):
    M, K = a.shape
    ns, N, kc = b.shape
    assert K == ns * kc
    tm, tn = _pick(M, tm), _pick(N, tn)

    def body(a_ref, *rest):
        b_refs, o_ref = rest[:ns], rest[ns + (after is not None)]
        acc = _dot_nt(a_ref[:, 0:kc], b_refs[0][...])
        for j in range(1, ns):
            acc += _dot_nt(a_ref[:, j * kc:(j + 1) * kc], b_refs[j][...])
        o_ref[...] = acc.astype(out_dtype)

    def shard(j):
        return pl.BlockSpec((None, tn, kc), lambda i, n: (j, n, 0))

    extra_specs, extra_args = ([], []) if after is None else ([pl.BlockSpec(memory_space=pl.ANY)], [after])
    return pl.pallas_call(
        body, name=name, grid=(M // tm, N // tn),
        in_specs=[pl.BlockSpec((tm, K), lambda i, n: (i, 0))] + [shard(j) for j in range(ns)] + extra_specs,
        out_specs=pl.BlockSpec((tm, tn), lambda i, n: (i, n)), out_shape=jax.ShapeDtypeStruct((M, N), out_dtype),
        compiler_params=_cparams(("parallel", "parallel")),
    )(a, *([b] * ns), *extra_args)


def _rstd(x):
    return lax.rsqrt(jnp.mean(x * x, axis=-1, keepdims=True) + EPS)


def _row_spec(tr, d):
    return pl.BlockSpec((tr, d), lambda i: (i, 0))


def _vec_spec(d):
    return pl.BlockSpec((1, d), lambda i: (0, 0))


def _acc_rows(ref, i, val, cols=slice(None)):
    @pl.when(i == 0)
    def _():
        ref[:, cols] = val

    @pl.when(i > 0)
    def _():
        ref[:, cols] += val


def _norm_pre(x, gain, *, name, tr=256):
    t, d = x.shape
    tr = _pick(t, tr, 8)

    def body(x_ref, g_ref, h_ref):
        xv = x_ref[...]
        h_ref[...] = (xv * _rstd(xv) * g_ref[...]).astype(BF16)

    return pl.pallas_call(
        body, name=name, grid=(t // tr,), in_specs=[_row_spec(tr, d), _vec_spec(d)], out_specs=_row_spec(tr, d),
        out_shape=jax.ShapeDtypeStruct((t, d), BF16), compiler_params=_cparams(("parallel",)),
    )(x, gain)


def _post_then_pre(x, y, post_gain, pre_gain, *, name, tr=256):
    t, d = x.shape
    tr = _pick(t, tr, 8)

    def body(x_ref, y_ref, pg_ref, ng_ref, xn_ref, h_ref):
        yv = y_ref[...]
        xn = x_ref[...] + yv * _rstd(yv) * pg_ref[...]
        xn_ref[...] = xn
        h_ref[...] = (xn * _rstd(xn) * ng_ref[...]).astype(BF16)

    return pl.pallas_call(
        body, name=name, grid=(t // tr,),
        in_specs=[_row_spec(tr, d), _row_spec(tr, d), _vec_spec(d), _vec_spec(d)],
        out_specs=[_row_spec(tr, d), _row_spec(tr, d)],
        out_shape=[jax.ShapeDtypeStruct((t, d), F32), jax.ShapeDtypeStruct((t, d), BF16)],
        compiler_params=_cparams(("parallel",)),
    )(x, y, post_gain, pre_gain)


def _norm_bwd(dy, n, r, gain):
    dn = dy * gain
    return r * (dn - n * jnp.mean(dn * n, axis=-1, keepdims=True))


def _loss_head(x, y, post_gain, target, *, name, tr=256):
    t, d = x.shape
    tr = _pick(t, tr, 8)

    def body(x_ref, y_ref, pg_ref, t_ref, loss_ref, dx_ref, dy_ref, dpg_ref):
        i = pl.program_id(0)
        yv = y_ref[...]
        r = _rstd(yv)
        n = yv * r
        err = x_ref[...] + n * pg_ref[...] - t_ref[...]
        dx = err * (1.0 / d)
        dx_ref[...] = dx
        part = 0.5 * jnp.sum(jnp.mean(err * err, axis=-1, keepdims=True), axis=0, keepdims=True)
        _acc_rows(loss_ref, i, jnp.broadcast_to(part, (1, LANES)))
        _acc_rows(dpg_ref, i, jnp.sum(dx * n, axis=0, keepdims=True))
        dy_ref[...] = _norm_bwd(dx, n, r, pg_ref[...]).astype(BF16)

    return pl.pallas_call(
        body, name=name, grid=(t // tr,),
        in_specs=[_row_spec(tr, d), _row_spec(tr, d), _vec_spec(d), _row_spec(tr, d)],
        out_specs=[_vec_spec(LANES), _row_spec(tr, d), _row_spec(tr, d), _vec_spec(d)],
        out_shape=[jax.ShapeDtypeStruct((1, LANES), F32), jax.ShapeDtypeStruct((t, d), F32),
                   jax.ShapeDtypeStruct((t, d), BF16), jax.ShapeDtypeStruct((1, d), F32)],
        compiler_params=_cparams(("arbitrary",)),
    )(x, y, post_gain, target)


def _mid_bwd(dx_out, dh, x, pre_gain, y_prev, post_gain_prev, *, name, tr=256):
    t, d = x.shape
    tr = _pick(t, tr, 8)

    def body(dxo_ref, dh_ref, x_ref, ng_ref, y_ref, pg_ref, dx_ref, dy_ref, dng_ref, dpg_ref):
        i = pl.program_id(0)
        xv = x_ref[...]
        r = _rstd(xv)
        xh = xv * r
        dhv = dh_ref[...]
        _acc_rows(dng_ref, i, jnp.sum(dhv * xh, axis=0, keepdims=True))
        dx = dxo_ref[...] + _norm_bwd(dhv, xh, r, ng_ref[...])
        dx_ref[...] = dx
        yv = y_ref[...]
        ry = _rstd(yv)
        n = yv * ry
        _acc_rows(dpg_ref, i, jnp.sum(dx * n, axis=0, keepdims=True))
        dy_ref[...] = _norm_bwd(dx, n, ry, pg_ref[...]).astype(BF16)

    return pl.pallas_call(
        body, name=name, grid=(t // tr,),
        in_specs=[_row_spec(tr, d), _row_spec(tr, d), _row_spec(tr, d), _vec_spec(d), _row_spec(tr, d), _vec_spec(d)],
        out_specs=[_row_spec(tr, d), _row_spec(tr, d), _vec_spec(d), _vec_spec(d)],
        out_shape=[jax.ShapeDtypeStruct((t, d), F32), jax.ShapeDtypeStruct((t, d), BF16),
                   jax.ShapeDtypeStruct((1, d), F32), jax.ShapeDtypeStruct((1, d), F32)],
        compiler_params=_cparams(("arbitrary",)),
    )(dx_out, dh, x, pre_gain, y_prev, post_gain_prev)


def _first_bwd(dx_out, dh, x, pre_gain, *, name, tr=256):
    t, d = x.shape
    tr = _pick(t, tr, 8)

    def body(dxo_ref, dh_ref, x_ref, ng_ref, dx_ref, dng_ref):
        i = pl.program_id(0)
        xv = x_ref[...]
        r = _rstd(xv)
        xh = xv * r
        dhv = dh_ref[...]
        _acc_rows(dng_ref, i, jnp.sum(dhv * xh, axis=0, keepdims=True))
        dx_ref[...] = dxo_ref[...] + _norm_bwd(dhv, xh, r, ng_ref[...])

    return pl.pallas_call(
        body, name=name, grid=(t // tr,),
        in_specs=[_row_spec(tr, d), _row_spec(tr, d), _row_spec(tr, d), _vec_spec(d)],
        out_specs=[_row_spec(tr, d), _vec_spec(d)],
        out_shape=[jax.ShapeDtypeStruct((t, d), F32), jax.ShapeDtypeStruct((1, d), F32)],
        compiler_params=_cparams(("arbitrary",)),
    )(dx_out, dh, x, pre_gain)


def _sigmoid(x):
    return 1.0 / (1.0 + jnp.exp(-x))


def _log_sigmoid(x):
    return jnp.minimum(x, 0.0) - jnp.log(1.0 + jnp.exp(-jnp.abs(x)))


_GELU_C = math.sqrt(2.0 / math.pi)


_GELU_A = 0.044715


def _gelu_parts(x, with_grad=True):
    x2 = x * x
    h = 0.5 * jnp.tanh(x * (_GELU_C + (_GELU_C * _GELU_A) * x2)) + 0.5
    val = x * h
    if not with_grad:
        return val, None
    return val, h * (1.0 + (1.0 - h) * (x * (2.0 * _GELU_C + (6.0 * _GELU_C * _GELU_A) * x2)))


def _split3(x):
    hi = x.astype(BF16)
    r1 = x - hi.astype(F32)
    mid = r1.astype(BF16)
    lo = (r1 - mid.astype(F32)).astype(BF16)
    return hi, mid, lo


def _tri_matmul(tri_bf16, x):
    hi, mid, lo = _split3(x)
    return _dot_nn(tri_bf16, hi) + _dot_nn(tri_bf16, mid) + _dot_nn(tri_bf16, lo)


def _gla_dims(d):
    dk, dv = d // 2, d
    return dk, dv, dk // GLA_HEADS, dv // GLA_HEADS


def _col_pieces(a, b, lay):
    ws, wp = lay
    out = []
    while a < b:
        j = a // ws
        end = min(b, (j + 1) * ws)
        out.append((j * wp + a - j * ws, end - a))
        a = end
    return out


def _load_cols(ref, a, b, lay):
    parts = [ref[:, s:s + n] for s, n in _col_pieces(a, b, lay)]
    return parts[0] if len(parts) == 1 else jnp.concatenate(parts, axis=1)


def _store_cols(ref, a, val, lay):
    off = 0
    for s, n in _col_pieces(a, a + val.shape[1], lay):
        ref[:, s:s + n] = val[:, off:off + n]
        off += n


def _gate_window(c_r, lay):
    (start, _), = _col_pieces(c_r, c_r + GLA_GATE_RANK, lay)
    assert (start % lay[1]) + LANES <= lay[1]
    return slice(start, start + LANES)


def _gla_gates(glr, k, w2_ref, b_ref):
    z = _dot_nn(glr.astype(BF16), w2_ref[...].astype(BF16)) + b_ref[...]
    la = _log_sigmoid(z) * (1.0 / GLA_TAU)
    row = lax.broadcasted_iota(jnp.int32, (CHUNK, CHUNK), 0)
    col = lax.broadcasted_iota(jnp.int32, (CHUNK, CHUNK), 1)
    incl = (row >= col).astype(BF16)
    bcum = _tri_matmul(incl, la)
    b_end = bcum[CHUNK - 1:CHUNK, :]
    e_rest = jnp.exp(b_end - bcum)
    return z, e_rest, k * e_rest, jnp.exp(b_end)


def _gla_fwd(proj, w2p, b_gate, o_gain, lay, *, name):
    t, wcols = proj.shape
    d = o_gain.shape[1]
    dk, dv, dkh, dvh = _gla_dims(d)
    nc = t // CHUNK
    c_k, c_v, c_g, c_r = dk, 2 * dk, 2 * dk + dv, 2 * dk + 2 * dv
    scale = dkh ** -0.5

    def body(p_ref, w2_ref, b_ref, og_ref, o_ref, a_ref, sb_ref, sfin_ref, s_ref):
        i = pl.program_id(0)

        @pl.when(i == 0)
        def _():
            s_ref[...] = jnp.zeros_like(s_ref)

        q = _load_cols(p_ref, 0, dk, lay) * scale
        k = _load_cols(p_ref, c_k, c_k + dk, lay)
        glr = p_ref[:, _gate_window(c_r, lay)]
        _, _, kdec, decay = _gla_gates(glr, k, w2_ref, b_ref)
        for h in range(GLA_HEADS):
            ks = slice(h * dkh, (h + 1) * dkh)
            vs = slice(h * dvh, (h + 1) * dvh)
            v_h = _load_cols(p_ref, c_v + h * dvh, c_v + (h + 1) * dvh, lay)
            g_h = _load_cols(p_ref, c_g + h * dvh, c_g + (h + 1) * dvh, lay)
            s_old = s_ref[h]
            sb_ref[0, h] = s_old
            s_new = s_old * decay[:, ks] + _dot_tn(v_h.astype(BF16), kdec[:, ks].astype(BF16))
            s_ref[h] = s_new
            o_h = _dot_nt(q[:, ks].astype(BF16), s_new.astype(BF16))
            o_ref[:, vs] = o_h
            on = o_h * _rstd(o_h)
            a_ref[:, vs] = (on * og_ref[:, vs] * (g_h * _sigmoid(g_h))).astype(BF16)

        @pl.when(i == nc - 1)
        def _():
            sfin_ref[...] = s_ref[...]

    full = lambda *shape: pl.BlockSpec(shape, lambda i: (0,) * len(shape))
    return pl.pallas_call(
        body, name=name, grid=(nc,),
        in_specs=[pl.BlockSpec((CHUNK, wcols), lambda i: (i, 0)), full(LANES, dk), full(1, dk), full(1, dv)],
        out_specs=[pl.BlockSpec((CHUNK, dv), lambda i: (i, 0)), pl.BlockSpec((CHUNK, dv), lambda i: (i, 0)),
                   pl.BlockSpec((1, GLA_HEADS, dvh, dkh), lambda i: (i, 0, 0, 0)), full(GLA_HEADS, dvh, dkh)],
        out_shape=[jax.ShapeDtypeStruct((t, dv), F32), jax.ShapeDtypeStruct((t, dv), BF16),
                   jax.ShapeDtypeStruct((nc, GLA_HEADS, dvh, dkh), F32),
                   jax.ShapeDtypeStruct((GLA_HEADS, dvh, dkh), F32)],
        scratch_shapes=[pltpu.VMEM((GLA_HEADS, dvh, dkh), F32)],
        compiler_params=_cparams(("arbitrary",)),
    )(proj, w2p, b_gate, o_gain)


def _gla_bwd(da, o, proj, w2p, b_gate, o_gain, s_before, s_final, lay, *, name):
    t, wcols = proj.shape
    d = o_gain.shape[1]
    dk, dv, dkh, dvh = _gla_dims(d)
    nc = t // CHUNK
    c_k, c_v, c_g, c_r = dk, 2 * dk, 2 * dk + dv, 2 * dk + 2 * dv
    scale = dkh ** -0.5

    def body(da_ref, o_ref, p_ref, w2_ref, b_ref, og_ref, sb_ref, sfin_ref,
             dp_ref, dog_ref, db_ref, dw2_ref, s_ref, gc_ref, dkd_ref):
        i = pl.program_id(0)

        @pl.when(i == 0)
        def _():
            s_ref[...] = sfin_ref[...]
            gc_ref[...] = jnp.zeros_like(gc_ref)

        ws, wp = lay
        for j in range(N_CHIPS):
            dp_ref[:, j * wp + ws:(j + 1) * wp] = jnp.zeros((CHUNK, wp - ws), BF16)
        q = _load_cols(p_ref, 0, dk, lay) * scale
        k = _load_cols(p_ref, c_k, c_k + dk, lay)
        glr = p_ref[:, _gate_window(c_r, lay)]
        z, e_rest, kdec, decay = _gla_gates(glr, k, w2_ref, b_ref)
        ddecay = []
        for h in range(GLA_HEADS):
            ks = slice(h * dkh, (h + 1) * dkh)
            vs = slice(h * dvh, (h + 1) * dvh)
            v_h = _load_cols(p_ref, c_v + h * dvh, c_v + (h + 1) * dvh, lay)
            g_h = _load_cols(p_ref, c_g + h * dvh, c_g + (h + 1) * dvh, lay)
            da_h = da_ref[:, vs]
            o_h = o_ref[:, vs]
            og_h = og_ref[:, vs]
            r = _rstd(o_h)
            on = o_h * r
            sg = _sigmoid(g_h)
            silu = g_h * sg
            _acc_rows(dog_ref, i, jnp.sum(da_h * silu * on, axis=0, keepdims=True), vs)
            _store_cols(dp_ref, c_g + h * dvh, (da_h * (on * og_h) * (sg * (1.0 + g_h * (1.0 - sg)))).astype(BF16),
                        lay)
            don = da_h * silu * og_h
            do_h = (r * (don - on * jnp.mean(don * on, axis=-1, keepdims=True))).astype(BF16)
            s_cur = s_ref[h]
            _store_cols(dp_ref, h * dkh, (_dot_nn(do_h, s_cur.astype(BF16)) * scale).astype(BF16), lay)
            g_tot = gc_ref[h] + _dot_tn(do_h, q[:, ks].astype(BF16))
            g_bf = g_tot.astype(BF16)
            dkd_ref[:, ks] = _dot_nn(v_h.astype(BF16), g_bf)
            _store_cols(dp_ref, c_v + h * dvh, _dot_nt(kdec[:, ks].astype(BF16), g_bf).astype(BF16), lay)
            s_prev = sb_ref[0, h]
            ddecay.append(jnp.sum(g_tot * s_prev, axis=0, keepdims=True))
            gc_ref[h] = g_tot * decay[:, ks]
            s_ref[h] = s_prev
        dkdec = dkd_ref[...]
        _store_cols(dp_ref, c_k, (dkdec * e_rest).astype(BF16), lay)
        d_e = dkdec * kdec
        row = lax.broadcasted_iota(jnp.int32, (CHUNK, CHUNK), 0)
        col = lax.broadcasted_iota(jnp.int32, (CHUNK, CHUNK), 1)
        excl = (row > col).astype(BF16)
        dla = jnp.concatenate(ddecay, axis=1) * decay + _tri_matmul(excl, d_e)
        dz = dla * (1.0 / GLA_TAU) * (1.0 - _sigmoid(z))
        _acc_rows(db_ref, i, jnp.sum(dz, axis=0, keepdims=True))
        dz_bf = dz.astype(BF16)
        dw2 = _dot_tn(glr.astype(BF16), dz_bf)

        @pl.when(i == 0)
        def _():
            dw2_ref[...] = dw2

        @pl.when(i > 0)
        def _():
            dw2_ref[...] += dw2

        dp_ref[:, _gate_window(c_r, lay)] = _dot_nt(dz_bf, w2_ref[...].astype(BF16)).astype(BF16)

    rev = lambda i: (nc - 1 - i, 0)
    full = lambda *shape: pl.BlockSpec(shape, lambda i: (0,) * len(shape))
    return pl.pallas_call(
        body, name=name, grid=(nc,),
        in_specs=[pl.BlockSpec((CHUNK, dv), rev), pl.BlockSpec((CHUNK, dv), rev), pl.BlockSpec((CHUNK, wcols), rev),
                  full(LANES, dk), full(1, dk), full(1, dv),
                  pl.BlockSpec((1, GLA_HEADS, dvh, dkh), lambda i: (nc - 1 - i, 0, 0, 0)), full(GLA_HEADS, dvh, dkh)],
        out_specs=[pl.BlockSpec((CHUNK, wcols), rev), full(1, dv), full(1, dk), full(LANES, dk)],
        out_shape=[jax.ShapeDtypeStruct((t, wcols), BF16), jax.ShapeDtypeStruct((1, dv), F32),
                   jax.ShapeDtypeStruct((1, dk), F32), jax.ShapeDtypeStruct((LANES, dk), F32)],
        scratch_shapes=[pltpu.VMEM((GLA_HEADS, dvh, dkh), F32), pltpu.VMEM((GLA_HEADS, dvh, dkh), F32),
                        pltpu.VMEM((CHUNK, dk), F32)],
        compiler_params=_cparams(("arbitrary",)),
    )(da, o, proj, w2p, b_gate, o_gain, s_before, s_final)


def _sgu_mid(p_ref, lg_ref, lb_ref, ws_ref, bst_ref, w, with_grad=True):
    gd = w // SGU_GROUPS
    u_act, du_fac = _gelu_parts(p_ref[:, 0:w], with_grad)
    vf, dv_fac = _gelu_parts(p_ref[:, w:2 * w], with_grad)
    mu = jnp.mean(vf, axis=-1, keepdims=True)
    cen = vf - mu
    rstd = lax.rsqrt(jnp.mean(cen * cen, axis=-1, keepdims=True) + EPS)
    xh = cen * rstd
    vn = (xh * lg_ref[...] + lb_ref[...]).astype(BF16)
    vs = [_dot_nn(ws_ref[g].astype(BF16), vn[:, g * gd:(g + 1) * gd]) + bst_ref[:, g:g + 1]
          for g in range(SGU_GROUPS)]
    return u_act, du_fac, dv_fac, rstd, xh, vn, vs


def _sgu_fwd(proj, ln_gain, ln_bias, ws_masked, bs_t, *, name):
    t, w3 = proj.shape
    w = w3 // 3
    gd = w // SGU_GROUPS
    nb = t // SGU_BLOCK

    def body(p_ref, lg_ref, lb_ref, ws_ref, bst_ref, a_ref):
        u_act, _, _, _, _, _, vs = _sgu_mid(p_ref, lg_ref, lb_ref, ws_ref, bst_ref, w, with_grad=False)
        for g in range(SGU_GROUPS):
            cs = slice(g * gd, (g + 1) * gd)
            gate = p_ref[:, 2 * w + g * gd:2 * w + (g + 1) * gd]
            a_ref[:, cs] = (u_act[:, cs] * vs[g] * (gate * _sigmoid(gate))).astype(BF16)

    full = lambda *shape: pl.BlockSpec(shape, lambda i: (0,) * len(shape))
    return pl.pallas_call(
        body, name=name, grid=(nb,),
        in_specs=[pl.BlockSpec((SGU_BLOCK, w3), lambda i: (i, 0)), full(1, w), full(1, w),
                  full(SGU_GROUPS, SGU_BLOCK, SGU_BLOCK), full(SGU_BLOCK, SGU_GROUPS)],
        out_specs=pl.BlockSpec((SGU_BLOCK, w), lambda i: (i, 0)),
        out_shape=jax.ShapeDtypeStruct((t, w), BF16),
        compiler_params=_cparams(("parallel",)),
    )(proj, ln_gain, ln_bias, ws_masked, bs_t)


def _sgu_bwd(da, proj, ln_gain, ln_bias, ws_masked, ws_masked_t, bs_t, *, name):
    t, w3 = proj.shape
    w = w3 // 3
    gd = w // SGU_GROUPS
    nb = t // SGU_BLOCK

    def body(da_ref, p_ref, lg_ref, lb_ref, ws_ref, wst_ref, bst_ref, dp_ref, dws_ref, dbst_ref, dlg_ref, dlb_ref,
             dvn_ref):
        i = pl.program_id(0)
        u_act, du_fac, dv_fac, rstd, xh, vn, vs = _sgu_mid(p_ref, lg_ref, lb_ref, ws_ref, bst_ref, w)
        for g in range(SGU_GROUPS):
            cs = slice(g * gd, (g + 1) * gd)
            gate = p_ref[:, 2 * w + g * gd:2 * w + (g + 1) * gd]
            sg = _sigmoid(gate)
            silu = gate * sg
            da_g = da_ref[:, cs]
            ua_g = u_act[:, cs]
            dp_ref[:, cs] = (da_g * vs[g] * silu * du_fac[:, cs]).astype(BF16)
            dp_ref[:, 2 * w + g * gd:2 * w + (g + 1) * gd] = (
                da_g * ua_g * vs[g] * (sg * (1.0 + gate * (1.0 - sg)))).astype(BF16)
            dvs = da_g * ua_g * silu
            dvs_bf = dvs.astype(BF16)
            dvn_ref[:, cs] = _dot_nn(wst_ref[g].astype(BF16), dvs_bf)
            dws = _dot_nt(dvs_bf, vn[:, cs])
            dbs = jnp.sum(dvs, axis=1, keepdims=True)

            @pl.when(i == 0)
            def _():
                dws_ref[g] = dws
                dbst_ref[:, g:g + 1] = dbs

            @pl.when(i > 0)
            def _():
                dws_ref[g] += dws
                dbst_ref[:, g:g + 1] += dbs

        dvn = dvn_ref[...]
        _acc_rows(dlg_ref, i, jnp.sum(dvn * xh, axis=0, keepdims=True))
        _acc_rows(dlb_ref, i, jnp.sum(dvn, axis=0, keepdims=True))
        dxh = dvn * lg_ref[...]
        dvf = rstd * (dxh - jnp.mean(dxh, axis=-1, keepdims=True)
                      - xh * jnp.mean(dxh * xh, axis=-1, keepdims=True))
        dp_ref[:, w:2 * w] = (dvf * dv_fac).astype(BF16)

    full = lambda *shape: pl.BlockSpec(shape, lambda i: (0,) * len(shape))
    return pl.pallas_call(
        body, name=name, grid=(nb,),
        in_specs=[pl.BlockSpec((SGU_BLOCK, w), lambda i: (i, 0)), pl.BlockSpec((SGU_BLOCK, w3), lambda i: (i, 0)),
                  full(1, w), full(1, w), full(SGU_GROUPS, SGU_BLOCK, SGU_BLOCK),
                  full(SGU_GROUPS, SGU_BLOCK, SGU_BLOCK), full(SGU_BLOCK, SGU_GROUPS)],
        out_specs=[pl.BlockSpec((SGU_BLOCK, w3), lambda i: (i, 0)), full(SGU_GROUPS, SGU_BLOCK, SGU_BLOCK),
                   full(SGU_BLOCK, SGU_GROUPS), full(1, w), full(1, w)],
        out_shape=[jax.ShapeDtypeStruct((t, w3), BF16), jax.ShapeDtypeStruct((SGU_GROUPS, SGU_BLOCK, SGU_BLOCK), F32),
                   jax.ShapeDtypeStruct((SGU_BLOCK, SGU_GROUPS), F32), jax.ShapeDtypeStruct((1, w), F32),
                   jax.ShapeDtypeStruct((1, w), F32)],
        scratch_shapes=[pltpu.VMEM((SGU_BLOCK, w), F32)],
        compiler_params=_cparams(("arbitrary",)),
    )(da, proj, ln_gain, ln_bias, ws_masked, ws_masked_t, bs_t)


def _tile2d(rows, cols, block_bytes, row_unit):
    if rows % row_unit == 0:
        return _pick(rows, max(row_unit, block_bytes // (4 * cols)), row_unit), cols
    return rows, _pick(cols, max(LANES, block_bytes // (4 * rows)))


def _adamw(w, g, m, v, *, name, block_bytes=1 << 20, after=None):
    rows, cols = w.shape
    tr, tc = _tile2d(rows, cols, block_bytes, 8)
    g_rows = g.shape[0]
    assert g_rows == rows or tr == rows
    extra_specs, extra_args = ([], []) if after is None else ([pl.BlockSpec(memory_space=pl.ANY)], [after])

    def body(w_ref, g_ref, m_ref, v_ref, *rest):
        go_ref, d_ref, mo_ref, vo_ref = rest[len(extra_args):]
        gv = g_ref[0:tr, :]
        go_ref[...] = gv
        mn = ADAM_B1 * m_ref[...] + (1.0 - ADAM_B1) * gv
        vn = ADAM_B2 * v_ref[...] + (1.0 - ADAM_B2) * (gv * gv)
        m_hat = mn / (1.0 - ADAM_B1 ** ADAM_STEP)
        v_hat = vn / (1.0 - ADAM_B2 ** ADAM_STEP)
        d_ref[...] = -ADAM_LR * (m_hat / (jnp.sqrt(v_hat) + ADAM_EPS) + ADAM_WD * w_ref[...])
        mo_ref[...] = mn
        vo_ref[...] = vn

    spec = pl.BlockSpec((tr, tc), lambda i, j: (i, j))
    g_spec = spec if g_rows == rows else pl.BlockSpec((g_rows, tc), lambda i, j: (0, j))
    return pl.pallas_call(
        body, name=name, grid=(rows // tr, cols // tc), in_specs=[spec, g_spec, spec, spec] + extra_specs,
        out_specs=[spec] * 4, out_shape=[jax.ShapeDtypeStruct((rows, cols), F32)] * 4,
        compiler_params=_cparams(("parallel", "parallel")),
    )(w, g, m, v, *extra_args)


def _matmul_dw_pair(a_me, a_sib, b_me, b_sib, core_idx, *, shards_on, name, after=None):
    T, M = a_me.shape
    N = b_me.shape[1]
    if shards_on == "rows":
        tm, hc = M // N_CHIPS, N // 2
        tn = _pick(hc, 512)
        per = hc // tn
        grid = (N_CHIPS, per)
        a_spec = pl.BlockSpec((T, tm), lambda i, n, h: (0, i))
        b_me_spec = pl.BlockSpec((T, tn), lambda i, n, h: (0, h[0] * per + n))
        b_sib_spec = pl.BlockSpec((T, tn), lambda i, n, h: (0, n))
        out_spec = pl.BlockSpec((None, tm, tn), lambda i, n, h: (i, 0, n))
        out_shape = jax.ShapeDtypeStruct((N_CHIPS, tm, hc), BF16)
    else:
        tm, hc = _pick(M, 1024), N // N_CHIPS // 2
        grid = (M // tm, N_CHIPS)
        a_spec = pl.BlockSpec((T, tm), lambda i, j, h: (0, i))
        b_me_spec = pl.BlockSpec((T, hc), lambda i, j, h: (0, 2 * j + h[0]))
        b_sib_spec = pl.BlockSpec((T, hc), lambda i, j, h: (0, j))
        out_spec = pl.BlockSpec((None, tm, hc), lambda i, j, h: (j, i, 0))
        out_shape = jax.ShapeDtypeStruct((N_CHIPS, M, hc), BF16)
    extra_specs, extra_args = ([], []) if after is None else ([pl.BlockSpec(memory_space=pl.ANY)], [after])

    def body(h_ref, am_ref, as_ref, bm_ref, bs_ref, *rest):
        o_ref = rest[len(extra_args)]
        o_ref[...] = (_dot_tn(am_ref[...], bm_ref[...]) + _dot_tn(as_ref[...], bs_ref[...])).astype(BF16)

    grid_spec = pltpu.PrefetchScalarGridSpec(
        num_scalar_prefetch=1, grid=grid, in_specs=[a_spec, a_spec, b_me_spec, b_sib_spec] + extra_specs,
        out_specs=out_spec)
    return pl.pallas_call(
        body, name=name, grid_spec=grid_spec, out_shape=out_shape, compiler_params=_cparams(("parallel", "parallel")),
    )(core_idx, a_me, a_sib, b_me, b_sib, *extra_args)


def _chip_sum(pair, landed, slots, *, name, block_bytes=1 << 20):
    _, r, hc = pair.shape
    tr, tc = _tile2d(r, hc, block_bytes, 16)
    ncb = hc // tc

    def body(s_ref, own_ref, l0_ref, l1_ref, l2_ref, o_ref):
        o_ref[...] = ((own_ref[...].astype(F32) + l0_ref[...].astype(F32)) + l1_ref[...].astype(F32)
                      ) + l2_ref[...].astype(F32)

    def slab(which):
        return pl.BlockSpec((None, tr, tc), lambda i, k, s: (s[which], i, k))

    grid_spec = pltpu.PrefetchScalarGridSpec(
        num_scalar_prefetch=1, grid=(r // tr, ncb),
        in_specs=[slab(0), slab(1), slab(2), slab(3)],
        out_specs=pl.BlockSpec((tr, tc), lambda i, k, s: (i, s[4] * ncb + k)))
    return pl.pallas_call(
        body, name=name, grid_spec=grid_spec, out_shape=jax.ShapeDtypeStruct((r, 2 * hc), F32),
        compiler_params=_cparams(("parallel", "parallel")),
    )(slots, pair, landed, landed, landed)


def _stack_sum(x, *, name, out_dtype=F32, block_bytes=1 << 20):
    s, r, c = x.shape
    tr = _pick(r, max(8, block_bytes // (4 * c)), 16) if r % 16 == 0 else r

    def body(x_ref, o_ref):
        acc = x_ref[0].astype(F32)
        for j in range(1, s):
            acc = acc + x_ref[j].astype(F32)
        o_ref[...] = acc.astype(out_dtype)

    return pl.pallas_call(
        body, name=name, grid=(r // tr,),
        in_specs=[pl.BlockSpec((s, tr, c), lambda i: (0, i, 0))], out_specs=pl.BlockSpec((tr, c), lambda i: (i, 0)),
        out_shape=jax.ShapeDtypeStruct((r, c), out_dtype), compiler_params=_cparams(("parallel",)),
    )(x)


HBM = pl.BlockSpec(memory_space=pltpu.HBM)


def _place():
    x, y, c = lax.axis_index("x"), lax.axis_index("y"), lax.axis_index("c")
    other_chips = [(1 - x, y), (x, 1 - y), (1 - x, 1 - y)]
    return x, y, c, other_chips


def _half_cols(cols, which):
    hc = cols // 2
    return pl.ds(pl.multiple_of(which * hc, LANES), hc)


SEM = pl.BlockSpec(memory_space=pltpu.SEMAPHORE)
ANY = pl.BlockSpec(memory_space=pl.ANY)
SIDE_EFFECT = pltpu.SideEffectType.DATAFLOW_SIDE_EFFECTING
TOKEN_SHAPE = (8, LANES)


def _hbm(shape, dtype):
    return pltpu.HBM(shape, dtype)


def _in_hbm(a):
    return pltpu.with_memory_space_constraint(a, pltpu.HBM)


def _gather_copy(src_ref, land_ref, ssem, rsem, k, chip_of_block, to, c):
    cols = src_ref.shape[1]
    return pltpu.make_async_remote_copy(
        src_ref=src_ref.at[:, _half_cols(cols, c)], dst_ref=land_ref.at[chip_of_block, :, _half_cols(cols, c)],
        send_sem=ssem.at[k], recv_sem=rsem.at[k], device_id=to, device_id_type=MESH)


def _gather_start(shards, *, name, after=()):
    n = len(shards)
    after = list(after)

    def body(*refs):
        srcs, lands = refs[:n], refs[n:2 * n]
        outs = refs[2 * n + len(after):]
        token = outs[-1]
        x, y, c, chips = _place()
        me = 2 * x + y
        for a in range(n):
            ssem, rsem = outs[4 * a], outs[4 * a + 1]
            for k, (cx, cy) in enumerate(chips):
                _gather_copy(srcs[a], lands[a], ssem, rsem, k, me, (cx, cy, c), c).start()
        token[...] = jnp.zeros_like(token)

    out_shape, out_specs, aliases = [], [], {}
    for a, s in enumerate(shards):
        out_shape += [pltpu.SemaphoreType.DMA((3,)), pltpu.SemaphoreType.DMA((3,)), _hbm(s.shape, s.dtype),
                      _hbm((N_CHIPS,) + s.shape, s.dtype)]
        out_specs += [SEM, SEM, HBM, HBM]
        aliases[a] = 4 * a + 2
        aliases[n + a] = 4 * a + 3
    out_shape.append(jax.ShapeDtypeStruct(TOKEN_SHAPE, F32))
    out_specs.append(pl.BlockSpec(memory_space=pltpu.VMEM))
    lands = [_in_hbm(lax.empty((N_CHIPS,) + s.shape, s.dtype)) for s in shards]
    res = pl.pallas_call(
        body, name=name, in_specs=[HBM] * (2 * n) + [ANY] * len(after), out_specs=out_specs, out_shape=out_shape,
        input_output_aliases=aliases, compiler_params=pltpu.CompilerParams(has_side_effects=SIDE_EFFECT),
    )(*[_in_hbm(s) for s in shards], *lands, *after)
    return [tuple(res[4 * a:4 * a + 4]) for a in range(n)], res[-1]


def _wait_call(wait_fn, parts, after, *, name):
    ssem, rsem, src, land = parts
    after = list(after) if isinstance(after, (list, tuple)) else [after]

    def body(src_ref, land_ref, ssem_ref, rsem_ref, *rest):
        wait_fn(src_ref, land_ref, ssem_ref, rsem_ref)

    return pl.pallas_call(
        body, name=name, in_specs=[HBM, HBM, SEM, SEM] + [ANY] * len(after), out_specs=[HBM, HBM],
        out_shape=[_hbm(src.shape, src.dtype), _hbm(land.shape, land.dtype)], input_output_aliases={0: 0, 1: 1},
        compiler_params=pltpu.CompilerParams(has_side_effects=SIDE_EFFECT),
    )(src, land, ssem, rsem, *after)


ALL_CHIPS = (0, 1, 2)


def _gather_wait(parts, after, *, name, ks=ALL_CHIPS):
    def wait(src_ref, land_ref, ssem_ref, rsem_ref):
        x, y, c, chips = _place()
        for k in ks:
            cx, cy = chips[k]
            cp = _gather_copy(src_ref, land_ref, ssem_ref, rsem_ref, k, 2 * cx + cy, (x, y, c), c)
            cp.wait_send()
            cp.wait_recv()

    src, land = _wait_call(wait, parts, after, name=name)
    return (parts[0], parts[1], src, land)


def _forward_copy(buf_ref, ssem, rsem, k, slab, which, to):
    part = buf_ref.at[slab, :, _half_cols(buf_ref.shape[2], which)]
    return pltpu.make_async_remote_copy(
        src_ref=part, dst_ref=part, send_sem=ssem.at[k], recv_sem=rsem.at[k], device_id=to, device_id_type=MESH)


def _sibling_forward(land, *, name, ks=ALL_CHIPS):
    def body(_, buf, send_sems, recv_sems):
        x, y, c, chips = _place()
        copies = []
        for k in ks:
            cx, cy = chips[k]
            cp = _forward_copy(buf, send_sems, recv_sems, k, 2 * cx + cy, c, (x, y, 1 - c))
            cp.start()
            copies.append(cp)
        for k in ks:
            cx, cy = chips[k]
            _forward_copy(buf, send_sems, recv_sems, k, 2 * cx + cy, 1 - c, (x, y, c)).wait_recv()
        for cp in copies:
            cp.wait_send()

    return pl.pallas_call(
        body, name=name, in_specs=[HBM], out_specs=HBM, out_shape=jax.ShapeDtypeStruct(land.shape, land.dtype),
        input_output_aliases={0: 0},
        scratch_shapes=[pltpu.SemaphoreType.DMA((3,)), pltpu.SemaphoreType.DMA((3,))],
    )(land)


def _share_copy(buf_ref, ssem, rsem, a, which, to):
    part = buf_ref.at[:, _half_cols(buf_ref.shape[1], which)]
    return pltpu.make_async_remote_copy(
        src_ref=part, dst_ref=part, send_sem=ssem.at[a], recv_sem=rsem.at[a], device_id=to, device_id_type=MESH)


def _share_start(arrays, *, name):
    n = len(arrays)

    def body(*refs):
        bufs, ssem, rsem, token = refs[:n], refs[n], refs[n + 1], refs[-1]
        x, y, c, _ = _place()
        for a in range(n):
            _share_copy(bufs[a], ssem, rsem, a, c, (x, y, 1 - c)).start()
        token[...] = jnp.zeros_like(token)

    res = pl.pallas_call(
        body, name=name, in_specs=[HBM] * n,
        out_specs=[SEM, SEM] + [HBM] * n + [pl.BlockSpec(memory_space=pltpu.VMEM)],
        out_shape=[pltpu.SemaphoreType.DMA((n,)), pltpu.SemaphoreType.DMA((n,))]
        + [_hbm(b.shape, b.dtype) for b in arrays] + [jax.ShapeDtypeStruct(TOKEN_SHAPE, F32)],
        input_output_aliases={a: 2 + a for a in range(n)},
        compiler_params=pltpu.CompilerParams(has_side_effects=SIDE_EFFECT),
    )(*[_in_hbm(b) for b in arrays])
    return (res[0], res[1], list(res[2:2 + n])), res[-1]


def _share_wait(parts, after, *, name):
    ssem, rsem, bufs = parts
    n = len(bufs)
    after = list(after) if isinstance(after, (list, tuple)) else [after]

    def body(*refs):
        buf_refs, ssem_ref, rsem_ref = refs[:n], refs[n], refs[n + 1]
        x, y, c, _ = _place()
        for a in range(n):
            _share_copy(buf_refs[a], ssem_ref, rsem_ref, a, c, (x, y, c)).wait_send()
            _share_copy(buf_refs[a], ssem_ref, rsem_ref, a, 1 - c, (x, y, c)).wait_recv()

    return pl.pallas_call(
        body, name=name, in_specs=[HBM] * n + [SEM, SEM] + [ANY] * len(after), out_specs=[HBM] * n,
        out_shape=[_hbm(b.shape, b.dtype) for b in bufs], input_output_aliases={a: a for a in range(n)},
        compiler_params=pltpu.CompilerParams(has_side_effects=SIDE_EFFECT),
    )(*bufs, ssem, rsem, *after)


def _scatter_copy(src_ref, land_ref, ssem, rsem, k, src_slab, dst_slab, to):
    return pltpu.make_async_remote_copy(
        src_ref=src_ref.at[src_slab], dst_ref=land_ref.at[dst_slab], send_sem=ssem.at[k], recv_sem=rsem.at[k],
        device_id=to, device_id_type=MESH)


def _scatter_start(part, *, name):
    def start(src_ref, land_ref, ssem, rsem):
        x, y, c, chips = _place()
        me = 2 * x + y
        for k, (cx, cy) in enumerate(chips):
            _scatter_copy(src_ref, land_ref, ssem, rsem, k, 2 * cx + cy, me, (cx, cy, c)).start()

    return _split_start(start, part, part.shape, N_CHIPS - 1, name=name)


def _scatter_wait(parts, after, *, name):
    def wait(src_ref, land_ref, ssem_ref, rsem_ref):
        x, y, c, chips = _place()
        for k, (cx, cy) in enumerate(chips):
            idx = 2 * cx + cy
            cp = _scatter_copy(src_ref, land_ref, ssem_ref, rsem_ref, k, idx, idx, (x, y, c))
            cp.wait_send()
            cp.wait_recv()

    return _wait_call(wait, parts, after, name=name)


def _split_start(start_fn, src, land_shape, n_sems, *, name):
    def body(src_ref, land_ref, ssem, rsem, src_out, land_out, token):
        start_fn(src_ref, land_ref, ssem, rsem)
        token[...] = jnp.zeros_like(token)

    res = pl.pallas_call(
        body, name=name, in_specs=[HBM, HBM], out_specs=[SEM, SEM, HBM, HBM, pl.BlockSpec(memory_space=pltpu.VMEM)],
        out_shape=[pltpu.SemaphoreType.DMA((n_sems,)), pltpu.SemaphoreType.DMA((n_sems,)), _hbm(src.shape, src.dtype),
                   _hbm(land_shape, src.dtype), jax.ShapeDtypeStruct(TOKEN_SHAPE, F32)],
        input_output_aliases={0: 2, 1: 3}, compiler_params=pltpu.CompilerParams(has_side_effects=SIDE_EFFECT),
    )(_in_hbm(src), _in_hbm(lax.empty(land_shape, src.dtype)))
    return tuple(res[:4]), res[4]


def _sibling_copy(src_ref, land_ref, ssem, rsem, k, half, which, to):
    if half:
        src_ref = src_ref.at[(slice(None),) * (len(src_ref.shape) - 1) + (_half_cols(src_ref.shape[-1], which),)]
    return pltpu.make_async_remote_copy(
        src_ref=src_ref, dst_ref=land_ref, send_sem=ssem.at[k], recv_sem=rsem.at[k], device_id=to, device_id_type=MESH)


def _to_sibling_start(items, *, name):
    n = len(items)
    shapes = [a.shape[:-1] + (a.shape[-1] // 2,) if half else a.shape for a, half in items]

    def body(*refs):
        srcs, lands, ssem, rsem, token = refs[:n], refs[n:2 * n], refs[2 * n], refs[2 * n + 1], refs[-1]
        x, y, c, _ = _place()
        for k, (_, half) in enumerate(items):
            _sibling_copy(srcs[k], lands[k], ssem, rsem, k, half, 1 - c, (x, y, 1 - c)).start()
        token[...] = jnp.zeros_like(token)

    res = pl.pallas_call(
        body, name=name, in_specs=[HBM] * (2 * n),
        out_specs=[SEM, SEM] + [HBM] * (2 * n) + [pl.BlockSpec(memory_space=pltpu.VMEM)],
        out_shape=[pltpu.SemaphoreType.DMA((n,)), pltpu.SemaphoreType.DMA((n,))]
        + [_hbm(a.shape, a.dtype) for a, _ in items] + [_hbm(s, a.dtype) for s, (a, _) in zip(shapes, items)]
        + [jax.ShapeDtypeStruct(TOKEN_SHAPE, F32)],
        input_output_aliases={k: 2 + k for k in range(2 * n)},
        compiler_params=pltpu.CompilerParams(has_side_effects=SIDE_EFFECT),
    )(*[_in_hbm(a) for a, _ in items], *[_in_hbm(lax.empty(s, a.dtype)) for s, (a, _) in zip(shapes, items)])
    return [(res[0], res[1], k, half, res[2 + k], res[2 + n + k]) for k, (_, half) in enumerate(items)], res[-1]


def _from_sibling(flight, after, *, name):
    ssem, rsem, k, half, src, land = flight

    def wait(src_ref, land_ref, ssem_ref, rsem_ref):
        x, y, c, _ = _place()
        cp = _sibling_copy(src_ref, land_ref, ssem_ref, rsem_ref, k, half, 1 - c, (x, y, c))
        cp.wait_send()
        cp.wait_recv()

    return _wait_call(wait, (ssem, rsem, src, land), after, name=name)


def _dev_peers(x, y, c, chips):
    return [(x, y, 1 - c)] + [(cx, cy, c) for cx, cy in chips] + [(cx, cy, 1 - c) for cx, cy in chips]


def _dev_gather_start(part, *, name):
    def start(src_ref, land_ref, ssem, rsem):
        x, y, c, chips = _place()
        for k, to in enumerate(_dev_peers(x, y, c, chips)):
            pltpu.make_async_remote_copy(
                src_ref=src_ref, dst_ref=land_ref.at[4 * x + 2 * y + c], send_sem=ssem.at[k], recv_sem=rsem.at[k],
                device_id=to, device_id_type=MESH).start()

    return _split_start(start, part, (N_DEV,) + part.shape, N_DEV - 1, name=name)


def _dev_gather_wait(parts, after, *, name):
    def wait(src_ref, land_ref, ssem_ref, rsem_ref):
        x, y, c, chips = _place()
        for k, (px, py, pc) in enumerate(_dev_peers(x, y, c, chips)):
            cp = pltpu.make_async_remote_copy(
                src_ref=src_ref, dst_ref=land_ref.at[4 * px + 2 * py + pc], send_sem=ssem_ref.at[k],
                recv_sem=rsem_ref.at[k], device_id=(x, y, c), device_id_type=MESH)
            cp.wait_send()
            cp.wait_recv()

    return _wait_call(wait, parts, after, name=name)[1]


def _sibling_share_halves(arrays, *, name):
    n = len(arrays)

    def body(*refs):
        bufs = refs[n:2 * n]
        send_sems, recv_sems = refs[2 * n:]
        x, y, c, _ = _place()
        copies = []
        for a in range(n):
            mine = bufs[a].at[:, _half_cols(bufs[a].shape[1], c)]
            cp = pltpu.make_async_remote_copy(
                src_ref=mine, dst_ref=mine, send_sem=send_sems.at[a], recv_sem=recv_sems.at[a],
                device_id=(x, y, 1 - c), device_id_type=MESH)
            cp.start()
            copies.append(cp)
        for a in range(n):
            theirs = bufs[a].at[:, _half_cols(bufs[a].shape[1], 1 - c)]
            pltpu.make_async_remote_copy(
                src_ref=theirs, dst_ref=theirs, send_sem=send_sems.at[a], recv_sem=recv_sems.at[a],
                device_id=(x, y, c), device_id_type=MESH).wait_recv()
        for cp in copies:
            cp.wait_send()

    return pl.pallas_call(
        body, name=name, in_specs=[HBM] * n, out_specs=[HBM] * n,
        out_shape=[jax.ShapeDtypeStruct(h.shape, h.dtype) for h in arrays],
        input_output_aliases={a: a for a in range(n)},
        scratch_shapes=[pltpu.SemaphoreType.DMA((n,)), pltpu.SemaphoreType.DMA((n,))],
    )(*arrays)


def _pack(arrays, rows_multiple=16, width=LANES):
    flat = jnp.concatenate([a.astype(F32).reshape(-1) for a in arrays])
    total = flat.shape[0]
    rows = -(-total // width)
    rows = -(-rows // rows_multiple) * rows_multiple
    return jnp.pad(flat, (0, rows * width - total)).reshape(rows, width)


def _unpack(buf, shapes):
    flat = buf.reshape(-1)
    out, off = [], 0
    for s in shapes:
        n = math.prod(s)
        out.append(flat[off:off + n].reshape(s))
        off += n
    return out


def kernel(x, norm_pre, norm_post, gla_w_in, gla_w_gate2, gla_b_gate, gla_o_gain, gla_w_out, sgu_w_in, sgu_ln_gain, sgu_ln_bias, sgu_w_spatial, sgu_b_spatial, sgu_w_out, loss_target, m_norm_pre, m_norm_post, m_gla_w_in, m_gla_w_gate2, m_gla_b_gate, m_gla_o_gain, m_gla_w_out, m_sgu_w_in, m_sgu_ln_gain, m_sgu_ln_bias, m_sgu_w_spatial, m_sgu_b_spatial, m_sgu_w_out, v_norm_pre, v_norm_post, v_gla_w_in, v_gla_w_gate2, v_gla_b_gate, v_gla_o_gain, v_gla_w_out, v_sgu_w_in, v_sgu_ln_gain, v_sgu_ln_bias, v_sgu_w_spatial, v_sgu_b_spatial, v_sgu_w_out):
    _, t, d = x.shape
    dk = d // 2
    ws = gla_w_in.shape[2]
    wp = -(-ws // LANES) * LANES
    lay = (ws, wp)
    chip =2 * lax.axis_index("x") + lax.axis_index("y")
    core = lax.axis_index("c")
    core_idx = core.astype(jnp.int32).reshape(1)
    others = jnp.arange(N_CHIPS - 1, dtype=jnp.int32)
    others = others + (others >= chip).astype(jnp.int32)
    slots = jnp.concatenate([chip.astype(jnp.int32).reshape(1), others, core_idx])

    x0 = x[0]
    target = loss_target[0]

    wt_in_g, mt_in_g, vt_in_g = gla_w_in[0].T, m_gla_w_in[0].T, v_gla_w_in[0].T

    small_shard = _pack([gla_w_gate2[0], sgu_ln_gain[0], sgu_ln_bias[0]], rows_multiple=8, width=2 * LANES)
    own = [small_shard, jnp.pad(wt_in_g.astype(BF16), ((0, wp - ws), (0, 0)))]
    in_flight, token = _gather_start(own, name="gather_start_a")
    own_later = [gla_w_out[0].astype(BF16), sgu_w_in[0].astype(BF16), sgu_w_out[0].astype(BF16)]
    in_flight_later, token_later = _gather_start(own_later, name="gather_start_b", after=[token])
    own, in_flight = own + own_later, in_flight + in_flight_later

    def with_own(i, land):
        return lax.dynamic_update_slice(land, own[i][None], (chip, 0, 0))

    def arrived(i, after, name):
        land = _gather_wait(in_flight[i], after, name=name + "_wait")[3]
        return with_own(i, _sibling_forward(land, name=name + "_share"))

    h0 = _norm_pre(x0, norm_pre[0:1] + token[0:1, 0:1] + token_later[0:1, 0:1], name="pre0")
    g_small = arrived(0, h0, "w_small")
    wt_g = arrived(1, [g_small, wt_in_g, mt_in_g, vt_in_g], "w_gla_in").reshape(N_CHIPS * wp, d)
    shard_shapes = [gla_w_gate2.shape[1:], sgu_ln_gain.shape[1:], sgu_ln_bias.shape[1:]]
    per_chip = [_unpack(g_small[j], shard_shapes) for j in range(N_CHIPS)]
    w2_full = jnp.concatenate([p[0] for p in per_chip], axis=1)
    ln_gain = jnp.concatenate([p[1] for p in per_chip], axis=0)[None, :]
    ln_bias = jnp.concatenate([p[2] for p in per_chip], axis=0)[None, :]
    w2p = jnp.pad(w2_full, ((0, LANES - GLA_GATE_RANK), (0, 0)))

    pos_chunk = jnp.arange(SGU_BLOCK) // CHUNK
    mask = pos_chunk[:, None] >= pos_chunk[None, :]
    ws_masked = jnp.where(mask[None], sgu_w_spatial[0], 0.0)
    ws_masked_t = ws_masked.transpose(0, 2, 1)
    bs_t = sgu_b_spatial[0].T

    proj0 = _matmul(h0, wt_g, mode="nt", out_dtype=F32, name="gla_in", tn=wp)
    o0, a0, s_before, s_final = _gla_fwd(proj0, w2p, gla_b_gate, gla_o_gain, lay, name="gla_scan")
    w_out_g = arrived(2, a0, "w_gla_out").reshape(d, d)
    y0 = _matmul(a0, w_out_g, mode="nn", out_dtype=F32, name="gla_out")
    x1, h1 = _post_then_pre(x0, y0, norm_post[0:1], norm_pre[1:2], name="post0_pre1")
    g_wi_s = arrived(3, h1, "w_sgu_in")
    proj1 = _matmul(h1, g_wi_s, mode="nn", out_dtype=F32, name="sgu_in", b_shards=True)
    a1 = _sgu_fwd(proj1, ln_gain, ln_bias, ws_masked, bs_t, name="sgu_gate")
    w_out_s = arrived(4, a1, "w_sgu_out").reshape(d, d)
    acts, tok = _to_sibling_start([(a1, False), (a0, False), (h1, False), (h0, True)], name="acts_to_sibling")
    a1, a0, h1, h0 = [f[4] for f in acts]
    y1 = _matmul(a1, w_out_s, mode="nn", out_dtype=F32, name="sgu_out", after=tok)
    loss_part, dx2, dy1, d_post1 = _loss_head(x1, y1, norm_post[1:2], target, name="loss_head")

    def behind(small, token):
        return small + token[0:1, 0:1]

    def pair_gradient(a_sent, b_sent, after, shards_on, name):
        a_me, a_sib = _from_sibling(a_sent, after, name=name + "_a_wait")
        b_me, b_sib = _from_sibling(b_sent, [a_sib] + list(after), name=name + "_b_wait")
        pair = _matmul_dw_pair(a_me, a_sib, b_me.reshape(t, -1), b_sib.reshape(t, -1), core_idx, shards_on=shards_on,
                               name=name + "_pair")
        return _scatter_start(pair, name=name + "_start")

    def reduced(flight, after, name):
        pair, landed = _scatter_wait(flight, after, name=name + "_wait")
        return _chip_sum(pair, landed, slots, name=name + "_sum")

    (dy1_sent,), tok = _to_sibling_start([(dy1, True)], name="dy1_to_sibling")
    dy1 = dy1_sent[4]
    da1 = _matmul(dy1, w_out_s, mode="nt", out_dtype=F32, name="d_sgu_act", after=tok)
    fl_wo_s, tok = pair_gradient(acts[0], dy1_sent, [da1], "rows", "g_sgu_out")
    dproj1, d_ws, d_bs_t, d_lg, d_lb = _sgu_bwd(da1, proj1, ln_gain, behind(ln_bias, tok), ws_masked, ws_masked_t,
                                                bs_t, name="sgu_gate_bwd")
    (dp1_sent,), tok = _to_sibling_start([(dproj1.reshape(t, N_CHIPS, -1), True)], name="dproj1_to_sibling")
    dproj1 = dp1_sent[4].reshape(t, -1)
    dh1 = _matmul_nt_shards(dproj1, g_wi_s, out_dtype=F32, name="d_sgu_h", after=tok)
    fl_wi_s, tok = pair_gradient(acts[2], dp1_sent, [dh1], "cols", "g_sgu_in")
    dx1, dy0, d_pre1, d_post0 = _mid_bwd(dx2, dh1, x1, behind(norm_pre[1:2], tok), y0, norm_post[0:1],
                                         name="pre1_post0_bwd")
    (dy0_sent,), tok = _to_sibling_start([(dy0, True)], name="dy0_to_sibling")
    dy0 = dy0_sent[4]
    da0 = _matmul(dy0, w_out_g, mode="nt", out_dtype=F32, name="d_gla_act", after=tok)
    fl_wo_g, tok = pair_gradient(acts[1], dy0_sent, [da0], "rows", "g_gla_out")
    dproj0, d_og, d_bg, d_w2p = _gla_bwd(da0, o0, proj0, w2p, behind(gla_b_gate, tok), gla_o_gain, s_before, s_final,
                                         lay, name="gla_scan_bwd")
    early_shapes = [norm_post.shape, gla_b_gate.shape, gla_o_gain.shape, sgu_w_spatial.shape, sgu_b_spatial.shape,
                    (1, GLA_GATE_RANK, dk), (1, d), (1, d), (1, LANES)]
    early_part = _pack([jnp.concatenate([d_post0, d_post1], axis=0), d_bg, d_og, jnp.where(mask[None], d_ws, 0.0)[None],
                        d_bs_t.T[None], d_w2p[:GLA_GATE_RANK][None], d_lg, d_lb, loss_part])
    early_flight, tok = _dev_gather_start(early_part, name="small_early_start")
    (dp0_sent,), tok_sent = _to_sibling_start([(dproj0, False)], name="dproj0_to_sibling")
    dproj0 = dp0_sent[4]
    dh0 = _matmul(dproj0, wt_g, mode="nn", out_dtype=F32, name="d_gla_h", tk=N_CHIPS * wp, after=tok_sent)
    fl_wi_g, tok_scatter = pair_gradient(dp0_sent, acts[3], [dh0, tok], "rows", "g_gla_in")
    r_wo_s = reduced(fl_wo_s, tok_scatter, "g_sgu_out")
    r_wi_s = reduced(fl_wi_s, r_wo_s, "g_sgu_in")
    r_wo_g = reduced(fl_wo_g, r_wi_s, "g_gla_out")
    sharing, tok = _share_start([r_wo_s, r_wi_s, r_wo_g], name="grads_share_a")
    grad_x, d_pre0 = _first_bwd(dx1, dh0, x0, behind(norm_pre[0:1], tok), name="pre0_bwd")

    late_part = _pack([jnp.concatenate([d_pre0, d_pre1], axis=0)])
    late_flight, tok = _dev_gather_start(late_part, name="small_late_start")

    def big_update(w, g, m, v, name, after=None):
        return [u[None] for u in _adamw(w[0], g, m[0], v[0], name=name, after=after)]

    g_wo_sgu, g_wi_sgu, g_wo_gla = _share_wait(sharing, [grad_x, tok], name="grads_share_a_wait")
    u_wo_sgu = big_update(sgu_w_out, g_wo_sgu, m_sgu_w_out, v_sgu_w_out, "adamw_sgu_w_out")
    u_wo_gla = big_update(gla_w_out, g_wo_gla, m_gla_w_out, v_gla_w_out, "adamw_gla_w_out")

    def summed_over_devices(part, flight, after, shapes, name):
        land = _dev_gather_wait(flight, after, name=name + "_wait")
        every = lax.dynamic_update_slice(land, part[None], (2 * chip + core, 0, 0))
        return _unpack(_stack_sum(every, name=name + "_sum"), shapes)

    updated = [u_wo_gla[1], u_wo_sgu[1]]
    (g_post, g_bg, g_og, g_wsp, g_bsp, g_w2_full, g_lg_full, g_lb_full, loss_vec) = summed_over_devices(
        early_part, early_flight, updated, early_shapes, "small_early")
    g_pre, = summed_over_devices(late_part, late_flight, updated, [norm_pre.shape], "small_late")
    loss = loss_vec[0, 0]
    g_w2 = lax.dynamic_slice_in_dim(g_w2_full, chip * (dk // N_CHIPS), dk // N_CHIPS, axis=2)
    g_lg = lax.dynamic_slice_in_dim(g_lg_full, chip * (d // N_CHIPS), d // N_CHIPS, axis=1)
    g_lb = lax.dynamic_slice_in_dim(g_lb_full, chip * (d // N_CHIPS), d // N_CHIPS, axis=1)

    small_w = [norm_pre, norm_post, gla_b_gate, gla_o_gain, sgu_w_spatial, sgu_b_spatial, gla_w_gate2, sgu_ln_gain,
               sgu_ln_bias]
    small_g = [g_pre, g_post, g_bg, g_og, g_wsp, g_bsp, g_w2, g_lg, g_lb]
    small_m = [m_norm_pre, m_norm_post, m_gla_b_gate, m_gla_o_gain, m_sgu_w_spatial, m_sgu_b_spatial, m_gla_w_gate2,
               m_sgu_ln_gain, m_sgu_ln_bias]
    small_v = [v_norm_pre, v_norm_post, v_gla_b_gate, v_gla_o_gain, v_sgu_w_spatial, v_sgu_b_spatial, v_gla_w_gate2,
               v_sgu_ln_gain, v_sgu_ln_bias]
    own_shapes = [w.shape for w in small_w]
    _, s_dl, s_m, s_v = _adamw(_pack(small_w), _pack(small_g), _pack(small_m), _pack(small_v), name="adamw_small")
    dl_s, m_s, v_s = _unpack(s_dl, own_shapes), _unpack(s_m, own_shapes), _unpack(s_v, own_shapes)

    r_wi_g = reduced(fl_wi_g, s_dl, "g_gla_in")
    gt_wi_gla, = _sibling_share_halves([r_wi_g], name="grads_share_b")
    u_wi_gla_t = _adamw(wt_in_g, gt_wi_gla, mt_in_g, vt_in_g, name="adamw_gla_w_in")
    u_wi_gla = [u.T[None] for u in u_wi_gla_t]
    u_wi_sgu = big_update(sgu_w_in, g_wi_sgu, m_sgu_w_in, v_sgu_w_in, "adamw_sgu_w_in", after=u_wi_gla_t[1])

    def ordered(small, kind):
        pre, post, bg, og, wsp, bsp, w2, lg, lb = small
        return [pre, post, u_wi_gla[kind], w2, bg, og, u_wo_gla[kind], u_wi_sgu[kind], lg, lb, wsp, bsp, u_wo_sgu[kind]]

    return (loss, grad_x[None], *ordered(small_g, 0), *ordered(dl_s, 1), *ordered(m_s, 2), *ordered(v_s, 3))
```

```python
import functools
import math

import jax
import jax.numpy as jnp
from jax import lax
from jax.experimental import pallas as pl
from jax.experimental.pallas import tpu as pltpu

F32 = jnp.float32
BF16 = jnp.bfloat16
MESH = pl.DeviceIdType.MESH

EPS = 1e-6
CHUNK = 64
GLA_HEADS = 4
GLA_GATE_RANK = 16
GLA_TAU = 16.0
SGU_BLOCK = 128
SGU_GROUPS = 8
N_CHIPS = 4
N_DEV = 8
LANES = 128

ADAM_LR = 0.001
ADAM_B1 = 0.9
ADAM_B2 = 0.999
ADAM_EPS = 1e-08
ADAM_WD = 0.01
ADAM_STEP = 10

VMEM_LIMIT = 56 * 1024 * 1024


def _cparams(sem=None):
    return pltpu.CompilerParams(dimension_semantics=sem, vmem_limit_bytes=VMEM_LIMIT)


def _pick(n, cap, unit=LANES):
    best = None
    for t in range(unit, min(n, cap) + 1, unit):
        if n % t == 0:
            best = t
    assert best is not None, (n, cap, unit)
    return best


def _dot(a, b, dims):
    return lax.dot_general(a, b, (dims, ((), ())), preferred_element_type=F32)


def _dot_nn(a, b):
    return _dot(a, b, ((1,), (0,)))


def _dot_nt(a, b):
    return _dot(a, b, ((1,), (1,)))


def _dot_tn(a, b):
    return _dot(a, b, ((0,), (0,)))


def _matmul(a, b, *, mode, out_dtype, name, tm=1024, tn=512, tk=2048, b_shards=False, out_shards=False, after=None,
            out_rows=None):
    if mode == "tn":
        K, M = a.shape
    else:
        M, K = a.shape
    if b_shards:
        ns, br, bc = b.shape
        if mode == "nt":
            N, Kb = br, ns * bc
        else:
            Kb, N = br, ns * bc
    else:
        if mode == "nt":
            N, Kb = b.shape
        else:
            Kb, N = b.shape
    assert K == Kb, (a.shape, b.shape, mode)
    tm = _pick(M, tm)
    tk = _pick(K, tk)
    if b_shards and mode != "nt":
        tn = _pick(bc, tn)
    elif out_shards:
        tn = _pick(N // N_CHIPS, tn)
    else:
        tn = _pick(N, tn)
    if b_shards and mode == "nt":
        tk = _pick(bc, tk)
    nk = K // tk
    grid = (M // tm, N // tn, nk)

    if mode == "tn":
        a_spec = pl.BlockSpec((tk, tm), lambda i, j, k: (k, i))
    else:
        a_spec = pl.BlockSpec((tm, tk), lambda i, j, k: (i, k))
    if b_shards:
        if mode == "nt":
            per = bc // tk
            b_spec = pl.BlockSpec((None, tn, tk), lambda i, j, k: (k // per, j, k % per))
        else:
            per = bc // tn
            b_spec = pl.BlockSpec((None, tk, tn), lambda i, j, k: (j // per, k, j % per))
    elif mode == "nt":
        b_spec = pl.BlockSpec((tn, tk), lambda i, j, k: (j, k))
    else:
        b_spec = pl.BlockSpec((tk, tn), lambda i, j, k: (k, j))
    if out_shards:
        per_o = (N // N_CHIPS) // tn
        out_spec = pl.BlockSpec((None, tm, tn), lambda i, j, k: (j // per_o, i, j % per_o))
        out_shape = jax.ShapeDtypeStruct((N_CHIPS, M, N // N_CHIPS), out_dtype)
    else:
        out_spec = pl.BlockSpec((tm, tn), lambda i, j, k: (i, j))
        out_shape = jax.ShapeDtypeStruct((M if out_rows is None else out_rows, N), out_dtype)

    dims = {"nn": ((1,), (0,)), "nt": ((1,), (1,)), "tn": ((0,), (0,))}[mode]

    def body(a_ref, b_ref, *rest):
        o_ref, scratch = (rest[1], rest[2:]) if after is not None else (rest[0], rest[1:])
        part = _dot(a_ref[...].astype(BF16), b_ref[...].astype(BF16), dims)
        if nk == 1:
            o_ref[...] = part.astype(out_dtype)
        else:
            acc_ref, = scratch
            k = pl.program_id(2)

            @pl.when(k == 0)
            def _():
                acc_ref[...] = part

            @pl.when(k > 0)
            def _():
                acc_ref[...] += part

            @pl.when(k == nk - 1)
            def _():
                o_ref[...] = acc_ref[...].astype(out_dtype)

    extra_specs, extra_args = ([], []) if after is None else ([pl.BlockSpec(memory_space=pl.ANY)], [after])
    return pl.pallas_call(
        body, name=name, grid=grid, in_specs=[a_spec, b_spec] + extra_specs, out_specs=out_spec, out_shape=out_shape,
        scratch_shapes=[] if nk == 1 else [pltpu.VMEM((tm, tn), F32)],
        compiler_params=_cparams(("parallel", "parallel", "arbitrary")),
    )(a, b, *extra_args)


def _matmul_into_cols(a, w, which, buf, *, name, tm=1024):
    M, K = a.shape
    _, N, _ = w.shape
    tm = _pick(M, tm)

    def body(which_ref, a_ref, w_ref, buf_ref, o_ref):
        o_ref[...] = _dot_nt(a_ref[...], w_ref[...])

    grid_spec = pltpu.PrefetchScalarGridSpec(
        num_scalar_prefetch=1, grid=(M // tm,),
        in_specs=[pl.BlockSpec((tm, K), lambda i, s: (i, 0)), pl.BlockSpec((None, N, K), lambda i, s: (s[1], 0, 0)),
                  pl.BlockSpec(memory_space=pl.ANY)],
        out_specs=pl.BlockSpec((tm, N), lambda i, s: (i, s[0])))
    return pl.pallas_call(
        body, name=name, grid_spec=grid_spec, out_shape=jax.ShapeDtypeStruct(buf.shape, buf.dtype),
        input_output_aliases={3: 0}, compiler_params=_cparams(("parallel",)),
    )(which, a, w, buf)


def _matmul_nt_shards(a, b, *, out_dtype, name, tm=1024, tn=512, after=None):
    M, K = a.shape
    ns, N, kc = b.shape
    assert K == ns * kc
    tm, tn = _pick(M, tm), _pick(N, tn)

    def body(a_ref, *rest):
        b_refs, o_ref = rest[:ns], rest[ns + (after is not None)]
        acc = _dot_nt(a_ref[:, 0:kc], b_refs[0][...])
        for j in range(1, ns):
            acc += _dot_nt(a_ref[:, j * kc:(j + 1) * kc], b_refs[j][...])
        o_ref[...] = acc.astype(out_dtype)

    def shard(j):
        return pl.BlockSpec((None, tn, kc), lambda i, n: (j, n, 0))

    extra_specs, extra_args = ([], []) if after is None else ([pl.BlockSpec(memory_space=pl.ANY)], [after])
    return pl.pallas_call(
        body, name=name, grid=(M // tm, N // tn),
        in_specs=[pl.BlockSpec((tm, K), lambda i, n: (i, 0))] + [shard(j) for j in range(ns)] + extra_specs,
        out_specs=pl.BlockSpec((tm, tn), lambda i, n: (i, n)), out_shape=jax.ShapeDtypeStruct((M, N), out_dtype),
        compiler_params=_cparams(("parallel", "parallel")),
    )(a, *([b] * ns), *extra_args)


def _rstd(x):
    return lax.rsqrt(jnp.mean(x * x, axis=-1, keepdims=True) + EPS)


def _row_spec(tr, d):
    return pl.BlockSpec((tr, d), lambda i: (i, 0))


def _vec_spec(d):
    return pl.BlockSpec((1, d), lambda i: (0, 0))


def _acc_rows(ref, i, val, cols=slice(None)):
    @pl.when(i == 0)
    def _():
        ref[:, cols] = val

    @pl.when(i > 0)
    def _():
        ref[:, cols] += val


def _norm_pre(x, gain, *, name, tr=256):
    t, d = x.shape
    tr = _pick(t, tr, 8)

    def body(x_ref, g_ref, h_ref):
        xv = x_ref[...]
        h_ref[...] = (xv * _rstd(xv) * g_ref[...]).astype(BF16)

    return pl.pallas_call(
        body, name=name, grid=(t // tr,), in_specs=[_row_spec(tr, d), _vec_spec(d)], out_specs=_row_spec(tr, d),
        out_shape=jax.ShapeDtypeStruct((t, d), BF16), compiler_params=_cparams(("parallel",)),
    )(x, gain)


def _post_then_pre(x, y, post_gain, pre_gain, *, name, tr=256):
    t, d = x.shape
    tr = _pick(t, tr, 8)

    def body(x_ref, y_ref, pg_ref, ng_ref, xn_ref, h_ref):
        yv = y_ref[...]
        xn = x_ref[...] + yv * _rstd(yv) * pg_ref[...]
        xn_ref[...] = xn
        h_ref[...] = (xn * _rstd(xn) * ng_ref[...]).astype(BF16)

    return pl.pallas_call(
        body, name=name, grid=(t // tr,),
        in_specs=[_row_spec(tr, d), _row_spec(tr, d), _vec_spec(d), _vec_spec(d)],
        out_specs=[_row_spec(tr, d), _row_spec(tr, d)],
        out_shape=[jax.ShapeDtypeStruct((t, d), F32), jax.ShapeDtypeStruct((t, d), BF16)],
        compiler_params=_cparams(("parallel",)),
    )(x, y, post_gain, pre_gain)


def _norm_bwd(dy, n, r, gain):
    dn = dy * gain
    return r * (dn - n * jnp.mean(dn * n, axis=-1, keepdims=True))


def _loss_head(x, y, post_gain, target, *, name, tr=256):
    t, d = x.shape
    tr = _pick(t, tr, 8)

    def body(x_ref, y_ref, pg_ref, t_ref, loss_ref, dx_ref, dy_ref, dpg_ref):
        i = pl.program_id(0)
        yv = y_ref[...]
        r = _rstd(yv)
        n = yv * r
        err = x_ref[...] + n * pg_ref[...] - t_ref[...]
        dx = err * (1.0 / d)
        dx_ref[...] = dx
        part = 0.5 * jnp.sum(jnp.mean(err * err, axis=-1, keepdims=True), axis=0, keepdims=True)
        _acc_rows(loss_ref, i, jnp.broadcast_to(part, (1, LANES)))
        _acc_rows(dpg_ref, i, jnp.sum(dx * n, axis=0, keepdims=True))
        dy_ref[...] = _norm_bwd(dx, n, r, pg_ref[...]).astype(BF16)

    return pl.pallas_call(
        body, name=name, grid=(t // tr,),
        in_specs=[_row_spec(tr, d), _row_spec(tr, d), _vec_spec(d), _row_spec(tr, d)],
        out_specs=[_vec_spec(LANES), _row_spec(tr, d), _row_spec(tr, d), _vec_spec(d)],
        out_shape=[jax.ShapeDtypeStruct((1, LANES), F32), jax.ShapeDtypeStruct((t, d), F32),
                   jax.ShapeDtypeStruct((t, d), BF16), jax.ShapeDtypeStruct((1, d), F32)],
        compiler_params=_cparams(("arbitrary",)),
    )(x, y, post_gain, target)


def _mid_bwd(dx_out, dh, x, pre_gain, y_prev, post_gain_prev, *, name, tr=256):
    t, d = x.shape
    tr = _pick(t, tr, 8)

    def body(dxo_ref, dh_ref, x_ref, ng_ref, y_ref, pg_ref, dx_ref, dy_ref, dng_ref, dpg_ref):
        i = pl.program_id(0)
        xv = x_ref[...]
        r = _rstd(xv)
        xh = xv * r
        dhv = dh_ref[...]
        _acc_rows(dng_ref, i, jnp.sum(dhv * xh, axis=0, keepdims=True))
        dx = dxo_ref[...] + _norm_bwd(dhv, xh, r, ng_ref[...])
        dx_ref[...] = dx
        yv = y_ref[...]
        ry = _rstd(yv)
        n = yv * ry
        _acc_rows(dpg_ref, i, jnp.sum(dx * n, axis=0, keepdims=True))
        dy_ref[...] = _norm_bwd(dx, n, ry, pg_ref[...]).astype(BF16)

    return pl.pallas_call(
        body, name=name, grid=(t // tr,),
        in_specs=[_row_spec(tr, d), _row_spec(tr, d), _row_spec(tr, d), _vec_spec(d), _row_spec(tr, d), _vec_spec(d)],
        out_specs=[_row_spec(tr, d), _row_spec(tr, d), _vec_spec(d), _vec_spec(d)],
        out_shape=[jax.ShapeDtypeStruct((t, d), F32), jax.ShapeDtypeStruct((t, d), BF16),
                   jax.ShapeDtypeStruct((1, d), F32), jax.ShapeDtypeStruct((1, d), F32)],
        compiler_params=_cparams(("arbitrary",)),
    )(dx_out, dh, x, pre_gain, y_prev, post_gain_prev)


def _first_bwd(dx_out, dh, x, pre_gain, *, name, tr=256):
    t, d = x.shape
    tr = _pick(t, tr, 8)

    def body(dxo_ref, dh_ref, x_ref, ng_ref, dx_ref, dng_ref):
        i = pl.program_id(0)
        xv = x_ref[...]
        r = _rstd(xv)
        xh = xv * r
        dhv = dh_ref[...]
        _acc_rows(dng_ref, i, jnp.sum(dhv * xh, axis=0, keepdims=True))
        dx_ref[...] = dxo_ref[...] + _norm_bwd(dhv, xh, r, ng_ref[...])

    return pl.pallas_call(
        body, name=name, grid=(t // tr,),
        in_specs=[_row_spec(tr, d), _row_spec(tr, d), _row_spec(tr, d), _vec_spec(d)],
        out_specs=[_row_spec(tr, d), _vec_spec(d)],
        out_shape=[jax.ShapeDtypeStruct((t, d), F32), jax.ShapeDtypeStruct((1, d), F32)],
        compiler_params=_cparams(("arbitrary",)),
    )(dx_out, dh, x, pre_gain)


def _sigmoid(x):
    return 1.0 / (1.0 + jnp.exp(-x))


def _log_sigmoid(x):
    return jnp.minimum(x, 0.0) - jnp.log(1.0 + jnp.exp(-jnp.abs(x)))


_GELU_C = math.sqrt(2.0 / math.pi)


_GELU_A = 0.044715


def _gelu_parts(x, with_grad=True):
    x2 = x * x
    h = 0.5 * jnp.tanh(x * (_GELU_C + (_GELU_C * _GELU_A) * x2)) + 0.5
    val = x * h
    if not with_grad:
        return val, None
    return val, h * (1.0 + (1.0 - h) * (x * (2.0 * _GELU_C + (6.0 * _GELU_C * _GELU_A) * x2)))


def _split3(x):
    hi = x.astype(BF16)
    r1 = x - hi.astype(F32)
    mid = r1.astype(BF16)
    lo = (r1 - mid.astype(F32)).astype(BF16)
    return hi, mid, lo


def _tri_matmul(tri_bf16, x):
    hi, mid, lo = _split3(x)
    return _dot_nn(tri_bf16, hi) + _dot_nn(tri_bf16, mid) + _dot_nn(tri_bf16, lo)


def _gla_dims(d):
    dk, dv = d // 2, d
    return dk, dv, dk // GLA_HEADS, dv // GLA_HEADS


def _col_pieces(a, b, lay):
    ws, wp = lay
    out = []
    while a < b:
        j = a // ws
        end = min(b, (j + 1) * ws)
        out.append((j * wp + a - j * ws, end - a))
        a = end
    return out


def _load_cols(ref, a, b, lay):
    parts = [ref[:, s:s + n] for s, n in _col_pieces(a, b, lay)]
    return parts[0] if len(parts) == 1 else jnp.concatenate(parts, axis=1)


def _store_cols(ref, a, val, lay):
    off = 0
    for s, n in _col_pieces(a, a + val.shape[1], lay):
        ref[:, s:s + n] = val[:, off:off + n]
        off += n


def _gate_window(c_r, lay):
    (start, _), = _col_pieces(c_r, c_r + GLA_GATE_RANK, lay)
    assert (start % lay[1]) + LANES <= lay[1]
    return slice(start, start + LANES)


def _gla_gates(glr, k, w2_ref, b_ref):
    z = _dot_nn(glr.astype(BF16), w2_ref[...].astype(BF16)) + b_ref[...]
    la = _log_sigmoid(z) * (1.0 / GLA_TAU)
    row = lax.broadcasted_iota(jnp.int32, (CHUNK, CHUNK), 0)
    col = lax.broadcasted_iota(jnp.int32, (CHUNK, CHUNK), 1)
    incl = (row >= col).astype(BF16)
    bcum = _tri_matmul(incl, la)
    b_end = bcum[CHUNK - 1:CHUNK, :]
    e_rest = jnp.exp(b_end - bcum)
    return z, e_rest, k * e_rest, jnp.exp(b_end)


def _gla_fwd(proj, w2p, b_gate, o_gain, lay, *, name):
    t, wcols = proj.shape
    d = o_gain.shape[1]
    dk, dv, dkh, dvh = _gla_dims(d)
    nc = t // CHUNK
    c_k, c_v, c_g, c_r = dk, 2 * dk, 2 * dk + dv, 2 * dk + 2 * dv
    scale = dkh ** -0.5

    def body(p_ref, w2_ref, b_ref, og_ref, o_ref, a_ref, sb_ref, sfin_ref, s_ref):
        i = pl.program_id(0)

        @pl.when(i == 0)
        def _():
            s_ref[...] = jnp.zeros_like(s_ref)

        q = _load_cols(p_ref, 0, dk, lay) * scale
        k = _load_cols(p_ref, c_k, c_k + dk, lay)
        glr = p_ref[:, _gate_window(c_r, lay)]
        _, _, kdec, decay = _gla_gates(glr, k, w2_ref, b_ref)
        for h in range(GLA_HEADS):
            ks = slice(h * dkh, (h + 1) * dkh)
            vs = slice(h * dvh, (h + 1) * dvh)
            v_h = _load_cols(p_ref, c_v + h * dvh, c_v + (h + 1) * dvh, lay)
            g_h = _load_cols(p_ref, c_g + h * dvh, c_g + (h + 1) * dvh, lay)
            s_old = s_ref[h]
            sb_ref[0, h] = s_old
            s_new = s_old * decay[:, ks] + _dot_tn(v_h.astype(BF16), kdec[:, ks].astype(BF16))
            s_ref[h] = s_new
            o_h = _dot_nt(q[:, ks].astype(BF16), s_new.astype(BF16))
            o_ref[:, vs] = o_h
            on = o_h * _rstd(o_h)
            a_ref[:, vs] = (on * og_ref[:, vs] * (g_h * _sigmoid(g_h))).astype(BF16)

        @pl.when(i == nc - 1)
        def _():
            sfin_ref[...] = s_ref[...]

    full = lambda *shape: pl.BlockSpec(shape, lambda i: (0,) * len(shape))
    return pl.pallas_call(
        body, name=name, grid=(nc,),
        in_specs=[pl.BlockSpec((CHUNK, wcols), lambda i: (i, 0)), full(LANES, dk), full(1, dk), full(1, dv)],
        out_specs=[pl.BlockSpec((CHUNK, dv), lambda i: (i, 0)), pl.BlockSpec((CHUNK, dv), lambda i: (i, 0)),
                   pl.BlockSpec((1, GLA_HEADS, dvh, dkh), lambda i: (i, 0, 0, 0)), full(GLA_HEADS, dvh, dkh)],
        out_shape=[jax.ShapeDtypeStruct((t, dv), F32), jax.ShapeDtypeStruct((t, dv), BF16),
                   jax.ShapeDtypeStruct((nc, GLA_HEADS, dvh, dkh), F32),
                   jax.ShapeDtypeStruct((GLA_HEADS, dvh, dkh), F32)],
        scratch_shapes=[pltpu.VMEM((GLA_HEADS, dvh, dkh), F32)],
        compiler_params=_cparams(("arbitrary",)),
    )(proj, w2p, b_gate, o_gain)


def _gla_bwd(da, o, proj, w2p, b_gate, o_gain, s_before, s_final, lay, *, name):
    t, wcols = proj.shape
    d = o_gain.shape[1]
    dk, dv, dkh, dvh = _gla_dims(d)
    nc = t // CHUNK
    c_k, c_v, c_g, c_r = dk, 2 * dk, 2 * dk + dv, 2 * dk + 2 * dv
    scale = dkh ** -0.5

    def body(da_ref, o_ref, p_ref, w2_ref, b_ref, og_ref, sb_ref, sfin_ref,
             dp_ref, dog_ref, db_ref, dw2_ref, s_ref, gc_ref, dkd_ref):
        i = pl.program_id(0)

        @pl.when(i == 0)
        def _():
            s_ref[...] = sfin_ref[...]
            gc_ref[...] = jnp.zeros_like(gc_ref)

        ws, wp = lay
        for j in range(N_CHIPS):
            dp_ref[:, j * wp + ws:(j + 1) * wp] = jnp.zeros((CHUNK, wp - ws), BF16)
        q = _load_cols(p_ref, 0, dk, lay) * scale
        k = _load_cols(p_ref, c_k, c_k + dk, lay)
        glr = p_ref[:, _gate_window(c_r, lay)]
        z, e_rest, kdec, decay = _gla_gates(glr, k, w2_ref, b_ref)
        ddecay = []
        for h in range(GLA_HEADS):
            ks = slice(h * dkh, (h + 1) * dkh)
            vs = slice(h * dvh, (h + 1) * dvh)
            v_h = _load_cols(p_ref, c_v + h * dvh, c_v + (h + 1) * dvh, lay)
            g_h = _load_cols(p_ref, c_g + h * dvh, c_g + (h + 1) * dvh, lay)
            da_h = da_ref[:, vs]
            o_h = o_ref[:, vs]
            og_h = og_ref[:, vs]
            r = _rstd(o_h)
            on = o_h * r
            sg = _sigmoid(g_h)
            silu = g_h * sg
            _acc_rows(dog_ref, i, jnp.sum(da_h * silu * on, axis=0, keepdims=True), vs)
            _store_cols(dp_ref, c_g + h * dvh, (da_h * (on * og_h) * (sg * (1.0 + g_h * (1.0 - sg)))).astype(BF16),
                        lay)
            don = da_h * silu * og_h
            do_h = (r * (don - on * jnp.mean(don * on, axis=-1, keepdims=True))).astype(BF16)
            s_cur = s_ref[h]
            _store_cols(dp_ref, h * dkh, (_dot_nn(do_h, s_cur.astype(BF16)) * scale).astype(BF16), lay)
            g_tot = gc_ref[h] + _dot_tn(do_h, q[:, ks].astype(BF16))
            g_bf = g_tot.astype(BF16)
            dkd_ref[:, ks] = _dot_nn(v_h.astype(BF16), g_bf)
            _store_cols(dp_ref, c_v + h * dvh, _dot_nt(kdec[:, ks].astype(BF16), g_bf).astype(BF16), lay)
            s_prev = sb_ref[0, h]
            ddecay.append(jnp.sum(g_tot * s_prev, axis=0, keepdims=True))
            gc_ref[h] = g_tot * decay[:, ks]
            s_ref[h] = s_prev
        dkdec = dkd_ref[...]
        _store_cols(dp_ref, c_k, (dkdec * e_rest).astype(BF16), lay)
        d_e = dkdec * kdec
        row = lax.broadcasted_iota(jnp.int32, (CHUNK, CHUNK), 0)
        col = lax.broadcasted_iota(jnp.int32, (CHUNK, CHUNK), 1)
        excl = (row > col).astype(BF16)
        dla = jnp.concatenate(ddecay, axis=1) * decay + _tri_matmul(excl, d_e)
        dz = dla * (1.0 / GLA_TAU) * (1.0 - _sigmoid(z))
        _acc_rows(db_ref, i, jnp.sum(dz, axis=0, keepdims=True))
        dz_bf = dz.astype(BF16)
        dw2 = _dot_tn(glr.astype(BF16), dz_bf)

        @pl.when(i == 0)
        def _():
            dw2_ref[...] = dw2

        @pl.when(i > 0)
        def _():
            dw2_ref[...] += dw2

        dp_ref[:, _gate_window(c_r, lay)] = _dot_nt(dz_bf, w2_ref[...].astype(BF16)).astype(BF16)

    rev = lambda i: (nc - 1 - i, 0)
    full = lambda *shape: pl.BlockSpec(shape, lambda i: (0,) * len(shape))
    return pl.pallas_call(
        body, name=name, grid=(nc,),
        in_specs=[pl.BlockSpec((CHUNK, dv), rev), pl.BlockSpec((CHUNK, dv), rev), pl.BlockSpec((CHUNK, wcols), rev),
                  full(LANES, dk), full(1, dk), full(1, dv),
                  pl.BlockSpec((1, GLA_HEADS, dvh, dkh), lambda i: (nc - 1 - i, 0, 0, 0)), full(GLA_HEADS, dvh, dkh)],
        out_specs=[pl.BlockSpec((CHUNK, wcols), rev), full(1, dv), full(1, dk), full(LANES, dk)],
        out_shape=[jax.ShapeDtypeStruct((t, wcols), BF16), jax.ShapeDtypeStruct((1, dv), F32),
                   jax.ShapeDtypeStruct((1, dk), F32), jax.ShapeDtypeStruct((LANES, dk), F32)],
        scratch_shapes=[pltpu.VMEM((GLA_HEADS, dvh, dkh), F32), pltpu.VMEM((GLA_HEADS, dvh, dkh), F32),
                        pltpu.VMEM((CHUNK, dk), F32)],
        compiler_params=_cparams(("arbitrary",)),
    )(da, o, proj, w2p, b_gate, o_gain, s_before, s_final)


def _sgu_mid(p_ref, lg_ref, lb_ref, ws_ref, bst_ref, w, with_grad=True):
    gd = w // SGU_GROUPS
    u_act, du_fac = _gelu_parts(p_ref[:, 0:w], with_grad)
    vf, dv_fac = _gelu_parts(p_ref[:, w:2 * w], with_grad)
    mu = jnp.mean(vf, axis=-1, keepdims=True)
    cen = vf - mu
    rstd = lax.rsqrt(jnp.mean(cen * cen, axis=-1, keepdims=True) + EPS)
    xh = cen * rstd
    vn = (xh * lg_ref[...] + lb_ref[...]).astype(BF16)
    vs = [_dot_nn(ws_ref[g].astype(BF16), vn[:, g * gd:(g + 1) * gd]) + bst_ref[:, g:g + 1]
          for g in range(SGU_GROUPS)]
    return u_act, du_fac, dv_fac, rstd, xh, vn, vs


def _sgu_fwd(proj, ln_gain, ln_bias, ws_masked, bs_t, *, name):
    t, w3 = proj.shape
    w = w3 // 3
    gd = w // SGU_GROUPS
    nb = t // SGU_BLOCK

    def body(p_ref, lg_ref, lb_ref, ws_ref, bst_ref, a_ref):
        u_act, _, _, _, _, _, vs = _sgu_mid(p_ref, lg_ref, lb_ref, ws_ref, bst_ref, w, with_grad=False)
        for g in range(SGU_GROUPS):
            cs = slice(g * gd, (g + 1) * gd)
            gate = p_ref[:, 2 * w + g * gd:2 * w + (g + 1) * gd]
            a_ref[:, cs] = (u_act[:, cs] * vs[g] * (gate * _sigmoid(gate))).astype(BF16)

    full = lambda *shape: pl.BlockSpec(shape, lambda i: (0,) * len(shape))
    return pl.pallas_call(
        body, name=name, grid=(nb,),
        in_specs=[pl.BlockSpec((SGU_BLOCK, w3), lambda i: (i, 0)), full(1, w), full(1, w),
                  full(SGU_GROUPS, SGU_BLOCK, SGU_BLOCK), full(SGU_BLOCK, SGU_GROUPS)],
        out_specs=pl.BlockSpec((SGU_BLOCK, w), lambda i: (i, 0)),
        out_shape=jax.ShapeDtypeStruct((t, w), BF16),
        compiler_params=_cparams(("parallel",)),
    )(proj, ln_gain, ln_bias, ws_masked, bs_t)


def _sgu_bwd(da, proj, ln_gain, ln_bias, ws_masked, ws_masked_t, bs_t, *, name):
    t, w3 = proj.shape
    w = w3 // 3
    gd = w // SGU_GROUPS
    nb = t // SGU_BLOCK

    def body(da_ref, p_ref, lg_ref, lb_ref, ws_ref, wst_ref, bst_ref, dp_ref, dws_ref, dbst_ref, dlg_ref, dlb_ref,
             dvn_ref):
        i = pl.program_id(0)
        u_act, du_fac, dv_fac, rstd, xh, vn, vs = _sgu_mid(p_ref, lg_ref, lb_ref, ws_ref, bst_ref, w)
        for g in range(SGU_GROUPS):
            cs = slice(g * gd, (g + 1) * gd)
            gate = p_ref[:, 2 * w + g * gd:2 * w + (g + 1) * gd]
            sg = _sigmoid(gate)
            silu = gate * sg
            da_g = da_ref[:, cs]
            ua_g = u_act[:, cs]
            dp_ref[:, cs] = (da_g * vs[g] * silu * du_fac[:, cs]).astype(BF16)
            dp_ref[:, 2 * w + g * gd:2 * w + (g + 1) * gd] = (
                da_g * ua_g * vs[g] * (sg * (1.0 + gate * (1.0 - sg)))).astype(BF16)
            dvs = da_g * ua_g * silu
            dvs_bf = dvs.astype(BF16)
            dvn_ref[:, cs] = _dot_nn(wst_ref[g].astype(BF16), dvs_bf)
            dws = _dot_nt(dvs_bf, vn[:, cs])
            dbs = jnp.sum(dvs, axis=1, keepdims=True)

            @pl.when(i == 0)
            def _():
                dws_ref[g] = dws
                dbst_ref[:, g:g + 1] = dbs

            @pl.when(i > 0)
            def _():
                dws_ref[g] += dws
                dbst_ref[:, g:g + 1] += dbs

        dvn = dvn_ref[...]
        _acc_rows(dlg_ref, i, jnp.sum(dvn * xh, axis=0, keepdims=True))
        _acc_rows(dlb_ref, i, jnp.sum(dvn, axis=0, keepdims=True))
        dxh = dvn * lg_ref[...]
        dvf = rstd * (dxh - jnp.mean(dxh, axis=-1, keepdims=True)
                      - xh * jnp.mean(dxh * xh, axis=-1, keepdims=True))
        dp_ref[:, w:2 * w] = (dvf * dv_fac).astype(BF16)

    full = lambda *shape: pl.BlockSpec(shape, lambda i: (0,) * len(shape))
    return pl.pallas_call(
        body, name=name, grid=(nb,),
        in_specs=[pl.BlockSpec((SGU_BLOCK, w), lambda i: (i, 0)), pl.BlockSpec((SGU_BLOCK, w3), lambda i: (i, 0)),
                  full(1, w), full(1, w), full(SGU_GROUPS, SGU_BLOCK, SGU_BLOCK),
                  full(SGU_GROUPS, SGU_BLOCK, SGU_BLOCK), full(SGU_BLOCK, SGU_GROUPS)],
        out_specs=[pl.BlockSpec((SGU_BLOCK, w3), lambda i: (i, 0)), full(SGU_GROUPS, SGU_BLOCK, SGU_BLOCK),
                   full(SGU_BLOCK, SGU_GROUPS), full(1, w), full(1, w)],
        out_shape=[jax.ShapeDtypeStruct((t, w3), BF16), jax.ShapeDtypeStruct((SGU_GROUPS, SGU_BLOCK, SGU_BLOCK), F32),
                   jax.ShapeDtypeStruct((SGU_BLOCK, SGU_GROUPS), F32), jax.ShapeDtypeStruct((1, w), F32),
                   jax.ShapeDtypeStruct((1, w), F32)],
        scratch_shapes=[pltpu.VMEM((SGU_BLOCK, w), F32)],
        compiler_params=_cparams(("arbitrary",)),
    )(da, proj, ln_gain, ln_bias, ws_masked, ws_masked_t, bs_t)


def _tile2d(rows, cols, block_bytes, row_unit):
    if rows % row_unit == 0:
        return _pick(rows, max(row_unit, block_bytes // (4 * cols)), row_unit), cols
    return rows, _pick(cols, max(LANES, block_bytes // (4 * rows)))


def _adamw(w, g, m, v, *, name, block_bytes=1 << 20, after=None):
    rows, cols = w.shape
    tr, tc = _tile2d(rows, cols, block_bytes, 8)
    g_rows = g.shape[0]
    assert g_rows == rows or tr == rows
    extra_specs, extra_args = ([], []) if after is None else ([pl.BlockSpec(memory_space=pl.ANY)], [after])

    def body(w_ref, g_ref, m_ref, v_ref, *rest):
        go_ref, d_ref, mo_ref, vo_ref = rest[len(extra_args):]
        gv = g_ref[0:tr, :]
        go_ref[...] = gv
        mn = ADAM_B1 * m_ref[...] + (1.0 - ADAM_B1) * gv
        vn = ADAM_B2 * v_ref[...] + (1.0 - ADAM_B2) * (gv * gv)
        m_hat = mn / (1.0 - ADAM_B1 ** ADAM_STEP)
        v_hat = vn / (1.0 - ADAM_B2 ** ADAM_STEP)
        d_ref[...] = -ADAM_LR * (m_hat / (jnp.sqrt(v_hat) + ADAM_EPS) + ADAM_WD * w_ref[...])
        mo_ref[...] = mn
        vo_ref[...] = vn

    spec = pl.BlockSpec((tr, tc), lambda i, j: (i, j))
    g_spec = spec if g_rows == rows else pl.BlockSpec((g_rows, tc), lambda i, j: (0, j))
    return pl.pallas_call(
        body, name=name, grid=(rows // tr, cols // tc), in_specs=[spec, g_spec, spec, spec] + extra_specs,
        out_specs=[spec] * 4, out_shape=[jax.ShapeDtypeStruct((rows, cols), F32)] * 4,
        compiler_params=_cparams(("parallel", "parallel")),
    )(w, g, m, v, *extra_args)


def _matmul_dw_pair(a_me, a_sib, b_me, b_sib, core_idx, *, shards_on, name, after=None):
    T, M = a_me.shape
    N = b_me.shape[1]
    if shards_on == "rows":
        tm, hc = M // N_CHIPS, N // 2
        tn = _pick(hc, 512)
        per = hc // tn
        grid = (N_CHIPS, per)
        a_spec = pl.BlockSpec((T, tm), lambda i, n, h: (0, i))
        b_me_spec = pl.BlockSpec((T, tn), lambda i, n, h: (0, h[0] * per + n))
        b_sib_spec = pl.BlockSpec((T, tn), lambda i, n, h: (0, n))
        out_spec = pl.BlockSpec((None, tm, tn), lambda i, n, h: (i, 0, n))
        out_shape = jax.ShapeDtypeStruct((N_CHIPS, tm, hc), BF16)
    else:
        tm, hc = _pick(M, 1024), N // N_CHIPS // 2
        grid = (M // tm, N_CHIPS)
        a_spec = pl.BlockSpec((T, tm), lambda i, j, h: (0, i))
        b_me_spec = pl.BlockSpec((T, hc), lambda i, j, h: (0, 2 * j + h[0]))
        b_sib_spec = pl.BlockSpec((T, hc), lambda i, j, h: (0, j))
        out_spec = pl.BlockSpec((None, tm, hc), lambda i, j, h: (j, i, 0))
        out_shape = jax.ShapeDtypeStruct((N_CHIPS, M, hc), BF16)
    extra_specs, extra_args = ([], []) if after is None else ([pl.BlockSpec(memory_space=pl.ANY)], [after])

    def body(h_ref, am_ref, as_ref, bm_ref, bs_ref, *rest):
        o_ref = rest[len(extra_args)]
        o_ref[...] = (_dot_tn(am_ref[...], bm_ref[...]) + _dot_tn(as_ref[...], bs_ref[...])).astype(BF16)

    grid_spec = pltpu.PrefetchScalarGridSpec(
        num_scalar_prefetch=1, grid=grid, in_specs=[a_spec, a_spec, b_me_spec, b_sib_spec] + extra_specs,
        out_specs=out_spec)
    return pl.pallas_call(
        body, name=name, grid_spec=grid_spec, out_shape=out_shape, compiler_params=_cparams(("parallel", "parallel")),
    )(core_idx, a_me, a_sib, b_me, b_sib, *extra_args)


def _chip_sum(pair, landed, slots, *, name, block_bytes=1 << 20):
    _, r, hc = pair.shape
    tr, tc = _tile2d(r, hc, block_bytes, 16)
    ncb = hc // tc

    def body(s_ref, own_ref, l0_ref, l1_ref, l2_ref, o_ref):
        o_ref[...] = ((own_ref[...].astype(F32) + l0_ref[...].astype(F32)) + l1_ref[...].astype(F32)
                      ) + l2_ref[...].astype(F32)

    def slab(which):
        return pl.BlockSpec((None, tr, tc), lambda i, k, s: (s[which], i, k))

    grid_spec = pltpu.PrefetchScalarGridSpec(
        num_scalar_prefetch=1, grid=(r // tr, ncb),
        in_specs=[slab(0), slab(1), slab(2), slab(3)],
        out_specs=pl.BlockSpec((tr, tc), lambda i, k, s: (i, s[4] * ncb + k)))
    return pl.pallas_call(
        body, name=name, grid_spec=grid_spec, out_shape=jax.ShapeDtypeStruct((r, 2 * hc), F32),
        compiler_params=_cparams(("parallel", "parallel")),
    )(slots, pair, landed, landed, landed)


def _stack_sum(x, *, name, out_dtype=F32, block_bytes=1 << 20):
    s, r, c = x.shape
    tr = _pick(r, max(8, block_bytes // (4 * c)), 16) if r % 16 == 0 else r

    def body(x_ref, o_ref):
        acc = x_ref[0].astype(F32)
        for j in range(1, s):
            acc = acc + x_ref[j].astype(F32)
        o_ref[...] = acc.astype(out_dtype)

    return pl.pallas_call(
        body, name=name, grid=(r // tr,),
        in_specs=[pl.BlockSpec((s, tr, c), lambda i: (0, i, 0))], out_specs=pl.BlockSpec((tr, c), lambda i: (i, 0)),
        out_shape=jax.ShapeDtypeStruct((r, c), out_dtype), compiler_params=_cparams(("parallel",)),
    )(x)


HBM = pl.BlockSpec(memory_space=pltpu.HBM)


def _place():
    x, y, c = lax.axis_index("x"), lax.axis_index("y"), lax.axis_index("c")
    other_chips = [(1 - x, y), (x, 1 - y), (1 - x, 1 - y)]
    return x, y, c, other_chips


def _half_cols(cols, which):
    hc = cols // 2
    return pl.ds(pl.multiple_of(which * hc, LANES), hc)


SEM = pl.BlockSpec(memory_space=pltpu.SEMAPHORE)
ANY = pl.BlockSpec(memory_space=pl.ANY)
SIDE_EFFECT = pltpu.SideEffectType.DATAFLOW_SIDE_EFFECTING
TOKEN_SHAPE = (8, LANES)


def _hbm(shape, dtype):
    return pltpu.HBM(shape, dtype)


def _in_hbm(a):
    return pltpu.with_memory_space_constraint(a, pltpu.HBM)


def _gather_copy(src_ref, land_ref, ssem, rsem, k, chip_of_block, to, c):
    cols = src_ref.shape[1]
    return pltpu.make_async_remote_copy(
        src_ref=src_ref.at[:, _half_cols(cols, c)], dst_ref=land_ref.at[chip_of_block, :, _half_cols(cols, c)],
        send_sem=ssem.at[k], recv_sem=rsem.at[k], device_id=to, device_id_type=MESH)


def _gather_start(shards, *, name, after=()):
    n = len(shards)
    after = list(after)

    def body(*refs):
        srcs, lands = refs[:n], refs[n:2 * n]
        outs = refs[2 * n + len(after):]
        token = outs[-1]
        x, y, c, chips = _place()
        me = 2 * x + y
        for a in range(n):
            ssem, rsem = outs[4 * a], outs[4 * a + 1]
            for k, (cx, cy) in enumerate(chips):
                _gather_copy(srcs[a], lands[a], ssem, rsem, k, me, (cx, cy, c), c).start()
        token[...] = jnp.zeros_like(token)

    out_shape, out_specs, aliases = [], [], {}
    for a, s in enumerate(shards):
        out_shape += [pltpu.SemaphoreType.DMA((3,)), pltpu.SemaphoreType.DMA((3,)), _hbm(s.shape, s.dtype),
                      _hbm((N_CHIPS,) + s.shape, s.dtype)]
        out_specs += [SEM, SEM, HBM, HBM]
        aliases[a] = 4 * a + 2
        aliases[n + a] = 4 * a + 3
    out_shape.append(jax.ShapeDtypeStruct(TOKEN_SHAPE, F32))
    out_specs.append(pl.BlockSpec(memory_space=pltpu.VMEM))
    lands = [_in_hbm(lax.empty((N_CHIPS,) + s.shape, s.dtype)) for s in shards]
    res = pl.pallas_call(
        body, name=name, in_specs=[HBM] * (2 * n) + [ANY] * len(after), out_specs=out_specs, out_shape=out_shape,
        input_output_aliases=aliases, compiler_params=pltpu.CompilerParams(has_side_effects=SIDE_EFFECT),
    )(*[_in_hbm(s) for s in shards], *lands, *after)
    return [tuple(res[4 * a:4 * a + 4]) for a in range(n)], res[-1]


def _wait_call(wait_fn, parts, after, *, name):
    ssem, rsem, src, land = parts
    after = list(after) if isinstance(after, (list, tuple)) else [after]

    def body(src_ref, land_ref, ssem_ref, rsem_ref, *rest):
        wait_fn(src_ref, land_ref, ssem_ref, rsem_ref)

    return pl.pallas_call(
        body, name=name, in_specs=[HBM, HBM, SEM, SEM] + [ANY] * len(after), out_specs=[HBM, HBM],
        out_shape=[_hbm(src.shape, src.dtype), _hbm(land.shape, land.dtype)], input_output_aliases={0: 0, 1: 1},
        compiler_params=pltpu.CompilerParams(has_side_effects=SIDE_EFFECT),
    )(src, land, ssem, rsem, *after)


ALL_CHIPS = (0, 1, 2)


def _gather_wait(parts, after, *, name, ks=ALL_CHIPS):
    def wait(src_ref, land_ref, ssem_ref, rsem_ref):
        x, y, c, chips = _place()
        for k in ks:
            cx, cy = chips[k]
            cp = _gather_copy(src_ref, land_ref, ssem_ref, rsem_ref, k, 2 * cx + cy, (x, y, c), c)
            cp.wait_send()
            cp.wait_recv()

    src, land = _wait_call(wait, parts, after, name=name)
    return (parts[0], parts[1], src, land)


def _forward_copy(buf_ref, ssem, rsem, k, slab, which, to):
    part = buf_ref.at[slab, :, _half_cols(buf_ref.shape[2], which)]
    return pltpu.make_async_remote_copy(
        src_ref=part, dst_ref=part, send_sem=ssem.at[k], recv_sem=rsem.at[k], device_id=to, device_id_type=MESH)


def _sibling_forward(land, *, name, ks=ALL_CHIPS):
    def body(_, buf, send_sems, recv_sems):
        x, y, c, chips = _place()
        copies = []
        for k in ks:
            cx, cy = chips[k]
            cp = _forward_copy(buf, send_sems, recv_sems, k, 2 * cx + cy, c, (x, y, 1 - c))
            cp.start()
            copies.append(cp)
        for k in ks:
            cx, cy = chips[k]
            _forward_copy(buf, send_sems, recv_sems, k, 2 * cx + cy, 1 - c, (x, y, c)).wait_recv()
        for cp in copies:
            cp.wait_send()

    return pl.pallas_call(
        body, name=name, in_specs=[HBM], out_specs=HBM, out_shape=jax.ShapeDtypeStruct(land.shape, land.dtype),
        input_output_aliases={0: 0},
        scratch_shapes=[pltpu.SemaphoreType.DMA((3,)), pltpu.SemaphoreType.DMA((3,))],
    )(land)


def _share_copy(buf_ref, ssem, rsem, a, which, to):
    part = buf_ref.at[:, _half_cols(buf_ref.shape[1], which)]
    return pltpu.make_async_remote_copy(
        src_ref=part, dst_ref=part, send_sem=ssem.at[a], recv_sem=rsem.at[a], device_id=to, device_id_type=MESH)


def _share_start(arrays, *, name):
    n = len(arrays)

    def body(*refs):
        bufs, ssem, rsem, token = refs[:n], refs[n], refs[n + 1], refs[-1]
        x, y, c, _ = _place()
        for a in range(n):
            _share_copy(bufs[a], ssem, rsem, a, c, (x, y, 1 - c)).start()
        token[...] = jnp.zeros_like(token)

    res = pl.pallas_call(
        body, name=name, in_specs=[HBM] * n,
        out_specs=[SEM, SEM] + [HBM] * n + [pl.BlockSpec(memory_space=pltpu.VMEM)],
        out_shape=[pltpu.SemaphoreType.DMA((n,)), pltpu.SemaphoreType.DMA((n,))]
        + [_hbm(b.shape, b.dtype) for b in arrays] + [jax.ShapeDtypeStruct(TOKEN_SHAPE, F32)],
        input_output_aliases={a: 2 + a for a in range(n)},
        compiler_params=pltpu.CompilerParams(has_side_effects=SIDE_EFFECT),
    )(*[_in_hbm(b) for b in arrays])
    return (res[0], res[1], list(res[2:2 + n])), res[-1]


def _share_wait(parts, after, *, name):
    ssem, rsem, bufs = parts
    n = len(bufs)
    after = list(after) if isinstance(after, (list, tuple)) else [after]

    def body(*refs):
        buf_refs, ssem_ref, rsem_ref = refs[:n], refs[n], refs[n + 1]
        x, y, c, _ = _place()
        for a in range(n):
            _share_copy(buf_refs[a], ssem_ref, rsem_ref, a, c, (x, y, c)).wait_send()
            _share_copy(buf_refs[a], ssem_ref, rsem_ref, a, 1 - c, (x, y, c)).wait_recv()

    return pl.pallas_call(
        body, name=name, in_specs=[HBM] * n + [SEM, SEM] + [ANY] * len(after), out_specs=[HBM] * n,
        out_shape=[_hbm(b.shape, b.dtype) for b in bufs], input_output_aliases={a: a for a in range(n)},
        compiler_params=pltpu.CompilerParams(has_side_effects=SIDE_EFFECT),
    )(*bufs, ssem, rsem, *after)


def _scatter_copy(src_ref, land_ref, ssem, rsem, k, src_slab, dst_slab, to):
    return pltpu.make_async_remote_copy(
        src_ref=src_ref.at[src_slab], dst_ref=land_ref.at[dst_slab], send_sem=ssem.at[k], recv_sem=rsem.at[k],
        device_id=to, device_id_type=MESH)


def _scatter_start(part, *, name):
    def start(src_ref, land_ref, ssem, rsem):
        x, y, c, chips = _place()
        me = 2 * x + y
        for k, (cx, cy) in enumerate(chips):
            _scatter_copy(src_ref, land_ref, ssem, rsem, k, 2 * cx + cy, me, (cx, cy, c)).start()

    return _split_start(start, part, part.shape, N_CHIPS - 1, name=name)


def _scatter_wait(parts, after, *, name):
    def wait(src_ref, land_ref, ssem_ref, rsem_ref):
        x, y, c, chips = _place()
        for k, (cx, cy) in enumerate(chips):
            idx = 2 * cx + cy
            cp = _scatter_copy(src_ref, land_ref, ssem_ref, rsem_ref, k, idx, idx, (x, y, c))
            cp.wait_send()
            cp.wait_recv()

    return _wait_call(wait, parts, after, name=name)


def _split_start(start_fn, src, land_shape, n_sems, *, name):
    def body(src_ref, land_ref, ssem, rsem, src_out, land_out, token):
        start_fn(src_ref, land_ref, ssem, rsem)
        token[...] = jnp.zeros_like(token)

    res = pl.pallas_call(
        body, name=name, in_specs=[HBM, HBM], out_specs=[SEM, SEM, HBM, HBM, pl.BlockSpec(memory_space=pltpu.VMEM)],
        out_shape=[pltpu.SemaphoreType.DMA((n_sems,)), pltpu.SemaphoreType.DMA((n_sems,)), _hbm(src.shape, src.dtype),
                   _hbm(land_shape, src.dtype), jax.ShapeDtypeStruct(TOKEN_SHAPE, F32)],
        input_output_aliases={0: 2, 1: 3}, compiler_params=pltpu.CompilerParams(has_side_effects=SIDE_EFFECT),
    )(_in_hbm(src), _in_hbm(lax.empty(land_shape, src.dtype)))
    return tuple(res[:4]), res[4]


def _sibling_copies(src_ref, land_ref, ssem, rsem, k0, groups, which, to):
    def copy(k, src, dst):
        return pltpu.make_async_remote_copy(
            src_ref=src, dst_ref=dst, send_sem=ssem.at[k], recv_sem=rsem.at[k], device_id=to, device_id_type=MESH)

    if groups == 0:
        return [copy(k0, src_ref, land_ref)]
    hw = src_ref.shape[1] // groups // 2
    return [copy(k0 + j, src_ref.at[:, pl.ds(pl.multiple_of((2 * j + which) * hw, LANES), hw)],
                 land_ref.at[:, j * hw:(j + 1) * hw]) for j in range(groups)]


def _to_sibling_start(items, *, name):
    n = len(items)
    shapes = [a.shape if g == 0 else (a.shape[0], a.shape[1] // 2) for a, g in items]
    first = [sum(max(g, 1) for _, g in items[:k]) for k in range(n + 1)]

    def body(*refs):
        srcs, lands, ssem, rsem, token = refs[:n], refs[n:2 * n], refs[2 * n], refs[2 * n + 1], refs[-1]
        x, y, c, _ = _place()
        for k, (_, g) in enumerate(items):
            for cp in _sibling_copies(srcs[k], lands[k], ssem, rsem, first[k], g, 1 - c, (x, y, 1 - c)):
                cp.start()
        token[...] = jnp.zeros_like(token)

    res = pl.pallas_call(
        body, name=name, in_specs=[HBM] * (2 * n),
        out_specs=[SEM, SEM] + [HBM] * (2 * n) + [pl.BlockSpec(memory_space=pltpu.VMEM)],
        out_shape=[pltpu.SemaphoreType.DMA((first[n],)), pltpu.SemaphoreType.DMA((first[n],))]
        + [_hbm(a.shape, a.dtype) for a, _ in items] + [_hbm(s, a.dtype) for s, (a, _) in zip(shapes, items)]
        + [jax.ShapeDtypeStruct(TOKEN_SHAPE, F32)],
        input_output_aliases={k: 2 + k for k in range(2 * n)},
        compiler_params=pltpu.CompilerParams(has_side_effects=SIDE_EFFECT),
    )(*[_in_hbm(a) for a, _ in items], *[_in_hbm(lax.empty(s, a.dtype)) for s, (a, _) in zip(shapes, items)])
    return [(res[0], res[1], first[k], g, res[2 + k], res[2 + n + k]) for k, (_, g) in enumerate(items)], res[-1]


def _from_sibling(flight, after, *, name):
    ssem, rsem, k0, groups, src, land = flight

    def wait(src_ref, land_ref, ssem_ref, rsem_ref):
        x, y, c, _ = _place()
        for cp in _sibling_copies(src_ref, land_ref, ssem_ref, rsem_ref, k0, groups, 1 - c, (x, y, c)):
            cp.wait_send()
            cp.wait_recv()

    return _wait_call(wait, (ssem, rsem, src, land), after, name=name)


def _dev_peers(x, y, c, chips):
    return [(x, y, 1 - c)] + [(cx, cy, c) for cx, cy in chips] + [(cx, cy, 1 - c) for cx, cy in chips]


def _dev_gather_start(part, *, name):
    def start(src_ref, land_ref, ssem, rsem):
        x, y, c, chips = _place()
        for k, to in enumerate(_dev_peers(x, y, c, chips)):
            pltpu.make_async_remote_copy(
                src_ref=src_ref, dst_ref=land_ref.at[4 * x + 2 * y + c], send_sem=ssem.at[k], recv_sem=rsem.at[k],
                device_id=to, device_id_type=MESH).start()

    return _split_start(start, part, (N_DEV,) + part.shape, N_DEV - 1, name=name)


def _dev_gather_wait(parts, after, *, name):
    def wait(src_ref, land_ref, ssem_ref, rsem_ref):
        x, y, c, chips = _place()
        for k, (px, py, pc) in enumerate(_dev_peers(x, y, c, chips)):
            cp = pltpu.make_async_remote_copy(
                src_ref=src_ref, dst_ref=land_ref.at[4 * px + 2 * py + pc], send_sem=ssem_ref.at[k],
                recv_sem=rsem_ref.at[k], device_id=(x, y, c), device_id_type=MESH)
            cp.wait_send()
            cp.wait_recv()

    return _wait_call(wait, parts, after, name=name)[1]


def _sibling_share_halves(arrays, *, name):
    n = len(arrays)

    def body(*refs):
        bufs = refs[n:2 * n]
        send_sems, recv_sems = refs[2 * n:]
        x, y, c, _ = _place()
        copies = []
        for a in range(n):
            mine = bufs[a].at[:, _half_cols(bufs[a].shape[1], c)]
            cp = pltpu.make_async_remote_copy(
                src_ref=mine, dst_ref=mine, send_sem=send_sems.at[a], recv_sem=recv_sems.at[a],
                device_id=(x, y, 1 - c), device_id_type=MESH)
            cp.start()
            copies.append(cp)
        for a in range(n):
            theirs = bufs[a].at[:, _half_cols(bufs[a].shape[1], 1 - c)]
            pltpu.make_async_remote_copy(
                src_ref=theirs, dst_ref=theirs, send_sem=send_sems.at[a], recv_sem=recv_sems.at[a],
                device_id=(x, y, c), device_id_type=MESH).wait_recv()
        for cp in copies:
            cp.wait_send()

    return pl.pallas_call(
        body, name=name, in_specs=[HBM] * n, out_specs=[HBM] * n,
        out_shape=[jax.ShapeDtypeStruct(h.shape, h.dtype) for h in arrays],
        input_output_aliases={a: a for a in range(n)},
        scratch_shapes=[pltpu.SemaphoreType.DMA((n,)), pltpu.SemaphoreType.DMA((n,))],
    )(*arrays)


def _pack(arrays, rows_multiple=16, width=LANES):
    flat = jnp.concatenate([a.astype(F32).reshape(-1) for a in arrays])
    total = flat.shape[0]
    rows = -(-total // width)
    rows = -(-rows // rows_multiple) * rows_multiple
    return jnp.pad(flat, (0, rows * width - total)).reshape(rows, width)


def _unpack(buf, shapes):
    flat = buf.reshape(-1)
    out, off = [], 0
    for s in shapes:
        n = math.prod(s)
        out.append(flat[off:off + n].reshape(s))
        off += n
    return out


def kernel(x, norm_pre, norm_post, gla_w_in, gla_w_gate2, gla_b_gate, gla_o_gain, gla_w_out, sgu_w_in, sgu_ln_gain, sgu_ln_bias, sgu_w_spatial, sgu_b_spatial, sgu_w_out, loss_target, m_norm_pre, m_norm_post, m_gla_w_in, m_gla_w_gate2, m_gla_b_gate, m_gla_o_gain, m_gla_w_out, m_sgu_w_in, m_sgu_ln_gain, m_sgu_ln_bias, m_sgu_w_spatial, m_sgu_b_spatial, m_sgu_w_out, v_norm_pre, v_norm_post, v_gla_w_in, v_gla_w_gate2, v_gla_b_gate, v_gla_o_gain, v_gla_w_out, v_sgu_w_in, v_sgu_ln_gain, v_sgu_ln_bias, v_sgu_w_spatial, v_sgu_b_spatial, v_sgu_w_out):
    _, t, d = x.shape
    dk = d // 2
    ws = gla_w_in.shape[2]
    wp = -(-ws // LANES) * LANES
    lay = (ws, wp)
    chip =2 * lax.axis_index("x") + lax.axis_index("y")
    core = lax.axis_index("c")
    core_idx = core.astype(jnp.int32).reshape(1)
    others = jnp.arange(N_CHIPS - 1, dtype=jnp.int32)
    others = others + (others >= chip).astype(jnp.int32)
    slots = jnp.concatenate([chip.astype(jnp.int32).reshape(1), others, core_idx])

    x0 = x[0]
    target = loss_target[0]

    wt_in_g, mt_in_g, vt_in_g = gla_w_in[0].T, m_gla_w_in[0].T, v_gla_w_in[0].T

    small_shard = _pack([gla_w_gate2[0], sgu_ln_gain[0], sgu_ln_bias[0]], rows_multiple=8, width=2 * LANES)
    own = [small_shard, jnp.pad(wt_in_g.astype(BF16), ((0, wp - ws), (0, 0)))]
    in_flight, token = _gather_start(own, name="gather_start_a")
    own_later = [gla_w_out[0].astype(BF16), sgu_w_in[0].astype(BF16), sgu_w_out[0].astype(BF16)]
    in_flight_later, token_later = _gather_start(own_later, name="gather_start_b", after=[token])
    own, in_flight = own + own_later, in_flight + in_flight_later

    def with_own(i, land):
        return lax.dynamic_update_slice(land, own[i][None], (chip, 0, 0))

    def arrived(i, after, name):
        land = _gather_wait(in_flight[i], after, name=name + "_wait")[3]
        return with_own(i, _sibling_forward(land, name=name + "_share"))

    h0 = _norm_pre(x0, norm_pre[0:1] + token[0:1, 0:1] + token_later[0:1, 0:1], name="pre0")
    g_small = arrived(0, h0, "w_small")
    wt_g = arrived(1, [g_small, wt_in_g, mt_in_g, vt_in_g], "w_gla_in").reshape(N_CHIPS * wp, d)
    shard_shapes = [gla_w_gate2.shape[1:], sgu_ln_gain.shape[1:], sgu_ln_bias.shape[1:]]
    per_chip = [_unpack(g_small[j], shard_shapes) for j in range(N_CHIPS)]
    w2_full = jnp.concatenate([p[0] for p in per_chip], axis=1)
    ln_gain = jnp.concatenate([p[1] for p in per_chip], axis=0)[None, :]
    ln_bias = jnp.concatenate([p[2] for p in per_chip], axis=0)[None, :]
    w2p = jnp.pad(w2_full, ((0, LANES - GLA_GATE_RANK), (0, 0)))

    pos_chunk = jnp.arange(SGU_BLOCK) // CHUNK
    mask = pos_chunk[:, None] >= pos_chunk[None, :]
    ws_masked = jnp.where(mask[None], sgu_w_spatial[0], 0.0)
    ws_masked_t = ws_masked.transpose(0, 2, 1)
    bs_t = sgu_b_spatial[0].T

    proj0 = _matmul(h0, wt_g, mode="nt", out_dtype=F32, name="gla_in", tn=wp)
    o0, a0, s_before, s_final = _gla_fwd(proj0, w2p, gla_b_gate, gla_o_gain, lay, name="gla_scan")
    w_out_g = arrived(2, a0, "w_gla_out").reshape(d, d)
    y0 = _matmul(a0, w_out_g, mode="nn", out_dtype=F32, name="gla_out")
    x1, h1 = _post_then_pre(x0, y0, norm_post[0:1], norm_pre[1:2], name="post0_pre1")
    g_wi_s = arrived(3, h1, "w_sgu_in")
    proj1 = _matmul(h1, g_wi_s, mode="nn", out_dtype=F32, name="sgu_in", b_shards=True)
    a1 = _sgu_fwd(proj1, ln_gain, ln_bias, ws_masked, bs_t, name="sgu_gate")
    w_out_s = arrived(4, a1, "w_sgu_out").reshape(d, d)
    acts, tok = _to_sibling_start([(a1, 0), (a0, 0), (h1, 0), (h0, 1)], name="acts_to_sibling")
    a1, a0, h1, h0 = [f[4] for f in acts]
    y1 = _matmul(a1, w_out_s, mode="nn", out_dtype=F32, name="sgu_out", after=tok)
    loss_part, dx2, dy1, d_post1 = _loss_head(x1, y1, norm_post[1:2], target, name="loss_head")

    def behind(small, token):
        return small + token[0:1, 0:1]

    def pair_gradient(a_sent, b_sent, after, shards_on, name):
        a_me, a_sib = _from_sibling(a_sent, after, name=name + "_a_wait")
        b_me, b_sib = _from_sibling(b_sent, [a_sib] + list(after), name=name + "_b_wait")
        pair = _matmul_dw_pair(a_me, a_sib, b_me, b_sib, core_idx, shards_on=shards_on,
                               name=name + "_pair")
        return _scatter_start(pair, name=name + "_start")

    def reduced(flight, after, name):
        pair, landed = _scatter_wait(flight, after, name=name + "_wait")
        return _chip_sum(pair, landed, slots, name=name + "_sum")

    (dy1_sent,), tok = _to_sibling_start([(dy1, 1)], name="dy1_to_sibling")
    dy1 = dy1_sent[4]
    da1 = _matmul(dy1, w_out_s, mode="nt", out_dtype=F32, name="d_sgu_act", after=tok)
    fl_wo_s, tok = pair_gradient(acts[0], dy1_sent, [da1], "rows", "g_sgu_out")
    dproj1, d_ws, d_bs_t, d_lg, d_lb = _sgu_bwd(da1, proj1, ln_gain, behind(ln_bias, tok), ws_masked, ws_masked_t,
                                                bs_t, name="sgu_gate_bwd")
    (dp1_sent,), tok = _to_sibling_start([(dproj1, N_CHIPS)], name="dproj1_to_sibling")
    dproj1 = dp1_sent[4]
    dh1 = _matmul_nt_shards(dproj1, g_wi_s, out_dtype=F32, name="d_sgu_h", after=tok)
    fl_wi_s, tok = pair_gradient(acts[2], dp1_sent, [dh1], "cols", "g_sgu_in")
    dx1, dy0, d_pre1, d_post0 = _mid_bwd(dx2, dh1, x1, behind(norm_pre[1:2], tok), y0, norm_post[0:1],
                                         name="pre1_post0_bwd")
    (dy0_sent,), tok = _to_sibling_start([(dy0, 1)], name="dy0_to_sibling")
    dy0 = dy0_sent[4]
    da0 = _matmul(dy0, w_out_g, mode="nt", out_dtype=F32, name="d_gla_act", after=tok)
    fl_wo_g, tok = pair_gradient(acts[1], dy0_sent, [da0], "rows", "g_gla_out")
    dproj0, d_og, d_bg, d_w2p = _gla_bwd(da0, o0, proj0, w2p, behind(gla_b_gate, tok), gla_o_gain, s_before, s_final,
                                         lay, name="gla_scan_bwd")
    early_shapes = [norm_post.shape, gla_b_gate.shape, gla_o_gain.shape, sgu_w_spatial.shape, sgu_b_spatial.shape,
                    (1, GLA_GATE_RANK, dk), (1, d), (1, d), (1, LANES)]
    early_part = _pack([jnp.concatenate([d_post0, d_post1], axis=0), d_bg, d_og, jnp.where(mask[None], d_ws, 0.0)[None],
                        d_bs_t.T[None], d_w2p[:GLA_GATE_RANK][None], d_lg, d_lb, loss_part])
    early_flight, tok = _dev_gather_start(early_part, name="small_early_start")
    (dp0_sent,), tok_sent = _to_sibling_start([(dproj0, 0)], name="dproj0_to_sibling")
    dproj0 = dp0_sent[4]
    dh0 = _matmul(dproj0, wt_g, mode="nn", out_dtype=F32, name="d_gla_h", tk=N_CHIPS * wp, after=tok_sent)
    fl_wi_g, tok_scatter = pair_gradient(dp0_sent, acts[3], [dh0, tok], "rows", "g_gla_in")
    r_wo_s = reduced(fl_wo_s, tok_scatter, "g_sgu_out")
    r_wi_s = reduced(fl_wi_s, r_wo_s, "g_sgu_in")
    r_wo_g = reduced(fl_wo_g, r_wi_s, "g_gla_out")
    sharing, tok = _share_start([r_wo_s, r_wi_s, r_wo_g], name="grads_share_a")
    grad_x, d_pre0 = _first_bwd(dx1, dh0, x0, behind(norm_pre[0:1], tok), name="pre0_bwd")

    late_part = _pack([jnp.concatenate([d_pre0, d_pre1], axis=0)])
    late_flight, tok = _dev_gather_start(late_part, name="small_late_start")

    def big_update(w, g, m, v, name, after=None):
        return [u[None] for u in _adamw(w[0], g, m[0], v[0], name=name, after=after)]

    g_wo_sgu, g_wi_sgu, g_wo_gla = _share_wait(sharing, [grad_x, tok], name="grads_share_a_wait")
    u_wo_sgu = big_update(sgu_w_out, g_wo_sgu, m_sgu_w_out, v_sgu_w_out, "adamw_sgu_w_out")
    u_wo_gla = big_update(gla_w_out, g_wo_gla, m_gla_w_out, v_gla_w_out, "adamw_gla_w_out")

    def summed_over_devices(part, flight, after, shapes, name):
        land = _dev_gather_wait(flight, after, name=name + "_wait")
        every = lax.dynamic_update_slice(land, part[None], (2 * chip + core, 0, 0))
        return _unpack(_stack_sum(every, name=name + "_sum"), shapes)

    (g_post, g_bg, g_og, g_wsp, g_bsp, g_w2_full, g_lg_full, g_lb_full, loss_vec) = summed_over_devices(
        early_part, early_flight, [u_wo_gla[1], u_wo_sgu[1]], early_shapes, "small_early")

    r_wi_g = reduced(fl_wi_g, loss_vec, "g_gla_in")
    gt_wi_gla, = _sibling_share_halves([r_wi_g], name="grads_share_b")
    u_wi_gla_t = _adamw(wt_in_g, gt_wi_gla, mt_in_g, vt_in_g, name="adamw_gla_w_in")
    u_wi_gla = [u.T[None] for u in u_wi_gla_t]
    u_wi_sgu = big_update(sgu_w_in, g_wi_sgu, m_sgu_w_in, v_sgu_w_in, "adamw_sgu_w_in", after=u_wi_gla_t[1])
    g_pre, = summed_over_devices(late_part, late_flight, u_wi_sgu[1], [norm_pre.shape], "small_late")
    loss = loss_vec[0, 0]
    g_w2 = lax.dynamic_slice_in_dim(g_w2_full, chip * (dk // N_CHIPS), dk // N_CHIPS, axis=2)
    g_lg = lax.dynamic_slice_in_dim(g_lg_full, chip * (d // N_CHIPS), d // N_CHIPS, axis=1)
    g_lb = lax.dynamic_slice_in_dim(g_lb_full, chip * (d // N_CHIPS), d // N_CHIPS, axis=1)

    small_w = [norm_pre, norm_post, gla_b_gate, gla_o_gain, sgu_w_spatial, sgu_b_spatial, gla_w_gate2, sgu_ln_gain,
               sgu_ln_bias]
    small_g = [g_pre, g_post, g_bg, g_og, g_wsp, g_bsp, g_w2, g_lg, g_lb]
    small_m = [m_norm_pre, m_norm_post, m_gla_b_gate, m_gla_o_gain, m_sgu_w_spatial, m_sgu_b_spatial, m_gla_w_gate2,
               m_sgu_ln_gain, m_sgu_ln_bias]
    small_v = [v_norm_pre, v_norm_post, v_gla_b_gate, v_gla_o_gain, v_sgu_w_spatial, v_sgu_b_spatial, v_gla_w_gate2,
               v_sgu_ln_gain, v_sgu_ln_bias]
    own_shapes = [w.shape for w in small_w]
    _, s_dl, s_m, s_v = _adamw(_pack(small_w), _pack(small_g), _pack(small_m), _pack(small_v), name="adamw_small")
    dl_s, m_s, v_s = _unpack(s_dl, own_shapes), _unpack(s_m, own_shapes), _unpack(s_v, own_shapes)

    def ordered(small, kind):
        pre, post, bg, og, wsp, bsp, w2, lg, lb = small
        return [pre, post, u_wi_gla[kind], w2, bg, og, u_wo_gla[kind], u_wi_sgu[kind], lg, lb, wsp, bsp, u_wo_sgu[kind]]

    return (loss, grad_x[None], *ordered(small_g, 0), *ordered(dl_s, 1), *ordered(m_s, 2), *ordered(v_s, 3))
```

```python
import functools
import math

import jax
import jax.numpy as jnp
from jax import lax
from jax.experimental import pallas as pl
from jax.experimental.pallas import tpu as pltpu

F32 = jnp.float32
BF16 = jnp.bfloat16
MESH = pl.DeviceIdType.MESH

EPS = 1e-6
CHUNK = 64
GLA_HEADS = 4
GLA_GATE_RANK = 16
GLA_TAU = 16.0
SGU_BLOCK = 128
SGU_GROUPS = 8
N_CHIPS = 4
N_DEV = 8
LANES = 128

ADAM_LR = 0.001
ADAM_B1 = 0.9
ADAM_B2 = 0.999
ADAM_EPS = 1e-08
ADAM_WD = 0.01
ADAM_STEP = 10

VMEM_LIMIT = 56 * 1024 * 1024


def _cparams(sem=None):
    return pltpu.CompilerParams(dimension_semantics=sem, vmem_limit_bytes=VMEM_LIMIT)


def _pick(n, cap, unit=LANES):
    best = None
    for t in range(unit, min(n, cap) + 1, unit):
        if n % t == 0:
            best = t
    assert best is not None, (n, cap, unit)
    return best


def _dot(a, b, dims):
    return lax.dot_general(a, b, (dims, ((), ())), preferred_element_type=F32)


def _dot_nn(a, b):
    return _dot(a, b, ((1,), (0,)))


def _dot_nt(a, b):
    return _dot(a, b, ((1,), (1,)))


def _dot_tn(a, b):
    return _dot(a, b, ((0,), (0,)))


def _matmul(a, b, *, mode, out_dtype, name, tm=1024, tn=512, tk=2048, b_shards=False, out_shards=False, after=None,
            out_rows=None):
    if mode == "tn":
        K, M = a.shape
    else:
        M, K = a.shape
    if b_shards:
        ns, br, bc = b.shape
        if mode == "nt":
            N, Kb = br, ns * bc
        else:
            Kb, N = br, ns * bc
    else:
        if mode == "nt":
            N, Kb = b.shape
        else:
            Kb, N = b.shape
    assert K == Kb, (a.shape, b.shape, mode)
    tm = _pick(M, tm)
    tk = _pick(K, tk)
    if b_shards and mode != "nt":
        tn = _pick(bc, tn)
    elif out_shards:
        tn = _pick(N // N_CHIPS, tn)
    else:
        tn = _pick(N, tn)
    if b_shards and mode == "nt":
        tk = _pick(bc, tk)
    nk = K // tk
    grid = (M // tm, N // tn, nk)

    if mode == "tn":
        a_spec = pl.BlockSpec((tk, tm), lambda i, j, k: (k, i))
    else:
        a_spec = pl.BlockSpec((tm, tk), lambda i, j, k: (i, k))
    if b_shards:
        if mode == "nt":
            per = bc // tk
            b_spec = pl.BlockSpec((None, tn, tk), lambda i, j, k: (k // per, j, k % per))
        else:
            per = bc // tn
            b_spec = pl.BlockSpec((None, tk, tn), lambda i, j, k: (j // per, k, j % per))
    elif mode == "nt":
        b_spec = pl.BlockSpec((tn, tk), lambda i, j, k: (j, k))
    else:
        b_spec = pl.BlockSpec((tk, tn), lambda i, j, k: (k, j))
    if out_shards:
        per_o = (N // N_CHIPS) // tn
        out_spec = pl.BlockSpec((None, tm, tn), lambda i, j, k: (j // per_o, i, j % per_o))
        out_shape = jax.ShapeDtypeStruct((N_CHIPS, M, N // N_CHIPS), out_dtype)
    else:
        out_spec = pl.BlockSpec((tm, tn), lambda i, j, k: (i, j))
        out_shape = jax.ShapeDtypeStruct((M if out_rows is None else out_rows, N), out_dtype)

    dims = {"nn": ((1,), (0,)), "nt": ((1,), (1,)), "tn": ((0,), (0,))}[mode]

    def body(a_ref, b_ref, *rest):
        o_ref, scratch = (rest[1], rest[2:]) if after is not None else (rest[0], rest[1:])
        part = _dot(a_ref[...].astype(BF16), b_ref[...].astype(BF16), dims)
        if nk == 1:
            o_ref[...] = part.astype(out_dtype)
        else:
            acc_ref, = scratch
            k = pl.program_id(2)

            @pl.when(k == 0)
            def _():
                acc_ref[...] = part

            @pl.when(k > 0)
            def _():
                acc_ref[...] += part

            @pl.when(k == nk - 1)
            def _():
                o_ref[...] = acc_ref[...].astype(out_dtype)

    extra_specs, extra_args = ([], []) if after is None else ([pl.BlockSpec(memory_space=pl.ANY)], [after])
    return pl.pallas_call(
        body, name=name, grid=grid, in_specs=[a_spec, b_spec] + extra_specs, out_specs=out_spec, out_shape=out_shape,
        scratch_shapes=[] if nk == 1 else [pltpu.VMEM((tm, tn), F32)],
        compiler_params=_cparams(("parallel", "parallel", "arbitrary")),
    )(a, b, *extra_args)


def _matmul_into_cols(a, w, which, buf, *, name, tm=1024):
    M, K = a.shape
    _, N, _ = w.shape
    tm = _pick(M, tm)

    def body(which_ref, a_ref, w_ref, buf_ref, o_ref):
        o_ref[...] = _dot_nt(a_ref[...], w_ref[...])

    grid_spec = pltpu.PrefetchScalarGridSpec(
        num_scalar_prefetch=1, grid=(M // tm,),
        in_specs=[pl.BlockSpec((tm, K), lambda i, s: (i, 0)), pl.BlockSpec((None, N, K), lambda i, s: (s[1], 0, 0)),
                  pl.BlockSpec(memory_space=pl.ANY)],
        out_specs=pl.BlockSpec((tm, N), lambda i, s: (i, s[0])))
    return pl.pallas_call(
        body, name=name, grid_spec=grid_spec, out_shape=jax.ShapeDtypeStruct(buf.shape, buf.dtype),
        input_output_aliases={3: 0}, compiler_params=_cparams(("parallel",)),
    )(which, a, w, buf)


def _matmul_nt_shards(a, b, *, out_dtype, name, tm=1024, tn=512, after=None):
    M, K = a.shape
    ns, N, kc = b.shape
    assert K == ns * kc
    tm, tn = _pick(M, tm), _pick(N, tn)

    def body(a_ref, *rest):
        b_refs, o_ref = rest[:ns], rest[ns + (after is not None)]
        acc = _dot_nt(a_ref[:, 0:kc], b_refs[0][...])
        for j in range(1, ns):
            acc += _dot_nt(a_ref[:, j * kc:(j + 1) * kc], b_refs[j][...])
        o_ref[...] = acc.astype(out_dtype)

    def shard(j):
        return pl.BlockSpec((None, tn, kc), lambda i, n: (j, n, 0))

    extra_specs, extra_args = ([], []) if after is None else ([pl.BlockSpec(memory_space=pl.ANY)], [after])
    return pl.pallas_call(
        body, name=name, grid=(M // tm, N // tn),
        in_specs=[pl.BlockSpec((tm, K), lambda i, n: (i, 0))] + [shard(j) for j in range(ns)] + extra_specs,
        out_specs=pl.BlockSpec((tm, tn), lambda i, n: (i, n)), out_shape=jax.ShapeDtypeStruct((M, N), out_dtype),
        compiler_params=_cparams(("parallel", "parallel")),
    )(a, *([b] * ns), *extra_args)


def _rstd(x):
    return lax.rsqrt(jnp.mean(x * x, axis=-1, keepdims=True) + EPS)


def _row_spec(tr, d):
    return pl.BlockSpec((tr, d), lambda i: (i, 0))


def _vec_spec(d):
    return pl.BlockSpec((1, d), lambda i: (0, 0))


def _acc_rows(ref, i, val, cols=slice(None)):
    @pl.when(i == 0)
    def _():
        ref[:, cols] = val

    @pl.when(i > 0)
    def _():
        ref[:, cols] += val


def _norm_pre(x, gain, *, name, tr=256):
    t, d = x.shape
    tr = _pick(t, tr, 8)

    def body(x_ref, g_ref, h_ref):
        xv = x_ref[...]
        h_ref[...] = (xv * _rstd(xv) * g_ref[...]).astype(BF16)

    return pl.pallas_call(
        body, name=name, grid=(t // tr,), in_specs=[_row_spec(tr, d), _vec_spec(d)], out_specs=_row_spec(tr, d),
        out_shape=jax.ShapeDtypeStruct((t, d), BF16), compiler_params=_cparams(("parallel",)),
    )(x, gain)


def _post_then_pre(x, y, post_gain, pre_gain, *, name, tr=256):
    t, d = x.shape
    tr = _pick(t, tr, 8)

    def body(x_ref, y_ref, pg_ref, ng_ref, xn_ref, h_ref):
        yv = y_ref[...]
        xn = x_ref[...] + yv * _rstd(yv) * pg_ref[...]
        xn_ref[...] = xn
        h_ref[...] = (xn * _rstd(xn) * ng_ref[...]).astype(BF16)

    return pl.pallas_call(
        body, name=name, grid=(t // tr,),
        in_specs=[_row_spec(tr, d), _row_spec(tr, d), _vec_spec(d), _vec_spec(d)],
        out_specs=[_row_spec(tr, d), _row_spec(tr, d)],
        out_shape=[jax.ShapeDtypeStruct((t, d), F32), jax.ShapeDtypeStruct((t, d), BF16)],
        compiler_params=_cparams(("parallel",)),
    )(x, y, post_gain, pre_gain)


def _norm_bwd(dy, n, r, gain):
    dn = dy * gain
    return r * (dn - n * jnp.mean(dn * n, axis=-1, keepdims=True))


def _loss_head(x, y, post_gain, target, *, name, tr=256):
    t, d = x.shape
    tr = _pick(t, tr, 8)

    def body(x_ref, y_ref, pg_ref, t_ref, loss_ref, dx_ref, dy_ref, dpg_ref):
        i = pl.program_id(0)
        yv = y_ref[...]
        r = _rstd(yv)
        n = yv * r
        err = x_ref[...] + n * pg_ref[...] - t_ref[...]
        dx = err * (1.0 / d)
        dx_ref[...] = dx
        part = 0.5 * jnp.sum(jnp.mean(err * err, axis=-1, keepdims=True), axis=0, keepdims=True)
        _acc_rows(loss_ref, i, jnp.broadcast_to(part, (1, LANES)))
        _acc_rows(dpg_ref, i, jnp.sum(dx * n, axis=0, keepdims=True))
        dy_ref[...] = _norm_bwd(dx, n, r, pg_ref[...]).astype(BF16)

    return pl.pallas_call(
        body, name=name, grid=(t // tr,),
        in_specs=[_row_spec(tr, d), _row_spec(tr, d), _vec_spec(d), _row_spec(tr, d)],
        out_specs=[_vec_spec(LANES), _row_spec(tr, d), _row_spec(tr, d), _vec_spec(d)],
        out_shape=[jax.ShapeDtypeStruct((1, LANES), F32), jax.ShapeDtypeStruct((t, d), F32),
                   jax.ShapeDtypeStruct((t, d), BF16), jax.ShapeDtypeStruct((1, d), F32)],
        compiler_params=_cparams(("arbitrary",)),
    )(x, y, post_gain, target)


def _mid_bwd(dx_out, dh, x, pre_gain, y_prev, post_gain_prev, *, name, tr=256):
    t, d = x.shape
    tr = _pick(t, tr, 8)

    def body(dxo_ref, dh_ref, x_ref, ng_ref, y_ref, pg_ref, dx_ref, dy_ref, dng_ref, dpg_ref):
        i = pl.program_id(0)
        xv = x_ref[...]
        r = _rstd(xv)
        xh = xv * r
        dhv = dh_ref[...]
        _acc_rows(dng_ref, i, jnp.sum(dhv * xh, axis=0, keepdims=True))
        dx = dxo_ref[...] + _norm_bwd(dhv, xh, r, ng_ref[...])
        dx_ref[...] = dx
        yv = y_ref[...]
        ry = _rstd(yv)
        n = yv * ry
        _acc_rows(dpg_ref, i, jnp.sum(dx * n, axis=0, keepdims=True))
        dy_ref[...] = _norm_bwd(dx, n, ry, pg_ref[...]).astype(BF16)

    return pl.pallas_call(
        body, name=name, grid=(t // tr,),
        in_specs=[_row_spec(tr, d), _row_spec(tr, d), _row_spec(tr, d), _vec_spec(d), _row_spec(tr, d), _vec_spec(d)],
        out_specs=[_row_spec(tr, d), _row_spec(tr, d), _vec_spec(d), _vec_spec(d)],
        out_shape=[jax.ShapeDtypeStruct((t, d), F32), jax.ShapeDtypeStruct((t, d), BF16),
                   jax.ShapeDtypeStruct((1, d), F32), jax.ShapeDtypeStruct((1, d), F32)],
        compiler_params=_cparams(("arbitrary",)),
    )(dx_out, dh, x, pre_gain, y_prev, post_gain_prev)


def _first_bwd(dx_out, dh, x, pre_gain, *, name, tr=256):
    t, d = x.shape
    tr = _pick(t, tr, 8)

    def body(dxo_ref, dh_ref, x_ref, ng_ref, dx_ref, dng_ref):
        i = pl.program_id(0)
        xv = x_ref[...]
        r = _rstd(xv)
        xh = xv * r
        dhv = dh_ref[...]
        _acc_rows(dng_ref, i, jnp.sum(dhv * xh, axis=0, keepdims=True))
        dx_ref[...] = dxo_ref[...] + _norm_bwd(dhv, xh, r, ng_ref[...])

    return pl.pallas_call(
        body, name=name, grid=(t // tr,),
        in_specs=[_row_spec(tr, d), _row_spec(tr, d), _row_spec(tr, d), _vec_spec(d)],
        out_specs=[_row_spec(tr, d), _vec_spec(d)],
        out_shape=[jax.ShapeDtypeStruct((t, d), F32), jax.ShapeDtypeStruct((1, d), F32)],
        compiler_params=_cparams(("arbitrary",)),
    )(dx_out, dh, x, pre_gain)


def _sigmoid(x):
    return 1.0 / (1.0 + jnp.exp(-x))


def _log_sigmoid(x):
    return jnp.minimum(x, 0.0) - jnp.log(1.0 + jnp.exp(-jnp.abs(x)))


_GELU_C = math.sqrt(2.0 / math.pi)


_GELU_A = 0.044715


def _gelu_parts(x, with_grad=True):
    x2 = x * x
    h = 0.5 * jnp.tanh(x * (_GELU_C + (_GELU_C * _GELU_A) * x2)) + 0.5
    val = x * h
    if not with_grad:
        return val, None
    return val, h * (1.0 + (1.0 - h) * (x * (2.0 * _GELU_C + (6.0 * _GELU_C * _GELU_A) * x2)))


def _split3(x):
    hi = x.astype(BF16)
    r1 = x - hi.astype(F32)
    mid = r1.astype(BF16)
    lo = (r1 - mid.astype(F32)).astype(BF16)
    return hi, mid, lo


def _tri_matmul(tri_bf16, x):
    hi, mid, lo = _split3(x)
    return _dot_nn(tri_bf16, hi) + _dot_nn(tri_bf16, mid) + _dot_nn(tri_bf16, lo)


def _gla_dims(d):
    dk, dv = d // 2, d
    return dk, dv, dk // GLA_HEADS, dv // GLA_HEADS


def _col_pieces(a, b, lay):
    ws, wp = lay
    out = []
    while a < b:
        j = a // ws
        end = min(b, (j + 1) * ws)
        out.append((j * wp + a - j * ws, end - a))
        a = end
    return out


def _load_cols(ref, a, b, lay):
    parts = [ref[:, s:s + n] for s, n in _col_pieces(a, b, lay)]
    return parts[0] if len(parts) == 1 else jnp.concatenate(parts, axis=1)


def _store_cols(ref, a, val, lay):
    off = 0
    for s, n in _col_pieces(a, a + val.shape[1], lay):
        ref[:, s:s + n] = val[:, off:off + n]
        off += n


def _gate_window(c_r, lay):
    (start, _), = _col_pieces(c_r, c_r + GLA_GATE_RANK, lay)
    assert (start % lay[1]) + LANES <= lay[1]
    return slice(start, start + LANES)


def _gla_gates(glr, k, w2_ref, b_ref):
    z = _dot_nn(glr.astype(BF16), w2_ref[...].astype(BF16)) + b_ref[...]
    la = _log_sigmoid(z) * (1.0 / GLA_TAU)
    row = lax.broadcasted_iota(jnp.int32, (CHUNK, CHUNK), 0)
    col = lax.broadcasted_iota(jnp.int32, (CHUNK, CHUNK), 1)
    incl = (row >= col).astype(BF16)
    bcum = _tri_matmul(incl, la)
    b_end = bcum[CHUNK - 1:CHUNK, :]
    e_rest = jnp.exp(b_end - bcum)
    return z, e_rest, k * e_rest, jnp.exp(b_end)


def _gla_fwd(proj, w2p, b_gate, o_gain, lay, *, name):
    t, wcols = proj.shape
    d = o_gain.shape[1]
    dk, dv, dkh, dvh = _gla_dims(d)
    nc = t // CHUNK
    c_k, c_v, c_g, c_r = dk, 2 * dk, 2 * dk + dv, 2 * dk + 2 * dv
    scale = dkh ** -0.5

    def body(p_ref, w2_ref, b_ref, og_ref, o_ref, a_ref, sb_ref, sfin_ref, s_ref):
        i = pl.program_id(0)

        @pl.when(i == 0)
        def _():
            s_ref[...] = jnp.zeros_like(s_ref)

        q = _load_cols(p_ref, 0, dk, lay) * scale
        k = _load_cols(p_ref, c_k, c_k + dk, lay)
        glr = p_ref[:, _gate_window(c_r, lay)]
        _, _, kdec, decay = _gla_gates(glr, k, w2_ref, b_ref)
        for h in range(GLA_HEADS):
            ks = slice(h * dkh, (h + 1) * dkh)
            vs = slice(h * dvh, (h + 1) * dvh)
            v_h = _load_cols(p_ref, c_v + h * dvh, c_v + (h + 1) * dvh, lay)
            g_h = _load_cols(p_ref, c_g + h * dvh, c_g + (h + 1) * dvh, lay)
            s_old = s_ref[h]
            sb_ref[0, h] = s_old
            s_new = s_old * decay[:, ks] + _dot_tn(v_h.astype(BF16), kdec[:, ks].astype(BF16))
            s_ref[h] = s_new
            o_h = _dot_nt(q[:, ks].astype(BF16), s_new.astype(BF16))
            o_ref[:, vs] = o_h
            on = o_h * _rstd(o_h)
            a_ref[:, vs] = (on * og_ref[:, vs] * (g_h * _sigmoid(g_h))).astype(BF16)

        @pl.when(i == nc - 1)
        def _():
            sfin_ref[...] = s_ref[...]

    full = lambda *shape: pl.BlockSpec(shape, lambda i: (0,) * len(shape))
    return pl.pallas_call(
        body, name=name, grid=(nc,),
        in_specs=[pl.BlockSpec((CHUNK, wcols), lambda i: (i, 0)), full(LANES, dk), full(1, dk), full(1, dv)],
        out_specs=[pl.BlockSpec((CHUNK, dv), lambda i: (i, 0)), pl.BlockSpec((CHUNK, dv), lambda i: (i, 0)),
                   pl.BlockSpec((1, GLA_HEADS, dvh, dkh), lambda i: (i, 0, 0, 0)), full(GLA_HEADS, dvh, dkh)],
        out_shape=[jax.ShapeDtypeStruct((t, dv), F32), jax.ShapeDtypeStruct((t, dv), BF16),
                   jax.ShapeDtypeStruct((nc, GLA_HEADS, dvh, dkh), F32),
                   jax.ShapeDtypeStruct((GLA_HEADS, dvh, dkh), F32)],
        scratch_shapes=[pltpu.VMEM((GLA_HEADS, dvh, dkh), F32)],
        compiler_params=_cparams(("arbitrary",)),
    )(proj, w2p, b_gate, o_gain)


def _gla_bwd(da, o, proj, w2p, b_gate, o_gain, s_before, s_final, lay, *, name):
    t, wcols = proj.shape
    d = o_gain.shape[1]
    dk, dv, dkh, dvh = _gla_dims(d)
    nc = t // CHUNK
    c_k, c_v, c_g, c_r = dk, 2 * dk, 2 * dk + dv, 2 * dk + 2 * dv
    scale = dkh ** -0.5

    def body(da_ref, o_ref, p_ref, w2_ref, b_ref, og_ref, sb_ref, sfin_ref,
             dp_ref, dog_ref, db_ref, dw2_ref, s_ref, gc_ref, dkd_ref):
        i = pl.program_id(0)

        @pl.when(i == 0)
        def _():
            s_ref[...] = sfin_ref[...]
            gc_ref[...] = jnp.zeros_like(gc_ref)

        ws, wp = lay
        for j in range(N_CHIPS):
            dp_ref[:, j * wp + ws:(j + 1) * wp] = jnp.zeros((CHUNK, wp - ws), BF16)
        q = _load_cols(p_ref, 0, dk, lay) * scale
        k = _load_cols(p_ref, c_k, c_k + dk, lay)
        glr = p_ref[:, _gate_window(c_r, lay)]
        z, e_rest, kdec, decay = _gla_gates(glr, k, w2_ref, b_ref)
        ddecay = []
        for h in range(GLA_HEADS):
            ks = slice(h * dkh, (h + 1) * dkh)
            vs = slice(h * dvh, (h + 1) * dvh)
            v_h = _load_cols(p_ref, c_v + h * dvh, c_v + (h + 1) * dvh, lay)
            g_h = _load_cols(p_ref, c_g + h * dvh, c_g + (h + 1) * dvh, lay)
            da_h = da_ref[:, vs]
            o_h = o_ref[:, vs]
            og_h = og_ref[:, vs]
            r = _rstd(o_h)
            on = o_h * r
            sg = _sigmoid(g_h)
            silu = g_h * sg
            _acc_rows(dog_ref, i, jnp.sum(da_h * silu * on, axis=0, keepdims=True), vs)
            _store_cols(dp_ref, c_g + h * dvh, (da_h * (on * og_h) * (sg * (1.0 + g_h * (1.0 - sg)))).astype(BF16),
                        lay)
            don = da_h * silu * og_h
            do_h = (r * (don - on * jnp.mean(don * on, axis=-1, keepdims=True))).astype(BF16)
            s_cur = s_ref[h]
            _store_cols(dp_ref, h * dkh, (_dot_nn(do_h, s_cur.astype(BF16)) * scale).astype(BF16), lay)
            g_tot = gc_ref[h] + _dot_tn(do_h, q[:, ks].astype(BF16))
            g_bf = g_tot.astype(BF16)
            dkd_ref[:, ks] = _dot_nn(v_h.astype(BF16), g_bf)
            _store_cols(dp_ref, c_v + h * dvh, _dot_nt(kdec[:, ks].astype(BF16), g_bf).astype(BF16), lay)
            s_prev = sb_ref[0, h]
            ddecay.append(jnp.sum(g_tot * s_prev, axis=0, keepdims=True))
            gc_ref[h] = g_tot * decay[:, ks]
            s_ref[h] = s_prev
        dkdec = dkd_ref[...]
        _store_cols(dp_ref, c_k, (dkdec * e_rest).astype(BF16), lay)
        d_e = dkdec * kdec
        row = lax.broadcasted_iota(jnp.int32, (CHUNK, CHUNK), 0)
        col = lax.broadcasted_iota(jnp.int32, (CHUNK, CHUNK), 1)
        excl = (row > col).astype(BF16)
        dla = jnp.concatenate(ddecay, axis=1) * decay + _tri_matmul(excl, d_e)
        dz = dla * (1.0 / GLA_TAU) * (1.0 - _sigmoid(z))
        _acc_rows(db_ref, i, jnp.sum(dz, axis=0, keepdims=True))
        dz_bf = dz.astype(BF16)
        dw2 = _dot_tn(glr.astype(BF16), dz_bf)

        @pl.when(i == 0)
        def _():
            dw2_ref[...] = dw2

        @pl.when(i > 0)
        def _():
            dw2_ref[...] += dw2

        dp_ref[:, _gate_window(c_r, lay)] = _dot_nt(dz_bf, w2_ref[...].astype(BF16)).astype(BF16)

    rev = lambda i: (nc - 1 - i, 0)
    full = lambda *shape: pl.BlockSpec(shape, lambda i: (0,) * len(shape))
    return pl.pallas_call(
        body, name=name, grid=(nc,),
        in_specs=[pl.BlockSpec((CHUNK, dv), rev), pl.BlockSpec((CHUNK, dv), rev), pl.BlockSpec((CHUNK, wcols), rev),
                  full(LANES, dk), full(1, dk), full(1, dv),
                  pl.BlockSpec((1, GLA_HEADS, dvh, dkh), lambda i: (nc - 1 - i, 0, 0, 0)), full(GLA_HEADS, dvh, dkh)],
        out_specs=[pl.BlockSpec((CHUNK, wcols), rev), full(1, dv), full(1, dk), full(LANES, dk)],
        out_shape=[jax.ShapeDtypeStruct((t, wcols), BF16), jax.ShapeDtypeStruct((1, dv), F32),
                   jax.ShapeDtypeStruct((1, dk), F32), jax.ShapeDtypeStruct((LANES, dk), F32)],
        scratch_shapes=[pltpu.VMEM((GLA_HEADS, dvh, dkh), F32), pltpu.VMEM((GLA_HEADS, dvh, dkh), F32),
                        pltpu.VMEM((CHUNK, dk), F32)],
        compiler_params=_cparams(("arbitrary",)),
    )(da, o, proj, w2p, b_gate, o_gain, s_before, s_final)


def _sgu_mid(p_ref, lg_ref, lb_ref, ws_ref, bst_ref, w, with_grad=True):
    gd = w // SGU_GROUPS
    u_act, du_fac = _gelu_parts(p_ref[:, 0:w], with_grad)
    vf, dv_fac = _gelu_parts(p_ref[:, w:2 * w], with_grad)
    mu = jnp.mean(vf, axis=-1, keepdims=True)
    cen = vf - mu
    rstd = lax.rsqrt(jnp.mean(cen * cen, axis=-1, keepdims=True) + EPS)
    xh = cen * rstd
    vn = (xh * lg_ref[...] + lb_ref[...]).astype(BF16)
    vs = [_dot_nn(ws_ref[g].astype(BF16), vn[:, g * gd:(g + 1) * gd]) + bst_ref[:, g:g + 1]
          for g in range(SGU_GROUPS)]
    return u_act, du_fac, dv_fac, rstd, xh, vn, vs


def _sgu_fwd(proj, ln_gain, ln_bias, ws_masked, bs_t, *, name):
    t, w3 = proj.shape
    w = w3 // 3
    gd = w // SGU_GROUPS
    nb = t // SGU_BLOCK

    def body(p_ref, lg_ref, lb_ref, ws_ref, bst_ref, a_ref):
        u_act, _, _, _, _, _, vs = _sgu_mid(p_ref, lg_ref, lb_ref, ws_ref, bst_ref, w, with_grad=False)
        for g in range(SGU_GROUPS):
            cs = slice(g * gd, (g + 1) * gd)
            gate = p_ref[:, 2 * w + g * gd:2 * w + (g + 1) * gd]
            a_ref[:, cs] = (u_act[:, cs] * vs[g] * (gate * _sigmoid(gate))).astype(BF16)

    full = lambda *shape: pl.BlockSpec(shape, lambda i: (0,) * len(shape))
    return pl.pallas_call(
        body, name=name, grid=(nb,),
        in_specs=[pl.BlockSpec((SGU_BLOCK, w3), lambda i: (i, 0)), full(1, w), full(1, w),
                  full(SGU_GROUPS, SGU_BLOCK, SGU_BLOCK), full(SGU_BLOCK, SGU_GROUPS)],
        out_specs=pl.BlockSpec((SGU_BLOCK, w), lambda i: (i, 0)),
        out_shape=jax.ShapeDtypeStruct((t, w), BF16),
        compiler_params=_cparams(("parallel",)),
    )(proj, ln_gain, ln_bias, ws_masked, bs_t)


def _sgu_bwd(da, proj, ln_gain, ln_bias, ws_masked, ws_masked_t, bs_t, *, name):
    t, w3 = proj.shape
    w = w3 // 3
    gd = w // SGU_GROUPS
    nb = t // SGU_BLOCK

    def body(da_ref, p_ref, lg_ref, lb_ref, ws_ref, wst_ref, bst_ref, dp_ref, dws_ref, dbst_ref, dlg_ref, dlb_ref,
             dvn_ref):
        i = pl.program_id(0)
        u_act, du_fac, dv_fac, rstd, xh, vn, vs = _sgu_mid(p_ref, lg_ref, lb_ref, ws_ref, bst_ref, w)
        for g in range(SGU_GROUPS):
            cs = slice(g * gd, (g + 1) * gd)
            gate = p_ref[:, 2 * w + g * gd:2 * w + (g + 1) * gd]
            sg = _sigmoid(gate)
            silu = gate * sg
            da_g = da_ref[:, cs]
            ua_g = u_act[:, cs]
            dp_ref[:, cs] = (da_g * vs[g] * silu * du_fac[:, cs]).astype(BF16)
            dp_ref[:, 2 * w + g * gd:2 * w + (g + 1) * gd] = (
                da_g * ua_g * vs[g] * (sg * (1.0 + gate * (1.0 - sg)))).astype(BF16)
            dvs = da_g * ua_g * silu
            dvs_bf = dvs.astype(BF16)
            dvn_ref[:, cs] = _dot_nn(wst_ref[g].astype(BF16), dvs_bf)
            dws = _dot_nt(dvs_bf, vn[:, cs])
            dbs = jnp.sum(dvs, axis=1, keepdims=True)

            @pl.when(i == 0)
            def _():
                dws_ref[g] = dws
                dbst_ref[:, g:g + 1] = dbs

            @pl.when(i > 0)
            def _():
                dws_ref[g] += dws
                dbst_ref[:, g:g + 1] += dbs

        dvn = dvn_ref[...]
        _acc_rows(dlg_ref, i, jnp.sum(dvn * xh, axis=0, keepdims=True))
        _acc_rows(dlb_ref, i, jnp.sum(dvn, axis=0, keepdims=True))
        dxh = dvn * lg_ref[...]
        dvf = rstd * (dxh - jnp.mean(dxh, axis=-1, keepdims=True)
                      - xh * jnp.mean(dxh * xh, axis=-1, keepdims=True))
        dp_ref[:, w:2 * w] = (dvf * dv_fac).astype(BF16)

    full = lambda *shape: pl.BlockSpec(shape, lambda i: (0,) * len(shape))
    return pl.pallas_call(
        body, name=name, grid=(nb,),
        in_specs=[pl.BlockSpec((SGU_BLOCK, w), lambda i: (i, 0)), pl.BlockSpec((SGU_BLOCK, w3), lambda i: (i, 0)),
                  full(1, w), full(1, w), full(SGU_GROUPS, SGU_BLOCK, SGU_BLOCK),
                  full(SGU_GROUPS, SGU_BLOCK, SGU_BLOCK), full(SGU_BLOCK, SGU_GROUPS)],
        out_specs=[pl.BlockSpec((SGU_BLOCK, w3), lambda i: (i, 0)), full(SGU_GROUPS, SGU_BLOCK, SGU_BLOCK),
                   full(SGU_BLOCK, SGU_GROUPS), full(1, w), full(1, w)],
        out_shape=[jax.ShapeDtypeStruct((t, w3), BF16), jax.ShapeDtypeStruct((SGU_GROUPS, SGU_BLOCK, SGU_BLOCK), F32),
                   jax.ShapeDtypeStruct((SGU_BLOCK, SGU_GROUPS), F32), jax.ShapeDtypeStruct((1, w), F32),
                   jax.ShapeDtypeStruct((1, w), F32)],
        scratch_shapes=[pltpu.VMEM((SGU_BLOCK, w), F32)],
        compiler_params=_cparams(("arbitrary",)),
    )(da, proj, ln_gain, ln_bias, ws_masked, ws_masked_t, bs_t)


def _tile2d(rows, cols, block_bytes, row_unit):
    if rows % row_unit == 0:
        return _pick(rows, max(row_unit, block_bytes // (4 * cols)), row_unit), cols
    return rows, _pick(cols, max(LANES, block_bytes // (4 * rows)))


def _adamw(w, g, m, v, *, name, block_bytes=1 << 20, after=None):
    rows, cols = w.shape
    tr, tc = _tile2d(rows, cols, block_bytes, 8)
    g_rows = g.shape[0]
    assert g_rows == rows or tr == rows
    extra_specs, extra_args = ([], []) if after is None else ([pl.BlockSpec(memory_space=pl.ANY)], [after])

    def body(w_ref, g_ref, m_ref, v_ref, *rest):
        go_ref, d_ref, mo_ref, vo_ref = rest[len(extra_args):]
        gv = g_ref[0:tr, :]
        go_ref[...] = gv
        mn = ADAM_B1 * m_ref[...] + (1.0 - ADAM_B1) * gv
        vn = ADAM_B2 * v_ref[...] + (1.0 - ADAM_B2) * (gv * gv)
        m_hat = mn / (1.0 - ADAM_B1 ** ADAM_STEP)
        v_hat = vn / (1.0 - ADAM_B2 ** ADAM_STEP)
        d_ref[...] = -ADAM_LR * (m_hat / (jnp.sqrt(v_hat) + ADAM_EPS) + ADAM_WD * w_ref[...])
        mo_ref[...] = mn
        vo_ref[...] = vn

    spec = pl.BlockSpec((tr, tc), lambda i, j: (i, j))
    g_spec = spec if g_rows == rows else pl.BlockSpec((g_rows, tc), lambda i, j: (0, j))
    return pl.pallas_call(
        body, name=name, grid=(rows // tr, cols // tc), in_specs=[spec, g_spec, spec, spec] + extra_specs,
        out_specs=[spec] * 4, out_shape=[jax.ShapeDtypeStruct((rows, cols), F32)] * 4,
        compiler_params=_cparams(("parallel", "parallel")),
    )(w, g, m, v, *extra_args)


def _matmul_dw_pair(a_me, a_sib, b_me, b_sib, core_idx, *, shards_on, name, after=None, part=(0, 1)):
    T, M = a_me.shape
    N = b_me.shape[1]
    if shards_on == "rows":
        p, count = part
        tm, hc = M // N_CHIPS, N // 2
        hp = hc // count
        tn = _pick(hp, 512)
        per = hp // tn
        grid = (N_CHIPS, per)
        a_spec = pl.BlockSpec((T, tm), lambda i, n, h: (0, i))
        b_me_spec = pl.BlockSpec((T, tn), lambda i, n, h: (0, (h[0] * count + p) * per + n))
        b_sib_spec = pl.BlockSpec((T, tn), lambda i, n, h: (0, p * per + n))
        out_spec = pl.BlockSpec((None, tm, tn), lambda i, n, h: (i, 0, n))
        out_shape = jax.ShapeDtypeStruct((N_CHIPS, tm, hp), BF16)
    else:
        tm, hc = _pick(M, 1024), N // N_CHIPS // 2
        grid = (M // tm, N_CHIPS)
        a_spec = pl.BlockSpec((T, tm), lambda i, j, h: (0, i))
        b_me_spec = pl.BlockSpec((T, hc), lambda i, j, h: (0, 2 * j + h[0]))
        b_sib_spec = pl.BlockSpec((T, hc), lambda i, j, h: (0, j))
        out_spec = pl.BlockSpec((None, tm, hc), lambda i, j, h: (j, i, 0))
        out_shape = jax.ShapeDtypeStruct((N_CHIPS, M, hc), BF16)
    extra_specs, extra_args = ([], []) if after is None else ([pl.BlockSpec(memory_space=pl.ANY)], [after])

    def body(h_ref, am_ref, as_ref, bm_ref, bs_ref, *rest):
        o_ref = rest[len(extra_args)]
        o_ref[...] = (_dot_tn(am_ref[...], bm_ref[...]) + _dot_tn(as_ref[...], bs_ref[...])).astype(BF16)

    grid_spec = pltpu.PrefetchScalarGridSpec(
        num_scalar_prefetch=1, grid=grid, in_specs=[a_spec, a_spec, b_me_spec, b_sib_spec] + extra_specs,
        out_specs=out_spec)
    return pl.pallas_call(
        body, name=name, grid_spec=grid_spec, out_shape=out_shape, compiler_params=_cparams(("parallel", "parallel")),
    )(core_idx, a_me, a_sib, b_me, b_sib, *extra_args)


def _chip_sum(pair, landed, slots, *, name, block_bytes=1 << 20, part=(0, 1), into=None):
    p, count = part
    _, r, hp = pair.shape
    tr, tc = _tile2d(r, hp, block_bytes, 16)
    ncb = hp // tc
    extra_specs, extra_args = ([], []) if into is None else ([pl.BlockSpec(memory_space=pl.ANY)], [into])

    def body(s_ref, own_ref, l0_ref, l1_ref, l2_ref, *rest):
        rest[-1][...] = ((own_ref[...].astype(F32) + l0_ref[...].astype(F32)) + l1_ref[...].astype(F32)
                         ) + l2_ref[...].astype(F32)

    def slab(which):
        return pl.BlockSpec((None, tr, tc), lambda i, k, s: (s[which], i, k))

    grid_spec = pltpu.PrefetchScalarGridSpec(
        num_scalar_prefetch=1, grid=(r // tr, ncb),
        in_specs=[slab(0), slab(1), slab(2), slab(3)] + extra_specs,
        out_specs=pl.BlockSpec((tr, tc), lambda i, k, s: (i, (s[4] * count + p) * ncb + k)))
    return pl.pallas_call(
        body, name=name, grid_spec=grid_spec, out_shape=jax.ShapeDtypeStruct((r, 2 * hp * count), F32),
        input_output_aliases={} if into is None else {5: 0},
        compiler_params=_cparams(("parallel", "parallel")),
    )(slots, pair, landed, landed, landed, *extra_args)


def _stack_sum(x, *, name, out_dtype=F32, block_bytes=1 << 20):
    s, r, c = x.shape
    tr = _pick(r, max(8, block_bytes // (4 * c)), 16) if r % 16 == 0 else r

    def body(x_ref, o_ref):
        acc = x_ref[0].astype(F32)
        for j in range(1, s):
            acc = acc + x_ref[j].astype(F32)
        o_ref[...] = acc.astype(out_dtype)

    return pl.pallas_call(
        body, name=name, grid=(r // tr,),
        in_specs=[pl.BlockSpec((s, tr, c), lambda i: (0, i, 0))], out_specs=pl.BlockSpec((tr, c), lambda i: (i, 0)),
        out_shape=jax.ShapeDtypeStruct((r, c), out_dtype), compiler_params=_cparams(("parallel",)),
    )(x)


HBM = pl.BlockSpec(memory_space=pltpu.HBM)


def _place():
    x, y, c = lax.axis_index("x"), lax.axis_index("y"), lax.axis_index("c")
    other_chips = [(1 - x, y), (x, 1 - y), (1 - x, 1 - y)]
    return x, y, c, other_chips


def _half_cols(cols, which):
    hc = cols // 2
    return pl.ds(pl.multiple_of(which * hc, LANES), hc)


SEM = pl.BlockSpec(memory_space=pltpu.SEMAPHORE)
ANY = pl.BlockSpec(memory_space=pl.ANY)
SIDE_EFFECT = pltpu.SideEffectType.DATAFLOW_SIDE_EFFECTING
TOKEN_SHAPE = (8, LANES)


def _hbm(shape, dtype):
    return pltpu.HBM(shape, dtype)


def _in_hbm(a):
    return pltpu.with_memory_space_constraint(a, pltpu.HBM)


def _gather_copy(src_ref, land_ref, ssem, rsem, k, chip_of_block, to, c):
    cols = src_ref.shape[1]
    return pltpu.make_async_remote_copy(
        src_ref=src_ref.at[:, _half_cols(cols, c)], dst_ref=land_ref.at[chip_of_block, :, _half_cols(cols, c)],
        send_sem=ssem.at[k], recv_sem=rsem.at[k], device_id=to, device_id_type=MESH)


def _gather_start(shards, *, name, after=()):
    n = len(shards)
    after = list(after)

    def body(*refs):
        srcs, lands = refs[:n], refs[n:2 * n]
        outs = refs[2 * n + len(after):]
        token = outs[-1]
        x, y, c, chips = _place()
        me = 2 * x + y
        for a in range(n):
            ssem, rsem = outs[4 * a], outs[4 * a + 1]
            for k, (cx, cy) in enumerate(chips):
                _gather_copy(srcs[a], lands[a], ssem, rsem, k, me, (cx, cy, c), c).start()
        token[...] = jnp.zeros_like(token)

    out_shape, out_specs, aliases = [], [], {}
    for a, s in enumerate(shards):
        out_shape += [pltpu.SemaphoreType.DMA((3,)), pltpu.SemaphoreType.DMA((3,)), _hbm(s.shape, s.dtype),
                      _hbm((N_CHIPS,) + s.shape, s.dtype)]
        out_specs += [SEM, SEM, HBM, HBM]
        aliases[a] = 4 * a + 2
        aliases[n + a] = 4 * a + 3
    out_shape.append(jax.ShapeDtypeStruct(TOKEN_SHAPE, F32))
    out_specs.append(pl.BlockSpec(memory_space=pltpu.VMEM))
    lands = [_in_hbm(lax.empty((N_CHIPS,) + s.shape, s.dtype)) for s in shards]
    res = pl.pallas_call(
        body, name=name, in_specs=[HBM] * (2 * n) + [ANY] * len(after), out_specs=out_specs, out_shape=out_shape,
        input_output_aliases=aliases, compiler_params=pltpu.CompilerParams(has_side_effects=SIDE_EFFECT),
    )(*[_in_hbm(s) for s in shards], *lands, *after)
    return [tuple(res[4 * a:4 * a + 4]) for a in range(n)], res[-1]


def _wait_call(wait_fn, parts, after, *, name):
    ssem, rsem, src, land = parts
    after = list(after) if isinstance(after, (list, tuple)) else [after]

    def body(src_ref, land_ref, ssem_ref, rsem_ref, *rest):
        wait_fn(src_ref, land_ref, ssem_ref, rsem_ref)

    return pl.pallas_call(
        body, name=name, in_specs=[HBM, HBM, SEM, SEM] + [ANY] * len(after), out_specs=[HBM, HBM],
        out_shape=[_hbm(src.shape, src.dtype), _hbm(land.shape, land.dtype)], input_output_aliases={0: 0, 1: 1},
        compiler_params=pltpu.CompilerParams(has_side_effects=SIDE_EFFECT),
    )(src, land, ssem, rsem, *after)


ALL_CHIPS = (0, 1, 2)


def _gather_wait(parts, after, *, name, ks=ALL_CHIPS):
    def wait(src_ref, land_ref, ssem_ref, rsem_ref):
        x, y, c, chips = _place()
        for k in ks:
            cx, cy = chips[k]
            cp = _gather_copy(src_ref, land_ref, ssem_ref, rsem_ref, k, 2 * cx + cy, (x, y, c), c)
            cp.wait_send()
            cp.wait_recv()

    src, land = _wait_call(wait, parts, after, name=name)
    return (parts[0], parts[1], src, land)


def _forward_copy(buf_ref, ssem, rsem, k, slab, which, to):
    part = buf_ref.at[slab, :, _half_cols(buf_ref.shape[2], which)]
    return pltpu.make_async_remote_copy(
        src_ref=part, dst_ref=part, send_sem=ssem.at[k], recv_sem=rsem.at[k], device_id=to, device_id_type=MESH)


def _sibling_forward(land, *, name, ks=ALL_CHIPS):
    def body(_, buf, send_sems, recv_sems):
        x, y, c, chips = _place()
        copies = []
        for k in ks:
            cx, cy = chips[k]
            cp = _forward_copy(buf, send_sems, recv_sems, k, 2 * cx + cy, c, (x, y, 1 - c))
            cp.start()
            copies.append(cp)
        for k in ks:
            cx, cy = chips[k]
            _forward_copy(buf, send_sems, recv_sems, k, 2 * cx + cy, 1 - c, (x, y, c)).wait_recv()
        for cp in copies:
            cp.wait_send()

    return pl.pallas_call(
        body, name=name, in_specs=[HBM], out_specs=HBM, out_shape=jax.ShapeDtypeStruct(land.shape, land.dtype),
        input_output_aliases={0: 0},
        scratch_shapes=[pltpu.SemaphoreType.DMA((3,)), pltpu.SemaphoreType.DMA((3,))],
    )(land)


def _share_copy(buf_ref, ssem, rsem, a, which, to):
    part = buf_ref.at[:, _half_cols(buf_ref.shape[1], which)]
    return pltpu.make_async_remote_copy(
        src_ref=part, dst_ref=part, send_sem=ssem.at[a], recv_sem=rsem.at[a], device_id=to, device_id_type=MESH)


def _share_start(arrays, *, name):
    n = len(arrays)

    def body(*refs):
        bufs, ssem, rsem, token = refs[:n], refs[n], refs[n + 1], refs[-1]
        x, y, c, _ = _place()
        for a in range(n):
            _share_copy(bufs[a], ssem, rsem, a, c, (x, y, 1 - c)).start()
        token[...] = jnp.zeros_like(token)

    res = pl.pallas_call(
        body, name=name, in_specs=[HBM] * n,
        out_specs=[SEM, SEM] + [HBM] * n + [pl.BlockSpec(memory_space=pltpu.VMEM)],
        out_shape=[pltpu.SemaphoreType.DMA((n,)), pltpu.SemaphoreType.DMA((n,))]
        + [_hbm(b.shape, b.dtype) for b in arrays] + [jax.ShapeDtypeStruct(TOKEN_SHAPE, F32)],
        input_output_aliases={a: 2 + a for a in range(n)},
        compiler_params=pltpu.CompilerParams(has_side_effects=SIDE_EFFECT),
    )(*[_in_hbm(b) for b in arrays])
    return (res[0], res[1], list(res[2:2 + n])), res[-1]


def _share_wait(parts, after, *, name):
    ssem, rsem, bufs = parts
    n = len(bufs)
    after = list(after) if isinstance(after, (list, tuple)) else [after]

    def body(*refs):
        buf_refs, ssem_ref, rsem_ref = refs[:n], refs[n], refs[n + 1]
        x, y, c, _ = _place()
        for a in range(n):
            _share_copy(buf_refs[a], ssem_ref, rsem_ref, a, c, (x, y, c)).wait_send()
            _share_copy(buf_refs[a], ssem_ref, rsem_ref, a, 1 - c, (x, y, c)).wait_recv()

    return pl.pallas_call(
        body, name=name, in_specs=[HBM] * n + [SEM, SEM] + [ANY] * len(after), out_specs=[HBM] * n,
        out_shape=[_hbm(b.shape, b.dtype) for b in bufs], input_output_aliases={a: a for a in range(n)},
        compiler_params=pltpu.CompilerParams(has_side_effects=SIDE_EFFECT),
    )(*bufs, ssem, rsem, *after)


def _scatter_copy(src_ref, land_ref, ssem, rsem, k, src_slab, dst_slab, to):
    return pltpu.make_async_remote_copy(
        src_ref=src_ref.at[src_slab], dst_ref=land_ref.at[dst_slab], send_sem=ssem.at[k], recv_sem=rsem.at[k],
        device_id=to, device_id_type=MESH)


def _scatter_start(part, *, name):
    def start(src_ref, land_ref, ssem, rsem):
        x, y, c, chips = _place()
        me = 2 * x + y
        for k, (cx, cy) in enumerate(chips):
            _scatter_copy(src_ref, land_ref, ssem, rsem, k, 2 * cx + cy, me, (cx, cy, c)).start()

    return _split_start(start, part, part.shape, N_CHIPS - 1, name=name)


def _scatter_wait(parts, after, *, name):
    def wait(src_ref, land_ref, ssem_ref, rsem_ref):
        x, y, c, chips = _place()
        for k, (cx, cy) in enumerate(chips):
            idx = 2 * cx + cy
            cp = _scatter_copy(src_ref, land_ref, ssem_ref, rsem_ref, k, idx, idx, (x, y, c))
            cp.wait_send()
            cp.wait_recv()

    return _wait_call(wait, parts, after, name=name)


def _split_start(start_fn, src, land_shape, n_sems, *, name):
    def body(src_ref, land_ref, ssem, rsem, src_out, land_out, token):
        start_fn(src_ref, land_ref, ssem, rsem)
        token[...] = jnp.zeros_like(token)

    res = pl.pallas_call(
        body, name=name, in_specs=[HBM, HBM], out_specs=[SEM, SEM, HBM, HBM, pl.BlockSpec(memory_space=pltpu.VMEM)],
        out_shape=[pltpu.SemaphoreType.DMA((n_sems,)), pltpu.SemaphoreType.DMA((n_sems,)), _hbm(src.shape, src.dtype),
                   _hbm(land_shape, src.dtype), jax.ShapeDtypeStruct(TOKEN_SHAPE, F32)],
        input_output_aliases={0: 2, 1: 3}, compiler_params=pltpu.CompilerParams(has_side_effects=SIDE_EFFECT),
    )(_in_hbm(src), _in_hbm(lax.empty(land_shape, src.dtype)))
    return tuple(res[:4]), res[4]


def _sibling_copies(src_ref, land_ref, ssem, rsem, k0, groups, which, to):
    def copy(k, src, dst):
        return pltpu.make_async_remote_copy(
            src_ref=src, dst_ref=dst, send_sem=ssem.at[k], recv_sem=rsem.at[k], device_id=to, device_id_type=MESH)

    if groups == 0:
        return [copy(k0, src_ref, land_ref)]
    hw = src_ref.shape[1] // groups // 2
    return [copy(k0 + j, src_ref.at[:, pl.ds(pl.multiple_of((2 * j + which) * hw, LANES), hw)],
                 land_ref.at[:, j * hw:(j + 1) * hw]) for j in range(groups)]


def _to_sibling_start(items, *, name):
    n = len(items)
    shapes = [a.shape if g == 0 else (a.shape[0], a.shape[1] // 2) for a, g in items]
    first = [sum(max(g, 1) for _, g in items[:k]) for k in range(n + 1)]

    def body(*refs):
        srcs, lands, ssem, rsem, token = refs[:n], refs[n:2 * n], refs[2 * n], refs[2 * n + 1], refs[-1]
        x, y, c, _ = _place()
        for k, (_, g) in enumerate(items):
            for cp in _sibling_copies(srcs[k], lands[k], ssem, rsem, first[k], g, 1 - c, (x, y, 1 - c)):
                cp.start()
        token[...] = jnp.zeros_like(token)

    res = pl.pallas_call(
        body, name=name, in_specs=[HBM] * (2 * n),
        out_specs=[SEM, SEM] + [HBM] * (2 * n) + [pl.BlockSpec(memory_space=pltpu.VMEM)],
        out_shape=[pltpu.SemaphoreType.DMA((first[n],)), pltpu.SemaphoreType.DMA((first[n],))]
        + [_hbm(a.shape, a.dtype) for a, _ in items] + [_hbm(s, a.dtype) for s, (a, _) in zip(shapes, items)]
        + [jax.ShapeDtypeStruct(TOKEN_SHAPE, F32)],
        input_output_aliases={k: 2 + k for k in range(2 * n)},
        compiler_params=pltpu.CompilerParams(has_side_effects=SIDE_EFFECT),
    )(*[_in_hbm(a) for a, _ in items], *[_in_hbm(lax.empty(s, a.dtype)) for s, (a, _) in zip(shapes, items)])
    return [(res[0], res[1], first[k], g, res[2 + k], res[2 + n + k]) for k, (_, g) in enumerate(items)], res[-1]


def _from_sibling(flight, after, *, name):
    ssem, rsem, k0, groups, src, land = flight

    def wait(src_ref, land_ref, ssem_ref, rsem_ref):
        x, y, c, _ = _place()
        for cp in _sibling_copies(src_ref, land_ref, ssem_ref, rsem_ref, k0, groups, 1 - c, (x, y, c)):
            cp.wait_send()
            cp.wait_recv()

    return _wait_call(wait, (ssem, rsem, src, land), after, name=name)


def _dev_peers(x, y, c, chips):
    return [(x, y, 1 - c)] + [(cx, cy, c) for cx, cy in chips] + [(cx, cy, 1 - c) for cx, cy in chips]


def _dev_gather_start(part, *, name):
    def start(src_ref, land_ref, ssem, rsem):
        x, y, c, chips = _place()
        for k, to in enumerate(_dev_peers(x, y, c, chips)):
            pltpu.make_async_remote_copy(
                src_ref=src_ref, dst_ref=land_ref.at[4 * x + 2 * y + c], send_sem=ssem.at[k], recv_sem=rsem.at[k],
                device_id=to, device_id_type=MESH).start()

    return _split_start(start, part, (N_DEV,) + part.shape, N_DEV - 1, name=name)


def _dev_gather_wait(parts, after, *, name):
    def wait(src_ref, land_ref, ssem_ref, rsem_ref):
        x, y, c, chips = _place()
        for k, (px, py, pc) in enumerate(_dev_peers(x, y, c, chips)):
            cp = pltpu.make_async_remote_copy(
                src_ref=src_ref, dst_ref=land_ref.at[4 * px + 2 * py + pc], send_sem=ssem_ref.at[k],
                recv_sem=rsem_ref.at[k], device_id=(x, y, c), device_id_type=MESH)
            cp.wait_send()
            cp.wait_recv()

    return _wait_call(wait, parts, after, name=name)[1]


def _sibling_share_halves(arrays, *, name):
    n = len(arrays)

    def body(*refs):
        bufs = refs[n:2 * n]
        send_sems, recv_sems = refs[2 * n:]
        x, y, c, _ = _place()
        copies = []
        for a in range(n):
            mine = bufs[a].at[:, _half_cols(bufs[a].shape[1], c)]
            cp = pltpu.make_async_remote_copy(
                src_ref=mine, dst_ref=mine, send_sem=send_sems.at[a], recv_sem=recv_sems.at[a],
                device_id=(x, y, 1 - c), device_id_type=MESH)
            cp.start()
            copies.append(cp)
        for a in range(n):
            theirs = bufs[a].at[:, _half_cols(bufs[a].shape[1], 1 - c)]
            pltpu.make_async_remote_copy(
                src_ref=theirs, dst_ref=theirs, send_sem=send_sems.at[a], recv_sem=recv_sems.at[a],
                device_id=(x, y, c), device_id_type=MESH).wait_recv()
        for cp in copies:
            cp.wait_send()

    return pl.pallas_call(
        body, name=name, in_specs=[HBM] * n, out_specs=[HBM] * n,
        out_shape=[jax.ShapeDtypeStruct(h.shape, h.dtype) for h in arrays],
        input_output_aliases={a: a for a in range(n)},
        scratch_shapes=[pltpu.SemaphoreType.DMA((n,)), pltpu.SemaphoreType.DMA((n,))],
    )(*arrays)


def _pack(arrays, rows_multiple=16, width=LANES):
    flat = jnp.concatenate([a.astype(F32).reshape(-1) for a in arrays])
    total = flat.shape[0]
    rows = -(-total // width)
    rows = -(-rows // rows_multiple) * rows_multiple
    return jnp.pad(flat, (0, rows * width - total)).reshape(rows, width)


def _unpack(buf, shapes):
    flat = buf.reshape(-1)
    out, off = [], 0
    for s in shapes:
        n = math.prod(s)
        out.append(flat[off:off + n].reshape(s))
        off += n
    return out


def kernel(x, norm_pre, norm_post, gla_w_in, gla_w_gate2, gla_b_gate, gla_o_gain, gla_w_out, sgu_w_in, sgu_ln_gain, sgu_ln_bias, sgu_w_spatial, sgu_b_spatial, sgu_w_out, loss_target, m_norm_pre, m_norm_post, m_gla_w_in, m_gla_w_gate2, m_gla_b_gate, m_gla_o_gain, m_gla_w_out, m_sgu_w_in, m_sgu_ln_gain, m_sgu_ln_bias, m_sgu_w_spatial, m_sgu_b_spatial, m_sgu_w_out, v_norm_pre, v_norm_post, v_gla_w_in, v_gla_w_gate2, v_gla_b_gate, v_gla_o_gain, v_gla_w_out, v_sgu_w_in, v_sgu_ln_gain, v_sgu_ln_bias, v_sgu_w_spatial, v_sgu_b_spatial, v_sgu_w_out):
    _, t, d = x.shape
    dk = d // 2
    ws = gla_w_in.shape[2]
    wp = -(-ws // LANES) * LANES
    lay = (ws, wp)
    chip =2 * lax.axis_index("x") + lax.axis_index("y")
    core = lax.axis_index("c")
    core_idx = core.astype(jnp.int32).reshape(1)
    others = jnp.arange(N_CHIPS - 1, dtype=jnp.int32)
    others = others + (others >= chip).astype(jnp.int32)
    slots = jnp.concatenate([chip.astype(jnp.int32).reshape(1), others, core_idx])

    x0 = x[0]
    target = loss_target[0]

    wt_in_g, mt_in_g, vt_in_g = gla_w_in[0].T, m_gla_w_in[0].T, v_gla_w_in[0].T

    small_shard = _pack([gla_w_gate2[0], sgu_ln_gain[0], sgu_ln_bias[0]], rows_multiple=8, width=2 * LANES)
    own = [small_shard, jnp.pad(wt_in_g.astype(BF16), ((0, wp - ws), (0, 0)))]
    in_flight, token = _gather_start(own, name="gather_start_a")
    own_later = [gla_w_out[0].astype(BF16), sgu_w_in[0].astype(BF16), sgu_w_out[0].astype(BF16)]
    in_flight_later, token_later = _gather_start(own_later, name="gather_start_b", after=[token])
    own, in_flight = own + own_later, in_flight + in_flight_later

    def with_own(i, land):
        return lax.dynamic_update_slice(land, own[i][None], (chip, 0, 0))

    def arrived(i, after, name):
        land = _gather_wait(in_flight[i], after, name=name + "_wait")[3]
        return with_own(i, _sibling_forward(land, name=name + "_share"))

    h0 = _norm_pre(x0, norm_pre[0:1] + token[0:1, 0:1] + token_later[0:1, 0:1], name="pre0")
    g_small = arrived(0, h0, "w_small")
    wt_g = arrived(1, [g_small, wt_in_g, mt_in_g, vt_in_g], "w_gla_in").reshape(N_CHIPS * wp, d)
    shard_shapes = [gla_w_gate2.shape[1:], sgu_ln_gain.shape[1:], sgu_ln_bias.shape[1:]]
    per_chip = [_unpack(g_small[j], shard_shapes) for j in range(N_CHIPS)]
    w2_full = jnp.concatenate([p[0] for p in per_chip], axis=1)
    ln_gain = jnp.concatenate([p[1] for p in per_chip], axis=0)[None, :]
    ln_bias = jnp.concatenate([p[2] for p in per_chip], axis=0)[None, :]
    w2p = jnp.pad(w2_full, ((0, LANES - GLA_GATE_RANK), (0, 0)))

    pos_chunk = jnp.arange(SGU_BLOCK) // CHUNK
    mask = pos_chunk[:, None] >= pos_chunk[None, :]
    ws_masked = jnp.where(mask[None], sgu_w_spatial[0], 0.0)
    ws_masked_t = ws_masked.transpose(0, 2, 1)
    bs_t = sgu_b_spatial[0].T

    proj0 = _matmul(h0, wt_g, mode="nt", out_dtype=F32, name="gla_in", tn=wp)
    o0, a0, s_before, s_final = _gla_fwd(proj0, w2p, gla_b_gate, gla_o_gain, lay, name="gla_scan")
    w_out_g = arrived(2, a0, "w_gla_out").reshape(d, d)
    y0 = _matmul(a0, w_out_g, mode="nn", out_dtype=F32, name="gla_out")
    x1, h1 = _post_then_pre(x0, y0, norm_post[0:1], norm_pre[1:2], name="post0_pre1")
    g_wi_s = arrived(3, h1, "w_sgu_in")
    proj1 = _matmul(h1, g_wi_s, mode="nn", out_dtype=F32, name="sgu_in", b_shards=True)
    a1 = _sgu_fwd(proj1, ln_gain, ln_bias, ws_masked, bs_t, name="sgu_gate")
    w_out_s = arrived(4, a1, "w_sgu_out").reshape(d, d)
    acts, tok = _to_sibling_start([(a1, 0), (a0, 0), (h1, 0), (h0, 1)], name="acts_to_sibling")
    a1, a0, h1, h0 = [f[4] for f in acts]
    y1 = _matmul(a1, w_out_s, mode="nn", out_dtype=F32, name="sgu_out", after=tok)
    loss_part, dx2, dy1, d_post1 = _loss_head(x1, y1, norm_post[1:2], target, name="loss_head")

    def behind(small, token):
        return small + token[0:1, 0:1]

    def pair_gradient(a_sent, b_sent, after, shards_on, name):
        a_me, a_sib = _from_sibling(a_sent, after, name=name + "_a_wait")
        b_me, b_sib = _from_sibling(b_sent, [a_sib] + list(after), name=name + "_b_wait")
        pair = _matmul_dw_pair(a_me, a_sib, b_me, b_sib, core_idx, shards_on=shards_on,
                               name=name + "_pair")
        return _scatter_start(pair, name=name + "_start")

    def reduced(flight, after, name):
        pair, landed = _scatter_wait(flight, after, name=name + "_wait")
        return _chip_sum(pair, landed, slots, name=name + "_sum")

    (dy1_sent,), tok = _to_sibling_start([(dy1, 1)], name="dy1_to_sibling")
    dy1 = dy1_sent[4]
    da1 = _matmul(dy1, w_out_s, mode="nt", out_dtype=F32, name="d_sgu_act", after=tok)
    fl_wo_s, tok = pair_gradient(acts[0], dy1_sent, [da1], "rows", "g_sgu_out")
    dproj1, d_ws, d_bs_t, d_lg, d_lb = _sgu_bwd(da1, proj1, ln_gain, behind(ln_bias, tok), ws_masked, ws_masked_t,
                                                bs_t, name="sgu_gate_bwd")
    (dp1_sent,), tok = _to_sibling_start([(dproj1, N_CHIPS)], name="dproj1_to_sibling")
    dproj1 = dp1_sent[4]
    dh1 = _matmul_nt_shards(dproj1, g_wi_s, out_dtype=F32, name="d_sgu_h", after=tok)
    fl_wi_s, tok = pair_gradient(acts[2], dp1_sent, [dh1], "cols", "g_sgu_in")
    dx1, dy0, d_pre1, d_post0 = _mid_bwd(dx2, dh1, x1, behind(norm_pre[1:2], tok), y0, norm_post[0:1],
                                         name="pre1_post0_bwd")
    (dy0_sent,), tok = _to_sibling_start([(dy0, 1)], name="dy0_to_sibling")
    dy0 = dy0_sent[4]
    da0 = _matmul(dy0, w_out_g, mode="nt", out_dtype=F32, name="d_gla_act", after=tok)
    fl_wo_g, tok = pair_gradient(acts[1], dy0_sent, [da0], "rows", "g_gla_out")
    dproj0, d_og, d_bg, d_w2p = _gla_bwd(da0, o0, proj0, w2p, behind(gla_b_gate, tok), gla_o_gain, s_before, s_final,
                                         lay, name="gla_scan_bwd")
    early_shapes = [norm_post.shape, gla_b_gate.shape, gla_o_gain.shape, sgu_w_spatial.shape, sgu_b_spatial.shape,
                    (1, GLA_GATE_RANK, dk), (1, d), (1, d), (1, LANES)]
    early_part = _pack([jnp.concatenate([d_post0, d_post1], axis=0), d_bg, d_og, jnp.where(mask[None], d_ws, 0.0)[None],
                        d_bs_t.T[None], d_w2p[:GLA_GATE_RANK][None], d_lg, d_lb, loss_part])
    early_flight, tok = _dev_gather_start(early_part, name="small_early_start")
    (dp0_sent,), tok_sent = _to_sibling_start([(dproj0, 0)], name="dproj0_to_sibling")
    dproj0 = dp0_sent[4]
    dh0 = _matmul(dproj0, wt_g, mode="nn", out_dtype=F32, name="d_gla_h", tk=N_CHIPS * wp, after=tok_sent)
    a_me, a_sib = _from_sibling(dp0_sent, [dh0, tok], name="g_gla_in_a_wait")
    b_me, b_sib = _from_sibling(acts[3], [a_sib, dh0], name="g_gla_in_b_wait")
    fl_wi_g, tok_scatter = [], None
    for p in range(2):
        pair = _matmul_dw_pair(a_me, a_sib, b_me, b_sib, core_idx, shards_on="rows", part=(p, 2),
                               name=f"g_gla_in_pair{p}", after=tok_scatter)
        flight, tok_scatter = _scatter_start(pair, name=f"g_gla_in_start{p}")
        fl_wi_g.append(flight)
    r_wo_s = reduced(fl_wo_s, tok_scatter, "g_sgu_out")
    r_wi_s = reduced(fl_wi_s, r_wo_s, "g_sgu_in")
    r_wo_g = reduced(fl_wo_g, r_wi_s, "g_gla_out")
    sharing, tok = _share_start([r_wo_s, r_wi_s, r_wo_g], name="grads_share_a")
    grad_x, d_pre0 = _first_bwd(dx1, dh0, x0, behind(norm_pre[0:1], tok), name="pre0_bwd")

    late_part = _pack([jnp.concatenate([d_pre0, d_pre1], axis=0)])
    late_flight, tok = _dev_gather_start(late_part, name="small_late_start")

    def big_update(w, g, m, v, name, after=None):
        return [u[None] for u in _adamw(w[0], g, m[0], v[0], name=name, after=after)]

    g_wo_sgu, g_wi_sgu, g_wo_gla = _share_wait(sharing, [grad_x, tok], name="grads_share_a_wait")
    u_wo_sgu = big_update(sgu_w_out, g_wo_sgu, m_sgu_w_out, v_sgu_w_out, "adamw_sgu_w_out")
    u_wo_gla = big_update(gla_w_out, g_wo_gla, m_gla_w_out, v_gla_w_out, "adamw_gla_w_out")
    u_wi_sgu = big_update(sgu_w_in, g_wi_sgu, m_sgu_w_in, v_sgu_w_in, "adamw_sgu_w_in")

    def summed_over_devices(part, flight, after, shapes, name):
        land = _dev_gather_wait(flight, after, name=name + "_wait")
        every = lax.dynamic_update_slice(land, part[None], (2 * chip + core, 0, 0))
        return _unpack(_stack_sum(every, name=name + "_sum"), shapes)

    (g_post, g_bg, g_og, g_wsp, g_bsp, g_w2_full, g_lg_full, g_lb_full, loss_vec) = summed_over_devices(
        early_part, early_flight, [u_wo_gla[1], u_wo_sgu[1], u_wi_sgu[1]], early_shapes, "small_early")

    r_wi_g, behind_this = None, loss_vec
    for p, flight in enumerate(fl_wi_g):
        pair, landed = _scatter_wait(flight, behind_this, name=f"g_gla_in_wait{p}")
        r_wi_g = behind_this = _chip_sum(pair, landed, slots, part=(p, 2), into=r_wi_g, name=f"g_gla_in_sum{p}")
    gt_wi_gla, = _sibling_share_halves([r_wi_g], name="grads_share_b")
    u_wi_gla_t = _adamw(wt_in_g, gt_wi_gla, mt_in_g, vt_in_g, name="adamw_gla_w_in")
    u_wi_gla = [u.T[None] for u in u_wi_gla_t]
    g_pre, = summed_over_devices(late_part, late_flight, u_wi_gla_t[1], [norm_pre.shape], "small_late")
    loss = loss_vec[0, 0]
    g_w2 = lax.dynamic_slice_in_dim(g_w2_full, chip * (dk // N_CHIPS), dk // N_CHIPS, axis=2)
    g_lg = lax.dynamic_slice_in_dim(g_lg_full, chip * (d // N_CHIPS), d // N_CHIPS, axis=1)
    g_lb = lax.dynamic_slice_in_dim(g_lb_full, chip * (d // N_CHIPS), d // N_CHIPS, axis=1)

    small_w = [norm_pre, norm_post, gla_b_gate, gla_o_gain, sgu_w_spatial, sgu_b_spatial, gla_w_gate2, sgu_ln_gain,
               sgu_ln_bias]
    small_g = [g_pre, g_post, g_bg, g_og, g_wsp, g_bsp, g_w2, g_lg, g_lb]
    small_m = [m_norm_pre, m_norm_post, m_gla_b_gate, m_gla_o_gain, m_sgu_w_spatial, m_sgu_b_spatial, m_gla_w_gate2,
               m_sgu_ln_gain, m_sgu_ln_bias]
    small_v = [v_norm_pre, v_norm_post, v_gla_b_gate, v_gla_o_gain, v_sgu_w_spatial, v_sgu_b_spatial, v_gla_w_gate2,
               v_sgu_ln_gain, v_sgu_ln_bias]
    own_shapes = [w.shape for w in small_w]
    _, s_dl, s_m, s_v = _adamw(_pack(small_w), _pack(small_g), _pack(small_m), _pack(small_v), name="adamw_small")
    dl_s, m_s, v_s = _unpack(s_dl, own_shapes), _unpack(s_m, own_shapes), _unpack(s_v, own_shapes)

    def ordered(small, kind):
        pre, post, bg, og, wsp, bsp, w2, lg, lb = small
        return [pre, post, u_wi_gla[kind], w2, bg, og, u_wo_gla[kind], u_wi_sgu[kind], lg, lb, wsp, bsp, u_wo_sgu[kind]]

    return (loss, grad_x[None], *ordered(small_g, 0), *ordered(dl_s, 1), *ordered(m_s, 2), *ordered(v_s, 3))
```

```python
import functools
import math

import jax
import jax.numpy as jnp
from jax import lax
from jax.experimental import pallas as pl
from jax.experimental.pallas import tpu as pltpu

F32 = jnp.float32
BF16 = jnp.bfloat16
MESH = pl.DeviceIdType.MESH

EPS = 1e-6
CHUNK = 64
GLA_HEADS = 4
GLA_GATE_RANK = 16
GLA_TAU = 16.0
SGU_BLOCK = 128
SGU_GROUPS = 8
N_CHIPS = 4
N_DEV = 8
LANES = 128

ADAM_LR = 0.001
ADAM_B1 = 0.9
ADAM_B2 = 0.999
ADAM_EPS = 1e-08
ADAM_WD = 0.01
ADAM_STEP = 10

VMEM_LIMIT = 56 * 1024 * 1024


def _cparams(sem=None):
    return pltpu.CompilerParams(dimension_semantics=sem, vmem_limit_bytes=VMEM_LIMIT)


def _pick(n, cap, unit=LANES):
    best = None
    for t in range(unit, min(n, cap) + 1, unit):
        if n % t == 0:
            best = t
    assert best is not None, (n, cap, unit)
    return best


def _dot(a, b, dims):
    return lax.dot_general(a, b, (dims, ((), ())), preferred_element_type=F32)


def _dot_nn(a, b):
    return _dot(a, b, ((1,), (0,)))


def _dot_nt(a, b):
    return _dot(a, b, ((1,), (1,)))


def _dot_tn(a, b):
    return _dot(a, b, ((0,), (0,)))


def _matmul(a, b, *, mode, out_dtype, name, tm=1024, tn=512, tk=2048, b_shards=False, out_shards=False, after=None,
            out_rows=None):
    if mode == "tn":
        K, M = a.shape
    else:
        M, K = a.shape
    if b_shards:
        ns, br, bc = b.shape
        if mode == "nt":
            N, Kb = br, ns * bc
        else:
            Kb, N = br, ns * bc
    else:
        if mode == "nt":
            N, Kb = b.shape
        else:
            Kb, N = b.shape
    assert K == Kb, (a.shape, b.shape, mode)
    tm = _pick(M, tm)
    tk = _pick(K, tk)
    if b_shards and mode != "nt":
        tn = _pick(bc, tn)
    elif out_shards:
        tn = _pick(N // N_CHIPS, tn)
    else:
        tn = _pick(N, tn)
    if b_shards and mode == "nt":
        tk = _pick(bc, tk)
    nk = K // tk
    grid = (M // tm, N // tn, nk)

    if mode == "tn":
        a_spec = pl.BlockSpec((tk, tm), lambda i, j, k: (k, i))
    else:
        a_spec = pl.BlockSpec((tm, tk), lambda i, j, k: (i, k))
    if b_shards:
        if mode == "nt":
            per = bc // tk
            b_spec = pl.BlockSpec((None, tn, tk), lambda i, j, k: (k // per, j, k % per))
        else:
            per = bc // tn
            b_spec = pl.BlockSpec((None, tk, tn), lambda i, j, k: (j // per, k, j % per))
    elif mode == "nt":
        b_spec = pl.BlockSpec((tn, tk), lambda i, j, k: (j, k))
    else:
        b_spec = pl.BlockSpec((tk, tn), lambda i, j, k: (k, j))
    if out_shards:
        per_o = (N // N_CHIPS) // tn
        out_spec = pl.BlockSpec((None, tm, tn), lambda i, j, k: (j // per_o, i, j % per_o))
        out_shape = jax.ShapeDtypeStruct((N_CHIPS, M, N // N_CHIPS), out_dtype)
    else:
        out_spec = pl.BlockSpec((tm, tn), lambda i, j, k: (i, j))
        out_shape = jax.ShapeDtypeStruct((M if out_rows is None else out_rows, N), out_dtype)

    dims = {"nn": ((1,), (0,)), "nt": ((1,), (1,)), "tn": ((0,), (0,))}[mode]

    def body(a_ref, b_ref, *rest):
        o_ref, scratch = (rest[1], rest[2:]) if after is not None else (rest[0], rest[1:])
        part = _dot(a_ref[...].astype(BF16), b_ref[...].astype(BF16), dims)
        if nk == 1:
            o_ref[...] = part.astype(out_dtype)
        else:
            acc_ref, = scratch
            k = pl.program_id(2)

            @pl.when(k == 0)
            def _():
                acc_ref[...] = part

            @pl.when(k > 0)
            def _():
                acc_ref[...] += part

            @pl.when(k == nk - 1)
            def _():
                o_ref[...] = acc_ref[...].astype(out_dtype)

    extra_specs, extra_args = ([], []) if after is None else ([pl.BlockSpec(memory_space=pl.ANY)], [after])
    return pl.pallas_call(
        body, name=name, grid=grid, in_specs=[a_spec, b_spec] + extra_specs, out_specs=out_spec, out_shape=out_shape,
        scratch_shapes=[] if nk == 1 else [pltpu.VMEM((tm, tn), F32)],
        compiler_params=_cparams(("parallel", "parallel", "arbitrary")),
    )(a, b, *extra_args)


def _matmul_into_cols(a, w, which, buf, *, name, tm=1024):
    M, K = a.shape
    _, N, _ = w.shape
    tm = _pick(M, tm)

    def body(which_ref, a_ref, w_ref, buf_ref, o_ref):
        o_ref[...] = _dot_nt(a_ref[...], w_ref[...])

    grid_spec = pltpu.PrefetchScalarGridSpec(
        num_scalar_prefetch=1, grid=(M // tm,),
        in_specs=[pl.BlockSpec((tm, K), lambda i, s: (i, 0)), pl.BlockSpec((None, N, K), lambda i, s: (s[1], 0, 0)),
                  pl.BlockSpec(memory_space=pl.ANY)],
        out_specs=pl.BlockSpec((tm, N), lambda i, s: (i, s[0])))
    return pl.pallas_call(
        body, name=name, grid_spec=grid_spec, out_shape=jax.ShapeDtypeStruct(buf.shape, buf.dtype),
        input_output_aliases={3: 0}, compiler_params=_cparams(("parallel",)),
    )(which, a, w, buf)


def _matmul_nt_shards(a, b, *, out_dtype, name, tm=1024, tn=512, after=None):
    M, K = a.shape
    ns, N, kc = b.shape
    assert K == ns * kc
    tm, tn = _pick(M, tm), _pick(N, tn)

    def body(a_ref, *rest):
        b_refs, o_ref = rest[:ns], rest[ns + (after is not None)]
        acc = _dot_nt(a_ref[:, 0:kc], b_refs[0][...])
        for j in range(1, ns):
            acc += _dot_nt(a_ref[:, j * kc:(j + 1) * kc], b_refs[j][...])
        o_ref[...] = acc.astype(out_dtype)

    def shard(j):
        return pl.BlockSpec((None, tn, kc), lambda i, n: (j, n, 0))

    extra_specs, extra_args = ([], []) if after is None else ([pl.BlockSpec(memory_space=pl.ANY)], [after])
    return pl.pallas_call(
        body, name=name, grid=(M // tm, N // tn),
        in_specs=[pl.BlockSpec((tm, K), lambda i, n: (i, 0))] + [shard(j) for j in range(ns)] + extra_specs,
        out_specs=pl.BlockSpec((tm, tn), lambda i, n: (i, n)), out_shape=jax.ShapeDtypeStruct((M, N), out_dtype),
        compiler_params=_cparams(("parallel", "parallel")),
    )(a, *([b] * ns), *extra_args)


def _rstd(x):
    return lax.rsqrt(jnp.mean(x * x, axis=-1, keepdims=True) + EPS)


def _row_spec(tr, d):
    return pl.BlockSpec((tr, d), lambda i: (i, 0))


def _vec_spec(d):
    return pl.BlockSpec((1, d), lambda i: (0, 0))


def _acc_rows(ref, i, val, cols=slice(None)):
    @pl.when(i == 0)
    def _():
        ref[:, cols] = val

    @pl.when(i > 0)
    def _():
        ref[:, cols] += val


def _norm_pre(x, gain, *, name, tr=256):
    t, d = x.shape
    tr = _pick(t, tr, 8)

    def body(x_ref, g_ref, h_ref):
        xv = x_ref[...]
        h_ref[...] = (xv * _rstd(xv) * g_ref[...]).astype(BF16)

    return pl.pallas_call(
        body, name=name, grid=(t // tr,), in_specs=[_row_spec(tr, d), _vec_spec(d)], out_specs=_row_spec(tr, d),
        out_shape=jax.ShapeDtypeStruct((t, d), BF16), compiler_params=_cparams(("parallel",)),
    )(x, gain)


def _post_then_pre(x, y, post_gain, pre_gain, *, name, tr=256):
    t, d = x.shape
    tr = _pick(t, tr, 8)

    def body(x_ref, y_ref, pg_ref, ng_ref, xn_ref, h_ref):
        yv = y_ref[...]
        xn = x_ref[...] + yv * _rstd(yv) * pg_ref[...]
        xn_ref[...] = xn
        h_ref[...] = (xn * _rstd(xn) * ng_ref[...]).astype(BF16)

    return pl.pallas_call(
        body, name=name, grid=(t // tr,),
        in_specs=[_row_spec(tr, d), _row_spec(tr, d), _vec_spec(d), _vec_spec(d)],
        out_specs=[_row_spec(tr, d), _row_spec(tr, d)],
        out_shape=[jax.ShapeDtypeStruct((t, d), F32), jax.ShapeDtypeStruct((t, d), BF16)],
        compiler_params=_cparams(("parallel",)),
    )(x, y, post_gain, pre_gain)


def _norm_bwd(dy, n, r, gain):
    dn = dy * gain
    return r * (dn - n * jnp.mean(dn * n, axis=-1, keepdims=True))


def _loss_head(x, y, post_gain, target, *, name, tr=256):
    t, d = x.shape
    tr = _pick(t, tr, 8)

    def body(x_ref, y_ref, pg_ref, t_ref, loss_ref, dx_ref, dy_ref, dpg_ref):
        i = pl.program_id(0)
        yv = y_ref[...]
        r = _rstd(yv)
        n = yv * r
        err = x_ref[...] + n * pg_ref[...] - t_ref[...]
        dx = err * (1.0 / d)
        dx_ref[...] = dx
        part = 0.5 * jnp.sum(jnp.mean(err * err, axis=-1, keepdims=True), axis=0, keepdims=True)
        _acc_rows(loss_ref, i, jnp.broadcast_to(part, (1, LANES)))
        _acc_rows(dpg_ref, i, jnp.sum(dx * n, axis=0, keepdims=True))
        dy_ref[...] = _norm_bwd(dx, n, r, pg_ref[...]).astype(BF16)

    return pl.pallas_call(
        body, name=name, grid=(t // tr,),
        in_specs=[_row_spec(tr, d), _row_spec(tr, d), _vec_spec(d), _row_spec(tr, d)],
        out_specs=[_vec_spec(LANES), _row_spec(tr, d), _row_spec(tr, d), _vec_spec(d)],
        out_shape=[jax.ShapeDtypeStruct((1, LANES), F32), jax.ShapeDtypeStruct((t, d), F32),
                   jax.ShapeDtypeStruct((t, d), BF16), jax.ShapeDtypeStruct((1, d), F32)],
        compiler_params=_cparams(("arbitrary",)),
    )(x, y, post_gain, target)


def _mid_bwd(dx_out, dh, x, pre_gain, y_prev, post_gain_prev, *, name, tr=256):
    t, d = x.shape
    tr = _pick(t, tr, 8)

    def body(dxo_ref, dh_ref, x_ref, ng_ref, y_ref, pg_ref, dx_ref, dy_ref, dng_ref, dpg_ref):
        i = pl.program_id(0)
        xv = x_ref[...]
        r = _rstd(xv)
        xh = xv * r
        dhv = dh_ref[...]
        _acc_rows(dng_ref, i, jnp.sum(dhv * xh, axis=0, keepdims=True))
        dx = dxo_ref[...] + _norm_bwd(dhv, xh, r, ng_ref[...])
        dx_ref[...] = dx
        yv = y_ref[...]
        ry = _rstd(yv)
        n = yv * ry
        _acc_rows(dpg_ref, i, jnp.sum(dx * n, axis=0, keepdims=True))
        dy_ref[...] = _norm_bwd(dx, n, ry, pg_ref[...]).astype(BF16)

    return pl.pallas_call(
        body, name=name, grid=(t // tr,),
        in_specs=[_row_spec(tr, d), _row_spec(tr, d), _row_spec(tr, d), _vec_spec(d), _row_spec(tr, d), _vec_spec(d)],
        out_specs=[_row_spec(tr, d), _row_spec(tr, d), _vec_spec(d), _vec_spec(d)],
        out_shape=[jax.ShapeDtypeStruct((t, d), F32), jax.ShapeDtypeStruct((t, d), BF16),
                   jax.ShapeDtypeStruct((1, d), F32), jax.ShapeDtypeStruct((1, d), F32)],
        compiler_params=_cparams(("arbitrary",)),
    )(dx_out, dh, x, pre_gain, y_prev, post_gain_prev)


def _first_bwd(dx_out, dh, x, pre_gain, *, name, tr=256):
    t, d = x.shape
    tr = _pick(t, tr, 8)

    def body(dxo_ref, dh_ref, x_ref, ng_ref, dx_ref, dng_ref):
        i = pl.program_id(0)
        xv = x_ref[...]
        r = _rstd(xv)
        xh = xv * r
        dhv = dh_ref[...]
        _acc_rows(dng_ref, i, jnp.sum(dhv * xh, axis=0, keepdims=True))
        dx_ref[...] = dxo_ref[...] + _norm_bwd(dhv, xh, r, ng_ref[...])

    return pl.pallas_call(
        body, name=name, grid=(t // tr,),
        in_specs=[_row_spec(tr, d), _row_spec(tr, d), _row_spec(tr, d), _vec_spec(d)],
        out_specs=[_row_spec(tr, d), _vec_spec(d)],
        out_shape=[jax.ShapeDtypeStruct((t, d), F32), jax.ShapeDtypeStruct((1, d), F32)],
        compiler_params=_cparams(("arbitrary",)),
    )(dx_out, dh, x, pre_gain)


def _sigmoid(x):
    return 1.0 / (1.0 + jnp.exp(-x))


def _log_sigmoid(x):
    return jnp.minimum(x, 0.0) - jnp.log(1.0 + jnp.exp(-jnp.abs(x)))


_GELU_C = math.sqrt(2.0 / math.pi)


_GELU_A = 0.044715


def _gelu_parts(x, with_grad=True):
    x2 = x * x
    h = 0.5 * jnp.tanh(x * (_GELU_C + (_GELU_C * _GELU_A) * x2)) + 0.5
    val = x * h
    if not with_grad:
        return val, None
    return val, h * (1.0 + (1.0 - h) * (x * (2.0 * _GELU_C + (6.0 * _GELU_C * _GELU_A) * x2)))


def _split3(x):
    hi = x.astype(BF16)
    r1 = x - hi.astype(F32)
    mid = r1.astype(BF16)
    lo = (r1 - mid.astype(F32)).astype(BF16)
    return hi, mid, lo


def _tri_matmul(tri_bf16, x):
    hi, mid, lo = _split3(x)
    return _dot_nn(tri_bf16, hi) + _dot_nn(tri_bf16, mid) + _dot_nn(tri_bf16, lo)


def _gla_dims(d):
    dk, dv = d // 2, d
    return dk, dv, dk // GLA_HEADS, dv // GLA_HEADS


def _col_pieces(a, b, lay):
    ws, wp = lay
    out = []
    while a < b:
        j = a // ws
        end = min(b, (j + 1) * ws)
        out.append((j * wp + a - j * ws, end - a))
        a = end
    return out


def _load_cols(ref, a, b, lay):
    parts = [ref[:, s:s + n] for s, n in _col_pieces(a, b, lay)]
    return parts[0] if len(parts) == 1 else jnp.concatenate(parts, axis=1)


def _store_cols(ref, a, val, lay):
    off = 0
    for s, n in _col_pieces(a, a + val.shape[1], lay):
        ref[:, s:s + n] = val[:, off:off + n]
        off += n


def _gate_window(c_r, lay):
    (start, _), = _col_pieces(c_r, c_r + GLA_GATE_RANK, lay)
    assert (start % lay[1]) + LANES <= lay[1]
    return slice(start, start + LANES)


def _gla_gates(glr, k, w2_ref, b_ref):
    z = _dot_nn(glr.astype(BF16), w2_ref[...].astype(BF16)) + b_ref[...]
    la = _log_sigmoid(z) * (1.0 / GLA_TAU)
    row = lax.broadcasted_iota(jnp.int32, (CHUNK, CHUNK), 0)
    col = lax.broadcasted_iota(jnp.int32, (CHUNK, CHUNK), 1)
    incl = (row >= col).astype(BF16)
    bcum = _tri_matmul(incl, la)
    b_end = bcum[CHUNK - 1:CHUNK, :]
    e_rest = jnp.exp(b_end - bcum)
    return z, e_rest, k * e_rest, jnp.exp(b_end)


def _gla_fwd(proj, w2p, b_gate, o_gain, lay, *, name):
    t, wcols = proj.shape
    d = o_gain.shape[1]
    dk, dv, dkh, dvh = _gla_dims(d)
    nc = t // CHUNK
    c_k, c_v, c_g, c_r = dk, 2 * dk, 2 * dk + dv, 2 * dk + 2 * dv
    scale = dkh ** -0.5

    def body(p_ref, w2_ref, b_ref, og_ref, o_ref, a_ref, sb_ref, sfin_ref, s_ref):
        i = pl.program_id(0)

        @pl.when(i == 0)
        def _():
            s_ref[...] = jnp.zeros_like(s_ref)

        q = _load_cols(p_ref, 0, dk, lay) * scale
        k = _load_cols(p_ref, c_k, c_k + dk, lay)
        glr = p_ref[:, _gate_window(c_r, lay)]
        _, _, kdec, decay = _gla_gates(glr, k, w2_ref, b_ref)
        for h in range(GLA_HEADS):
            ks = slice(h * dkh, (h + 1) * dkh)
            vs = slice(h * dvh, (h + 1) * dvh)
            v_h = _load_cols(p_ref, c_v + h * dvh, c_v + (h + 1) * dvh, lay)
            g_h = _load_cols(p_ref, c_g + h * dvh, c_g + (h + 1) * dvh, lay)
            s_old = s_ref[h]
            sb_ref[0, h] = s_old
            s_new = s_old * decay[:, ks] + _dot_tn(v_h.astype(BF16), kdec[:, ks].astype(BF16))
            s_ref[h] = s_new
            o_h = _dot_nt(q[:, ks].astype(BF16), s_new.astype(BF16))
            o_ref[:, vs] = o_h
            on = o_h * _rstd(o_h)
            a_ref[:, vs] = (on * og_ref[:, vs] * (g_h * _sigmoid(g_h))).astype(BF16)

        @pl.when(i == nc - 1)
        def _():
            sfin_ref[...] = s_ref[...]

    full = lambda *shape: pl.BlockSpec(shape, lambda i: (0,) * len(shape))
    return pl.pallas_call(
        body, name=name, grid=(nc,),
        in_specs=[pl.BlockSpec((CHUNK, wcols), lambda i: (i, 0)), full(LANES, dk), full(1, dk), full(1, dv)],
        out_specs=[pl.BlockSpec((CHUNK, dv), lambda i: (i, 0)), pl.BlockSpec((CHUNK, dv), lambda i: (i, 0)),
                   pl.BlockSpec((1, GLA_HEADS, dvh, dkh), lambda i: (i, 0, 0, 0)), full(GLA_HEADS, dvh, dkh)],
        out_shape=[jax.ShapeDtypeStruct((t, dv), F32), jax.ShapeDtypeStruct((t, dv), BF16),
                   jax.ShapeDtypeStruct((nc, GLA_HEADS, dvh, dkh), F32),
                   jax.ShapeDtypeStruct((GLA_HEADS, dvh, dkh), F32)],
        scratch_shapes=[pltpu.VMEM((GLA_HEADS, dvh, dkh), F32)],
        compiler_params=_cparams(("arbitrary",)),
    )(proj, w2p, b_gate, o_gain)


def _gla_bwd(da, o, proj, w2p, b_gate, o_gain, s_before, s_final, lay, *, name):
    t, wcols = proj.shape
    d = o_gain.shape[1]
    dk, dv, dkh, dvh = _gla_dims(d)
    nc = t // CHUNK
    c_k, c_v, c_g, c_r = dk, 2 * dk, 2 * dk + dv, 2 * dk + 2 * dv
    scale = dkh ** -0.5

    def body(da_ref, o_ref, p_ref, w2_ref, b_ref, og_ref, sb_ref, sfin_ref,
             dp_ref, dog_ref, db_ref, dw2_ref, s_ref, gc_ref, dkd_ref):
        i = pl.program_id(0)

        @pl.when(i == 0)
        def _():
            s_ref[...] = sfin_ref[...]
            gc_ref[...] = jnp.zeros_like(gc_ref)

        ws, wp = lay
        for j in range(N_CHIPS):
            dp_ref[:, j * wp + ws:(j + 1) * wp] = jnp.zeros((CHUNK, wp - ws), BF16)
        q = _load_cols(p_ref, 0, dk, lay) * scale
        k = _load_cols(p_ref, c_k, c_k + dk, lay)
        glr = p_ref[:, _gate_window(c_r, lay)]
        z, e_rest, kdec, decay = _gla_gates(glr, k, w2_ref, b_ref)
        ddecay = []
        for h in range(GLA_HEADS):
            ks = slice(h * dkh, (h + 1) * dkh)
            vs = slice(h * dvh, (h + 1) * dvh)
            v_h = _load_cols(p_ref, c_v + h * dvh, c_v + (h + 1) * dvh, lay)
            g_h = _load_cols(p_ref, c_g + h * dvh, c_g + (h + 1) * dvh, lay)
            da_h = da_ref[:, vs]
            o_h = o_ref[:, vs]
            og_h = og_ref[:, vs]
            r = _rstd(o_h)
            on = o_h * r
            sg = _sigmoid(g_h)
            silu = g_h * sg
            _acc_rows(dog_ref, i, jnp.sum(da_h * silu * on, axis=0, keepdims=True), vs)
            _store_cols(dp_ref, c_g + h * dvh, (da_h * (on * og_h) * (sg * (1.0 + g_h * (1.0 - sg)))).astype(BF16),
                        lay)
            don = da_h * silu * og_h
            do_h = (r * (don - on * jnp.mean(don * on, axis=-1, keepdims=True))).astype(BF16)
            s_cur = s_ref[h]
            _store_cols(dp_ref, h * dkh, (_dot_nn(do_h, s_cur.astype(BF16)) * scale).astype(BF16), lay)
            g_tot = gc_ref[h] + _dot_tn(do_h, q[:, ks].astype(BF16))
            g_bf = g_tot.astype(BF16)
            dkd_ref[:, ks] = _dot_nn(v_h.astype(BF16), g_bf)
            _store_cols(dp_ref, c_v + h * dvh, _dot_nt(kdec[:, ks].astype(BF16), g_bf).astype(BF16), lay)
            s_prev = sb_ref[0, h]
            ddecay.append(jnp.sum(g_tot * s_prev, axis=0, keepdims=True))
            gc_ref[h] = g_tot * decay[:, ks]
            s_ref[h] = s_prev
        dkdec = dkd_ref[...]
        _store_cols(dp_ref, c_k, (dkdec * e_rest).astype(BF16), lay)
        d_e = dkdec * kdec
        row = lax.broadcasted_iota(jnp.int32, (CHUNK, CHUNK), 0)
        col = lax.broadcasted_iota(jnp.int32, (CHUNK, CHUNK), 1)
        excl = (row > col).astype(BF16)
        dla = jnp.concatenate(ddecay, axis=1) * decay + _tri_matmul(excl, d_e)
        dz = dla * (1.0 / GLA_TAU) * (1.0 - _sigmoid(z))
        _acc_rows(db_ref, i, jnp.sum(dz, axis=0, keepdims=True))
        dz_bf = dz.astype(BF16)
        dw2 = _dot_tn(glr.astype(BF16), dz_bf)

        @pl.when(i == 0)
        def _():
            dw2_ref[...] = dw2

        @pl.when(i > 0)
        def _():
            dw2_ref[...] += dw2

        dp_ref[:, _gate_window(c_r, lay)] = _dot_nt(dz_bf, w2_ref[...].astype(BF16)).astype(BF16)

    rev = lambda i: (nc - 1 - i, 0)
    full = lambda *shape: pl.BlockSpec(shape, lambda i: (0,) * len(shape))
    return pl.pallas_call(
        body, name=name, grid=(nc,),
        in_specs=[pl.BlockSpec((CHUNK, dv), rev), pl.BlockSpec((CHUNK, dv), rev), pl.BlockSpec((CHUNK, wcols), rev),
                  full(LANES, dk), full(1, dk), full(1, dv),
                  pl.BlockSpec((1, GLA_HEADS, dvh, dkh), lambda i: (nc - 1 - i, 0, 0, 0)), full(GLA_HEADS, dvh, dkh)],
        out_specs=[pl.BlockSpec((CHUNK, wcols), rev), full(1, dv), full(1, dk), full(LANES, dk)],
        out_shape=[jax.ShapeDtypeStruct((t, wcols), BF16), jax.ShapeDtypeStruct((1, dv), F32),
                   jax.ShapeDtypeStruct((1, dk), F32), jax.ShapeDtypeStruct((LANES, dk), F32)],
        scratch_shapes=[pltpu.VMEM((GLA_HEADS, dvh, dkh), F32), pltpu.VMEM((GLA_HEADS, dvh, dkh), F32),
                        pltpu.VMEM((CHUNK, dk), F32)],
        compiler_params=_cparams(("arbitrary",)),
    )(da, o, proj, w2p, b_gate, o_gain, s_before, s_final)


def _sgu_mid(p_ref, lg_ref, lb_ref, ws_ref, bst_ref, w, with_grad=True):
    gd = w // SGU_GROUPS
    u_act, du_fac = _gelu_parts(p_ref[:, 0:w], with_grad)
    vf, dv_fac = _gelu_parts(p_ref[:, w:2 * w], with_grad)
    mu = jnp.mean(vf, axis=-1, keepdims=True)
    cen = vf - mu
    rstd = lax.rsqrt(jnp.mean(cen * cen, axis=-1, keepdims=True) + EPS)
    xh = cen * rstd
    vn = (xh * lg_ref[...] + lb_ref[...]).astype(BF16)
    vs = [_dot_nn(ws_ref[g].astype(BF16), vn[:, g * gd:(g + 1) * gd]) + bst_ref[:, g:g + 1]
          for g in range(SGU_GROUPS)]
    return u_act, du_fac, dv_fac, rstd, xh, vn, vs


def _sgu_fwd(proj, ln_gain, ln_bias, ws_masked, bs_t, *, name):
    t, w3 = proj.shape
    w = w3 // 3
    gd = w // SGU_GROUPS
    nb = t // SGU_BLOCK

    def body(p_ref, lg_ref, lb_ref, ws_ref, bst_ref, a_ref):
        u_act, _, _, _, _, _, vs = _sgu_mid(p_ref, lg_ref, lb_ref, ws_ref, bst_ref, w, with_grad=False)
        for g in range(SGU_GROUPS):
            cs = slice(g * gd, (g + 1) * gd)
            gate = p_ref[:, 2 * w + g * gd:2 * w + (g + 1) * gd]
            a_ref[:, cs] = (u_act[:, cs] * vs[g] * (gate * _sigmoid(gate))).astype(BF16)

    full = lambda *shape: pl.BlockSpec(shape, lambda i: (0,) * len(shape))
    return pl.pallas_call(
        body, name=name, grid=(nb,),
        in_specs=[pl.BlockSpec((SGU_BLOCK, w3), lambda i: (i, 0)), full(1, w), full(1, w),
                  full(SGU_GROUPS, SGU_BLOCK, SGU_BLOCK), full(SGU_BLOCK, SGU_GROUPS)],
        out_specs=pl.BlockSpec((SGU_BLOCK, w), lambda i: (i, 0)),
        out_shape=jax.ShapeDtypeStruct((t, w), BF16),
        compiler_params=_cparams(("parallel",)),
    )(proj, ln_gain, ln_bias, ws_masked, bs_t)


def _sgu_bwd(da, proj, ln_gain, ln_bias, ws_masked, ws_masked_t, bs_t, *, name):
    t, w3 = proj.shape
    w = w3 // 3
    gd = w // SGU_GROUPS
    nb = t // SGU_BLOCK

    def body(da_ref, p_ref, lg_ref, lb_ref, ws_ref, wst_ref, bst_ref, dp_ref, dws_ref, dbst_ref, dlg_ref, dlb_ref,
             dvn_ref):
        i = pl.program_id(0)
        u_act, du_fac, dv_fac, rstd, xh, vn, vs = _sgu_mid(p_ref, lg_ref, lb_ref, ws_ref, bst_ref, w)
        for g in range(SGU_GROUPS):
            cs = slice(g * gd, (g + 1) * gd)
            gate = p_ref[:, 2 * w + g * gd:2 * w + (g + 1) * gd]
            sg = _sigmoid(gate)
            silu = gate * sg
            da_g = da_ref[:, cs]
            ua_g = u_act[:, cs]
            dp_ref[:, cs] = (da_g * vs[g] * silu * du_fac[:, cs]).astype(BF16)
            dp_ref[:, 2 * w + g * gd:2 * w + (g + 1) * gd] = (
                da_g * ua_g * vs[g] * (sg * (1.0 + gate * (1.0 - sg)))).astype(BF16)
            dvs = da_g * ua_g * silu
            dvs_bf = dvs.astype(BF16)
            dvn_ref[:, cs] = _dot_nn(wst_ref[g].astype(BF16), dvs_bf)
            dws = _dot_nt(dvs_bf, vn[:, cs])
            dbs = jnp.sum(dvs, axis=1, keepdims=True)

            @pl.when(i == 0)
            def _():
                dws_ref[g] = dws
                dbst_ref[:, g:g + 1] = dbs

            @pl.when(i > 0)
            def _():
                dws_ref[g] += dws
                dbst_ref[:, g:g + 1] += dbs

        dvn = dvn_ref[...]
        _acc_rows(dlg_ref, i, jnp.sum(dvn * xh, axis=0, keepdims=True))
        _acc_rows(dlb_ref, i, jnp.sum(dvn, axis=0, keepdims=True))
        dxh = dvn * lg_ref[...]
        dvf = rstd * (dxh - jnp.mean(dxh, axis=-1, keepdims=True)
                      - xh * jnp.mean(dxh * xh, axis=-1, keepdims=True))
        dp_ref[:, w:2 * w] = (dvf * dv_fac).astype(BF16)

    full = lambda *shape: pl.BlockSpec(shape, lambda i: (0,) * len(shape))
    return pl.pallas_call(
        body, name=name, grid=(nb,),
        in_specs=[pl.BlockSpec((SGU_BLOCK, w), lambda i: (i, 0)), pl.BlockSpec((SGU_BLOCK, w3), lambda i: (i, 0)),
                  full(1, w), full(1, w), full(SGU_GROUPS, SGU_BLOCK, SGU_BLOCK),
                  full(SGU_GROUPS, SGU_BLOCK, SGU_BLOCK), full(SGU_BLOCK, SGU_GROUPS)],
        out_specs=[pl.BlockSpec((SGU_BLOCK, w3), lambda i: (i, 0)), full(SGU_GROUPS, SGU_BLOCK, SGU_BLOCK),
                   full(SGU_BLOCK, SGU_GROUPS), full(1, w), full(1, w)],
        out_shape=[jax.ShapeDtypeStruct((t, w3), BF16), jax.ShapeDtypeStruct((SGU_GROUPS, SGU_BLOCK, SGU_BLOCK), F32),
                   jax.ShapeDtypeStruct((SGU_BLOCK, SGU_GROUPS), F32), jax.ShapeDtypeStruct((1, w), F32),
                   jax.ShapeDtypeStruct((1, w), F32)],
        scratch_shapes=[pltpu.VMEM((SGU_BLOCK, w), F32)],
        compiler_params=_cparams(("arbitrary",)),
    )(da, proj, ln_gain, ln_bias, ws_masked, ws_masked_t, bs_t)


def _tile2d(rows, cols, block_bytes, row_unit):
    if rows % row_unit == 0:
        return _pick(rows, max(row_unit, block_bytes // (4 * cols)), row_unit), cols
    return rows, _pick(cols, max(LANES, block_bytes // (4 * rows)))


def _adamw(w, g, m, v, *, name, block_bytes=1 << 20, after=None, half=None, into=None):
    rows, cols = w.shape
    span = cols if half is None else cols // 2
    tr, tc = _tile2d(rows, span, block_bytes, 8)
    nj = span // tc
    g_rows = g.shape[0]
    assert g_rows == rows or tr == rows
    any_spec = pl.BlockSpec(memory_space=pl.ANY)
    extra_args = ([] if after is None else [after]) + ([] if into is None else list(into))

    def body(h_ref, w_ref, g_ref, m_ref, v_ref, *rest):
        go_ref, d_ref, mo_ref, vo_ref = rest[len(extra_args):]
        gv = g_ref[0:tr, :]
        go_ref[...] = gv
        mn = ADAM_B1 * m_ref[...] + (1.0 - ADAM_B1) * gv
        vn = ADAM_B2 * v_ref[...] + (1.0 - ADAM_B2) * (gv * gv)
        m_hat = mn / (1.0 - ADAM_B1 ** ADAM_STEP)
        v_hat = vn / (1.0 - ADAM_B2 ** ADAM_STEP)
        d_ref[...] = -ADAM_LR * (m_hat / (jnp.sqrt(v_hat) + ADAM_EPS) + ADAM_WD * w_ref[...])
        mo_ref[...] = mn
        vo_ref[...] = vn

    spec = pl.BlockSpec((tr, tc), lambda i, j, h: (i, h[0] * nj + j))
    g_spec = spec if g_rows == rows else pl.BlockSpec((g_rows, tc), lambda i, j, h: (0, h[0] * nj + j))
    grid_spec = pltpu.PrefetchScalarGridSpec(
        num_scalar_prefetch=1, grid=(rows // tr, nj),
        in_specs=[spec, g_spec, spec, spec] + [any_spec] * len(extra_args), out_specs=[spec] * 4)
    first_into = 5 + (after is not None)
    return pl.pallas_call(
        body, name=name, grid_spec=grid_spec, out_shape=[jax.ShapeDtypeStruct((rows, cols), F32)] * 4,
        input_output_aliases={} if into is None else {first_into + k: k for k in range(4)},
        compiler_params=_cparams(("parallel", "parallel")),
    )(jnp.zeros((1,), jnp.int32) if half is None else half, w, g, m, v, *extra_args)


def _matmul_dw_pair(a_me, a_sib, b_me, b_sib, core_idx, *, shards_on, name, after=None, part=(0, 1)):
    T, M = a_me.shape
    N = b_me.shape[1]
    if shards_on == "rows":
        p, count = part
        tm, hc = M // N_CHIPS, N // 2
        hp = hc // count
        tn = _pick(hp, 512)
        per = hp // tn
        grid = (N_CHIPS, per)
        a_spec = pl.BlockSpec((T, tm), lambda i, n, h: (0, i))
        b_me_spec = pl.BlockSpec((T, tn), lambda i, n, h: (0, (h[0] * count + p) * per + n))
        b_sib_spec = pl.BlockSpec((T, tn), lambda i, n, h: (0, p * per + n))
        out_spec = pl.BlockSpec((None, tm, tn), lambda i, n, h: (i, 0, n))
        out_shape = jax.ShapeDtypeStruct((N_CHIPS, tm, hp), BF16)
    else:
        tm, hc = _pick(M, 1024), N // N_CHIPS // 2
        grid = (M // tm, N_CHIPS)
        a_spec = pl.BlockSpec((T, tm), lambda i, j, h: (0, i))
        b_me_spec = pl.BlockSpec((T, hc), lambda i, j, h: (0, 2 * j + h[0]))
        b_sib_spec = pl.BlockSpec((T, hc), lambda i, j, h: (0, j))
        out_spec = pl.BlockSpec((None, tm, hc), lambda i, j, h: (j, i, 0))
        out_shape = jax.ShapeDtypeStruct((N_CHIPS, M, hc), BF16)
    extra_specs, extra_args = ([], []) if after is None else ([pl.BlockSpec(memory_space=pl.ANY)], [after])

    def body(h_ref, am_ref, as_ref, bm_ref, bs_ref, *rest):
        o_ref = rest[len(extra_args)]
        o_ref[...] = (_dot_tn(am_ref[...], bm_ref[...]) + _dot_tn(as_ref[...], bs_ref[...])).astype(BF16)

    grid_spec = pltpu.PrefetchScalarGridSpec(
        num_scalar_prefetch=1, grid=grid, in_specs=[a_spec, a_spec, b_me_spec, b_sib_spec] + extra_specs,
        out_specs=out_spec)
    return pl.pallas_call(
        body, name=name, grid_spec=grid_spec, out_shape=out_shape, compiler_params=_cparams(("parallel", "parallel")),
    )(core_idx, a_me, a_sib, b_me, b_sib, *extra_args)


def _chip_sum(pair, landed, slots, *, name, block_bytes=1 << 20, part=(0, 1), into=None):
    p, count = part
    _, r, hp = pair.shape
    tr, tc = _tile2d(r, hp, block_bytes, 16)
    ncb = hp // tc
    extra_specs, extra_args = ([], []) if into is None else ([pl.BlockSpec(memory_space=pl.ANY)], [into])

    def body(s_ref, own_ref, l0_ref, l1_ref, l2_ref, *rest):
        rest[-1][...] = ((own_ref[...].astype(F32) + l0_ref[...].astype(F32)) + l1_ref[...].astype(F32)
                         ) + l2_ref[...].astype(F32)

    def slab(which):
        return pl.BlockSpec((None, tr, tc), lambda i, k, s: (s[which], i, k))

    grid_spec = pltpu.PrefetchScalarGridSpec(
        num_scalar_prefetch=1, grid=(r // tr, ncb),
        in_specs=[slab(0), slab(1), slab(2), slab(3)] + extra_specs,
        out_specs=pl.BlockSpec((tr, tc), lambda i, k, s: (i, (s[4] * count + p) * ncb + k)))
    return pl.pallas_call(
        body, name=name, grid_spec=grid_spec, out_shape=jax.ShapeDtypeStruct((r, 2 * hp * count), F32),
        input_output_aliases={} if into is None else {5: 0},
        compiler_params=_cparams(("parallel", "parallel")),
    )(slots, pair, landed, landed, landed, *extra_args)


def _stack_sum(x, *, name, out_dtype=F32, block_bytes=1 << 20):
    s, r, c = x.shape
    tr = _pick(r, max(8, block_bytes // (4 * c)), 16) if r % 16 == 0 else r

    def body(x_ref, o_ref):
        acc = x_ref[0].astype(F32)
        for j in range(1, s):
            acc = acc + x_ref[j].astype(F32)
        o_ref[...] = acc.astype(out_dtype)

    return pl.pallas_call(
        body, name=name, grid=(r // tr,),
        in_specs=[pl.BlockSpec((s, tr, c), lambda i: (0, i, 0))], out_specs=pl.BlockSpec((tr, c), lambda i: (i, 0)),
        out_shape=jax.ShapeDtypeStruct((r, c), out_dtype), compiler_params=_cparams(("parallel",)),
    )(x)


HBM = pl.BlockSpec(memory_space=pltpu.HBM)


def _place():
    x, y, c = lax.axis_index("x"), lax.axis_index("y"), lax.axis_index("c")
    other_chips = [(1 - x, y), (x, 1 - y), (1 - x, 1 - y)]
    return x, y, c, other_chips


def _half_cols(cols, which):
    hc = cols // 2
    return pl.ds(pl.multiple_of(which * hc, LANES), hc)


SEM = pl.BlockSpec(memory_space=pltpu.SEMAPHORE)
ANY = pl.BlockSpec(memory_space=pl.ANY)
SIDE_EFFECT = pltpu.SideEffectType.DATAFLOW_SIDE_EFFECTING
TOKEN_SHAPE = (8, LANES)


def _hbm(shape, dtype):
    return pltpu.HBM(shape, dtype)


def _in_hbm(a):
    return pltpu.with_memory_space_constraint(a, pltpu.HBM)


def _gather_copy(src_ref, land_ref, ssem, rsem, k, chip_of_block, to, c):
    cols = src_ref.shape[1]
    return pltpu.make_async_remote_copy(
        src_ref=src_ref.at[:, _half_cols(cols, c)], dst_ref=land_ref.at[chip_of_block, :, _half_cols(cols, c)],
        send_sem=ssem.at[k], recv_sem=rsem.at[k], device_id=to, device_id_type=MESH)


def _gather_start(shards, *, name, after=()):
    n = len(shards)
    after = list(after)

    def body(*refs):
        srcs, lands = refs[:n], refs[n:2 * n]
        outs = refs[2 * n + len(after):]
        token = outs[-1]
        x, y, c, chips = _place()
        me = 2 * x + y
        for a in range(n):
            ssem, rsem = outs[4 * a], outs[4 * a + 1]
            for k, (cx, cy) in enumerate(chips):
                _gather_copy(srcs[a], lands[a], ssem, rsem, k, me, (cx, cy, c), c).start()
        token[...] = jnp.zeros_like(token)

    out_shape, out_specs, aliases = [], [], {}
    for a, s in enumerate(shards):
        out_shape += [pltpu.SemaphoreType.DMA((3,)), pltpu.SemaphoreType.DMA((3,)), _hbm(s.shape, s.dtype),
                      _hbm((N_CHIPS,) + s.shape, s.dtype)]
        out_specs += [SEM, SEM, HBM, HBM]
        aliases[a] = 4 * a + 2
        aliases[n + a] = 4 * a + 3
    out_shape.append(jax.ShapeDtypeStruct(TOKEN_SHAPE, F32))
    out_specs.append(pl.BlockSpec(memory_space=pltpu.VMEM))
    lands = [_in_hbm(lax.empty((N_CHIPS,) + s.shape, s.dtype)) for s in shards]
    res = pl.pallas_call(
        body, name=name, in_specs=[HBM] * (2 * n) + [ANY] * len(after), out_specs=out_specs, out_shape=out_shape,
        input_output_aliases=aliases, compiler_params=pltpu.CompilerParams(has_side_effects=SIDE_EFFECT),
    )(*[_in_hbm(s) for s in shards], *lands, *after)
    return [tuple(res[4 * a:4 * a + 4]) for a in range(n)], res[-1]


def _wait_call(wait_fn, parts, after, *, name):
    ssem, rsem, src, land = parts
    after = list(after) if isinstance(after, (list, tuple)) else [after]

    def body(src_ref, land_ref, ssem_ref, rsem_ref, *rest):
        wait_fn(src_ref, land_ref, ssem_ref, rsem_ref)

    return pl.pallas_call(
        body, name=name, in_specs=[HBM, HBM, SEM, SEM] + [ANY] * len(after), out_specs=[HBM, HBM],
        out_shape=[_hbm(src.shape, src.dtype), _hbm(land.shape, land.dtype)], input_output_aliases={0: 0, 1: 1},
        compiler_params=pltpu.CompilerParams(has_side_effects=SIDE_EFFECT),
    )(src, land, ssem, rsem, *after)


ALL_CHIPS = (0, 1, 2)


def _gather_wait(parts, after, *, name, ks=ALL_CHIPS):
    def wait(src_ref, land_ref, ssem_ref, rsem_ref):
        x, y, c, chips = _place()
        for k in ks:
            cx, cy = chips[k]
            cp = _gather_copy(src_ref, land_ref, ssem_ref, rsem_ref, k, 2 * cx + cy, (x, y, c), c)
            cp.wait_send()
            cp.wait_recv()

    src, land = _wait_call(wait, parts, after, name=name)
    return (parts[0], parts[1], src, land)


def _forward_copy(buf_ref, ssem, rsem, k, slab, which, to):
    part = buf_ref.at[slab, :, _half_cols(buf_ref.shape[2], which)]
    return pltpu.make_async_remote_copy(
        src_ref=part, dst_ref=part, send_sem=ssem.at[k], recv_sem=rsem.at[k], device_id=to, device_id_type=MESH)


def _sibling_forward(land, *, name, ks=ALL_CHIPS):
    def body(_, buf, send_sems, recv_sems):
        x, y, c, chips = _place()
        copies = []
        for k in ks:
            cx, cy = chips[k]
            cp = _forward_copy(buf, send_sems, recv_sems, k, 2 * cx + cy, c, (x, y, 1 - c))
            cp.start()
            copies.append(cp)
        for k in ks:
            cx, cy = chips[k]
            _forward_copy(buf, send_sems, recv_sems, k, 2 * cx + cy, 1 - c, (x, y, c)).wait_recv()
        for cp in copies:
            cp.wait_send()

    return pl.pallas_call(
        body, name=name, in_specs=[HBM], out_specs=HBM, out_shape=jax.ShapeDtypeStruct(land.shape, land.dtype),
        input_output_aliases={0: 0},
        scratch_shapes=[pltpu.SemaphoreType.DMA((3,)), pltpu.SemaphoreType.DMA((3,))],
    )(land)


def _share_copy(buf_ref, ssem, rsem, a, which, to):
    part = buf_ref.at[:, _half_cols(buf_ref.shape[1], which)]
    return pltpu.make_async_remote_copy(
        src_ref=part, dst_ref=part, send_sem=ssem.at[a], recv_sem=rsem.at[a], device_id=to, device_id_type=MESH)


def _share_start(arrays, *, name):
    n = len(arrays)

    def body(*refs):
        bufs, ssem, rsem, token = refs[:n], refs[n], refs[n + 1], refs[-1]
        x, y, c, _ = _place()
        for a in range(n):
            _share_copy(bufs[a], ssem, rsem, a, c, (x, y, 1 - c)).start()
        token[...] = jnp.zeros_like(token)

    res = pl.pallas_call(
        body, name=name, in_specs=[HBM] * n,
        out_specs=[SEM, SEM] + [HBM] * n + [pl.BlockSpec(memory_space=pltpu.VMEM)],
        out_shape=[pltpu.SemaphoreType.DMA((n,)), pltpu.SemaphoreType.DMA((n,))]
        + [_hbm(b.shape, b.dtype) for b in arrays] + [jax.ShapeDtypeStruct(TOKEN_SHAPE, F32)],
        input_output_aliases={a: 2 + a for a in range(n)},
        compiler_params=pltpu.CompilerParams(has_side_effects=SIDE_EFFECT),
    )(*[_in_hbm(b) for b in arrays])
    return (res[0], res[1], list(res[2:2 + n])), res[-1]


def _share_wait(parts, after, *, name):
    ssem, rsem, bufs = parts
    n = len(bufs)
    after = list(after) if isinstance(after, (list, tuple)) else [after]

    def body(*refs):
        buf_refs, ssem_ref, rsem_ref = refs[:n], refs[n], refs[n + 1]
        x, y, c, _ = _place()
        for a in range(n):
            _share_copy(buf_refs[a], ssem_ref, rsem_ref, a, c, (x, y, c)).wait_send()
            _share_copy(buf_refs[a], ssem_ref, rsem_ref, a, 1 - c, (x, y, c)).wait_recv()

    return pl.pallas_call(
        body, name=name, in_specs=[HBM] * n + [SEM, SEM] + [ANY] * len(after), out_specs=[HBM] * n,
        out_shape=[_hbm(b.shape, b.dtype) for b in bufs], input_output_aliases={a: a for a in range(n)},
        compiler_params=pltpu.CompilerParams(has_side_effects=SIDE_EFFECT),
    )(*bufs, ssem, rsem, *after)


def _scatter_copy(src_ref, land_ref, ssem, rsem, k, src_slab, dst_slab, to):
    return pltpu.make_async_remote_copy(
        src_ref=src_ref.at[src_slab], dst_ref=land_ref.at[dst_slab], send_sem=ssem.at[k], recv_sem=rsem.at[k],
        device_id=to, device_id_type=MESH)


def _scatter_start(part, *, name):
    def start(src_ref, land_ref, ssem, rsem):
        x, y, c, chips = _place()
        me = 2 * x + y
        for k, (cx, cy) in enumerate(chips):
            _scatter_copy(src_ref, land_ref, ssem, rsem, k, 2 * cx + cy, me, (cx, cy, c)).start()

    return _split_start(start, part, part.shape, N_CHIPS - 1, name=name)


def _scatter_wait(parts, after, *, name):
    def wait(src_ref, land_ref, ssem_ref, rsem_ref):
        x, y, c, chips = _place()
        for k, (cx, cy) in enumerate(chips):
            idx = 2 * cx + cy
            cp = _scatter_copy(src_ref, land_ref, ssem_ref, rsem_ref, k, idx, idx, (x, y, c))
            cp.wait_send()
            cp.wait_recv()

    return _wait_call(wait, parts, after, name=name)


def _split_start(start_fn, src, land_shape, n_sems, *, name):
    def body(src_ref, land_ref, ssem, rsem, src_out, land_out, token):
        start_fn(src_ref, land_ref, ssem, rsem)
        token[...] = jnp.zeros_like(token)

    res = pl.pallas_call(
        body, name=name, in_specs=[HBM, HBM], out_specs=[SEM, SEM, HBM, HBM, pl.BlockSpec(memory_space=pltpu.VMEM)],
        out_shape=[pltpu.SemaphoreType.DMA((n_sems,)), pltpu.SemaphoreType.DMA((n_sems,)), _hbm(src.shape, src.dtype),
                   _hbm(land_shape, src.dtype), jax.ShapeDtypeStruct(TOKEN_SHAPE, F32)],
        input_output_aliases={0: 2, 1: 3}, compiler_params=pltpu.CompilerParams(has_side_effects=SIDE_EFFECT),
    )(_in_hbm(src), _in_hbm(lax.empty(land_shape, src.dtype)))
    return tuple(res[:4]), res[4]


def _sibling_copies(src_ref, land_ref, ssem, rsem, k0, groups, which, to):
    def copy(k, src, dst):
        return pltpu.make_async_remote_copy(
            src_ref=src, dst_ref=dst, send_sem=ssem.at[k], recv_sem=rsem.at[k], device_id=to, device_id_type=MESH)

    if groups == 0:
        return [copy(k0, src_ref, land_ref)]
    hw = src_ref.shape[1] // groups // 2
    return [copy(k0 + j, src_ref.at[:, pl.ds(pl.multiple_of((2 * j + which) * hw, LANES), hw)],
                 land_ref.at[:, j * hw:(j + 1) * hw]) for j in range(groups)]


def _to_sibling_start(items, *, name):
    n = len(items)
    shapes = [a.shape if g == 0 else (a.shape[0], a.shape[1] // 2) for a, g in items]
    first = [sum(max(g, 1) for _, g in items[:k]) for k in range(n + 1)]

    def body(*refs):
        srcs, lands, ssem, rsem, token = refs[:n], refs[n:2 * n], refs[2 * n], refs[2 * n + 1], refs[-1]
        x, y, c, _ = _place()
        for k, (_, g) in enumerate(items):
            for cp in _sibling_copies(srcs[k], lands[k], ssem, rsem, first[k], g, 1 - c, (x, y, 1 - c)):
                cp.start()
        token[...] = jnp.zeros_like(token)

    res = pl.pallas_call(
        body, name=name, in_specs=[HBM] * (2 * n),
        out_specs=[SEM, SEM] + [HBM] * (2 * n) + [pl.BlockSpec(memory_space=pltpu.VMEM)],
        out_shape=[pltpu.SemaphoreType.DMA((first[n],)), pltpu.SemaphoreType.DMA((first[n],))]
        + [_hbm(a.shape, a.dtype) for a, _ in items] + [_hbm(s, a.dtype) for s, (a, _) in zip(shapes, items)]
        + [jax.ShapeDtypeStruct(TOKEN_SHAPE, F32)],
        input_output_aliases={k: 2 + k for k in range(2 * n)},
        compiler_params=pltpu.CompilerParams(has_side_effects=SIDE_EFFECT),
    )(*[_in_hbm(a) for a, _ in items], *[_in_hbm(lax.empty(s, a.dtype)) for s, (a, _) in zip(shapes, items)])
    return [(res[0], res[1], first[k], g, res[2 + k], res[2 + n + k]) for k, (_, g) in enumerate(items)], res[-1]


def _from_sibling(flight, after, *, name):
    ssem, rsem, k0, groups, src, land = flight

    def wait(src_ref, land_ref, ssem_ref, rsem_ref):
        x, y, c, _ = _place()
        for cp in _sibling_copies(src_ref, land_ref, ssem_ref, rsem_ref, k0, groups, 1 - c, (x, y, c)):
            cp.wait_send()
            cp.wait_recv()

    return _wait_call(wait, (ssem, rsem, src, land), after, name=name)


def _dev_peers(x, y, c, chips):
    return [(x, y, 1 - c)] + [(cx, cy, c) for cx, cy in chips] + [(cx, cy, 1 - c) for cx, cy in chips]


def _dev_gather_start(part, *, name):
    def start(src_ref, land_ref, ssem, rsem):
        x, y, c, chips = _place()
        for k, to in enumerate(_dev_peers(x, y, c, chips)):
            pltpu.make_async_remote_copy(
                src_ref=src_ref, dst_ref=land_ref.at[4 * x + 2 * y + c], send_sem=ssem.at[k], recv_sem=rsem.at[k],
                device_id=to, device_id_type=MESH).start()

    return _split_start(start, part, (N_DEV,) + part.shape, N_DEV - 1, name=name)


def _dev_gather_wait(parts, after, *, name):
    def wait(src_ref, land_ref, ssem_ref, rsem_ref):
        x, y, c, chips = _place()
        for k, (px, py, pc) in enumerate(_dev_peers(x, y, c, chips)):
            cp = pltpu.make_async_remote_copy(
                src_ref=src_ref, dst_ref=land_ref.at[4 * px + 2 * py + pc], send_sem=ssem_ref.at[k],
                recv_sem=rsem_ref.at[k], device_id=(x, y, c), device_id_type=MESH)
            cp.wait_send()
            cp.wait_recv()

    return _wait_call(wait, parts, after, name=name)[1]


def _sibling_share_halves(arrays, *, name):
    n = len(arrays)

    def body(*refs):
        bufs = refs[n:2 * n]
        send_sems, recv_sems = refs[2 * n:]
        x, y, c, _ = _place()
        copies = []
        for a in range(n):
            mine = bufs[a].at[:, _half_cols(bufs[a].shape[1], c)]
            cp = pltpu.make_async_remote_copy(
                src_ref=mine, dst_ref=mine, send_sem=send_sems.at[a], recv_sem=recv_sems.at[a],
                device_id=(x, y, 1 - c), device_id_type=MESH)
            cp.start()
            copies.append(cp)
        for a in range(n):
            theirs = bufs[a].at[:, _half_cols(bufs[a].shape[1], 1 - c)]
            pltpu.make_async_remote_copy(
                src_ref=theirs, dst_ref=theirs, send_sem=send_sems.at[a], recv_sem=recv_sems.at[a],
                device_id=(x, y, c), device_id_type=MESH).wait_recv()
        for cp in copies:
            cp.wait_send()

    return pl.pallas_call(
        body, name=name, in_specs=[HBM] * n, out_specs=[HBM] * n,
        out_shape=[jax.ShapeDtypeStruct(h.shape, h.dtype) for h in arrays],
        input_output_aliases={a: a for a in range(n)},
        scratch_shapes=[pltpu.SemaphoreType.DMA((n,)), pltpu.SemaphoreType.DMA((n,))],
    )(*arrays)


def _pack(arrays, rows_multiple=16, width=LANES):
    flat = jnp.concatenate([a.astype(F32).reshape(-1) for a in arrays])
    total = flat.shape[0]
    rows = -(-total // width)
    rows = -(-rows // rows_multiple) * rows_multiple
    return jnp.pad(flat, (0, rows * width - total)).reshape(rows, width)


def _unpack(buf, shapes):
    flat = buf.reshape(-1)
    out, off = [], 0
    for s in shapes:
        n = math.prod(s)
        out.append(flat[off:off + n].reshape(s))
        off += n
    return out


def kernel(x, norm_pre, norm_post, gla_w_in, gla_w_gate2, gla_b_gate, gla_o_gain, gla_w_out, sgu_w_in, sgu_ln_gain, sgu_ln_bias, sgu_w_spatial, sgu_b_spatial, sgu_w_out, loss_target, m_norm_pre, m_norm_post, m_gla_w_in, m_gla_w_gate2, m_gla_b_gate, m_gla_o_gain, m_gla_w_out, m_sgu_w_in, m_sgu_ln_gain, m_sgu_ln_bias, m_sgu_w_spatial, m_sgu_b_spatial, m_sgu_w_out, v_norm_pre, v_norm_post, v_gla_w_in, v_gla_w_gate2, v_gla_b_gate, v_gla_o_gain, v_gla_w_out, v_sgu_w_in, v_sgu_ln_gain, v_sgu_ln_bias, v_sgu_w_spatial, v_sgu_b_spatial, v_sgu_w_out):
    _, t, d = x.shape
    dk = d // 2
    ws = gla_w_in.shape[2]
    wp = -(-ws // LANES) * LANES
    lay = (ws, wp)
    chip =2 * lax.axis_index("x") + lax.axis_index("y")
    core = lax.axis_index("c")
    core_idx = core.astype(jnp.int32).reshape(1)
    others = jnp.arange(N_CHIPS - 1, dtype=jnp.int32)
    others = others + (others >= chip).astype(jnp.int32)
    slots = jnp.concatenate([chip.astype(jnp.int32).reshape(1), others, core_idx])

    x0 = x[0]
    target = loss_target[0]

    wt_in_g, mt_in_g, vt_in_g = gla_w_in[0].T, m_gla_w_in[0].T, v_gla_w_in[0].T

    small_shard = _pack([gla_w_gate2[0], sgu_ln_gain[0], sgu_ln_bias[0]], rows_multiple=8, width=2 * LANES)
    own = [small_shard, jnp.pad(wt_in_g.astype(BF16), ((0, wp - ws), (0, 0)))]
    in_flight, token = _gather_start(own, name="gather_start_a")
    own_later = [gla_w_out[0].astype(BF16), sgu_w_in[0].astype(BF16), sgu_w_out[0].astype(BF16)]
    in_flight_later, token_later = _gather_start(own_later, name="gather_start_b", after=[token])
    own, in_flight = own + own_later, in_flight + in_flight_later

    def with_own(i, land):
        return lax.dynamic_update_slice(land, own[i][None], (chip, 0, 0))

    def arrived(i, after, name):
        land = _gather_wait(in_flight[i], after, name=name + "_wait")[3]
        return with_own(i, _sibling_forward(land, name=name + "_share"))

    h0 = _norm_pre(x0, norm_pre[0:1] + token[0:1, 0:1] + token_later[0:1, 0:1], name="pre0")
    g_small = arrived(0, h0, "w_small")
    wt_g = arrived(1, [g_small, wt_in_g, mt_in_g, vt_in_g], "w_gla_in").reshape(N_CHIPS * wp, d)
    shard_shapes = [gla_w_gate2.shape[1:], sgu_ln_gain.shape[1:], sgu_ln_bias.shape[1:]]
    per_chip = [_unpack(g_small[j], shard_shapes) for j in range(N_CHIPS)]
    w2_full = jnp.concatenate([p[0] for p in per_chip], axis=1)
    ln_gain = jnp.concatenate([p[1] for p in per_chip], axis=0)[None, :]
    ln_bias = jnp.concatenate([p[2] for p in per_chip], axis=0)[None, :]
    w2p = jnp.pad(w2_full, ((0, LANES - GLA_GATE_RANK), (0, 0)))

    pos_chunk = jnp.arange(SGU_BLOCK) // CHUNK
    mask = pos_chunk[:, None] >= pos_chunk[None, :]
    ws_masked = jnp.where(mask[None], sgu_w_spatial[0], 0.0)
    ws_masked_t = ws_masked.transpose(0, 2, 1)
    bs_t = sgu_b_spatial[0].T

    proj0 = _matmul(h0, wt_g, mode="nt", out_dtype=F32, name="gla_in", tn=wp)
    o0, a0, s_before, s_final = _gla_fwd(proj0, w2p, gla_b_gate, gla_o_gain, lay, name="gla_scan")
    w_out_g = arrived(2, a0, "w_gla_out").reshape(d, d)
    y0 = _matmul(a0, w_out_g, mode="nn", out_dtype=F32, name="gla_out")
    x1, h1 = _post_then_pre(x0, y0, norm_post[0:1], norm_pre[1:2], name="post0_pre1")
    g_wi_s = arrived(3, h1, "w_sgu_in")
    proj1 = _matmul(h1, g_wi_s, mode="nn", out_dtype=F32, name="sgu_in", b_shards=True)
    a1 = _sgu_fwd(proj1, ln_gain, ln_bias, ws_masked, bs_t, name="sgu_gate")
    w_out_s = arrived(4, a1, "w_sgu_out").reshape(d, d)
    acts, tok = _to_sibling_start([(a1, 0), (a0, 0), (h1, 0), (h0, 1)], name="acts_to_sibling")
    a1, a0, h1, h0 = [f[4] for f in acts]
    y1 = _matmul(a1, w_out_s, mode="nn", out_dtype=F32, name="sgu_out", after=tok)
    loss_part, dx2, dy1, d_post1 = _loss_head(x1, y1, norm_post[1:2], target, name="loss_head")

    def behind(small, token):
        return small + token[0:1, 0:1]

    def pair_gradient(a_sent, b_sent, after, shards_on, name):
        a_me, a_sib = _from_sibling(a_sent, after, name=name + "_a_wait")
        b_me, b_sib = _from_sibling(b_sent, [a_sib] + list(after), name=name + "_b_wait")
        pair = _matmul_dw_pair(a_me, a_sib, b_me, b_sib, core_idx, shards_on=shards_on,
                               name=name + "_pair")
        return _scatter_start(pair, name=name + "_start")

    def reduced(flight, after, name):
        pair, landed = _scatter_wait(flight, after, name=name + "_wait")
        return _chip_sum(pair, landed, slots, name=name + "_sum")

    (dy1_sent,), tok = _to_sibling_start([(dy1, 1)], name="dy1_to_sibling")
    dy1 = dy1_sent[4]
    da1 = _matmul(dy1, w_out_s, mode="nt", out_dtype=F32, name="d_sgu_act", after=tok)
    fl_wo_s, tok = pair_gradient(acts[0], dy1_sent, [da1], "rows", "g_sgu_out")
    dproj1, d_ws, d_bs_t, d_lg, d_lb = _sgu_bwd(da1, proj1, ln_gain, behind(ln_bias, tok), ws_masked, ws_masked_t,
                                                bs_t, name="sgu_gate_bwd")
    (dp1_sent,), tok = _to_sibling_start([(dproj1, N_CHIPS)], name="dproj1_to_sibling")
    dproj1 = dp1_sent[4]
    dh1 = _matmul_nt_shards(dproj1, g_wi_s, out_dtype=F32, name="d_sgu_h", after=tok)
    fl_wi_s, tok = pair_gradient(acts[2], dp1_sent, [dh1], "cols", "g_sgu_in")
    dx1, dy0, d_pre1, d_post0 = _mid_bwd(dx2, dh1, x1, behind(norm_pre[1:2], tok), y0, norm_post[0:1],
                                         name="pre1_post0_bwd")
    (dy0_sent,), tok = _to_sibling_start([(dy0, 1)], name="dy0_to_sibling")
    dy0 = dy0_sent[4]
    da0 = _matmul(dy0, w_out_g, mode="nt", out_dtype=F32, name="d_gla_act", after=tok)
    fl_wo_g, tok = pair_gradient(acts[1], dy0_sent, [da0], "rows", "g_gla_out")
    dproj0, d_og, d_bg, d_w2p = _gla_bwd(da0, o0, proj0, w2p, behind(gla_b_gate, tok), gla_o_gain, s_before, s_final,
                                         lay, name="gla_scan_bwd")
    early_shapes = [norm_post.shape, gla_b_gate.shape, gla_o_gain.shape, sgu_w_spatial.shape, sgu_b_spatial.shape,
                    (1, GLA_GATE_RANK, dk), (1, d), (1, d), (1, LANES)]
    early_part = _pack([jnp.concatenate([d_post0, d_post1], axis=0), d_bg, d_og, jnp.where(mask[None], d_ws, 0.0)[None],
                        d_bs_t.T[None], d_w2p[:GLA_GATE_RANK][None], d_lg, d_lb, loss_part])
    early_flight, tok = _dev_gather_start(early_part, name="small_early_start")
    (dp0_sent,), tok_sent = _to_sibling_start([(dproj0, 0)], name="dproj0_to_sibling")
    dproj0 = dp0_sent[4]
    dh0 = _matmul(dproj0, wt_g, mode="nn", out_dtype=F32, name="d_gla_h", tk=N_CHIPS * wp, after=tok_sent)
    a_me, a_sib = _from_sibling(dp0_sent, [dh0, tok], name="g_gla_in_a_wait")
    b_me, b_sib = _from_sibling(acts[3], [a_sib, dh0], name="g_gla_in_b_wait")
    fl_wi_g, tok_scatter = [], None
    for p in range(2):
        pair = _matmul_dw_pair(a_me, a_sib, b_me, b_sib, core_idx, shards_on="rows", part=(p, 2),
                               name=f"g_gla_in_pair{p}", after=tok_scatter)
        flight, tok_scatter = _scatter_start(pair, name=f"g_gla_in_start{p}")
        fl_wi_g.append(flight)
    r_wo_s = reduced(fl_wo_s, tok_scatter, "g_sgu_out")
    r_wi_s = reduced(fl_wi_s, r_wo_s, "g_sgu_in")
    r_wo_g = reduced(fl_wo_g, r_wi_s, "g_gla_out")
    sharing, tok = _share_start([r_wo_s, r_wi_s, r_wo_g], name="grads_share_a")
    grad_x, d_pre0 = _first_bwd(dx1, dh0, x0, behind(norm_pre[0:1], tok), name="pre0_bwd")

    late_part = _pack([jnp.concatenate([d_pre0, d_pre1], axis=0)])
    late_flight, tok = _dev_gather_start(late_part, name="small_late_start")

    def big_update(w, g, m, v, name, after=None):
        return [u[None] for u in _adamw(w[0], g, m[0], v[0], name=name, after=after)]

    g_wo_sgu, g_wi_sgu, g_wo_gla = _share_wait(sharing, [grad_x, tok], name="grads_share_a_wait")
    u_wo_sgu = big_update(sgu_w_out, g_wo_sgu, m_sgu_w_out, v_sgu_w_out, "adamw_sgu_w_out")
    u_wo_gla = big_update(gla_w_out, g_wo_gla, m_gla_w_out, v_gla_w_out, "adamw_gla_w_out")
    u_wi_sgu = big_update(sgu_w_in, g_wi_sgu, m_sgu_w_in, v_sgu_w_in, "adamw_sgu_w_in")

    def summed_over_devices(part, flight, after, shapes, name):
        land = _dev_gather_wait(flight, after, name=name + "_wait")
        every = lax.dynamic_update_slice(land, part[None], (2 * chip + core, 0, 0))
        return _unpack(_stack_sum(every, name=name + "_sum"), shapes)

    (g_post, g_bg, g_og, g_wsp, g_bsp, g_w2_full, g_lg_full, g_lb_full, loss_vec) = summed_over_devices(
        early_part, early_flight, [u_wo_gla[1], u_wo_sgu[1], u_wi_sgu[1]], early_shapes, "small_early")

    r_wi_g, behind_this = None, loss_vec
    for p, flight in enumerate(fl_wi_g):
        pair, landed = _scatter_wait(flight, behind_this, name=f"g_gla_in_wait{p}")
        r_wi_g = behind_this = _chip_sum(pair, landed, slots, part=(p, 2), into=r_wi_g, name=f"g_gla_in_sum{p}")
    sharing_b, tok = _share_start([r_wi_g], name="grads_share_b")
    mine = _adamw(wt_in_g, sharing_b[2][0], mt_in_g, vt_in_g, name="adamw_gla_w_in_mine", half=core_idx, after=tok)
    gt_wi_gla, = _share_wait(sharing_b, mine[1], name="grads_share_b_wait")
    u_wi_gla_t = _adamw(wt_in_g, gt_wi_gla, mt_in_g, vt_in_g, name="adamw_gla_w_in_theirs", half=1 - core_idx,
                        into=mine)
    u_wi_gla = [u.T[None] for u in u_wi_gla_t]
    g_pre, = summed_over_devices(late_part, late_flight, u_wi_gla_t[1], [norm_pre.shape], "small_late")
    loss = loss_vec[0, 0]
    g_w2 = lax.dynamic_slice_in_dim(g_w2_full, chip * (dk // N_CHIPS), dk // N_CHIPS, axis=2)
    g_lg = lax.dynamic_slice_in_dim(g_lg_full, chip * (d // N_CHIPS), d // N_CHIPS, axis=1)
    g_lb = lax.dynamic_slice_in_dim(g_lb_full, chip * (d // N_CHIPS), d // N_CHIPS, axis=1)

    small_w = [norm_pre, norm_post, gla_b_gate, gla_o_gain, sgu_w_spatial, sgu_b_spatial, gla_w_gate2, sgu_ln_gain,
               sgu_ln_bias]
    small_g = [g_pre, g_post, g_bg, g_og, g_wsp, g_bsp, g_w2, g_lg, g_lb]
    small_m = [m_norm_pre, m_norm_post, m_gla_b_gate, m_gla_o_gain, m_sgu_w_spatial, m_sgu_b_spatial, m_gla_w_gate2,
               m_sgu_ln_gain, m_sgu_ln_bias]
    small_v = [v_norm_pre, v_norm_post, v_gla_b_gate, v_gla_o_gain, v_sgu_w_spatial, v_sgu_b_spatial, v_gla_w_gate2,
               v_sgu_ln_gain, v_sgu_ln_bias]
    own_shapes = [w.shape for w in small_w]
    _, s_dl, s_m, s_v = _adamw(_pack(small_w), _pack(small_g), _pack(small_m), _pack(small_v), name="adamw_small")
    dl_s, m_s, v_s = _unpack(s_dl, own_shapes), _unpack(s_m, own_shapes), _unpack(s_v, own_shapes)

    def ordered(small, kind):
        pre, post, bg, og, wsp, bsp, w2, lg, lb = small
        return [pre, post, u_wi_gla[kind], w2, bg, og, u_wo_gla[kind], u_wi_sgu[kind], lg, lb, wsp, bsp, u_wo_sgu[kind]]

    return (loss, grad_x[None], *ordered(small_g, 0), *ordered(dl_s, 1), *ordered(m_s, 2), *ordered(v_s, 3))
```

```python
import functools
import math

import jax
import jax.numpy as jnp
from jax import lax
from jax.experimental import pallas as pl
from jax.experimental.pallas import tpu as pltpu

F32 = jnp.float32
BF16 = jnp.bfloat16
MESH = pl.DeviceIdType.MESH

EPS = 1e-6
CHUNK = 64
GLA_HEADS = 4
GLA_GATE_RANK = 16
GLA_TAU = 16.0
SGU_BLOCK = 128
SGU_GROUPS = 8
N_CHIPS = 4
N_DEV = 8
LANES = 128

ADAM_LR = 0.001
ADAM_B1 = 0.9
ADAM_B2 = 0.999
ADAM_EPS = 1e-08
ADAM_WD = 0.01
ADAM_STEP = 10

VMEM_LIMIT = 56 * 1024 * 1024


def _cparams(sem=None):
    return pltpu.CompilerParams(dimension_semantics=sem, vmem_limit_bytes=VMEM_LIMIT)


def _pick(n, cap, unit=LANES):
    best = None
    for t in range(unit, min(n, cap) + 1, unit):
        if n % t == 0:
            best = t
    assert best is not None, (n, cap, unit)
    return best


def _dot(a, b, dims):
    return lax.dot_general(a, b, (dims, ((), ())), preferred_element_type=F32)


def _dot_nn(a, b):
    return _dot(a, b, ((1,), (0,)))


def _dot_nt(a, b):
    return _dot(a, b, ((1,), (1,)))


def _dot_tn(a, b):
    return _dot(a, b, ((0,), (0,)))


def _matmul(a, b, *, mode, out_dtype, name, tm=1024, tn=512, tk=2048, b_shards=False, out_shards=False, after=None,
            out_rows=None):
    if mode == "tn":
        K, M = a.shape
    else:
        M, K = a.shape
    if b_shards:
        ns, br, bc = b.shape
        if mode == "nt":
            N, Kb = br, ns * bc
        else:
            Kb, N = br, ns * bc
    else:
        if mode == "nt":
            N, Kb = b.shape
        else:
            Kb, N = b.shape
    assert K == Kb, (a.shape, b.shape, mode)
    tm = _pick(M, tm)
    tk = _pick(K, tk)
    if b_shards and mode != "nt":
        tn = _pick(bc, tn)
    elif out_shards:
        tn = _pick(N // N_CHIPS, tn)
    else:
        tn = _pick(N, tn)
    if b_shards and mode == "nt":
        tk = _pick(bc, tk)
    nk = K // tk
    grid = (M // tm, N // tn, nk)

    if mode == "tn":
        a_spec = pl.BlockSpec((tk, tm), lambda i, j, k: (k, i))
    else:
        a_spec = pl.BlockSpec((tm, tk), lambda i, j, k: (i, k))
    if b_shards:
        if mode == "nt":
            per = bc // tk
            b_spec = pl.BlockSpec((None, tn, tk), lambda i, j, k: (k // per, j, k % per))
        else:
            per = bc // tn
            b_spec = pl.BlockSpec((None, tk, tn), lambda i, j, k: (j // per, k, j % per))
    elif mode == "nt":
        b_spec = pl.BlockSpec((tn, tk), lambda i, j, k: (j, k))
    else:
        b_spec = pl.BlockSpec((tk, tn), lambda i, j, k: (k, j))
    if out_shards:
        per_o = (N // N_CHIPS) // tn
        out_spec = pl.BlockSpec((None, tm, tn), lambda i, j, k: (j // per_o, i, j % per_o))
        out_shape = jax.ShapeDtypeStruct((N_CHIPS, M, N // N_CHIPS), out_dtype)
    else:
        out_spec = pl.BlockSpec((tm, tn), lambda i, j, k: (i, j))
        out_shape = jax.ShapeDtypeStruct((M if out_rows is None else out_rows, N), out_dtype)

    dims = {"nn": ((1,), (0,)), "nt": ((1,), (1,)), "tn": ((0,), (0,))}[mode]

    def body(a_ref, b_ref, *rest):
        o_ref, scratch = (rest[1], rest[2:]) if after is not None else (rest[0], rest[1:])
        part = _dot(a_ref[...].astype(BF16), b_ref[...].astype(BF16), dims)
        if nk == 1:
            o_ref[...] = part.astype(out_dtype)
        else:
            acc_ref, = scratch
            k = pl.program_id(2)

            @pl.when(k == 0)
            def _():
                acc_ref[...] = part

            @pl.when(k > 0)
            def _():
                acc_ref[...] += part

            @pl.when(k == nk - 1)
            def _():
                o_ref[...] = acc_ref[...].astype(out_dtype)

    extra_specs, extra_args = ([], []) if after is None else ([pl.BlockSpec(memory_space=pl.ANY)], [after])
    return pl.pallas_call(
        body, name=name, grid=grid, in_specs=[a_spec, b_spec] + extra_specs, out_specs=out_spec, out_shape=out_shape,
        scratch_shapes=[] if nk == 1 else [pltpu.VMEM((tm, tn), F32)],
        compiler_params=_cparams(("parallel", "parallel", "arbitrary")),
    )(a, b, *extra_args)


def _matmul_into_cols(a, w, which, buf, *, name, tm=1024):
    M, K = a.shape
    _, N, _ = w.shape
    tm = _pick(M, tm)

    def body(which_ref, a_ref, w_ref, buf_ref, o_ref):
        o_ref[...] = _dot_nt(a_ref[...], w_ref[...])

    grid_spec = pltpu.PrefetchScalarGridSpec(
        num_scalar_prefetch=1, grid=(M // tm,),
        in_specs=[pl.BlockSpec((tm, K), lambda i, s: (i, 0)), pl.BlockSpec((None, N, K), lambda i, s: (s[1], 0, 0)),
                  pl.BlockSpec(memory_space=pl.ANY)],
        out_specs=pl.BlockSpec((tm, N), lambda i, s: (i, s[0])))
    return pl.pallas_call(
        body, name=name, grid_spec=grid_spec, out_shape=jax.ShapeDtypeStruct(buf.shape, buf.dtype),
        input_output_aliases={3: 0}, compiler_params=_cparams(("parallel",)),
    )(which, a, w, buf)


def _matmul_nt_shards(a, b, *, out_dtype, name, tm=1024, tn=512, after=None):
    M, K = a.shape
    ns, N, kc = b.shape
    assert K == ns * kc
    tm, tn = _pick(M, tm), _pick(N, tn)

    def body(a_ref, *rest):
        b_refs, o_ref = rest[:ns], rest[ns + (after is not None)]
        acc = _dot_nt(a_ref[:, 0:kc], b_refs[0][...])
        for j in range(1, ns):
            acc += _dot_nt(a_ref[:, j * kc:(j + 1) * kc], b_refs[j][...])
        o_ref[...] = acc.astype(out_dtype)

    def shard(j):
        return pl.BlockSpec((None, tn, kc), lambda i, n: (j, n, 0))

    extra_specs, extra_args = ([], []) if after is None else ([pl.BlockSpec(memory_space=pl.ANY)], [after])
    return pl.pallas_call(
        body, name=name, grid=(M // tm, N // tn),
        in_specs=[pl.BlockSpec((tm, K), lambda i, n: (i, 0))] + [shard(j) for j in range(ns)] + extra_specs,
        out_specs=pl.BlockSpec((tm, tn), lambda i, n: (i, n)), out_shape=jax.ShapeDtypeStruct((M, N), out_dtype),
        compiler_params=_cparams(("parallel", "parallel")),
    )(a, *([b] * ns), *extra_args)


def _rstd(x):
    return lax.rsqrt(jnp.mean(x * x, axis=-1, keepdims=True) + EPS)


def _row_spec(tr, d):
    return pl.BlockSpec((tr, d), lambda i: (i, 0))


def _vec_spec(d):
    return pl.BlockSpec((1, d), lambda i: (0, 0))


def _acc_rows(ref, i, val, cols=slice(None)):
    @pl.when(i == 0)
    def _():
        ref[:, cols] = val

    @pl.when(i > 0)
    def _():
        ref[:, cols] += val


def _norm_pre(x, gain, *, name, tr=256):
    t, d = x.shape
    tr = _pick(t, tr, 8)

    def body(x_ref, g_ref, h_ref):
        xv = x_ref[...]
        h_ref[...] = (xv * _rstd(xv) * g_ref[...]).astype(BF16)

    return pl.pallas_call(
        body, name=name, grid=(t // tr,), in_specs=[_row_spec(tr, d), _vec_spec(d)], out_specs=_row_spec(tr, d),
        out_shape=jax.ShapeDtypeStruct((t, d), BF16), compiler_params=_cparams(("parallel",)),
    )(x, gain)


def _post_then_pre(x, y, post_gain, pre_gain, *, name, tr=256):
    t, d = x.shape
    tr = _pick(t, tr, 8)

    def body(x_ref, y_ref, pg_ref, ng_ref, xn_ref, h_ref):
        yv = y_ref[...]
        xn = x_ref[...] + yv * _rstd(yv) * pg_ref[...]
        xn_ref[...] = xn
        h_ref[...] = (xn * _rstd(xn) * ng_ref[...]).astype(BF16)

    return pl.pallas_call(
        body, name=name, grid=(t // tr,),
        in_specs=[_row_spec(tr, d), _row_spec(tr, d), _vec_spec(d), _vec_spec(d)],
        out_specs=[_row_spec(tr, d), _row_spec(tr, d)],
        out_shape=[jax.ShapeDtypeStruct((t, d), F32), jax.ShapeDtypeStruct((t, d), BF16)],
        compiler_params=_cparams(("parallel",)),
    )(x, y, post_gain, pre_gain)


def _norm_bwd(dy, n, r, gain):
    dn = dy * gain
    return r * (dn - n * jnp.mean(dn * n, axis=-1, keepdims=True))


def _loss_head(x, y, post_gain, target, *, name, tr=256):
    t, d = x.shape
    tr = _pick(t, tr, 8)

    def body(x_ref, y_ref, pg_ref, t_ref, loss_ref, dx_ref, dy_ref, dpg_ref):
        i = pl.program_id(0)
        yv = y_ref[...]
        r = _rstd(yv)
        n = yv * r
        err = x_ref[...] + n * pg_ref[...] - t_ref[...]
        dx = err * (1.0 / d)
        dx_ref[...] = dx
        part = 0.5 * jnp.sum(jnp.mean(err * err, axis=-1, keepdims=True), axis=0, keepdims=True)
        _acc_rows(loss_ref, i, jnp.broadcast_to(part, (1, LANES)))
        _acc_rows(dpg_ref, i, jnp.sum(dx * n, axis=0, keepdims=True))
        dy_ref[...] = _norm_bwd(dx, n, r, pg_ref[...]).astype(BF16)

    return pl.pallas_call(
        body, name=name, grid=(t // tr,),
        in_specs=[_row_spec(tr, d), _row_spec(tr, d), _vec_spec(d), _row_spec(tr, d)],
        out_specs=[_vec_spec(LANES), _row_spec(tr, d), _row_spec(tr, d), _vec_spec(d)],
        out_shape=[jax.ShapeDtypeStruct((1, LANES), F32), jax.ShapeDtypeStruct((t, d), F32),
                   jax.ShapeDtypeStruct((t, d), BF16), jax.ShapeDtypeStruct((1, d), F32)],
        compiler_params=_cparams(("arbitrary",)),
    )(x, y, post_gain, target)


def _mid_bwd(dx_out, dh, x, pre_gain, y_prev, post_gain_prev, *, name, tr=256):
    t, d = x.shape
    tr = _pick(t, tr, 8)

    def body(dxo_ref, dh_ref, x_ref, ng_ref, y_ref, pg_ref, dx_ref, dy_ref, dng_ref, dpg_ref):
        i = pl.program_id(0)
        xv = x_ref[...]
        r = _rstd(xv)
        xh = xv * r
        dhv = dh_ref[...]
        _acc_rows(dng_ref, i, jnp.sum(dhv * xh, axis=0, keepdims=True))
        dx = dxo_ref[...] + _norm_bwd(dhv, xh, r, ng_ref[...])
        dx_ref[...] = dx
        yv = y_ref[...]
        ry = _rstd(yv)
        n = yv * ry
        _acc_rows(dpg_ref, i, jnp.sum(dx * n, axis=0, keepdims=True))
        dy_ref[...] = _norm_bwd(dx, n, ry, pg_ref[...]).astype(BF16)

    return pl.pallas_call(
        body, name=name, grid=(t // tr,),
        in_specs=[_row_spec(tr, d), _row_spec(tr, d), _row_spec(tr, d), _vec_spec(d), _row_spec(tr, d), _vec_spec(d)],
        out_specs=[_row_spec(tr, d), _row_spec(tr, d), _vec_spec(d), _vec_spec(d)],
        out_shape=[jax.ShapeDtypeStruct((t, d), F32), jax.ShapeDtypeStruct((t, d), BF16),
                   jax.ShapeDtypeStruct((1, d), F32), jax.ShapeDtypeStruct((1, d), F32)],
        compiler_params=_cparams(("arbitrary",)),
    )(dx_out, dh, x, pre_gain, y_prev, post_gain_prev)


def _first_bwd(dx_out, dh, x, pre_gain, *, name, tr=256):
    t, d = x.shape
    tr = _pick(t, tr, 8)

    def body(dxo_ref, dh_ref, x_ref, ng_ref, dx_ref, dng_ref):
        i = pl.program_id(0)
        xv = x_ref[...]
        r = _rstd(xv)
        xh = xv * r
        dhv = dh_ref[...]
        _acc_rows(dng_ref, i, jnp.sum(dhv * xh, axis=0, keepdims=True))
        dx_ref[...] = dxo_ref[...] + _norm_bwd(dhv, xh, r, ng_ref[...])

    return pl.pallas_call(
        body, name=name, grid=(t // tr,),
        in_specs=[_row_spec(tr, d), _row_spec(tr, d), _row_spec(tr, d), _vec_spec(d)],
        out_specs=[_row_spec(tr, d), _vec_spec(d)],
        out_shape=[jax.ShapeDtypeStruct((t, d), F32), jax.ShapeDtypeStruct((1, d), F32)],
        compiler_params=_cparams(("arbitrary",)),
    )(dx_out, dh, x, pre_gain)


def _sigmoid(x):
    return 1.0 / (1.0 + jnp.exp(-x))


def _log_sigmoid(x):
    return jnp.minimum(x, 0.0) - jnp.log(1.0 + jnp.exp(-jnp.abs(x)))


_GELU_C = math.sqrt(2.0 / math.pi)


_GELU_A = 0.044715


def _gelu_parts(x, with_grad=True):
    x2 = x * x
    h = 0.5 * jnp.tanh(x * (_GELU_C + (_GELU_C * _GELU_A) * x2)) + 0.5
    val = x * h
    if not with_grad:
        return val, None
    return val, h * (1.0 + (1.0 - h) * (x * (2.0 * _GELU_C + (6.0 * _GELU_C * _GELU_A) * x2)))


def _split3(x):
    hi = x.astype(BF16)
    r1 = x - hi.astype(F32)
    mid = r1.astype(BF16)
    lo = (r1 - mid.astype(F32)).astype(BF16)
    return hi, mid, lo


def _tri_matmul(tri_bf16, x):
    hi, mid, lo = _split3(x)
    return _dot_nn(tri_bf16, hi) + _dot_nn(tri_bf16, mid) + _dot_nn(tri_bf16, lo)


def _gla_dims(d):
    dk, dv = d // 2, d
    return dk, dv, dk // GLA_HEADS, dv // GLA_HEADS


def _col_pieces(a, b, lay):
    ws, wp = lay
    out = []
    while a < b:
        j = a // ws
        end = min(b, (j + 1) * ws)
        out.append((j * wp + a - j * ws, end - a))
        a = end
    return out


def _load_cols(ref, a, b, lay):
    parts = [ref[:, s:s + n] for s, n in _col_pieces(a, b, lay)]
    return parts[0] if len(parts) == 1 else jnp.concatenate(parts, axis=1)


def _store_cols(ref, a, val, lay):
    off = 0
    for s, n in _col_pieces(a, a + val.shape[1], lay):
        ref[:, s:s + n] = val[:, off:off + n]
        off += n


def _gate_window(c_r, lay):
    (start, _), = _col_pieces(c_r, c_r + GLA_GATE_RANK, lay)
    assert (start % lay[1]) + LANES <= lay[1]
    return slice(start, start + LANES)


def _gla_gates(glr, k, w2_ref, b_ref):
    z = _dot_nn(glr.astype(BF16), w2_ref[...].astype(BF16)) + b_ref[...]
    la = _log_sigmoid(z) * (1.0 / GLA_TAU)
    row = lax.broadcasted_iota(jnp.int32, (CHUNK, CHUNK), 0)
    col = lax.broadcasted_iota(jnp.int32, (CHUNK, CHUNK), 1)
    incl = (row >= col).astype(BF16)
    bcum = _tri_matmul(incl, la)
    b_end = bcum[CHUNK - 1:CHUNK, :]
    e_rest = jnp.exp(b_end - bcum)
    return z, e_rest, k * e_rest, jnp.exp(b_end)


def _gla_fwd(proj, w2p, b_gate, o_gain, lay, *, name):
    t, wcols = proj.shape
    d = o_gain.shape[1]
    dk, dv, dkh, dvh = _gla_dims(d)
    nc = t // CHUNK
    c_k, c_v, c_g, c_r = dk, 2 * dk, 2 * dk + dv, 2 * dk + 2 * dv
    scale = dkh ** -0.5

    def body(p_ref, w2_ref, b_ref, og_ref, o_ref, a_ref, sb_ref, sfin_ref, s_ref):
        i = pl.program_id(0)

        @pl.when(i == 0)
        def _():
            s_ref[...] = jnp.zeros_like(s_ref)

        q = _load_cols(p_ref, 0, dk, lay) * scale
        k = _load_cols(p_ref, c_k, c_k + dk, lay)
        glr = p_ref[:, _gate_window(c_r, lay)]
        _, _, kdec, decay = _gla_gates(glr, k, w2_ref, b_ref)
        for h in range(GLA_HEADS):
            ks = slice(h * dkh, (h + 1) * dkh)
            vs = slice(h * dvh, (h + 1) * dvh)
            v_h = _load_cols(p_ref, c_v + h * dvh, c_v + (h + 1) * dvh, lay)
            g_h = _load_cols(p_ref, c_g + h * dvh, c_g + (h + 1) * dvh, lay)
            s_old = s_ref[h]
            sb_ref[0, h] = s_old
            s_new = s_old * decay[:, ks] + _dot_tn(v_h.astype(BF16), kdec[:, ks].astype(BF16))
            s_ref[h] = s_new
            o_h = _dot_nt(q[:, ks].astype(BF16), s_new.astype(BF16))
            o_ref[:, vs] = o_h
            on = o_h * _rstd(o_h)
            a_ref[:, vs] = (on * og_ref[:, vs] * (g_h * _sigmoid(g_h))).astype(BF16)

        @pl.when(i == nc - 1)
        def _():
            sfin_ref[...] = s_ref[...]

    full = lambda *shape: pl.BlockSpec(shape, lambda i: (0,) * len(shape))
    return pl.pallas_call(
        body, name=name, grid=(nc,),
        in_specs=[pl.BlockSpec((CHUNK, wcols), lambda i: (i, 0)), full(LANES, dk), full(1, dk), full(1, dv)],
        out_specs=[pl.BlockSpec((CHUNK, dv), lambda i: (i, 0)), pl.BlockSpec((CHUNK, dv), lambda i: (i, 0)),
                   pl.BlockSpec((1, GLA_HEADS, dvh, dkh), lambda i: (i, 0, 0, 0)), full(GLA_HEADS, dvh, dkh)],
        out_shape=[jax.ShapeDtypeStruct((t, dv), F32), jax.ShapeDtypeStruct((t, dv), BF16),
                   jax.ShapeDtypeStruct((nc, GLA_HEADS, dvh, dkh), F32),
                   jax.ShapeDtypeStruct((GLA_HEADS, dvh, dkh), F32)],
        scratch_shapes=[pltpu.VMEM((GLA_HEADS, dvh, dkh), F32)],
        compiler_params=_cparams(("arbitrary",)),
    )(proj, w2p, b_gate, o_gain)


def _gla_bwd(da, o, proj, w2p, b_gate, o_gain, s_before, s_final, lay, *, name):
    t, wcols = proj.shape
    d = o_gain.shape[1]
    dk, dv, dkh, dvh = _gla_dims(d)
    nc = t // CHUNK
    c_k, c_v, c_g, c_r = dk, 2 * dk, 2 * dk + dv, 2 * dk + 2 * dv
    scale = dkh ** -0.5

    def body(da_ref, o_ref, p_ref, w2_ref, b_ref, og_ref, sb_ref, sfin_ref,
             dp_ref, dog_ref, db_ref, dw2_ref, s_ref, gc_ref, dkd_ref):
        i = pl.program_id(0)

        @pl.when(i == 0)
        def _():
            s_ref[...] = sfin_ref[...]
            gc_ref[...] = jnp.zeros_like(gc_ref)

        ws, wp = lay
        for j in range(N_CHIPS):
            dp_ref[:, j * wp + ws:(j + 1) * wp] = jnp.zeros((CHUNK, wp - ws), BF16)
        q = _load_cols(p_ref, 0, dk, lay) * scale
        k = _load_cols(p_ref, c_k, c_k + dk, lay)
        glr = p_ref[:, _gate_window(c_r, lay)]
        z, e_rest, kdec, decay = _gla_gates(glr, k, w2_ref, b_ref)
        ddecay = []
        for h in range(GLA_HEADS):
            ks = slice(h * dkh, (h + 1) * dkh)
            vs = slice(h * dvh, (h + 1) * dvh)
            v_h = _load_cols(p_ref, c_v + h * dvh, c_v + (h + 1) * dvh, lay)
            g_h = _load_cols(p_ref, c_g + h * dvh, c_g + (h + 1) * dvh, lay)
            da_h = da_ref[:, vs]
            o_h = o_ref[:, vs]
            og_h = og_ref[:, vs]
            r = _rstd(o_h)
            on = o_h * r
            sg = _sigmoid(g_h)
            silu = g_h * sg
            _acc_rows(dog_ref, i, jnp.sum(da_h * silu * on, axis=0, keepdims=True), vs)
            _store_cols(dp_ref, c_g + h * dvh, (da_h * (on * og_h) * (sg * (1.0 + g_h * (1.0 - sg)))).astype(BF16),
                        lay)
            don = da_h * silu * og_h
            do_h = (r * (don - on * jnp.mean(don * on, axis=-1, keepdims=True))).astype(BF16)
            s_cur = s_ref[h]
            _store_cols(dp_ref, h * dkh, (_dot_nn(do_h, s_cur.astype(BF16)) * scale).astype(BF16), lay)
            g_tot = gc_ref[h] + _dot_tn(do_h, q[:, ks].astype(BF16))
            g_bf = g_tot.astype(BF16)
            dkd_ref[:, ks] = _dot_nn(v_h.astype(BF16), g_bf)
            _store_cols(dp_ref, c_v + h * dvh, _dot_nt(kdec[:, ks].astype(BF16), g_bf).astype(BF16), lay)
            s_prev = sb_ref[0, h]
            ddecay.append(jnp.sum(g_tot * s_prev, axis=0, keepdims=True))
            gc_ref[h] = g_tot * decay[:, ks]
            s_ref[h] = s_prev
        dkdec = dkd_ref[...]
        _store_cols(dp_ref, c_k, (dkdec * e_rest).astype(BF16), lay)
        d_e = dkdec * kdec
        row = lax.broadcasted_iota(jnp.int32, (CHUNK, CHUNK), 0)
        col = lax.broadcasted_iota(jnp.int32, (CHUNK, CHUNK), 1)
        excl = (row > col).astype(BF16)
        dla = jnp.concatenate(ddecay, axis=1) * decay + _tri_matmul(excl, d_e)
        dz = dla * (1.0 / GLA_TAU) * (1.0 - _sigmoid(z))
        _acc_rows(db_ref, i, jnp.sum(dz, axis=0, keepdims=True))
        dz_bf = dz.astype(BF16)
        dw2 = _dot_tn(glr.astype(BF16), dz_bf)

        @pl.when(i == 0)
        def _():
            dw2_ref[...] = dw2

        @pl.when(i > 0)
        def _():
            dw2_ref[...] += dw2

        dp_ref[:, _gate_window(c_r, lay)] = _dot_nt(dz_bf, w2_ref[...].astype(BF16)).astype(BF16)

    rev = lambda i: (nc - 1 - i, 0)
    full = lambda *shape: pl.BlockSpec(shape, lambda i: (0,) * len(shape))
    return pl.pallas_call(
        body, name=name, grid=(nc,),
        in_specs=[pl.BlockSpec((CHUNK, dv), rev), pl.BlockSpec((CHUNK, dv), rev), pl.BlockSpec((CHUNK, wcols), rev),
                  full(LANES, dk), full(1, dk), full(1, dv),
                  pl.BlockSpec((1, GLA_HEADS, dvh, dkh), lambda i: (nc - 1 - i, 0, 0, 0)), full(GLA_HEADS, dvh, dkh)],
        out_specs=[pl.BlockSpec((CHUNK, wcols), rev), full(1, dv), full(1, dk), full(LANES, dk)],
        out_shape=[jax.ShapeDtypeStruct((t, wcols), BF16), jax.ShapeDtypeStruct((1, dv), F32),
                   jax.ShapeDtypeStruct((1, dk), F32), jax.ShapeDtypeStruct((LANES, dk), F32)],
        scratch_shapes=[pltpu.VMEM((GLA_HEADS, dvh, dkh), F32), pltpu.VMEM((GLA_HEADS, dvh, dkh), F32),
                        pltpu.VMEM((CHUNK, dk), F32)],
        compiler_params=_cparams(("arbitrary",)),
    )(da, o, proj, w2p, b_gate, o_gain, s_before, s_final)


def _sgu_mid(p_ref, lg_ref, lb_ref, ws_ref, bst_ref, w, with_grad=True):
    gd = w // SGU_GROUPS
    u_act, du_fac = _gelu_parts(p_ref[:, 0:w], with_grad)
    vf, dv_fac = _gelu_parts(p_ref[:, w:2 * w], with_grad)
    mu = jnp.mean(vf, axis=-1, keepdims=True)
    cen = vf - mu
    rstd = lax.rsqrt(jnp.mean(cen * cen, axis=-1, keepdims=True) + EPS)
    xh = cen * rstd
    vn = (xh * lg_ref[...] + lb_ref[...]).astype(BF16)
    vs = [_dot_nn(ws_ref[g].astype(BF16), vn[:, g * gd:(g + 1) * gd]) + bst_ref[:, g:g + 1]
          for g in range(SGU_GROUPS)]
    return u_act, du_fac, dv_fac, rstd, xh, vn, vs


def _sgu_fwd(proj, ln_gain, ln_bias, ws_masked, bs_t, *, name):
    t, w3 = proj.shape
    w = w3 // 3
    gd = w // SGU_GROUPS
    nb = t // SGU_BLOCK

    def body(p_ref, lg_ref, lb_ref, ws_ref, bst_ref, a_ref):
        u_act, _, _, _, _, _, vs = _sgu_mid(p_ref, lg_ref, lb_ref, ws_ref, bst_ref, w, with_grad=False)
        for g in range(SGU_GROUPS):
            cs = slice(g * gd, (g + 1) * gd)
            gate = p_ref[:, 2 * w + g * gd:2 * w + (g + 1) * gd]
            a_ref[:, cs] = (u_act[:, cs] * vs[g] * (gate * _sigmoid(gate))).astype(BF16)

    full = lambda *shape: pl.BlockSpec(shape, lambda i: (0,) * len(shape))
    return pl.pallas_call(
        body, name=name, grid=(nb,),
        in_specs=[pl.BlockSpec((SGU_BLOCK, w3), lambda i: (i, 0)), full(1, w), full(1, w),
                  full(SGU_GROUPS, SGU_BLOCK, SGU_BLOCK), full(SGU_BLOCK, SGU_GROUPS)],
        out_specs=pl.BlockSpec((SGU_BLOCK, w), lambda i: (i, 0)),
        out_shape=jax.ShapeDtypeStruct((t, w), BF16),
        compiler_params=_cparams(("parallel",)),
    )(proj, ln_gain, ln_bias, ws_masked, bs_t)


def _sgu_bwd(da, proj, ln_gain, ln_bias, ws_masked, ws_masked_t, bs_t, *, name):
    t, w3 = proj.shape
    w = w3 // 3
    gd = w // SGU_GROUPS
    nb = t // SGU_BLOCK

    def body(da_ref, p_ref, lg_ref, lb_ref, ws_ref, wst_ref, bst_ref, dp_ref, dws_ref, dbst_ref, dlg_ref, dlb_ref,
             dvn_ref):
        i = pl.program_id(0)
        u_act, du_fac, dv_fac, rstd, xh, vn, vs = _sgu_mid(p_ref, lg_ref, lb_ref, ws_ref, bst_ref, w)
        for g in range(SGU_GROUPS):
            cs = slice(g * gd, (g + 1) * gd)
            gate = p_ref[:, 2 * w + g * gd:2 * w + (g + 1) * gd]
            sg = _sigmoid(gate)
            silu = gate * sg
            da_g = da_ref[:, cs]
            ua_g = u_act[:, cs]
            dp_ref[:, cs] = (da_g * vs[g] * silu * du_fac[:, cs]).astype(BF16)
            dp_ref[:, 2 * w + g * gd:2 * w + (g + 1) * gd] = (
                da_g * ua_g * vs[g] * (sg * (1.0 + gate * (1.0 - sg)))).astype(BF16)
            dvs = da_g * ua_g * silu
            dvs_bf = dvs.astype(BF16)
            dvn_ref[:, cs] = _dot_nn(wst_ref[g].astype(BF16), dvs_bf)
            dws = _dot_nt(dvs_bf, vn[:, cs])
            dbs = jnp.sum(dvs, axis=1, keepdims=True)

            @pl.when(i == 0)
            def _():
                dws_ref[g] = dws
                dbst_ref[:, g:g + 1] = dbs

            @pl.when(i > 0)
            def _():
                dws_ref[g] += dws
                dbst_ref[:, g:g + 1] += dbs

        dvn = dvn_ref[...]
        _acc_rows(dlg_ref, i, jnp.sum(dvn * xh, axis=0, keepdims=True))
        _acc_rows(dlb_ref, i, jnp.sum(dvn, axis=0, keepdims=True))
        dxh = dvn * lg_ref[...]
        dvf = rstd * (dxh - jnp.mean(dxh, axis=-1, keepdims=True)
                      - xh * jnp.mean(dxh * xh, axis=-1, keepdims=True))
        dp_ref[:, w:2 * w] = (dvf * dv_fac).astype(BF16)

    full = lambda *shape: pl.BlockSpec(shape, lambda i: (0,) * len(shape))
    return pl.pallas_call(
        body, name=name, grid=(nb,),
        in_specs=[pl.BlockSpec((SGU_BLOCK, w), lambda i: (i, 0)), pl.BlockSpec((SGU_BLOCK, w3), lambda i: (i, 0)),
                  full(1, w), full(1, w), full(SGU_GROUPS, SGU_BLOCK, SGU_BLOCK),
                  full(SGU_GROUPS, SGU_BLOCK, SGU_BLOCK), full(SGU_BLOCK, SGU_GROUPS)],
        out_specs=[pl.BlockSpec((SGU_BLOCK, w3), lambda i: (i, 0)), full(SGU_GROUPS, SGU_BLOCK, SGU_BLOCK),
                   full(SGU_BLOCK, SGU_GROUPS), full(1, w), full(1, w)],
        out_shape=[jax.ShapeDtypeStruct((t, w3), BF16), jax.ShapeDtypeStruct((SGU_GROUPS, SGU_BLOCK, SGU_BLOCK), F32),
                   jax.ShapeDtypeStruct((SGU_BLOCK, SGU_GROUPS), F32), jax.ShapeDtypeStruct((1, w), F32),
                   jax.ShapeDtypeStruct((1, w), F32)],
        scratch_shapes=[pltpu.VMEM((SGU_BLOCK, w), F32)],
        compiler_params=_cparams(("arbitrary",)),
    )(da, proj, ln_gain, ln_bias, ws_masked, ws_masked_t, bs_t)


def _tile2d(rows, cols, block_bytes, row_unit):
    if rows % row_unit == 0:
        return _pick(rows, max(row_unit, block_bytes // (4 * cols)), row_unit), cols
    return rows, _pick(cols, max(LANES, block_bytes // (4 * rows)))


def _adamw(w, g, m, v, *, name, block_bytes=1 << 20, after=None):
    rows, cols = w.shape
    tr, tc = _tile2d(rows, cols, block_bytes, 8)
    g_rows = g.shape[0]
    assert g_rows == rows or tr == rows
    extra_specs, extra_args = ([], []) if after is None else ([pl.BlockSpec(memory_space=pl.ANY)], [after])

    def body(w_ref, g_ref, m_ref, v_ref, *rest):
        go_ref, d_ref, mo_ref, vo_ref = rest[len(extra_args):]
        gv = g_ref[0:tr, :]
        go_ref[...] = gv
        mn = ADAM_B1 * m_ref[...] + (1.0 - ADAM_B1) * gv
        vn = ADAM_B2 * v_ref[...] + (1.0 - ADAM_B2) * (gv * gv)
        m_hat = mn / (1.0 - ADAM_B1 ** ADAM_STEP)
        v_hat = vn / (1.0 - ADAM_B2 ** ADAM_STEP)
        d_ref[...] = -ADAM_LR * (m_hat / (jnp.sqrt(v_hat) + ADAM_EPS) + ADAM_WD * w_ref[...])
        mo_ref[...] = mn
        vo_ref[...] = vn

    spec = pl.BlockSpec((tr, tc), lambda i, j: (i, j))
    g_spec = spec if g_rows == rows else pl.BlockSpec((g_rows, tc), lambda i, j: (0, j))
    return pl.pallas_call(
        body, name=name, grid=(rows // tr, cols // tc), in_specs=[spec, g_spec, spec, spec] + extra_specs,
        out_specs=[spec] * 4, out_shape=[jax.ShapeDtypeStruct((rows, cols), F32)] * 4,
        compiler_params=_cparams(("parallel", "parallel")),
    )(w, g, m, v, *extra_args)


def _matmul_dw_pair(a_me, a_sib, b_me, b_sib, core_idx, *, shards_on, name, after=None, part=(0, 1)):
    T, M = a_me.shape
    N = b_me.shape[1]
    if shards_on == "rows":
        p, count = part
        tm, hc = M // N_CHIPS, N // 2
        hp = hc // count
        tn = _pick(hp, 512)
        per = hp // tn
        grid = (N_CHIPS, per)
        a_spec = pl.BlockSpec((T, tm), lambda i, n, h: (0, i))
        b_me_spec = pl.BlockSpec((T, tn), lambda i, n, h: (0, (h[0] * count + p) * per + n))
        b_sib_spec = pl.BlockSpec((T, tn), lambda i, n, h: (0, p * per + n))
        out_spec = pl.BlockSpec((None, tm, tn), lambda i, n, h: (i, 0, n))
        out_shape = jax.ShapeDtypeStruct((N_CHIPS, tm, hp), BF16)
    else:
        tm, hc = _pick(M, 1024), N // N_CHIPS // 2
        grid = (M // tm, N_CHIPS)
        a_spec = pl.BlockSpec((T, tm), lambda i, j, h: (0, i))
        b_me_spec = pl.BlockSpec((T, hc), lambda i, j, h: (0, 2 * j + h[0]))
        b_sib_spec = pl.BlockSpec((T, hc), lambda i, j, h: (0, j))
        out_spec = pl.BlockSpec((None, tm, hc), lambda i, j, h: (j, i, 0))
        out_shape = jax.ShapeDtypeStruct((N_CHIPS, M, hc), BF16)
    extra_specs, extra_args = ([], []) if after is None else ([pl.BlockSpec(memory_space=pl.ANY)], [after])

    def body(h_ref, am_ref, as_ref, bm_ref, bs_ref, *rest):
        o_ref = rest[len(extra_args)]
        o_ref[...] = (_dot_tn(am_ref[...], bm_ref[...]) + _dot_tn(as_ref[...], bs_ref[...])).astype(BF16)

    grid_spec = pltpu.PrefetchScalarGridSpec(
        num_scalar_prefetch=1, grid=grid, in_specs=[a_spec, a_spec, b_me_spec, b_sib_spec] + extra_specs,
        out_specs=out_spec)
    return pl.pallas_call(
        body, name=name, grid_spec=grid_spec, out_shape=out_shape, compiler_params=_cparams(("parallel", "parallel")),
    )(core_idx, a_me, a_sib, b_me, b_sib, *extra_args)


def _chip_sum(pair, landed, slots, *, name, block_bytes=1 << 20, part=(0, 1), into=None):
    p, count = part
    _, r, hp = pair.shape
    tr, tc = _tile2d(r, hp, block_bytes, 16)
    ncb = hp // tc
    extra_specs, extra_args = ([], []) if into is None else ([pl.BlockSpec(memory_space=pl.ANY)], [into])

    def body(s_ref, own_ref, l0_ref, l1_ref, l2_ref, *rest):
        rest[-1][...] = ((own_ref[...].astype(F32) + l0_ref[...].astype(F32)) + l1_ref[...].astype(F32)
                         ) + l2_ref[...].astype(F32)

    def slab(which):
        return pl.BlockSpec((None, tr, tc), lambda i, k, s: (s[which], i, k))

    grid_spec = pltpu.PrefetchScalarGridSpec(
        num_scalar_prefetch=1, grid=(r // tr, ncb),
        in_specs=[slab(0), slab(1), slab(2), slab(3)] + extra_specs,
        out_specs=pl.BlockSpec((tr, tc), lambda i, k, s: (i, (s[4] * count + p) * ncb + k)))
    return pl.pallas_call(
        body, name=name, grid_spec=grid_spec, out_shape=jax.ShapeDtypeStruct((r, 2 * hp * count), F32),
        input_output_aliases={} if into is None else {5: 0},
        compiler_params=_cparams(("parallel", "parallel")),
    )(slots, pair, landed, landed, landed, *extra_args)


def _stack_sum(x, *, name, out_dtype=F32, block_bytes=1 << 20):
    s, r, c = x.shape
    tr = _pick(r, max(8, block_bytes // (4 * c)), 16) if r % 16 == 0 else r

    def body(x_ref, o_ref):
        acc = x_ref[0].astype(F32)
        for j in range(1, s):
            acc = acc + x_ref[j].astype(F32)
        o_ref[...] = acc.astype(out_dtype)

    return pl.pallas_call(
        body, name=name, grid=(r // tr,),
        in_specs=[pl.BlockSpec((s, tr, c), lambda i: (0, i, 0))], out_specs=pl.BlockSpec((tr, c), lambda i: (i, 0)),
        out_shape=jax.ShapeDtypeStruct((r, c), out_dtype), compiler_params=_cparams(("parallel",)),
    )(x)


HBM = pl.BlockSpec(memory_space=pltpu.HBM)


def _place():
    x, y, c = lax.axis_index("x"), lax.axis_index("y"), lax.axis_index("c")
    other_chips = [(1 - x, y), (x, 1 - y), (1 - x, 1 - y)]
    return x, y, c, other_chips


def _half_cols(cols, which):
    hc = cols // 2
    return pl.ds(pl.multiple_of(which * hc, LANES), hc)


SEM = pl.BlockSpec(memory_space=pltpu.SEMAPHORE)
ANY = pl.BlockSpec(memory_space=pl.ANY)
SIDE_EFFECT = pltpu.SideEffectType.DATAFLOW_SIDE_EFFECTING
TOKEN_SHAPE = (8, LANES)


def _hbm(shape, dtype):
    return pltpu.HBM(shape, dtype)


def _in_hbm(a):
    return pltpu.with_memory_space_constraint(a, pltpu.HBM)


def _gather_copy(src_ref, land_ref, ssem, rsem, k, chip_of_block, to, c):
    cols = src_ref.shape[1]
    return pltpu.make_async_remote_copy(
        src_ref=src_ref.at[:, _half_cols(cols, c)], dst_ref=land_ref.at[chip_of_block, :, _half_cols(cols, c)],
        send_sem=ssem.at[k], recv_sem=rsem.at[k], device_id=to, device_id_type=MESH)


def _gather_start(shards, *, name, after=()):
    n = len(shards)
    after = list(after)

    def body(*refs):
        srcs, lands = refs[:n], refs[n:2 * n]
        outs = refs[2 * n + len(after):]
        token = outs[-1]
        x, y, c, chips = _place()
        me = 2 * x + y
        for a in range(n):
            ssem, rsem = outs[4 * a], outs[4 * a + 1]
            for k, (cx, cy) in enumerate(chips):
                _gather_copy(srcs[a], lands[a], ssem, rsem, k, me, (cx, cy, c), c).start()
        token[...] = jnp.zeros_like(token)

    out_shape, out_specs, aliases = [], [], {}
    for a, s in enumerate(shards):
        out_shape += [pltpu.SemaphoreType.DMA((3,)), pltpu.SemaphoreType.DMA((3,)), _hbm(s.shape, s.dtype),
                      _hbm((N_CHIPS,) + s.shape, s.dtype)]
        out_specs += [SEM, SEM, HBM, HBM]
        aliases[a] = 4 * a + 2
        aliases[n + a] = 4 * a + 3
    out_shape.append(jax.ShapeDtypeStruct(TOKEN_SHAPE, F32))
    out_specs.append(pl.BlockSpec(memory_space=pltpu.VMEM))
    lands = [_in_hbm(lax.empty((N_CHIPS,) + s.shape, s.dtype)) for s in shards]
    res = pl.pallas_call(
        body, name=name, in_specs=[HBM] * (2 * n) + [ANY] * len(after), out_specs=out_specs, out_shape=out_shape,
        input_output_aliases=aliases, compiler_params=pltpu.CompilerParams(has_side_effects=SIDE_EFFECT),
    )(*[_in_hbm(s) for s in shards], *lands, *after)
    return [tuple(res[4 * a:4 * a + 4]) for a in range(n)], res[-1]


def _wait_call(wait_fn, parts, after, *, name):
    ssem, rsem, src, land = parts
    after = list(after) if isinstance(after, (list, tuple)) else [after]

    def body(src_ref, land_ref, ssem_ref, rsem_ref, *rest):
        wait_fn(src_ref, land_ref, ssem_ref, rsem_ref)

    return pl.pallas_call(
        body, name=name, in_specs=[HBM, HBM, SEM, SEM] + [ANY] * len(after), out_specs=[HBM, HBM],
        out_shape=[_hbm(src.shape, src.dtype), _hbm(land.shape, land.dtype)], input_output_aliases={0: 0, 1: 1},
        compiler_params=pltpu.CompilerParams(has_side_effects=SIDE_EFFECT),
    )(src, land, ssem, rsem, *after)


ALL_CHIPS = (0, 1, 2)


def _gather_wait(parts, after, *, name, ks=ALL_CHIPS):
    def wait(src_ref, land_ref, ssem_ref, rsem_ref):
        x, y, c, chips = _place()
        for k in ks:
            cx, cy = chips[k]
            cp = _gather_copy(src_ref, land_ref, ssem_ref, rsem_ref, k, 2 * cx + cy, (x, y, c), c)
            cp.wait_send()
            cp.wait_recv()

    src, land = _wait_call(wait, parts, after, name=name)
    return (parts[0], parts[1], src, land)


def _forward_copy(buf_ref, ssem, rsem, k, slab, which, to):
    part = buf_ref.at[slab, :, _half_cols(buf_ref.shape[2], which)]
    return pltpu.make_async_remote_copy(
        src_ref=part, dst_ref=part, send_sem=ssem.at[k], recv_sem=rsem.at[k], device_id=to, device_id_type=MESH)


def _sibling_forward(land, *, name, ks=ALL_CHIPS):
    def body(_, buf, send_sems, recv_sems):
        x, y, c, chips = _place()
        copies = []
        for k in ks:
            cx, cy = chips[k]
            cp = _forward_copy(buf, send_sems, recv_sems, k, 2 * cx + cy, c, (x, y, 1 - c))
            cp.start()
            copies.append(cp)
        for k in ks:
            cx, cy = chips[k]
            _forward_copy(buf, send_sems, recv_sems, k, 2 * cx + cy, 1 - c, (x, y, c)).wait_recv()
        for cp in copies:
            cp.wait_send()

    return pl.pallas_call(
        body, name=name, in_specs=[HBM], out_specs=HBM, out_shape=jax.ShapeDtypeStruct(land.shape, land.dtype),
        input_output_aliases={0: 0},
        scratch_shapes=[pltpu.SemaphoreType.DMA((3,)), pltpu.SemaphoreType.DMA((3,))],
    )(land)


def _share_copy(buf_ref, ssem, rsem, a, which, to):
    part = buf_ref.at[:, _half_cols(buf_ref.shape[1], which)]
    return pltpu.make_async_remote_copy(
        src_ref=part, dst_ref=part, send_sem=ssem.at[a], recv_sem=rsem.at[a], device_id=to, device_id_type=MESH)


def _share_start(arrays, *, name):
    n = len(arrays)

    def body(*refs):
        bufs, ssem, rsem, token = refs[:n], refs[n], refs[n + 1], refs[-1]
        x, y, c, _ = _place()
        for a in range(n):
            _share_copy(bufs[a], ssem, rsem, a, c, (x, y, 1 - c)).start()
        token[...] = jnp.zeros_like(token)

    res = pl.pallas_call(
        body, name=name, in_specs=[HBM] * n,
        out_specs=[SEM, SEM] + [HBM] * n + [pl.BlockSpec(memory_space=pltpu.VMEM)],
        out_shape=[pltpu.SemaphoreType.DMA((n,)), pltpu.SemaphoreType.DMA((n,))]
        + [_hbm(b.shape, b.dtype) for b in arrays] + [jax.ShapeDtypeStruct(TOKEN_SHAPE, F32)],
        input_output_aliases={a: 2 + a for a in range(n)},
        compiler_params=pltpu.CompilerParams(has_side_effects=SIDE_EFFECT),
    )(*[_in_hbm(b) for b in arrays])
    return (res[0], res[1], list(res[2:2 + n])), res[-1]


def _share_wait(parts, after, *, name):
    ssem, rsem, bufs = parts
    n = len(bufs)
    after = list(after) if isinstance(after, (list, tuple)) else [after]

    def body(*refs):
        buf_refs, ssem_ref, rsem_ref = refs[:n], refs[n], refs[n + 1]
        x, y, c, _ = _place()
        for a in range(n):
            _share_copy(buf_refs[a], ssem_ref, rsem_ref, a, c, (x, y, c)).wait_send()
            _share_copy(buf_refs[a], ssem_ref, rsem_ref, a, 1 - c, (x, y, c)).wait_recv()

    return pl.pallas_call(
        body, name=name, in_specs=[HBM] * n + [SEM, SEM] + [ANY] * len(after), out_specs=[HBM] * n,
        out_shape=[_hbm(b.shape, b.dtype) for b in bufs], input_output_aliases={a: a for a in range(n)},
        compiler_params=pltpu.CompilerParams(has_side_effects=SIDE_EFFECT),
    )(*bufs, ssem, rsem, *after)


def _scatter_copy(src_ref, land_ref, ssem, rsem, k, src_slab, dst_slab, to):
    return pltpu.make_async_remote_copy(
        src_ref=src_ref.at[src_slab], dst_ref=land_ref.at[dst_slab], send_sem=ssem.at[k], recv_sem=rsem.at[k],
        device_id=to, device_id_type=MESH)


def _scatter_start(part, *, name):
    def start(src_ref, land_ref, ssem, rsem):
        x, y, c, chips = _place()
        me = 2 * x + y
        for k, (cx, cy) in enumerate(chips):
            _scatter_copy(src_ref, land_ref, ssem, rsem, k, 2 * cx + cy, me, (cx, cy, c)).start()

    return _split_start(start, part, part.shape, N_CHIPS - 1, name=name)


def _scatter_wait(parts, after, *, name):
    def wait(src_ref, land_ref, ssem_ref, rsem_ref):
        x, y, c, chips = _place()
        for k, (cx, cy) in enumerate(chips):
            idx = 2 * cx + cy
            cp = _scatter_copy(src_ref, land_ref, ssem_ref, rsem_ref, k, idx, idx, (x, y, c))
            cp.wait_send()
            cp.wait_recv()

    return _wait_call(wait, parts, after, name=name)


def _split_start(start_fn, src, land_shape, n_sems, *, name):
    def body(src_ref, land_ref, ssem, rsem, src_out, land_out, token):
        start_fn(src_ref, land_ref, ssem, rsem)
        token[...] = jnp.zeros_like(token)

    res = pl.pallas_call(
        body, name=name, in_specs=[HBM, HBM], out_specs=[SEM, SEM, HBM, HBM, pl.BlockSpec(memory_space=pltpu.VMEM)],
        out_shape=[pltpu.SemaphoreType.DMA((n_sems,)), pltpu.SemaphoreType.DMA((n_sems,)), _hbm(src.shape, src.dtype),
                   _hbm(land_shape, src.dtype), jax.ShapeDtypeStruct(TOKEN_SHAPE, F32)],
        input_output_aliases={0: 2, 1: 3}, compiler_params=pltpu.CompilerParams(has_side_effects=SIDE_EFFECT),
    )(_in_hbm(src), _in_hbm(lax.empty(land_shape, src.dtype)))
    return tuple(res[:4]), res[4]


def _sibling_copies(src_ref, land_ref, ssem, rsem, k0, groups, which, to):
    def copy(k, src, dst):
        return pltpu.make_async_remote_copy(
            src_ref=src, dst_ref=dst, send_sem=ssem.at[k], recv_sem=rsem.at[k], device_id=to, device_id_type=MESH)

    if groups == 0:
        return [copy(k0, src_ref, land_ref)]
    hw = src_ref.shape[1] // groups // 2
    return [copy(k0 + j, src_ref.at[:, pl.ds(pl.multiple_of((2 * j + which) * hw, LANES), hw)],
                 land_ref.at[:, j * hw:(j + 1) * hw]) for j in range(groups)]


def _to_sibling_start(items, *, name):
    n = len(items)
    shapes = [a.shape if g == 0 else (a.shape[0], a.shape[1] // 2) for a, g in items]
    first = [sum(max(g, 1) for _, g in items[:k]) for k in range(n + 1)]

    def body(*refs):
        srcs, lands, ssem, rsem, token = refs[:n], refs[n:2 * n], refs[2 * n], refs[2 * n + 1], refs[-1]
        x, y, c, _ = _place()
        for k, (_, g) in enumerate(items):
            for cp in _sibling_copies(srcs[k], lands[k], ssem, rsem, first[k], g, 1 - c, (x, y, 1 - c)):
                cp.start()
        token[...] = jnp.zeros_like(token)

    res = pl.pallas_call(
        body, name=name, in_specs=[HBM] * (2 * n),
        out_specs=[SEM, SEM] + [HBM] * (2 * n) + [pl.BlockSpec(memory_space=pltpu.VMEM)],
        out_shape=[pltpu.SemaphoreType.DMA((first[n],)), pltpu.SemaphoreType.DMA((first[n],))]
        + [_hbm(a.shape, a.dtype) for a, _ in items] + [_hbm(s, a.dtype) for s, (a, _) in zip(shapes, items)]
        + [jax.ShapeDtypeStruct(TOKEN_SHAPE, F32)],
        input_output_aliases={k: 2 + k for k in range(2 * n)},
        compiler_params=pltpu.CompilerParams(has_side_effects=SIDE_EFFECT),
    )(*[_in_hbm(a) for a, _ in items], *[_in_hbm(lax.empty(s, a.dtype)) for s, (a, _) in zip(shapes, items)])
    return [(res[0], res[1], first[k], g, res[2 + k], res[2 + n + k]) for k, (_, g) in enumerate(items)], res[-1]


def _from_sibling(flight, after, *, name):
    ssem, rsem, k0, groups, src, land = flight

    def wait(src_ref, land_ref, ssem_ref, rsem_ref):
        x, y, c, _ = _place()
        for cp in _sibling_copies(src_ref, land_ref, ssem_ref, rsem_ref, k0, groups, 1 - c, (x, y, c)):
            cp.wait_send()
            cp.wait_recv()

    return _wait_call(wait, (ssem, rsem, src, land), after, name=name)


def _dev_peers(x, y, c, chips):
    return [(x, y, 1 - c)] + [(cx, cy, c) for cx, cy in chips] + [(cx, cy, 1 - c) for cx, cy in chips]


def _dev_gather_start(part, *, name):
    def start(src_ref, land_ref, ssem, rsem):
        x, y, c, chips = _place()
        for k, to in enumerate(_dev_peers(x, y, c, chips)):
            pltpu.make_async_remote_copy(
                src_ref=src_ref, dst_ref=land_ref.at[4 * x + 2 * y + c], send_sem=ssem.at[k], recv_sem=rsem.at[k],
                device_id=to, device_id_type=MESH).start()

    return _split_start(start, part, (N_DEV,) + part.shape, N_DEV - 1, name=name)


def _dev_gather_wait(parts, after, *, name):
    def wait(src_ref, land_ref, ssem_ref, rsem_ref):
        x, y, c, chips = _place()
        for k, (px, py, pc) in enumerate(_dev_peers(x, y, c, chips)):
            cp = pltpu.make_async_remote_copy(
                src_ref=src_ref, dst_ref=land_ref.at[4 * px + 2 * py + pc], send_sem=ssem_ref.at[k],
                recv_sem=rsem_ref.at[k], device_id=(x, y, c), device_id_type=MESH)
            cp.wait_send()
            cp.wait_recv()

    return _wait_call(wait, parts, after, name=name)[1]


def _sibling_share_halves(arrays, *, name):
    n = len(arrays)

    def body(*refs):
        bufs = refs[n:2 * n]
        send_sems, recv_sems = refs[2 * n:]
        x, y, c, _ = _place()
        copies = []
        for a in range(n):
            mine = bufs[a].at[:, _half_cols(bufs[a].shape[1], c)]
            cp = pltpu.make_async_remote_copy(
                src_ref=mine, dst_ref=mine, send_sem=send_sems.at[a], recv_sem=recv_sems.at[a],
                device_id=(x, y, 1 - c), device_id_type=MESH)
            cp.start()
            copies.append(cp)
        for a in range(n):
            theirs = bufs[a].at[:, _half_cols(bufs[a].shape[1], 1 - c)]
            pltpu.make_async_remote_copy(
                src_ref=theirs, dst_ref=theirs, send_sem=send_sems.at[a], recv_sem=recv_sems.at[a],
                device_id=(x, y, c), device_id_type=MESH).wait_recv()
        for cp in copies:
            cp.wait_send()

    return pl.pallas_call(
        body, name=name, in_specs=[HBM] * n, out_specs=[HBM] * n,
        out_shape=[jax.ShapeDtypeStruct(h.shape, h.dtype) for h in arrays],
        input_output_aliases={a: a for a in range(n)},
        scratch_shapes=[pltpu.SemaphoreType.DMA((n,)), pltpu.SemaphoreType.DMA((n,))],
    )(*arrays)


def _pack(arrays, rows_multiple=16, width=LANES):
    flat = jnp.concatenate([a.astype(F32).reshape(-1) for a in arrays])
    total = flat.shape[0]
    rows = -(-total // width)
    rows = -(-rows // rows_multiple) * rows_multiple
    return jnp.pad(flat, (0, rows * width - total)).reshape(rows, width)


def _unpack(buf, shapes):
    flat = buf.reshape(-1)
    out, off = [], 0
    for s in shapes:
        n = math.prod(s)
        out.append(flat[off:off + n].reshape(s))
        off += n
    return out


def kernel(x, norm_pre, norm_post, gla_w_in, gla_w_gate2, gla_b_gate, gla_o_gain, gla_w_out, sgu_w_in, sgu_ln_gain, sgu_ln_bias, sgu_w_spatial, sgu_b_spatial, sgu_w_out, loss_target, m_norm_pre, m_norm_post, m_gla_w_in, m_gla_w_gate2, m_gla_b_gate, m_gla_o_gain, m_gla_w_out, m_sgu_w_in, m_sgu_ln_gain, m_sgu_ln_bias, m_sgu_w_spatial, m_sgu_b_spatial, m_sgu_w_out, v_norm_pre, v_norm_post, v_gla_w_in, v_gla_w_gate2, v_gla_b_gate, v_gla_o_gain, v_gla_w_out, v_sgu_w_in, v_sgu_ln_gain, v_sgu_ln_bias, v_sgu_w_spatial, v_sgu_b_spatial, v_sgu_w_out):
    _, t, d = x.shape
    dk = d // 2
    ws = gla_w_in.shape[2]
    wp = -(-ws // LANES) * LANES
    lay = (ws, wp)
    chip =2 * lax.axis_index("x") + lax.axis_index("y")
    core = lax.axis_index("c")
    core_idx = core.astype(jnp.int32).reshape(1)
    others = jnp.arange(N_CHIPS - 1, dtype=jnp.int32)
    others = others + (others >= chip).astype(jnp.int32)
    slots = jnp.concatenate([chip.astype(jnp.int32).reshape(1), others, core_idx])

    x0 = x[0]
    target = loss_target[0]

    wt_in_g, mt_in_g, vt_in_g = gla_w_in[0].T, m_gla_w_in[0].T, v_gla_w_in[0].T

    small_shard = _pack([gla_w_gate2[0], sgu_ln_gain[0], sgu_ln_bias[0]], rows_multiple=8, width=2 * LANES)
    own = [small_shard, jnp.pad(wt_in_g.astype(BF16), ((0, wp - ws), (0, 0)))]
    in_flight, token = _gather_start(own, name="gather_start_a")
    own_later = [(p[0] + token[0, 0]).astype(BF16) for p in (gla_w_out, sgu_w_in, sgu_w_out)]
    in_flight_later, token_later = _gather_start(own_later, name="gather_start_b", after=[token])
    own, in_flight = own + own_later, in_flight + in_flight_later

    def with_own(i, land):
        return lax.dynamic_update_slice(land, own[i][None], (chip, 0, 0))

    def arrived(i, after, name):
        land = _gather_wait(in_flight[i], after, name=name + "_wait")[3]
        return with_own(i, _sibling_forward(land, name=name + "_share"))

    h0 = _norm_pre(x0, norm_pre[0:1] + token[0:1, 0:1] + token_later[0:1, 0:1], name="pre0")
    g_small = arrived(0, h0, "w_small")
    wt_g = arrived(1, [g_small, wt_in_g, mt_in_g, vt_in_g], "w_gla_in").reshape(N_CHIPS * wp, d)
    shard_shapes = [gla_w_gate2.shape[1:], sgu_ln_gain.shape[1:], sgu_ln_bias.shape[1:]]
    per_chip = [_unpack(g_small[j], shard_shapes) for j in range(N_CHIPS)]
    w2_full = jnp.concatenate([p[0] for p in per_chip], axis=1)
    ln_gain = jnp.concatenate([p[1] for p in per_chip], axis=0)[None, :]
    ln_bias = jnp.concatenate([p[2] for p in per_chip], axis=0)[None, :]
    w2p = jnp.pad(w2_full, ((0, LANES - GLA_GATE_RANK), (0, 0)))

    pos_chunk = jnp.arange(SGU_BLOCK) // CHUNK
    mask = pos_chunk[:, None] >= pos_chunk[None, :]
    ws_masked = jnp.where(mask[None], sgu_w_spatial[0], 0.0)
    ws_masked_t = ws_masked.transpose(0, 2, 1)
    bs_t = sgu_b_spatial[0].T

    proj0 = _matmul(h0, wt_g, mode="nt", out_dtype=F32, name="gla_in", tn=wp)
    o0, a0, s_before, s_final = _gla_fwd(proj0, w2p, gla_b_gate, gla_o_gain, lay, name="gla_scan")
    w_out_g = arrived(2, a0, "w_gla_out").reshape(d, d)
    y0 = _matmul(a0, w_out_g, mode="nn", out_dtype=F32, name="gla_out")
    x1, h1 = _post_then_pre(x0, y0, norm_post[0:1], norm_pre[1:2], name="post0_pre1")
    g_wi_s = arrived(3, h1, "w_sgu_in")
    proj1 = _matmul(h1, g_wi_s, mode="nn", out_dtype=F32, name="sgu_in", b_shards=True)
    a1 = _sgu_fwd(proj1, ln_gain, ln_bias, ws_masked, bs_t, name="sgu_gate")
    w_out_s = arrived(4, a1, "w_sgu_out").reshape(d, d)
    acts, tok = _to_sibling_start([(a1, 0), (a0, 0), (h1, 0), (h0, 1)], name="acts_to_sibling")
    a1, a0, h1, h0 = [f[4] for f in acts]
    y1 = _matmul(a1, w_out_s, mode="nn", out_dtype=F32, name="sgu_out", after=tok)
    loss_part, dx2, dy1, d_post1 = _loss_head(x1, y1, norm_post[1:2], target, name="loss_head")

    def behind(small, token):
        return small + token[0:1, 0:1]

    def pair_gradient(a_sent, b_sent, after, shards_on, name):
        a_me, a_sib = _from_sibling(a_sent, after, name=name + "_a_wait")
        b_me, b_sib = _from_sibling(b_sent, [a_sib] + list(after), name=name + "_b_wait")
        pair = _matmul_dw_pair(a_me, a_sib, b_me, b_sib, core_idx, shards_on=shards_on,
                               name=name + "_pair")
        return _scatter_start(pair, name=name + "_start")

    def reduced(flight, after, name):
        pair, landed = _scatter_wait(flight, after, name=name + "_wait")
        return _chip_sum(pair, landed, slots, name=name + "_sum")

    (dy1_sent,), tok = _to_sibling_start([(dy1, 1)], name="dy1_to_sibling")
    dy1 = dy1_sent[4]
    da1 = _matmul(dy1, w_out_s, mode="nt", out_dtype=F32, name="d_sgu_act", after=tok)
    fl_wo_s, tok = pair_gradient(acts[0], dy1_sent, [da1], "rows", "g_sgu_out")
    dproj1, d_ws, d_bs_t, d_lg, d_lb = _sgu_bwd(da1, proj1, ln_gain, behind(ln_bias, tok), ws_masked, ws_masked_t,
                                                bs_t, name="sgu_gate_bwd")
    (dp1_sent,), tok = _to_sibling_start([(dproj1, N_CHIPS)], name="dproj1_to_sibling")
    dproj1 = dp1_sent[4]
    dh1 = _matmul_nt_shards(dproj1, g_wi_s, out_dtype=F32, name="d_sgu_h", after=tok)
    fl_wi_s, tok = pair_gradient(acts[2], dp1_sent, [dh1], "cols", "g_sgu_in")
    dx1, dy0, d_pre1, d_post0 = _mid_bwd(dx2, dh1, x1, behind(norm_pre[1:2], tok), y0, norm_post[0:1],
                                         name="pre1_post0_bwd")
    (dy0_sent,), tok = _to_sibling_start([(dy0, 1)], name="dy0_to_sibling")
    dy0 = dy0_sent[4]
    da0 = _matmul(dy0, w_out_g, mode="nt", out_dtype=F32, name="d_gla_act", after=tok)
    fl_wo_g, tok = pair_gradient(acts[1], dy0_sent, [da0], "rows", "g_gla_out")
    dproj0, d_og, d_bg, d_w2p = _gla_bwd(da0, o0, proj0, w2p, behind(gla_b_gate, tok), gla_o_gain, s_before, s_final,
                                         lay, name="gla_scan_bwd")
    early_shapes = [norm_post.shape, gla_b_gate.shape, gla_o_gain.shape, sgu_w_spatial.shape, sgu_b_spatial.shape,
                    (1, GLA_GATE_RANK, dk), (1, d), (1, d), (1, LANES)]
    early_part = _pack([jnp.concatenate([d_post0, d_post1], axis=0), d_bg, d_og, jnp.where(mask[None], d_ws, 0.0)[None],
                        d_bs_t.T[None], d_w2p[:GLA_GATE_RANK][None], d_lg, d_lb, loss_part])
    early_flight, tok = _dev_gather_start(early_part, name="small_early_start")
    (dp0_sent,), tok_sent = _to_sibling_start([(dproj0, 0)], name="dproj0_to_sibling")
    dproj0 = dp0_sent[4]
    dh0 = _matmul(dproj0, wt_g, mode="nn", out_dtype=F32, name="d_gla_h", tk=N_CHIPS * wp, after=tok_sent)
    a_me, a_sib = _from_sibling(dp0_sent, [dh0, tok], name="g_gla_in_a_wait")
    b_me, b_sib = _from_sibling(acts[3], [a_sib, dh0], name="g_gla_in_b_wait")
    fl_wi_g, tok_scatter = [], None
    for p in range(2):
        pair = _matmul_dw_pair(a_me, a_sib, b_me, b_sib, core_idx, shards_on="rows", part=(p, 2),
                               name=f"g_gla_in_pair{p}", after=tok_scatter)
        flight, tok_scatter = _scatter_start(pair, name=f"g_gla_in_start{p}")
        fl_wi_g.append(flight)
    r_wo_s = reduced(fl_wo_s, tok_scatter, "g_sgu_out")
    r_wi_s = reduced(fl_wi_s, r_wo_s, "g_sgu_in")
    r_wo_g = reduced(fl_wo_g, r_wi_s, "g_gla_out")
    sharing, tok = _share_start([r_wo_s, r_wi_s, r_wo_g], name="grads_share_a")
    grad_x, d_pre0 = _first_bwd(dx1, dh0, x0, behind(norm_pre[0:1], tok), name="pre0_bwd")

    late_part = _pack([jnp.concatenate([d_pre0, d_pre1], axis=0)])
    late_flight, tok = _dev_gather_start(late_part, name="small_late_start")

    def big_update(w, g, m, v, name, after=None):
        return [u[None] for u in _adamw(w[0], g, m[0], v[0], name=name, after=after)]

    g_wo_sgu, g_wi_sgu, g_wo_gla = _share_wait(sharing, [grad_x, tok], name="grads_share_a_wait")
    u_wi_sgu = big_update(sgu_w_in, g_wi_sgu, m_sgu_w_in, v_sgu_w_in, "adamw_sgu_w_in")
    u_wo_gla = big_update(gla_w_out, g_wo_gla, m_gla_w_out, v_gla_w_out, "adamw_gla_w_out", after=u_wi_sgu[1])

    r_wi_g, behind_this = None, u_wo_gla[1]
    for p, flight in enumerate(fl_wi_g):
        pair, landed = _scatter_wait(flight, behind_this, name=f"g_gla_in_wait{p}")
        r_wi_g = behind_this = _chip_sum(pair, landed, slots, part=(p, 2), into=r_wi_g, name=f"g_gla_in_sum{p}")
    gt_wi_gla, = _sibling_share_halves([r_wi_g], name="grads_share_b")
    u_wi_gla_t = _adamw(wt_in_g, gt_wi_gla, mt_in_g, vt_in_g, name="adamw_gla_w_in")
    u_wi_gla = [u.T[None] for u in u_wi_gla_t]
    u_wo_sgu = big_update(sgu_w_out, g_wo_sgu, m_sgu_w_out, v_sgu_w_out, "adamw_sgu_w_out", after=u_wi_gla_t[1])

    def summed_over_devices(part, flight, after, shapes, name):
        land = _dev_gather_wait(flight, after, name=name + "_wait")
        every = lax.dynamic_update_slice(land, part[None], (2 * chip + core, 0, 0))
        return _unpack(_stack_sum(every, name=name + "_sum"), shapes)

    (g_post, g_bg, g_og, g_wsp, g_bsp, g_w2_full, g_lg_full, g_lb_full, loss_vec) = summed_over_devices(
        early_part, early_flight, u_wo_sgu[1], early_shapes, "small_early")
    g_pre, = summed_over_devices(late_part, late_flight, loss_vec, [norm_pre.shape], "small_late")
    loss = loss_vec[0, 0]
    g_w2 = lax.dynamic_slice_in_dim(g_w2_full, chip * (dk // N_CHIPS), dk // N_CHIPS, axis=2)
    g_lg = lax.dynamic_slice_in_dim(g_lg_full, chip * (d // N_CHIPS), d // N_CHIPS, axis=1)
    g_lb = lax.dynamic_slice_in_dim(g_lb_full, chip * (d // N_CHIPS), d // N_CHIPS, axis=1)

    small_w = [norm_pre, norm_post, gla_b_gate, gla_o_gain, sgu_w_spatial, sgu_b_spatial, gla_w_gate2, sgu_ln_gain,
               sgu_ln_bias]
    small_g = [g_pre, g_post, g_bg, g_og, g_wsp, g_bsp, g_w2, g_lg, g_lb]
    small_m = [m_norm_pre, m_norm_post, m_gla_b_gate, m_gla_o_gain, m_sgu_w_spatial, m_sgu_b_spatial, m_gla_w_gate2,
               m_sgu_ln_gain, m_sgu_ln_bias]
    small_v = [v_norm_pre, v_norm_post, v_gla_b_gate, v_gla_o_gain, v_sgu_w_spatial, v_sgu_b_spatial, v_gla_w_gate2,
               v_sgu_ln_gain, v_sgu_ln_bias]
    own_shapes = [w.shape for w in small_w]
    _, s_dl, s_m, s_v = _adamw(_pack(small_w), _pack(small_g), _pack(small_m), _pack(small_v), name="adamw_small")
    dl_s, m_s, v_s = _unpack(s_dl, own_shapes), _unpack(s_m, own_shapes), _unpack(s_v, own_shapes)

    def ordered(small, kind):
        pre, post, bg, og, wsp, bsp, w2, lg, lb = small
        return [pre, post, u_wi_gla[kind], w2, bg, og, u_wo_gla[kind], u_wi_sgu[kind], lg, lb, wsp, bsp, u_wo_sgu[kind]]

    return (loss, grad_x[None], *ordered(small_g, 0), *ordered(dl_s, 1), *ordered(m_s, 2), *ordered(v_s, 3))
```

```python
import math

import jax
import jax.numpy as jnp
from jax import lax
from jax.experimental import pallas as pl
from jax.experimental.pallas import tpu as pltpu

F32 = jnp.float32
BF16 = jnp.bfloat16
MESH = pl.DeviceIdType.MESH

EPS = 1e-6
CHUNK = 64
GLA_HEADS = 4
GLA_GATE_RANK = 16
GLA_TAU = 16.0
SGU_BLOCK = 128
SGU_GROUPS = 8
N_CHIPS = 4
N_DEV = 8
LANES = 128

ADAM_LR = 0.001
ADAM_B1 = 0.9
ADAM_B2 = 0.999
ADAM_EPS = 1e-08
ADAM_WD = 0.01
ADAM_STEP = 10

VMEM_LIMIT = 56 * 1024 * 1024


def _cparams(sem=None):
    return pltpu.CompilerParams(dimension_semantics=sem, vmem_limit_bytes=VMEM_LIMIT)


def _pick(n, cap, unit=LANES):
    best = None
    for t in range(unit, min(n, cap) + 1, unit):
        if n % t == 0:
            best = t
    assert best is not None, (n, cap, unit)
    return best


def _dot(a, b, dims):
    return lax.dot_general(a, b, (dims, ((), ())), preferred_element_type=F32)


def _dot_nn(a, b):
    return _dot(a, b, ((1,), (0,)))


def _dot_nt(a, b):
    return _dot(a, b, ((1,), (1,)))


def _dot_tn(a, b):
    return _dot(a, b, ((0,), (0,)))


def _matmul(a, b, *, mode, out_dtype, name, tm=1024, tn=512, b_shards=False, after=None):
    M, K = a.shape
    if b_shards:
        ns, Kb, bc = b.shape
        N, tn = ns * bc, _pick(bc, tn)
        per = bc // tn
        b_spec = pl.BlockSpec((None, K, tn), lambda i, j: (j // per, 0, j % per))
    elif mode == "nt":
        N, Kb = b.shape
        tn = _pick(N, tn)
        b_spec = pl.BlockSpec((tn, K), lambda i, j: (j, 0))
    else:
        Kb, N = b.shape
        tn = _pick(N, tn)
        b_spec = pl.BlockSpec((K, tn), lambda i, j: (0, j))
    assert K == Kb and a.dtype == b.dtype == BF16, (a.shape, b.shape, mode)
    tm = _pick(M, tm)
    dims = ((1,), (1,)) if mode == "nt" else ((1,), (0,))
    extra_specs, extra_args = ([], []) if after is None else ([pl.BlockSpec(memory_space=pl.ANY)], [after])

    def body(a_ref, b_ref, *rest):
        rest[-1][...] = _dot(a_ref[...], b_ref[...], dims).astype(out_dtype)

    return pl.pallas_call(
        body, name=name, grid=(M // tm, N // tn),
        in_specs=[pl.BlockSpec((tm, K), lambda i, j: (i, 0)), b_spec] + extra_specs,
        out_specs=pl.BlockSpec((tm, tn), lambda i, j: (i, j)), out_shape=jax.ShapeDtypeStruct((M, N), out_dtype),
        compiler_params=_cparams(("parallel", "parallel")),
    )(a, b, *extra_args)


def _matmul_nt_shards(a, b, *, out_dtype, name, tm=1024, tn=512, after=None):
    M, K = a.shape
    ns, N, kc = b.shape
    assert K == ns * kc
    tm, tn = _pick(M, tm), _pick(N, tn)

    def body(a_ref, *rest):
        b_refs, o_ref = rest[:ns], rest[ns + (after is not None)]
        acc = _dot_nt(a_ref[:, 0:kc], b_refs[0][...])
        for j in range(1, ns):
            acc += _dot_nt(a_ref[:, j * kc:(j + 1) * kc], b_refs[j][...])
        o_ref[...] = acc.astype(out_dtype)

    def shard(j):
        return pl.BlockSpec((None, tn, kc), lambda i, n: (j, n, 0))

    extra_specs, extra_args = ([], []) if after is None else ([pl.BlockSpec(memory_space=pl.ANY)], [after])
    return pl.pallas_call(
        body, name=name, grid=(M // tm, N // tn),
        in_specs=[pl.BlockSpec((tm, K), lambda i, n: (i, 0))] + [shard(j) for j in range(ns)] + extra_specs,
        out_specs=pl.BlockSpec((tm, tn), lambda i, n: (i, n)), out_shape=jax.ShapeDtypeStruct((M, N), out_dtype),
        compiler_params=_cparams(("parallel", "parallel")),
    )(a, *([b] * ns), *extra_args)


def _rstd(x):
    return lax.rsqrt(jnp.mean(x * x, axis=-1, keepdims=True) + EPS)


def _row_spec(tr, d):
    return pl.BlockSpec((tr, d), lambda i: (i, 0))


def _vec_spec(d):
    return pl.BlockSpec((1, d), lambda i: (0, 0))


def _acc_rows(ref, i, val, cols=slice(None)):
    @pl.when(i == 0)
    def _():
        ref[:, cols] = val

    @pl.when(i > 0)
    def _():
        ref[:, cols] += val


def _norm_pre(x, gain, *, name, tr=256):
    t, d = x.shape
    tr = _pick(t, tr, 8)

    def body(x_ref, g_ref, h_ref):
        xv = x_ref[...]
        h_ref[...] = (xv * _rstd(xv) * g_ref[...]).astype(BF16)

    return pl.pallas_call(
        body, name=name, grid=(t // tr,), in_specs=[_row_spec(tr, d), _vec_spec(d)], out_specs=_row_spec(tr, d),
        out_shape=jax.ShapeDtypeStruct((t, d), BF16), compiler_params=_cparams(("parallel",)),
    )(x, gain)


def _post_then_pre(x, y, post_gain, pre_gain, *, name, tr=256):
    t, d = x.shape
    tr = _pick(t, tr, 8)

    def body(x_ref, y_ref, pg_ref, ng_ref, xn_ref, h_ref):
        yv = y_ref[...]
        xn = x_ref[...] + yv * _rstd(yv) * pg_ref[...]
        xn_ref[...] = xn
        h_ref[...] = (xn * _rstd(xn) * ng_ref[...]).astype(BF16)

    return pl.pallas_call(
        body, name=name, grid=(t // tr,),
        in_specs=[_row_spec(tr, d), _row_spec(tr, d), _vec_spec(d), _vec_spec(d)],
        out_specs=[_row_spec(tr, d), _row_spec(tr, d)],
        out_shape=[jax.ShapeDtypeStruct((t, d), F32), jax.ShapeDtypeStruct((t, d), BF16)],
        compiler_params=_cparams(("parallel",)),
    )(x, y, post_gain, pre_gain)


def _norm_bwd(dy, n, r, gain):
    dn = dy * gain
    return r * (dn - n * jnp.mean(dn * n, axis=-1, keepdims=True))


def _loss_head(x, y, post_gain, target, *, name, tr=256):
    t, d = x.shape
    tr = _pick(t, tr, 8)

    def body(x_ref, y_ref, pg_ref, t_ref, loss_ref, dx_ref, dy_ref, dpg_ref):
        i = pl.program_id(0)
        yv = y_ref[...]
        r = _rstd(yv)
        n = yv * r
        err = x_ref[...] + n * pg_ref[...] - t_ref[...]
        dx = err * (1.0 / d)
        dx_ref[...] = dx
        part = 0.5 * jnp.sum(jnp.mean(err * err, axis=-1, keepdims=True), axis=0, keepdims=True)
        _acc_rows(loss_ref, i, jnp.broadcast_to(part, (1, LANES)))
        _acc_rows(dpg_ref, i, jnp.sum(dx * n, axis=0, keepdims=True))
        dy_ref[...] = _norm_bwd(dx, n, r, pg_ref[...]).astype(BF16)

    return pl.pallas_call(
        body, name=name, grid=(t // tr,),
        in_specs=[_row_spec(tr, d), _row_spec(tr, d), _vec_spec(d), _row_spec(tr, d)],
        out_specs=[_vec_spec(LANES), _row_spec(tr, d), _row_spec(tr, d), _vec_spec(d)],
        out_shape=[jax.ShapeDtypeStruct((1, LANES), F32), jax.ShapeDtypeStruct((t, d), F32),
                   jax.ShapeDtypeStruct((t, d), BF16), jax.ShapeDtypeStruct((1, d), F32)],
        compiler_params=_cparams(("arbitrary",)),
    )(x, y, post_gain, target)


def _mid_bwd(dx_out, dh, x, pre_gain, y_prev, post_gain_prev, *, name, tr=256):
    t, d = x.shape
    tr = _pick(t, tr, 8)

    def body(dxo_ref, dh_ref, x_ref, ng_ref, y_ref, pg_ref, dx_ref, dy_ref, dng_ref, dpg_ref):
        i = pl.program_id(0)
        xv = x_ref[...]
        r = _rstd(xv)
        xh = xv * r
        dhv = dh_ref[...]
        _acc_rows(dng_ref, i, jnp.sum(dhv * xh, axis=0, keepdims=True))
        dx = dxo_ref[...] + _norm_bwd(dhv, xh, r, ng_ref[...])
        dx_ref[...] = dx
        yv = y_ref[...]
        ry = _rstd(yv)
        n = yv * ry
        _acc_rows(dpg_ref, i, jnp.sum(dx * n, axis=0, keepdims=True))
        dy_ref[...] = _norm_bwd(dx, n, ry, pg_ref[...]).astype(BF16)

    return pl.pallas_call(
        body, name=name, grid=(t // tr,),
        in_specs=[_row_spec(tr, d), _row_spec(tr, d), _row_spec(tr, d), _vec_spec(d), _row_spec(tr, d), _vec_spec(d)],
        out_specs=[_row_spec(tr, d), _row_spec(tr, d), _vec_spec(d), _vec_spec(d)],
        out_shape=[jax.ShapeDtypeStruct((t, d), F32), jax.ShapeDtypeStruct((t, d), BF16),
                   jax.ShapeDtypeStruct((1, d), F32), jax.ShapeDtypeStruct((1, d), F32)],
        compiler_params=_cparams(("arbitrary",)),
    )(dx_out, dh, x, pre_gain, y_prev, post_gain_prev)


def _first_bwd(dx_out, dh, x, pre_gain, *, name, tr=256):
    t, d = x.shape
    tr = _pick(t, tr, 8)

    def body(dxo_ref, dh_ref, x_ref, ng_ref, dx_ref, dng_ref):
        i = pl.program_id(0)
        xv = x_ref[...]
        r = _rstd(xv)
        xh = xv * r
        dhv = dh_ref[...]
        _acc_rows(dng_ref, i, jnp.sum(dhv * xh, axis=0, keepdims=True))
        dx_ref[...] = dxo_ref[...] + _norm_bwd(dhv, xh, r, ng_ref[...])

    return pl.pallas_call(
        body, name=name, grid=(t // tr,),
        in_specs=[_row_spec(tr, d), _row_spec(tr, d), _row_spec(tr, d), _vec_spec(d)],
        out_specs=[_row_spec(tr, d), _vec_spec(d)],
        out_shape=[jax.ShapeDtypeStruct((t, d), F32), jax.ShapeDtypeStruct((1, d), F32)],
        compiler_params=_cparams(("arbitrary",)),
    )(dx_out, dh, x, pre_gain)


def _sigmoid(x):
    return 1.0 / (1.0 + jnp.exp(-x))


def _log_sigmoid(x):
    return jnp.minimum(x, 0.0) - jnp.log(1.0 + jnp.exp(-jnp.abs(x)))


_GELU_C = math.sqrt(2.0 / math.pi)


_GELU_A = 0.044715


def _gelu_parts(x, with_grad=True):
    x2 = x * x
    h = 0.5 * jnp.tanh(x * (_GELU_C + (_GELU_C * _GELU_A) * x2)) + 0.5
    val = x * h
    if not with_grad:
        return val, None
    return val, h * (1.0 + (1.0 - h) * (x * (2.0 * _GELU_C + (6.0 * _GELU_C * _GELU_A) * x2)))


def _split3(x):
    hi = x.astype(BF16)
    r1 = x - hi.astype(F32)
    mid = r1.astype(BF16)
    lo = (r1 - mid.astype(F32)).astype(BF16)
    return hi, mid, lo


def _tri_matmul(tri_bf16, x):
    hi, mid, lo = _split3(x)
    return _dot_nn(tri_bf16, hi) + _dot_nn(tri_bf16, mid) + _dot_nn(tri_bf16, lo)


def _gla_dims(d):
    dk, dv = d // 2, d
    return dk, dv, dk // GLA_HEADS, dv // GLA_HEADS


def _col_pieces(a, b, lay):
    ws, wp = lay
    out = []
    while a < b:
        j = a // ws
        end = min(b, (j + 1) * ws)
        out.append((j * wp + a - j * ws, end - a))
        a = end
    return out


def _load_cols(ref, a, b, lay):
    parts = [ref[:, s:s + n] for s, n in _col_pieces(a, b, lay)]
    return parts[0] if len(parts) == 1 else jnp.concatenate(parts, axis=1)


def _store_cols(ref, a, val, lay):
    off = 0
    for s, n in _col_pieces(a, a + val.shape[1], lay):
        ref[:, s:s + n] = val[:, off:off + n]
        off += n


def _gate_window(c_r, lay):
    (start, _), = _col_pieces(c_r, c_r + GLA_GATE_RANK, lay)
    assert (start % lay[1]) + LANES <= lay[1]
    return slice(start, start + LANES)


def _gla_gates(glr, k, w2_ref, b_ref):
    z = _dot_nn(glr.astype(BF16), w2_ref[...].astype(BF16)) + b_ref[...]
    la = _log_sigmoid(z) * (1.0 / GLA_TAU)
    row = lax.broadcasted_iota(jnp.int32, (CHUNK, CHUNK), 0)
    col = lax.broadcasted_iota(jnp.int32, (CHUNK, CHUNK), 1)
    incl = (row >= col).astype(BF16)
    bcum = _tri_matmul(incl, la)
    b_end = bcum[CHUNK - 1:CHUNK, :]
    e_rest = jnp.exp(b_end - bcum)
    return z, e_rest, k * e_rest, jnp.exp(b_end)


def _gla_fwd(proj, w2p, b_gate, o_gain, lay, *, name):
    t, wcols = proj.shape
    d = o_gain.shape[1]
    dk, dv, dkh, dvh = _gla_dims(d)
    nc = t // CHUNK
    c_k, c_v, c_g, c_r = dk, 2 * dk, 2 * dk + dv, 2 * dk + 2 * dv
    scale = dkh ** -0.5

    def body(p_ref, w2_ref, b_ref, og_ref, o_ref, a_ref, sb_ref, sfin_ref, s_ref):
        i = pl.program_id(0)

        @pl.when(i == 0)
        def _():
            s_ref[...] = jnp.zeros_like(s_ref)

        q = _load_cols(p_ref, 0, dk, lay) * scale
        k = _load_cols(p_ref, c_k, c_k + dk, lay)
        glr = p_ref[:, _gate_window(c_r, lay)]
        _, _, kdec, decay = _gla_gates(glr, k, w2_ref, b_ref)
        for h in range(GLA_HEADS):
            ks = slice(h * dkh, (h + 1) * dkh)
            vs = slice(h * dvh, (h + 1) * dvh)
            v_h = _load_cols(p_ref, c_v + h * dvh, c_v + (h + 1) * dvh, lay)
            g_h = _load_cols(p_ref, c_g + h * dvh, c_g + (h + 1) * dvh, lay)
            s_old = s_ref[h]
            sb_ref[0, h] = s_old
            s_new = s_old * decay[:, ks] + _dot_tn(v_h.astype(BF16), kdec[:, ks].astype(BF16))
            s_ref[h] = s_new
            o_h = _dot_nt(q[:, ks].astype(BF16), s_new.astype(BF16))
            o_ref[:, vs] = o_h
            on = o_h * _rstd(o_h)
            a_ref[:, vs] = (on * og_ref[:, vs] * (g_h * _sigmoid(g_h))).astype(BF16)

        @pl.when(i == nc - 1)
        def _():
            sfin_ref[...] = s_ref[...]

    full = lambda *shape: pl.BlockSpec(shape, lambda i: (0,) * len(shape))
    return pl.pallas_call(
        body, name=name, grid=(nc,),
        in_specs=[pl.BlockSpec((CHUNK, wcols), lambda i: (i, 0)), full(LANES, dk), full(1, dk), full(1, dv)],
        out_specs=[pl.BlockSpec((CHUNK, dv), lambda i: (i, 0)), pl.BlockSpec((CHUNK, dv), lambda i: (i, 0)),
                   pl.BlockSpec((1, GLA_HEADS, dvh, dkh), lambda i: (i, 0, 0, 0)), full(GLA_HEADS, dvh, dkh)],
        out_shape=[jax.ShapeDtypeStruct((t, dv), F32), jax.ShapeDtypeStruct((t, dv), BF16),
                   jax.ShapeDtypeStruct((nc, GLA_HEADS, dvh, dkh), F32),
                   jax.ShapeDtypeStruct((GLA_HEADS, dvh, dkh), F32)],
        scratch_shapes=[pltpu.VMEM((GLA_HEADS, dvh, dkh), F32)],
        compiler_params=_cparams(("arbitrary",)),
    )(proj, w2p, b_gate, o_gain)


def _gla_bwd(da, o, proj, w2p, b_gate, o_gain, s_before, s_final, lay, *, name):
    t, wcols = proj.shape
    d = o_gain.shape[1]
    dk, dv, dkh, dvh = _gla_dims(d)
    nc = t // CHUNK
    c_k, c_v, c_g, c_r = dk, 2 * dk, 2 * dk + dv, 2 * dk + 2 * dv
    scale = dkh ** -0.5

    def body(da_ref, o_ref, p_ref, w2_ref, b_ref, og_ref, sb_ref, sfin_ref,
             dp_ref, dog_ref, db_ref, dw2_ref, s_ref, gc_ref, dkd_ref):
        i = pl.program_id(0)

        @pl.when(i == 0)
        def _():
            s_ref[...] = sfin_ref[...]
            gc_ref[...] = jnp.zeros_like(gc_ref)

        ws, wp = lay
        for j in range(N_CHIPS):
            dp_ref[:, j * wp + ws:(j + 1) * wp] = jnp.zeros((CHUNK, wp - ws), BF16)
        q = _load_cols(p_ref, 0, dk, lay) * scale
        k = _load_cols(p_ref, c_k, c_k + dk, lay)
        glr = p_ref[:, _gate_window(c_r, lay)]
        z, e_rest, kdec, decay = _gla_gates(glr, k, w2_ref, b_ref)
        ddecay = []
        for h in range(GLA_HEADS):
            ks = slice(h * dkh, (h + 1) * dkh)
            vs = slice(h * dvh, (h + 1) * dvh)
            v_h = _load_cols(p_ref, c_v + h * dvh, c_v + (h + 1) * dvh, lay)
            g_h = _load_cols(p_ref, c_g + h * dvh, c_g + (h + 1) * dvh, lay)
            da_h = da_ref[:, vs]
            o_h = o_ref[:, vs]
            og_h = og_ref[:, vs]
            r = _rstd(o_h)
            on = o_h * r
            sg = _sigmoid(g_h)
            silu = g_h * sg
            _acc_rows(dog_ref, i, jnp.sum(da_h * silu * on, axis=0, keepdims=True), vs)
            _store_cols(dp_ref, c_g + h * dvh, (da_h * (on * og_h) * (sg * (1.0 + g_h * (1.0 - sg)))).astype(BF16),
                        lay)
            don = da_h * silu * og_h
            do_h = (r * (don - on * jnp.mean(don * on, axis=-1, keepdims=True))).astype(BF16)
            s_cur = s_ref[h]
            _store_cols(dp_ref, h * dkh, (_dot_nn(do_h, s_cur.astype(BF16)) * scale).astype(BF16), lay)
            g_tot = gc_ref[h] + _dot_tn(do_h, q[:, ks].astype(BF16))
            g_bf = g_tot.astype(BF16)
            dkd_ref[:, ks] = _dot_nn(v_h.astype(BF16), g_bf)
            _store_cols(dp_ref, c_v + h * dvh, _dot_nt(kdec[:, ks].astype(BF16), g_bf).astype(BF16), lay)
            s_prev = sb_ref[0, h]
            ddecay.append(jnp.sum(g_tot * s_prev, axis=0, keepdims=True))
            gc_ref[h] = g_tot * decay[:, ks]
            s_ref[h] = s_prev
        dkdec = dkd_ref[...]
        _store_cols(dp_ref, c_k, (dkdec * e_rest).astype(BF16), lay)
        d_e = dkdec * kdec
        row = lax.broadcasted_iota(jnp.int32, (CHUNK, CHUNK), 0)
        col = lax.broadcasted_iota(jnp.int32, (CHUNK, CHUNK), 1)
        excl = (row > col).astype(BF16)
        dla = jnp.concatenate(ddecay, axis=1) * decay + _tri_matmul(excl, d_e)
        dz = dla * (1.0 / GLA_TAU) * (1.0 - _sigmoid(z))
        _acc_rows(db_ref, i, jnp.sum(dz, axis=0, keepdims=True))
        dz_bf = dz.astype(BF16)
        dw2 = _dot_tn(glr.astype(BF16), dz_bf)

        @pl.when(i == 0)
        def _():
            dw2_ref[...] = dw2

        @pl.when(i > 0)
        def _():
            dw2_ref[...] += dw2

        dp_ref[:, _gate_window(c_r, lay)] = _dot_nt(dz_bf, w2_ref[...].astype(BF16)).astype(BF16)

    rev = lambda i: (nc - 1 - i, 0)
    full = lambda *shape: pl.BlockSpec(shape, lambda i: (0,) * len(shape))
    return pl.pallas_call(
        body, name=name, grid=(nc,),
        in_specs=[pl.BlockSpec((CHUNK, dv), rev), pl.BlockSpec((CHUNK, dv), rev), pl.BlockSpec((CHUNK, wcols), rev),
                  full(LANES, dk), full(1, dk), full(1, dv),
                  pl.BlockSpec((1, GLA_HEADS, dvh, dkh), lambda i: (nc - 1 - i, 0, 0, 0)), full(GLA_HEADS, dvh, dkh)],
        out_specs=[pl.BlockSpec((CHUNK, wcols), rev), full(1, dv), full(1, dk), full(LANES, dk)],
        out_shape=[jax.ShapeDtypeStruct((t, wcols), BF16), jax.ShapeDtypeStruct((1, dv), F32),
                   jax.ShapeDtypeStruct((1, dk), F32), jax.ShapeDtypeStruct((LANES, dk), F32)],
        scratch_shapes=[pltpu.VMEM((GLA_HEADS, dvh, dkh), F32), pltpu.VMEM((GLA_HEADS, dvh, dkh), F32),
                        pltpu.VMEM((CHUNK, dk), F32)],
        compiler_params=_cparams(("arbitrary",)),
    )(da, o, proj, w2p, b_gate, o_gain, s_before, s_final)


def _sgu_mid(p_ref, lg_ref, lb_ref, ws_ref, bst_ref, w, with_grad=True):
    gd = w // SGU_GROUPS
    u_act, du_fac = _gelu_parts(p_ref[:, 0:w], with_grad)
    vf, dv_fac = _gelu_parts(p_ref[:, w:2 * w], with_grad)
    mu = jnp.mean(vf, axis=-1, keepdims=True)
    cen = vf - mu
    rstd = lax.rsqrt(jnp.mean(cen * cen, axis=-1, keepdims=True) + EPS)
    xh = cen * rstd
    vn = (xh * lg_ref[...] + lb_ref[...]).astype(BF16)
    vs = [_dot_nn(ws_ref[g].astype(BF16), vn[:, g * gd:(g + 1) * gd]) + bst_ref[:, g:g + 1]
          for g in range(SGU_GROUPS)]
    return u_act, du_fac, dv_fac, rstd, xh, vn, vs


def _sgu_fwd(proj, ln_gain, ln_bias, ws_masked, bs_t, *, name):
    t, w3 = proj.shape
    w = w3 // 3
    gd = w // SGU_GROUPS
    nb = t // SGU_BLOCK

    def body(p_ref, lg_ref, lb_ref, ws_ref, bst_ref, a_ref):
        u_act, _, _, _, _, _, vs = _sgu_mid(p_ref, lg_ref, lb_ref, ws_ref, bst_ref, w, with_grad=False)
        for g in range(SGU_GROUPS):
            cs = slice(g * gd, (g + 1) * gd)
            gate = p_ref[:, 2 * w + g * gd:2 * w + (g + 1) * gd]
            a_ref[:, cs] = (u_act[:, cs] * vs[g] * (gate * _sigmoid(gate))).astype(BF16)

    full = lambda *shape: pl.BlockSpec(shape, lambda i: (0,) * len(shape))
    return pl.pallas_call(
        body, name=name, grid=(nb,),
        in_specs=[pl.BlockSpec((SGU_BLOCK, w3), lambda i: (i, 0)), full(1, w), full(1, w),
                  full(SGU_GROUPS, SGU_BLOCK, SGU_BLOCK), full(SGU_BLOCK, SGU_GROUPS)],
        out_specs=pl.BlockSpec((SGU_BLOCK, w), lambda i: (i, 0)),
        out_shape=jax.ShapeDtypeStruct((t, w), BF16),
        compiler_params=_cparams(("parallel",)),
    )(proj, ln_gain, ln_bias, ws_masked, bs_t)


def _sgu_bwd(da, proj, ln_gain, ln_bias, ws_masked, ws_masked_t, bs_t, *, name):
    t, w3 = proj.shape
    w = w3 // 3
    gd = w // SGU_GROUPS
    nb = t // SGU_BLOCK

    def body(da_ref, p_ref, lg_ref, lb_ref, ws_ref, wst_ref, bst_ref, dp_ref, dws_ref, dbst_ref, dlg_ref, dlb_ref,
             dvn_ref):
        i = pl.program_id(0)
        u_act, du_fac, dv_fac, rstd, xh, vn, vs = _sgu_mid(p_ref, lg_ref, lb_ref, ws_ref, bst_ref, w)
        for g in range(SGU_GROUPS):
            cs = slice(g * gd, (g + 1) * gd)
            gate = p_ref[:, 2 * w + g * gd:2 * w + (g + 1) * gd]
            sg = _sigmoid(gate)
            silu = gate * sg
            da_g = da_ref[:, cs]
            ua_g = u_act[:, cs]
            dp_ref[:, cs] = (da_g * vs[g] * silu * du_fac[:, cs]).astype(BF16)
            dp_ref[:, 2 * w + g * gd:2 * w + (g + 1) * gd] = (
                da_g * ua_g * vs[g] * (sg * (1.0 + gate * (1.0 - sg)))).astype(BF16)
            dvs = da_g * ua_g * silu
            dvs_bf = dvs.astype(BF16)
            dvn_ref[:, cs] = _dot_nn(wst_ref[g].astype(BF16), dvs_bf)
            dws = _dot_nt(dvs_bf, vn[:, cs])
            dbs = jnp.sum(dvs, axis=1, keepdims=True)

            @pl.when(i == 0)
            def _():
                dws_ref[g] = dws
                dbst_ref[:, g:g + 1] = dbs

            @pl.when(i > 0)
            def _():
                dws_ref[g] += dws
                dbst_ref[:, g:g + 1] += dbs

        dvn = dvn_ref[...]
        _acc_rows(dlg_ref, i, jnp.sum(dvn * xh, axis=0, keepdims=True))
        _acc_rows(dlb_ref, i, jnp.sum(dvn, axis=0, keepdims=True))
        dxh = dvn * lg_ref[...]
        dvf = rstd * (dxh - jnp.mean(dxh, axis=-1, keepdims=True)
                      - xh * jnp.mean(dxh * xh, axis=-1, keepdims=True))
        dp_ref[:, w:2 * w] = (dvf * dv_fac).astype(BF16)

    full = lambda *shape: pl.BlockSpec(shape, lambda i: (0,) * len(shape))
    return pl.pallas_call(
        body, name=name, grid=(nb,),
        in_specs=[pl.BlockSpec((SGU_BLOCK, w), lambda i: (i, 0)), pl.BlockSpec((SGU_BLOCK, w3), lambda i: (i, 0)),
                  full(1, w), full(1, w), full(SGU_GROUPS, SGU_BLOCK, SGU_BLOCK),
                  full(SGU_GROUPS, SGU_BLOCK, SGU_BLOCK), full(SGU_BLOCK, SGU_GROUPS)],
        out_specs=[pl.BlockSpec((SGU_BLOCK, w3), lambda i: (i, 0)), full(SGU_GROUPS, SGU_BLOCK, SGU_BLOCK),
                   full(SGU_BLOCK, SGU_GROUPS), full(1, w), full(1, w)],
        out_shape=[jax.ShapeDtypeStruct((t, w3), BF16), jax.ShapeDtypeStruct((SGU_GROUPS, SGU_BLOCK, SGU_BLOCK), F32),
                   jax.ShapeDtypeStruct((SGU_BLOCK, SGU_GROUPS), F32), jax.ShapeDtypeStruct((1, w), F32),
                   jax.ShapeDtypeStruct((1, w), F32)],
        scratch_shapes=[pltpu.VMEM((SGU_BLOCK, w), F32)],
        compiler_params=_cparams(("arbitrary",)),
    )(da, proj, ln_gain, ln_bias, ws_masked, ws_masked_t, bs_t)


def _tile2d(rows, cols, block_bytes, row_unit):
    if rows % row_unit == 0:
        return _pick(rows, max(row_unit, block_bytes // (4 * cols)), row_unit), cols
    return rows, _pick(cols, max(LANES, block_bytes // (4 * rows)))


def _adamw(w, g, m, v, *, name, block_bytes=1 << 20, after=None):
    rows, cols = w.shape
    tr, tc = _tile2d(rows, cols, block_bytes, 8)
    g_rows = g.shape[0]
    assert g_rows == rows or tr == rows
    extra_specs, extra_args = ([], []) if after is None else ([pl.BlockSpec(memory_space=pl.ANY)], [after])

    def body(w_ref, g_ref, m_ref, v_ref, *rest):
        go_ref, d_ref, mo_ref, vo_ref = rest[len(extra_args):]
        gv = g_ref[0:tr, :]
        go_ref[...] = gv
        mn = ADAM_B1 * m_ref[...] + (1.0 - ADAM_B1) * gv
        vn = ADAM_B2 * v_ref[...] + (1.0 - ADAM_B2) * (gv * gv)
        m_hat = mn / (1.0 - ADAM_B1 ** ADAM_STEP)
        v_hat = vn / (1.0 - ADAM_B2 ** ADAM_STEP)
        d_ref[...] = -ADAM_LR * (m_hat / (jnp.sqrt(v_hat) + ADAM_EPS) + ADAM_WD * w_ref[...])
        mo_ref[...] = mn
        vo_ref[...] = vn

    spec = pl.BlockSpec((tr, tc), lambda i, j: (i, j))
    g_spec = spec if g_rows == rows else pl.BlockSpec((g_rows, tc), lambda i, j: (0, j))
    return pl.pallas_call(
        body, name=name, grid=(rows // tr, cols // tc), in_specs=[spec, g_spec, spec, spec] + extra_specs,
        out_specs=[spec] * 4, out_shape=[jax.ShapeDtypeStruct((rows, cols), F32)] * 4,
        compiler_params=_cparams(("parallel", "parallel")),
    )(w, g, m, v, *extra_args)


def _matmul_dw_pair(a_me, a_sib, b_me, b_sib, core_idx, *, shards_on, name, after=None, part=(0, 1)):
    T, M = a_me.shape
    N = b_me.shape[1]
    if shards_on == "rows":
        p, count = part
        tm, hc = M // N_CHIPS, N // 2
        hp = hc // count
        tn = _pick(hp, 512)
        per = hp // tn
        grid = (N_CHIPS, per)
        a_spec = pl.BlockSpec((T, tm), lambda i, n, h: (0, i))
        b_me_spec = pl.BlockSpec((T, tn), lambda i, n, h: (0, (h[0] * count + p) * per + n))
        b_sib_spec = pl.BlockSpec((T, tn), lambda i, n, h: (0, p * per + n))
        out_spec = pl.BlockSpec((None, tm, tn), lambda i, n, h: (i, 0, n))
        out_shape = jax.ShapeDtypeStruct((N_CHIPS, tm, hp), BF16)
    else:
        tm, hc = _pick(M, 1024), N // N_CHIPS // 2
        grid = (M // tm, N_CHIPS)
        a_spec = pl.BlockSpec((T, tm), lambda i, j, h: (0, i))
        b_me_spec = pl.BlockSpec((T, hc), lambda i, j, h: (0, 2 * j + h[0]))
        b_sib_spec = pl.BlockSpec((T, hc), lambda i, j, h: (0, j))
        out_spec = pl.BlockSpec((None, tm, hc), lambda i, j, h: (j, i, 0))
        out_shape = jax.ShapeDtypeStruct((N_CHIPS, M, hc), BF16)
    extra_specs, extra_args = ([], []) if after is None else ([pl.BlockSpec(memory_space=pl.ANY)], [after])

    def body(h_ref, am_ref, as_ref, bm_ref, bs_ref, *rest):
        o_ref = rest[len(extra_args)]
        o_ref[...] = (_dot_tn(am_ref[...], bm_ref[...]) + _dot_tn(as_ref[...], bs_ref[...])).astype(BF16)

    grid_spec = pltpu.PrefetchScalarGridSpec(
        num_scalar_prefetch=1, grid=grid, in_specs=[a_spec, a_spec, b_me_spec, b_sib_spec] + extra_specs,
        out_specs=out_spec)
    return pl.pallas_call(
        body, name=name, grid_spec=grid_spec, out_shape=out_shape, compiler_params=_cparams(("parallel", "parallel")),
    )(core_idx, a_me, a_sib, b_me, b_sib, *extra_args)


def _chip_sum(pair, landed, slots, *, name, block_bytes=1 << 20, part=(0, 1), into=None):
    p, count = part
    _, r, hp = pair.shape
    tr, tc = _tile2d(r, hp, block_bytes, 16)
    ncb = hp // tc
    extra_specs, extra_args = ([], []) if into is None else ([pl.BlockSpec(memory_space=pl.ANY)], [into])

    def body(s_ref, own_ref, l0_ref, l1_ref, l2_ref, *rest):
        rest[-1][...] = ((own_ref[...].astype(F32) + l0_ref[...].astype(F32)) + l1_ref[...].astype(F32)
                         ) + l2_ref[...].astype(F32)

    def slab(which):
        return pl.BlockSpec((None, tr, tc), lambda i, k, s: (s[which], i, k))

    grid_spec = pltpu.PrefetchScalarGridSpec(
        num_scalar_prefetch=1, grid=(r // tr, ncb),
        in_specs=[slab(0), slab(1), slab(2), slab(3)] + extra_specs,
        out_specs=pl.BlockSpec((tr, tc), lambda i, k, s: (i, (s[4] * count + p) * ncb + k)))
    return pl.pallas_call(
        body, name=name, grid_spec=grid_spec, out_shape=jax.ShapeDtypeStruct((r, 2 * hp * count), F32),
        input_output_aliases={} if into is None else {5: 0},
        compiler_params=_cparams(("parallel", "parallel")),
    )(slots, pair, landed, landed, landed, *extra_args)


def _stack_sum(x, *, name, out_dtype=F32, block_bytes=1 << 20):
    s, r, c = x.shape
    tr = _pick(r, max(8, block_bytes // (4 * c)), 16) if r % 16 == 0 else r

    def body(x_ref, o_ref):
        acc = x_ref[0].astype(F32)
        for j in range(1, s):
            acc = acc + x_ref[j].astype(F32)
        o_ref[...] = acc.astype(out_dtype)

    return pl.pallas_call(
        body, name=name, grid=(r // tr,),
        in_specs=[pl.BlockSpec((s, tr, c), lambda i: (0, i, 0))], out_specs=pl.BlockSpec((tr, c), lambda i: (i, 0)),
        out_shape=jax.ShapeDtypeStruct((r, c), out_dtype), compiler_params=_cparams(("parallel",)),
    )(x)


HBM = pl.BlockSpec(memory_space=pltpu.HBM)


def _place():
    x, y, c = lax.axis_index("x"), lax.axis_index("y"), lax.axis_index("c")
    other_chips = [(1 - x, y), (x, 1 - y), (1 - x, 1 - y)]
    return x, y, c, other_chips


def _half_cols(cols, which):
    hc = cols // 2
    return pl.ds(pl.multiple_of(which * hc, LANES), hc)


SEM = pl.BlockSpec(memory_space=pltpu.SEMAPHORE)
ANY = pl.BlockSpec(memory_space=pl.ANY)
SIDE_EFFECT = pltpu.SideEffectType.DATAFLOW_SIDE_EFFECTING
TOKEN_SHAPE = (8, LANES)


def _hbm(shape, dtype):
    return pltpu.HBM(shape, dtype)


def _in_hbm(a):
    return pltpu.with_memory_space_constraint(a, pltpu.HBM)


def _gather_copy(src_ref, land_ref, ssem, rsem, k, chip_of_block, to, c):
    cols = src_ref.shape[1]
    return pltpu.make_async_remote_copy(
        src_ref=src_ref.at[:, _half_cols(cols, c)], dst_ref=land_ref.at[chip_of_block, :, _half_cols(cols, c)],
        send_sem=ssem.at[k], recv_sem=rsem.at[k], device_id=to, device_id_type=MESH)


def _gather_start(shards, *, name, after=()):
    n = len(shards)
    after = list(after)

    def body(*refs):
        srcs, lands = refs[:n], refs[n:2 * n]
        outs = refs[2 * n + len(after):]
        token = outs[-1]
        x, y, c, chips = _place()
        me = 2 * x + y
        for a in range(n):
            ssem, rsem = outs[4 * a], outs[4 * a + 1]
            for k, (cx, cy) in enumerate(chips):
                _gather_copy(srcs[a], lands[a], ssem, rsem, k, me, (cx, cy, c), c).start()
        token[...] = jnp.zeros_like(token)

    out_shape, out_specs, aliases = [], [], {}
    for a, s in enumerate(shards):
        out_shape += [pltpu.SemaphoreType.DMA((3,)), pltpu.SemaphoreType.DMA((3,)), _hbm(s.shape, s.dtype),
                      _hbm((N_CHIPS,) + s.shape, s.dtype)]
        out_specs += [SEM, SEM, HBM, HBM]
        aliases[a] = 4 * a + 2
        aliases[n + a] = 4 * a + 3
    out_shape.append(jax.ShapeDtypeStruct(TOKEN_SHAPE, F32))
    out_specs.append(pl.BlockSpec(memory_space=pltpu.VMEM))
    lands = [_in_hbm(lax.empty((N_CHIPS,) + s.shape, s.dtype)) for s in shards]
    res = pl.pallas_call(
        body, name=name, in_specs=[HBM] * (2 * n) + [ANY] * len(after), out_specs=out_specs, out_shape=out_shape,
        input_output_aliases=aliases, compiler_params=pltpu.CompilerParams(has_side_effects=SIDE_EFFECT),
    )(*[_in_hbm(s) for s in shards], *lands, *after)
    return [tuple(res[4 * a:4 * a + 4]) for a in range(n)], res[-1]


def _wait_call(wait_fn, parts, after, *, name):
    ssem, rsem, src, land = parts
    after = list(after) if isinstance(after, (list, tuple)) else [after]

    def body(src_ref, land_ref, ssem_ref, rsem_ref, *rest):
        wait_fn(src_ref, land_ref, ssem_ref, rsem_ref)

    return pl.pallas_call(
        body, name=name, in_specs=[HBM, HBM, SEM, SEM] + [ANY] * len(after), out_specs=[HBM, HBM],
        out_shape=[_hbm(src.shape, src.dtype), _hbm(land.shape, land.dtype)], input_output_aliases={0: 0, 1: 1},
        compiler_params=pltpu.CompilerParams(has_side_effects=SIDE_EFFECT),
    )(src, land, ssem, rsem, *after)


def _gather_wait(parts, after, *, name):
    def wait(src_ref, land_ref, ssem_ref, rsem_ref):
        x, y, c, chips = _place()
        for k, (cx, cy) in enumerate(chips):
            cp = _gather_copy(src_ref, land_ref, ssem_ref, rsem_ref, k, 2 * cx + cy, (x, y, c), c)
            cp.wait_send()
            cp.wait_recv()

    return _wait_call(wait, parts, after, name=name)[1]


def _forward_copy(buf_ref, ssem, rsem, k, slab, which, to):
    part = buf_ref.at[slab, :, _half_cols(buf_ref.shape[2], which)]
    return pltpu.make_async_remote_copy(
        src_ref=part, dst_ref=part, send_sem=ssem.at[k], recv_sem=rsem.at[k], device_id=to, device_id_type=MESH)


def _sibling_forward(land, *, name):
    def body(_, buf, send_sems, recv_sems):
        x, y, c, chips = _place()
        copies = []
        for k, (cx, cy) in enumerate(chips):
            cp = _forward_copy(buf, send_sems, recv_sems, k, 2 * cx + cy, c, (x, y, 1 - c))
            cp.start()
            copies.append(cp)
        for k, (cx, cy) in enumerate(chips):
            _forward_copy(buf, send_sems, recv_sems, k, 2 * cx + cy, 1 - c, (x, y, c)).wait_recv()
        for cp in copies:
            cp.wait_send()

    return pl.pallas_call(
        body, name=name, in_specs=[HBM], out_specs=HBM, out_shape=jax.ShapeDtypeStruct(land.shape, land.dtype),
        input_output_aliases={0: 0},
        scratch_shapes=[pltpu.SemaphoreType.DMA((3,)), pltpu.SemaphoreType.DMA((3,))],
    )(land)


def _share_copy(buf_ref, ssem, rsem, a, which, to):
    part = buf_ref.at[:, _half_cols(buf_ref.shape[1], which)]
    return pltpu.make_async_remote_copy(
        src_ref=part, dst_ref=part, send_sem=ssem.at[a], recv_sem=rsem.at[a], device_id=to, device_id_type=MESH)


def _share_start(arrays, *, name):
    n = len(arrays)

    def body(*refs):
        bufs, ssem, rsem, token = refs[:n], refs[n], refs[n + 1], refs[-1]
        x, y, c, _ = _place()
        for a in range(n):
            _share_copy(bufs[a], ssem, rsem, a, c, (x, y, 1 - c)).start()
        token[...] = jnp.zeros_like(token)

    res = pl.pallas_call(
        body, name=name, in_specs=[HBM] * n,
        out_specs=[SEM, SEM] + [HBM] * n + [pl.BlockSpec(memory_space=pltpu.VMEM)],
        out_shape=[pltpu.SemaphoreType.DMA((n,)), pltpu.SemaphoreType.DMA((n,))]
        + [_hbm(b.shape, b.dtype) for b in arrays] + [jax.ShapeDtypeStruct(TOKEN_SHAPE, F32)],
        input_output_aliases={a: 2 + a for a in range(n)},
        compiler_params=pltpu.CompilerParams(has_side_effects=SIDE_EFFECT),
    )(*[_in_hbm(b) for b in arrays])
    return (res[0], res[1], list(res[2:2 + n])), res[-1]


def _share_wait(parts, after, *, name):
    ssem, rsem, bufs = parts
    n = len(bufs)
    after = list(after) if isinstance(after, (list, tuple)) else [after]

    def body(*refs):
        buf_refs, ssem_ref, rsem_ref = refs[:n], refs[n], refs[n + 1]
        x, y, c, _ = _place()
        for a in range(n):
            _share_copy(buf_refs[a], ssem_ref, rsem_ref, a, c, (x, y, c)).wait_send()
            _share_copy(buf_refs[a], ssem_ref, rsem_ref, a, 1 - c, (x, y, c)).wait_recv()

    return pl.pallas_call(
        body, name=name, in_specs=[HBM] * n + [SEM, SEM] + [ANY] * len(after), out_specs=[HBM] * n,
        out_shape=[_hbm(b.shape, b.dtype) for b in bufs], input_output_aliases={a: a for a in range(n)},
        compiler_params=pltpu.CompilerParams(has_side_effects=SIDE_EFFECT),
    )(*bufs, ssem, rsem, *after)


def _scatter_copy(src_ref, land_ref, ssem, rsem, k, src_slab, dst_slab, to):
    return pltpu.make_async_remote_copy(
        src_ref=src_ref.at[src_slab], dst_ref=land_ref.at[dst_slab], send_sem=ssem.at[k], recv_sem=rsem.at[k],
        device_id=to, device_id_type=MESH)


def _scatter_start(part, *, name):
    def start(src_ref, land_ref, ssem, rsem):
        x, y, c, chips = _place()
        me = 2 * x + y
        for k, (cx, cy) in enumerate(chips):
            _scatter_copy(src_ref, land_ref, ssem, rsem, k, 2 * cx + cy, me, (cx, cy, c)).start()

    return _split_start(start, part, part.shape, N_CHIPS - 1, name=name)


def _scatter_wait(parts, after, *, name):
    def wait(src_ref, land_ref, ssem_ref, rsem_ref):
        x, y, c, chips = _place()
        for k, (cx, cy) in enumerate(chips):
            idx = 2 * cx + cy
            cp = _scatter_copy(src_ref, land_ref, ssem_ref, rsem_ref, k, idx, idx, (x, y, c))
            cp.wait_send()
            cp.wait_recv()

    return _wait_call(wait, parts, after, name=name)


def _split_start(start_fn, src, land_shape, n_sems, *, name):
    def body(src_ref, land_ref, ssem, rsem, src_out, land_out, token):
        start_fn(src_ref, land_ref, ssem, rsem)
        token[...] = jnp.zeros_like(token)

    res = pl.pallas_call(
        body, name=name, in_specs=[HBM, HBM], out_specs=[SEM, SEM, HBM, HBM, pl.BlockSpec(memory_space=pltpu.VMEM)],
        out_shape=[pltpu.SemaphoreType.DMA((n_sems,)), pltpu.SemaphoreType.DMA((n_sems,)), _hbm(src.shape, src.dtype),
                   _hbm(land_shape, src.dtype), jax.ShapeDtypeStruct(TOKEN_SHAPE, F32)],
        input_output_aliases={0: 2, 1: 3}, compiler_params=pltpu.CompilerParams(has_side_effects=SIDE_EFFECT),
    )(_in_hbm(src), _in_hbm(lax.empty(land_shape, src.dtype)))
    return tuple(res[:4]), res[4]


def _sibling_copies(src_ref, land_ref, ssem, rsem, k0, groups, which, to):
    def copy(k, src, dst):
        return pltpu.make_async_remote_copy(
            src_ref=src, dst_ref=dst, send_sem=ssem.at[k], recv_sem=rsem.at[k], device_id=to, device_id_type=MESH)

    if groups == 0:
        return [copy(k0, src_ref, land_ref)]
    hw = src_ref.shape[1] // groups // 2
    return [copy(k0 + j, src_ref.at[:, pl.ds(pl.multiple_of((2 * j + which) * hw, LANES), hw)],
                 land_ref.at[:, j * hw:(j + 1) * hw]) for j in range(groups)]


def _to_sibling_start(items, *, name):
    n = len(items)
    shapes = [a.shape if g == 0 else (a.shape[0], a.shape[1] // 2) for a, g in items]
    first = [sum(max(g, 1) for _, g in items[:k]) for k in range(n + 1)]

    def body(*refs):
        srcs, lands, ssem, rsem, token = refs[:n], refs[n:2 * n], refs[2 * n], refs[2 * n + 1], refs[-1]
        x, y, c, _ = _place()
        for k, (_, g) in enumerate(items):
            for cp in _sibling_copies(srcs[k], lands[k], ssem, rsem, first[k], g, 1 - c, (x, y, 1 - c)):
                cp.start()
        token[...] = jnp.zeros_like(token)

    res = pl.pallas_call(
        body, name=name, in_specs=[HBM] * (2 * n),
        out_specs=[SEM, SEM] + [HBM] * (2 * n) + [pl.BlockSpec(memory_space=pltpu.VMEM)],
        out_shape=[pltpu.SemaphoreType.DMA((first[n],)), pltpu.SemaphoreType.DMA((first[n],))]
        + [_hbm(a.shape, a.dtype) for a, _ in items] + [_hbm(s, a.dtype) for s, (a, _) in zip(shapes, items)]
        + [jax.ShapeDtypeStruct(TOKEN_SHAPE, F32)],
        input_output_aliases={k: 2 + k for k in range(2 * n)},
        compiler_params=pltpu.CompilerParams(has_side_effects=SIDE_EFFECT),
    )(*[_in_hbm(a) for a, _ in items], *[_in_hbm(lax.empty(s, a.dtype)) for s, (a, _) in zip(shapes, items)])
    return [(res[0], res[1], first[k], g, res[2 + k], res[2 + n + k]) for k, (_, g) in enumerate(items)], res[-1]


def _from_sibling(flight, after, *, name):
    ssem, rsem, k0, groups, src, land = flight

    def wait(src_ref, land_ref, ssem_ref, rsem_ref):
        x, y, c, _ = _place()
        for cp in _sibling_copies(src_ref, land_ref, ssem_ref, rsem_ref, k0, groups, 1 - c, (x, y, c)):
            cp.wait_send()
            cp.wait_recv()

    return _wait_call(wait, (ssem, rsem, src, land), after, name=name)


def _dev_peers(x, y, c, chips):
    return [(x, y, 1 - c)] + [(cx, cy, c) for cx, cy in chips] + [(cx, cy, 1 - c) for cx, cy in chips]


def _dev_gather_start(part, *, name):
    def start(src_ref, land_ref, ssem, rsem):
        x, y, c, chips = _place()
        for k, to in enumerate(_dev_peers(x, y, c, chips)):
            pltpu.make_async_remote_copy(
                src_ref=src_ref, dst_ref=land_ref.at[4 * x + 2 * y + c], send_sem=ssem.at[k], recv_sem=rsem.at[k],
                device_id=to, device_id_type=MESH).start()

    return _split_start(start, part, (N_DEV,) + part.shape, N_DEV - 1, name=name)


def _dev_gather_wait(parts, after, *, name):
    def wait(src_ref, land_ref, ssem_ref, rsem_ref):
        x, y, c, chips = _place()
        for k, (px, py, pc) in enumerate(_dev_peers(x, y, c, chips)):
            cp = pltpu.make_async_remote_copy(
                src_ref=src_ref, dst_ref=land_ref.at[4 * px + 2 * py + pc], send_sem=ssem_ref.at[k],
                recv_sem=rsem_ref.at[k], device_id=(x, y, c), device_id_type=MESH)
            cp.wait_send()
            cp.wait_recv()

    return _wait_call(wait, parts, after, name=name)[1]


def _sibling_share_halves(arrays, *, name):
    n = len(arrays)

    def body(*refs):
        bufs = refs[n:2 * n]
        send_sems, recv_sems = refs[2 * n:]
        x, y, c, _ = _place()
        copies = []
        for a in range(n):
            mine = bufs[a].at[:, _half_cols(bufs[a].shape[1], c)]
            cp = pltpu.make_async_remote_copy(
                src_ref=mine, dst_ref=mine, send_sem=send_sems.at[a], recv_sem=recv_sems.at[a],
                device_id=(x, y, 1 - c), device_id_type=MESH)
            cp.start()
            copies.append(cp)
        for a in range(n):
            theirs = bufs[a].at[:, _half_cols(bufs[a].shape[1], 1 - c)]
            pltpu.make_async_remote_copy(
                src_ref=theirs, dst_ref=theirs, send_sem=send_sems.at[a], recv_sem=recv_sems.at[a],
                device_id=(x, y, c), device_id_type=MESH).wait_recv()
        for cp in copies:
            cp.wait_send()

    return pl.pallas_call(
        body, name=name, in_specs=[HBM] * n, out_specs=[HBM] * n,
        out_shape=[jax.ShapeDtypeStruct(h.shape, h.dtype) for h in arrays],
        input_output_aliases={a: a for a in range(n)},
        scratch_shapes=[pltpu.SemaphoreType.DMA((n,)), pltpu.SemaphoreType.DMA((n,))],
    )(*arrays)


def _pack(arrays, rows_multiple=16, width=LANES):
    flat = jnp.concatenate([a.astype(F32).reshape(-1) for a in arrays])
    total = flat.shape[0]
    rows = -(-total // width)
    rows = -(-rows // rows_multiple) * rows_multiple
    return jnp.pad(flat, (0, rows * width - total)).reshape(rows, width)


def _unpack(buf, shapes):
    flat = buf.reshape(-1)
    out, off = [], 0
    for s in shapes:
        n = math.prod(s)
        out.append(flat[off:off + n].reshape(s))
        off += n
    return out


def kernel(x, norm_pre, norm_post, gla_w_in, gla_w_gate2, gla_b_gate, gla_o_gain, gla_w_out, sgu_w_in, sgu_ln_gain, sgu_ln_bias, sgu_w_spatial, sgu_b_spatial, sgu_w_out, loss_target, m_norm_pre, m_norm_post, m_gla_w_in, m_gla_w_gate2, m_gla_b_gate, m_gla_o_gain, m_gla_w_out, m_sgu_w_in, m_sgu_ln_gain, m_sgu_ln_bias, m_sgu_w_spatial, m_sgu_b_spatial, m_sgu_w_out, v_norm_pre, v_norm_post, v_gla_w_in, v_gla_w_gate2, v_gla_b_gate, v_gla_o_gain, v_gla_w_out, v_sgu_w_in, v_sgu_ln_gain, v_sgu_ln_bias, v_sgu_w_spatial, v_sgu_b_spatial, v_sgu_w_out):
    _, t, d = x.shape
    dk = d // 2
    ws = gla_w_in.shape[2]
    wp = -(-ws // LANES) * LANES
    lay = (ws, wp)
    chip =2 * lax.axis_index("x") + lax.axis_index("y")
    core = lax.axis_index("c")
    core_idx = core.astype(jnp.int32).reshape(1)
    others = jnp.arange(N_CHIPS - 1, dtype=jnp.int32)
    others = others + (others >= chip).astype(jnp.int32)
    slots = jnp.concatenate([chip.astype(jnp.int32).reshape(1), others, core_idx])

    x0 = x[0]
    target = loss_target[0]

    wt_in_g, mt_in_g, vt_in_g = gla_w_in[0].T, m_gla_w_in[0].T, v_gla_w_in[0].T

    small_shard = _pack([gla_w_gate2[0], sgu_ln_gain[0], sgu_ln_bias[0]], rows_multiple=8, width=2 * LANES)
    own = [small_shard, jnp.pad(wt_in_g.astype(BF16), ((0, wp - ws), (0, 0)))]
    in_flight, token = _gather_start(own, name="gather_start_a")
    own_later = [(p[0] + token[0, 0]).astype(BF16) for p in (gla_w_out, sgu_w_in, sgu_w_out)]
    in_flight_later, token_later = _gather_start(own_later, name="gather_start_b", after=[token])
    own, in_flight = own + own_later, in_flight + in_flight_later

    def with_own(i, land):
        return lax.dynamic_update_slice(land, own[i][None], (chip, 0, 0))

    def arrived(i, after, name):
        land = _gather_wait(in_flight[i], after, name=name + "_wait")
        return with_own(i, _sibling_forward(land, name=name + "_share"))

    h0 = _norm_pre(x0, norm_pre[0:1] + token[0:1, 0:1] + token_later[0:1, 0:1], name="pre0")
    g_small = arrived(0, h0, "w_small")
    wt_g = arrived(1, [g_small, wt_in_g, mt_in_g, vt_in_g], "w_gla_in").reshape(N_CHIPS * wp, d)
    shard_shapes = [gla_w_gate2.shape[1:], sgu_ln_gain.shape[1:], sgu_ln_bias.shape[1:]]
    per_chip = [_unpack(g_small[j], shard_shapes) for j in range(N_CHIPS)]
    w2_full = jnp.concatenate([p[0] for p in per_chip], axis=1)
    ln_gain = jnp.concatenate([p[1] for p in per_chip], axis=0)[None, :]
    ln_bias = jnp.concatenate([p[2] for p in per_chip], axis=0)[None, :]
    w2p = jnp.pad(w2_full, ((0, LANES - GLA_GATE_RANK), (0, 0)))

    pos_chunk = jnp.arange(SGU_BLOCK) // CHUNK
    mask = pos_chunk[:, None] >= pos_chunk[None, :]
    ws_masked = jnp.where(mask[None], sgu_w_spatial[0], 0.0)
    ws_masked_t = ws_masked.transpose(0, 2, 1)
    bs_t = sgu_b_spatial[0].T

    proj0 = _matmul(h0, wt_g, mode="nt", out_dtype=F32, name="gla_in", tn=wp)
    o0, a0, s_before, s_final = _gla_fwd(proj0, w2p, gla_b_gate, gla_o_gain, lay, name="gla_scan")
    w_out_g = arrived(2, a0, "w_gla_out").reshape(d, d)
    y0 = _matmul(a0, w_out_g, mode="nn", out_dtype=F32, name="gla_out")
    x1, h1 = _post_then_pre(x0, y0, norm_post[0:1], norm_pre[1:2], name="post0_pre1")
    g_wi_s = arrived(3, h1, "w_sgu_in")
    proj1 = _matmul(h1, g_wi_s, mode="nn", out_dtype=F32, name="sgu_in", b_shards=True)
    a1 = _sgu_fwd(proj1, ln_gain, ln_bias, ws_masked, bs_t, name="sgu_gate")
    w_out_s = arrived(4, a1, "w_sgu_out").reshape(d, d)
    acts, tok = _to_sibling_start([(a1, 0), (a0, 0), (h1, 0), (h0, 1)], name="acts_to_sibling")
    a1, a0, h1, h0 = [f[4] for f in acts]
    y1 = _matmul(a1, w_out_s, mode="nn", out_dtype=F32, name="sgu_out", after=tok)
    loss_part, dx2, dy1, d_post1 = _loss_head(x1, y1, norm_post[1:2], target, name="loss_head")

    def behind(small, token):
        return small + token[0:1, 0:1]

    def pair_gradient(a_sent, b_sent, after, shards_on, name):
        a_me, a_sib = _from_sibling(a_sent, after, name=name + "_a_wait")
        b_me, b_sib = _from_sibling(b_sent, [a_sib] + list(after), name=name + "_b_wait")
        pair = _matmul_dw_pair(a_me, a_sib, b_me, b_sib, core_idx, shards_on=shards_on,
                               name=name + "_pair")
        return _scatter_start(pair, name=name + "_start")

    def reduced(flight, after, name):
        pair, landed = _scatter_wait(flight, after, name=name + "_wait")
        return _chip_sum(pair, landed, slots, name=name + "_sum")

    (dy1_sent,), tok = _to_sibling_start([(dy1, 1)], name="dy1_to_sibling")
    dy1 = dy1_sent[4]
    da1 = _matmul(dy1, w_out_s, mode="nt", out_dtype=F32, name="d_sgu_act", after=tok)
    fl_wo_s, tok = pair_gradient(acts[0], dy1_sent, [da1], "rows", "g_sgu_out")
    dproj1, d_ws, d_bs_t, d_lg, d_lb = _sgu_bwd(da1, proj1, ln_gain, behind(ln_bias, tok), ws_masked, ws_masked_t,
                                                bs_t, name="sgu_gate_bwd")
    (dp1_sent,), tok = _to_sibling_start([(dproj1, N_CHIPS)], name="dproj1_to_sibling")
    dproj1 = dp1_sent[4]
    dh1 = _matmul_nt_shards(dproj1, g_wi_s, out_dtype=F32, name="d_sgu_h", after=tok)
    fl_wi_s, tok = pair_gradient(acts[2], dp1_sent, [dh1], "cols", "g_sgu_in")
    dx1, dy0, d_pre1, d_post0 = _mid_bwd(dx2, dh1, x1, behind(norm_pre[1:2], tok), y0, norm_post[0:1],
                                         name="pre1_post0_bwd")
    (dy0_sent,), tok = _to_sibling_start([(dy0, 1)], name="dy0_to_sibling")
    dy0 = dy0_sent[4]
    da0 = _matmul(dy0, w_out_g, mode="nt", out_dtype=F32, name="d_gla_act", after=tok)
    fl_wo_g, tok = pair_gradient(acts[1], dy0_sent, [da0], "rows", "g_gla_out")
    dproj0, d_og, d_bg, d_w2p = _gla_bwd(da0, o0, proj0, w2p, behind(gla_b_gate, tok), gla_o_gain, s_before, s_final,
                                         lay, name="gla_scan_bwd")
    early_shapes = [norm_post.shape, gla_b_gate.shape, gla_o_gain.shape, sgu_w_spatial.shape, sgu_b_spatial.shape,
                    (1, GLA_GATE_RANK, dk), (1, d), (1, d), (1, LANES)]
    early_part = _pack([jnp.concatenate([d_post0, d_post1], axis=0), d_bg, d_og, jnp.where(mask[None], d_ws, 0.0)[None],
                        d_bs_t.T[None], d_w2p[:GLA_GATE_RANK][None], d_lg, d_lb, loss_part])
    early_flight, tok = _dev_gather_start(early_part, name="small_early_start")
    (dp0_sent,), tok_sent = _to_sibling_start([(dproj0, 0)], name="dproj0_to_sibling")
    dproj0 = dp0_sent[4]
    dh0 = _matmul(dproj0, wt_g, mode="nn", out_dtype=F32, name="d_gla_h", after=tok_sent)
    a_me, a_sib = _from_sibling(dp0_sent, [dh0, tok], name="g_gla_in_a_wait")
    b_me, b_sib = _from_sibling(acts[3], [a_sib, dh0], name="g_gla_in_b_wait")
    fl_wi_g, tok_scatter = [], None
    for p in range(2):
        pair = _matmul_dw_pair(a_me, a_sib, b_me, b_sib, core_idx, shards_on="rows", part=(p, 2),
                               name=f"g_gla_in_pair{p}", after=tok_scatter)
        flight, tok_scatter = _scatter_start(pair, name=f"g_gla_in_start{p}")
        fl_wi_g.append(flight)
    r_wo_s = reduced(fl_wo_s, tok_scatter, "g_sgu_out")
    r_wi_s = reduced(fl_wi_s, r_wo_s, "g_sgu_in")
    r_wo_g = reduced(fl_wo_g, r_wi_s, "g_gla_out")
    sharing, tok = _share_start([r_wo_s, r_wi_s, r_wo_g], name="grads_share_a")
    grad_x, d_pre0 = _first_bwd(dx1, dh0, x0, behind(norm_pre[0:1], tok), name="pre0_bwd")

    late_part = _pack([jnp.concatenate([d_pre0, d_pre1], axis=0)])
    late_flight, tok = _dev_gather_start(late_part, name="small_late_start")

    def big_update(w, g, m, v, name, after=None):
        return [u[None] for u in _adamw(w[0], g, m[0], v[0], name=name, after=after)]

    g_wo_sgu, g_wi_sgu, g_wo_gla = _share_wait(sharing, [grad_x, tok], name="grads_share_a_wait")
    u_wi_sgu = big_update(sgu_w_in, g_wi_sgu, m_sgu_w_in, v_sgu_w_in, "adamw_sgu_w_in")
    u_wo_gla = big_update(gla_w_out, g_wo_gla, m_gla_w_out, v_gla_w_out, "adamw_gla_w_out", after=u_wi_sgu[1])

    r_wi_g, behind_this = None, u_wo_gla[1]
    for p, flight in enumerate(fl_wi_g):
        pair, landed = _scatter_wait(flight, behind_this, name=f"g_gla_in_wait{p}")
        r_wi_g = behind_this = _chip_sum(pair, landed, slots, part=(p, 2), into=r_wi_g, name=f"g_gla_in_sum{p}")
    gt_wi_gla, = _sibling_share_halves([r_wi_g], name="grads_share_b")
    u_wi_gla_t = _adamw(wt_in_g, gt_wi_gla, mt_in_g, vt_in_g, name="adamw_gla_w_in")
    u_wi_gla = [u.T[None] for u in u_wi_gla_t]
    u_wo_sgu = big_update(sgu_w_out, g_wo_sgu, m_sgu_w_out, v_sgu_w_out, "adamw_sgu_w_out", after=u_wi_gla_t[1])

    def summed_over_devices(part, flight, after, shapes, name):
        land = _dev_gather_wait(flight, after, name=name + "_wait")
        every = lax.dynamic_update_slice(land, part[None], (2 * chip + core, 0, 0))
        return _unpack(_stack_sum(every, name=name + "_sum"), shapes)

    (g_post, g_bg, g_og, g_wsp, g_bsp, g_w2_full, g_lg_full, g_lb_full, loss_vec) = summed_over_devices(
        early_part, early_flight, u_wo_sgu[1], early_shapes, "small_early")
    g_pre, = summed_over_devices(late_part, late_flight, loss_vec, [norm_pre.shape], "small_late")
    loss = loss_vec[0, 0]
    g_w2 = lax.dynamic_slice_in_dim(g_w2_full, chip * (dk // N_CHIPS), dk // N_CHIPS, axis=2)
    g_lg = lax.dynamic_slice_in_dim(g_lg_full, chip * (d // N_CHIPS), d // N_CHIPS, axis=1)
    g_lb = lax.dynamic_slice_in_dim(g_lb_full, chip * (d // N_CHIPS), d // N_CHIPS, axis=1)

    small_w = [norm_pre, norm_post, gla_b_gate, gla_o_gain, sgu_w_spatial, sgu_b_spatial, gla_w_gate2, sgu_ln_gain,
               sgu_ln_bias]
    small_g = [g_pre, g_post, g_bg, g_og, g_wsp, g_bsp, g_w2, g_lg, g_lb]
    small_m = [m_norm_pre, m_norm_post, m_gla_b_gate, m_gla_o_gain, m_sgu_w_spatial, m_sgu_b_spatial, m_gla_w_gate2,
               m_sgu_ln_gain, m_sgu_ln_bias]
    small_v = [v_norm_pre, v_norm_post, v_gla_b_gate, v_gla_o_gain, v_sgu_w_spatial, v_sgu_b_spatial, v_gla_w_gate2,
               v_sgu_ln_gain, v_sgu_ln_bias]
    own_shapes = [w.shape for w in small_w]
    _, s_dl, s_m, s_v = _adamw(_pack(small_w), _pack(small_g), _pack(small_m), _pack(small_v), name="adamw_small")
    dl_s, m_s, v_s = _unpack(s_dl, own_shapes), _unpack(s_m, own_shapes), _unpack(s_v, own_shapes)

    def ordered(small, kind):
        pre, post, bg, og, wsp, bsp, w2, lg, lb = small
        return [pre, post, u_wi_gla[kind], w2, bg, og, u_wo_gla[kind], u_wi_sgu[kind], lg, lb, wsp, bsp, u_wo_sgu[kind]]

    return (loss, grad_x[None], *ordered(small_g, 0), *ordered(dl_s, 1), *ordered(m_s, 2), *ordered(v_s, 3))
```

```python
import math

import jax
import jax.numpy as jnp
from jax import lax
from jax.experimental import pallas as pl
from jax.experimental.pallas import tpu as pltpu

F32 = jnp.float32
BF16 = jnp.bfloat16
MESH = pl.DeviceIdType.MESH

EPS = 1e-6
CHUNK = 64
GLA_HEADS = 4
GLA_GATE_RANK = 16
GLA_TAU = 16.0
SGU_BLOCK = 128
SGU_GROUPS = 8
N_CHIPS = 4
N_DEV = 8
LANES = 128

ADAM_LR = 0.001
ADAM_B1 = 0.9
ADAM_B2 = 0.999
ADAM_EPS = 1e-08
ADAM_WD = 0.01
ADAM_STEP = 10

VMEM_LIMIT = 56 * 1024 * 1024


def _cparams(sem=None):
    return pltpu.CompilerParams(dimension_semantics=sem, vmem_limit_bytes=VMEM_LIMIT)


def _pick(n, cap, unit=LANES):
    best = None
    for t in range(unit, min(n, cap) + 1, unit):
        if n % t == 0:
            best = t
    assert best is not None, (n, cap, unit)
    return best


def _dot(a, b, dims):
    return lax.dot_general(a, b, (dims, ((), ())), preferred_element_type=F32)


def _dot_nn(a, b):
    return _dot(a, b, ((1,), (0,)))


def _dot_nt(a, b):
    return _dot(a, b, ((1,), (1,)))


def _dot_tn(a, b):
    return _dot(a, b, ((0,), (0,)))


def _matmul(a, b, *, mode, out_dtype, name, tm=1024, tn=512, b_shards=False, after=None):
    M, K = a.shape
    if b_shards:
        ns, Kb, bc = b.shape
        N, tn = ns * bc, _pick(bc, tn)
        per = bc // tn
        b_spec = pl.BlockSpec((None, K, tn), lambda i, j: (j // per, 0, j % per))
    elif mode == "nt":
        N, Kb = b.shape
        tn = _pick(N, tn)
        b_spec = pl.BlockSpec((tn, K), lambda i, j: (j, 0))
    else:
        Kb, N = b.shape
        tn = _pick(N, tn)
        b_spec = pl.BlockSpec((K, tn), lambda i, j: (0, j))
    assert K == Kb and a.dtype == b.dtype == BF16, (a.shape, b.shape, mode)
    tm = _pick(M, tm)
    dims = ((1,), (1,)) if mode == "nt" else ((1,), (0,))
    extra_specs, extra_args = ([], []) if after is None else ([pl.BlockSpec(memory_space=pl.ANY)], [after])

    def body(a_ref, b_ref, *rest):
        rest[-1][...] = _dot(a_ref[...], b_ref[...], dims).astype(out_dtype)

    return pl.pallas_call(
        body, name=name, grid=(M // tm, N // tn),
        in_specs=[pl.BlockSpec((tm, K), lambda i, j: (i, 0)), b_spec] + extra_specs,
        out_specs=pl.BlockSpec((tm, tn), lambda i, j: (i, j)), out_shape=jax.ShapeDtypeStruct((M, N), out_dtype),
        compiler_params=_cparams(("parallel", "parallel")),
    )(a, b, *extra_args)


def _matmul_nt_shards(a, b, *, out_dtype, name, tm=1024, tn=512, after=None):
    M, K = a.shape
    ns, N, kc = b.shape
    assert K == ns * kc
    tm, tn = _pick(M, tm), _pick(N, tn)

    def body(a_ref, *rest):
        b_refs, o_ref = rest[:ns], rest[ns + (after is not None)]
        acc = _dot_nt(a_ref[:, 0:kc], b_refs[0][...])
        for j in range(1, ns):
            acc += _dot_nt(a_ref[:, j * kc:(j + 1) * kc], b_refs[j][...])
        o_ref[...] = acc.astype(out_dtype)

    def shard(j):
        return pl.BlockSpec((None, tn, kc), lambda i, n: (j, n, 0))

    extra_specs, extra_args = ([], []) if after is None else ([pl.BlockSpec(memory_space=pl.ANY)], [after])
    return pl.pallas_call(
        body, name=name, grid=(M // tm, N // tn),
        in_specs=[pl.BlockSpec((tm, K), lambda i, n: (i, 0))] + [shard(j) for j in range(ns)] + extra_specs,
        out_specs=pl.BlockSpec((tm, tn), lambda i, n: (i, n)), out_shape=jax.ShapeDtypeStruct((M, N), out_dtype),
        compiler_params=_cparams(("parallel", "parallel")),
    )(a, *([b] * ns), *extra_args)


def _rstd(x):
    return lax.rsqrt(jnp.mean(x * x, axis=-1, keepdims=True) + EPS)


def _row_spec(tr, d):
    return pl.BlockSpec((tr, d), lambda i: (i, 0))


def _vec_spec(d):
    return pl.BlockSpec((1, d), lambda i: (0, 0))


def _acc_rows(ref, i, val, cols=slice(None)):
    @pl.when(i == 0)
    def _():
        ref[:, cols] = val

    @pl.when(i > 0)
    def _():
        ref[:, cols] += val


def _norm_pre(x, gain, *, name, tr=256):
    t, d = x.shape
    tr = _pick(t, tr, 8)

    def body(x_ref, g_ref, h_ref):
        xv = x_ref[...]
        h_ref[...] = (xv * _rstd(xv) * g_ref[...]).astype(BF16)

    return pl.pallas_call(
        body, name=name, grid=(t // tr,), in_specs=[_row_spec(tr, d), _vec_spec(d)], out_specs=_row_spec(tr, d),
        out_shape=jax.ShapeDtypeStruct((t, d), BF16), compiler_params=_cparams(("parallel",)),
    )(x, gain)


def _post_then_pre(x, y, post_gain, pre_gain, *, name, tr=256):
    t, d = x.shape
    tr = _pick(t, tr, 8)

    def body(x_ref, y_ref, pg_ref, ng_ref, xn_ref, h_ref):
        yv = y_ref[...]
        xn = x_ref[...] + yv * _rstd(yv) * pg_ref[...]
        xn_ref[...] = xn
        h_ref[...] = (xn * _rstd(xn) * ng_ref[...]).astype(BF16)

    return pl.pallas_call(
        body, name=name, grid=(t // tr,),
        in_specs=[_row_spec(tr, d), _row_spec(tr, d), _vec_spec(d), _vec_spec(d)],
        out_specs=[_row_spec(tr, d), _row_spec(tr, d)],
        out_shape=[jax.ShapeDtypeStruct((t, d), F32), jax.ShapeDtypeStruct((t, d), BF16)],
        compiler_params=_cparams(("parallel",)),
    )(x, y, post_gain, pre_gain)


def _norm_bwd(dy, n, r, gain):
    dn = dy * gain
    return r * (dn - n * jnp.mean(dn * n, axis=-1, keepdims=True))


def _loss_head(x, y, post_gain, target, *, name, tr=256):
    t, d = x.shape
    tr = _pick(t, tr, 8)

    def body(x_ref, y_ref, pg_ref, t_ref, loss_ref, dx_ref, dy_ref, dpg_ref):
        i = pl.program_id(0)
        yv = y_ref[...]
        r = _rstd(yv)
        n = yv * r
        err = x_ref[...] + n * pg_ref[...] - t_ref[...]
        dx = err * (1.0 / d)
        dx_ref[...] = dx
        part = 0.5 * jnp.sum(jnp.mean(err * err, axis=-1, keepdims=True), axis=0, keepdims=True)
        _acc_rows(loss_ref, i, jnp.broadcast_to(part, (1, LANES)))
        _acc_rows(dpg_ref, i, jnp.sum(dx * n, axis=0, keepdims=True))
        dy_ref[...] = _norm_bwd(dx, n, r, pg_ref[...]).astype(BF16)

    return pl.pallas_call(
        body, name=name, grid=(t // tr,),
        in_specs=[_row_spec(tr, d), _row_spec(tr, d), _vec_spec(d), _row_spec(tr, d)],
        out_specs=[_vec_spec(LANES), _row_spec(tr, d), _row_spec(tr, d), _vec_spec(d)],
        out_shape=[jax.ShapeDtypeStruct((1, LANES), F32), jax.ShapeDtypeStruct((t, d), F32),
                   jax.ShapeDtypeStruct((t, d), BF16), jax.ShapeDtypeStruct((1, d), F32)],
        compiler_params=_cparams(("arbitrary",)),
    )(x, y, post_gain, target)


def _mid_bwd(dx_out, dh, x, pre_gain, y_prev, post_gain_prev, *, name, tr=256):
    t, d = x.shape
    tr = _pick(t, tr, 8)

    def body(dxo_ref, dh_ref, x_ref, ng_ref, y_ref, pg_ref, dx_ref, dy_ref, dng_ref, dpg_ref):
        i = pl.program_id(0)
        xv = x_ref[...]
        r = _rstd(xv)
        xh = xv * r
        dhv = dh_ref[...]
        _acc_rows(dng_ref, i, jnp.sum(dhv * xh, axis=0, keepdims=True))
        dx = dxo_ref[...] + _norm_bwd(dhv, xh, r, ng_ref[...])
        dx_ref[...] = dx
        yv = y_ref[...]
        ry = _rstd(yv)
        n = yv * ry
        _acc_rows(dpg_ref, i, jnp.sum(dx * n, axis=0, keepdims=True))
        dy_ref[...] = _norm_bwd(dx, n, ry, pg_ref[...]).astype(BF16)

    return pl.pallas_call(
        body, name=name, grid=(t // tr,),
        in_specs=[_row_spec(tr, d), _row_spec(tr, d), _row_spec(tr, d), _vec_spec(d), _row_spec(tr, d), _vec_spec(d)],
        out_specs=[_row_spec(tr, d), _row_spec(tr, d), _vec_spec(d), _vec_spec(d)],
        out_shape=[jax.ShapeDtypeStruct((t, d), F32), jax.ShapeDtypeStruct((t, d), BF16),
                   jax.ShapeDtypeStruct((1, d), F32), jax.ShapeDtypeStruct((1, d), F32)],
        compiler_params=_cparams(("arbitrary",)),
    )(dx_out, dh, x, pre_gain, y_prev, post_gain_prev)


def _first_bwd(dx_out, dh, x, pre_gain, *, name, tr=256):
    t, d = x.shape
    tr = _pick(t, tr, 8)

    def body(dxo_ref, dh_ref, x_ref, ng_ref, dx_ref, dng_ref):
        i = pl.program_id(0)
        xv = x_ref[...]
        r = _rstd(xv)
        xh = xv * r
        dhv = dh_ref[...]
        _acc_rows(dng_ref, i, jnp.sum(dhv * xh, axis=0, keepdims=True))
        dx_ref[...] = dxo_ref[...] + _norm_bwd(dhv, xh, r, ng_ref[...])

    return pl.pallas_call(
        body, name=name, grid=(t // tr,),
        in_specs=[_row_spec(tr, d), _row_spec(tr, d), _row_spec(tr, d), _vec_spec(d)],
        out_specs=[_row_spec(tr, d), _vec_spec(d)],
        out_shape=[jax.ShapeDtypeStruct((t, d), F32), jax.ShapeDtypeStruct((1, d), F32)],
        compiler_params=_cparams(("arbitrary",)),
    )(dx_out, dh, x, pre_gain)


def _sigmoid(x):
    return 1.0 / (1.0 + jnp.exp(-x))


def _log_sigmoid(x):
    return jnp.minimum(x, 0.0) - jnp.log(1.0 + jnp.exp(-jnp.abs(x)))


_GELU_C = math.sqrt(2.0 / math.pi)


_GELU_A = 0.044715


def _gelu_parts(x, with_grad=True):
    x2 = x * x
    h = 0.5 * jnp.tanh(x * (_GELU_C + (_GELU_C * _GELU_A) * x2)) + 0.5
    val = x * h
    if not with_grad:
        return val, None
    return val, h * (1.0 + (1.0 - h) * (x * (2.0 * _GELU_C + (6.0 * _GELU_C * _GELU_A) * x2)))


def _split3(x):
    hi = x.astype(BF16)
    r1 = x - hi.astype(F32)
    mid = r1.astype(BF16)
    lo = (r1 - mid.astype(F32)).astype(BF16)
    return hi, mid, lo


def _tri_matmul(tri_bf16, x):
    hi, mid, lo = _split3(x)
    return _dot_nn(tri_bf16, hi) + _dot_nn(tri_bf16, mid) + _dot_nn(tri_bf16, lo)


def _gla_dims(d):
    dk, dv = d // 2, d
    return dk, dv, dk // GLA_HEADS, dv // GLA_HEADS


def _col_pieces(a, b, lay):
    ws, wp = lay
    out = []
    while a < b:
        j = a // ws
        end = min(b, (j + 1) * ws)
        out.append((j * wp + a - j * ws, end - a))
        a = end
    return out


def _load_cols(ref, a, b, lay):
    parts = [ref[:, s:s + n] for s, n in _col_pieces(a, b, lay)]
    return parts[0] if len(parts) == 1 else jnp.concatenate(parts, axis=1)


def _store_cols(ref, a, val, lay):
    off = 0
    for s, n in _col_pieces(a, a + val.shape[1], lay):
        ref[:, s:s + n] = val[:, off:off + n]
        off += n


def _gate_window(c_r, lay):
    (start, _), = _col_pieces(c_r, c_r + GLA_GATE_RANK, lay)
    assert (start % lay[1]) + LANES <= lay[1]
    return slice(start, start + LANES)


def _gla_gates(glr, k, w2_ref, b_ref):
    z = _dot_nn(glr.astype(BF16), w2_ref[...].astype(BF16)) + b_ref[...]
    la = _log_sigmoid(z) * (1.0 / GLA_TAU)
    row = lax.broadcasted_iota(jnp.int32, (CHUNK, CHUNK), 0)
    col = lax.broadcasted_iota(jnp.int32, (CHUNK, CHUNK), 1)
    incl = (row >= col).astype(BF16)
    bcum = _tri_matmul(incl, la)
    b_end = bcum[CHUNK - 1:CHUNK, :]
    e_rest = jnp.exp(b_end - bcum)
    return z, e_rest, k * e_rest, jnp.exp(b_end)


def _gla_fwd(proj, w2p, b_gate, o_gain, lay, *, name):
    t, wcols = proj.shape
    d = o_gain.shape[1]
    dk, dv, dkh, dvh = _gla_dims(d)
    nc = t // CHUNK
    c_k, c_v, c_g, c_r = dk, 2 * dk, 2 * dk + dv, 2 * dk + 2 * dv
    scale = dkh ** -0.5

    def body(p_ref, w2_ref, b_ref, og_ref, o_ref, a_ref, sb_ref, sfin_ref, s_ref):
        i = pl.program_id(0)

        @pl.when(i == 0)
        def _():
            s_ref[...] = jnp.zeros_like(s_ref)

        q = _load_cols(p_ref, 0, dk, lay) * scale
        k = _load_cols(p_ref, c_k, c_k + dk, lay)
        glr = p_ref[:, _gate_window(c_r, lay)]
        _, _, kdec, decay = _gla_gates(glr, k, w2_ref, b_ref)
        for h in range(GLA_HEADS):
            ks = slice(h * dkh, (h + 1) * dkh)
            vs = slice(h * dvh, (h + 1) * dvh)
            v_h = _load_cols(p_ref, c_v + h * dvh, c_v + (h + 1) * dvh, lay)
            g_h = _load_cols(p_ref, c_g + h * dvh, c_g + (h + 1) * dvh, lay)
            s_old = s_ref[h]
            sb_ref[0, h] = s_old
            s_new = s_old * decay[:, ks] + _dot_tn(v_h.astype(BF16), kdec[:, ks].astype(BF16))
            s_ref[h] = s_new
            o_h = _dot_nt(q[:, ks].astype(BF16), s_new.astype(BF16))
            o_ref[:, vs] = o_h
            on = o_h * _rstd(o_h)
            a_ref[:, vs] = (on * og_ref[:, vs] * (g_h * _sigmoid(g_h))).astype(BF16)

        @pl.when(i == nc - 1)
        def _():
            sfin_ref[...] = s_ref[...]

    full = lambda *shape: pl.BlockSpec(shape, lambda i: (0,) * len(shape))
    return pl.pallas_call(
        body, name=name, grid=(nc,),
        in_specs=[pl.BlockSpec((CHUNK, wcols), lambda i: (i, 0)), full(LANES, dk), full(1, dk), full(1, dv)],
        out_specs=[pl.BlockSpec((CHUNK, dv), lambda i: (i, 0)), pl.BlockSpec((CHUNK, dv), lambda i: (i, 0)),
                   pl.BlockSpec((1, GLA_HEADS, dvh, dkh), lambda i: (i, 0, 0, 0)), full(GLA_HEADS, dvh, dkh)],
        out_shape=[jax.ShapeDtypeStruct((t, dv), F32), jax.ShapeDtypeStruct((t, dv), BF16),
                   jax.ShapeDtypeStruct((nc, GLA_HEADS, dvh, dkh), F32),
                   jax.ShapeDtypeStruct((GLA_HEADS, dvh, dkh), F32)],
        scratch_shapes=[pltpu.VMEM((GLA_HEADS, dvh, dkh), F32)],
        compiler_params=_cparams(("arbitrary",)),
    )(proj, w2p, b_gate, o_gain)


def _gla_bwd(da, o, proj, w2p, b_gate, o_gain, s_before, s_final, lay, *, name):
    t, wcols = proj.shape
    d = o_gain.shape[1]
    dk, dv, dkh, dvh = _gla_dims(d)
    nc = t // CHUNK
    c_k, c_v, c_g, c_r = dk, 2 * dk, 2 * dk + dv, 2 * dk + 2 * dv
    scale = dkh ** -0.5

    def body(da_ref, o_ref, p_ref, w2_ref, b_ref, og_ref, sb_ref, sfin_ref,
             dp_ref, dog_ref, db_ref, dw2_ref, s_ref, gc_ref, dkd_ref):
        i = pl.program_id(0)

        @pl.when(i == 0)
        def _():
            s_ref[...] = sfin_ref[...]
            gc_ref[...] = jnp.zeros_like(gc_ref)

        ws, wp = lay
        for j in range(N_CHIPS):
            dp_ref[:, j * wp + ws:(j + 1) * wp] = jnp.zeros((CHUNK, wp - ws), BF16)
        q = _load_cols(p_ref, 0, dk, lay) * scale
        k = _load_cols(p_ref, c_k, c_k + dk, lay)
        glr = p_ref[:, _gate_window(c_r, lay)]
        z, e_rest, kdec, decay = _gla_gates(glr, k, w2_ref, b_ref)
        ddecay = []
        for h in range(GLA_HEADS):
            ks = slice(h * dkh, (h + 1) * dkh)
            vs = slice(h * dvh, (h + 1) * dvh)
            v_h = _load_cols(p_ref, c_v + h * dvh, c_v + (h + 1) * dvh, lay)
            g_h = _load_cols(p_ref, c_g + h * dvh, c_g + (h + 1) * dvh, lay)
            da_h = da_ref[:, vs]
            o_h = o_ref[:, vs]
            og_h = og_ref[:, vs]
            r = _rstd(o_h)
            on = o_h * r
            sg = _sigmoid(g_h)
            silu = g_h * sg
            _acc_rows(dog_ref, i, jnp.sum(da_h * silu * on, axis=0, keepdims=True), vs)
            _store_cols(dp_ref, c_g + h * dvh, (da_h * (on * og_h) * (sg * (1.0 + g_h * (1.0 - sg)))).astype(BF16),
                        lay)
            don = da_h * silu * og_h
            do_h = (r * (don - on * jnp.mean(don * on, axis=-1, keepdims=True))).astype(BF16)
            s_cur = s_ref[h]
            _store_cols(dp_ref, h * dkh, (_dot_nn(do_h, s_cur.astype(BF16)) * scale).astype(BF16), lay)
            g_tot = gc_ref[h] + _dot_tn(do_h, q[:, ks].astype(BF16))
            g_bf = g_tot.astype(BF16)
            dkd_ref[:, ks] = _dot_nn(v_h.astype(BF16), g_bf)
            _store_cols(dp_ref, c_v + h * dvh, _dot_nt(kdec[:, ks].astype(BF16), g_bf).astype(BF16), lay)
            s_prev = sb_ref[0, h]
            ddecay.append(jnp.sum(g_tot * s_prev, axis=0, keepdims=True))
            gc_ref[h] = g_tot * decay[:, ks]
            s_ref[h] = s_prev
        dkdec = dkd_ref[...]
        _store_cols(dp_ref, c_k, (dkdec * e_rest).astype(BF16), lay)
        d_e = dkdec * kdec
        row = lax.broadcasted_iota(jnp.int32, (CHUNK, CHUNK), 0)
        col = lax.broadcasted_iota(jnp.int32, (CHUNK, CHUNK), 1)
        excl = (row > col).astype(BF16)
        dla = jnp.concatenate(ddecay, axis=1) * decay + _tri_matmul(excl, d_e)
        dz = dla * (1.0 / GLA_TAU) * (1.0 - _sigmoid(z))
        _acc_rows(db_ref, i, jnp.sum(dz, axis=0, keepdims=True))
        dz_bf = dz.astype(BF16)
        dw2 = _dot_tn(glr.astype(BF16), dz_bf)

        @pl.when(i == 0)
        def _():
            dw2_ref[...] = dw2

        @pl.when(i > 0)
        def _():
            dw2_ref[...] += dw2

        dp_ref[:, _gate_window(c_r, lay)] = _dot_nt(dz_bf, w2_ref[...].astype(BF16)).astype(BF16)

    rev = lambda i: (nc - 1 - i, 0)
    full = lambda *shape: pl.BlockSpec(shape, lambda i: (0,) * len(shape))
    return pl.pallas_call(
        body, name=name, grid=(nc,),
        in_specs=[pl.BlockSpec((CHUNK, dv), rev), pl.BlockSpec((CHUNK, dv), rev), pl.BlockSpec((CHUNK, wcols), rev),
                  full(LANES, dk), full(1, dk), full(1, dv),
                  pl.BlockSpec((1, GLA_HEADS, dvh, dkh), lambda i: (nc - 1 - i, 0, 0, 0)), full(GLA_HEADS, dvh, dkh)],
        out_specs=[pl.BlockSpec((CHUNK, wcols), rev), full(1, dv), full(1, dk), full(LANES, dk)],
        out_shape=[jax.ShapeDtypeStruct((t, wcols), BF16), jax.ShapeDtypeStruct((1, dv), F32),
                   jax.ShapeDtypeStruct((1, dk), F32), jax.ShapeDtypeStruct((LANES, dk), F32)],
        scratch_shapes=[pltpu.VMEM((GLA_HEADS, dvh, dkh), F32), pltpu.VMEM((GLA_HEADS, dvh, dkh), F32),
                        pltpu.VMEM((CHUNK, dk), F32)],
        compiler_params=_cparams(("arbitrary",)),
    )(da, o, proj, w2p, b_gate, o_gain, s_before, s_final)


def _sgu_mid(p_ref, lg_ref, lb_ref, ws_ref, bst_ref, w, with_grad=True):
    gd = w // SGU_GROUPS
    u_act, du_fac = _gelu_parts(p_ref[:, 0:w], with_grad)
    vf, dv_fac = _gelu_parts(p_ref[:, w:2 * w], with_grad)
    mu = jnp.mean(vf, axis=-1, keepdims=True)
    cen = vf - mu
    rstd = lax.rsqrt(jnp.mean(cen * cen, axis=-1, keepdims=True) + EPS)
    xh = cen * rstd
    vn = (xh * lg_ref[...] + lb_ref[...]).astype(BF16)
    vs = [_dot_nn(ws_ref[g].astype(BF16), vn[:, g * gd:(g + 1) * gd]) + bst_ref[:, g:g + 1]
          for g in range(SGU_GROUPS)]
    return u_act, du_fac, dv_fac, rstd, xh, vn, vs


def _sgu_fwd(proj, ln_gain, ln_bias, ws_masked, bs_t, *, name):
    t, w3 = proj.shape
    w = w3 // 3
    gd = w // SGU_GROUPS
    nb = t // SGU_BLOCK

    def body(p_ref, lg_ref, lb_ref, ws_ref, bst_ref, a_ref):
        u_act, _, _, _, _, _, vs = _sgu_mid(p_ref, lg_ref, lb_ref, ws_ref, bst_ref, w, with_grad=False)
        for g in range(SGU_GROUPS):
            cs = slice(g * gd, (g + 1) * gd)
            gate = p_ref[:, 2 * w + g * gd:2 * w + (g + 1) * gd]
            a_ref[:, cs] = (u_act[:, cs] * vs[g] * (gate * _sigmoid(gate))).astype(BF16)

    full = lambda *shape: pl.BlockSpec(shape, lambda i: (0,) * len(shape))
    return pl.pallas_call(
        body, name=name, grid=(nb,),
        in_specs=[pl.BlockSpec((SGU_BLOCK, w3), lambda i: (i, 0)), full(1, w), full(1, w),
                  full(SGU_GROUPS, SGU_BLOCK, SGU_BLOCK), full(SGU_BLOCK, SGU_GROUPS)],
        out_specs=pl.BlockSpec((SGU_BLOCK, w), lambda i: (i, 0)),
        out_shape=jax.ShapeDtypeStruct((t, w), BF16),
        compiler_params=_cparams(("parallel",)),
    )(proj, ln_gain, ln_bias, ws_masked, bs_t)


def _sgu_bwd(da, proj, ln_gain, ln_bias, ws_masked, ws_masked_t, bs_t, *, name):
    t, w3 = proj.shape
    w = w3 // 3
    gd = w // SGU_GROUPS
    nb = t // SGU_BLOCK

    def body(da_ref, p_ref, lg_ref, lb_ref, ws_ref, wst_ref, bst_ref, dp_ref, dws_ref, dbst_ref, dlg_ref, dlb_ref,
             dvn_ref):
        i = pl.program_id(0)
        u_act, du_fac, dv_fac, rstd, xh, vn, vs = _sgu_mid(p_ref, lg_ref, lb_ref, ws_ref, bst_ref, w)
        for g in range(SGU_GROUPS):
            cs = slice(g * gd, (g + 1) * gd)
            gate = p_ref[:, 2 * w + g * gd:2 * w + (g + 1) * gd]
            sg = _sigmoid(gate)
            silu = gate * sg
            da_g = da_ref[:, cs]
            ua_g = u_act[:, cs]
            dp_ref[:, cs] = (da_g * vs[g] * silu * du_fac[:, cs]).astype(BF16)
            dp_ref[:, 2 * w + g * gd:2 * w + (g + 1) * gd] = (
                da_g * ua_g * vs[g] * (sg * (1.0 + gate * (1.0 - sg)))).astype(BF16)
            dvs = da_g * ua_g * silu
            dvs_bf = dvs.astype(BF16)
            dvn_ref[:, cs] = _dot_nn(wst_ref[g].astype(BF16), dvs_bf)
            dws = _dot_nt(dvs_bf, vn[:, cs])
            dbs = jnp.sum(dvs, axis=1, keepdims=True)

            @pl.when(i == 0)
            def _():
                dws_ref[g] = dws
                dbst_ref[:, g:g + 1] = dbs

            @pl.when(i > 0)
            def _():
                dws_ref[g] += dws
                dbst_ref[:, g:g + 1] += dbs

        dvn = dvn_ref[...]
        _acc_rows(dlg_ref, i, jnp.sum(dvn * xh, axis=0, keepdims=True))
        _acc_rows(dlb_ref, i, jnp.sum(dvn, axis=0, keepdims=True))
        dxh = dvn * lg_ref[...]
        dvf = rstd * (dxh - jnp.mean(dxh, axis=-1, keepdims=True)
                      - xh * jnp.mean(dxh * xh, axis=-1, keepdims=True))
        dp_ref[:, w:2 * w] = (dvf * dv_fac).astype(BF16)

    full = lambda *shape: pl.BlockSpec(shape, lambda i: (0,) * len(shape))
    return pl.pallas_call(
        body, name=name, grid=(nb,),
        in_specs=[pl.BlockSpec((SGU_BLOCK, w), lambda i: (i, 0)), pl.BlockSpec((SGU_BLOCK, w3), lambda i: (i, 0)),
                  full(1, w), full(1, w), full(SGU_GROUPS, SGU_BLOCK, SGU_BLOCK),
                  full(SGU_GROUPS, SGU_BLOCK, SGU_BLOCK), full(SGU_BLOCK, SGU_GROUPS)],
        out_specs=[pl.BlockSpec((SGU_BLOCK, w3), lambda i: (i, 0)), full(SGU_GROUPS, SGU_BLOCK, SGU_BLOCK),
                   full(SGU_BLOCK, SGU_GROUPS), full(1, w), full(1, w)],
        out_shape=[jax.ShapeDtypeStruct((t, w3), BF16), jax.ShapeDtypeStruct((SGU_GROUPS, SGU_BLOCK, SGU_BLOCK), F32),
                   jax.ShapeDtypeStruct((SGU_BLOCK, SGU_GROUPS), F32), jax.ShapeDtypeStruct((1, w), F32),
                   jax.ShapeDtypeStruct((1, w), F32)],
        scratch_shapes=[pltpu.VMEM((SGU_BLOCK, w), F32)],
        compiler_params=_cparams(("arbitrary",)),
    )(da, proj, ln_gain, ln_bias, ws_masked, ws_masked_t, bs_t)


def _tile2d(rows, cols, block_bytes, row_unit):
    if rows % row_unit == 0:
        return _pick(rows, max(row_unit, block_bytes // (4 * cols)), row_unit), cols
    return rows, _pick(cols, max(LANES, block_bytes // (4 * rows)))


def _adamw(w, g, m, v, *, name, block_bytes=1 << 20, after=None):
    rows, cols = w.shape
    tr, tc = _tile2d(rows, cols, block_bytes, 8)
    g_rows = g.shape[0]
    assert g_rows == rows or tr == rows
    extra_specs, extra_args = ([], []) if after is None else ([pl.BlockSpec(memory_space=pl.ANY)], [after])

    def body(w_ref, g_ref, m_ref, v_ref, *rest):
        go_ref, d_ref, mo_ref, vo_ref = rest[len(extra_args):]
        gv = g_ref[0:tr, :]
        go_ref[...] = gv
        mn = ADAM_B1 * m_ref[...] + (1.0 - ADAM_B1) * gv
        vn = ADAM_B2 * v_ref[...] + (1.0 - ADAM_B2) * (gv * gv)
        m_hat = mn / (1.0 - ADAM_B1 ** ADAM_STEP)
        v_hat = vn / (1.0 - ADAM_B2 ** ADAM_STEP)
        d_ref[...] = -ADAM_LR * (m_hat / (jnp.sqrt(v_hat) + ADAM_EPS) + ADAM_WD * w_ref[...])
        mo_ref[...] = mn
        vo_ref[...] = vn

    spec = pl.BlockSpec((tr, tc), lambda i, j: (i, j))
    g_spec = spec if g_rows == rows else pl.BlockSpec((g_rows, tc), lambda i, j: (0, j))
    return pl.pallas_call(
        body, name=name, grid=(rows // tr, cols // tc), in_specs=[spec, g_spec, spec, spec] + extra_specs,
        out_specs=[spec] * 4, out_shape=[jax.ShapeDtypeStruct((rows, cols), F32)] * 4,
        compiler_params=_cparams(("parallel", "parallel")),
    )(w, g, m, v, *extra_args)


def _matmul_dw_pair(a_me, a_sib, b_me, b_sib, core_idx, *, shards_on, name, after=None, part=(0, 1)):
    T, M = a_me.shape
    N = b_me.shape[1]
    if shards_on == "rows":
        p, count = part
        tm, hc = M // N_CHIPS, N // 2
        hp = hc // count
        tn = _pick(hp, 512)
        per = hp // tn
        grid = (N_CHIPS, per)
        a_spec = pl.BlockSpec((T, tm), lambda i, n, h: (0, i))
        b_me_spec = pl.BlockSpec((T, tn), lambda i, n, h: (0, (h[0] * count + p) * per + n))
        b_sib_spec = pl.BlockSpec((T, tn), lambda i, n, h: (0, p * per + n))
        out_spec = pl.BlockSpec((None, tm, tn), lambda i, n, h: (i, 0, n))
        out_shape = jax.ShapeDtypeStruct((N_CHIPS, tm, hp), BF16)
    else:
        tm, hc = _pick(M, 1024), N // N_CHIPS // 2
        grid = (M // tm, N_CHIPS)
        a_spec = pl.BlockSpec((T, tm), lambda i, j, h: (0, i))
        b_me_spec = pl.BlockSpec((T, hc), lambda i, j, h: (0, 2 * j + h[0]))
        b_sib_spec = pl.BlockSpec((T, hc), lambda i, j, h: (0, j))
        out_spec = pl.BlockSpec((None, tm, hc), lambda i, j, h: (j, i, 0))
        out_shape = jax.ShapeDtypeStruct((N_CHIPS, M, hc), BF16)
    extra_specs, extra_args = ([], []) if after is None else ([pl.BlockSpec(memory_space=pl.ANY)], [after])

    def body(h_ref, am_ref, as_ref, bm_ref, bs_ref, *rest):
        o_ref = rest[len(extra_args)]
        o_ref[...] = (_dot_tn(am_ref[...], bm_ref[...]) + _dot_tn(as_ref[...], bs_ref[...])).astype(BF16)

    grid_spec = pltpu.PrefetchScalarGridSpec(
        num_scalar_prefetch=1, grid=grid, in_specs=[a_spec, a_spec, b_me_spec, b_sib_spec] + extra_specs,
        out_specs=out_spec)
    return pl.pallas_call(
        body, name=name, grid_spec=grid_spec, out_shape=out_shape, compiler_params=_cparams(("parallel", "parallel")),
    )(core_idx, a_me, a_sib, b_me, b_sib, *extra_args)


def _chip_sum(pair, landed, slots, *, name, block_bytes=1 << 20, part=(0, 1), into=None):
    p, count = part
    _, r, hp = pair.shape
    tr, tc = _tile2d(r, hp, block_bytes, 16)
    ncb = hp // tc
    extra_specs, extra_args = ([], []) if into is None else ([pl.BlockSpec(memory_space=pl.ANY)], [into])

    def body(s_ref, own_ref, l0_ref, l1_ref, l2_ref, *rest):
        rest[-1][...] = ((own_ref[...].astype(F32) + l0_ref[...].astype(F32)) + l1_ref[...].astype(F32)
                         ) + l2_ref[...].astype(F32)

    def slab(which):
        return pl.BlockSpec((None, tr, tc), lambda i, k, s: (s[which], i, k))

    grid_spec = pltpu.PrefetchScalarGridSpec(
        num_scalar_prefetch=1, grid=(r // tr, ncb),
        in_specs=[slab(0), slab(1), slab(2), slab(3)] + extra_specs,
        out_specs=pl.BlockSpec((tr, tc), lambda i, k, s: (i, (s[4] * count + p) * ncb + k)))
    return pl.pallas_call(
        body, name=name, grid_spec=grid_spec, out_shape=jax.ShapeDtypeStruct((r, 2 * hp * count), F32),
        input_output_aliases={} if into is None else {5: 0},
        compiler_params=_cparams(("parallel", "parallel")),
    )(slots, pair, landed, landed, landed, *extra_args)


def _stack_sum(x, *, name, out_dtype=F32, block_bytes=1 << 20):
    s, r, c = x.shape
    tr = _pick(r, max(8, block_bytes // (4 * c)), 16) if r % 16 == 0 else r

    def body(x_ref, o_ref):
        acc = x_ref[0].astype(F32)
        for j in range(1, s):
            acc = acc + x_ref[j].astype(F32)
        o_ref[...] = acc.astype(out_dtype)

    return pl.pallas_call(
        body, name=name, grid=(r // tr,),
        in_specs=[pl.BlockSpec((s, tr, c), lambda i: (0, i, 0))], out_specs=pl.BlockSpec((tr, c), lambda i: (i, 0)),
        out_shape=jax.ShapeDtypeStruct((r, c), out_dtype), compiler_params=_cparams(("parallel",)),
    )(x)


HBM = pl.BlockSpec(memory_space=pltpu.HBM)


def _place():
    x, y, c = lax.axis_index("x"), lax.axis_index("y"), lax.axis_index("c")
    other_chips = [(1 - x, y), (x, 1 - y), (1 - x, 1 - y)]
    return x, y, c, other_chips


def _half_cols(cols, which):
    hc = cols // 2
    return pl.ds(pl.multiple_of(which * hc, LANES), hc)


SEM = pl.BlockSpec(memory_space=pltpu.SEMAPHORE)
ANY = pl.BlockSpec(memory_space=pl.ANY)
SIDE_EFFECT = pltpu.SideEffectType.DATAFLOW_SIDE_EFFECTING
TOKEN_SHAPE = (8, LANES)


def _hbm(shape, dtype):
    return pltpu.HBM(shape, dtype)


def _in_hbm(a):
    return pltpu.with_memory_space_constraint(a, pltpu.HBM)


def _gather_copy(src_ref, land_ref, ssem, rsem, k, chip_of_block, to, c):
    cols = src_ref.shape[1]
    return pltpu.make_async_remote_copy(
        src_ref=src_ref.at[:, _half_cols(cols, c)], dst_ref=land_ref.at[chip_of_block, :, _half_cols(cols, c)],
        send_sem=ssem.at[k], recv_sem=rsem.at[k], device_id=to, device_id_type=MESH)


NEIGHBOURS = (0, 1)
ALL_CHIPS = (0, 1, 2)


def _gather_start(shards, *, name, after=(), relayed=()):
    n = len(shards)
    after = list(after)

    def body(*refs):
        srcs, lands = refs[:n], refs[n:2 * n]
        outs = refs[2 * n + len(after):]
        token = outs[-1]
        x, y, c, chips = _place()
        me = 2 * x + y
        for a in range(n):
            ssem, rsem = outs[4 * a], outs[4 * a + 1]
            for k in NEIGHBOURS if a in relayed else ALL_CHIPS:
                cx, cy = chips[k]
                _gather_copy(srcs[a], lands[a], ssem, rsem, k, me, (cx, cy, c), c).start()
        token[...] = jnp.zeros_like(token)

    out_shape, out_specs, aliases = [], [], {}
    for a, s in enumerate(shards):
        out_shape += [pltpu.SemaphoreType.DMA((3,)), pltpu.SemaphoreType.DMA((3,)), _hbm(s.shape, s.dtype),
                      _hbm((N_CHIPS,) + s.shape, s.dtype)]
        out_specs += [SEM, SEM, HBM, HBM]
        aliases[a] = 4 * a + 2
        aliases[n + a] = 4 * a + 3
    out_shape.append(jax.ShapeDtypeStruct(TOKEN_SHAPE, F32))
    out_specs.append(pl.BlockSpec(memory_space=pltpu.VMEM))
    lands = [_in_hbm(lax.empty((N_CHIPS,) + s.shape, s.dtype)) for s in shards]
    res = pl.pallas_call(
        body, name=name, in_specs=[HBM] * (2 * n) + [ANY] * len(after), out_specs=out_specs, out_shape=out_shape,
        input_output_aliases=aliases, compiler_params=pltpu.CompilerParams(has_side_effects=SIDE_EFFECT),
    )(*[_in_hbm(s) for s in shards], *lands, *after)
    return [tuple(res[4 * a:4 * a + 4]) for a in range(n)], res[-1]


def _wait_call(wait_fn, parts, after, *, name):
    ssem, rsem, src, land = parts
    after = list(after) if isinstance(after, (list, tuple)) else [after]

    def body(src_ref, land_ref, ssem_ref, rsem_ref, *rest):
        wait_fn(src_ref, land_ref, ssem_ref, rsem_ref)

    return pl.pallas_call(
        body, name=name, in_specs=[HBM, HBM, SEM, SEM] + [ANY] * len(after), out_specs=[HBM, HBM],
        out_shape=[_hbm(src.shape, src.dtype), _hbm(land.shape, land.dtype)], input_output_aliases={0: 0, 1: 1},
        compiler_params=pltpu.CompilerParams(has_side_effects=SIDE_EFFECT),
    )(src, land, ssem, rsem, *after)


def _gather_wait(parts, after, *, name, ks=ALL_CHIPS):
    def wait(src_ref, land_ref, ssem_ref, rsem_ref):
        x, y, c, chips = _place()
        for k in ks:
            cx, cy = chips[k]
            cp = _gather_copy(src_ref, land_ref, ssem_ref, rsem_ref, k, 2 * cx + cy, (x, y, c), c)
            cp.wait_send()
            cp.wait_recv()

    return _wait_call(wait, parts, after, name=name)


def _relay_copy(buf_ref, ssem, rsem, k, slab, to, c):
    hr = buf_ref.shape[1] // 2
    part = buf_ref.at[slab, pl.ds(k * hr, hr), _half_cols(buf_ref.shape[2], c)]
    return pltpu.make_async_remote_copy(
        src_ref=part, dst_ref=part, send_sem=ssem.at[k], recv_sem=rsem.at[k], device_id=to, device_id_type=MESH)


def _relay_start(land, *, name):
    def body(buf_ref, ssem, rsem, buf_out, token):
        x, y, c, _ = _place()
        _relay_copy(buf_ref, ssem, rsem, 0, 2 * (1 - x) + y, (x, 1 - y, c), c).start()
        _relay_copy(buf_ref, ssem, rsem, 1, 2 * x + 1 - y, (1 - x, y, c), c).start()
        token[...] = jnp.zeros_like(token)

    res = pl.pallas_call(
        body, name=name, in_specs=[HBM], out_specs=[SEM, SEM, HBM, pl.BlockSpec(memory_space=pltpu.VMEM)],
        out_shape=[pltpu.SemaphoreType.DMA((2,)), pltpu.SemaphoreType.DMA((2,)), _hbm(land.shape, land.dtype),
                   jax.ShapeDtypeStruct(TOKEN_SHAPE, F32)],
        input_output_aliases={0: 2}, compiler_params=pltpu.CompilerParams(has_side_effects=SIDE_EFFECT),
    )(land)
    return tuple(res[:3]), res[3]


def _relay_wait(parts, after, *, name):
    ssem, rsem, buf = parts
    after = list(after) if isinstance(after, (list, tuple)) else [after]

    def body(buf_ref, ssem_ref, rsem_ref, *rest):
        x, y, c, _ = _place()
        diagonal = 2 * (1 - x) + 1 - y
        _relay_copy(buf_ref, ssem_ref, rsem_ref, 0, 2 * (1 - x) + y, (x, y, c), c).wait_send()
        _relay_copy(buf_ref, ssem_ref, rsem_ref, 1, 2 * x + 1 - y, (x, y, c), c).wait_send()
        _relay_copy(buf_ref, ssem_ref, rsem_ref, 0, diagonal, (x, y, c), c).wait_recv()
        _relay_copy(buf_ref, ssem_ref, rsem_ref, 1, diagonal, (x, y, c), c).wait_recv()

    return pl.pallas_call(
        body, name=name, in_specs=[HBM, SEM, SEM] + [ANY] * len(after), out_specs=HBM,
        out_shape=_hbm(buf.shape, buf.dtype), input_output_aliases={0: 0},
        compiler_params=pltpu.CompilerParams(has_side_effects=SIDE_EFFECT),
    )(buf, ssem, rsem, *after)


def _forward_copy(buf_ref, ssem, rsem, k, slab, which, to):
    part = buf_ref.at[slab, :, _half_cols(buf_ref.shape[2], which)]
    return pltpu.make_async_remote_copy(
        src_ref=part, dst_ref=part, send_sem=ssem.at[k], recv_sem=rsem.at[k], device_id=to, device_id_type=MESH)


def _sibling_forward(land, *, name):
    def body(_, buf, send_sems, recv_sems):
        x, y, c, chips = _place()
        copies = []
        for k, (cx, cy) in enumerate(chips):
            cp = _forward_copy(buf, send_sems, recv_sems, k, 2 * cx + cy, c, (x, y, 1 - c))
            cp.start()
            copies.append(cp)
        for k, (cx, cy) in enumerate(chips):
            _forward_copy(buf, send_sems, recv_sems, k, 2 * cx + cy, 1 - c, (x, y, c)).wait_recv()
        for cp in copies:
            cp.wait_send()

    return pl.pallas_call(
        body, name=name, in_specs=[HBM], out_specs=HBM, out_shape=jax.ShapeDtypeStruct(land.shape, land.dtype),
        input_output_aliases={0: 0},
        scratch_shapes=[pltpu.SemaphoreType.DMA((3,)), pltpu.SemaphoreType.DMA((3,))],
    )(land)


def _share_copy(buf_ref, ssem, rsem, a, which, to):
    part = buf_ref.at[:, _half_cols(buf_ref.shape[1], which)]
    return pltpu.make_async_remote_copy(
        src_ref=part, dst_ref=part, send_sem=ssem.at[a], recv_sem=rsem.at[a], device_id=to, device_id_type=MESH)


def _share_start(arrays, *, name):
    n = len(arrays)

    def body(*refs):
        bufs, ssem, rsem, token = refs[:n], refs[n], refs[n + 1], refs[-1]
        x, y, c, _ = _place()
        for a in range(n):
            _share_copy(bufs[a], ssem, rsem, a, c, (x, y, 1 - c)).start()
        token[...] = jnp.zeros_like(token)

    res = pl.pallas_call(
        body, name=name, in_specs=[HBM] * n,
        out_specs=[SEM, SEM] + [HBM] * n + [pl.BlockSpec(memory_space=pltpu.VMEM)],
        out_shape=[pltpu.SemaphoreType.DMA((n,)), pltpu.SemaphoreType.DMA((n,))]
        + [_hbm(b.shape, b.dtype) for b in arrays] + [jax.ShapeDtypeStruct(TOKEN_SHAPE, F32)],
        input_output_aliases={a: 2 + a for a in range(n)},
        compiler_params=pltpu.CompilerParams(has_side_effects=SIDE_EFFECT),
    )(*[_in_hbm(b) for b in arrays])
    return (res[0], res[1], list(res[2:2 + n])), res[-1]


def _share_wait(parts, after, *, name):
    ssem, rsem, bufs = parts
    n = len(bufs)
    after = list(after) if isinstance(after, (list, tuple)) else [after]

    def body(*refs):
        buf_refs, ssem_ref, rsem_ref = refs[:n], refs[n], refs[n + 1]
        x, y, c, _ = _place()
        for a in range(n):
            _share_copy(buf_refs[a], ssem_ref, rsem_ref, a, c, (x, y, c)).wait_send()
            _share_copy(buf_refs[a], ssem_ref, rsem_ref, a, 1 - c, (x, y, c)).wait_recv()

    return pl.pallas_call(
        body, name=name, in_specs=[HBM] * n + [SEM, SEM] + [ANY] * len(after), out_specs=[HBM] * n,
        out_shape=[_hbm(b.shape, b.dtype) for b in bufs], input_output_aliases={a: a for a in range(n)},
        compiler_params=pltpu.CompilerParams(has_side_effects=SIDE_EFFECT),
    )(*bufs, ssem, rsem, *after)


def _scatter_copy(src_ref, land_ref, ssem, rsem, k, src_slab, dst_slab, to):
    return pltpu.make_async_remote_copy(
        src_ref=src_ref.at[src_slab], dst_ref=land_ref.at[dst_slab], send_sem=ssem.at[k], recv_sem=rsem.at[k],
        device_id=to, device_id_type=MESH)


def _scatter_start(part, *, name):
    def start(src_ref, land_ref, ssem, rsem):
        x, y, c, chips = _place()
        me = 2 * x + y
        for k, (cx, cy) in enumerate(chips):
            _scatter_copy(src_ref, land_ref, ssem, rsem, k, 2 * cx + cy, me, (cx, cy, c)).start()

    return _split_start(start, part, part.shape, N_CHIPS - 1, name=name)


def _scatter_wait(parts, after, *, name):
    def wait(src_ref, land_ref, ssem_ref, rsem_ref):
        x, y, c, chips = _place()
        for k, (cx, cy) in enumerate(chips):
            idx = 2 * cx + cy
            cp = _scatter_copy(src_ref, land_ref, ssem_ref, rsem_ref, k, idx, idx, (x, y, c))
            cp.wait_send()
            cp.wait_recv()

    return _wait_call(wait, parts, after, name=name)


def _split_start(start_fn, src, land_shape, n_sems, *, name):
    def body(src_ref, land_ref, ssem, rsem, src_out, land_out, token):
        start_fn(src_ref, land_ref, ssem, rsem)
        token[...] = jnp.zeros_like(token)

    res = pl.pallas_call(
        body, name=name, in_specs=[HBM, HBM], out_specs=[SEM, SEM, HBM, HBM, pl.BlockSpec(memory_space=pltpu.VMEM)],
        out_shape=[pltpu.SemaphoreType.DMA((n_sems,)), pltpu.SemaphoreType.DMA((n_sems,)), _hbm(src.shape, src.dtype),
                   _hbm(land_shape, src.dtype), jax.ShapeDtypeStruct(TOKEN_SHAPE, F32)],
        input_output_aliases={0: 2, 1: 3}, compiler_params=pltpu.CompilerParams(has_side_effects=SIDE_EFFECT),
    )(_in_hbm(src), _in_hbm(lax.empty(land_shape, src.dtype)))
    return tuple(res[:4]), res[4]


def _sibling_copies(src_ref, land_ref, ssem, rsem, k0, groups, which, to):
    def copy(k, src, dst):
        return pltpu.make_async_remote_copy(
            src_ref=src, dst_ref=dst, send_sem=ssem.at[k], recv_sem=rsem.at[k], device_id=to, device_id_type=MESH)

    if groups == 0:
        return [copy(k0, src_ref, land_ref)]
    hw = src_ref.shape[1] // groups // 2
    return [copy(k0 + j, src_ref.at[:, pl.ds(pl.multiple_of((2 * j + which) * hw, LANES), hw)],
                 land_ref.at[:, j * hw:(j + 1) * hw]) for j in range(groups)]


def _to_sibling_start(items, *, name):
    n = len(items)
    shapes = [a.shape if g == 0 else (a.shape[0], a.shape[1] // 2) for a, g in items]
    first = [sum(max(g, 1) for _, g in items[:k]) for k in range(n + 1)]

    def body(*refs):
        srcs, lands, ssem, rsem, token = refs[:n], refs[n:2 * n], refs[2 * n], refs[2 * n + 1], refs[-1]
        x, y, c, _ = _place()
        for k, (_, g) in enumerate(items):
            for cp in _sibling_copies(srcs[k], lands[k], ssem, rsem, first[k], g, 1 - c, (x, y, 1 - c)):
                cp.start()
        token[...] = jnp.zeros_like(token)

    res = pl.pallas_call(
        body, name=name, in_specs=[HBM] * (2 * n),
        out_specs=[SEM, SEM] + [HBM] * (2 * n) + [pl.BlockSpec(memory_space=pltpu.VMEM)],
        out_shape=[pltpu.SemaphoreType.DMA((first[n],)), pltpu.SemaphoreType.DMA((first[n],))]
        + [_hbm(a.shape, a.dtype) for a, _ in items] + [_hbm(s, a.dtype) for s, (a, _) in zip(shapes, items)]
        + [jax.ShapeDtypeStruct(TOKEN_SHAPE, F32)],
        input_output_aliases={k: 2 + k for k in range(2 * n)},
        compiler_params=pltpu.CompilerParams(has_side_effects=SIDE_EFFECT),
    )(*[_in_hbm(a) for a, _ in items], *[_in_hbm(lax.empty(s, a.dtype)) for s, (a, _) in zip(shapes, items)])
    return [(res[0], res[1], first[k], g, res[2 + k], res[2 + n + k]) for k, (_, g) in enumerate(items)], res[-1]


def _from_sibling(flight, after, *, name):
    ssem, rsem, k0, groups, src, land = flight

    def wait(src_ref, land_ref, ssem_ref, rsem_ref):
        x, y, c, _ = _place()
        for cp in _sibling_copies(src_ref, land_ref, ssem_ref, rsem_ref, k0, groups, 1 - c, (x, y, c)):
            cp.wait_send()
            cp.wait_recv()

    return _wait_call(wait, (ssem, rsem, src, land), after, name=name)


def _dev_peers(x, y, c, chips):
    return [(x, y, 1 - c)] + [(cx, cy, c) for cx, cy in chips] + [(cx, cy, 1 - c) for cx, cy in chips]


def _dev_gather_start(part, *, name):
    def start(src_ref, land_ref, ssem, rsem):
        x, y, c, chips = _place()
        for k, to in enumerate(_dev_peers(x, y, c, chips)):
            pltpu.make_async_remote_copy(
                src_ref=src_ref, dst_ref=land_ref.at[4 * x + 2 * y + c], send_sem=ssem.at[k], recv_sem=rsem.at[k],
                device_id=to, device_id_type=MESH).start()

    return _split_start(start, part, (N_DEV,) + part.shape, N_DEV - 1, name=name)


def _dev_gather_wait(parts, after, *, name):
    def wait(src_ref, land_ref, ssem_ref, rsem_ref):
        x, y, c, chips = _place()
        for k, (px, py, pc) in enumerate(_dev_peers(x, y, c, chips)):
            cp = pltpu.make_async_remote_copy(
                src_ref=src_ref, dst_ref=land_ref.at[4 * px + 2 * py + pc], send_sem=ssem_ref.at[k],
                recv_sem=rsem_ref.at[k], device_id=(x, y, c), device_id_type=MESH)
            cp.wait_send()
            cp.wait_recv()

    return _wait_call(wait, parts, after, name=name)[1]


def _sibling_share_halves(arrays, *, name):
    n = len(arrays)

    def body(*refs):
        bufs = refs[n:2 * n]
        send_sems, recv_sems = refs[2 * n:]
        x, y, c, _ = _place()
        copies = []
        for a in range(n):
            mine = bufs[a].at[:, _half_cols(bufs[a].shape[1], c)]
            cp = pltpu.make_async_remote_copy(
                src_ref=mine, dst_ref=mine, send_sem=send_sems.at[a], recv_sem=recv_sems.at[a],
                device_id=(x, y, 1 - c), device_id_type=MESH)
            cp.start()
            copies.append(cp)
        for a in range(n):
            theirs = bufs[a].at[:, _half_cols(bufs[a].shape[1], 1 - c)]
            pltpu.make_async_remote_copy(
                src_ref=theirs, dst_ref=theirs, send_sem=send_sems.at[a], recv_sem=recv_sems.at[a],
                device_id=(x, y, c), device_id_type=MESH).wait_recv()
        for cp in copies:
            cp.wait_send()

    return pl.pallas_call(
        body, name=name, in_specs=[HBM] * n, out_specs=[HBM] * n,
        out_shape=[jax.ShapeDtypeStruct(h.shape, h.dtype) for h in arrays],
        input_output_aliases={a: a for a in range(n)},
        scratch_shapes=[pltpu.SemaphoreType.DMA((n,)), pltpu.SemaphoreType.DMA((n,))],
    )(*arrays)


def _pack(arrays, rows_multiple=16, width=LANES):
    flat = jnp.concatenate([a.astype(F32).reshape(-1) for a in arrays])
    total = flat.shape[0]
    rows = -(-total // width)
    rows = -(-rows // rows_multiple) * rows_multiple
    return jnp.pad(flat, (0, rows * width - total)).reshape(rows, width)


def _unpack(buf, shapes):
    flat = buf.reshape(-1)
    out, off = [], 0
    for s in shapes:
        n = math.prod(s)
        out.append(flat[off:off + n].reshape(s))
        off += n
    return out


def kernel(x, norm_pre, norm_post, gla_w_in, gla_w_gate2, gla_b_gate, gla_o_gain, gla_w_out, sgu_w_in, sgu_ln_gain, sgu_ln_bias, sgu_w_spatial, sgu_b_spatial, sgu_w_out, loss_target, m_norm_pre, m_norm_post, m_gla_w_in, m_gla_w_gate2, m_gla_b_gate, m_gla_o_gain, m_gla_w_out, m_sgu_w_in, m_sgu_ln_gain, m_sgu_ln_bias, m_sgu_w_spatial, m_sgu_b_spatial, m_sgu_w_out, v_norm_pre, v_norm_post, v_gla_w_in, v_gla_w_gate2, v_gla_b_gate, v_gla_o_gain, v_gla_w_out, v_sgu_w_in, v_sgu_ln_gain, v_sgu_ln_bias, v_sgu_w_spatial, v_sgu_b_spatial, v_sgu_w_out):
    _, t, d = x.shape
    dk = d // 2
    ws = gla_w_in.shape[2]
    wp = -(-ws // LANES) * LANES
    lay = (ws, wp)
    chip =2 * lax.axis_index("x") + lax.axis_index("y")
    core = lax.axis_index("c")
    core_idx = core.astype(jnp.int32).reshape(1)
    others = jnp.arange(N_CHIPS - 1, dtype=jnp.int32)
    others = others + (others >= chip).astype(jnp.int32)
    slots = jnp.concatenate([chip.astype(jnp.int32).reshape(1), others, core_idx])

    x0 = x[0]
    target = loss_target[0]

    wt_in_g, mt_in_g, vt_in_g = gla_w_in[0].T, m_gla_w_in[0].T, v_gla_w_in[0].T

    small_shard = _pack([gla_w_gate2[0], sgu_ln_gain[0], sgu_ln_bias[0]], rows_multiple=8, width=2 * LANES)
    own = [small_shard, jnp.pad(wt_in_g.astype(BF16), ((0, wp - ws), (0, 0)))]
    in_flight, token = _gather_start(own, name="gather_start_a", relayed=(1,))

    def with_sibling_and_own(mine, land, name):
        return lax.dynamic_update_slice(_sibling_forward(land, name=name + "_share"), mine[None], (chip, 0, 0))

    h0 = _norm_pre(x0, norm_pre[0:1] + token[0:1, 0:1], name="pre0")
    g_small = with_sibling_and_own(*_gather_wait(in_flight[0], h0, name="w_small_wait"), "w_small")
    mine, land = _gather_wait(in_flight[1], [g_small, wt_in_g, mt_in_g, vt_in_g], name="w_gla_in_wait", ks=NEIGHBOURS)
    relay, token = _relay_start(land, name="w_gla_in_relay")
    own_later = [(p[0] + token[0, 0]).astype(BF16) for p in (gla_w_out, sgu_w_in, sgu_w_out)]
    in_flight_later, token = _gather_start(own_later, name="gather_start_b", after=[token])
    in_flight = in_flight + in_flight_later
    land = _relay_wait(relay, token, name="w_gla_in_relay_wait")
    wt_g = with_sibling_and_own(mine, land, "w_gla_in").reshape(N_CHIPS * wp, d)

    def arrived(i, after, name):
        return with_sibling_and_own(*_gather_wait(in_flight[i], after, name=name + "_wait"), name)
    shard_shapes = [gla_w_gate2.shape[1:], sgu_ln_gain.shape[1:], sgu_ln_bias.shape[1:]]
    per_chip = [_unpack(g_small[j], shard_shapes) for j in range(N_CHIPS)]
    w2_full = jnp.concatenate([p[0] for p in per_chip], axis=1)
    ln_gain = jnp.concatenate([p[1] for p in per_chip], axis=0)[None, :]
    ln_bias = jnp.concatenate([p[2] for p in per_chip], axis=0)[None, :]
    w2p = jnp.pad(w2_full, ((0, LANES - GLA_GATE_RANK), (0, 0)))

    pos_chunk = jnp.arange(SGU_BLOCK) // CHUNK
    mask = pos_chunk[:, None] >= pos_chunk[None, :]
    ws_masked = jnp.where(mask[None], sgu_w_spatial[0], 0.0)
    ws_masked_t = ws_masked.transpose(0, 2, 1)
    bs_t = sgu_b_spatial[0].T

    proj0 = _matmul(h0, wt_g, mode="nt", out_dtype=F32, name="gla_in", tn=wp)
    o0, a0, s_before, s_final = _gla_fwd(proj0, w2p, gla_b_gate, gla_o_gain, lay, name="gla_scan")
    w_out_g = arrived(2, a0, "w_gla_out").reshape(d, d)
    y0 = _matmul(a0, w_out_g, mode="nn", out_dtype=F32, name="gla_out")
    x1, h1 = _post_then_pre(x0, y0, norm_post[0:1], norm_pre[1:2], name="post0_pre1")
    g_wi_s = arrived(3, h1, "w_sgu_in")
    proj1 = _matmul(h1, g_wi_s, mode="nn", out_dtype=F32, name="sgu_in", b_shards=True)
    a1 = _sgu_fwd(proj1, ln_gain, ln_bias, ws_masked, bs_t, name="sgu_gate")
    w_out_s = arrived(4, a1, "w_sgu_out").reshape(d, d)
    acts, tok = _to_sibling_start([(a1, 0), (a0, 0), (h1, 0), (h0, 1)], name="acts_to_sibling")
    a1, a0, h1, h0 = [f[4] for f in acts]
    y1 = _matmul(a1, w_out_s, mode="nn", out_dtype=F32, name="sgu_out", after=tok)
    loss_part, dx2, dy1, d_post1 = _loss_head(x1, y1, norm_post[1:2], target, name="loss_head")

    def behind(small, token):
        return small + token[0:1, 0:1]

    def pair_gradient(a_sent, b_sent, after, shards_on, name):
        a_me, a_sib = _from_sibling(a_sent, after, name=name + "_a_wait")
        b_me, b_sib = _from_sibling(b_sent, [a_sib] + list(after), name=name + "_b_wait")
        pair = _matmul_dw_pair(a_me, a_sib, b_me, b_sib, core_idx, shards_on=shards_on,
                               name=name + "_pair")
        return _scatter_start(pair, name=name + "_start")

    def reduced(flight, after, name):
        pair, landed = _scatter_wait(flight, after, name=name + "_wait")
        return _chip_sum(pair, landed, slots, name=name + "_sum")

    (dy1_sent,), tok = _to_sibling_start([(dy1, 1)], name="dy1_to_sibling")
    dy1 = dy1_sent[4]
    da1 = _matmul(dy1, w_out_s, mode="nt", out_dtype=F32, name="d_sgu_act", after=tok)
    fl_wo_s, tok = pair_gradient(acts[0], dy1_sent, [da1], "rows", "g_sgu_out")
    dproj1, d_ws, d_bs_t, d_lg, d_lb = _sgu_bwd(da1, proj1, ln_gain, behind(ln_bias, tok), ws_masked, ws_masked_t,
                                                bs_t, name="sgu_gate_bwd")
    (dp1_sent,), tok = _to_sibling_start([(dproj1, N_CHIPS)], name="dproj1_to_sibling")
    dproj1 = dp1_sent[4]
    dh1 = _matmul_nt_shards(dproj1, g_wi_s, out_dtype=F32, name="d_sgu_h", after=tok)
    fl_wi_s, tok = pair_gradient(acts[2], dp1_sent, [dh1], "cols", "g_sgu_in")
    dx1, dy0, d_pre1, d_post0 = _mid_bwd(dx2, dh1, x1, behind(norm_pre[1:2], tok), y0, norm_post[0:1],
                                         name="pre1_post0_bwd")
    (dy0_sent,), tok = _to_sibling_start([(dy0, 1)], name="dy0_to_sibling")
    dy0 = dy0_sent[4]
    da0 = _matmul(dy0, w_out_g, mode="nt", out_dtype=F32, name="d_gla_act", after=tok)
    fl_wo_g, tok = pair_gradient(acts[1], dy0_sent, [da0], "rows", "g_gla_out")
    dproj0, d_og, d_bg, d_w2p = _gla_bwd(da0, o0, proj0, w2p, behind(gla_b_gate, tok), gla_o_gain, s_before, s_final,
                                         lay, name="gla_scan_bwd")
    early_shapes = [norm_post.shape, gla_b_gate.shape, gla_o_gain.shape, sgu_w_spatial.shape, sgu_b_spatial.shape,
                    (1, GLA_GATE_RANK, dk), (1, d), (1, d), (1, LANES)]
    early_part = _pack([jnp.concatenate([d_post0, d_post1], axis=0), d_bg, d_og, jnp.where(mask[None], d_ws, 0.0)[None],
                        d_bs_t.T[None], d_w2p[:GLA_GATE_RANK][None], d_lg, d_lb, loss_part])
    early_flight, tok = _dev_gather_start(early_part, name="small_early_start")
    (dp0_sent,), tok_sent = _to_sibling_start([(dproj0, 0)], name="dproj0_to_sibling")
    dproj0 = dp0_sent[4]
    dh0 = _matmul(dproj0, wt_g, mode="nn", out_dtype=F32, name="d_gla_h", after=tok_sent)
    a_me, a_sib = _from_sibling(dp0_sent, [dh0, tok], name="g_gla_in_a_wait")
    b_me, b_sib = _from_sibling(acts[3], [a_sib, dh0], name="g_gla_in_b_wait")
    fl_wi_g, tok_scatter = [], None
    for p in range(2):
        pair = _matmul_dw_pair(a_me, a_sib, b_me, b_sib, core_idx, shards_on="rows", part=(p, 2),
                               name=f"g_gla_in_pair{p}", after=tok_scatter)
        flight, tok_scatter = _scatter_start(pair, name=f"g_gla_in_start{p}")
        fl_wi_g.append(flight)
    r_wo_s = reduced(fl_wo_s, tok_scatter, "g_sgu_out")
    r_wi_s = reduced(fl_wi_s, r_wo_s, "g_sgu_in")
    r_wo_g = reduced(fl_wo_g, r_wi_s, "g_gla_out")
    sharing, tok = _share_start([r_wo_s, r_wi_s, r_wo_g], name="grads_share_a")
    grad_x, d_pre0 = _first_bwd(dx1, dh0, x0, behind(norm_pre[0:1], tok), name="pre0_bwd")

    late_part = _pack([jnp.concatenate([d_pre0, d_pre1], axis=0)])
    late_flight, tok = _dev_gather_start(late_part, name="small_late_start")

    def big_update(w, g, m, v, name, after=None):
        return [u[None] for u in _adamw(w[0], g, m[0], v[0], name=name, after=after)]

    g_wo_sgu, g_wi_sgu, g_wo_gla = _share_wait(sharing, [grad_x, tok], name="grads_share_a_wait")
    u_wi_sgu = big_update(sgu_w_in, g_wi_sgu, m_sgu_w_in, v_sgu_w_in, "adamw_sgu_w_in")
    u_wo_gla = big_update(gla_w_out, g_wo_gla, m_gla_w_out, v_gla_w_out, "adamw_gla_w_out", after=u_wi_sgu[1])

    r_wi_g, behind_this = None, u_wo_gla[1]
    for p, flight in enumerate(fl_wi_g):
        pair, landed = _scatter_wait(flight, behind_this, name=f"g_gla_in_wait{p}")
        r_wi_g = behind_this = _chip_sum(pair, landed, slots, part=(p, 2), into=r_wi_g, name=f"g_gla_in_sum{p}")
    gt_wi_gla, = _sibling_share_halves([r_wi_g], name="grads_share_b")
    u_wi_gla_t = _adamw(wt_in_g, gt_wi_gla, mt_in_g, vt_in_g, name="adamw_gla_w_in")
    u_wi_gla = [u.T[None] for u in u_wi_gla_t]
    u_wo_sgu = big_update(sgu_w_out, g_wo_sgu, m_sgu_w_out, v_sgu_w_out, "adamw_sgu_w_out", after=u_wi_gla_t[1])

    def summed_over_devices(part, flight, after, shapes, name):
        land = _dev_gather_wait(flight, after, name=name + "_wait")
        every = lax.dynamic_update_slice(land, part[None], (2 * chip + core, 0, 0))
        return _unpack(_stack_sum(every, name=name + "_sum"), shapes)

    (g_post, g_bg, g_og, g_wsp, g_bsp, g_w2_full, g_lg_full, g_lb_full, loss_vec) = summed_over_devices(
        early_part, early_flight, u_wo_sgu[1], early_shapes, "small_early")
    g_pre, = summed_over_devices(late_part, late_flight, loss_vec, [norm_pre.shape], "small_late")
    loss = loss_vec[0, 0]
    g_w2 = lax.dynamic_slice_in_dim(g_w2_full, chip * (dk // N_CHIPS), dk // N_CHIPS, axis=2)
    g_lg = lax.dynamic_slice_in_dim(g_lg_full, chip * (d // N_CHIPS), d // N_CHIPS, axis=1)
    g_lb = lax.dynamic_slice_in_dim(g_lb_full, chip * (d // N_CHIPS), d // N_CHIPS, axis=1)

    small_w = [norm_pre, norm_post, gla_b_gate, gla_o_gain, sgu_w_spatial, sgu_b_spatial, gla_w_gate2, sgu_ln_gain,
               sgu_ln_bias]
    small_g = [g_pre, g_post, g_bg, g_og, g_wsp, g_bsp, g_w2, g_lg, g_lb]
    small_m = [m_norm_pre, m_norm_post, m_gla_b_gate, m_gla_o_gain, m_sgu_w_spatial, m_sgu_b_spatial, m_gla_w_gate2,
               m_sgu_ln_gain, m_sgu_ln_bias]
    small_v = [v_norm_pre, v_norm_post, v_gla_b_gate, v_gla_o_gain, v_sgu_w_spatial, v_sgu_b_spatial, v_gla_w_gate2,
               v_sgu_ln_gain, v_sgu_ln_bias]
    own_shapes = [w.shape for w in small_w]
    _, s_dl, s_m, s_v = _adamw(_pack(small_w), _pack(small_g), _pack(small_m), _pack(small_v), name="adamw_small")
    dl_s, m_s, v_s = _unpack(s_dl, own_shapes), _unpack(s_m, own_shapes), _unpack(s_v, own_shapes)

    def ordered(small, kind):
        pre, post, bg, og, wsp, bsp, w2, lg, lb = small
        return [pre, post, u_wi_gla[kind], w2, bg, og, u_wo_gla[kind], u_wi_sgu[kind], lg, lb, wsp, bsp, u_wo_sgu[kind]]

    return (loss, grad_x[None], *ordered(small_g, 0), *ordered(dl_s, 1), *ordered(m_s, 2), *ordered(v_s, 3))
```

```python
import math

import jax
import jax.numpy as jnp
from jax import lax
from jax.experimental import pallas as pl
from jax.experimental.pallas import tpu as pltpu

F32 = jnp.float32
BF16 = jnp.bfloat16
MESH = pl.DeviceIdType.MESH

EPS = 1e-6
CHUNK = 64
GLA_HEADS = 4
GLA_GATE_RANK = 16
GLA_TAU = 16.0
SGU_BLOCK = 128
SGU_GROUPS = 8
N_CHIPS = 4
N_DEV = 8
LANES = 128

ADAM_LR = 0.001
ADAM_B1 = 0.9
ADAM_B2 = 0.999
ADAM_EPS = 1e-08
ADAM_WD = 0.01
ADAM_STEP = 10

VMEM_LIMIT = 56 * 1024 * 1024


def _cparams(sem=None):
    return pltpu.CompilerParams(dimension_semantics=sem, vmem_limit_bytes=VMEM_LIMIT)


def _pick(n, cap, unit=LANES):
    best = None
    for t in range(unit, min(n, cap) + 1, unit):
        if n % t == 0:
            best = t
    assert best is not None, (n, cap, unit)
    return best


def _dot(a, b, dims):
    return lax.dot_general(a, b, (dims, ((), ())), preferred_element_type=F32)


def _dot_nn(a, b):
    return _dot(a, b, ((1,), (0,)))


def _dot_nt(a, b):
    return _dot(a, b, ((1,), (1,)))


def _dot_tn(a, b):
    return _dot(a, b, ((0,), (0,)))


def _matmul(a, b, *, mode, out_dtype, name, tm=1024, tn=512, b_shards=False, after=None):
    M, K = a.shape
    if b_shards:
        ns, Kb, bc = b.shape
        N, tn = ns * bc, _pick(bc, tn)
        per = bc // tn
        b_spec = pl.BlockSpec((None, K, tn), lambda i, j: (j // per, 0, j % per))
    elif mode == "nt":
        N, Kb = b.shape
        tn = _pick(N, tn)
        b_spec = pl.BlockSpec((tn, K), lambda i, j: (j, 0))
    else:
        Kb, N = b.shape
        tn = _pick(N, tn)
        b_spec = pl.BlockSpec((K, tn), lambda i, j: (0, j))
    assert K == Kb and a.dtype == b.dtype == BF16, (a.shape, b.shape, mode)
    tm = _pick(M, tm)
    dims = ((1,), (1,)) if mode == "nt" else ((1,), (0,))
    extra_specs, extra_args = ([], []) if after is None else ([pl.BlockSpec(memory_space=pl.ANY)], [after])

    def body(a_ref, b_ref, *rest):
        rest[-1][...] = _dot(a_ref[...], b_ref[...], dims).astype(out_dtype)

    return pl.pallas_call(
        body, name=name, grid=(M // tm, N // tn),
        in_specs=[pl.BlockSpec((tm, K), lambda i, j: (i, 0)), b_spec] + extra_specs,
        out_specs=pl.BlockSpec((tm, tn), lambda i, j: (i, j)), out_shape=jax.ShapeDtypeStruct((M, N), out_dtype),
        compiler_params=_cparams(("parallel", "parallel")),
    )(a, b, *extra_args)


def _matmul_nt_shards(a, b, *, out_dtype, name, tm=1024, tn=512, after=None):
    M, K = a.shape
    ns, N, kc = b.shape
    assert K == ns * kc
    tm, tn = _pick(M, tm), _pick(N, tn)

    def body(a_ref, *rest):
        b_refs, o_ref = rest[:ns], rest[ns + (after is not None)]
        acc = _dot_nt(a_ref[:, 0:kc], b_refs[0][...])
        for j in range(1, ns):
            acc += _dot_nt(a_ref[:, j * kc:(j + 1) * kc], b_refs[j][...])
        o_ref[...] = acc.astype(out_dtype)

    def shard(j):
        return pl.BlockSpec((None, tn, kc), lambda i, n: (j, n, 0))

    extra_specs, extra_args = ([], []) if after is None else ([pl.BlockSpec(memory_space=pl.ANY)], [after])
    return pl.pallas_call(
        body, name=name, grid=(M // tm, N // tn),
        in_specs=[pl.BlockSpec((tm, K), lambda i, n: (i, 0))] + [shard(j) for j in range(ns)] + extra_specs,
        out_specs=pl.BlockSpec((tm, tn), lambda i, n: (i, n)), out_shape=jax.ShapeDtypeStruct((M, N), out_dtype),
        compiler_params=_cparams(("parallel", "parallel")),
    )(a, *([b] * ns), *extra_args)


def _rstd(x):
    return lax.rsqrt(jnp.mean(x * x, axis=-1, keepdims=True) + EPS)


def _row_spec(tr, d):
    return pl.BlockSpec((tr, d), lambda i: (i, 0))


def _vec_spec(d):
    return pl.BlockSpec((1, d), lambda i: (0, 0))


def _acc_rows(ref, i, val, cols=slice(None)):
    @pl.when(i == 0)
    def _():
        ref[:, cols] = val

    @pl.when(i > 0)
    def _():
        ref[:, cols] += val


def _norm_pre(x, gain, *, name, tr=256):
    t, d = x.shape
    tr = _pick(t, tr, 8)

    def body(x_ref, g_ref, h_ref):
        xv = x_ref[...]
        h_ref[...] = (xv * _rstd(xv) * g_ref[...]).astype(BF16)

    return pl.pallas_call(
        body, name=name, grid=(t // tr,), in_specs=[_row_spec(tr, d), _vec_spec(d)], out_specs=_row_spec(tr, d),
        out_shape=jax.ShapeDtypeStruct((t, d), BF16), compiler_params=_cparams(("parallel",)),
    )(x, gain)


def _post_then_pre(x, y, post_gain, pre_gain, *, name, tr=256):
    t, d = x.shape
    tr = _pick(t, tr, 8)

    def body(x_ref, y_ref, pg_ref, ng_ref, xn_ref, h_ref):
        yv = y_ref[...]
        xn = x_ref[...] + yv * _rstd(yv) * pg_ref[...]
        xn_ref[...] = xn
        h_ref[...] = (xn * _rstd(xn) * ng_ref[...]).astype(BF16)

    return pl.pallas_call(
        body, name=name, grid=(t // tr,),
        in_specs=[_row_spec(tr, d), _row_spec(tr, d), _vec_spec(d), _vec_spec(d)],
        out_specs=[_row_spec(tr, d), _row_spec(tr, d)],
        out_shape=[jax.ShapeDtypeStruct((t, d), F32), jax.ShapeDtypeStruct((t, d), BF16)],
        compiler_params=_cparams(("parallel",)),
    )(x, y, post_gain, pre_gain)


def _norm_bwd(dy, n, r, gain):
    dn = dy * gain
    return r * (dn - n * jnp.mean(dn * n, axis=-1, keepdims=True))


def _loss_head(x, y, post_gain, target, *, name, tr=256):
    t, d = x.shape
    tr = _pick(t, tr, 8)

    def body(x_ref, y_ref, pg_ref, t_ref, loss_ref, dx_ref, dy_ref, dpg_ref):
        i = pl.program_id(0)
        yv = y_ref[...]
        r = _rstd(yv)
        n = yv * r
        err = x_ref[...] + n * pg_ref[...] - t_ref[...]
        dx = err * (1.0 / d)
        dx_ref[...] = dx
        part = 0.5 * jnp.sum(jnp.mean(err * err, axis=-1, keepdims=True), axis=0, keepdims=True)
        _acc_rows(loss_ref, i, jnp.broadcast_to(part, (1, LANES)))
        _acc_rows(dpg_ref, i, jnp.sum(dx * n, axis=0, keepdims=True))
        dy_ref[...] = _norm_bwd(dx, n, r, pg_ref[...]).astype(BF16)

    return pl.pallas_call(
        body, name=name, grid=(t // tr,),
        in_specs=[_row_spec(tr, d), _row_spec(tr, d), _vec_spec(d), _row_spec(tr, d)],
        out_specs=[_vec_spec(LANES), _row_spec(tr, d), _row_spec(tr, d), _vec_spec(d)],
        out_shape=[jax.ShapeDtypeStruct((1, LANES), F32), jax.ShapeDtypeStruct((t, d), F32),
                   jax.ShapeDtypeStruct((t, d), BF16), jax.ShapeDtypeStruct((1, d), F32)],
        compiler_params=_cparams(("arbitrary",)),
    )(x, y, post_gain, target)


def _mid_bwd(dx_out, dh, x, pre_gain, y_prev, post_gain_prev, *, name, tr=256):
    t, d = x.shape
    tr = _pick(t, tr, 8)

    def body(dxo_ref, dh_ref, x_ref, ng_ref, y_ref, pg_ref, dx_ref, dy_ref, dng_ref, dpg_ref):
        i = pl.program_id(0)
        xv = x_ref[...]
        r = _rstd(xv)
        xh = xv * r
        dhv = dh_ref[...]
        _acc_rows(dng_ref, i, jnp.sum(dhv * xh, axis=0, keepdims=True))
        dx = dxo_ref[...] + _norm_bwd(dhv, xh, r, ng_ref[...])
        dx_ref[...] = dx
        yv = y_ref[...]
        ry = _rstd(yv)
        n = yv * ry
        _acc_rows(dpg_ref, i, jnp.sum(dx * n, axis=0, keepdims=True))
        dy_ref[...] = _norm_bwd(dx, n, ry, pg_ref[...]).astype(BF16)

    return pl.pallas_call(
        body, name=name, grid=(t // tr,),
        in_specs=[_row_spec(tr, d), _row_spec(tr, d), _row_spec(tr, d), _vec_spec(d), _row_spec(tr, d), _vec_spec(d)],
        out_specs=[_row_spec(tr, d), _row_spec(tr, d), _vec_spec(d), _vec_spec(d)],
        out_shape=[jax.ShapeDtypeStruct((t, d), F32), jax.ShapeDtypeStruct((t, d), BF16),
                   jax.ShapeDtypeStruct((1, d), F32), jax.ShapeDtypeStruct((1, d), F32)],
        compiler_params=_cparams(("arbitrary",)),
    )(dx_out, dh, x, pre_gain, y_prev, post_gain_prev)


def _first_bwd(dx_out, dh, x, pre_gain, *, name, tr=256):
    t, d = x.shape
    tr = _pick(t, tr, 8)

    def body(dxo_ref, dh_ref, x_ref, ng_ref, dx_ref, dng_ref):
        i = pl.program_id(0)
        xv = x_ref[...]
        r = _rstd(xv)
        xh = xv * r
        dhv = dh_ref[...]
        _acc_rows(dng_ref, i, jnp.sum(dhv * xh, axis=0, keepdims=True))
        dx_ref[...] = dxo_ref[...] + _norm_bwd(dhv, xh, r, ng_ref[...])

    return pl.pallas_call(
        body, name=name, grid=(t // tr,),
        in_specs=[_row_spec(tr, d), _row_spec(tr, d), _row_spec(tr, d), _vec_spec(d)],
        out_specs=[_row_spec(tr, d), _vec_spec(d)],
        out_shape=[jax.ShapeDtypeStruct((t, d), F32), jax.ShapeDtypeStruct((1, d), F32)],
        compiler_params=_cparams(("arbitrary",)),
    )(dx_out, dh, x, pre_gain)


def _sigmoid(x):
    return 1.0 / (1.0 + jnp.exp(-x))


def _log_sigmoid(x):
    return jnp.minimum(x, 0.0) - jnp.log(1.0 + jnp.exp(-jnp.abs(x)))


_GELU_C = math.sqrt(2.0 / math.pi)


_GELU_A = 0.044715


def _gelu_parts(x, with_grad=True):
    x2 = x * x
    h = 0.5 * jnp.tanh(x * (_GELU_C + (_GELU_C * _GELU_A) * x2)) + 0.5
    val = x * h
    if not with_grad:
        return val, None
    return val, h * (1.0 + (1.0 - h) * (x * (2.0 * _GELU_C + (6.0 * _GELU_C * _GELU_A) * x2)))


def _split3(x):
    hi = x.astype(BF16)
    r1 = x - hi.astype(F32)
    mid = r1.astype(BF16)
    lo = (r1 - mid.astype(F32)).astype(BF16)
    return hi, mid, lo


def _tri_matmul(tri_bf16, x):
    hi, mid, lo = _split3(x)
    return _dot_nn(tri_bf16, hi) + _dot_nn(tri_bf16, mid) + _dot_nn(tri_bf16, lo)


def _gla_dims(d):
    dk, dv = d // 2, d
    return dk, dv, dk // GLA_HEADS, dv // GLA_HEADS


def _col_pieces(a, b, lay):
    ws, wp = lay
    out = []
    while a < b:
        j = a // ws
        end = min(b, (j + 1) * ws)
        out.append((j * wp + a - j * ws, end - a))
        a = end
    return out


def _load_cols(ref, a, b, lay):
    parts = [ref[:, s:s + n] for s, n in _col_pieces(a, b, lay)]
    return parts[0] if len(parts) == 1 else jnp.concatenate(parts, axis=1)


def _store_cols(ref, a, val, lay):
    off = 0
    for s, n in _col_pieces(a, a + val.shape[1], lay):
        ref[:, s:s + n] = val[:, off:off + n]
        off += n


def _gate_window(c_r, lay):
    (start, _), = _col_pieces(c_r, c_r + GLA_GATE_RANK, lay)
    assert (start % lay[1]) + LANES <= lay[1]
    return slice(start, start + LANES)


def _gla_gates(glr, k, w2_ref, b_ref):
    z = _dot_nn(glr.astype(BF16), w2_ref[...].astype(BF16)) + b_ref[...]
    la = _log_sigmoid(z) * (1.0 / GLA_TAU)
    row = lax.broadcasted_iota(jnp.int32, (CHUNK, CHUNK), 0)
    col = lax.broadcasted_iota(jnp.int32, (CHUNK, CHUNK), 1)
    incl = (row >= col).astype(BF16)
    bcum = _tri_matmul(incl, la)
    b_end = bcum[CHUNK - 1:CHUNK, :]
    e_rest = jnp.exp(b_end - bcum)
    return z, e_rest, k * e_rest, jnp.exp(b_end)


def _gla_fwd(proj, w2p, b_gate, o_gain, lay, *, name):
    t, wcols = proj.shape
    d = o_gain.shape[1]
    dk, dv, dkh, dvh = _gla_dims(d)
    nc = t // CHUNK
    c_k, c_v, c_g, c_r = dk, 2 * dk, 2 * dk + dv, 2 * dk + 2 * dv
    scale = dkh ** -0.5

    def body(p_ref, w2_ref, b_ref, og_ref, o_ref, a_ref, sb_ref, sfin_ref, s_ref):
        i = pl.program_id(0)

        @pl.when(i == 0)
        def _():
            s_ref[...] = jnp.zeros_like(s_ref)

        q = _load_cols(p_ref, 0, dk, lay) * scale
        k = _load_cols(p_ref, c_k, c_k + dk, lay)
        glr = p_ref[:, _gate_window(c_r, lay)]
        _, _, kdec, decay = _gla_gates(glr, k, w2_ref, b_ref)
        for h in range(GLA_HEADS):
            ks = slice(h * dkh, (h + 1) * dkh)
            vs = slice(h * dvh, (h + 1) * dvh)
            v_h = _load_cols(p_ref, c_v + h * dvh, c_v + (h + 1) * dvh, lay)
            g_h = _load_cols(p_ref, c_g + h * dvh, c_g + (h + 1) * dvh, lay)
            s_old = s_ref[h]
            sb_ref[0, h] = s_old
            s_new = s_old * decay[:, ks] + _dot_tn(v_h.astype(BF16), kdec[:, ks].astype(BF16))
            s_ref[h] = s_new
            o_h = _dot_nt(q[:, ks].astype(BF16), s_new.astype(BF16))
            o_ref[:, vs] = o_h
            on = o_h * _rstd(o_h)
            a_ref[:, vs] = (on * og_ref[:, vs] * (g_h * _sigmoid(g_h))).astype(BF16)

        @pl.when(i == nc - 1)
        def _():
            sfin_ref[...] = s_ref[...]

    full = lambda *shape: pl.BlockSpec(shape, lambda i: (0,) * len(shape))
    return pl.pallas_call(
        body, name=name, grid=(nc,),
        in_specs=[pl.BlockSpec((CHUNK, wcols), lambda i: (i, 0)), full(LANES, dk), full(1, dk), full(1, dv)],
        out_specs=[pl.BlockSpec((CHUNK, dv), lambda i: (i, 0)), pl.BlockSpec((CHUNK, dv), lambda i: (i, 0)),
                   pl.BlockSpec((1, GLA_HEADS, dvh, dkh), lambda i: (i, 0, 0, 0)), full(GLA_HEADS, dvh, dkh)],
        out_shape=[jax.ShapeDtypeStruct((t, dv), F32), jax.ShapeDtypeStruct((t, dv), BF16),
                   jax.ShapeDtypeStruct((nc, GLA_HEADS, dvh, dkh), F32),
                   jax.ShapeDtypeStruct((GLA_HEADS, dvh, dkh), F32)],
        scratch_shapes=[pltpu.VMEM((GLA_HEADS, dvh, dkh), F32)],
        compiler_params=_cparams(("arbitrary",)),
    )(proj, w2p, b_gate, o_gain)


def _gla_bwd(da, o, proj, w2p, b_gate, o_gain, s_before, s_final, lay, *, name):
    t, wcols = proj.shape
    d = o_gain.shape[1]
    dk, dv, dkh, dvh = _gla_dims(d)
    nc = t // CHUNK
    c_k, c_v, c_g, c_r = dk, 2 * dk, 2 * dk + dv, 2 * dk + 2 * dv
    scale = dkh ** -0.5

    def body(da_ref, o_ref, p_ref, w2_ref, b_ref, og_ref, sb_ref, sfin_ref,
             dp_ref, dog_ref, db_ref, dw2_ref, s_ref, gc_ref, dkd_ref):
        i = pl.program_id(0)

        @pl.when(i == 0)
        def _():
            s_ref[...] = sfin_ref[...]
            gc_ref[...] = jnp.zeros_like(gc_ref)

        ws, wp = lay
        for j in range(N_CHIPS):
            dp_ref[:, j * wp + ws:(j + 1) * wp] = jnp.zeros((CHUNK, wp - ws), BF16)
        q = _load_cols(p_ref, 0, dk, lay) * scale
        k = _load_cols(p_ref, c_k, c_k + dk, lay)
        glr = p_ref[:, _gate_window(c_r, lay)]
        z, e_rest, kdec, decay = _gla_gates(glr, k, w2_ref, b_ref)
        ddecay = []
        for h in range(GLA_HEADS):
            ks = slice(h * dkh, (h + 1) * dkh)
            vs = slice(h * dvh, (h + 1) * dvh)
            v_h = _load_cols(p_ref, c_v + h * dvh, c_v + (h + 1) * dvh, lay)
            g_h = _load_cols(p_ref, c_g + h * dvh, c_g + (h + 1) * dvh, lay)
            da_h = da_ref[:, vs]
            o_h = o_ref[:, vs]
            og_h = og_ref[:, vs]
            r = _rstd(o_h)
            on = o_h * r
            sg = _sigmoid(g_h)
            silu = g_h * sg
            _acc_rows(dog_ref, i, jnp.sum(da_h * silu * on, axis=0, keepdims=True), vs)
            _store_cols(dp_ref, c_g + h * dvh, (da_h * (on * og_h) * (sg * (1.0 + g_h * (1.0 - sg)))).astype(BF16),
                        lay)
            don = da_h * silu * og_h
            do_h = (r * (don - on * jnp.mean(don * on, axis=-1, keepdims=True))).astype(BF16)
            s_cur = s_ref[h]
            _store_cols(dp_ref, h * dkh, (_dot_nn(do_h, s_cur.astype(BF16)) * scale).astype(BF16), lay)
            g_tot = gc_ref[h] + _dot_tn(do_h, q[:, ks].astype(BF16))
            g_bf = g_tot.astype(BF16)
            dkd_ref[:, ks] = _dot_nn(v_h.astype(BF16), g_bf)
            _store_cols(dp_ref, c_v + h * dvh, _dot_nt(kdec[:, ks].astype(BF16), g_bf).astype(BF16), lay)
            s_prev = sb_ref[0, h]
            ddecay.append(jnp.sum(g_tot * s_prev, axis=0, keepdims=True))
            gc_ref[h] = g_tot * decay[:, ks]
            s_ref[h] = s_prev
        dkdec = dkd_ref[...]
        _store_cols(dp_ref, c_k, (dkdec * e_rest).astype(BF16), lay)
        d_e = dkdec * kdec
        row = lax.broadcasted_iota(jnp.int32, (CHUNK, CHUNK), 0)
        col = lax.broadcasted_iota(jnp.int32, (CHUNK, CHUNK), 1)
        excl = (row > col).astype(BF16)
        dla = jnp.concatenate(ddecay, axis=1) * decay + _tri_matmul(excl, d_e)
        dz = dla * (1.0 / GLA_TAU) * (1.0 - _sigmoid(z))
        _acc_rows(db_ref, i, jnp.sum(dz, axis=0, keepdims=True))
        dz_bf = dz.astype(BF16)
        dw2 = _dot_tn(glr.astype(BF16), dz_bf)

        @pl.when(i == 0)
        def _():
            dw2_ref[...] = dw2

        @pl.when(i > 0)
        def _():
            dw2_ref[...] += dw2

        dp_ref[:, _gate_window(c_r, lay)] = _dot_nt(dz_bf, w2_ref[...].astype(BF16)).astype(BF16)

    rev = lambda i: (nc - 1 - i, 0)
    full = lambda *shape: pl.BlockSpec(shape, lambda i: (0,) * len(shape))
    return pl.pallas_call(
        body, name=name, grid=(nc,),
        in_specs=[pl.BlockSpec((CHUNK, dv), rev), pl.BlockSpec((CHUNK, dv), rev), pl.BlockSpec((CHUNK, wcols), rev),
                  full(LANES, dk), full(1, dk), full(1, dv),
                  pl.BlockSpec((1, GLA_HEADS, dvh, dkh), lambda i: (nc - 1 - i, 0, 0, 0)), full(GLA_HEADS, dvh, dkh)],
        out_specs=[pl.BlockSpec((CHUNK, wcols), rev), full(1, dv), full(1, dk), full(LANES, dk)],
        out_shape=[jax.ShapeDtypeStruct((t, wcols), BF16), jax.ShapeDtypeStruct((1, dv), F32),
                   jax.ShapeDtypeStruct((1, dk), F32), jax.ShapeDtypeStruct((LANES, dk), F32)],
        scratch_shapes=[pltpu.VMEM((GLA_HEADS, dvh, dkh), F32), pltpu.VMEM((GLA_HEADS, dvh, dkh), F32),
                        pltpu.VMEM((CHUNK, dk), F32)],
        compiler_params=_cparams(("arbitrary",)),
    )(da, o, proj, w2p, b_gate, o_gain, s_before, s_final)


def _sgu_mid(p_ref, lg_ref, lb_ref, ws_ref, bst_ref, w, with_grad=True):
    gd = w // SGU_GROUPS
    u_act, du_fac = _gelu_parts(p_ref[:, 0:w], with_grad)
    vf, dv_fac = _gelu_parts(p_ref[:, w:2 * w], with_grad)
    mu = jnp.mean(vf, axis=-1, keepdims=True)
    cen = vf - mu
    rstd = lax.rsqrt(jnp.mean(cen * cen, axis=-1, keepdims=True) + EPS)
    xh = cen * rstd
    vn = (xh * lg_ref[...] + lb_ref[...]).astype(BF16)
    vs = [_dot_nn(ws_ref[g].astype(BF16), vn[:, g * gd:(g + 1) * gd]) + bst_ref[:, g:g + 1]
          for g in range(SGU_GROUPS)]
    return u_act, du_fac, dv_fac, rstd, xh, vn, vs


def _sgu_fwd(proj, ln_gain, ln_bias, ws_masked, bs_t, *, name):
    t, w3 = proj.shape
    w = w3 // 3
    gd = w // SGU_GROUPS
    nb = t // SGU_BLOCK

    def body(p_ref, lg_ref, lb_ref, ws_ref, bst_ref, a_ref):
        u_act, _, _, _, _, _, vs = _sgu_mid(p_ref, lg_ref, lb_ref, ws_ref, bst_ref, w, with_grad=False)
        for g in range(SGU_GROUPS):
            cs = slice(g * gd, (g + 1) * gd)
            gate = p_ref[:, 2 * w + g * gd:2 * w + (g + 1) * gd]
            a_ref[:, cs] = (u_act[:, cs] * vs[g] * (gate * _sigmoid(gate))).astype(BF16)

    full = lambda *shape: pl.BlockSpec(shape, lambda i: (0,) * len(shape))
    return pl.pallas_call(
        body, name=name, grid=(nb,),
        in_specs=[pl.BlockSpec((SGU_BLOCK, w3), lambda i: (i, 0)), full(1, w), full(1, w),
                  full(SGU_GROUPS, SGU_BLOCK, SGU_BLOCK), full(SGU_BLOCK, SGU_GROUPS)],
        out_specs=pl.BlockSpec((SGU_BLOCK, w), lambda i: (i, 0)),
        out_shape=jax.ShapeDtypeStruct((t, w), BF16),
        compiler_params=_cparams(("parallel",)),
    )(proj, ln_gain, ln_bias, ws_masked, bs_t)


def _sgu_bwd(da, proj, ln_gain, ln_bias, ws_masked, ws_masked_t, bs_t, *, name):
    t, w3 = proj.shape
    w = w3 // 3
    gd = w // SGU_GROUPS
    nb = t // SGU_BLOCK

    def body(da_ref, p_ref, lg_ref, lb_ref, ws_ref, wst_ref, bst_ref, dp_ref, dws_ref, dbst_ref, dlg_ref, dlb_ref,
             dvn_ref):
        i = pl.program_id(0)
        u_act, du_fac, dv_fac, rstd, xh, vn, vs = _sgu_mid(p_ref, lg_ref, lb_ref, ws_ref, bst_ref, w)
        for g in range(SGU_GROUPS):
            cs = slice(g * gd, (g + 1) * gd)
            gate = p_ref[:, 2 * w + g * gd:2 * w + (g + 1) * gd]
            sg = _sigmoid(gate)
            silu = gate * sg
            da_g = da_ref[:, cs]
            ua_g = u_act[:, cs]
            dp_ref[:, cs] = (da_g * vs[g] * silu * du_fac[:, cs]).astype(BF16)
            dp_ref[:, 2 * w + g * gd:2 * w + (g + 1) * gd] = (
                da_g * ua_g * vs[g] * (sg * (1.0 + gate * (1.0 - sg)))).astype(BF16)
            dvs = da_g * ua_g * silu
            dvs_bf = dvs.astype(BF16)
            dvn_ref[:, cs] = _dot_nn(wst_ref[g].astype(BF16), dvs_bf)
            dws = _dot_nt(dvs_bf, vn[:, cs])
            dbs = jnp.sum(dvs, axis=1, keepdims=True)

            @pl.when(i == 0)
            def _():
                dws_ref[g] = dws
                dbst_ref[:, g:g + 1] = dbs

            @pl.when(i > 0)
            def _():
                dws_ref[g] += dws
                dbst_ref[:, g:g + 1] += dbs

        dvn = dvn_ref[...]
        _acc_rows(dlg_ref, i, jnp.sum(dvn * xh, axis=0, keepdims=True))
        _acc_rows(dlb_ref, i, jnp.sum(dvn, axis=0, keepdims=True))
        dxh = dvn * lg_ref[...]
        dvf = rstd * (dxh - jnp.mean(dxh, axis=-1, keepdims=True)
                      - xh * jnp.mean(dxh * xh, axis=-1, keepdims=True))
        dp_ref[:, w:2 * w] = (dvf * dv_fac).astype(BF16)

    full = lambda *shape: pl.BlockSpec(shape, lambda i: (0,) * len(shape))
    return pl.pallas_call(
        body, name=name, grid=(nb,),
        in_specs=[pl.BlockSpec((SGU_BLOCK, w), lambda i: (i, 0)), pl.BlockSpec((SGU_BLOCK, w3), lambda i: (i, 0)),
                  full(1, w), full(1, w), full(SGU_GROUPS, SGU_BLOCK, SGU_BLOCK),
                  full(SGU_GROUPS, SGU_BLOCK, SGU_BLOCK), full(SGU_BLOCK, SGU_GROUPS)],
        out_specs=[pl.BlockSpec((SGU_BLOCK, w3), lambda i: (i, 0)), full(SGU_GROUPS, SGU_BLOCK, SGU_BLOCK),
                   full(SGU_BLOCK, SGU_GROUPS), full(1, w), full(1, w)],
        out_shape=[jax.ShapeDtypeStruct((t, w3), BF16), jax.ShapeDtypeStruct((SGU_GROUPS, SGU_BLOCK, SGU_BLOCK), F32),
                   jax.ShapeDtypeStruct((SGU_BLOCK, SGU_GROUPS), F32), jax.ShapeDtypeStruct((1, w), F32),
                   jax.ShapeDtypeStruct((1, w), F32)],
        scratch_shapes=[pltpu.VMEM((SGU_BLOCK, w), F32)],
        compiler_params=_cparams(("arbitrary",)),
    )(da, proj, ln_gain, ln_bias, ws_masked, ws_masked_t, bs_t)


def _tile2d(rows, cols, block_bytes, row_unit):
    if rows % row_unit == 0:
        return _pick(rows, max(row_unit, block_bytes // (4 * cols)), row_unit), cols
    return rows, _pick(cols, max(LANES, block_bytes // (4 * rows)))


def _adamw(w, g, m, v, *, name, block_bytes=1 << 20, after=None):
    rows, cols = w.shape
    tr, tc = _tile2d(rows, cols, block_bytes, 8)
    g_rows = g.shape[0]
    assert g_rows == rows or tr == rows
    extra_specs, extra_args = ([], []) if after is None else ([pl.BlockSpec(memory_space=pl.ANY)], [after])

    def body(w_ref, g_ref, m_ref, v_ref, *rest):
        go_ref, d_ref, mo_ref, vo_ref = rest[len(extra_args):]
        gv = g_ref[0:tr, :]
        go_ref[...] = gv
        mn = ADAM_B1 * m_ref[...] + (1.0 - ADAM_B1) * gv
        vn = ADAM_B2 * v_ref[...] + (1.0 - ADAM_B2) * (gv * gv)
        m_hat = mn / (1.0 - ADAM_B1 ** ADAM_STEP)
        v_hat = vn / (1.0 - ADAM_B2 ** ADAM_STEP)
        d_ref[...] = -ADAM_LR * (m_hat / (jnp.sqrt(v_hat) + ADAM_EPS) + ADAM_WD * w_ref[...])
        mo_ref[...] = mn
        vo_ref[...] = vn

    spec = pl.BlockSpec((tr, tc), lambda i, j: (i, j))
    g_spec = spec if g_rows == rows else pl.BlockSpec((g_rows, tc), lambda i, j: (0, j))
    return pl.pallas_call(
        body, name=name, grid=(rows // tr, cols // tc), in_specs=[spec, g_spec, spec, spec] + extra_specs,
        out_specs=[spec] * 4, out_shape=[jax.ShapeDtypeStruct((rows, cols), F32)] * 4,
        compiler_params=_cparams(("parallel", "parallel")),
    )(w, g, m, v, *extra_args)


def _matmul_dw_pair(a_me, a_sib, b_me, b_sib, core_idx, *, shards_on, name, after=None, part=(0, 1)):
    T, M = a_me.shape
    N = b_me.shape[1]
    if shards_on == "rows":
        p, count = part
        tm, hc = M // N_CHIPS, N // 2
        hp = hc // count
        tn = _pick(hp, 512)
        per = hp // tn
        grid = (N_CHIPS, per)
        a_spec = pl.BlockSpec((T, tm), lambda i, n, h: (0, i))
        b_me_spec = pl.BlockSpec((T, tn), lambda i, n, h: (0, (h[0] * count + p) * per + n))
        b_sib_spec = pl.BlockSpec((T, tn), lambda i, n, h: (0, p * per + n))
        out_spec = pl.BlockSpec((None, tm, tn), lambda i, n, h: (i, 0, n))
        out_shape = jax.ShapeDtypeStruct((N_CHIPS, tm, hp), BF16)
    else:
        tm, hc = _pick(M, 1024), N // N_CHIPS // 2
        grid = (M // tm, N_CHIPS)
        a_spec = pl.BlockSpec((T, tm), lambda i, j, h: (0, i))
        b_me_spec = pl.BlockSpec((T, hc), lambda i, j, h: (0, 2 * j + h[0]))
        b_sib_spec = pl.BlockSpec((T, hc), lambda i, j, h: (0, j))
        out_spec = pl.BlockSpec((None, tm, hc), lambda i, j, h: (j, i, 0))
        out_shape = jax.ShapeDtypeStruct((N_CHIPS, M, hc), BF16)
    extra_specs, extra_args = ([], []) if after is None else ([pl.BlockSpec(memory_space=pl.ANY)], [after])

    def body(h_ref, am_ref, as_ref, bm_ref, bs_ref, *rest):
        o_ref = rest[len(extra_args)]
        o_ref[...] = (_dot_tn(am_ref[...], bm_ref[...]) + _dot_tn(as_ref[...], bs_ref[...])).astype(BF16)

    grid_spec = pltpu.PrefetchScalarGridSpec(
        num_scalar_prefetch=1, grid=grid, in_specs=[a_spec, a_spec, b_me_spec, b_sib_spec] + extra_specs,
        out_specs=out_spec)
    return pl.pallas_call(
        body, name=name, grid_spec=grid_spec, out_shape=out_shape, compiler_params=_cparams(("parallel", "parallel")),
    )(core_idx, a_me, a_sib, b_me, b_sib, *extra_args)


def _chip_sum(pair, landed, slots, *, name, block_bytes=1 << 20, part=(0, 1), into=None):
    p, count = part
    _, r, hp = pair.shape
    tr, tc = _tile2d(r, hp, block_bytes, 16)
    ncb = hp // tc
    extra_specs, extra_args = ([], []) if into is None else ([pl.BlockSpec(memory_space=pl.ANY)], [into])

    def body(s_ref, own_ref, l0_ref, l1_ref, l2_ref, *rest):
        rest[-1][...] = ((own_ref[...].astype(F32) + l0_ref[...].astype(F32)) + l1_ref[...].astype(F32)
                         ) + l2_ref[...].astype(F32)

    def slab(which):
        return pl.BlockSpec((None, tr, tc), lambda i, k, s: (s[which], i, k))

    grid_spec = pltpu.PrefetchScalarGridSpec(
        num_scalar_prefetch=1, grid=(r // tr, ncb),
        in_specs=[slab(0), slab(1), slab(2), slab(3)] + extra_specs,
        out_specs=pl.BlockSpec((tr, tc), lambda i, k, s: (i, (s[4] * count + p) * ncb + k)))
    return pl.pallas_call(
        body, name=name, grid_spec=grid_spec, out_shape=jax.ShapeDtypeStruct((r, 2 * hp * count), F32),
        input_output_aliases={} if into is None else {5: 0},
        compiler_params=_cparams(("parallel", "parallel")),
    )(slots, pair, landed, landed, landed, *extra_args)


def _stack_sum(x, *, name, out_dtype=F32, block_bytes=1 << 20):
    s, r, c = x.shape
    tr = _pick(r, max(8, block_bytes // (4 * c)), 16) if r % 16 == 0 else r

    def body(x_ref, o_ref):
        acc = x_ref[0].astype(F32)
        for j in range(1, s):
            acc = acc + x_ref[j].astype(F32)
        o_ref[...] = acc.astype(out_dtype)

    return pl.pallas_call(
        body, name=name, grid=(r // tr,),
        in_specs=[pl.BlockSpec((s, tr, c), lambda i: (0, i, 0))], out_specs=pl.BlockSpec((tr, c), lambda i: (i, 0)),
        out_shape=jax.ShapeDtypeStruct((r, c), out_dtype), compiler_params=_cparams(("parallel",)),
    )(x)


HBM = pl.BlockSpec(memory_space=pltpu.HBM)


def _place():
    x, y, c = lax.axis_index("x"), lax.axis_index("y"), lax.axis_index("c")
    other_chips = [(1 - x, y), (x, 1 - y), (1 - x, 1 - y)]
    return x, y, c, other_chips


def _half_cols(cols, which):
    hc = cols // 2
    return pl.ds(pl.multiple_of(which * hc, LANES), hc)


SEM = pl.BlockSpec(memory_space=pltpu.SEMAPHORE)
ANY = pl.BlockSpec(memory_space=pl.ANY)
SIDE_EFFECT = pltpu.SideEffectType.DATAFLOW_SIDE_EFFECTING
TOKEN_SHAPE = (8, LANES)


def _hbm(shape, dtype):
    return pltpu.HBM(shape, dtype)


def _in_hbm(a):
    return pltpu.with_memory_space_constraint(a, pltpu.HBM)


def _gather_copy(src_ref, land_ref, ssem, rsem, k, chip_of_block, to, c):
    cols = src_ref.shape[1]
    return pltpu.make_async_remote_copy(
        src_ref=src_ref.at[:, _half_cols(cols, c)], dst_ref=land_ref.at[chip_of_block, :, _half_cols(cols, c)],
        send_sem=ssem.at[k], recv_sem=rsem.at[k], device_id=to, device_id_type=MESH)


NEIGHBOURS = (0, 1)
ALL_CHIPS = (0, 1, 2)


def _gather_start(shards, *, name, after=(), relayed=()):
    n = len(shards)
    after = list(after)

    def body(*refs):
        srcs, lands = refs[:n], refs[n:2 * n]
        outs = refs[2 * n + len(after):]
        token = outs[-1]
        x, y, c, chips = _place()
        me = 2 * x + y
        for a in range(n):
            ssem, rsem = outs[4 * a], outs[4 * a + 1]
            for k in NEIGHBOURS if a in relayed else ALL_CHIPS:
                cx, cy = chips[k]
                _gather_copy(srcs[a], lands[a], ssem, rsem, k, me, (cx, cy, c), c).start()
        token[...] = jnp.zeros_like(token)

    out_shape, out_specs, aliases = [], [], {}
    for a, s in enumerate(shards):
        out_shape += [pltpu.SemaphoreType.DMA((3,)), pltpu.SemaphoreType.DMA((3,)), _hbm(s.shape, s.dtype),
                      _hbm((N_CHIPS,) + s.shape, s.dtype)]
        out_specs += [SEM, SEM, HBM, HBM]
        aliases[a] = 4 * a + 2
        aliases[n + a] = 4 * a + 3
    out_shape.append(jax.ShapeDtypeStruct(TOKEN_SHAPE, F32))
    out_specs.append(pl.BlockSpec(memory_space=pltpu.VMEM))
    lands = [_in_hbm(lax.empty((N_CHIPS,) + s.shape, s.dtype)) for s in shards]
    res = pl.pallas_call(
        body, name=name, in_specs=[HBM] * (2 * n) + [ANY] * len(after), out_specs=out_specs, out_shape=out_shape,
        input_output_aliases=aliases, compiler_params=pltpu.CompilerParams(has_side_effects=SIDE_EFFECT),
    )(*[_in_hbm(s) for s in shards], *lands, *after)
    return [tuple(res[4 * a:4 * a + 4]) for a in range(n)], res[-1]


def _wait_call(wait_fn, parts, after, *, name):
    ssem, rsem, src, land = parts
    after = list(after) if isinstance(after, (list, tuple)) else [after]

    def body(src_ref, land_ref, ssem_ref, rsem_ref, *rest):
        wait_fn(src_ref, land_ref, ssem_ref, rsem_ref)

    return pl.pallas_call(
        body, name=name, in_specs=[HBM, HBM, SEM, SEM] + [ANY] * len(after), out_specs=[HBM, HBM],
        out_shape=[_hbm(src.shape, src.dtype), _hbm(land.shape, land.dtype)], input_output_aliases={0: 0, 1: 1},
        compiler_params=pltpu.CompilerParams(has_side_effects=SIDE_EFFECT),
    )(src, land, ssem, rsem, *after)


def _gather_wait(parts, after, *, name, ks=ALL_CHIPS):
    def wait(src_ref, land_ref, ssem_ref, rsem_ref):
        x, y, c, chips = _place()
        for k in ks:
            cx, cy = chips[k]
            cp = _gather_copy(src_ref, land_ref, ssem_ref, rsem_ref, k, 2 * cx + cy, (x, y, c), c)
            cp.wait_send()
            cp.wait_recv()

    return _wait_call(wait, parts, after, name=name)


def _relay_copy(buf_ref, ssem, rsem, k, slab, to, c):
    hr = buf_ref.shape[1] // 2
    part = buf_ref.at[slab, pl.ds(k * hr, hr), _half_cols(buf_ref.shape[2], c)]
    return pltpu.make_async_remote_copy(
        src_ref=part, dst_ref=part, send_sem=ssem.at[k], recv_sem=rsem.at[k], device_id=to, device_id_type=MESH)


def _relay_start(land, *, name):
    def body(buf_ref, ssem, rsem, buf_out, token):
        x, y, c, _ = _place()
        _relay_copy(buf_ref, ssem, rsem, 0, 2 * (1 - x) + y, (x, 1 - y, c), c).start()
        _relay_copy(buf_ref, ssem, rsem, 1, 2 * x + 1 - y, (1 - x, y, c), c).start()
        token[...] = jnp.zeros_like(token)

    res = pl.pallas_call(
        body, name=name, in_specs=[HBM], out_specs=[SEM, SEM, HBM, pl.BlockSpec(memory_space=pltpu.VMEM)],
        out_shape=[pltpu.SemaphoreType.DMA((2,)), pltpu.SemaphoreType.DMA((2,)), _hbm(land.shape, land.dtype),
                   jax.ShapeDtypeStruct(TOKEN_SHAPE, F32)],
        input_output_aliases={0: 2}, compiler_params=pltpu.CompilerParams(has_side_effects=SIDE_EFFECT),
    )(land)
    return tuple(res[:3]), res[3]


def _relay_wait(parts, after, *, name):
    ssem, rsem, buf = parts
    after = list(after) if isinstance(after, (list, tuple)) else [after]

    def body(buf_ref, ssem_ref, rsem_ref, *rest):
        x, y, c, _ = _place()
        diagonal = 2 * (1 - x) + 1 - y
        _relay_copy(buf_ref, ssem_ref, rsem_ref, 0, 2 * (1 - x) + y, (x, y, c), c).wait_send()
        _relay_copy(buf_ref, ssem_ref, rsem_ref, 1, 2 * x + 1 - y, (x, y, c), c).wait_send()
        _relay_copy(buf_ref, ssem_ref, rsem_ref, 0, diagonal, (x, y, c), c).wait_recv()
        _relay_copy(buf_ref, ssem_ref, rsem_ref, 1, diagonal, (x, y, c), c).wait_recv()

    return pl.pallas_call(
        body, name=name, in_specs=[HBM, SEM, SEM] + [ANY] * len(after), out_specs=HBM,
        out_shape=_hbm(buf.shape, buf.dtype), input_output_aliases={0: 0},
        compiler_params=pltpu.CompilerParams(has_side_effects=SIDE_EFFECT),
    )(buf, ssem, rsem, *after)


def _forward_copy(buf_ref, ssem, rsem, k, slab, which, to):
    part = buf_ref.at[slab, :, _half_cols(buf_ref.shape[2], which)]
    return pltpu.make_async_remote_copy(
        src_ref=part, dst_ref=part, send_sem=ssem.at[k], recv_sem=rsem.at[k], device_id=to, device_id_type=MESH)


def _sibling_forward(land, *, name):
    def body(_, buf, send_sems, recv_sems):
        x, y, c, chips = _place()
        copies = []
        for k, (cx, cy) in enumerate(chips):
            cp = _forward_copy(buf, send_sems, recv_sems, k, 2 * cx + cy, c, (x, y, 1 - c))
            cp.start()
            copies.append(cp)
        for k, (cx, cy) in enumerate(chips):
            _forward_copy(buf, send_sems, recv_sems, k, 2 * cx + cy, 1 - c, (x, y, c)).wait_recv()
        for cp in copies:
            cp.wait_send()

    return pl.pallas_call(
        body, name=name, in_specs=[HBM], out_specs=HBM, out_shape=jax.ShapeDtypeStruct(land.shape, land.dtype),
        input_output_aliases={0: 0},
        scratch_shapes=[pltpu.SemaphoreType.DMA((3,)), pltpu.SemaphoreType.DMA((3,))],
    )(land)


def _forward_start(land, *, name):
    def body(buf_ref, ssem, rsem, buf_out, token):
        x, y, c, chips = _place()
        for k, (cx, cy) in enumerate(chips):
            _forward_copy(buf_ref, ssem, rsem, k, 2 * cx + cy, c, (x, y, 1 - c)).start()
        token[...] = jnp.zeros_like(token)

    res = pl.pallas_call(
        body, name=name, in_specs=[HBM], out_specs=[SEM, SEM, HBM, pl.BlockSpec(memory_space=pltpu.VMEM)],
        out_shape=[pltpu.SemaphoreType.DMA((3,)), pltpu.SemaphoreType.DMA((3,)), _hbm(land.shape, land.dtype),
                   jax.ShapeDtypeStruct(TOKEN_SHAPE, F32)],
        input_output_aliases={0: 2}, compiler_params=pltpu.CompilerParams(has_side_effects=SIDE_EFFECT),
    )(land)
    return tuple(res[:3]), res[3]


def _forward_wait(parts, after, *, name):
    ssem, rsem, buf = parts
    after = list(after) if isinstance(after, (list, tuple)) else [after]

    def body(buf_ref, ssem_ref, rsem_ref, *rest):
        x, y, c, chips = _place()
        for k, (cx, cy) in enumerate(chips):
            _forward_copy(buf_ref, ssem_ref, rsem_ref, k, 2 * cx + cy, c, (x, y, c)).wait_send()
            _forward_copy(buf_ref, ssem_ref, rsem_ref, k, 2 * cx + cy, 1 - c, (x, y, c)).wait_recv()

    return pl.pallas_call(
        body, name=name, in_specs=[HBM, SEM, SEM] + [ANY] * len(after), out_specs=HBM,
        out_shape=_hbm(buf.shape, buf.dtype), input_output_aliases={0: 0},
        compiler_params=pltpu.CompilerParams(has_side_effects=SIDE_EFFECT),
    )(buf, ssem, rsem, *after)


def _share_copy(buf_ref, ssem, rsem, a, which, to):
    part = buf_ref.at[:, _half_cols(buf_ref.shape[1], which)]
    return pltpu.make_async_remote_copy(
        src_ref=part, dst_ref=part, send_sem=ssem.at[a], recv_sem=rsem.at[a], device_id=to, device_id_type=MESH)


def _share_start(arrays, *, name):
    n = len(arrays)

    def body(*refs):
        bufs, ssem, rsem, token = refs[:n], refs[n], refs[n + 1], refs[-1]
        x, y, c, _ = _place()
        for a in range(n):
            _share_copy(bufs[a], ssem, rsem, a, c, (x, y, 1 - c)).start()
        token[...] = jnp.zeros_like(token)

    res = pl.pallas_call(
        body, name=name, in_specs=[HBM] * n,
        out_specs=[SEM, SEM] + [HBM] * n + [pl.BlockSpec(memory_space=pltpu.VMEM)],
        out_shape=[pltpu.SemaphoreType.DMA((n,)), pltpu.SemaphoreType.DMA((n,))]
        + [_hbm(b.shape, b.dtype) for b in arrays] + [jax.ShapeDtypeStruct(TOKEN_SHAPE, F32)],
        input_output_aliases={a: 2 + a for a in range(n)},
        compiler_params=pltpu.CompilerParams(has_side_effects=SIDE_EFFECT),
    )(*[_in_hbm(b) for b in arrays])
    return (res[0], res[1], list(res[2:2 + n])), res[-1]


def _share_wait(parts, after, *, name):
    ssem, rsem, bufs = parts
    n = len(bufs)
    after = list(after) if isinstance(after, (list, tuple)) else [after]

    def body(*refs):
        buf_refs, ssem_ref, rsem_ref = refs[:n], refs[n], refs[n + 1]
        x, y, c, _ = _place()
        for a in range(n):
            _share_copy(buf_refs[a], ssem_ref, rsem_ref, a, c, (x, y, c)).wait_send()
            _share_copy(buf_refs[a], ssem_ref, rsem_ref, a, 1 - c, (x, y, c)).wait_recv()

    return pl.pallas_call(
        body, name=name, in_specs=[HBM] * n + [SEM, SEM] + [ANY] * len(after), out_specs=[HBM] * n,
        out_shape=[_hbm(b.shape, b.dtype) for b in bufs], input_output_aliases={a: a for a in range(n)},
        compiler_params=pltpu.CompilerParams(has_side_effects=SIDE_EFFECT),
    )(*bufs, ssem, rsem, *after)


def _scatter_copy(src_ref, land_ref, ssem, rsem, k, src_slab, dst_slab, to):
    return pltpu.make_async_remote_copy(
        src_ref=src_ref.at[src_slab], dst_ref=land_ref.at[dst_slab], send_sem=ssem.at[k], recv_sem=rsem.at[k],
        device_id=to, device_id_type=MESH)


def _scatter_start(part, *, name):
    def start(src_ref, land_ref, ssem, rsem):
        x, y, c, chips = _place()
        me = 2 * x + y
        for k, (cx, cy) in enumerate(chips):
            _scatter_copy(src_ref, land_ref, ssem, rsem, k, 2 * cx + cy, me, (cx, cy, c)).start()

    return _split_start(start, part, part.shape, N_CHIPS - 1, name=name)


def _scatter_wait(parts, after, *, name):
    def wait(src_ref, land_ref, ssem_ref, rsem_ref):
        x, y, c, chips = _place()
        for k, (cx, cy) in enumerate(chips):
            idx = 2 * cx + cy
            cp = _scatter_copy(src_ref, land_ref, ssem_ref, rsem_ref, k, idx, idx, (x, y, c))
            cp.wait_send()
            cp.wait_recv()

    return _wait_call(wait, parts, after, name=name)


def _split_start(start_fn, src, land_shape, n_sems, *, name):
    def body(src_ref, land_ref, ssem, rsem, src_out, land_out, token):
        start_fn(src_ref, land_ref, ssem, rsem)
        token[...] = jnp.zeros_like(token)

    res = pl.pallas_call(
        body, name=name, in_specs=[HBM, HBM], out_specs=[SEM, SEM, HBM, HBM, pl.BlockSpec(memory_space=pltpu.VMEM)],
        out_shape=[pltpu.SemaphoreType.DMA((n_sems,)), pltpu.SemaphoreType.DMA((n_sems,)), _hbm(src.shape, src.dtype),
                   _hbm(land_shape, src.dtype), jax.ShapeDtypeStruct(TOKEN_SHAPE, F32)],
        input_output_aliases={0: 2, 1: 3}, compiler_params=pltpu.CompilerParams(has_side_effects=SIDE_EFFECT),
    )(_in_hbm(src), _in_hbm(lax.empty(land_shape, src.dtype)))
    return tuple(res[:4]), res[4]


def _sibling_copies(src_ref, land_ref, ssem, rsem, k0, groups, which, to):
    def copy(k, src, dst):
        return pltpu.make_async_remote_copy(
            src_ref=src, dst_ref=dst, send_sem=ssem.at[k], recv_sem=rsem.at[k], device_id=to, device_id_type=MESH)

    if groups == 0:
        return [copy(k0, src_ref, land_ref)]
    hw = src_ref.shape[1] // groups // 2
    return [copy(k0 + j, src_ref.at[:, pl.ds(pl.multiple_of((2 * j + which) * hw, LANES), hw)],
                 land_ref.at[:, j * hw:(j + 1) * hw]) for j in range(groups)]


def _to_sibling_start(items, *, name):
    n = len(items)
    shapes = [a.shape if g == 0 else (a.shape[0], a.shape[1] // 2) for a, g in items]
    first = [sum(max(g, 1) for _, g in items[:k]) for k in range(n + 1)]

    def body(*refs):
        srcs, lands, ssem, rsem, token = refs[:n], refs[n:2 * n], refs[2 * n], refs[2 * n + 1], refs[-1]
        x, y, c, _ = _place()
        for k, (_, g) in enumerate(items):
            for cp in _sibling_copies(srcs[k], lands[k], ssem, rsem, first[k], g, 1 - c, (x, y, 1 - c)):
                cp.start()
        token[...] = jnp.zeros_like(token)

    res = pl.pallas_call(
        body, name=name, in_specs=[HBM] * (2 * n),
        out_specs=[SEM, SEM] + [HBM] * (2 * n) + [pl.BlockSpec(memory_space=pltpu.VMEM)],
        out_shape=[pltpu.SemaphoreType.DMA((first[n],)), pltpu.SemaphoreType.DMA((first[n],))]
        + [_hbm(a.shape, a.dtype) for a, _ in items] + [_hbm(s, a.dtype) for s, (a, _) in zip(shapes, items)]
        + [jax.ShapeDtypeStruct(TOKEN_SHAPE, F32)],
        input_output_aliases={k: 2 + k for k in range(2 * n)},
        compiler_params=pltpu.CompilerParams(has_side_effects=SIDE_EFFECT),
    )(*[_in_hbm(a) for a, _ in items], *[_in_hbm(lax.empty(s, a.dtype)) for s, (a, _) in zip(shapes, items)])
    return [(res[0], res[1], first[k], g, res[2 + k], res[2 + n + k]) for k, (_, g) in enumerate(items)], res[-1]


def _from_sibling(flight, after, *, name):
    ssem, rsem, k0, groups, src, land = flight

    def wait(src_ref, land_ref, ssem_ref, rsem_ref):
        x, y, c, _ = _place()
        for cp in _sibling_copies(src_ref, land_ref, ssem_ref, rsem_ref, k0, groups, 1 - c, (x, y, c)):
            cp.wait_send()
            cp.wait_recv()

    return _wait_call(wait, (ssem, rsem, src, land), after, name=name)


def _dev_peers(x, y, c, chips):
    return [(x, y, 1 - c)] + [(cx, cy, c) for cx, cy in chips] + [(cx, cy, 1 - c) for cx, cy in chips]


def _dev_gather_start(part, *, name):
    def start(src_ref, land_ref, ssem, rsem):
        x, y, c, chips = _place()
        for k, to in enumerate(_dev_peers(x, y, c, chips)):
            pltpu.make_async_remote_copy(
                src_ref=src_ref, dst_ref=land_ref.at[4 * x + 2 * y + c], send_sem=ssem.at[k], recv_sem=rsem.at[k],
                device_id=to, device_id_type=MESH).start()

    return _split_start(start, part, (N_DEV,) + part.shape, N_DEV - 1, name=name)


def _dev_gather_wait(parts, after, *, name):
    def wait(src_ref, land_ref, ssem_ref, rsem_ref):
        x, y, c, chips = _place()
        for k, (px, py, pc) in enumerate(_dev_peers(x, y, c, chips)):
            cp = pltpu.make_async_remote_copy(
                src_ref=src_ref, dst_ref=land_ref.at[4 * px + 2 * py + pc], send_sem=ssem_ref.at[k],
                recv_sem=rsem_ref.at[k], device_id=(x, y, c), device_id_type=MESH)
            cp.wait_send()
            cp.wait_recv()

    return _wait_call(wait, parts, after, name=name)[1]


def _sibling_share_halves(arrays, *, name):
    n = len(arrays)

    def body(*refs):
        bufs = refs[n:2 * n]
        send_sems, recv_sems = refs[2 * n:]
        x, y, c, _ = _place()
        copies = []
        for a in range(n):
            mine = bufs[a].at[:, _half_cols(bufs[a].shape[1], c)]
            cp = pltpu.make_async_remote_copy(
                src_ref=mine, dst_ref=mine, send_sem=send_sems.at[a], recv_sem=recv_sems.at[a],
                device_id=(x, y, 1 - c), device_id_type=MESH)
            cp.start()
            copies.append(cp)
        for a in range(n):
            theirs = bufs[a].at[:, _half_cols(bufs[a].shape[1], 1 - c)]
            pltpu.make_async_remote_copy(
                src_ref=theirs, dst_ref=theirs, send_sem=send_sems.at[a], recv_sem=recv_sems.at[a],
                device_id=(x, y, c), device_id_type=MESH).wait_recv()
        for cp in copies:
            cp.wait_send()

    return pl.pallas_call(
        body, name=name, in_specs=[HBM] * n, out_specs=[HBM] * n,
        out_shape=[jax.ShapeDtypeStruct(h.shape, h.dtype) for h in arrays],
        input_output_aliases={a: a for a in range(n)},
        scratch_shapes=[pltpu.SemaphoreType.DMA((n,)), pltpu.SemaphoreType.DMA((n,))],
    )(*arrays)


def _pack(arrays, rows_multiple=16, width=LANES):
    flat = jnp.concatenate([a.astype(F32).reshape(-1) for a in arrays])
    total = flat.shape[0]
    rows = -(-total // width)
    rows = -(-rows // rows_multiple) * rows_multiple
    return jnp.pad(flat, (0, rows * width - total)).reshape(rows, width)


def _unpack(buf, shapes):
    flat = buf.reshape(-1)
    out, off = [], 0
    for s in shapes:
        n = math.prod(s)
        out.append(flat[off:off + n].reshape(s))
        off += n
    return out


def kernel(x, norm_pre, norm_post, gla_w_in, gla_w_gate2, gla_b_gate, gla_o_gain, gla_w_out, sgu_w_in, sgu_ln_gain, sgu_ln_bias, sgu_w_spatial, sgu_b_spatial, sgu_w_out, loss_target, m_norm_pre, m_norm_post, m_gla_w_in, m_gla_w_gate2, m_gla_b_gate, m_gla_o_gain, m_gla_w_out, m_sgu_w_in, m_sgu_ln_gain, m_sgu_ln_bias, m_sgu_w_spatial, m_sgu_b_spatial, m_sgu_w_out, v_norm_pre, v_norm_post, v_gla_w_in, v_gla_w_gate2, v_gla_b_gate, v_gla_o_gain, v_gla_w_out, v_sgu_w_in, v_sgu_ln_gain, v_sgu_ln_bias, v_sgu_w_spatial, v_sgu_b_spatial, v_sgu_w_out):
    _, t, d = x.shape
    dk = d // 2
    ws = gla_w_in.shape[2]
    wp = -(-ws // LANES) * LANES
    lay = (ws, wp)
    chip =2 * lax.axis_index("x") + lax.axis_index("y")
    core = lax.axis_index("c")
    core_idx = core.astype(jnp.int32).reshape(1)
    others = jnp.arange(N_CHIPS - 1, dtype=jnp.int32)
    others = others + (others >= chip).astype(jnp.int32)
    slots = jnp.concatenate([chip.astype(jnp.int32).reshape(1), others, core_idx])

    x0 = x[0]
    target = loss_target[0]

    wt_in_g, mt_in_g, vt_in_g = gla_w_in[0].T, m_gla_w_in[0].T, v_gla_w_in[0].T

    small_shard = _pack([gla_w_gate2[0], sgu_ln_gain[0], sgu_ln_bias[0]], rows_multiple=8, width=2 * LANES)
    own = [small_shard, jnp.pad(wt_in_g.astype(BF16), ((0, wp - ws), (0, 0)))]
    in_flight, token = _gather_start(own, name="gather_start_a", relayed=(1,))

    def with_sibling_and_own(mine, land, name):
        return lax.dynamic_update_slice(_sibling_forward(land, name=name + "_share"), mine[None], (chip, 0, 0))

    h0 = _norm_pre(x0, norm_pre[0:1] + token[0:1, 0:1], name="pre0")
    g_small = with_sibling_and_own(*_gather_wait(in_flight[0], h0, name="w_small_wait"), "w_small")
    mine, land = _gather_wait(in_flight[1], [g_small, wt_in_g, mt_in_g, vt_in_g], name="w_gla_in_wait", ks=NEIGHBOURS)
    relay, token = _relay_start(land, name="w_gla_in_relay")
    own_later = [(p[0] + token[0, 0]).astype(BF16) for p in (gla_w_out, sgu_w_in, sgu_w_out)]
    in_flight_later, token = _gather_start(own_later, name="gather_start_b", after=[token])
    in_flight = in_flight + in_flight_later
    land = _relay_wait(relay, token, name="w_gla_in_relay_wait")
    wt_g = with_sibling_and_own(mine, land, "w_gla_in").reshape(N_CHIPS * wp, d)

    def behind(small, token):
        return small + token[0:1, 0:1]

    def arriving(i, after, name):
        mine, land = _gather_wait(in_flight[i], after, name=name + "_wait")
        crossing, token = _forward_start(land, name=name + "_share")
        return (mine, crossing), token

    def arrived(pending, after, name):
        mine, crossing = pending
        return lax.dynamic_update_slice(_forward_wait(crossing, after, name=name + "_share_wait"), mine[None],
                                        (chip, 0, 0))

    shard_shapes = [gla_w_gate2.shape[1:], sgu_ln_gain.shape[1:], sgu_ln_bias.shape[1:]]
    per_chip = [_unpack(g_small[j], shard_shapes) for j in range(N_CHIPS)]
    w2_full = jnp.concatenate([p[0] for p in per_chip], axis=1)
    ln_gain = jnp.concatenate([p[1] for p in per_chip], axis=0)[None, :]
    ln_bias = jnp.concatenate([p[2] for p in per_chip], axis=0)[None, :]
    w2p = jnp.pad(w2_full, ((0, LANES - GLA_GATE_RANK), (0, 0)))

    pos_chunk = jnp.arange(SGU_BLOCK) // CHUNK
    mask = pos_chunk[:, None] >= pos_chunk[None, :]
    ws_masked = jnp.where(mask[None], sgu_w_spatial[0], 0.0)
    ws_masked_t = ws_masked.transpose(0, 2, 1)
    bs_t = sgu_b_spatial[0].T

    proj0 = _matmul(h0, wt_g, mode="nt", out_dtype=F32, name="gla_in", tn=wp)
    pending, tok = arriving(2, proj0, "w_gla_out")
    o0, a0, s_before, s_final = _gla_fwd(proj0, w2p, behind(gla_b_gate, tok), gla_o_gain, lay, name="gla_scan")
    w_out_g = arrived(pending, a0, "w_gla_out").reshape(d, d)
    pending, tok = arriving(3, w_out_g, "w_sgu_in")
    y0 = _matmul(a0, w_out_g, mode="nn", out_dtype=F32, name="gla_out", after=tok)
    x1, h1 = _post_then_pre(x0, y0, norm_post[0:1], norm_pre[1:2], name="post0_pre1")
    g_wi_s = arrived(pending, h1, "w_sgu_in")
    pending, tok = arriving(4, g_wi_s, "w_sgu_out")
    proj1 = _matmul(h1, g_wi_s, mode="nn", out_dtype=F32, name="sgu_in", b_shards=True, after=tok)
    a1 = _sgu_fwd(proj1, ln_gain, ln_bias, ws_masked, bs_t, name="sgu_gate")
    w_out_s = arrived(pending, a1, "w_sgu_out").reshape(d, d)
    acts, tok = _to_sibling_start([(a1, 0), (a0, 0), (h1, 0), (h0, 1)], name="acts_to_sibling")
    a1, a0, h1, h0 = [f[4] for f in acts]
    y1 = _matmul(a1, w_out_s, mode="nn", out_dtype=F32, name="sgu_out", after=tok)
    loss_part, dx2, dy1, d_post1 = _loss_head(x1, y1, norm_post[1:2], target, name="loss_head")

    def pair_gradient(a_sent, b_sent, after, shards_on, name):
        a_me, a_sib = _from_sibling(a_sent, after, name=name + "_a_wait")
        b_me, b_sib = _from_sibling(b_sent, [a_sib] + list(after), name=name + "_b_wait")
        pair = _matmul_dw_pair(a_me, a_sib, b_me, b_sib, core_idx, shards_on=shards_on,
                               name=name + "_pair")
        return _scatter_start(pair, name=name + "_start")

    def reduced(flight, after, name):
        pair, landed = _scatter_wait(flight, after, name=name + "_wait")
        return _chip_sum(pair, landed, slots, name=name + "_sum")

    (dy1_sent,), tok = _to_sibling_start([(dy1, 1)], name="dy1_to_sibling")
    dy1 = dy1_sent[4]
    da1 = _matmul(dy1, w_out_s, mode="nt", out_dtype=F32, name="d_sgu_act", after=tok)
    fl_wo_s, tok = pair_gradient(acts[0], dy1_sent, [da1], "rows", "g_sgu_out")
    dproj1, d_ws, d_bs_t, d_lg, d_lb = _sgu_bwd(da1, proj1, ln_gain, behind(ln_bias, tok), ws_masked, ws_masked_t,
                                                bs_t, name="sgu_gate_bwd")
    (dp1_sent,), tok = _to_sibling_start([(dproj1, N_CHIPS)], name="dproj1_to_sibling")
    dproj1 = dp1_sent[4]
    dh1 = _matmul_nt_shards(dproj1, g_wi_s, out_dtype=F32, name="d_sgu_h", after=tok)
    fl_wi_s, tok = pair_gradient(acts[2], dp1_sent, [dh1], "cols", "g_sgu_in")
    dx1, dy0, d_pre1, d_post0 = _mid_bwd(dx2, dh1, x1, behind(norm_pre[1:2], tok), y0, norm_post[0:1],
                                         name="pre1_post0_bwd")
    (dy0_sent,), tok = _to_sibling_start([(dy0, 1)], name="dy0_to_sibling")
    dy0 = dy0_sent[4]
    da0 = _matmul(dy0, w_out_g, mode="nt", out_dtype=F32, name="d_gla_act", after=tok)
    fl_wo_g, tok = pair_gradient(acts[1], dy0_sent, [da0], "rows", "g_gla_out")
    dproj0, d_og, d_bg, d_w2p = _gla_bwd(da0, o0, proj0, w2p, behind(gla_b_gate, tok), gla_o_gain, s_before, s_final,
                                         lay, name="gla_scan_bwd")
    early_shapes = [norm_post.shape, gla_b_gate.shape, gla_o_gain.shape, sgu_w_spatial.shape, sgu_b_spatial.shape,
                    (1, GLA_GATE_RANK, dk), (1, d), (1, d), (1, LANES)]
    early_part = _pack([jnp.concatenate([d_post0, d_post1], axis=0), d_bg, d_og, jnp.where(mask[None], d_ws, 0.0)[None],
                        d_bs_t.T[None], d_w2p[:GLA_GATE_RANK][None], d_lg, d_lb, loss_part])
    early_flight, tok = _dev_gather_start(early_part, name="small_early_start")
    (dp0_sent,), tok_sent = _to_sibling_start([(dproj0, 0)], name="dproj0_to_sibling")
    dproj0 = dp0_sent[4]
    dh0 = _matmul(dproj0, wt_g, mode="nn", out_dtype=F32, name="d_gla_h", after=tok_sent)
    a_me, a_sib = _from_sibling(dp0_sent, [dh0, tok], name="g_gla_in_a_wait")
    b_me, b_sib = _from_sibling(acts[3], [a_sib, dh0], name="g_gla_in_b_wait")
    fl_wi_g, tok_scatter = [], None
    for p in range(2):
        pair = _matmul_dw_pair(a_me, a_sib, b_me, b_sib, core_idx, shards_on="rows", part=(p, 2),
                               name=f"g_gla_in_pair{p}", after=tok_scatter)
        flight, tok_scatter = _scatter_start(pair, name=f"g_gla_in_start{p}")
        fl_wi_g.append(flight)
    r_wo_s = reduced(fl_wo_s, tok_scatter, "g_sgu_out")
    r_wi_s = reduced(fl_wi_s, r_wo_s, "g_sgu_in")
    r_wo_g = reduced(fl_wo_g, r_wi_s, "g_gla_out")
    sharing, tok = _share_start([r_wo_s, r_wi_s, r_wo_g], name="grads_share_a")
    grad_x, d_pre0 = _first_bwd(dx1, dh0, x0, behind(norm_pre[0:1], tok), name="pre0_bwd")

    late_part = _pack([jnp.concatenate([d_pre0, d_pre1], axis=0)])
    late_flight, tok = _dev_gather_start(late_part, name="small_late_start")

    def big_update(w, g, m, v, name, after=None):
        return [u[None] for u in _adamw(w[0], g, m[0], v[0], name=name, after=after)]

    g_wo_sgu, g_wi_sgu, g_wo_gla = _share_wait(sharing, [grad_x, tok], name="grads_share_a_wait")
    u_wi_sgu = big_update(sgu_w_in, g_wi_sgu, m_sgu_w_in, v_sgu_w_in, "adamw_sgu_w_in")
    u_wo_gla = big_update(gla_w_out, g_wo_gla, m_gla_w_out, v_gla_w_out, "adamw_gla_w_out", after=u_wi_sgu[1])

    r_wi_g, behind_this = None, u_wo_gla[1]
    for p, flight in enumerate(fl_wi_g):
        pair, landed = _scatter_wait(flight, behind_this, name=f"g_gla_in_wait{p}")
        r_wi_g = behind_this = _chip_sum(pair, landed, slots, part=(p, 2), into=r_wi_g, name=f"g_gla_in_sum{p}")
    gt_wi_gla, = _sibling_share_halves([r_wi_g], name="grads_share_b")
    u_wi_gla_t = _adamw(wt_in_g, gt_wi_gla, mt_in_g, vt_in_g, name="adamw_gla_w_in")
    u_wi_gla = [u.T[None] for u in u_wi_gla_t]
    u_wo_sgu = big_update(sgu_w_out, g_wo_sgu, m_sgu_w_out, v_sgu_w_out, "adamw_sgu_w_out", after=u_wi_gla_t[1])

    def summed_over_devices(part, flight, after, shapes, name):
        land = _dev_gather_wait(flight, after, name=name + "_wait")
        every = lax.dynamic_update_slice(land, part[None], (2 * chip + core, 0, 0))
        return _unpack(_stack_sum(every, name=name + "_sum"), shapes)

    (g_post, g_bg, g_og, g_wsp, g_bsp, g_w2_full, g_lg_full, g_lb_full, loss_vec) = summed_over_devices(
        early_part, early_flight, u_wo_sgu[1], early_shapes, "small_early")
    g_pre, = summed_over_devices(late_part, late_flight, loss_vec, [norm_pre.shape], "small_late")
    loss = loss_vec[0, 0]
    g_w2 = lax.dynamic_slice_in_dim(g_w2_full, chip * (dk // N_CHIPS), dk // N_CHIPS, axis=2)
    g_lg = lax.dynamic_slice_in_dim(g_lg_full, chip * (d // N_CHIPS), d // N_CHIPS, axis=1)
    g_lb = lax.dynamic_slice_in_dim(g_lb_full, chip * (d // N_CHIPS), d // N_CHIPS, axis=1)

    small_w = [norm_pre, norm_post, gla_b_gate, gla_o_gain, sgu_w_spatial, sgu_b_spatial, gla_w_gate2, sgu_ln_gain,
               sgu_ln_bias]
    small_g = [g_pre, g_post, g_bg, g_og, g_wsp, g_bsp, g_w2, g_lg, g_lb]
    small_m = [m_norm_pre, m_norm_post, m_gla_b_gate, m_gla_o_gain, m_sgu_w_spatial, m_sgu_b_spatial, m_gla_w_gate2,
               m_sgu_ln_gain, m_sgu_ln_bias]
    small_v = [v_norm_pre, v_norm_post, v_gla_b_gate, v_gla_o_gain, v_sgu_w_spatial, v_sgu_b_spatial, v_gla_w_gate2,
               v_sgu_ln_gain, v_sgu_ln_bias]
    own_shapes = [w.shape for w in small_w]
    _, s_dl, s_m, s_v = _adamw(_pack(small_w), _pack(small_g), _pack(small_m), _pack(small_v), name="adamw_small")
    dl_s, m_s, v_s = _unpack(s_dl, own_shapes), _unpack(s_m, own_shapes), _unpack(s_v, own_shapes)

    def ordered(small, kind):
        pre, post, bg, og, wsp, bsp, w2, lg, lb = small
        return [pre, post, u_wi_gla[kind], w2, bg, og, u_wo_gla[kind], u_wi_sgu[kind], lg, lb, wsp, bsp, u_wo_sgu[kind]]

    return (loss, grad_x[None], *ordered(small_g, 0), *ordered(dl_s, 1), *ordered(m_s, 2), *ordered(v_s, 3))
```

```python
import math

import jax
import jax.numpy as jnp
from jax import lax
from jax.experimental import pallas as pl
from jax.experimental.pallas import tpu as pltpu

F32 = jnp.float32
BF16 = jnp.bfloat16
MESH = pl.DeviceIdType.MESH

EPS = 1e-6
CHUNK = 64
GLA_HEADS = 4
GLA_GATE_RANK = 16
GLA_TAU = 16.0
SGU_BLOCK = 128
SGU_GROUPS = 8
N_CHIPS = 4
N_DEV = 8
LANES = 128

ADAM_LR = 0.001
ADAM_B1 = 0.9
ADAM_B2 = 0.999
ADAM_EPS = 1e-08
ADAM_WD = 0.01
ADAM_STEP = 10

VMEM_LIMIT = 56 * 1024 * 1024


def _cparams(sem=None):
    return pltpu.CompilerParams(dimension_semantics=sem, vmem_limit_bytes=VMEM_LIMIT)


def _pick(n, cap, unit=LANES):
    best = None
    for t in range(unit, min(n, cap) + 1, unit):
        if n % t == 0:
            best = t
    assert best is not None, (n, cap, unit)
    return best


def _dot(a, b, dims):
    return lax.dot_general(a, b, (dims, ((), ())), preferred_element_type=F32)


def _dot_nn(a, b):
    return _dot(a, b, ((1,), (0,)))


def _dot_nt(a, b):
    return _dot(a, b, ((1,), (1,)))


def _dot_tn(a, b):
    return _dot(a, b, ((0,), (0,)))


def _matmul(a, b, *, mode, out_dtype, name, tm=1024, tn=512, b_shards=False, after=None):
    M, K = a.shape
    if b_shards:
        ns, Kb, bc = b.shape
        N, tn = ns * bc, _pick(bc, tn)
        per = bc // tn
        b_spec = pl.BlockSpec((None, K, tn), lambda i, j: (j // per, 0, j % per))
    elif mode == "nt":
        N, Kb = b.shape
        tn = _pick(N, tn)
        b_spec = pl.BlockSpec((tn, K), lambda i, j: (j, 0))
    else:
        Kb, N = b.shape
        tn = _pick(N, tn)
        b_spec = pl.BlockSpec((K, tn), lambda i, j: (0, j))
    assert K == Kb and a.dtype == b.dtype == BF16, (a.shape, b.shape, mode)
    tm = _pick(M, tm)
    dims = ((1,), (1,)) if mode == "nt" else ((1,), (0,))
    extra_specs, extra_args = ([], []) if after is None else ([pl.BlockSpec(memory_space=pl.ANY)], [after])

    def body(a_ref, b_ref, *rest):
        rest[-1][...] = _dot(a_ref[...], b_ref[...], dims).astype(out_dtype)

    return pl.pallas_call(
        body, name=name, grid=(M // tm, N // tn),
        in_specs=[pl.BlockSpec((tm, K), lambda i, j: (i, 0)), b_spec] + extra_specs,
        out_specs=pl.BlockSpec((tm, tn), lambda i, j: (i, j)), out_shape=jax.ShapeDtypeStruct((M, N), out_dtype),
        compiler_params=_cparams(("parallel", "parallel")),
    )(a, b, *extra_args)


def _matmul_nt_shards(a, b, *, out_dtype, name, tm=1024, tn=512, after=None):
    M, K = a.shape
    ns, N, kc = b.shape
    assert K == ns * kc
    tm, tn = _pick(M, tm), _pick(N, tn)

    def body(a_ref, *rest):
        b_refs, o_ref = rest[:ns], rest[ns + (after is not None)]
        acc = _dot_nt(a_ref[:, 0:kc], b_refs[0][...])
        for j in range(1, ns):
            acc += _dot_nt(a_ref[:, j * kc:(j + 1) * kc], b_refs[j][...])
        o_ref[...] = acc.astype(out_dtype)

    def shard(j):
        return pl.BlockSpec((None, tn, kc), lambda i, n: (j, n, 0))

    extra_specs, extra_args = ([], []) if after is None else ([pl.BlockSpec(memory_space=pl.ANY)], [after])
    return pl.pallas_call(
        body, name=name, grid=(M // tm, N // tn),
        in_specs=[pl.BlockSpec((tm, K), lambda i, n: (i, 0))] + [shard(j) for j in range(ns)] + extra_specs,
        out_specs=pl.BlockSpec((tm, tn), lambda i, n: (i, n)), out_shape=jax.ShapeDtypeStruct((M, N), out_dtype),
        compiler_params=_cparams(("parallel", "parallel")),
    )(a, *([b] * ns), *extra_args)


def _rstd(x):
    return lax.rsqrt(jnp.mean(x * x, axis=-1, keepdims=True) + EPS)


def _row_spec(tr, d):
    return pl.BlockSpec((tr, d), lambda i: (i, 0))


def _vec_spec(d):
    return pl.BlockSpec((1, d), lambda i: (0, 0))


def _acc_rows(ref, i, val, cols=slice(None)):
    @pl.when(i == 0)
    def _():
        ref[:, cols] = val

    @pl.when(i > 0)
    def _():
        ref[:, cols] += val


def _norm_pre(x, gain, *, name, tr=256):
    t, d = x.shape
    tr = _pick(t, tr, 8)

    def body(x_ref, g_ref, h_ref):
        xv = x_ref[...]
        h_ref[...] = (xv * _rstd(xv) * g_ref[...]).astype(BF16)

    return pl.pallas_call(
        body, name=name, grid=(t // tr,), in_specs=[_row_spec(tr, d), _vec_spec(d)], out_specs=_row_spec(tr, d),
        out_shape=jax.ShapeDtypeStruct((t, d), BF16), compiler_params=_cparams(("parallel",)),
    )(x, gain)


def _post_then_pre(x, y, post_gain, pre_gain, *, name, tr=256):
    t, d = x.shape
    tr = _pick(t, tr, 8)

    def body(x_ref, y_ref, pg_ref, ng_ref, xn_ref, h_ref):
        yv = y_ref[...]
        xn = x_ref[...] + yv * _rstd(yv) * pg_ref[...]
        xn_ref[...] = xn
        h_ref[...] = (xn * _rstd(xn) * ng_ref[...]).astype(BF16)

    return pl.pallas_call(
        body, name=name, grid=(t // tr,),
        in_specs=[_row_spec(tr, d), _row_spec(tr, d), _vec_spec(d), _vec_spec(d)],
        out_specs=[_row_spec(tr, d), _row_spec(tr, d)],
        out_shape=[jax.ShapeDtypeStruct((t, d), F32), jax.ShapeDtypeStruct((t, d), BF16)],
        compiler_params=_cparams(("parallel",)),
    )(x, y, post_gain, pre_gain)


def _norm_bwd(dy, n, r, gain):
    dn = dy * gain
    return r * (dn - n * jnp.mean(dn * n, axis=-1, keepdims=True))


def _loss_head(x, y, post_gain, target, *, name, tr=256):
    t, d = x.shape
    tr = _pick(t, tr, 8)

    def body(x_ref, y_ref, pg_ref, t_ref, loss_ref, dx_ref, dy_ref, dpg_ref):
        i = pl.program_id(0)
        yv = y_ref[...]
        r = _rstd(yv)
        n = yv * r
        err = x_ref[...] + n * pg_ref[...] - t_ref[...]
        dx = err * (1.0 / d)
        dx_ref[...] = dx
        part = 0.5 * jnp.sum(jnp.mean(err * err, axis=-1, keepdims=True), axis=0, keepdims=True)
        _acc_rows(loss_ref, i, jnp.broadcast_to(part, (1, LANES)))
        _acc_rows(dpg_ref, i, jnp.sum(dx * n, axis=0, keepdims=True))
        dy_ref[...] = _norm_bwd(dx, n, r, pg_ref[...]).astype(BF16)

    return pl.pallas_call(
        body, name=name, grid=(t // tr,),
        in_specs=[_row_spec(tr, d), _row_spec(tr, d), _vec_spec(d), _row_spec(tr, d)],
        out_specs=[_vec_spec(LANES), _row_spec(tr, d), _row_spec(tr, d), _vec_spec(d)],
        out_shape=[jax.ShapeDtypeStruct((1, LANES), F32), jax.ShapeDtypeStruct((t, d), F32),
                   jax.ShapeDtypeStruct((t, d), BF16), jax.ShapeDtypeStruct((1, d), F32)],
        compiler_params=_cparams(("arbitrary",)),
    )(x, y, post_gain, target)


def _mid_bwd(dx_out, dh, x, pre_gain, y_prev, post_gain_prev, *, name, tr=256):
    t, d = x.shape
    tr = _pick(t, tr, 8)

    def body(dxo_ref, dh_ref, x_ref, ng_ref, y_ref, pg_ref, dx_ref, dy_ref, dng_ref, dpg_ref):
        i = pl.program_id(0)
        xv = x_ref[...]
        r = _rstd(xv)
        xh = xv * r
        dhv = dh_ref[...]
        _acc_rows(dng_ref, i, jnp.sum(dhv * xh, axis=0, keepdims=True))
        dx = dxo_ref[...] + _norm_bwd(dhv, xh, r, ng_ref[...])
        dx_ref[...] = dx
        yv = y_ref[...]
        ry = _rstd(yv)
        n = yv * ry
        _acc_rows(dpg_ref, i, jnp.sum(dx * n, axis=0, keepdims=True))
        dy_ref[...] = _norm_bwd(dx, n, ry, pg_ref[...]).astype(BF16)

    return pl.pallas_call(
        body, name=name, grid=(t // tr,),
        in_specs=[_row_spec(tr, d), _row_spec(tr, d), _row_spec(tr, d), _vec_spec(d), _row_spec(tr, d), _vec_spec(d)],
        out_specs=[_row_spec(tr, d), _row_spec(tr, d), _vec_spec(d), _vec_spec(d)],
        out_shape=[jax.ShapeDtypeStruct((t, d), F32), jax.ShapeDtypeStruct((t, d), BF16),
                   jax.ShapeDtypeStruct((1, d), F32), jax.ShapeDtypeStruct((1, d), F32)],
        compiler_params=_cparams(("arbitrary",)),
    )(dx_out, dh, x, pre_gain, y_prev, post_gain_prev)


def _first_bwd(dx_out, dh, x, pre_gain, *, name, tr=256):
    t, d = x.shape
    tr = _pick(t, tr, 8)

    def body(dxo_ref, dh_ref, x_ref, ng_ref, dx_ref, dng_ref):
        i = pl.program_id(0)
        xv = x_ref[...]
        r = _rstd(xv)
        xh = xv * r
        dhv = dh_ref[...]
        _acc_rows(dng_ref, i, jnp.sum(dhv * xh, axis=0, keepdims=True))
        dx_ref[...] = dxo_ref[...] + _norm_bwd(dhv, xh, r, ng_ref[...])

    return pl.pallas_call(
        body, name=name, grid=(t // tr,),
        in_specs=[_row_spec(tr, d), _row_spec(tr, d), _row_spec(tr, d), _vec_spec(d)],
        out_specs=[_row_spec(tr, d), _vec_spec(d)],
        out_shape=[jax.ShapeDtypeStruct((t, d), F32), jax.ShapeDtypeStruct((1, d), F32)],
        compiler_params=_cparams(("arbitrary",)),
    )(dx_out, dh, x, pre_gain)


def _sigmoid(x):
    return 1.0 / (1.0 + jnp.exp(-x))


def _log_sigmoid(x):
    return jnp.minimum(x, 0.0) - jnp.log(1.0 + jnp.exp(-jnp.abs(x)))


_GELU_C = math.sqrt(2.0 / math.pi)


_GELU_A = 0.044715


def _gelu_parts(x, with_grad=True):
    x2 = x * x
    h = 0.5 * jnp.tanh(x * (_GELU_C + (_GELU_C * _GELU_A) * x2)) + 0.5
    val = x * h
    if not with_grad:
        return val, None
    return val, h * (1.0 + (1.0 - h) * (x * (2.0 * _GELU_C + (6.0 * _GELU_C * _GELU_A) * x2)))


def _split3(x):
    hi = x.astype(BF16)
    r1 = x - hi.astype(F32)
    mid = r1.astype(BF16)
    lo = (r1 - mid.astype(F32)).astype(BF16)
    return hi, mid, lo


def _tri_matmul(tri_bf16, x):
    hi, mid, lo = _split3(x)
    return _dot_nn(tri_bf16, hi) + _dot_nn(tri_bf16, mid) + _dot_nn(tri_bf16, lo)


def _gla_dims(d):
    dk, dv = d // 2, d
    return dk, dv, dk // GLA_HEADS, dv // GLA_HEADS


def _col_pieces(a, b, lay):
    ws, wp = lay
    out = []
    while a < b:
        j = a // ws
        end = min(b, (j + 1) * ws)
        out.append((j * wp + a - j * ws, end - a))
        a = end
    return out


def _load_cols(ref, a, b, lay):
    parts = [ref[:, s:s + n] for s, n in _col_pieces(a, b, lay)]
    return parts[0] if len(parts) == 1 else jnp.concatenate(parts, axis=1)


def _store_cols(ref, a, val, lay):
    off = 0
    for s, n in _col_pieces(a, a + val.shape[1], lay):
        ref[:, s:s + n] = val[:, off:off + n]
        off += n


def _gate_window(c_r, lay):
    (start, _), = _col_pieces(c_r, c_r + GLA_GATE_RANK, lay)
    assert (start % lay[1]) + LANES <= lay[1]
    return slice(start, start + LANES)


def _gla_gates(glr, k, w2_ref, b_ref):
    z = _dot_nn(glr.astype(BF16), w2_ref[...].astype(BF16)) + b_ref[...]
    la = _log_sigmoid(z) * (1.0 / GLA_TAU)
    row = lax.broadcasted_iota(jnp.int32, (CHUNK, CHUNK), 0)
    col = lax.broadcasted_iota(jnp.int32, (CHUNK, CHUNK), 1)
    incl = (row >= col).astype(BF16)
    bcum = _tri_matmul(incl, la)
    b_end = bcum[CHUNK - 1:CHUNK, :]
    e_rest = jnp.exp(b_end - bcum)
    return z, e_rest, k * e_rest, jnp.exp(b_end)


def _gla_fwd(proj, w2p, b_gate, o_gain, lay, *, name):
    t, wcols = proj.shape
    d = o_gain.shape[1]
    dk, dv, dkh, dvh = _gla_dims(d)
    nc = t // CHUNK
    c_k, c_v, c_g, c_r = dk, 2 * dk, 2 * dk + dv, 2 * dk + 2 * dv
    scale = dkh ** -0.5

    def body(p_ref, w2_ref, b_ref, og_ref, o_ref, a_ref, sb_ref, sfin_ref, s_ref):
        i = pl.program_id(0)

        @pl.when(i == 0)
        def _():
            s_ref[...] = jnp.zeros_like(s_ref)

        q = _load_cols(p_ref, 0, dk, lay) * scale
        k = _load_cols(p_ref, c_k, c_k + dk, lay)
        glr = p_ref[:, _gate_window(c_r, lay)]
        _, _, kdec, decay = _gla_gates(glr, k, w2_ref, b_ref)
        for h in range(GLA_HEADS):
            ks = slice(h * dkh, (h + 1) * dkh)
            vs = slice(h * dvh, (h + 1) * dvh)
            v_h = _load_cols(p_ref, c_v + h * dvh, c_v + (h + 1) * dvh, lay)
            g_h = _load_cols(p_ref, c_g + h * dvh, c_g + (h + 1) * dvh, lay)
            s_old = s_ref[h]
            sb_ref[0, h] = s_old
            s_new = s_old * decay[:, ks] + _dot_tn(v_h.astype(BF16), kdec[:, ks].astype(BF16))
            s_ref[h] = s_new
            o_h = _dot_nt(q[:, ks].astype(BF16), s_new.astype(BF16))
            o_ref[:, vs] = o_h
            on = o_h * _rstd(o_h)
            a_ref[:, vs] = (on * og_ref[:, vs] * (g_h * _sigmoid(g_h))).astype(BF16)

        @pl.when(i == nc - 1)
        def _():
            sfin_ref[...] = s_ref[...]

    full = lambda *shape: pl.BlockSpec(shape, lambda i: (0,) * len(shape))
    return pl.pallas_call(
        body, name=name, grid=(nc,),
        in_specs=[pl.BlockSpec((CHUNK, wcols), lambda i: (i, 0)), full(LANES, dk), full(1, dk), full(1, dv)],
        out_specs=[pl.BlockSpec((CHUNK, dv), lambda i: (i, 0)), pl.BlockSpec((CHUNK, dv), lambda i: (i, 0)),
                   pl.BlockSpec((1, GLA_HEADS, dvh, dkh), lambda i: (i, 0, 0, 0)), full(GLA_HEADS, dvh, dkh)],
        out_shape=[jax.ShapeDtypeStruct((t, dv), F32), jax.ShapeDtypeStruct((t, dv), BF16),
                   jax.ShapeDtypeStruct((nc, GLA_HEADS, dvh, dkh), F32),
                   jax.ShapeDtypeStruct((GLA_HEADS, dvh, dkh), F32)],
        scratch_shapes=[pltpu.VMEM((GLA_HEADS, dvh, dkh), F32)],
        compiler_params=_cparams(("arbitrary",)),
    )(proj, w2p, b_gate, o_gain)


def _gla_bwd(da, o, proj, w2p, b_gate, o_gain, s_before, s_final, lay, *, name):
    t, wcols = proj.shape
    d = o_gain.shape[1]
    dk, dv, dkh, dvh = _gla_dims(d)
    nc = t // CHUNK
    c_k, c_v, c_g, c_r = dk, 2 * dk, 2 * dk + dv, 2 * dk + 2 * dv
    scale = dkh ** -0.5

    def body(da_ref, o_ref, p_ref, w2_ref, b_ref, og_ref, sb_ref, sfin_ref,
             dp_ref, dog_ref, db_ref, dw2_ref, s_ref, gc_ref, dkd_ref):
        i = pl.program_id(0)

        @pl.when(i == 0)
        def _():
            s_ref[...] = sfin_ref[...]
            gc_ref[...] = jnp.zeros_like(gc_ref)

        ws, wp = lay
        for j in range(N_CHIPS):
            dp_ref[:, j * wp + ws:(j + 1) * wp] = jnp.zeros((CHUNK, wp - ws), BF16)
        q = _load_cols(p_ref, 0, dk, lay) * scale
        k = _load_cols(p_ref, c_k, c_k + dk, lay)
        glr = p_ref[:, _gate_window(c_r, lay)]
        z, e_rest, kdec, decay = _gla_gates(glr, k, w2_ref, b_ref)
        ddecay = []
        for h in range(GLA_HEADS):
            ks = slice(h * dkh, (h + 1) * dkh)
            vs = slice(h * dvh, (h + 1) * dvh)
            v_h = _load_cols(p_ref, c_v + h * dvh, c_v + (h + 1) * dvh, lay)
            g_h = _load_cols(p_ref, c_g + h * dvh, c_g + (h + 1) * dvh, lay)
            da_h = da_ref[:, vs]
            o_h = o_ref[:, vs]
            og_h = og_ref[:, vs]
            r = _rstd(o_h)
            on = o_h * r
            sg = _sigmoid(g_h)
            silu = g_h * sg
            _acc_rows(dog_ref, i, jnp.sum(da_h * silu * on, axis=0, keepdims=True), vs)
            _store_cols(dp_ref, c_g + h * dvh, (da_h * (on * og_h) * (sg * (1.0 + g_h * (1.0 - sg)))).astype(BF16),
                        lay)
            don = da_h * silu * og_h
            do_h = (r * (don - on * jnp.mean(don * on, axis=-1, keepdims=True))).astype(BF16)
            s_cur = s_ref[h]
            _store_cols(dp_ref, h * dkh, (_dot_nn(do_h, s_cur.astype(BF16)) * scale).astype(BF16), lay)
            g_tot = gc_ref[h] + _dot_tn(do_h, q[:, ks].astype(BF16))
            g_bf = g_tot.astype(BF16)
            dkd_ref[:, ks] = _dot_nn(v_h.astype(BF16), g_bf)
            _store_cols(dp_ref, c_v + h * dvh, _dot_nt(kdec[:, ks].astype(BF16), g_bf).astype(BF16), lay)
            s_prev = sb_ref[0, h]
            ddecay.append(jnp.sum(g_tot * s_prev, axis=0, keepdims=True))
            gc_ref[h] = g_tot * decay[:, ks]
            s_ref[h] = s_prev
        dkdec = dkd_ref[...]
        _store_cols(dp_ref, c_k, (dkdec * e_rest).astype(BF16), lay)
        d_e = dkdec * kdec
        row = lax.broadcasted_iota(jnp.int32, (CHUNK, CHUNK), 0)
        col = lax.broadcasted_iota(jnp.int32, (CHUNK, CHUNK), 1)
        excl = (row > col).astype(BF16)
        dla = jnp.concatenate(ddecay, axis=1) * decay + _tri_matmul(excl, d_e)
        dz = dla * (1.0 / GLA_TAU) * (1.0 - _sigmoid(z))
        _acc_rows(db_ref, i, jnp.sum(dz, axis=0, keepdims=True))
        dz_bf = dz.astype(BF16)
        dw2 = _dot_tn(glr.astype(BF16), dz_bf)

        @pl.when(i == 0)
        def _():
            dw2_ref[...] = dw2

        @pl.when(i > 0)
        def _():
            dw2_ref[...] += dw2

        dp_ref[:, _gate_window(c_r, lay)] = _dot_nt(dz_bf, w2_ref[...].astype(BF16)).astype(BF16)

    rev = lambda i: (nc - 1 - i, 0)
    full = lambda *shape: pl.BlockSpec(shape, lambda i: (0,) * len(shape))
    return pl.pallas_call(
        body, name=name, grid=(nc,),
        in_specs=[pl.BlockSpec((CHUNK, dv), rev), pl.BlockSpec((CHUNK, dv), rev), pl.BlockSpec((CHUNK, wcols), rev),
                  full(LANES, dk), full(1, dk), full(1, dv),
                  pl.BlockSpec((1, GLA_HEADS, dvh, dkh), lambda i: (nc - 1 - i, 0, 0, 0)), full(GLA_HEADS, dvh, dkh)],
        out_specs=[pl.BlockSpec((CHUNK, wcols), rev), full(1, dv), full(1, dk), full(LANES, dk)],
        out_shape=[jax.ShapeDtypeStruct((t, wcols), BF16), jax.ShapeDtypeStruct((1, dv), F32),
                   jax.ShapeDtypeStruct((1, dk), F32), jax.ShapeDtypeStruct((LANES, dk), F32)],
        scratch_shapes=[pltpu.VMEM((GLA_HEADS, dvh, dkh), F32), pltpu.VMEM((GLA_HEADS, dvh, dkh), F32),
                        pltpu.VMEM((CHUNK, dk), F32)],
        compiler_params=_cparams(("arbitrary",)),
    )(da, o, proj, w2p, b_gate, o_gain, s_before, s_final)


def _sgu_mid(p_ref, lg_ref, lb_ref, ws_ref, bst_ref, w, with_grad=True):
    gd = w // SGU_GROUPS
    u_act, du_fac = _gelu_parts(p_ref[:, 0:w], with_grad)
    vf, dv_fac = _gelu_parts(p_ref[:, w:2 * w], with_grad)
    mu = jnp.mean(vf, axis=-1, keepdims=True)
    cen = vf - mu
    rstd = lax.rsqrt(jnp.mean(cen * cen, axis=-1, keepdims=True) + EPS)
    xh = cen * rstd
    vn = (xh * lg_ref[...] + lb_ref[...]).astype(BF16)
    vs = [_dot_nn(ws_ref[g].astype(BF16), vn[:, g * gd:(g + 1) * gd]) + bst_ref[:, g:g + 1]
          for g in range(SGU_GROUPS)]
    return u_act, du_fac, dv_fac, rstd, xh, vn, vs


def _sgu_fwd(proj, ln_gain, ln_bias, ws_masked, bs_t, *, name):
    t, w3 = proj.shape
    w = w3 // 3
    gd = w // SGU_GROUPS
    nb = t // SGU_BLOCK

    def body(p_ref, lg_ref, lb_ref, ws_ref, bst_ref, a_ref):
        u_act, _, _, _, _, _, vs = _sgu_mid(p_ref, lg_ref, lb_ref, ws_ref, bst_ref, w, with_grad=False)
        for g in range(SGU_GROUPS):
            cs = slice(g * gd, (g + 1) * gd)
            gate = p_ref[:, 2 * w + g * gd:2 * w + (g + 1) * gd]
            a_ref[:, cs] = (u_act[:, cs] * vs[g] * (gate * _sigmoid(gate))).astype(BF16)

    full = lambda *shape: pl.BlockSpec(shape, lambda i: (0,) * len(shape))
    return pl.pallas_call(
        body, name=name, grid=(nb,),
        in_specs=[pl.BlockSpec((SGU_BLOCK, w3), lambda i: (i, 0)), full(1, w), full(1, w),
                  full(SGU_GROUPS, SGU_BLOCK, SGU_BLOCK), full(SGU_BLOCK, SGU_GROUPS)],
        out_specs=pl.BlockSpec((SGU_BLOCK, w), lambda i: (i, 0)),
        out_shape=jax.ShapeDtypeStruct((t, w), BF16),
        compiler_params=_cparams(("parallel",)),
    )(proj, ln_gain, ln_bias, ws_masked, bs_t)


def _sgu_bwd(da, proj, ln_gain, ln_bias, ws_masked, ws_masked_t, bs_t, *, name):
    t, w3 = proj.shape
    w = w3 // 3
    gd = w // SGU_GROUPS
    nb = t // SGU_BLOCK

    def body(da_ref, p_ref, lg_ref, lb_ref, ws_ref, wst_ref, bst_ref, dp_ref, dws_ref, dbst_ref, dlg_ref, dlb_ref,
             dvn_ref):
        i = pl.program_id(0)
        u_act, du_fac, dv_fac, rstd, xh, vn, vs = _sgu_mid(p_ref, lg_ref, lb_ref, ws_ref, bst_ref, w)
        for g in range(SGU_GROUPS):
            cs = slice(g * gd, (g + 1) * gd)
            gate = p_ref[:, 2 * w + g * gd:2 * w + (g + 1) * gd]
            sg = _sigmoid(gate)
            silu = gate * sg
            da_g = da_ref[:, cs]
            ua_g = u_act[:, cs]
            dp_ref[:, cs] = (da_g * vs[g] * silu * du_fac[:, cs]).astype(BF16)
            dp_ref[:, 2 * w + g * gd:2 * w + (g + 1) * gd] = (
                da_g * ua_g * vs[g] * (sg * (1.0 + gate * (1.0 - sg)))).astype(BF16)
            dvs = da_g * ua_g * silu
            dvs_bf = dvs.astype(BF16)
            dvn_ref[:, cs] = _dot_nn(wst_ref[g].astype(BF16), dvs_bf)
            dws = _dot_nt(dvs_bf, vn[:, cs])
            dbs = jnp.sum(dvs, axis=1, keepdims=True)

            @pl.when(i == 0)
            def _():
                dws_ref[g] = dws
                dbst_ref[:, g:g + 1] = dbs

            @pl.when(i > 0)
            def _():
                dws_ref[g] += dws
                dbst_ref[:, g:g + 1] += dbs

        dvn = dvn_ref[...]
        _acc_rows(dlg_ref, i, jnp.sum(dvn * xh, axis=0, keepdims=True))
        _acc_rows(dlb_ref, i, jnp.sum(dvn, axis=0, keepdims=True))
        dxh = dvn * lg_ref[...]
        dvf = rstd * (dxh - jnp.mean(dxh, axis=-1, keepdims=True)
                      - xh * jnp.mean(dxh * xh, axis=-1, keepdims=True))
        dp_ref[:, w:2 * w] = (dvf * dv_fac).astype(BF16)

    full = lambda *shape: pl.BlockSpec(shape, lambda i: (0,) * len(shape))
    return pl.pallas_call(
        body, name=name, grid=(nb,),
        in_specs=[pl.BlockSpec((SGU_BLOCK, w), lambda i: (i, 0)), pl.BlockSpec((SGU_BLOCK, w3), lambda i: (i, 0)),
                  full(1, w), full(1, w), full(SGU_GROUPS, SGU_BLOCK, SGU_BLOCK),
                  full(SGU_GROUPS, SGU_BLOCK, SGU_BLOCK), full(SGU_BLOCK, SGU_GROUPS)],
        out_specs=[pl.BlockSpec((SGU_BLOCK, w3), lambda i: (i, 0)), full(SGU_GROUPS, SGU_BLOCK, SGU_BLOCK),
                   full(SGU_BLOCK, SGU_GROUPS), full(1, w), full(1, w)],
        out_shape=[jax.ShapeDtypeStruct((t, w3), BF16), jax.ShapeDtypeStruct((SGU_GROUPS, SGU_BLOCK, SGU_BLOCK), F32),
                   jax.ShapeDtypeStruct((SGU_BLOCK, SGU_GROUPS), F32), jax.ShapeDtypeStruct((1, w), F32),
                   jax.ShapeDtypeStruct((1, w), F32)],
        scratch_shapes=[pltpu.VMEM((SGU_BLOCK, w), F32)],
        compiler_params=_cparams(("arbitrary",)),
    )(da, proj, ln_gain, ln_bias, ws_masked, ws_masked_t, bs_t)


def _tile2d(rows, cols, block_bytes, row_unit):
    if rows % row_unit == 0:
        return _pick(rows, max(row_unit, block_bytes // (4 * cols)), row_unit), cols
    return rows, _pick(cols, max(LANES, block_bytes // (4 * rows)))


def _adamw(w, g, m, v, *, name, block_bytes=1 << 20, after=None):
    rows, cols = w.shape
    tr, tc = _tile2d(rows, cols, block_bytes, 8)
    g_rows = g.shape[0]
    assert g_rows == rows or tr == rows
    extra_specs, extra_args = ([], []) if after is None else ([pl.BlockSpec(memory_space=pl.ANY)], [after])

    def body(w_ref, g_ref, m_ref, v_ref, *rest):
        go_ref, d_ref, mo_ref, vo_ref = rest[len(extra_args):]
        gv = g_ref[0:tr, :]
        go_ref[...] = gv
        mn = ADAM_B1 * m_ref[...] + (1.0 - ADAM_B1) * gv
        vn = ADAM_B2 * v_ref[...] + (1.0 - ADAM_B2) * (gv * gv)
        m_hat = mn / (1.0 - ADAM_B1 ** ADAM_STEP)
        v_hat = vn / (1.0 - ADAM_B2 ** ADAM_STEP)
        d_ref[...] = -ADAM_LR * (m_hat / (jnp.sqrt(v_hat) + ADAM_EPS) + ADAM_WD * w_ref[...])
        mo_ref[...] = mn
        vo_ref[...] = vn

    spec = pl.BlockSpec((tr, tc), lambda i, j: (i, j))
    g_spec = spec if g_rows == rows else pl.BlockSpec((g_rows, tc), lambda i, j: (0, j))
    return pl.pallas_call(
        body, name=name, grid=(rows // tr, cols // tc), in_specs=[spec, g_spec, spec, spec] + extra_specs,
        out_specs=[spec] * 4, out_shape=[jax.ShapeDtypeStruct((rows, cols), F32)] * 4,
        compiler_params=_cparams(("parallel", "parallel")),
    )(w, g, m, v, *extra_args)


def _matmul_dw_pair(a_me, a_sib, b_me, b_sib, core_idx, *, shards_on, name, after=None):
    T, M = a_me.shape
    N = b_me.shape[1]
    if shards_on == "rows":
        tm, hc = M // N_CHIPS, N // 2
        tn = _pick(hc, 512)
        per = hc // tn
        grid = (N_CHIPS, per)
        a_spec = pl.BlockSpec((T, tm), lambda i, n, h: (0, i))
        b_me_spec = pl.BlockSpec((T, tn), lambda i, n, h: (0, h[0] * per + n))
        b_sib_spec = pl.BlockSpec((T, tn), lambda i, n, h: (0, n))
        out_spec = pl.BlockSpec((None, tm, tn), lambda i, n, h: (i, 0, n))
        out_shape = jax.ShapeDtypeStruct((N_CHIPS, tm, hc), BF16)
    else:
        tm, hc = _pick(M, 1024), N // N_CHIPS // 2
        grid = (M // tm, N_CHIPS)
        a_spec = pl.BlockSpec((T, tm), lambda i, j, h: (0, i))
        b_me_spec = pl.BlockSpec((T, hc), lambda i, j, h: (0, 2 * j + h[0]))
        b_sib_spec = pl.BlockSpec((T, hc), lambda i, j, h: (0, j))
        out_spec = pl.BlockSpec((None, tm, hc), lambda i, j, h: (j, i, 0))
        out_shape = jax.ShapeDtypeStruct((N_CHIPS, M, hc), BF16)
    extra_specs, extra_args = ([], []) if after is None else ([pl.BlockSpec(memory_space=pl.ANY)], [after])

    def body(h_ref, am_ref, as_ref, bm_ref, bs_ref, *rest):
        o_ref = rest[len(extra_args)]
        o_ref[...] = (_dot_tn(am_ref[...], bm_ref[...]) + _dot_tn(as_ref[...], bs_ref[...])).astype(BF16)

    grid_spec = pltpu.PrefetchScalarGridSpec(
        num_scalar_prefetch=1, grid=grid, in_specs=[a_spec, a_spec, b_me_spec, b_sib_spec] + extra_specs,
        out_specs=out_spec)
    return pl.pallas_call(
        body, name=name, grid_spec=grid_spec, out_shape=out_shape, compiler_params=_cparams(("parallel", "parallel")),
    )(core_idx, a_me, a_sib, b_me, b_sib, *extra_args)


def _matmul_dw_own(a_me, b_me, core_idx, *, name, after=None):
    T, M = a_me.shape
    tm, hc = M // N_CHIPS, b_me.shape[1] // 2
    tn = _pick(hc, 512)
    per = hc // tn
    extra_specs, extra_args = ([], []) if after is None else ([pl.BlockSpec(memory_space=pl.ANY)], [after])

    def body(h_ref, a_ref, b_ref, *rest):
        rest[-1][...] = _dot_tn(a_ref[...], b_ref[...])

    grid_spec = pltpu.PrefetchScalarGridSpec(
        num_scalar_prefetch=1, grid=(N_CHIPS, per),
        in_specs=[pl.BlockSpec((T, tm), lambda i, n, h: (0, i)),
                  pl.BlockSpec((T, tn), lambda i, n, h: (0, h[0] * per + n))] + extra_specs,
        out_specs=pl.BlockSpec((None, tm, tn), lambda i, n, h: (i, 0, n)))
    return pl.pallas_call(
        body, name=name, grid_spec=grid_spec, out_shape=jax.ShapeDtypeStruct((N_CHIPS, tm, hc), F32),
        compiler_params=_cparams(("parallel", "parallel")),
    )(core_idx, a_me, b_me, *extra_args)


def _matmul_dw_add(own, a_sib, b_sib, *, part, name, after=None):
    p, count = part
    T, M = a_sib.shape
    tm, hp = M // N_CHIPS, b_sib.shape[1] // count
    tn = _pick(hp, 512)
    per = hp // tn
    extra_specs, extra_args = ([], []) if after is None else ([pl.BlockSpec(memory_space=pl.ANY)], [after])

    def body(own_ref, a_ref, b_ref, *rest):
        rest[-1][...] = (own_ref[...] + _dot_tn(a_ref[...], b_ref[...])).astype(BF16)

    return pl.pallas_call(
        body, name=name, grid=(N_CHIPS, per),
        in_specs=[pl.BlockSpec((None, tm, tn), lambda i, n: (i, 0, p * per + n)),
                  pl.BlockSpec((T, tm), lambda i, n: (0, i)),
                  pl.BlockSpec((T, tn), lambda i, n: (0, p * per + n))] + extra_specs,
        out_specs=pl.BlockSpec((None, tm, tn), lambda i, n: (i, 0, n)),
        out_shape=jax.ShapeDtypeStruct((N_CHIPS, tm, hp), BF16), compiler_params=_cparams(("parallel", "parallel")),
    )(own, a_sib, b_sib, *extra_args)


def _chip_sum(pair, landed, slots, *, name, block_bytes=1 << 20, part=(0, 1), into=None):
    p, count = part
    _, r, hp = pair.shape
    tr, tc = _tile2d(r, hp, block_bytes, 16)
    ncb = hp // tc
    extra_specs, extra_args = ([], []) if into is None else ([pl.BlockSpec(memory_space=pl.ANY)], [into])

    def body(s_ref, own_ref, l0_ref, l1_ref, l2_ref, *rest):
        rest[-1][...] = ((own_ref[...].astype(F32) + l0_ref[...].astype(F32)) + l1_ref[...].astype(F32)
                         ) + l2_ref[...].astype(F32)

    def slab(which):
        return pl.BlockSpec((None, tr, tc), lambda i, k, s: (s[which], i, k))

    grid_spec = pltpu.PrefetchScalarGridSpec(
        num_scalar_prefetch=1, grid=(r // tr, ncb),
        in_specs=[slab(0), slab(1), slab(2), slab(3)] + extra_specs,
        out_specs=pl.BlockSpec((tr, tc), lambda i, k, s: (i, (s[4] * count + p) * ncb + k)))
    return pl.pallas_call(
        body, name=name, grid_spec=grid_spec, out_shape=jax.ShapeDtypeStruct((r, 2 * hp * count), F32),
        input_output_aliases={} if into is None else {5: 0},
        compiler_params=_cparams(("parallel", "parallel")),
    )(slots, pair, landed, landed, landed, *extra_args)


def _stack_sum(x, *, name, out_dtype=F32, block_bytes=1 << 20):
    s, r, c = x.shape
    tr = _pick(r, max(8, block_bytes // (4 * c)), 16) if r % 16 == 0 else r

    def body(x_ref, o_ref):
        acc = x_ref[0].astype(F32)
        for j in range(1, s):
            acc = acc + x_ref[j].astype(F32)
        o_ref[...] = acc.astype(out_dtype)

    return pl.pallas_call(
        body, name=name, grid=(r // tr,),
        in_specs=[pl.BlockSpec((s, tr, c), lambda i: (0, i, 0))], out_specs=pl.BlockSpec((tr, c), lambda i: (i, 0)),
        out_shape=jax.ShapeDtypeStruct((r, c), out_dtype), compiler_params=_cparams(("parallel",)),
    )(x)


HBM = pl.BlockSpec(memory_space=pltpu.HBM)


def _place():
    x, y, c = lax.axis_index("x"), lax.axis_index("y"), lax.axis_index("c")
    other_chips = [(1 - x, y), (x, 1 - y), (1 - x, 1 - y)]
    return x, y, c, other_chips


def _half_cols(cols, which):
    hc = cols // 2
    return pl.ds(pl.multiple_of(which * hc, LANES), hc)


SEM = pl.BlockSpec(memory_space=pltpu.SEMAPHORE)
ANY = pl.BlockSpec(memory_space=pl.ANY)
SIDE_EFFECT = pltpu.SideEffectType.DATAFLOW_SIDE_EFFECTING
TOKEN_SHAPE = (8, LANES)


def _hbm(shape, dtype):
    return pltpu.HBM(shape, dtype)


def _in_hbm(a):
    return pltpu.with_memory_space_constraint(a, pltpu.HBM)


def _gather_copy(src_ref, land_ref, ssem, rsem, k, chip_of_block, to, c):
    cols = src_ref.shape[1]
    return pltpu.make_async_remote_copy(
        src_ref=src_ref.at[:, _half_cols(cols, c)], dst_ref=land_ref.at[chip_of_block, :, _half_cols(cols, c)],
        send_sem=ssem.at[k], recv_sem=rsem.at[k], device_id=to, device_id_type=MESH)


NEIGHBOURS = (0, 1)
ALL_CHIPS = (0, 1, 2)


def _gather_start(shards, *, name, after=(), relayed=()):
    n = len(shards)
    after = list(after)

    def body(*refs):
        srcs, lands = refs[:n], refs[n:2 * n]
        outs = refs[2 * n + len(after):]
        token = outs[-1]
        x, y, c, chips = _place()
        me = 2 * x + y
        for a in range(n):
            ssem, rsem = outs[4 * a], outs[4 * a + 1]
            for k in NEIGHBOURS if a in relayed else ALL_CHIPS:
                cx, cy = chips[k]
                _gather_copy(srcs[a], lands[a], ssem, rsem, k, me, (cx, cy, c), c).start()
        token[...] = jnp.zeros_like(token)

    out_shape, out_specs, aliases = [], [], {}
    for a, s in enumerate(shards):
        out_shape += [pltpu.SemaphoreType.DMA((3,)), pltpu.SemaphoreType.DMA((3,)), _hbm(s.shape, s.dtype),
                      _hbm((N_CHIPS,) + s.shape, s.dtype)]
        out_specs += [SEM, SEM, HBM, HBM]
        aliases[a] = 4 * a + 2
        aliases[n + a] = 4 * a + 3
    out_shape.append(jax.ShapeDtypeStruct(TOKEN_SHAPE, F32))
    out_specs.append(pl.BlockSpec(memory_space=pltpu.VMEM))
    lands = [_in_hbm(lax.empty((N_CHIPS,) + s.shape, s.dtype)) for s in shards]
    res = pl.pallas_call(
        body, name=name, in_specs=[HBM] * (2 * n) + [ANY] * len(after), out_specs=out_specs, out_shape=out_shape,
        input_output_aliases=aliases, compiler_params=pltpu.CompilerParams(has_side_effects=SIDE_EFFECT),
    )(*[_in_hbm(s) for s in shards], *lands, *after)
    return [tuple(res[4 * a:4 * a + 4]) for a in range(n)], res[-1]


def _wait_call(wait_fn, parts, after, *, name):
    ssem, rsem, src, land = parts
    after = list(after) if isinstance(after, (list, tuple)) else [after]

    def body(src_ref, land_ref, ssem_ref, rsem_ref, *rest):
        wait_fn(src_ref, land_ref, ssem_ref, rsem_ref)

    return pl.pallas_call(
        body, name=name, in_specs=[HBM, HBM, SEM, SEM] + [ANY] * len(after), out_specs=[HBM, HBM],
        out_shape=[_hbm(src.shape, src.dtype), _hbm(land.shape, land.dtype)], input_output_aliases={0: 0, 1: 1},
        compiler_params=pltpu.CompilerParams(has_side_effects=SIDE_EFFECT),
    )(src, land, ssem, rsem, *after)


def _gather_wait(parts, after, *, name, ks=ALL_CHIPS):
    def wait(src_ref, land_ref, ssem_ref, rsem_ref):
        x, y, c, chips = _place()
        for k in ks:
            cx, cy = chips[k]
            cp = _gather_copy(src_ref, land_ref, ssem_ref, rsem_ref, k, 2 * cx + cy, (x, y, c), c)
            cp.wait_send()
            cp.wait_recv()

    return _wait_call(wait, parts, after, name=name)


def _relay_copy(buf_ref, ssem, rsem, k, slab, to, c):
    hr = buf_ref.shape[1] // 2
    part = buf_ref.at[slab, pl.ds(k * hr, hr), _half_cols(buf_ref.shape[2], c)]
    return pltpu.make_async_remote_copy(
        src_ref=part, dst_ref=part, send_sem=ssem.at[k], recv_sem=rsem.at[k], device_id=to, device_id_type=MESH)


def _relay_start(land, *, name):
    def body(buf_ref, ssem, rsem, buf_out, token):
        x, y, c, _ = _place()
        _relay_copy(buf_ref, ssem, rsem, 0, 2 * (1 - x) + y, (x, 1 - y, c), c).start()
        _relay_copy(buf_ref, ssem, rsem, 1, 2 * x + 1 - y, (1 - x, y, c), c).start()
        token[...] = jnp.zeros_like(token)

    res = pl.pallas_call(
        body, name=name, in_specs=[HBM], out_specs=[SEM, SEM, HBM, pl.BlockSpec(memory_space=pltpu.VMEM)],
        out_shape=[pltpu.SemaphoreType.DMA((2,)), pltpu.SemaphoreType.DMA((2,)), _hbm(land.shape, land.dtype),
                   jax.ShapeDtypeStruct(TOKEN_SHAPE, F32)],
        input_output_aliases={0: 2}, compiler_params=pltpu.CompilerParams(has_side_effects=SIDE_EFFECT),
    )(land)
    return tuple(res[:3]), res[3]


def _relay_wait(parts, after, *, name):
    ssem, rsem, buf = parts
    after = list(after) if isinstance(after, (list, tuple)) else [after]

    def body(buf_ref, ssem_ref, rsem_ref, *rest):
        x, y, c, _ = _place()
        diagonal = 2 * (1 - x) + 1 - y
        _relay_copy(buf_ref, ssem_ref, rsem_ref, 0, 2 * (1 - x) + y, (x, y, c), c).wait_send()
        _relay_copy(buf_ref, ssem_ref, rsem_ref, 1, 2 * x + 1 - y, (x, y, c), c).wait_send()
        _relay_copy(buf_ref, ssem_ref, rsem_ref, 0, diagonal, (x, y, c), c).wait_recv()
        _relay_copy(buf_ref, ssem_ref, rsem_ref, 1, diagonal, (x, y, c), c).wait_recv()

    return pl.pallas_call(
        body, name=name, in_specs=[HBM, SEM, SEM] + [ANY] * len(after), out_specs=HBM,
        out_shape=_hbm(buf.shape, buf.dtype), input_output_aliases={0: 0},
        compiler_params=pltpu.CompilerParams(has_side_effects=SIDE_EFFECT),
    )(buf, ssem, rsem, *after)


def _forward_copy(buf_ref, ssem, rsem, k, slab, which, to):
    part = buf_ref.at[slab, :, _half_cols(buf_ref.shape[2], which)]
    return pltpu.make_async_remote_copy(
        src_ref=part, dst_ref=part, send_sem=ssem.at[k], recv_sem=rsem.at[k], device_id=to, device_id_type=MESH)


def _sibling_forward(land, *, name):
    def body(_, buf, send_sems, recv_sems):
        x, y, c, chips = _place()
        copies = []
        for k, (cx, cy) in enumerate(chips):
            cp = _forward_copy(buf, send_sems, recv_sems, k, 2 * cx + cy, c, (x, y, 1 - c))
            cp.start()
            copies.append(cp)
        for k, (cx, cy) in enumerate(chips):
            _forward_copy(buf, send_sems, recv_sems, k, 2 * cx + cy, 1 - c, (x, y, c)).wait_recv()
        for cp in copies:
            cp.wait_send()

    return pl.pallas_call(
        body, name=name, in_specs=[HBM], out_specs=HBM, out_shape=jax.ShapeDtypeStruct(land.shape, land.dtype),
        input_output_aliases={0: 0},
        scratch_shapes=[pltpu.SemaphoreType.DMA((3,)), pltpu.SemaphoreType.DMA((3,))],
    )(land)


def _forward_start(land, *, name):
    def body(buf_ref, ssem, rsem, buf_out, token):
        x, y, c, chips = _place()
        for k, (cx, cy) in enumerate(chips):
            _forward_copy(buf_ref, ssem, rsem, k, 2 * cx + cy, c, (x, y, 1 - c)).start()
        token[...] = jnp.zeros_like(token)

    res = pl.pallas_call(
        body, name=name, in_specs=[HBM], out_specs=[SEM, SEM, HBM, pl.BlockSpec(memory_space=pltpu.VMEM)],
        out_shape=[pltpu.SemaphoreType.DMA((3,)), pltpu.SemaphoreType.DMA((3,)), _hbm(land.shape, land.dtype),
                   jax.ShapeDtypeStruct(TOKEN_SHAPE, F32)],
        input_output_aliases={0: 2}, compiler_params=pltpu.CompilerParams(has_side_effects=SIDE_EFFECT),
    )(land)
    return tuple(res[:3]), res[3]


def _forward_wait(parts, after, *, name):
    ssem, rsem, buf = parts
    after = list(after) if isinstance(after, (list, tuple)) else [after]

    def body(buf_ref, ssem_ref, rsem_ref, *rest):
        x, y, c, chips = _place()
        for k, (cx, cy) in enumerate(chips):
            _forward_copy(buf_ref, ssem_ref, rsem_ref, k, 2 * cx + cy, c, (x, y, c)).wait_send()
            _forward_copy(buf_ref, ssem_ref, rsem_ref, k, 2 * cx + cy, 1 - c, (x, y, c)).wait_recv()

    return pl.pallas_call(
        body, name=name, in_specs=[HBM, SEM, SEM] + [ANY] * len(after), out_specs=HBM,
        out_shape=_hbm(buf.shape, buf.dtype), input_output_aliases={0: 0},
        compiler_params=pltpu.CompilerParams(has_side_effects=SIDE_EFFECT),
    )(buf, ssem, rsem, *after)


def _share_copy(buf_ref, ssem, rsem, a, which, to):
    part = buf_ref.at[:, _half_cols(buf_ref.shape[1], which)]
    return pltpu.make_async_remote_copy(
        src_ref=part, dst_ref=part, send_sem=ssem.at[a], recv_sem=rsem.at[a], device_id=to, device_id_type=MESH)


def _share_start(arrays, *, name):
    n = len(arrays)

    def body(*refs):
        bufs, ssem, rsem, token = refs[:n], refs[n], refs[n + 1], refs[-1]
        x, y, c, _ = _place()
        for a in range(n):
            _share_copy(bufs[a], ssem, rsem, a, c, (x, y, 1 - c)).start()
        token[...] = jnp.zeros_like(token)

    res = pl.pallas_call(
        body, name=name, in_specs=[HBM] * n,
        out_specs=[SEM, SEM] + [HBM] * n + [pl.BlockSpec(memory_space=pltpu.VMEM)],
        out_shape=[pltpu.SemaphoreType.DMA((n,)), pltpu.SemaphoreType.DMA((n,))]
        + [_hbm(b.shape, b.dtype) for b in arrays] + [jax.ShapeDtypeStruct(TOKEN_SHAPE, F32)],
        input_output_aliases={a: 2 + a for a in range(n)},
        compiler_params=pltpu.CompilerParams(has_side_effects=SIDE_EFFECT),
    )(*[_in_hbm(b) for b in arrays])
    return (res[0], res[1], list(res[2:2 + n])), res[-1]


def _share_wait(parts, after, *, name):
    ssem, rsem, bufs = parts
    n = len(bufs)
    after = list(after) if isinstance(after, (list, tuple)) else [after]

    def body(*refs):
        buf_refs, ssem_ref, rsem_ref = refs[:n], refs[n], refs[n + 1]
        x, y, c, _ = _place()
        for a in range(n):
            _share_copy(buf_refs[a], ssem_ref, rsem_ref, a, c, (x, y, c)).wait_send()
            _share_copy(buf_refs[a], ssem_ref, rsem_ref, a, 1 - c, (x, y, c)).wait_recv()

    return pl.pallas_call(
        body, name=name, in_specs=[HBM] * n + [SEM, SEM] + [ANY] * len(after), out_specs=[HBM] * n,
        out_shape=[_hbm(b.shape, b.dtype) for b in bufs], input_output_aliases={a: a for a in range(n)},
        compiler_params=pltpu.CompilerParams(has_side_effects=SIDE_EFFECT),
    )(*bufs, ssem, rsem, *after)


def _scatter_copy(src_ref, land_ref, ssem, rsem, k, src_slab, dst_slab, to):
    return pltpu.make_async_remote_copy(
        src_ref=src_ref.at[src_slab], dst_ref=land_ref.at[dst_slab], send_sem=ssem.at[k], recv_sem=rsem.at[k],
        device_id=to, device_id_type=MESH)


def _scatter_start(part, *, name):
    def start(src_ref, land_ref, ssem, rsem):
        x, y, c, chips = _place()
        me = 2 * x + y
        for k, (cx, cy) in enumerate(chips):
            _scatter_copy(src_ref, land_ref, ssem, rsem, k, 2 * cx + cy, me, (cx, cy, c)).start()

    return _split_start(start, part, part.shape, N_CHIPS - 1, name=name)


def _scatter_wait(parts, after, *, name):
    def wait(src_ref, land_ref, ssem_ref, rsem_ref):
        x, y, c, chips = _place()
        for k, (cx, cy) in enumerate(chips):
            idx = 2 * cx + cy
            cp = _scatter_copy(src_ref, land_ref, ssem_ref, rsem_ref, k, idx, idx, (x, y, c))
            cp.wait_send()
            cp.wait_recv()

    return _wait_call(wait, parts, after, name=name)


def _split_start(start_fn, src, land_shape, n_sems, *, name):
    def body(src_ref, land_ref, ssem, rsem, src_out, land_out, token):
        start_fn(src_ref, land_ref, ssem, rsem)
        token[...] = jnp.zeros_like(token)

    res = pl.pallas_call(
        body, name=name, in_specs=[HBM, HBM], out_specs=[SEM, SEM, HBM, HBM, pl.BlockSpec(memory_space=pltpu.VMEM)],
        out_shape=[pltpu.SemaphoreType.DMA((n_sems,)), pltpu.SemaphoreType.DMA((n_sems,)), _hbm(src.shape, src.dtype),
                   _hbm(land_shape, src.dtype), jax.ShapeDtypeStruct(TOKEN_SHAPE, F32)],
        input_output_aliases={0: 2, 1: 3}, compiler_params=pltpu.CompilerParams(has_side_effects=SIDE_EFFECT),
    )(_in_hbm(src), _in_hbm(lax.empty(land_shape, src.dtype)))
    return tuple(res[:4]), res[4]


def _sibling_copies(src_ref, land_ref, ssem, rsem, k0, groups, which, to):
    def copy(k, src, dst):
        return pltpu.make_async_remote_copy(
            src_ref=src, dst_ref=dst, send_sem=ssem.at[k], recv_sem=rsem.at[k], device_id=to, device_id_type=MESH)

    if groups == 0:
        return [copy(k0, src_ref, land_ref)]
    hw = src_ref.shape[1] // groups // 2
    return [copy(k0 + j, src_ref.at[:, pl.ds(pl.multiple_of((2 * j + which) * hw, LANES), hw)],
                 land_ref.at[:, j * hw:(j + 1) * hw]) for j in range(groups)]


def _to_sibling_start(items, *, name):
    n = len(items)
    shapes = [a.shape if g == 0 else (a.shape[0], a.shape[1] // 2) for a, g in items]
    first = [sum(max(g, 1) for _, g in items[:k]) for k in range(n + 1)]

    def body(*refs):
        srcs, lands, ssem, rsem, token = refs[:n], refs[n:2 * n], refs[2 * n], refs[2 * n + 1], refs[-1]
        x, y, c, _ = _place()
        for k, (_, g) in enumerate(items):
            for cp in _sibling_copies(srcs[k], lands[k], ssem, rsem, first[k], g, 1 - c, (x, y, 1 - c)):
                cp.start()
        token[...] = jnp.zeros_like(token)

    res = pl.pallas_call(
        body, name=name, in_specs=[HBM] * (2 * n),
        out_specs=[SEM, SEM] + [HBM] * (2 * n) + [pl.BlockSpec(memory_space=pltpu.VMEM)],
        out_shape=[pltpu.SemaphoreType.DMA((first[n],)), pltpu.SemaphoreType.DMA((first[n],))]
        + [_hbm(a.shape, a.dtype) for a, _ in items] + [_hbm(s, a.dtype) for s, (a, _) in zip(shapes, items)]
        + [jax.ShapeDtypeStruct(TOKEN_SHAPE, F32)],
        input_output_aliases={k: 2 + k for k in range(2 * n)},
        compiler_params=pltpu.CompilerParams(has_side_effects=SIDE_EFFECT),
    )(*[_in_hbm(a) for a, _ in items], *[_in_hbm(lax.empty(s, a.dtype)) for s, (a, _) in zip(shapes, items)])
    return [(res[0], res[1], first[k], g, res[2 + k], res[2 + n + k]) for k, (_, g) in enumerate(items)], res[-1]


def _from_sibling(flight, after, *, name):
    ssem, rsem, k0, groups, src, land = flight

    def wait(src_ref, land_ref, ssem_ref, rsem_ref):
        x, y, c, _ = _place()
        for cp in _sibling_copies(src_ref, land_ref, ssem_ref, rsem_ref, k0, groups, 1 - c, (x, y, c)):
            cp.wait_send()
            cp.wait_recv()

    return _wait_call(wait, (ssem, rsem, src, land), after, name=name)


def _dev_peers(x, y, c, chips):
    return [(x, y, 1 - c)] + [(cx, cy, c) for cx, cy in chips] + [(cx, cy, 1 - c) for cx, cy in chips]


def _dev_gather_start(part, *, name):
    def start(src_ref, land_ref, ssem, rsem):
        x, y, c, chips = _place()
        for k, to in enumerate(_dev_peers(x, y, c, chips)):
            pltpu.make_async_remote_copy(
                src_ref=src_ref, dst_ref=land_ref.at[4 * x + 2 * y + c], send_sem=ssem.at[k], recv_sem=rsem.at[k],
                device_id=to, device_id_type=MESH).start()

    return _split_start(start, part, (N_DEV,) + part.shape, N_DEV - 1, name=name)


def _dev_gather_wait(parts, after, *, name):
    def wait(src_ref, land_ref, ssem_ref, rsem_ref):
        x, y, c, chips = _place()
        for k, (px, py, pc) in enumerate(_dev_peers(x, y, c, chips)):
            cp = pltpu.make_async_remote_copy(
                src_ref=src_ref, dst_ref=land_ref.at[4 * px + 2 * py + pc], send_sem=ssem_ref.at[k],
                recv_sem=rsem_ref.at[k], device_id=(x, y, c), device_id_type=MESH)
            cp.wait_send()
            cp.wait_recv()

    return _wait_call(wait, parts, after, name=name)[1]


def _sibling_share_halves(arrays, *, name):
    n = len(arrays)

    def body(*refs):
        bufs = refs[n:2 * n]
        send_sems, recv_sems = refs[2 * n:]
        x, y, c, _ = _place()
        copies = []
        for a in range(n):
            mine = bufs[a].at[:, _half_cols(bufs[a].shape[1], c)]
            cp = pltpu.make_async_remote_copy(
                src_ref=mine, dst_ref=mine, send_sem=send_sems.at[a], recv_sem=recv_sems.at[a],
                device_id=(x, y, 1 - c), device_id_type=MESH)
            cp.start()
            copies.append(cp)
        for a in range(n):
            theirs = bufs[a].at[:, _half_cols(bufs[a].shape[1], 1 - c)]
            pltpu.make_async_remote_copy(
                src_ref=theirs, dst_ref=theirs, send_sem=send_sems.at[a], recv_sem=recv_sems.at[a],
                device_id=(x, y, c), device_id_type=MESH).wait_recv()
        for cp in copies:
            cp.wait_send()

    return pl.pallas_call(
        body, name=name, in_specs=[HBM] * n, out_specs=[HBM] * n,
        out_shape=[jax.ShapeDtypeStruct(h.shape, h.dtype) for h in arrays],
        input_output_aliases={a: a for a in range(n)},
        scratch_shapes=[pltpu.SemaphoreType.DMA((n,)), pltpu.SemaphoreType.DMA((n,))],
    )(*arrays)


def _pack(arrays, rows_multiple=16, width=LANES):
    flat = jnp.concatenate([a.astype(F32).reshape(-1) for a in arrays])
    total = flat.shape[0]
    rows = -(-total // width)
    rows = -(-rows // rows_multiple) * rows_multiple
    return jnp.pad(flat, (0, rows * width - total)).reshape(rows, width)


def _unpack(buf, shapes):
    flat = buf.reshape(-1)
    out, off = [], 0
    for s in shapes:
        n = math.prod(s)
        out.append(flat[off:off + n].reshape(s))
        off += n
    return out


def kernel(x, norm_pre, norm_post, gla_w_in, gla_w_gate2, gla_b_gate, gla_o_gain, gla_w_out, sgu_w_in, sgu_ln_gain, sgu_ln_bias, sgu_w_spatial, sgu_b_spatial, sgu_w_out, loss_target, m_norm_pre, m_norm_post, m_gla_w_in, m_gla_w_gate2, m_gla_b_gate, m_gla_o_gain, m_gla_w_out, m_sgu_w_in, m_sgu_ln_gain, m_sgu_ln_bias, m_sgu_w_spatial, m_sgu_b_spatial, m_sgu_w_out, v_norm_pre, v_norm_post, v_gla_w_in, v_gla_w_gate2, v_gla_b_gate, v_gla_o_gain, v_gla_w_out, v_sgu_w_in, v_sgu_ln_gain, v_sgu_ln_bias, v_sgu_w_spatial, v_sgu_b_spatial, v_sgu_w_out):
    _, t, d = x.shape
    dk = d // 2
    ws = gla_w_in.shape[2]
    wp = -(-ws // LANES) * LANES
    lay = (ws, wp)
    chip =2 * lax.axis_index("x") + lax.axis_index("y")
    core = lax.axis_index("c")
    core_idx = core.astype(jnp.int32).reshape(1)
    others = jnp.arange(N_CHIPS - 1, dtype=jnp.int32)
    others = others + (others >= chip).astype(jnp.int32)
    slots = jnp.concatenate([chip.astype(jnp.int32).reshape(1), others, core_idx])

    x0 = x[0]
    target = loss_target[0]

    wt_in_g, mt_in_g, vt_in_g = gla_w_in[0].T, m_gla_w_in[0].T, v_gla_w_in[0].T

    small_shard = _pack([gla_w_gate2[0], sgu_ln_gain[0], sgu_ln_bias[0]], rows_multiple=8, width=2 * LANES)
    own = [small_shard, jnp.pad(wt_in_g.astype(BF16), ((0, wp - ws), (0, 0)))]
    in_flight, token = _gather_start(own, name="gather_start_a", relayed=(1,))

    def with_sibling_and_own(mine, land, name):
        return lax.dynamic_update_slice(_sibling_forward(land, name=name + "_share"), mine[None], (chip, 0, 0))

    h0 = _norm_pre(x0, norm_pre[0:1] + token[0:1, 0:1], name="pre0")
    g_small = with_sibling_and_own(*_gather_wait(in_flight[0], h0, name="w_small_wait"), "w_small")
    mine, land = _gather_wait(in_flight[1], [g_small, wt_in_g, mt_in_g, vt_in_g], name="w_gla_in_wait", ks=NEIGHBOURS)
    relay, token = _relay_start(land, name="w_gla_in_relay")
    own_later = [(p[0] + token[0, 0]).astype(BF16) for p in (gla_w_out, sgu_w_in, sgu_w_out)]
    in_flight_later, token = _gather_start(own_later, name="gather_start_b", after=[token])
    in_flight = in_flight + in_flight_later
    land = _relay_wait(relay, token, name="w_gla_in_relay_wait")
    wt_g = with_sibling_and_own(mine, land, "w_gla_in").reshape(N_CHIPS * wp, d)

    def behind(small, token):
        return small + token[0:1, 0:1]

    def arriving(i, after, name):
        mine, land = _gather_wait(in_flight[i], after, name=name + "_wait")
        crossing, token = _forward_start(land, name=name + "_share")
        return (mine, crossing), token

    def arrived(pending, after, name):
        mine, crossing = pending
        return lax.dynamic_update_slice(_forward_wait(crossing, after, name=name + "_share_wait"), mine[None],
                                        (chip, 0, 0))

    shard_shapes = [gla_w_gate2.shape[1:], sgu_ln_gain.shape[1:], sgu_ln_bias.shape[1:]]
    per_chip = [_unpack(g_small[j], shard_shapes) for j in range(N_CHIPS)]
    w2_full = jnp.concatenate([p[0] for p in per_chip], axis=1)
    ln_gain = jnp.concatenate([p[1] for p in per_chip], axis=0)[None, :]
    ln_bias = jnp.concatenate([p[2] for p in per_chip], axis=0)[None, :]
    w2p = jnp.pad(w2_full, ((0, LANES - GLA_GATE_RANK), (0, 0)))

    pos_chunk = jnp.arange(SGU_BLOCK) // CHUNK
    mask = pos_chunk[:, None] >= pos_chunk[None, :]
    ws_masked = jnp.where(mask[None], sgu_w_spatial[0], 0.0)
    ws_masked_t = ws_masked.transpose(0, 2, 1)
    bs_t = sgu_b_spatial[0].T

    proj0 = _matmul(h0, wt_g, mode="nt", out_dtype=F32, name="gla_in", tn=wp)
    pending, tok = arriving(2, proj0, "w_gla_out")
    o0, a0, s_before, s_final = _gla_fwd(proj0, w2p, behind(gla_b_gate, tok), gla_o_gain, lay, name="gla_scan")
    w_out_g = arrived(pending, a0, "w_gla_out").reshape(d, d)
    pending, tok = arriving(3, w_out_g, "w_sgu_in")
    y0 = _matmul(a0, w_out_g, mode="nn", out_dtype=F32, name="gla_out", after=tok)
    x1, h1 = _post_then_pre(x0, y0, norm_post[0:1], norm_pre[1:2], name="post0_pre1")
    g_wi_s = arrived(pending, h1, "w_sgu_in")
    pending, tok = arriving(4, g_wi_s, "w_sgu_out")
    proj1 = _matmul(h1, g_wi_s, mode="nn", out_dtype=F32, name="sgu_in", b_shards=True, after=tok)
    a1 = _sgu_fwd(proj1, ln_gain, ln_bias, ws_masked, bs_t, name="sgu_gate")
    w_out_s = arrived(pending, a1, "w_sgu_out").reshape(d, d)
    acts, tok = _to_sibling_start([(a1, 0), (a0, 0), (h1, 0), (h0, 1)], name="acts_to_sibling")
    a1, a0, h1, h0 = [f[4] for f in acts]
    y1 = _matmul(a1, w_out_s, mode="nn", out_dtype=F32, name="sgu_out", after=tok)
    loss_part, dx2, dy1, d_post1 = _loss_head(x1, y1, norm_post[1:2], target, name="loss_head")

    def pair_gradient(a_sent, b_sent, after, shards_on, name):
        a_me, a_sib = _from_sibling(a_sent, after, name=name + "_a_wait")
        b_me, b_sib = _from_sibling(b_sent, [a_sib] + list(after), name=name + "_b_wait")
        pair = _matmul_dw_pair(a_me, a_sib, b_me, b_sib, core_idx, shards_on=shards_on,
                               name=name + "_pair")
        return _scatter_start(pair, name=name + "_start")

    def reduced(flight, after, name):
        pair, landed = _scatter_wait(flight, after, name=name + "_wait")
        return _chip_sum(pair, landed, slots, name=name + "_sum")

    (dy1_sent,), tok = _to_sibling_start([(dy1, 1)], name="dy1_to_sibling")
    dy1 = dy1_sent[4]
    da1 = _matmul(dy1, w_out_s, mode="nt", out_dtype=F32, name="d_sgu_act", after=tok)
    fl_wo_s, tok = pair_gradient(acts[0], dy1_sent, [da1], "rows", "g_sgu_out")
    dproj1, d_ws, d_bs_t, d_lg, d_lb = _sgu_bwd(da1, proj1, ln_gain, behind(ln_bias, tok), ws_masked, ws_masked_t,
                                                bs_t, name="sgu_gate_bwd")
    (dp1_sent,), tok = _to_sibling_start([(dproj1, N_CHIPS)], name="dproj1_to_sibling")
    dproj1 = dp1_sent[4]
    dh1 = _matmul_nt_shards(dproj1, g_wi_s, out_dtype=F32, name="d_sgu_h", after=tok)
    fl_wi_s, tok = pair_gradient(acts[2], dp1_sent, [dh1], "cols", "g_sgu_in")
    dx1, dy0, d_pre1, d_post0 = _mid_bwd(dx2, dh1, x1, behind(norm_pre[1:2], tok), y0, norm_post[0:1],
                                         name="pre1_post0_bwd")
    (dy0_sent,), tok = _to_sibling_start([(dy0, 1)], name="dy0_to_sibling")
    dy0 = dy0_sent[4]
    da0 = _matmul(dy0, w_out_g, mode="nt", out_dtype=F32, name="d_gla_act", after=tok)
    fl_wo_g, tok = pair_gradient(acts[1], dy0_sent, [da0], "rows", "g_gla_out")
    dproj0, d_og, d_bg, d_w2p = _gla_bwd(da0, o0, proj0, w2p, behind(gla_b_gate, tok), gla_o_gain, s_before, s_final,
                                         lay, name="gla_scan_bwd")
    early_shapes = [norm_post.shape, gla_b_gate.shape, gla_o_gain.shape, sgu_w_spatial.shape, sgu_b_spatial.shape,
                    (1, GLA_GATE_RANK, dk), (1, d), (1, d), (1, LANES)]
    early_part = _pack([jnp.concatenate([d_post0, d_post1], axis=0), d_bg, d_og, jnp.where(mask[None], d_ws, 0.0)[None],
                        d_bs_t.T[None], d_w2p[:GLA_GATE_RANK][None], d_lg, d_lb, loss_part])
    early_flight, tok = _dev_gather_start(early_part, name="small_early_start")
    (dp0_sent,), tok_sent = _to_sibling_start([(dproj0, 0)], name="dproj0_to_sibling")
    dproj0 = dp0_sent[4]
    b_me, b_sib = _from_sibling(acts[3], [tok, tok_sent], name="g_gla_in_b_wait")
    own_part = _matmul_dw_own(dproj0, b_me, core_idx, name="g_gla_in_own", after=tok_sent)
    dproj0, a_sib = _from_sibling(dp0_sent, own_part, name="g_gla_in_a_wait")
    fl_wi_g, tok_scatter = [], None
    for p in range(2):
        pair = _matmul_dw_add(own_part, a_sib, b_sib, part=(p, 2), name=f"g_gla_in_pair{p}", after=tok_scatter)
        flight, tok_scatter = _scatter_start(pair, name=f"g_gla_in_start{p}")
        fl_wi_g.append(flight)
    dh0 = _matmul(dproj0, wt_g, mode="nn", out_dtype=F32, name="d_gla_h", after=tok_scatter)
    r_wo_s = reduced(fl_wo_s, dh0, "g_sgu_out")
    r_wi_s = reduced(fl_wi_s, r_wo_s, "g_sgu_in")
    r_wo_g = reduced(fl_wo_g, r_wi_s, "g_gla_out")
    sharing, tok = _share_start([r_wo_s, r_wi_s, r_wo_g], name="grads_share_a")
    grad_x, d_pre0 = _first_bwd(dx1, dh0, x0, behind(norm_pre[0:1], tok), name="pre0_bwd")

    late_part = _pack([jnp.concatenate([d_pre0, d_pre1], axis=0)])
    late_flight, tok = _dev_gather_start(late_part, name="small_late_start")

    def big_update(w, g, m, v, name, after=None):
        return [u[None] for u in _adamw(w[0], g, m[0], v[0], name=name, after=after)]

    g_wo_sgu, g_wi_sgu, g_wo_gla = _share_wait(sharing, [grad_x, tok], name="grads_share_a_wait")
    u_wi_sgu = big_update(sgu_w_in, g_wi_sgu, m_sgu_w_in, v_sgu_w_in, "adamw_sgu_w_in")
    u_wo_gla = big_update(gla_w_out, g_wo_gla, m_gla_w_out, v_gla_w_out, "adamw_gla_w_out", after=u_wi_sgu[1])

    r_wi_g, behind_this = None, u_wo_gla[1]
    for p, flight in enumerate(fl_wi_g):
        pair, landed = _scatter_wait(flight, behind_this, name=f"g_gla_in_wait{p}")
        r_wi_g = behind_this = _chip_sum(pair, landed, slots, part=(p, 2), into=r_wi_g, name=f"g_gla_in_sum{p}")
    gt_wi_gla, = _sibling_share_halves([r_wi_g], name="grads_share_b")
    u_wi_gla_t = _adamw(wt_in_g, gt_wi_gla, mt_in_g, vt_in_g, name="adamw_gla_w_in")
    u_wi_gla = [u.T[None] for u in u_wi_gla_t]
    u_wo_sgu = big_update(sgu_w_out, g_wo_sgu, m_sgu_w_out, v_sgu_w_out, "adamw_sgu_w_out", after=u_wi_gla_t[1])

    def summed_over_devices(part, flight, after, shapes, name):
        land = _dev_gather_wait(flight, after, name=name + "_wait")
        every = lax.dynamic_update_slice(land, part[None], (2 * chip + core, 0, 0))
        return _unpack(_stack_sum(every, name=name + "_sum"), shapes)

    (g_post, g_bg, g_og, g_wsp, g_bsp, g_w2_full, g_lg_full, g_lb_full, loss_vec) = summed_over_devices(
        early_part, early_flight, u_wo_sgu[1], early_shapes, "small_early")
    g_pre, = summed_over_devices(late_part, late_flight, loss_vec, [norm_pre.shape], "small_late")
    loss = loss_vec[0, 0]
    g_w2 = lax.dynamic_slice_in_dim(g_w2_full, chip * (dk // N_CHIPS), dk // N_CHIPS, axis=2)
    g_lg = lax.dynamic_slice_in_dim(g_lg_full, chip * (d // N_CHIPS), d // N_CHIPS, axis=1)
    g_lb = lax.dynamic_slice_in_dim(g_lb_full, chip * (d // N_CHIPS), d // N_CHIPS, axis=1)

    small_w = [norm_pre, norm_post, gla_b_gate, gla_o_gain, sgu_w_spatial, sgu_b_spatial, gla_w_gate2, sgu_ln_gain,
               sgu_ln_bias]
    small_g = [g_pre, g_post, g_bg, g_og, g_wsp, g_bsp, g_w2, g_lg, g_lb]
    small_m = [m_norm_pre, m_norm_post, m_gla_b_gate, m_gla_o_gain, m_sgu_w_spatial, m_sgu_b_spatial, m_gla_w_gate2,
               m_sgu_ln_gain, m_sgu_ln_bias]
    small_v = [v_norm_pre, v_norm_post, v_gla_b_gate, v_gla_o_gain, v_sgu_w_spatial, v_sgu_b_spatial, v_gla_w_gate2,
               v_sgu_ln_gain, v_sgu_ln_bias]
    own_shapes = [w.shape for w in small_w]
    _, s_dl, s_m, s_v = _adamw(_pack(small_w), _pack(small_g), _pack(small_m), _pack(small_v), name="adamw_small")
    dl_s, m_s, v_s = _unpack(s_dl, own_shapes), _unpack(s_m, own_shapes), _unpack(s_v, own_shapes)

    def ordered(small, kind):
        pre, post, bg, og, wsp, bsp, w2, lg, lb = small
        return [pre, post, u_wi_gla[kind], w2, bg, og, u_wo_gla[kind], u_wi_sgu[kind], lg, lb, wsp, bsp, u_wo_sgu[kind]]

    return (loss, grad_x[None], *ordered(small_g, 0), *ordered(dl_s, 1), *ordered(m_s, 2), *ordered(v_s, 3))
```

```python
import math

import jax
import jax.numpy as jnp
from jax import lax
from jax.experimental import pallas as pl
from jax.experimental.pallas import tpu as pltpu

F32 = jnp.float32
BF16 = jnp.bfloat16
MESH = pl.DeviceIdType.MESH

EPS = 1e-6
CHUNK = 64
GLA_HEADS = 4
GLA_GATE_RANK = 16
GLA_TAU = 16.0
SGU_BLOCK = 128
SGU_GROUPS = 8
N_CHIPS = 4
N_DEV = 8
LANES = 128

ADAM_LR = 0.001
ADAM_B1 = 0.9
ADAM_B2 = 0.999
ADAM_EPS = 1e-08
ADAM_WD = 0.01
ADAM_STEP = 10

VMEM_LIMIT = 56 * 1024 * 1024


def _cparams(sem=None):
    return pltpu.CompilerParams(dimension_semantics=sem, vmem_limit_bytes=VMEM_LIMIT)


def _pick(n, cap, unit=LANES):
    best = None
    for t in range(unit, min(n, cap) + 1, unit):
        if n % t == 0:
            best = t
    assert best is not None, (n, cap, unit)
    return best


def _dot(a, b, dims):
    return lax.dot_general(a, b, (dims, ((), ())), preferred_element_type=F32)


def _dot_nn(a, b):
    return _dot(a, b, ((1,), (0,)))


def _dot_nt(a, b):
    return _dot(a, b, ((1,), (1,)))


def _dot_tn(a, b):
    return _dot(a, b, ((0,), (0,)))


def _matmul(a, b, *, mode, out_dtype, name, tm=1024, tn=512, b_shards=False, after=None):
    M, K = a.shape
    if b_shards:
        ns, Kb, bc = b.shape
        N, tn = ns * bc, _pick(bc, tn)
        per = bc // tn
        b_spec = pl.BlockSpec((None, K, tn), lambda i, j: (j // per, 0, j % per))
    elif mode == "nt":
        N, Kb = b.shape
        tn = _pick(N, tn)
        b_spec = pl.BlockSpec((tn, K), lambda i, j: (j, 0))
    else:
        Kb, N = b.shape
        tn = _pick(N, tn)
        b_spec = pl.BlockSpec((K, tn), lambda i, j: (0, j))
    assert K == Kb and a.dtype == b.dtype == BF16, (a.shape, b.shape, mode)
    tm = _pick(M, tm)
    dims = ((1,), (1,)) if mode == "nt" else ((1,), (0,))
    extra_specs, extra_args = ([], []) if after is None else ([pl.BlockSpec(memory_space=pl.ANY)], [after])

    def body(a_ref, b_ref, *rest):
        rest[-1][...] = _dot(a_ref[...], b_ref[...], dims).astype(out_dtype)

    return pl.pallas_call(
        body, name=name, grid=(M // tm, N // tn),
        in_specs=[pl.BlockSpec((tm, K), lambda i, j: (i, 0)), b_spec] + extra_specs,
        out_specs=pl.BlockSpec((tm, tn), lambda i, j: (i, j)), out_shape=jax.ShapeDtypeStruct((M, N), out_dtype),
        compiler_params=_cparams(("parallel", "parallel")),
    )(a, b, *extra_args)


def _matmul_nt_shards(a, b, *, out_dtype, name, tm=1024, tn=512, after=None):
    M, K = a.shape
    ns, N, kc = b.shape
    assert K == ns * kc
    tm, tn = _pick(M, tm), _pick(N, tn)

    def body(a_ref, *rest):
        b_refs, o_ref = rest[:ns], rest[ns + (after is not None)]
        acc = _dot_nt(a_ref[:, 0:kc], b_refs[0][...])
        for j in range(1, ns):
            acc += _dot_nt(a_ref[:, j * kc:(j + 1) * kc], b_refs[j][...])
        o_ref[...] = acc.astype(out_dtype)

    def shard(j):
        return pl.BlockSpec((None, tn, kc), lambda i, n: (j, n, 0))

    extra_specs, extra_args = ([], []) if after is None else ([pl.BlockSpec(memory_space=pl.ANY)], [after])
    return pl.pallas_call(
        body, name=name, grid=(M // tm, N // tn),
        in_specs=[pl.BlockSpec((tm, K), lambda i, n: (i, 0))] + [shard(j) for j in range(ns)] + extra_specs,
        out_specs=pl.BlockSpec((tm, tn), lambda i, n: (i, n)), out_shape=jax.ShapeDtypeStruct((M, N), out_dtype),
        compiler_params=_cparams(("parallel", "parallel")),
    )(a, *([b] * ns), *extra_args)


def _rstd(x):
    return lax.rsqrt(jnp.mean(x * x, axis=-1, keepdims=True) + EPS)


def _row_spec(tr, d):
    return pl.BlockSpec((tr, d), lambda i: (i, 0))


def _vec_spec(d):
    return pl.BlockSpec((1, d), lambda i: (0, 0))


def _acc_rows(ref, i, val, cols=slice(None)):
    @pl.when(i == 0)
    def _():
        ref[:, cols] = val

    @pl.when(i > 0)
    def _():
        ref[:, cols] += val


def _norm_pre(x, gain, *, name, tr=256):
    t, d = x.shape
    tr = _pick(t, tr, 8)

    def body(x_ref, g_ref, h_ref):
        xv = x_ref[...]
        h_ref[...] = (xv * _rstd(xv) * g_ref[...]).astype(BF16)

    return pl.pallas_call(
        body, name=name, grid=(t // tr,), in_specs=[_row_spec(tr, d), _vec_spec(d)], out_specs=_row_spec(tr, d),
        out_shape=jax.ShapeDtypeStruct((t, d), BF16), compiler_params=_cparams(("parallel",)),
    )(x, gain)


def _post_then_pre(x, y, post_gain, pre_gain, *, name, tr=256):
    t, d = x.shape
    tr = _pick(t, tr, 8)

    def body(x_ref, y_ref, pg_ref, ng_ref, xn_ref, h_ref):
        yv = y_ref[...]
        xn = x_ref[...] + yv * _rstd(yv) * pg_ref[...]
        xn_ref[...] = xn
        h_ref[...] = (xn * _rstd(xn) * ng_ref[...]).astype(BF16)

    return pl.pallas_call(
        body, name=name, grid=(t // tr,),
        in_specs=[_row_spec(tr, d), _row_spec(tr, d), _vec_spec(d), _vec_spec(d)],
        out_specs=[_row_spec(tr, d), _row_spec(tr, d)],
        out_shape=[jax.ShapeDtypeStruct((t, d), F32), jax.ShapeDtypeStruct((t, d), BF16)],
        compiler_params=_cparams(("parallel",)),
    )(x, y, post_gain, pre_gain)


def _norm_bwd(dy, n, r, gain):
    dn = dy * gain
    return r * (dn - n * jnp.mean(dn * n, axis=-1, keepdims=True))


def _loss_head(x, y, post_gain, target, *, name, tr=256):
    t, d = x.shape
    tr = _pick(t, tr, 8)

    def body(x_ref, y_ref, pg_ref, t_ref, loss_ref, dx_ref, dy_ref, dpg_ref):
        i = pl.program_id(0)
        yv = y_ref[...]
        r = _rstd(yv)
        n = yv * r
        err = x_ref[...] + n * pg_ref[...] - t_ref[...]
        dx = err * (1.0 / d)
        dx_ref[...] = dx
        part = 0.5 * jnp.sum(jnp.mean(err * err, axis=-1, keepdims=True), axis=0, keepdims=True)
        _acc_rows(loss_ref, i, jnp.broadcast_to(part, (1, LANES)))
        _acc_rows(dpg_ref, i, jnp.sum(dx * n, axis=0, keepdims=True))
        dy_ref[...] = _norm_bwd(dx, n, r, pg_ref[...]).astype(BF16)

    return pl.pallas_call(
        body, name=name, grid=(t // tr,),
        in_specs=[_row_spec(tr, d), _row_spec(tr, d), _vec_spec(d), _row_spec(tr, d)],
        out_specs=[_vec_spec(LANES), _row_spec(tr, d), _row_spec(tr, d), _vec_spec(d)],
        out_shape=[jax.ShapeDtypeStruct((1, LANES), F32), jax.ShapeDtypeStruct((t, d), F32),
                   jax.ShapeDtypeStruct((t, d), BF16), jax.ShapeDtypeStruct((1, d), F32)],
        compiler_params=_cparams(("arbitrary",)),
    )(x, y, post_gain, target)


def _mid_bwd(dx_out, dh, x, pre_gain, y_prev, post_gain_prev, *, name, tr=256):
    t, d = x.shape
    tr = _pick(t, tr, 8)

    def body(dxo_ref, dh_ref, x_ref, ng_ref, y_ref, pg_ref, dx_ref, dy_ref, dng_ref, dpg_ref):
        i = pl.program_id(0)
        xv = x_ref[...]
        r = _rstd(xv)
        xh = xv * r
        dhv = dh_ref[...]
        _acc_rows(dng_ref, i, jnp.sum(dhv * xh, axis=0, keepdims=True))
        dx = dxo_ref[...] + _norm_bwd(dhv, xh, r, ng_ref[...])
        dx_ref[...] = dx
        yv = y_ref[...]
        ry = _rstd(yv)
        n = yv * ry
        _acc_rows(dpg_ref, i, jnp.sum(dx * n, axis=0, keepdims=True))
        dy_ref[...] = _norm_bwd(dx, n, ry, pg_ref[...]).astype(BF16)

    return pl.pallas_call(
        body, name=name, grid=(t // tr,),
        in_specs=[_row_spec(tr, d), _row_spec(tr, d), _row_spec(tr, d), _vec_spec(d), _row_spec(tr, d), _vec_spec(d)],
        out_specs=[_row_spec(tr, d), _row_spec(tr, d), _vec_spec(d), _vec_spec(d)],
        out_shape=[jax.ShapeDtypeStruct((t, d), F32), jax.ShapeDtypeStruct((t, d), BF16),
                   jax.ShapeDtypeStruct((1, d), F32), jax.ShapeDtypeStruct((1, d), F32)],
        compiler_params=_cparams(("arbitrary",)),
    )(dx_out, dh, x, pre_gain, y_prev, post_gain_prev)


def _first_bwd(dx_out, dh, x, pre_gain, *, name, tr=256):
    t, d = x.shape
    tr = _pick(t, tr, 8)

    def body(dxo_ref, dh_ref, x_ref, ng_ref, dx_ref, dng_ref):
        i = pl.program_id(0)
        xv = x_ref[...]
        r = _rstd(xv)
        xh = xv * r
        dhv = dh_ref[...]
        _acc_rows(dng_ref, i, jnp.sum(dhv * xh, axis=0, keepdims=True))
        dx_ref[...] = dxo_ref[...] + _norm_bwd(dhv, xh, r, ng_ref[...])

    return pl.pallas_call(
        body, name=name, grid=(t // tr,),
        in_specs=[_row_spec(tr, d), _row_spec(tr, d), _row_spec(tr, d), _vec_spec(d)],
        out_specs=[_row_spec(tr, d), _vec_spec(d)],
        out_shape=[jax.ShapeDtypeStruct((t, d), F32), jax.ShapeDtypeStruct((1, d), F32)],
        compiler_params=_cparams(("arbitrary",)),
    )(dx_out, dh, x, pre_gain)


def _sigmoid(x):
    return 1.0 / (1.0 + jnp.exp(-x))


def _log_sigmoid(x):
    return jnp.minimum(x, 0.0) - jnp.log(1.0 + jnp.exp(-jnp.abs(x)))


_GELU_C = math.sqrt(2.0 / math.pi)


_GELU_A = 0.044715


def _gelu_parts(x, with_grad=True):
    x2 = x * x
    h = 0.5 * jnp.tanh(x * (_GELU_C + (_GELU_C * _GELU_A) * x2)) + 0.5
    val = x * h
    if not with_grad:
        return val, None
    return val, h * (1.0 + (1.0 - h) * (x * (2.0 * _GELU_C + (6.0 * _GELU_C * _GELU_A) * x2)))


def _split3(x):
    hi = x.astype(BF16)
    r1 = x - hi.astype(F32)
    mid = r1.astype(BF16)
    lo = (r1 - mid.astype(F32)).astype(BF16)
    return hi, mid, lo


def _tri_matmul(tri_bf16, x):
    hi, mid, lo = _split3(x)
    return _dot_nn(tri_bf16, hi) + _dot_nn(tri_bf16, mid) + _dot_nn(tri_bf16, lo)


def _gla_dims(d):
    dk, dv = d // 2, d
    return dk, dv, dk // GLA_HEADS, dv // GLA_HEADS


def _col_pieces(a, b, lay):
    ws, wp = lay
    out = []
    while a < b:
        j = a // ws
        end = min(b, (j + 1) * ws)
        out.append((j * wp + a - j * ws, end - a))
        a = end
    return out


def _load_cols(ref, a, b, lay):
    parts = [ref[:, s:s + n] for s, n in _col_pieces(a, b, lay)]
    return parts[0] if len(parts) == 1 else jnp.concatenate(parts, axis=1)


def _store_cols(ref, a, val, lay):
    off = 0
    for s, n in _col_pieces(a, a + val.shape[1], lay):
        ref[:, s:s + n] = val[:, off:off + n]
        off += n


def _gate_window(c_r, lay):
    (start, _), = _col_pieces(c_r, c_r + GLA_GATE_RANK, lay)
    assert (start % lay[1]) + LANES <= lay[1]
    return slice(start, start + LANES)


def _gla_gates(glr, k, w2_ref, b_ref):
    z = _dot_nn(glr.astype(BF16), w2_ref[...].astype(BF16)) + b_ref[...]
    la = _log_sigmoid(z) * (1.0 / GLA_TAU)
    row = lax.broadcasted_iota(jnp.int32, (CHUNK, CHUNK), 0)
    col = lax.broadcasted_iota(jnp.int32, (CHUNK, CHUNK), 1)
    incl = (row >= col).astype(BF16)
    bcum = _tri_matmul(incl, la)
    b_end = bcum[CHUNK - 1:CHUNK, :]
    e_rest = jnp.exp(b_end - bcum)
    return z, e_rest, k * e_rest, jnp.exp(b_end)


def _gla_fwd(proj, w2p, b_gate, o_gain, lay, *, name):
    t, wcols = proj.shape
    d = o_gain.shape[1]
    dk, dv, dkh, dvh = _gla_dims(d)
    nc = t // CHUNK
    c_k, c_v, c_g, c_r = dk, 2 * dk, 2 * dk + dv, 2 * dk + 2 * dv
    scale = dkh ** -0.5

    def body(p_ref, w2_ref, b_ref, og_ref, o_ref, a_ref, sb_ref, sfin_ref, s_ref):
        i = pl.program_id(0)

        @pl.when(i == 0)
        def _():
            s_ref[...] = jnp.zeros_like(s_ref)

        q = _load_cols(p_ref, 0, dk, lay) * scale
        k = _load_cols(p_ref, c_k, c_k + dk, lay)
        glr = p_ref[:, _gate_window(c_r, lay)]
        _, _, kdec, decay = _gla_gates(glr, k, w2_ref, b_ref)
        for h in range(GLA_HEADS):
            ks = slice(h * dkh, (h + 1) * dkh)
            vs = slice(h * dvh, (h + 1) * dvh)
            v_h = _load_cols(p_ref, c_v + h * dvh, c_v + (h + 1) * dvh, lay)
            g_h = _load_cols(p_ref, c_g + h * dvh, c_g + (h + 1) * dvh, lay)
            s_old = s_ref[h]
            sb_ref[0, h] = s_old
            s_new = s_old * decay[:, ks] + _dot_tn(v_h.astype(BF16), kdec[:, ks].astype(BF16))
            s_ref[h] = s_new
            o_h = _dot_nt(q[:, ks].astype(BF16), s_new.astype(BF16))
            o_ref[:, vs] = o_h
            on = o_h * _rstd(o_h)
            a_ref[:, vs] = (on * og_ref[:, vs] * (g_h * _sigmoid(g_h))).astype(BF16)

        @pl.when(i == nc - 1)
        def _():
            sfin_ref[...] = s_ref[...]

    full = lambda *shape: pl.BlockSpec(shape, lambda i: (0,) * len(shape))
    return pl.pallas_call(
        body, name=name, grid=(nc,),
        in_specs=[pl.BlockSpec((CHUNK, wcols), lambda i: (i, 0)), full(LANES, dk), full(1, dk), full(1, dv)],
        out_specs=[pl.BlockSpec((CHUNK, dv), lambda i: (i, 0)), pl.BlockSpec((CHUNK, dv), lambda i: (i, 0)),
                   pl.BlockSpec((1, GLA_HEADS, dvh, dkh), lambda i: (i, 0, 0, 0)), full(GLA_HEADS, dvh, dkh)],
        out_shape=[jax.ShapeDtypeStruct((t, dv), F32), jax.ShapeDtypeStruct((t, dv), BF16),
                   jax.ShapeDtypeStruct((nc, GLA_HEADS, dvh, dkh), F32),
                   jax.ShapeDtypeStruct((GLA_HEADS, dvh, dkh), F32)],
        scratch_shapes=[pltpu.VMEM((GLA_HEADS, dvh, dkh), F32)],
        compiler_params=_cparams(("arbitrary",)),
    )(proj, w2p, b_gate, o_gain)


def _gla_bwd(da, o, proj, w2p, b_gate, o_gain, s_before, s_final, lay, *, name):
    t, wcols = proj.shape
    d = o_gain.shape[1]
    dk, dv, dkh, dvh = _gla_dims(d)
    nc = t // CHUNK
    c_k, c_v, c_g, c_r = dk, 2 * dk, 2 * dk + dv, 2 * dk + 2 * dv
    scale = dkh ** -0.5

    def body(da_ref, o_ref, p_ref, w2_ref, b_ref, og_ref, sb_ref, sfin_ref,
             dp_ref, dog_ref, db_ref, dw2_ref, s_ref, gc_ref, dkd_ref):
        i = pl.program_id(0)

        @pl.when(i == 0)
        def _():
            s_ref[...] = sfin_ref[...]
            gc_ref[...] = jnp.zeros_like(gc_ref)

        ws, wp = lay
        for j in range(N_CHIPS):
            dp_ref[:, j * wp + ws:(j + 1) * wp] = jnp.zeros((CHUNK, wp - ws), BF16)
        q = _load_cols(p_ref, 0, dk, lay) * scale
        k = _load_cols(p_ref, c_k, c_k + dk, lay)
        glr = p_ref[:, _gate_window(c_r, lay)]
        z, e_rest, kdec, decay = _gla_gates(glr, k, w2_ref, b_ref)
        ddecay = []
        for h in range(GLA_HEADS):
            ks = slice(h * dkh, (h + 1) * dkh)
            vs = slice(h * dvh, (h + 1) * dvh)
            v_h = _load_cols(p_ref, c_v + h * dvh, c_v + (h + 1) * dvh, lay)
            g_h = _load_cols(p_ref, c_g + h * dvh, c_g + (h + 1) * dvh, lay)
            da_h = da_ref[:, vs]
            o_h = o_ref[:, vs]
            og_h = og_ref[:, vs]
            r = _rstd(o_h)
            on = o_h * r
            sg = _sigmoid(g_h)
            silu = g_h * sg
            _acc_rows(dog_ref, i, jnp.sum(da_h * silu * on, axis=0, keepdims=True), vs)
            _store_cols(dp_ref, c_g + h * dvh, (da_h * (on * og_h) * (sg * (1.0 + g_h * (1.0 - sg)))).astype(BF16),
                        lay)
            don = da_h * silu * og_h
            do_h = (r * (don - on * jnp.mean(don * on, axis=-1, keepdims=True))).astype(BF16)
            s_cur = s_ref[h]
            _store_cols(dp_ref, h * dkh, (_dot_nn(do_h, s_cur.astype(BF16)) * scale).astype(BF16), lay)
            g_tot = gc_ref[h] + _dot_tn(do_h, q[:, ks].astype(BF16))
            g_bf = g_tot.astype(BF16)
            dkd_ref[:, ks] = _dot_nn(v_h.astype(BF16), g_bf)
            _store_cols(dp_ref, c_v + h * dvh, _dot_nt(kdec[:, ks].astype(BF16), g_bf).astype(BF16), lay)
            s_prev = sb_ref[0, h]
            ddecay.append(jnp.sum(g_tot * s_prev, axis=0, keepdims=True))
            gc_ref[h] = g_tot * decay[:, ks]
            s_ref[h] = s_prev
        dkdec = dkd_ref[...]
        _store_cols(dp_ref, c_k, (dkdec * e_rest).astype(BF16), lay)
        d_e = dkdec * kdec
        row = lax.broadcasted_iota(jnp.int32, (CHUNK, CHUNK), 0)
        col = lax.broadcasted_iota(jnp.int32, (CHUNK, CHUNK), 1)
        excl = (row > col).astype(BF16)
        dla = jnp.concatenate(ddecay, axis=1) * decay + _tri_matmul(excl, d_e)
        dz = dla * (1.0 / GLA_TAU) * (1.0 - _sigmoid(z))
        _acc_rows(db_ref, i, jnp.sum(dz, axis=0, keepdims=True))
        dz_bf = dz.astype(BF16)
        dw2 = _dot_tn(glr.astype(BF16), dz_bf)

        @pl.when(i == 0)
        def _():
            dw2_ref[...] = dw2

        @pl.when(i > 0)
        def _():
            dw2_ref[...] += dw2

        dp_ref[:, _gate_window(c_r, lay)] = _dot_nt(dz_bf, w2_ref[...].astype(BF16)).astype(BF16)

    rev = lambda i: (nc - 1 - i, 0)
    full = lambda *shape: pl.BlockSpec(shape, lambda i: (0,) * len(shape))
    return pl.pallas_call(
        body, name=name, grid=(nc,),
        in_specs=[pl.BlockSpec((CHUNK, dv), rev), pl.BlockSpec((CHUNK, dv), rev), pl.BlockSpec((CHUNK, wcols), rev),
                  full(LANES, dk), full(1, dk), full(1, dv),
                  pl.BlockSpec((1, GLA_HEADS, dvh, dkh), lambda i: (nc - 1 - i, 0, 0, 0)), full(GLA_HEADS, dvh, dkh)],
        out_specs=[pl.BlockSpec((CHUNK, wcols), rev), full(1, dv), full(1, dk), full(LANES, dk)],
        out_shape=[jax.ShapeDtypeStruct((t, wcols), BF16), jax.ShapeDtypeStruct((1, dv), F32),
                   jax.ShapeDtypeStruct((1, dk), F32), jax.ShapeDtypeStruct((LANES, dk), F32)],
        scratch_shapes=[pltpu.VMEM((GLA_HEADS, dvh, dkh), F32), pltpu.VMEM((GLA_HEADS, dvh, dkh), F32),
                        pltpu.VMEM((CHUNK, dk), F32)],
        compiler_params=_cparams(("arbitrary",)),
    )(da, o, proj, w2p, b_gate, o_gain, s_before, s_final)


def _sgu_mid(p_ref, lg_ref, lb_ref, ws_ref, bst_ref, w, with_grad=True):
    gd = w // SGU_GROUPS
    u_act, du_fac = _gelu_parts(p_ref[:, 0:w], with_grad)
    vf, dv_fac = _gelu_parts(p_ref[:, w:2 * w], with_grad)
    mu = jnp.mean(vf, axis=-1, keepdims=True)
    cen = vf - mu
    rstd = lax.rsqrt(jnp.mean(cen * cen, axis=-1, keepdims=True) + EPS)
    xh = cen * rstd
    vn = (xh * lg_ref[...] + lb_ref[...]).astype(BF16)
    vs = [_dot_nn(ws_ref[g].astype(BF16), vn[:, g * gd:(g + 1) * gd]) + bst_ref[:, g:g + 1]
          for g in range(SGU_GROUPS)]
    return u_act, du_fac, dv_fac, rstd, xh, vn, vs


def _sgu_fwd(proj, ln_gain, ln_bias, ws_masked, bs_t, *, name):
    t, w3 = proj.shape
    w = w3 // 3
    gd = w // SGU_GROUPS
    nb = t // SGU_BLOCK

    def body(p_ref, lg_ref, lb_ref, ws_ref, bst_ref, a_ref):
        u_act, _, _, _, _, _, vs = _sgu_mid(p_ref, lg_ref, lb_ref, ws_ref, bst_ref, w, with_grad=False)
        for g in range(SGU_GROUPS):
            cs = slice(g * gd, (g + 1) * gd)
            gate = p_ref[:, 2 * w + g * gd:2 * w + (g + 1) * gd]
            a_ref[:, cs] = (u_act[:, cs] * vs[g] * (gate * _sigmoid(gate))).astype(BF16)

    full = lambda *shape: pl.BlockSpec(shape, lambda i: (0,) * len(shape))
    return pl.pallas_call(
        body, name=name, grid=(nb,),
        in_specs=[pl.BlockSpec((SGU_BLOCK, w3), lambda i: (i, 0)), full(1, w), full(1, w),
                  full(SGU_GROUPS, SGU_BLOCK, SGU_BLOCK), full(SGU_BLOCK, SGU_GROUPS)],
        out_specs=pl.BlockSpec((SGU_BLOCK, w), lambda i: (i, 0)),
        out_shape=jax.ShapeDtypeStruct((t, w), BF16),
        compiler_params=_cparams(("parallel",)),
    )(proj, ln_gain, ln_bias, ws_masked, bs_t)


def _sgu_bwd(da, proj, ln_gain, ln_bias, ws_masked, ws_masked_t, bs_t, *, name):
    t, w3 = proj.shape
    w = w3 // 3
    gd = w // SGU_GROUPS
    nb = t // SGU_BLOCK

    def body(da_ref, p_ref, lg_ref, lb_ref, ws_ref, wst_ref, bst_ref, dp_ref, dws_ref, dbst_ref, dlg_ref, dlb_ref,
             dvn_ref):
        i = pl.program_id(0)
        u_act, du_fac, dv_fac, rstd, xh, vn, vs = _sgu_mid(p_ref, lg_ref, lb_ref, ws_ref, bst_ref, w)
        for g in range(SGU_GROUPS):
            cs = slice(g * gd, (g + 1) * gd)
            gate = p_ref[:, 2 * w + g * gd:2 * w + (g + 1) * gd]
            sg = _sigmoid(gate)
            silu = gate * sg
            da_g = da_ref[:, cs]
            ua_g = u_act[:, cs]
            dp_ref[:, cs] = (da_g * vs[g] * silu * du_fac[:, cs]).astype(BF16)
            dp_ref[:, 2 * w + g * gd:2 * w + (g + 1) * gd] = (
                da_g * ua_g * vs[g] * (sg * (1.0 + gate * (1.0 - sg)))).astype(BF16)
            dvs = da_g * ua_g * silu
            dvs_bf = dvs.astype(BF16)
            dvn_ref[:, cs] = _dot_nn(wst_ref[g].astype(BF16), dvs_bf)
            dws = _dot_nt(dvs_bf, vn[:, cs])
            dbs = jnp.sum(dvs, axis=1, keepdims=True)

            @pl.when(i == 0)
            def _():
                dws_ref[g] = dws
                dbst_ref[:, g:g + 1] = dbs

            @pl.when(i > 0)
            def _():
                dws_ref[g] += dws
                dbst_ref[:, g:g + 1] += dbs

        dvn = dvn_ref[...]
        _acc_rows(dlg_ref, i, jnp.sum(dvn * xh, axis=0, keepdims=True))
        _acc_rows(dlb_ref, i, jnp.sum(dvn, axis=0, keepdims=True))
        dxh = dvn * lg_ref[...]
        dvf = rstd * (dxh - jnp.mean(dxh, axis=-1, keepdims=True)
                      - xh * jnp.mean(dxh * xh, axis=-1, keepdims=True))
        dp_ref[:, w:2 * w] = (dvf * dv_fac).astype(BF16)

    full = lambda *shape: pl.BlockSpec(shape, lambda i: (0,) * len(shape))
    return pl.pallas_call(
        body, name=name, grid=(nb,),
        in_specs=[pl.BlockSpec((SGU_BLOCK, w), lambda i: (i, 0)), pl.BlockSpec((SGU_BLOCK, w3), lambda i: (i, 0)),
                  full(1, w), full(1, w), full(SGU_GROUPS, SGU_BLOCK, SGU_BLOCK),
                  full(SGU_GROUPS, SGU_BLOCK, SGU_BLOCK), full(SGU_BLOCK, SGU_GROUPS)],
        out_specs=[pl.BlockSpec((SGU_BLOCK, w3), lambda i: (i, 0)), full(SGU_GROUPS, SGU_BLOCK, SGU_BLOCK),
                   full(SGU_BLOCK, SGU_GROUPS), full(1, w), full(1, w)],
        out_shape=[jax.ShapeDtypeStruct((t, w3), BF16), jax.ShapeDtypeStruct((SGU_GROUPS, SGU_BLOCK, SGU_BLOCK), F32),
                   jax.ShapeDtypeStruct((SGU_BLOCK, SGU_GROUPS), F32), jax.ShapeDtypeStruct((1, w), F32),
                   jax.ShapeDtypeStruct((1, w), F32)],
        scratch_shapes=[pltpu.VMEM((SGU_BLOCK, w), F32)],
        compiler_params=_cparams(("arbitrary",)),
    )(da, proj, ln_gain, ln_bias, ws_masked, ws_masked_t, bs_t)


def _tile2d(rows, cols, block_bytes, row_unit):
    if rows % row_unit == 0:
        return _pick(rows, max(row_unit, block_bytes // (4 * cols)), row_unit), cols
    return rows, _pick(cols, max(LANES, block_bytes // (4 * rows)))


def _adamw(w, g, m, v, *, name, block_bytes=1 << 20, after=None):
    rows, cols = w.shape
    tr, tc = _tile2d(rows, cols, block_bytes, 8)
    g_rows = g.shape[0]
    assert g_rows == rows or tr == rows
    extra_specs, extra_args = ([], []) if after is None else ([pl.BlockSpec(memory_space=pl.ANY)], [after])

    def body(w_ref, g_ref, m_ref, v_ref, *rest):
        go_ref, d_ref, mo_ref, vo_ref = rest[len(extra_args):]
        gv = g_ref[0:tr, :]
        go_ref[...] = gv
        mn = ADAM_B1 * m_ref[...] + (1.0 - ADAM_B1) * gv
        vn = ADAM_B2 * v_ref[...] + (1.0 - ADAM_B2) * (gv * gv)
        m_hat = mn / (1.0 - ADAM_B1 ** ADAM_STEP)
        v_hat = vn / (1.0 - ADAM_B2 ** ADAM_STEP)
        d_ref[...] = -ADAM_LR * (m_hat / (jnp.sqrt(v_hat) + ADAM_EPS) + ADAM_WD * w_ref[...])
        mo_ref[...] = mn
        vo_ref[...] = vn

    spec = pl.BlockSpec((tr, tc), lambda i, j: (i, j))
    g_spec = spec if g_rows == rows else pl.BlockSpec((g_rows, tc), lambda i, j: (0, j))
    return pl.pallas_call(
        body, name=name, grid=(rows // tr, cols // tc), in_specs=[spec, g_spec, spec, spec] + extra_specs,
        out_specs=[spec] * 4, out_shape=[jax.ShapeDtypeStruct((rows, cols), F32)] * 4,
        compiler_params=_cparams(("parallel", "parallel")),
    )(w, g, m, v, *extra_args)


def _matmul_dw_pair(a_me, a_sib, b_me, b_sib, core_idx, *, shards_on, name, after=None, part=(0, 1)):
    T, M = a_me.shape
    N = b_me.shape[1]
    if shards_on == "rows":
        p, count = part
        tm, hc = M // N_CHIPS, N // 2
        hp = hc // count
        tn = _pick(hp, 512)
        per = hp // tn
        grid = (N_CHIPS, per)
        a_spec = pl.BlockSpec((T, tm), lambda i, n, h: (0, i))
        b_me_spec = pl.BlockSpec((T, tn), lambda i, n, h: (0, (h[0] * count + p) * per + n))
        b_sib_spec = pl.BlockSpec((T, tn), lambda i, n, h: (0, p * per + n))
        out_spec = pl.BlockSpec((None, tm, tn), lambda i, n, h: (i, 0, n))
        out_shape = jax.ShapeDtypeStruct((N_CHIPS, tm, hp), BF16)
    else:
        tm, hc = _pick(M, 1024), N // N_CHIPS // 2
        grid = (M // tm, N_CHIPS)
        a_spec = pl.BlockSpec((T, tm), lambda i, j, h: (0, i))
        b_me_spec = pl.BlockSpec((T, hc), lambda i, j, h: (0, 2 * j + h[0]))
        b_sib_spec = pl.BlockSpec((T, hc), lambda i, j, h: (0, j))
        out_spec = pl.BlockSpec((None, tm, hc), lambda i, j, h: (j, i, 0))
        out_shape = jax.ShapeDtypeStruct((N_CHIPS, M, hc), BF16)
    extra_specs, extra_args = ([], []) if after is None else ([pl.BlockSpec(memory_space=pl.ANY)], [after])

    def body(h_ref, am_ref, as_ref, bm_ref, bs_ref, *rest):
        o_ref = rest[len(extra_args)]
        o_ref[...] = (_dot_tn(am_ref[...], bm_ref[...]) + _dot_tn(as_ref[...], bs_ref[...])).astype(BF16)

    grid_spec = pltpu.PrefetchScalarGridSpec(
        num_scalar_prefetch=1, grid=grid, in_specs=[a_spec, a_spec, b_me_spec, b_sib_spec] + extra_specs,
        out_specs=out_spec)
    return pl.pallas_call(
        body, name=name, grid_spec=grid_spec, out_shape=out_shape, compiler_params=_cparams(("parallel", "parallel")),
    )(core_idx, a_me, a_sib, b_me, b_sib, *extra_args)


def _chip_sum(pair, landed, slots, *, name, block_bytes=1 << 20, part=(0, 1), into=None):
    p, count = part
    _, r, hp = pair.shape
    tr, tc = _tile2d(r, hp, block_bytes, 16)
    ncb = hp // tc
    extra_specs, extra_args = ([], []) if into is None else ([pl.BlockSpec(memory_space=pl.ANY)], [into])

    def body(s_ref, own_ref, l0_ref, l1_ref, l2_ref, *rest):
        rest[-1][...] = ((own_ref[...].astype(F32) + l0_ref[...].astype(F32)) + l1_ref[...].astype(F32)
                         ) + l2_ref[...].astype(F32)

    def slab(which):
        return pl.BlockSpec((None, tr, tc), lambda i, k, s: (s[which], i, k))

    grid_spec = pltpu.PrefetchScalarGridSpec(
        num_scalar_prefetch=1, grid=(r // tr, ncb),
        in_specs=[slab(0), slab(1), slab(2), slab(3)] + extra_specs,
        out_specs=pl.BlockSpec((tr, tc), lambda i, k, s: (i, (s[4] * count + p) * ncb + k)))
    return pl.pallas_call(
        body, name=name, grid_spec=grid_spec, out_shape=jax.ShapeDtypeStruct((r, 2 * hp * count), F32),
        input_output_aliases={} if into is None else {5: 0},
        compiler_params=_cparams(("parallel", "parallel")),
    )(slots, pair, landed, landed, landed, *extra_args)


def _stack_sum(x, *, name, out_dtype=F32, block_bytes=1 << 20):
    s, r, c = x.shape
    tr = _pick(r, max(8, block_bytes // (4 * c)), 16) if r % 16 == 0 else r

    def body(x_ref, o_ref):
        acc = x_ref[0].astype(F32)
        for j in range(1, s):
            acc = acc + x_ref[j].astype(F32)
        o_ref[...] = acc.astype(out_dtype)

    return pl.pallas_call(
        body, name=name, grid=(r // tr,),
        in_specs=[pl.BlockSpec((s, tr, c), lambda i: (0, i, 0))], out_specs=pl.BlockSpec((tr, c), lambda i: (i, 0)),
        out_shape=jax.ShapeDtypeStruct((r, c), out_dtype), compiler_params=_cparams(("parallel",)),
    )(x)


HBM = pl.BlockSpec(memory_space=pltpu.HBM)


def _place():
    x, y, c = lax.axis_index("x"), lax.axis_index("y"), lax.axis_index("c")
    other_chips = [(1 - x, y), (x, 1 - y), (1 - x, 1 - y)]
    return x, y, c, other_chips


def _half_cols(cols, which):
    hc = cols // 2
    return pl.ds(pl.multiple_of(which * hc, LANES), hc)


SEM = pl.BlockSpec(memory_space=pltpu.SEMAPHORE)
ANY = pl.BlockSpec(memory_space=pl.ANY)
SIDE_EFFECT = pltpu.SideEffectType.DATAFLOW_SIDE_EFFECTING
TOKEN_SHAPE = (8, LANES)


def _hbm(shape, dtype):
    return pltpu.HBM(shape, dtype)


def _in_hbm(a):
    return pltpu.with_memory_space_constraint(a, pltpu.HBM)


def _gather_copy(src_ref, land_ref, ssem, rsem, k, chip_of_block, to, c):
    cols = src_ref.shape[1]
    return pltpu.make_async_remote_copy(
        src_ref=src_ref.at[:, _half_cols(cols, c)], dst_ref=land_ref.at[chip_of_block, :, _half_cols(cols, c)],
        send_sem=ssem.at[k], recv_sem=rsem.at[k], device_id=to, device_id_type=MESH)


NEIGHBOURS = (0, 1)
ALL_CHIPS = (0, 1, 2)


def _gather_start(shards, *, name, after=(), relayed=()):
    n = len(shards)
    after = list(after)

    def body(*refs):
        srcs, lands = refs[:n], refs[n:2 * n]
        outs = refs[2 * n + len(after):]
        token = outs[-1]
        x, y, c, chips = _place()
        me = 2 * x + y
        for a in range(n):
            ssem, rsem = outs[4 * a], outs[4 * a + 1]
            for k in NEIGHBOURS if a in relayed else ALL_CHIPS:
                cx, cy = chips[k]
                _gather_copy(srcs[a], lands[a], ssem, rsem, k, me, (cx, cy, c), c).start()
        token[...] = jnp.zeros_like(token)

    out_shape, out_specs, aliases = [], [], {}
    for a, s in enumerate(shards):
        out_shape += [pltpu.SemaphoreType.DMA((3,)), pltpu.SemaphoreType.DMA((3,)), _hbm(s.shape, s.dtype),
                      _hbm((N_CHIPS,) + s.shape, s.dtype)]
        out_specs += [SEM, SEM, HBM, HBM]
        aliases[a] = 4 * a + 2
        aliases[n + a] = 4 * a + 3
    out_shape.append(jax.ShapeDtypeStruct(TOKEN_SHAPE, F32))
    out_specs.append(pl.BlockSpec(memory_space=pltpu.VMEM))
    lands = [_in_hbm(lax.empty((N_CHIPS,) + s.shape, s.dtype)) for s in shards]
    res = pl.pallas_call(
        body, name=name, in_specs=[HBM] * (2 * n) + [ANY] * len(after), out_specs=out_specs, out_shape=out_shape,
        input_output_aliases=aliases, compiler_params=pltpu.CompilerParams(has_side_effects=SIDE_EFFECT),
    )(*[_in_hbm(s) for s in shards], *lands, *after)
    return [tuple(res[4 * a:4 * a + 4]) for a in range(n)], res[-1]


def _wait_call(wait_fn, parts, after, *, name):
    ssem, rsem, src, land = parts
    after = list(after) if isinstance(after, (list, tuple)) else [after]

    def body(src_ref, land_ref, ssem_ref, rsem_ref, *rest):
        wait_fn(src_ref, land_ref, ssem_ref, rsem_ref)

    return pl.pallas_call(
        body, name=name, in_specs=[HBM, HBM, SEM, SEM] + [ANY] * len(after), out_specs=[HBM, HBM],
        out_shape=[_hbm(src.shape, src.dtype), _hbm(land.shape, land.dtype)], input_output_aliases={0: 0, 1: 1},
        compiler_params=pltpu.CompilerParams(has_side_effects=SIDE_EFFECT),
    )(src, land, ssem, rsem, *after)


def _gather_wait(parts, after, *, name, ks=ALL_CHIPS):
    def wait(src_ref, land_ref, ssem_ref, rsem_ref):
        x, y, c, chips = _place()
        for k in ks:
            cx, cy = chips[k]
            cp = _gather_copy(src_ref, land_ref, ssem_ref, rsem_ref, k, 2 * cx + cy, (x, y, c), c)
            cp.wait_send()
            cp.wait_recv()

    return _wait_call(wait, parts, after, name=name)


def _relay_copy(buf_ref, ssem, rsem, k, slab, to, c):
    hr = buf_ref.shape[1] // 2
    part = buf_ref.at[slab, pl.ds(k * hr, hr), _half_cols(buf_ref.shape[2], c)]
    return pltpu.make_async_remote_copy(
        src_ref=part, dst_ref=part, send_sem=ssem.at[k], recv_sem=rsem.at[k], device_id=to, device_id_type=MESH)


def _relay_start(land, *, name):
    def body(buf_ref, ssem, rsem, buf_out, token):
        x, y, c, _ = _place()
        _relay_copy(buf_ref, ssem, rsem, 0, 2 * (1 - x) + y, (x, 1 - y, c), c).start()
        _relay_copy(buf_ref, ssem, rsem, 1, 2 * x + 1 - y, (1 - x, y, c), c).start()
        token[...] = jnp.zeros_like(token)

    res = pl.pallas_call(
        body, name=name, in_specs=[HBM], out_specs=[SEM, SEM, HBM, pl.BlockSpec(memory_space=pltpu.VMEM)],
        out_shape=[pltpu.SemaphoreType.DMA((2,)), pltpu.SemaphoreType.DMA((2,)), _hbm(land.shape, land.dtype),
                   jax.ShapeDtypeStruct(TOKEN_SHAPE, F32)],
        input_output_aliases={0: 2}, compiler_params=pltpu.CompilerParams(has_side_effects=SIDE_EFFECT),
    )(land)
    return tuple(res[:3]), res[3]


def _relay_wait(parts, after, *, name):
    ssem, rsem, buf = parts
    after = list(after) if isinstance(after, (list, tuple)) else [after]

    def body(buf_ref, ssem_ref, rsem_ref, *rest):
        x, y, c, _ = _place()
        diagonal = 2 * (1 - x) + 1 - y
        _relay_copy(buf_ref, ssem_ref, rsem_ref, 0, 2 * (1 - x) + y, (x, y, c), c).wait_send()
        _relay_copy(buf_ref, ssem_ref, rsem_ref, 1, 2 * x + 1 - y, (x, y, c), c).wait_send()
        _relay_copy(buf_ref, ssem_ref, rsem_ref, 0, diagonal, (x, y, c), c).wait_recv()
        _relay_copy(buf_ref, ssem_ref, rsem_ref, 1, diagonal, (x, y, c), c).wait_recv()

    return pl.pallas_call(
        body, name=name, in_specs=[HBM, SEM, SEM] + [ANY] * len(after), out_specs=HBM,
        out_shape=_hbm(buf.shape, buf.dtype), input_output_aliases={0: 0},
        compiler_params=pltpu.CompilerParams(has_side_effects=SIDE_EFFECT),
    )(buf, ssem, rsem, *after)


def _forward_copy(buf_ref, ssem, rsem, k, slab, which, to):
    part = buf_ref.at[slab, :, _half_cols(buf_ref.shape[2], which)]
    return pltpu.make_async_remote_copy(
        src_ref=part, dst_ref=part, send_sem=ssem.at[k], recv_sem=rsem.at[k], device_id=to, device_id_type=MESH)


def _sibling_forward(land, *, name):
    def body(_, buf, send_sems, recv_sems):
        x, y, c, chips = _place()
        copies = []
        for k, (cx, cy) in enumerate(chips):
            cp = _forward_copy(buf, send_sems, recv_sems, k, 2 * cx + cy, c, (x, y, 1 - c))
            cp.start()
            copies.append(cp)
        for k, (cx, cy) in enumerate(chips):
            _forward_copy(buf, send_sems, recv_sems, k, 2 * cx + cy, 1 - c, (x, y, c)).wait_recv()
        for cp in copies:
            cp.wait_send()

    return pl.pallas_call(
        body, name=name, in_specs=[HBM], out_specs=HBM, out_shape=jax.ShapeDtypeStruct(land.shape, land.dtype),
        input_output_aliases={0: 0},
        scratch_shapes=[pltpu.SemaphoreType.DMA((3,)), pltpu.SemaphoreType.DMA((3,))],
    )(land)


def _sibling_handshake():
    x, y, c, _ = _place()
    barrier = pltpu.get_barrier_semaphore()
    pl.semaphore_signal(barrier, inc=1, device_id=(x, y, 1 - c), device_id_type=MESH)
    pl.semaphore_wait(barrier, 1)


def _sibling_only(cid):
    return pltpu.CompilerParams(has_side_effects=SIDE_EFFECT, collective_id=cid)


def _forward_start(land, *, name, cid):
    def body(buf_ref, ssem, rsem, buf_out, token):
        _sibling_handshake()
        x, y, c, chips = _place()
        for k, (cx, cy) in enumerate(chips):
            _forward_copy(buf_ref, ssem, rsem, k, 2 * cx + cy, c, (x, y, 1 - c)).start()
        token[...] = jnp.zeros_like(token)

    res = pl.pallas_call(
        body, name=name, in_specs=[HBM], out_specs=[SEM, SEM, HBM, pl.BlockSpec(memory_space=pltpu.VMEM)],
        out_shape=[pltpu.SemaphoreType.DMA((3,)), pltpu.SemaphoreType.DMA((3,)), _hbm(land.shape, land.dtype),
                   jax.ShapeDtypeStruct(TOKEN_SHAPE, F32)],
        input_output_aliases={0: 2}, compiler_params=_sibling_only(cid),
    )(land)
    return tuple(res[:3]), res[3]


def _forward_wait(parts, after, *, name):
    ssem, rsem, buf = parts
    after = list(after) if isinstance(after, (list, tuple)) else [after]

    def body(buf_ref, ssem_ref, rsem_ref, *rest):
        x, y, c, chips = _place()
        for k, (cx, cy) in enumerate(chips):
            _forward_copy(buf_ref, ssem_ref, rsem_ref, k, 2 * cx + cy, c, (x, y, c)).wait_send()
            _forward_copy(buf_ref, ssem_ref, rsem_ref, k, 2 * cx + cy, 1 - c, (x, y, c)).wait_recv()

    return pl.pallas_call(
        body, name=name, in_specs=[HBM, SEM, SEM] + [ANY] * len(after), out_specs=HBM,
        out_shape=_hbm(buf.shape, buf.dtype), input_output_aliases={0: 0},
        compiler_params=pltpu.CompilerParams(has_side_effects=SIDE_EFFECT),
    )(buf, ssem, rsem, *after)


def _share_copy(buf_ref, ssem, rsem, a, which, to):
    part = buf_ref.at[:, _half_cols(buf_ref.shape[1], which)]
    return pltpu.make_async_remote_copy(
        src_ref=part, dst_ref=part, send_sem=ssem.at[a], recv_sem=rsem.at[a], device_id=to, device_id_type=MESH)


def _share_start(arrays, *, name, cid):
    n = len(arrays)

    def body(*refs):
        bufs, ssem, rsem, token = refs[:n], refs[n], refs[n + 1], refs[-1]
        _sibling_handshake()
        x, y, c, _ = _place()
        for a in range(n):
            _share_copy(bufs[a], ssem, rsem, a, c, (x, y, 1 - c)).start()
        token[...] = jnp.zeros_like(token)

    res = pl.pallas_call(
        body, name=name, in_specs=[HBM] * n,
        out_specs=[SEM, SEM] + [HBM] * n + [pl.BlockSpec(memory_space=pltpu.VMEM)],
        out_shape=[pltpu.SemaphoreType.DMA((n,)), pltpu.SemaphoreType.DMA((n,))]
        + [_hbm(b.shape, b.dtype) for b in arrays] + [jax.ShapeDtypeStruct(TOKEN_SHAPE, F32)],
        input_output_aliases={a: 2 + a for a in range(n)}, compiler_params=_sibling_only(cid),
    )(*[_in_hbm(b) for b in arrays])
    return (res[0], res[1], list(res[2:2 + n])), res[-1]


def _share_wait(parts, after, *, name):
    ssem, rsem, bufs = parts
    n = len(bufs)
    after = list(after) if isinstance(after, (list, tuple)) else [after]

    def body(*refs):
        buf_refs, ssem_ref, rsem_ref = refs[:n], refs[n], refs[n + 1]
        x, y, c, _ = _place()
        for a in range(n):
            _share_copy(buf_refs[a], ssem_ref, rsem_ref, a, c, (x, y, c)).wait_send()
            _share_copy(buf_refs[a], ssem_ref, rsem_ref, a, 1 - c, (x, y, c)).wait_recv()

    return pl.pallas_call(
        body, name=name, in_specs=[HBM] * n + [SEM, SEM] + [ANY] * len(after), out_specs=[HBM] * n,
        out_shape=[_hbm(b.shape, b.dtype) for b in bufs], input_output_aliases={a: a for a in range(n)},
        compiler_params=pltpu.CompilerParams(has_side_effects=SIDE_EFFECT),
    )(*bufs, ssem, rsem, *after)


def _scatter_copy(src_ref, land_ref, ssem, rsem, k, src_slab, dst_slab, to):
    return pltpu.make_async_remote_copy(
        src_ref=src_ref.at[src_slab], dst_ref=land_ref.at[dst_slab], send_sem=ssem.at[k], recv_sem=rsem.at[k],
        device_id=to, device_id_type=MESH)


def _scatter_start(part, *, name):
    def start(src_ref, land_ref, ssem, rsem):
        x, y, c, chips = _place()
        me = 2 * x + y
        for k, (cx, cy) in enumerate(chips):
            _scatter_copy(src_ref, land_ref, ssem, rsem, k, 2 * cx + cy, me, (cx, cy, c)).start()

    return _split_start(start, part, part.shape, N_CHIPS - 1, name=name)


def _scatter_wait(parts, after, *, name):
    def wait(src_ref, land_ref, ssem_ref, rsem_ref):
        x, y, c, chips = _place()
        for k, (cx, cy) in enumerate(chips):
            idx = 2 * cx + cy
            cp = _scatter_copy(src_ref, land_ref, ssem_ref, rsem_ref, k, idx, idx, (x, y, c))
            cp.wait_send()
            cp.wait_recv()

    return _wait_call(wait, parts, after, name=name)


def _split_start(start_fn, src, land_shape, n_sems, *, name):
    def body(src_ref, land_ref, ssem, rsem, src_out, land_out, token):
        start_fn(src_ref, land_ref, ssem, rsem)
        token[...] = jnp.zeros_like(token)

    res = pl.pallas_call(
        body, name=name, in_specs=[HBM, HBM], out_specs=[SEM, SEM, HBM, HBM, pl.BlockSpec(memory_space=pltpu.VMEM)],
        out_shape=[pltpu.SemaphoreType.DMA((n_sems,)), pltpu.SemaphoreType.DMA((n_sems,)), _hbm(src.shape, src.dtype),
                   _hbm(land_shape, src.dtype), jax.ShapeDtypeStruct(TOKEN_SHAPE, F32)],
        input_output_aliases={0: 2, 1: 3}, compiler_params=pltpu.CompilerParams(has_side_effects=SIDE_EFFECT),
    )(_in_hbm(src), _in_hbm(lax.empty(land_shape, src.dtype)))
    return tuple(res[:4]), res[4]


def _sibling_copies(src_ref, land_ref, ssem, rsem, k0, groups, which, to):
    def copy(k, src, dst):
        return pltpu.make_async_remote_copy(
            src_ref=src, dst_ref=dst, send_sem=ssem.at[k], recv_sem=rsem.at[k], device_id=to, device_id_type=MESH)

    if groups == 0:
        return [copy(k0, src_ref, land_ref)]
    hw = src_ref.shape[1] // groups // 2
    return [copy(k0 + j, src_ref.at[:, pl.ds(pl.multiple_of((2 * j + which) * hw, LANES), hw)],
                 land_ref.at[:, j * hw:(j + 1) * hw]) for j in range(groups)]


def _to_sibling_start(items, *, name, cid):
    n = len(items)
    shapes = [a.shape if g == 0 else (a.shape[0], a.shape[1] // 2) for a, g in items]
    first = [sum(max(g, 1) for _, g in items[:k]) for k in range(n + 1)]

    def body(*refs):
        srcs, lands, ssem, rsem, token = refs[:n], refs[n:2 * n], refs[2 * n], refs[2 * n + 1], refs[-1]
        _sibling_handshake()
        x, y, c, _ = _place()
        for k, (_, g) in enumerate(items):
            for cp in _sibling_copies(srcs[k], lands[k], ssem, rsem, first[k], g, 1 - c, (x, y, 1 - c)):
                cp.start()
        token[...] = jnp.zeros_like(token)

    res = pl.pallas_call(
        body, name=name, in_specs=[HBM] * (2 * n),
        out_specs=[SEM, SEM] + [HBM] * (2 * n) + [pl.BlockSpec(memory_space=pltpu.VMEM)],
        out_shape=[pltpu.SemaphoreType.DMA((first[n],)), pltpu.SemaphoreType.DMA((first[n],))]
        + [_hbm(a.shape, a.dtype) for a, _ in items] + [_hbm(s, a.dtype) for s, (a, _) in zip(shapes, items)]
        + [jax.ShapeDtypeStruct(TOKEN_SHAPE, F32)],
        input_output_aliases={k: 2 + k for k in range(2 * n)}, compiler_params=_sibling_only(cid),
    )(*[_in_hbm(a) for a, _ in items], *[_in_hbm(lax.empty(s, a.dtype)) for s, (a, _) in zip(shapes, items)])
    return [(res[0], res[1], first[k], g, res[2 + k], res[2 + n + k]) for k, (_, g) in enumerate(items)], res[-1]


def _from_sibling(flight, after, *, name):
    ssem, rsem, k0, groups, src, land = flight

    def wait(src_ref, land_ref, ssem_ref, rsem_ref):
        x, y, c, _ = _place()
        for cp in _sibling_copies(src_ref, land_ref, ssem_ref, rsem_ref, k0, groups, 1 - c, (x, y, c)):
            cp.wait_send()
            cp.wait_recv()

    return _wait_call(wait, (ssem, rsem, src, land), after, name=name)


def _dev_peers(x, y, c, chips):
    return [(x, y, 1 - c)] + [(cx, cy, c) for cx, cy in chips] + [(cx, cy, 1 - c) for cx, cy in chips]


def _dev_gather_start(part, *, name):
    def start(src_ref, land_ref, ssem, rsem):
        x, y, c, chips = _place()
        for k, to in enumerate(_dev_peers(x, y, c, chips)):
            pltpu.make_async_remote_copy(
                src_ref=src_ref, dst_ref=land_ref.at[4 * x + 2 * y + c], send_sem=ssem.at[k], recv_sem=rsem.at[k],
                device_id=to, device_id_type=MESH).start()

    return _split_start(start, part, (N_DEV,) + part.shape, N_DEV - 1, name=name)


def _dev_gather_wait(parts, after, *, name):
    def wait(src_ref, land_ref, ssem_ref, rsem_ref):
        x, y, c, chips = _place()
        for k, (px, py, pc) in enumerate(_dev_peers(x, y, c, chips)):
            cp = pltpu.make_async_remote_copy(
                src_ref=src_ref, dst_ref=land_ref.at[4 * px + 2 * py + pc], send_sem=ssem_ref.at[k],
                recv_sem=rsem_ref.at[k], device_id=(x, y, c), device_id_type=MESH)
            cp.wait_send()
            cp.wait_recv()

    return _wait_call(wait, parts, after, name=name)[1]


def _sibling_share_halves(arrays, *, name):
    n = len(arrays)

    def body(*refs):
        bufs = refs[n:2 * n]
        send_sems, recv_sems = refs[2 * n:]
        x, y, c, _ = _place()
        copies = []
        for a in range(n):
            mine = bufs[a].at[:, _half_cols(bufs[a].shape[1], c)]
            cp = pltpu.make_async_remote_copy(
                src_ref=mine, dst_ref=mine, send_sem=send_sems.at[a], recv_sem=recv_sems.at[a],
                device_id=(x, y, 1 - c), device_id_type=MESH)
            cp.start()
            copies.append(cp)
        for a in range(n):
            theirs = bufs[a].at[:, _half_cols(bufs[a].shape[1], 1 - c)]
            pltpu.make_async_remote_copy(
                src_ref=theirs, dst_ref=theirs, send_sem=send_sems.at[a], recv_sem=recv_sems.at[a],
                device_id=(x, y, c), device_id_type=MESH).wait_recv()
        for cp in copies:
            cp.wait_send()

    return pl.pallas_call(
        body, name=name, in_specs=[HBM] * n, out_specs=[HBM] * n,
        out_shape=[jax.ShapeDtypeStruct(h.shape, h.dtype) for h in arrays],
        input_output_aliases={a: a for a in range(n)},
        scratch_shapes=[pltpu.SemaphoreType.DMA((n,)), pltpu.SemaphoreType.DMA((n,))],
    )(*arrays)


def _pack(arrays, rows_multiple=16, width=LANES):
    flat = jnp.concatenate([a.astype(F32).reshape(-1) for a in arrays])
    total = flat.shape[0]
    rows = -(-total // width)
    rows = -(-rows // rows_multiple) * rows_multiple
    return jnp.pad(flat, (0, rows * width - total)).reshape(rows, width)


def _unpack(buf, shapes):
    flat = buf.reshape(-1)
    out, off = [], 0
    for s in shapes:
        n = math.prod(s)
        out.append(flat[off:off + n].reshape(s))
        off += n
    return out


def kernel(x, norm_pre, norm_post, gla_w_in, gla_w_gate2, gla_b_gate, gla_o_gain, gla_w_out, sgu_w_in, sgu_ln_gain, sgu_ln_bias, sgu_w_spatial, sgu_b_spatial, sgu_w_out, loss_target, m_norm_pre, m_norm_post, m_gla_w_in, m_gla_w_gate2, m_gla_b_gate, m_gla_o_gain, m_gla_w_out, m_sgu_w_in, m_sgu_ln_gain, m_sgu_ln_bias, m_sgu_w_spatial, m_sgu_b_spatial, m_sgu_w_out, v_norm_pre, v_norm_post, v_gla_w_in, v_gla_w_gate2, v_gla_b_gate, v_gla_o_gain, v_gla_w_out, v_sgu_w_in, v_sgu_ln_gain, v_sgu_ln_bias, v_sgu_w_spatial, v_sgu_b_spatial, v_sgu_w_out):
    _, t, d = x.shape
    dk = d // 2
    ws = gla_w_in.shape[2]
    wp = -(-ws // LANES) * LANES
    lay = (ws, wp)
    chip =2 * lax.axis_index("x") + lax.axis_index("y")
    core = lax.axis_index("c")
    core_idx = core.astype(jnp.int32).reshape(1)
    others = jnp.arange(N_CHIPS - 1, dtype=jnp.int32)
    others = others + (others >= chip).astype(jnp.int32)
    slots = jnp.concatenate([chip.astype(jnp.int32).reshape(1), others, core_idx])

    x0 = x[0]
    target = loss_target[0]

    wt_in_g, mt_in_g, vt_in_g = gla_w_in[0].T, m_gla_w_in[0].T, v_gla_w_in[0].T

    small_shard = _pack([gla_w_gate2[0], sgu_ln_gain[0], sgu_ln_bias[0]], rows_multiple=8, width=2 * LANES)
    own = [small_shard, jnp.pad(wt_in_g.astype(BF16), ((0, wp - ws), (0, 0)))]
    in_flight, token = _gather_start(own, name="gather_start_a", relayed=(1,))

    def with_sibling_and_own(mine, land, name):
        return lax.dynamic_update_slice(_sibling_forward(land, name=name + "_share"), mine[None], (chip, 0, 0))

    h0 = _norm_pre(x0, norm_pre[0:1] + token[0:1, 0:1], name="pre0")
    g_small = with_sibling_and_own(*_gather_wait(in_flight[0], h0, name="w_small_wait"), "w_small")
    mine, land = _gather_wait(in_flight[1], [g_small, wt_in_g, mt_in_g, vt_in_g], name="w_gla_in_wait", ks=NEIGHBOURS)
    relay, token = _relay_start(land, name="w_gla_in_relay")
    own_later = [(p[0] + token[0, 0]).astype(BF16) for p in (gla_w_out, sgu_w_in, sgu_w_out)]
    in_flight_later, token = _gather_start(own_later, name="gather_start_b", after=[token])
    in_flight = in_flight + in_flight_later
    land = _relay_wait(relay, token, name="w_gla_in_relay_wait")
    wt_g = with_sibling_and_own(mine, land, "w_gla_in").reshape(N_CHIPS * wp, d)

    def behind(small, token):
        return small + token[0:1, 0:1]

    def arriving(i, after, name):
        mine, land = _gather_wait(in_flight[i], after, name=name + "_wait")
        crossing, token = _forward_start(land, name=name + "_share", cid=i)
        return (mine, crossing), token

    def arrived(pending, after, name):
        mine, crossing = pending
        return lax.dynamic_update_slice(_forward_wait(crossing, after, name=name + "_share_wait"), mine[None],
                                        (chip, 0, 0))

    shard_shapes = [gla_w_gate2.shape[1:], sgu_ln_gain.shape[1:], sgu_ln_bias.shape[1:]]
    per_chip = [_unpack(g_small[j], shard_shapes) for j in range(N_CHIPS)]
    w2_full = jnp.concatenate([p[0] for p in per_chip], axis=1)
    ln_gain = jnp.concatenate([p[1] for p in per_chip], axis=0)[None, :]
    ln_bias = jnp.concatenate([p[2] for p in per_chip], axis=0)[None, :]
    w2p = jnp.pad(w2_full, ((0, LANES - GLA_GATE_RANK), (0, 0)))

    pos_chunk = jnp.arange(SGU_BLOCK) // CHUNK
    mask = pos_chunk[:, None] >= pos_chunk[None, :]
    ws_masked = jnp.where(mask[None], sgu_w_spatial[0], 0.0)
    ws_masked_t = ws_masked.transpose(0, 2, 1)
    bs_t = sgu_b_spatial[0].T

    proj0 = _matmul(h0, wt_g, mode="nt", out_dtype=F32, name="gla_in", tn=wp)
    pending, tok = arriving(2, proj0, "w_gla_out")
    o0, a0, s_before, s_final = _gla_fwd(proj0, w2p, behind(gla_b_gate, tok), gla_o_gain, lay, name="gla_scan")
    w_out_g = arrived(pending, a0, "w_gla_out").reshape(d, d)
    pending, tok = arriving(3, w_out_g, "w_sgu_in")
    y0 = _matmul(a0, w_out_g, mode="nn", out_dtype=F32, name="gla_out", after=tok)
    x1, h1 = _post_then_pre(x0, y0, norm_post[0:1], norm_pre[1:2], name="post0_pre1")
    g_wi_s = arrived(pending, h1, "w_sgu_in")
    pending, tok = arriving(4, g_wi_s, "w_sgu_out")
    proj1 = _matmul(h1, g_wi_s, mode="nn", out_dtype=F32, name="sgu_in", b_shards=True, after=tok)
    a1 = _sgu_fwd(proj1, ln_gain, ln_bias, ws_masked, bs_t, name="sgu_gate")
    w_out_s = arrived(pending, a1, "w_sgu_out").reshape(d, d)
    acts, tok = _to_sibling_start([(a1, 0), (a0, 0), (h1, 0), (h0, 1)], name="acts_to_sibling", cid=5)
    a1, a0, h1, h0 = [f[4] for f in acts]
    y1 = _matmul(a1, w_out_s, mode="nn", out_dtype=F32, name="sgu_out", after=tok)
    loss_part, dx2, dy1, d_post1 = _loss_head(x1, y1, norm_post[1:2], target, name="loss_head")

    def pair_gradient(a_sent, b_sent, after, shards_on, name):
        a_me, a_sib = _from_sibling(a_sent, after, name=name + "_a_wait")
        b_me, b_sib = _from_sibling(b_sent, [a_sib] + list(after), name=name + "_b_wait")
        pair = _matmul_dw_pair(a_me, a_sib, b_me, b_sib, core_idx, shards_on=shards_on,
                               name=name + "_pair")
        return _scatter_start(pair, name=name + "_start")

    def reduced(flight, after, name):
        pair, landed = _scatter_wait(flight, after, name=name + "_wait")
        return _chip_sum(pair, landed, slots, name=name + "_sum")

    (dy1_sent,), tok = _to_sibling_start([(dy1, 1)], name="dy1_to_sibling", cid=6)
    dy1 = dy1_sent[4]
    da1 = _matmul(dy1, w_out_s, mode="nt", out_dtype=F32, name="d_sgu_act", after=tok)
    fl_wo_s, tok = pair_gradient(acts[0], dy1_sent, [da1], "rows", "g_sgu_out")
    dproj1, d_ws, d_bs_t, d_lg, d_lb = _sgu_bwd(da1, proj1, ln_gain, behind(ln_bias, tok), ws_masked, ws_masked_t,
                                                bs_t, name="sgu_gate_bwd")
    (dp1_sent,), tok = _to_sibling_start([(dproj1, N_CHIPS)], name="dproj1_to_sibling", cid=7)
    dproj1 = dp1_sent[4]
    dh1 = _matmul_nt_shards(dproj1, g_wi_s, out_dtype=F32, name="d_sgu_h", after=tok)
    fl_wi_s, tok = pair_gradient(acts[2], dp1_sent, [dh1], "cols", "g_sgu_in")
    dx1, dy0, d_pre1, d_post0 = _mid_bwd(dx2, dh1, x1, behind(norm_pre[1:2], tok), y0, norm_post[0:1],
                                         name="pre1_post0_bwd")
    (dy0_sent,), tok = _to_sibling_start([(dy0, 1)], name="dy0_to_sibling", cid=8)
    dy0 = dy0_sent[4]
    da0 = _matmul(dy0, w_out_g, mode="nt", out_dtype=F32, name="d_gla_act", after=tok)
    fl_wo_g, tok = pair_gradient(acts[1], dy0_sent, [da0], "rows", "g_gla_out")
    dproj0, d_og, d_bg, d_w2p = _gla_bwd(da0, o0, proj0, w2p, behind(gla_b_gate, tok), gla_o_gain, s_before, s_final,
                                         lay, name="gla_scan_bwd")
    early_shapes = [norm_post.shape, gla_b_gate.shape, gla_o_gain.shape, sgu_w_spatial.shape, sgu_b_spatial.shape,
                    (1, GLA_GATE_RANK, dk), (1, d), (1, d), (1, LANES)]
    early_part = _pack([jnp.concatenate([d_post0, d_post1], axis=0), d_bg, d_og, jnp.where(mask[None], d_ws, 0.0)[None],
                        d_bs_t.T[None], d_w2p[:GLA_GATE_RANK][None], d_lg, d_lb, loss_part])
    early_flight, tok = _dev_gather_start(early_part, name="small_early_start")
    (dp0_sent,), tok_sent = _to_sibling_start([(dproj0, 0)], name="dproj0_to_sibling", cid=9)
    dproj0 = dp0_sent[4]
    dh0 = _matmul(dproj0, wt_g, mode="nn", out_dtype=F32, name="d_gla_h", after=tok_sent)
    a_me, a_sib = _from_sibling(dp0_sent, [dh0, tok], name="g_gla_in_a_wait")
    b_me, b_sib = _from_sibling(acts[3], [a_sib, dh0], name="g_gla_in_b_wait")
    fl_wi_g, tok_scatter = [], None
    for p in range(2):
        pair = _matmul_dw_pair(a_me, a_sib, b_me, b_sib, core_idx, shards_on="rows", part=(p, 2),
                               name=f"g_gla_in_pair{p}", after=tok_scatter)
        flight, tok_scatter = _scatter_start(pair, name=f"g_gla_in_start{p}")
        fl_wi_g.append(flight)
    r_wo_s = reduced(fl_wo_s, tok_scatter, "g_sgu_out")
    r_wi_s = reduced(fl_wi_s, r_wo_s, "g_sgu_in")
    r_wo_g = reduced(fl_wo_g, r_wi_s, "g_gla_out")
    sharing, tok = _share_start([r_wo_s, r_wi_s, r_wo_g], name="grads_share_a", cid=10)
    grad_x, d_pre0 = _first_bwd(dx1, dh0, x0, behind(norm_pre[0:1], tok), name="pre0_bwd")

    late_part = _pack([jnp.concatenate([d_pre0, d_pre1], axis=0)])
    late_flight, tok = _dev_gather_start(late_part, name="small_late_start")

    def big_update(w, g, m, v, name, after=None):
        return [u[None] for u in _adamw(w[0], g, m[0], v[0], name=name, after=after)]

    g_wo_sgu, g_wi_sgu, g_wo_gla = _share_wait(sharing, [grad_x, tok], name="grads_share_a_wait")
    u_wi_sgu = big_update(sgu_w_in, g_wi_sgu, m_sgu_w_in, v_sgu_w_in, "adamw_sgu_w_in")
    u_wo_gla = big_update(gla_w_out, g_wo_gla, m_gla_w_out, v_gla_w_out, "adamw_gla_w_out", after=u_wi_sgu[1])

    r_wi_g, behind_this = None, u_wo_gla[1]
    for p, flight in enumerate(fl_wi_g):
        pair, landed = _scatter_wait(flight, behind_this, name=f"g_gla_in_wait{p}")
        r_wi_g = behind_this = _chip_sum(pair, landed, slots, part=(p, 2), into=r_wi_g, name=f"g_gla_in_sum{p}")
    gt_wi_gla, = _sibling_share_halves([r_wi_g], name="grads_share_b")
    u_wi_gla_t = _adamw(wt_in_g, gt_wi_gla, mt_in_g, vt_in_g, name="adamw_gla_w_in")
    u_wi_gla = [u.T[None] for u in u_wi_gla_t]
    u_wo_sgu = big_update(sgu_w_out, g_wo_sgu, m_sgu_w_out, v_sgu_w_out, "adamw_sgu_w_out", after=u_wi_gla_t[1])

    def summed_over_devices(part, flight, after, shapes, name):
        land = _dev_gather_wait(flight, after, name=name + "_wait")
        every = lax.dynamic_update_slice(land, part[None], (2 * chip + core, 0, 0))
        return _unpack(_stack_sum(every, name=name + "_sum"), shapes)

    (g_post, g_bg, g_og, g_wsp, g_bsp, g_w2_full, g_lg_full, g_lb_full, loss_vec) = summed_over_devices(
        early_part, early_flight, u_wo_sgu[1], early_shapes, "small_early")
    g_pre, = summed_over_devices(late_part, late_flight, loss_vec, [norm_pre.shape], "small_late")
    loss = loss_vec[0, 0]
    g_w2 = lax.dynamic_slice_in_dim(g_w2_full, chip * (dk // N_CHIPS), dk // N_CHIPS, axis=2)
    g_lg = lax.dynamic_slice_in_dim(g_lg_full, chip * (d // N_CHIPS), d // N_CHIPS, axis=1)
    g_lb = lax.dynamic_slice_in_dim(g_lb_full, chip * (d // N_CHIPS), d // N_CHIPS, axis=1)

    small_w = [norm_pre, norm_post, gla_b_gate, gla_o_gain, sgu_w_spatial, sgu_b_spatial, gla_w_gate2, sgu_ln_gain,
               sgu_ln_bias]
    small_g = [g_pre, g_post, g_bg, g_og, g_wsp, g_bsp, g_w2, g_lg, g_lb]
    small_m = [m_norm_pre, m_norm_post, m_gla_b_gate, m_gla_o_gain, m_sgu_w_spatial, m_sgu_b_spatial, m_gla_w_gate2,
               m_sgu_ln_gain, m_sgu_ln_bias]
    small_v = [v_norm_pre, v_norm_post, v_gla_b_gate, v_gla_o_gain, v_sgu_w_spatial, v_sgu_b_spatial, v_gla_w_gate2,
               v_sgu_ln_gain, v_sgu_ln_bias]
    own_shapes = [w.shape for w in small_w]
    _, s_dl, s_m, s_v = _adamw(_pack(small_w), _pack(small_g), _pack(small_m), _pack(small_v), name="adamw_small")
    dl_s, m_s, v_s = _unpack(s_dl, own_shapes), _unpack(s_m, own_shapes), _unpack(s_v, own_shapes)

    def ordered(small, kind):
        pre, post, bg, og, wsp, bsp, w2, lg, lb = small
        return [pre, post, u_wi_gla[kind], w2, bg, og, u_wo_gla[kind], u_wi_sgu[kind], lg, lb, wsp, bsp, u_wo_sgu[kind]]

    return (loss, grad_x[None], *ordered(small_g, 0), *ordered(dl_s, 1), *ordered(m_s, 2), *ordered(v_s, 3))
```

```python
import math

import jax
import jax.numpy as jnp
from jax import lax
from jax.experimental import pallas as pl
from jax.experimental.pallas import tpu as pltpu

F32 = jnp.float32
BF16 = jnp.bfloat16
MESH = pl.DeviceIdType.MESH

EPS = 1e-6
CHUNK = 64
GLA_HEADS = 4
GLA_GATE_RANK = 16
GLA_TAU = 16.0
SGU_BLOCK = 128
SGU_GROUPS = 8
N_CHIPS = 4
N_DEV = 8
LANES = 128

ADAM_LR = 0.001
ADAM_B1 = 0.9
ADAM_B2 = 0.999
ADAM_EPS = 1e-08
ADAM_WD = 0.01
ADAM_STEP = 10

VMEM_LIMIT = 56 * 1024 * 1024


def _cparams(sem=None):
    return pltpu.CompilerParams(dimension_semantics=sem, vmem_limit_bytes=VMEM_LIMIT)


def _pick(n, cap, unit=LANES):
    best = None
    for t in range(unit, min(n, cap) + 1, unit):
        if n % t == 0:
            best = t
    assert best is not None, (n, cap, unit)
    return best


def _dot(a, b, dims):
    return lax.dot_general(a, b, (dims, ((), ())), preferred_element_type=F32)


def _dot_nn(a, b):
    return _dot(a, b, ((1,), (0,)))


def _dot_nt(a, b):
    return _dot(a, b, ((1,), (1,)))


def _dot_tn(a, b):
    return _dot(a, b, ((0,), (0,)))


def _matmul(a, b, *, mode, out_dtype, name, tm=1024, tn=512, b_shards=False, after=None):
    M, K = a.shape
    if b_shards:
        ns, Kb, bc = b.shape
        N, tn = ns * bc, _pick(bc, tn)
        per = bc // tn
        b_spec = pl.BlockSpec((None, K, tn), lambda i, j: (j // per, 0, j % per))
    elif mode == "nt":
        N, Kb = b.shape
        tn = _pick(N, tn)
        b_spec = pl.BlockSpec((tn, K), lambda i, j: (j, 0))
    else:
        Kb, N = b.shape
        tn = _pick(N, tn)
        b_spec = pl.BlockSpec((K, tn), lambda i, j: (0, j))
    assert K == Kb and a.dtype == b.dtype == BF16, (a.shape, b.shape, mode)
    tm = _pick(M, tm)
    dims = ((1,), (1,)) if mode == "nt" else ((1,), (0,))
    extra_specs, extra_args = ([], []) if after is None else ([pl.BlockSpec(memory_space=pl.ANY)], [after])

    def body(a_ref, b_ref, *rest):
        rest[-1][...] = _dot(a_ref[...], b_ref[...], dims).astype(out_dtype)

    return pl.pallas_call(
        body, name=name, grid=(M // tm, N // tn),
        in_specs=[pl.BlockSpec((tm, K), lambda i, j: (i, 0)), b_spec] + extra_specs,
        out_specs=pl.BlockSpec((tm, tn), lambda i, j: (i, j)), out_shape=jax.ShapeDtypeStruct((M, N), out_dtype),
        compiler_params=_cparams(("parallel", "parallel")),
    )(a, b, *extra_args)


def _matmul_nt_shards(a, b, *, out_dtype, name, tm=1024, tn=512, after=None):
    M, K = a.shape
    ns, N, kc = b.shape
    assert K == ns * kc
    tm, tn = _pick(M, tm), _pick(N, tn)

    def body(a_ref, *rest):
        b_refs, o_ref = rest[:ns], rest[ns + (after is not None)]
        acc = _dot_nt(a_ref[:, 0:kc], b_refs[0][...])
        for j in range(1, ns):
            acc += _dot_nt(a_ref[:, j * kc:(j + 1) * kc], b_refs[j][...])
        o_ref[...] = acc.astype(out_dtype)

    def shard(j):
        return pl.BlockSpec((None, tn, kc), lambda i, n: (j, n, 0))

    extra_specs, extra_args = ([], []) if after is None else ([pl.BlockSpec(memory_space=pl.ANY)], [after])
    return pl.pallas_call(
        body, name=name, grid=(M // tm, N // tn),
        in_specs=[pl.BlockSpec((tm, K), lambda i, n: (i, 0))] + [shard(j) for j in range(ns)] + extra_specs,
        out_specs=pl.BlockSpec((tm, tn), lambda i, n: (i, n)), out_shape=jax.ShapeDtypeStruct((M, N), out_dtype),
        compiler_params=_cparams(("parallel", "parallel")),
    )(a, *([b] * ns), *extra_args)


def _rstd(x):
    return lax.rsqrt(jnp.mean(x * x, axis=-1, keepdims=True) + EPS)


def _row_spec(tr, d):
    return pl.BlockSpec((tr, d), lambda i: (i, 0))


def _vec_spec(d):
    return pl.BlockSpec((1, d), lambda i: (0, 0))


def _acc_rows(ref, i, val, cols=slice(None)):
    @pl.when(i == 0)
    def _():
        ref[:, cols] = val

    @pl.when(i > 0)
    def _():
        ref[:, cols] += val


def _norm_pre(x, gain, *, name, tr=256):
    t, d = x.shape
    tr = _pick(t, tr, 8)

    def body(x_ref, g_ref, h_ref):
        xv = x_ref[...]
        h_ref[...] = (xv * _rstd(xv) * g_ref[...]).astype(BF16)

    return pl.pallas_call(
        body, name=name, grid=(t // tr,), in_specs=[_row_spec(tr, d), _vec_spec(d)], out_specs=_row_spec(tr, d),
        out_shape=jax.ShapeDtypeStruct((t, d), BF16), compiler_params=_cparams(("parallel",)),
    )(x, gain)


def _post_then_pre(x, y, post_gain, pre_gain, *, name, tr=256):
    t, d = x.shape
    tr = _pick(t, tr, 8)

    def body(x_ref, y_ref, pg_ref, ng_ref, xn_ref, h_ref):
        yv = y_ref[...]
        xn = x_ref[...] + yv * _rstd(yv) * pg_ref[...]
        xn_ref[...] = xn
        h_ref[...] = (xn * _rstd(xn) * ng_ref[...]).astype(BF16)

    return pl.pallas_call(
        body, name=name, grid=(t // tr,),
        in_specs=[_row_spec(tr, d), _row_spec(tr, d), _vec_spec(d), _vec_spec(d)],
        out_specs=[_row_spec(tr, d), _row_spec(tr, d)],
        out_shape=[jax.ShapeDtypeStruct((t, d), F32), jax.ShapeDtypeStruct((t, d), BF16)],
        compiler_params=_cparams(("parallel",)),
    )(x, y, post_gain, pre_gain)


def _norm_bwd(dy, n, r, gain):
    dn = dy * gain
    return r * (dn - n * jnp.mean(dn * n, axis=-1, keepdims=True))


def _loss_head(x, y, post_gain, target, *, name, tr=256):
    t, d = x.shape
    tr = _pick(t, tr, 8)

    def body(x_ref, y_ref, pg_ref, t_ref, loss_ref, dx_ref, dy_ref, dpg_ref):
        i = pl.program_id(0)
        yv = y_ref[...]
        r = _rstd(yv)
        n = yv * r
        err = x_ref[...] + n * pg_ref[...] - t_ref[...]
        dx = err * (1.0 / d)
        dx_ref[...] = dx
        part = 0.5 * jnp.sum(jnp.mean(err * err, axis=-1, keepdims=True), axis=0, keepdims=True)
        _acc_rows(loss_ref, i, jnp.broadcast_to(part, (1, LANES)))
        _acc_rows(dpg_ref, i, jnp.sum(dx * n, axis=0, keepdims=True))
        dy_ref[...] = _norm_bwd(dx, n, r, pg_ref[...]).astype(BF16)

    return pl.pallas_call(
        body, name=name, grid=(t // tr,),
        in_specs=[_row_spec(tr, d), _row_spec(tr, d), _vec_spec(d), _row_spec(tr, d)],
        out_specs=[_vec_spec(LANES), _row_spec(tr, d), _row_spec(tr, d), _vec_spec(d)],
        out_shape=[jax.ShapeDtypeStruct((1, LANES), F32), jax.ShapeDtypeStruct((t, d), F32),
                   jax.ShapeDtypeStruct((t, d), BF16), jax.ShapeDtypeStruct((1, d), F32)],
        compiler_params=_cparams(("arbitrary",)),
    )(x, y, post_gain, target)


def _mid_bwd(dx_out, dh, x, pre_gain, y_prev, post_gain_prev, *, name, tr=256):
    t, d = x.shape
    tr = _pick(t, tr, 8)

    def body(dxo_ref, dh_ref, x_ref, ng_ref, y_ref, pg_ref, dx_ref, dy_ref, dng_ref, dpg_ref):
        i = pl.program_id(0)
        xv = x_ref[...]
        r = _rstd(xv)
        xh = xv * r
        dhv = dh_ref[...]
        _acc_rows(dng_ref, i, jnp.sum(dhv * xh, axis=0, keepdims=True))
        dx = dxo_ref[...] + _norm_bwd(dhv, xh, r, ng_ref[...])
        dx_ref[...] = dx
        yv = y_ref[...]
        ry = _rstd(yv)
        n = yv * ry
        _acc_rows(dpg_ref, i, jnp.sum(dx * n, axis=0, keepdims=True))
        dy_ref[...] = _norm_bwd(dx, n, ry, pg_ref[...]).astype(BF16)

    return pl.pallas_call(
        body, name=name, grid=(t // tr,),
        in_specs=[_row_spec(tr, d), _row_spec(tr, d), _row_spec(tr, d), _vec_spec(d), _row_spec(tr, d), _vec_spec(d)],
        out_specs=[_row_spec(tr, d), _row_spec(tr, d), _vec_spec(d), _vec_spec(d)],
        out_shape=[jax.ShapeDtypeStruct((t, d), F32), jax.ShapeDtypeStruct((t, d), BF16),
                   jax.ShapeDtypeStruct((1, d), F32), jax.ShapeDtypeStruct((1, d), F32)],
        compiler_params=_cparams(("arbitrary",)),
    )(dx_out, dh, x, pre_gain, y_prev, post_gain_prev)


def _first_bwd(dx_out, dh, x, pre_gain, *, name, tr=256):
    t, d = x.shape
    tr = _pick(t, tr, 8)

    def body(dxo_ref, dh_ref, x_ref, ng_ref, dx_ref, dng_ref):
        i = pl.program_id(0)
        xv = x_ref[...]
        r = _rstd(xv)
        xh = xv * r
        dhv = dh_ref[...]
        _acc_rows(dng_ref, i, jnp.sum(dhv * xh, axis=0, keepdims=True))
        dx_ref[...] = dxo_ref[...] + _norm_bwd(dhv, xh, r, ng_ref[...])

    return pl.pallas_call(
        body, name=name, grid=(t // tr,),
        in_specs=[_row_spec(tr, d), _row_spec(tr, d), _row_spec(tr, d), _vec_spec(d)],
        out_specs=[_row_spec(tr, d), _vec_spec(d)],
        out_shape=[jax.ShapeDtypeStruct((t, d), F32), jax.ShapeDtypeStruct((1, d), F32)],
        compiler_params=_cparams(("arbitrary",)),
    )(dx_out, dh, x, pre_gain)


def _sigmoid(x):
    return 1.0 / (1.0 + jnp.exp(-x))


def _log_sigmoid(x):
    return jnp.minimum(x, 0.0) - jnp.log(1.0 + jnp.exp(-jnp.abs(x)))


_GELU_C = math.sqrt(2.0 / math.pi)


_GELU_A = 0.044715


def _gelu_parts(x, with_grad=True):
    x2 = x * x
    h = 0.5 * jnp.tanh(x * (_GELU_C + (_GELU_C * _GELU_A) * x2)) + 0.5
    val = x * h
    if not with_grad:
        return val, None
    return val, h * (1.0 + (1.0 - h) * (x * (2.0 * _GELU_C + (6.0 * _GELU_C * _GELU_A) * x2)))


def _split3(x):
    hi = x.astype(BF16)
    r1 = x - hi.astype(F32)
    mid = r1.astype(BF16)
    lo = (r1 - mid.astype(F32)).astype(BF16)
    return hi, mid, lo


def _tri_matmul(tri_bf16, x):
    hi, mid, lo = _split3(x)
    return _dot_nn(tri_bf16, hi) + _dot_nn(tri_bf16, mid) + _dot_nn(tri_bf16, lo)


def _gla_dims(d):
    dk, dv = d // 2, d
    return dk, dv, dk // GLA_HEADS, dv // GLA_HEADS


def _col_pieces(a, b, lay):
    ws, wp = lay
    out = []
    while a < b:
        j = a // ws
        end = min(b, (j + 1) * ws)
        out.append((j * wp + a - j * ws, end - a))
        a = end
    return out


def _load_cols(ref, a, b, lay):
    parts = [ref[:, s:s + n] for s, n in _col_pieces(a, b, lay)]
    return parts[0] if len(parts) == 1 else jnp.concatenate(parts, axis=1)


def _store_cols(ref, a, val, lay):
    off = 0
    for s, n in _col_pieces(a, a + val.shape[1], lay):
        ref[:, s:s + n] = val[:, off:off + n]
        off += n


def _gate_window(c_r, lay):
    (start, _), = _col_pieces(c_r, c_r + GLA_GATE_RANK, lay)
    assert (start % lay[1]) + LANES <= lay[1]
    return slice(start, start + LANES)


def _gla_gates(glr, k, w2_ref, b_ref):
    z = _dot_nn(glr.astype(BF16), w2_ref[...].astype(BF16)) + b_ref[...]
    la = _log_sigmoid(z) * (1.0 / GLA_TAU)
    row = lax.broadcasted_iota(jnp.int32, (CHUNK, CHUNK), 0)
    col = lax.broadcasted_iota(jnp.int32, (CHUNK, CHUNK), 1)
    incl = (row >= col).astype(BF16)
    bcum = _tri_matmul(incl, la)
    b_end = bcum[CHUNK - 1:CHUNK, :]
    e_rest = jnp.exp(b_end - bcum)
    return z, e_rest, k * e_rest, jnp.exp(b_end)


def _gla_fwd(proj, w2p, b_gate, o_gain, lay, *, name):
    t, wcols = proj.shape
    d = o_gain.shape[1]
    dk, dv, dkh, dvh = _gla_dims(d)
    nc = t // CHUNK
    c_k, c_v, c_g, c_r = dk, 2 * dk, 2 * dk + dv, 2 * dk + 2 * dv
    scale = dkh ** -0.5

    def body(p_ref, w2_ref, b_ref, og_ref, o_ref, a_ref, sb_ref, sfin_ref, s_ref):
        i = pl.program_id(0)

        @pl.when(i == 0)
        def _():
            s_ref[...] = jnp.zeros_like(s_ref)

        q = _load_cols(p_ref, 0, dk, lay) * scale
        k = _load_cols(p_ref, c_k, c_k + dk, lay)
        glr = p_ref[:, _gate_window(c_r, lay)]
        _, _, kdec, decay = _gla_gates(glr, k, w2_ref, b_ref)
        for h in range(GLA_HEADS):
            ks = slice(h * dkh, (h + 1) * dkh)
            vs = slice(h * dvh, (h + 1) * dvh)
            v_h = _load_cols(p_ref, c_v + h * dvh, c_v + (h + 1) * dvh, lay)
            g_h = _load_cols(p_ref, c_g + h * dvh, c_g + (h + 1) * dvh, lay)
            s_old = s_ref[h]
            sb_ref[0, h] = s_old
            s_new = s_old * decay[:, ks] + _dot_tn(v_h.astype(BF16), kdec[:, ks].astype(BF16))
            s_ref[h] = s_new
            o_h = _dot_nt(q[:, ks].astype(BF16), s_new.astype(BF16))
            o_ref[:, vs] = o_h
            on = o_h * _rstd(o_h)
            a_ref[:, vs] = (on * og_ref[:, vs] * (g_h * _sigmoid(g_h))).astype(BF16)

        @pl.when(i == nc - 1)
        def _():
            sfin_ref[...] = s_ref[...]

    full = lambda *shape: pl.BlockSpec(shape, lambda i: (0,) * len(shape))
    return pl.pallas_call(
        body, name=name, grid=(nc,),
        in_specs=[pl.BlockSpec((CHUNK, wcols), lambda i: (i, 0)), full(LANES, dk), full(1, dk), full(1, dv)],
        out_specs=[pl.BlockSpec((CHUNK, dv), lambda i: (i, 0)), pl.BlockSpec((CHUNK, dv), lambda i: (i, 0)),
                   pl.BlockSpec((1, GLA_HEADS, dvh, dkh), lambda i: (i, 0, 0, 0)), full(GLA_HEADS, dvh, dkh)],
        out_shape=[jax.ShapeDtypeStruct((t, dv), F32), jax.ShapeDtypeStruct((t, dv), BF16),
                   jax.ShapeDtypeStruct((nc, GLA_HEADS, dvh, dkh), F32),
                   jax.ShapeDtypeStruct((GLA_HEADS, dvh, dkh), F32)],
        scratch_shapes=[pltpu.VMEM((GLA_HEADS, dvh, dkh), F32)],
        compiler_params=_cparams(("arbitrary",)),
    )(proj, w2p, b_gate, o_gain)


def _gla_bwd(da, o, proj, w2p, b_gate, o_gain, s_before, s_final, lay, *, name):
    t, wcols = proj.shape
    d = o_gain.shape[1]
    dk, dv, dkh, dvh = _gla_dims(d)
    nc = t // CHUNK
    c_k, c_v, c_g, c_r = dk, 2 * dk, 2 * dk + dv, 2 * dk + 2 * dv
    scale = dkh ** -0.5

    def body(da_ref, o_ref, p_ref, w2_ref, b_ref, og_ref, sb_ref, sfin_ref,
             dp_ref, dog_ref, db_ref, dw2_ref, s_ref, gc_ref, dkd_ref):
        i = pl.program_id(0)

        @pl.when(i == 0)
        def _():
            s_ref[...] = sfin_ref[...]
            gc_ref[...] = jnp.zeros_like(gc_ref)

        ws, wp = lay
        for j in range(N_CHIPS):
            dp_ref[:, j * wp + ws:(j + 1) * wp] = jnp.zeros((CHUNK, wp - ws), BF16)
        q = _load_cols(p_ref, 0, dk, lay) * scale
        k = _load_cols(p_ref, c_k, c_k + dk, lay)
        glr = p_ref[:, _gate_window(c_r, lay)]
        z, e_rest, kdec, decay = _gla_gates(glr, k, w2_ref, b_ref)
        ddecay = []
        for h in range(GLA_HEADS):
            ks = slice(h * dkh, (h + 1) * dkh)
            vs = slice(h * dvh, (h + 1) * dvh)
            v_h = _load_cols(p_ref, c_v + h * dvh, c_v + (h + 1) * dvh, lay)
            g_h = _load_cols(p_ref, c_g + h * dvh, c_g + (h + 1) * dvh, lay)
            da_h = da_ref[:, vs]
            o_h = o_ref[:, vs]
            og_h = og_ref[:, vs]
            r = _rstd(o_h)
            on = o_h * r
            sg = _sigmoid(g_h)
            silu = g_h * sg
            _acc_rows(dog_ref, i, jnp.sum(da_h * silu * on, axis=0, keepdims=True), vs)
            _store_cols(dp_ref, c_g + h * dvh, (da_h * (on * og_h) * (sg * (1.0 + g_h * (1.0 - sg)))).astype(BF16),
                        lay)
            don = da_h * silu * og_h
            do_h = (r * (don - on * jnp.mean(don * on, axis=-1, keepdims=True))).astype(BF16)
            s_cur = s_ref[h]
            _store_cols(dp_ref, h * dkh, (_dot_nn(do_h, s_cur.astype(BF16)) * scale).astype(BF16), lay)
            g_tot = gc_ref[h] + _dot_tn(do_h, q[:, ks].astype(BF16))
            g_bf = g_tot.astype(BF16)
            dkd_ref[:, ks] = _dot_nn(v_h.astype(BF16), g_bf)
            _store_cols(dp_ref, c_v + h * dvh, _dot_nt(kdec[:, ks].astype(BF16), g_bf).astype(BF16), lay)
            s_prev = sb_ref[0, h]
            ddecay.append(jnp.sum(g_tot * s_prev, axis=0, keepdims=True))
            gc_ref[h] = g_tot * decay[:, ks]
            s_ref[h] = s_prev
        dkdec = dkd_ref[...]
        _store_cols(dp_ref, c_k, (dkdec * e_rest).astype(BF16), lay)
        d_e = dkdec * kdec
        row = lax.broadcasted_iota(jnp.int32, (CHUNK, CHUNK), 0)
        col = lax.broadcasted_iota(jnp.int32, (CHUNK, CHUNK), 1)
        excl = (row > col).astype(BF16)
        dla = jnp.concatenate(ddecay, axis=1) * decay + _tri_matmul(excl, d_e)
        dz = dla * (1.0 / GLA_TAU) * (1.0 - _sigmoid(z))
        _acc_rows(db_ref, i, jnp.sum(dz, axis=0, keepdims=True))
        dz_bf = dz.astype(BF16)
        dw2 = _dot_tn(glr.astype(BF16), dz_bf)

        @pl.when(i == 0)
        def _():
            dw2_ref[...] = dw2

        @pl.when(i > 0)
        def _():
            dw2_ref[...] += dw2

        dp_ref[:, _gate_window(c_r, lay)] = _dot_nt(dz_bf, w2_ref[...].astype(BF16)).astype(BF16)

    rev = lambda i: (nc - 1 - i, 0)
    full = lambda *shape: pl.BlockSpec(shape, lambda i: (0,) * len(shape))
    return pl.pallas_call(
        body, name=name, grid=(nc,),
        in_specs=[pl.BlockSpec((CHUNK, dv), rev), pl.BlockSpec((CHUNK, dv), rev), pl.BlockSpec((CHUNK, wcols), rev),
                  full(LANES, dk), full(1, dk), full(1, dv),
                  pl.BlockSpec((1, GLA_HEADS, dvh, dkh), lambda i: (nc - 1 - i, 0, 0, 0)), full(GLA_HEADS, dvh, dkh)],
        out_specs=[pl.BlockSpec((CHUNK, wcols), rev), full(1, dv), full(1, dk), full(LANES, dk)],
        out_shape=[jax.ShapeDtypeStruct((t, wcols), BF16), jax.ShapeDtypeStruct((1, dv), F32),
                   jax.ShapeDtypeStruct((1, dk), F32), jax.ShapeDtypeStruct((LANES, dk), F32)],
        scratch_shapes=[pltpu.VMEM((GLA_HEADS, dvh, dkh), F32), pltpu.VMEM((GLA_HEADS, dvh, dkh), F32),
                        pltpu.VMEM((CHUNK, dk), F32)],
        compiler_params=_cparams(("arbitrary",)),
    )(da, o, proj, w2p, b_gate, o_gain, s_before, s_final)


def _sgu_mid(p_ref, lg_ref, lb_ref, ws_ref, bst_ref, w, with_grad=True):
    gd = w // SGU_GROUPS
    u_act, du_fac = _gelu_parts(p_ref[:, 0:w], with_grad)
    vf, dv_fac = _gelu_parts(p_ref[:, w:2 * w], with_grad)
    mu = jnp.mean(vf, axis=-1, keepdims=True)
    cen = vf - mu
    rstd = lax.rsqrt(jnp.mean(cen * cen, axis=-1, keepdims=True) + EPS)
    xh = cen * rstd
    vn = (xh * lg_ref[...] + lb_ref[...]).astype(BF16)
    vs = [_dot_nn(ws_ref[g].astype(BF16), vn[:, g * gd:(g + 1) * gd]) + bst_ref[:, g:g + 1]
          for g in range(SGU_GROUPS)]
    return u_act, du_fac, dv_fac, rstd, xh, vn, vs


def _sgu_fwd(proj, ln_gain, ln_bias, ws_masked, bs_t, *, name):
    t, w3 = proj.shape
    w = w3 // 3
    gd = w // SGU_GROUPS
    nb = t // SGU_BLOCK

    def body(p_ref, lg_ref, lb_ref, ws_ref, bst_ref, a_ref):
        u_act, _, _, _, _, _, vs = _sgu_mid(p_ref, lg_ref, lb_ref, ws_ref, bst_ref, w, with_grad=False)
        for g in range(SGU_GROUPS):
            cs = slice(g * gd, (g + 1) * gd)
            gate = p_ref[:, 2 * w + g * gd:2 * w + (g + 1) * gd]
            a_ref[:, cs] = (u_act[:, cs] * vs[g] * (gate * _sigmoid(gate))).astype(BF16)

    full = lambda *shape: pl.BlockSpec(shape, lambda i: (0,) * len(shape))
    return pl.pallas_call(
        body, name=name, grid=(nb,),
        in_specs=[pl.BlockSpec((SGU_BLOCK, w3), lambda i: (i, 0)), full(1, w), full(1, w),
                  full(SGU_GROUPS, SGU_BLOCK, SGU_BLOCK), full(SGU_BLOCK, SGU_GROUPS)],
        out_specs=pl.BlockSpec((SGU_BLOCK, w), lambda i: (i, 0)),
        out_shape=jax.ShapeDtypeStruct((t, w), BF16),
        compiler_params=_cparams(("parallel",)),
    )(proj, ln_gain, ln_bias, ws_masked, bs_t)


def _sgu_bwd(da, proj, ln_gain, ln_bias, ws_masked, ws_masked_t, bs_t, *, name):
    t, w3 = proj.shape
    w = w3 // 3
    gd = w // SGU_GROUPS
    nb = t // SGU_BLOCK

    def body(da_ref, p_ref, lg_ref, lb_ref, ws_ref, wst_ref, bst_ref, dp_ref, dws_ref, dbst_ref, dlg_ref, dlb_ref,
             dvn_ref):
        i = pl.program_id(0)
        u_act, du_fac, dv_fac, rstd, xh, vn, vs = _sgu_mid(p_ref, lg_ref, lb_ref, ws_ref, bst_ref, w)
        for g in range(SGU_GROUPS):
            cs = slice(g * gd, (g + 1) * gd)
            gate = p_ref[:, 2 * w + g * gd:2 * w + (g + 1) * gd]
            sg = _sigmoid(gate)
            silu = gate * sg
            da_g = da_ref[:, cs]
            ua_g = u_act[:, cs]
            dp_ref[:, cs] = (da_g * vs[g] * silu * du_fac[:, cs]).astype(BF16)
            dp_ref[:, 2 * w + g * gd:2 * w + (g + 1) * gd] = (
                da_g * ua_g * vs[g] * (sg * (1.0 + gate * (1.0 - sg)))).astype(BF16)
            dvs = da_g * ua_g * silu
            dvs_bf = dvs.astype(BF16)
            dvn_ref[:, cs] = _dot_nn(wst_ref[g].astype(BF16), dvs_bf)
            dws = _dot_nt(dvs_bf, vn[:, cs])
            dbs = jnp.sum(dvs, axis=1, keepdims=True)

            @pl.when(i == 0)
            def _():
                dws_ref[g] = dws
                dbst_ref[:, g:g + 1] = dbs

            @pl.when(i > 0)
            def _():
                dws_ref[g] += dws
                dbst_ref[:, g:g + 1] += dbs

        dvn = dvn_ref[...]
        _acc_rows(dlg_ref, i, jnp.sum(dvn * xh, axis=0, keepdims=True))
        _acc_rows(dlb_ref, i, jnp.sum(dvn, axis=0, keepdims=True))
        dxh = dvn * lg_ref[...]
        dvf = rstd * (dxh - jnp.mean(dxh, axis=-1, keepdims=True)
                      - xh * jnp.mean(dxh * xh, axis=-1, keepdims=True))
        dp_ref[:, w:2 * w] = (dvf * dv_fac).astype(BF16)

    full = lambda *shape: pl.BlockSpec(shape, lambda i: (0,) * len(shape))
    return pl.pallas_call(
        body, name=name, grid=(nb,),
        in_specs=[pl.BlockSpec((SGU_BLOCK, w), lambda i: (i, 0)), pl.BlockSpec((SGU_BLOCK, w3), lambda i: (i, 0)),
                  full(1, w), full(1, w), full(SGU_GROUPS, SGU_BLOCK, SGU_BLOCK),
                  full(SGU_GROUPS, SGU_BLOCK, SGU_BLOCK), full(SGU_BLOCK, SGU_GROUPS)],
        out_specs=[pl.BlockSpec((SGU_BLOCK, w3), lambda i: (i, 0)), full(SGU_GROUPS, SGU_BLOCK, SGU_BLOCK),
                   full(SGU_BLOCK, SGU_GROUPS), full(1, w), full(1, w)],
        out_shape=[jax.ShapeDtypeStruct((t, w3), BF16), jax.ShapeDtypeStruct((SGU_GROUPS, SGU_BLOCK, SGU_BLOCK), F32),
                   jax.ShapeDtypeStruct((SGU_BLOCK, SGU_GROUPS), F32), jax.ShapeDtypeStruct((1, w), F32),
                   jax.ShapeDtypeStruct((1, w), F32)],
        scratch_shapes=[pltpu.VMEM((SGU_BLOCK, w), F32)],
        compiler_params=_cparams(("arbitrary",)),
    )(da, proj, ln_gain, ln_bias, ws_masked, ws_masked_t, bs_t)


def _tile2d(rows, cols, block_bytes, row_unit):
    if rows % row_unit == 0:
        return _pick(rows, max(row_unit, block_bytes // (4 * cols)), row_unit), cols
    return rows, _pick(cols, max(LANES, block_bytes // (4 * rows)))


def _adamw(w, g, m, v, *, name, block_bytes=1 << 20, after=None):
    rows, cols = w.shape
    tr, tc = _tile2d(rows, cols, block_bytes, 8)
    g_rows = g.shape[0]
    assert g_rows == rows or tr == rows
    extra_specs, extra_args = ([], []) if after is None else ([pl.BlockSpec(memory_space=pl.ANY)], [after])

    def body(w_ref, g_ref, m_ref, v_ref, *rest):
        go_ref, d_ref, mo_ref, vo_ref = rest[len(extra_args):]
        gv = g_ref[0:tr, :]
        go_ref[...] = gv
        mn = ADAM_B1 * m_ref[...] + (1.0 - ADAM_B1) * gv
        vn = ADAM_B2 * v_ref[...] + (1.0 - ADAM_B2) * (gv * gv)
        m_hat = mn / (1.0 - ADAM_B1 ** ADAM_STEP)
        v_hat = vn / (1.0 - ADAM_B2 ** ADAM_STEP)
        d_ref[...] = -ADAM_LR * (m_hat / (jnp.sqrt(v_hat) + ADAM_EPS) + ADAM_WD * w_ref[...])
        mo_ref[...] = mn
        vo_ref[...] = vn

    spec = pl.BlockSpec((tr, tc), lambda i, j: (i, j))
    g_spec = spec if g_rows == rows else pl.BlockSpec((g_rows, tc), lambda i, j: (0, j))
    return pl.pallas_call(
        body, name=name, grid=(rows // tr, cols // tc), in_specs=[spec, g_spec, spec, spec] + extra_specs,
        out_specs=[spec] * 4, out_shape=[jax.ShapeDtypeStruct((rows, cols), F32)] * 4,
        compiler_params=_cparams(("parallel", "parallel")),
    )(w, g, m, v, *extra_args)


def _matmul_dw_pair(a_me, a_sib, b_me, b_sib, core_idx, *, shards_on, name, after=None, part=(0, 1)):
    T, M = a_me.shape
    N = b_me.shape[1]
    if shards_on == "rows":
        p, count = part
        tm, hc = M // N_CHIPS, N // 2
        hp = hc // count
        tn = _pick(hp, 512)
        per = hp // tn
        grid = (N_CHIPS, per)
        a_spec = pl.BlockSpec((T, tm), lambda i, n, h: (0, i))
        b_me_spec = pl.BlockSpec((T, tn), lambda i, n, h: (0, (h[0] * count + p) * per + n))
        b_sib_spec = pl.BlockSpec((T, tn), lambda i, n, h: (0, p * per + n))
        out_spec = pl.BlockSpec((None, tm, tn), lambda i, n, h: (i, 0, n))
        out_shape = jax.ShapeDtypeStruct((N_CHIPS, tm, hp), BF16)
    else:
        tm, hc = _pick(M, 1024), N // N_CHIPS // 2
        grid = (M // tm, N_CHIPS)
        a_spec = pl.BlockSpec((T, tm), lambda i, j, h: (0, i))
        b_me_spec = pl.BlockSpec((T, hc), lambda i, j, h: (0, 2 * j + h[0]))
        b_sib_spec = pl.BlockSpec((T, hc), lambda i, j, h: (0, j))
        out_spec = pl.BlockSpec((None, tm, hc), lambda i, j, h: (j, i, 0))
        out_shape = jax.ShapeDtypeStruct((N_CHIPS, M, hc), BF16)
    extra_specs, extra_args = ([], []) if after is None else ([pl.BlockSpec(memory_space=pl.ANY)], [after])

    def body(h_ref, am_ref, as_ref, bm_ref, bs_ref, *rest):
        o_ref = rest[len(extra_args)]
        o_ref[...] = (_dot_tn(am_ref[...], bm_ref[...]) + _dot_tn(as_ref[...], bs_ref[...])).astype(BF16)

    grid_spec = pltpu.PrefetchScalarGridSpec(
        num_scalar_prefetch=1, grid=grid, in_specs=[a_spec, a_spec, b_me_spec, b_sib_spec] + extra_specs,
        out_specs=out_spec)
    return pl.pallas_call(
        body, name=name, grid_spec=grid_spec, out_shape=out_shape, compiler_params=_cparams(("parallel", "parallel")),
    )(core_idx, a_me, a_sib, b_me, b_sib, *extra_args)


def _chip_sum(pair, landed, slots, *, name, block_bytes=1 << 20, part=(0, 1), into=None):
    p, count = part
    _, r, hp = pair.shape
    tr, tc = _tile2d(r, hp, block_bytes, 16)
    ncb = hp // tc
    extra_specs, extra_args = ([], []) if into is None else ([pl.BlockSpec(memory_space=pl.ANY)], [into])

    def body(s_ref, own_ref, l0_ref, l1_ref, l2_ref, *rest):
        rest[-1][...] = ((own_ref[...].astype(F32) + l0_ref[...].astype(F32)) + l1_ref[...].astype(F32)
                         ) + l2_ref[...].astype(F32)

    def slab(which):
        return pl.BlockSpec((None, tr, tc), lambda i, k, s: (s[which], i, k))

    grid_spec = pltpu.PrefetchScalarGridSpec(
        num_scalar_prefetch=1, grid=(r // tr, ncb),
        in_specs=[slab(0), slab(1), slab(2), slab(3)] + extra_specs,
        out_specs=pl.BlockSpec((tr, tc), lambda i, k, s: (i, (s[4] * count + p) * ncb + k)))
    return pl.pallas_call(
        body, name=name, grid_spec=grid_spec, out_shape=jax.ShapeDtypeStruct((r, 2 * hp * count), F32),
        input_output_aliases={} if into is None else {5: 0},
        compiler_params=_cparams(("parallel", "parallel")),
    )(slots, pair, landed, landed, landed, *extra_args)


def _stack_sum(x, *, name, out_dtype=F32, block_bytes=1 << 20):
    s, r, c = x.shape
    tr = _pick(r, max(8, block_bytes // (4 * c)), 16) if r % 16 == 0 else r

    def body(x_ref, o_ref):
        acc = x_ref[0].astype(F32)
        for j in range(1, s):
            acc = acc + x_ref[j].astype(F32)
        o_ref[...] = acc.astype(out_dtype)

    return pl.pallas_call(
        body, name=name, grid=(r // tr,),
        in_specs=[pl.BlockSpec((s, tr, c), lambda i: (0, i, 0))], out_specs=pl.BlockSpec((tr, c), lambda i: (i, 0)),
        out_shape=jax.ShapeDtypeStruct((r, c), out_dtype), compiler_params=_cparams(("parallel",)),
    )(x)


HBM = pl.BlockSpec(memory_space=pltpu.HBM)


def _place():
    x, y, c = lax.axis_index("x"), lax.axis_index("y"), lax.axis_index("c")
    other_chips = [(1 - x, y), (x, 1 - y), (1 - x, 1 - y)]
    return x, y, c, other_chips


def _handshake(peers):
    barrier = pltpu.get_barrier_semaphore()
    for peer in peers:
        pl.semaphore_signal(barrier, inc=1, device_id=peer, device_id_type=MESH)
    pl.semaphore_wait(barrier, len(peers))


def _sibling():
    x, y, c, _ = _place()
    return [(x, y, 1 - c)]


def _same_core_chips():
    x, y, c, chips = _place()
    return [(cx, cy, c) for cx, cy in chips]


def _same_core_neighbours():
    x, y, c, _ = _place()
    return [(1 - x, y, c), (x, 1 - y, c)]


def _split_params(cid):
    return pltpu.CompilerParams(has_side_effects=SIDE_EFFECT, collective_id=cid)


def _half_cols(cols, which):
    hc = cols // 2
    return pl.ds(pl.multiple_of(which * hc, LANES), hc)


SEM = pl.BlockSpec(memory_space=pltpu.SEMAPHORE)
ANY = pl.BlockSpec(memory_space=pl.ANY)
SIDE_EFFECT = pltpu.SideEffectType.DATAFLOW_SIDE_EFFECTING
TOKEN_SHAPE = (8, LANES)


def _hbm(shape, dtype):
    return pltpu.HBM(shape, dtype)


def _in_hbm(a):
    return pltpu.with_memory_space_constraint(a, pltpu.HBM)


def _gather_copy(src_ref, land_ref, ssem, rsem, k, chip_of_block, to, c):
    cols = src_ref.shape[1]
    return pltpu.make_async_remote_copy(
        src_ref=src_ref.at[:, _half_cols(cols, c)], dst_ref=land_ref.at[chip_of_block, :, _half_cols(cols, c)],
        send_sem=ssem.at[k], recv_sem=rsem.at[k], device_id=to, device_id_type=MESH)


NEIGHBOURS = (0, 1)
ALL_CHIPS = (0, 1, 2)


def _gather_start(shards, *, name, cid, after=(), relayed=()):
    n = len(shards)
    after = list(after)

    def body(*refs):
        srcs, lands = refs[:n], refs[n:2 * n]
        outs = refs[2 * n + len(after):]
        token = outs[-1]
        _handshake(_same_core_chips())
        x, y, c, chips = _place()
        me = 2 * x + y
        for a in range(n):
            ssem, rsem = outs[4 * a], outs[4 * a + 1]
            for k in NEIGHBOURS if a in relayed else ALL_CHIPS:
                cx, cy = chips[k]
                _gather_copy(srcs[a], lands[a], ssem, rsem, k, me, (cx, cy, c), c).start()
        token[...] = jnp.zeros_like(token)

    out_shape, out_specs, aliases = [], [], {}
    for a, s in enumerate(shards):
        out_shape += [pltpu.SemaphoreType.DMA((3,)), pltpu.SemaphoreType.DMA((3,)), _hbm(s.shape, s.dtype),
                      _hbm((N_CHIPS,) + s.shape, s.dtype)]
        out_specs += [SEM, SEM, HBM, HBM]
        aliases[a] = 4 * a + 2
        aliases[n + a] = 4 * a + 3
    out_shape.append(jax.ShapeDtypeStruct(TOKEN_SHAPE, F32))
    out_specs.append(pl.BlockSpec(memory_space=pltpu.VMEM))
    lands = [_in_hbm(lax.empty((N_CHIPS,) + s.shape, s.dtype)) for s in shards]
    res = pl.pallas_call(
        body, name=name, in_specs=[HBM] * (2 * n) + [ANY] * len(after), out_specs=out_specs, out_shape=out_shape,
        input_output_aliases=aliases, compiler_params=_split_params(cid),
    )(*[_in_hbm(s) for s in shards], *lands, *after)
    return [tuple(res[4 * a:4 * a + 4]) for a in range(n)], res[-1]


def _wait_call(wait_fn, parts, after, *, name):
    ssem, rsem, src, land = parts
    after = list(after) if isinstance(after, (list, tuple)) else [after]

    def body(src_ref, land_ref, ssem_ref, rsem_ref, *rest):
        wait_fn(src_ref, land_ref, ssem_ref, rsem_ref)

    return pl.pallas_call(
        body, name=name, in_specs=[HBM, HBM, SEM, SEM] + [ANY] * len(after), out_specs=[HBM, HBM],
        out_shape=[_hbm(src.shape, src.dtype), _hbm(land.shape, land.dtype)], input_output_aliases={0: 0, 1: 1},
        compiler_params=pltpu.CompilerParams(has_side_effects=SIDE_EFFECT),
    )(src, land, ssem, rsem, *after)


def _gather_wait(parts, after, *, name, ks=ALL_CHIPS):
    def wait(src_ref, land_ref, ssem_ref, rsem_ref):
        x, y, c, chips = _place()
        for k in ks:
            cx, cy = chips[k]
            cp = _gather_copy(src_ref, land_ref, ssem_ref, rsem_ref, k, 2 * cx + cy, (x, y, c), c)
            cp.wait_send()
            cp.wait_recv()

    return _wait_call(wait, parts, after, name=name)


def _relay_copy(buf_ref, ssem, rsem, k, slab, to, c):
    hr = buf_ref.shape[1] // 2
    part = buf_ref.at[slab, pl.ds(k * hr, hr), _half_cols(buf_ref.shape[2], c)]
    return pltpu.make_async_remote_copy(
        src_ref=part, dst_ref=part, send_sem=ssem.at[k], recv_sem=rsem.at[k], device_id=to, device_id_type=MESH)


def _relay_start(land, *, name, cid):
    def body(buf_ref, ssem, rsem, buf_out, token):
        _handshake(_same_core_neighbours())
        x, y, c, _ = _place()
        _relay_copy(buf_ref, ssem, rsem, 0, 2 * (1 - x) + y, (x, 1 - y, c), c).start()
        _relay_copy(buf_ref, ssem, rsem, 1, 2 * x + 1 - y, (1 - x, y, c), c).start()
        token[...] = jnp.zeros_like(token)

    res = pl.pallas_call(
        body, name=name, in_specs=[HBM], out_specs=[SEM, SEM, HBM, pl.BlockSpec(memory_space=pltpu.VMEM)],
        out_shape=[pltpu.SemaphoreType.DMA((2,)), pltpu.SemaphoreType.DMA((2,)), _hbm(land.shape, land.dtype),
                   jax.ShapeDtypeStruct(TOKEN_SHAPE, F32)],
        input_output_aliases={0: 2}, compiler_params=_split_params(cid),
    )(land)
    return tuple(res[:3]), res[3]


def _relay_wait(parts, after, *, name):
    ssem, rsem, buf = parts
    after = list(after) if isinstance(after, (list, tuple)) else [after]

    def body(buf_ref, ssem_ref, rsem_ref, *rest):
        x, y, c, _ = _place()
        diagonal = 2 * (1 - x) + 1 - y
        _relay_copy(buf_ref, ssem_ref, rsem_ref, 0, 2 * (1 - x) + y, (x, y, c), c).wait_send()
        _relay_copy(buf_ref, ssem_ref, rsem_ref, 1, 2 * x + 1 - y, (x, y, c), c).wait_send()
        _relay_copy(buf_ref, ssem_ref, rsem_ref, 0, diagonal, (x, y, c), c).wait_recv()
        _relay_copy(buf_ref, ssem_ref, rsem_ref, 1, diagonal, (x, y, c), c).wait_recv()

    return pl.pallas_call(
        body, name=name, in_specs=[HBM, SEM, SEM] + [ANY] * len(after), out_specs=HBM,
        out_shape=_hbm(buf.shape, buf.dtype), input_output_aliases={0: 0},
        compiler_params=pltpu.CompilerParams(has_side_effects=SIDE_EFFECT),
    )(buf, ssem, rsem, *after)


def _forward_copy(buf_ref, ssem, rsem, k, slab, which, to):
    part = buf_ref.at[slab, :, _half_cols(buf_ref.shape[2], which)]
    return pltpu.make_async_remote_copy(
        src_ref=part, dst_ref=part, send_sem=ssem.at[k], recv_sem=rsem.at[k], device_id=to, device_id_type=MESH)


def _sibling_forward(land, *, name, cid):
    def body(_, buf, send_sems, recv_sems):
        _handshake(_sibling())
        x, y, c, chips = _place()
        copies = []
        for k, (cx, cy) in enumerate(chips):
            cp = _forward_copy(buf, send_sems, recv_sems, k, 2 * cx + cy, c, (x, y, 1 - c))
            cp.start()
            copies.append(cp)
        for k, (cx, cy) in enumerate(chips):
            _forward_copy(buf, send_sems, recv_sems, k, 2 * cx + cy, 1 - c, (x, y, c)).wait_recv()
        for cp in copies:
            cp.wait_send()

    return pl.pallas_call(
        body, name=name, in_specs=[HBM], out_specs=HBM, out_shape=jax.ShapeDtypeStruct(land.shape, land.dtype),
        input_output_aliases={0: 0},
        scratch_shapes=[pltpu.SemaphoreType.DMA((3,)), pltpu.SemaphoreType.DMA((3,))],
        compiler_params=pltpu.CompilerParams(collective_id=cid),
    )(land)


def _forward_start(land, *, name, cid):
    def body(buf_ref, ssem, rsem, buf_out, token):
        _handshake(_sibling())
        x, y, c, chips = _place()
        for k, (cx, cy) in enumerate(chips):
            _forward_copy(buf_ref, ssem, rsem, k, 2 * cx + cy, c, (x, y, 1 - c)).start()
        token[...] = jnp.zeros_like(token)

    res = pl.pallas_call(
        body, name=name, in_specs=[HBM], out_specs=[SEM, SEM, HBM, pl.BlockSpec(memory_space=pltpu.VMEM)],
        out_shape=[pltpu.SemaphoreType.DMA((3,)), pltpu.SemaphoreType.DMA((3,)), _hbm(land.shape, land.dtype),
                   jax.ShapeDtypeStruct(TOKEN_SHAPE, F32)],
        input_output_aliases={0: 2}, compiler_params=_split_params(cid),
    )(land)
    return tuple(res[:3]), res[3]


def _forward_wait(parts, after, *, name):
    ssem, rsem, buf = parts
    after = list(after) if isinstance(after, (list, tuple)) else [after]

    def body(buf_ref, ssem_ref, rsem_ref, *rest):
        x, y, c, chips = _place()
        for k, (cx, cy) in enumerate(chips):
            _forward_copy(buf_ref, ssem_ref, rsem_ref, k, 2 * cx + cy, c, (x, y, c)).wait_send()
            _forward_copy(buf_ref, ssem_ref, rsem_ref, k, 2 * cx + cy, 1 - c, (x, y, c)).wait_recv()

    return pl.pallas_call(
        body, name=name, in_specs=[HBM, SEM, SEM] + [ANY] * len(after), out_specs=HBM,
        out_shape=_hbm(buf.shape, buf.dtype), input_output_aliases={0: 0},
        compiler_params=pltpu.CompilerParams(has_side_effects=SIDE_EFFECT),
    )(buf, ssem, rsem, *after)


def _share_copy(buf_ref, ssem, rsem, a, which, to):
    part = buf_ref.at[:, _half_cols(buf_ref.shape[1], which)]
    return pltpu.make_async_remote_copy(
        src_ref=part, dst_ref=part, send_sem=ssem.at[a], recv_sem=rsem.at[a], device_id=to, device_id_type=MESH)


def _share_start(arrays, *, name, cid):
    n = len(arrays)

    def body(*refs):
        bufs, ssem, rsem, token = refs[:n], refs[n], refs[n + 1], refs[-1]
        _handshake(_sibling())
        x, y, c, _ = _place()
        for a in range(n):
            _share_copy(bufs[a], ssem, rsem, a, c, (x, y, 1 - c)).start()
        token[...] = jnp.zeros_like(token)

    res = pl.pallas_call(
        body, name=name, in_specs=[HBM] * n,
        out_specs=[SEM, SEM] + [HBM] * n + [pl.BlockSpec(memory_space=pltpu.VMEM)],
        out_shape=[pltpu.SemaphoreType.DMA((n,)), pltpu.SemaphoreType.DMA((n,))]
        + [_hbm(b.shape, b.dtype) for b in arrays] + [jax.ShapeDtypeStruct(TOKEN_SHAPE, F32)],
        input_output_aliases={a: 2 + a for a in range(n)}, compiler_params=_split_params(cid),
    )(*[_in_hbm(b) for b in arrays])
    return (res[0], res[1], list(res[2:2 + n])), res[-1]


def _share_wait(parts, after, *, name):
    ssem, rsem, bufs = parts
    n = len(bufs)
    after = list(after) if isinstance(after, (list, tuple)) else [after]

    def body(*refs):
        buf_refs, ssem_ref, rsem_ref = refs[:n], refs[n], refs[n + 1]
        x, y, c, _ = _place()
        for a in range(n):
            _share_copy(buf_refs[a], ssem_ref, rsem_ref, a, c, (x, y, c)).wait_send()
            _share_copy(buf_refs[a], ssem_ref, rsem_ref, a, 1 - c, (x, y, c)).wait_recv()

    return pl.pallas_call(
        body, name=name, in_specs=[HBM] * n + [SEM, SEM] + [ANY] * len(after), out_specs=[HBM] * n,
        out_shape=[_hbm(b.shape, b.dtype) for b in bufs], input_output_aliases={a: a for a in range(n)},
        compiler_params=pltpu.CompilerParams(has_side_effects=SIDE_EFFECT),
    )(*bufs, ssem, rsem, *after)


def _scatter_copy(src_ref, land_ref, ssem, rsem, k, src_slab, dst_slab, to):
    return pltpu.make_async_remote_copy(
        src_ref=src_ref.at[src_slab], dst_ref=land_ref.at[dst_slab], send_sem=ssem.at[k], recv_sem=rsem.at[k],
        device_id=to, device_id_type=MESH)


def _scatter_start(part, *, name, cid):
    def start(src_ref, land_ref, ssem, rsem):
        x, y, c, chips = _place()
        me = 2 * x + y
        for k, (cx, cy) in enumerate(chips):
            _scatter_copy(src_ref, land_ref, ssem, rsem, k, 2 * cx + cy, me, (cx, cy, c)).start()

    return _split_start(start, _same_core_chips, part, part.shape, N_CHIPS - 1, name=name, cid=cid)


def _scatter_wait(parts, after, *, name):
    def wait(src_ref, land_ref, ssem_ref, rsem_ref):
        x, y, c, chips = _place()
        for k, (cx, cy) in enumerate(chips):
            idx = 2 * cx + cy
            cp = _scatter_copy(src_ref, land_ref, ssem_ref, rsem_ref, k, idx, idx, (x, y, c))
            cp.wait_send()
            cp.wait_recv()

    return _wait_call(wait, parts, after, name=name)


def _split_start(start_fn, peers_fn, src, land_shape, n_sems, *, name, cid):
    def body(src_ref, land_ref, ssem, rsem, src_out, land_out, token):
        _handshake(peers_fn())
        start_fn(src_ref, land_ref, ssem, rsem)
        token[...] = jnp.zeros_like(token)

    res = pl.pallas_call(
        body, name=name, in_specs=[HBM, HBM], out_specs=[SEM, SEM, HBM, HBM, pl.BlockSpec(memory_space=pltpu.VMEM)],
        out_shape=[pltpu.SemaphoreType.DMA((n_sems,)), pltpu.SemaphoreType.DMA((n_sems,)), _hbm(src.shape, src.dtype),
                   _hbm(land_shape, src.dtype), jax.ShapeDtypeStruct(TOKEN_SHAPE, F32)],
        input_output_aliases={0: 2, 1: 3}, compiler_params=_split_params(cid),
    )(_in_hbm(src), _in_hbm(lax.empty(land_shape, src.dtype)))
    return tuple(res[:4]), res[4]


def _sibling_copies(src_ref, land_ref, ssem, rsem, k0, groups, which, to):
    def copy(k, src, dst):
        return pltpu.make_async_remote_copy(
            src_ref=src, dst_ref=dst, send_sem=ssem.at[k], recv_sem=rsem.at[k], device_id=to, device_id_type=MESH)

    if groups == 0:
        return [copy(k0, src_ref, land_ref)]
    hw = src_ref.shape[1] // groups // 2
    return [copy(k0 + j, src_ref.at[:, pl.ds(pl.multiple_of((2 * j + which) * hw, LANES), hw)],
                 land_ref.at[:, j * hw:(j + 1) * hw]) for j in range(groups)]


def _to_sibling_start(items, *, name, cid):
    n = len(items)
    shapes = [a.shape if g == 0 else (a.shape[0], a.shape[1] // 2) for a, g in items]
    first = [sum(max(g, 1) for _, g in items[:k]) for k in range(n + 1)]

    def body(*refs):
        srcs, lands, ssem, rsem, token = refs[:n], refs[n:2 * n], refs[2 * n], refs[2 * n + 1], refs[-1]
        _handshake(_sibling())
        x, y, c, _ = _place()
        for k, (_, g) in enumerate(items):
            for cp in _sibling_copies(srcs[k], lands[k], ssem, rsem, first[k], g, 1 - c, (x, y, 1 - c)):
                cp.start()
        token[...] = jnp.zeros_like(token)

    res = pl.pallas_call(
        body, name=name, in_specs=[HBM] * (2 * n),
        out_specs=[SEM, SEM] + [HBM] * (2 * n) + [pl.BlockSpec(memory_space=pltpu.VMEM)],
        out_shape=[pltpu.SemaphoreType.DMA((first[n],)), pltpu.SemaphoreType.DMA((first[n],))]
        + [_hbm(a.shape, a.dtype) for a, _ in items] + [_hbm(s, a.dtype) for s, (a, _) in zip(shapes, items)]
        + [jax.ShapeDtypeStruct(TOKEN_SHAPE, F32)],
        input_output_aliases={k: 2 + k for k in range(2 * n)}, compiler_params=_split_params(cid),
    )(*[_in_hbm(a) for a, _ in items], *[_in_hbm(lax.empty(s, a.dtype)) for s, (a, _) in zip(shapes, items)])
    return [(res[0], res[1], first[k], g, res[2 + k], res[2 + n + k]) for k, (_, g) in enumerate(items)], res[-1]


def _from_sibling(flight, after, *, name):
    ssem, rsem, k0, groups, src, land = flight

    def wait(src_ref, land_ref, ssem_ref, rsem_ref):
        x, y, c, _ = _place()
        for cp in _sibling_copies(src_ref, land_ref, ssem_ref, rsem_ref, k0, groups, 1 - c, (x, y, c)):
            cp.wait_send()
            cp.wait_recv()

    return _wait_call(wait, (ssem, rsem, src, land), after, name=name)


def _dev_peers(x, y, c, chips):
    return [(x, y, 1 - c)] + [(cx, cy, c) for cx, cy in chips] + [(cx, cy, 1 - c) for cx, cy in chips]


def _dev_gather_start(part, *, name, cid):
    def start(src_ref, land_ref, ssem, rsem):
        x, y, c, chips = _place()
        for k, to in enumerate(_dev_peers(x, y, c, chips)):
            pltpu.make_async_remote_copy(
                src_ref=src_ref, dst_ref=land_ref.at[4 * x + 2 * y + c], send_sem=ssem.at[k], recv_sem=rsem.at[k],
                device_id=to, device_id_type=MESH).start()

    return _split_start(start, lambda: _dev_peers(*_place()), part, (N_DEV,) + part.shape, N_DEV - 1, name=name,
                        cid=cid)


def _dev_gather_wait(parts, after, *, name):
    def wait(src_ref, land_ref, ssem_ref, rsem_ref):
        x, y, c, chips = _place()
        for k, (px, py, pc) in enumerate(_dev_peers(x, y, c, chips)):
            cp = pltpu.make_async_remote_copy(
                src_ref=src_ref, dst_ref=land_ref.at[4 * px + 2 * py + pc], send_sem=ssem_ref.at[k],
                recv_sem=rsem_ref.at[k], device_id=(x, y, c), device_id_type=MESH)
            cp.wait_send()
            cp.wait_recv()

    return _wait_call(wait, parts, after, name=name)[1]


def _sibling_share_halves(arrays, *, name, cid):
    n = len(arrays)

    def body(*refs):
        bufs = refs[n:2 * n]
        send_sems, recv_sems = refs[2 * n:]
        _handshake(_sibling())
        x, y, c, _ = _place()
        copies = []
        for a in range(n):
            mine = bufs[a].at[:, _half_cols(bufs[a].shape[1], c)]
            cp = pltpu.make_async_remote_copy(
                src_ref=mine, dst_ref=mine, send_sem=send_sems.at[a], recv_sem=recv_sems.at[a],
                device_id=(x, y, 1 - c), device_id_type=MESH)
            cp.start()
            copies.append(cp)
        for a in range(n):
            theirs = bufs[a].at[:, _half_cols(bufs[a].shape[1], 1 - c)]
            pltpu.make_async_remote_copy(
                src_ref=theirs, dst_ref=theirs, send_sem=send_sems.at[a], recv_sem=recv_sems.at[a],
                device_id=(x, y, c), device_id_type=MESH).wait_recv()
        for cp in copies:
            cp.wait_send()

    return pl.pallas_call(
        body, name=name, in_specs=[HBM] * n, out_specs=[HBM] * n,
        out_shape=[jax.ShapeDtypeStruct(h.shape, h.dtype) for h in arrays],
        input_output_aliases={a: a for a in range(n)},
        scratch_shapes=[pltpu.SemaphoreType.DMA((n,)), pltpu.SemaphoreType.DMA((n,))],
        compiler_params=pltpu.CompilerParams(collective_id=cid),
    )(*arrays)


def _pack(arrays, rows_multiple=16, width=LANES):
    flat = jnp.concatenate([a.astype(F32).reshape(-1) for a in arrays])
    total = flat.shape[0]
    rows = -(-total // width)
    rows = -(-rows // rows_multiple) * rows_multiple
    return jnp.pad(flat, (0, rows * width - total)).reshape(rows, width)


def _unpack(buf, shapes):
    flat = buf.reshape(-1)
    out, off = [], 0
    for s in shapes:
        n = math.prod(s)
        out.append(flat[off:off + n].reshape(s))
        off += n
    return out


def kernel(x, norm_pre, norm_post, gla_w_in, gla_w_gate2, gla_b_gate, gla_o_gain, gla_w_out, sgu_w_in, sgu_ln_gain, sgu_ln_bias, sgu_w_spatial, sgu_b_spatial, sgu_w_out, loss_target, m_norm_pre, m_norm_post, m_gla_w_in, m_gla_w_gate2, m_gla_b_gate, m_gla_o_gain, m_gla_w_out, m_sgu_w_in, m_sgu_ln_gain, m_sgu_ln_bias, m_sgu_w_spatial, m_sgu_b_spatial, m_sgu_w_out, v_norm_pre, v_norm_post, v_gla_w_in, v_gla_w_gate2, v_gla_b_gate, v_gla_o_gain, v_gla_w_out, v_sgu_w_in, v_sgu_ln_gain, v_sgu_ln_bias, v_sgu_w_spatial, v_sgu_b_spatial, v_sgu_w_out):
    _, t, d = x.shape
    dk = d // 2
    ws = gla_w_in.shape[2]
    wp = -(-ws // LANES) * LANES
    lay = (ws, wp)
    chip =2 * lax.axis_index("x") + lax.axis_index("y")
    core = lax.axis_index("c")
    core_idx = core.astype(jnp.int32).reshape(1)
    others = jnp.arange(N_CHIPS - 1, dtype=jnp.int32)
    others = others + (others >= chip).astype(jnp.int32)
    slots = jnp.concatenate([chip.astype(jnp.int32).reshape(1), others, core_idx])

    x0 = x[0]
    target = loss_target[0]

    wt_in_g, mt_in_g, vt_in_g = gla_w_in[0].T, m_gla_w_in[0].T, v_gla_w_in[0].T

    small_shard = _pack([gla_w_gate2[0], sgu_ln_gain[0], sgu_ln_bias[0]], rows_multiple=8, width=2 * LANES)
    own = [small_shard, jnp.pad(wt_in_g.astype(BF16), ((0, wp - ws), (0, 0)))]
    in_flight, token = _gather_start(own, name="gather_start_a", cid=0, relayed=(1,))

    def with_sibling_and_own(mine, land, name, cid):
        return lax.dynamic_update_slice(_sibling_forward(land, name=name + "_share", cid=cid), mine[None],
                                        (chip, 0, 0))

    h0 = _norm_pre(x0, norm_pre[0:1] + token[0:1, 0:1], name="pre0")
    g_small = with_sibling_and_own(*_gather_wait(in_flight[0], h0, name="w_small_wait"), "w_small", 12)
    mine, land = _gather_wait(in_flight[1], [g_small, wt_in_g, mt_in_g, vt_in_g], name="w_gla_in_wait", ks=NEIGHBOURS)
    relay, token = _relay_start(land, name="w_gla_in_relay", cid=11)
    own_later = [(p[0] + token[0, 0]).astype(BF16) for p in (gla_w_out, sgu_w_in, sgu_w_out)]
    in_flight_later, token = _gather_start(own_later, name="gather_start_b", cid=1, after=[token])
    in_flight = in_flight + in_flight_later
    land = _relay_wait(relay, token, name="w_gla_in_relay_wait")
    wt_g = with_sibling_and_own(mine, land, "w_gla_in", 13).reshape(N_CHIPS * wp, d)

    def behind(small, token):
        return small + token[0:1, 0:1]

    def arriving(i, after, name):
        mine, land = _gather_wait(in_flight[i], after, name=name + "_wait")
        crossing, token = _forward_start(land, name=name + "_share", cid=i)
        return (mine, crossing), token

    def arrived(pending, after, name):
        mine, crossing = pending
        return lax.dynamic_update_slice(_forward_wait(crossing, after, name=name + "_share_wait"), mine[None],
                                        (chip, 0, 0))

    shard_shapes = [gla_w_gate2.shape[1:], sgu_ln_gain.shape[1:], sgu_ln_bias.shape[1:]]
    per_chip = [_unpack(g_small[j], shard_shapes) for j in range(N_CHIPS)]
    w2_full = jnp.concatenate([p[0] for p in per_chip], axis=1)
    ln_gain = jnp.concatenate([p[1] for p in per_chip], axis=0)[None, :]
    ln_bias = jnp.concatenate([p[2] for p in per_chip], axis=0)[None, :]
    w2p = jnp.pad(w2_full, ((0, LANES - GLA_GATE_RANK), (0, 0)))

    pos_chunk = jnp.arange(SGU_BLOCK) // CHUNK
    mask = pos_chunk[:, None] >= pos_chunk[None, :]
    ws_masked = jnp.where(mask[None], sgu_w_spatial[0], 0.0)
    ws_masked_t = ws_masked.transpose(0, 2, 1)
    bs_t = sgu_b_spatial[0].T

    proj0 = _matmul(h0, wt_g, mode="nt", out_dtype=F32, name="gla_in", tn=wp)
    pending, tok = arriving(2, proj0, "w_gla_out")
    o0, a0, s_before, s_final = _gla_fwd(proj0, w2p, behind(gla_b_gate, tok), gla_o_gain, lay, name="gla_scan")
    w_out_g = arrived(pending, a0, "w_gla_out").reshape(d, d)
    pending, tok = arriving(3, w_out_g, "w_sgu_in")
    y0 = _matmul(a0, w_out_g, mode="nn", out_dtype=F32, name="gla_out", after=tok)
    x1, h1 = _post_then_pre(x0, y0, norm_post[0:1], norm_pre[1:2], name="post0_pre1")
    g_wi_s = arrived(pending, h1, "w_sgu_in")
    pending, tok = arriving(4, g_wi_s, "w_sgu_out")
    proj1 = _matmul(h1, g_wi_s, mode="nn", out_dtype=F32, name="sgu_in", b_shards=True, after=tok)
    a1 = _sgu_fwd(proj1, ln_gain, ln_bias, ws_masked, bs_t, name="sgu_gate")
    w_out_s = arrived(pending, a1, "w_sgu_out").reshape(d, d)
    acts, tok = _to_sibling_start([(a1, 0), (a0, 0), (h1, 0), (h0, 1)], name="acts_to_sibling", cid=5)
    a1, a0, h1, h0 = [f[4] for f in acts]
    y1 = _matmul(a1, w_out_s, mode="nn", out_dtype=F32, name="sgu_out", after=tok)
    loss_part, dx2, dy1, d_post1 = _loss_head(x1, y1, norm_post[1:2], target, name="loss_head")

    def pair_gradient(a_sent, b_sent, after, shards_on, name, cid):
        a_me, a_sib = _from_sibling(a_sent, after, name=name + "_a_wait")
        b_me, b_sib = _from_sibling(b_sent, [a_sib] + list(after), name=name + "_b_wait")
        pair = _matmul_dw_pair(a_me, a_sib, b_me, b_sib, core_idx, shards_on=shards_on,
                               name=name + "_pair")
        return _scatter_start(pair, name=name + "_start", cid=cid)

    def reduced(flight, after, name):
        pair, landed = _scatter_wait(flight, after, name=name + "_wait")
        return _chip_sum(pair, landed, slots, name=name + "_sum")

    (dy1_sent,), tok = _to_sibling_start([(dy1, 1)], name="dy1_to_sibling", cid=6)
    dy1 = dy1_sent[4]
    da1 = _matmul(dy1, w_out_s, mode="nt", out_dtype=F32, name="d_sgu_act", after=tok)
    fl_wo_s, tok = pair_gradient(acts[0], dy1_sent, [da1], "rows", "g_sgu_out", 15)
    dproj1, d_ws, d_bs_t, d_lg, d_lb = _sgu_bwd(da1, proj1, ln_gain, behind(ln_bias, tok), ws_masked, ws_masked_t,
                                                bs_t, name="sgu_gate_bwd")
    (dp1_sent,), tok = _to_sibling_start([(dproj1, N_CHIPS)], name="dproj1_to_sibling", cid=7)
    dproj1 = dp1_sent[4]
    dh1 = _matmul_nt_shards(dproj1, g_wi_s, out_dtype=F32, name="d_sgu_h", after=tok)
    fl_wi_s, tok = pair_gradient(acts[2], dp1_sent, [dh1], "cols", "g_sgu_in", 16)
    dx1, dy0, d_pre1, d_post0 = _mid_bwd(dx2, dh1, x1, behind(norm_pre[1:2], tok), y0, norm_post[0:1],
                                         name="pre1_post0_bwd")
    (dy0_sent,), tok = _to_sibling_start([(dy0, 1)], name="dy0_to_sibling", cid=8)
    dy0 = dy0_sent[4]
    da0 = _matmul(dy0, w_out_g, mode="nt", out_dtype=F32, name="d_gla_act", after=tok)
    fl_wo_g, tok = pair_gradient(acts[1], dy0_sent, [da0], "rows", "g_gla_out", 17)
    dproj0, d_og, d_bg, d_w2p = _gla_bwd(da0, o0, proj0, w2p, behind(gla_b_gate, tok), gla_o_gain, s_before, s_final,
                                         lay, name="gla_scan_bwd")
    early_shapes = [norm_post.shape, gla_b_gate.shape, gla_o_gain.shape, sgu_w_spatial.shape, sgu_b_spatial.shape,
                    (1, GLA_GATE_RANK, dk), (1, d), (1, d), (1, LANES)]
    early_part = _pack([jnp.concatenate([d_post0, d_post1], axis=0), d_bg, d_og, jnp.where(mask[None], d_ws, 0.0)[None],
                        d_bs_t.T[None], d_w2p[:GLA_GATE_RANK][None], d_lg, d_lb, loss_part])
    early_flight, tok = _dev_gather_start(early_part, name="small_early_start", cid=20)
    (dp0_sent,), tok_sent = _to_sibling_start([(dproj0, 0)], name="dproj0_to_sibling", cid=9)
    dproj0 = dp0_sent[4]
    dh0 = _matmul(dproj0, wt_g, mode="nn", out_dtype=F32, name="d_gla_h", after=tok_sent)
    a_me, a_sib = _from_sibling(dp0_sent, [dh0, tok], name="g_gla_in_a_wait")
    b_me, b_sib = _from_sibling(acts[3], [a_sib, dh0], name="g_gla_in_b_wait")
    fl_wi_g, tok_scatter = [], None
    for p in range(2):
        pair = _matmul_dw_pair(a_me, a_sib, b_me, b_sib, core_idx, shards_on="rows", part=(p, 2),
                               name=f"g_gla_in_pair{p}", after=tok_scatter)
        flight, tok_scatter = _scatter_start(pair, name=f"g_gla_in_start{p}", cid=18 + p)
        fl_wi_g.append(flight)
    r_wo_s = reduced(fl_wo_s, tok_scatter, "g_sgu_out")
    r_wi_s = reduced(fl_wi_s, r_wo_s, "g_sgu_in")
    r_wo_g = reduced(fl_wo_g, r_wi_s, "g_gla_out")
    sharing, tok = _share_start([r_wo_s, r_wi_s, r_wo_g], name="grads_share_a", cid=10)
    grad_x, d_pre0 = _first_bwd(dx1, dh0, x0, behind(norm_pre[0:1], tok), name="pre0_bwd")

    late_part = _pack([jnp.concatenate([d_pre0, d_pre1], axis=0)])
    late_flight, tok = _dev_gather_start(late_part, name="small_late_start", cid=21)

    def big_update(w, g, m, v, name, after=None):
        return [u[None] for u in _adamw(w[0], g, m[0], v[0], name=name, after=after)]

    g_wo_sgu, g_wi_sgu, g_wo_gla = _share_wait(sharing, [grad_x, tok], name="grads_share_a_wait")
    u_wi_sgu = big_update(sgu_w_in, g_wi_sgu, m_sgu_w_in, v_sgu_w_in, "adamw_sgu_w_in")
    u_wo_gla = big_update(gla_w_out, g_wo_gla, m_gla_w_out, v_gla_w_out, "adamw_gla_w_out", after=u_wi_sgu[1])

    r_wi_g, behind_this = None, u_wo_gla[1]
    for p, flight in enumerate(fl_wi_g):
        pair, landed = _scatter_wait(flight, behind_this, name=f"g_gla_in_wait{p}")
        r_wi_g = behind_this = _chip_sum(pair, landed, slots, part=(p, 2), into=r_wi_g, name=f"g_gla_in_sum{p}")
    gt_wi_gla, = _sibling_share_halves([r_wi_g], name="grads_share_b", cid=14)
    u_wi_gla_t = _adamw(wt_in_g, gt_wi_gla, mt_in_g, vt_in_g, name="adamw_gla_w_in")
    u_wi_gla = [u.T[None] for u in u_wi_gla_t]
    u_wo_sgu = big_update(sgu_w_out, g_wo_sgu, m_sgu_w_out, v_sgu_w_out, "adamw_sgu_w_out", after=u_wi_gla_t[1])

    def summed_over_devices(part, flight, after, shapes, name):
        land = _dev_gather_wait(flight, after, name=name + "_wait")
        every = lax.dynamic_update_slice(land, part[None], (2 * chip + core, 0, 0))
        return _unpack(_stack_sum(every, name=name + "_sum"), shapes)

    (g_post, g_bg, g_og, g_wsp, g_bsp, g_w2_full, g_lg_full, g_lb_full, loss_vec) = summed_over_devices(
        early_part, early_flight, u_wo_sgu[1], early_shapes, "small_early")
    g_pre, = summed_over_devices(late_part, late_flight, loss_vec, [norm_pre.shape], "small_late")
    loss = loss_vec[0, 0]
    g_w2 = lax.dynamic_slice_in_dim(g_w2_full, chip * (dk // N_CHIPS), dk // N_CHIPS, axis=2)
    g_lg = lax.dynamic_slice_in_dim(g_lg_full, chip * (d // N_CHIPS), d // N_CHIPS, axis=1)
    g_lb = lax.dynamic_slice_in_dim(g_lb_full, chip * (d // N_CHIPS), d // N_CHIPS, axis=1)

    small_w = [norm_pre, norm_post, gla_b_gate, gla_o_gain, sgu_w_spatial, sgu_b_spatial, gla_w_gate2, sgu_ln_gain,
               sgu_ln_bias]
    small_g = [g_pre, g_post, g_bg, g_og, g_wsp, g_bsp, g_w2, g_lg, g_lb]
    small_m = [m_norm_pre, m_norm_post, m_gla_b_gate, m_gla_o_gain, m_sgu_w_spatial, m_sgu_b_spatial, m_gla_w_gate2,
               m_sgu_ln_gain, m_sgu_ln_bias]
    small_v = [v_norm_pre, v_norm_post, v_gla_b_gate, v_gla_o_gain, v_sgu_w_spatial, v_sgu_b_spatial, v_gla_w_gate2,
               v_sgu_ln_gain, v_sgu_ln_bias]
    own_shapes = [w.shape for w in small_w]
    _, s_dl, s_m, s_v = _adamw(_pack(small_w), _pack(small_g), _pack(small_m), _pack(small_v), name="adamw_small")
    dl_s, m_s, v_s = _unpack(s_dl, own_shapes), _unpack(s_m, own_shapes), _unpack(s_v, own_shapes)

    def ordered(small, kind):
        pre, post, bg, og, wsp, bsp, w2, lg, lb = small
        return [pre, post, u_wi_gla[kind], w2, bg, og, u_wo_gla[kind], u_wi_sgu[kind], lg, lb, wsp, bsp, u_wo_sgu[kind]]

    return (loss, grad_x[None], *ordered(small_g, 0), *ordered(dl_s, 1), *ordered(m_s, 2), *ordered(v_s, 3))
```

```python
import math

import jax
import jax.numpy as jnp
from jax import lax
from jax.experimental import pallas as pl
from jax.experimental.pallas import tpu as pltpu

F32 = jnp.float32
BF16 = jnp.bfloat16
MESH = pl.DeviceIdType.MESH

EPS = 1e-6
CHUNK = 64
GLA_HEADS = 4
GLA_GATE_RANK = 16
GLA_TAU = 16.0
SGU_BLOCK = 128
SGU_GROUPS = 8
N_CHIPS = 4
N_DEV = 8
LANES = 128

ADAM_LR = 0.001
ADAM_B1 = 0.9
ADAM_B2 = 0.999
ADAM_EPS = 1e-08
ADAM_WD = 0.01
ADAM_STEP = 10

VMEM_LIMIT = 56 * 1024 * 1024


def _cparams(sem=None):
    return pltpu.CompilerParams(dimension_semantics=sem, vmem_limit_bytes=VMEM_LIMIT)


def _pick(n, cap, unit=LANES):
    best = None
    for t in range(unit, min(n, cap) + 1, unit):
        if n % t == 0:
            best = t
    assert best is not None, (n, cap, unit)
    return best


def _dot(a, b, dims):
    return lax.dot_general(a, b, (dims, ((), ())), preferred_element_type=F32)


def _dot_nn(a, b):
    return _dot(a, b, ((1,), (0,)))


def _dot_nt(a, b):
    return _dot(a, b, ((1,), (1,)))


def _dot_tn(a, b):
    return _dot(a, b, ((0,), (0,)))


def _matmul(a, b, *, mode, out_dtype, name, tm=1024, tn=512, b_shards=False, after=None):
    M, K = a.shape
    if b_shards:
        ns, Kb, bc = b.shape
        N, tn = ns * bc, _pick(bc, tn)
        per = bc // tn
        b_spec = pl.BlockSpec((None, K, tn), lambda i, j: (j // per, 0, j % per))
    elif mode == "nt":
        N, Kb = b.shape
        tn = _pick(N, tn)
        b_spec = pl.BlockSpec((tn, K), lambda i, j: (j, 0))
    else:
        Kb, N = b.shape
        tn = _pick(N, tn)
        b_spec = pl.BlockSpec((K, tn), lambda i, j: (0, j))
    assert K == Kb and a.dtype == b.dtype == BF16, (a.shape, b.shape, mode)
    tm = _pick(M, tm)
    dims = ((1,), (1,)) if mode == "nt" else ((1,), (0,))
    extra_specs, extra_args = ([], []) if after is None else ([pl.BlockSpec(memory_space=pl.ANY)], [after])

    def body(a_ref, b_ref, *rest):
        rest[-1][...] = _dot(a_ref[...], b_ref[...], dims).astype(out_dtype)

    return pl.pallas_call(
        body, name=name, grid=(M // tm, N // tn),
        in_specs=[pl.BlockSpec((tm, K), lambda i, j: (i, 0)), b_spec] + extra_specs,
        out_specs=pl.BlockSpec((tm, tn), lambda i, j: (i, j)), out_shape=jax.ShapeDtypeStruct((M, N), out_dtype),
        compiler_params=_cparams(("parallel", "parallel")),
    )(a, b, *extra_args)


def _matmul_nt_shards(a, b, *, out_dtype, name, tm=1024, tn=512, after=None):
    M, K = a.shape
    ns, N, kc = b.shape
    assert K == ns * kc
    tm, tn = _pick(M, tm), _pick(N, tn)

    def body(a_ref, *rest):
        b_refs, o_ref = rest[:ns], rest[ns + (after is not None)]
        acc = _dot_nt(a_ref[:, 0:kc], b_refs[0][...])
        for j in range(1, ns):
            acc += _dot_nt(a_ref[:, j * kc:(j + 1) * kc], b_refs[j][...])
        o_ref[...] = acc.astype(out_dtype)

    def shard(j):
        return pl.BlockSpec((None, tn, kc), lambda i, n: (j, n, 0))

    extra_specs, extra_args = ([], []) if after is None else ([pl.BlockSpec(memory_space=pl.ANY)], [after])
    return pl.pallas_call(
        body, name=name, grid=(M // tm, N // tn),
        in_specs=[pl.BlockSpec((tm, K), lambda i, n: (i, 0))] + [shard(j) for j in range(ns)] + extra_specs,
        out_specs=pl.BlockSpec((tm, tn), lambda i, n: (i, n)), out_shape=jax.ShapeDtypeStruct((M, N), out_dtype),
        compiler_params=_cparams(("parallel", "parallel")),
    )(a, *([b] * ns), *extra_args)


def _rstd(x):
    return lax.rsqrt(jnp.mean(x * x, axis=-1, keepdims=True) + EPS)


def _row_spec(tr, d):
    return pl.BlockSpec((tr, d), lambda i: (i, 0))


def _vec_spec(d):
    return pl.BlockSpec((1, d), lambda i: (0, 0))


def _acc_rows(ref, i, val, cols=slice(None)):
    @pl.when(i == 0)
    def _():
        ref[:, cols] = val

    @pl.when(i > 0)
    def _():
        ref[:, cols] += val


def _norm_pre(x, gain, *, name, tr=256):
    t, d = x.shape
    tr = _pick(t, tr, 8)

    def body(x_ref, g_ref, h_ref):
        xv = x_ref[...]
        h_ref[...] = (xv * _rstd(xv) * g_ref[...]).astype(BF16)

    return pl.pallas_call(
        body, name=name, grid=(t // tr,), in_specs=[_row_spec(tr, d), _vec_spec(d)], out_specs=_row_spec(tr, d),
        out_shape=jax.ShapeDtypeStruct((t, d), BF16), compiler_params=_cparams(("parallel",)),
    )(x, gain)


def _post_then_pre(x, y, post_gain, pre_gain, *, name, tr=256):
    t, d = x.shape
    tr = _pick(t, tr, 8)

    def body(x_ref, y_ref, pg_ref, ng_ref, xn_ref, h_ref):
        yv = y_ref[...]
        xn = x_ref[...] + yv * _rstd(yv) * pg_ref[...]
        xn_ref[...] = xn
        h_ref[...] = (xn * _rstd(xn) * ng_ref[...]).astype(BF16)

    return pl.pallas_call(
        body, name=name, grid=(t // tr,),
        in_specs=[_row_spec(tr, d), _row_spec(tr, d), _vec_spec(d), _vec_spec(d)],
        out_specs=[_row_spec(tr, d), _row_spec(tr, d)],
        out_shape=[jax.ShapeDtypeStruct((t, d), F32), jax.ShapeDtypeStruct((t, d), BF16)],
        compiler_params=_cparams(("parallel",)),
    )(x, y, post_gain, pre_gain)


def _norm_bwd(dy, n, r, gain):
    dn = dy * gain
    return r * (dn - n * jnp.mean(dn * n, axis=-1, keepdims=True))


def _loss_head(x, y, post_gain, target, *, name, tr=256):
    t, d = x.shape
    tr = _pick(t, tr, 8)

    def body(x_ref, y_ref, pg_ref, t_ref, loss_ref, dx_ref, dy_ref, dpg_ref):
        i = pl.program_id(0)
        yv = y_ref[...]
        r = _rstd(yv)
        n = yv * r
        err = x_ref[...] + n * pg_ref[...] - t_ref[...]
        dx = err * (1.0 / d)
        dx_ref[...] = dx
        part = 0.5 * jnp.sum(jnp.mean(err * err, axis=-1, keepdims=True), axis=0, keepdims=True)
        _acc_rows(loss_ref, i, jnp.broadcast_to(part, (1, LANES)))
        _acc_rows(dpg_ref, i, jnp.sum(dx * n, axis=0, keepdims=True))
        dy_ref[...] = _norm_bwd(dx, n, r, pg_ref[...]).astype(BF16)

    return pl.pallas_call(
        body, name=name, grid=(t // tr,),
        in_specs=[_row_spec(tr, d), _row_spec(tr, d), _vec_spec(d), _row_spec(tr, d)],
        out_specs=[_vec_spec(LANES), _row_spec(tr, d), _row_spec(tr, d), _vec_spec(d)],
        out_shape=[jax.ShapeDtypeStruct((1, LANES), F32), jax.ShapeDtypeStruct((t, d), F32),
                   jax.ShapeDtypeStruct((t, d), BF16), jax.ShapeDtypeStruct((1, d), F32)],
        compiler_params=_cparams(("arbitrary",)),
    )(x, y, post_gain, target)


def _mid_bwd(dx_out, dh, x, pre_gain, y_prev, post_gain_prev, *, name, tr=256):
    t, d = x.shape
    tr = _pick(t, tr, 8)

    def body(dxo_ref, dh_ref, x_ref, ng_ref, y_ref, pg_ref, dx_ref, dy_ref, dng_ref, dpg_ref):
        i = pl.program_id(0)
        xv = x_ref[...]
        r = _rstd(xv)
        xh = xv * r
        dhv = dh_ref[...]
        _acc_rows(dng_ref, i, jnp.sum(dhv * xh, axis=0, keepdims=True))
        dx = dxo_ref[...] + _norm_bwd(dhv, xh, r, ng_ref[...])
        dx_ref[...] = dx
        yv = y_ref[...]
        ry = _rstd(yv)
        n = yv * ry
        _acc_rows(dpg_ref, i, jnp.sum(dx * n, axis=0, keepdims=True))
        dy_ref[...] = _norm_bwd(dx, n, ry, pg_ref[...]).astype(BF16)

    return pl.pallas_call(
        body, name=name, grid=(t // tr,),
        in_specs=[_row_spec(tr, d), _row_spec(tr, d), _row_spec(tr, d), _vec_spec(d), _row_spec(tr, d), _vec_spec(d)],
        out_specs=[_row_spec(tr, d), _row_spec(tr, d), _vec_spec(d), _vec_spec(d)],
        out_shape=[jax.ShapeDtypeStruct((t, d), F32), jax.ShapeDtypeStruct((t, d), BF16),
                   jax.ShapeDtypeStruct((1, d), F32), jax.ShapeDtypeStruct((1, d), F32)],
        compiler_params=_cparams(("arbitrary",)),
    )(dx_out, dh, x, pre_gain, y_prev, post_gain_prev)


def _first_bwd(dx_out, dh, x, pre_gain, *, name, tr=256):
    t, d = x.shape
    tr = _pick(t, tr, 8)

    def body(dxo_ref, dh_ref, x_ref, ng_ref, dx_ref, dng_ref):
        i = pl.program_id(0)
        xv = x_ref[...]
        r = _rstd(xv)
        xh = xv * r
        dhv = dh_ref[...]
        _acc_rows(dng_ref, i, jnp.sum(dhv * xh, axis=0, keepdims=True))
        dx_ref[...] = dxo_ref[...] + _norm_bwd(dhv, xh, r, ng_ref[...])

    return pl.pallas_call(
        body, name=name, grid=(t // tr,),
        in_specs=[_row_spec(tr, d), _row_spec(tr, d), _row_spec(tr, d), _vec_spec(d)],
        out_specs=[_row_spec(tr, d), _vec_spec(d)],
        out_shape=[jax.ShapeDtypeStruct((t, d), F32), jax.ShapeDtypeStruct((1, d), F32)],
        compiler_params=_cparams(("arbitrary",)),
    )(dx_out, dh, x, pre_gain)


def _sigmoid(x):
    return 1.0 / (1.0 + jnp.exp(-x))


def _log_sigmoid(x):
    return jnp.minimum(x, 0.0) - jnp.log(1.0 + jnp.exp(-jnp.abs(x)))


_GELU_C = math.sqrt(2.0 / math.pi)


_GELU_A = 0.044715


def _gelu_parts(x, with_grad=True):
    x2 = x * x
    h = 0.5 * jnp.tanh(x * (_GELU_C + (_GELU_C * _GELU_A) * x2)) + 0.5
    val = x * h
    if not with_grad:
        return val, None
    return val, h * (1.0 + (1.0 - h) * (x * (2.0 * _GELU_C + (6.0 * _GELU_C * _GELU_A) * x2)))


def _split3(x):
    hi = x.astype(BF16)
    r1 = x - hi.astype(F32)
    mid = r1.astype(BF16)
    lo = (r1 - mid.astype(F32)).astype(BF16)
    return hi, mid, lo


def _tri_matmul(tri_bf16, x):
    hi, mid, lo = _split3(x)
    return _dot_nn(tri_bf16, hi) + _dot_nn(tri_bf16, mid) + _dot_nn(tri_bf16, lo)


def _gla_dims(d):
    dk, dv = d // 2, d
    return dk, dv, dk // GLA_HEADS, dv // GLA_HEADS


def _col_pieces(a, b, lay):
    ws, wp = lay
    out = []
    while a < b:
        j = a // ws
        end = min(b, (j + 1) * ws)
        out.append((j * wp + a - j * ws, end - a))
        a = end
    return out


def _load_cols(ref, a, b, lay):
    parts = [ref[:, s:s + n] for s, n in _col_pieces(a, b, lay)]
    return parts[0] if len(parts) == 1 else jnp.concatenate(parts, axis=1)


def _store_cols(ref, a, val, lay):
    off = 0
    for s, n in _col_pieces(a, a + val.shape[1], lay):
        ref[:, s:s + n] = val[:, off:off + n]
        off += n


def _gate_window(c_r, lay):
    (start, _), = _col_pieces(c_r, c_r + GLA_GATE_RANK, lay)
    assert (start % lay[1]) + LANES <= lay[1]
    return slice(start, start + LANES)


def _gla_gates(glr, k, w2_ref, b_ref):
    z = _dot_nn(glr.astype(BF16), w2_ref[...].astype(BF16)) + b_ref[...]
    la = _log_sigmoid(z) * (1.0 / GLA_TAU)
    row = lax.broadcasted_iota(jnp.int32, (CHUNK, CHUNK), 0)
    col = lax.broadcasted_iota(jnp.int32, (CHUNK, CHUNK), 1)
    incl = (row >= col).astype(BF16)
    bcum = _tri_matmul(incl, la)
    b_end = bcum[CHUNK - 1:CHUNK, :]
    e_rest = jnp.exp(b_end - bcum)
    return z, e_rest, k * e_rest, jnp.exp(b_end)


def _gla_fwd(proj, w2p, b_gate, o_gain, lay, *, name):
    t, wcols = proj.shape
    d = o_gain.shape[1]
    dk, dv, dkh, dvh = _gla_dims(d)
    nc = t // CHUNK
    c_k, c_v, c_g, c_r = dk, 2 * dk, 2 * dk + dv, 2 * dk + 2 * dv
    scale = dkh ** -0.5

    def body(p_ref, w2_ref, b_ref, og_ref, o_ref, a_ref, sb_ref, sfin_ref, s_ref):
        i = pl.program_id(0)

        @pl.when(i == 0)
        def _():
            s_ref[...] = jnp.zeros_like(s_ref)

        q = _load_cols(p_ref, 0, dk, lay) * scale
        k = _load_cols(p_ref, c_k, c_k + dk, lay)
        glr = p_ref[:, _gate_window(c_r, lay)]
        _, _, kdec, decay = _gla_gates(glr, k, w2_ref, b_ref)
        for h in range(GLA_HEADS):
            ks = slice(h * dkh, (h + 1) * dkh)
            vs = slice(h * dvh, (h + 1) * dvh)
            v_h = _load_cols(p_ref, c_v + h * dvh, c_v + (h + 1) * dvh, lay)
            g_h = _load_cols(p_ref, c_g + h * dvh, c_g + (h + 1) * dvh, lay)
            s_old = s_ref[h]
            sb_ref[0, h] = s_old
            s_new = s_old * decay[:, ks] + _dot_tn(v_h.astype(BF16), kdec[:, ks].astype(BF16))
            s_ref[h] = s_new
            o_h = _dot_nt(q[:, ks].astype(BF16), s_new.astype(BF16))
            o_ref[:, vs] = o_h
            on = o_h * _rstd(o_h)
            a_ref[:, vs] = (on * og_ref[:, vs] * (g_h * _sigmoid(g_h))).astype(BF16)

        @pl.when(i == nc - 1)
        def _():
            sfin_ref[...] = s_ref[...]

    full = lambda *shape: pl.BlockSpec(shape, lambda i: (0,) * len(shape))
    return pl.pallas_call(
        body, name=name, grid=(nc,),
        in_specs=[pl.BlockSpec((CHUNK, wcols), lambda i: (i, 0)), full(LANES, dk), full(1, dk), full(1, dv)],
        out_specs=[pl.BlockSpec((CHUNK, dv), lambda i: (i, 0)), pl.BlockSpec((CHUNK, dv), lambda i: (i, 0)),
                   pl.BlockSpec((1, GLA_HEADS, dvh, dkh), lambda i: (i, 0, 0, 0)), full(GLA_HEADS, dvh, dkh)],
        out_shape=[jax.ShapeDtypeStruct((t, dv), F32), jax.ShapeDtypeStruct((t, dv), BF16),
                   jax.ShapeDtypeStruct((nc, GLA_HEADS, dvh, dkh), F32),
                   jax.ShapeDtypeStruct((GLA_HEADS, dvh, dkh), F32)],
        scratch_shapes=[pltpu.VMEM((GLA_HEADS, dvh, dkh), F32)],
        compiler_params=_cparams(("arbitrary",)),
    )(proj, w2p, b_gate, o_gain)


def _gla_bwd(da, o, proj, w2p, b_gate, o_gain, s_before, s_final, lay, *, name):
    t, wcols = proj.shape
    d = o_gain.shape[1]
    dk, dv, dkh, dvh = _gla_dims(d)
    nc = t // CHUNK
    c_k, c_v, c_g, c_r = dk, 2 * dk, 2 * dk + dv, 2 * dk + 2 * dv
    scale = dkh ** -0.5

    def body(da_ref, o_ref, p_ref, w2_ref, b_ref, og_ref, sb_ref, sfin_ref,
             dp_ref, dog_ref, db_ref, dw2_ref, s_ref, gc_ref, dkd_ref):
        i = pl.program_id(0)

        @pl.when(i == 0)
        def _():
            s_ref[...] = sfin_ref[...]
            gc_ref[...] = jnp.zeros_like(gc_ref)

        ws, wp = lay
        for j in range(N_CHIPS):
            dp_ref[:, j * wp + ws:(j + 1) * wp] = jnp.zeros((CHUNK, wp - ws), BF16)
        q = _load_cols(p_ref, 0, dk, lay) * scale
        k = _load_cols(p_ref, c_k, c_k + dk, lay)
        glr = p_ref[:, _gate_window(c_r, lay)]
        z, e_rest, kdec, decay = _gla_gates(glr, k, w2_ref, b_ref)
        ddecay = []
        for h in range(GLA_HEADS):
            ks = slice(h * dkh, (h + 1) * dkh)
            vs = slice(h * dvh, (h + 1) * dvh)
            v_h = _load_cols(p_ref, c_v + h * dvh, c_v + (h + 1) * dvh, lay)
            g_h = _load_cols(p_ref, c_g + h * dvh, c_g + (h + 1) * dvh, lay)
            da_h = da_ref[:, vs]
            o_h = o_ref[:, vs]
            og_h = og_ref[:, vs]
            r = _rstd(o_h)
            on = o_h * r
            sg = _sigmoid(g_h)
            silu = g_h * sg
            _acc_rows(dog_ref, i, jnp.sum(da_h * silu * on, axis=0, keepdims=True), vs)
            _store_cols(dp_ref, c_g + h * dvh, (da_h * (on * og_h) * (sg * (1.0 + g_h * (1.0 - sg)))).astype(BF16),
                        lay)
            don = da_h * silu * og_h
            do_h = (r * (don - on * jnp.mean(don * on, axis=-1, keepdims=True))).astype(BF16)
            s_cur = s_ref[h]
            _store_cols(dp_ref, h * dkh, (_dot_nn(do_h, s_cur.astype(BF16)) * scale).astype(BF16), lay)
            g_tot = gc_ref[h] + _dot_tn(do_h, q[:, ks].astype(BF16))
            g_bf = g_tot.astype(BF16)
            dkd_ref[:, ks] = _dot_nn(v_h.astype(BF16), g_bf)
            _store_cols(dp_ref, c_v + h * dvh, _dot_nt(kdec[:, ks].astype(BF16), g_bf).astype(BF16), lay)
            s_prev = sb_ref[0, h]
            ddecay.append(jnp.sum(g_tot * s_prev, axis=0, keepdims=True))
            gc_ref[h] = g_tot * decay[:, ks]
            s_ref[h] = s_prev
        dkdec = dkd_ref[...]
        _store_cols(dp_ref, c_k, (dkdec * e_rest).astype(BF16), lay)
        d_e = dkdec * kdec
        row = lax.broadcasted_iota(jnp.int32, (CHUNK, CHUNK), 0)
        col = lax.broadcasted_iota(jnp.int32, (CHUNK, CHUNK), 1)
        excl = (row > col).astype(BF16)
        dla = jnp.concatenate(ddecay, axis=1) * decay + _tri_matmul(excl, d_e)
        dz = dla * (1.0 / GLA_TAU) * (1.0 - _sigmoid(z))
        _acc_rows(db_ref, i, jnp.sum(dz, axis=0, keepdims=True))
        dz_bf = dz.astype(BF16)
        dw2 = _dot_tn(glr.astype(BF16), dz_bf)

        @pl.when(i == 0)
        def _():
            dw2_ref[...] = dw2

        @pl.when(i > 0)
        def _():
            dw2_ref[...] += dw2

        dp_ref[:, _gate_window(c_r, lay)] = _dot_nt(dz_bf, w2_ref[...].astype(BF16)).astype(BF16)

    rev = lambda i: (nc - 1 - i, 0)
    full = lambda *shape: pl.BlockSpec(shape, lambda i: (0,) * len(shape))
    return pl.pallas_call(
        body, name=name, grid=(nc,),
        in_specs=[pl.BlockSpec((CHUNK, dv), rev), pl.BlockSpec((CHUNK, dv), rev), pl.BlockSpec((CHUNK, wcols), rev),
                  full(LANES, dk), full(1, dk), full(1, dv),
                  pl.BlockSpec((1, GLA_HEADS, dvh, dkh), lambda i: (nc - 1 - i, 0, 0, 0)), full(GLA_HEADS, dvh, dkh)],
        out_specs=[pl.BlockSpec((CHUNK, wcols), rev), full(1, dv), full(1, dk), full(LANES, dk)],
        out_shape=[jax.ShapeDtypeStruct((t, wcols), BF16), jax.ShapeDtypeStruct((1, dv), F32),
                   jax.ShapeDtypeStruct((1, dk), F32), jax.ShapeDtypeStruct((LANES, dk), F32)],
        scratch_shapes=[pltpu.VMEM((GLA_HEADS, dvh, dkh), F32), pltpu.VMEM((GLA_HEADS, dvh, dkh), F32),
                        pltpu.VMEM((CHUNK, dk), F32)],
        compiler_params=_cparams(("arbitrary",)),
    )(da, o, proj, w2p, b_gate, o_gain, s_before, s_final)


def _sgu_mid(p_ref, lg_ref, lb_ref, ws_ref, bst_ref, w, with_grad=True):
    gd = w // SGU_GROUPS
    u_act, du_fac = _gelu_parts(p_ref[:, 0:w], with_grad)
    vf, dv_fac = _gelu_parts(p_ref[:, w:2 * w], with_grad)
    mu = jnp.mean(vf, axis=-1, keepdims=True)
    cen = vf - mu
    rstd = lax.rsqrt(jnp.mean(cen * cen, axis=-1, keepdims=True) + EPS)
    xh = cen * rstd
    vn = (xh * lg_ref[...] + lb_ref[...]).astype(BF16)
    vs = [_dot_nn(ws_ref[g].astype(BF16), vn[:, g * gd:(g + 1) * gd]) + bst_ref[:, g:g + 1]
          for g in range(SGU_GROUPS)]
    return u_act, du_fac, dv_fac, rstd, xh, vn, vs


def _sgu_fwd(proj, ln_gain, ln_bias, ws_masked, bs_t, *, name):
    t, w3 = proj.shape
    w = w3 // 3
    gd = w // SGU_GROUPS
    nb = t // SGU_BLOCK

    def body(p_ref, lg_ref, lb_ref, ws_ref, bst_ref, a_ref):
        u_act, _, _, _, _, _, vs = _sgu_mid(p_ref, lg_ref, lb_ref, ws_ref, bst_ref, w, with_grad=False)
        for g in range(SGU_GROUPS):
            cs = slice(g * gd, (g + 1) * gd)
            gate = p_ref[:, 2 * w + g * gd:2 * w + (g + 1) * gd]
            a_ref[:, cs] = (u_act[:, cs] * vs[g] * (gate * _sigmoid(gate))).astype(BF16)

    full = lambda *shape: pl.BlockSpec(shape, lambda i: (0,) * len(shape))
    return pl.pallas_call(
        body, name=name, grid=(nb,),
        in_specs=[pl.BlockSpec((SGU_BLOCK, w3), lambda i: (i, 0)), full(1, w), full(1, w),
                  full(SGU_GROUPS, SGU_BLOCK, SGU_BLOCK), full(SGU_BLOCK, SGU_GROUPS)],
        out_specs=pl.BlockSpec((SGU_BLOCK, w), lambda i: (i, 0)),
        out_shape=jax.ShapeDtypeStruct((t, w), BF16),
        compiler_params=_cparams(("parallel",)),
    )(proj, ln_gain, ln_bias, ws_masked, bs_t)


def _sgu_bwd(da, proj, ln_gain, ln_bias, ws_masked, ws_masked_t, bs_t, *, name):
    t, w3 = proj.shape
    w = w3 // 3
    gd = w // SGU_GROUPS
    nb = t // SGU_BLOCK

    def body(da_ref, p_ref, lg_ref, lb_ref, ws_ref, wst_ref, bst_ref, dp_ref, dws_ref, dbst_ref, dlg_ref, dlb_ref,
             dvn_ref):
        i = pl.program_id(0)
        u_act, du_fac, dv_fac, rstd, xh, vn, vs = _sgu_mid(p_ref, lg_ref, lb_ref, ws_ref, bst_ref, w)
        for g in range(SGU_GROUPS):
            cs = slice(g * gd, (g + 1) * gd)
            gate = p_ref[:, 2 * w + g * gd:2 * w + (g + 1) * gd]
            sg = _sigmoid(gate)
            silu = gate * sg
            da_g = da_ref[:, cs]
            ua_g = u_act[:, cs]
            dp_ref[:, cs] = (da_g * vs[g] * silu * du_fac[:, cs]).astype(BF16)
            dp_ref[:, 2 * w + g * gd:2 * w + (g + 1) * gd] = (
                da_g * ua_g * vs[g] * (sg * (1.0 + gate * (1.0 - sg)))).astype(BF16)
            dvs = da_g * ua_g * silu
            dvs_bf = dvs.astype(BF16)
            dvn_ref[:, cs] = _dot_nn(wst_ref[g].astype(BF16), dvs_bf)
            dws = _dot_nt(dvs_bf, vn[:, cs])
            dbs = jnp.sum(dvs, axis=1, keepdims=True)

            @pl.when(i == 0)
            def _():
                dws_ref[g] = dws
                dbst_ref[:, g:g + 1] = dbs

            @pl.when(i > 0)
            def _():
                dws_ref[g] += dws
                dbst_ref[:, g:g + 1] += dbs

        dvn = dvn_ref[...]
        _acc_rows(dlg_ref, i, jnp.sum(dvn * xh, axis=0, keepdims=True))
        _acc_rows(dlb_ref, i, jnp.sum(dvn, axis=0, keepdims=True))
        dxh = dvn * lg_ref[...]
        dvf = rstd * (dxh - jnp.mean(dxh, axis=-1, keepdims=True)
                      - xh * jnp.mean(dxh * xh, axis=-1, keepdims=True))
        dp_ref[:, w:2 * w] = (dvf * dv_fac).astype(BF16)

    full = lambda *shape: pl.BlockSpec(shape, lambda i: (0,) * len(shape))
    return pl.pallas_call(
        body, name=name, grid=(nb,),
        in_specs=[pl.BlockSpec((SGU_BLOCK, w), lambda i: (i, 0)), pl.BlockSpec((SGU_BLOCK, w3), lambda i: (i, 0)),
                  full(1, w), full(1, w), full(SGU_GROUPS, SGU_BLOCK, SGU_BLOCK),
                  full(SGU_GROUPS, SGU_BLOCK, SGU_BLOCK), full(SGU_BLOCK, SGU_GROUPS)],
        out_specs=[pl.BlockSpec((SGU_BLOCK, w3), lambda i: (i, 0)), full(SGU_GROUPS, SGU_BLOCK, SGU_BLOCK),
                   full(SGU_BLOCK, SGU_GROUPS), full(1, w), full(1, w)],
        out_shape=[jax.ShapeDtypeStruct((t, w3), BF16), jax.ShapeDtypeStruct((SGU_GROUPS, SGU_BLOCK, SGU_BLOCK), F32),
                   jax.ShapeDtypeStruct((SGU_BLOCK, SGU_GROUPS), F32), jax.ShapeDtypeStruct((1, w), F32),
                   jax.ShapeDtypeStruct((1, w), F32)],
        scratch_shapes=[pltpu.VMEM((SGU_BLOCK, w), F32)],
        compiler_params=_cparams(("arbitrary",)),
    )(da, proj, ln_gain, ln_bias, ws_masked, ws_masked_t, bs_t)


def _tile2d(rows, cols, block_bytes, row_unit):
    if rows % row_unit == 0:
        return _pick(rows, max(row_unit, block_bytes // (4 * cols)), row_unit), cols
    return rows, _pick(cols, max(LANES, block_bytes // (4 * rows)))


def _adamw(w, g, m, v, *, name, block_bytes=1 << 20, after=None):
    rows, cols = w.shape
    tr, tc = _tile2d(rows, cols, block_bytes, 8)
    g_rows = g.shape[0]
    assert g_rows == rows or tr == rows
    extra_specs, extra_args = ([], []) if after is None else ([pl.BlockSpec(memory_space=pl.ANY)], [after])

    def body(w_ref, g_ref, m_ref, v_ref, *rest):
        go_ref, d_ref, mo_ref, vo_ref = rest[len(extra_args):]
        gv = g_ref[0:tr, :]
        go_ref[...] = gv
        mn = ADAM_B1 * m_ref[...] + (1.0 - ADAM_B1) * gv
        vn = ADAM_B2 * v_ref[...] + (1.0 - ADAM_B2) * (gv * gv)
        m_hat = mn / (1.0 - ADAM_B1 ** ADAM_STEP)
        v_hat = vn / (1.0 - ADAM_B2 ** ADAM_STEP)
        d_ref[...] = -ADAM_LR * (m_hat / (jnp.sqrt(v_hat) + ADAM_EPS) + ADAM_WD * w_ref[...])
        mo_ref[...] = mn
        vo_ref[...] = vn

    spec = pl.BlockSpec((tr, tc), lambda i, j: (i, j))
    g_spec = spec if g_rows == rows else pl.BlockSpec((g_rows, tc), lambda i, j: (0, j))
    return pl.pallas_call(
        body, name=name, grid=(rows // tr, cols // tc), in_specs=[spec, g_spec, spec, spec] + extra_specs,
        out_specs=[spec] * 4, out_shape=[jax.ShapeDtypeStruct((rows, cols), F32)] * 4,
        compiler_params=_cparams(("parallel", "parallel")),
    )(w, g, m, v, *extra_args)


def _matmul_dw_pair(a_me, a_sib, b_me, b_sib, core_idx, *, shards_on, name, after=None, part=(0, 1)):
    T, M = a_me.shape
    N = b_me.shape[1]
    if shards_on == "rows":
        p, count = part
        tm, hc = M // N_CHIPS, N // 2
        hp = hc // count
        tn = _pick(hp, 512)
        per = hp // tn
        grid = (N_CHIPS, per)
        a_spec = pl.BlockSpec((T, tm), lambda i, n, h: (0, i))
        b_me_spec = pl.BlockSpec((T, tn), lambda i, n, h: (0, (h[0] * count + p) * per + n))
        b_sib_spec = pl.BlockSpec((T, tn), lambda i, n, h: (0, p * per + n))
        out_spec = pl.BlockSpec((None, tm, tn), lambda i, n, h: (i, 0, n))
        out_shape = jax.ShapeDtypeStruct((N_CHIPS, tm, hp), BF16)
    else:
        tm, hc = _pick(M, 1024), N // N_CHIPS // 2
        grid = (M // tm, N_CHIPS)
        a_spec = pl.BlockSpec((T, tm), lambda i, j, h: (0, i))
        b_me_spec = pl.BlockSpec((T, hc), lambda i, j, h: (0, 2 * j + h[0]))
        b_sib_spec = pl.BlockSpec((T, hc), lambda i, j, h: (0, j))
        out_spec = pl.BlockSpec((None, tm, hc), lambda i, j, h: (j, i, 0))
        out_shape = jax.ShapeDtypeStruct((N_CHIPS, M, hc), BF16)
    extra_specs, extra_args = ([], []) if after is None else ([pl.BlockSpec(memory_space=pl.ANY)], [after])

    def body(h_ref, am_ref, as_ref, bm_ref, bs_ref, *rest):
        o_ref = rest[len(extra_args)]
        o_ref[...] = (_dot_tn(am_ref[...], bm_ref[...]) + _dot_tn(as_ref[...], bs_ref[...])).astype(BF16)

    grid_spec = pltpu.PrefetchScalarGridSpec(
        num_scalar_prefetch=1, grid=grid, in_specs=[a_spec, a_spec, b_me_spec, b_sib_spec] + extra_specs,
        out_specs=out_spec)
    return pl.pallas_call(
        body, name=name, grid_spec=grid_spec, out_shape=out_shape, compiler_params=_cparams(("parallel", "parallel")),
    )(core_idx, a_me, a_sib, b_me, b_sib, *extra_args)


def _chip_sum(pair, landed, slots, *, name, block_bytes=1 << 20, part=(0, 1), into=None):
    p, count = part
    _, r, hp = pair.shape
    tr, tc = _tile2d(r, hp, block_bytes, 16)
    ncb = hp // tc
    extra_specs, extra_args = ([], []) if into is None else ([pl.BlockSpec(memory_space=pl.ANY)], [into])

    def body(s_ref, own_ref, l0_ref, l1_ref, l2_ref, *rest):
        rest[-1][...] = ((own_ref[...].astype(F32) + l0_ref[...].astype(F32)) + l1_ref[...].astype(F32)
                         ) + l2_ref[...].astype(F32)

    def slab(which):
        return pl.BlockSpec((None, tr, tc), lambda i, k, s: (s[which], i, k))

    grid_spec = pltpu.PrefetchScalarGridSpec(
        num_scalar_prefetch=1, grid=(r // tr, ncb),
        in_specs=[slab(0), slab(1), slab(2), slab(3)] + extra_specs,
        out_specs=pl.BlockSpec((tr, tc), lambda i, k, s: (i, (s[4] * count + p) * ncb + k)))
    return pl.pallas_call(
        body, name=name, grid_spec=grid_spec, out_shape=jax.ShapeDtypeStruct((r, 2 * hp * count), F32),
        input_output_aliases={} if into is None else {5: 0},
        compiler_params=_cparams(("parallel", "parallel")),
    )(slots, pair, landed, landed, landed, *extra_args)


def _stack_sum(x, *, name, out_dtype=F32, block_bytes=1 << 20):
    s, r, c = x.shape
    tr = _pick(r, max(8, block_bytes // (4 * c)), 16) if r % 16 == 0 else r

    def body(x_ref, o_ref):
        acc = x_ref[0].astype(F32)
        for j in range(1, s):
            acc = acc + x_ref[j].astype(F32)
        o_ref[...] = acc.astype(out_dtype)

    return pl.pallas_call(
        body, name=name, grid=(r // tr,),
        in_specs=[pl.BlockSpec((s, tr, c), lambda i: (0, i, 0))], out_specs=pl.BlockSpec((tr, c), lambda i: (i, 0)),
        out_shape=jax.ShapeDtypeStruct((r, c), out_dtype), compiler_params=_cparams(("parallel",)),
    )(x)


HBM = pl.BlockSpec(memory_space=pltpu.HBM)


def _place():
    x, y, c = lax.axis_index("x"), lax.axis_index("y"), lax.axis_index("c")
    other_chips = [(1 - x, y), (x, 1 - y), (1 - x, 1 - y)]
    return x, y, c, other_chips


def _handshake(peers):
    barrier = pltpu.get_barrier_semaphore()
    for peer in peers:
        pl.semaphore_signal(barrier, inc=1, device_id=peer, device_id_type=MESH)
    pl.semaphore_wait(barrier, len(peers))


def _sibling():
    x, y, c, _ = _place()
    return [(x, y, 1 - c)]


def _same_core_chips():
    x, y, c, chips = _place()
    return [(cx, cy, c) for cx, cy in chips]


def _same_core_neighbours():
    x, y, c, _ = _place()
    return [(1 - x, y, c), (x, 1 - y, c)]


def _split_params(cid):
    return pltpu.CompilerParams(has_side_effects=SIDE_EFFECT, collective_id=cid)


def _half_cols(cols, which):
    hc = cols // 2
    return pl.ds(pl.multiple_of(which * hc, LANES), hc)


SEM = pl.BlockSpec(memory_space=pltpu.SEMAPHORE)
ANY = pl.BlockSpec(memory_space=pl.ANY)
SIDE_EFFECT = pltpu.SideEffectType.DATAFLOW_SIDE_EFFECTING
TOKEN_SHAPE = (8, LANES)


def _hbm(shape, dtype):
    return pltpu.HBM(shape, dtype)


def _in_hbm(a):
    return pltpu.with_memory_space_constraint(a, pltpu.HBM)


def _gather_copy(src_ref, land_ref, ssem, rsem, k, chip_of_block, to, c):
    cols = src_ref.shape[1]
    return pltpu.make_async_remote_copy(
        src_ref=src_ref.at[:, _half_cols(cols, c)], dst_ref=land_ref.at[chip_of_block, :, _half_cols(cols, c)],
        send_sem=ssem.at[k], recv_sem=rsem.at[k], device_id=to, device_id_type=MESH)


NEIGHBOURS = (0, 1)
ALL_CHIPS = (0, 1, 2)


def _gather_start(shards, *, name, cid, after=(), relayed=()):
    n = len(shards)
    after = list(after)

    def body(*refs):
        srcs, lands = refs[:n], refs[n:2 * n]
        outs = refs[2 * n + len(after):]
        token = outs[-1]
        _handshake(_same_core_chips())
        x, y, c, chips = _place()
        me = 2 * x + y
        for a in range(n):
            ssem, rsem = outs[4 * a], outs[4 * a + 1]
            for k in NEIGHBOURS if a in relayed else ALL_CHIPS:
                cx, cy = chips[k]
                _gather_copy(srcs[a], lands[a], ssem, rsem, k, me, (cx, cy, c), c).start()
        token[...] = jnp.zeros_like(token)

    out_shape, out_specs, aliases = [], [], {}
    for a, s in enumerate(shards):
        out_shape += [pltpu.SemaphoreType.DMA((3,)), pltpu.SemaphoreType.DMA((3,)), _hbm(s.shape, s.dtype),
                      _hbm((N_CHIPS,) + s.shape, s.dtype)]
        out_specs += [SEM, SEM, HBM, HBM]
        aliases[a] = 4 * a + 2
        aliases[n + a] = 4 * a + 3
    out_shape.append(jax.ShapeDtypeStruct(TOKEN_SHAPE, F32))
    out_specs.append(pl.BlockSpec(memory_space=pltpu.VMEM))
    lands = [_in_hbm(lax.empty((N_CHIPS,) + s.shape, s.dtype)) for s in shards]
    res = pl.pallas_call(
        body, name=name, in_specs=[HBM] * (2 * n) + [ANY] * len(after), out_specs=out_specs, out_shape=out_shape,
        input_output_aliases=aliases, compiler_params=_split_params(cid),
    )(*[_in_hbm(s) for s in shards], *lands, *after)
    return [tuple(res[4 * a:4 * a + 4]) for a in range(n)], res[-1]


def _wait_call(wait_fn, parts, after, *, name):
    ssem, rsem, src, land = parts
    after = list(after) if isinstance(after, (list, tuple)) else [after]

    def body(src_ref, land_ref, ssem_ref, rsem_ref, *rest):
        wait_fn(src_ref, land_ref, ssem_ref, rsem_ref)

    return pl.pallas_call(
        body, name=name, in_specs=[HBM, HBM, SEM, SEM] + [ANY] * len(after), out_specs=[HBM, HBM],
        out_shape=[_hbm(src.shape, src.dtype), _hbm(land.shape, land.dtype)], input_output_aliases={0: 0, 1: 1},
        compiler_params=pltpu.CompilerParams(has_side_effects=SIDE_EFFECT),
    )(src, land, ssem, rsem, *after)


def _gather_wait(parts, after, *, name, ks=ALL_CHIPS):
    def wait(src_ref, land_ref, ssem_ref, rsem_ref):
        x, y, c, chips = _place()
        for k in ks:
            cx, cy = chips[k]
            cp = _gather_copy(src_ref, land_ref, ssem_ref, rsem_ref, k, 2 * cx + cy, (x, y, c), c)
            cp.wait_send()
            cp.wait_recv()

    return _wait_call(wait, parts, after, name=name)


def _relay_copy(buf_ref, ssem, rsem, k, slab, to, c):
    hr = buf_ref.shape[1] // 2
    part = buf_ref.at[slab, pl.ds(k * hr, hr), _half_cols(buf_ref.shape[2], c)]
    return pltpu.make_async_remote_copy(
        src_ref=part, dst_ref=part, send_sem=ssem.at[k], recv_sem=rsem.at[k], device_id=to, device_id_type=MESH)


def _relay_start(land, *, name, cid):
    def body(buf_ref, ssem, rsem, buf_out, token):
        _handshake(_same_core_neighbours())
        x, y, c, _ = _place()
        _relay_copy(buf_ref, ssem, rsem, 0, 2 * (1 - x) + y, (x, 1 - y, c), c).start()
        _relay_copy(buf_ref, ssem, rsem, 1, 2 * x + 1 - y, (1 - x, y, c), c).start()
        token[...] = jnp.zeros_like(token)

    res = pl.pallas_call(
        body, name=name, in_specs=[HBM], out_specs=[SEM, SEM, HBM, pl.BlockSpec(memory_space=pltpu.VMEM)],
        out_shape=[pltpu.SemaphoreType.DMA((2,)), pltpu.SemaphoreType.DMA((2,)), _hbm(land.shape, land.dtype),
                   jax.ShapeDtypeStruct(TOKEN_SHAPE, F32)],
        input_output_aliases={0: 2}, compiler_params=_split_params(cid),
    )(land)
    return tuple(res[:3]), res[3]


def _relay_wait(parts, after, *, name):
    ssem, rsem, buf = parts
    after = list(after) if isinstance(after, (list, tuple)) else [after]

    def body(buf_ref, ssem_ref, rsem_ref, *rest):
        x, y, c, _ = _place()
        diagonal = 2 * (1 - x) + 1 - y
        _relay_copy(buf_ref, ssem_ref, rsem_ref, 0, 2 * (1 - x) + y, (x, y, c), c).wait_send()
        _relay_copy(buf_ref, ssem_ref, rsem_ref, 1, 2 * x + 1 - y, (x, y, c), c).wait_send()
        _relay_copy(buf_ref, ssem_ref, rsem_ref, 0, diagonal, (x, y, c), c).wait_recv()
        _relay_copy(buf_ref, ssem_ref, rsem_ref, 1, diagonal, (x, y, c), c).wait_recv()

    return pl.pallas_call(
        body, name=name, in_specs=[HBM, SEM, SEM] + [ANY] * len(after), out_specs=HBM,
        out_shape=_hbm(buf.shape, buf.dtype), input_output_aliases={0: 0},
        compiler_params=pltpu.CompilerParams(has_side_effects=SIDE_EFFECT),
    )(buf, ssem, rsem, *after)


def _forward_copy(buf_ref, ssem, rsem, k, slab, which, to):
    part = buf_ref.at[slab, :, _half_cols(buf_ref.shape[2], which)]
    return pltpu.make_async_remote_copy(
        src_ref=part, dst_ref=part, send_sem=ssem.at[k], recv_sem=rsem.at[k], device_id=to, device_id_type=MESH)


def _sibling_forward(land, *, name, cid, ks=ALL_CHIPS):
    def body(_, buf, send_sems, recv_sems):
        _handshake(_sibling())
        x, y, c, chips = _place()
        copies = []
        for k in ks:
            cx, cy = chips[k]
            cp = _forward_copy(buf, send_sems, recv_sems, k, 2 * cx + cy, c, (x, y, 1 - c))
            cp.start()
            copies.append(cp)
        for k in ks:
            cx, cy = chips[k]
            _forward_copy(buf, send_sems, recv_sems, k, 2 * cx + cy, 1 - c, (x, y, c)).wait_recv()
        for cp in copies:
            cp.wait_send()

    return pl.pallas_call(
        body, name=name, in_specs=[HBM], out_specs=HBM, out_shape=jax.ShapeDtypeStruct(land.shape, land.dtype),
        input_output_aliases={0: 0},
        scratch_shapes=[pltpu.SemaphoreType.DMA((3,)), pltpu.SemaphoreType.DMA((3,))],
        compiler_params=pltpu.CompilerParams(collective_id=cid),
    )(land)


def _forward_start(land, *, name, cid, ks=ALL_CHIPS):
    def body(buf_ref, ssem, rsem, buf_out, token):
        _handshake(_sibling())
        x, y, c, chips = _place()
        for k in ks:
            cx, cy = chips[k]
            _forward_copy(buf_ref, ssem, rsem, k, 2 * cx + cy, c, (x, y, 1 - c)).start()
        token[...] = jnp.zeros_like(token)

    res = pl.pallas_call(
        body, name=name, in_specs=[HBM], out_specs=[SEM, SEM, HBM, pl.BlockSpec(memory_space=pltpu.VMEM)],
        out_shape=[pltpu.SemaphoreType.DMA((3,)), pltpu.SemaphoreType.DMA((3,)), _hbm(land.shape, land.dtype),
                   jax.ShapeDtypeStruct(TOKEN_SHAPE, F32)],
        input_output_aliases={0: 2}, compiler_params=_split_params(cid),
    )(land)
    return tuple(res[:3]), res[3]


def _forward_wait(parts, after, *, name, ks=ALL_CHIPS):
    ssem, rsem, buf = parts
    after = list(after) if isinstance(after, (list, tuple)) else [after]

    def body(buf_ref, ssem_ref, rsem_ref, *rest):
        x, y, c, chips = _place()
        for k in ks:
            cx, cy = chips[k]
            _forward_copy(buf_ref, ssem_ref, rsem_ref, k, 2 * cx + cy, c, (x, y, c)).wait_send()
            _forward_copy(buf_ref, ssem_ref, rsem_ref, k, 2 * cx + cy, 1 - c, (x, y, c)).wait_recv()

    return pl.pallas_call(
        body, name=name, in_specs=[HBM, SEM, SEM] + [ANY] * len(after), out_specs=HBM,
        out_shape=_hbm(buf.shape, buf.dtype), input_output_aliases={0: 0},
        compiler_params=pltpu.CompilerParams(has_side_effects=SIDE_EFFECT),
    )(buf, ssem, rsem, *after)


def _share_copy(buf_ref, ssem, rsem, a, which, to):
    part = buf_ref.at[:, _half_cols(buf_ref.shape[1], which)]
    return pltpu.make_async_remote_copy(
        src_ref=part, dst_ref=part, send_sem=ssem.at[a], recv_sem=rsem.at[a], device_id=to, device_id_type=MESH)


def _share_start(arrays, *, name, cid):
    n = len(arrays)

    def body(*refs):
        bufs, ssem, rsem, token = refs[:n], refs[n], refs[n + 1], refs[-1]
        _handshake(_sibling())
        x, y, c, _ = _place()
        for a in range(n):
            _share_copy(bufs[a], ssem, rsem, a, c, (x, y, 1 - c)).start()
        token[...] = jnp.zeros_like(token)

    res = pl.pallas_call(
        body, name=name, in_specs=[HBM] * n,
        out_specs=[SEM, SEM] + [HBM] * n + [pl.BlockSpec(memory_space=pltpu.VMEM)],
        out_shape=[pltpu.SemaphoreType.DMA((n,)), pltpu.SemaphoreType.DMA((n,))]
        + [_hbm(b.shape, b.dtype) for b in arrays] + [jax.ShapeDtypeStruct(TOKEN_SHAPE, F32)],
        input_output_aliases={a: 2 + a for a in range(n)}, compiler_params=_split_params(cid),
    )(*[_in_hbm(b) for b in arrays])
    return (res[0], res[1], list(res[2:2 + n])), res[-1]


def _share_wait(parts, after, *, name):
    ssem, rsem, bufs = parts
    n = len(bufs)
    after = list(after) if isinstance(after, (list, tuple)) else [after]

    def body(*refs):
        buf_refs, ssem_ref, rsem_ref = refs[:n], refs[n], refs[n + 1]
        x, y, c, _ = _place()
        for a in range(n):
            _share_copy(buf_refs[a], ssem_ref, rsem_ref, a, c, (x, y, c)).wait_send()
            _share_copy(buf_refs[a], ssem_ref, rsem_ref, a, 1 - c, (x, y, c)).wait_recv()

    return pl.pallas_call(
        body, name=name, in_specs=[HBM] * n + [SEM, SEM] + [ANY] * len(after), out_specs=[HBM] * n,
        out_shape=[_hbm(b.shape, b.dtype) for b in bufs], input_output_aliases={a: a for a in range(n)},
        compiler_params=pltpu.CompilerParams(has_side_effects=SIDE_EFFECT),
    )(*bufs, ssem, rsem, *after)


def _scatter_copy(src_ref, land_ref, ssem, rsem, k, src_slab, dst_slab, to):
    return pltpu.make_async_remote_copy(
        src_ref=src_ref.at[src_slab], dst_ref=land_ref.at[dst_slab], send_sem=ssem.at[k], recv_sem=rsem.at[k],
        device_id=to, device_id_type=MESH)


def _scatter_start(part, *, name, cid):
    def start(src_ref, land_ref, ssem, rsem):
        x, y, c, chips = _place()
        me = 2 * x + y
        for k, (cx, cy) in enumerate(chips):
            _scatter_copy(src_ref, land_ref, ssem, rsem, k, 2 * cx + cy, me, (cx, cy, c)).start()

    return _split_start(start, _same_core_chips, part, part.shape, N_CHIPS - 1, name=name, cid=cid)


def _scatter_wait(parts, after, *, name):
    def wait(src_ref, land_ref, ssem_ref, rsem_ref):
        x, y, c, chips = _place()
        for k, (cx, cy) in enumerate(chips):
            idx = 2 * cx + cy
            cp = _scatter_copy(src_ref, land_ref, ssem_ref, rsem_ref, k, idx, idx, (x, y, c))
            cp.wait_send()
            cp.wait_recv()

    return _wait_call(wait, parts, after, name=name)


def _split_start(start_fn, peers_fn, src, land_shape, n_sems, *, name, cid):
    def body(src_ref, land_ref, ssem, rsem, src_out, land_out, token):
        _handshake(peers_fn())
        start_fn(src_ref, land_ref, ssem, rsem)
        token[...] = jnp.zeros_like(token)

    res = pl.pallas_call(
        body, name=name, in_specs=[HBM, HBM], out_specs=[SEM, SEM, HBM, HBM, pl.BlockSpec(memory_space=pltpu.VMEM)],
        out_shape=[pltpu.SemaphoreType.DMA((n_sems,)), pltpu.SemaphoreType.DMA((n_sems,)), _hbm(src.shape, src.dtype),
                   _hbm(land_shape, src.dtype), jax.ShapeDtypeStruct(TOKEN_SHAPE, F32)],
        input_output_aliases={0: 2, 1: 3}, compiler_params=_split_params(cid),
    )(_in_hbm(src), _in_hbm(lax.empty(land_shape, src.dtype)))
    return tuple(res[:4]), res[4]


def _sibling_copies(src_ref, land_ref, ssem, rsem, k0, groups, which, to):
    def copy(k, src, dst):
        return pltpu.make_async_remote_copy(
            src_ref=src, dst_ref=dst, send_sem=ssem.at[k], recv_sem=rsem.at[k], device_id=to, device_id_type=MESH)

    if groups == 0:
        return [copy(k0, src_ref, land_ref)]
    hw = src_ref.shape[1] // groups // 2
    return [copy(k0 + j, src_ref.at[:, pl.ds(pl.multiple_of((2 * j + which) * hw, LANES), hw)],
                 land_ref.at[:, j * hw:(j + 1) * hw]) for j in range(groups)]


def _to_sibling_start(items, *, name, cid):
    n = len(items)
    shapes = [a.shape if g == 0 else (a.shape[0], a.shape[1] // 2) for a, g in items]
    first = [sum(max(g, 1) for _, g in items[:k]) for k in range(n + 1)]

    def body(*refs):
        srcs, lands, ssem, rsem, token = refs[:n], refs[n:2 * n], refs[2 * n], refs[2 * n + 1], refs[-1]
        _handshake(_sibling())
        x, y, c, _ = _place()
        for k, (_, g) in enumerate(items):
            for cp in _sibling_copies(srcs[k], lands[k], ssem, rsem, first[k], g, 1 - c, (x, y, 1 - c)):
                cp.start()
        token[...] = jnp.zeros_like(token)

    res = pl.pallas_call(
        body, name=name, in_specs=[HBM] * (2 * n),
        out_specs=[SEM, SEM] + [HBM] * (2 * n) + [pl.BlockSpec(memory_space=pltpu.VMEM)],
        out_shape=[pltpu.SemaphoreType.DMA((first[n],)), pltpu.SemaphoreType.DMA((first[n],))]
        + [_hbm(a.shape, a.dtype) for a, _ in items] + [_hbm(s, a.dtype) for s, (a, _) in zip(shapes, items)]
        + [jax.ShapeDtypeStruct(TOKEN_SHAPE, F32)],
        input_output_aliases={k: 2 + k for k in range(2 * n)}, compiler_params=_split_params(cid),
    )(*[_in_hbm(a) for a, _ in items], *[_in_hbm(lax.empty(s, a.dtype)) for s, (a, _) in zip(shapes, items)])
    return [(res[0], res[1], first[k], g, res[2 + k], res[2 + n + k]) for k, (_, g) in enumerate(items)], res[-1]


def _from_sibling(flight, after, *, name):
    ssem, rsem, k0, groups, src, land = flight

    def wait(src_ref, land_ref, ssem_ref, rsem_ref):
        x, y, c, _ = _place()
        for cp in _sibling_copies(src_ref, land_ref, ssem_ref, rsem_ref, k0, groups, 1 - c, (x, y, c)):
            cp.wait_send()
            cp.wait_recv()

    return _wait_call(wait, (ssem, rsem, src, land), after, name=name)


def _dev_peers(x, y, c, chips):
    return [(x, y, 1 - c)] + [(cx, cy, c) for cx, cy in chips] + [(cx, cy, 1 - c) for cx, cy in chips]


def _dev_gather_start(part, *, name, cid):
    def start(src_ref, land_ref, ssem, rsem):
        x, y, c, chips = _place()
        for k, to in enumerate(_dev_peers(x, y, c, chips)):
            pltpu.make_async_remote_copy(
                src_ref=src_ref, dst_ref=land_ref.at[4 * x + 2 * y + c], send_sem=ssem.at[k], recv_sem=rsem.at[k],
                device_id=to, device_id_type=MESH).start()

    return _split_start(start, lambda: _dev_peers(*_place()), part, (N_DEV,) + part.shape, N_DEV - 1, name=name,
                        cid=cid)


def _dev_gather_wait(parts, after, *, name):
    def wait(src_ref, land_ref, ssem_ref, rsem_ref):
        x, y, c, chips = _place()
        for k, (px, py, pc) in enumerate(_dev_peers(x, y, c, chips)):
            cp = pltpu.make_async_remote_copy(
                src_ref=src_ref, dst_ref=land_ref.at[4 * px + 2 * py + pc], send_sem=ssem_ref.at[k],
                recv_sem=rsem_ref.at[k], device_id=(x, y, c), device_id_type=MESH)
            cp.wait_send()
            cp.wait_recv()

    return _wait_call(wait, parts, after, name=name)[1]


def _sibling_share_halves(arrays, *, name, cid):
    n = len(arrays)

    def body(*refs):
        bufs = refs[n:2 * n]
        send_sems, recv_sems = refs[2 * n:]
        _handshake(_sibling())
        x, y, c, _ = _place()
        copies = []
        for a in range(n):
            mine = bufs[a].at[:, _half_cols(bufs[a].shape[1], c)]
            cp = pltpu.make_async_remote_copy(
                src_ref=mine, dst_ref=mine, send_sem=send_sems.at[a], recv_sem=recv_sems.at[a],
                device_id=(x, y, 1 - c), device_id_type=MESH)
            cp.start()
            copies.append(cp)
        for a in range(n):
            theirs = bufs[a].at[:, _half_cols(bufs[a].shape[1], 1 - c)]
            pltpu.make_async_remote_copy(
                src_ref=theirs, dst_ref=theirs, send_sem=send_sems.at[a], recv_sem=recv_sems.at[a],
                device_id=(x, y, c), device_id_type=MESH).wait_recv()
        for cp in copies:
            cp.wait_send()

    return pl.pallas_call(
        body, name=name, in_specs=[HBM] * n, out_specs=[HBM] * n,
        out_shape=[jax.ShapeDtypeStruct(h.shape, h.dtype) for h in arrays],
        input_output_aliases={a: a for a in range(n)},
        scratch_shapes=[pltpu.SemaphoreType.DMA((n,)), pltpu.SemaphoreType.DMA((n,))],
        compiler_params=pltpu.CompilerParams(collective_id=cid),
    )(*arrays)


def _pack(arrays, rows_multiple=16, width=LANES):
    flat = jnp.concatenate([a.astype(F32).reshape(-1) for a in arrays])
    total = flat.shape[0]
    rows = -(-total // width)
    rows = -(-rows // rows_multiple) * rows_multiple
    return jnp.pad(flat, (0, rows * width - total)).reshape(rows, width)


def _unpack(buf, shapes):
    flat = buf.reshape(-1)
    out, off = [], 0
    for s in shapes:
        n = math.prod(s)
        out.append(flat[off:off + n].reshape(s))
        off += n
    return out


def kernel(x, norm_pre, norm_post, gla_w_in, gla_w_gate2, gla_b_gate, gla_o_gain, gla_w_out, sgu_w_in, sgu_ln_gain, sgu_ln_bias, sgu_w_spatial, sgu_b_spatial, sgu_w_out, loss_target, m_norm_pre, m_norm_post, m_gla_w_in, m_gla_w_gate2, m_gla_b_gate, m_gla_o_gain, m_gla_w_out, m_sgu_w_in, m_sgu_ln_gain, m_sgu_ln_bias, m_sgu_w_spatial, m_sgu_b_spatial, m_sgu_w_out, v_norm_pre, v_norm_post, v_gla_w_in, v_gla_w_gate2, v_gla_b_gate, v_gla_o_gain, v_gla_w_out, v_sgu_w_in, v_sgu_ln_gain, v_sgu_ln_bias, v_sgu_w_spatial, v_sgu_b_spatial, v_sgu_w_out):
    _, t, d = x.shape
    dk = d // 2
    ws = gla_w_in.shape[2]
    wp = -(-ws // LANES) * LANES
    lay = (ws, wp)
    chip =2 * lax.axis_index("x") + lax.axis_index("y")
    core = lax.axis_index("c")
    core_idx = core.astype(jnp.int32).reshape(1)
    others = jnp.arange(N_CHIPS - 1, dtype=jnp.int32)
    others = others + (others >= chip).astype(jnp.int32)
    slots = jnp.concatenate([chip.astype(jnp.int32).reshape(1), others, core_idx])

    x0 = x[0]
    target = loss_target[0]

    wt_in_g, mt_in_g, vt_in_g = gla_w_in[0].T, m_gla_w_in[0].T, v_gla_w_in[0].T

    small_shard = _pack([gla_w_gate2[0], sgu_ln_gain[0], sgu_ln_bias[0]], rows_multiple=8, width=2 * LANES)
    own = [small_shard, jnp.pad(wt_in_g.astype(BF16), ((0, wp - ws), (0, 0)))]
    in_flight, token = _gather_start(own, name="gather_start_a", cid=0, relayed=(1,))

    def with_sibling_and_own(mine, land, name, cid):
        return lax.dynamic_update_slice(_sibling_forward(land, name=name + "_share", cid=cid), mine[None],
                                        (chip, 0, 0))

    h0 = _norm_pre(x0, norm_pre[0:1] + token[0:1, 0:1], name="pre0")
    g_small = with_sibling_and_own(*_gather_wait(in_flight[0], h0, name="w_small_wait"), "w_small", 12)
    mine, land = _gather_wait(in_flight[1], [g_small, wt_in_g, mt_in_g, vt_in_g], name="w_gla_in_wait", ks=NEIGHBOURS)
    relay, token = _relay_start(land, name="w_gla_in_relay", cid=11)
    crossing, token = _forward_start(relay[2], name="w_gla_in_share_near", cid=22, ks=NEIGHBOURS)
    own_later = [(p[0] + token[0, 0]).astype(BF16) for p in (gla_w_out, sgu_w_in, sgu_w_out)]
    in_flight_later, token = _gather_start(own_later, name="gather_start_b", cid=1, after=[token])
    in_flight = in_flight + in_flight_later
    land = _relay_wait((relay[0], relay[1], crossing[2]), token, name="w_gla_in_relay_wait")
    land = _forward_wait((crossing[0], crossing[1], land), token, name="w_gla_in_share_near_wait", ks=NEIGHBOURS)
    land = _sibling_forward(land, name="w_gla_in_share_far", cid=13, ks=(2,))
    wt_g = lax.dynamic_update_slice(land, mine[None], (chip, 0, 0)).reshape(N_CHIPS * wp, d)

    def behind(small, token):
        return small + token[0:1, 0:1]

    def arriving(i, after, name):
        mine, land = _gather_wait(in_flight[i], after, name=name + "_wait")
        crossing, token = _forward_start(land, name=name + "_share", cid=i)
        return (mine, crossing), token

    def arrived(pending, after, name):
        mine, crossing = pending
        return lax.dynamic_update_slice(_forward_wait(crossing, after, name=name + "_share_wait"), mine[None],
                                        (chip, 0, 0))

    shard_shapes = [gla_w_gate2.shape[1:], sgu_ln_gain.shape[1:], sgu_ln_bias.shape[1:]]
    per_chip = [_unpack(g_small[j], shard_shapes) for j in range(N_CHIPS)]
    w2_full = jnp.concatenate([p[0] for p in per_chip], axis=1)
    ln_gain = jnp.concatenate([p[1] for p in per_chip], axis=0)[None, :]
    ln_bias = jnp.concatenate([p[2] for p in per_chip], axis=0)[None, :]
    w2p = jnp.pad(w2_full, ((0, LANES - GLA_GATE_RANK), (0, 0)))

    pos_chunk = jnp.arange(SGU_BLOCK) // CHUNK
    mask = pos_chunk[:, None] >= pos_chunk[None, :]
    ws_masked = jnp.where(mask[None], sgu_w_spatial[0], 0.0)
    ws_masked_t = ws_masked.transpose(0, 2, 1)
    bs_t = sgu_b_spatial[0].T

    proj0 = _matmul(h0, wt_g, mode="nt", out_dtype=F32, name="gla_in", tn=wp)
    pending, tok = arriving(2, proj0, "w_gla_out")
    o0, a0, s_before, s_final = _gla_fwd(proj0, w2p, behind(gla_b_gate, tok), gla_o_gain, lay, name="gla_scan")
    w_out_g = arrived(pending, a0, "w_gla_out").reshape(d, d)
    pending, tok = arriving(3, w_out_g, "w_sgu_in")
    y0 = _matmul(a0, w_out_g, mode="nn", out_dtype=F32, name="gla_out", after=tok)
    x1, h1 = _post_then_pre(x0, y0, norm_post[0:1], norm_pre[1:2], name="post0_pre1")
    g_wi_s = arrived(pending, h1, "w_sgu_in")
    pending, tok = arriving(4, g_wi_s, "w_sgu_out")
    proj1 = _matmul(h1, g_wi_s, mode="nn", out_dtype=F32, name="sgu_in", b_shards=True, after=tok)
    a1 = _sgu_fwd(proj1, ln_gain, ln_bias, ws_masked, bs_t, name="sgu_gate")
    w_out_s = arrived(pending, a1, "w_sgu_out").reshape(d, d)
    acts, tok = _to_sibling_start([(a1, 0), (a0, 0), (h1, 0), (h0, 1)], name="acts_to_sibling", cid=5)
    a1, a0, h1, h0 = [f[4] for f in acts]
    y1 = _matmul(a1, w_out_s, mode="nn", out_dtype=F32, name="sgu_out", after=tok)
    loss_part, dx2, dy1, d_post1 = _loss_head(x1, y1, norm_post[1:2], target, name="loss_head")

    def pair_gradient(a_sent, b_sent, after, shards_on, name, cid):
        a_me, a_sib = _from_sibling(a_sent, after, name=name + "_a_wait")
        b_me, b_sib = _from_sibling(b_sent, [a_sib] + list(after), name=name + "_b_wait")
        pair = _matmul_dw_pair(a_me, a_sib, b_me, b_sib, core_idx, shards_on=shards_on,
                               name=name + "_pair")
        return _scatter_start(pair, name=name + "_start", cid=cid)

    def reduced(flight, after, name):
        pair, landed = _scatter_wait(flight, after, name=name + "_wait")
        return _chip_sum(pair, landed, slots, name=name + "_sum")

    (dy1_sent,), tok = _to_sibling_start([(dy1, 1)], name="dy1_to_sibling", cid=6)
    dy1 = dy1_sent[4]
    da1 = _matmul(dy1, w_out_s, mode="nt", out_dtype=F32, name="d_sgu_act", after=tok)
    fl_wo_s, tok = pair_gradient(acts[0], dy1_sent, [da1], "rows", "g_sgu_out", 15)
    dproj1, d_ws, d_bs_t, d_lg, d_lb = _sgu_bwd(da1, proj1, ln_gain, behind(ln_bias, tok), ws_masked, ws_masked_t,
                                                bs_t, name="sgu_gate_bwd")
    (dp1_sent,), tok = _to_sibling_start([(dproj1, N_CHIPS)], name="dproj1_to_sibling", cid=7)
    dproj1 = dp1_sent[4]
    dh1 = _matmul_nt_shards(dproj1, g_wi_s, out_dtype=F32, name="d_sgu_h", after=tok)
    fl_wi_s, tok = pair_gradient(acts[2], dp1_sent, [dh1], "cols", "g_sgu_in", 16)
    dx1, dy0, d_pre1, d_post0 = _mid_bwd(dx2, dh1, x1, behind(norm_pre[1:2], tok), y0, norm_post[0:1],
                                         name="pre1_post0_bwd")
    (dy0_sent,), tok = _to_sibling_start([(dy0, 1)], name="dy0_to_sibling", cid=8)
    dy0 = dy0_sent[4]
    da0 = _matmul(dy0, w_out_g, mode="nt", out_dtype=F32, name="d_gla_act", after=tok)
    fl_wo_g, tok = pair_gradient(acts[1], dy0_sent, [da0], "rows", "g_gla_out", 17)
    dproj0, d_og, d_bg, d_w2p = _gla_bwd(da0, o0, proj0, w2p, behind(gla_b_gate, tok), gla_o_gain, s_before, s_final,
                                         lay, name="gla_scan_bwd")
    early_shapes = [norm_post.shape, gla_b_gate.shape, gla_o_gain.shape, sgu_w_spatial.shape, sgu_b_spatial.shape,
                    (1, GLA_GATE_RANK, dk), (1, d), (1, d), (1, LANES)]
    early_part = _pack([jnp.concatenate([d_post0, d_post1], axis=0), d_bg, d_og, jnp.where(mask[None], d_ws, 0.0)[None],
                        d_bs_t.T[None], d_w2p[:GLA_GATE_RANK][None], d_lg, d_lb, loss_part])
    early_flight, tok = _dev_gather_start(early_part, name="small_early_start", cid=20)
    (dp0_sent,), tok_sent = _to_sibling_start([(dproj0, 0)], name="dproj0_to_sibling", cid=9)
    dproj0 = dp0_sent[4]
    dh0 = _matmul(dproj0, wt_g, mode="nn", out_dtype=F32, name="d_gla_h", after=tok_sent)
    a_me, a_sib = _from_sibling(dp0_sent, [dh0, tok], name="g_gla_in_a_wait")
    b_me, b_sib = _from_sibling(acts[3], [a_sib, dh0], name="g_gla_in_b_wait")
    fl_wi_g, tok_scatter = [], None
    for p in range(2):
        pair = _matmul_dw_pair(a_me, a_sib, b_me, b_sib, core_idx, shards_on="rows", part=(p, 2),
                               name=f"g_gla_in_pair{p}", after=tok_scatter)
        flight, tok_scatter = _scatter_start(pair, name=f"g_gla_in_start{p}", cid=18 + p)
        fl_wi_g.append(flight)
    r_wo_s = reduced(fl_wo_s, tok_scatter, "g_sgu_out")
    r_wi_s = reduced(fl_wi_s, r_wo_s, "g_sgu_in")
    r_wo_g = reduced(fl_wo_g, r_wi_s, "g_gla_out")
    sharing, tok = _share_start([r_wo_s, r_wi_s, r_wo_g], name="grads_share_a", cid=10)
    grad_x, d_pre0 = _first_bwd(dx1, dh0, x0, behind(norm_pre[0:1], tok), name="pre0_bwd")

    late_part = _pack([jnp.concatenate([d_pre0, d_pre1], axis=0)])
    late_flight, tok = _dev_gather_start(late_part, name="small_late_start", cid=21)

    def big_update(w, g, m, v, name, after=None):
        return [u[None] for u in _adamw(w[0], g, m[0], v[0], name=name, after=after)]

    g_wo_sgu, g_wi_sgu, g_wo_gla = _share_wait(sharing, [grad_x, tok], name="grads_share_a_wait")
    u_wi_sgu = big_update(sgu_w_in, g_wi_sgu, m_sgu_w_in, v_sgu_w_in, "adamw_sgu_w_in")
    u_wo_gla = big_update(gla_w_out, g_wo_gla, m_gla_w_out, v_gla_w_out, "adamw_gla_w_out", after=u_wi_sgu[1])

    r_wi_g, behind_this = None, u_wo_gla[1]
    for p, flight in enumerate(fl_wi_g):
        pair, landed = _scatter_wait(flight, behind_this, name=f"g_gla_in_wait{p}")
        r_wi_g = behind_this = _chip_sum(pair, landed, slots, part=(p, 2), into=r_wi_g, name=f"g_gla_in_sum{p}")
    gt_wi_gla, = _sibling_share_halves([r_wi_g], name="grads_share_b", cid=14)
    u_wi_gla_t = _adamw(wt_in_g, gt_wi_gla, mt_in_g, vt_in_g, name="adamw_gla_w_in")
    u_wi_gla = [u.T[None] for u in u_wi_gla_t]
    u_wo_sgu = big_update(sgu_w_out, g_wo_sgu, m_sgu_w_out, v_sgu_w_out, "adamw_sgu_w_out", after=u_wi_gla_t[1])

    def summed_over_devices(part, flight, after, shapes, name):
        land = _dev_gather_wait(flight, after, name=name + "_wait")
        every = lax.dynamic_update_slice(land, part[None], (2 * chip + core, 0, 0))
        return _unpack(_stack_sum(every, name=name + "_sum"), shapes)

    (g_post, g_bg, g_og, g_wsp, g_bsp, g_w2_full, g_lg_full, g_lb_full, loss_vec) = summed_over_devices(
        early_part, early_flight, u_wo_sgu[1], early_shapes, "small_early")
    g_pre, = summed_over_devices(late_part, late_flight, loss_vec, [norm_pre.shape], "small_late")
    loss = loss_vec[0, 0]
    g_w2 = lax.dynamic_slice_in_dim(g_w2_full, chip * (dk // N_CHIPS), dk // N_CHIPS, axis=2)
    g_lg = lax.dynamic_slice_in_dim(g_lg_full, chip * (d // N_CHIPS), d // N_CHIPS, axis=1)
    g_lb = lax.dynamic_slice_in_dim(g_lb_full, chip * (d // N_CHIPS), d // N_CHIPS, axis=1)

    small_w = [norm_pre, norm_post, gla_b_gate, gla_o_gain, sgu_w_spatial, sgu_b_spatial, gla_w_gate2, sgu_ln_gain,
               sgu_ln_bias]
    small_g = [g_pre, g_post, g_bg, g_og, g_wsp, g_bsp, g_w2, g_lg, g_lb]
    small_m = [m_norm_pre, m_norm_post, m_gla_b_gate, m_gla_o_gain, m_sgu_w_spatial, m_sgu_b_spatial, m_gla_w_gate2,
               m_sgu_ln_gain, m_sgu_ln_bias]
    small_v = [v_norm_pre, v_norm_post, v_gla_b_gate, v_gla_o_gain, v_sgu_w_spatial, v_sgu_b_spatial, v_gla_w_gate2,
               v_sgu_ln_gain, v_sgu_ln_bias]
    own_shapes = [w.shape for w in small_w]
    _, s_dl, s_m, s_v = _adamw(_pack(small_w), _pack(small_g), _pack(small_m), _pack(small_v), name="adamw_small")
    dl_s, m_s, v_s = _unpack(s_dl, own_shapes), _unpack(s_m, own_shapes), _unpack(s_v, own_shapes)

    def ordered(small, kind):
        pre, post, bg, og, wsp, bsp, w2, lg, lb = small
        return [pre, post, u_wi_gla[kind], w2, bg, og, u_wo_gla[kind], u_wi_sgu[kind], lg, lb, wsp, bsp, u_wo_sgu[kind]]

    return (loss, grad_x[None], *ordered(small_g, 0), *ordered(dl_s, 1), *ordered(m_s, 2), *ordered(v_s, 3))
```

```python
import math

import jax
import jax.numpy as jnp
from jax import lax
from jax.experimental import pallas as pl
from jax.experimental.pallas import tpu as pltpu

F32 = jnp.float32
BF16 = jnp.bfloat16
MESH = pl.DeviceIdType.MESH

EPS = 1e-6
CHUNK = 64
GLA_HEADS = 4
GLA_GATE_RANK = 16
GLA_TAU = 16.0
SGU_BLOCK = 128
SGU_GROUPS = 8
N_CHIPS = 4
N_DEV = 8
LANES = 128

ADAM_LR = 0.001
ADAM_B1 = 0.9
ADAM_B2 = 0.999
ADAM_EPS = 1e-08
ADAM_WD = 0.01
ADAM_STEP = 10

VMEM_LIMIT = 56 * 1024 * 1024


def _cparams(sem=None):
    return pltpu.CompilerParams(dimension_semantics=sem, vmem_limit_bytes=VMEM_LIMIT)


def _pick(n, cap, unit=LANES):
    best = None
    for t in range(unit, min(n, cap) + 1, unit):
        if n % t == 0:
            best = t
    assert best is not None, (n, cap, unit)
    return best


def _dot(a, b, dims):
    return lax.dot_general(a, b, (dims, ((), ())), preferred_element_type=F32)


def _dot_nn(a, b):
    return _dot(a, b, ((1,), (0,)))


def _dot_nt(a, b):
    return _dot(a, b, ((1,), (1,)))


def _dot_tn(a, b):
    return _dot(a, b, ((0,), (0,)))


def _matmul(a, b, *, mode, out_dtype, name, tm=1024, tn=512, b_shards=False, after=None):
    M, K = a.shape
    if b_shards:
        ns, Kb, bc = b.shape
        N, tn = ns * bc, _pick(bc, tn)
        per = bc // tn
        b_spec = pl.BlockSpec((None, K, tn), lambda i, j: (j // per, 0, j % per))
    elif mode == "nt":
        N, Kb = b.shape
        tn = _pick(N, tn)
        b_spec = pl.BlockSpec((tn, K), lambda i, j: (j, 0))
    else:
        Kb, N = b.shape
        tn = _pick(N, tn)
        b_spec = pl.BlockSpec((K, tn), lambda i, j: (0, j))
    assert K == Kb and a.dtype == b.dtype == BF16, (a.shape, b.shape, mode)
    tm = _pick(M, tm)
    dims = ((1,), (1,)) if mode == "nt" else ((1,), (0,))
    extra_specs, extra_args = ([], []) if after is None else ([pl.BlockSpec(memory_space=pl.ANY)], [after])

    def body(a_ref, b_ref, *rest):
        rest[-1][...] = _dot(a_ref[...], b_ref[...], dims).astype(out_dtype)

    return pl.pallas_call(
        body, name=name, grid=(M // tm, N // tn),
        in_specs=[pl.BlockSpec((tm, K), lambda i, j: (i, 0)), b_spec] + extra_specs,
        out_specs=pl.BlockSpec((tm, tn), lambda i, j: (i, j)), out_shape=jax.ShapeDtypeStruct((M, N), out_dtype),
        compiler_params=_cparams(("parallel", "parallel")),
    )(a, b, *extra_args)


def _matmul_nt_shards(a, b, *, out_dtype, name, tm=1024, tn=512, after=None):
    M, K = a.shape
    ns, N, kc = b.shape
    assert K == ns * kc
    tm, tn = _pick(M, tm), _pick(N, tn)

    def body(a_ref, *rest):
        b_refs, o_ref = rest[:ns], rest[ns + (after is not None)]
        acc = _dot_nt(a_ref[:, 0:kc], b_refs[0][...])
        for j in range(1, ns):
            acc += _dot_nt(a_ref[:, j * kc:(j + 1) * kc], b_refs[j][...])
        o_ref[...] = acc.astype(out_dtype)

    def shard(j):
        return pl.BlockSpec((None, tn, kc), lambda i, n: (j, n, 0))

    extra_specs, extra_args = ([], []) if after is None else ([pl.BlockSpec(memory_space=pl.ANY)], [after])
    return pl.pallas_call(
        body, name=name, grid=(M // tm, N // tn),
        in_specs=[pl.BlockSpec((tm, K), lambda i, n: (i, 0))] + [shard(j) for j in range(ns)] + extra_specs,
        out_specs=pl.BlockSpec((tm, tn), lambda i, n: (i, n)), out_shape=jax.ShapeDtypeStruct((M, N), out_dtype),
        compiler_params=_cparams(("parallel", "parallel")),
    )(a, *([b] * ns), *extra_args)


def _rstd(x):
    return lax.rsqrt(jnp.mean(x * x, axis=-1, keepdims=True) + EPS)


def _row_spec(tr, d):
    return pl.BlockSpec((tr, d), lambda i: (i, 0))


def _vec_spec(d):
    return pl.BlockSpec((1, d), lambda i: (0, 0))


def _acc_rows(ref, i, val, cols=slice(None)):
    @pl.when(i == 0)
    def _():
        ref[:, cols] = val

    @pl.when(i > 0)
    def _():
        ref[:, cols] += val


def _norm_pre(x, gain, *, name, tr=256):
    t, d = x.shape
    tr = _pick(t, tr, 8)

    def body(x_ref, g_ref, h_ref):
        xv = x_ref[...]
        h_ref[...] = (xv * _rstd(xv) * g_ref[...]).astype(BF16)

    return pl.pallas_call(
        body, name=name, grid=(t // tr,), in_specs=[_row_spec(tr, d), _vec_spec(d)], out_specs=_row_spec(tr, d),
        out_shape=jax.ShapeDtypeStruct((t, d), BF16), compiler_params=_cparams(("parallel",)),
    )(x, gain)


def _post_then_pre(x, y, post_gain, pre_gain, *, name, tr=256):
    t, d = x.shape
    tr = _pick(t, tr, 8)

    def body(x_ref, y_ref, pg_ref, ng_ref, xn_ref, h_ref):
        yv = y_ref[...]
        xn = x_ref[...] + yv * _rstd(yv) * pg_ref[...]
        xn_ref[...] = xn
        h_ref[...] = (xn * _rstd(xn) * ng_ref[...]).astype(BF16)

    return pl.pallas_call(
        body, name=name, grid=(t // tr,),
        in_specs=[_row_spec(tr, d), _row_spec(tr, d), _vec_spec(d), _vec_spec(d)],
        out_specs=[_row_spec(tr, d), _row_spec(tr, d)],
        out_shape=[jax.ShapeDtypeStruct((t, d), F32), jax.ShapeDtypeStruct((t, d), BF16)],
        compiler_params=_cparams(("parallel",)),
    )(x, y, post_gain, pre_gain)


def _norm_bwd(dy, n, r, gain):
    dn = dy * gain
    return r * (dn - n * jnp.mean(dn * n, axis=-1, keepdims=True))


def _loss_head(x, y, post_gain, target, *, name, tr=256):
    t, d = x.shape
    tr = _pick(t, tr, 8)

    def body(x_ref, y_ref, pg_ref, t_ref, loss_ref, dx_ref, dy_ref, dpg_ref):
        i = pl.program_id(0)
        yv = y_ref[...]
        r = _rstd(yv)
        n = yv * r
        err = x_ref[...] + n * pg_ref[...] - t_ref[...]
        dx = err * (1.0 / d)
        dx_ref[...] = dx
        part = 0.5 * jnp.sum(jnp.mean(err * err, axis=-1, keepdims=True), axis=0, keepdims=True)
        _acc_rows(loss_ref, i, jnp.broadcast_to(part, (1, LANES)))
        _acc_rows(dpg_ref, i, jnp.sum(dx * n, axis=0, keepdims=True))
        dy_ref[...] = _norm_bwd(dx, n, r, pg_ref[...]).astype(BF16)

    return pl.pallas_call(
        body, name=name, grid=(t // tr,),
        in_specs=[_row_spec(tr, d), _row_spec(tr, d), _vec_spec(d), _row_spec(tr, d)],
        out_specs=[_vec_spec(LANES), _row_spec(tr, d), _row_spec(tr, d), _vec_spec(d)],
        out_shape=[jax.ShapeDtypeStruct((1, LANES), F32), jax.ShapeDtypeStruct((t, d), F32),
                   jax.ShapeDtypeStruct((t, d), BF16), jax.ShapeDtypeStruct((1, d), F32)],
        compiler_params=_cparams(("arbitrary",)),
    )(x, y, post_gain, target)


def _mid_bwd(dx_out, dh, x, pre_gain, y_prev, post_gain_prev, *, name, tr=256):
    t, d = x.shape
    tr = _pick(t, tr, 8)

    def body(dxo_ref, dh_ref, x_ref, ng_ref, y_ref, pg_ref, dx_ref, dy_ref, dng_ref, dpg_ref):
        i = pl.program_id(0)
        xv = x_ref[...]
        r = _rstd(xv)
        xh = xv * r
        dhv = dh_ref[...]
        _acc_rows(dng_ref, i, jnp.sum(dhv * xh, axis=0, keepdims=True))
        dx = dxo_ref[...] + _norm_bwd(dhv, xh, r, ng_ref[...])
        dx_ref[...] = dx
        yv = y_ref[...]
        ry = _rstd(yv)
        n = yv * ry
        _acc_rows(dpg_ref, i, jnp.sum(dx * n, axis=0, keepdims=True))
        dy_ref[...] = _norm_bwd(dx, n, ry, pg_ref[...]).astype(BF16)

    return pl.pallas_call(
        body, name=name, grid=(t // tr,),
        in_specs=[_row_spec(tr, d), _row_spec(tr, d), _row_spec(tr, d), _vec_spec(d), _row_spec(tr, d), _vec_spec(d)],
        out_specs=[_row_spec(tr, d), _row_spec(tr, d), _vec_spec(d), _vec_spec(d)],
        out_shape=[jax.ShapeDtypeStruct((t, d), F32), jax.ShapeDtypeStruct((t, d), BF16),
                   jax.ShapeDtypeStruct((1, d), F32), jax.ShapeDtypeStruct((1, d), F32)],
        compiler_params=_cparams(("arbitrary",)),
    )(dx_out, dh, x, pre_gain, y_prev, post_gain_prev)


def _first_bwd(dx_out, dh, x, pre_gain, *, name, tr=256):
    t, d = x.shape
    tr = _pick(t, tr, 8)

    def body(dxo_ref, dh_ref, x_ref, ng_ref, dx_ref, dng_ref):
        i = pl.program_id(0)
        xv = x_ref[...]
        r = _rstd(xv)
        xh = xv * r
        dhv = dh_ref[...]
        _acc_rows(dng_ref, i, jnp.sum(dhv * xh, axis=0, keepdims=True))
        dx_ref[...] = dxo_ref[...] + _norm_bwd(dhv, xh, r, ng_ref[...])

    return pl.pallas_call(
        body, name=name, grid=(t // tr,),
        in_specs=[_row_spec(tr, d), _row_spec(tr, d), _row_spec(tr, d), _vec_spec(d)],
        out_specs=[_row_spec(tr, d), _vec_spec(d)],
        out_shape=[jax.ShapeDtypeStruct((t, d), F32), jax.ShapeDtypeStruct((1, d), F32)],
        compiler_params=_cparams(("arbitrary",)),
    )(dx_out, dh, x, pre_gain)


def _sigmoid(x):
    return 1.0 / (1.0 + jnp.exp(-x))


def _log_sigmoid(x):
    return jnp.minimum(x, 0.0) - jnp.log(1.0 + jnp.exp(-jnp.abs(x)))


_GELU_C = math.sqrt(2.0 / math.pi)


_GELU_A = 0.044715


def _gelu_parts(x, with_grad=True):
    x2 = x * x
    h = 0.5 * jnp.tanh(x * (_GELU_C + (_GELU_C * _GELU_A) * x2)) + 0.5
    val = x * h
    if not with_grad:
        return val, None
    return val, h * (1.0 + (1.0 - h) * (x * (2.0 * _GELU_C + (6.0 * _GELU_C * _GELU_A) * x2)))


def _split3(x):
    hi = x.astype(BF16)
    r1 = x - hi.astype(F32)
    mid = r1.astype(BF16)
    lo = (r1 - mid.astype(F32)).astype(BF16)
    return hi, mid, lo


def _tri_matmul(tri_bf16, x):
    hi, mid, lo = _split3(x)
    return _dot_nn(tri_bf16, hi) + _dot_nn(tri_bf16, mid) + _dot_nn(tri_bf16, lo)


def _gla_dims(d):
    dk, dv = d // 2, d
    return dk, dv, dk // GLA_HEADS, dv // GLA_HEADS


def _col_pieces(a, b, lay):
    ws, wp = lay
    out = []
    while a < b:
        j = a // ws
        end = min(b, (j + 1) * ws)
        out.append((j * wp + a - j * ws, end - a))
        a = end
    return out


def _load_cols(ref, a, b, lay):
    parts = [ref[:, s:s + n] for s, n in _col_pieces(a, b, lay)]
    return parts[0] if len(parts) == 1 else jnp.concatenate(parts, axis=1)


def _store_cols(ref, a, val, lay):
    off = 0
    for s, n in _col_pieces(a, a + val.shape[1], lay):
        ref[:, s:s + n] = val[:, off:off + n]
        off += n


def _gate_window(c_r, lay):
    (start, _), = _col_pieces(c_r, c_r + GLA_GATE_RANK, lay)
    assert (start % lay[1]) + LANES <= lay[1]
    return slice(start, start + LANES)


def _gla_gates(glr, k, w2_ref, b_ref):
    z = _dot_nn(glr.astype(BF16), w2_ref[...].astype(BF16)) + b_ref[...]
    la = _log_sigmoid(z) * (1.0 / GLA_TAU)
    row = lax.broadcasted_iota(jnp.int32, (CHUNK, CHUNK), 0)
    col = lax.broadcasted_iota(jnp.int32, (CHUNK, CHUNK), 1)
    incl = (row >= col).astype(BF16)
    bcum = _tri_matmul(incl, la)
    b_end = bcum[CHUNK - 1:CHUNK, :]
    e_rest = jnp.exp(b_end - bcum)
    return z, e_rest, k * e_rest, jnp.exp(b_end)


def _gla_fwd(proj, w2p, b_gate, o_gain, lay, *, name):
    t, wcols = proj.shape
    d = o_gain.shape[1]
    dk, dv, dkh, dvh = _gla_dims(d)
    nc = t // CHUNK
    c_k, c_v, c_g, c_r = dk, 2 * dk, 2 * dk + dv, 2 * dk + 2 * dv
    scale = dkh ** -0.5

    def body(p_ref, w2_ref, b_ref, og_ref, o_ref, a_ref, sb_ref, sfin_ref, s_ref):
        i = pl.program_id(0)

        @pl.when(i == 0)
        def _():
            s_ref[...] = jnp.zeros_like(s_ref)

        q = _load_cols(p_ref, 0, dk, lay) * scale
        k = _load_cols(p_ref, c_k, c_k + dk, lay)
        glr = p_ref[:, _gate_window(c_r, lay)]
        _, _, kdec, decay = _gla_gates(glr, k, w2_ref, b_ref)
        for h in range(GLA_HEADS):
            ks = slice(h * dkh, (h + 1) * dkh)
            vs = slice(h * dvh, (h + 1) * dvh)
            v_h = _load_cols(p_ref, c_v + h * dvh, c_v + (h + 1) * dvh, lay)
            g_h = _load_cols(p_ref, c_g + h * dvh, c_g + (h + 1) * dvh, lay)
            s_old = s_ref[h]
            sb_ref[0, h] = s_old
            s_new = s_old * decay[:, ks] + _dot_tn(v_h.astype(BF16), kdec[:, ks].astype(BF16))
            s_ref[h] = s_new
            o_h = _dot_nt(q[:, ks].astype(BF16), s_new.astype(BF16))
            o_ref[:, vs] = o_h
            on = o_h * _rstd(o_h)
            a_ref[:, vs] = (on * og_ref[:, vs] * (g_h * _sigmoid(g_h))).astype(BF16)

        @pl.when(i == nc - 1)
        def _():
            sfin_ref[...] = s_ref[...]

    full = lambda *shape: pl.BlockSpec(shape, lambda i: (0,) * len(shape))
    return pl.pallas_call(
        body, name=name, grid=(nc,),
        in_specs=[pl.BlockSpec((CHUNK, wcols), lambda i: (i, 0)), full(LANES, dk), full(1, dk), full(1, dv)],
        out_specs=[pl.BlockSpec((CHUNK, dv), lambda i: (i, 0)), pl.BlockSpec((CHUNK, dv), lambda i: (i, 0)),
                   pl.BlockSpec((1, GLA_HEADS, dvh, dkh), lambda i: (i, 0, 0, 0)), full(GLA_HEADS, dvh, dkh)],
        out_shape=[jax.ShapeDtypeStruct((t, dv), F32), jax.ShapeDtypeStruct((t, dv), BF16),
                   jax.ShapeDtypeStruct((nc, GLA_HEADS, dvh, dkh), F32),
                   jax.ShapeDtypeStruct((GLA_HEADS, dvh, dkh), F32)],
        scratch_shapes=[pltpu.VMEM((GLA_HEADS, dvh, dkh), F32)],
        compiler_params=_cparams(("arbitrary",)),
    )(proj, w2p, b_gate, o_gain)


def _gla_bwd(da, o, proj, w2p, b_gate, o_gain, s_before, s_final, lay, *, name):
    t, wcols = proj.shape
    d = o_gain.shape[1]
    dk, dv, dkh, dvh = _gla_dims(d)
    nc = t // CHUNK
    c_k, c_v, c_g, c_r = dk, 2 * dk, 2 * dk + dv, 2 * dk + 2 * dv
    scale = dkh ** -0.5

    def body(da_ref, o_ref, p_ref, w2_ref, b_ref, og_ref, sb_ref, sfin_ref,
             dp_ref, dog_ref, db_ref, dw2_ref, s_ref, gc_ref, dkd_ref):
        i = pl.program_id(0)

        @pl.when(i == 0)
        def _():
            s_ref[...] = sfin_ref[...]
            gc_ref[...] = jnp.zeros_like(gc_ref)

        ws, wp = lay
        for j in range(N_CHIPS):
            dp_ref[:, j * wp + ws:(j + 1) * wp] = jnp.zeros((CHUNK, wp - ws), BF16)
        q = _load_cols(p_ref, 0, dk, lay) * scale
        k = _load_cols(p_ref, c_k, c_k + dk, lay)
        glr = p_ref[:, _gate_window(c_r, lay)]
        z, e_rest, kdec, decay = _gla_gates(glr, k, w2_ref, b_ref)
        ddecay = []
        for h in range(GLA_HEADS):
            ks = slice(h * dkh, (h + 1) * dkh)
            vs = slice(h * dvh, (h + 1) * dvh)
            v_h = _load_cols(p_ref, c_v + h * dvh, c_v + (h + 1) * dvh, lay)
            g_h = _load_cols(p_ref, c_g + h * dvh, c_g + (h + 1) * dvh, lay)
            da_h = da_ref[:, vs]
            o_h = o_ref[:, vs]
            og_h = og_ref[:, vs]
            r = _rstd(o_h)
            on = o_h * r
            sg = _sigmoid(g_h)
            silu = g_h * sg
            _acc_rows(dog_ref, i, jnp.sum(da_h * silu * on, axis=0, keepdims=True), vs)
            _store_cols(dp_ref, c_g + h * dvh, (da_h * (on * og_h) * (sg * (1.0 + g_h * (1.0 - sg)))).astype(BF16),
                        lay)
            don = da_h * silu * og_h
            do_h = (r * (don - on * jnp.mean(don * on, axis=-1, keepdims=True))).astype(BF16)
            s_cur = s_ref[h]
            _store_cols(dp_ref, h * dkh, (_dot_nn(do_h, s_cur.astype(BF16)) * scale).astype(BF16), lay)
            g_tot = gc_ref[h] + _dot_tn(do_h, q[:, ks].astype(BF16))
            g_bf = g_tot.astype(BF16)
            dkd_ref[:, ks] = _dot_nn(v_h.astype(BF16), g_bf)
            _store_cols(dp_ref, c_v + h * dvh, _dot_nt(kdec[:, ks].astype(BF16), g_bf).astype(BF16), lay)
            s_prev = sb_ref[0, h]
            ddecay.append(jnp.sum(g_tot * s_prev, axis=0, keepdims=True))
            gc_ref[h] = g_tot * decay[:, ks]
            s_ref[h] = s_prev
        dkdec = dkd_ref[...]
        _store_cols(dp_ref, c_k, (dkdec * e_rest).astype(BF16), lay)
        d_e = dkdec * kdec
        row = lax.broadcasted_iota(jnp.int32, (CHUNK, CHUNK), 0)
        col = lax.broadcasted_iota(jnp.int32, (CHUNK, CHUNK), 1)
        excl = (row > col).astype(BF16)
        dla = jnp.concatenate(ddecay, axis=1) * decay + _tri_matmul(excl, d_e)
        dz = dla * (1.0 / GLA_TAU) * (1.0 - _sigmoid(z))
        _acc_rows(db_ref, i, jnp.sum(dz, axis=0, keepdims=True))
        dz_bf = dz.astype(BF16)
        dw2 = _dot_tn(glr.astype(BF16), dz_bf)

        @pl.when(i == 0)
        def _():
            dw2_ref[...] = dw2

        @pl.when(i > 0)
        def _():
            dw2_ref[...] += dw2

        dp_ref[:, _gate_window(c_r, lay)] = _dot_nt(dz_bf, w2_ref[...].astype(BF16)).astype(BF16)

    rev = lambda i: (nc - 1 - i, 0)
    full = lambda *shape: pl.BlockSpec(shape, lambda i: (0,) * len(shape))
    return pl.pallas_call(
        body, name=name, grid=(nc,),
        in_specs=[pl.BlockSpec((CHUNK, dv), rev), pl.BlockSpec((CHUNK, dv), rev), pl.BlockSpec((CHUNK, wcols), rev),
                  full(LANES, dk), full(1, dk), full(1, dv),
                  pl.BlockSpec((1, GLA_HEADS, dvh, dkh), lambda i: (nc - 1 - i, 0, 0, 0)), full(GLA_HEADS, dvh, dkh)],
        out_specs=[pl.BlockSpec((CHUNK, wcols), rev), full(1, dv), full(1, dk), full(LANES, dk)],
        out_shape=[jax.ShapeDtypeStruct((t, wcols), BF16), jax.ShapeDtypeStruct((1, dv), F32),
                   jax.ShapeDtypeStruct((1, dk), F32), jax.ShapeDtypeStruct((LANES, dk), F32)],
        scratch_shapes=[pltpu.VMEM((GLA_HEADS, dvh, dkh), F32), pltpu.VMEM((GLA_HEADS, dvh, dkh), F32),
                        pltpu.VMEM((CHUNK, dk), F32)],
        compiler_params=_cparams(("arbitrary",)),
    )(da, o, proj, w2p, b_gate, o_gain, s_before, s_final)


def _sgu_mid(p_ref, lg_ref, lb_ref, ws_ref, bst_ref, w, with_grad=True):
    gd = w // SGU_GROUPS
    u_act, du_fac = _gelu_parts(p_ref[:, 0:w], with_grad)
    vf, dv_fac = _gelu_parts(p_ref[:, w:2 * w], with_grad)
    mu = jnp.mean(vf, axis=-1, keepdims=True)
    cen = vf - mu
    rstd = lax.rsqrt(jnp.mean(cen * cen, axis=-1, keepdims=True) + EPS)
    xh = cen * rstd
    vn = (xh * lg_ref[...] + lb_ref[...]).astype(BF16)
    vs = [_dot_nn(ws_ref[g].astype(BF16), vn[:, g * gd:(g + 1) * gd]) + bst_ref[:, g:g + 1]
          for g in range(SGU_GROUPS)]
    return u_act, du_fac, dv_fac, rstd, xh, vn, vs


def _sgu_fwd(proj, ln_gain, ln_bias, ws_masked, bs_t, *, name):
    t, w3 = proj.shape
    w = w3 // 3
    gd = w // SGU_GROUPS
    nb = t // SGU_BLOCK

    def body(p_ref, lg_ref, lb_ref, ws_ref, bst_ref, a_ref):
        u_act, _, _, _, _, _, vs = _sgu_mid(p_ref, lg_ref, lb_ref, ws_ref, bst_ref, w, with_grad=False)
        for g in range(SGU_GROUPS):
            cs = slice(g * gd, (g + 1) * gd)
            gate = p_ref[:, 2 * w + g * gd:2 * w + (g + 1) * gd]
            a_ref[:, cs] = (u_act[:, cs] * vs[g] * (gate * _sigmoid(gate))).astype(BF16)

    full = lambda *shape: pl.BlockSpec(shape, lambda i: (0,) * len(shape))
    return pl.pallas_call(
        body, name=name, grid=(nb,),
        in_specs=[pl.BlockSpec((SGU_BLOCK, w3), lambda i: (i, 0)), full(1, w), full(1, w),
                  full(SGU_GROUPS, SGU_BLOCK, SGU_BLOCK), full(SGU_BLOCK, SGU_GROUPS)],
        out_specs=pl.BlockSpec((SGU_BLOCK, w), lambda i: (i, 0)),
        out_shape=jax.ShapeDtypeStruct((t, w), BF16),
        compiler_params=_cparams(("parallel",)),
    )(proj, ln_gain, ln_bias, ws_masked, bs_t)


def _sgu_bwd(da, proj, ln_gain, ln_bias, ws_masked, ws_masked_t, bs_t, *, name):
    t, w3 = proj.shape
    w = w3 // 3
    gd = w // SGU_GROUPS
    nb = t // SGU_BLOCK

    def body(da_ref, p_ref, lg_ref, lb_ref, ws_ref, wst_ref, bst_ref, dp_ref, dws_ref, dbst_ref, dlg_ref, dlb_ref,
             dvn_ref):
        i = pl.program_id(0)
        u_act, du_fac, dv_fac, rstd, xh, vn, vs = _sgu_mid(p_ref, lg_ref, lb_ref, ws_ref, bst_ref, w)
        for g in range(SGU_GROUPS):
            cs = slice(g * gd, (g + 1) * gd)
            gate = p_ref[:, 2 * w + g * gd:2 * w + (g + 1) * gd]
            sg = _sigmoid(gate)
            silu = gate * sg
            da_g = da_ref[:, cs]
            ua_g = u_act[:, cs]
            dp_ref[:, cs] = (da_g * vs[g] * silu * du_fac[:, cs]).astype(BF16)
            dp_ref[:, 2 * w + g * gd:2 * w + (g + 1) * gd] = (
                da_g * ua_g * vs[g] * (sg * (1.0 + gate * (1.0 - sg)))).astype(BF16)
            dvs = da_g * ua_g * silu
            dvs_bf = dvs.astype(BF16)
            dvn_ref[:, cs] = _dot_nn(wst_ref[g].astype(BF16), dvs_bf)
            dws = _dot_nt(dvs_bf, vn[:, cs])
            dbs = jnp.sum(dvs, axis=1, keepdims=True)

            @pl.when(i == 0)
            def _():
                dws_ref[g] = dws
                dbst_ref[:, g:g + 1] = dbs

            @pl.when(i > 0)
            def _():
                dws_ref[g] += dws
                dbst_ref[:, g:g + 1] += dbs

        dvn = dvn_ref[...]
        _acc_rows(dlg_ref, i, jnp.sum(dvn * xh, axis=0, keepdims=True))
        _acc_rows(dlb_ref, i, jnp.sum(dvn, axis=0, keepdims=True))
        dxh = dvn * lg_ref[...]
        dvf = rstd * (dxh - jnp.mean(dxh, axis=-1, keepdims=True)
                      - xh * jnp.mean(dxh * xh, axis=-1, keepdims=True))
        dp_ref[:, w:2 * w] = (dvf * dv_fac).astype(BF16)

    full = lambda *shape: pl.BlockSpec(shape, lambda i: (0,) * len(shape))
    return pl.pallas_call(
        body, name=name, grid=(nb,),
        in_specs=[pl.BlockSpec((SGU_BLOCK, w), lambda i: (i, 0)), pl.BlockSpec((SGU_BLOCK, w3), lambda i: (i, 0)),
                  full(1, w), full(1, w), full(SGU_GROUPS, SGU_BLOCK, SGU_BLOCK),
                  full(SGU_GROUPS, SGU_BLOCK, SGU_BLOCK), full(SGU_BLOCK, SGU_GROUPS)],
        out_specs=[pl.BlockSpec((SGU_BLOCK, w3), lambda i: (i, 0)), full(SGU_GROUPS, SGU_BLOCK, SGU_BLOCK),
                   full(SGU_BLOCK, SGU_GROUPS), full(1, w), full(1, w)],
        out_shape=[jax.ShapeDtypeStruct((t, w3), BF16), jax.ShapeDtypeStruct((SGU_GROUPS, SGU_BLOCK, SGU_BLOCK), F32),
                   jax.ShapeDtypeStruct((SGU_BLOCK, SGU_GROUPS), F32), jax.ShapeDtypeStruct((1, w), F32),
                   jax.ShapeDtypeStruct((1, w), F32)],
        scratch_shapes=[pltpu.VMEM((SGU_BLOCK, w), F32)],
        compiler_params=_cparams(("arbitrary",)),
    )(da, proj, ln_gain, ln_bias, ws_masked, ws_masked_t, bs_t)


def _tile2d(rows, cols, block_bytes, row_unit):
    if rows % row_unit == 0:
        return _pick(rows, max(row_unit, block_bytes // (4 * cols)), row_unit), cols
    return rows, _pick(cols, max(LANES, block_bytes // (4 * rows)))


def _adamw(w, g, m, v, *, name, block_bytes=1 << 20, after=None):
    rows, cols = w.shape
    tr, tc = _tile2d(rows, cols, block_bytes, 8)
    g_rows = g.shape[0]
    assert g_rows == rows or tr == rows
    extra_specs, extra_args = ([], []) if after is None else ([pl.BlockSpec(memory_space=pl.ANY)], [after])

    def body(w_ref, g_ref, m_ref, v_ref, *rest):
        go_ref, d_ref, mo_ref, vo_ref = rest[len(extra_args):]
        gv = g_ref[0:tr, :]
        go_ref[...] = gv
        mn = ADAM_B1 * m_ref[...] + (1.0 - ADAM_B1) * gv
        vn = ADAM_B2 * v_ref[...] + (1.0 - ADAM_B2) * (gv * gv)
        m_hat = mn / (1.0 - ADAM_B1 ** ADAM_STEP)
        v_hat = vn / (1.0 - ADAM_B2 ** ADAM_STEP)
        d_ref[...] = -ADAM_LR * (m_hat / (jnp.sqrt(v_hat) + ADAM_EPS) + ADAM_WD * w_ref[...])
        mo_ref[...] = mn
        vo_ref[...] = vn

    spec = pl.BlockSpec((tr, tc), lambda i, j: (i, j))
    g_spec = spec if g_rows == rows else pl.BlockSpec((g_rows, tc), lambda i, j: (0, j))
    return pl.pallas_call(
        body, name=name, grid=(rows // tr, cols // tc), in_specs=[spec, g_spec, spec, spec] + extra_specs,
        out_specs=[spec] * 4, out_shape=[jax.ShapeDtypeStruct((rows, cols), F32)] * 4,
        compiler_params=_cparams(("parallel", "parallel")),
    )(w, g, m, v, *extra_args)


def _matmul_dw_pair(a_me, a_sib, b_me, b_sib, core_idx, *, shards_on, name, after=None, part=(0, 1)):
    T, M = a_me.shape
    N = b_me.shape[1]
    if shards_on == "rows":
        p, count = part
        tm, hc = M // N_CHIPS, N // 2
        hp = hc // count
        tn = _pick(hp, 512)
        per = hp // tn
        grid = (N_CHIPS, per)
        a_spec = pl.BlockSpec((T, tm), lambda i, n, h: (0, i))
        b_me_spec = pl.BlockSpec((T, tn), lambda i, n, h: (0, (h[0] * count + p) * per + n))
        b_sib_spec = pl.BlockSpec((T, tn), lambda i, n, h: (0, p * per + n))
        out_spec = pl.BlockSpec((None, tm, tn), lambda i, n, h: (i, 0, n))
        out_shape = jax.ShapeDtypeStruct((N_CHIPS, tm, hp), BF16)
    else:
        tm, hc = _pick(M, 1024), N // N_CHIPS // 2
        grid = (M // tm, N_CHIPS)
        a_spec = pl.BlockSpec((T, tm), lambda i, j, h: (0, i))
        b_me_spec = pl.BlockSpec((T, hc), lambda i, j, h: (0, 2 * j + h[0]))
        b_sib_spec = pl.BlockSpec((T, hc), lambda i, j, h: (0, j))
        out_spec = pl.BlockSpec((None, tm, hc), lambda i, j, h: (j, i, 0))
        out_shape = jax.ShapeDtypeStruct((N_CHIPS, M, hc), BF16)
    extra_specs, extra_args = ([], []) if after is None else ([pl.BlockSpec(memory_space=pl.ANY)], [after])

    def body(h_ref, am_ref, as_ref, bm_ref, bs_ref, *rest):
        o_ref = rest[len(extra_args)]
        o_ref[...] = (_dot_tn(am_ref[...], bm_ref[...]) + _dot_tn(as_ref[...], bs_ref[...])).astype(BF16)

    grid_spec = pltpu.PrefetchScalarGridSpec(
        num_scalar_prefetch=1, grid=grid, in_specs=[a_spec, a_spec, b_me_spec, b_sib_spec] + extra_specs,
        out_specs=out_spec)
    return pl.pallas_call(
        body, name=name, grid_spec=grid_spec, out_shape=out_shape, compiler_params=_cparams(("parallel", "parallel")),
    )(core_idx, a_me, a_sib, b_me, b_sib, *extra_args)


def _chip_sum(pair, landed, slots, *, name, block_bytes=1 << 20, part=(0, 1), into=None):
    p, count = part
    _, r, hp = pair.shape
    tr, tc = _tile2d(r, hp, block_bytes, 16)
    ncb = hp // tc
    extra_specs, extra_args = ([], []) if into is None else ([pl.BlockSpec(memory_space=pl.ANY)], [into])

    def body(s_ref, own_ref, l0_ref, l1_ref, l2_ref, *rest):
        rest[-1][...] = ((own_ref[...].astype(F32) + l0_ref[...].astype(F32)) + l1_ref[...].astype(F32)
                         ) + l2_ref[...].astype(F32)

    def slab(which):
        return pl.BlockSpec((None, tr, tc), lambda i, k, s: (s[which], i, k))

    grid_spec = pltpu.PrefetchScalarGridSpec(
        num_scalar_prefetch=1, grid=(r // tr, ncb),
        in_specs=[slab(0), slab(1), slab(2), slab(3)] + extra_specs,
        out_specs=pl.BlockSpec((tr, tc), lambda i, k, s: (i, (s[4] * count + p) * ncb + k)))
    return pl.pallas_call(
        body, name=name, grid_spec=grid_spec, out_shape=jax.ShapeDtypeStruct((r, 2 * hp * count), F32),
        input_output_aliases={} if into is None else {5: 0},
        compiler_params=_cparams(("parallel", "parallel")),
    )(slots, pair, landed, landed, landed, *extra_args)


def _stack_sum(x, *, name, out_dtype=F32, block_bytes=1 << 20):
    s, r, c = x.shape
    tr = _pick(r, max(8, block_bytes // (4 * c)), 16) if r % 16 == 0 else r

    def body(x_ref, o_ref):
        acc = x_ref[0].astype(F32)
        for j in range(1, s):
            acc = acc + x_ref[j].astype(F32)
        o_ref[...] = acc.astype(out_dtype)

    return pl.pallas_call(
        body, name=name, grid=(r // tr,),
        in_specs=[pl.BlockSpec((s, tr, c), lambda i: (0, i, 0))], out_specs=pl.BlockSpec((tr, c), lambda i: (i, 0)),
        out_shape=jax.ShapeDtypeStruct((r, c), out_dtype), compiler_params=_cparams(("parallel",)),
    )(x)


HBM = pl.BlockSpec(memory_space=pltpu.HBM)


def _place():
    x, y, c = lax.axis_index("x"), lax.axis_index("y"), lax.axis_index("c")
    other_chips = [(1 - x, y), (x, 1 - y), (1 - x, 1 - y)]
    return x, y, c, other_chips


def _handshake(peers):
    barrier = pltpu.get_barrier_semaphore()
    for peer in peers:
        pl.semaphore_signal(barrier, inc=1, device_id=peer, device_id_type=MESH)
    pl.semaphore_wait(barrier, len(peers))


def _sibling():
    x, y, c, _ = _place()
    return [(x, y, 1 - c)]


def _same_core_chips():
    x, y, c, chips = _place()
    return [(cx, cy, c) for cx, cy in chips]


def _same_core_neighbours():
    x, y, c, _ = _place()
    return [(1 - x, y, c), (x, 1 - y, c)]


def _split_params(cid):
    return pltpu.CompilerParams(has_side_effects=SIDE_EFFECT, collective_id=cid)


def _half_cols(cols, which):
    hc = cols // 2
    return pl.ds(pl.multiple_of(which * hc, LANES), hc)


SEM = pl.BlockSpec(memory_space=pltpu.SEMAPHORE)
ANY = pl.BlockSpec(memory_space=pl.ANY)
SIDE_EFFECT = pltpu.SideEffectType.DATAFLOW_SIDE_EFFECTING
TOKEN_SHAPE = (8, LANES)


def _hbm(shape, dtype):
    return pltpu.HBM(shape, dtype)


def _in_hbm(a):
    return pltpu.with_memory_space_constraint(a, pltpu.HBM)


def _gather_copy(src_ref, land_ref, ssem, rsem, k, chip_of_block, to, c):
    cols = src_ref.shape[1]
    return pltpu.make_async_remote_copy(
        src_ref=src_ref.at[:, _half_cols(cols, c)], dst_ref=land_ref.at[chip_of_block, :, _half_cols(cols, c)],
        send_sem=ssem.at[k], recv_sem=rsem.at[k], device_id=to, device_id_type=MESH)


NEIGHBOURS = (0, 1)
ALL_CHIPS = (0, 1, 2)


def _gather_start(shards, *, name, cid, after=(), relayed=()):
    n = len(shards)
    after = list(after)

    def body(*refs):
        srcs, lands = refs[:n], refs[n:2 * n]
        outs = refs[2 * n + len(after):]
        token = outs[-1]
        _handshake(_same_core_chips())
        x, y, c, chips = _place()
        me = 2 * x + y
        for a in range(n):
            ssem, rsem = outs[4 * a], outs[4 * a + 1]
            for k in NEIGHBOURS if a in relayed else ALL_CHIPS:
                cx, cy = chips[k]
                _gather_copy(srcs[a], lands[a], ssem, rsem, k, me, (cx, cy, c), c).start()
        token[...] = jnp.zeros_like(token)

    out_shape, out_specs, aliases = [], [], {}
    for a, s in enumerate(shards):
        out_shape += [pltpu.SemaphoreType.DMA((3,)), pltpu.SemaphoreType.DMA((3,)), _hbm(s.shape, s.dtype),
                      _hbm((N_CHIPS,) + s.shape, s.dtype)]
        out_specs += [SEM, SEM, HBM, HBM]
        aliases[a] = 4 * a + 2
        aliases[n + a] = 4 * a + 3
    out_shape.append(jax.ShapeDtypeStruct(TOKEN_SHAPE, F32))
    out_specs.append(pl.BlockSpec(memory_space=pltpu.VMEM))
    lands = [_in_hbm(lax.empty((N_CHIPS,) + s.shape, s.dtype)) for s in shards]
    res = pl.pallas_call(
        body, name=name, in_specs=[HBM] * (2 * n) + [ANY] * len(after), out_specs=out_specs, out_shape=out_shape,
        input_output_aliases=aliases, compiler_params=_split_params(cid),
    )(*[_in_hbm(s) for s in shards], *lands, *after)
    return [tuple(res[4 * a:4 * a + 4]) for a in range(n)], res[-1]


def _wait_call(wait_fn, parts, after, *, name):
    ssem, rsem, src, land = parts
    after = list(after) if isinstance(after, (list, tuple)) else [after]

    def body(src_ref, land_ref, ssem_ref, rsem_ref, *rest):
        wait_fn(src_ref, land_ref, ssem_ref, rsem_ref)

    return pl.pallas_call(
        body, name=name, in_specs=[HBM, HBM, SEM, SEM] + [ANY] * len(after), out_specs=[HBM, HBM],
        out_shape=[_hbm(src.shape, src.dtype), _hbm(land.shape, land.dtype)], input_output_aliases={0: 0, 1: 1},
        compiler_params=pltpu.CompilerParams(has_side_effects=SIDE_EFFECT),
    )(src, land, ssem, rsem, *after)


def _gather_wait(parts, after, *, name, ks=ALL_CHIPS):
    def wait(src_ref, land_ref, ssem_ref, rsem_ref):
        x, y, c, chips = _place()
        for k in ks:
            cx, cy = chips[k]
            cp = _gather_copy(src_ref, land_ref, ssem_ref, rsem_ref, k, 2 * cx + cy, (x, y, c), c)
            cp.wait_send()
            cp.wait_recv()

    return _wait_call(wait, parts, after, name=name)


def _relay_copy(buf_ref, ssem, rsem, k, slab, to, c):
    hr = buf_ref.shape[1] // 2
    part = buf_ref.at[slab, pl.ds(k * hr, hr), _half_cols(buf_ref.shape[2], c)]
    return pltpu.make_async_remote_copy(
        src_ref=part, dst_ref=part, send_sem=ssem.at[k], recv_sem=rsem.at[k], device_id=to, device_id_type=MESH)


def _relay_start(land, *, name, cid):
    def body(buf_ref, ssem, rsem, buf_out, token):
        _handshake(_same_core_neighbours())
        x, y, c, _ = _place()
        _relay_copy(buf_ref, ssem, rsem, 0, 2 * (1 - x) + y, (x, 1 - y, c), c).start()
        _relay_copy(buf_ref, ssem, rsem, 1, 2 * x + 1 - y, (1 - x, y, c), c).start()
        token[...] = jnp.zeros_like(token)

    res = pl.pallas_call(
        body, name=name, in_specs=[HBM], out_specs=[SEM, SEM, HBM, pl.BlockSpec(memory_space=pltpu.VMEM)],
        out_shape=[pltpu.SemaphoreType.DMA((2,)), pltpu.SemaphoreType.DMA((2,)), _hbm(land.shape, land.dtype),
                   jax.ShapeDtypeStruct(TOKEN_SHAPE, F32)],
        input_output_aliases={0: 2}, compiler_params=_split_params(cid),
    )(land)
    return tuple(res[:3]), res[3]


def _relay_wait(parts, after, *, name):
    ssem, rsem, buf = parts
    after = list(after) if isinstance(after, (list, tuple)) else [after]

    def body(buf_ref, ssem_ref, rsem_ref, *rest):
        x, y, c, _ = _place()
        diagonal = 2 * (1 - x) + 1 - y
        _relay_copy(buf_ref, ssem_ref, rsem_ref, 0, 2 * (1 - x) + y, (x, y, c), c).wait_send()
        _relay_copy(buf_ref, ssem_ref, rsem_ref, 1, 2 * x + 1 - y, (x, y, c), c).wait_send()
        _relay_copy(buf_ref, ssem_ref, rsem_ref, 0, diagonal, (x, y, c), c).wait_recv()
        _relay_copy(buf_ref, ssem_ref, rsem_ref, 1, diagonal, (x, y, c), c).wait_recv()

    return pl.pallas_call(
        body, name=name, in_specs=[HBM, SEM, SEM] + [ANY] * len(after), out_specs=HBM,
        out_shape=_hbm(buf.shape, buf.dtype), input_output_aliases={0: 0},
        compiler_params=pltpu.CompilerParams(has_side_effects=SIDE_EFFECT),
    )(buf, ssem, rsem, *after)


def _forward_copy(buf_ref, ssem, rsem, k, slab, which, to):
    part = buf_ref.at[slab, :, _half_cols(buf_ref.shape[2], which)]
    return pltpu.make_async_remote_copy(
        src_ref=part, dst_ref=part, send_sem=ssem.at[k], recv_sem=rsem.at[k], device_id=to, device_id_type=MESH)


def _sibling_forward(land, *, name, cid, ks=ALL_CHIPS):
    def body(_, buf, send_sems, recv_sems):
        _handshake(_sibling())
        x, y, c, chips = _place()
        copies = []
        for k in ks:
            cx, cy = chips[k]
            cp = _forward_copy(buf, send_sems, recv_sems, k, 2 * cx + cy, c, (x, y, 1 - c))
            cp.start()
            copies.append(cp)
        for k in ks:
            cx, cy = chips[k]
            _forward_copy(buf, send_sems, recv_sems, k, 2 * cx + cy, 1 - c, (x, y, c)).wait_recv()
        for cp in copies:
            cp.wait_send()

    return pl.pallas_call(
        body, name=name, in_specs=[HBM], out_specs=HBM, out_shape=jax.ShapeDtypeStruct(land.shape, land.dtype),
        input_output_aliases={0: 0},
        scratch_shapes=[pltpu.SemaphoreType.DMA((3,)), pltpu.SemaphoreType.DMA((3,))],
        compiler_params=pltpu.CompilerParams(collective_id=cid),
    )(land)


def _forward_start(land, *, name, cid, ks=ALL_CHIPS):
    def body(buf_ref, ssem, rsem, buf_out, token):
        _handshake(_sibling())
        x, y, c, chips = _place()
        for k in ks:
            cx, cy = chips[k]
            _forward_copy(buf_ref, ssem, rsem, k, 2 * cx + cy, c, (x, y, 1 - c)).start()
        token[...] = jnp.zeros_like(token)

    res = pl.pallas_call(
        body, name=name, in_specs=[HBM], out_specs=[SEM, SEM, HBM, pl.BlockSpec(memory_space=pltpu.VMEM)],
        out_shape=[pltpu.SemaphoreType.DMA((3,)), pltpu.SemaphoreType.DMA((3,)), _hbm(land.shape, land.dtype),
                   jax.ShapeDtypeStruct(TOKEN_SHAPE, F32)],
        input_output_aliases={0: 2}, compiler_params=_split_params(cid),
    )(land)
    return tuple(res[:3]), res[3]


def _forward_wait(parts, after, *, name, ks=ALL_CHIPS):
    ssem, rsem, buf = parts
    after = list(after) if isinstance(after, (list, tuple)) else [after]

    def body(buf_ref, ssem_ref, rsem_ref, *rest):
        x, y, c, chips = _place()
        for k in ks:
            cx, cy = chips[k]
            _forward_copy(buf_ref, ssem_ref, rsem_ref, k, 2 * cx + cy, c, (x, y, c)).wait_send()
            _forward_copy(buf_ref, ssem_ref, rsem_ref, k, 2 * cx + cy, 1 - c, (x, y, c)).wait_recv()

    return pl.pallas_call(
        body, name=name, in_specs=[HBM, SEM, SEM] + [ANY] * len(after), out_specs=HBM,
        out_shape=_hbm(buf.shape, buf.dtype), input_output_aliases={0: 0},
        compiler_params=pltpu.CompilerParams(has_side_effects=SIDE_EFFECT),
    )(buf, ssem, rsem, *after)


def _share_copy(buf_ref, ssem, rsem, a, which, to):
    part = buf_ref.at[:, _half_cols(buf_ref.shape[1], which)]
    return pltpu.make_async_remote_copy(
        src_ref=part, dst_ref=part, send_sem=ssem.at[a], recv_sem=rsem.at[a], device_id=to, device_id_type=MESH)


def _share_start(arrays, *, name, cid):
    n = len(arrays)

    def body(*refs):
        bufs, ssem, rsem, token = refs[:n], refs[n], refs[n + 1], refs[-1]
        _handshake(_sibling())
        x, y, c, _ = _place()
        for a in range(n):
            _share_copy(bufs[a], ssem, rsem, a, c, (x, y, 1 - c)).start()
        token[...] = jnp.zeros_like(token)

    res = pl.pallas_call(
        body, name=name, in_specs=[HBM] * n,
        out_specs=[SEM, SEM] + [HBM] * n + [pl.BlockSpec(memory_space=pltpu.VMEM)],
        out_shape=[pltpu.SemaphoreType.DMA((n,)), pltpu.SemaphoreType.DMA((n,))]
        + [_hbm(b.shape, b.dtype) for b in arrays] + [jax.ShapeDtypeStruct(TOKEN_SHAPE, F32)],
        input_output_aliases={a: 2 + a for a in range(n)}, compiler_params=_split_params(cid),
    )(*[_in_hbm(b) for b in arrays])
    return (res[0], res[1], list(res[2:2 + n])), res[-1]


def _share_wait(parts, after, *, name):
    ssem, rsem, bufs = parts
    n = len(bufs)
    after = list(after) if isinstance(after, (list, tuple)) else [after]

    def body(*refs):
        buf_refs, ssem_ref, rsem_ref = refs[:n], refs[n], refs[n + 1]
        x, y, c, _ = _place()
        for a in range(n):
            _share_copy(buf_refs[a], ssem_ref, rsem_ref, a, c, (x, y, c)).wait_send()
            _share_copy(buf_refs[a], ssem_ref, rsem_ref, a, 1 - c, (x, y, c)).wait_recv()

    return pl.pallas_call(
        body, name=name, in_specs=[HBM] * n + [SEM, SEM] + [ANY] * len(after), out_specs=[HBM] * n,
        out_shape=[_hbm(b.shape, b.dtype) for b in bufs], input_output_aliases={a: a for a in range(n)},
        compiler_params=pltpu.CompilerParams(has_side_effects=SIDE_EFFECT),
    )(*bufs, ssem, rsem, *after)


def _scatter_copy(src_ref, land_ref, ssem, rsem, k, src_slab, dst_slab, to):
    return pltpu.make_async_remote_copy(
        src_ref=src_ref.at[src_slab], dst_ref=land_ref.at[dst_slab], send_sem=ssem.at[k], recv_sem=rsem.at[k],
        device_id=to, device_id_type=MESH)


def _scatter_start(part, *, name, cid):
    def start(src_ref, land_ref, ssem, rsem):
        x, y, c, chips = _place()
        me = 2 * x + y
        for k, (cx, cy) in enumerate(chips):
            _scatter_copy(src_ref, land_ref, ssem, rsem, k, 2 * cx + cy, me, (cx, cy, c)).start()

    return _split_start(start, _same_core_chips, part, part.shape, N_CHIPS - 1, name=name, cid=cid)


def _scatter_wait(parts, after, *, name):
    def wait(src_ref, land_ref, ssem_ref, rsem_ref):
        x, y, c, chips = _place()
        for k, (cx, cy) in enumerate(chips):
            idx = 2 * cx + cy
            cp = _scatter_copy(src_ref, land_ref, ssem_ref, rsem_ref, k, idx, idx, (x, y, c))
            cp.wait_send()
            cp.wait_recv()

    return _wait_call(wait, parts, after, name=name)


def _split_start(start_fn, peers_fn, src, land_shape, n_sems, *, name, cid):
    def body(src_ref, land_ref, ssem, rsem, src_out, land_out, token):
        _handshake(peers_fn())
        start_fn(src_ref, land_ref, ssem, rsem)
        token[...] = jnp.zeros_like(token)

    res = pl.pallas_call(
        body, name=name, in_specs=[HBM, HBM], out_specs=[SEM, SEM, HBM, HBM, pl.BlockSpec(memory_space=pltpu.VMEM)],
        out_shape=[pltpu.SemaphoreType.DMA((n_sems,)), pltpu.SemaphoreType.DMA((n_sems,)), _hbm(src.shape, src.dtype),
                   _hbm(land_shape, src.dtype), jax.ShapeDtypeStruct(TOKEN_SHAPE, F32)],
        input_output_aliases={0: 2, 1: 3}, compiler_params=_split_params(cid),
    )(_in_hbm(src), _in_hbm(lax.empty(land_shape, src.dtype)))
    return tuple(res[:4]), res[4]


def _sibling_copies(src_ref, land_ref, ssem, rsem, k0, groups, which, to):
    def copy(k, src, dst):
        return pltpu.make_async_remote_copy(
            src_ref=src, dst_ref=dst, send_sem=ssem.at[k], recv_sem=rsem.at[k], device_id=to, device_id_type=MESH)

    if groups == 0:
        return [copy(k0, src_ref, land_ref)]
    hw = src_ref.shape[1] // groups // 2
    return [copy(k0 + j, src_ref.at[:, pl.ds(pl.multiple_of((2 * j + which) * hw, LANES), hw)],
                 land_ref.at[:, j * hw:(j + 1) * hw]) for j in range(groups)]


def _to_sibling_start(items, *, name, cid):
    n = len(items)
    shapes = [a.shape if g == 0 else (a.shape[0], a.shape[1] // 2) for a, g in items]
    first = [sum(max(g, 1) for _, g in items[:k]) for k in range(n + 1)]

    def body(*refs):
        srcs, lands, ssem, rsem, token = refs[:n], refs[n:2 * n], refs[2 * n], refs[2 * n + 1], refs[-1]
        _handshake(_sibling())
        x, y, c, _ = _place()
        for k, (_, g) in enumerate(items):
            for cp in _sibling_copies(srcs[k], lands[k], ssem, rsem, first[k], g, 1 - c, (x, y, 1 - c)):
                cp.start()
        token[...] = jnp.zeros_like(token)

    res = pl.pallas_call(
        body, name=name, in_specs=[HBM] * (2 * n),
        out_specs=[SEM, SEM] + [HBM] * (2 * n) + [pl.BlockSpec(memory_space=pltpu.VMEM)],
        out_shape=[pltpu.SemaphoreType.DMA((first[n],)), pltpu.SemaphoreType.DMA((first[n],))]
        + [_hbm(a.shape, a.dtype) for a, _ in items] + [_hbm(s, a.dtype) for s, (a, _) in zip(shapes, items)]
        + [jax.ShapeDtypeStruct(TOKEN_SHAPE, F32)],
        input_output_aliases={k: 2 + k for k in range(2 * n)}, compiler_params=_split_params(cid),
    )(*[_in_hbm(a) for a, _ in items], *[_in_hbm(lax.empty(s, a.dtype)) for s, (a, _) in zip(shapes, items)])
    return [(res[0], res[1], first[k], g, res[2 + k], res[2 + n + k]) for k, (_, g) in enumerate(items)], res[-1]


def _from_sibling(flight, after, *, name):
    ssem, rsem, k0, groups, src, land = flight

    def wait(src_ref, land_ref, ssem_ref, rsem_ref):
        x, y, c, _ = _place()
        for cp in _sibling_copies(src_ref, land_ref, ssem_ref, rsem_ref, k0, groups, 1 - c, (x, y, c)):
            cp.wait_send()
            cp.wait_recv()

    return _wait_call(wait, (ssem, rsem, src, land), after, name=name)


def _dev_peers(x, y, c, chips):
    return [(x, y, 1 - c)] + [(cx, cy, c) for cx, cy in chips] + [(cx, cy, 1 - c) for cx, cy in chips]


def _dev_gather_start(part, *, name, cid):
    def start(src_ref, land_ref, ssem, rsem):
        x, y, c, chips = _place()
        for k, to in enumerate(_dev_peers(x, y, c, chips)):
            pltpu.make_async_remote_copy(
                src_ref=src_ref, dst_ref=land_ref.at[4 * x + 2 * y + c], send_sem=ssem.at[k], recv_sem=rsem.at[k],
                device_id=to, device_id_type=MESH).start()

    return _split_start(start, lambda: _dev_peers(*_place()), part, (N_DEV,) + part.shape, N_DEV - 1, name=name,
                        cid=cid)


def _dev_gather_wait(parts, after, *, name):
    def wait(src_ref, land_ref, ssem_ref, rsem_ref):
        x, y, c, chips = _place()
        for k, (px, py, pc) in enumerate(_dev_peers(x, y, c, chips)):
            cp = pltpu.make_async_remote_copy(
                src_ref=src_ref, dst_ref=land_ref.at[4 * px + 2 * py + pc], send_sem=ssem_ref.at[k],
                recv_sem=rsem_ref.at[k], device_id=(x, y, c), device_id_type=MESH)
            cp.wait_send()
            cp.wait_recv()

    return _wait_call(wait, parts, after, name=name)[1]


def _sibling_share_halves(arrays, *, name, cid):
    n = len(arrays)

    def body(*refs):
        bufs = refs[n:2 * n]
        send_sems, recv_sems = refs[2 * n:]
        _handshake(_sibling())
        x, y, c, _ = _place()
        copies = []
        for a in range(n):
            mine = bufs[a].at[:, _half_cols(bufs[a].shape[1], c)]
            cp = pltpu.make_async_remote_copy(
                src_ref=mine, dst_ref=mine, send_sem=send_sems.at[a], recv_sem=recv_sems.at[a],
                device_id=(x, y, 1 - c), device_id_type=MESH)
            cp.start()
            copies.append(cp)
        for a in range(n):
            theirs = bufs[a].at[:, _half_cols(bufs[a].shape[1], 1 - c)]
            pltpu.make_async_remote_copy(
                src_ref=theirs, dst_ref=theirs, send_sem=send_sems.at[a], recv_sem=recv_sems.at[a],
                device_id=(x, y, c), device_id_type=MESH).wait_recv()
        for cp in copies:
            cp.wait_send()

    return pl.pallas_call(
        body, name=name, in_specs=[HBM] * n, out_specs=[HBM] * n,
        out_shape=[jax.ShapeDtypeStruct(h.shape, h.dtype) for h in arrays],
        input_output_aliases={a: a for a in range(n)},
        scratch_shapes=[pltpu.SemaphoreType.DMA((n,)), pltpu.SemaphoreType.DMA((n,))],
        compiler_params=pltpu.CompilerParams(collective_id=cid),
    )(*arrays)


def _pack(arrays, rows_multiple=16, width=LANES):
    flat = jnp.concatenate([a.astype(F32).reshape(-1) for a in arrays])
    total = flat.shape[0]
    rows = -(-total // width)
    rows = -(-rows // rows_multiple) * rows_multiple
    return jnp.pad(flat, (0, rows * width - total)).reshape(rows, width)


def _unpack(buf, shapes):
    flat = buf.reshape(-1)
    out, off = [], 0
    for s in shapes:
        n = math.prod(s)
        out.append(flat[off:off + n].reshape(s))
        off += n
    return out


def kernel(x, norm_pre, norm_post, gla_w_in, gla_w_gate2, gla_b_gate, gla_o_gain, gla_w_out, sgu_w_in, sgu_ln_gain, sgu_ln_bias, sgu_w_spatial, sgu_b_spatial, sgu_w_out, loss_target, m_norm_pre, m_norm_post, m_gla_w_in, m_gla_w_gate2, m_gla_b_gate, m_gla_o_gain, m_gla_w_out, m_sgu_w_in, m_sgu_ln_gain, m_sgu_ln_bias, m_sgu_w_spatial, m_sgu_b_spatial, m_sgu_w_out, v_norm_pre, v_norm_post, v_gla_w_in, v_gla_w_gate2, v_gla_b_gate, v_gla_o_gain, v_gla_w_out, v_sgu_w_in, v_sgu_ln_gain, v_sgu_ln_bias, v_sgu_w_spatial, v_sgu_b_spatial, v_sgu_w_out):
    _, t, d = x.shape
    dk = d // 2
    ws = gla_w_in.shape[2]
    wp = -(-ws // LANES) * LANES
    lay = (ws, wp)
    chip =2 * lax.axis_index("x") + lax.axis_index("y")
    core = lax.axis_index("c")
    core_idx = core.astype(jnp.int32).reshape(1)
    others = jnp.arange(N_CHIPS - 1, dtype=jnp.int32)
    others = others + (others >= chip).astype(jnp.int32)
    slots = jnp.concatenate([chip.astype(jnp.int32).reshape(1), others, core_idx])

    x0 = x[0]
    target = loss_target[0]

    wt_in_g, mt_in_g, vt_in_g = gla_w_in[0].T, m_gla_w_in[0].T, v_gla_w_in[0].T

    small_shard = _pack([gla_w_gate2[0], sgu_ln_gain[0], sgu_ln_bias[0]], rows_multiple=8, width=2 * LANES)
    own = [small_shard, jnp.pad(wt_in_g.astype(BF16), ((0, wp - ws), (0, 0)))]
    in_flight, token = _gather_start(own, name="gather_start_a", cid=0, relayed=(1,))

    def with_sibling_and_own(mine, land, name, cid):
        return lax.dynamic_update_slice(_sibling_forward(land, name=name + "_share", cid=cid), mine[None],
                                        (chip, 0, 0))

    h0 = _norm_pre(x0, norm_pre[0:1] + token[0:1, 0:1], name="pre0")
    g_small = with_sibling_and_own(*_gather_wait(in_flight[0], h0, name="w_small_wait"), "w_small", 12)
    mine, land = _gather_wait(in_flight[1], [g_small, wt_in_g, mt_in_g, vt_in_g], name="w_gla_in_wait", ks=NEIGHBOURS)
    relay, token = _relay_start(land, name="w_gla_in_relay", cid=11)
    crossing, token = _forward_start(relay[2], name="w_gla_in_share_near", cid=22, ks=NEIGHBOURS)
    own_later = [(p[0] + token[0, 0]).astype(BF16) for p in (gla_w_out, sgu_w_in, sgu_w_out)]
    in_flight_later, token = _gather_start(own_later, name="gather_start_b", cid=1, after=[token])
    in_flight = in_flight + in_flight_later
    land = _relay_wait((relay[0], relay[1], crossing[2]), token, name="w_gla_in_relay_wait")
    land = _forward_wait((crossing[0], crossing[1], land), token, name="w_gla_in_share_near_wait", ks=NEIGHBOURS)
    land = _sibling_forward(land, name="w_gla_in_share_far", cid=13, ks=(2,))
    wt_g = lax.dynamic_update_slice(land, mine[None], (chip, 0, 0)).reshape(N_CHIPS * wp, d)

    def behind(small, token):
        return small + token[0:1, 0:1]

    def arriving(i, after, name):
        mine, land = _gather_wait(in_flight[i], after, name=name + "_wait")
        crossing, token = _forward_start(land, name=name + "_share", cid=i)
        return (mine, crossing), token

    def arrived(pending, after, name):
        mine, crossing = pending
        return lax.dynamic_update_slice(_forward_wait(crossing, after, name=name + "_share_wait"), mine[None],
                                        (chip, 0, 0))

    shard_shapes = [gla_w_gate2.shape[1:], sgu_ln_gain.shape[1:], sgu_ln_bias.shape[1:]]
    per_chip = [_unpack(g_small[j], shard_shapes) for j in range(N_CHIPS)]
    w2_full = jnp.concatenate([p[0] for p in per_chip], axis=1)
    ln_gain = jnp.concatenate([p[1] for p in per_chip], axis=0)[None, :]
    ln_bias = jnp.concatenate([p[2] for p in per_chip], axis=0)[None, :]
    w2p = jnp.pad(w2_full, ((0, LANES - GLA_GATE_RANK), (0, 0)))

    pos_chunk = jnp.arange(SGU_BLOCK) // CHUNK
    mask = pos_chunk[:, None] >= pos_chunk[None, :]
    ws_masked = jnp.where(mask[None], sgu_w_spatial[0], 0.0)
    ws_masked_t = ws_masked.transpose(0, 2, 1)
    bs_t = sgu_b_spatial[0].T

    proj0 = _matmul(h0, wt_g, mode="nt", out_dtype=F32, name="gla_in", tn=wp)
    pending, tok = arriving(2, proj0, "w_gla_out")
    o0, a0, s_before, s_final = _gla_fwd(proj0, w2p, behind(gla_b_gate, tok), gla_o_gain, lay, name="gla_scan")
    w_out_g = arrived(pending, a0, "w_gla_out").reshape(d, d)
    y0 = _matmul(a0, w_out_g, mode="nn", out_dtype=F32, name="gla_out")
    pending, tok = arriving(3, y0, "w_sgu_in")
    x1, h1 = _post_then_pre(x0, y0, behind(norm_post[0:1], tok), norm_pre[1:2], name="post0_pre1")
    g_wi_s = arrived(pending, h1, "w_sgu_in")
    pending, tok = arriving(4, g_wi_s, "w_sgu_out")
    proj1 = _matmul(h1, g_wi_s, mode="nn", out_dtype=F32, name="sgu_in", b_shards=True, after=tok)
    a1 = _sgu_fwd(proj1, ln_gain, ln_bias, ws_masked, bs_t, name="sgu_gate")
    w_out_s = arrived(pending, a1, "w_sgu_out").reshape(d, d)
    acts, tok = _to_sibling_start([(a1, 0), (a0, 0), (h1, 0), (h0, 1)], name="acts_to_sibling", cid=5)
    a1, a0, h1, h0 = [f[4] for f in acts]
    y1 = _matmul(a1, w_out_s, mode="nn", out_dtype=F32, name="sgu_out", after=tok)
    loss_part, dx2, dy1, d_post1 = _loss_head(x1, y1, norm_post[1:2], target, name="loss_head")

    def pair_gradient(a_sent, b_sent, after, shards_on, name, cid):
        a_me, a_sib = _from_sibling(a_sent, after, name=name + "_a_wait")
        b_me, b_sib = _from_sibling(b_sent, [a_sib] + list(after), name=name + "_b_wait")
        pair = _matmul_dw_pair(a_me, a_sib, b_me, b_sib, core_idx, shards_on=shards_on,
                               name=name + "_pair")
        return _scatter_start(pair, name=name + "_start", cid=cid)

    def reduced(flight, after, name):
        pair, landed = _scatter_wait(flight, after, name=name + "_wait")
        return _chip_sum(pair, landed, slots, name=name + "_sum")

    (dy1_sent,), tok = _to_sibling_start([(dy1, 1)], name="dy1_to_sibling", cid=6)
    dy1 = dy1_sent[4]
    da1 = _matmul(dy1, w_out_s, mode="nt", out_dtype=F32, name="d_sgu_act", after=tok)
    fl_wo_s, tok = pair_gradient(acts[0], dy1_sent, [da1], "rows", "g_sgu_out", 15)
    dproj1, d_ws, d_bs_t, d_lg, d_lb = _sgu_bwd(da1, proj1, ln_gain, behind(ln_bias, tok), ws_masked, ws_masked_t,
                                                bs_t, name="sgu_gate_bwd")
    (dp1_sent,), tok = _to_sibling_start([(dproj1, N_CHIPS)], name="dproj1_to_sibling", cid=7)
    dproj1 = dp1_sent[4]
    dh1 = _matmul_nt_shards(dproj1, g_wi_s, out_dtype=F32, name="d_sgu_h", after=tok)
    fl_wi_s, tok = pair_gradient(acts[2], dp1_sent, [dh1], "cols", "g_sgu_in", 16)
    dx1, dy0, d_pre1, d_post0 = _mid_bwd(dx2, dh1, x1, behind(norm_pre[1:2], tok), y0, norm_post[0:1],
                                         name="pre1_post0_bwd")
    (dy0_sent,), tok = _to_sibling_start([(dy0, 1)], name="dy0_to_sibling", cid=8)
    dy0 = dy0_sent[4]
    da0 = _matmul(dy0, w_out_g, mode="nt", out_dtype=F32, name="d_gla_act", after=tok)
    fl_wo_g, tok = pair_gradient(acts[1], dy0_sent, [da0], "rows", "g_gla_out", 17)
    dproj0, d_og, d_bg, d_w2p = _gla_bwd(da0, o0, proj0, w2p, behind(gla_b_gate, tok), gla_o_gain, s_before, s_final,
                                         lay, name="gla_scan_bwd")
    early_shapes = [norm_post.shape, gla_b_gate.shape, gla_o_gain.shape, sgu_w_spatial.shape, sgu_b_spatial.shape,
                    (1, GLA_GATE_RANK, dk), (1, d), (1, d), (1, LANES)]
    early_part = _pack([jnp.concatenate([d_post0, d_post1], axis=0), d_bg, d_og, jnp.where(mask[None], d_ws, 0.0)[None],
                        d_bs_t.T[None], d_w2p[:GLA_GATE_RANK][None], d_lg, d_lb, loss_part])
    early_flight, tok = _dev_gather_start(early_part, name="small_early_start", cid=20)
    (dp0_sent,), tok_sent = _to_sibling_start([(dproj0, 0)], name="dproj0_to_sibling", cid=9)
    dproj0 = dp0_sent[4]
    dh0 = _matmul(dproj0, wt_g, mode="nn", out_dtype=F32, name="d_gla_h", after=tok_sent)
    a_me, a_sib = _from_sibling(dp0_sent, [dh0, tok], name="g_gla_in_a_wait")
    b_me, b_sib = _from_sibling(acts[3], [a_sib, dh0], name="g_gla_in_b_wait")
    fl_wi_g, tok_scatter = [], None
    for p in range(2):
        pair = _matmul_dw_pair(a_me, a_sib, b_me, b_sib, core_idx, shards_on="rows", part=(p, 2),
                               name=f"g_gla_in_pair{p}", after=tok_scatter)
        flight, tok_scatter = _scatter_start(pair, name=f"g_gla_in_start{p}", cid=18 + p)
        fl_wi_g.append(flight)
    r_wo_s = reduced(fl_wo_s, tok_scatter, "g_sgu_out")
    r_wi_s = reduced(fl_wi_s, r_wo_s, "g_sgu_in")
    r_wo_g = reduced(fl_wo_g, r_wi_s, "g_gla_out")
    sharing, tok = _share_start([r_wo_s, r_wi_s, r_wo_g], name="grads_share_a", cid=10)
    grad_x, d_pre0 = _first_bwd(dx1, dh0, x0, behind(norm_pre[0:1], tok), name="pre0_bwd")

    late_part = _pack([jnp.concatenate([d_pre0, d_pre1], axis=0)])
    late_flight, tok = _dev_gather_start(late_part, name="small_late_start", cid=21)

    def big_update(w, g, m, v, name, after=None):
        return [u[None] for u in _adamw(w[0], g, m[0], v[0], name=name, after=after)]

    g_wo_sgu, g_wi_sgu, g_wo_gla = _share_wait(sharing, [grad_x, tok], name="grads_share_a_wait")
    u_wi_sgu = big_update(sgu_w_in, g_wi_sgu, m_sgu_w_in, v_sgu_w_in, "adamw_sgu_w_in")
    u_wo_gla = big_update(gla_w_out, g_wo_gla, m_gla_w_out, v_gla_w_out, "adamw_gla_w_out", after=u_wi_sgu[1])

    r_wi_g, behind_this = None, u_wo_gla[1]
    for p, flight in enumerate(fl_wi_g):
        pair, landed = _scatter_wait(flight, behind_this, name=f"g_gla_in_wait{p}")
        r_wi_g = behind_this = _chip_sum(pair, landed, slots, part=(p, 2), into=r_wi_g, name=f"g_gla_in_sum{p}")
    gt_wi_gla, = _sibling_share_halves([r_wi_g], name="grads_share_b", cid=14)
    u_wi_gla_t = _adamw(wt_in_g, gt_wi_gla, mt_in_g, vt_in_g, name="adamw_gla_w_in")
    u_wi_gla = [u.T[None] for u in u_wi_gla_t]
    u_wo_sgu = big_update(sgu_w_out, g_wo_sgu, m_sgu_w_out, v_sgu_w_out, "adamw_sgu_w_out", after=u_wi_gla_t[1])

    def summed_over_devices(part, flight, after, shapes, name):
        land = _dev_gather_wait(flight, after, name=name + "_wait")
        every = lax.dynamic_update_slice(land, part[None], (2 * chip + core, 0, 0))
        return _unpack(_stack_sum(every, name=name + "_sum"), shapes)

    (g_post, g_bg, g_og, g_wsp, g_bsp, g_w2_full, g_lg_full, g_lb_full, loss_vec) = summed_over_devices(
        early_part, early_flight, u_wo_sgu[1], early_shapes, "small_early")
    g_pre, = summed_over_devices(late_part, late_flight, loss_vec, [norm_pre.shape], "small_late")
    loss = loss_vec[0, 0]
    g_w2 = lax.dynamic_slice_in_dim(g_w2_full, chip * (dk // N_CHIPS), dk // N_CHIPS, axis=2)
    g_lg = lax.dynamic_slice_in_dim(g_lg_full, chip * (d // N_CHIPS), d // N_CHIPS, axis=1)
    g_lb = lax.dynamic_slice_in_dim(g_lb_full, chip * (d // N_CHIPS), d // N_CHIPS, axis=1)

    small_w = [norm_pre, norm_post, gla_b_gate, gla_o_gain, sgu_w_spatial, sgu_b_spatial, gla_w_gate2, sgu_ln_gain,
               sgu_ln_bias]
    small_g = [g_pre, g_post, g_bg, g_og, g_wsp, g_bsp, g_w2, g_lg, g_lb]
    small_m = [m_norm_pre, m_norm_post, m_gla_b_gate, m_gla_o_gain, m_sgu_w_spatial, m_sgu_b_spatial, m_gla_w_gate2,
               m_sgu_ln_gain, m_sgu_ln_bias]
    small_v = [v_norm_pre, v_norm_post, v_gla_b_gate, v_gla_o_gain, v_sgu_w_spatial, v_sgu_b_spatial, v_gla_w_gate2,
               v_sgu_ln_gain, v_sgu_ln_bias]
    own_shapes = [w.shape for w in small_w]
    _, s_dl, s_m, s_v = _adamw(_pack(small_w), _pack(small_g), _pack(small_m), _pack(small_v), name="adamw_small")
    dl_s, m_s, v_s = _unpack(s_dl, own_shapes), _unpack(s_m, own_shapes), _unpack(s_v, own_shapes)

    def ordered(small, kind):
        pre, post, bg, og, wsp, bsp, w2, lg, lb = small
        return [pre, post, u_wi_gla[kind], w2, bg, og, u_wo_gla[kind], u_wi_sgu[kind], lg, lb, wsp, bsp, u_wo_sgu[kind]]

    return (loss, grad_x[None], *ordered(small_g, 0), *ordered(dl_s, 1), *ordered(m_s, 2), *ordered(v_s, 3))
```

```python
import math

import jax
import jax.numpy as jnp
from jax import lax
from jax.experimental import pallas as pl
from jax.experimental.pallas import tpu as pltpu

F32 = jnp.float32
BF16 = jnp.bfloat16
MESH = pl.DeviceIdType.MESH

EPS = 1e-6
CHUNK = 64
GLA_HEADS = 4
GLA_GATE_RANK = 16
GLA_TAU = 16.0
SGU_BLOCK = 128
SGU_GROUPS = 8
N_CHIPS = 4
N_DEV = 8
LANES = 128

ADAM_LR = 0.001
ADAM_B1 = 0.9
ADAM_B2 = 0.999
ADAM_EPS = 1e-08
ADAM_WD = 0.01
ADAM_STEP = 10

VMEM_LIMIT = 56 * 1024 * 1024


def _cparams(sem=None):
    return pltpu.CompilerParams(dimension_semantics=sem, vmem_limit_bytes=VMEM_LIMIT)


def _pick(n, cap, unit=LANES):
    best = None
    for t in range(unit, min(n, cap) + 1, unit):
        if n % t == 0:
            best = t
    assert best is not None, (n, cap, unit)
    return best


def _dot(a, b, dims):
    return lax.dot_general(a, b, (dims, ((), ())), preferred_element_type=F32)


def _dot_nn(a, b):
    return _dot(a, b, ((1,), (0,)))


def _dot_nt(a, b):
    return _dot(a, b, ((1,), (1,)))


def _dot_tn(a, b):
    return _dot(a, b, ((0,), (0,)))


def _matmul(a, b, *, mode, out_dtype, name, tm=1024, tn=512, b_shards=False, after=None):
    M, K = a.shape
    if b_shards:
        ns, Kb, bc = b.shape
        N, tn = ns * bc, _pick(bc, tn)
        per = bc // tn
        b_spec = pl.BlockSpec((None, K, tn), lambda i, j: (j // per, 0, j % per))
    elif mode == "nt":
        N, Kb = b.shape
        tn = _pick(N, tn)
        b_spec = pl.BlockSpec((tn, K), lambda i, j: (j, 0))
    else:
        Kb, N = b.shape
        tn = _pick(N, tn)
        b_spec = pl.BlockSpec((K, tn), lambda i, j: (0, j))
    assert K == Kb and a.dtype == b.dtype == BF16, (a.shape, b.shape, mode)
    tm = _pick(M, tm)
    dims = ((1,), (1,)) if mode == "nt" else ((1,), (0,))
    extra_specs, extra_args = ([], []) if after is None else ([pl.BlockSpec(memory_space=pl.ANY)], [after])

    def body(a_ref, b_ref, *rest):
        rest[-1][...] = _dot(a_ref[...], b_ref[...], dims).astype(out_dtype)

    return pl.pallas_call(
        body, name=name, grid=(M // tm, N // tn),
        in_specs=[pl.BlockSpec((tm, K), lambda i, j: (i, 0)), b_spec] + extra_specs,
        out_specs=pl.BlockSpec((tm, tn), lambda i, j: (i, j)), out_shape=jax.ShapeDtypeStruct((M, N), out_dtype),
        compiler_params=_cparams(("parallel", "parallel")),
    )(a, b, *extra_args)


def _matmul_nt_shards(a, b, *, out_dtype, name, tm=1024, tn=512, after=None):
    M, K = a.shape
    ns, N, kc = b.shape
    assert K == ns * kc
    tm, tn = _pick(M, tm), _pick(N, tn)

    def body(a_ref, *rest):
        b_refs, o_ref = rest[:ns], rest[ns + (after is not None)]
        acc = _dot_nt(a_ref[:, 0:kc], b_refs[0][...])
        for j in range(1, ns):
            acc += _dot_nt(a_ref[:, j * kc:(j + 1) * kc], b_refs[j][...])
        o_ref[...] = acc.astype(out_dtype)

    def shard(j):
        return pl.BlockSpec((None, tn, kc), lambda i, n: (j, n, 0))

    extra_specs, extra_args = ([], []) if after is None else ([pl.BlockSpec(memory_space=pl.ANY)], [after])
    return pl.pallas_call(
        body, name=name, grid=(M // tm, N // tn),
        in_specs=[pl.BlockSpec((tm, K), lambda i, n: (i, 0))] + [shard(j) for j in range(ns)] + extra_specs,
        out_specs=pl.BlockSpec((tm, tn), lambda i, n: (i, n)), out_shape=jax.ShapeDtypeStruct((M, N), out_dtype),
        compiler_params=_cparams(("parallel", "parallel")),
    )(a, *([b] * ns), *extra_args)


def _rstd(x):
    return lax.rsqrt(jnp.mean(x * x, axis=-1, keepdims=True) + EPS)


def _row_spec(tr, d):
    return pl.BlockSpec((tr, d), lambda i: (i, 0))


def _vec_spec(d):
    return pl.BlockSpec((1, d), lambda i: (0, 0))


def _acc_rows(ref, i, val, cols=slice(None)):
    @pl.when(i == 0)
    def _():
        ref[:, cols] = val

    @pl.when(i > 0)
    def _():
        ref[:, cols] += val


def _norm_pre(x, gain, *, name, tr=256):
    t, d = x.shape
    tr = _pick(t, tr, 8)

    def body(x_ref, g_ref, h_ref):
        xv = x_ref[...]
        h_ref[...] = (xv * _rstd(xv) * g_ref[...]).astype(BF16)

    return pl.pallas_call(
        body, name=name, grid=(t // tr,), in_specs=[_row_spec(tr, d), _vec_spec(d)], out_specs=_row_spec(tr, d),
        out_shape=jax.ShapeDtypeStruct((t, d), BF16), compiler_params=_cparams(("parallel",)),
    )(x, gain)


def _post_then_pre(x, y, post_gain, pre_gain, *, name, tr=256):
    t, d = x.shape
    tr = _pick(t, tr, 8)

    def body(x_ref, y_ref, pg_ref, ng_ref, xn_ref, h_ref):
        yv = y_ref[...]
        xn = x_ref[...] + yv * _rstd(yv) * pg_ref[...]
        xn_ref[...] = xn
        h_ref[...] = (xn * _rstd(xn) * ng_ref[...]).astype(BF16)

    return pl.pallas_call(
        body, name=name, grid=(t // tr,),
        in_specs=[_row_spec(tr, d), _row_spec(tr, d), _vec_spec(d), _vec_spec(d)],
        out_specs=[_row_spec(tr, d), _row_spec(tr, d)],
        out_shape=[jax.ShapeDtypeStruct((t, d), F32), jax.ShapeDtypeStruct((t, d), BF16)],
        compiler_params=_cparams(("parallel",)),
    )(x, y, post_gain, pre_gain)


def _norm_bwd(dy, n, r, gain):
    dn = dy * gain
    return r * (dn - n * jnp.mean(dn * n, axis=-1, keepdims=True))


def _loss_head(x, y, post_gain, target, *, name, tr=256):
    t, d = x.shape
    tr = _pick(t, tr, 8)

    def body(x_ref, y_ref, pg_ref, t_ref, loss_ref, dx_ref, dy_ref, dpg_ref):
        i = pl.program_id(0)
        yv = y_ref[...]
        r = _rstd(yv)
        n = yv * r
        err = x_ref[...] + n * pg_ref[...] - t_ref[...]
        dx = err * (1.0 / d)
        dx_ref[...] = dx
        part = 0.5 * jnp.sum(jnp.mean(err * err, axis=-1, keepdims=True), axis=0, keepdims=True)
        _acc_rows(loss_ref, i, jnp.broadcast_to(part, (1, LANES)))
        _acc_rows(dpg_ref, i, jnp.sum(dx * n, axis=0, keepdims=True))
        dy_ref[...] = _norm_bwd(dx, n, r, pg_ref[...]).astype(BF16)

    return pl.pallas_call(
        body, name=name, grid=(t // tr,),
        in_specs=[_row_spec(tr, d), _row_spec(tr, d), _vec_spec(d), _row_spec(tr, d)],
        out_specs=[_vec_spec(LANES), _row_spec(tr, d), _row_spec(tr, d), _vec_spec(d)],
        out_shape=[jax.ShapeDtypeStruct((1, LANES), F32), jax.ShapeDtypeStruct((t, d), F32),
                   jax.ShapeDtypeStruct((t, d), BF16), jax.ShapeDtypeStruct((1, d), F32)],
        compiler_params=_cparams(("arbitrary",)),
    )(x, y, post_gain, target)


def _mid_bwd(dx_out, dh, x, pre_gain, y_prev, post_gain_prev, *, name, tr=256):
    t, d = x.shape
    tr = _pick(t, tr, 8)

    def body(dxo_ref, dh_ref, x_ref, ng_ref, y_ref, pg_ref, dx_ref, dy_ref, dng_ref, dpg_ref):
        i = pl.program_id(0)
        xv = x_ref[...]
        r = _rstd(xv)
        xh = xv * r
        dhv = dh_ref[...]
        _acc_rows(dng_ref, i, jnp.sum(dhv * xh, axis=0, keepdims=True))
        dx = dxo_ref[...] + _norm_bwd(dhv, xh, r, ng_ref[...])
        dx_ref[...] = dx
        yv = y_ref[...]
        ry = _rstd(yv)
        n = yv * ry
        _acc_rows(dpg_ref, i, jnp.sum(dx * n, axis=0, keepdims=True))
        dy_ref[...] = _norm_bwd(dx, n, ry, pg_ref[...]).astype(BF16)

    return pl.pallas_call(
        body, name=name, grid=(t // tr,),
        in_specs=[_row_spec(tr, d), _row_spec(tr, d), _row_spec(tr, d), _vec_spec(d), _row_spec(tr, d), _vec_spec(d)],
        out_specs=[_row_spec(tr, d), _row_spec(tr, d), _vec_spec(d), _vec_spec(d)],
        out_shape=[jax.ShapeDtypeStruct((t, d), F32), jax.ShapeDtypeStruct((t, d), BF16),
                   jax.ShapeDtypeStruct((1, d), F32), jax.ShapeDtypeStruct((1, d), F32)],
        compiler_params=_cparams(("arbitrary",)),
    )(dx_out, dh, x, pre_gain, y_prev, post_gain_prev)


def _first_bwd(dx_out, dh, x, pre_gain, *, name, tr=256):
    t, d = x.shape
    tr = _pick(t, tr, 8)

    def body(dxo_ref, dh_ref, x_ref, ng_ref, dx_ref, dng_ref):
        i = pl.program_id(0)
        xv = x_ref[...]
        r = _rstd(xv)
        xh = xv * r
        dhv = dh_ref[...]
        _acc_rows(dng_ref, i, jnp.sum(dhv * xh, axis=0, keepdims=True))
        dx_ref[...] = dxo_ref[...] + _norm_bwd(dhv, xh, r, ng_ref[...])

    return pl.pallas_call(
        body, name=name, grid=(t // tr,),
        in_specs=[_row_spec(tr, d), _row_spec(tr, d), _row_spec(tr, d), _vec_spec(d)],
        out_specs=[_row_spec(tr, d), _vec_spec(d)],
        out_shape=[jax.ShapeDtypeStruct((t, d), F32), jax.ShapeDtypeStruct((1, d), F32)],
        compiler_params=_cparams(("arbitrary",)),
    )(dx_out, dh, x, pre_gain)


def _sigmoid(x):
    return 1.0 / (1.0 + jnp.exp(-x))


def _log_sigmoid(x):
    return jnp.minimum(x, 0.0) - jnp.log(1.0 + jnp.exp(-jnp.abs(x)))


_GELU_C = math.sqrt(2.0 / math.pi)


_GELU_A = 0.044715


def _gelu_parts(x, with_grad=True):
    x2 = x * x
    h = 0.5 * jnp.tanh(x * (_GELU_C + (_GELU_C * _GELU_A) * x2)) + 0.5
    val = x * h
    if not with_grad:
        return val, None
    return val, h * (1.0 + (1.0 - h) * (x * (2.0 * _GELU_C + (6.0 * _GELU_C * _GELU_A) * x2)))


def _split3(x):
    hi = x.astype(BF16)
    r1 = x - hi.astype(F32)
    mid = r1.astype(BF16)
    lo = (r1 - mid.astype(F32)).astype(BF16)
    return hi, mid, lo


def _tri_matmul(tri_bf16, x):
    hi, mid, lo = _split3(x)
    return _dot_nn(tri_bf16, hi) + _dot_nn(tri_bf16, mid) + _dot_nn(tri_bf16, lo)


def _gla_dims(d):
    dk, dv = d // 2, d
    return dk, dv, dk // GLA_HEADS, dv // GLA_HEADS


def _col_pieces(a, b, lay):
    ws, wp = lay
    out = []
    while a < b:
        j = a // ws
        end = min(b, (j + 1) * ws)
        out.append((j * wp + a - j * ws, end - a))
        a = end
    return out


def _load_cols(ref, a, b, lay):
    parts = [ref[:, s:s + n] for s, n in _col_pieces(a, b, lay)]
    return parts[0] if len(parts) == 1 else jnp.concatenate(parts, axis=1)


def _store_cols(ref, a, val, lay):
    off = 0
    for s, n in _col_pieces(a, a + val.shape[1], lay):
        ref[:, s:s + n] = val[:, off:off + n]
        off += n


def _gate_window(c_r, lay):
    (start, _), = _col_pieces(c_r, c_r + GLA_GATE_RANK, lay)
    assert (start % lay[1]) + LANES <= lay[1]
    return slice(start, start + LANES)


def _gla_gates(glr, k, w2_ref, b_ref):
    z = _dot_nn(glr.astype(BF16), w2_ref[...].astype(BF16)) + b_ref[...]
    la = _log_sigmoid(z) * (1.0 / GLA_TAU)
    row = lax.broadcasted_iota(jnp.int32, (CHUNK, CHUNK), 0)
    col = lax.broadcasted_iota(jnp.int32, (CHUNK, CHUNK), 1)
    incl = (row >= col).astype(BF16)
    bcum = _tri_matmul(incl, la)
    b_end = bcum[CHUNK - 1:CHUNK, :]
    e_rest = jnp.exp(b_end - bcum)
    return z, e_rest, k * e_rest, jnp.exp(b_end)


def _gla_fwd(proj, w2p, b_gate, o_gain, lay, *, name):
    t, wcols = proj.shape
    d = o_gain.shape[1]
    dk, dv, dkh, dvh = _gla_dims(d)
    nc = t // CHUNK
    c_k, c_v, c_g, c_r = dk, 2 * dk, 2 * dk + dv, 2 * dk + 2 * dv
    scale = dkh ** -0.5

    def body(p_ref, w2_ref, b_ref, og_ref, o_ref, a_ref, sb_ref, sfin_ref, s_ref):
        i = pl.program_id(0)

        @pl.when(i == 0)
        def _():
            s_ref[...] = jnp.zeros_like(s_ref)

        q = _load_cols(p_ref, 0, dk, lay) * scale
        k = _load_cols(p_ref, c_k, c_k + dk, lay)
        glr = p_ref[:, _gate_window(c_r, lay)]
        _, _, kdec, decay = _gla_gates(glr, k, w2_ref, b_ref)
        for h in range(GLA_HEADS):
            ks = slice(h * dkh, (h + 1) * dkh)
            vs = slice(h * dvh, (h + 1) * dvh)
            v_h = _load_cols(p_ref, c_v + h * dvh, c_v + (h + 1) * dvh, lay)
            g_h = _load_cols(p_ref, c_g + h * dvh, c_g + (h + 1) * dvh, lay)
            s_old = s_ref[h]
            sb_ref[0, h] = s_old
            s_new = s_old * decay[:, ks] + _dot_tn(v_h.astype(BF16), kdec[:, ks].astype(BF16))
            s_ref[h] = s_new
            o_h = _dot_nt(q[:, ks].astype(BF16), s_new.astype(BF16))
            o_ref[:, vs] = o_h
            on = o_h * _rstd(o_h)
            a_ref[:, vs] = (on * og_ref[:, vs] * (g_h * _sigmoid(g_h))).astype(BF16)

        @pl.when(i == nc - 1)
        def _():
            sfin_ref[...] = s_ref[...]

    full = lambda *shape: pl.BlockSpec(shape, lambda i: (0,) * len(shape))
    return pl.pallas_call(
        body, name=name, grid=(nc,),
        in_specs=[pl.BlockSpec((CHUNK, wcols), lambda i: (i, 0)), full(LANES, dk), full(1, dk), full(1, dv)],
        out_specs=[pl.BlockSpec((CHUNK, dv), lambda i: (i, 0)), pl.BlockSpec((CHUNK, dv), lambda i: (i, 0)),
                   pl.BlockSpec((1, GLA_HEADS, dvh, dkh), lambda i: (i, 0, 0, 0)), full(GLA_HEADS, dvh, dkh)],
        out_shape=[jax.ShapeDtypeStruct((t, dv), F32), jax.ShapeDtypeStruct((t, dv), BF16),
                   jax.ShapeDtypeStruct((nc, GLA_HEADS, dvh, dkh), F32),
                   jax.ShapeDtypeStruct((GLA_HEADS, dvh, dkh), F32)],
        scratch_shapes=[pltpu.VMEM((GLA_HEADS, dvh, dkh), F32)],
        compiler_params=_cparams(("arbitrary",)),
    )(proj, w2p, b_gate, o_gain)


def _gla_bwd(da, o, proj, w2p, b_gate, o_gain, s_before, s_final, lay, *, name):
    t, wcols = proj.shape
    d = o_gain.shape[1]
    dk, dv, dkh, dvh = _gla_dims(d)
    nc = t // CHUNK
    c_k, c_v, c_g, c_r = dk, 2 * dk, 2 * dk + dv, 2 * dk + 2 * dv
    scale = dkh ** -0.5

    def body(da_ref, o_ref, p_ref, w2_ref, b_ref, og_ref, sb_ref, sfin_ref,
             dp_ref, dog_ref, db_ref, dw2_ref, s_ref, gc_ref, dkd_ref):
        i = pl.program_id(0)

        @pl.when(i == 0)
        def _():
            s_ref[...] = sfin_ref[...]
            gc_ref[...] = jnp.zeros_like(gc_ref)

        ws, wp = lay
        for j in range(N_CHIPS):
            dp_ref[:, j * wp + ws:(j + 1) * wp] = jnp.zeros((CHUNK, wp - ws), BF16)
        q = _load_cols(p_ref, 0, dk, lay) * scale
        k = _load_cols(p_ref, c_k, c_k + dk, lay)
        glr = p_ref[:, _gate_window(c_r, lay)]
        z, e_rest, kdec, decay = _gla_gates(glr, k, w2_ref, b_ref)
        ddecay = []
        for h in range(GLA_HEADS):
            ks = slice(h * dkh, (h + 1) * dkh)
            vs = slice(h * dvh, (h + 1) * dvh)
            v_h = _load_cols(p_ref, c_v + h * dvh, c_v + (h + 1) * dvh, lay)
            g_h = _load_cols(p_ref, c_g + h * dvh, c_g + (h + 1) * dvh, lay)
            da_h = da_ref[:, vs]
            o_h = o_ref[:, vs]
            og_h = og_ref[:, vs]
            r = _rstd(o_h)
            on = o_h * r
            sg = _sigmoid(g_h)
            silu = g_h * sg
            _acc_rows(dog_ref, i, jnp.sum(da_h * silu * on, axis=0, keepdims=True), vs)
            _store_cols(dp_ref, c_g + h * dvh, (da_h * (on * og_h) * (sg * (1.0 + g_h * (1.0 - sg)))).astype(BF16),
                        lay)
            don = da_h * silu * og_h
            do_h = (r * (don - on * jnp.mean(don * on, axis=-1, keepdims=True))).astype(BF16)
            s_cur = s_ref[h]
            _store_cols(dp_ref, h * dkh, (_dot_nn(do_h, s_cur.astype(BF16)) * scale).astype(BF16), lay)
            g_tot = gc_ref[h] + _dot_tn(do_h, q[:, ks].astype(BF16))
            g_bf = g_tot.astype(BF16)
            dkd_ref[:, ks] = _dot_nn(v_h.astype(BF16), g_bf)
            _store_cols(dp_ref, c_v + h * dvh, _dot_nt(kdec[:, ks].astype(BF16), g_bf).astype(BF16), lay)
            s_prev = sb_ref[0, h]
            ddecay.append(jnp.sum(g_tot * s_prev, axis=0, keepdims=True))
            gc_ref[h] = g_tot * decay[:, ks]
            s_ref[h] = s_prev
        dkdec = dkd_ref[...]
        _store_cols(dp_ref, c_k, (dkdec * e_rest).astype(BF16), lay)
        d_e = dkdec * kdec
        row = lax.broadcasted_iota(jnp.int32, (CHUNK, CHUNK), 0)
        col = lax.broadcasted_iota(jnp.int32, (CHUNK, CHUNK), 1)
        excl = (row > col).astype(BF16)
        dla = jnp.concatenate(ddecay, axis=1) * decay + _tri_matmul(excl, d_e)
        dz = dla * (1.0 / GLA_TAU) * (1.0 - _sigmoid(z))
        _acc_rows(db_ref, i, jnp.sum(dz, axis=0, keepdims=True))
        dz_bf = dz.astype(BF16)
        dw2 = _dot_tn(glr.astype(BF16), dz_bf)

        @pl.when(i == 0)
        def _():
            dw2_ref[...] = dw2

        @pl.when(i > 0)
        def _():
            dw2_ref[...] += dw2

        dp_ref[:, _gate_window(c_r, lay)] = _dot_nt(dz_bf, w2_ref[...].astype(BF16)).astype(BF16)

    rev = lambda i: (nc - 1 - i, 0)
    full = lambda *shape: pl.BlockSpec(shape, lambda i: (0,) * len(shape))
    return pl.pallas_call(
        body, name=name, grid=(nc,),
        in_specs=[pl.BlockSpec((CHUNK, dv), rev), pl.BlockSpec((CHUNK, dv), rev), pl.BlockSpec((CHUNK, wcols), rev),
                  full(LANES, dk), full(1, dk), full(1, dv),
                  pl.BlockSpec((1, GLA_HEADS, dvh, dkh), lambda i: (nc - 1 - i, 0, 0, 0)), full(GLA_HEADS, dvh, dkh)],
        out_specs=[pl.BlockSpec((CHUNK, wcols), rev), full(1, dv), full(1, dk), full(LANES, dk)],
        out_shape=[jax.ShapeDtypeStruct((t, wcols), BF16), jax.ShapeDtypeStruct((1, dv), F32),
                   jax.ShapeDtypeStruct((1, dk), F32), jax.ShapeDtypeStruct((LANES, dk), F32)],
        scratch_shapes=[pltpu.VMEM((GLA_HEADS, dvh, dkh), F32), pltpu.VMEM((GLA_HEADS, dvh, dkh), F32),
                        pltpu.VMEM((CHUNK, dk), F32)],
        compiler_params=_cparams(("arbitrary",)),
    )(da, o, proj, w2p, b_gate, o_gain, s_before, s_final)


def _sgu_mid(p_ref, lg_ref, lb_ref, ws_ref, bst_ref, w, with_grad=True):
    gd = w // SGU_GROUPS
    u_act, du_fac = _gelu_parts(p_ref[:, 0:w], with_grad)
    vf, dv_fac = _gelu_parts(p_ref[:, w:2 * w], with_grad)
    mu = jnp.mean(vf, axis=-1, keepdims=True)
    cen = vf - mu
    rstd = lax.rsqrt(jnp.mean(cen * cen, axis=-1, keepdims=True) + EPS)
    xh = cen * rstd
    vn = (xh * lg_ref[...] + lb_ref[...]).astype(BF16)
    vs = [_dot_nn(ws_ref[g].astype(BF16), vn[:, g * gd:(g + 1) * gd]) + bst_ref[:, g:g + 1]
          for g in range(SGU_GROUPS)]
    return u_act, du_fac, dv_fac, rstd, xh, vn, vs


def _sgu_fwd(proj, ln_gain, ln_bias, ws_masked, bs_t, *, name):
    t, w3 = proj.shape
    w = w3 // 3
    gd = w // SGU_GROUPS
    nb = t // SGU_BLOCK

    def body(p_ref, lg_ref, lb_ref, ws_ref, bst_ref, a_ref):
        u_act, _, _, _, _, _, vs = _sgu_mid(p_ref, lg_ref, lb_ref, ws_ref, bst_ref, w, with_grad=False)
        for g in range(SGU_GROUPS):
            cs = slice(g * gd, (g + 1) * gd)
            gate = p_ref[:, 2 * w + g * gd:2 * w + (g + 1) * gd]
            a_ref[:, cs] = (u_act[:, cs] * vs[g] * (gate * _sigmoid(gate))).astype(BF16)

    full = lambda *shape: pl.BlockSpec(shape, lambda i: (0,) * len(shape))
    return pl.pallas_call(
        body, name=name, grid=(nb,),
        in_specs=[pl.BlockSpec((SGU_BLOCK, w3), lambda i: (i, 0)), full(1, w), full(1, w),
                  full(SGU_GROUPS, SGU_BLOCK, SGU_BLOCK), full(SGU_BLOCK, SGU_GROUPS)],
        out_specs=pl.BlockSpec((SGU_BLOCK, w), lambda i: (i, 0)),
        out_shape=jax.ShapeDtypeStruct((t, w), BF16),
        compiler_params=_cparams(("parallel",)),
    )(proj, ln_gain, ln_bias, ws_masked, bs_t)


def _sgu_bwd(da, proj, ln_gain, ln_bias, ws_masked, ws_masked_t, bs_t, *, name):
    t, w3 = proj.shape
    w = w3 // 3
    gd = w // SGU_GROUPS
    nb = t // SGU_BLOCK

    def body(da_ref, p_ref, lg_ref, lb_ref, ws_ref, wst_ref, bst_ref, dp_ref, dws_ref, dbst_ref, dlg_ref, dlb_ref,
             dvn_ref):
        i = pl.program_id(0)
        u_act, du_fac, dv_fac, rstd, xh, vn, vs = _sgu_mid(p_ref, lg_ref, lb_ref, ws_ref, bst_ref, w)
        for g in range(SGU_GROUPS):
            cs = slice(g * gd, (g + 1) * gd)
            gate = p_ref[:, 2 * w + g * gd:2 * w + (g + 1) * gd]
            sg = _sigmoid(gate)
            silu = gate * sg
            da_g = da_ref[:, cs]
            ua_g = u_act[:, cs]
            dp_ref[:, cs] = (da_g * vs[g] * silu * du_fac[:, cs]).astype(BF16)
            dp_ref[:, 2 * w + g * gd:2 * w + (g + 1) * gd] = (
                da_g * ua_g * vs[g] * (sg * (1.0 + gate * (1.0 - sg)))).astype(BF16)
            dvs = da_g * ua_g * silu
            dvs_bf = dvs.astype(BF16)
            dvn_ref[:, cs] = _dot_nn(wst_ref[g].astype(BF16), dvs_bf)
            dws = _dot_nt(dvs_bf, vn[:, cs])
            dbs = jnp.sum(dvs, axis=1, keepdims=True)

            @pl.when(i == 0)
            def _():
                dws_ref[g] = dws
                dbst_ref[:, g:g + 1] = dbs

            @pl.when(i > 0)
            def _():
                dws_ref[g] += dws
                dbst_ref[:, g:g + 1] += dbs

        dvn = dvn_ref[...]
        _acc_rows(dlg_ref, i, jnp.sum(dvn * xh, axis=0, keepdims=True))
        _acc_rows(dlb_ref, i, jnp.sum(dvn, axis=0, keepdims=True))
        dxh = dvn * lg_ref[...]
        dvf = rstd * (dxh - jnp.mean(dxh, axis=-1, keepdims=True)
                      - xh * jnp.mean(dxh * xh, axis=-1, keepdims=True))
        dp_ref[:, w:2 * w] = (dvf * dv_fac).astype(BF16)

    full = lambda *shape: pl.BlockSpec(shape, lambda i: (0,) * len(shape))
    return pl.pallas_call(
        body, name=name, grid=(nb,),
        in_specs=[pl.BlockSpec((SGU_BLOCK, w), lambda i: (i, 0)), pl.BlockSpec((SGU_BLOCK, w3), lambda i: (i, 0)),
                  full(1, w), full(1, w), full(SGU_GROUPS, SGU_BLOCK, SGU_BLOCK),
                  full(SGU_GROUPS, SGU_BLOCK, SGU_BLOCK), full(SGU_BLOCK, SGU_GROUPS)],
        out_specs=[pl.BlockSpec((SGU_BLOCK, w3), lambda i: (i, 0)), full(SGU_GROUPS, SGU_BLOCK, SGU_BLOCK),
                   full(SGU_BLOCK, SGU_GROUPS), full(1, w), full(1, w)],
        out_shape=[jax.ShapeDtypeStruct((t, w3), BF16), jax.ShapeDtypeStruct((SGU_GROUPS, SGU_BLOCK, SGU_BLOCK), F32),
                   jax.ShapeDtypeStruct((SGU_BLOCK, SGU_GROUPS), F32), jax.ShapeDtypeStruct((1, w), F32),
                   jax.ShapeDtypeStruct((1, w), F32)],
        scratch_shapes=[pltpu.VMEM((SGU_BLOCK, w), F32)],
        compiler_params=_cparams(("arbitrary",)),
    )(da, proj, ln_gain, ln_bias, ws_masked, ws_masked_t, bs_t)


def _tile2d(rows, cols, block_bytes, row_unit):
    if rows % row_unit == 0:
        return _pick(rows, max(row_unit, block_bytes // (4 * cols)), row_unit), cols
    return rows, _pick(cols, max(LANES, block_bytes // (4 * rows)))


def _adamw(w, g, m, v, *, name, block_bytes=1 << 20, after=None):
    rows, cols = w.shape
    tr, tc = _tile2d(rows, cols, block_bytes, 8)
    g_rows = g.shape[0]
    assert g_rows == rows or tr == rows
    extra_specs, extra_args = ([], []) if after is None else ([pl.BlockSpec(memory_space=pl.ANY)], [after])

    def body(w_ref, g_ref, m_ref, v_ref, *rest):
        go_ref, d_ref, mo_ref, vo_ref = rest[len(extra_args):]
        gv = g_ref[0:tr, :]
        go_ref[...] = gv
        mn = ADAM_B1 * m_ref[...] + (1.0 - ADAM_B1) * gv
        vn = ADAM_B2 * v_ref[...] + (1.0 - ADAM_B2) * (gv * gv)
        m_hat = mn / (1.0 - ADAM_B1 ** ADAM_STEP)
        v_hat = vn / (1.0 - ADAM_B2 ** ADAM_STEP)
        d_ref[...] = -ADAM_LR * (m_hat / (jnp.sqrt(v_hat) + ADAM_EPS) + ADAM_WD * w_ref[...])
        mo_ref[...] = mn
        vo_ref[...] = vn

    spec = pl.BlockSpec((tr, tc), lambda i, j: (i, j))
    g_spec = spec if g_rows == rows else pl.BlockSpec((g_rows, tc), lambda i, j: (0, j))
    return pl.pallas_call(
        body, name=name, grid=(rows // tr, cols // tc), in_specs=[spec, g_spec, spec, spec] + extra_specs,
        out_specs=[spec] * 4, out_shape=[jax.ShapeDtypeStruct((rows, cols), F32)] * 4,
        compiler_params=_cparams(("parallel", "parallel")),
    )(w, g, m, v, *extra_args)


def _matmul_dw_pair(a_me, a_sib, b_me, b_sib, core_idx, *, shards_on, name, after=None, part=(0, 1)):
    T, M = a_me.shape
    N = b_me.shape[1]
    if shards_on == "rows":
        p, count = part
        tm, hc = M // N_CHIPS, N // 2
        hp = hc // count
        tn = _pick(hp, 512)
        per = hp // tn
        grid = (N_CHIPS, per)
        a_spec = pl.BlockSpec((T, tm), lambda i, n, h: (0, i))
        b_me_spec = pl.BlockSpec((T, tn), lambda i, n, h: (0, (h[0] * count + p) * per + n))
        b_sib_spec = pl.BlockSpec((T, tn), lambda i, n, h: (0, p * per + n))
        out_spec = pl.BlockSpec((None, tm, tn), lambda i, n, h: (i, 0, n))
        out_shape = jax.ShapeDtypeStruct((N_CHIPS, tm, hp), BF16)
    else:
        tm, hc = _pick(M, 1024), N // N_CHIPS // 2
        grid = (M // tm, N_CHIPS)
        a_spec = pl.BlockSpec((T, tm), lambda i, j, h: (0, i))
        b_me_spec = pl.BlockSpec((T, hc), lambda i, j, h: (0, 2 * j + h[0]))
        b_sib_spec = pl.BlockSpec((T, hc), lambda i, j, h: (0, j))
        out_spec = pl.BlockSpec((None, tm, hc), lambda i, j, h: (j, i, 0))
        out_shape = jax.ShapeDtypeStruct((N_CHIPS, M, hc), BF16)
    extra_specs, extra_args = ([], []) if after is None else ([pl.BlockSpec(memory_space=pl.ANY)], [after])

    def body(h_ref, am_ref, as_ref, bm_ref, bs_ref, *rest):
        o_ref = rest[len(extra_args)]
        o_ref[...] = (_dot_tn(am_ref[...], bm_ref[...]) + _dot_tn(as_ref[...], bs_ref[...])).astype(BF16)

    grid_spec = pltpu.PrefetchScalarGridSpec(
        num_scalar_prefetch=1, grid=grid, in_specs=[a_spec, a_spec, b_me_spec, b_sib_spec] + extra_specs,
        out_specs=out_spec)
    return pl.pallas_call(
        body, name=name, grid_spec=grid_spec, out_shape=out_shape, compiler_params=_cparams(("parallel", "parallel")),
    )(core_idx, a_me, a_sib, b_me, b_sib, *extra_args)


def _chip_sum(pair, landed, slots, *, name, block_bytes=1 << 20, part=(0, 1), into=None):
    p, count = part
    _, r, hp = pair.shape
    tr, tc = _tile2d(r, hp, block_bytes, 16)
    ncb = hp // tc
    extra_specs, extra_args = ([], []) if into is None else ([pl.BlockSpec(memory_space=pl.ANY)], [into])

    def body(s_ref, own_ref, l0_ref, l1_ref, l2_ref, *rest):
        rest[-1][...] = ((own_ref[...].astype(F32) + l0_ref[...].astype(F32)) + l1_ref[...].astype(F32)
                         ) + l2_ref[...].astype(F32)

    def slab(which):
        return pl.BlockSpec((None, tr, tc), lambda i, k, s: (s[which], i, k))

    grid_spec = pltpu.PrefetchScalarGridSpec(
        num_scalar_prefetch=1, grid=(r // tr, ncb),
        in_specs=[slab(0), slab(1), slab(2), slab(3)] + extra_specs,
        out_specs=pl.BlockSpec((tr, tc), lambda i, k, s: (i, (s[4] * count + p) * ncb + k)))
    return pl.pallas_call(
        body, name=name, grid_spec=grid_spec, out_shape=jax.ShapeDtypeStruct((r, 2 * hp * count), F32),
        input_output_aliases={} if into is None else {5: 0},
        compiler_params=_cparams(("parallel", "parallel")),
    )(slots, pair, landed, landed, landed, *extra_args)


def _stack_sum(x, *, name, out_dtype=F32, block_bytes=1 << 20):
    s, r, c = x.shape
    tr = _pick(r, max(8, block_bytes // (4 * c)), 16) if r % 16 == 0 else r

    def body(x_ref, o_ref):
        acc = x_ref[0].astype(F32)
        for j in range(1, s):
            acc = acc + x_ref[j].astype(F32)
        o_ref[...] = acc.astype(out_dtype)

    return pl.pallas_call(
        body, name=name, grid=(r // tr,),
        in_specs=[pl.BlockSpec((s, tr, c), lambda i: (0, i, 0))], out_specs=pl.BlockSpec((tr, c), lambda i: (i, 0)),
        out_shape=jax.ShapeDtypeStruct((r, c), out_dtype), compiler_params=_cparams(("parallel",)),
    )(x)


HBM = pl.BlockSpec(memory_space=pltpu.HBM)


def _place():
    x, y, c = lax.axis_index("x"), lax.axis_index("y"), lax.axis_index("c")
    other_chips = [(1 - x, y), (x, 1 - y), (1 - x, 1 - y)]
    return x, y, c, other_chips


def _handshake(peers):
    barrier = pltpu.get_barrier_semaphore()
    for peer in peers:
        pl.semaphore_signal(barrier, inc=1, device_id=peer, device_id_type=MESH)
    pl.semaphore_wait(barrier, len(peers))


def _sibling():
    x, y, c, _ = _place()
    return [(x, y, 1 - c)]


def _same_core_chips():
    x, y, c, chips = _place()
    return [(cx, cy, c) for cx, cy in chips]


def _same_core_neighbours():
    x, y, c, _ = _place()
    return [(1 - x, y, c), (x, 1 - y, c)]


def _split_params(cid):
    return pltpu.CompilerParams(has_side_effects=SIDE_EFFECT, collective_id=cid)


def _half_cols(cols, which):
    hc = cols // 2
    return pl.ds(pl.multiple_of(which * hc, LANES), hc)


SEM = pl.BlockSpec(memory_space=pltpu.SEMAPHORE)
ANY = pl.BlockSpec(memory_space=pl.ANY)
SIDE_EFFECT = pltpu.SideEffectType.DATAFLOW_SIDE_EFFECTING
TOKEN_SHAPE = (8, LANES)


def _hbm(shape, dtype):
    return pltpu.HBM(shape, dtype)


def _in_hbm(a):
    return pltpu.with_memory_space_constraint(a, pltpu.HBM)


def _gather_copy(src_ref, land_ref, ssem, rsem, k, chip_of_block, to, c):
    cols = src_ref.shape[1]
    return pltpu.make_async_remote_copy(
        src_ref=src_ref.at[:, _half_cols(cols, c)], dst_ref=land_ref.at[chip_of_block, :, _half_cols(cols, c)],
        send_sem=ssem.at[k], recv_sem=rsem.at[k], device_id=to, device_id_type=MESH)


NEIGHBOURS = (0, 1)
ALL_CHIPS = (0, 1, 2)


def _gather_start(shards, *, name, cid, after=(), relayed=(), own_slab=None):
    n = len(shards)
    after = list(after)

    def body(*refs):
        srcs, lands = refs[:n], refs[n:2 * n]
        outs = refs[2 * n + len(after):]
        token = outs[-1]
        _handshake(_same_core_chips())
        x, y, c, chips = _place()
        me = 2 * x + y
        for a in range(n):
            ssem, rsem = outs[4 * a], outs[4 * a + 1]
            for k in NEIGHBOURS if a in relayed else ALL_CHIPS:
                cx, cy = chips[k]
                _gather_copy(srcs[a], lands[a], ssem, rsem, k, me, (cx, cy, c), c).start()
        token[...] = jnp.zeros_like(token)

    out_shape, out_specs, aliases = [], [], {}
    for a, s in enumerate(shards):
        out_shape += [pltpu.SemaphoreType.DMA((3,)), pltpu.SemaphoreType.DMA((3,)), _hbm(s.shape, s.dtype),
                      _hbm((N_CHIPS,) + s.shape, s.dtype)]
        out_specs += [SEM, SEM, HBM, HBM]
        aliases[a] = 4 * a + 2
        aliases[n + a] = 4 * a + 3
    out_shape.append(jax.ShapeDtypeStruct(TOKEN_SHAPE, F32))
    out_specs.append(pl.BlockSpec(memory_space=pltpu.VMEM))
    lands = [lax.empty((N_CHIPS,) + s.shape, s.dtype) for s in shards]
    if own_slab is not None:
        lands = [lax.dynamic_update_slice(land, s[None], (own_slab, 0, 0)) for land, s in zip(lands, shards)]
    lands = [_in_hbm(land) for land in lands]
    res = pl.pallas_call(
        body, name=name, in_specs=[HBM] * (2 * n) + [ANY] * len(after), out_specs=out_specs, out_shape=out_shape,
        input_output_aliases=aliases, compiler_params=_split_params(cid),
    )(*[_in_hbm(s) for s in shards], *lands, *after)
    return [tuple(res[4 * a:4 * a + 4]) for a in range(n)], res[-1]


def _wait_call(wait_fn, parts, after, *, name):
    ssem, rsem, src, land = parts
    after = list(after) if isinstance(after, (list, tuple)) else [after]

    def body(src_ref, land_ref, ssem_ref, rsem_ref, *rest):
        wait_fn(src_ref, land_ref, ssem_ref, rsem_ref)

    return pl.pallas_call(
        body, name=name, in_specs=[HBM, HBM, SEM, SEM] + [ANY] * len(after), out_specs=[HBM, HBM],
        out_shape=[_hbm(src.shape, src.dtype), _hbm(land.shape, land.dtype)], input_output_aliases={0: 0, 1: 1},
        compiler_params=pltpu.CompilerParams(has_side_effects=SIDE_EFFECT),
    )(src, land, ssem, rsem, *after)


def _gather_wait(parts, after, *, name, ks=ALL_CHIPS):
    def wait(src_ref, land_ref, ssem_ref, rsem_ref):
        x, y, c, chips = _place()
        for k in ks:
            cx, cy = chips[k]
            cp = _gather_copy(src_ref, land_ref, ssem_ref, rsem_ref, k, 2 * cx + cy, (x, y, c), c)
            cp.wait_send()
            cp.wait_recv()

    return _wait_call(wait, parts, after, name=name)


def _relay_copy(buf_ref, ssem, rsem, k, slab, to, c):
    hr = buf_ref.shape[1] // 2
    part = buf_ref.at[slab, pl.ds(k * hr, hr), _half_cols(buf_ref.shape[2], c)]
    return pltpu.make_async_remote_copy(
        src_ref=part, dst_ref=part, send_sem=ssem.at[k], recv_sem=rsem.at[k], device_id=to, device_id_type=MESH)


def _relay_start(land, *, name, cid):
    def body(buf_ref, ssem, rsem, buf_out, token):
        _handshake(_same_core_neighbours())
        x, y, c, _ = _place()
        _relay_copy(buf_ref, ssem, rsem, 0, 2 * (1 - x) + y, (x, 1 - y, c), c).start()
        _relay_copy(buf_ref, ssem, rsem, 1, 2 * x + 1 - y, (1 - x, y, c), c).start()
        token[...] = jnp.zeros_like(token)

    res = pl.pallas_call(
        body, name=name, in_specs=[HBM], out_specs=[SEM, SEM, HBM, pl.BlockSpec(memory_space=pltpu.VMEM)],
        out_shape=[pltpu.SemaphoreType.DMA((2,)), pltpu.SemaphoreType.DMA((2,)), _hbm(land.shape, land.dtype),
                   jax.ShapeDtypeStruct(TOKEN_SHAPE, F32)],
        input_output_aliases={0: 2}, compiler_params=_split_params(cid),
    )(land)
    return tuple(res[:3]), res[3]


def _relay_wait(parts, after, *, name):
    ssem, rsem, buf = parts
    after = list(after) if isinstance(after, (list, tuple)) else [after]

    def body(buf_ref, ssem_ref, rsem_ref, *rest):
        x, y, c, _ = _place()
        diagonal = 2 * (1 - x) + 1 - y
        _relay_copy(buf_ref, ssem_ref, rsem_ref, 0, 2 * (1 - x) + y, (x, y, c), c).wait_send()
        _relay_copy(buf_ref, ssem_ref, rsem_ref, 1, 2 * x + 1 - y, (x, y, c), c).wait_send()
        _relay_copy(buf_ref, ssem_ref, rsem_ref, 0, diagonal, (x, y, c), c).wait_recv()
        _relay_copy(buf_ref, ssem_ref, rsem_ref, 1, diagonal, (x, y, c), c).wait_recv()

    return pl.pallas_call(
        body, name=name, in_specs=[HBM, SEM, SEM] + [ANY] * len(after), out_specs=HBM,
        out_shape=_hbm(buf.shape, buf.dtype), input_output_aliases={0: 0},
        compiler_params=pltpu.CompilerParams(has_side_effects=SIDE_EFFECT),
    )(buf, ssem, rsem, *after)


def _forward_copy(buf_ref, ssem, rsem, k, slab, which, to):
    part = buf_ref.at[slab, :, _half_cols(buf_ref.shape[2], which)]
    return pltpu.make_async_remote_copy(
        src_ref=part, dst_ref=part, send_sem=ssem.at[k], recv_sem=rsem.at[k], device_id=to, device_id_type=MESH)


def _sibling_forward(land, *, name, cid, ks=ALL_CHIPS):
    def body(_, buf, send_sems, recv_sems):
        _handshake(_sibling())
        x, y, c, chips = _place()
        copies = []
        for k in ks:
            cx, cy = chips[k]
            cp = _forward_copy(buf, send_sems, recv_sems, k, 2 * cx + cy, c, (x, y, 1 - c))
            cp.start()
            copies.append(cp)
        for k in ks:
            cx, cy = chips[k]
            _forward_copy(buf, send_sems, recv_sems, k, 2 * cx + cy, 1 - c, (x, y, c)).wait_recv()
        for cp in copies:
            cp.wait_send()

    return pl.pallas_call(
        body, name=name, in_specs=[HBM], out_specs=HBM, out_shape=jax.ShapeDtypeStruct(land.shape, land.dtype),
        input_output_aliases={0: 0},
        scratch_shapes=[pltpu.SemaphoreType.DMA((3,)), pltpu.SemaphoreType.DMA((3,))],
        compiler_params=pltpu.CompilerParams(collective_id=cid),
    )(land)


def _forward_start(land, *, name, cid, ks=ALL_CHIPS):
    def body(buf_ref, ssem, rsem, buf_out, token):
        _handshake(_sibling())
        x, y, c, chips = _place()
        for k in ks:
            cx, cy = chips[k]
            _forward_copy(buf_ref, ssem, rsem, k, 2 * cx + cy, c, (x, y, 1 - c)).start()
        token[...] = jnp.zeros_like(token)

    res = pl.pallas_call(
        body, name=name, in_specs=[HBM], out_specs=[SEM, SEM, HBM, pl.BlockSpec(memory_space=pltpu.VMEM)],
        out_shape=[pltpu.SemaphoreType.DMA((3,)), pltpu.SemaphoreType.DMA((3,)), _hbm(land.shape, land.dtype),
                   jax.ShapeDtypeStruct(TOKEN_SHAPE, F32)],
        input_output_aliases={0: 2}, compiler_params=_split_params(cid),
    )(land)
    return tuple(res[:3]), res[3]


def _forward_wait(parts, after, *, name, ks=ALL_CHIPS):
    ssem, rsem, buf = parts
    after = list(after) if isinstance(after, (list, tuple)) else [after]

    def body(buf_ref, ssem_ref, rsem_ref, *rest):
        x, y, c, chips = _place()
        for k in ks:
            cx, cy = chips[k]
            _forward_copy(buf_ref, ssem_ref, rsem_ref, k, 2 * cx + cy, c, (x, y, c)).wait_send()
            _forward_copy(buf_ref, ssem_ref, rsem_ref, k, 2 * cx + cy, 1 - c, (x, y, c)).wait_recv()

    return pl.pallas_call(
        body, name=name, in_specs=[HBM, SEM, SEM] + [ANY] * len(after), out_specs=HBM,
        out_shape=_hbm(buf.shape, buf.dtype), input_output_aliases={0: 0},
        compiler_params=pltpu.CompilerParams(has_side_effects=SIDE_EFFECT),
    )(buf, ssem, rsem, *after)


def _share_copy(buf_ref, ssem, rsem, a, which, to):
    part = buf_ref.at[:, _half_cols(buf_ref.shape[1], which)]
    return pltpu.make_async_remote_copy(
        src_ref=part, dst_ref=part, send_sem=ssem.at[a], recv_sem=rsem.at[a], device_id=to, device_id_type=MESH)


def _share_start(arrays, *, name, cid):
    n = len(arrays)

    def body(*refs):
        bufs, ssem, rsem, token = refs[:n], refs[n], refs[n + 1], refs[-1]
        _handshake(_sibling())
        x, y, c, _ = _place()
        for a in range(n):
            _share_copy(bufs[a], ssem, rsem, a, c, (x, y, 1 - c)).start()
        token[...] = jnp.zeros_like(token)

    res = pl.pallas_call(
        body, name=name, in_specs=[HBM] * n,
        out_specs=[SEM, SEM] + [HBM] * n + [pl.BlockSpec(memory_space=pltpu.VMEM)],
        out_shape=[pltpu.SemaphoreType.DMA((n,)), pltpu.SemaphoreType.DMA((n,))]
        + [_hbm(b.shape, b.dtype) for b in arrays] + [jax.ShapeDtypeStruct(TOKEN_SHAPE, F32)],
        input_output_aliases={a: 2 + a for a in range(n)}, compiler_params=_split_params(cid),
    )(*[_in_hbm(b) for b in arrays])
    return (res[0], res[1], list(res[2:2 + n])), res[-1]


def _share_wait(parts, after, *, name):
    ssem, rsem, bufs = parts
    n = len(bufs)
    after = list(after) if isinstance(after, (list, tuple)) else [after]

    def body(*refs):
        buf_refs, ssem_ref, rsem_ref = refs[:n], refs[n], refs[n + 1]
        x, y, c, _ = _place()
        for a in range(n):
            _share_copy(buf_refs[a], ssem_ref, rsem_ref, a, c, (x, y, c)).wait_send()
            _share_copy(buf_refs[a], ssem_ref, rsem_ref, a, 1 - c, (x, y, c)).wait_recv()

    return pl.pallas_call(
        body, name=name, in_specs=[HBM] * n + [SEM, SEM] + [ANY] * len(after), out_specs=[HBM] * n,
        out_shape=[_hbm(b.shape, b.dtype) for b in bufs], input_output_aliases={a: a for a in range(n)},
        compiler_params=pltpu.CompilerParams(has_side_effects=SIDE_EFFECT),
    )(*bufs, ssem, rsem, *after)


def _scatter_copy(src_ref, land_ref, ssem, rsem, k, src_slab, dst_slab, to):
    return pltpu.make_async_remote_copy(
        src_ref=src_ref.at[src_slab], dst_ref=land_ref.at[dst_slab], send_sem=ssem.at[k], recv_sem=rsem.at[k],
        device_id=to, device_id_type=MESH)


def _scatter_start(part, *, name, cid):
    def start(src_ref, land_ref, ssem, rsem):
        x, y, c, chips = _place()
        me = 2 * x + y
        for k, (cx, cy) in enumerate(chips):
            _scatter_copy(src_ref, land_ref, ssem, rsem, k, 2 * cx + cy, me, (cx, cy, c)).start()

    return _split_start(start, _same_core_chips, part, part.shape, N_CHIPS - 1, name=name, cid=cid)


def _scatter_wait(parts, after, *, name):
    def wait(src_ref, land_ref, ssem_ref, rsem_ref):
        x, y, c, chips = _place()
        for k, (cx, cy) in enumerate(chips):
            idx = 2 * cx + cy
            cp = _scatter_copy(src_ref, land_ref, ssem_ref, rsem_ref, k, idx, idx, (x, y, c))
            cp.wait_send()
            cp.wait_recv()

    return _wait_call(wait, parts, after, name=name)


def _split_start(start_fn, peers_fn, src, land_shape, n_sems, *, name, cid):
    def body(src_ref, land_ref, ssem, rsem, src_out, land_out, token):
        _handshake(peers_fn())
        start_fn(src_ref, land_ref, ssem, rsem)
        token[...] = jnp.zeros_like(token)

    res = pl.pallas_call(
        body, name=name, in_specs=[HBM, HBM], out_specs=[SEM, SEM, HBM, HBM, pl.BlockSpec(memory_space=pltpu.VMEM)],
        out_shape=[pltpu.SemaphoreType.DMA((n_sems,)), pltpu.SemaphoreType.DMA((n_sems,)), _hbm(src.shape, src.dtype),
                   _hbm(land_shape, src.dtype), jax.ShapeDtypeStruct(TOKEN_SHAPE, F32)],
        input_output_aliases={0: 2, 1: 3}, compiler_params=_split_params(cid),
    )(_in_hbm(src), _in_hbm(lax.empty(land_shape, src.dtype)))
    return tuple(res[:4]), res[4]


def _sibling_copies(src_ref, land_ref, ssem, rsem, k0, groups, which, to):
    def copy(k, src, dst):
        return pltpu.make_async_remote_copy(
            src_ref=src, dst_ref=dst, send_sem=ssem.at[k], recv_sem=rsem.at[k], device_id=to, device_id_type=MESH)

    if groups == 0:
        return [copy(k0, src_ref, land_ref)]
    hw = src_ref.shape[1] // groups // 2
    return [copy(k0 + j, src_ref.at[:, pl.ds(pl.multiple_of((2 * j + which) * hw, LANES), hw)],
                 land_ref.at[:, j * hw:(j + 1) * hw]) for j in range(groups)]


def _to_sibling_start(items, *, name, cid):
    n = len(items)
    shapes = [a.shape if g == 0 else (a.shape[0], a.shape[1] // 2) for a, g in items]
    first = [sum(max(g, 1) for _, g in items[:k]) for k in range(n + 1)]

    def body(*refs):
        srcs, lands, ssem, rsem, token = refs[:n], refs[n:2 * n], refs[2 * n], refs[2 * n + 1], refs[-1]
        _handshake(_sibling())
        x, y, c, _ = _place()
        for k, (_, g) in enumerate(items):
            for cp in _sibling_copies(srcs[k], lands[k], ssem, rsem, first[k], g, 1 - c, (x, y, 1 - c)):
                cp.start()
        token[...] = jnp.zeros_like(token)

    res = pl.pallas_call(
        body, name=name, in_specs=[HBM] * (2 * n),
        out_specs=[SEM, SEM] + [HBM] * (2 * n) + [pl.BlockSpec(memory_space=pltpu.VMEM)],
        out_shape=[pltpu.SemaphoreType.DMA((first[n],)), pltpu.SemaphoreType.DMA((first[n],))]
        + [_hbm(a.shape, a.dtype) for a, _ in items] + [_hbm(s, a.dtype) for s, (a, _) in zip(shapes, items)]
        + [jax.ShapeDtypeStruct(TOKEN_SHAPE, F32)],
        input_output_aliases={k: 2 + k for k in range(2 * n)}, compiler_params=_split_params(cid),
    )(*[_in_hbm(a) for a, _ in items], *[_in_hbm(lax.empty(s, a.dtype)) for s, (a, _) in zip(shapes, items)])
    return [(res[0], res[1], first[k], g, res[2 + k], res[2 + n + k]) for k, (_, g) in enumerate(items)], res[-1]


def _from_sibling(flight, after, *, name):
    ssem, rsem, k0, groups, src, land = flight

    def wait(src_ref, land_ref, ssem_ref, rsem_ref):
        x, y, c, _ = _place()
        for cp in _sibling_copies(src_ref, land_ref, ssem_ref, rsem_ref, k0, groups, 1 - c, (x, y, c)):
            cp.wait_send()
            cp.wait_recv()

    return _wait_call(wait, (ssem, rsem, src, land), after, name=name)


def _dev_peers(x, y, c, chips):
    return [(x, y, 1 - c)] + [(cx, cy, c) for cx, cy in chips] + [(cx, cy, 1 - c) for cx, cy in chips]


def _dev_gather_start(part, *, name, cid):
    def start(src_ref, land_ref, ssem, rsem):
        x, y, c, chips = _place()
        for k, to in enumerate(_dev_peers(x, y, c, chips)):
            pltpu.make_async_remote_copy(
                src_ref=src_ref, dst_ref=land_ref.at[4 * x + 2 * y + c], send_sem=ssem.at[k], recv_sem=rsem.at[k],
                device_id=to, device_id_type=MESH).start()

    return _split_start(start, lambda: _dev_peers(*_place()), part, (N_DEV,) + part.shape, N_DEV - 1, name=name,
                        cid=cid)


def _dev_gather_wait(parts, after, *, name):
    def wait(src_ref, land_ref, ssem_ref, rsem_ref):
        x, y, c, chips = _place()
        for k, (px, py, pc) in enumerate(_dev_peers(x, y, c, chips)):
            cp = pltpu.make_async_remote_copy(
                src_ref=src_ref, dst_ref=land_ref.at[4 * px + 2 * py + pc], send_sem=ssem_ref.at[k],
                recv_sem=rsem_ref.at[k], device_id=(x, y, c), device_id_type=MESH)
            cp.wait_send()
            cp.wait_recv()

    return _wait_call(wait, parts, after, name=name)[1]


def _sibling_share_halves(arrays, *, name, cid):
    n = len(arrays)

    def body(*refs):
        bufs = refs[n:2 * n]
        send_sems, recv_sems = refs[2 * n:]
        _handshake(_sibling())
        x, y, c, _ = _place()
        copies = []
        for a in range(n):
            mine = bufs[a].at[:, _half_cols(bufs[a].shape[1], c)]
            cp = pltpu.make_async_remote_copy(
                src_ref=mine, dst_ref=mine, send_sem=send_sems.at[a], recv_sem=recv_sems.at[a],
                device_id=(x, y, 1 - c), device_id_type=MESH)
            cp.start()
            copies.append(cp)
        for a in range(n):
            theirs = bufs[a].at[:, _half_cols(bufs[a].shape[1], 1 - c)]
            pltpu.make_async_remote_copy(
                src_ref=theirs, dst_ref=theirs, send_sem=send_sems.at[a], recv_sem=recv_sems.at[a],
                device_id=(x, y, c), device_id_type=MESH).wait_recv()
        for cp in copies:
            cp.wait_send()

    return pl.pallas_call(
        body, name=name, in_specs=[HBM] * n, out_specs=[HBM] * n,
        out_shape=[jax.ShapeDtypeStruct(h.shape, h.dtype) for h in arrays],
        input_output_aliases={a: a for a in range(n)},
        scratch_shapes=[pltpu.SemaphoreType.DMA((n,)), pltpu.SemaphoreType.DMA((n,))],
        compiler_params=pltpu.CompilerParams(collective_id=cid),
    )(*arrays)


def _pack(arrays, rows_multiple=16, width=LANES):
    flat = jnp.concatenate([a.astype(F32).reshape(-1) for a in arrays])
    total = flat.shape[0]
    rows = -(-total // width)
    rows = -(-rows // rows_multiple) * rows_multiple
    return jnp.pad(flat, (0, rows * width - total)).reshape(rows, width)


def _unpack(buf, shapes):
    flat = buf.reshape(-1)
    out, off = [], 0
    for s in shapes:
        n = math.prod(s)
        out.append(flat[off:off + n].reshape(s))
        off += n
    return out


def kernel(x, norm_pre, norm_post, gla_w_in, gla_w_gate2, gla_b_gate, gla_o_gain, gla_w_out, sgu_w_in, sgu_ln_gain, sgu_ln_bias, sgu_w_spatial, sgu_b_spatial, sgu_w_out, loss_target, m_norm_pre, m_norm_post, m_gla_w_in, m_gla_w_gate2, m_gla_b_gate, m_gla_o_gain, m_gla_w_out, m_sgu_w_in, m_sgu_ln_gain, m_sgu_ln_bias, m_sgu_w_spatial, m_sgu_b_spatial, m_sgu_w_out, v_norm_pre, v_norm_post, v_gla_w_in, v_gla_w_gate2, v_gla_b_gate, v_gla_o_gain, v_gla_w_out, v_sgu_w_in, v_sgu_ln_gain, v_sgu_ln_bias, v_sgu_w_spatial, v_sgu_b_spatial, v_sgu_w_out):
    _, t, d = x.shape
    dk = d // 2
    ws = gla_w_in.shape[2]
    wp = -(-ws // LANES) * LANES
    lay = (ws, wp)
    chip =2 * lax.axis_index("x") + lax.axis_index("y")
    core = lax.axis_index("c")
    core_idx = core.astype(jnp.int32).reshape(1)
    others = jnp.arange(N_CHIPS - 1, dtype=jnp.int32)
    others = others + (others >= chip).astype(jnp.int32)
    slots = jnp.concatenate([chip.astype(jnp.int32).reshape(1), others, core_idx])

    x0 = x[0]
    target = loss_target[0]

    wt_in_g, mt_in_g, vt_in_g = gla_w_in[0].T, m_gla_w_in[0].T, v_gla_w_in[0].T

    small_shard = _pack([gla_w_gate2[0], sgu_ln_gain[0], sgu_ln_bias[0]], rows_multiple=8, width=2 * LANES)
    own = [small_shard, jnp.pad(wt_in_g.astype(BF16), ((0, wp - ws), (0, 0)))]
    in_flight, token = _gather_start(own, name="gather_start_a", cid=0, relayed=(1,))

    def with_sibling_and_own(mine, land, name, cid):
        return lax.dynamic_update_slice(_sibling_forward(land, name=name + "_share", cid=cid), mine[None],
                                        (chip, 0, 0))

    h0 = _norm_pre(x0, norm_pre[0:1] + token[0:1, 0:1], name="pre0")
    g_small = with_sibling_and_own(*_gather_wait(in_flight[0], h0, name="w_small_wait"), "w_small", 12)
    mine, land = _gather_wait(in_flight[1], [g_small, wt_in_g, mt_in_g, vt_in_g], name="w_gla_in_wait", ks=NEIGHBOURS)
    relay, token = _relay_start(land, name="w_gla_in_relay", cid=11)
    crossing, token = _forward_start(relay[2], name="w_gla_in_share_near", cid=22, ks=NEIGHBOURS)
    own_later = [(p[0] + token[0, 0]).astype(BF16) for p in (gla_w_out, sgu_w_in, sgu_w_out)]
    in_flight_later, token = _gather_start(own_later, name="gather_start_b", cid=1, after=[token], own_slab=chip)
    in_flight = in_flight + in_flight_later
    land = _relay_wait((relay[0], relay[1], crossing[2]), token, name="w_gla_in_relay_wait")
    land = _forward_wait((crossing[0], crossing[1], land), token, name="w_gla_in_share_near_wait", ks=NEIGHBOURS)
    land = _sibling_forward(land, name="w_gla_in_share_far", cid=13, ks=(2,))
    wt_g = lax.dynamic_update_slice(land, mine[None], (chip, 0, 0)).reshape(N_CHIPS * wp, d)

    def behind(small, token):
        return small + token[0:1, 0:1]

    def arriving(i, after, name):
        mine, land = _gather_wait(in_flight[i], after, name=name + "_wait")
        crossing, token = _forward_start(land, name=name + "_share", cid=i)
        return (mine, crossing), token

    def arrived(pending, after, name):
        _, crossing = pending
        return _forward_wait(crossing, after, name=name + "_share_wait")

    shard_shapes = [gla_w_gate2.shape[1:], sgu_ln_gain.shape[1:], sgu_ln_bias.shape[1:]]
    per_chip = [_unpack(g_small[j], shard_shapes) for j in range(N_CHIPS)]
    w2_full = jnp.concatenate([p[0] for p in per_chip], axis=1)
    ln_gain = jnp.concatenate([p[1] for p in per_chip], axis=0)[None, :]
    ln_bias = jnp.concatenate([p[2] for p in per_chip], axis=0)[None, :]
    w2p = jnp.pad(w2_full, ((0, LANES - GLA_GATE_RANK), (0, 0)))

    pos_chunk = jnp.arange(SGU_BLOCK) // CHUNK
    mask = pos_chunk[:, None] >= pos_chunk[None, :]
    ws_masked = jnp.where(mask[None], sgu_w_spatial[0], 0.0)
    ws_masked_t = ws_masked.transpose(0, 2, 1)
    bs_t = sgu_b_spatial[0].T

    proj0 = _matmul(h0, wt_g, mode="nt", out_dtype=F32, name="gla_in", tn=wp)
    pending, tok = arriving(2, proj0, "w_gla_out")
    o0, a0, s_before, s_final = _gla_fwd(proj0, w2p, behind(gla_b_gate, tok), gla_o_gain, lay, name="gla_scan")
    w_out_g = arrived(pending, a0, "w_gla_out").reshape(d, d)
    y0 = _matmul(a0, w_out_g, mode="nn", out_dtype=F32, name="gla_out")
    pending, tok = arriving(3, y0, "w_sgu_in")
    x1, h1 = _post_then_pre(x0, y0, behind(norm_post[0:1], tok), norm_pre[1:2], name="post0_pre1")
    g_wi_s = arrived(pending, h1, "w_sgu_in")
    pending, tok = arriving(4, g_wi_s, "w_sgu_out")
    proj1 = _matmul(h1, g_wi_s, mode="nn", out_dtype=F32, name="sgu_in", b_shards=True, after=tok)
    a1 = _sgu_fwd(proj1, ln_gain, ln_bias, ws_masked, bs_t, name="sgu_gate")
    w_out_s = arrived(pending, a1, "w_sgu_out").reshape(d, d)
    acts, tok = _to_sibling_start([(a1, 0), (a0, 0), (h1, 0), (h0, 1)], name="acts_to_sibling", cid=5)
    a1, a0, h1, h0 = [f[4] for f in acts]
    y1 = _matmul(a1, w_out_s, mode="nn", out_dtype=F32, name="sgu_out", after=tok)
    loss_part, dx2, dy1, d_post1 = _loss_head(x1, y1, norm_post[1:2], target, name="loss_head")

    def pair_gradient(a_sent, b_sent, after, shards_on, name, cid):
        a_me, a_sib = _from_sibling(a_sent, after, name=name + "_a_wait")
        b_me, b_sib = _from_sibling(b_sent, [a_sib] + list(after), name=name + "_b_wait")
        pair = _matmul_dw_pair(a_me, a_sib, b_me, b_sib, core_idx, shards_on=shards_on,
                               name=name + "_pair")
        return _scatter_start(pair, name=name + "_start", cid=cid)

    def reduced(flight, after, name):
        pair, landed = _scatter_wait(flight, after, name=name + "_wait")
        return _chip_sum(pair, landed, slots, name=name + "_sum")

    (dy1_sent,), tok = _to_sibling_start([(dy1, 1)], name="dy1_to_sibling", cid=6)
    dy1 = dy1_sent[4]
    da1 = _matmul(dy1, w_out_s, mode="nt", out_dtype=F32, name="d_sgu_act", after=tok)
    fl_wo_s, tok = pair_gradient(acts[0], dy1_sent, [da1], "rows", "g_sgu_out", 15)
    dproj1, d_ws, d_bs_t, d_lg, d_lb = _sgu_bwd(da1, proj1, ln_gain, behind(ln_bias, tok), ws_masked, ws_masked_t,
                                                bs_t, name="sgu_gate_bwd")
    (dp1_sent,), tok = _to_sibling_start([(dproj1, N_CHIPS)], name="dproj1_to_sibling", cid=7)
    dproj1 = dp1_sent[4]
    dh1 = _matmul_nt_shards(dproj1, g_wi_s, out_dtype=F32, name="d_sgu_h", after=tok)
    fl_wi_s, tok = pair_gradient(acts[2], dp1_sent, [dh1], "cols", "g_sgu_in", 16)
    dx1, dy0, d_pre1, d_post0 = _mid_bwd(dx2, dh1, x1, behind(norm_pre[1:2], tok), y0, norm_post[0:1],
                                         name="pre1_post0_bwd")
    (dy0_sent,), tok = _to_sibling_start([(dy0, 1)], name="dy0_to_sibling", cid=8)
    dy0 = dy0_sent[4]
    da0 = _matmul(dy0, w_out_g, mode="nt", out_dtype=F32, name="d_gla_act", after=tok)
    fl_wo_g, tok = pair_gradient(acts[1], dy0_sent, [da0], "rows", "g_gla_out", 17)
    dproj0, d_og, d_bg, d_w2p = _gla_bwd(da0, o0, proj0, w2p, behind(gla_b_gate, tok), gla_o_gain, s_before, s_final,
                                         lay, name="gla_scan_bwd")
    early_shapes = [norm_post.shape, gla_b_gate.shape, gla_o_gain.shape, sgu_w_spatial.shape, sgu_b_spatial.shape,
                    (1, GLA_GATE_RANK, dk), (1, d), (1, d), (1, LANES)]
    early_part = _pack([jnp.concatenate([d_post0, d_post1], axis=0), d_bg, d_og, jnp.where(mask[None], d_ws, 0.0)[None],
                        d_bs_t.T[None], d_w2p[:GLA_GATE_RANK][None], d_lg, d_lb, loss_part])
    early_flight, tok = _dev_gather_start(early_part, name="small_early_start", cid=20)
    (dp0_sent,), tok_sent = _to_sibling_start([(dproj0, 0)], name="dproj0_to_sibling", cid=9)
    dproj0 = dp0_sent[4]
    dh0 = _matmul(dproj0, wt_g, mode="nn", out_dtype=F32, name="d_gla_h", after=tok_sent)
    a_me, a_sib = _from_sibling(dp0_sent, [dh0, tok], name="g_gla_in_a_wait")
    b_me, b_sib = _from_sibling(acts[3], [a_sib, dh0], name="g_gla_in_b_wait")
    fl_wi_g, tok_scatter = [], None
    for p in range(2):
        pair = _matmul_dw_pair(a_me, a_sib, b_me, b_sib, core_idx, shards_on="rows", part=(p, 2),
                               name=f"g_gla_in_pair{p}", after=tok_scatter)
        flight, tok_scatter = _scatter_start(pair, name=f"g_gla_in_start{p}", cid=18 + p)
        fl_wi_g.append(flight)
    r_wo_s = reduced(fl_wo_s, tok_scatter, "g_sgu_out")
    r_wi_s = reduced(fl_wi_s, r_wo_s, "g_sgu_in")
    r_wo_g = reduced(fl_wo_g, r_wi_s, "g_gla_out")
    sharing, tok = _share_start([r_wo_s, r_wi_s, r_wo_g], name="grads_share_a", cid=10)
    grad_x, d_pre0 = _first_bwd(dx1, dh0, x0, behind(norm_pre[0:1], tok), name="pre0_bwd")

    late_part = _pack([jnp.concatenate([d_pre0, d_pre1], axis=0)])
    late_flight, tok = _dev_gather_start(late_part, name="small_late_start", cid=21)

    def big_update(w, g, m, v, name, after=None):
        return [u[None] for u in _adamw(w[0], g, m[0], v[0], name=name, after=after)]

    g_wo_sgu, g_wi_sgu, g_wo_gla = _share_wait(sharing, [grad_x, tok], name="grads_share_a_wait")
    u_wi_sgu = big_update(sgu_w_in, g_wi_sgu, m_sgu_w_in, v_sgu_w_in, "adamw_sgu_w_in")
    u_wo_gla = big_update(gla_w_out, g_wo_gla, m_gla_w_out, v_gla_w_out, "adamw_gla_w_out", after=u_wi_sgu[1])

    r_wi_g, behind_this = None, u_wo_gla[1]
    for p, flight in enumerate(fl_wi_g):
        pair, landed = _scatter_wait(flight, behind_this, name=f"g_gla_in_wait{p}")
        r_wi_g = behind_this = _chip_sum(pair, landed, slots, part=(p, 2), into=r_wi_g, name=f"g_gla_in_sum{p}")
    gt_wi_gla, = _sibling_share_halves([r_wi_g], name="grads_share_b", cid=14)
    u_wi_gla_t = _adamw(wt_in_g, gt_wi_gla, mt_in_g, vt_in_g, name="adamw_gla_w_in")
    u_wi_gla = [u.T[None] for u in u_wi_gla_t]
    u_wo_sgu = big_update(sgu_w_out, g_wo_sgu, m_sgu_w_out, v_sgu_w_out, "adamw_sgu_w_out", after=u_wi_gla_t[1])

    def summed_over_devices(part, flight, after, shapes, name):
        land = _dev_gather_wait(flight, after, name=name + "_wait")
        every = lax.dynamic_update_slice(land, part[None], (2 * chip + core, 0, 0))
        return _unpack(_stack_sum(every, name=name + "_sum"), shapes)

    (g_post, g_bg, g_og, g_wsp, g_bsp, g_w2_full, g_lg_full, g_lb_full, loss_vec) = summed_over_devices(
        early_part, early_flight, u_wo_sgu[1], early_shapes, "small_early")
    g_pre, = summed_over_devices(late_part, late_flight, loss_vec, [norm_pre.shape], "small_late")
    loss = loss_vec[0, 0]
    g_w2 = lax.dynamic_slice_in_dim(g_w2_full, chip * (dk // N_CHIPS), dk // N_CHIPS, axis=2)
    g_lg = lax.dynamic_slice_in_dim(g_lg_full, chip * (d // N_CHIPS), d // N_CHIPS, axis=1)
    g_lb = lax.dynamic_slice_in_dim(g_lb_full, chip * (d // N_CHIPS), d // N_CHIPS, axis=1)

    small_w = [norm_pre, norm_post, gla_b_gate, gla_o_gain, sgu_w_spatial, sgu_b_spatial, gla_w_gate2, sgu_ln_gain,
               sgu_ln_bias]
    small_g = [g_pre, g_post, g_bg, g_og, g_wsp, g_bsp, g_w2, g_lg, g_lb]
    small_m = [m_norm_pre, m_norm_post, m_gla_b_gate, m_gla_o_gain, m_sgu_w_spatial, m_sgu_b_spatial, m_gla_w_gate2,
               m_sgu_ln_gain, m_sgu_ln_bias]
    small_v = [v_norm_pre, v_norm_post, v_gla_b_gate, v_gla_o_gain, v_sgu_w_spatial, v_sgu_b_spatial, v_gla_w_gate2,
               v_sgu_ln_gain, v_sgu_ln_bias]
    own_shapes = [w.shape for w in small_w]
    _, s_dl, s_m, s_v = _adamw(_pack(small_w), _pack(small_g), _pack(small_m), _pack(small_v), name="adamw_small")
    dl_s, m_s, v_s = _unpack(s_dl, own_shapes), _unpack(s_m, own_shapes), _unpack(s_v, own_shapes)

    def ordered(small, kind):
        pre, post, bg, og, wsp, bsp, w2, lg, lb = small
        return [pre, post, u_wi_gla[kind], w2, bg, og, u_wo_gla[kind], u_wi_sgu[kind], lg, lb, wsp, bsp, u_wo_sgu[kind]]

    return (loss, grad_x[None], *ordered(small_g, 0), *ordered(dl_s, 1), *ordered(m_s, 2), *ordered(v_s, 3))
```

```python
import math

import jax
import jax.numpy as jnp
from jax import lax
from jax.experimental import pallas as pl
from jax.experimental.pallas import tpu as pltpu

F32 = jnp.float32
BF16 = jnp.bfloat16
MESH = pl.DeviceIdType.MESH

EPS = 1e-6
CHUNK = 64
GLA_HEADS = 4
GLA_GATE_RANK = 16
GLA_TAU = 16.0
SGU_BLOCK = 128
SGU_GROUPS = 8
N_CHIPS = 4
N_DEV = 8
LANES = 128

ADAM_LR = 0.001
ADAM_B1 = 0.9
ADAM_B2 = 0.999
ADAM_EPS = 1e-08
ADAM_WD = 0.01
ADAM_STEP = 10

VMEM_LIMIT = 56 * 1024 * 1024


def _cparams(sem=None):
    return pltpu.CompilerParams(dimension_semantics=sem, vmem_limit_bytes=VMEM_LIMIT)


def _pick(n, cap, unit=LANES):
    best = None
    for t in range(unit, min(n, cap) + 1, unit):
        if n % t == 0:
            best = t
    assert best is not None, (n, cap, unit)
    return best


def _dot(a, b, dims):
    return lax.dot_general(a, b, (dims, ((), ())), preferred_element_type=F32)


def _dot_nn(a, b):
    return _dot(a, b, ((1,), (0,)))


def _dot_nt(a, b):
    return _dot(a, b, ((1,), (1,)))


def _dot_tn(a, b):
    return _dot(a, b, ((0,), (0,)))


def _matmul(a, b, *, mode, out_dtype, name, tm=1024, tn=512, b_shards=False, after=None):
    M, K = a.shape
    if b_shards:
        ns, Kb, bc = b.shape
        N, tn = ns * bc, _pick(bc, tn)
        per = bc // tn
        b_spec = pl.BlockSpec((None, K, tn), lambda i, j: (j // per, 0, j % per))
    elif mode == "nt":
        N, Kb = b.shape
        tn = _pick(N, tn)
        b_spec = pl.BlockSpec((tn, K), lambda i, j: (j, 0))
    else:
        Kb, N = b.shape
        tn = _pick(N, tn)
        b_spec = pl.BlockSpec((K, tn), lambda i, j: (0, j))
    assert K == Kb and a.dtype == b.dtype == BF16, (a.shape, b.shape, mode)
    tm = _pick(M, tm)
    dims = ((1,), (1,)) if mode == "nt" else ((1,), (0,))
    extra_specs, extra_args = ([], []) if after is None else ([pl.BlockSpec(memory_space=pl.ANY)], [after])

    def body(a_ref, b_ref, *rest):
        rest[-1][...] = _dot(a_ref[...], b_ref[...], dims).astype(out_dtype)

    return pl.pallas_call(
        body, name=name, grid=(M // tm, N // tn),
        in_specs=[pl.BlockSpec((tm, K), lambda i, j: (i, 0)), b_spec] + extra_specs,
        out_specs=pl.BlockSpec((tm, tn), lambda i, j: (i, j)), out_shape=jax.ShapeDtypeStruct((M, N), out_dtype),
        compiler_params=_cparams(("parallel", "parallel")),
    )(a, b, *extra_args)


def _matmul_nt_shards(a, b, *, out_dtype, name, tm=1024, tn=512, after=None):
    M, K = a.shape
    ns, N, kc = b.shape
    assert K == ns * kc
    tm, tn = _pick(M, tm), _pick(N, tn)

    def body(a_ref, *rest):
        b_refs, o_ref = rest[:ns], rest[ns + (after is not None)]
        acc = _dot_nt(a_ref[:, 0:kc], b_refs[0][...])
        for j in range(1, ns):
            acc += _dot_nt(a_ref[:, j * kc:(j + 1) * kc], b_refs[j][...])
        o_ref[...] = acc.astype(out_dtype)

    def shard(j):
        return pl.BlockSpec((None, tn, kc), lambda i, n: (j, n, 0))

    extra_specs, extra_args = ([], []) if after is None else ([pl.BlockSpec(memory_space=pl.ANY)], [after])
    return pl.pallas_call(
        body, name=name, grid=(M // tm, N // tn),
        in_specs=[pl.BlockSpec((tm, K), lambda i, n: (i, 0))] + [shard(j) for j in range(ns)] + extra_specs,
        out_specs=pl.BlockSpec((tm, tn), lambda i, n: (i, n)), out_shape=jax.ShapeDtypeStruct((M, N), out_dtype),
        compiler_params=_cparams(("parallel", "parallel")),
    )(a, *([b] * ns), *extra_args)


def _rstd(x):
    return lax.rsqrt(jnp.mean(x * x, axis=-1, keepdims=True) + EPS)


def _row_spec(tr, d):
    return pl.BlockSpec((tr, d), lambda i: (i, 0))


def _vec_spec(d):
    return pl.BlockSpec((1, d), lambda i: (0, 0))


def _acc_rows(ref, i, val, cols=slice(None)):
    @pl.when(i == 0)
    def _():
        ref[:, cols] = val

    @pl.when(i > 0)
    def _():
        ref[:, cols] += val


def _norm_pre(x, gain, *, name, tr=256):
    t, d = x.shape
    tr = _pick(t, tr, 8)

    def body(x_ref, g_ref, h_ref):
        xv = x_ref[...]
        h_ref[...] = (xv * _rstd(xv) * g_ref[...]).astype(BF16)

    return pl.pallas_call(
        body, name=name, grid=(t // tr,), in_specs=[_row_spec(tr, d), _vec_spec(d)], out_specs=_row_spec(tr, d),
        out_shape=jax.ShapeDtypeStruct((t, d), BF16), compiler_params=_cparams(("parallel",)),
    )(x, gain)


def _post_then_pre(x, y, post_gain, pre_gain, *, name, tr=256):
    t, d = x.shape
    tr = _pick(t, tr, 8)

    def body(x_ref, y_ref, pg_ref, ng_ref, xn_ref, h_ref):
        yv = y_ref[...]
        xn = x_ref[...] + yv * _rstd(yv) * pg_ref[...]
        xn_ref[...] = xn
        h_ref[...] = (xn * _rstd(xn) * ng_ref[...]).astype(BF16)

    return pl.pallas_call(
        body, name=name, grid=(t // tr,),
        in_specs=[_row_spec(tr, d), _row_spec(tr, d), _vec_spec(d), _vec_spec(d)],
        out_specs=[_row_spec(tr, d), _row_spec(tr, d)],
        out_shape=[jax.ShapeDtypeStruct((t, d), F32), jax.ShapeDtypeStruct((t, d), BF16)],
        compiler_params=_cparams(("parallel",)),
    )(x, y, post_gain, pre_gain)


def _norm_bwd(dy, n, r, gain):
    dn = dy * gain
    return r * (dn - n * jnp.mean(dn * n, axis=-1, keepdims=True))


def _loss_head(x, y, post_gain, target, *, name, tr=256):
    t, d = x.shape
    tr = _pick(t, tr, 8)

    def body(x_ref, y_ref, pg_ref, t_ref, loss_ref, dx_ref, dy_ref, dpg_ref):
        i = pl.program_id(0)
        yv = y_ref[...]
        r = _rstd(yv)
        n = yv * r
        err = x_ref[...] + n * pg_ref[...] - t_ref[...]
        dx = err * (1.0 / d)
        dx_ref[...] = dx
        part = 0.5 * jnp.sum(jnp.mean(err * err, axis=-1, keepdims=True), axis=0, keepdims=True)
        _acc_rows(loss_ref, i, jnp.broadcast_to(part, (1, LANES)))
        _acc_rows(dpg_ref, i, jnp.sum(dx * n, axis=0, keepdims=True))
        dy_ref[...] = _norm_bwd(dx, n, r, pg_ref[...]).astype(BF16)

    return pl.pallas_call(
        body, name=name, grid=(t // tr,),
        in_specs=[_row_spec(tr, d), _row_spec(tr, d), _vec_spec(d), _row_spec(tr, d)],
        out_specs=[_vec_spec(LANES), _row_spec(tr, d), _row_spec(tr, d), _vec_spec(d)],
        out_shape=[jax.ShapeDtypeStruct((1, LANES), F32), jax.ShapeDtypeStruct((t, d), F32),
                   jax.ShapeDtypeStruct((t, d), BF16), jax.ShapeDtypeStruct((1, d), F32)],
        compiler_params=_cparams(("arbitrary",)),
    )(x, y, post_gain, target)


def _mid_bwd(dx_out, dh, x, pre_gain, y_prev, post_gain_prev, *, name, tr=256):
    t, d = x.shape
    tr = _pick(t, tr, 8)

    def body(dxo_ref, dh_ref, x_ref, ng_ref, y_ref, pg_ref, dx_ref, dy_ref, dng_ref, dpg_ref):
        i = pl.program_id(0)
        xv = x_ref[...]
        r = _rstd(xv)
        xh = xv * r
        dhv = dh_ref[...]
        _acc_rows(dng_ref, i, jnp.sum(dhv * xh, axis=0, keepdims=True))
        dx = dxo_ref[...] + _norm_bwd(dhv, xh, r, ng_ref[...])
        dx_ref[...] = dx
        yv = y_ref[...]
        ry = _rstd(yv)
        n = yv * ry
        _acc_rows(dpg_ref, i, jnp.sum(dx * n, axis=0, keepdims=True))
        dy_ref[...] = _norm_bwd(dx, n, ry, pg_ref[...]).astype(BF16)

    return pl.pallas_call(
        body, name=name, grid=(t // tr,),
        in_specs=[_row_spec(tr, d), _row_spec(tr, d), _row_spec(tr, d), _vec_spec(d), _row_spec(tr, d), _vec_spec(d)],
        out_specs=[_row_spec(tr, d), _row_spec(tr, d), _vec_spec(d), _vec_spec(d)],
        out_shape=[jax.ShapeDtypeStruct((t, d), F32), jax.ShapeDtypeStruct((t, d), BF16),
                   jax.ShapeDtypeStruct((1, d), F32), jax.ShapeDtypeStruct((1, d), F32)],
        compiler_params=_cparams(("arbitrary",)),
    )(dx_out, dh, x, pre_gain, y_prev, post_gain_prev)


def _first_bwd(dx_out, dh, x, pre_gain, *, name, tr=256):
    t, d = x.shape
    tr = _pick(t, tr, 8)

    def body(dxo_ref, dh_ref, x_ref, ng_ref, dx_ref, dng_ref):
        i = pl.program_id(0)
        xv = x_ref[...]
        r = _rstd(xv)
        xh = xv * r
        dhv = dh_ref[...]
        _acc_rows(dng_ref, i, jnp.sum(dhv * xh, axis=0, keepdims=True))
        dx_ref[...] = dxo_ref[...] + _norm_bwd(dhv, xh, r, ng_ref[...])

    return pl.pallas_call(
        body, name=name, grid=(t // tr,),
        in_specs=[_row_spec(tr, d), _row_spec(tr, d), _row_spec(tr, d), _vec_spec(d)],
        out_specs=[_row_spec(tr, d), _vec_spec(d)],
        out_shape=[jax.ShapeDtypeStruct((t, d), F32), jax.ShapeDtypeStruct((1, d), F32)],
        compiler_params=_cparams(("arbitrary",)),
    )(dx_out, dh, x, pre_gain)


def _sigmoid(x):
    return 1.0 / (1.0 + jnp.exp(-x))


def _log_sigmoid(x):
    return jnp.minimum(x, 0.0) - jnp.log(1.0 + jnp.exp(-jnp.abs(x)))


_GELU_C = math.sqrt(2.0 / math.pi)


_GELU_A = 0.044715


def _gelu_parts(x, with_grad=True):
    x2 = x * x
    h = 0.5 * jnp.tanh(x * (_GELU_C + (_GELU_C * _GELU_A) * x2)) + 0.5
    val = x * h
    if not with_grad:
        return val, None
    return val, h * (1.0 + (1.0 - h) * (x * (2.0 * _GELU_C + (6.0 * _GELU_C * _GELU_A) * x2)))


def _split3(x):
    hi = x.astype(BF16)
    r1 = x - hi.astype(F32)
    mid = r1.astype(BF16)
    lo = (r1 - mid.astype(F32)).astype(BF16)
    return hi, mid, lo


def _tri_matmul(tri_bf16, x):
    hi, mid, lo = _split3(x)
    return _dot_nn(tri_bf16, hi) + _dot_nn(tri_bf16, mid) + _dot_nn(tri_bf16, lo)


def _gla_dims(d):
    dk, dv = d // 2, d
    return dk, dv, dk // GLA_HEADS, dv // GLA_HEADS


def _col_pieces(a, b, lay):
    ws, wp = lay
    out = []
    while a < b:
        j = a // ws
        end = min(b, (j + 1) * ws)
        out.append((j * wp + a - j * ws, end - a))
        a = end
    return out


def _load_cols(ref, a, b, lay):
    parts = [ref[:, s:s + n] for s, n in _col_pieces(a, b, lay)]
    return parts[0] if len(parts) == 1 else jnp.concatenate(parts, axis=1)


def _store_cols(ref, a, val, lay):
    off = 0
    for s, n in _col_pieces(a, a + val.shape[1], lay):
        ref[:, s:s + n] = val[:, off:off + n]
        off += n


def _gate_window(c_r, lay):
    (start, _), = _col_pieces(c_r, c_r + GLA_GATE_RANK, lay)
    assert (start % lay[1]) + LANES <= lay[1]
    return slice(start, start + LANES)


def _gla_gates(glr, k, w2_ref, b_ref):
    z = _dot_nn(glr.astype(BF16), w2_ref[...].astype(BF16)) + b_ref[...]
    la = _log_sigmoid(z) * (1.0 / GLA_TAU)
    row = lax.broadcasted_iota(jnp.int32, (CHUNK, CHUNK), 0)
    col = lax.broadcasted_iota(jnp.int32, (CHUNK, CHUNK), 1)
    incl = (row >= col).astype(BF16)
    bcum = _tri_matmul(incl, la)
    b_end = bcum[CHUNK - 1:CHUNK, :]
    e_rest = jnp.exp(b_end - bcum)
    return z, e_rest, k * e_rest, jnp.exp(b_end)


def _gla_fwd(proj, w2p, b_gate, o_gain, lay, *, name):
    t, wcols = proj.shape
    d = o_gain.shape[1]
    dk, dv, dkh, dvh = _gla_dims(d)
    nc = t // CHUNK
    c_k, c_v, c_g, c_r = dk, 2 * dk, 2 * dk + dv, 2 * dk + 2 * dv
    scale = dkh ** -0.5

    def body(p_ref, w2_ref, b_ref, og_ref, o_ref, a_ref, sb_ref, sfin_ref, s_ref):
        i = pl.program_id(0)

        @pl.when(i == 0)
        def _():
            s_ref[...] = jnp.zeros_like(s_ref)

        q = _load_cols(p_ref, 0, dk, lay) * scale
        k = _load_cols(p_ref, c_k, c_k + dk, lay)
        glr = p_ref[:, _gate_window(c_r, lay)]
        _, _, kdec, decay = _gla_gates(glr, k, w2_ref, b_ref)
        for h in range(GLA_HEADS):
            ks = slice(h * dkh, (h + 1) * dkh)
            vs = slice(h * dvh, (h + 1) * dvh)
            v_h = _load_cols(p_ref, c_v + h * dvh, c_v + (h + 1) * dvh, lay)
            g_h = _load_cols(p_ref, c_g + h * dvh, c_g + (h + 1) * dvh, lay)
            s_old = s_ref[h]
            sb_ref[0, h] = s_old
            s_new = s_old * decay[:, ks] + _dot_tn(v_h.astype(BF16), kdec[:, ks].astype(BF16))
            s_ref[h] = s_new
            o_h = _dot_nt(q[:, ks].astype(BF16), s_new.astype(BF16))
            o_ref[:, vs] = o_h
            on = o_h * _rstd(o_h)
            a_ref[:, vs] = (on * og_ref[:, vs] * (g_h * _sigmoid(g_h))).astype(BF16)

        @pl.when(i == nc - 1)
        def _():
            sfin_ref[...] = s_ref[...]

    full = lambda *shape: pl.BlockSpec(shape, lambda i: (0,) * len(shape))
    return pl.pallas_call(
        body, name=name, grid=(nc,),
        in_specs=[pl.BlockSpec((CHUNK, wcols), lambda i: (i, 0)), full(LANES, dk), full(1, dk), full(1, dv)],
        out_specs=[pl.BlockSpec((CHUNK, dv), lambda i: (i, 0)), pl.BlockSpec((CHUNK, dv), lambda i: (i, 0)),
                   pl.BlockSpec((1, GLA_HEADS, dvh, dkh), lambda i: (i, 0, 0, 0)), full(GLA_HEADS, dvh, dkh)],
        out_shape=[jax.ShapeDtypeStruct((t, dv), F32), jax.ShapeDtypeStruct((t, dv), BF16),
                   jax.ShapeDtypeStruct((nc, GLA_HEADS, dvh, dkh), F32),
                   jax.ShapeDtypeStruct((GLA_HEADS, dvh, dkh), F32)],
        scratch_shapes=[pltpu.VMEM((GLA_HEADS, dvh, dkh), F32)],
        compiler_params=_cparams(("arbitrary",)),
    )(proj, w2p, b_gate, o_gain)


def _gla_bwd(da, o, proj, w2p, b_gate, o_gain, s_before, s_final, lay, *, name):
    t, wcols = proj.shape
    d = o_gain.shape[1]
    dk, dv, dkh, dvh = _gla_dims(d)
    nc = t // CHUNK
    c_k, c_v, c_g, c_r = dk, 2 * dk, 2 * dk + dv, 2 * dk + 2 * dv
    scale = dkh ** -0.5

    def body(da_ref, o_ref, p_ref, w2_ref, b_ref, og_ref, sb_ref, sfin_ref,
             dp_ref, dog_ref, db_ref, dw2_ref, s_ref, gc_ref, dkd_ref):
        i = pl.program_id(0)

        @pl.when(i == 0)
        def _():
            s_ref[...] = sfin_ref[...]
            gc_ref[...] = jnp.zeros_like(gc_ref)

        ws, wp = lay
        for j in range(N_CHIPS):
            dp_ref[:, j * wp + ws:(j + 1) * wp] = jnp.zeros((CHUNK, wp - ws), BF16)
        q = _load_cols(p_ref, 0, dk, lay) * scale
        k = _load_cols(p_ref, c_k, c_k + dk, lay)
        glr = p_ref[:, _gate_window(c_r, lay)]
        z, e_rest, kdec, decay = _gla_gates(glr, k, w2_ref, b_ref)
        ddecay = []
        for h in range(GLA_HEADS):
            ks = slice(h * dkh, (h + 1) * dkh)
            vs = slice(h * dvh, (h + 1) * dvh)
            v_h = _load_cols(p_ref, c_v + h * dvh, c_v + (h + 1) * dvh, lay)
            g_h = _load_cols(p_ref, c_g + h * dvh, c_g + (h + 1) * dvh, lay)
            da_h = da_ref[:, vs]
            o_h = o_ref[:, vs]
            og_h = og_ref[:, vs]
            r = _rstd(o_h)
            on = o_h * r
            sg = _sigmoid(g_h)
            silu = g_h * sg
            _acc_rows(dog_ref, i, jnp.sum(da_h * silu * on, axis=0, keepdims=True), vs)
            _store_cols(dp_ref, c_g + h * dvh, (da_h * (on * og_h) * (sg * (1.0 + g_h * (1.0 - sg)))).astype(BF16),
                        lay)
            don = da_h * silu * og_h
            do_h = (r * (don - on * jnp.mean(don * on, axis=-1, keepdims=True))).astype(BF16)
            s_cur = s_ref[h]
            _store_cols(dp_ref, h * dkh, (_dot_nn(do_h, s_cur.astype(BF16)) * scale).astype(BF16), lay)
            g_tot = gc_ref[h] + _dot_tn(do_h, q[:, ks].astype(BF16))
            g_bf = g_tot.astype(BF16)
            dkd_ref[:, ks] = _dot_nn(v_h.astype(BF16), g_bf)
            _store_cols(dp_ref, c_v + h * dvh, _dot_nt(kdec[:, ks].astype(BF16), g_bf).astype(BF16), lay)
            s_prev = sb_ref[0, h]
            ddecay.append(jnp.sum(g_tot * s_prev, axis=0, keepdims=True))
            gc_ref[h] = g_tot * decay[:, ks]
            s_ref[h] = s_prev
        dkdec = dkd_ref[...]
        _store_cols(dp_ref, c_k, (dkdec * e_rest).astype(BF16), lay)
        d_e = dkdec * kdec
        row = lax.broadcasted_iota(jnp.int32, (CHUNK, CHUNK), 0)
        col = lax.broadcasted_iota(jnp.int32, (CHUNK, CHUNK), 1)
        excl = (row > col).astype(BF16)
        dla = jnp.concatenate(ddecay, axis=1) * decay + _tri_matmul(excl, d_e)
        dz = dla * (1.0 / GLA_TAU) * (1.0 - _sigmoid(z))
        _acc_rows(db_ref, i, jnp.sum(dz, axis=0, keepdims=True))
        dz_bf = dz.astype(BF16)
        dw2 = _dot_tn(glr.astype(BF16), dz_bf)

        @pl.when(i == 0)
        def _():
            dw2_ref[...] = dw2

        @pl.when(i > 0)
        def _():
            dw2_ref[...] += dw2

        dp_ref[:, _gate_window(c_r, lay)] = _dot_nt(dz_bf, w2_ref[...].astype(BF16)).astype(BF16)

    rev = lambda i: (nc - 1 - i, 0)
    full = lambda *shape: pl.BlockSpec(shape, lambda i: (0,) * len(shape))
    return pl.pallas_call(
        body, name=name, grid=(nc,),
        in_specs=[pl.BlockSpec((CHUNK, dv), rev), pl.BlockSpec((CHUNK, dv), rev), pl.BlockSpec((CHUNK, wcols), rev),
                  full(LANES, dk), full(1, dk), full(1, dv),
                  pl.BlockSpec((1, GLA_HEADS, dvh, dkh), lambda i: (nc - 1 - i, 0, 0, 0)), full(GLA_HEADS, dvh, dkh)],
        out_specs=[pl.BlockSpec((CHUNK, wcols), rev), full(1, dv), full(1, dk), full(LANES, dk)],
        out_shape=[jax.ShapeDtypeStruct((t, wcols), BF16), jax.ShapeDtypeStruct((1, dv), F32),
                   jax.ShapeDtypeStruct((1, dk), F32), jax.ShapeDtypeStruct((LANES, dk), F32)],
        scratch_shapes=[pltpu.VMEM((GLA_HEADS, dvh, dkh), F32), pltpu.VMEM((GLA_HEADS, dvh, dkh), F32),
                        pltpu.VMEM((CHUNK, dk), F32)],
        compiler_params=_cparams(("arbitrary",)),
    )(da, o, proj, w2p, b_gate, o_gain, s_before, s_final)


def _sgu_mid(p_ref, lg_ref, lb_ref, ws_ref, bst_ref, w, with_grad=True):
    gd = w // SGU_GROUPS
    u_act, du_fac = _gelu_parts(p_ref[:, 0:w], with_grad)
    vf, dv_fac = _gelu_parts(p_ref[:, w:2 * w], with_grad)
    mu = jnp.mean(vf, axis=-1, keepdims=True)
    cen = vf - mu
    rstd = lax.rsqrt(jnp.mean(cen * cen, axis=-1, keepdims=True) + EPS)
    xh = cen * rstd
    vn = (xh * lg_ref[...] + lb_ref[...]).astype(BF16)
    vs = [_dot_nn(ws_ref[g].astype(BF16), vn[:, g * gd:(g + 1) * gd]) + bst_ref[:, g:g + 1]
          for g in range(SGU_GROUPS)]
    return u_act, du_fac, dv_fac, rstd, xh, vn, vs


def _sgu_fwd(proj, ln_gain, ln_bias, ws_masked, bs_t, *, name):
    t, w3 = proj.shape
    w = w3 // 3
    gd = w // SGU_GROUPS
    nb = t // SGU_BLOCK

    def body(p_ref, lg_ref, lb_ref, ws_ref, bst_ref, a_ref):
        u_act, _, _, _, _, _, vs = _sgu_mid(p_ref, lg_ref, lb_ref, ws_ref, bst_ref, w, with_grad=False)
        for g in range(SGU_GROUPS):
            cs = slice(g * gd, (g + 1) * gd)
            gate = p_ref[:, 2 * w + g * gd:2 * w + (g + 1) * gd]
            a_ref[:, cs] = (u_act[:, cs] * vs[g] * (gate * _sigmoid(gate))).astype(BF16)

    full = lambda *shape: pl.BlockSpec(shape, lambda i: (0,) * len(shape))
    return pl.pallas_call(
        body, name=name, grid=(nb,),
        in_specs=[pl.BlockSpec((SGU_BLOCK, w3), lambda i: (i, 0)), full(1, w), full(1, w),
                  full(SGU_GROUPS, SGU_BLOCK, SGU_BLOCK), full(SGU_BLOCK, SGU_GROUPS)],
        out_specs=pl.BlockSpec((SGU_BLOCK, w), lambda i: (i, 0)),
        out_shape=jax.ShapeDtypeStruct((t, w), BF16),
        compiler_params=_cparams(("parallel",)),
    )(proj, ln_gain, ln_bias, ws_masked, bs_t)


def _sgu_bwd(da, proj, ln_gain, ln_bias, ws_masked, ws_masked_t, bs_t, *, name):
    t, w3 = proj.shape
    w = w3 // 3
    gd = w // SGU_GROUPS
    nb = t // SGU_BLOCK

    def body(da_ref, p_ref, lg_ref, lb_ref, ws_ref, wst_ref, bst_ref, dp_ref, dws_ref, dbst_ref, dlg_ref, dlb_ref,
             dvn_ref):
        i = pl.program_id(0)
        u_act, du_fac, dv_fac, rstd, xh, vn, vs = _sgu_mid(p_ref, lg_ref, lb_ref, ws_ref, bst_ref, w)
        for g in range(SGU_GROUPS):
            cs = slice(g * gd, (g + 1) * gd)
            gate = p_ref[:, 2 * w + g * gd:2 * w + (g + 1) * gd]
            sg = _sigmoid(gate)
            silu = gate * sg
            da_g = da_ref[:, cs]
            ua_g = u_act[:, cs]
            dp_ref[:, cs] = (da_g * vs[g] * silu * du_fac[:, cs]).astype(BF16)
            dp_ref[:, 2 * w + g * gd:2 * w + (g + 1) * gd] = (
                da_g * ua_g * vs[g] * (sg * (1.0 + gate * (1.0 - sg)))).astype(BF16)
            dvs = da_g * ua_g * silu
            dvs_bf = dvs.astype(BF16)
            dvn_ref[:, cs] = _dot_nn(wst_ref[g].astype(BF16), dvs_bf)
            dws = _dot_nt(dvs_bf, vn[:, cs])
            dbs = jnp.sum(dvs, axis=1, keepdims=True)

            @pl.when(i == 0)
            def _():
                dws_ref[g] = dws
                dbst_ref[:, g:g + 1] = dbs

            @pl.when(i > 0)
            def _():
                dws_ref[g] += dws
                dbst_ref[:, g:g + 1] += dbs

        dvn = dvn_ref[...]
        _acc_rows(dlg_ref, i, jnp.sum(dvn * xh, axis=0, keepdims=True))
        _acc_rows(dlb_ref, i, jnp.sum(dvn, axis=0, keepdims=True))
        dxh = dvn * lg_ref[...]
        dvf = rstd * (dxh - jnp.mean(dxh, axis=-1, keepdims=True)
                      - xh * jnp.mean(dxh * xh, axis=-1, keepdims=True))
        dp_ref[:, w:2 * w] = (dvf * dv_fac).astype(BF16)

    full = lambda *shape: pl.BlockSpec(shape, lambda i: (0,) * len(shape))
    return pl.pallas_call(
        body, name=name, grid=(nb,),
        in_specs=[pl.BlockSpec((SGU_BLOCK, w), lambda i: (i, 0)), pl.BlockSpec((SGU_BLOCK, w3), lambda i: (i, 0)),
                  full(1, w), full(1, w), full(SGU_GROUPS, SGU_BLOCK, SGU_BLOCK),
                  full(SGU_GROUPS, SGU_BLOCK, SGU_BLOCK), full(SGU_BLOCK, SGU_GROUPS)],
        out_specs=[pl.BlockSpec((SGU_BLOCK, w3), lambda i: (i, 0)), full(SGU_GROUPS, SGU_BLOCK, SGU_BLOCK),
                   full(SGU_BLOCK, SGU_GROUPS), full(1, w), full(1, w)],
        out_shape=[jax.ShapeDtypeStruct((t, w3), BF16), jax.ShapeDtypeStruct((SGU_GROUPS, SGU_BLOCK, SGU_BLOCK), F32),
                   jax.ShapeDtypeStruct((SGU_BLOCK, SGU_GROUPS), F32), jax.ShapeDtypeStruct((1, w), F32),
                   jax.ShapeDtypeStruct((1, w), F32)],
        scratch_shapes=[pltpu.VMEM((SGU_BLOCK, w), F32)],
        compiler_params=_cparams(("arbitrary",)),
    )(da, proj, ln_gain, ln_bias, ws_masked, ws_masked_t, bs_t)


def _tile2d(rows, cols, block_bytes, row_unit):
    if rows % row_unit == 0:
        return _pick(rows, max(row_unit, block_bytes // (4 * cols)), row_unit), cols
    return rows, _pick(cols, max(LANES, block_bytes // (4 * rows)))


def _adamw(w, g, m, v, *, name, block_bytes=1 << 20, after=None):
    rows, cols = w.shape
    tr, tc = _tile2d(rows, cols, block_bytes, 8)
    g_rows = g.shape[0]
    assert g_rows == rows or tr == rows
    extra_specs, extra_args = ([], []) if after is None else ([pl.BlockSpec(memory_space=pl.ANY)], [after])

    def body(w_ref, g_ref, m_ref, v_ref, *rest):
        go_ref, d_ref, mo_ref, vo_ref = rest[len(extra_args):]
        gv = g_ref[0:tr, :]
        go_ref[...] = gv
        mn = ADAM_B1 * m_ref[...] + (1.0 - ADAM_B1) * gv
        vn = ADAM_B2 * v_ref[...] + (1.0 - ADAM_B2) * (gv * gv)
        m_hat = mn / (1.0 - ADAM_B1 ** ADAM_STEP)
        v_hat = vn / (1.0 - ADAM_B2 ** ADAM_STEP)
        d_ref[...] = -ADAM_LR * (m_hat / (jnp.sqrt(v_hat) + ADAM_EPS) + ADAM_WD * w_ref[...])
        mo_ref[...] = mn
        vo_ref[...] = vn

    spec = pl.BlockSpec((tr, tc), lambda i, j: (i, j))
    g_spec = spec if g_rows == rows else pl.BlockSpec((g_rows, tc), lambda i, j: (0, j))
    return pl.pallas_call(
        body, name=name, grid=(rows // tr, cols // tc), in_specs=[spec, g_spec, spec, spec] + extra_specs,
        out_specs=[spec] * 4, out_shape=[jax.ShapeDtypeStruct((rows, cols), F32)] * 4,
        compiler_params=_cparams(("parallel", "parallel")),
    )(w, g, m, v, *extra_args)


def _matmul_dw_pair(a_me, a_sib, b_me, b_sib, core_idx, *, shards_on, name, after=None, part=(0, 1)):
    T, M = a_me.shape
    N = b_me.shape[1]
    if shards_on == "rows":
        p, count = part
        tm, hc = M // N_CHIPS, N // 2
        hp = hc // count
        tn = _pick(hp, 512)
        per = hp // tn
        grid = (N_CHIPS, per)
        a_spec = pl.BlockSpec((T, tm), lambda i, n, h: (0, i))
        b_me_spec = pl.BlockSpec((T, tn), lambda i, n, h: (0, (h[0] * count + p) * per + n))
        b_sib_spec = pl.BlockSpec((T, tn), lambda i, n, h: (0, p * per + n))
        out_spec = pl.BlockSpec((None, tm, tn), lambda i, n, h: (i, 0, n))
        out_shape = jax.ShapeDtypeStruct((N_CHIPS, tm, hp), BF16)
    else:
        tm, hc = _pick(M, 1024), N // N_CHIPS // 2
        grid = (M // tm, N_CHIPS)
        a_spec = pl.BlockSpec((T, tm), lambda i, j, h: (0, i))
        b_me_spec = pl.BlockSpec((T, hc), lambda i, j, h: (0, 2 * j + h[0]))
        b_sib_spec = pl.BlockSpec((T, hc), lambda i, j, h: (0, j))
        out_spec = pl.BlockSpec((None, tm, hc), lambda i, j, h: (j, i, 0))
        out_shape = jax.ShapeDtypeStruct((N_CHIPS, M, hc), BF16)
    extra_specs, extra_args = ([], []) if after is None else ([pl.BlockSpec(memory_space=pl.ANY)], [after])

    def body(h_ref, am_ref, as_ref, bm_ref, bs_ref, *rest):
        o_ref = rest[len(extra_args)]
        o_ref[...] = (_dot_tn(am_ref[...], bm_ref[...]) + _dot_tn(as_ref[...], bs_ref[...])).astype(BF16)

    grid_spec = pltpu.PrefetchScalarGridSpec(
        num_scalar_prefetch=1, grid=grid, in_specs=[a_spec, a_spec, b_me_spec, b_sib_spec] + extra_specs,
        out_specs=out_spec)
    return pl.pallas_call(
        body, name=name, grid_spec=grid_spec, out_shape=out_shape, compiler_params=_cparams(("parallel", "parallel")),
    )(core_idx, a_me, a_sib, b_me, b_sib, *extra_args)


def _chip_sum(pair, landed, slots, *, name, block_bytes=1 << 20, part=(0, 1), into=None):
    p, count = part
    _, r, hp = pair.shape
    tr, tc = _tile2d(r, hp, block_bytes, 16)
    ncb = hp // tc
    extra_specs, extra_args = ([], []) if into is None else ([pl.BlockSpec(memory_space=pl.ANY)], [into])

    def body(s_ref, own_ref, l0_ref, l1_ref, l2_ref, *rest):
        rest[-1][...] = ((own_ref[...].astype(F32) + l0_ref[...].astype(F32)) + l1_ref[...].astype(F32)
                         ) + l2_ref[...].astype(F32)

    def slab(which):
        return pl.BlockSpec((None, tr, tc), lambda i, k, s: (s[which], i, k))

    grid_spec = pltpu.PrefetchScalarGridSpec(
        num_scalar_prefetch=1, grid=(r // tr, ncb),
        in_specs=[slab(0), slab(1), slab(2), slab(3)] + extra_specs,
        out_specs=pl.BlockSpec((tr, tc), lambda i, k, s: (i, (s[4] * count + p) * ncb + k)))
    return pl.pallas_call(
        body, name=name, grid_spec=grid_spec, out_shape=jax.ShapeDtypeStruct((r, 2 * hp * count), F32),
        input_output_aliases={} if into is None else {5: 0},
        compiler_params=_cparams(("parallel", "parallel")),
    )(slots, pair, landed, landed, landed, *extra_args)


def _stack_sum(x, *, name, out_dtype=F32, block_bytes=1 << 20):
    s, r, c = x.shape
    tr = _pick(r, max(8, block_bytes // (4 * c)), 16) if r % 16 == 0 else r

    def body(x_ref, o_ref):
        acc = x_ref[0].astype(F32)
        for j in range(1, s):
            acc = acc + x_ref[j].astype(F32)
        o_ref[...] = acc.astype(out_dtype)

    return pl.pallas_call(
        body, name=name, grid=(r // tr,),
        in_specs=[pl.BlockSpec((s, tr, c), lambda i: (0, i, 0))], out_specs=pl.BlockSpec((tr, c), lambda i: (i, 0)),
        out_shape=jax.ShapeDtypeStruct((r, c), out_dtype), compiler_params=_cparams(("parallel",)),
    )(x)


HBM = pl.BlockSpec(memory_space=pltpu.HBM)


def _place():
    x, y, c = lax.axis_index("x"), lax.axis_index("y"), lax.axis_index("c")
    other_chips = [(1 - x, y), (x, 1 - y), (1 - x, 1 - y)]
    return x, y, c, other_chips


def _handshake(peers):
    barrier = pltpu.get_barrier_semaphore()
    for peer in peers:
        pl.semaphore_signal(barrier, inc=1, device_id=peer, device_id_type=MESH)
    pl.semaphore_wait(barrier, len(peers))


def _sibling():
    x, y, c, _ = _place()
    return [(x, y, 1 - c)]


def _same_core_chips():
    x, y, c, chips = _place()
    return [(cx, cy, c) for cx, cy in chips]


def _same_core_neighbours():
    x, y, c, _ = _place()
    return [(1 - x, y, c), (x, 1 - y, c)]


def _split_params(cid):
    return pltpu.CompilerParams(has_side_effects=SIDE_EFFECT, collective_id=cid)


def _half_cols(cols, which):
    hc = cols // 2
    return pl.ds(pl.multiple_of(which * hc, LANES), hc)


SEM = pl.BlockSpec(memory_space=pltpu.SEMAPHORE)
ANY = pl.BlockSpec(memory_space=pl.ANY)
SIDE_EFFECT = pltpu.SideEffectType.DATAFLOW_SIDE_EFFECTING
TOKEN_SHAPE = (8, LANES)


def _hbm(shape, dtype):
    return pltpu.HBM(shape, dtype)


def _in_hbm(a):
    return pltpu.with_memory_space_constraint(a, pltpu.HBM)


def _gather_copy(src_ref, land_ref, ssem, rsem, k, chip_of_block, to, c):
    cols = src_ref.shape[1]
    return pltpu.make_async_remote_copy(
        src_ref=src_ref.at[:, _half_cols(cols, c)], dst_ref=land_ref.at[chip_of_block, :, _half_cols(cols, c)],
        send_sem=ssem.at[k], recv_sem=rsem.at[k], device_id=to, device_id_type=MESH)


NEIGHBOURS = (0, 1)
ALL_CHIPS = (0, 1, 2)


def _gather_start(shards, *, name, cid, after=(), relayed=(), lands=None):
    n = len(shards)
    after = list(after)

    def body(*refs):
        srcs, lands = refs[:n], refs[n:2 * n]
        outs = refs[2 * n + len(after):]
        token = outs[-1]
        _handshake(_same_core_chips())
        x, y, c, chips = _place()
        me = 2 * x + y
        for a in range(n):
            ssem, rsem = outs[4 * a], outs[4 * a + 1]
            for k in NEIGHBOURS if a in relayed else ALL_CHIPS:
                cx, cy = chips[k]
                _gather_copy(srcs[a], lands[a], ssem, rsem, k, me, (cx, cy, c), c).start()
        token[...] = jnp.zeros_like(token)

    out_shape, out_specs, aliases = [], [], {}
    for a, s in enumerate(shards):
        out_shape += [pltpu.SemaphoreType.DMA((3,)), pltpu.SemaphoreType.DMA((3,)), _hbm(s.shape, s.dtype),
                      _hbm((N_CHIPS,) + s.shape, s.dtype)]
        out_specs += [SEM, SEM, HBM, HBM]
        aliases[a] = 4 * a + 2
        aliases[n + a] = 4 * a + 3
    out_shape.append(jax.ShapeDtypeStruct(TOKEN_SHAPE, F32))
    out_specs.append(pl.BlockSpec(memory_space=pltpu.VMEM))
    if lands is None:
        lands = [lax.empty((N_CHIPS,) + s.shape, s.dtype) for s in shards]
    lands = [_in_hbm(land) for land in lands]
    res = pl.pallas_call(
        body, name=name, in_specs=[HBM] * (2 * n) + [ANY] * len(after), out_specs=out_specs, out_shape=out_shape,
        input_output_aliases=aliases, compiler_params=_split_params(cid),
    )(*[_in_hbm(s) for s in shards], *lands, *after)
    return [tuple(res[4 * a:4 * a + 4]) for a in range(n)], res[-1]


def _wait_call(wait_fn, parts, after, *, name):
    ssem, rsem, src, land = parts
    after = list(after) if isinstance(after, (list, tuple)) else [after]

    def body(src_ref, land_ref, ssem_ref, rsem_ref, *rest):
        wait_fn(src_ref, land_ref, ssem_ref, rsem_ref)

    return pl.pallas_call(
        body, name=name, in_specs=[HBM, HBM, SEM, SEM] + [ANY] * len(after), out_specs=[HBM, HBM],
        out_shape=[_hbm(src.shape, src.dtype), _hbm(land.shape, land.dtype)], input_output_aliases={0: 0, 1: 1},
        compiler_params=pltpu.CompilerParams(has_side_effects=SIDE_EFFECT),
    )(src, land, ssem, rsem, *after)


def _gather_wait(parts, after, *, name, ks=ALL_CHIPS):
    def wait(src_ref, land_ref, ssem_ref, rsem_ref):
        x, y, c, chips = _place()
        for k in ks:
            cx, cy = chips[k]
            cp = _gather_copy(src_ref, land_ref, ssem_ref, rsem_ref, k, 2 * cx + cy, (x, y, c), c)
            cp.wait_send()
            cp.wait_recv()

    return _wait_call(wait, parts, after, name=name)


def _relay_copy(buf_ref, ssem, rsem, k, slab, to, c):
    hr = buf_ref.shape[1] // 2
    part = buf_ref.at[slab, pl.ds(k * hr, hr), _half_cols(buf_ref.shape[2], c)]
    return pltpu.make_async_remote_copy(
        src_ref=part, dst_ref=part, send_sem=ssem.at[k], recv_sem=rsem.at[k], device_id=to, device_id_type=MESH)


def _relay_start(land, *, name, cid):
    def body(buf_ref, ssem, rsem, buf_out, token):
        _handshake(_same_core_neighbours())
        x, y, c, _ = _place()
        _relay_copy(buf_ref, ssem, rsem, 0, 2 * (1 - x) + y, (x, 1 - y, c), c).start()
        _relay_copy(buf_ref, ssem, rsem, 1, 2 * x + 1 - y, (1 - x, y, c), c).start()
        token[...] = jnp.zeros_like(token)

    res = pl.pallas_call(
        body, name=name, in_specs=[HBM], out_specs=[SEM, SEM, HBM, pl.BlockSpec(memory_space=pltpu.VMEM)],
        out_shape=[pltpu.SemaphoreType.DMA((2,)), pltpu.SemaphoreType.DMA((2,)), _hbm(land.shape, land.dtype),
                   jax.ShapeDtypeStruct(TOKEN_SHAPE, F32)],
        input_output_aliases={0: 2}, compiler_params=_split_params(cid),
    )(land)
    return tuple(res[:3]), res[3]


def _relay_wait(parts, after, *, name):
    ssem, rsem, buf = parts
    after = list(after) if isinstance(after, (list, tuple)) else [after]

    def body(buf_ref, ssem_ref, rsem_ref, *rest):
        x, y, c, _ = _place()
        diagonal = 2 * (1 - x) + 1 - y
        _relay_copy(buf_ref, ssem_ref, rsem_ref, 0, 2 * (1 - x) + y, (x, y, c), c).wait_send()
        _relay_copy(buf_ref, ssem_ref, rsem_ref, 1, 2 * x + 1 - y, (x, y, c), c).wait_send()
        _relay_copy(buf_ref, ssem_ref, rsem_ref, 0, diagonal, (x, y, c), c).wait_recv()
        _relay_copy(buf_ref, ssem_ref, rsem_ref, 1, diagonal, (x, y, c), c).wait_recv()

    return pl.pallas_call(
        body, name=name, in_specs=[HBM, SEM, SEM] + [ANY] * len(after), out_specs=HBM,
        out_shape=_hbm(buf.shape, buf.dtype), input_output_aliases={0: 0},
        compiler_params=pltpu.CompilerParams(has_side_effects=SIDE_EFFECT),
    )(buf, ssem, rsem, *after)


def _forward_copy(buf_ref, ssem, rsem, k, slab, which, to):
    part = buf_ref.at[slab, :, _half_cols(buf_ref.shape[2], which)]
    return pltpu.make_async_remote_copy(
        src_ref=part, dst_ref=part, send_sem=ssem.at[k], recv_sem=rsem.at[k], device_id=to, device_id_type=MESH)


def _sibling_forward(land, *, name, cid, ks=ALL_CHIPS):
    def body(_, buf, send_sems, recv_sems):
        _handshake(_sibling())
        x, y, c, chips = _place()
        copies = []
        for k in ks:
            cx, cy = chips[k]
            cp = _forward_copy(buf, send_sems, recv_sems, k, 2 * cx + cy, c, (x, y, 1 - c))
            cp.start()
            copies.append(cp)
        for k in ks:
            cx, cy = chips[k]
            _forward_copy(buf, send_sems, recv_sems, k, 2 * cx + cy, 1 - c, (x, y, c)).wait_recv()
        for cp in copies:
            cp.wait_send()

    return pl.pallas_call(
        body, name=name, in_specs=[HBM], out_specs=HBM, out_shape=jax.ShapeDtypeStruct(land.shape, land.dtype),
        input_output_aliases={0: 0},
        scratch_shapes=[pltpu.SemaphoreType.DMA((3,)), pltpu.SemaphoreType.DMA((3,))],
        compiler_params=pltpu.CompilerParams(collective_id=cid),
    )(land)


def _forward_start(land, *, name, cid, ks=ALL_CHIPS):
    def body(buf_ref, ssem, rsem, buf_out, token):
        _handshake(_sibling())
        x, y, c, chips = _place()
        for k in ks:
            cx, cy = chips[k]
            _forward_copy(buf_ref, ssem, rsem, k, 2 * cx + cy, c, (x, y, 1 - c)).start()
        token[...] = jnp.zeros_like(token)

    res = pl.pallas_call(
        body, name=name, in_specs=[HBM], out_specs=[SEM, SEM, HBM, pl.BlockSpec(memory_space=pltpu.VMEM)],
        out_shape=[pltpu.SemaphoreType.DMA((3,)), pltpu.SemaphoreType.DMA((3,)), _hbm(land.shape, land.dtype),
                   jax.ShapeDtypeStruct(TOKEN_SHAPE, F32)],
        input_output_aliases={0: 2}, compiler_params=_split_params(cid),
    )(land)
    return tuple(res[:3]), res[3]


def _forward_wait(parts, after, *, name, ks=ALL_CHIPS):
    ssem, rsem, buf = parts
    after = list(after) if isinstance(after, (list, tuple)) else [after]

    def body(buf_ref, ssem_ref, rsem_ref, *rest):
        x, y, c, chips = _place()
        for k in ks:
            cx, cy = chips[k]
            _forward_copy(buf_ref, ssem_ref, rsem_ref, k, 2 * cx + cy, c, (x, y, c)).wait_send()
            _forward_copy(buf_ref, ssem_ref, rsem_ref, k, 2 * cx + cy, 1 - c, (x, y, c)).wait_recv()

    return pl.pallas_call(
        body, name=name, in_specs=[HBM, SEM, SEM] + [ANY] * len(after), out_specs=HBM,
        out_shape=_hbm(buf.shape, buf.dtype), input_output_aliases={0: 0},
        compiler_params=pltpu.CompilerParams(has_side_effects=SIDE_EFFECT),
    )(buf, ssem, rsem, *after)


def _share_copy(buf_ref, ssem, rsem, a, which, to):
    part = buf_ref.at[:, _half_cols(buf_ref.shape[1], which)]
    return pltpu.make_async_remote_copy(
        src_ref=part, dst_ref=part, send_sem=ssem.at[a], recv_sem=rsem.at[a], device_id=to, device_id_type=MESH)


def _share_start(arrays, *, name, cid):
    n = len(arrays)

    def body(*refs):
        bufs, ssem, rsem, token = refs[:n], refs[n], refs[n + 1], refs[-1]
        _handshake(_sibling())
        x, y, c, _ = _place()
        for a in range(n):
            _share_copy(bufs[a], ssem, rsem, a, c, (x, y, 1 - c)).start()
        token[...] = jnp.zeros_like(token)

    res = pl.pallas_call(
        body, name=name, in_specs=[HBM] * n,
        out_specs=[SEM, SEM] + [HBM] * n + [pl.BlockSpec(memory_space=pltpu.VMEM)],
        out_shape=[pltpu.SemaphoreType.DMA((n,)), pltpu.SemaphoreType.DMA((n,))]
        + [_hbm(b.shape, b.dtype) for b in arrays] + [jax.ShapeDtypeStruct(TOKEN_SHAPE, F32)],
        input_output_aliases={a: 2 + a for a in range(n)}, compiler_params=_split_params(cid),
    )(*[_in_hbm(b) for b in arrays])
    return (res[0], res[1], list(res[2:2 + n])), res[-1]


def _share_wait(parts, after, *, name):
    ssem, rsem, bufs = parts
    n = len(bufs)
    after = list(after) if isinstance(after, (list, tuple)) else [after]

    def body(*refs):
        buf_refs, ssem_ref, rsem_ref = refs[:n], refs[n], refs[n + 1]
        x, y, c, _ = _place()
        for a in range(n):
            _share_copy(buf_refs[a], ssem_ref, rsem_ref, a, c, (x, y, c)).wait_send()
            _share_copy(buf_refs[a], ssem_ref, rsem_ref, a, 1 - c, (x, y, c)).wait_recv()

    return pl.pallas_call(
        body, name=name, in_specs=[HBM] * n + [SEM, SEM] + [ANY] * len(after), out_specs=[HBM] * n,
        out_shape=[_hbm(b.shape, b.dtype) for b in bufs], input_output_aliases={a: a for a in range(n)},
        compiler_params=pltpu.CompilerParams(has_side_effects=SIDE_EFFECT),
    )(*bufs, ssem, rsem, *after)


def _scatter_copy(src_ref, land_ref, ssem, rsem, k, src_slab, dst_slab, to):
    return pltpu.make_async_remote_copy(
        src_ref=src_ref.at[src_slab], dst_ref=land_ref.at[dst_slab], send_sem=ssem.at[k], recv_sem=rsem.at[k],
        device_id=to, device_id_type=MESH)


def _scatter_start(part, *, name, cid):
    def start(src_ref, land_ref, ssem, rsem):
        x, y, c, chips = _place()
        me = 2 * x + y
        for k, (cx, cy) in enumerate(chips):
            _scatter_copy(src_ref, land_ref, ssem, rsem, k, 2 * cx + cy, me, (cx, cy, c)).start()

    return _split_start(start, _same_core_chips, part, part.shape, N_CHIPS - 1, name=name, cid=cid)


def _scatter_wait(parts, after, *, name):
    def wait(src_ref, land_ref, ssem_ref, rsem_ref):
        x, y, c, chips = _place()
        for k, (cx, cy) in enumerate(chips):
            idx = 2 * cx + cy
            cp = _scatter_copy(src_ref, land_ref, ssem_ref, rsem_ref, k, idx, idx, (x, y, c))
            cp.wait_send()
            cp.wait_recv()

    return _wait_call(wait, parts, after, name=name)


def _split_start(start_fn, peers_fn, src, land_shape, n_sems, *, name, cid):
    def body(src_ref, land_ref, ssem, rsem, src_out, land_out, token):
        _handshake(peers_fn())
        start_fn(src_ref, land_ref, ssem, rsem)
        token[...] = jnp.zeros_like(token)

    res = pl.pallas_call(
        body, name=name, in_specs=[HBM, HBM], out_specs=[SEM, SEM, HBM, HBM, pl.BlockSpec(memory_space=pltpu.VMEM)],
        out_shape=[pltpu.SemaphoreType.DMA((n_sems,)), pltpu.SemaphoreType.DMA((n_sems,)), _hbm(src.shape, src.dtype),
                   _hbm(land_shape, src.dtype), jax.ShapeDtypeStruct(TOKEN_SHAPE, F32)],
        input_output_aliases={0: 2, 1: 3}, compiler_params=_split_params(cid),
    )(_in_hbm(src), _in_hbm(lax.empty(land_shape, src.dtype)))
    return tuple(res[:4]), res[4]


def _sibling_copies(src_ref, land_ref, ssem, rsem, k0, groups, which, to):
    def copy(k, src, dst):
        return pltpu.make_async_remote_copy(
            src_ref=src, dst_ref=dst, send_sem=ssem.at[k], recv_sem=rsem.at[k], device_id=to, device_id_type=MESH)

    if groups == 0:
        return [copy(k0, src_ref, land_ref)]
    hw = src_ref.shape[1] // groups // 2
    return [copy(k0 + j, src_ref.at[:, pl.ds(pl.multiple_of((2 * j + which) * hw, LANES), hw)],
                 land_ref.at[:, j * hw:(j + 1) * hw]) for j in range(groups)]


def _to_sibling_start(items, *, name, cid):
    n = len(items)
    shapes = [a.shape if g == 0 else (a.shape[0], a.shape[1] // 2) for a, g in items]
    first = [sum(max(g, 1) for _, g in items[:k]) for k in range(n + 1)]

    def body(*refs):
        srcs, lands, ssem, rsem, token = refs[:n], refs[n:2 * n], refs[2 * n], refs[2 * n + 1], refs[-1]
        _handshake(_sibling())
        x, y, c, _ = _place()
        for k, (_, g) in enumerate(items):
            for cp in _sibling_copies(srcs[k], lands[k], ssem, rsem, first[k], g, 1 - c, (x, y, 1 - c)):
                cp.start()
        token[...] = jnp.zeros_like(token)

    res = pl.pallas_call(
        body, name=name, in_specs=[HBM] * (2 * n),
        out_specs=[SEM, SEM] + [HBM] * (2 * n) + [pl.BlockSpec(memory_space=pltpu.VMEM)],
        out_shape=[pltpu.SemaphoreType.DMA((first[n],)), pltpu.SemaphoreType.DMA((first[n],))]
        + [_hbm(a.shape, a.dtype) for a, _ in items] + [_hbm(s, a.dtype) for s, (a, _) in zip(shapes, items)]
        + [jax.ShapeDtypeStruct(TOKEN_SHAPE, F32)],
        input_output_aliases={k: 2 + k for k in range(2 * n)}, compiler_params=_split_params(cid),
    )(*[_in_hbm(a) for a, _ in items], *[_in_hbm(lax.empty(s, a.dtype)) for s, (a, _) in zip(shapes, items)])
    return [(res[0], res[1], first[k], g, res[2 + k], res[2 + n + k]) for k, (_, g) in enumerate(items)], res[-1]


def _from_sibling(flight, after, *, name):
    ssem, rsem, k0, groups, src, land = flight

    def wait(src_ref, land_ref, ssem_ref, rsem_ref):
        x, y, c, _ = _place()
        for cp in _sibling_copies(src_ref, land_ref, ssem_ref, rsem_ref, k0, groups, 1 - c, (x, y, c)):
            cp.wait_send()
            cp.wait_recv()

    return _wait_call(wait, (ssem, rsem, src, land), after, name=name)


def _dev_peers(x, y, c, chips):
    return [(x, y, 1 - c)] + [(cx, cy, c) for cx, cy in chips] + [(cx, cy, 1 - c) for cx, cy in chips]


def _dev_gather_start(part, *, name, cid):
    def start(src_ref, land_ref, ssem, rsem):
        x, y, c, chips = _place()
        for k, to in enumerate(_dev_peers(x, y, c, chips)):
            pltpu.make_async_remote_copy(
                src_ref=src_ref, dst_ref=land_ref.at[4 * x + 2 * y + c], send_sem=ssem.at[k], recv_sem=rsem.at[k],
                device_id=to, device_id_type=MESH).start()

    return _split_start(start, lambda: _dev_peers(*_place()), part, (N_DEV,) + part.shape, N_DEV - 1, name=name,
                        cid=cid)


def _dev_gather_wait(parts, after, *, name):
    def wait(src_ref, land_ref, ssem_ref, rsem_ref):
        x, y, c, chips = _place()
        for k, (px, py, pc) in enumerate(_dev_peers(x, y, c, chips)):
            cp = pltpu.make_async_remote_copy(
                src_ref=src_ref, dst_ref=land_ref.at[4 * px + 2 * py + pc], send_sem=ssem_ref.at[k],
                recv_sem=rsem_ref.at[k], device_id=(x, y, c), device_id_type=MESH)
            cp.wait_send()
            cp.wait_recv()

    return _wait_call(wait, parts, after, name=name)[1]


def _sibling_share_halves(arrays, *, name, cid):
    n = len(arrays)

    def body(*refs):
        bufs = refs[n:2 * n]
        send_sems, recv_sems = refs[2 * n:]
        _handshake(_sibling())
        x, y, c, _ = _place()
        copies = []
        for a in range(n):
            mine = bufs[a].at[:, _half_cols(bufs[a].shape[1], c)]
            cp = pltpu.make_async_remote_copy(
                src_ref=mine, dst_ref=mine, send_sem=send_sems.at[a], recv_sem=recv_sems.at[a],
                device_id=(x, y, 1 - c), device_id_type=MESH)
            cp.start()
            copies.append(cp)
        for a in range(n):
            theirs = bufs[a].at[:, _half_cols(bufs[a].shape[1], 1 - c)]
            pltpu.make_async_remote_copy(
                src_ref=theirs, dst_ref=theirs, send_sem=send_sems.at[a], recv_sem=recv_sems.at[a],
                device_id=(x, y, c), device_id_type=MESH).wait_recv()
        for cp in copies:
            cp.wait_send()

    return pl.pallas_call(
        body, name=name, in_specs=[HBM] * n, out_specs=[HBM] * n,
        out_shape=[jax.ShapeDtypeStruct(h.shape, h.dtype) for h in arrays],
        input_output_aliases={a: a for a in range(n)},
        scratch_shapes=[pltpu.SemaphoreType.DMA((n,)), pltpu.SemaphoreType.DMA((n,))],
        compiler_params=pltpu.CompilerParams(collective_id=cid),
    )(*arrays)


def _pack(arrays, rows_multiple=16, width=LANES):
    flat = jnp.concatenate([a.astype(F32).reshape(-1) for a in arrays])
    total = flat.shape[0]
    rows = -(-total // width)
    rows = -(-rows // rows_multiple) * rows_multiple
    return jnp.pad(flat, (0, rows * width - total)).reshape(rows, width)


def _unpack(buf, shapes):
    flat = buf.reshape(-1)
    out, off = [], 0
    for s in shapes:
        n = math.prod(s)
        out.append(flat[off:off + n].reshape(s))
        off += n
    return out


def kernel(x, norm_pre, norm_post, gla_w_in, gla_w_gate2, gla_b_gate, gla_o_gain, gla_w_out, sgu_w_in, sgu_ln_gain, sgu_ln_bias, sgu_w_spatial, sgu_b_spatial, sgu_w_out, loss_target, m_norm_pre, m_norm_post, m_gla_w_in, m_gla_w_gate2, m_gla_b_gate, m_gla_o_gain, m_gla_w_out, m_sgu_w_in, m_sgu_ln_gain, m_sgu_ln_bias, m_sgu_w_spatial, m_sgu_b_spatial, m_sgu_w_out, v_norm_pre, v_norm_post, v_gla_w_in, v_gla_w_gate2, v_gla_b_gate, v_gla_o_gain, v_gla_w_out, v_sgu_w_in, v_sgu_ln_gain, v_sgu_ln_bias, v_sgu_w_spatial, v_sgu_b_spatial, v_sgu_w_out):
    _, t, d = x.shape
    dk = d // 2
    ws = gla_w_in.shape[2]
    wp = -(-ws // LANES) * LANES
    lay = (ws, wp)
    chip =2 * lax.axis_index("x") + lax.axis_index("y")
    core = lax.axis_index("c")
    core_idx = core.astype(jnp.int32).reshape(1)
    others = jnp.arange(N_CHIPS - 1, dtype=jnp.int32)
    others = others + (others >= chip).astype(jnp.int32)
    slots = jnp.concatenate([chip.astype(jnp.int32).reshape(1), others, core_idx])

    x0 = x[0]
    target = loss_target[0]

    wt_in_g, mt_in_g, vt_in_g = gla_w_in[0].T, m_gla_w_in[0].T, v_gla_w_in[0].T

    small_shard = _pack([gla_w_gate2[0], sgu_ln_gain[0], sgu_ln_bias[0]], rows_multiple=8, width=2 * LANES)
    own = [small_shard, jnp.pad(wt_in_g.astype(BF16), ((0, wp - ws), (0, 0)))]
    in_flight, token = _gather_start(own, name="gather_start_a", cid=0, relayed=(1,))

    def with_sibling_and_own(mine, land, name, cid):
        return lax.dynamic_update_slice(_sibling_forward(land, name=name + "_share", cid=cid), mine[None],
                                        (chip, 0, 0))

    h0 = _norm_pre(x0, norm_pre[0:1] + token[0:1, 0:1], name="pre0")
    g_small = with_sibling_and_own(*_gather_wait(in_flight[0], h0, name="w_small_wait"), "w_small", 12)
    own_later = [(p[0] + token[0, 0]).astype(BF16) for p in (gla_w_out, sgu_w_in, sgu_w_out)]
    lands_later = [lax.dynamic_update_slice(lax.empty((N_CHIPS,) + s.shape, BF16), s[None], (chip, 0, 0))
                   for s in own_later]
    mine, land = _gather_wait(in_flight[1], [g_small, wt_in_g, mt_in_g, vt_in_g] + lands_later, name="w_gla_in_wait",
                              ks=NEIGHBOURS)
    relay, token = _relay_start(land, name="w_gla_in_relay", cid=11)
    crossing, token = _forward_start(relay[2], name="w_gla_in_share_near", cid=22, ks=NEIGHBOURS)
    in_flight_later, token = _gather_start(own_later, name="gather_start_b", cid=1, after=[token], lands=lands_later)
    in_flight = in_flight + in_flight_later
    land = _relay_wait((relay[0], relay[1], crossing[2]), token, name="w_gla_in_relay_wait")
    land = _forward_wait((crossing[0], crossing[1], land), token, name="w_gla_in_share_near_wait", ks=NEIGHBOURS)
    land = _sibling_forward(land, name="w_gla_in_share_far", cid=13, ks=(2,))
    wt_g = lax.dynamic_update_slice(land, mine[None], (chip, 0, 0)).reshape(N_CHIPS * wp, d)

    def behind(small, token):
        return small + token[0:1, 0:1]

    def arriving(i, after, name):
        mine, land = _gather_wait(in_flight[i], after, name=name + "_wait")
        crossing, token = _forward_start(land, name=name + "_share", cid=i)
        return (mine, crossing), token

    def arrived(pending, after, name):
        _, crossing = pending
        return _forward_wait(crossing, after, name=name + "_share_wait")

    shard_shapes = [gla_w_gate2.shape[1:], sgu_ln_gain.shape[1:], sgu_ln_bias.shape[1:]]
    per_chip = [_unpack(g_small[j], shard_shapes) for j in range(N_CHIPS)]
    w2_full = jnp.concatenate([p[0] for p in per_chip], axis=1)
    ln_gain = jnp.concatenate([p[1] for p in per_chip], axis=0)[None, :]
    ln_bias = jnp.concatenate([p[2] for p in per_chip], axis=0)[None, :]
    w2p = jnp.pad(w2_full, ((0, LANES - GLA_GATE_RANK), (0, 0)))

    pos_chunk = jnp.arange(SGU_BLOCK) // CHUNK
    mask = pos_chunk[:, None] >= pos_chunk[None, :]
    ws_masked = jnp.where(mask[None], sgu_w_spatial[0], 0.0)
    ws_masked_t = ws_masked.transpose(0, 2, 1)
    bs_t = sgu_b_spatial[0].T

    proj0 = _matmul(h0, wt_g, mode="nt", out_dtype=F32, name="gla_in", tn=wp)
    pending, tok = arriving(2, proj0, "w_gla_out")
    o0, a0, s_before, s_final = _gla_fwd(proj0, w2p, behind(gla_b_gate, tok), gla_o_gain, lay, name="gla_scan")
    w_out_g = arrived(pending, a0, "w_gla_out").reshape(d, d)
    y0 = _matmul(a0, w_out_g, mode="nn", out_dtype=F32, name="gla_out")
    pending, tok = arriving(3, y0, "w_sgu_in")
    x1, h1 = _post_then_pre(x0, y0, behind(norm_post[0:1], tok), norm_pre[1:2], name="post0_pre1")
    g_wi_s = arrived(pending, h1, "w_sgu_in")
    pending, tok = arriving(4, g_wi_s, "w_sgu_out")
    proj1 = _matmul(h1, g_wi_s, mode="nn", out_dtype=F32, name="sgu_in", b_shards=True, after=tok)
    a1 = _sgu_fwd(proj1, ln_gain, ln_bias, ws_masked, bs_t, name="sgu_gate")
    w_out_s = arrived(pending, a1, "w_sgu_out").reshape(d, d)
    acts, tok = _to_sibling_start([(a1, 0), (a0, 0), (h1, 0), (h0, 1)], name="acts_to_sibling", cid=5)
    a1, a0, h1, h0 = [f[4] for f in acts]
    y1 = _matmul(a1, w_out_s, mode="nn", out_dtype=F32, name="sgu_out", after=tok)
    loss_part, dx2, dy1, d_post1 = _loss_head(x1, y1, norm_post[1:2], target, name="loss_head")

    def pair_gradient(a_sent, b_sent, after, shards_on, name, cid):
        a_me, a_sib = _from_sibling(a_sent, after, name=name + "_a_wait")
        b_me, b_sib = _from_sibling(b_sent, [a_sib] + list(after), name=name + "_b_wait")
        pair = _matmul_dw_pair(a_me, a_sib, b_me, b_sib, core_idx, shards_on=shards_on,
                               name=name + "_pair")
        return _scatter_start(pair, name=name + "_start", cid=cid)

    def reduced(flight, after, name):
        pair, landed = _scatter_wait(flight, after, name=name + "_wait")
        return _chip_sum(pair, landed, slots, name=name + "_sum")

    (dy1_sent,), tok = _to_sibling_start([(dy1, 1)], name="dy1_to_sibling", cid=6)
    dy1 = dy1_sent[4]
    da1 = _matmul(dy1, w_out_s, mode="nt", out_dtype=F32, name="d_sgu_act", after=tok)
    fl_wo_s, tok = pair_gradient(acts[0], dy1_sent, [da1], "rows", "g_sgu_out", 15)
    dproj1, d_ws, d_bs_t, d_lg, d_lb = _sgu_bwd(da1, proj1, ln_gain, behind(ln_bias, tok), ws_masked, ws_masked_t,
                                                bs_t, name="sgu_gate_bwd")
    (dp1_sent,), tok = _to_sibling_start([(dproj1, N_CHIPS)], name="dproj1_to_sibling", cid=7)
    dproj1 = dp1_sent[4]
    dh1 = _matmul_nt_shards(dproj1, g_wi_s, out_dtype=F32, name="d_sgu_h", after=tok)
    fl_wi_s, tok = pair_gradient(acts[2], dp1_sent, [dh1], "cols", "g_sgu_in", 16)
    dx1, dy0, d_pre1, d_post0 = _mid_bwd(dx2, dh1, x1, behind(norm_pre[1:2], tok), y0, norm_post[0:1],
                                         name="pre1_post0_bwd")
    (dy0_sent,), tok = _to_sibling_start([(dy0, 1)], name="dy0_to_sibling", cid=8)
    dy0 = dy0_sent[4]
    da0 = _matmul(dy0, w_out_g, mode="nt", out_dtype=F32, name="d_gla_act", after=tok)
    fl_wo_g, tok = pair_gradient(acts[1], dy0_sent, [da0], "rows", "g_gla_out", 17)
    dproj0, d_og, d_bg, d_w2p = _gla_bwd(da0, o0, proj0, w2p, behind(gla_b_gate, tok), gla_o_gain, s_before, s_final,
                                         lay, name="gla_scan_bwd")
    early_shapes = [norm_post.shape, gla_b_gate.shape, gla_o_gain.shape, sgu_w_spatial.shape, sgu_b_spatial.shape,
                    (1, GLA_GATE_RANK, dk), (1, d), (1, d), (1, LANES)]
    early_part = _pack([jnp.concatenate([d_post0, d_post1], axis=0), d_bg, d_og, jnp.where(mask[None], d_ws, 0.0)[None],
                        d_bs_t.T[None], d_w2p[:GLA_GATE_RANK][None], d_lg, d_lb, loss_part])
    early_flight, tok = _dev_gather_start(early_part, name="small_early_start", cid=20)
    (dp0_sent,), tok_sent = _to_sibling_start([(dproj0, 0)], name="dproj0_to_sibling", cid=9)
    dproj0 = dp0_sent[4]
    dh0 = _matmul(dproj0, wt_g, mode="nn", out_dtype=F32, name="d_gla_h", after=tok_sent)
    a_me, a_sib = _from_sibling(dp0_sent, [dh0, tok], name="g_gla_in_a_wait")
    b_me, b_sib = _from_sibling(acts[3], [a_sib, dh0], name="g_gla_in_b_wait")
    fl_wi_g, tok_scatter = [], None
    for p in range(2):
        pair = _matmul_dw_pair(a_me, a_sib, b_me, b_sib, core_idx, shards_on="rows", part=(p, 2),
                               name=f"g_gla_in_pair{p}", after=tok_scatter)
        flight, tok_scatter = _scatter_start(pair, name=f"g_gla_in_start{p}", cid=18 + p)
        fl_wi_g.append(flight)
    r_wo_s = reduced(fl_wo_s, tok_scatter, "g_sgu_out")
    r_wi_s = reduced(fl_wi_s, r_wo_s, "g_sgu_in")
    r_wo_g = reduced(fl_wo_g, r_wi_s, "g_gla_out")
    sharing, tok = _share_start([r_wo_s, r_wi_s, r_wo_g], name="grads_share_a", cid=10)
    grad_x, d_pre0 = _first_bwd(dx1, dh0, x0, behind(norm_pre[0:1], tok), name="pre0_bwd")

    late_part = _pack([jnp.concatenate([d_pre0, d_pre1], axis=0)])
    late_flight, tok = _dev_gather_start(late_part, name="small_late_start", cid=21)

    def big_update(w, g, m, v, name, after=None):
        return [u[None] for u in _adamw(w[0], g, m[0], v[0], name=name, after=after)]

    g_wo_sgu, g_wi_sgu, g_wo_gla = _share_wait(sharing, [grad_x, tok], name="grads_share_a_wait")
    u_wi_sgu = big_update(sgu_w_in, g_wi_sgu, m_sgu_w_in, v_sgu_w_in, "adamw_sgu_w_in")
    u_wo_gla = big_update(gla_w_out, g_wo_gla, m_gla_w_out, v_gla_w_out, "adamw_gla_w_out", after=u_wi_sgu[1])

    r_wi_g, behind_this = None, u_wo_gla[1]
    for p, flight in enumerate(fl_wi_g):
        pair, landed = _scatter_wait(flight, behind_this, name=f"g_gla_in_wait{p}")
        r_wi_g = behind_this = _chip_sum(pair, landed, slots, part=(p, 2), into=r_wi_g, name=f"g_gla_in_sum{p}")
    gt_wi_gla, = _sibling_share_halves([r_wi_g], name="grads_share_b", cid=14)
    u_wi_gla_t = _adamw(wt_in_g, gt_wi_gla, mt_in_g, vt_in_g, name="adamw_gla_w_in")
    u_wi_gla = [u.T[None] for u in u_wi_gla_t]
    u_wo_sgu = big_update(sgu_w_out, g_wo_sgu, m_sgu_w_out, v_sgu_w_out, "adamw_sgu_w_out", after=u_wi_gla_t[1])

    def summed_over_devices(part, flight, after, shapes, name):
        land = _dev_gather_wait(flight, after, name=name + "_wait")
        every = lax.dynamic_update_slice(land, part[None], (2 * chip + core, 0, 0))
        return _unpack(_stack_sum(every, name=name + "_sum"), shapes)

    (g_post, g_bg, g_og, g_wsp, g_bsp, g_w2_full, g_lg_full, g_lb_full, loss_vec) = summed_over_devices(
        early_part, early_flight, u_wo_sgu[1], early_shapes, "small_early")
    g_pre, = summed_over_devices(late_part, late_flight, loss_vec, [norm_pre.shape], "small_late")
    loss = loss_vec[0, 0]
    g_w2 = lax.dynamic_slice_in_dim(g_w2_full, chip * (dk // N_CHIPS), dk // N_CHIPS, axis=2)
    g_lg = lax.dynamic_slice_in_dim(g_lg_full, chip * (d // N_CHIPS), d // N_CHIPS, axis=1)
    g_lb = lax.dynamic_slice_in_dim(g_lb_full, chip * (d // N_CHIPS), d // N_CHIPS, axis=1)

    small_w = [norm_pre, norm_post, gla_b_gate, gla_o_gain, sgu_w_spatial, sgu_b_spatial, gla_w_gate2, sgu_ln_gain,
               sgu_ln_bias]
    small_g = [g_pre, g_post, g_bg, g_og, g_wsp, g_bsp, g_w2, g_lg, g_lb]
    small_m = [m_norm_pre, m_norm_post, m_gla_b_gate, m_gla_o_gain, m_sgu_w_spatial, m_sgu_b_spatial, m_gla_w_gate2,
               m_sgu_ln_gain, m_sgu_ln_bias]
    small_v = [v_norm_pre, v_norm_post, v_gla_b_gate, v_gla_o_gain, v_sgu_w_spatial, v_sgu_b_spatial, v_gla_w_gate2,
               v_sgu_ln_gain, v_sgu_ln_bias]
    own_shapes = [w.shape for w in small_w]
    _, s_dl, s_m, s_v = _adamw(_pack(small_w), _pack(small_g), _pack(small_m), _pack(small_v), name="adamw_small")
    dl_s, m_s, v_s = _unpack(s_dl, own_shapes), _unpack(s_m, own_shapes), _unpack(s_v, own_shapes)

    def ordered(small, kind):
        pre, post, bg, og, wsp, bsp, w2, lg, lb = small
        return [pre, post, u_wi_gla[kind], w2, bg, og, u_wo_gla[kind], u_wi_sgu[kind], lg, lb, wsp, bsp, u_wo_sgu[kind]]

    return (loss, grad_x[None], *ordered(small_g, 0), *ordered(dl_s, 1), *ordered(m_s, 2), *ordered(v_s, 3))
```

```python
import math

import jax
import jax.numpy as jnp
from jax import lax
from jax.experimental import pallas as pl
from jax.experimental.pallas import tpu as pltpu

F32 = jnp.float32
BF16 = jnp.bfloat16
MESH = pl.DeviceIdType.MESH

EPS = 1e-6
CHUNK = 64
GLA_HEADS = 4
GLA_GATE_RANK = 16
GLA_TAU = 16.0
SGU_BLOCK = 128
SGU_GROUPS = 8
N_CHIPS = 4
N_DEV = 8
LANES = 128

ADAM_LR = 0.001
ADAM_B1 = 0.9
ADAM_B2 = 0.999
ADAM_EPS = 1e-08
ADAM_WD = 0.01
ADAM_STEP = 10

VMEM_LIMIT = 56 * 1024 * 1024


def _cparams(sem=None):
    return pltpu.CompilerParams(dimension_semantics=sem, vmem_limit_bytes=VMEM_LIMIT)


def _pick(n, cap, unit=LANES):
    best = None
    for t in range(unit, min(n, cap) + 1, unit):
        if n % t == 0:
            best = t
    assert best is not None, (n, cap, unit)
    return best


def _dot(a, b, dims):
    return lax.dot_general(a, b, (dims, ((), ())), preferred_element_type=F32)


def _dot_nn(a, b):
    return _dot(a, b, ((1,), (0,)))


def _dot_nt(a, b):
    return _dot(a, b, ((1,), (1,)))


def _dot_tn(a, b):
    return _dot(a, b, ((0,), (0,)))


def _matmul(a, b, *, mode, out_dtype, name, tm=1024, tn=512, b_shards=False, after=None):
    M, K = a.shape
    if b_shards:
        ns, Kb, bc = b.shape
        N, tn = ns * bc, _pick(bc, tn)
        per = bc // tn
        b_spec = pl.BlockSpec((None, K, tn), lambda i, j: (j // per, 0, j % per))
    elif mode == "nt":
        N, Kb = b.shape
        tn = _pick(N, tn)
        b_spec = pl.BlockSpec((tn, K), lambda i, j: (j, 0))
    else:
        Kb, N = b.shape
        tn = _pick(N, tn)
        b_spec = pl.BlockSpec((K, tn), lambda i, j: (0, j))
    assert K == Kb and a.dtype == b.dtype == BF16, (a.shape, b.shape, mode)
    tm = _pick(M, tm)
    dims = ((1,), (1,)) if mode == "nt" else ((1,), (0,))
    extra_specs, extra_args = ([], []) if after is None else ([pl.BlockSpec(memory_space=pl.ANY)], [after])

    def body(a_ref, b_ref, *rest):
        rest[-1][...] = _dot(a_ref[...], b_ref[...], dims).astype(out_dtype)

    return pl.pallas_call(
        body, name=name, grid=(M // tm, N // tn),
        in_specs=[pl.BlockSpec((tm, K), lambda i, j: (i, 0)), b_spec] + extra_specs,
        out_specs=pl.BlockSpec((tm, tn), lambda i, j: (i, j)), out_shape=jax.ShapeDtypeStruct((M, N), out_dtype),
        compiler_params=_cparams(("parallel", "parallel")),
    )(a, b, *extra_args)


def _matmul_nt_shards(a, b, *, out_dtype, name, tm=1024, tn=512, after=None):
    M, K = a.shape
    ns, N, kc = b.shape
    assert K == ns * kc
    tm, tn = _pick(M, tm), _pick(N, tn)

    def body(a_ref, *rest):
        b_refs, o_ref = rest[:ns], rest[ns + (after is not None)]
        acc = _dot_nt(a_ref[:, 0:kc], b_refs[0][...])
        for j in range(1, ns):
            acc += _dot_nt(a_ref[:, j * kc:(j + 1) * kc], b_refs[j][...])
        o_ref[...] = acc.astype(out_dtype)

    def shard(j):
        return pl.BlockSpec((None, tn, kc), lambda i, n: (j, n, 0))

    extra_specs, extra_args = ([], []) if after is None else ([pl.BlockSpec(memory_space=pl.ANY)], [after])
    return pl.pallas_call(
        body, name=name, grid=(M // tm, N // tn),
        in_specs=[pl.BlockSpec((tm, K), lambda i, n: (i, 0))] + [shard(j) for j in range(ns)] + extra_specs,
        out_specs=pl.BlockSpec((tm, tn), lambda i, n: (i, n)), out_shape=jax.ShapeDtypeStruct((M, N), out_dtype),
        compiler_params=_cparams(("parallel", "parallel")),
    )(a, *([b] * ns), *extra_args)


def _rstd(x):
    return lax.rsqrt(jnp.mean(x * x, axis=-1, keepdims=True) + EPS)


def _row_spec(tr, d):
    return pl.BlockSpec((tr, d), lambda i: (i, 0))


def _vec_spec(d):
    return pl.BlockSpec((1, d), lambda i: (0, 0))


def _acc_rows(ref, i, val, cols=slice(None)):
    @pl.when(i == 0)
    def _():
        ref[:, cols] = val

    @pl.when(i > 0)
    def _():
        ref[:, cols] += val


def _norm_pre(x, gain, *, name, tr=256):
    t, d = x.shape
    tr = _pick(t, tr, 8)

    def body(x_ref, g_ref, h_ref):
        xv = x_ref[...]
        h_ref[...] = (xv * _rstd(xv) * g_ref[...]).astype(BF16)

    return pl.pallas_call(
        body, name=name, grid=(t // tr,), in_specs=[_row_spec(tr, d), _vec_spec(d)], out_specs=_row_spec(tr, d),
        out_shape=jax.ShapeDtypeStruct((t, d), BF16), compiler_params=_cparams(("parallel",)),
    )(x, gain)


def _post_then_pre(x, y, post_gain, pre_gain, *, name, tr=256):
    t, d = x.shape
    tr = _pick(t, tr, 8)

    def body(x_ref, y_ref, pg_ref, ng_ref, xn_ref, h_ref):
        yv = y_ref[...]
        xn = x_ref[...] + yv * _rstd(yv) * pg_ref[...]
        xn_ref[...] = xn
        h_ref[...] = (xn * _rstd(xn) * ng_ref[...]).astype(BF16)

    return pl.pallas_call(
        body, name=name, grid=(t // tr,),
        in_specs=[_row_spec(tr, d), _row_spec(tr, d), _vec_spec(d), _vec_spec(d)],
        out_specs=[_row_spec(tr, d), _row_spec(tr, d)],
        out_shape=[jax.ShapeDtypeStruct((t, d), F32), jax.ShapeDtypeStruct((t, d), BF16)],
        compiler_params=_cparams(("parallel",)),
    )(x, y, post_gain, pre_gain)


def _norm_bwd(dy, n, r, gain):
    dn = dy * gain
    return r * (dn - n * jnp.mean(dn * n, axis=-1, keepdims=True))


def _loss_head(x, y, post_gain, target, *, name, tr=256):
    t, d = x.shape
    tr = _pick(t, tr, 8)

    def body(x_ref, y_ref, pg_ref, t_ref, loss_ref, dx_ref, dy_ref, dpg_ref):
        i = pl.program_id(0)
        yv = y_ref[...]
        r = _rstd(yv)
        n = yv * r
        err = x_ref[...] + n * pg_ref[...] - t_ref[...]
        dx = err * (1.0 / d)
        dx_ref[...] = dx
        part = 0.5 * jnp.sum(jnp.mean(err * err, axis=-1, keepdims=True), axis=0, keepdims=True)
        _acc_rows(loss_ref, i, jnp.broadcast_to(part, (1, LANES)))
        _acc_rows(dpg_ref, i, jnp.sum(dx * n, axis=0, keepdims=True))
        dy_ref[...] = _norm_bwd(dx, n, r, pg_ref[...]).astype(BF16)

    return pl.pallas_call(
        body, name=name, grid=(t // tr,),
        in_specs=[_row_spec(tr, d), _row_spec(tr, d), _vec_spec(d), _row_spec(tr, d)],
        out_specs=[_vec_spec(LANES), _row_spec(tr, d), _row_spec(tr, d), _vec_spec(d)],
        out_shape=[jax.ShapeDtypeStruct((1, LANES), F32), jax.ShapeDtypeStruct((t, d), F32),
                   jax.ShapeDtypeStruct((t, d), BF16), jax.ShapeDtypeStruct((1, d), F32)],
        compiler_params=_cparams(("arbitrary",)),
    )(x, y, post_gain, target)


def _mid_bwd(dx_out, dh, x, pre_gain, y_prev, post_gain_prev, *, name, tr=256):
    t, d = x.shape
    tr = _pick(t, tr, 8)

    def body(dxo_ref, dh_ref, x_ref, ng_ref, y_ref, pg_ref, dx_ref, dy_ref, dng_ref, dpg_ref):
        i = pl.program_id(0)
        xv = x_ref[...]
        r = _rstd(xv)
        xh = xv * r
        dhv = dh_ref[...]
        _acc_rows(dng_ref, i, jnp.sum(dhv * xh, axis=0, keepdims=True))
        dx = dxo_ref[...] + _norm_bwd(dhv, xh, r, ng_ref[...])
        dx_ref[...] = dx
        yv = y_ref[...]
        ry = _rstd(yv)
        n = yv * ry
        _acc_rows(dpg_ref, i, jnp.sum(dx * n, axis=0, keepdims=True))
        dy_ref[...] = _norm_bwd(dx, n, ry, pg_ref[...]).astype(BF16)

    return pl.pallas_call(
        body, name=name, grid=(t // tr,),
        in_specs=[_row_spec(tr, d), _row_spec(tr, d), _row_spec(tr, d), _vec_spec(d), _row_spec(tr, d), _vec_spec(d)],
        out_specs=[_row_spec(tr, d), _row_spec(tr, d), _vec_spec(d), _vec_spec(d)],
        out_shape=[jax.ShapeDtypeStruct((t, d), F32), jax.ShapeDtypeStruct((t, d), BF16),
                   jax.ShapeDtypeStruct((1, d), F32), jax.ShapeDtypeStruct((1, d), F32)],
        compiler_params=_cparams(("arbitrary",)),
    )(dx_out, dh, x, pre_gain, y_prev, post_gain_prev)


def _first_bwd(dx_out, dh, x, pre_gain, *, name, tr=256):
    t, d = x.shape
    tr = _pick(t, tr, 8)

    def body(dxo_ref, dh_ref, x_ref, ng_ref, dx_ref, dng_ref):
        i = pl.program_id(0)
        xv = x_ref[...]
        r = _rstd(xv)
        xh = xv * r
        dhv = dh_ref[...]
        _acc_rows(dng_ref, i, jnp.sum(dhv * xh, axis=0, keepdims=True))
        dx_ref[...] = dxo_ref[...] + _norm_bwd(dhv, xh, r, ng_ref[...])

    return pl.pallas_call(
        body, name=name, grid=(t // tr,),
        in_specs=[_row_spec(tr, d), _row_spec(tr, d), _row_spec(tr, d), _vec_spec(d)],
        out_specs=[_row_spec(tr, d), _vec_spec(d)],
        out_shape=[jax.ShapeDtypeStruct((t, d), F32), jax.ShapeDtypeStruct((1, d), F32)],
        compiler_params=_cparams(("arbitrary",)),
    )(dx_out, dh, x, pre_gain)


def _sigmoid(x):
    return 1.0 / (1.0 + jnp.exp(-x))


def _log_sigmoid(x):
    return jnp.minimum(x, 0.0) - jnp.log(1.0 + jnp.exp(-jnp.abs(x)))


_GELU_C = math.sqrt(2.0 / math.pi)


_GELU_A = 0.044715


def _gelu_parts(x, with_grad=True):
    x2 = x * x
    h = 0.5 * jnp.tanh(x * (_GELU_C + (_GELU_C * _GELU_A) * x2)) + 0.5
    val = x * h
    if not with_grad:
        return val, None
    return val, h * (1.0 + (1.0 - h) * (x * (2.0 * _GELU_C + (6.0 * _GELU_C * _GELU_A) * x2)))


def _split3(x):
    hi = x.astype(BF16)
    r1 = x - hi.astype(F32)
    mid = r1.astype(BF16)
    lo = (r1 - mid.astype(F32)).astype(BF16)
    return hi, mid, lo


def _tri_matmul(tri_bf16, x):
    hi, mid, lo = _split3(x)
    return _dot_nn(tri_bf16, hi) + _dot_nn(tri_bf16, mid) + _dot_nn(tri_bf16, lo)


def _gla_dims(d):
    dk, dv = d // 2, d
    return dk, dv, dk // GLA_HEADS, dv // GLA_HEADS


def _col_pieces(a, b, lay):
    ws, wp = lay
    out = []
    while a < b:
        j = a // ws
        end = min(b, (j + 1) * ws)
        out.append((j * wp + a - j * ws, end - a))
        a = end
    return out


def _load_cols(ref, a, b, lay):
    parts = [ref[:, s:s + n] for s, n in _col_pieces(a, b, lay)]
    return parts[0] if len(parts) == 1 else jnp.concatenate(parts, axis=1)


def _store_cols(ref, a, val, lay):
    off = 0
    for s, n in _col_pieces(a, a + val.shape[1], lay):
        ref[:, s:s + n] = val[:, off:off + n]
        off += n


def _gate_window(c_r, lay):
    (start, _), = _col_pieces(c_r, c_r + GLA_GATE_RANK, lay)
    assert (start % lay[1]) + LANES <= lay[1]
    return slice(start, start + LANES)


def _gla_gates(glr, k, w2_ref, b_ref):
    z = _dot_nn(glr.astype(BF16), w2_ref[...].astype(BF16)) + b_ref[...]
    la = _log_sigmoid(z) * (1.0 / GLA_TAU)
    row = lax.broadcasted_iota(jnp.int32, (CHUNK, CHUNK), 0)
    col = lax.broadcasted_iota(jnp.int32, (CHUNK, CHUNK), 1)
    incl = (row >= col).astype(BF16)
    bcum = _tri_matmul(incl, la)
    b_end = bcum[CHUNK - 1:CHUNK, :]
    e_rest = jnp.exp(b_end - bcum)
    return z, e_rest, k * e_rest, jnp.exp(b_end)


def _gla_fwd(proj, w2p, b_gate, o_gain, lay, *, name):
    t, wcols = proj.shape
    d = o_gain.shape[1]
    dk, dv, dkh, dvh = _gla_dims(d)
    nc = t // CHUNK
    c_k, c_v, c_g, c_r = dk, 2 * dk, 2 * dk + dv, 2 * dk + 2 * dv
    scale = dkh ** -0.5

    def body(p_ref, w2_ref, b_ref, og_ref, o_ref, a_ref, sb_ref, sfin_ref, s_ref):
        i = pl.program_id(0)

        @pl.when(i == 0)
        def _():
            s_ref[...] = jnp.zeros_like(s_ref)

        q = _load_cols(p_ref, 0, dk, lay) * scale
        k = _load_cols(p_ref, c_k, c_k + dk, lay)
        glr = p_ref[:, _gate_window(c_r, lay)]
        _, _, kdec, decay = _gla_gates(glr, k, w2_ref, b_ref)
        for h in range(GLA_HEADS):
            ks = slice(h * dkh, (h + 1) * dkh)
            vs = slice(h * dvh, (h + 1) * dvh)
            v_h = _load_cols(p_ref, c_v + h * dvh, c_v + (h + 1) * dvh, lay)
            g_h = _load_cols(p_ref, c_g + h * dvh, c_g + (h + 1) * dvh, lay)
            s_old = s_ref[h]
            sb_ref[0, h] = s_old
            s_new = s_old * decay[:, ks] + _dot_tn(v_h.astype(BF16), kdec[:, ks].astype(BF16))
            s_ref[h] = s_new
            o_h = _dot_nt(q[:, ks].astype(BF16), s_new.astype(BF16))
            o_ref[:, vs] = o_h
            on = o_h * _rstd(o_h)
            a_ref[:, vs] = (on * og_ref[:, vs] * (g_h * _sigmoid(g_h))).astype(BF16)

        @pl.when(i == nc - 1)
        def _():
            sfin_ref[...] = s_ref[...]

    full = lambda *shape: pl.BlockSpec(shape, lambda i: (0,) * len(shape))
    return pl.pallas_call(
        body, name=name, grid=(nc,),
        in_specs=[pl.BlockSpec((CHUNK, wcols), lambda i: (i, 0)), full(LANES, dk), full(1, dk), full(1, dv)],
        out_specs=[pl.BlockSpec((CHUNK, dv), lambda i: (i, 0)), pl.BlockSpec((CHUNK, dv), lambda i: (i, 0)),
                   pl.BlockSpec((1, GLA_HEADS, dvh, dkh), lambda i: (i, 0, 0, 0)), full(GLA_HEADS, dvh, dkh)],
        out_shape=[jax.ShapeDtypeStruct((t, dv), F32), jax.ShapeDtypeStruct((t, dv), BF16),
                   jax.ShapeDtypeStruct((nc, GLA_HEADS, dvh, dkh), F32),
                   jax.ShapeDtypeStruct((GLA_HEADS, dvh, dkh), F32)],
        scratch_shapes=[pltpu.VMEM((GLA_HEADS, dvh, dkh), F32)],
        compiler_params=_cparams(("arbitrary",)),
    )(proj, w2p, b_gate, o_gain)


def _gla_bwd(da, o, proj, w2p, b_gate, o_gain, s_before, s_final, lay, *, name):
    t, wcols = proj.shape
    d = o_gain.shape[1]
    dk, dv, dkh, dvh = _gla_dims(d)
    nc = t // CHUNK
    c_k, c_v, c_g, c_r = dk, 2 * dk, 2 * dk + dv, 2 * dk + 2 * dv
    scale = dkh ** -0.5

    def body(da_ref, o_ref, p_ref, w2_ref, b_ref, og_ref, sb_ref, sfin_ref,
             dp_ref, dog_ref, db_ref, dw2_ref, s_ref, gc_ref, dkd_ref):
        i = pl.program_id(0)

        @pl.when(i == 0)
        def _():
            s_ref[...] = sfin_ref[...]
            gc_ref[...] = jnp.zeros_like(gc_ref)

        ws, wp = lay
        for j in range(N_CHIPS):
            dp_ref[:, j * wp + ws:(j + 1) * wp] = jnp.zeros((CHUNK, wp - ws), BF16)
        q = _load_cols(p_ref, 0, dk, lay) * scale
        k = _load_cols(p_ref, c_k, c_k + dk, lay)
        glr = p_ref[:, _gate_window(c_r, lay)]
        z, e_rest, kdec, decay = _gla_gates(glr, k, w2_ref, b_ref)
        ddecay = []
        for h in range(GLA_HEADS):
            ks = slice(h * dkh, (h + 1) * dkh)
            vs = slice(h * dvh, (h + 1) * dvh)
            v_h = _load_cols(p_ref, c_v + h * dvh, c_v + (h + 1) * dvh, lay)
            g_h = _load_cols(p_ref, c_g + h * dvh, c_g + (h + 1) * dvh, lay)
            da_h = da_ref[:, vs]
            o_h = o_ref[:, vs]
            og_h = og_ref[:, vs]
            r = _rstd(o_h)
            on = o_h * r
            sg = _sigmoid(g_h)
            silu = g_h * sg
            _acc_rows(dog_ref, i, jnp.sum(da_h * silu * on, axis=0, keepdims=True), vs)
            _store_cols(dp_ref, c_g + h * dvh, (da_h * (on * og_h) * (sg * (1.0 + g_h * (1.0 - sg)))).astype(BF16),
                        lay)
            don = da_h * silu * og_h
            do_h = (r * (don - on * jnp.mean(don * on, axis=-1, keepdims=True))).astype(BF16)
            s_cur = s_ref[h]
            _store_cols(dp_ref, h * dkh, (_dot_nn(do_h, s_cur.astype(BF16)) * scale).astype(BF16), lay)
            g_tot = gc_ref[h] + _dot_tn(do_h, q[:, ks].astype(BF16))
            g_bf = g_tot.astype(BF16)
            dkd_ref[:, ks] = _dot_nn(v_h.astype(BF16), g_bf)
            _store_cols(dp_ref, c_v + h * dvh, _dot_nt(kdec[:, ks].astype(BF16), g_bf).astype(BF16), lay)
            s_prev = sb_ref[0, h]
            ddecay.append(jnp.sum(g_tot * s_prev, axis=0, keepdims=True))
            gc_ref[h] = g_tot * decay[:, ks]
            s_ref[h] = s_prev
        dkdec = dkd_ref[...]
        _store_cols(dp_ref, c_k, (dkdec * e_rest).astype(BF16), lay)
        d_e = dkdec * kdec
        row = lax.broadcasted_iota(jnp.int32, (CHUNK, CHUNK), 0)
        col = lax.broadcasted_iota(jnp.int32, (CHUNK, CHUNK), 1)
        excl = (row > col).astype(BF16)
        dla = jnp.concatenate(ddecay, axis=1) * decay + _tri_matmul(excl, d_e)
        dz = dla * (1.0 / GLA_TAU) * (1.0 - _sigmoid(z))
        _acc_rows(db_ref, i, jnp.sum(dz, axis=0, keepdims=True))
        dz_bf = dz.astype(BF16)
        dw2 = _dot_tn(glr.astype(BF16), dz_bf)

        @pl.when(i == 0)
        def _():
            dw2_ref[...] = dw2

        @pl.when(i > 0)
        def _():
            dw2_ref[...] += dw2

        dp_ref[:, _gate_window(c_r, lay)] = _dot_nt(dz_bf, w2_ref[...].astype(BF16)).astype(BF16)

    rev = lambda i: (nc - 1 - i, 0)
    full = lambda *shape: pl.BlockSpec(shape, lambda i: (0,) * len(shape))
    return pl.pallas_call(
        body, name=name, grid=(nc,),
        in_specs=[pl.BlockSpec((CHUNK, dv), rev), pl.BlockSpec((CHUNK, dv), rev), pl.BlockSpec((CHUNK, wcols), rev),
                  full(LANES, dk), full(1, dk), full(1, dv),
                  pl.BlockSpec((1, GLA_HEADS, dvh, dkh), lambda i: (nc - 1 - i, 0, 0, 0)), full(GLA_HEADS, dvh, dkh)],
        out_specs=[pl.BlockSpec((CHUNK, wcols), rev), full(1, dv), full(1, dk), full(LANES, dk)],
        out_shape=[jax.ShapeDtypeStruct((t, wcols), BF16), jax.ShapeDtypeStruct((1, dv), F32),
                   jax.ShapeDtypeStruct((1, dk), F32), jax.ShapeDtypeStruct((LANES, dk), F32)],
        scratch_shapes=[pltpu.VMEM((GLA_HEADS, dvh, dkh), F32), pltpu.VMEM((GLA_HEADS, dvh, dkh), F32),
                        pltpu.VMEM((CHUNK, dk), F32)],
        compiler_params=_cparams(("arbitrary",)),
    )(da, o, proj, w2p, b_gate, o_gain, s_before, s_final)


def _sgu_mid(p_ref, lg_ref, lb_ref, ws_ref, bst_ref, w, with_grad=True):
    gd = w // SGU_GROUPS
    u_act, du_fac = _gelu_parts(p_ref[:, 0:w], with_grad)
    vf, dv_fac = _gelu_parts(p_ref[:, w:2 * w], with_grad)
    mu = jnp.mean(vf, axis=-1, keepdims=True)
    cen = vf - mu
    rstd = lax.rsqrt(jnp.mean(cen * cen, axis=-1, keepdims=True) + EPS)
    xh = cen * rstd
    vn = (xh * lg_ref[...] + lb_ref[...]).astype(BF16)
    vs = [_dot_nn(ws_ref[g].astype(BF16), vn[:, g * gd:(g + 1) * gd]) + bst_ref[:, g:g + 1]
          for g in range(SGU_GROUPS)]
    return u_act, du_fac, dv_fac, rstd, xh, vn, vs


def _sgu_fwd(proj, ln_gain, ln_bias, ws_masked, bs_t, *, name):
    t, w3 = proj.shape
    w = w3 // 3
    gd = w // SGU_GROUPS
    nb = t // SGU_BLOCK

    def body(p_ref, lg_ref, lb_ref, ws_ref, bst_ref, a_ref):
        u_act, _, _, _, _, _, vs = _sgu_mid(p_ref, lg_ref, lb_ref, ws_ref, bst_ref, w, with_grad=False)
        for g in range(SGU_GROUPS):
            cs = slice(g * gd, (g + 1) * gd)
            gate = p_ref[:, 2 * w + g * gd:2 * w + (g + 1) * gd]
            a_ref[:, cs] = (u_act[:, cs] * vs[g] * (gate * _sigmoid(gate))).astype(BF16)

    full = lambda *shape: pl.BlockSpec(shape, lambda i: (0,) * len(shape))
    return pl.pallas_call(
        body, name=name, grid=(nb,),
        in_specs=[pl.BlockSpec((SGU_BLOCK, w3), lambda i: (i, 0)), full(1, w), full(1, w),
                  full(SGU_GROUPS, SGU_BLOCK, SGU_BLOCK), full(SGU_BLOCK, SGU_GROUPS)],
        out_specs=pl.BlockSpec((SGU_BLOCK, w), lambda i: (i, 0)),
        out_shape=jax.ShapeDtypeStruct((t, w), BF16),
        compiler_params=_cparams(("parallel",)),
    )(proj, ln_gain, ln_bias, ws_masked, bs_t)


def _sgu_bwd(da, proj, ln_gain, ln_bias, ws_masked, ws_masked_t, bs_t, *, name):
    t, w3 = proj.shape
    w = w3 // 3
    gd = w // SGU_GROUPS
    nb = t // SGU_BLOCK

    def body(da_ref, p_ref, lg_ref, lb_ref, ws_ref, wst_ref, bst_ref, dp_ref, dws_ref, dbst_ref, dlg_ref, dlb_ref,
             dvn_ref):
        i = pl.program_id(0)
        u_act, du_fac, dv_fac, rstd, xh, vn, vs = _sgu_mid(p_ref, lg_ref, lb_ref, ws_ref, bst_ref, w)
        for g in range(SGU_GROUPS):
            cs = slice(g * gd, (g + 1) * gd)
            gate = p_ref[:, 2 * w + g * gd:2 * w + (g + 1) * gd]
            sg = _sigmoid(gate)
            silu = gate * sg
            da_g = da_ref[:, cs]
            ua_g = u_act[:, cs]
            dp_ref[:, cs] = (da_g * vs[g] * silu * du_fac[:, cs]).astype(BF16)
            dp_ref[:, 2 * w + g * gd:2 * w + (g + 1) * gd] = (
                da_g * ua_g * vs[g] * (sg * (1.0 + gate * (1.0 - sg)))).astype(BF16)
            dvs = da_g * ua_g * silu
            dvs_bf = dvs.astype(BF16)
            dvn_ref[:, cs] = _dot_nn(wst_ref[g].astype(BF16), dvs_bf)
            dws = _dot_nt(dvs_bf, vn[:, cs])
            dbs = jnp.sum(dvs, axis=1, keepdims=True)

            @pl.when(i == 0)
            def _():
                dws_ref[g] = dws
                dbst_ref[:, g:g + 1] = dbs

            @pl.when(i > 0)
            def _():
                dws_ref[g] += dws
                dbst_ref[:, g:g + 1] += dbs

        dvn = dvn_ref[...]
        _acc_rows(dlg_ref, i, jnp.sum(dvn * xh, axis=0, keepdims=True))
        _acc_rows(dlb_ref, i, jnp.sum(dvn, axis=0, keepdims=True))
        dxh = dvn * lg_ref[...]
        dvf = rstd * (dxh - jnp.mean(dxh, axis=-1, keepdims=True)
                      - xh * jnp.mean(dxh * xh, axis=-1, keepdims=True))
        dp_ref[:, w:2 * w] = (dvf * dv_fac).astype(BF16)

    full = lambda *shape: pl.BlockSpec(shape, lambda i: (0,) * len(shape))
    return pl.pallas_call(
        body, name=name, grid=(nb,),
        in_specs=[pl.BlockSpec((SGU_BLOCK, w), lambda i: (i, 0)), pl.BlockSpec((SGU_BLOCK, w3), lambda i: (i, 0)),
                  full(1, w), full(1, w), full(SGU_GROUPS, SGU_BLOCK, SGU_BLOCK),
                  full(SGU_GROUPS, SGU_BLOCK, SGU_BLOCK), full(SGU_BLOCK, SGU_GROUPS)],
        out_specs=[pl.BlockSpec((SGU_BLOCK, w3), lambda i: (i, 0)), full(SGU_GROUPS, SGU_BLOCK, SGU_BLOCK),
                   full(SGU_BLOCK, SGU_GROUPS), full(1, w), full(1, w)],
        out_shape=[jax.ShapeDtypeStruct((t, w3), BF16), jax.ShapeDtypeStruct((SGU_GROUPS, SGU_BLOCK, SGU_BLOCK), F32),
                   jax.ShapeDtypeStruct((SGU_BLOCK, SGU_GROUPS), F32), jax.ShapeDtypeStruct((1, w), F32),
                   jax.ShapeDtypeStruct((1, w), F32)],
        scratch_shapes=[pltpu.VMEM((SGU_BLOCK, w), F32)],
        compiler_params=_cparams(("arbitrary",)),
    )(da, proj, ln_gain, ln_bias, ws_masked, ws_masked_t, bs_t)


def _tile2d(rows, cols, block_bytes, row_unit):
    if rows % row_unit == 0:
        return _pick(rows, max(row_unit, block_bytes // (4 * cols)), row_unit), cols
    return rows, _pick(cols, max(LANES, block_bytes // (4 * rows)))


def _adamw(w, g, m, v, *, name, block_bytes=1 << 20, after=None):
    rows, cols = w.shape
    tr, tc = _tile2d(rows, cols, block_bytes, 8)
    g_rows = g.shape[0]
    assert g_rows == rows or tr == rows
    extra_specs, extra_args = ([], []) if after is None else ([pl.BlockSpec(memory_space=pl.ANY)], [after])

    def body(w_ref, g_ref, m_ref, v_ref, *rest):
        go_ref, d_ref, mo_ref, vo_ref = rest[len(extra_args):]
        gv = g_ref[0:tr, :]
        go_ref[...] = gv
        mn = ADAM_B1 * m_ref[...] + (1.0 - ADAM_B1) * gv
        vn = ADAM_B2 * v_ref[...] + (1.0 - ADAM_B2) * (gv * gv)
        m_hat = mn / (1.0 - ADAM_B1 ** ADAM_STEP)
        v_hat = vn / (1.0 - ADAM_B2 ** ADAM_STEP)
        d_ref[...] = -ADAM_LR * (m_hat / (jnp.sqrt(v_hat) + ADAM_EPS) + ADAM_WD * w_ref[...])
        mo_ref[...] = mn
        vo_ref[...] = vn

    spec = pl.BlockSpec((tr, tc), lambda i, j: (i, j))
    g_spec = spec if g_rows == rows else pl.BlockSpec((g_rows, tc), lambda i, j: (0, j))
    return pl.pallas_call(
        body, name=name, grid=(rows // tr, cols // tc), in_specs=[spec, g_spec, spec, spec] + extra_specs,
        out_specs=[spec] * 4, out_shape=[jax.ShapeDtypeStruct((rows, cols), F32)] * 4,
        compiler_params=_cparams(("parallel", "parallel")),
    )(w, g, m, v, *extra_args)


def _matmul_dw_pair(a_me, a_sib, b_me, b_sib, core_idx, *, shards_on, name, after=None, part=(0, 1)):
    T, M = a_me.shape
    N = b_me.shape[1]
    if shards_on == "rows":
        p, count = part
        tm, hc = M // N_CHIPS, N // 2
        hp = hc // count
        tn = _pick(hp, 512)
        per = hp // tn
        grid = (N_CHIPS, per)
        a_spec = pl.BlockSpec((T, tm), lambda i, n, h: (0, i))
        b_me_spec = pl.BlockSpec((T, tn), lambda i, n, h: (0, (h[0] * count + p) * per + n))
        b_sib_spec = pl.BlockSpec((T, tn), lambda i, n, h: (0, p * per + n))
        out_spec = pl.BlockSpec((None, tm, tn), lambda i, n, h: (i, 0, n))
        out_shape = jax.ShapeDtypeStruct((N_CHIPS, tm, hp), BF16)
    else:
        tm, hc = _pick(M, 1024), N // N_CHIPS // 2
        grid = (M // tm, N_CHIPS)
        a_spec = pl.BlockSpec((T, tm), lambda i, j, h: (0, i))
        b_me_spec = pl.BlockSpec((T, hc), lambda i, j, h: (0, 2 * j + h[0]))
        b_sib_spec = pl.BlockSpec((T, hc), lambda i, j, h: (0, j))
        out_spec = pl.BlockSpec((None, tm, hc), lambda i, j, h: (j, i, 0))
        out_shape = jax.ShapeDtypeStruct((N_CHIPS, M, hc), BF16)
    extra_specs, extra_args = ([], []) if after is None else ([pl.BlockSpec(memory_space=pl.ANY)], [after])

    def body(h_ref, am_ref, as_ref, bm_ref, bs_ref, *rest):
        o_ref = rest[len(extra_args)]
        o_ref[...] = (_dot_tn(am_ref[...], bm_ref[...]) + _dot_tn(as_ref[...], bs_ref[...])).astype(BF16)

    grid_spec = pltpu.PrefetchScalarGridSpec(
        num_scalar_prefetch=1, grid=grid, in_specs=[a_spec, a_spec, b_me_spec, b_sib_spec] + extra_specs,
        out_specs=out_spec)
    return pl.pallas_call(
        body, name=name, grid_spec=grid_spec, out_shape=out_shape, compiler_params=_cparams(("parallel", "parallel")),
    )(core_idx, a_me, a_sib, b_me, b_sib, *extra_args)


def _chip_sum(pair, landed, slots, *, name, block_bytes=1 << 20, part=(0, 1), into=None):
    p, count = part
    _, r, hp = pair.shape
    tr, tc = _tile2d(r, hp, block_bytes, 16)
    ncb = hp // tc
    extra_specs, extra_args = ([], []) if into is None else ([pl.BlockSpec(memory_space=pl.ANY)], [into])

    def body(s_ref, own_ref, l0_ref, l1_ref, l2_ref, *rest):
        rest[-1][...] = ((own_ref[...].astype(F32) + l0_ref[...].astype(F32)) + l1_ref[...].astype(F32)
                         ) + l2_ref[...].astype(F32)

    def slab(which):
        return pl.BlockSpec((None, tr, tc), lambda i, k, s: (s[which], i, k))

    grid_spec = pltpu.PrefetchScalarGridSpec(
        num_scalar_prefetch=1, grid=(r // tr, ncb),
        in_specs=[slab(0), slab(1), slab(2), slab(3)] + extra_specs,
        out_specs=pl.BlockSpec((tr, tc), lambda i, k, s: (i, (s[4] * count + p) * ncb + k)))
    return pl.pallas_call(
        body, name=name, grid_spec=grid_spec, out_shape=jax.ShapeDtypeStruct((r, 2 * hp * count), F32),
        input_output_aliases={} if into is None else {5: 0},
        compiler_params=_cparams(("parallel", "parallel")),
    )(slots, pair, landed, landed, landed, *extra_args)


def _stack_sum(x, *, name, out_dtype=F32, block_bytes=1 << 20):
    s, r, c = x.shape
    tr = _pick(r, max(8, block_bytes // (4 * c)), 16) if r % 16 == 0 else r

    def body(x_ref, o_ref):
        acc = x_ref[0].astype(F32)
        for j in range(1, s):
            acc = acc + x_ref[j].astype(F32)
        o_ref[...] = acc.astype(out_dtype)

    return pl.pallas_call(
        body, name=name, grid=(r // tr,),
        in_specs=[pl.BlockSpec((s, tr, c), lambda i: (0, i, 0))], out_specs=pl.BlockSpec((tr, c), lambda i: (i, 0)),
        out_shape=jax.ShapeDtypeStruct((r, c), out_dtype), compiler_params=_cparams(("parallel",)),
    )(x)


HBM = pl.BlockSpec(memory_space=pltpu.HBM)


def _place():
    x, y, c = lax.axis_index("x"), lax.axis_index("y"), lax.axis_index("c")
    other_chips = [(1 - x, y), (x, 1 - y), (1 - x, 1 - y)]
    return x, y, c, other_chips


def _handshake(peers):
    barrier = pltpu.get_barrier_semaphore()
    for peer in peers:
        pl.semaphore_signal(barrier, inc=1, device_id=peer, device_id_type=MESH)
    pl.semaphore_wait(barrier, len(peers))


def _sibling():
    x, y, c, _ = _place()
    return [(x, y, 1 - c)]


def _same_core_chips():
    x, y, c, chips = _place()
    return [(cx, cy, c) for cx, cy in chips]


def _same_core_neighbours():
    x, y, c, _ = _place()
    return [(1 - x, y, c), (x, 1 - y, c)]


def _split_params(cid):
    return pltpu.CompilerParams(has_side_effects=SIDE_EFFECT, collective_id=cid)


def _half_cols(cols, which):
    hc = cols // 2
    return pl.ds(pl.multiple_of(which * hc, LANES), hc)


SEM = pl.BlockSpec(memory_space=pltpu.SEMAPHORE)
ANY = pl.BlockSpec(memory_space=pl.ANY)
SIDE_EFFECT = pltpu.SideEffectType.DATAFLOW_SIDE_EFFECTING
TOKEN_SHAPE = (8, LANES)


def _hbm(shape, dtype):
    return pltpu.HBM(shape, dtype)


def _in_hbm(a):
    return pltpu.with_memory_space_constraint(a, pltpu.HBM)


def _gather_copy(src_ref, land_ref, ssem, rsem, k, chip_of_block, to, c):
    cols = src_ref.shape[1]
    return pltpu.make_async_remote_copy(
        src_ref=src_ref.at[:, _half_cols(cols, c)], dst_ref=land_ref.at[chip_of_block, :, _half_cols(cols, c)],
        send_sem=ssem.at[k], recv_sem=rsem.at[k], device_id=to, device_id_type=MESH)


NEIGHBOURS = (0, 1)
ALL_CHIPS = (0, 1, 2)


def _gather_start(shards, *, name, cid, after=(), relayed=(), own_slab=None):
    n = len(shards)
    after = list(after)

    def body(*refs):
        srcs, lands = refs[:n], refs[n:2 * n]
        outs = refs[2 * n + len(after):]
        token = outs[-1]
        _handshake(_same_core_chips())
        x, y, c, chips = _place()
        me = 2 * x + y
        for a in range(n):
            ssem, rsem = outs[4 * a], outs[4 * a + 1]
            for k in NEIGHBOURS if a in relayed else ALL_CHIPS:
                cx, cy = chips[k]
                _gather_copy(srcs[a], lands[a], ssem, rsem, k, me, (cx, cy, c), c).start()
        token[...] = jnp.zeros_like(token)

    out_shape, out_specs, aliases = [], [], {}
    for a, s in enumerate(shards):
        out_shape += [pltpu.SemaphoreType.DMA((3,)), pltpu.SemaphoreType.DMA((3,)), _hbm(s.shape, s.dtype),
                      _hbm((N_CHIPS,) + s.shape, s.dtype)]
        out_specs += [SEM, SEM, HBM, HBM]
        aliases[a] = 4 * a + 2
        aliases[n + a] = 4 * a + 3
    out_shape.append(jax.ShapeDtypeStruct(TOKEN_SHAPE, F32))
    out_specs.append(pl.BlockSpec(memory_space=pltpu.VMEM))
    lands = [lax.empty((N_CHIPS,) + s.shape, s.dtype) for s in shards]
    if own_slab is not None:
        lands = [lax.dynamic_update_slice(land, s[None], (own_slab, 0, 0)) for land, s in zip(lands, shards)]
    lands = [_in_hbm(land) for land in lands]
    res = pl.pallas_call(
        body, name=name, in_specs=[HBM] * (2 * n) + [ANY] * len(after), out_specs=out_specs, out_shape=out_shape,
        input_output_aliases=aliases, compiler_params=_split_params(cid),
    )(*[_in_hbm(s) for s in shards], *lands, *after)
    return [tuple(res[4 * a:4 * a + 4]) for a in range(n)], res[-1]


def _wait_call(wait_fn, parts, after, *, name):
    ssem, rsem, src, land = parts
    after = list(after) if isinstance(after, (list, tuple)) else [after]

    def body(src_ref, land_ref, ssem_ref, rsem_ref, *rest):
        wait_fn(src_ref, land_ref, ssem_ref, rsem_ref)

    return pl.pallas_call(
        body, name=name, in_specs=[HBM, HBM, SEM, SEM] + [ANY] * len(after), out_specs=[HBM, HBM],
        out_shape=[_hbm(src.shape, src.dtype), _hbm(land.shape, land.dtype)], input_output_aliases={0: 0, 1: 1},
        compiler_params=pltpu.CompilerParams(has_side_effects=SIDE_EFFECT),
    )(src, land, ssem, rsem, *after)


def _gather_wait(parts, after, *, name, ks=ALL_CHIPS):
    def wait(src_ref, land_ref, ssem_ref, rsem_ref):
        x, y, c, chips = _place()
        for k in ks:
            cx, cy = chips[k]
            cp = _gather_copy(src_ref, land_ref, ssem_ref, rsem_ref, k, 2 * cx + cy, (x, y, c), c)
            cp.wait_send()
            cp.wait_recv()

    return _wait_call(wait, parts, after, name=name)


def _relay_copy(buf_ref, ssem, rsem, k, slab, to, c):
    hr = buf_ref.shape[1] // 2
    part = buf_ref.at[slab, pl.ds(k * hr, hr), _half_cols(buf_ref.shape[2], c)]
    return pltpu.make_async_remote_copy(
        src_ref=part, dst_ref=part, send_sem=ssem.at[k], recv_sem=rsem.at[k], device_id=to, device_id_type=MESH)


def _relay_start(land, *, name, cid):
    def body(buf_ref, ssem, rsem, buf_out, token):
        _handshake(_same_core_neighbours())
        x, y, c, _ = _place()
        _relay_copy(buf_ref, ssem, rsem, 0, 2 * (1 - x) + y, (x, 1 - y, c), c).start()
        _relay_copy(buf_ref, ssem, rsem, 1, 2 * x + 1 - y, (1 - x, y, c), c).start()
        token[...] = jnp.zeros_like(token)

    res = pl.pallas_call(
        body, name=name, in_specs=[HBM], out_specs=[SEM, SEM, HBM, pl.BlockSpec(memory_space=pltpu.VMEM)],
        out_shape=[pltpu.SemaphoreType.DMA((2,)), pltpu.SemaphoreType.DMA((2,)), _hbm(land.shape, land.dtype),
                   jax.ShapeDtypeStruct(TOKEN_SHAPE, F32)],
        input_output_aliases={0: 2}, compiler_params=_split_params(cid),
    )(land)
    return tuple(res[:3]), res[3]


def _relay_wait(parts, after, *, name):
    ssem, rsem, buf = parts
    after = list(after) if isinstance(after, (list, tuple)) else [after]

    def body(buf_ref, ssem_ref, rsem_ref, *rest):
        x, y, c, _ = _place()
        diagonal = 2 * (1 - x) + 1 - y
        _relay_copy(buf_ref, ssem_ref, rsem_ref, 0, 2 * (1 - x) + y, (x, y, c), c).wait_send()
        _relay_copy(buf_ref, ssem_ref, rsem_ref, 1, 2 * x + 1 - y, (x, y, c), c).wait_send()
        _relay_copy(buf_ref, ssem_ref, rsem_ref, 0, diagonal, (x, y, c), c).wait_recv()
        _relay_copy(buf_ref, ssem_ref, rsem_ref, 1, diagonal, (x, y, c), c).wait_recv()

    return pl.pallas_call(
        body, name=name, in_specs=[HBM, SEM, SEM] + [ANY] * len(after), out_specs=HBM,
        out_shape=_hbm(buf.shape, buf.dtype), input_output_aliases={0: 0},
        compiler_params=pltpu.CompilerParams(has_side_effects=SIDE_EFFECT),
    )(buf, ssem, rsem, *after)


def _forward_copy(buf_ref, ssem, rsem, k, slab, which, to):
    part = buf_ref.at[slab, :, _half_cols(buf_ref.shape[2], which)]
    return pltpu.make_async_remote_copy(
        src_ref=part, dst_ref=part, send_sem=ssem.at[k], recv_sem=rsem.at[k], device_id=to, device_id_type=MESH)


def _sibling_forward(land, *, name, cid, ks=ALL_CHIPS):
    def body(_, buf, send_sems, recv_sems):
        _handshake(_sibling())
        x, y, c, chips = _place()
        copies = []
        for k in ks:
            cx, cy = chips[k]
            cp = _forward_copy(buf, send_sems, recv_sems, k, 2 * cx + cy, c, (x, y, 1 - c))
            cp.start()
            copies.append(cp)
        for k in ks:
            cx, cy = chips[k]
            _forward_copy(buf, send_sems, recv_sems, k, 2 * cx + cy, 1 - c, (x, y, c)).wait_recv()
        for cp in copies:
            cp.wait_send()

    return pl.pallas_call(
        body, name=name, in_specs=[HBM], out_specs=HBM, out_shape=jax.ShapeDtypeStruct(land.shape, land.dtype),
        input_output_aliases={0: 0},
        scratch_shapes=[pltpu.SemaphoreType.DMA((3,)), pltpu.SemaphoreType.DMA((3,))],
        compiler_params=pltpu.CompilerParams(collective_id=cid),
    )(land)


def _forward_start(land, *, name, cid, ks=ALL_CHIPS):
    def body(buf_ref, ssem, rsem, buf_out, token):
        _handshake(_sibling())
        x, y, c, chips = _place()
        for k in ks:
            cx, cy = chips[k]
            _forward_copy(buf_ref, ssem, rsem, k, 2 * cx + cy, c, (x, y, 1 - c)).start()
        token[...] = jnp.zeros_like(token)

    res = pl.pallas_call(
        body, name=name, in_specs=[HBM], out_specs=[SEM, SEM, HBM, pl.BlockSpec(memory_space=pltpu.VMEM)],
        out_shape=[pltpu.SemaphoreType.DMA((3,)), pltpu.SemaphoreType.DMA((3,)), _hbm(land.shape, land.dtype),
                   jax.ShapeDtypeStruct(TOKEN_SHAPE, F32)],
        input_output_aliases={0: 2}, compiler_params=_split_params(cid),
    )(land)
    return tuple(res[:3]), res[3]


def _forward_wait(parts, after, *, name, ks=ALL_CHIPS):
    ssem, rsem, buf = parts
    after = list(after) if isinstance(after, (list, tuple)) else [after]

    def body(buf_ref, ssem_ref, rsem_ref, *rest):
        x, y, c, chips = _place()
        for k in ks:
            cx, cy = chips[k]
            _forward_copy(buf_ref, ssem_ref, rsem_ref, k, 2 * cx + cy, c, (x, y, c)).wait_send()
            _forward_copy(buf_ref, ssem_ref, rsem_ref, k, 2 * cx + cy, 1 - c, (x, y, c)).wait_recv()

    return pl.pallas_call(
        body, name=name, in_specs=[HBM, SEM, SEM] + [ANY] * len(after), out_specs=HBM,
        out_shape=_hbm(buf.shape, buf.dtype), input_output_aliases={0: 0},
        compiler_params=pltpu.CompilerParams(has_side_effects=SIDE_EFFECT),
    )(buf, ssem, rsem, *after)


def _share_copy(buf_ref, ssem, rsem, a, which, to):
    part = buf_ref.at[:, _half_cols(buf_ref.shape[1], which)]
    return pltpu.make_async_remote_copy(
        src_ref=part, dst_ref=part, send_sem=ssem.at[a], recv_sem=rsem.at[a], device_id=to, device_id_type=MESH)


def _share_start(arrays, *, name, cid):
    n = len(arrays)

    def body(*refs):
        bufs, ssem, rsem, token = refs[:n], refs[n], refs[n + 1], refs[-1]
        _handshake(_sibling())
        x, y, c, _ = _place()
        for a in range(n):
            _share_copy(bufs[a], ssem, rsem, a, c, (x, y, 1 - c)).start()
        token[...] = jnp.zeros_like(token)

    res = pl.pallas_call(
        body, name=name, in_specs=[HBM] * n,
        out_specs=[SEM, SEM] + [HBM] * n + [pl.BlockSpec(memory_space=pltpu.VMEM)],
        out_shape=[pltpu.SemaphoreType.DMA((n,)), pltpu.SemaphoreType.DMA((n,))]
        + [_hbm(b.shape, b.dtype) for b in arrays] + [jax.ShapeDtypeStruct(TOKEN_SHAPE, F32)],
        input_output_aliases={a: 2 + a for a in range(n)}, compiler_params=_split_params(cid),
    )(*[_in_hbm(b) for b in arrays])
    return (res[0], res[1], list(res[2:2 + n])), res[-1]


def _share_wait(parts, after, *, name):
    ssem, rsem, bufs = parts
    n = len(bufs)
    after = list(after) if isinstance(after, (list, tuple)) else [after]

    def body(*refs):
        buf_refs, ssem_ref, rsem_ref = refs[:n], refs[n], refs[n + 1]
        x, y, c, _ = _place()
        for a in range(n):
            _share_copy(buf_refs[a], ssem_ref, rsem_ref, a, c, (x, y, c)).wait_send()
            _share_copy(buf_refs[a], ssem_ref, rsem_ref, a, 1 - c, (x, y, c)).wait_recv()

    return pl.pallas_call(
        body, name=name, in_specs=[HBM] * n + [SEM, SEM] + [ANY] * len(after), out_specs=[HBM] * n,
        out_shape=[_hbm(b.shape, b.dtype) for b in bufs], input_output_aliases={a: a for a in range(n)},
        compiler_params=pltpu.CompilerParams(has_side_effects=SIDE_EFFECT),
    )(*bufs, ssem, rsem, *after)


def _scatter_copy(src_ref, land_ref, ssem, rsem, k, src_slab, dst_slab, to):
    return pltpu.make_async_remote_copy(
        src_ref=src_ref.at[src_slab], dst_ref=land_ref.at[dst_slab], send_sem=ssem.at[k], recv_sem=rsem.at[k],
        device_id=to, device_id_type=MESH)


def _scatter_start(part, *, name, cid):
    def start(src_ref, land_ref, ssem, rsem):
        x, y, c, chips = _place()
        me = 2 * x + y
        for k, (cx, cy) in enumerate(chips):
            _scatter_copy(src_ref, land_ref, ssem, rsem, k, 2 * cx + cy, me, (cx, cy, c)).start()

    return _split_start(start, _same_core_chips, part, part.shape, N_CHIPS - 1, name=name, cid=cid)


def _scatter_wait(parts, after, *, name):
    def wait(src_ref, land_ref, ssem_ref, rsem_ref):
        x, y, c, chips = _place()
        for k, (cx, cy) in enumerate(chips):
            idx = 2 * cx + cy
            cp = _scatter_copy(src_ref, land_ref, ssem_ref, rsem_ref, k, idx, idx, (x, y, c))
            cp.wait_send()
            cp.wait_recv()

    return _wait_call(wait, parts, after, name=name)


def _split_start(start_fn, peers_fn, src, land_shape, n_sems, *, name, cid):
    def body(src_ref, land_ref, ssem, rsem, src_out, land_out, token):
        _handshake(peers_fn())
        start_fn(src_ref, land_ref, ssem, rsem)
        token[...] = jnp.zeros_like(token)

    res = pl.pallas_call(
        body, name=name, in_specs=[HBM, HBM], out_specs=[SEM, SEM, HBM, HBM, pl.BlockSpec(memory_space=pltpu.VMEM)],
        out_shape=[pltpu.SemaphoreType.DMA((n_sems,)), pltpu.SemaphoreType.DMA((n_sems,)), _hbm(src.shape, src.dtype),
                   _hbm(land_shape, src.dtype), jax.ShapeDtypeStruct(TOKEN_SHAPE, F32)],
        input_output_aliases={0: 2, 1: 3}, compiler_params=_split_params(cid),
    )(_in_hbm(src), _in_hbm(lax.empty(land_shape, src.dtype)))
    return tuple(res[:4]), res[4]


def _sibling_copies(src_ref, land_ref, ssem, rsem, k0, groups, which, to):
    def copy(k, src, dst):
        return pltpu.make_async_remote_copy(
            src_ref=src, dst_ref=dst, send_sem=ssem.at[k], recv_sem=rsem.at[k], device_id=to, device_id_type=MESH)

    if groups == 0:
        return [copy(k0, src_ref, land_ref)]
    hw = src_ref.shape[1] // groups // 2
    return [copy(k0 + j, src_ref.at[:, pl.ds(pl.multiple_of((2 * j + which) * hw, LANES), hw)],
                 land_ref.at[:, j * hw:(j + 1) * hw]) for j in range(groups)]


def _to_sibling_start(items, *, name, cid):
    n = len(items)
    shapes = [a.shape if g == 0 else (a.shape[0], a.shape[1] // 2) for a, g in items]
    first = [sum(max(g, 1) for _, g in items[:k]) for k in range(n + 1)]

    def body(*refs):
        srcs, lands, ssem, rsem, token = refs[:n], refs[n:2 * n], refs[2 * n], refs[2 * n + 1], refs[-1]
        _handshake(_sibling())
        x, y, c, _ = _place()
        for k, (_, g) in enumerate(items):
            for cp in _sibling_copies(srcs[k], lands[k], ssem, rsem, first[k], g, 1 - c, (x, y, 1 - c)):
                cp.start()
        token[...] = jnp.zeros_like(token)

    res = pl.pallas_call(
        body, name=name, in_specs=[HBM] * (2 * n),
        out_specs=[SEM, SEM] + [HBM] * (2 * n) + [pl.BlockSpec(memory_space=pltpu.VMEM)],
        out_shape=[pltpu.SemaphoreType.DMA((first[n],)), pltpu.SemaphoreType.DMA((first[n],))]
        + [_hbm(a.shape, a.dtype) for a, _ in items] + [_hbm(s, a.dtype) for s, (a, _) in zip(shapes, items)]
        + [jax.ShapeDtypeStruct(TOKEN_SHAPE, F32)],
        input_output_aliases={k: 2 + k for k in range(2 * n)}, compiler_params=_split_params(cid),
    )(*[_in_hbm(a) for a, _ in items], *[_in_hbm(lax.empty(s, a.dtype)) for s, (a, _) in zip(shapes, items)])
    return [(res[0], res[1], first[k], g, res[2 + k], res[2 + n + k]) for k, (_, g) in enumerate(items)], res[-1]


def _from_sibling(flight, after, *, name):
    ssem, rsem, k0, groups, src, land = flight

    def wait(src_ref, land_ref, ssem_ref, rsem_ref):
        x, y, c, _ = _place()
        for cp in _sibling_copies(src_ref, land_ref, ssem_ref, rsem_ref, k0, groups, 1 - c, (x, y, c)):
            cp.wait_send()
            cp.wait_recv()

    return _wait_call(wait, (ssem, rsem, src, land), after, name=name)


def _dev_peers(x, y, c, chips):
    return [(x, y, 1 - c)] + [(cx, cy, c) for cx, cy in chips] + [(cx, cy, 1 - c) for cx, cy in chips]


def _dev_gather_start(part, *, name, cid):
    def start(src_ref, land_ref, ssem, rsem):
        x, y, c, chips = _place()
        for k, to in enumerate(_dev_peers(x, y, c, chips)):
            pltpu.make_async_remote_copy(
                src_ref=src_ref, dst_ref=land_ref.at[4 * x + 2 * y + c], send_sem=ssem.at[k], recv_sem=rsem.at[k],
                device_id=to, device_id_type=MESH).start()

    return _split_start(start, lambda: _dev_peers(*_place()), part, (N_DEV,) + part.shape, N_DEV - 1, name=name,
                        cid=cid)


def _dev_gather_wait(parts, after, *, name):
    def wait(src_ref, land_ref, ssem_ref, rsem_ref):
        x, y, c, chips = _place()
        for k, (px, py, pc) in enumerate(_dev_peers(x, y, c, chips)):
            cp = pltpu.make_async_remote_copy(
                src_ref=src_ref, dst_ref=land_ref.at[4 * px + 2 * py + pc], send_sem=ssem_ref.at[k],
                recv_sem=rsem_ref.at[k], device_id=(x, y, c), device_id_type=MESH)
            cp.wait_send()
            cp.wait_recv()

    return _wait_call(wait, parts, after, name=name)[1]


def _sibling_share_halves(arrays, *, name, cid):
    n = len(arrays)

    def body(*refs):
        bufs = refs[n:2 * n]
        send_sems, recv_sems = refs[2 * n:]
        _handshake(_sibling())
        x, y, c, _ = _place()
        copies = []
        for a in range(n):
            mine = bufs[a].at[:, _half_cols(bufs[a].shape[1], c)]
            cp = pltpu.make_async_remote_copy(
                src_ref=mine, dst_ref=mine, send_sem=send_sems.at[a], recv_sem=recv_sems.at[a],
                device_id=(x, y, 1 - c), device_id_type=MESH)
            cp.start()
            copies.append(cp)
        for a in range(n):
            theirs = bufs[a].at[:, _half_cols(bufs[a].shape[1], 1 - c)]
            pltpu.make_async_remote_copy(
                src_ref=theirs, dst_ref=theirs, send_sem=send_sems.at[a], recv_sem=recv_sems.at[a],
                device_id=(x, y, c), device_id_type=MESH).wait_recv()
        for cp in copies:
            cp.wait_send()

    return pl.pallas_call(
        body, name=name, in_specs=[HBM] * n, out_specs=[HBM] * n,
        out_shape=[jax.ShapeDtypeStruct(h.shape, h.dtype) for h in arrays],
        input_output_aliases={a: a for a in range(n)},
        scratch_shapes=[pltpu.SemaphoreType.DMA((n,)), pltpu.SemaphoreType.DMA((n,))],
        compiler_params=pltpu.CompilerParams(collective_id=cid),
    )(*arrays)


def _pack(arrays, rows_multiple=16, width=LANES):
    flat = jnp.concatenate([a.astype(F32).reshape(-1) for a in arrays])
    total = flat.shape[0]
    rows = -(-total // width)
    rows = -(-rows // rows_multiple) * rows_multiple
    return jnp.pad(flat, (0, rows * width - total)).reshape(rows, width)


def _unpack(buf, shapes):
    flat = buf.reshape(-1)
    out, off = [], 0
    for s in shapes:
        n = math.prod(s)
        out.append(flat[off:off + n].reshape(s))
        off += n
    return out


def kernel(x, norm_pre, norm_post, gla_w_in, gla_w_gate2, gla_b_gate, gla_o_gain, gla_w_out, sgu_w_in, sgu_ln_gain, sgu_ln_bias, sgu_w_spatial, sgu_b_spatial, sgu_w_out, loss_target, m_norm_pre, m_norm_post, m_gla_w_in, m_gla_w_gate2, m_gla_b_gate, m_gla_o_gain, m_gla_w_out, m_sgu_w_in, m_sgu_ln_gain, m_sgu_ln_bias, m_sgu_w_spatial, m_sgu_b_spatial, m_sgu_w_out, v_norm_pre, v_norm_post, v_gla_w_in, v_gla_w_gate2, v_gla_b_gate, v_gla_o_gain, v_gla_w_out, v_sgu_w_in, v_sgu_ln_gain, v_sgu_ln_bias, v_sgu_w_spatial, v_sgu_b_spatial, v_sgu_w_out):
    _, t, d = x.shape
    dk = d // 2
    ws = gla_w_in.shape[2]
    wp = -(-ws // LANES) * LANES
    lay = (ws, wp)
    chip =2 * lax.axis_index("x") + lax.axis_index("y")
    core = lax.axis_index("c")
    core_idx = core.astype(jnp.int32).reshape(1)
    others = jnp.arange(N_CHIPS - 1, dtype=jnp.int32)
    others = others + (others >= chip).astype(jnp.int32)
    slots = jnp.concatenate([chip.astype(jnp.int32).reshape(1), others, core_idx])

    x0 = x[0]
    target = loss_target[0]

    wt_in_g, mt_in_g, vt_in_g = gla_w_in[0].T, m_gla_w_in[0].T, v_gla_w_in[0].T

    small_shard = _pack([gla_w_gate2[0], sgu_ln_gain[0], sgu_ln_bias[0]], rows_multiple=8, width=2 * LANES)
    own = [small_shard, jnp.pad(wt_in_g.astype(BF16), ((0, wp - ws), (0, 0)))]
    in_flight, token = _gather_start(own, name="gather_start_a", cid=0, relayed=(1,))

    def with_sibling_and_own(mine, land, name, cid):
        return lax.dynamic_update_slice(_sibling_forward(land, name=name + "_share", cid=cid), mine[None],
                                        (chip, 0, 0))

    h0 = _norm_pre(x0, norm_pre[0:1] + token[0:1, 0:1], name="pre0")
    g_small = with_sibling_and_own(*_gather_wait(in_flight[0], h0, name="w_small_wait"), "w_small", 12)
    mine, land = _gather_wait(in_flight[1], [g_small, wt_in_g, mt_in_g, vt_in_g], name="w_gla_in_wait", ks=NEIGHBOURS)
    relay, token = _relay_start(land, name="w_gla_in_relay", cid=11)
    crossing, token = _forward_start(relay[2], name="w_gla_in_share_near", cid=22, ks=NEIGHBOURS)
    own_later = [(p[0] + token[0, 0]).astype(BF16) for p in (gla_w_out, sgu_w_in, sgu_w_out)]
    in_flight_later, token = _gather_start(own_later, name="gather_start_b", cid=1, after=[token], own_slab=chip)
    in_flight = in_flight + in_flight_later
    land = _relay_wait((relay[0], relay[1], crossing[2]), token, name="w_gla_in_relay_wait")
    land = _forward_wait((crossing[0], crossing[1], land), token, name="w_gla_in_share_near_wait", ks=NEIGHBOURS)
    land = _sibling_forward(land, name="w_gla_in_share_far", cid=13, ks=(2,))
    wt_g = lax.dynamic_update_slice(land, mine[None], (chip, 0, 0)).reshape(N_CHIPS * wp, d)

    def behind(small, token):
        return small + token[0:1, 0:1]

    def arriving(i, after, name):
        mine, land = _gather_wait(in_flight[i], after, name=name + "_wait")
        crossing, token = _forward_start(land, name=name + "_share", cid=i)
        return (mine, crossing), token

    def arrived(pending, after, name):
        _, crossing = pending
        return _forward_wait(crossing, after, name=name + "_share_wait")

    shard_shapes = [gla_w_gate2.shape[1:], sgu_ln_gain.shape[1:], sgu_ln_bias.shape[1:]]
    per_chip = [_unpack(g_small[j], shard_shapes) for j in range(N_CHIPS)]
    w2_full = jnp.concatenate([p[0] for p in per_chip], axis=1)
    ln_gain = jnp.concatenate([p[1] for p in per_chip], axis=0)[None, :]
    ln_bias = jnp.concatenate([p[2] for p in per_chip], axis=0)[None, :]
    w2p = jnp.pad(w2_full, ((0, LANES - GLA_GATE_RANK), (0, 0)))

    pos_chunk = jnp.arange(SGU_BLOCK) // CHUNK
    mask = pos_chunk[:, None] >= pos_chunk[None, :]
    ws_masked = jnp.where(mask[None], sgu_w_spatial[0], 0.0)
    ws_masked_t = ws_masked.transpose(0, 2, 1)
    bs_t = sgu_b_spatial[0].T

    proj0 = _matmul(h0, wt_g, mode="nt", out_dtype=F32, name="gla_in", tn=wp)
    pending, tok = arriving(2, proj0, "w_gla_out")
    o0, a0, s_before, s_final = _gla_fwd(proj0, w2p, behind(gla_b_gate, tok), gla_o_gain, lay, name="gla_scan")
    w_out_g = arrived(pending, a0, "w_gla_out").reshape(d, d)
    y0 = _matmul(a0, w_out_g, mode="nn", out_dtype=F32, name="gla_out")
    pending, tok = arriving(3, y0, "w_sgu_in")
    x1, h1 = _post_then_pre(x0, y0, behind(norm_post[0:1], tok), norm_pre[1:2], name="post0_pre1")
    g_wi_s = arrived(pending, h1, "w_sgu_in")
    pending, tok = arriving(4, g_wi_s, "w_sgu_out")
    proj1 = _matmul(h1, g_wi_s, mode="nn", out_dtype=F32, name="sgu_in", b_shards=True, after=tok)
    a1 = _sgu_fwd(proj1, ln_gain, ln_bias, ws_masked, bs_t, name="sgu_gate")
    w_out_s = arrived(pending, a1, "w_sgu_out").reshape(d, d)
    y1 = _matmul(a1, w_out_s, mode="nn", out_dtype=F32, name="sgu_out")
    loss_part, dx2, dy1, d_post1 = _loss_head(x1, y1, norm_post[1:2], target, name="loss_head")

    def pair_gradient(a_sent, b_sent, after, shards_on, name, cid):
        a_me, a_sib = _from_sibling(a_sent, after, name=name + "_a_wait")
        b_me, b_sib = _from_sibling(b_sent, [a_sib] + list(after), name=name + "_b_wait")
        pair = _matmul_dw_pair(a_me, a_sib, b_me, b_sib, core_idx, shards_on=shards_on,
                               name=name + "_pair")
        return _scatter_start(pair, name=name + "_start", cid=cid)

    def reduced(flight, after, name):
        pair, landed = _scatter_wait(flight, after, name=name + "_wait")
        return _chip_sum(pair, landed, slots, name=name + "_sum")

    (dy1_sent, *acts), tok = _to_sibling_start([(dy1, 1), (a1, 0), (a0, 0), (h1, 0), (h0, 1)], name="dy1_acts_to_sibling",
                                               cid=6)
    dy1 = dy1_sent[4]
    da1 = _matmul(dy1, w_out_s, mode="nt", out_dtype=F32, name="d_sgu_act", after=tok)
    fl_wo_s, tok = pair_gradient(acts[0], dy1_sent, [da1], "rows", "g_sgu_out", 15)
    dproj1, d_ws, d_bs_t, d_lg, d_lb = _sgu_bwd(da1, proj1, ln_gain, behind(ln_bias, tok), ws_masked, ws_masked_t,
                                                bs_t, name="sgu_gate_bwd")
    (dp1_sent,), tok = _to_sibling_start([(dproj1, N_CHIPS)], name="dproj1_to_sibling", cid=7)
    dproj1 = dp1_sent[4]
    dh1 = _matmul_nt_shards(dproj1, g_wi_s, out_dtype=F32, name="d_sgu_h", after=tok)
    fl_wi_s, tok = pair_gradient(acts[2], dp1_sent, [dh1], "cols", "g_sgu_in", 16)
    dx1, dy0, d_pre1, d_post0 = _mid_bwd(dx2, dh1, x1, behind(norm_pre[1:2], tok), y0, norm_post[0:1],
                                         name="pre1_post0_bwd")
    (dy0_sent,), tok = _to_sibling_start([(dy0, 1)], name="dy0_to_sibling", cid=8)
    dy0 = dy0_sent[4]
    da0 = _matmul(dy0, w_out_g, mode="nt", out_dtype=F32, name="d_gla_act", after=tok)
    fl_wo_g, tok = pair_gradient(acts[1], dy0_sent, [da0], "rows", "g_gla_out", 17)
    dproj0, d_og, d_bg, d_w2p = _gla_bwd(da0, o0, proj0, w2p, behind(gla_b_gate, tok), gla_o_gain, s_before, s_final,
                                         lay, name="gla_scan_bwd")
    early_shapes = [norm_post.shape, gla_b_gate.shape, gla_o_gain.shape, sgu_w_spatial.shape, sgu_b_spatial.shape,
                    (1, GLA_GATE_RANK, dk), (1, d), (1, d), (1, LANES)]
    early_part = _pack([jnp.concatenate([d_post0, d_post1], axis=0), d_bg, d_og, jnp.where(mask[None], d_ws, 0.0)[None],
                        d_bs_t.T[None], d_w2p[:GLA_GATE_RANK][None], d_lg, d_lb, loss_part])
    early_flight, tok = _dev_gather_start(early_part, name="small_early_start", cid=20)
    (dp0_sent,), tok_sent = _to_sibling_start([(dproj0, 0)], name="dproj0_to_sibling", cid=9)
    dproj0 = dp0_sent[4]
    dh0 = _matmul(dproj0, wt_g, mode="nn", out_dtype=F32, name="d_gla_h", after=tok_sent)
    a_me, a_sib = _from_sibling(dp0_sent, [dh0, tok], name="g_gla_in_a_wait")
    b_me, b_sib = _from_sibling(acts[3], [a_sib, dh0], name="g_gla_in_b_wait")
    fl_wi_g, tok_scatter = [], None
    for p in range(2):
        pair = _matmul_dw_pair(a_me, a_sib, b_me, b_sib, core_idx, shards_on="rows", part=(p, 2),
                               name=f"g_gla_in_pair{p}", after=tok_scatter)
        flight, tok_scatter = _scatter_start(pair, name=f"g_gla_in_start{p}", cid=18 + p)
        fl_wi_g.append(flight)
    r_wo_s = reduced(fl_wo_s, tok_scatter, "g_sgu_out")
    r_wi_s = reduced(fl_wi_s, r_wo_s, "g_sgu_in")
    r_wo_g = reduced(fl_wo_g, r_wi_s, "g_gla_out")
    sharing, tok = _share_start([r_wo_s, r_wi_s, r_wo_g], name="grads_share_a", cid=10)
    grad_x, d_pre0 = _first_bwd(dx1, dh0, x0, behind(norm_pre[0:1], tok), name="pre0_bwd")

    late_part = _pack([jnp.concatenate([d_pre0, d_pre1], axis=0)])
    late_flight, tok = _dev_gather_start(late_part, name="small_late_start", cid=21)

    def big_update(w, g, m, v, name, after=None):
        return [u[None] for u in _adamw(w[0], g, m[0], v[0], name=name, after=after)]

    g_wo_sgu, g_wi_sgu, g_wo_gla = _share_wait(sharing, [grad_x, tok], name="grads_share_a_wait")
    u_wi_sgu = big_update(sgu_w_in, g_wi_sgu, m_sgu_w_in, v_sgu_w_in, "adamw_sgu_w_in")
    u_wo_gla = big_update(gla_w_out, g_wo_gla, m_gla_w_out, v_gla_w_out, "adamw_gla_w_out", after=u_wi_sgu[1])

    r_wi_g, behind_this = None, u_wo_gla[1]
    for p, flight in enumerate(fl_wi_g):
        pair, landed = _scatter_wait(flight, behind_this, name=f"g_gla_in_wait{p}")
        r_wi_g = behind_this = _chip_sum(pair, landed, slots, part=(p, 2), into=r_wi_g, name=f"g_gla_in_sum{p}")
    gt_wi_gla, = _sibling_share_halves([r_wi_g], name="grads_share_b", cid=14)
    u_wi_gla_t = _adamw(wt_in_g, gt_wi_gla, mt_in_g, vt_in_g, name="adamw_gla_w_in")
    u_wi_gla = [u.T[None] for u in u_wi_gla_t]
    u_wo_sgu = big_update(sgu_w_out, g_wo_sgu, m_sgu_w_out, v_sgu_w_out, "adamw_sgu_w_out", after=u_wi_gla_t[1])

    def summed_over_devices(part, flight, after, shapes, name):
        land = _dev_gather_wait(flight, after, name=name + "_wait")
        every = lax.dynamic_update_slice(land, part[None], (2 * chip + core, 0, 0))
        return _unpack(_stack_sum(every, name=name + "_sum"), shapes)

    (g_post, g_bg, g_og, g_wsp, g_bsp, g_w2_full, g_lg_full, g_lb_full, loss_vec) = summed_over_devices(
        early_part, early_flight, u_wo_sgu[1], early_shapes, "small_early")
    g_pre, = summed_over_devices(late_part, late_flight, loss_vec, [norm_pre.shape], "small_late")
    loss = loss_vec[0, 0]
    g_w2 = lax.dynamic_slice_in_dim(g_w2_full, chip * (dk // N_CHIPS), dk // N_CHIPS, axis=2)
    g_lg = lax.dynamic_slice_in_dim(g_lg_full, chip * (d // N_CHIPS), d // N_CHIPS, axis=1)
    g_lb = lax.dynamic_slice_in_dim(g_lb_full, chip * (d // N_CHIPS), d // N_CHIPS, axis=1)

    small_w = [norm_pre, norm_post, gla_b_gate, gla_o_gain, sgu_w_spatial, sgu_b_spatial, gla_w_gate2, sgu_ln_gain,
               sgu_ln_bias]
    small_g = [g_pre, g_post, g_bg, g_og, g_wsp, g_bsp, g_w2, g_lg, g_lb]
    small_m = [m_norm_pre, m_norm_post, m_gla_b_gate, m_gla_o_gain, m_sgu_w_spatial, m_sgu_b_spatial, m_gla_w_gate2,
               m_sgu_ln_gain, m_sgu_ln_bias]
    small_v = [v_norm_pre, v_norm_post, v_gla_b_gate, v_gla_o_gain, v_sgu_w_spatial, v_sgu_b_spatial, v_gla_w_gate2,
               v_sgu_ln_gain, v_sgu_ln_bias]
    own_shapes = [w.shape for w in small_w]
    _, s_dl, s_m, s_v = _adamw(_pack(small_w), _pack(small_g), _pack(small_m), _pack(small_v), name="adamw_small")
    dl_s, m_s, v_s = _unpack(s_dl, own_shapes), _unpack(s_m, own_shapes), _unpack(s_v, own_shapes)

    def ordered(small, kind):
        pre, post, bg, og, wsp, bsp, w2, lg, lb = small
        return [pre, post, u_wi_gla[kind], w2, bg, og, u_wo_gla[kind], u_wi_sgu[kind], lg, lb, wsp, bsp, u_wo_sgu[kind]]

    return (loss, grad_x[None], *ordered(small_g, 0), *ordered(dl_s, 1), *ordered(m_s, 2), *ordered(v_s, 3))
```

```python
import math

import jax
import jax.numpy as jnp
from jax import lax
from jax.experimental import pallas as pl
from jax.experimental.pallas import tpu as pltpu

F32 = jnp.float32
BF16 = jnp.bfloat16
MESH = pl.DeviceIdType.MESH

EPS = 1e-6
CHUNK = 64
GLA_HEADS = 4
GLA_GATE_RANK = 16
GLA_TAU = 16.0
SGU_BLOCK = 128
SGU_GROUPS = 8
N_CHIPS = 4
N_DEV = 8
LANES = 128

ADAM_LR = 0.001
ADAM_B1 = 0.9
ADAM_B2 = 0.999
ADAM_EPS = 1e-08
ADAM_WD = 0.01
ADAM_STEP = 10

VMEM_LIMIT = 56 * 1024 * 1024


def _cparams(sem=None):
    return pltpu.CompilerParams(dimension_semantics=sem, vmem_limit_bytes=VMEM_LIMIT)


def _pick(n, cap, unit=LANES):
    best = None
    for t in range(unit, min(n, cap) + 1, unit):
        if n % t == 0:
            best = t
    assert best is not None, (n, cap, unit)
    return best


def _dot(a, b, dims):
    return lax.dot_general(a, b, (dims, ((), ())), preferred_element_type=F32)


def _dot_nn(a, b):
    return _dot(a, b, ((1,), (0,)))


def _dot_nt(a, b):
    return _dot(a, b, ((1,), (1,)))


def _dot_tn(a, b):
    return _dot(a, b, ((0,), (0,)))


def _matmul(a, b, *, mode, out_dtype, name, tm=1024, tn=512, b_shards=False, after=None):
    M, K = a.shape
    if b_shards:
        ns, Kb, bc = b.shape
        N, tn = ns * bc, _pick(bc, tn)
        per = bc // tn
        b_spec = pl.BlockSpec((None, K, tn), lambda i, j: (j // per, 0, j % per))
    elif mode == "nt":
        N, Kb = b.shape
        tn = _pick(N, tn)
        b_spec = pl.BlockSpec((tn, K), lambda i, j: (j, 0))
    else:
        Kb, N = b.shape
        tn = _pick(N, tn)
        b_spec = pl.BlockSpec((K, tn), lambda i, j: (0, j))
    assert K == Kb and a.dtype == b.dtype == BF16, (a.shape, b.shape, mode)
    tm = _pick(M, tm)
    dims = ((1,), (1,)) if mode == "nt" else ((1,), (0,))
    extra_specs, extra_args = ([], []) if after is None else ([pl.BlockSpec(memory_space=pl.ANY)], [after])

    def body(a_ref, b_ref, *rest):
        rest[-1][...] = _dot(a_ref[...], b_ref[...], dims).astype(out_dtype)

    return pl.pallas_call(
        body, name=name, grid=(M // tm, N // tn),
        in_specs=[pl.BlockSpec((tm, K), lambda i, j: (i, 0)), b_spec] + extra_specs,
        out_specs=pl.BlockSpec((tm, tn), lambda i, j: (i, j)), out_shape=jax.ShapeDtypeStruct((M, N), out_dtype),
        compiler_params=_cparams(("parallel", "parallel")),
    )(a, b, *extra_args)


def _matmul_nt_shards(a, b, *, out_dtype, name, tm=1024, tn=512, after=None):
    M, K = a.shape
    ns, N, kc = b.shape
    assert K == ns * kc
    tm, tn = _pick(M, tm), _pick(N, tn)

    def body(a_ref, *rest):
        b_refs, o_ref = rest[:ns], rest[ns + (after is not None)]
        acc = _dot_nt(a_ref[:, 0:kc], b_refs[0][...])
        for j in range(1, ns):
            acc += _dot_nt(a_ref[:, j * kc:(j + 1) * kc], b_refs[j][...])
        o_ref[...] = acc.astype(out_dtype)

    def shard(j):
        return pl.BlockSpec((None, tn, kc), lambda i, n: (j, n, 0))

    extra_specs, extra_args = ([], []) if after is None else ([pl.BlockSpec(memory_space=pl.ANY)], [after])
    return pl.pallas_call(
        body, name=name, grid=(M // tm, N // tn),
        in_specs=[pl.BlockSpec((tm, K), lambda i, n: (i, 0))] + [shard(j) for j in range(ns)] + extra_specs,
        out_specs=pl.BlockSpec((tm, tn), lambda i, n: (i, n)), out_shape=jax.ShapeDtypeStruct((M, N), out_dtype),
        compiler_params=_cparams(("parallel", "parallel")),
    )(a, *([b] * ns), *extra_args)


def _rstd(x):
    return lax.rsqrt(jnp.mean(x * x, axis=-1, keepdims=True) + EPS)


def _row_spec(tr, d):
    return pl.BlockSpec((tr, d), lambda i: (i, 0))


def _vec_spec(d):
    return pl.BlockSpec((1, d), lambda i: (0, 0))


def _acc_rows(ref, i, val, cols=slice(None)):
    @pl.when(i == 0)
    def _():
        ref[:, cols] = val

    @pl.when(i > 0)
    def _():
        ref[:, cols] += val


def _norm_pre(x, gain, *, name, tr=256):
    t, d = x.shape
    tr = _pick(t, tr, 8)

    def body(x_ref, g_ref, h_ref):
        xv = x_ref[...]
        h_ref[...] = (xv * _rstd(xv) * g_ref[...]).astype(BF16)

    return pl.pallas_call(
        body, name=name, grid=(t // tr,), in_specs=[_row_spec(tr, d), _vec_spec(d)], out_specs=_row_spec(tr, d),
        out_shape=jax.ShapeDtypeStruct((t, d), BF16), compiler_params=_cparams(("parallel",)),
    )(x, gain)


def _post_then_pre(x, y, post_gain, pre_gain, *, name, tr=256):
    t, d = x.shape
    tr = _pick(t, tr, 8)

    def body(x_ref, y_ref, pg_ref, ng_ref, xn_ref, h_ref):
        yv = y_ref[...]
        xn = x_ref[...] + yv * _rstd(yv) * pg_ref[...]
        xn_ref[...] = xn
        h_ref[...] = (xn * _rstd(xn) * ng_ref[...]).astype(BF16)

    return pl.pallas_call(
        body, name=name, grid=(t // tr,),
        in_specs=[_row_spec(tr, d), _row_spec(tr, d), _vec_spec(d), _vec_spec(d)],
        out_specs=[_row_spec(tr, d), _row_spec(tr, d)],
        out_shape=[jax.ShapeDtypeStruct((t, d), F32), jax.ShapeDtypeStruct((t, d), BF16)],
        compiler_params=_cparams(("parallel",)),
    )(x, y, post_gain, pre_gain)


def _norm_bwd(dy, n, r, gain):
    dn = dy * gain
    return r * (dn - n * jnp.mean(dn * n, axis=-1, keepdims=True))


def _loss_head(x, y, post_gain, target, *, name, tr=256):
    t, d = x.shape
    tr = _pick(t, tr, 8)

    def body(x_ref, y_ref, pg_ref, t_ref, loss_ref, dx_ref, dy_ref, dpg_ref):
        i = pl.program_id(0)
        yv = y_ref[...]
        r = _rstd(yv)
        n = yv * r
        err = x_ref[...] + n * pg_ref[...] - t_ref[...]
        dx = err * (1.0 / d)
        dx_ref[...] = dx
        part = 0.5 * jnp.sum(jnp.mean(err * err, axis=-1, keepdims=True), axis=0, keepdims=True)
        _acc_rows(loss_ref, i, jnp.broadcast_to(part, (1, LANES)))
        _acc_rows(dpg_ref, i, jnp.sum(dx * n, axis=0, keepdims=True))
        dy_ref[...] = _norm_bwd(dx, n, r, pg_ref[...]).astype(BF16)

    return pl.pallas_call(
        body, name=name, grid=(t // tr,),
        in_specs=[_row_spec(tr, d), _row_spec(tr, d), _vec_spec(d), _row_spec(tr, d)],
        out_specs=[_vec_spec(LANES), _row_spec(tr, d), _row_spec(tr, d), _vec_spec(d)],
        out_shape=[jax.ShapeDtypeStruct((1, LANES), F32), jax.ShapeDtypeStruct((t, d), F32),
                   jax.ShapeDtypeStruct((t, d), BF16), jax.ShapeDtypeStruct((1, d), F32)],
        compiler_params=_cparams(("arbitrary",)),
    )(x, y, post_gain, target)


def _mid_bwd(dx_out, dh, x, pre_gain, y_prev, post_gain_prev, *, name, tr=256):
    t, d = x.shape
    tr = _pick(t, tr, 8)

    def body(dxo_ref, dh_ref, x_ref, ng_ref, y_ref, pg_ref, dx_ref, dy_ref, dng_ref, dpg_ref):
        i = pl.program_id(0)
        xv = x_ref[...]
        r = _rstd(xv)
        xh = xv * r
        dhv = dh_ref[...]
        _acc_rows(dng_ref, i, jnp.sum(dhv * xh, axis=0, keepdims=True))
        dx = dxo_ref[...] + _norm_bwd(dhv, xh, r, ng_ref[...])
        dx_ref[...] = dx
        yv = y_ref[...]
        ry = _rstd(yv)
        n = yv * ry
        _acc_rows(dpg_ref, i, jnp.sum(dx * n, axis=0, keepdims=True))
        dy_ref[...] = _norm_bwd(dx, n, ry, pg_ref[...]).astype(BF16)

    return pl.pallas_call(
        body, name=name, grid=(t // tr,),
        in_specs=[_row_spec(tr, d), _row_spec(tr, d), _row_spec(tr, d), _vec_spec(d), _row_spec(tr, d), _vec_spec(d)],
        out_specs=[_row_spec(tr, d), _row_spec(tr, d), _vec_spec(d), _vec_spec(d)],
        out_shape=[jax.ShapeDtypeStruct((t, d), F32), jax.ShapeDtypeStruct((t, d), BF16),
                   jax.ShapeDtypeStruct((1, d), F32), jax.ShapeDtypeStruct((1, d), F32)],
        compiler_params=_cparams(("arbitrary",)),
    )(dx_out, dh, x, pre_gain, y_prev, post_gain_prev)


def _first_bwd(dx_out, dh, x, pre_gain, *, name, tr=256):
    t, d = x.shape
    tr = _pick(t, tr, 8)

    def body(dxo_ref, dh_ref, x_ref, ng_ref, dx_ref, dng_ref):
        i = pl.program_id(0)
        xv = x_ref[...]
        r = _rstd(xv)
        xh = xv * r
        dhv = dh_ref[...]
        _acc_rows(dng_ref, i, jnp.sum(dhv * xh, axis=0, keepdims=True))
        dx_ref[...] = dxo_ref[...] + _norm_bwd(dhv, xh, r, ng_ref[...])

    return pl.pallas_call(
        body, name=name, grid=(t // tr,),
        in_specs=[_row_spec(tr, d), _row_spec(tr, d), _row_spec(tr, d), _vec_spec(d)],
        out_specs=[_row_spec(tr, d), _vec_spec(d)],
        out_shape=[jax.ShapeDtypeStruct((t, d), F32), jax.ShapeDtypeStruct((1, d), F32)],
        compiler_params=_cparams(("arbitrary",)),
    )(dx_out, dh, x, pre_gain)


def _sigmoid(x):
    return 1.0 / (1.0 + jnp.exp(-x))


def _log_sigmoid(x):
    return jnp.minimum(x, 0.0) - jnp.log(1.0 + jnp.exp(-jnp.abs(x)))


_GELU_C = math.sqrt(2.0 / math.pi)


_GELU_A = 0.044715


def _gelu_parts(x, with_grad=True):
    x2 = x * x
    h = 0.5 * jnp.tanh(x * (_GELU_C + (_GELU_C * _GELU_A) * x2)) + 0.5
    val = x * h
    if not with_grad:
        return val, None
    return val, h * (1.0 + (1.0 - h) * (x * (2.0 * _GELU_C + (6.0 * _GELU_C * _GELU_A) * x2)))


def _split3(x):
    hi = x.astype(BF16)
    r1 = x - hi.astype(F32)
    mid = r1.astype(BF16)
    lo = (r1 - mid.astype(F32)).astype(BF16)
    return hi, mid, lo


def _tri_matmul(tri_bf16, x):
    hi, mid, lo = _split3(x)
    return _dot_nn(tri_bf16, hi) + _dot_nn(tri_bf16, mid) + _dot_nn(tri_bf16, lo)


def _gla_dims(d):
    dk, dv = d // 2, d
    return dk, dv, dk // GLA_HEADS, dv // GLA_HEADS


def _col_pieces(a, b, lay):
    ws, wp = lay
    out = []
    while a < b:
        j = a // ws
        end = min(b, (j + 1) * ws)
        out.append((j * wp + a - j * ws, end - a))
        a = end
    return out


def _load_cols(ref, a, b, lay):
    parts = [ref[:, s:s + n] for s, n in _col_pieces(a, b, lay)]
    return parts[0] if len(parts) == 1 else jnp.concatenate(parts, axis=1)


def _store_cols(ref, a, val, lay):
    off = 0
    for s, n in _col_pieces(a, a + val.shape[1], lay):
        ref[:, s:s + n] = val[:, off:off + n]
        off += n


def _gate_window(c_r, lay):
    (start, _), = _col_pieces(c_r, c_r + GLA_GATE_RANK, lay)
    assert (start % lay[1]) + LANES <= lay[1]
    return slice(start, start + LANES)


def _gla_gates(glr, k, w2_ref, b_ref):
    z = _dot_nn(glr.astype(BF16), w2_ref[...].astype(BF16)) + b_ref[...]
    la = _log_sigmoid(z) * (1.0 / GLA_TAU)
    row = lax.broadcasted_iota(jnp.int32, (CHUNK, CHUNK), 0)
    col = lax.broadcasted_iota(jnp.int32, (CHUNK, CHUNK), 1)
    incl = (row >= col).astype(BF16)
    bcum = _tri_matmul(incl, la)
    b_end = bcum[CHUNK - 1:CHUNK, :]
    e_rest = jnp.exp(b_end - bcum)
    return z, e_rest, k * e_rest, jnp.exp(b_end)


def _gla_fwd(proj, w2p, b_gate, o_gain, lay, *, name):
    t, wcols = proj.shape
    d = o_gain.shape[1]
    dk, dv, dkh, dvh = _gla_dims(d)
    nc = t // CHUNK
    c_k, c_v, c_g, c_r = dk, 2 * dk, 2 * dk + dv, 2 * dk + 2 * dv
    scale = dkh ** -0.5

    def body(p_ref, w2_ref, b_ref, og_ref, o_ref, a_ref, sb_ref, sfin_ref, s_ref):
        i = pl.program_id(0)

        @pl.when(i == 0)
        def _():
            s_ref[...] = jnp.zeros_like(s_ref)

        q = _load_cols(p_ref, 0, dk, lay) * scale
        k = _load_cols(p_ref, c_k, c_k + dk, lay)
        glr = p_ref[:, _gate_window(c_r, lay)]
        _, _, kdec, decay = _gla_gates(glr, k, w2_ref, b_ref)
        for h in range(GLA_HEADS):
            ks = slice(h * dkh, (h + 1) * dkh)
            vs = slice(h * dvh, (h + 1) * dvh)
            v_h = _load_cols(p_ref, c_v + h * dvh, c_v + (h + 1) * dvh, lay)
            g_h = _load_cols(p_ref, c_g + h * dvh, c_g + (h + 1) * dvh, lay)
            s_old = s_ref[h]
            sb_ref[0, h] = s_old
            s_new = s_old * decay[:, ks] + _dot_tn(v_h.astype(BF16), kdec[:, ks].astype(BF16))
            s_ref[h] = s_new
            o_h = _dot_nt(q[:, ks].astype(BF16), s_new.astype(BF16))
            o_ref[:, vs] = o_h
            on = o_h * _rstd(o_h)
            a_ref[:, vs] = (on * og_ref[:, vs] * (g_h * _sigmoid(g_h))).astype(BF16)

        @pl.when(i == nc - 1)
        def _():
            sfin_ref[...] = s_ref[...]

    full = lambda *shape: pl.BlockSpec(shape, lambda i: (0,) * len(shape))
    return pl.pallas_call(
        body, name=name, grid=(nc,),
        in_specs=[pl.BlockSpec((CHUNK, wcols), lambda i: (i, 0)), full(LANES, dk), full(1, dk), full(1, dv)],
        out_specs=[pl.BlockSpec((CHUNK, dv), lambda i: (i, 0)), pl.BlockSpec((CHUNK, dv), lambda i: (i, 0)),
                   pl.BlockSpec((1, GLA_HEADS, dvh, dkh), lambda i: (i, 0, 0, 0)), full(GLA_HEADS, dvh, dkh)],
        out_shape=[jax.ShapeDtypeStruct((t, dv), F32), jax.ShapeDtypeStruct((t, dv), BF16),
                   jax.ShapeDtypeStruct((nc, GLA_HEADS, dvh, dkh), F32),
                   jax.ShapeDtypeStruct((GLA_HEADS, dvh, dkh), F32)],
        scratch_shapes=[pltpu.VMEM((GLA_HEADS, dvh, dkh), F32)],
        compiler_params=_cparams(("arbitrary",)),
    )(proj, w2p, b_gate, o_gain)


def _gla_bwd(da, o, proj, w2p, b_gate, o_gain, s_before, s_final, lay, *, name):
    t, wcols = proj.shape
    d = o_gain.shape[1]
    dk, dv, dkh, dvh = _gla_dims(d)
    nc = t // CHUNK
    c_k, c_v, c_g, c_r = dk, 2 * dk, 2 * dk + dv, 2 * dk + 2 * dv
    scale = dkh ** -0.5

    def body(da_ref, o_ref, p_ref, w2_ref, b_ref, og_ref, sb_ref, sfin_ref,
             dp_ref, dog_ref, db_ref, dw2_ref, s_ref, gc_ref, dkd_ref):
        i = pl.program_id(0)

        @pl.when(i == 0)
        def _():
            s_ref[...] = sfin_ref[...]
            gc_ref[...] = jnp.zeros_like(gc_ref)

        ws, wp = lay
        for j in range(N_CHIPS):
            dp_ref[:, j * wp + ws:(j + 1) * wp] = jnp.zeros((CHUNK, wp - ws), BF16)
        q = _load_cols(p_ref, 0, dk, lay) * scale
        k = _load_cols(p_ref, c_k, c_k + dk, lay)
        glr = p_ref[:, _gate_window(c_r, lay)]
        z, e_rest, kdec, decay = _gla_gates(glr, k, w2_ref, b_ref)
        ddecay = []
        for h in range(GLA_HEADS):
            ks = slice(h * dkh, (h + 1) * dkh)
            vs = slice(h * dvh, (h + 1) * dvh)
            v_h = _load_cols(p_ref, c_v + h * dvh, c_v + (h + 1) * dvh, lay)
            g_h = _load_cols(p_ref, c_g + h * dvh, c_g + (h + 1) * dvh, lay)
            da_h = da_ref[:, vs]
            o_h = o_ref[:, vs]
            og_h = og_ref[:, vs]
            r = _rstd(o_h)
            on = o_h * r
            sg = _sigmoid(g_h)
            silu = g_h * sg
            _acc_rows(dog_ref, i, jnp.sum(da_h * silu * on, axis=0, keepdims=True), vs)
            _store_cols(dp_ref, c_g + h * dvh, (da_h * (on * og_h) * (sg * (1.0 + g_h * (1.0 - sg)))).astype(BF16),
                        lay)
            don = da_h * silu * og_h
            do_h = (r * (don - on * jnp.mean(don * on, axis=-1, keepdims=True))).astype(BF16)
            s_cur = s_ref[h]
            _store_cols(dp_ref, h * dkh, (_dot_nn(do_h, s_cur.astype(BF16)) * scale).astype(BF16), lay)
            g_tot = gc_ref[h] + _dot_tn(do_h, q[:, ks].astype(BF16))
            g_bf = g_tot.astype(BF16)
            dkd_ref[:, ks] = _dot_nn(v_h.astype(BF16), g_bf)
            _store_cols(dp_ref, c_v + h * dvh, _dot_nt(kdec[:, ks].astype(BF16), g_bf).astype(BF16), lay)
            s_prev = sb_ref[0, h]
            ddecay.append(jnp.sum(g_tot * s_prev, axis=0, keepdims=True))
            gc_ref[h] = g_tot * decay[:, ks]
            s_ref[h] = s_prev
        dkdec = dkd_ref[...]
        _store_cols(dp_ref, c_k, (dkdec * e_rest).astype(BF16), lay)
        d_e = dkdec * kdec
        row = lax.broadcasted_iota(jnp.int32, (CHUNK, CHUNK), 0)
        col = lax.broadcasted_iota(jnp.int32, (CHUNK, CHUNK), 1)
        excl = (row > col).astype(BF16)
        dla = jnp.concatenate(ddecay, axis=1) * decay + _tri_matmul(excl, d_e)
        dz = dla * (1.0 / GLA_TAU) * (1.0 - _sigmoid(z))
        _acc_rows(db_ref, i, jnp.sum(dz, axis=0, keepdims=True))
        dz_bf = dz.astype(BF16)
        dw2 = _dot_tn(glr.astype(BF16), dz_bf)

        @pl.when(i == 0)
        def _():
            dw2_ref[...] = dw2

        @pl.when(i > 0)
        def _():
            dw2_ref[...] += dw2

        dp_ref[:, _gate_window(c_r, lay)] = _dot_nt(dz_bf, w2_ref[...].astype(BF16)).astype(BF16)

    rev = lambda i: (nc - 1 - i, 0)
    full = lambda *shape: pl.BlockSpec(shape, lambda i: (0,) * len(shape))
    return pl.pallas_call(
        body, name=name, grid=(nc,),
        in_specs=[pl.BlockSpec((CHUNK, dv), rev), pl.BlockSpec((CHUNK, dv), rev), pl.BlockSpec((CHUNK, wcols), rev),
                  full(LANES, dk), full(1, dk), full(1, dv),
                  pl.BlockSpec((1, GLA_HEADS, dvh, dkh), lambda i: (nc - 1 - i, 0, 0, 0)), full(GLA_HEADS, dvh, dkh)],
        out_specs=[pl.BlockSpec((CHUNK, wcols), rev), full(1, dv), full(1, dk), full(LANES, dk)],
        out_shape=[jax.ShapeDtypeStruct((t, wcols), BF16), jax.ShapeDtypeStruct((1, dv), F32),
                   jax.ShapeDtypeStruct((1, dk), F32), jax.ShapeDtypeStruct((LANES, dk), F32)],
        scratch_shapes=[pltpu.VMEM((GLA_HEADS, dvh, dkh), F32), pltpu.VMEM((GLA_HEADS, dvh, dkh), F32),
                        pltpu.VMEM((CHUNK, dk), F32)],
        compiler_params=_cparams(("arbitrary",)),
    )(da, o, proj, w2p, b_gate, o_gain, s_before, s_final)


def _sgu_mid(p_ref, lg_ref, lb_ref, ws_ref, bst_ref, w, with_grad=True):
    gd = w // SGU_GROUPS
    u_act, du_fac = _gelu_parts(p_ref[:, 0:w], with_grad)
    vf, dv_fac = _gelu_parts(p_ref[:, w:2 * w], with_grad)
    mu = jnp.mean(vf, axis=-1, keepdims=True)
    cen = vf - mu
    rstd = lax.rsqrt(jnp.mean(cen * cen, axis=-1, keepdims=True) + EPS)
    xh = cen * rstd
    vn = (xh * lg_ref[...] + lb_ref[...]).astype(BF16)
    vs = [_dot_nn(ws_ref[g].astype(BF16), vn[:, g * gd:(g + 1) * gd]) + bst_ref[:, g:g + 1]
          for g in range(SGU_GROUPS)]
    return u_act, du_fac, dv_fac, rstd, xh, vn, vs


def _sgu_fwd(proj, ln_gain, ln_bias, ws_masked, bs_t, *, name):
    t, w3 = proj.shape
    w = w3 // 3
    gd = w // SGU_GROUPS
    nb = t // SGU_BLOCK

    def body(p_ref, lg_ref, lb_ref, ws_ref, bst_ref, a_ref):
        u_act, _, _, _, _, _, vs = _sgu_mid(p_ref, lg_ref, lb_ref, ws_ref, bst_ref, w, with_grad=False)
        for g in range(SGU_GROUPS):
            cs = slice(g * gd, (g + 1) * gd)
            gate = p_ref[:, 2 * w + g * gd:2 * w + (g + 1) * gd]
            a_ref[:, cs] = (u_act[:, cs] * vs[g] * (gate * _sigmoid(gate))).astype(BF16)

    full = lambda *shape: pl.BlockSpec(shape, lambda i: (0,) * len(shape))
    return pl.pallas_call(
        body, name=name, grid=(nb,),
        in_specs=[pl.BlockSpec((SGU_BLOCK, w3), lambda i: (i, 0)), full(1, w), full(1, w),
                  full(SGU_GROUPS, SGU_BLOCK, SGU_BLOCK), full(SGU_BLOCK, SGU_GROUPS)],
        out_specs=pl.BlockSpec((SGU_BLOCK, w), lambda i: (i, 0)),
        out_shape=jax.ShapeDtypeStruct((t, w), BF16),
        compiler_params=_cparams(("parallel",)),
    )(proj, ln_gain, ln_bias, ws_masked, bs_t)


def _sgu_bwd(da, proj, ln_gain, ln_bias, ws_masked, ws_masked_t, bs_t, *, name):
    t, w3 = proj.shape
    w = w3 // 3
    gd = w // SGU_GROUPS
    nb = t // SGU_BLOCK

    def body(da_ref, p_ref, lg_ref, lb_ref, ws_ref, wst_ref, bst_ref, dp_ref, dws_ref, dbst_ref, dlg_ref, dlb_ref,
             dvn_ref):
        i = pl.program_id(0)
        u_act, du_fac, dv_fac, rstd, xh, vn, vs = _sgu_mid(p_ref, lg_ref, lb_ref, ws_ref, bst_ref, w)
        for g in range(SGU_GROUPS):
            cs = slice(g * gd, (g + 1) * gd)
            gate = p_ref[:, 2 * w + g * gd:2 * w + (g + 1) * gd]
            sg = _sigmoid(gate)
            silu = gate * sg
            da_g = da_ref[:, cs]
            ua_g = u_act[:, cs]
            dp_ref[:, cs] = (da_g * vs[g] * silu * du_fac[:, cs]).astype(BF16)
            dp_ref[:, 2 * w + g * gd:2 * w + (g + 1) * gd] = (
                da_g * ua_g * vs[g] * (sg * (1.0 + gate * (1.0 - sg)))).astype(BF16)
            dvs = da_g * ua_g * silu
            dvs_bf = dvs.astype(BF16)
            dvn_ref[:, cs] = _dot_nn(wst_ref[g].astype(BF16), dvs_bf)
            dws = _dot_nt(dvs_bf, vn[:, cs])
            dbs = jnp.sum(dvs, axis=1, keepdims=True)

            @pl.when(i == 0)
            def _():
                dws_ref[g] = dws
                dbst_ref[:, g:g + 1] = dbs

            @pl.when(i > 0)
            def _():
                dws_ref[g] += dws
                dbst_ref[:, g:g + 1] += dbs

        dvn = dvn_ref[...]
        _acc_rows(dlg_ref, i, jnp.sum(dvn * xh, axis=0, keepdims=True))
        _acc_rows(dlb_ref, i, jnp.sum(dvn, axis=0, keepdims=True))
        dxh = dvn * lg_ref[...]
        dvf = rstd * (dxh - jnp.mean(dxh, axis=-1, keepdims=True)
                      - xh * jnp.mean(dxh * xh, axis=-1, keepdims=True))
        dp_ref[:, w:2 * w] = (dvf * dv_fac).astype(BF16)

    full = lambda *shape: pl.BlockSpec(shape, lambda i: (0,) * len(shape))
    return pl.pallas_call(
        body, name=name, grid=(nb,),
        in_specs=[pl.BlockSpec((SGU_BLOCK, w), lambda i: (i, 0)), pl.BlockSpec((SGU_BLOCK, w3), lambda i: (i, 0)),
                  full(1, w), full(1, w), full(SGU_GROUPS, SGU_BLOCK, SGU_BLOCK),
                  full(SGU_GROUPS, SGU_BLOCK, SGU_BLOCK), full(SGU_BLOCK, SGU_GROUPS)],
        out_specs=[pl.BlockSpec((SGU_BLOCK, w3), lambda i: (i, 0)), full(SGU_GROUPS, SGU_BLOCK, SGU_BLOCK),
                   full(SGU_BLOCK, SGU_GROUPS), full(1, w), full(1, w)],
        out_shape=[jax.ShapeDtypeStruct((t, w3), BF16), jax.ShapeDtypeStruct((SGU_GROUPS, SGU_BLOCK, SGU_BLOCK), F32),
                   jax.ShapeDtypeStruct((SGU_BLOCK, SGU_GROUPS), F32), jax.ShapeDtypeStruct((1, w), F32),
                   jax.ShapeDtypeStruct((1, w), F32)],
        scratch_shapes=[pltpu.VMEM((SGU_BLOCK, w), F32)],
        compiler_params=_cparams(("arbitrary",)),
    )(da, proj, ln_gain, ln_bias, ws_masked, ws_masked_t, bs_t)


def _tile2d(rows, cols, block_bytes, row_unit):
    if rows % row_unit == 0:
        return _pick(rows, max(row_unit, block_bytes // (4 * cols)), row_unit), cols
    return rows, _pick(cols, max(LANES, block_bytes // (4 * rows)))


def _adamw(w, g, m, v, *, name, block_bytes=1 << 20, after=None):
    rows, cols = w.shape
    tr, tc = _tile2d(rows, cols, block_bytes, 8)
    g_rows = g.shape[0]
    assert g_rows == rows or tr == rows
    extra_specs, extra_args = ([], []) if after is None else ([pl.BlockSpec(memory_space=pl.ANY)], [after])

    def body(w_ref, g_ref, m_ref, v_ref, *rest):
        go_ref, d_ref, mo_ref, vo_ref = rest[len(extra_args):]
        gv = g_ref[0:tr, :]
        go_ref[...] = gv
        mn = ADAM_B1 * m_ref[...] + (1.0 - ADAM_B1) * gv
        vn = ADAM_B2 * v_ref[...] + (1.0 - ADAM_B2) * (gv * gv)
        m_hat = mn / (1.0 - ADAM_B1 ** ADAM_STEP)
        v_hat = vn / (1.0 - ADAM_B2 ** ADAM_STEP)
        d_ref[...] = -ADAM_LR * (m_hat / (jnp.sqrt(v_hat) + ADAM_EPS) + ADAM_WD * w_ref[...])
        mo_ref[...] = mn
        vo_ref[...] = vn

    spec = pl.BlockSpec((tr, tc), lambda i, j: (i, j))
    g_spec = spec if g_rows == rows else pl.BlockSpec((g_rows, tc), lambda i, j: (0, j))
    return pl.pallas_call(
        body, name=name, grid=(rows // tr, cols // tc), in_specs=[spec, g_spec, spec, spec] + extra_specs,
        out_specs=[spec] * 4, out_shape=[jax.ShapeDtypeStruct((rows, cols), F32)] * 4,
        compiler_params=_cparams(("parallel", "parallel")),
    )(w, g, m, v, *extra_args)


def _matmul_dw_pair(a_me, a_sib, b_me, b_sib, core_idx, *, shards_on, name, after=None, part=(0, 1)):
    T, M = a_me.shape
    N = b_me.shape[1]
    if shards_on == "rows":
        p, count = part
        tm, hc = M // N_CHIPS, N // 2
        hp = hc // count
        tn = _pick(hp, 1024)
        per = hp // tn
        grid = (N_CHIPS, per)
        a_spec = pl.BlockSpec((T, tm), lambda i, n, h: (0, i))
        b_me_spec = pl.BlockSpec((T, tn), lambda i, n, h: (0, (h[0] * count + p) * per + n))
        b_sib_spec = pl.BlockSpec((T, tn), lambda i, n, h: (0, p * per + n))
        out_spec = pl.BlockSpec((None, tm, tn), lambda i, n, h: (i, 0, n))
        out_shape = jax.ShapeDtypeStruct((N_CHIPS, tm, hp), BF16)
    else:
        tm, hc = _pick(M, 1024), N // N_CHIPS // 2
        grid = (M // tm, N_CHIPS)
        a_spec = pl.BlockSpec((T, tm), lambda i, j, h: (0, i))
        b_me_spec = pl.BlockSpec((T, hc), lambda i, j, h: (0, 2 * j + h[0]))
        b_sib_spec = pl.BlockSpec((T, hc), lambda i, j, h: (0, j))
        out_spec = pl.BlockSpec((None, tm, hc), lambda i, j, h: (j, i, 0))
        out_shape = jax.ShapeDtypeStruct((N_CHIPS, M, hc), BF16)
    extra_specs, extra_args = ([], []) if after is None else ([pl.BlockSpec(memory_space=pl.ANY)], [after])

    def body(h_ref, am_ref, as_ref, bm_ref, bs_ref, *rest):
        o_ref = rest[len(extra_args)]
        o_ref[...] = (_dot_tn(am_ref[...], bm_ref[...]) + _dot_tn(as_ref[...], bs_ref[...])).astype(BF16)

    grid_spec = pltpu.PrefetchScalarGridSpec(
        num_scalar_prefetch=1, grid=grid, in_specs=[a_spec, a_spec, b_me_spec, b_sib_spec] + extra_specs,
        out_specs=out_spec)
    return pl.pallas_call(
        body, name=name, grid_spec=grid_spec, out_shape=out_shape, compiler_params=_cparams(("parallel", "parallel")),
    )(core_idx, a_me, a_sib, b_me, b_sib, *extra_args)


def _chip_sum(pair, landed, slots, *, name, block_bytes=1 << 20, part=(0, 1), into=None):
    p, count = part
    _, r, hp = pair.shape
    tr, tc = _tile2d(r, hp, block_bytes, 16)
    ncb = hp // tc
    extra_specs, extra_args = ([], []) if into is None else ([pl.BlockSpec(memory_space=pl.ANY)], [into])

    def body(s_ref, own_ref, l0_ref, l1_ref, l2_ref, *rest):
        rest[-1][...] = ((own_ref[...].astype(F32) + l0_ref[...].astype(F32)) + l1_ref[...].astype(F32)
                         ) + l2_ref[...].astype(F32)

    def slab(which):
        return pl.BlockSpec((None, tr, tc), lambda i, k, s: (s[which], i, k))

    grid_spec = pltpu.PrefetchScalarGridSpec(
        num_scalar_prefetch=1, grid=(r // tr, ncb),
        in_specs=[slab(0), slab(1), slab(2), slab(3)] + extra_specs,
        out_specs=pl.BlockSpec((tr, tc), lambda i, k, s: (i, (s[4] * count + p) * ncb + k)))
    return pl.pallas_call(
        body, name=name, grid_spec=grid_spec, out_shape=jax.ShapeDtypeStruct((r, 2 * hp * count), F32),
        input_output_aliases={} if into is None else {5: 0},
        compiler_params=_cparams(("parallel", "parallel")),
    )(slots, pair, landed, landed, landed, *extra_args)


def _stack_sum(x, *, name, out_dtype=F32, block_bytes=1 << 20):
    s, r, c = x.shape
    tr = _pick(r, max(8, block_bytes // (4 * c)), 16) if r % 16 == 0 else r

    def body(x_ref, o_ref):
        acc = x_ref[0].astype(F32)
        for j in range(1, s):
            acc = acc + x_ref[j].astype(F32)
        o_ref[...] = acc.astype(out_dtype)

    return pl.pallas_call(
        body, name=name, grid=(r // tr,),
        in_specs=[pl.BlockSpec((s, tr, c), lambda i: (0, i, 0))], out_specs=pl.BlockSpec((tr, c), lambda i: (i, 0)),
        out_shape=jax.ShapeDtypeStruct((r, c), out_dtype), compiler_params=_cparams(("parallel",)),
    )(x)


HBM = pl.BlockSpec(memory_space=pltpu.HBM)


def _place():
    x, y, c = lax.axis_index("x"), lax.axis_index("y"), lax.axis_index("c")
    other_chips = [(1 - x, y), (x, 1 - y), (1 - x, 1 - y)]
    return x, y, c, other_chips


def _handshake(peers):
    barrier = pltpu.get_barrier_semaphore()
    for peer in peers:
        pl.semaphore_signal(barrier, inc=1, device_id=peer, device_id_type=MESH)
    pl.semaphore_wait(barrier, len(peers))


def _sibling():
    x, y, c, _ = _place()
    return [(x, y, 1 - c)]


def _same_core_chips():
    x, y, c, chips = _place()
    return [(cx, cy, c) for cx, cy in chips]


def _same_core_neighbours():
    x, y, c, _ = _place()
    return [(1 - x, y, c), (x, 1 - y, c)]


def _split_params(cid):
    return pltpu.CompilerParams(has_side_effects=SIDE_EFFECT, collective_id=cid)


def _half_cols(cols, which):
    hc = cols // 2
    return pl.ds(pl.multiple_of(which * hc, LANES), hc)


SEM = pl.BlockSpec(memory_space=pltpu.SEMAPHORE)
ANY = pl.BlockSpec(memory_space=pl.ANY)
SIDE_EFFECT = pltpu.SideEffectType.DATAFLOW_SIDE_EFFECTING
TOKEN_SHAPE = (8, LANES)


def _hbm(shape, dtype):
    return pltpu.HBM(shape, dtype)


def _in_hbm(a):
    return pltpu.with_memory_space_constraint(a, pltpu.HBM)


def _gather_copy(src_ref, land_ref, ssem, rsem, k, chip_of_block, to, c):
    cols = src_ref.shape[1]
    return pltpu.make_async_remote_copy(
        src_ref=src_ref.at[:, _half_cols(cols, c)], dst_ref=land_ref.at[chip_of_block, :, _half_cols(cols, c)],
        send_sem=ssem.at[k], recv_sem=rsem.at[k], device_id=to, device_id_type=MESH)


NEIGHBOURS = (0, 1)
ALL_CHIPS = (0, 1, 2)


def _gather_start(shards, *, name, cid, after=(), relayed=(), own_slab=None):
    n = len(shards)
    after = list(after)

    def body(*refs):
        srcs, lands = refs[:n], refs[n:2 * n]
        outs = refs[2 * n + len(after):]
        token = outs[-1]
        _handshake(_same_core_chips())
        x, y, c, chips = _place()
        me = 2 * x + y
        for a in range(n):
            ssem, rsem = outs[4 * a], outs[4 * a + 1]
            for k in NEIGHBOURS if a in relayed else ALL_CHIPS:
                cx, cy = chips[k]
                _gather_copy(srcs[a], lands[a], ssem, rsem, k, me, (cx, cy, c), c).start()
        token[...] = jnp.zeros_like(token)

    out_shape, out_specs, aliases = [], [], {}
    for a, s in enumerate(shards):
        out_shape += [pltpu.SemaphoreType.DMA((3,)), pltpu.SemaphoreType.DMA((3,)), _hbm(s.shape, s.dtype),
                      _hbm((N_CHIPS,) + s.shape, s.dtype)]
        out_specs += [SEM, SEM, HBM, HBM]
        aliases[a] = 4 * a + 2
        aliases[n + a] = 4 * a + 3
    out_shape.append(jax.ShapeDtypeStruct(TOKEN_SHAPE, F32))
    out_specs.append(pl.BlockSpec(memory_space=pltpu.VMEM))
    lands = [lax.empty((N_CHIPS,) + s.shape, s.dtype) for s in shards]
    if own_slab is not None:
        lands = [lax.dynamic_update_slice(land, s[None], (own_slab, 0, 0)) for land, s in zip(lands, shards)]
    lands = [_in_hbm(land) for land in lands]
    res = pl.pallas_call(
        body, name=name, in_specs=[HBM] * (2 * n) + [ANY] * len(after), out_specs=out_specs, out_shape=out_shape,
        input_output_aliases=aliases, compiler_params=_split_params(cid),
    )(*[_in_hbm(s) for s in shards], *lands, *after)
    return [tuple(res[4 * a:4 * a + 4]) for a in range(n)], res[-1]


def _wait_call(wait_fn, parts, after, *, name):
    ssem, rsem, src, land = parts
    after = list(after) if isinstance(after, (list, tuple)) else [after]

    def body(src_ref, land_ref, ssem_ref, rsem_ref, *rest):
        wait_fn(src_ref, land_ref, ssem_ref, rsem_ref)

    return pl.pallas_call(
        body, name=name, in_specs=[HBM, HBM, SEM, SEM] + [ANY] * len(after), out_specs=[HBM, HBM],
        out_shape=[_hbm(src.shape, src.dtype), _hbm(land.shape, land.dtype)], input_output_aliases={0: 0, 1: 1},
        compiler_params=pltpu.CompilerParams(has_side_effects=SIDE_EFFECT),
    )(src, land, ssem, rsem, *after)


def _gather_wait(parts, after, *, name, ks=ALL_CHIPS):
    def wait(src_ref, land_ref, ssem_ref, rsem_ref):
        x, y, c, chips = _place()
        for k in ks:
            cx, cy = chips[k]
            cp = _gather_copy(src_ref, land_ref, ssem_ref, rsem_ref, k, 2 * cx + cy, (x, y, c), c)
            cp.wait_send()
            cp.wait_recv()

    return _wait_call(wait, parts, after, name=name)


def _relay_copy(buf_ref, ssem, rsem, k, slab, to, c):
    hr = buf_ref.shape[1] // 2
    part = buf_ref.at[slab, pl.ds(k * hr, hr), _half_cols(buf_ref.shape[2], c)]
    return pltpu.make_async_remote_copy(
        src_ref=part, dst_ref=part, send_sem=ssem.at[k], recv_sem=rsem.at[k], device_id=to, device_id_type=MESH)


def _relay_start(land, *, name, cid):
    def body(buf_ref, ssem, rsem, buf_out, token):
        _handshake(_same_core_neighbours())
        x, y, c, _ = _place()
        _relay_copy(buf_ref, ssem, rsem, 0, 2 * (1 - x) + y, (x, 1 - y, c), c).start()
        _relay_copy(buf_ref, ssem, rsem, 1, 2 * x + 1 - y, (1 - x, y, c), c).start()
        token[...] = jnp.zeros_like(token)

    res = pl.pallas_call(
        body, name=name, in_specs=[HBM], out_specs=[SEM, SEM, HBM, pl.BlockSpec(memory_space=pltpu.VMEM)],
        out_shape=[pltpu.SemaphoreType.DMA((2,)), pltpu.SemaphoreType.DMA((2,)), _hbm(land.shape, land.dtype),
                   jax.ShapeDtypeStruct(TOKEN_SHAPE, F32)],
        input_output_aliases={0: 2}, compiler_params=_split_params(cid),
    )(land)
    return tuple(res[:3]), res[3]


def _relay_wait(parts, after, *, name):
    ssem, rsem, buf = parts
    after = list(after) if isinstance(after, (list, tuple)) else [after]

    def body(buf_ref, ssem_ref, rsem_ref, *rest):
        x, y, c, _ = _place()
        diagonal = 2 * (1 - x) + 1 - y
        _relay_copy(buf_ref, ssem_ref, rsem_ref, 0, 2 * (1 - x) + y, (x, y, c), c).wait_send()
        _relay_copy(buf_ref, ssem_ref, rsem_ref, 1, 2 * x + 1 - y, (x, y, c), c).wait_send()
        _relay_copy(buf_ref, ssem_ref, rsem_ref, 0, diagonal, (x, y, c), c).wait_recv()
        _relay_copy(buf_ref, ssem_ref, rsem_ref, 1, diagonal, (x, y, c), c).wait_recv()

    return pl.pallas_call(
        body, name=name, in_specs=[HBM, SEM, SEM] + [ANY] * len(after), out_specs=HBM,
        out_shape=_hbm(buf.shape, buf.dtype), input_output_aliases={0: 0},
        compiler_params=pltpu.CompilerParams(has_side_effects=SIDE_EFFECT),
    )(buf, ssem, rsem, *after)


def _forward_copy(buf_ref, ssem, rsem, k, slab, which, to):
    part = buf_ref.at[slab, :, _half_cols(buf_ref.shape[2], which)]
    return pltpu.make_async_remote_copy(
        src_ref=part, dst_ref=part, send_sem=ssem.at[k], recv_sem=rsem.at[k], device_id=to, device_id_type=MESH)


def _sibling_forward(land, *, name, cid, ks=ALL_CHIPS):
    def body(_, buf, send_sems, recv_sems):
        _handshake(_sibling())
        x, y, c, chips = _place()
        copies = []
        for k in ks:
            cx, cy = chips[k]
            cp = _forward_copy(buf, send_sems, recv_sems, k, 2 * cx + cy, c, (x, y, 1 - c))
            cp.start()
            copies.append(cp)
        for k in ks:
            cx, cy = chips[k]
            _forward_copy(buf, send_sems, recv_sems, k, 2 * cx + cy, 1 - c, (x, y, c)).wait_recv()
        for cp in copies:
            cp.wait_send()

    return pl.pallas_call(
        body, name=name, in_specs=[HBM], out_specs=HBM, out_shape=jax.ShapeDtypeStruct(land.shape, land.dtype),
        input_output_aliases={0: 0},
        scratch_shapes=[pltpu.SemaphoreType.DMA((3,)), pltpu.SemaphoreType.DMA((3,))],
        compiler_params=pltpu.CompilerParams(collective_id=cid),
    )(land)


def _forward_start(land, *, name, cid, ks=ALL_CHIPS):
    def body(buf_ref, ssem, rsem, buf_out, token):
        _handshake(_sibling())
        x, y, c, chips = _place()
        for k in ks:
            cx, cy = chips[k]
            _forward_copy(buf_ref, ssem, rsem, k, 2 * cx + cy, c, (x, y, 1 - c)).start()
        token[...] = jnp.zeros_like(token)

    res = pl.pallas_call(
        body, name=name, in_specs=[HBM], out_specs=[SEM, SEM, HBM, pl.BlockSpec(memory_space=pltpu.VMEM)],
        out_shape=[pltpu.SemaphoreType.DMA((3,)), pltpu.SemaphoreType.DMA((3,)), _hbm(land.shape, land.dtype),
                   jax.ShapeDtypeStruct(TOKEN_SHAPE, F32)],
        input_output_aliases={0: 2}, compiler_params=_split_params(cid),
    )(land)
    return tuple(res[:3]), res[3]


def _forward_wait(parts, after, *, name, ks=ALL_CHIPS):
    ssem, rsem, buf = parts
    after = list(after) if isinstance(after, (list, tuple)) else [after]

    def body(buf_ref, ssem_ref, rsem_ref, *rest):
        x, y, c, chips = _place()
        for k in ks:
            cx, cy = chips[k]
            _forward_copy(buf_ref, ssem_ref, rsem_ref, k, 2 * cx + cy, c, (x, y, c)).wait_send()
            _forward_copy(buf_ref, ssem_ref, rsem_ref, k, 2 * cx + cy, 1 - c, (x, y, c)).wait_recv()

    return pl.pallas_call(
        body, name=name, in_specs=[HBM, SEM, SEM] + [ANY] * len(after), out_specs=HBM,
        out_shape=_hbm(buf.shape, buf.dtype), input_output_aliases={0: 0},
        compiler_params=pltpu.CompilerParams(has_side_effects=SIDE_EFFECT),
    )(buf, ssem, rsem, *after)


def _share_copy(buf_ref, ssem, rsem, a, which, to):
    part = buf_ref.at[:, _half_cols(buf_ref.shape[1], which)]
    return pltpu.make_async_remote_copy(
        src_ref=part, dst_ref=part, send_sem=ssem.at[a], recv_sem=rsem.at[a], device_id=to, device_id_type=MESH)


def _share_start(arrays, *, name, cid):
    n = len(arrays)

    def body(*refs):
        bufs, ssem, rsem, token = refs[:n], refs[n], refs[n + 1], refs[-1]
        _handshake(_sibling())
        x, y, c, _ = _place()
        for a in range(n):
            _share_copy(bufs[a], ssem, rsem, a, c, (x, y, 1 - c)).start()
        token[...] = jnp.zeros_like(token)

    res = pl.pallas_call(
        body, name=name, in_specs=[HBM] * n,
        out_specs=[SEM, SEM] + [HBM] * n + [pl.BlockSpec(memory_space=pltpu.VMEM)],
        out_shape=[pltpu.SemaphoreType.DMA((n,)), pltpu.SemaphoreType.DMA((n,))]
        + [_hbm(b.shape, b.dtype) for b in arrays] + [jax.ShapeDtypeStruct(TOKEN_SHAPE, F32)],
        input_output_aliases={a: 2 + a for a in range(n)}, compiler_params=_split_params(cid),
    )(*[_in_hbm(b) for b in arrays])
    return (res[0], res[1], list(res[2:2 + n])), res[-1]


def _share_wait(parts, after, *, name):
    ssem, rsem, bufs = parts
    n = len(bufs)
    after = list(after) if isinstance(after, (list, tuple)) else [after]

    def body(*refs):
        buf_refs, ssem_ref, rsem_ref = refs[:n], refs[n], refs[n + 1]
        x, y, c, _ = _place()
        for a in range(n):
            _share_copy(buf_refs[a], ssem_ref, rsem_ref, a, c, (x, y, c)).wait_send()
            _share_copy(buf_refs[a], ssem_ref, rsem_ref, a, 1 - c, (x, y, c)).wait_recv()

    return pl.pallas_call(
        body, name=name, in_specs=[HBM] * n + [SEM, SEM] + [ANY] * len(after), out_specs=[HBM] * n,
        out_shape=[_hbm(b.shape, b.dtype) for b in bufs], input_output_aliases={a: a for a in range(n)},
        compiler_params=pltpu.CompilerParams(has_side_effects=SIDE_EFFECT),
    )(*bufs, ssem, rsem, *after)


def _scatter_copy(src_ref, land_ref, ssem, rsem, k, src_slab, dst_slab, to):
    return pltpu.make_async_remote_copy(
        src_ref=src_ref.at[src_slab], dst_ref=land_ref.at[dst_slab], send_sem=ssem.at[k], recv_sem=rsem.at[k],
        device_id=to, device_id_type=MESH)


def _scatter_start(part, *, name, cid):
    def start(src_ref, land_ref, ssem, rsem):
        x, y, c, chips = _place()
        me = 2 * x + y
        for k, (cx, cy) in enumerate(chips):
            _scatter_copy(src_ref, land_ref, ssem, rsem, k, 2 * cx + cy, me, (cx, cy, c)).start()

    return _split_start(start, _same_core_chips, part, part.shape, N_CHIPS - 1, name=name, cid=cid)


def _scatter_wait(parts, after, *, name):
    def wait(src_ref, land_ref, ssem_ref, rsem_ref):
        x, y, c, chips = _place()
        for k, (cx, cy) in enumerate(chips):
            idx = 2 * cx + cy
            cp = _scatter_copy(src_ref, land_ref, ssem_ref, rsem_ref, k, idx, idx, (x, y, c))
            cp.wait_send()
            cp.wait_recv()

    return _wait_call(wait, parts, after, name=name)


def _split_start(start_fn, peers_fn, src, land_shape, n_sems, *, name, cid):
    def body(src_ref, land_ref, ssem, rsem, src_out, land_out, token):
        _handshake(peers_fn())
        start_fn(src_ref, land_ref, ssem, rsem)
        token[...] = jnp.zeros_like(token)

    res = pl.pallas_call(
        body, name=name, in_specs=[HBM, HBM], out_specs=[SEM, SEM, HBM, HBM, pl.BlockSpec(memory_space=pltpu.VMEM)],
        out_shape=[pltpu.SemaphoreType.DMA((n_sems,)), pltpu.SemaphoreType.DMA((n_sems,)), _hbm(src.shape, src.dtype),
                   _hbm(land_shape, src.dtype), jax.ShapeDtypeStruct(TOKEN_SHAPE, F32)],
        input_output_aliases={0: 2, 1: 3}, compiler_params=_split_params(cid),
    )(_in_hbm(src), _in_hbm(lax.empty(land_shape, src.dtype)))
    return tuple(res[:4]), res[4]


def _sibling_copies(src_ref, land_ref, ssem, rsem, k0, groups, which, to):
    def copy(k, src, dst):
        return pltpu.make_async_remote_copy(
            src_ref=src, dst_ref=dst, send_sem=ssem.at[k], recv_sem=rsem.at[k], device_id=to, device_id_type=MESH)

    if groups == 0:
        return [copy(k0, src_ref, land_ref)]
    hw = src_ref.shape[1] // groups // 2
    return [copy(k0 + j, src_ref.at[:, pl.ds(pl.multiple_of((2 * j + which) * hw, LANES), hw)],
                 land_ref.at[:, j * hw:(j + 1) * hw]) for j in range(groups)]


def _to_sibling_start(items, *, name, cid):
    n = len(items)
    shapes = [a.shape if g == 0 else (a.shape[0], a.shape[1] // 2) for a, g in items]
    first = [sum(max(g, 1) for _, g in items[:k]) for k in range(n + 1)]

    def body(*refs):
        srcs, lands, ssem, rsem, token = refs[:n], refs[n:2 * n], refs[2 * n], refs[2 * n + 1], refs[-1]
        _handshake(_sibling())
        x, y, c, _ = _place()
        for k, (_, g) in enumerate(items):
            for cp in _sibling_copies(srcs[k], lands[k], ssem, rsem, first[k], g, 1 - c, (x, y, 1 - c)):
                cp.start()
        token[...] = jnp.zeros_like(token)

    res = pl.pallas_call(
        body, name=name, in_specs=[HBM] * (2 * n),
        out_specs=[SEM, SEM] + [HBM] * (2 * n) + [pl.BlockSpec(memory_space=pltpu.VMEM)],
        out_shape=[pltpu.SemaphoreType.DMA((first[n],)), pltpu.SemaphoreType.DMA((first[n],))]
        + [_hbm(a.shape, a.dtype) for a, _ in items] + [_hbm(s, a.dtype) for s, (a, _) in zip(shapes, items)]
        + [jax.ShapeDtypeStruct(TOKEN_SHAPE, F32)],
        input_output_aliases={k: 2 + k for k in range(2 * n)}, compiler_params=_split_params(cid),
    )(*[_in_hbm(a) for a, _ in items], *[_in_hbm(lax.empty(s, a.dtype)) for s, (a, _) in zip(shapes, items)])
    return [(res[0], res[1], first[k], g, res[2 + k], res[2 + n + k]) for k, (_, g) in enumerate(items)], res[-1]


def _from_sibling(flight, after, *, name):
    ssem, rsem, k0, groups, src, land = flight

    def wait(src_ref, land_ref, ssem_ref, rsem_ref):
        x, y, c, _ = _place()
        for cp in _sibling_copies(src_ref, land_ref, ssem_ref, rsem_ref, k0, groups, 1 - c, (x, y, c)):
            cp.wait_send()
            cp.wait_recv()

    return _wait_call(wait, (ssem, rsem, src, land), after, name=name)


def _dev_peers(x, y, c, chips):
    return [(x, y, 1 - c)] + [(cx, cy, c) for cx, cy in chips] + [(cx, cy, 1 - c) for cx, cy in chips]


def _dev_gather_start(part, *, name, cid):
    def start(src_ref, land_ref, ssem, rsem):
        x, y, c, chips = _place()
        for k, to in enumerate(_dev_peers(x, y, c, chips)):
            pltpu.make_async_remote_copy(
                src_ref=src_ref, dst_ref=land_ref.at[4 * x + 2 * y + c], send_sem=ssem.at[k], recv_sem=rsem.at[k],
                device_id=to, device_id_type=MESH).start()

    return _split_start(start, lambda: _dev_peers(*_place()), part, (N_DEV,) + part.shape, N_DEV - 1, name=name,
                        cid=cid)


def _dev_gather_wait(parts, after, *, name):
    def wait(src_ref, land_ref, ssem_ref, rsem_ref):
        x, y, c, chips = _place()
        for k, (px, py, pc) in enumerate(_dev_peers(x, y, c, chips)):
            cp = pltpu.make_async_remote_copy(
                src_ref=src_ref, dst_ref=land_ref.at[4 * px + 2 * py + pc], send_sem=ssem_ref.at[k],
                recv_sem=rsem_ref.at[k], device_id=(x, y, c), device_id_type=MESH)
            cp.wait_send()
            cp.wait_recv()

    return _wait_call(wait, parts, after, name=name)[1]


def _sibling_share_halves(arrays, *, name, cid):
    n = len(arrays)

    def body(*refs):
        bufs = refs[n:2 * n]
        send_sems, recv_sems = refs[2 * n:]
        _handshake(_sibling())
        x, y, c, _ = _place()
        copies = []
        for a in range(n):
            mine = bufs[a].at[:, _half_cols(bufs[a].shape[1], c)]
            cp = pltpu.make_async_remote_copy(
                src_ref=mine, dst_ref=mine, send_sem=send_sems.at[a], recv_sem=recv_sems.at[a],
                device_id=(x, y, 1 - c), device_id_type=MESH)
            cp.start()
            copies.append(cp)
        for a in range(n):
            theirs = bufs[a].at[:, _half_cols(bufs[a].shape[1], 1 - c)]
            pltpu.make_async_remote_copy(
                src_ref=theirs, dst_ref=theirs, send_sem=send_sems.at[a], recv_sem=recv_sems.at[a],
                device_id=(x, y, c), device_id_type=MESH).wait_recv()
        for cp in copies:
            cp.wait_send()

    return pl.pallas_call(
        body, name=name, in_specs=[HBM] * n, out_specs=[HBM] * n,
        out_shape=[jax.ShapeDtypeStruct(h.shape, h.dtype) for h in arrays],
        input_output_aliases={a: a for a in range(n)},
        scratch_shapes=[pltpu.SemaphoreType.DMA((n,)), pltpu.SemaphoreType.DMA((n,))],
        compiler_params=pltpu.CompilerParams(collective_id=cid),
    )(*arrays)


def _pack(arrays, rows_multiple=16, width=LANES):
    flat = jnp.concatenate([a.astype(F32).reshape(-1) for a in arrays])
    total = flat.shape[0]
    rows = -(-total // width)
    rows = -(-rows // rows_multiple) * rows_multiple
    return jnp.pad(flat, (0, rows * width - total)).reshape(rows, width)


def _unpack(buf, shapes):
    flat = buf.reshape(-1)
    out, off = [], 0
    for s in shapes:
        n = math.prod(s)
        out.append(flat[off:off + n].reshape(s))
        off += n
    return out


def kernel(x, norm_pre, norm_post, gla_w_in, gla_w_gate2, gla_b_gate, gla_o_gain, gla_w_out, sgu_w_in, sgu_ln_gain, sgu_ln_bias, sgu_w_spatial, sgu_b_spatial, sgu_w_out, loss_target, m_norm_pre, m_norm_post, m_gla_w_in, m_gla_w_gate2, m_gla_b_gate, m_gla_o_gain, m_gla_w_out, m_sgu_w_in, m_sgu_ln_gain, m_sgu_ln_bias, m_sgu_w_spatial, m_sgu_b_spatial, m_sgu_w_out, v_norm_pre, v_norm_post, v_gla_w_in, v_gla_w_gate2, v_gla_b_gate, v_gla_o_gain, v_gla_w_out, v_sgu_w_in, v_sgu_ln_gain, v_sgu_ln_bias, v_sgu_w_spatial, v_sgu_b_spatial, v_sgu_w_out):
    _, t, d = x.shape
    dk = d // 2
    ws = gla_w_in.shape[2]
    wp = -(-ws // LANES) * LANES
    lay = (ws, wp)
    chip =2 * lax.axis_index("x") + lax.axis_index("y")
    core = lax.axis_index("c")
    core_idx = core.astype(jnp.int32).reshape(1)
    others = jnp.arange(N_CHIPS - 1, dtype=jnp.int32)
    others = others + (others >= chip).astype(jnp.int32)
    slots = jnp.concatenate([chip.astype(jnp.int32).reshape(1), others, core_idx])

    x0 = x[0]
    target = loss_target[0]

    wt_in_g, mt_in_g, vt_in_g = gla_w_in[0].T, m_gla_w_in[0].T, v_gla_w_in[0].T

    small_shard = _pack([gla_w_gate2[0], sgu_ln_gain[0], sgu_ln_bias[0]], rows_multiple=8, width=2 * LANES)
    own = [small_shard, jnp.pad(wt_in_g.astype(BF16), ((0, wp - ws), (0, 0)))]
    in_flight, token = _gather_start(own, name="gather_start_a", cid=0, relayed=(1,))

    def with_sibling_and_own(mine, land, name, cid):
        return lax.dynamic_update_slice(_sibling_forward(land, name=name + "_share", cid=cid), mine[None],
                                        (chip, 0, 0))

    h0 = _norm_pre(x0, norm_pre[0:1] + token[0:1, 0:1], name="pre0")
    g_small = with_sibling_and_own(*_gather_wait(in_flight[0], h0, name="w_small_wait"), "w_small", 12)
    mine, land = _gather_wait(in_flight[1], [g_small, wt_in_g, mt_in_g, vt_in_g], name="w_gla_in_wait", ks=NEIGHBOURS)
    relay, token = _relay_start(land, name="w_gla_in_relay", cid=11)
    crossing, token = _forward_start(relay[2], name="w_gla_in_share_near", cid=22, ks=NEIGHBOURS)
    own_later = [(p[0] + token[0, 0]).astype(BF16) for p in (gla_w_out, sgu_w_in, sgu_w_out)]
    in_flight_later, token = _gather_start(own_later, name="gather_start_b", cid=1, after=[token], own_slab=chip)
    in_flight = in_flight + in_flight_later
    land = _relay_wait((relay[0], relay[1], crossing[2]), token, name="w_gla_in_relay_wait")
    land = _forward_wait((crossing[0], crossing[1], land), token, name="w_gla_in_share_near_wait", ks=NEIGHBOURS)
    land = _sibling_forward(land, name="w_gla_in_share_far", cid=13, ks=(2,))
    wt_g = lax.dynamic_update_slice(land, mine[None], (chip, 0, 0)).reshape(N_CHIPS * wp, d)

    def behind(small, token):
        return small + token[0:1, 0:1]

    def arriving(i, after, name):
        mine, land = _gather_wait(in_flight[i], after, name=name + "_wait")
        crossing, token = _forward_start(land, name=name + "_share", cid=i)
        return (mine, crossing), token

    def arrived(pending, after, name):
        _, crossing = pending
        return _forward_wait(crossing, after, name=name + "_share_wait")

    shard_shapes = [gla_w_gate2.shape[1:], sgu_ln_gain.shape[1:], sgu_ln_bias.shape[1:]]
    per_chip = [_unpack(g_small[j], shard_shapes) for j in range(N_CHIPS)]
    w2_full = jnp.concatenate([p[0] for p in per_chip], axis=1)
    ln_gain = jnp.concatenate([p[1] for p in per_chip], axis=0)[None, :]
    ln_bias = jnp.concatenate([p[2] for p in per_chip], axis=0)[None, :]
    w2p = jnp.pad(w2_full, ((0, LANES - GLA_GATE_RANK), (0, 0)))

    pos_chunk = jnp.arange(SGU_BLOCK) // CHUNK
    mask = pos_chunk[:, None] >= pos_chunk[None, :]
    ws_masked = jnp.where(mask[None], sgu_w_spatial[0], 0.0)
    ws_masked_t = ws_masked.transpose(0, 2, 1)
    bs_t = sgu_b_spatial[0].T

    proj0 = _matmul(h0, wt_g, mode="nt", out_dtype=F32, name="gla_in", tn=wp)
    pending, tok = arriving(2, proj0, "w_gla_out")
    o0, a0, s_before, s_final = _gla_fwd(proj0, w2p, behind(gla_b_gate, tok), gla_o_gain, lay, name="gla_scan")
    w_out_g = arrived(pending, a0, "w_gla_out").reshape(d, d)
    y0 = _matmul(a0, w_out_g, mode="nn", out_dtype=F32, name="gla_out", tn=1024)
    pending, tok = arriving(3, y0, "w_sgu_in")
    x1, h1 = _post_then_pre(x0, y0, behind(norm_post[0:1], tok), norm_pre[1:2], name="post0_pre1")
    g_wi_s = arrived(pending, h1, "w_sgu_in")
    pending, tok = arriving(4, g_wi_s, "w_sgu_out")
    proj1 = _matmul(h1, g_wi_s, mode="nn", out_dtype=F32, name="sgu_in", b_shards=True, after=tok, tn=768)
    a1 = _sgu_fwd(proj1, ln_gain, ln_bias, ws_masked, bs_t, name="sgu_gate")
    w_out_s = arrived(pending, a1, "w_sgu_out").reshape(d, d)
    acts, tok = _to_sibling_start([(a1, 0), (a0, 0), (h1, 0), (h0, 1)], name="acts_to_sibling", cid=5)
    a1, a0, h1, h0 = [f[4] for f in acts]
    y1 = _matmul(a1, w_out_s, mode="nn", out_dtype=F32, name="sgu_out", after=tok, tn=1024)
    loss_part, dx2, dy1, d_post1 = _loss_head(x1, y1, norm_post[1:2], target, name="loss_head")

    def pair_gradient(a_sent, b_sent, after, shards_on, name, cid):
        a_me, a_sib = _from_sibling(a_sent, after, name=name + "_a_wait")
        b_me, b_sib = _from_sibling(b_sent, [a_sib] + list(after), name=name + "_b_wait")
        pair = _matmul_dw_pair(a_me, a_sib, b_me, b_sib, core_idx, shards_on=shards_on,
                               name=name + "_pair")
        return _scatter_start(pair, name=name + "_start", cid=cid)

    def reduced(flight, after, name):
        pair, landed = _scatter_wait(flight, after, name=name + "_wait")
        return _chip_sum(pair, landed, slots, name=name + "_sum")

    (dy1_sent,), tok = _to_sibling_start([(dy1, 1)], name="dy1_to_sibling", cid=6)
    dy1 = dy1_sent[4]
    da1 = _matmul(dy1, w_out_s, mode="nt", out_dtype=F32, name="d_sgu_act", after=tok, tn=1024)
    fl_wo_s, tok = pair_gradient(acts[0], dy1_sent, [da1], "rows", "g_sgu_out", 15)
    dproj1, d_ws, d_bs_t, d_lg, d_lb = _sgu_bwd(da1, proj1, ln_gain, behind(ln_bias, tok), ws_masked, ws_masked_t,
                                                bs_t, name="sgu_gate_bwd")
    (dp1_sent,), tok = _to_sibling_start([(dproj1, N_CHIPS)], name="dproj1_to_sibling", cid=7)
    dproj1 = dp1_sent[4]
    dh1 = _matmul_nt_shards(dproj1, g_wi_s, out_dtype=F32, name="d_sgu_h", after=tok)
    fl_wi_s, tok = pair_gradient(acts[2], dp1_sent, [dh1], "cols", "g_sgu_in", 16)
    dx1, dy0, d_pre1, d_post0 = _mid_bwd(dx2, dh1, x1, behind(norm_pre[1:2], tok), y0, norm_post[0:1],
                                         name="pre1_post0_bwd")
    (dy0_sent,), tok = _to_sibling_start([(dy0, 1)], name="dy0_to_sibling", cid=8)
    dy0 = dy0_sent[4]
    da0 = _matmul(dy0, w_out_g, mode="nt", out_dtype=F32, name="d_gla_act", after=tok, tn=1024)
    fl_wo_g, tok = pair_gradient(acts[1], dy0_sent, [da0], "rows", "g_gla_out", 17)
    dproj0, d_og, d_bg, d_w2p = _gla_bwd(da0, o0, proj0, w2p, behind(gla_b_gate, tok), gla_o_gain, s_before, s_final,
                                         lay, name="gla_scan_bwd")
    early_shapes = [norm_post.shape, gla_b_gate.shape, gla_o_gain.shape, sgu_w_spatial.shape, sgu_b_spatial.shape,
                    (1, GLA_GATE_RANK, dk), (1, d), (1, d), (1, LANES)]
    early_part = _pack([jnp.concatenate([d_post0, d_post1], axis=0), d_bg, d_og, jnp.where(mask[None], d_ws, 0.0)[None],
                        d_bs_t.T[None], d_w2p[:GLA_GATE_RANK][None], d_lg, d_lb, loss_part])
    early_flight, tok = _dev_gather_start(early_part, name="small_early_start", cid=20)
    (dp0_sent,), tok_sent = _to_sibling_start([(dproj0, 0)], name="dproj0_to_sibling", cid=9)
    dproj0 = dp0_sent[4]
    dh0 = _matmul(dproj0, wt_g, mode="nn", out_dtype=F32, name="d_gla_h", after=tok_sent)
    a_me, a_sib = _from_sibling(dp0_sent, [dh0, tok], name="g_gla_in_a_wait")
    b_me, b_sib = _from_sibling(acts[3], [a_sib, dh0], name="g_gla_in_b_wait")
    fl_wi_g, tok_scatter = [], None
    for p in range(2):
        pair = _matmul_dw_pair(a_me, a_sib, b_me, b_sib, core_idx, shards_on="rows", part=(p, 2),
                               name=f"g_gla_in_pair{p}", after=tok_scatter)
        flight, tok_scatter = _scatter_start(pair, name=f"g_gla_in_start{p}", cid=18 + p)
        fl_wi_g.append(flight)
    r_wo_s = reduced(fl_wo_s, tok_scatter, "g_sgu_out")
    r_wi_s = reduced(fl_wi_s, r_wo_s, "g_sgu_in")
    r_wo_g = reduced(fl_wo_g, r_wi_s, "g_gla_out")
    sharing, tok = _share_start([r_wo_s, r_wi_s, r_wo_g], name="grads_share_a", cid=10)
    grad_x, d_pre0 = _first_bwd(dx1, dh0, x0, behind(norm_pre[0:1], tok), name="pre0_bwd")

    late_part = _pack([jnp.concatenate([d_pre0, d_pre1], axis=0)])
    late_flight, tok = _dev_gather_start(late_part, name="small_late_start", cid=21)

    def big_update(w, g, m, v, name, after=None):
        return [u[None] for u in _adamw(w[0], g, m[0], v[0], name=name, after=after)]

    g_wo_sgu, g_wi_sgu, g_wo_gla = _share_wait(sharing, [grad_x, tok], name="grads_share_a_wait")
    u_wi_sgu = big_update(sgu_w_in, g_wi_sgu, m_sgu_w_in, v_sgu_w_in, "adamw_sgu_w_in")
    u_wo_gla = big_update(gla_w_out, g_wo_gla, m_gla_w_out, v_gla_w_out, "adamw_gla_w_out", after=u_wi_sgu[1])

    r_wi_g, behind_this = None, u_wo_gla[1]
    for p, flight in enumerate(fl_wi_g):
        pair, landed = _scatter_wait(flight, behind_this, name=f"g_gla_in_wait{p}")
        r_wi_g = behind_this = _chip_sum(pair, landed, slots, part=(p, 2), into=r_wi_g, name=f"g_gla_in_sum{p}")
    gt_wi_gla, = _sibling_share_halves([r_wi_g], name="grads_share_b", cid=14)
    u_wi_gla_t = _adamw(wt_in_g, gt_wi_gla, mt_in_g, vt_in_g, name="adamw_gla_w_in")
    u_wi_gla = [u.T[None] for u in u_wi_gla_t]
    u_wo_sgu = big_update(sgu_w_out, g_wo_sgu, m_sgu_w_out, v_sgu_w_out, "adamw_sgu_w_out", after=u_wi_gla_t[1])

    def summed_over_devices(part, flight, after, shapes, name):
        land = _dev_gather_wait(flight, after, name=name + "_wait")
        every = lax.dynamic_update_slice(land, part[None], (2 * chip + core, 0, 0))
        return _unpack(_stack_sum(every, name=name + "_sum"), shapes)

    (g_post, g_bg, g_og, g_wsp, g_bsp, g_w2_full, g_lg_full, g_lb_full, loss_vec) = summed_over_devices(
        early_part, early_flight, u_wo_sgu[1], early_shapes, "small_early")
    g_pre, = summed_over_devices(late_part, late_flight, loss_vec, [norm_pre.shape], "small_late")
    loss = loss_vec[0, 0]
    g_w2 = lax.dynamic_slice_in_dim(g_w2_full, chip * (dk // N_CHIPS), dk // N_CHIPS, axis=2)
    g_lg = lax.dynamic_slice_in_dim(g_lg_full, chip * (d // N_CHIPS), d // N_CHIPS, axis=1)
    g_lb = lax.dynamic_slice_in_dim(g_lb_full, chip * (d // N_CHIPS), d // N_CHIPS, axis=1)

    small_w = [norm_pre, norm_post, gla_b_gate, gla_o_gain, sgu_w_spatial, sgu_b_spatial, gla_w_gate2, sgu_ln_gain,
               sgu_ln_bias]
    small_g = [g_pre, g_post, g_bg, g_og, g_wsp, g_bsp, g_w2, g_lg, g_lb]
    small_m = [m_norm_pre, m_norm_post, m_gla_b_gate, m_gla_o_gain, m_sgu_w_spatial, m_sgu_b_spatial, m_gla_w_gate2,
               m_sgu_ln_gain, m_sgu_ln_bias]
    small_v = [v_norm_pre, v_norm_post, v_gla_b_gate, v_gla_o_gain, v_sgu_w_spatial, v_sgu_b_spatial, v_gla_w_gate2,
               v_sgu_ln_gain, v_sgu_ln_bias]
    own_shapes = [w.shape for w in small_w]
    _, s_dl, s_m, s_v = _adamw(_pack(small_w), _pack(small_g), _pack(small_m), _pack(small_v), name="adamw_small")
    dl_s, m_s, v_s = _unpack(s_dl, own_shapes), _unpack(s_m, own_shapes), _unpack(s_v, own_shapes)

    def ordered(small, kind):
        pre, post, bg, og, wsp, bsp, w2, lg, lb = small
        return [pre, post, u_wi_gla[kind], w2, bg, og, u_wo_gla[kind], u_wi_sgu[kind], lg, lb, wsp, bsp, u_wo_sgu[kind]]

    return (loss, grad_x[None], *ordered(small_g, 0), *ordered(dl_s, 1), *ordered(m_s, 2), *ordered(v_s, 3))
```

```python
import math

import jax
import jax.numpy as jnp
from jax import lax
from jax.experimental import pallas as pl
from jax.experimental.pallas import tpu as pltpu

F32 = jnp.float32
BF16 = jnp.bfloat16
MESH = pl.DeviceIdType.MESH

EPS = 1e-6
CHUNK = 64
GLA_HEADS = 4
GLA_GATE_RANK = 16
GLA_TAU = 16.0
SGU_BLOCK = 128
SGU_GROUPS = 8
N_CHIPS = 4
N_DEV = 8
LANES = 128

ADAM_LR = 0.001
ADAM_B1 = 0.9
ADAM_B2 = 0.999
ADAM_EPS = 1e-08
ADAM_WD = 0.01
ADAM_STEP = 10

VMEM_LIMIT = 56 * 1024 * 1024
ELEMENTWISE_BLOCK_BYTES = 2 << 20


def _cparams(sem=None):
    return pltpu.CompilerParams(dimension_semantics=sem, vmem_limit_bytes=VMEM_LIMIT)


def _pick(n, cap, unit=LANES):
    best = None
    for t in range(unit, min(n, cap) + 1, unit):
        if n % t == 0:
            best = t
    assert best is not None, (n, cap, unit)
    return best


def _dot(a, b, dims):
    return lax.dot_general(a, b, (dims, ((), ())), preferred_element_type=F32)


def _dot_nn(a, b):
    return _dot(a, b, ((1,), (0,)))


def _dot_nt(a, b):
    return _dot(a, b, ((1,), (1,)))


def _dot_tn(a, b):
    return _dot(a, b, ((0,), (0,)))


def _matmul(a, b, *, mode, out_dtype, name, tm=1024, tn=512, b_shards=False, after=None):
    M, K = a.shape
    if b_shards:
        ns, Kb, bc = b.shape
        N, tn = ns * bc, _pick(bc, tn)
        per = bc // tn
        b_spec = pl.BlockSpec((None, K, tn), lambda i, j: (j // per, 0, j % per))
    elif mode == "nt":
        N, Kb = b.shape
        tn = _pick(N, tn)
        b_spec = pl.BlockSpec((tn, K), lambda i, j: (j, 0))
    else:
        Kb, N = b.shape
        tn = _pick(N, tn)
        b_spec = pl.BlockSpec((K, tn), lambda i, j: (0, j))
    assert K == Kb and a.dtype == b.dtype == BF16, (a.shape, b.shape, mode)
    tm = _pick(M, tm)
    dims = ((1,), (1,)) if mode == "nt" else ((1,), (0,))
    extra_specs, extra_args = ([], []) if after is None else ([pl.BlockSpec(memory_space=pl.ANY)], [after])

    def body(a_ref, b_ref, *rest):
        rest[-1][...] = _dot(a_ref[...], b_ref[...], dims).astype(out_dtype)

    return pl.pallas_call(
        body, name=name, grid=(M // tm, N // tn),
        in_specs=[pl.BlockSpec((tm, K), lambda i, j: (i, 0)), b_spec] + extra_specs,
        out_specs=pl.BlockSpec((tm, tn), lambda i, j: (i, j)), out_shape=jax.ShapeDtypeStruct((M, N), out_dtype),
        compiler_params=_cparams(("parallel", "parallel")),
    )(a, b, *extra_args)


def _matmul_nt_shards(a, b, *, out_dtype, name, tm=1024, tn=512, after=None):
    M, K = a.shape
    ns, N, kc = b.shape
    assert K == ns * kc
    tm, tn = _pick(M, tm), _pick(N, tn)

    def body(a_ref, *rest):
        b_refs, o_ref = rest[:ns], rest[ns + (after is not None)]
        acc = _dot_nt(a_ref[:, 0:kc], b_refs[0][...])
        for j in range(1, ns):
            acc += _dot_nt(a_ref[:, j * kc:(j + 1) * kc], b_refs[j][...])
        o_ref[...] = acc.astype(out_dtype)

    def shard(j):
        return pl.BlockSpec((None, tn, kc), lambda i, n: (j, n, 0))

    extra_specs, extra_args = ([], []) if after is None else ([pl.BlockSpec(memory_space=pl.ANY)], [after])
    return pl.pallas_call(
        body, name=name, grid=(M // tm, N // tn),
        in_specs=[pl.BlockSpec((tm, K), lambda i, n: (i, 0))] + [shard(j) for j in range(ns)] + extra_specs,
        out_specs=pl.BlockSpec((tm, tn), lambda i, n: (i, n)), out_shape=jax.ShapeDtypeStruct((M, N), out_dtype),
        compiler_params=_cparams(("parallel", "parallel")),
    )(a, *([b] * ns), *extra_args)


def _rstd(x):
    return lax.rsqrt(jnp.mean(x * x, axis=-1, keepdims=True) + EPS)


def _row_spec(tr, d):
    return pl.BlockSpec((tr, d), lambda i: (i, 0))


def _vec_spec(d):
    return pl.BlockSpec((1, d), lambda i: (0, 0))


def _acc_rows(ref, i, val, cols=slice(None)):
    @pl.when(i == 0)
    def _():
        ref[:, cols] = val

    @pl.when(i > 0)
    def _():
        ref[:, cols] += val


def _norm_pre(x, gain, *, name, tr=256):
    t, d = x.shape
    tr = _pick(t, tr, 8)

    def body(x_ref, g_ref, h_ref):
        xv = x_ref[...]
        h_ref[...] = (xv * _rstd(xv) * g_ref[...]).astype(BF16)

    return pl.pallas_call(
        body, name=name, grid=(t // tr,), in_specs=[_row_spec(tr, d), _vec_spec(d)], out_specs=_row_spec(tr, d),
        out_shape=jax.ShapeDtypeStruct((t, d), BF16), compiler_params=_cparams(("parallel",)),
    )(x, gain)


def _post_then_pre(x, y, post_gain, pre_gain, *, name, tr=256):
    t, d = x.shape
    tr = _pick(t, tr, 8)

    def body(x_ref, y_ref, pg_ref, ng_ref, xn_ref, h_ref):
        yv = y_ref[...]
        xn = x_ref[...] + yv * _rstd(yv) * pg_ref[...]
        xn_ref[...] = xn
        h_ref[...] = (xn * _rstd(xn) * ng_ref[...]).astype(BF16)

    return pl.pallas_call(
        body, name=name, grid=(t // tr,),
        in_specs=[_row_spec(tr, d), _row_spec(tr, d), _vec_spec(d), _vec_spec(d)],
        out_specs=[_row_spec(tr, d), _row_spec(tr, d)],
        out_shape=[jax.ShapeDtypeStruct((t, d), F32), jax.ShapeDtypeStruct((t, d), BF16)],
        compiler_params=_cparams(("parallel",)),
    )(x, y, post_gain, pre_gain)


def _norm_bwd(dy, n, r, gain):
    dn = dy * gain
    return r * (dn - n * jnp.mean(dn * n, axis=-1, keepdims=True))


def _loss_head(x, y, post_gain, target, *, name, tr=256):
    t, d = x.shape
    tr = _pick(t, tr, 8)

    def body(x_ref, y_ref, pg_ref, t_ref, loss_ref, dx_ref, dy_ref, dpg_ref):
        i = pl.program_id(0)
        yv = y_ref[...]
        r = _rstd(yv)
        n = yv * r
        err = x_ref[...] + n * pg_ref[...] - t_ref[...]
        dx = err * (1.0 / d)
        dx_ref[...] = dx
        part = 0.5 * jnp.sum(jnp.mean(err * err, axis=-1, keepdims=True), axis=0, keepdims=True)
        _acc_rows(loss_ref, i, jnp.broadcast_to(part, (1, LANES)))
        _acc_rows(dpg_ref, i, jnp.sum(dx * n, axis=0, keepdims=True))
        dy_ref[...] = _norm_bwd(dx, n, r, pg_ref[...]).astype(BF16)

    return pl.pallas_call(
        body, name=name, grid=(t // tr,),
        in_specs=[_row_spec(tr, d), _row_spec(tr, d), _vec_spec(d), _row_spec(tr, d)],
        out_specs=[_vec_spec(LANES), _row_spec(tr, d), _row_spec(tr, d), _vec_spec(d)],
        out_shape=[jax.ShapeDtypeStruct((1, LANES), F32), jax.ShapeDtypeStruct((t, d), F32),
                   jax.ShapeDtypeStruct((t, d), BF16), jax.ShapeDtypeStruct((1, d), F32)],
        compiler_params=_cparams(("arbitrary",)),
    )(x, y, post_gain, target)


def _mid_bwd(dx_out, dh, x, pre_gain, y_prev, post_gain_prev, *, name, tr=256):
    t, d = x.shape
    tr = _pick(t, tr, 8)

    def body(dxo_ref, dh_ref, x_ref, ng_ref, y_ref, pg_ref, dx_ref, dy_ref, dng_ref, dpg_ref):
        i = pl.program_id(0)
        xv = x_ref[...]
        r = _rstd(xv)
        xh = xv * r
        dhv = dh_ref[...]
        _acc_rows(dng_ref, i, jnp.sum(dhv * xh, axis=0, keepdims=True))
        dx = dxo_ref[...] + _norm_bwd(dhv, xh, r, ng_ref[...])
        dx_ref[...] = dx
        yv = y_ref[...]
        ry = _rstd(yv)
        n = yv * ry
        _acc_rows(dpg_ref, i, jnp.sum(dx * n, axis=0, keepdims=True))
        dy_ref[...] = _norm_bwd(dx, n, ry, pg_ref[...]).astype(BF16)

    return pl.pallas_call(
        body, name=name, grid=(t // tr,),
        in_specs=[_row_spec(tr, d), _row_spec(tr, d), _row_spec(tr, d), _vec_spec(d), _row_spec(tr, d), _vec_spec(d)],
        out_specs=[_row_spec(tr, d), _row_spec(tr, d), _vec_spec(d), _vec_spec(d)],
        out_shape=[jax.ShapeDtypeStruct((t, d), F32), jax.ShapeDtypeStruct((t, d), BF16),
                   jax.ShapeDtypeStruct((1, d), F32), jax.ShapeDtypeStruct((1, d), F32)],
        compiler_params=_cparams(("arbitrary",)),
    )(dx_out, dh, x, pre_gain, y_prev, post_gain_prev)


def _first_bwd(dx_out, dh, x, pre_gain, *, name, tr=256):
    t, d = x.shape
    tr = _pick(t, tr, 8)

    def body(dxo_ref, dh_ref, x_ref, ng_ref, dx_ref, dng_ref):
        i = pl.program_id(0)
        xv = x_ref[...]
        r = _rstd(xv)
        xh = xv * r
        dhv = dh_ref[...]
        _acc_rows(dng_ref, i, jnp.sum(dhv * xh, axis=0, keepdims=True))
        dx_ref[...] = dxo_ref[...] + _norm_bwd(dhv, xh, r, ng_ref[...])

    return pl.pallas_call(
        body, name=name, grid=(t // tr,),
        in_specs=[_row_spec(tr, d), _row_spec(tr, d), _row_spec(tr, d), _vec_spec(d)],
        out_specs=[_row_spec(tr, d), _vec_spec(d)],
        out_shape=[jax.ShapeDtypeStruct((t, d), F32), jax.ShapeDtypeStruct((1, d), F32)],
        compiler_params=_cparams(("arbitrary",)),
    )(dx_out, dh, x, pre_gain)


def _sigmoid(x):
    return 1.0 / (1.0 + jnp.exp(-x))


def _log_sigmoid(x):
    return jnp.minimum(x, 0.0) - jnp.log(1.0 + jnp.exp(-jnp.abs(x)))


_GELU_C = math.sqrt(2.0 / math.pi)


_GELU_A = 0.044715


def _gelu_parts(x, with_grad=True):
    x2 = x * x
    h = 0.5 * jnp.tanh(x * (_GELU_C + (_GELU_C * _GELU_A) * x2)) + 0.5
    val = x * h
    if not with_grad:
        return val, None
    return val, h * (1.0 + (1.0 - h) * (x * (2.0 * _GELU_C + (6.0 * _GELU_C * _GELU_A) * x2)))


def _split3(x):
    hi = x.astype(BF16)
    r1 = x - hi.astype(F32)
    mid = r1.astype(BF16)
    lo = (r1 - mid.astype(F32)).astype(BF16)
    return hi, mid, lo


def _tri_matmul(tri_bf16, x):
    hi, mid, lo = _split3(x)
    return _dot_nn(tri_bf16, hi) + _dot_nn(tri_bf16, mid) + _dot_nn(tri_bf16, lo)


def _gla_dims(d):
    dk, dv = d // 2, d
    return dk, dv, dk // GLA_HEADS, dv // GLA_HEADS


def _col_pieces(a, b, lay):
    ws, wp = lay
    out = []
    while a < b:
        j = a // ws
        end = min(b, (j + 1) * ws)
        out.append((j * wp + a - j * ws, end - a))
        a = end
    return out


def _load_cols(ref, a, b, lay):
    parts = [ref[:, s:s + n] for s, n in _col_pieces(a, b, lay)]
    return parts[0] if len(parts) == 1 else jnp.concatenate(parts, axis=1)


def _store_cols(ref, a, val, lay):
    off = 0
    for s, n in _col_pieces(a, a + val.shape[1], lay):
        ref[:, s:s + n] = val[:, off:off + n]
        off += n


def _gate_window(c_r, lay):
    (start, _), = _col_pieces(c_r, c_r + GLA_GATE_RANK, lay)
    assert (start % lay[1]) + LANES <= lay[1]
    return slice(start, start + LANES)


def _gla_gates(glr, k, w2_ref, b_ref):
    z = _dot_nn(glr.astype(BF16), w2_ref[...].astype(BF16)) + b_ref[...]
    la = _log_sigmoid(z) * (1.0 / GLA_TAU)
    row = lax.broadcasted_iota(jnp.int32, (CHUNK, CHUNK), 0)
    col = lax.broadcasted_iota(jnp.int32, (CHUNK, CHUNK), 1)
    incl = (row >= col).astype(BF16)
    bcum = _tri_matmul(incl, la)
    b_end = bcum[CHUNK - 1:CHUNK, :]
    e_rest = jnp.exp(b_end - bcum)
    return z, e_rest, k * e_rest, jnp.exp(b_end)


def _gla_fwd(proj, w2p, b_gate, o_gain, lay, *, name):
    t, wcols = proj.shape
    d = o_gain.shape[1]
    dk, dv, dkh, dvh = _gla_dims(d)
    nc = t // CHUNK
    c_k, c_v, c_g, c_r = dk, 2 * dk, 2 * dk + dv, 2 * dk + 2 * dv
    scale = dkh ** -0.5

    def body(p_ref, w2_ref, b_ref, og_ref, o_ref, a_ref, sb_ref, sfin_ref, s_ref):
        i = pl.program_id(0)

        @pl.when(i == 0)
        def _():
            s_ref[...] = jnp.zeros_like(s_ref)

        q = _load_cols(p_ref, 0, dk, lay) * scale
        k = _load_cols(p_ref, c_k, c_k + dk, lay)
        glr = p_ref[:, _gate_window(c_r, lay)]
        _, _, kdec, decay = _gla_gates(glr, k, w2_ref, b_ref)
        for h in range(GLA_HEADS):
            ks = slice(h * dkh, (h + 1) * dkh)
            vs = slice(h * dvh, (h + 1) * dvh)
            v_h = _load_cols(p_ref, c_v + h * dvh, c_v + (h + 1) * dvh, lay)
            g_h = _load_cols(p_ref, c_g + h * dvh, c_g + (h + 1) * dvh, lay)
            s_old = s_ref[h]
            sb_ref[0, h] = s_old
            s_new = s_old * decay[:, ks] + _dot_tn(v_h.astype(BF16), kdec[:, ks].astype(BF16))
            s_ref[h] = s_new
            o_h = _dot_nt(q[:, ks].astype(BF16), s_new.astype(BF16))
            o_ref[:, vs] = o_h
            on = o_h * _rstd(o_h)
            a_ref[:, vs] = (on * og_ref[:, vs] * (g_h * _sigmoid(g_h))).astype(BF16)

        @pl.when(i == nc - 1)
        def _():
            sfin_ref[...] = s_ref[...]

    full = lambda *shape: pl.BlockSpec(shape, lambda i: (0,) * len(shape))
    return pl.pallas_call(
        body, name=name, grid=(nc,),
        in_specs=[pl.BlockSpec((CHUNK, wcols), lambda i: (i, 0)), full(LANES, dk), full(1, dk), full(1, dv)],
        out_specs=[pl.BlockSpec((CHUNK, dv), lambda i: (i, 0)), pl.BlockSpec((CHUNK, dv), lambda i: (i, 0)),
                   pl.BlockSpec((1, GLA_HEADS, dvh, dkh), lambda i: (i, 0, 0, 0)), full(GLA_HEADS, dvh, dkh)],
        out_shape=[jax.ShapeDtypeStruct((t, dv), F32), jax.ShapeDtypeStruct((t, dv), BF16),
                   jax.ShapeDtypeStruct((nc, GLA_HEADS, dvh, dkh), F32),
                   jax.ShapeDtypeStruct((GLA_HEADS, dvh, dkh), F32)],
        scratch_shapes=[pltpu.VMEM((GLA_HEADS, dvh, dkh), F32)],
        compiler_params=_cparams(("arbitrary",)),
    )(proj, w2p, b_gate, o_gain)


def _gla_bwd(da, o, proj, w2p, b_gate, o_gain, s_before, s_final, lay, *, name):
    t, wcols = proj.shape
    d = o_gain.shape[1]
    dk, dv, dkh, dvh = _gla_dims(d)
    nc = t // CHUNK
    c_k, c_v, c_g, c_r = dk, 2 * dk, 2 * dk + dv, 2 * dk + 2 * dv
    scale = dkh ** -0.5

    def body(da_ref, o_ref, p_ref, w2_ref, b_ref, og_ref, sb_ref, sfin_ref,
             dp_ref, dog_ref, db_ref, dw2_ref, s_ref, gc_ref, dkd_ref):
        i = pl.program_id(0)

        @pl.when(i == 0)
        def _():
            s_ref[...] = sfin_ref[...]
            gc_ref[...] = jnp.zeros_like(gc_ref)

        ws, wp = lay
        for j in range(N_CHIPS):
            dp_ref[:, j * wp + ws:(j + 1) * wp] = jnp.zeros((CHUNK, wp - ws), BF16)
        q = _load_cols(p_ref, 0, dk, lay) * scale
        k = _load_cols(p_ref, c_k, c_k + dk, lay)
        glr = p_ref[:, _gate_window(c_r, lay)]
        z, e_rest, kdec, decay = _gla_gates(glr, k, w2_ref, b_ref)
        ddecay = []
        for h in range(GLA_HEADS):
            ks = slice(h * dkh, (h + 1) * dkh)
            vs = slice(h * dvh, (h + 1) * dvh)
            v_h = _load_cols(p_ref, c_v + h * dvh, c_v + (h + 1) * dvh, lay)
            g_h = _load_cols(p_ref, c_g + h * dvh, c_g + (h + 1) * dvh, lay)
            da_h = da_ref[:, vs]
            o_h = o_ref[:, vs]
            og_h = og_ref[:, vs]
            r = _rstd(o_h)
            on = o_h * r
            sg = _sigmoid(g_h)
            silu = g_h * sg
            _acc_rows(dog_ref, i, jnp.sum(da_h * silu * on, axis=0, keepdims=True), vs)
            _store_cols(dp_ref, c_g + h * dvh, (da_h * (on * og_h) * (sg * (1.0 + g_h * (1.0 - sg)))).astype(BF16),
                        lay)
            don = da_h * silu * og_h
            do_h = (r * (don - on * jnp.mean(don * on, axis=-1, keepdims=True))).astype(BF16)
            s_cur = s_ref[h]
            _store_cols(dp_ref, h * dkh, (_dot_nn(do_h, s_cur.astype(BF16)) * scale).astype(BF16), lay)
            g_tot = gc_ref[h] + _dot_tn(do_h, q[:, ks].astype(BF16))
            g_bf = g_tot.astype(BF16)
            dkd_ref[:, ks] = _dot_nn(v_h.astype(BF16), g_bf)
            _store_cols(dp_ref, c_v + h * dvh, _dot_nt(kdec[:, ks].astype(BF16), g_bf).astype(BF16), lay)
            s_prev = sb_ref[0, h]
            ddecay.append(jnp.sum(g_tot * s_prev, axis=0, keepdims=True))
            gc_ref[h] = g_tot * decay[:, ks]
            s_ref[h] = s_prev
        dkdec = dkd_ref[...]
        _store_cols(dp_ref, c_k, (dkdec * e_rest).astype(BF16), lay)
        d_e = dkdec * kdec
        row = lax.broadcasted_iota(jnp.int32, (CHUNK, CHUNK), 0)
        col = lax.broadcasted_iota(jnp.int32, (CHUNK, CHUNK), 1)
        excl = (row > col).astype(BF16)
        dla = jnp.concatenate(ddecay, axis=1) * decay + _tri_matmul(excl, d_e)
        dz = dla * (1.0 / GLA_TAU) * (1.0 - _sigmoid(z))
        _acc_rows(db_ref, i, jnp.sum(dz, axis=0, keepdims=True))
        dz_bf = dz.astype(BF16)
        dw2 = _dot_tn(glr.astype(BF16), dz_bf)

        @pl.when(i == 0)
        def _():
            dw2_ref[...] = dw2

        @pl.when(i > 0)
        def _():
            dw2_ref[...] += dw2

        dp_ref[:, _gate_window(c_r, lay)] = _dot_nt(dz_bf, w2_ref[...].astype(BF16)).astype(BF16)

    rev = lambda i: (nc - 1 - i, 0)
    full = lambda *shape: pl.BlockSpec(shape, lambda i: (0,) * len(shape))
    return pl.pallas_call(
        body, name=name, grid=(nc,),
        in_specs=[pl.BlockSpec((CHUNK, dv), rev), pl.BlockSpec((CHUNK, dv), rev), pl.BlockSpec((CHUNK, wcols), rev),
                  full(LANES, dk), full(1, dk), full(1, dv),
                  pl.BlockSpec((1, GLA_HEADS, dvh, dkh), lambda i: (nc - 1 - i, 0, 0, 0)), full(GLA_HEADS, dvh, dkh)],
        out_specs=[pl.BlockSpec((CHUNK, wcols), rev), full(1, dv), full(1, dk), full(LANES, dk)],
        out_shape=[jax.ShapeDtypeStruct((t, wcols), BF16), jax.ShapeDtypeStruct((1, dv), F32),
                   jax.ShapeDtypeStruct((1, dk), F32), jax.ShapeDtypeStruct((LANES, dk), F32)],
        scratch_shapes=[pltpu.VMEM((GLA_HEADS, dvh, dkh), F32), pltpu.VMEM((GLA_HEADS, dvh, dkh), F32),
                        pltpu.VMEM((CHUNK, dk), F32)],
        compiler_params=_cparams(("arbitrary",)),
    )(da, o, proj, w2p, b_gate, o_gain, s_before, s_final)


def _sgu_mid(p_ref, lg_ref, lb_ref, ws_ref, bst_ref, w, with_grad=True):
    gd = w // SGU_GROUPS
    u_act, du_fac = _gelu_parts(p_ref[:, 0:w], with_grad)
    vf, dv_fac = _gelu_parts(p_ref[:, w:2 * w], with_grad)
    mu = jnp.mean(vf, axis=-1, keepdims=True)
    cen = vf - mu
    rstd = lax.rsqrt(jnp.mean(cen * cen, axis=-1, keepdims=True) + EPS)
    xh = cen * rstd
    vn = (xh * lg_ref[...] + lb_ref[...]).astype(BF16)
    vs = [_dot_nn(ws_ref[g].astype(BF16), vn[:, g * gd:(g + 1) * gd]) + bst_ref[:, g:g + 1]
          for g in range(SGU_GROUPS)]
    return u_act, du_fac, dv_fac, rstd, xh, vn, vs


def _sgu_fwd(proj, ln_gain, ln_bias, ws_masked, bs_t, *, name):
    t, w3 = proj.shape
    w = w3 // 3
    gd = w // SGU_GROUPS
    nb = t // SGU_BLOCK

    def body(p_ref, lg_ref, lb_ref, ws_ref, bst_ref, a_ref):
        u_act, _, _, _, _, _, vs = _sgu_mid(p_ref, lg_ref, lb_ref, ws_ref, bst_ref, w, with_grad=False)
        for g in range(SGU_GROUPS):
            cs = slice(g * gd, (g + 1) * gd)
            gate = p_ref[:, 2 * w + g * gd:2 * w + (g + 1) * gd]
            a_ref[:, cs] = (u_act[:, cs] * vs[g] * (gate * _sigmoid(gate))).astype(BF16)

    full = lambda *shape: pl.BlockSpec(shape, lambda i: (0,) * len(shape))
    return pl.pallas_call(
        body, name=name, grid=(nb,),
        in_specs=[pl.BlockSpec((SGU_BLOCK, w3), lambda i: (i, 0)), full(1, w), full(1, w),
                  full(SGU_GROUPS, SGU_BLOCK, SGU_BLOCK), full(SGU_BLOCK, SGU_GROUPS)],
        out_specs=pl.BlockSpec((SGU_BLOCK, w), lambda i: (i, 0)),
        out_shape=jax.ShapeDtypeStruct((t, w), BF16),
        compiler_params=_cparams(("parallel",)),
    )(proj, ln_gain, ln_bias, ws_masked, bs_t)


def _sgu_bwd(da, proj, ln_gain, ln_bias, ws_masked, ws_masked_t, bs_t, *, name):
    t, w3 = proj.shape
    w = w3 // 3
    gd = w // SGU_GROUPS
    nb = t // SGU_BLOCK

    def body(da_ref, p_ref, lg_ref, lb_ref, ws_ref, wst_ref, bst_ref, dp_ref, dws_ref, dbst_ref, dlg_ref, dlb_ref,
             dvn_ref):
        i = pl.program_id(0)
        u_act, du_fac, dv_fac, rstd, xh, vn, vs = _sgu_mid(p_ref, lg_ref, lb_ref, ws_ref, bst_ref, w)
        for g in range(SGU_GROUPS):
            cs = slice(g * gd, (g + 1) * gd)
            gate = p_ref[:, 2 * w + g * gd:2 * w + (g + 1) * gd]
            sg = _sigmoid(gate)
            silu = gate * sg
            da_g = da_ref[:, cs]
            ua_g = u_act[:, cs]
            dp_ref[:, cs] = (da_g * vs[g] * silu * du_fac[:, cs]).astype(BF16)
            dp_ref[:, 2 * w + g * gd:2 * w + (g + 1) * gd] = (
                da_g * ua_g * vs[g] * (sg * (1.0 + gate * (1.0 - sg)))).astype(BF16)
            dvs = da_g * ua_g * silu
            dvs_bf = dvs.astype(BF16)
            dvn_ref[:, cs] = _dot_nn(wst_ref[g].astype(BF16), dvs_bf)
            dws = _dot_nt(dvs_bf, vn[:, cs])
            dbs = jnp.sum(dvs, axis=1, keepdims=True)

            @pl.when(i == 0)
            def _():
                dws_ref[g] = dws
                dbst_ref[:, g:g + 1] = dbs

            @pl.when(i > 0)
            def _():
                dws_ref[g] += dws
                dbst_ref[:, g:g + 1] += dbs

        dvn = dvn_ref[...]
        _acc_rows(dlg_ref, i, jnp.sum(dvn * xh, axis=0, keepdims=True))
        _acc_rows(dlb_ref, i, jnp.sum(dvn, axis=0, keepdims=True))
        dxh = dvn * lg_ref[...]
        dvf = rstd * (dxh - jnp.mean(dxh, axis=-1, keepdims=True)
                      - xh * jnp.mean(dxh * xh, axis=-1, keepdims=True))
        dp_ref[:, w:2 * w] = (dvf * dv_fac).astype(BF16)

    full = lambda *shape: pl.BlockSpec(shape, lambda i: (0,) * len(shape))
    return pl.pallas_call(
        body, name=name, grid=(nb,),
        in_specs=[pl.BlockSpec((SGU_BLOCK, w), lambda i: (i, 0)), pl.BlockSpec((SGU_BLOCK, w3), lambda i: (i, 0)),
                  full(1, w), full(1, w), full(SGU_GROUPS, SGU_BLOCK, SGU_BLOCK),
                  full(SGU_GROUPS, SGU_BLOCK, SGU_BLOCK), full(SGU_BLOCK, SGU_GROUPS)],
        out_specs=[pl.BlockSpec((SGU_BLOCK, w3), lambda i: (i, 0)), full(SGU_GROUPS, SGU_BLOCK, SGU_BLOCK),
                   full(SGU_BLOCK, SGU_GROUPS), full(1, w), full(1, w)],
        out_shape=[jax.ShapeDtypeStruct((t, w3), BF16), jax.ShapeDtypeStruct((SGU_GROUPS, SGU_BLOCK, SGU_BLOCK), F32),
                   jax.ShapeDtypeStruct((SGU_BLOCK, SGU_GROUPS), F32), jax.ShapeDtypeStruct((1, w), F32),
                   jax.ShapeDtypeStruct((1, w), F32)],
        scratch_shapes=[pltpu.VMEM((SGU_BLOCK, w), F32)],
        compiler_params=_cparams(("arbitrary",)),
    )(da, proj, ln_gain, ln_bias, ws_masked, ws_masked_t, bs_t)


def _tile2d(rows, cols, block_bytes, row_unit):
    if rows % row_unit == 0:
        return _pick(rows, max(row_unit, block_bytes // (4 * cols)), row_unit), cols
    return rows, _pick(cols, max(LANES, block_bytes // (4 * rows)))


def _adamw(w, g, m, v, *, name, block_bytes=ELEMENTWISE_BLOCK_BYTES, after=None):
    rows, cols = w.shape
    tr, tc = _tile2d(rows, cols, block_bytes, 8)
    g_rows = g.shape[0]
    assert g_rows == rows or tr == rows
    extra_specs, extra_args = ([], []) if after is None else ([pl.BlockSpec(memory_space=pl.ANY)], [after])

    def body(w_ref, g_ref, m_ref, v_ref, *rest):
        go_ref, d_ref, mo_ref, vo_ref = rest[len(extra_args):]
        gv = g_ref[0:tr, :]
        go_ref[...] = gv
        mn = ADAM_B1 * m_ref[...] + (1.0 - ADAM_B1) * gv
        vn = ADAM_B2 * v_ref[...] + (1.0 - ADAM_B2) * (gv * gv)
        m_hat = mn / (1.0 - ADAM_B1 ** ADAM_STEP)
        v_hat = vn / (1.0 - ADAM_B2 ** ADAM_STEP)
        d_ref[...] = -ADAM_LR * (m_hat / (jnp.sqrt(v_hat) + ADAM_EPS) + ADAM_WD * w_ref[...])
        mo_ref[...] = mn
        vo_ref[...] = vn

    spec = pl.BlockSpec((tr, tc), lambda i, j: (i, j))
    g_spec = spec if g_rows == rows else pl.BlockSpec((g_rows, tc), lambda i, j: (0, j))
    return pl.pallas_call(
        body, name=name, grid=(rows // tr, cols // tc), in_specs=[spec, g_spec, spec, spec] + extra_specs,
        out_specs=[spec] * 4, out_shape=[jax.ShapeDtypeStruct((rows, cols), F32)] * 4,
        compiler_params=_cparams(("parallel", "parallel")),
    )(w, g, m, v, *extra_args)


def _matmul_dw_pair(a_me, a_sib, b_me, b_sib, core_idx, *, shards_on, name, after=None, part=(0, 1)):
    T, M = a_me.shape
    N = b_me.shape[1]
    if shards_on == "rows":
        p, count = part
        tm, hc = M // N_CHIPS, N // 2
        hp = hc // count
        tn = _pick(hp, 1024)
        per = hp // tn
        grid = (N_CHIPS, per)
        a_spec = pl.BlockSpec((T, tm), lambda i, n, h: (0, i))
        b_me_spec = pl.BlockSpec((T, tn), lambda i, n, h: (0, (h[0] * count + p) * per + n))
        b_sib_spec = pl.BlockSpec((T, tn), lambda i, n, h: (0, p * per + n))
        out_spec = pl.BlockSpec((None, tm, tn), lambda i, n, h: (i, 0, n))
        out_shape = jax.ShapeDtypeStruct((N_CHIPS, tm, hp), BF16)
    else:
        tm, hc = _pick(M, 1024), N // N_CHIPS // 2
        grid = (M // tm, N_CHIPS)
        a_spec = pl.BlockSpec((T, tm), lambda i, j, h: (0, i))
        b_me_spec = pl.BlockSpec((T, hc), lambda i, j, h: (0, 2 * j + h[0]))
        b_sib_spec = pl.BlockSpec((T, hc), lambda i, j, h: (0, j))
        out_spec = pl.BlockSpec((None, tm, hc), lambda i, j, h: (j, i, 0))
        out_shape = jax.ShapeDtypeStruct((N_CHIPS, M, hc), BF16)
    extra_specs, extra_args = ([], []) if after is None else ([pl.BlockSpec(memory_space=pl.ANY)], [after])

    def body(h_ref, am_ref, as_ref, bm_ref, bs_ref, *rest):
        o_ref = rest[len(extra_args)]
        o_ref[...] = (_dot_tn(am_ref[...], bm_ref[...]) + _dot_tn(as_ref[...], bs_ref[...])).astype(BF16)

    grid_spec = pltpu.PrefetchScalarGridSpec(
        num_scalar_prefetch=1, grid=grid, in_specs=[a_spec, a_spec, b_me_spec, b_sib_spec] + extra_specs,
        out_specs=out_spec)
    return pl.pallas_call(
        body, name=name, grid_spec=grid_spec, out_shape=out_shape, compiler_params=_cparams(("parallel", "parallel")),
    )(core_idx, a_me, a_sib, b_me, b_sib, *extra_args)


def _chip_sum(pair, landed, slots, *, name, block_bytes=ELEMENTWISE_BLOCK_BYTES, part=(0, 1), into=None):
    p, count = part
    _, r, hp = pair.shape
    tr, tc = _tile2d(r, hp, block_bytes, 16)
    ncb = hp // tc
    extra_specs, extra_args = ([], []) if into is None else ([pl.BlockSpec(memory_space=pl.ANY)], [into])

    def body(s_ref, own_ref, l0_ref, l1_ref, l2_ref, *rest):
        rest[-1][...] = ((own_ref[...].astype(F32) + l0_ref[...].astype(F32)) + l1_ref[...].astype(F32)
                         ) + l2_ref[...].astype(F32)

    def slab(which):
        return pl.BlockSpec((None, tr, tc), lambda i, k, s: (s[which], i, k))

    grid_spec = pltpu.PrefetchScalarGridSpec(
        num_scalar_prefetch=1, grid=(r // tr, ncb),
        in_specs=[slab(0), slab(1), slab(2), slab(3)] + extra_specs,
        out_specs=pl.BlockSpec((tr, tc), lambda i, k, s: (i, (s[4] * count + p) * ncb + k)))
    return pl.pallas_call(
        body, name=name, grid_spec=grid_spec, out_shape=jax.ShapeDtypeStruct((r, 2 * hp * count), F32),
        input_output_aliases={} if into is None else {5: 0},
        compiler_params=_cparams(("parallel", "parallel")),
    )(slots, pair, landed, landed, landed, *extra_args)


def _stack_sum(x, *, name, out_dtype=F32, block_bytes=ELEMENTWISE_BLOCK_BYTES):
    s, r, c = x.shape
    tr = _pick(r, max(8, block_bytes // (4 * c)), 16) if r % 16 == 0 else r

    def body(x_ref, o_ref):
        acc = x_ref[0].astype(F32)
        for j in range(1, s):
            acc = acc + x_ref[j].astype(F32)
        o_ref[...] = acc.astype(out_dtype)

    return pl.pallas_call(
        body, name=name, grid=(r // tr,),
        in_specs=[pl.BlockSpec((s, tr, c), lambda i: (0, i, 0))], out_specs=pl.BlockSpec((tr, c), lambda i: (i, 0)),
        out_shape=jax.ShapeDtypeStruct((r, c), out_dtype), compiler_params=_cparams(("parallel",)),
    )(x)


HBM = pl.BlockSpec(memory_space=pltpu.HBM)


def _place():
    x, y, c = lax.axis_index("x"), lax.axis_index("y"), lax.axis_index("c")
    other_chips = [(1 - x, y), (x, 1 - y), (1 - x, 1 - y)]
    return x, y, c, other_chips


def _handshake(peers):
    barrier = pltpu.get_barrier_semaphore()
    for peer in peers:
        pl.semaphore_signal(barrier, inc=1, device_id=peer, device_id_type=MESH)
    pl.semaphore_wait(barrier, len(peers))


def _sibling():
    x, y, c, _ = _place()
    return [(x, y, 1 - c)]


def _same_core_chips():
    x, y, c, chips = _place()
    return [(cx, cy, c) for cx, cy in chips]


def _same_core_neighbours():
    x, y, c, _ = _place()
    return [(1 - x, y, c), (x, 1 - y, c)]


def _split_params(cid):
    return pltpu.CompilerParams(has_side_effects=SIDE_EFFECT, collective_id=cid)


def _half_cols(cols, which):
    hc = cols // 2
    return pl.ds(pl.multiple_of(which * hc, LANES), hc)


SEM = pl.BlockSpec(memory_space=pltpu.SEMAPHORE)
ANY = pl.BlockSpec(memory_space=pl.ANY)
SIDE_EFFECT = pltpu.SideEffectType.DATAFLOW_SIDE_EFFECTING
TOKEN_SHAPE = (8, LANES)


def _hbm(shape, dtype):
    return pltpu.HBM(shape, dtype)


def _in_hbm(a):
    return pltpu.with_memory_space_constraint(a, pltpu.HBM)


def _gather_copy(src_ref, land_ref, ssem, rsem, k, chip_of_block, to, c):
    cols = src_ref.shape[1]
    return pltpu.make_async_remote_copy(
        src_ref=src_ref.at[:, _half_cols(cols, c)], dst_ref=land_ref.at[chip_of_block, :, _half_cols(cols, c)],
        send_sem=ssem.at[k], recv_sem=rsem.at[k], device_id=to, device_id_type=MESH)


NEIGHBOURS = (0, 1)
ALL_CHIPS = (0, 1, 2)


def _gather_start(shards, *, name, cid, after=(), relayed=(), own_slab=None):
    n = len(shards)
    after = list(after)

    def body(*refs):
        srcs, lands = refs[:n], refs[n:2 * n]
        outs = refs[2 * n + len(after):]
        token = outs[-1]
        _handshake(_same_core_chips())
        x, y, c, chips = _place()
        me = 2 * x + y
        for a in range(n):
            ssem, rsem = outs[4 * a], outs[4 * a + 1]
            for k in NEIGHBOURS if a in relayed else ALL_CHIPS:
                cx, cy = chips[k]
                _gather_copy(srcs[a], lands[a], ssem, rsem, k, me, (cx, cy, c), c).start()
        token[...] = jnp.zeros_like(token)

    out_shape, out_specs, aliases = [], [], {}
    for a, s in enumerate(shards):
        out_shape += [pltpu.SemaphoreType.DMA((3,)), pltpu.SemaphoreType.DMA((3,)), _hbm(s.shape, s.dtype),
                      _hbm((N_CHIPS,) + s.shape, s.dtype)]
        out_specs += [SEM, SEM, HBM, HBM]
        aliases[a] = 4 * a + 2
        aliases[n + a] = 4 * a + 3
    out_shape.append(jax.ShapeDtypeStruct(TOKEN_SHAPE, F32))
    out_specs.append(pl.BlockSpec(memory_space=pltpu.VMEM))
    lands = [lax.empty((N_CHIPS,) + s.shape, s.dtype) for s in shards]
    if own_slab is not None:
        lands = [lax.dynamic_update_slice(land, s[None], (own_slab, 0, 0)) for land, s in zip(lands, shards)]
    lands = [_in_hbm(land) for land in lands]
    res = pl.pallas_call(
        body, name=name, in_specs=[HBM] * (2 * n) + [ANY] * len(after), out_specs=out_specs, out_shape=out_shape,
        input_output_aliases=aliases, compiler_params=_split_params(cid),
    )(*[_in_hbm(s) for s in shards], *lands, *after)
    return [tuple(res[4 * a:4 * a + 4]) for a in range(n)], res[-1]


def _wait_call(wait_fn, parts, after, *, name):
    ssem, rsem, src, land = parts
    after = list(after) if isinstance(after, (list, tuple)) else [after]

    def body(src_ref, land_ref, ssem_ref, rsem_ref, *rest):
        wait_fn(src_ref, land_ref, ssem_ref, rsem_ref)

    return pl.pallas_call(
        body, name=name, in_specs=[HBM, HBM, SEM, SEM] + [ANY] * len(after), out_specs=[HBM, HBM],
        out_shape=[_hbm(src.shape, src.dtype), _hbm(land.shape, land.dtype)], input_output_aliases={0: 0, 1: 1},
        compiler_params=pltpu.CompilerParams(has_side_effects=SIDE_EFFECT),
    )(src, land, ssem, rsem, *after)


def _gather_wait(parts, after, *, name, ks=ALL_CHIPS):
    def wait(src_ref, land_ref, ssem_ref, rsem_ref):
        x, y, c, chips = _place()
        for k in ks:
            cx, cy = chips[k]
            cp = _gather_copy(src_ref, land_ref, ssem_ref, rsem_ref, k, 2 * cx + cy, (x, y, c), c)
            cp.wait_send()
            cp.wait_recv()

    return _wait_call(wait, parts, after, name=name)


def _relay_copy(buf_ref, ssem, rsem, k, slab, to, c):
    hr = buf_ref.shape[1] // 2
    part = buf_ref.at[slab, pl.ds(k * hr, hr), _half_cols(buf_ref.shape[2], c)]
    return pltpu.make_async_remote_copy(
        src_ref=part, dst_ref=part, send_sem=ssem.at[k], recv_sem=rsem.at[k], device_id=to, device_id_type=MESH)


def _relay_start(land, *, name, cid):
    def body(buf_ref, ssem, rsem, buf_out, token):
        _handshake(_same_core_neighbours())
        x, y, c, _ = _place()
        _relay_copy(buf_ref, ssem, rsem, 0, 2 * (1 - x) + y, (x, 1 - y, c), c).start()
        _relay_copy(buf_ref, ssem, rsem, 1, 2 * x + 1 - y, (1 - x, y, c), c).start()
        token[...] = jnp.zeros_like(token)

    res = pl.pallas_call(
        body, name=name, in_specs=[HBM], out_specs=[SEM, SEM, HBM, pl.BlockSpec(memory_space=pltpu.VMEM)],
        out_shape=[pltpu.SemaphoreType.DMA((2,)), pltpu.SemaphoreType.DMA((2,)), _hbm(land.shape, land.dtype),
                   jax.ShapeDtypeStruct(TOKEN_SHAPE, F32)],
        input_output_aliases={0: 2}, compiler_params=_split_params(cid),
    )(land)
    return tuple(res[:3]), res[3]


def _relay_wait(parts, after, *, name):
    ssem, rsem, buf = parts
    after = list(after) if isinstance(after, (list, tuple)) else [after]

    def body(buf_ref, ssem_ref, rsem_ref, *rest):
        x, y, c, _ = _place()
        diagonal = 2 * (1 - x) + 1 - y
        _relay_copy(buf_ref, ssem_ref, rsem_ref, 0, 2 * (1 - x) + y, (x, y, c), c).wait_send()
        _relay_copy(buf_ref, ssem_ref, rsem_ref, 1, 2 * x + 1 - y, (x, y, c), c).wait_send()
        _relay_copy(buf_ref, ssem_ref, rsem_ref, 0, diagonal, (x, y, c), c).wait_recv()
        _relay_copy(buf_ref, ssem_ref, rsem_ref, 1, diagonal, (x, y, c), c).wait_recv()

    return pl.pallas_call(
        body, name=name, in_specs=[HBM, SEM, SEM] + [ANY] * len(after), out_specs=HBM,
        out_shape=_hbm(buf.shape, buf.dtype), input_output_aliases={0: 0},
        compiler_params=pltpu.CompilerParams(has_side_effects=SIDE_EFFECT),
    )(buf, ssem, rsem, *after)


def _forward_copy(buf_ref, ssem, rsem, k, slab, which, to):
    part = buf_ref.at[slab, :, _half_cols(buf_ref.shape[2], which)]
    return pltpu.make_async_remote_copy(
        src_ref=part, dst_ref=part, send_sem=ssem.at[k], recv_sem=rsem.at[k], device_id=to, device_id_type=MESH)


def _sibling_forward(land, *, name, cid, ks=ALL_CHIPS):
    def body(_, buf, send_sems, recv_sems):
        _handshake(_sibling())
        x, y, c, chips = _place()
        copies = []
        for k in ks:
            cx, cy = chips[k]
            cp = _forward_copy(buf, send_sems, recv_sems, k, 2 * cx + cy, c, (x, y, 1 - c))
            cp.start()
            copies.append(cp)
        for k in ks:
            cx, cy = chips[k]
            _forward_copy(buf, send_sems, recv_sems, k, 2 * cx + cy, 1 - c, (x, y, c)).wait_recv()
        for cp in copies:
            cp.wait_send()

    return pl.pallas_call(
        body, name=name, in_specs=[HBM], out_specs=HBM, out_shape=jax.ShapeDtypeStruct(land.shape, land.dtype),
        input_output_aliases={0: 0},
        scratch_shapes=[pltpu.SemaphoreType.DMA((3,)), pltpu.SemaphoreType.DMA((3,))],
        compiler_params=pltpu.CompilerParams(collective_id=cid),
    )(land)


def _forward_start(land, *, name, cid, ks=ALL_CHIPS):
    def body(buf_ref, ssem, rsem, buf_out, token):
        _handshake(_sibling())
        x, y, c, chips = _place()
        for k in ks:
            cx, cy = chips[k]
            _forward_copy(buf_ref, ssem, rsem, k, 2 * cx + cy, c, (x, y, 1 - c)).start()
        token[...] = jnp.zeros_like(token)

    res = pl.pallas_call(
        body, name=name, in_specs=[HBM], out_specs=[SEM, SEM, HBM, pl.BlockSpec(memory_space=pltpu.VMEM)],
        out_shape=[pltpu.SemaphoreType.DMA((3,)), pltpu.SemaphoreType.DMA((3,)), _hbm(land.shape, land.dtype),
                   jax.ShapeDtypeStruct(TOKEN_SHAPE, F32)],
        input_output_aliases={0: 2}, compiler_params=_split_params(cid),
    )(land)
    return tuple(res[:3]), res[3]


def _forward_wait(parts, after, *, name, ks=ALL_CHIPS):
    ssem, rsem, buf = parts
    after = list(after) if isinstance(after, (list, tuple)) else [after]

    def body(buf_ref, ssem_ref, rsem_ref, *rest):
        x, y, c, chips = _place()
        for k in ks:
            cx, cy = chips[k]
            _forward_copy(buf_ref, ssem_ref, rsem_ref, k, 2 * cx + cy, c, (x, y, c)).wait_send()
            _forward_copy(buf_ref, ssem_ref, rsem_ref, k, 2 * cx + cy, 1 - c, (x, y, c)).wait_recv()

    return pl.pallas_call(
        body, name=name, in_specs=[HBM, SEM, SEM] + [ANY] * len(after), out_specs=HBM,
        out_shape=_hbm(buf.shape, buf.dtype), input_output_aliases={0: 0},
        compiler_params=pltpu.CompilerParams(has_side_effects=SIDE_EFFECT),
    )(buf, ssem, rsem, *after)


def _share_copy(buf_ref, ssem, rsem, a, which, to):
    part = buf_ref.at[:, _half_cols(buf_ref.shape[1], which)]
    return pltpu.make_async_remote_copy(
        src_ref=part, dst_ref=part, send_sem=ssem.at[a], recv_sem=rsem.at[a], device_id=to, device_id_type=MESH)


def _share_start(arrays, *, name, cid):
    n = len(arrays)

    def body(*refs):
        bufs, ssem, rsem, token = refs[:n], refs[n], refs[n + 1], refs[-1]
        _handshake(_sibling())
        x, y, c, _ = _place()
        for a in range(n):
            _share_copy(bufs[a], ssem, rsem, a, c, (x, y, 1 - c)).start()
        token[...] = jnp.zeros_like(token)

    res = pl.pallas_call(
        body, name=name, in_specs=[HBM] * n,
        out_specs=[SEM, SEM] + [HBM] * n + [pl.BlockSpec(memory_space=pltpu.VMEM)],
        out_shape=[pltpu.SemaphoreType.DMA((n,)), pltpu.SemaphoreType.DMA((n,))]
        + [_hbm(b.shape, b.dtype) for b in arrays] + [jax.ShapeDtypeStruct(TOKEN_SHAPE, F32)],
        input_output_aliases={a: 2 + a for a in range(n)}, compiler_params=_split_params(cid),
    )(*[_in_hbm(b) for b in arrays])
    return (res[0], res[1], list(res[2:2 + n])), res[-1]


def _share_wait(parts, after, *, name):
    ssem, rsem, bufs = parts
    n = len(bufs)
    after = list(after) if isinstance(after, (list, tuple)) else [after]

    def body(*refs):
        buf_refs, ssem_ref, rsem_ref = refs[:n], refs[n], refs[n + 1]
        x, y, c, _ = _place()
        for a in range(n):
            _share_copy(buf_refs[a], ssem_ref, rsem_ref, a, c, (x, y, c)).wait_send()
            _share_copy(buf_refs[a], ssem_ref, rsem_ref, a, 1 - c, (x, y, c)).wait_recv()

    return pl.pallas_call(
        body, name=name, in_specs=[HBM] * n + [SEM, SEM] + [ANY] * len(after), out_specs=[HBM] * n,
        out_shape=[_hbm(b.shape, b.dtype) for b in bufs], input_output_aliases={a: a for a in range(n)},
        compiler_params=pltpu.CompilerParams(has_side_effects=SIDE_EFFECT),
    )(*bufs, ssem, rsem, *after)


def _scatter_copy(src_ref, land_ref, ssem, rsem, k, src_slab, dst_slab, to):
    return pltpu.make_async_remote_copy(
        src_ref=src_ref.at[src_slab], dst_ref=land_ref.at[dst_slab], send_sem=ssem.at[k], recv_sem=rsem.at[k],
        device_id=to, device_id_type=MESH)


def _scatter_start(part, *, name, cid):
    def start(src_ref, land_ref, ssem, rsem):
        x, y, c, chips = _place()
        me = 2 * x + y
        for k, (cx, cy) in enumerate(chips):
            _scatter_copy(src_ref, land_ref, ssem, rsem, k, 2 * cx + cy, me, (cx, cy, c)).start()

    return _split_start(start, _same_core_chips, part, part.shape, N_CHIPS - 1, name=name, cid=cid)


def _scatter_wait(parts, after, *, name):
    def wait(src_ref, land_ref, ssem_ref, rsem_ref):
        x, y, c, chips = _place()
        for k, (cx, cy) in enumerate(chips):
            idx = 2 * cx + cy
            cp = _scatter_copy(src_ref, land_ref, ssem_ref, rsem_ref, k, idx, idx, (x, y, c))
            cp.wait_send()
            cp.wait_recv()

    return _wait_call(wait, parts, after, name=name)


def _split_start(start_fn, peers_fn, src, land_shape, n_sems, *, name, cid):
    def body(src_ref, land_ref, ssem, rsem, src_out, land_out, token):
        _handshake(peers_fn())
        start_fn(src_ref, land_ref, ssem, rsem)
        token[...] = jnp.zeros_like(token)

    res = pl.pallas_call(
        body, name=name, in_specs=[HBM, HBM], out_specs=[SEM, SEM, HBM, HBM, pl.BlockSpec(memory_space=pltpu.VMEM)],
        out_shape=[pltpu.SemaphoreType.DMA((n_sems,)), pltpu.SemaphoreType.DMA((n_sems,)), _hbm(src.shape, src.dtype),
                   _hbm(land_shape, src.dtype), jax.ShapeDtypeStruct(TOKEN_SHAPE, F32)],
        input_output_aliases={0: 2, 1: 3}, compiler_params=_split_params(cid),
    )(_in_hbm(src), _in_hbm(lax.empty(land_shape, src.dtype)))
    return tuple(res[:4]), res[4]


def _sibling_copies(src_ref, land_ref, ssem, rsem, k0, groups, which, to):
    def copy(k, src, dst):
        return pltpu.make_async_remote_copy(
            src_ref=src, dst_ref=dst, send_sem=ssem.at[k], recv_sem=rsem.at[k], device_id=to, device_id_type=MESH)

    if groups == 0:
        return [copy(k0, src_ref, land_ref)]
    hw = src_ref.shape[1] // groups // 2
    return [copy(k0 + j, src_ref.at[:, pl.ds(pl.multiple_of((2 * j + which) * hw, LANES), hw)],
                 land_ref.at[:, j * hw:(j + 1) * hw]) for j in range(groups)]


def _to_sibling_start(items, *, name, cid):
    n = len(items)
    shapes = [a.shape if g == 0 else (a.shape[0], a.shape[1] // 2) for a, g in items]
    first = [sum(max(g, 1) for _, g in items[:k]) for k in range(n + 1)]

    def body(*refs):
        srcs, lands, ssem, rsem, token = refs[:n], refs[n:2 * n], refs[2 * n], refs[2 * n + 1], refs[-1]
        _handshake(_sibling())
        x, y, c, _ = _place()
        for k, (_, g) in enumerate(items):
            for cp in _sibling_copies(srcs[k], lands[k], ssem, rsem, first[k], g, 1 - c, (x, y, 1 - c)):
                cp.start()
        token[...] = jnp.zeros_like(token)

    res = pl.pallas_call(
        body, name=name, in_specs=[HBM] * (2 * n),
        out_specs=[SEM, SEM] + [HBM] * (2 * n) + [pl.BlockSpec(memory_space=pltpu.VMEM)],
        out_shape=[pltpu.SemaphoreType.DMA((first[n],)), pltpu.SemaphoreType.DMA((first[n],))]
        + [_hbm(a.shape, a.dtype) for a, _ in items] + [_hbm(s, a.dtype) for s, (a, _) in zip(shapes, items)]
        + [jax.ShapeDtypeStruct(TOKEN_SHAPE, F32)],
        input_output_aliases={k: 2 + k for k in range(2 * n)}, compiler_params=_split_params(cid),
    )(*[_in_hbm(a) for a, _ in items], *[_in_hbm(lax.empty(s, a.dtype)) for s, (a, _) in zip(shapes, items)])
    return [(res[0], res[1], first[k], g, res[2 + k], res[2 + n + k]) for k, (_, g) in enumerate(items)], res[-1]


def _from_sibling(flight, after, *, name):
    ssem, rsem, k0, groups, src, land = flight

    def wait(src_ref, land_ref, ssem_ref, rsem_ref):
        x, y, c, _ = _place()
        for cp in _sibling_copies(src_ref, land_ref, ssem_ref, rsem_ref, k0, groups, 1 - c, (x, y, c)):
            cp.wait_send()
            cp.wait_recv()

    return _wait_call(wait, (ssem, rsem, src, land), after, name=name)


def _dev_peers(x, y, c, chips):
    return [(x, y, 1 - c)] + [(cx, cy, c) for cx, cy in chips] + [(cx, cy, 1 - c) for cx, cy in chips]


def _dev_gather_start(part, *, name, cid):
    def start(src_ref, land_ref, ssem, rsem):
        x, y, c, chips = _place()
        for k, to in enumerate(_dev_peers(x, y, c, chips)):
            pltpu.make_async_remote_copy(
                src_ref=src_ref, dst_ref=land_ref.at[4 * x + 2 * y + c], send_sem=ssem.at[k], recv_sem=rsem.at[k],
                device_id=to, device_id_type=MESH).start()

    return _split_start(start, lambda: _dev_peers(*_place()), part, (N_DEV,) + part.shape, N_DEV - 1, name=name,
                        cid=cid)


def _dev_gather_wait(parts, after, *, name):
    def wait(src_ref, land_ref, ssem_ref, rsem_ref):
        x, y, c, chips = _place()
        for k, (px, py, pc) in enumerate(_dev_peers(x, y, c, chips)):
            cp = pltpu.make_async_remote_copy(
                src_ref=src_ref, dst_ref=land_ref.at[4 * px + 2 * py + pc], send_sem=ssem_ref.at[k],
                recv_sem=rsem_ref.at[k], device_id=(x, y, c), device_id_type=MESH)
            cp.wait_send()
            cp.wait_recv()

    return _wait_call(wait, parts, after, name=name)[1]


def _sibling_share_halves(arrays, *, name, cid):
    n = len(arrays)

    def body(*refs):
        bufs = refs[n:2 * n]
        send_sems, recv_sems = refs[2 * n:]
        _handshake(_sibling())
        x, y, c, _ = _place()
        copies = []
        for a in range(n):
            mine = bufs[a].at[:, _half_cols(bufs[a].shape[1], c)]
            cp = pltpu.make_async_remote_copy(
                src_ref=mine, dst_ref=mine, send_sem=send_sems.at[a], recv_sem=recv_sems.at[a],
                device_id=(x, y, 1 - c), device_id_type=MESH)
            cp.start()
            copies.append(cp)
        for a in range(n):
            theirs = bufs[a].at[:, _half_cols(bufs[a].shape[1], 1 - c)]
            pltpu.make_async_remote_copy(
                src_ref=theirs, dst_ref=theirs, send_sem=send_sems.at[a], recv_sem=recv_sems.at[a],
                device_id=(x, y, c), device_id_type=MESH).wait_recv()
        for cp in copies:
            cp.wait_send()

    return pl.pallas_call(
        body, name=name, in_specs=[HBM] * n, out_specs=[HBM] * n,
        out_shape=[jax.ShapeDtypeStruct(h.shape, h.dtype) for h in arrays],
        input_output_aliases={a: a for a in range(n)},
        scratch_shapes=[pltpu.SemaphoreType.DMA((n,)), pltpu.SemaphoreType.DMA((n,))],
        compiler_params=pltpu.CompilerParams(collective_id=cid),
    )(*arrays)


def _pack(arrays, rows_multiple=16, width=LANES):
    flat = jnp.concatenate([a.astype(F32).reshape(-1) for a in arrays])
    total = flat.shape[0]
    rows = -(-total // width)
    rows = -(-rows // rows_multiple) * rows_multiple
    return jnp.pad(flat, (0, rows * width - total)).reshape(rows, width)


def _unpack(buf, shapes):
    flat = buf.reshape(-1)
    out, off = [], 0
    for s in shapes:
        n = math.prod(s)
        out.append(flat[off:off + n].reshape(s))
        off += n
    return out


def kernel(x, norm_pre, norm_post, gla_w_in, gla_w_gate2, gla_b_gate, gla_o_gain, gla_w_out, sgu_w_in, sgu_ln_gain, sgu_ln_bias, sgu_w_spatial, sgu_b_spatial, sgu_w_out, loss_target, m_norm_pre, m_norm_post, m_gla_w_in, m_gla_w_gate2, m_gla_b_gate, m_gla_o_gain, m_gla_w_out, m_sgu_w_in, m_sgu_ln_gain, m_sgu_ln_bias, m_sgu_w_spatial, m_sgu_b_spatial, m_sgu_w_out, v_norm_pre, v_norm_post, v_gla_w_in, v_gla_w_gate2, v_gla_b_gate, v_gla_o_gain, v_gla_w_out, v_sgu_w_in, v_sgu_ln_gain, v_sgu_ln_bias, v_sgu_w_spatial, v_sgu_b_spatial, v_sgu_w_out):
    _, t, d = x.shape
    dk = d // 2
    ws = gla_w_in.shape[2]
    wp = -(-ws // LANES) * LANES
    lay = (ws, wp)
    chip =2 * lax.axis_index("x") + lax.axis_index("y")
    core = lax.axis_index("c")
    core_idx = core.astype(jnp.int32).reshape(1)
    others = jnp.arange(N_CHIPS - 1, dtype=jnp.int32)
    others = others + (others >= chip).astype(jnp.int32)
    slots = jnp.concatenate([chip.astype(jnp.int32).reshape(1), others, core_idx])

    x0 = x[0]
    target = loss_target[0]

    wt_in_g, mt_in_g, vt_in_g = gla_w_in[0].T, m_gla_w_in[0].T, v_gla_w_in[0].T

    small_shard = _pack([gla_w_gate2[0], sgu_ln_gain[0], sgu_ln_bias[0]], rows_multiple=8, width=2 * LANES)
    own = [small_shard, jnp.pad(wt_in_g.astype(BF16), ((0, wp - ws), (0, 0)))]
    in_flight, token = _gather_start(own, name="gather_start_a", cid=0, relayed=(1,))

    def with_sibling_and_own(mine, land, name, cid):
        return lax.dynamic_update_slice(_sibling_forward(land, name=name + "_share", cid=cid), mine[None],
                                        (chip, 0, 0))

    h0 = _norm_pre(x0, norm_pre[0:1] + token[0:1, 0:1], name="pre0")
    g_small = with_sibling_and_own(*_gather_wait(in_flight[0], h0, name="w_small_wait"), "w_small", 12)
    mine, land = _gather_wait(in_flight[1], [g_small, wt_in_g, mt_in_g, vt_in_g], name="w_gla_in_wait", ks=NEIGHBOURS)
    relay, token = _relay_start(land, name="w_gla_in_relay", cid=11)
    crossing, token = _forward_start(relay[2], name="w_gla_in_share_near", cid=22, ks=NEIGHBOURS)
    own_later = [(p[0] + token[0, 0]).astype(BF16) for p in (gla_w_out, sgu_w_in, sgu_w_out)]
    in_flight_later, token = _gather_start(own_later, name="gather_start_b", cid=1, after=[token], own_slab=chip)
    in_flight = in_flight + in_flight_later
    land = _relay_wait((relay[0], relay[1], crossing[2]), token, name="w_gla_in_relay_wait")
    land = _forward_wait((crossing[0], crossing[1], land), token, name="w_gla_in_share_near_wait", ks=NEIGHBOURS)
    land = _sibling_forward(land, name="w_gla_in_share_far", cid=13, ks=(2,))
    wt_g = lax.dynamic_update_slice(land, mine[None], (chip, 0, 0)).reshape(N_CHIPS * wp, d)

    def behind(small, token):
        return small + token[0:1, 0:1]

    def arriving(i, after, name):
        mine, land = _gather_wait(in_flight[i], after, name=name + "_wait")
        crossing, token = _forward_start(land, name=name + "_share", cid=i)
        return (mine, crossing), token

    def arrived(pending, after, name):
        _, crossing = pending
        return _forward_wait(crossing, after, name=name + "_share_wait")

    shard_shapes = [gla_w_gate2.shape[1:], sgu_ln_gain.shape[1:], sgu_ln_bias.shape[1:]]
    per_chip = [_unpack(g_small[j], shard_shapes) for j in range(N_CHIPS)]
    w2_full = jnp.concatenate([p[0] for p in per_chip], axis=1)
    ln_gain = jnp.concatenate([p[1] for p in per_chip], axis=0)[None, :]
    ln_bias = jnp.concatenate([p[2] for p in per_chip], axis=0)[None, :]
    w2p = jnp.pad(w2_full, ((0, LANES - GLA_GATE_RANK), (0, 0)))

    pos_chunk = jnp.arange(SGU_BLOCK) // CHUNK
    mask = pos_chunk[:, None] >= pos_chunk[None, :]
    ws_masked = jnp.where(mask[None], sgu_w_spatial[0], 0.0)
    ws_masked_t = ws_masked.transpose(0, 2, 1)
    bs_t = sgu_b_spatial[0].T

    proj0 = _matmul(h0, wt_g, mode="nt", out_dtype=F32, name="gla_in", tn=wp)
    pending, tok = arriving(2, proj0, "w_gla_out")
    o0, a0, s_before, s_final = _gla_fwd(proj0, w2p, behind(gla_b_gate, tok), gla_o_gain, lay, name="gla_scan")
    w_out_g = arrived(pending, a0, "w_gla_out").reshape(d, d)
    y0 = _matmul(a0, w_out_g, mode="nn", out_dtype=F32, name="gla_out", tn=1024)
    pending, tok = arriving(3, y0, "w_sgu_in")
    x1, h1 = _post_then_pre(x0, y0, behind(norm_post[0:1], tok), norm_pre[1:2], name="post0_pre1")
    g_wi_s = arrived(pending, h1, "w_sgu_in")
    pending, tok = arriving(4, g_wi_s, "w_sgu_out")
    proj1 = _matmul(h1, g_wi_s, mode="nn", out_dtype=F32, name="sgu_in", b_shards=True, after=tok, tn=768)
    a1 = _sgu_fwd(proj1, ln_gain, ln_bias, ws_masked, bs_t, name="sgu_gate")
    w_out_s = arrived(pending, a1, "w_sgu_out").reshape(d, d)
    acts, tok = _to_sibling_start([(a1, 0), (a0, 0), (h1, 0), (h0, 1)], name="acts_to_sibling", cid=5)
    a1, a0, h1, h0 = [f[4] for f in acts]
    y1 = _matmul(a1, w_out_s, mode="nn", out_dtype=F32, name="sgu_out", after=tok, tn=1024)
    loss_part, dx2, dy1, d_post1 = _loss_head(x1, y1, norm_post[1:2], target, name="loss_head")

    def pair_gradient(a_sent, b_sent, after, shards_on, name, cid):
        a_me, a_sib = _from_sibling(a_sent, after, name=name + "_a_wait")
        b_me, b_sib = _from_sibling(b_sent, [a_sib] + list(after), name=name + "_b_wait")
        pair = _matmul_dw_pair(a_me, a_sib, b_me, b_sib, core_idx, shards_on=shards_on,
                               name=name + "_pair")
        return _scatter_start(pair, name=name + "_start", cid=cid)

    def reduced(flight, after, name):
        pair, landed = _scatter_wait(flight, after, name=name + "_wait")
        return _chip_sum(pair, landed, slots, name=name + "_sum")

    (dy1_sent,), tok = _to_sibling_start([(dy1, 1)], name="dy1_to_sibling", cid=6)
    dy1 = dy1_sent[4]
    da1 = _matmul(dy1, w_out_s, mode="nt", out_dtype=F32, name="d_sgu_act", after=tok, tn=1024)
    fl_wo_s, tok = pair_gradient(acts[0], dy1_sent, [da1], "rows", "g_sgu_out", 15)
    dproj1, d_ws, d_bs_t, d_lg, d_lb = _sgu_bwd(da1, proj1, ln_gain, behind(ln_bias, tok), ws_masked, ws_masked_t,
                                                bs_t, name="sgu_gate_bwd")
    (dp1_sent,), tok = _to_sibling_start([(dproj1, N_CHIPS)], name="dproj1_to_sibling", cid=7)
    dproj1 = dp1_sent[4]
    dh1 = _matmul_nt_shards(dproj1, g_wi_s, out_dtype=F32, name="d_sgu_h", after=tok)
    fl_wi_s, tok = pair_gradient(acts[2], dp1_sent, [dh1], "cols", "g_sgu_in", 16)
    dx1, dy0, d_pre1, d_post0 = _mid_bwd(dx2, dh1, x1, behind(norm_pre[1:2], tok), y0, norm_post[0:1],
                                         name="pre1_post0_bwd")
    (dy0_sent,), tok = _to_sibling_start([(dy0, 1)], name="dy0_to_sibling", cid=8)
    dy0 = dy0_sent[4]
    da0 = _matmul(dy0, w_out_g, mode="nt", out_dtype=F32, name="d_gla_act", after=tok, tn=1024)
    fl_wo_g, tok = pair_gradient(acts[1], dy0_sent, [da0], "rows", "g_gla_out", 17)
    dproj0, d_og, d_bg, d_w2p = _gla_bwd(da0, o0, proj0, w2p, behind(gla_b_gate, tok), gla_o_gain, s_before, s_final,
                                         lay, name="gla_scan_bwd")
    early_shapes = [norm_post.shape, gla_b_gate.shape, gla_o_gain.shape, sgu_w_spatial.shape, sgu_b_spatial.shape,
                    (1, GLA_GATE_RANK, dk), (1, d), (1, d), (1, LANES)]
    early_part = _pack([jnp.concatenate([d_post0, d_post1], axis=0), d_bg, d_og, jnp.where(mask[None], d_ws, 0.0)[None],
                        d_bs_t.T[None], d_w2p[:GLA_GATE_RANK][None], d_lg, d_lb, loss_part])
    early_flight, tok = _dev_gather_start(early_part, name="small_early_start", cid=20)
    (dp0_sent,), tok_sent = _to_sibling_start([(dproj0, 0)], name="dproj0_to_sibling", cid=9)
    dproj0 = dp0_sent[4]
    dh0 = _matmul(dproj0, wt_g, mode="nn", out_dtype=F32, name="d_gla_h", after=tok_sent)
    a_me, a_sib = _from_sibling(dp0_sent, [dh0, tok], name="g_gla_in_a_wait")
    b_me, b_sib = _from_sibling(acts[3], [a_sib, dh0], name="g_gla_in_b_wait")
    fl_wi_g, tok_scatter = [], None
    for p in range(2):
        pair = _matmul_dw_pair(a_me, a_sib, b_me, b_sib, core_idx, shards_on="rows", part=(p, 2),
                               name=f"g_gla_in_pair{p}", after=tok_scatter)
        flight, tok_scatter = _scatter_start(pair, name=f"g_gla_in_start{p}", cid=18 + p)
        fl_wi_g.append(flight)
    r_wo_s = reduced(fl_wo_s, tok_scatter, "g_sgu_out")
    r_wi_s = reduced(fl_wi_s, r_wo_s, "g_sgu_in")
    r_wo_g = reduced(fl_wo_g, r_wi_s, "g_gla_out")
    sharing, tok = _share_start([r_wo_s, r_wi_s, r_wo_g], name="grads_share_a", cid=10)
    grad_x, d_pre0 = _first_bwd(dx1, dh0, x0, behind(norm_pre[0:1], tok), name="pre0_bwd")

    late_part = _pack([jnp.concatenate([d_pre0, d_pre1], axis=0)])
    late_flight, tok = _dev_gather_start(late_part, name="small_late_start", cid=21)

    def big_update(w, g, m, v, name, after=None):
        return [u[None] for u in _adamw(w[0], g, m[0], v[0], name=name, after=after)]

    g_wo_sgu, g_wi_sgu, g_wo_gla = _share_wait(sharing, [grad_x, tok], name="grads_share_a_wait")
    u_wi_sgu = big_update(sgu_w_in, g_wi_sgu, m_sgu_w_in, v_sgu_w_in, "adamw_sgu_w_in")
    u_wo_gla = big_update(gla_w_out, g_wo_gla, m_gla_w_out, v_gla_w_out, "adamw_gla_w_out", after=u_wi_sgu[1])

    r_wi_g, behind_this = None, u_wo_gla[1]
    for p, flight in enumerate(fl_wi_g):
        pair, landed = _scatter_wait(flight, behind_this, name=f"g_gla_in_wait{p}")
        r_wi_g = behind_this = _chip_sum(pair, landed, slots, part=(p, 2), into=r_wi_g, name=f"g_gla_in_sum{p}")
    gt_wi_gla, = _sibling_share_halves([r_wi_g], name="grads_share_b", cid=14)
    u_wi_gla_t = _adamw(wt_in_g, gt_wi_gla, mt_in_g, vt_in_g, name="adamw_gla_w_in")
    u_wi_gla = [u.T[None] for u in u_wi_gla_t]
    u_wo_sgu = big_update(sgu_w_out, g_wo_sgu, m_sgu_w_out, v_sgu_w_out, "adamw_sgu_w_out", after=u_wi_gla_t[1])

    def summed_over_devices(part, flight, after, shapes, name):
        land = _dev_gather_wait(flight, after, name=name + "_wait")
        every = lax.dynamic_update_slice(land, part[None], (2 * chip + core, 0, 0))
        return _unpack(_stack_sum(every, name=name + "_sum"), shapes)

    (g_post, g_bg, g_og, g_wsp, g_bsp, g_w2_full, g_lg_full, g_lb_full, loss_vec) = summed_over_devices(
        early_part, early_flight, u_wo_sgu[1], early_shapes, "small_early")
    g_pre, = summed_over_devices(late_part, late_flight, loss_vec, [norm_pre.shape], "small_late")
    loss = loss_vec[0, 0]
    g_w2 = lax.dynamic_slice_in_dim(g_w2_full, chip * (dk // N_CHIPS), dk // N_CHIPS, axis=2)
    g_lg = lax.dynamic_slice_in_dim(g_lg_full, chip * (d // N_CHIPS), d // N_CHIPS, axis=1)
    g_lb = lax.dynamic_slice_in_dim(g_lb_full, chip * (d // N_CHIPS), d // N_CHIPS, axis=1)

    small_w = [norm_pre, norm_post, gla_b_gate, gla_o_gain, sgu_w_spatial, sgu_b_spatial, gla_w_gate2, sgu_ln_gain,
               sgu_ln_bias]
    small_g = [g_pre, g_post, g_bg, g_og, g_wsp, g_bsp, g_w2, g_lg, g_lb]
    small_m = [m_norm_pre, m_norm_post, m_gla_b_gate, m_gla_o_gain, m_sgu_w_spatial, m_sgu_b_spatial, m_gla_w_gate2,
               m_sgu_ln_gain, m_sgu_ln_bias]
    small_v = [v_norm_pre, v_norm_post, v_gla_b_gate, v_gla_o_gain, v_sgu_w_spatial, v_sgu_b_spatial, v_gla_w_gate2,
               v_sgu_ln_gain, v_sgu_ln_bias]
    own_shapes = [w.shape for w in small_w]
    _, s_dl, s_m, s_v = _adamw(_pack(small_w), _pack(small_g), _pack(small_m), _pack(small_v), name="adamw_small")
    dl_s, m_s, v_s = _unpack(s_dl, own_shapes), _unpack(s_m, own_shapes), _unpack(s_v, own_shapes)

    def ordered(small, kind):
        pre, post, bg, og, wsp, bsp, w2, lg, lb = small
        return [pre, post, u_wi_gla[kind], w2, bg, og, u_wo_gla[kind], u_wi_sgu[kind], lg, lb, wsp, bsp, u_wo_sgu[kind]]

    return (loss, grad_x[None], *ordered(small_g, 0), *ordered(dl_s, 1), *ordered(m_s, 2), *ordered(v_s, 3))
```

```python
import math

import jax
import jax.numpy as jnp
from jax import lax
from jax.experimental import pallas as pl
from jax.experimental.pallas import tpu as pltpu

F32 = jnp.float32
BF16 = jnp.bfloat16
MESH = pl.DeviceIdType.MESH

EPS = 1e-6
CHUNK = 64
GLA_HEADS = 4
GLA_GATE_RANK = 16
GLA_TAU = 16.0
SGU_BLOCK = 128
SGU_GROUPS = 8
N_CHIPS = 4
N_DEV = 8
LANES = 128

ADAM_LR = 0.001
ADAM_B1 = 0.9
ADAM_B2 = 0.999
ADAM_EPS = 1e-08
ADAM_WD = 0.01
ADAM_STEP = 10

VMEM_LIMIT = 56 * 1024 * 1024
ELEMENTWISE_BLOCK_BYTES = 2 << 20


def _cparams(sem=None):
    return pltpu.CompilerParams(dimension_semantics=sem, vmem_limit_bytes=VMEM_LIMIT)


def _pick(n, cap, unit=LANES):
    best = None
    for t in range(unit, min(n, cap) + 1, unit):
        if n % t == 0:
            best = t
    assert best is not None, (n, cap, unit)
    return best


def _dot(a, b, dims):
    return lax.dot_general(a, b, (dims, ((), ())), preferred_element_type=F32)


def _dot_nn(a, b):
    return _dot(a, b, ((1,), (0,)))


def _dot_nt(a, b):
    return _dot(a, b, ((1,), (1,)))


def _dot_tn(a, b):
    return _dot(a, b, ((0,), (0,)))


def _matmul(a, b, *, mode, out_dtype, name, tm=1024, tn=512, b_shards=False, after=None):
    M, K = a.shape
    if b_shards:
        ns, Kb, bc = b.shape
        N, tn = ns * bc, _pick(bc, tn)
        per = bc // tn
        b_spec = pl.BlockSpec((None, K, tn), lambda i, j: (j // per, 0, j % per))
    elif mode == "nt":
        N, Kb = b.shape
        tn = _pick(N, tn)
        b_spec = pl.BlockSpec((tn, K), lambda i, j: (j, 0))
    else:
        Kb, N = b.shape
        tn = _pick(N, tn)
        b_spec = pl.BlockSpec((K, tn), lambda i, j: (0, j))
    assert K == Kb and a.dtype == b.dtype == BF16, (a.shape, b.shape, mode)
    tm = _pick(M, tm)
    dims = ((1,), (1,)) if mode == "nt" else ((1,), (0,))
    extra_specs, extra_args = ([], []) if after is None else ([pl.BlockSpec(memory_space=pl.ANY)], [after])

    def body(a_ref, b_ref, *rest):
        rest[-1][...] = _dot(a_ref[...], b_ref[...], dims).astype(out_dtype)

    return pl.pallas_call(
        body, name=name, grid=(M // tm, N // tn),
        in_specs=[pl.BlockSpec((tm, K), lambda i, j: (i, 0)), b_spec] + extra_specs,
        out_specs=pl.BlockSpec((tm, tn), lambda i, j: (i, j)), out_shape=jax.ShapeDtypeStruct((M, N), out_dtype),
        compiler_params=_cparams(("parallel", "parallel")),
    )(a, b, *extra_args)


def _matmul_nt_shards(a, b, *, out_dtype, name, tm=1024, tn=512, after=None):
    M, K = a.shape
    ns, N, kc = b.shape
    assert K == ns * kc
    tm, tn = _pick(M, tm), _pick(N, tn)

    def body(a_ref, *rest):
        b_refs, o_ref = rest[:ns], rest[ns + (after is not None)]
        acc = _dot_nt(a_ref[:, 0:kc], b_refs[0][...])
        for j in range(1, ns):
            acc += _dot_nt(a_ref[:, j * kc:(j + 1) * kc], b_refs[j][...])
        o_ref[...] = acc.astype(out_dtype)

    def shard(j):
        return pl.BlockSpec((None, tn, kc), lambda i, n: (j, n, 0))

    extra_specs, extra_args = ([], []) if after is None else ([pl.BlockSpec(memory_space=pl.ANY)], [after])
    return pl.pallas_call(
        body, name=name, grid=(M // tm, N // tn),
        in_specs=[pl.BlockSpec((tm, K), lambda i, n: (i, 0))] + [shard(j) for j in range(ns)] + extra_specs,
        out_specs=pl.BlockSpec((tm, tn), lambda i, n: (i, n)), out_shape=jax.ShapeDtypeStruct((M, N), out_dtype),
        compiler_params=_cparams(("parallel", "parallel")),
    )(a, *([b] * ns), *extra_args)


def _rstd(x):
    return lax.rsqrt(jnp.mean(x * x, axis=-1, keepdims=True) + EPS)


def _row_spec(tr, d):
    return pl.BlockSpec((tr, d), lambda i: (i, 0))


def _vec_spec(d):
    return pl.BlockSpec((1, d), lambda i: (0, 0))


def _acc_rows(ref, i, val, cols=slice(None)):
    @pl.when(i == 0)
    def _():
        ref[:, cols] = val

    @pl.when(i > 0)
    def _():
        ref[:, cols] += val


def _norm_pre(x, gain, *, name, tr=256):
    t, d = x.shape
    tr = _pick(t, tr, 8)

    def body(x_ref, g_ref, h_ref):
        xv = x_ref[...]
        h_ref[...] = (xv * _rstd(xv) * g_ref[...]).astype(BF16)

    return pl.pallas_call(
        body, name=name, grid=(t // tr,), in_specs=[_row_spec(tr, d), _vec_spec(d)], out_specs=_row_spec(tr, d),
        out_shape=jax.ShapeDtypeStruct((t, d), BF16), compiler_params=_cparams(("parallel",)),
    )(x, gain)


def _post_then_pre(x, y, post_gain, pre_gain, *, name, tr=256):
    t, d = x.shape
    tr = _pick(t, tr, 8)

    def body(x_ref, y_ref, pg_ref, ng_ref, xn_ref, h_ref):
        yv = y_ref[...]
        xn = x_ref[...] + yv * _rstd(yv) * pg_ref[...]
        xn_ref[...] = xn
        h_ref[...] = (xn * _rstd(xn) * ng_ref[...]).astype(BF16)

    return pl.pallas_call(
        body, name=name, grid=(t // tr,),
        in_specs=[_row_spec(tr, d), _row_spec(tr, d), _vec_spec(d), _vec_spec(d)],
        out_specs=[_row_spec(tr, d), _row_spec(tr, d)],
        out_shape=[jax.ShapeDtypeStruct((t, d), F32), jax.ShapeDtypeStruct((t, d), BF16)],
        compiler_params=_cparams(("parallel",)),
    )(x, y, post_gain, pre_gain)


def _norm_bwd(dy, n, r, gain):
    dn = dy * gain
    return r * (dn - n * jnp.mean(dn * n, axis=-1, keepdims=True))


def _loss_head(x, y, post_gain, target, *, name, tr=256):
    t, d = x.shape
    tr = _pick(t, tr, 8)

    def body(x_ref, y_ref, pg_ref, t_ref, loss_ref, dx_ref, dy_ref, dpg_ref):
        i = pl.program_id(0)
        yv = y_ref[...]
        r = _rstd(yv)
        n = yv * r
        err = x_ref[...] + n * pg_ref[...] - t_ref[...]
        dx = err * (1.0 / d)
        dx_ref[...] = dx
        part = 0.5 * jnp.sum(jnp.mean(err * err, axis=-1, keepdims=True), axis=0, keepdims=True)
        _acc_rows(loss_ref, i, jnp.broadcast_to(part, (1, LANES)))
        _acc_rows(dpg_ref, i, jnp.sum(dx * n, axis=0, keepdims=True))
        dy_ref[...] = _norm_bwd(dx, n, r, pg_ref[...]).astype(BF16)

    return pl.pallas_call(
        body, name=name, grid=(t // tr,),
        in_specs=[_row_spec(tr, d), _row_spec(tr, d), _vec_spec(d), _row_spec(tr, d)],
        out_specs=[_vec_spec(LANES), _row_spec(tr, d), _row_spec(tr, d), _vec_spec(d)],
        out_shape=[jax.ShapeDtypeStruct((1, LANES), F32), jax.ShapeDtypeStruct((t, d), F32),
                   jax.ShapeDtypeStruct((t, d), BF16), jax.ShapeDtypeStruct((1, d), F32)],
        compiler_params=_cparams(("arbitrary",)),
    )(x, y, post_gain, target)


def _mid_bwd(dx_out, dh, x, pre_gain, y_prev, post_gain_prev, *, name, tr=256):
    t, d = x.shape
    tr = _pick(t, tr, 8)

    def body(dxo_ref, dh_ref, x_ref, ng_ref, y_ref, pg_ref, dx_ref, dy_ref, dng_ref, dpg_ref):
        i = pl.program_id(0)
        xv = x_ref[...]
        r = _rstd(xv)
        xh = xv * r
        dhv = dh_ref[...]
        _acc_rows(dng_ref, i, jnp.sum(dhv * xh, axis=0, keepdims=True))
        dx = dxo_ref[...] + _norm_bwd(dhv, xh, r, ng_ref[...])
        dx_ref[...] = dx
        yv = y_ref[...]
        ry = _rstd(yv)
        n = yv * ry
        _acc_rows(dpg_ref, i, jnp.sum(dx * n, axis=0, keepdims=True))
        dy_ref[...] = _norm_bwd(dx, n, ry, pg_ref[...]).astype(BF16)

    return pl.pallas_call(
        body, name=name, grid=(t // tr,),
        in_specs=[_row_spec(tr, d), _row_spec(tr, d), _row_spec(tr, d), _vec_spec(d), _row_spec(tr, d), _vec_spec(d)],
        out_specs=[_row_spec(tr, d), _row_spec(tr, d), _vec_spec(d), _vec_spec(d)],
        out_shape=[jax.ShapeDtypeStruct((t, d), F32), jax.ShapeDtypeStruct((t, d), BF16),
                   jax.ShapeDtypeStruct((1, d), F32), jax.ShapeDtypeStruct((1, d), F32)],
        compiler_params=_cparams(("arbitrary",)),
    )(dx_out, dh, x, pre_gain, y_prev, post_gain_prev)


def _first_bwd(dx_out, dh, x, pre_gain, *, name, tr=256):
    t, d = x.shape
    tr = _pick(t, tr, 8)

    def body(dxo_ref, dh_ref, x_ref, ng_ref, dx_ref, dng_ref):
        i = pl.program_id(0)
        xv = x_ref[...]
        r = _rstd(xv)
        xh = xv * r
        dhv = dh_ref[...]
        _acc_rows(dng_ref, i, jnp.sum(dhv * xh, axis=0, keepdims=True))
        dx_ref[...] = dxo_ref[...] + _norm_bwd(dhv, xh, r, ng_ref[...])

    return pl.pallas_call(
        body, name=name, grid=(t // tr,),
        in_specs=[_row_spec(tr, d), _row_spec(tr, d), _row_spec(tr, d), _vec_spec(d)],
        out_specs=[_row_spec(tr, d), _vec_spec(d)],
        out_shape=[jax.ShapeDtypeStruct((t, d), F32), jax.ShapeDtypeStruct((1, d), F32)],
        compiler_params=_cparams(("arbitrary",)),
    )(dx_out, dh, x, pre_gain)


def _sigmoid(x):
    return 1.0 / (1.0 + jnp.exp(-x))


def _log_sigmoid(x):
    return jnp.minimum(x, 0.0) - jnp.log(1.0 + jnp.exp(-jnp.abs(x)))


_GELU_C = math.sqrt(2.0 / math.pi)


_GELU_A = 0.044715


def _gelu_parts(x, with_grad=True):
    x2 = x * x
    h = 0.5 * jnp.tanh(x * (_GELU_C + (_GELU_C * _GELU_A) * x2)) + 0.5
    val = x * h
    if not with_grad:
        return val, None
    return val, h * (1.0 + (1.0 - h) * (x * (2.0 * _GELU_C + (6.0 * _GELU_C * _GELU_A) * x2)))


def _split3(x):
    hi = x.astype(BF16)
    r1 = x - hi.astype(F32)
    mid = r1.astype(BF16)
    lo = (r1 - mid.astype(F32)).astype(BF16)
    return hi, mid, lo


def _tri_matmul(tri_bf16, x):
    hi, mid, lo = _split3(x)
    return _dot_nn(tri_bf16, hi) + _dot_nn(tri_bf16, mid) + _dot_nn(tri_bf16, lo)


def _gla_dims(d):
    dk, dv = d // 2, d
    return dk, dv, dk // GLA_HEADS, dv // GLA_HEADS


def _col_pieces(a, b, lay):
    ws, wp = lay
    out = []
    while a < b:
        j = a // ws
        end = min(b, (j + 1) * ws)
        out.append((j * wp + a - j * ws, end - a))
        a = end
    return out


def _load_cols(ref, a, b, lay):
    parts = [ref[:, s:s + n] for s, n in _col_pieces(a, b, lay)]
    return parts[0] if len(parts) == 1 else jnp.concatenate(parts, axis=1)


def _store_cols(ref, a, val, lay):
    off = 0
    for s, n in _col_pieces(a, a + val.shape[1], lay):
        ref[:, s:s + n] = val[:, off:off + n]
        off += n


def _gate_window(c_r, lay):
    (start, _), = _col_pieces(c_r, c_r + GLA_GATE_RANK, lay)
    assert (start % lay[1]) + LANES <= lay[1]
    return slice(start, start + LANES)


def _gla_gates(glr, k, w2_ref, b_ref):
    z = _dot_nn(glr.astype(BF16), w2_ref[...].astype(BF16)) + b_ref[...]
    la = _log_sigmoid(z) * (1.0 / GLA_TAU)
    row = lax.broadcasted_iota(jnp.int32, (CHUNK, CHUNK), 0)
    col = lax.broadcasted_iota(jnp.int32, (CHUNK, CHUNK), 1)
    incl = (row >= col).astype(BF16)
    bcum = _tri_matmul(incl, la)
    b_end = bcum[CHUNK - 1:CHUNK, :]
    e_rest = jnp.exp(b_end - bcum)
    return z, e_rest, k * e_rest, jnp.exp(b_end)


def _gla_fwd(proj, w2p, b_gate, o_gain, lay, *, name):
    t, wcols = proj.shape
    d = o_gain.shape[1]
    dk, dv, dkh, dvh = _gla_dims(d)
    nc = t // CHUNK
    c_k, c_v, c_g, c_r = dk, 2 * dk, 2 * dk + dv, 2 * dk + 2 * dv
    scale = dkh ** -0.5

    def body(p_ref, w2_ref, b_ref, og_ref, o_ref, a_ref, sb_ref, sfin_ref, s_ref):
        i = pl.program_id(0)

        @pl.when(i == 0)
        def _():
            s_ref[...] = jnp.zeros_like(s_ref)

        q = _load_cols(p_ref, 0, dk, lay) * scale
        k = _load_cols(p_ref, c_k, c_k + dk, lay)
        glr = p_ref[:, _gate_window(c_r, lay)]
        _, _, kdec, decay = _gla_gates(glr, k, w2_ref, b_ref)
        for h in range(GLA_HEADS):
            ks = slice(h * dkh, (h + 1) * dkh)
            vs = slice(h * dvh, (h + 1) * dvh)
            v_h = _load_cols(p_ref, c_v + h * dvh, c_v + (h + 1) * dvh, lay)
            g_h = _load_cols(p_ref, c_g + h * dvh, c_g + (h + 1) * dvh, lay)
            s_old = s_ref[h]
            sb_ref[0, h] = s_old
            s_new = s_old * decay[:, ks] + _dot_tn(v_h.astype(BF16), kdec[:, ks].astype(BF16))
            s_ref[h] = s_new
            o_h = _dot_nt(q[:, ks].astype(BF16), s_new.astype(BF16))
            o_ref[:, vs] = o_h
            on = o_h * _rstd(o_h)
            a_ref[:, vs] = (on * og_ref[:, vs] * (g_h * _sigmoid(g_h))).astype(BF16)

        @pl.when(i == nc - 1)
        def _():
            sfin_ref[...] = s_ref[...]

    full = lambda *shape: pl.BlockSpec(shape, lambda i: (0,) * len(shape))
    return pl.pallas_call(
        body, name=name, grid=(nc,),
        in_specs=[pl.BlockSpec((CHUNK, wcols), lambda i: (i, 0)), full(LANES, dk), full(1, dk), full(1, dv)],
        out_specs=[pl.BlockSpec((CHUNK, dv), lambda i: (i, 0)), pl.BlockSpec((CHUNK, dv), lambda i: (i, 0)),
                   pl.BlockSpec((1, GLA_HEADS, dvh, dkh), lambda i: (i, 0, 0, 0)), full(GLA_HEADS, dvh, dkh)],
        out_shape=[jax.ShapeDtypeStruct((t, dv), F32), jax.ShapeDtypeStruct((t, dv), BF16),
                   jax.ShapeDtypeStruct((nc, GLA_HEADS, dvh, dkh), F32),
                   jax.ShapeDtypeStruct((GLA_HEADS, dvh, dkh), F32)],
        scratch_shapes=[pltpu.VMEM((GLA_HEADS, dvh, dkh), F32)],
        compiler_params=_cparams(("arbitrary",)),
    )(proj, w2p, b_gate, o_gain)


def _gla_bwd(da, o, proj, w2p, b_gate, o_gain, s_before, s_final, lay, *, name):
    t, wcols = proj.shape
    d = o_gain.shape[1]
    dk, dv, dkh, dvh = _gla_dims(d)
    nc = t // CHUNK
    c_k, c_v, c_g, c_r = dk, 2 * dk, 2 * dk + dv, 2 * dk + 2 * dv
    scale = dkh ** -0.5

    def body(da_ref, o_ref, p_ref, w2_ref, b_ref, og_ref, sb_ref, sfin_ref,
             dp_ref, dog_ref, db_ref, dw2_ref, s_ref, gc_ref, dkd_ref):
        i = pl.program_id(0)

        @pl.when(i == 0)
        def _():
            s_ref[...] = sfin_ref[...]
            gc_ref[...] = jnp.zeros_like(gc_ref)

        ws, wp = lay
        for j in range(N_CHIPS):
            dp_ref[:, j * wp + ws:(j + 1) * wp] = jnp.zeros((CHUNK, wp - ws), BF16)
        q = _load_cols(p_ref, 0, dk, lay) * scale
        k = _load_cols(p_ref, c_k, c_k + dk, lay)
        glr = p_ref[:, _gate_window(c_r, lay)]
        z, e_rest, kdec, decay = _gla_gates(glr, k, w2_ref, b_ref)
        ddecay = []
        for h in range(GLA_HEADS):
            ks = slice(h * dkh, (h + 1) * dkh)
            vs = slice(h * dvh, (h + 1) * dvh)
            v_h = _load_cols(p_ref, c_v + h * dvh, c_v + (h + 1) * dvh, lay)
            g_h = _load_cols(p_ref, c_g + h * dvh, c_g + (h + 1) * dvh, lay)
            da_h = da_ref[:, vs]
            o_h = o_ref[:, vs]
            og_h = og_ref[:, vs]
            r = _rstd(o_h)
            on = o_h * r
            sg = _sigmoid(g_h)
            silu = g_h * sg
            _acc_rows(dog_ref, i, jnp.sum(da_h * silu * on, axis=0, keepdims=True), vs)
            _store_cols(dp_ref, c_g + h * dvh, (da_h * (on * og_h) * (sg * (1.0 + g_h * (1.0 - sg)))).astype(BF16),
                        lay)
            don = da_h * silu * og_h
            do_h = (r * (don - on * jnp.mean(don * on, axis=-1, keepdims=True))).astype(BF16)
            s_cur = s_ref[h]
            _store_cols(dp_ref, h * dkh, (_dot_nn(do_h, s_cur.astype(BF16)) * scale).astype(BF16), lay)
            g_tot = gc_ref[h] + _dot_tn(do_h, q[:, ks].astype(BF16))
            g_bf = g_tot.astype(BF16)
            dkd_ref[:, ks] = _dot_nn(v_h.astype(BF16), g_bf)
            _store_cols(dp_ref, c_v + h * dvh, _dot_nt(kdec[:, ks].astype(BF16), g_bf).astype(BF16), lay)
            s_prev = sb_ref[0, h]
            ddecay.append(jnp.sum(g_tot * s_prev, axis=0, keepdims=True))
            gc_ref[h] = g_tot * decay[:, ks]
            s_ref[h] = s_prev
        dkdec = dkd_ref[...]
        _store_cols(dp_ref, c_k, (dkdec * e_rest).astype(BF16), lay)
        d_e = dkdec * kdec
        row = lax.broadcasted_iota(jnp.int32, (CHUNK, CHUNK), 0)
        col = lax.broadcasted_iota(jnp.int32, (CHUNK, CHUNK), 1)
        excl = (row > col).astype(BF16)
        dla = jnp.concatenate(ddecay, axis=1) * decay + _tri_matmul(excl, d_e)
        dz = dla * (1.0 / GLA_TAU) * (1.0 - _sigmoid(z))
        _acc_rows(db_ref, i, jnp.sum(dz, axis=0, keepdims=True))
        dz_bf = dz.astype(BF16)
        dw2 = _dot_tn(glr.astype(BF16), dz_bf)

        @pl.when(i == 0)
        def _():
            dw2_ref[...] = dw2

        @pl.when(i > 0)
        def _():
            dw2_ref[...] += dw2

        dp_ref[:, _gate_window(c_r, lay)] = _dot_nt(dz_bf, w2_ref[...].astype(BF16)).astype(BF16)

    rev = lambda i: (nc - 1 - i, 0)
    full = lambda *shape: pl.BlockSpec(shape, lambda i: (0,) * len(shape))
    return pl.pallas_call(
        body, name=name, grid=(nc,),
        in_specs=[pl.BlockSpec((CHUNK, dv), rev), pl.BlockSpec((CHUNK, dv), rev), pl.BlockSpec((CHUNK, wcols), rev),
                  full(LANES, dk), full(1, dk), full(1, dv),
                  pl.BlockSpec((1, GLA_HEADS, dvh, dkh), lambda i: (nc - 1 - i, 0, 0, 0)), full(GLA_HEADS, dvh, dkh)],
        out_specs=[pl.BlockSpec((CHUNK, wcols), rev), full(1, dv), full(1, dk), full(LANES, dk)],
        out_shape=[jax.ShapeDtypeStruct((t, wcols), BF16), jax.ShapeDtypeStruct((1, dv), F32),
                   jax.ShapeDtypeStruct((1, dk), F32), jax.ShapeDtypeStruct((LANES, dk), F32)],
        scratch_shapes=[pltpu.VMEM((GLA_HEADS, dvh, dkh), F32), pltpu.VMEM((GLA_HEADS, dvh, dkh), F32),
                        pltpu.VMEM((CHUNK, dk), F32)],
        compiler_params=_cparams(("arbitrary",)),
    )(da, o, proj, w2p, b_gate, o_gain, s_before, s_final)


def _sgu_mid(p_ref, lg_ref, lb_ref, ws_ref, bst_ref, w, with_grad=True):
    gd = w // SGU_GROUPS
    u_act, du_fac = _gelu_parts(p_ref[:, 0:w], with_grad)
    vf, dv_fac = _gelu_parts(p_ref[:, w:2 * w], with_grad)
    mu = jnp.mean(vf, axis=-1, keepdims=True)
    cen = vf - mu
    rstd = lax.rsqrt(jnp.mean(cen * cen, axis=-1, keepdims=True) + EPS)
    xh = cen * rstd
    vn = (xh * lg_ref[...] + lb_ref[...]).astype(BF16)
    vs = [_dot_nn(ws_ref[g].astype(BF16), vn[:, g * gd:(g + 1) * gd]) + bst_ref[:, g:g + 1]
          for g in range(SGU_GROUPS)]
    return u_act, du_fac, dv_fac, rstd, xh, vn, vs


def _sgu_fwd(proj, ln_gain, ln_bias, ws_masked, bs_t, *, name):
    t, w3 = proj.shape
    w = w3 // 3
    gd = w // SGU_GROUPS
    nb = t // SGU_BLOCK

    def body(p_ref, lg_ref, lb_ref, ws_ref, bst_ref, a_ref):
        u_act, _, _, _, _, _, vs = _sgu_mid(p_ref, lg_ref, lb_ref, ws_ref, bst_ref, w, with_grad=False)
        for g in range(SGU_GROUPS):
            cs = slice(g * gd, (g + 1) * gd)
            gate = p_ref[:, 2 * w + g * gd:2 * w + (g + 1) * gd]
            a_ref[:, cs] = (u_act[:, cs] * vs[g] * (gate * _sigmoid(gate))).astype(BF16)

    full = lambda *shape: pl.BlockSpec(shape, lambda i: (0,) * len(shape))
    return pl.pallas_call(
        body, name=name, grid=(nb,),
        in_specs=[pl.BlockSpec((SGU_BLOCK, w3), lambda i: (i, 0)), full(1, w), full(1, w),
                  full(SGU_GROUPS, SGU_BLOCK, SGU_BLOCK), full(SGU_BLOCK, SGU_GROUPS)],
        out_specs=pl.BlockSpec((SGU_BLOCK, w), lambda i: (i, 0)),
        out_shape=jax.ShapeDtypeStruct((t, w), BF16),
        compiler_params=_cparams(("parallel",)),
    )(proj, ln_gain, ln_bias, ws_masked, bs_t)


def _sgu_bwd(da, proj, ln_gain, ln_bias, ws_masked, ws_masked_t, bs_t, *, name):
    t, w3 = proj.shape
    w = w3 // 3
    gd = w // SGU_GROUPS
    nb = t // SGU_BLOCK

    def body(da_ref, p_ref, lg_ref, lb_ref, ws_ref, wst_ref, bst_ref, dp_ref, dws_ref, dbst_ref, dlg_ref, dlb_ref,
             dvn_ref):
        i = pl.program_id(0)
        u_act, du_fac, dv_fac, rstd, xh, vn, vs = _sgu_mid(p_ref, lg_ref, lb_ref, ws_ref, bst_ref, w)
        for g in range(SGU_GROUPS):
            cs = slice(g * gd, (g + 1) * gd)
            gate = p_ref[:, 2 * w + g * gd:2 * w + (g + 1) * gd]
            sg = _sigmoid(gate)
            silu = gate * sg
            da_g = da_ref[:, cs]
            ua_g = u_act[:, cs]
            dp_ref[:, cs] = (da_g * vs[g] * silu * du_fac[:, cs]).astype(BF16)
            dp_ref[:, 2 * w + g * gd:2 * w + (g + 1) * gd] = (
                da_g * ua_g * vs[g] * (sg * (1.0 + gate * (1.0 - sg)))).astype(BF16)
            dvs = da_g * ua_g * silu
            dvs_bf = dvs.astype(BF16)
            dvn_ref[:, cs] = _dot_nn(wst_ref[g].astype(BF16), dvs_bf)
            dws = _dot_nt(dvs_bf, vn[:, cs])
            dbs = jnp.sum(dvs, axis=1, keepdims=True)

            @pl.when(i == 0)
            def _():
                dws_ref[g] = dws
                dbst_ref[:, g:g + 1] = dbs

            @pl.when(i > 0)
            def _():
                dws_ref[g] += dws
                dbst_ref[:, g:g + 1] += dbs

        dvn = dvn_ref[...]
        _acc_rows(dlg_ref, i, jnp.sum(dvn * xh, axis=0, keepdims=True))
        _acc_rows(dlb_ref, i, jnp.sum(dvn, axis=0, keepdims=True))
        dxh = dvn * lg_ref[...]
        dvf = rstd * (dxh - jnp.mean(dxh, axis=-1, keepdims=True)
                      - xh * jnp.mean(dxh * xh, axis=-1, keepdims=True))
        dp_ref[:, w:2 * w] = (dvf * dv_fac).astype(BF16)

    full = lambda *shape: pl.BlockSpec(shape, lambda i: (0,) * len(shape))
    return pl.pallas_call(
        body, name=name, grid=(nb,),
        in_specs=[pl.BlockSpec((SGU_BLOCK, w), lambda i: (i, 0)), pl.BlockSpec((SGU_BLOCK, w3), lambda i: (i, 0)),
                  full(1, w), full(1, w), full(SGU_GROUPS, SGU_BLOCK, SGU_BLOCK),
                  full(SGU_GROUPS, SGU_BLOCK, SGU_BLOCK), full(SGU_BLOCK, SGU_GROUPS)],
        out_specs=[pl.BlockSpec((SGU_BLOCK, w3), lambda i: (i, 0)), full(SGU_GROUPS, SGU_BLOCK, SGU_BLOCK),
                   full(SGU_BLOCK, SGU_GROUPS), full(1, w), full(1, w)],
        out_shape=[jax.ShapeDtypeStruct((t, w3), BF16), jax.ShapeDtypeStruct((SGU_GROUPS, SGU_BLOCK, SGU_BLOCK), F32),
                   jax.ShapeDtypeStruct((SGU_BLOCK, SGU_GROUPS), F32), jax.ShapeDtypeStruct((1, w), F32),
                   jax.ShapeDtypeStruct((1, w), F32)],
        scratch_shapes=[pltpu.VMEM((SGU_BLOCK, w), F32)],
        compiler_params=_cparams(("arbitrary",)),
    )(da, proj, ln_gain, ln_bias, ws_masked, ws_masked_t, bs_t)


def _tile2d(rows, cols, block_bytes, row_unit):
    if rows % row_unit == 0:
        return _pick(rows, max(row_unit, block_bytes // (4 * cols)), row_unit), cols
    return rows, _pick(cols, max(LANES, block_bytes // (4 * rows)))


def _adamw(w, g, m, v, *, name, block_bytes=ELEMENTWISE_BLOCK_BYTES, after=None):
    rows, cols = w.shape
    tr, tc = _tile2d(rows, cols, block_bytes, 8)
    g_rows = g.shape[0]
    assert g_rows == rows or tr == rows
    extra_specs, extra_args = ([], []) if after is None else ([pl.BlockSpec(memory_space=pl.ANY)], [after])

    def body(w_ref, g_ref, m_ref, v_ref, *rest):
        go_ref, d_ref, mo_ref, vo_ref = rest[len(extra_args):]
        gv = g_ref[0:tr, :]
        go_ref[...] = gv
        mn = ADAM_B1 * m_ref[...] + (1.0 - ADAM_B1) * gv
        vn = ADAM_B2 * v_ref[...] + (1.0 - ADAM_B2) * (gv * gv)
        m_hat = mn / (1.0 - ADAM_B1 ** ADAM_STEP)
        v_hat = vn / (1.0 - ADAM_B2 ** ADAM_STEP)
        d_ref[...] = -ADAM_LR * (m_hat / (jnp.sqrt(v_hat) + ADAM_EPS) + ADAM_WD * w_ref[...])
        mo_ref[...] = mn
        vo_ref[...] = vn

    spec = pl.BlockSpec((tr, tc), lambda i, j: (i, j))
    g_spec = spec if g_rows == rows else pl.BlockSpec((g_rows, tc), lambda i, j: (0, j))
    return pl.pallas_call(
        body, name=name, grid=(rows // tr, cols // tc), in_specs=[spec, g_spec, spec, spec] + extra_specs,
        out_specs=[spec] * 4, out_shape=[jax.ShapeDtypeStruct((rows, cols), F32)] * 4,
        compiler_params=_cparams(("parallel", "parallel")),
    )(w, g, m, v, *extra_args)


def _matmul_dw_pair(a_me, a_sib, b_me, b_sib, core_idx, *, shards_on, name, after=None, part=(0, 1)):
    T, M = a_me.shape
    N = b_me.shape[1]
    if shards_on == "rows":
        p, count = part
        tm, hc = M // N_CHIPS, N // 2
        hp = hc // count
        tn = _pick(hp, 1024)
        per = hp // tn
        grid = (N_CHIPS, per)
        a_spec = pl.BlockSpec((T, tm), lambda i, n, h: (0, i))
        b_me_spec = pl.BlockSpec((T, tn), lambda i, n, h: (0, (h[0] * count + p) * per + n))
        b_sib_spec = pl.BlockSpec((T, tn), lambda i, n, h: (0, p * per + n))
        out_spec = pl.BlockSpec((None, tm, tn), lambda i, n, h: (i, 0, n))
        out_shape = jax.ShapeDtypeStruct((N_CHIPS, tm, hp), BF16)
    else:
        tm, hc = _pick(M, 1024), N // N_CHIPS // 2
        grid = (M // tm, N_CHIPS)
        a_spec = pl.BlockSpec((T, tm), lambda i, j, h: (0, i))
        b_me_spec = pl.BlockSpec((T, hc), lambda i, j, h: (0, 2 * j + h[0]))
        b_sib_spec = pl.BlockSpec((T, hc), lambda i, j, h: (0, j))
        out_spec = pl.BlockSpec((None, tm, hc), lambda i, j, h: (j, i, 0))
        out_shape = jax.ShapeDtypeStruct((N_CHIPS, M, hc), BF16)
    extra_specs, extra_args = ([], []) if after is None else ([pl.BlockSpec(memory_space=pl.ANY)], [after])

    def body(h_ref, am_ref, as_ref, bm_ref, bs_ref, *rest):
        o_ref = rest[len(extra_args)]
        o_ref[...] = (_dot_tn(am_ref[...], bm_ref[...]) + _dot_tn(as_ref[...], bs_ref[...])).astype(BF16)

    grid_spec = pltpu.PrefetchScalarGridSpec(
        num_scalar_prefetch=1, grid=grid, in_specs=[a_spec, a_spec, b_me_spec, b_sib_spec] + extra_specs,
        out_specs=out_spec)
    return pl.pallas_call(
        body, name=name, grid_spec=grid_spec, out_shape=out_shape, compiler_params=_cparams(("parallel", "parallel")),
    )(core_idx, a_me, a_sib, b_me, b_sib, *extra_args)


def _chip_sum(pair, landed, slots, *, name, block_bytes=ELEMENTWISE_BLOCK_BYTES, part=(0, 1), into=None):
    p, count = part
    _, r, hp = pair.shape
    tr, tc = _tile2d(r, hp, block_bytes, 16)
    ncb = hp // tc
    extra_specs, extra_args = ([], []) if into is None else ([pl.BlockSpec(memory_space=pl.ANY)], [into])

    def body(s_ref, own_ref, l0_ref, l1_ref, l2_ref, *rest):
        rest[-1][...] = ((own_ref[...].astype(F32) + l0_ref[...].astype(F32)) + l1_ref[...].astype(F32)
                         ) + l2_ref[...].astype(F32)

    def slab(which):
        return pl.BlockSpec((None, tr, tc), lambda i, k, s: (s[which], i, k))

    grid_spec = pltpu.PrefetchScalarGridSpec(
        num_scalar_prefetch=1, grid=(r // tr, ncb),
        in_specs=[slab(0), slab(1), slab(2), slab(3)] + extra_specs,
        out_specs=pl.BlockSpec((tr, tc), lambda i, k, s: (i, (s[4] * count + p) * ncb + k)))
    return pl.pallas_call(
        body, name=name, grid_spec=grid_spec, out_shape=jax.ShapeDtypeStruct((r, 2 * hp * count), F32),
        input_output_aliases={} if into is None else {5: 0},
        compiler_params=_cparams(("parallel", "parallel")),
    )(slots, pair, landed, landed, landed, *extra_args)


def _stack_sum(x, *, name, out_dtype=F32, block_bytes=ELEMENTWISE_BLOCK_BYTES):
    s, r, c = x.shape
    tr = _pick(r, max(8, block_bytes // (4 * c)), 16) if r % 16 == 0 else r

    def body(x_ref, o_ref):
        acc = x_ref[0].astype(F32)
        for j in range(1, s):
            acc = acc + x_ref[j].astype(F32)
        o_ref[...] = acc.astype(out_dtype)

    return pl.pallas_call(
        body, name=name, grid=(r // tr,),
        in_specs=[pl.BlockSpec((s, tr, c), lambda i: (0, i, 0))], out_specs=pl.BlockSpec((tr, c), lambda i: (i, 0)),
        out_shape=jax.ShapeDtypeStruct((r, c), out_dtype), compiler_params=_cparams(("parallel",)),
    )(x)


HBM = pl.BlockSpec(memory_space=pltpu.HBM)


def _place():
    x, y, c = lax.axis_index("x"), lax.axis_index("y"), lax.axis_index("c")
    other_chips = [(1 - x, y), (x, 1 - y), (1 - x, 1 - y)]
    return x, y, c, other_chips


def _handshake(peers):
    barrier = pltpu.get_barrier_semaphore()
    for peer in peers:
        pl.semaphore_signal(barrier, inc=1, device_id=peer, device_id_type=MESH)
    pl.semaphore_wait(barrier, len(peers))


def _sibling():
    x, y, c, _ = _place()
    return [(x, y, 1 - c)]


def _same_core_chips():
    x, y, c, chips = _place()
    return [(cx, cy, c) for cx, cy in chips]


def _same_core_neighbours():
    x, y, c, _ = _place()
    return [(1 - x, y, c), (x, 1 - y, c)]


def _split_params(cid):
    return pltpu.CompilerParams(has_side_effects=SIDE_EFFECT, collective_id=cid)


def _half_cols(cols, which):
    hc = cols // 2
    return pl.ds(pl.multiple_of(which * hc, LANES), hc)


SEM = pl.BlockSpec(memory_space=pltpu.SEMAPHORE)
ANY = pl.BlockSpec(memory_space=pl.ANY)
SIDE_EFFECT = pltpu.SideEffectType.DATAFLOW_SIDE_EFFECTING
TOKEN_SHAPE = (8, LANES)


def _hbm(shape, dtype):
    return pltpu.HBM(shape, dtype)


def _in_hbm(a):
    return pltpu.with_memory_space_constraint(a, pltpu.HBM)


def _gather_copy(src_ref, land_ref, ssem, rsem, k, chip_of_block, to, c):
    cols = src_ref.shape[1]
    return pltpu.make_async_remote_copy(
        src_ref=src_ref.at[:, _half_cols(cols, c)], dst_ref=land_ref.at[chip_of_block, :, _half_cols(cols, c)],
        send_sem=ssem.at[k], recv_sem=rsem.at[k], device_id=to, device_id_type=MESH)


NEIGHBOURS = (0, 1)
ALL_CHIPS = (0, 1, 2)


def _gather_start(shards, *, name, cid, after=(), relayed=(), own_slab=None):
    n = len(shards)
    after = list(after)

    def body(*refs):
        srcs, lands = refs[:n], refs[n:2 * n]
        outs = refs[2 * n + len(after):]
        token = outs[-1]
        _handshake(_same_core_chips())
        x, y, c, chips = _place()
        me = 2 * x + y
        for a in range(n):
            ssem, rsem = outs[4 * a], outs[4 * a + 1]
            for k in NEIGHBOURS if a in relayed else ALL_CHIPS:
                cx, cy = chips[k]
                _gather_copy(srcs[a], lands[a], ssem, rsem, k, me, (cx, cy, c), c).start()
        token[...] = jnp.zeros_like(token)

    out_shape, out_specs, aliases = [], [], {}
    for a, s in enumerate(shards):
        out_shape += [pltpu.SemaphoreType.DMA((3,)), pltpu.SemaphoreType.DMA((3,)), _hbm(s.shape, s.dtype),
                      _hbm((N_CHIPS,) + s.shape, s.dtype)]
        out_specs += [SEM, SEM, HBM, HBM]
        aliases[a] = 4 * a + 2
        aliases[n + a] = 4 * a + 3
    out_shape.append(jax.ShapeDtypeStruct(TOKEN_SHAPE, F32))
    out_specs.append(pl.BlockSpec(memory_space=pltpu.VMEM))
    lands = [lax.empty((N_CHIPS,) + s.shape, s.dtype) for s in shards]
    if own_slab is not None:
        lands = [lax.dynamic_update_slice(land, s[None], (own_slab, 0, 0)) for land, s in zip(lands, shards)]
    lands = [_in_hbm(land) for land in lands]
    res = pl.pallas_call(
        body, name=name, in_specs=[HBM] * (2 * n) + [ANY] * len(after), out_specs=out_specs, out_shape=out_shape,
        input_output_aliases=aliases, compiler_params=_split_params(cid),
    )(*[_in_hbm(s) for s in shards], *lands, *after)
    return [tuple(res[4 * a:4 * a + 4]) for a in range(n)], res[-1]


def _wait_call(wait_fn, parts, after, *, name):
    ssem, rsem, src, land = parts
    after = list(after) if isinstance(after, (list, tuple)) else [after]

    def body(src_ref, land_ref, ssem_ref, rsem_ref, *rest):
        wait_fn(src_ref, land_ref, ssem_ref, rsem_ref)

    return pl.pallas_call(
        body, name=name, in_specs=[HBM, HBM, SEM, SEM] + [ANY] * len(after), out_specs=[HBM, HBM],
        out_shape=[_hbm(src.shape, src.dtype), _hbm(land.shape, land.dtype)], input_output_aliases={0: 0, 1: 1},
        compiler_params=pltpu.CompilerParams(has_side_effects=SIDE_EFFECT),
    )(src, land, ssem, rsem, *after)


def _gather_wait(parts, after, *, name, ks=ALL_CHIPS):
    def wait(src_ref, land_ref, ssem_ref, rsem_ref):
        x, y, c, chips = _place()
        for k in ks:
            cx, cy = chips[k]
            cp = _gather_copy(src_ref, land_ref, ssem_ref, rsem_ref, k, 2 * cx + cy, (x, y, c), c)
            cp.wait_send()
            cp.wait_recv()

    return _wait_call(wait, parts, after, name=name)


def _relay_copy(buf_ref, ssem, rsem, k, slab, to, c):
    hr = buf_ref.shape[1] // 2
    part = buf_ref.at[slab, pl.ds(k * hr, hr), _half_cols(buf_ref.shape[2], c)]
    return pltpu.make_async_remote_copy(
        src_ref=part, dst_ref=part, send_sem=ssem.at[k], recv_sem=rsem.at[k], device_id=to, device_id_type=MESH)


def _relay_start(land, *, name, cid):
    def body(buf_ref, ssem, rsem, buf_out, token):
        _handshake(_same_core_neighbours())
        x, y, c, _ = _place()
        _relay_copy(buf_ref, ssem, rsem, 0, 2 * (1 - x) + y, (x, 1 - y, c), c).start()
        _relay_copy(buf_ref, ssem, rsem, 1, 2 * x + 1 - y, (1 - x, y, c), c).start()
        token[...] = jnp.zeros_like(token)

    res = pl.pallas_call(
        body, name=name, in_specs=[HBM], out_specs=[SEM, SEM, HBM, pl.BlockSpec(memory_space=pltpu.VMEM)],
        out_shape=[pltpu.SemaphoreType.DMA((2,)), pltpu.SemaphoreType.DMA((2,)), _hbm(land.shape, land.dtype),
                   jax.ShapeDtypeStruct(TOKEN_SHAPE, F32)],
        input_output_aliases={0: 2}, compiler_params=_split_params(cid),
    )(land)
    return tuple(res[:3]), res[3]


def _relay_wait(parts, after, *, name):
    ssem, rsem, buf = parts
    after = list(after) if isinstance(after, (list, tuple)) else [after]

    def body(buf_ref, ssem_ref, rsem_ref, *rest):
        x, y, c, _ = _place()
        diagonal = 2 * (1 - x) + 1 - y
        _relay_copy(buf_ref, ssem_ref, rsem_ref, 0, 2 * (1 - x) + y, (x, y, c), c).wait_send()
        _relay_copy(buf_ref, ssem_ref, rsem_ref, 1, 2 * x + 1 - y, (x, y, c), c).wait_send()
        _relay_copy(buf_ref, ssem_ref, rsem_ref, 0, diagonal, (x, y, c), c).wait_recv()
        _relay_copy(buf_ref, ssem_ref, rsem_ref, 1, diagonal, (x, y, c), c).wait_recv()

    return pl.pallas_call(
        body, name=name, in_specs=[HBM, SEM, SEM] + [ANY] * len(after), out_specs=HBM,
        out_shape=_hbm(buf.shape, buf.dtype), input_output_aliases={0: 0},
        compiler_params=pltpu.CompilerParams(has_side_effects=SIDE_EFFECT),
    )(buf, ssem, rsem, *after)


def _forward_copy(buf_ref, ssem, rsem, k, slab, which, to):
    part = buf_ref.at[slab, :, _half_cols(buf_ref.shape[2], which)]
    return pltpu.make_async_remote_copy(
        src_ref=part, dst_ref=part, send_sem=ssem.at[k], recv_sem=rsem.at[k], device_id=to, device_id_type=MESH)


def _sibling_forward(land, *, name, cid, ks=ALL_CHIPS):
    def body(_, buf, send_sems, recv_sems):
        _handshake(_sibling())
        x, y, c, chips = _place()
        copies = []
        for k in ks:
            cx, cy = chips[k]
            cp = _forward_copy(buf, send_sems, recv_sems, k, 2 * cx + cy, c, (x, y, 1 - c))
            cp.start()
            copies.append(cp)
        for k in ks:
            cx, cy = chips[k]
            _forward_copy(buf, send_sems, recv_sems, k, 2 * cx + cy, 1 - c, (x, y, c)).wait_recv()
        for cp in copies:
            cp.wait_send()

    return pl.pallas_call(
        body, name=name, in_specs=[HBM], out_specs=HBM, out_shape=jax.ShapeDtypeStruct(land.shape, land.dtype),
        input_output_aliases={0: 0},
        scratch_shapes=[pltpu.SemaphoreType.DMA((3,)), pltpu.SemaphoreType.DMA((3,))],
        compiler_params=pltpu.CompilerParams(collective_id=cid),
    )(land)


def _forward_start(land, *, name, cid, ks=ALL_CHIPS):
    def body(buf_ref, ssem, rsem, buf_out, token):
        _handshake(_sibling())
        x, y, c, chips = _place()
        for k in ks:
            cx, cy = chips[k]
            _forward_copy(buf_ref, ssem, rsem, k, 2 * cx + cy, c, (x, y, 1 - c)).start()
        token[...] = jnp.zeros_like(token)

    res = pl.pallas_call(
        body, name=name, in_specs=[HBM], out_specs=[SEM, SEM, HBM, pl.BlockSpec(memory_space=pltpu.VMEM)],
        out_shape=[pltpu.SemaphoreType.DMA((3,)), pltpu.SemaphoreType.DMA((3,)), _hbm(land.shape, land.dtype),
                   jax.ShapeDtypeStruct(TOKEN_SHAPE, F32)],
        input_output_aliases={0: 2}, compiler_params=_split_params(cid),
    )(land)
    return tuple(res[:3]), res[3]


def _forward_wait(parts, after, *, name, ks=ALL_CHIPS):
    ssem, rsem, buf = parts
    after = list(after) if isinstance(after, (list, tuple)) else [after]

    def body(buf_ref, ssem_ref, rsem_ref, *rest):
        x, y, c, chips = _place()
        for k in ks:
            cx, cy = chips[k]
            _forward_copy(buf_ref, ssem_ref, rsem_ref, k, 2 * cx + cy, c, (x, y, c)).wait_send()
            _forward_copy(buf_ref, ssem_ref, rsem_ref, k, 2 * cx + cy, 1 - c, (x, y, c)).wait_recv()

    return pl.pallas_call(
        body, name=name, in_specs=[HBM, SEM, SEM] + [ANY] * len(after), out_specs=HBM,
        out_shape=_hbm(buf.shape, buf.dtype), input_output_aliases={0: 0},
        compiler_params=pltpu.CompilerParams(has_side_effects=SIDE_EFFECT),
    )(buf, ssem, rsem, *after)


def _share_copy(buf_ref, ssem, rsem, a, which, to):
    part = buf_ref.at[:, _half_cols(buf_ref.shape[1], which)]
    return pltpu.make_async_remote_copy(
        src_ref=part, dst_ref=part, send_sem=ssem.at[a], recv_sem=rsem.at[a], device_id=to, device_id_type=MESH)


def _share_start(arrays, *, name, cid):
    n = len(arrays)

    def body(*refs):
        bufs, ssem, rsem, token = refs[:n], refs[n], refs[n + 1], refs[-1]
        _handshake(_sibling())
        x, y, c, _ = _place()
        for a in range(n):
            _share_copy(bufs[a], ssem, rsem, a, c, (x, y, 1 - c)).start()
        token[...] = jnp.zeros_like(token)

    res = pl.pallas_call(
        body, name=name, in_specs=[HBM] * n,
        out_specs=[SEM, SEM] + [HBM] * n + [pl.BlockSpec(memory_space=pltpu.VMEM)],
        out_shape=[pltpu.SemaphoreType.DMA((n,)), pltpu.SemaphoreType.DMA((n,))]
        + [_hbm(b.shape, b.dtype) for b in arrays] + [jax.ShapeDtypeStruct(TOKEN_SHAPE, F32)],
        input_output_aliases={a: 2 + a for a in range(n)}, compiler_params=_split_params(cid),
    )(*[_in_hbm(b) for b in arrays])
    return (res[0], res[1], list(res[2:2 + n])), res[-1]


def _share_wait(parts, after, *, name):
    ssem, rsem, bufs = parts
    n = len(bufs)
    after = list(after) if isinstance(after, (list, tuple)) else [after]

    def body(*refs):
        buf_refs, ssem_ref, rsem_ref = refs[:n], refs[n], refs[n + 1]
        x, y, c, _ = _place()
        for a in range(n):
            _share_copy(buf_refs[a], ssem_ref, rsem_ref, a, c, (x, y, c)).wait_send()
            _share_copy(buf_refs[a], ssem_ref, rsem_ref, a, 1 - c, (x, y, c)).wait_recv()

    return pl.pallas_call(
        body, name=name, in_specs=[HBM] * n + [SEM, SEM] + [ANY] * len(after), out_specs=[HBM] * n,
        out_shape=[_hbm(b.shape, b.dtype) for b in bufs], input_output_aliases={a: a for a in range(n)},
        compiler_params=pltpu.CompilerParams(has_side_effects=SIDE_EFFECT),
    )(*bufs, ssem, rsem, *after)


def _scatter_copy(src_ref, land_ref, ssem, rsem, k, src_slab, dst_slab, to):
    return pltpu.make_async_remote_copy(
        src_ref=src_ref.at[src_slab], dst_ref=land_ref.at[dst_slab], send_sem=ssem.at[k], recv_sem=rsem.at[k],
        device_id=to, device_id_type=MESH)


def _scatter_start(part, *, name, cid):
    def start(src_ref, land_ref, ssem, rsem):
        x, y, c, chips = _place()
        me = 2 * x + y
        for k, (cx, cy) in enumerate(chips):
            _scatter_copy(src_ref, land_ref, ssem, rsem, k, 2 * cx + cy, me, (cx, cy, c)).start()

    return _split_start(start, _same_core_chips, part, part.shape, N_CHIPS - 1, name=name, cid=cid)


def _scatter_wait(parts, after, *, name):
    def wait(src_ref, land_ref, ssem_ref, rsem_ref):
        x, y, c, chips = _place()
        for k, (cx, cy) in enumerate(chips):
            idx = 2 * cx + cy
            cp = _scatter_copy(src_ref, land_ref, ssem_ref, rsem_ref, k, idx, idx, (x, y, c))
            cp.wait_send()
            cp.wait_recv()

    return _wait_call(wait, parts, after, name=name)


def _split_start(start_fn, peers_fn, src, land_shape, n_sems, *, name, cid):
    def body(src_ref, land_ref, ssem, rsem, src_out, land_out, token):
        _handshake(peers_fn())
        start_fn(src_ref, land_ref, ssem, rsem)
        token[...] = jnp.zeros_like(token)

    res = pl.pallas_call(
        body, name=name, in_specs=[HBM, HBM], out_specs=[SEM, SEM, HBM, HBM, pl.BlockSpec(memory_space=pltpu.VMEM)],
        out_shape=[pltpu.SemaphoreType.DMA((n_sems,)), pltpu.SemaphoreType.DMA((n_sems,)), _hbm(src.shape, src.dtype),
                   _hbm(land_shape, src.dtype), jax.ShapeDtypeStruct(TOKEN_SHAPE, F32)],
        input_output_aliases={0: 2, 1: 3}, compiler_params=_split_params(cid),
    )(_in_hbm(src), _in_hbm(lax.empty(land_shape, src.dtype)))
    return tuple(res[:4]), res[4]


def _sibling_copies(src_ref, land_ref, ssem, rsem, k0, groups, which, to):
    def copy(k, src, dst):
        return pltpu.make_async_remote_copy(
            src_ref=src, dst_ref=dst, send_sem=ssem.at[k], recv_sem=rsem.at[k], device_id=to, device_id_type=MESH)

    if groups == 0:
        return [copy(k0, src_ref, land_ref)]
    hw = src_ref.shape[1] // groups // 2
    return [copy(k0 + j, src_ref.at[:, pl.ds(pl.multiple_of((2 * j + which) * hw, LANES), hw)],
                 land_ref.at[:, j * hw:(j + 1) * hw]) for j in range(groups)]


def _to_sibling_start(items, *, name, cid):
    n = len(items)
    shapes = [a.shape if g == 0 else (a.shape[0], a.shape[1] // 2) for a, g in items]
    first = [sum(max(g, 1) for _, g in items[:k]) for k in range(n + 1)]

    def body(*refs):
        srcs, lands, ssem, rsem, token = refs[:n], refs[n:2 * n], refs[2 * n], refs[2 * n + 1], refs[-1]
        _handshake(_sibling())
        x, y, c, _ = _place()
        for k, (_, g) in enumerate(items):
            for cp in _sibling_copies(srcs[k], lands[k], ssem, rsem, first[k], g, 1 - c, (x, y, 1 - c)):
                cp.start()
        token[...] = jnp.zeros_like(token)

    res = pl.pallas_call(
        body, name=name, in_specs=[HBM] * (2 * n),
        out_specs=[SEM, SEM] + [HBM] * (2 * n) + [pl.BlockSpec(memory_space=pltpu.VMEM)],
        out_shape=[pltpu.SemaphoreType.DMA((first[n],)), pltpu.SemaphoreType.DMA((first[n],))]
        + [_hbm(a.shape, a.dtype) for a, _ in items] + [_hbm(s, a.dtype) for s, (a, _) in zip(shapes, items)]
        + [jax.ShapeDtypeStruct(TOKEN_SHAPE, F32)],
        input_output_aliases={k: 2 + k for k in range(2 * n)}, compiler_params=_split_params(cid),
    )(*[_in_hbm(a) for a, _ in items], *[_in_hbm(lax.empty(s, a.dtype)) for s, (a, _) in zip(shapes, items)])
    return [(res[0], res[1], first[k], g, res[2 + k], res[2 + n + k]) for k, (_, g) in enumerate(items)], res[-1]


def _from_sibling(flight, after, *, name):
    ssem, rsem, k0, groups, src, land = flight

    def wait(src_ref, land_ref, ssem_ref, rsem_ref):
        x, y, c, _ = _place()
        for cp in _sibling_copies(src_ref, land_ref, ssem_ref, rsem_ref, k0, groups, 1 - c, (x, y, c)):
            cp.wait_send()
            cp.wait_recv()

    return _wait_call(wait, (ssem, rsem, src, land), after, name=name)


def _dev_peers(x, y, c, chips):
    return [(x, y, 1 - c)] + [(cx, cy, c) for cx, cy in chips] + [(cx, cy, 1 - c) for cx, cy in chips]


def _dev_gather_start(part, *, name, cid):
    def start(src_ref, land_ref, ssem, rsem):
        x, y, c, chips = _place()
        for k, to in enumerate(_dev_peers(x, y, c, chips)):
            pltpu.make_async_remote_copy(
                src_ref=src_ref, dst_ref=land_ref.at[4 * x + 2 * y + c], send_sem=ssem.at[k], recv_sem=rsem.at[k],
                device_id=to, device_id_type=MESH).start()

    return _split_start(start, lambda: _dev_peers(*_place()), part, (N_DEV,) + part.shape, N_DEV - 1, name=name,
                        cid=cid)


def _dev_gather_wait(parts, after, *, name):
    def wait(src_ref, land_ref, ssem_ref, rsem_ref):
        x, y, c, chips = _place()
        for k, (px, py, pc) in enumerate(_dev_peers(x, y, c, chips)):
            cp = pltpu.make_async_remote_copy(
                src_ref=src_ref, dst_ref=land_ref.at[4 * px + 2 * py + pc], send_sem=ssem_ref.at[k],
                recv_sem=rsem_ref.at[k], device_id=(x, y, c), device_id_type=MESH)
            cp.wait_send()
            cp.wait_recv()

    return _wait_call(wait, parts, after, name=name)[1]


def _sibling_share_halves(arrays, *, name, cid):
    n = len(arrays)

    def body(*refs):
        bufs = refs[n:2 * n]
        send_sems, recv_sems = refs[2 * n:]
        _handshake(_sibling())
        x, y, c, _ = _place()
        copies = []
        for a in range(n):
            mine = bufs[a].at[:, _half_cols(bufs[a].shape[1], c)]
            cp = pltpu.make_async_remote_copy(
                src_ref=mine, dst_ref=mine, send_sem=send_sems.at[a], recv_sem=recv_sems.at[a],
                device_id=(x, y, 1 - c), device_id_type=MESH)
            cp.start()
            copies.append(cp)
        for a in range(n):
            theirs = bufs[a].at[:, _half_cols(bufs[a].shape[1], 1 - c)]
            pltpu.make_async_remote_copy(
                src_ref=theirs, dst_ref=theirs, send_sem=send_sems.at[a], recv_sem=recv_sems.at[a],
                device_id=(x, y, c), device_id_type=MESH).wait_recv()
        for cp in copies:
            cp.wait_send()

    return pl.pallas_call(
        body, name=name, in_specs=[HBM] * n, out_specs=[HBM] * n,
        out_shape=[jax.ShapeDtypeStruct(h.shape, h.dtype) for h in arrays],
        input_output_aliases={a: a for a in range(n)},
        scratch_shapes=[pltpu.SemaphoreType.DMA((n,)), pltpu.SemaphoreType.DMA((n,))],
        compiler_params=pltpu.CompilerParams(collective_id=cid),
    )(*arrays)


def _pack(arrays, rows_multiple=16, width=LANES):
    flat = jnp.concatenate([a.astype(F32).reshape(-1) for a in arrays])
    total = flat.shape[0]
    rows = -(-total // width)
    rows = -(-rows // rows_multiple) * rows_multiple
    return jnp.pad(flat, (0, rows * width - total)).reshape(rows, width)


def _unpack(buf, shapes):
    flat = buf.reshape(-1)
    out, off = [], 0
    for s in shapes:
        n = math.prod(s)
        out.append(flat[off:off + n].reshape(s))
        off += n
    return out


def kernel(x, norm_pre, norm_post, gla_w_in, gla_w_gate2, gla_b_gate, gla_o_gain, gla_w_out, sgu_w_in, sgu_ln_gain, sgu_ln_bias, sgu_w_spatial, sgu_b_spatial, sgu_w_out, loss_target, m_norm_pre, m_norm_post, m_gla_w_in, m_gla_w_gate2, m_gla_b_gate, m_gla_o_gain, m_gla_w_out, m_sgu_w_in, m_sgu_ln_gain, m_sgu_ln_bias, m_sgu_w_spatial, m_sgu_b_spatial, m_sgu_w_out, v_norm_pre, v_norm_post, v_gla_w_in, v_gla_w_gate2, v_gla_b_gate, v_gla_o_gain, v_gla_w_out, v_sgu_w_in, v_sgu_ln_gain, v_sgu_ln_bias, v_sgu_w_spatial, v_sgu_b_spatial, v_sgu_w_out):
    _, t, d = x.shape
    dk = d // 2
    ws = gla_w_in.shape[2]
    wp = -(-ws // LANES) * LANES
    lay = (ws, wp)
    chip =2 * lax.axis_index("x") + lax.axis_index("y")
    core = lax.axis_index("c")
    core_idx = core.astype(jnp.int32).reshape(1)
    others = jnp.arange(N_CHIPS - 1, dtype=jnp.int32)
    others = others + (others >= chip).astype(jnp.int32)
    slots = jnp.concatenate([chip.astype(jnp.int32).reshape(1), others, core_idx])

    x0 = x[0]
    target = loss_target[0]

    wt_in_g, mt_in_g, vt_in_g = gla_w_in[0].T, m_gla_w_in[0].T, v_gla_w_in[0].T

    small_shard = _pack([gla_w_gate2[0], sgu_ln_gain[0], sgu_ln_bias[0]], rows_multiple=8, width=2 * LANES)
    own = [small_shard, jnp.pad(wt_in_g.astype(BF16), ((0, wp - ws), (0, 0)))]
    in_flight, token = _gather_start(own, name="gather_start_a", cid=0, relayed=(1,))

    def with_sibling_and_own(mine, land, name, cid):
        return lax.dynamic_update_slice(_sibling_forward(land, name=name + "_share", cid=cid), mine[None],
                                        (chip, 0, 0))

    h0 = _norm_pre(x0, norm_pre[0:1] + token[0:1, 0:1], name="pre0")
    g_small = with_sibling_and_own(*_gather_wait(in_flight[0], h0, name="w_small_wait"), "w_small", 12)
    mine, land = _gather_wait(in_flight[1], [g_small, wt_in_g, mt_in_g, vt_in_g], name="w_gla_in_wait", ks=NEIGHBOURS)
    relay, token = _relay_start(land, name="w_gla_in_relay", cid=11)
    crossing, token = _forward_start(relay[2], name="w_gla_in_share_near", cid=22, ks=NEIGHBOURS)
    own_later = [(p[0] + token[0, 0]).astype(BF16) for p in (gla_w_out, sgu_w_in, sgu_w_out)]
    in_flight_later, token = _gather_start(own_later, name="gather_start_b", cid=1, after=[token], own_slab=chip)
    in_flight = in_flight + in_flight_later
    land = _relay_wait((relay[0], relay[1], crossing[2]), token, name="w_gla_in_relay_wait")
    land = _forward_wait((crossing[0], crossing[1], land), token, name="w_gla_in_share_near_wait", ks=NEIGHBOURS)
    land = _sibling_forward(land, name="w_gla_in_share_far", cid=13, ks=(2,))
    wt_g = lax.dynamic_update_slice(land, mine[None], (chip, 0, 0)).reshape(N_CHIPS * wp, d)

    def behind(small, token):
        return small + token[0:1, 0:1]

    def arriving(i, after, name):
        mine, land = _gather_wait(in_flight[i], after, name=name + "_wait")
        crossing, token = _forward_start(land, name=name + "_share", cid=i)
        return (mine, crossing), token

    def arrived(pending, after, name):
        _, crossing = pending
        return _forward_wait(crossing, after, name=name + "_share_wait")

    shard_shapes = [gla_w_gate2.shape[1:], sgu_ln_gain.shape[1:], sgu_ln_bias.shape[1:]]
    per_chip = [_unpack(g_small[j], shard_shapes) for j in range(N_CHIPS)]
    w2_full = jnp.concatenate([p[0] for p in per_chip], axis=1)
    ln_gain = jnp.concatenate([p[1] for p in per_chip], axis=0)[None, :]
    ln_bias = jnp.concatenate([p[2] for p in per_chip], axis=0)[None, :]
    w2p = jnp.pad(w2_full, ((0, LANES - GLA_GATE_RANK), (0, 0)))

    pos_chunk = jnp.arange(SGU_BLOCK) // CHUNK
    mask = pos_chunk[:, None] >= pos_chunk[None, :]
    ws_masked = jnp.where(mask[None], sgu_w_spatial[0], 0.0)
    ws_masked_t = ws_masked.transpose(0, 2, 1)
    bs_t = sgu_b_spatial[0].T

    proj0 = _matmul(h0, wt_g, mode="nt", out_dtype=F32, name="gla_in", tn=wp)
    pending, tok = arriving(2, proj0, "w_gla_out")
    o0, a0, s_before, s_final = _gla_fwd(proj0, w2p, behind(gla_b_gate, tok), gla_o_gain, lay, name="gla_scan")
    w_out_g = arrived(pending, a0, "w_gla_out").reshape(d, d)
    y0 = _matmul(a0, w_out_g, mode="nn", out_dtype=F32, name="gla_out", tn=1024)
    pending, tok = arriving(3, y0, "w_sgu_in")
    x1, h1 = _post_then_pre(x0, y0, behind(norm_post[0:1], tok), norm_pre[1:2], name="post0_pre1")
    g_wi_s = arrived(pending, h1, "w_sgu_in")
    pending, tok = arriving(4, g_wi_s, "w_sgu_out")
    proj1 = _matmul(h1, g_wi_s, mode="nn", out_dtype=F32, name="sgu_in", b_shards=True, after=tok, tn=768)
    a1 = _sgu_fwd(proj1, ln_gain, ln_bias, ws_masked, bs_t, name="sgu_gate")
    w_out_s = arrived(pending, a1, "w_sgu_out").reshape(d, d)
    acts, tok = _to_sibling_start([(a1, 0), (a0, 0), (h1, 0), (h0, 1)], name="acts_to_sibling", cid=5)
    a1, a0, h1, h0 = [f[4] for f in acts]
    y1 = _matmul(a1, w_out_s, mode="nn", out_dtype=F32, name="sgu_out", after=tok, tn=1024)
    loss_part, dx2, dy1, d_post1 = _loss_head(x1, y1, norm_post[1:2], target, name="loss_head")

    def pair_gradient(a_sent, b_sent, after, shards_on, name, cid):
        a_me, a_sib = _from_sibling(a_sent, after, name=name + "_a_wait")
        b_me, b_sib = _from_sibling(b_sent, [a_sib] + list(after), name=name + "_b_wait")
        pair = _matmul_dw_pair(a_me, a_sib, b_me, b_sib, core_idx, shards_on=shards_on,
                               name=name + "_pair")
        return _scatter_start(pair, name=name + "_start", cid=cid)

    def reduced(flight, after, name):
        pair, landed = _scatter_wait(flight, after, name=name + "_wait")
        return _chip_sum(pair, landed, slots, name=name + "_sum")

    (dy1_sent,), tok = _to_sibling_start([(dy1, 1)], name="dy1_to_sibling", cid=6)
    dy1 = dy1_sent[4]
    da1 = _matmul(dy1, w_out_s, mode="nt", out_dtype=F32, name="d_sgu_act", after=tok, tn=1024)
    fl_wo_s, tok = pair_gradient(acts[0], dy1_sent, [da1], "rows", "g_sgu_out", 15)
    dproj1, d_ws, d_bs_t, d_lg, d_lb = _sgu_bwd(da1, proj1, ln_gain, behind(ln_bias, tok), ws_masked, ws_masked_t,
                                                bs_t, name="sgu_gate_bwd")
    (dp1_sent,), tok = _to_sibling_start([(dproj1, N_CHIPS)], name="dproj1_to_sibling", cid=7)
    dproj1 = dp1_sent[4]
    dh1 = _matmul_nt_shards(dproj1, g_wi_s, out_dtype=F32, name="d_sgu_h", after=tok)
    fl_wi_s, tok = pair_gradient(acts[2], dp1_sent, [dh1], "cols", "g_sgu_in", 16)
    dx1, dy0, d_pre1, d_post0 = _mid_bwd(dx2, dh1, x1, behind(norm_pre[1:2], tok), y0, norm_post[0:1],
                                         name="pre1_post0_bwd")
    (dy0_sent,), tok = _to_sibling_start([(dy0, 1)], name="dy0_to_sibling", cid=8)
    dy0 = dy0_sent[4]
    da0 = _matmul(dy0, w_out_g, mode="nt", out_dtype=F32, name="d_gla_act", after=tok, tn=1024)
    fl_wo_g, tok = pair_gradient(acts[1], dy0_sent, [da0], "rows", "g_gla_out", 17)
    dproj0, d_og, d_bg, d_w2p = _gla_bwd(da0, o0, proj0, w2p, behind(gla_b_gate, tok), gla_o_gain, s_before, s_final,
                                         lay, name="gla_scan_bwd")
    early_shapes = [norm_post.shape, gla_b_gate.shape, gla_o_gain.shape, sgu_w_spatial.shape, sgu_b_spatial.shape,
                    (1, GLA_GATE_RANK, dk), (1, d), (1, d), (1, LANES)]
    early_part = _pack([jnp.concatenate([d_post0, d_post1], axis=0), d_bg, d_og, jnp.where(mask[None], d_ws, 0.0)[None],
                        d_bs_t.T[None], d_w2p[:GLA_GATE_RANK][None], d_lg, d_lb, loss_part])
    early_flight, tok = _dev_gather_start(early_part, name="small_early_start", cid=20)
    (dp0_sent,), tok_sent = _to_sibling_start([(dproj0, 0)], name="dproj0_to_sibling", cid=9)
    dproj0 = dp0_sent[4]
    dh0 = _matmul(dproj0, wt_g, mode="nn", out_dtype=F32, name="d_gla_h", after=tok_sent)
    a_me, a_sib = _from_sibling(dp0_sent, [dh0, tok], name="g_gla_in_a_wait")
    b_me, b_sib = _from_sibling(acts[3], [a_sib, dh0], name="g_gla_in_b_wait")
    fl_wi_g, tok_scatter = [], None
    for p in range(2):
        pair = _matmul_dw_pair(a_me, a_sib, b_me, b_sib, core_idx, shards_on="rows", part=(p, 2),
                               name=f"g_gla_in_pair{p}", after=tok_scatter)
        flight, tok_scatter = _scatter_start(pair, name=f"g_gla_in_start{p}", cid=18 + p)
        fl_wi_g.append(flight)
    r_wo_s = reduced(fl_wo_s, tok_scatter, "g_sgu_out")
    r_wi_s = reduced(fl_wi_s, r_wo_s, "g_sgu_in")
    r_wo_g = reduced(fl_wo_g, r_wi_s, "g_gla_out")
    sharing, tok = _share_start([r_wo_s, r_wi_s, r_wo_g], name="grads_share_a", cid=10)
    grad_x, d_pre0 = _first_bwd(dx1, dh0, x0, behind(norm_pre[0:1], tok), name="pre0_bwd")

    late_part = _pack([jnp.concatenate([d_pre0, d_pre1], axis=0)])
    late_flight, tok = _dev_gather_start(late_part, name="small_late_start", cid=21)

    def big_update(w, g, m, v, name, after=None):
        return [u[None] for u in _adamw(w[0], g, m[0], v[0], name=name, after=after)]

    g_wo_sgu, g_wi_sgu, g_wo_gla = _share_wait(sharing, [grad_x, tok], name="grads_share_a_wait")
    u_wi_sgu = big_update(sgu_w_in, g_wi_sgu, m_sgu_w_in, v_sgu_w_in, "adamw_sgu_w_in")
    u_wo_gla = big_update(gla_w_out, g_wo_gla, m_gla_w_out, v_gla_w_out, "adamw_gla_w_out", after=u_wi_sgu[1])

    r_wi_g, behind_this = None, u_wo_gla[1]
    for p, flight in enumerate(fl_wi_g):
        pair, landed = _scatter_wait(flight, behind_this, name=f"g_gla_in_wait{p}")
        r_wi_g = behind_this = _chip_sum(pair, landed, slots, part=(p, 2), into=r_wi_g, name=f"g_gla_in_sum{p}")
    gt_wi_gla, = _sibling_share_halves([r_wi_g], name="grads_share_b", cid=14)
    u_wi_gla_t = _adamw(wt_in_g, gt_wi_gla, mt_in_g, vt_in_g, name="adamw_gla_w_in")
    u_wi_gla = [u.T[None] for u in (gt_wi_gla[0:wt_in_g.shape[0]], *u_wi_gla_t[1:])]
    u_wo_sgu = big_update(sgu_w_out, g_wo_sgu, m_sgu_w_out, v_sgu_w_out, "adamw_sgu_w_out", after=u_wi_gla_t[1])

    def summed_over_devices(part, flight, after, shapes, name):
        land = _dev_gather_wait(flight, after, name=name + "_wait")
        every = lax.dynamic_update_slice(land, part[None], (2 * chip + core, 0, 0))
        return _unpack(_stack_sum(every, name=name + "_sum"), shapes)

    (g_post, g_bg, g_og, g_wsp, g_bsp, g_w2_full, g_lg_full, g_lb_full, loss_vec) = summed_over_devices(
        early_part, early_flight, u_wo_sgu[1], early_shapes, "small_early")
    g_pre, = summed_over_devices(late_part, late_flight, loss_vec, [norm_pre.shape], "small_late")
    loss = loss_vec[0, 0]
    g_w2 = lax.dynamic_slice_in_dim(g_w2_full, chip * (dk // N_CHIPS), dk // N_CHIPS, axis=2)
    g_lg = lax.dynamic_slice_in_dim(g_lg_full, chip * (d // N_CHIPS), d // N_CHIPS, axis=1)
    g_lb = lax.dynamic_slice_in_dim(g_lb_full, chip * (d // N_CHIPS), d // N_CHIPS, axis=1)

    small_w = [norm_pre, norm_post, gla_b_gate, gla_o_gain, sgu_w_spatial, sgu_b_spatial, gla_w_gate2, sgu_ln_gain,
               sgu_ln_bias]
    small_g = [g_pre, g_post, g_bg, g_og, g_wsp, g_bsp, g_w2, g_lg, g_lb]
    small_m = [m_norm_pre, m_norm_post, m_gla_b_gate, m_gla_o_gain, m_sgu_w_spatial, m_sgu_b_spatial, m_gla_w_gate2,
               m_sgu_ln_gain, m_sgu_ln_bias]
    small_v = [v_norm_pre, v_norm_post, v_gla_b_gate, v_gla_o_gain, v_sgu_w_spatial, v_sgu_b_spatial, v_gla_w_gate2,
               v_sgu_ln_gain, v_sgu_ln_bias]
    own_shapes = [w.shape for w in small_w]
    _, s_dl, s_m, s_v = _adamw(_pack(small_w), _pack(small_g), _pack(small_m), _pack(small_v), name="adamw_small")
    dl_s, m_s, v_s = _unpack(s_dl, own_shapes), _unpack(s_m, own_shapes), _unpack(s_v, own_shapes)

    def ordered(small, kind):
        pre, post, bg, og, wsp, bsp, w2, lg, lb = small
        return [pre, post, u_wi_gla[kind], w2, bg, og, u_wo_gla[kind], u_wi_sgu[kind], lg, lb, wsp, bsp, u_wo_sgu[kind]]

    return (loss, grad_x[None], *ordered(small_g, 0), *ordered(dl_s, 1), *ordered(m_s, 2), *ordered(v_s, 3))
```

```python
import math

import jax
import jax.numpy as jnp
from jax import lax
from jax.experimental import pallas as pl
from jax.experimental.pallas import tpu as pltpu

F32 = jnp.float32
BF16 = jnp.bfloat16
MESH = pl.DeviceIdType.MESH

EPS = 1e-6
CHUNK = 64
GLA_HEADS = 4
GLA_GATE_RANK = 16
GLA_TAU = 16.0
SGU_BLOCK = 128
SGU_GROUPS = 8
N_CHIPS = 4
N_DEV = 8
LANES = 128

ADAM_LR = 0.001
ADAM_B1 = 0.9
ADAM_B2 = 0.999
ADAM_EPS = 1e-08
ADAM_WD = 0.01
ADAM_STEP = 10

VMEM_LIMIT = 56 * 1024 * 1024
ELEMENTWISE_BLOCK_BYTES = 2 << 20


def _cparams(sem=None):
    return pltpu.CompilerParams(dimension_semantics=sem, vmem_limit_bytes=VMEM_LIMIT)


def _pick(n, cap, unit=LANES):
    best = None
    for t in range(unit, min(n, cap) + 1, unit):
        if n % t == 0:
            best = t
    assert best is not None, (n, cap, unit)
    return best


def _dot(a, b, dims):
    return lax.dot_general(a, b, (dims, ((), ())), preferred_element_type=F32)


def _dot_nn(a, b):
    return _dot(a, b, ((1,), (0,)))


def _dot_nt(a, b):
    return _dot(a, b, ((1,), (1,)))


def _dot_tn(a, b):
    return _dot(a, b, ((0,), (0,)))


def _matmul(a, b, *, mode, out_dtype, name, tm=1024, tn=512, b_shards=False, after=None):
    M, K = a.shape
    if b_shards:
        ns, Kb, bc = b.shape
        N, tn = ns * bc, _pick(bc, tn)
        per = bc // tn
        b_spec = pl.BlockSpec((None, K, tn), lambda i, j: (j // per, 0, j % per))
    elif mode == "nt":
        N, Kb = b.shape
        tn = _pick(N, tn)
        b_spec = pl.BlockSpec((tn, K), lambda i, j: (j, 0))
    else:
        Kb, N = b.shape
        tn = _pick(N, tn)
        b_spec = pl.BlockSpec((K, tn), lambda i, j: (0, j))
    assert K == Kb and a.dtype == b.dtype == BF16, (a.shape, b.shape, mode)
    tm = _pick(M, tm)
    dims = ((1,), (1,)) if mode == "nt" else ((1,), (0,))
    extra_specs, extra_args = ([], []) if after is None else ([pl.BlockSpec(memory_space=pl.ANY)], [after])

    def body(a_ref, b_ref, *rest):
        rest[-1][...] = _dot(a_ref[...], b_ref[...], dims).astype(out_dtype)

    return pl.pallas_call(
        body, name=name, grid=(M // tm, N // tn),
        in_specs=[pl.BlockSpec((tm, K), lambda i, j: (i, 0)), b_spec] + extra_specs,
        out_specs=pl.BlockSpec((tm, tn), lambda i, j: (i, j)), out_shape=jax.ShapeDtypeStruct((M, N), out_dtype),
        compiler_params=_cparams(("parallel", "parallel")),
    )(a, b, *extra_args)


def _matmul_nt_shards(a, b, *, out_dtype, name, tm=1024, tn=512, after=None):
    M, K = a.shape
    ns, N, kc = b.shape
    assert K == ns * kc
    tm, tn = _pick(M, tm), _pick(N, tn)

    def body(a_ref, *rest):
        b_refs, o_ref = rest[:ns], rest[ns + (after is not None)]
        acc = _dot_nt(a_ref[:, 0:kc], b_refs[0][...])
        for j in range(1, ns):
            acc += _dot_nt(a_ref[:, j * kc:(j + 1) * kc], b_refs[j][...])
        o_ref[...] = acc.astype(out_dtype)

    def shard(j):
        return pl.BlockSpec((None, tn, kc), lambda i, n: (j, n, 0))

    extra_specs, extra_args = ([], []) if after is None else ([pl.BlockSpec(memory_space=pl.ANY)], [after])
    return pl.pallas_call(
        body, name=name, grid=(M // tm, N // tn),
        in_specs=[pl.BlockSpec((tm, K), lambda i, n: (i, 0))] + [shard(j) for j in range(ns)] + extra_specs,
        out_specs=pl.BlockSpec((tm, tn), lambda i, n: (i, n)), out_shape=jax.ShapeDtypeStruct((M, N), out_dtype),
        compiler_params=_cparams(("parallel", "parallel")),
    )(a, *([b] * ns), *extra_args)


def _rstd(x):
    return lax.rsqrt(jnp.mean(x * x, axis=-1, keepdims=True) + EPS)


def _row_spec(tr, d):
    return pl.BlockSpec((tr, d), lambda i: (i, 0))


def _vec_spec(d):
    return pl.BlockSpec((1, d), lambda i: (0, 0))


def _acc_rows(ref, i, val, cols=slice(None)):
    @pl.when(i == 0)
    def _():
        ref[:, cols] = val

    @pl.when(i > 0)
    def _():
        ref[:, cols] += val


def _norm_pre(x, gain, *, name, tr=256):
    t, d = x.shape
    tr = _pick(t, tr, 8)

    def body(x_ref, g_ref, h_ref):
        xv = x_ref[...]
        h_ref[...] = (xv * _rstd(xv) * g_ref[...]).astype(BF16)

    return pl.pallas_call(
        body, name=name, grid=(t // tr,), in_specs=[_row_spec(tr, d), _vec_spec(d)], out_specs=_row_spec(tr, d),
        out_shape=jax.ShapeDtypeStruct((t, d), BF16), compiler_params=_cparams(("parallel",)),
    )(x, gain)


def _post_then_pre(x, y, post_gain, pre_gain, *, name, tr=256):
    t, d = x.shape
    tr = _pick(t, tr, 8)

    def body(x_ref, y_ref, pg_ref, ng_ref, xn_ref, h_ref):
        yv = y_ref[...]
        xn = x_ref[...] + yv * _rstd(yv) * pg_ref[...]
        xn_ref[...] = xn
        h_ref[...] = (xn * _rstd(xn) * ng_ref[...]).astype(BF16)

    return pl.pallas_call(
        body, name=name, grid=(t // tr,),
        in_specs=[_row_spec(tr, d), _row_spec(tr, d), _vec_spec(d), _vec_spec(d)],
        out_specs=[_row_spec(tr, d), _row_spec(tr, d)],
        out_shape=[jax.ShapeDtypeStruct((t, d), F32), jax.ShapeDtypeStruct((t, d), BF16)],
        compiler_params=_cparams(("parallel",)),
    )(x, y, post_gain, pre_gain)


def _norm_bwd(dy, n, r, gain):
    dn = dy * gain
    return r * (dn - n * jnp.mean(dn * n, axis=-1, keepdims=True))


def _loss_head(x, y, post_gain, target, *, name, tr=256):
    t, d = x.shape
    tr = _pick(t, tr, 8)

    def body(x_ref, y_ref, pg_ref, t_ref, loss_ref, dx_ref, dy_ref, dpg_ref):
        i = pl.program_id(0)
        yv = y_ref[...]
        r = _rstd(yv)
        n = yv * r
        err = x_ref[...] + n * pg_ref[...] - t_ref[...]
        dx = err * (1.0 / d)
        dx_ref[...] = dx
        part = 0.5 * jnp.sum(jnp.mean(err * err, axis=-1, keepdims=True), axis=0, keepdims=True)
        _acc_rows(loss_ref, i, jnp.broadcast_to(part, (1, LANES)))
        _acc_rows(dpg_ref, i, jnp.sum(dx * n, axis=0, keepdims=True))
        dy_ref[...] = _norm_bwd(dx, n, r, pg_ref[...]).astype(BF16)

    return pl.pallas_call(
        body, name=name, grid=(t // tr,),
        in_specs=[_row_spec(tr, d), _row_spec(tr, d), _vec_spec(d), _row_spec(tr, d)],
        out_specs=[_vec_spec(LANES), _row_spec(tr, d), _row_spec(tr, d), _vec_spec(d)],
        out_shape=[jax.ShapeDtypeStruct((1, LANES), F32), jax.ShapeDtypeStruct((t, d), F32),
                   jax.ShapeDtypeStruct((t, d), BF16), jax.ShapeDtypeStruct((1, d), F32)],
        compiler_params=_cparams(("arbitrary",)),
    )(x, y, post_gain, target)


def _mid_bwd(dx_out, dh, x, pre_gain, y_prev, post_gain_prev, *, name, tr=256):
    t, d = x.shape
    tr = _pick(t, tr, 8)

    def body(dxo_ref, dh_ref, x_ref, ng_ref, y_ref, pg_ref, dx_ref, dy_ref, dng_ref, dpg_ref):
        i = pl.program_id(0)
        xv = x_ref[...]
        r = _rstd(xv)
        xh = xv * r
        dhv = dh_ref[...]
        _acc_rows(dng_ref, i, jnp.sum(dhv * xh, axis=0, keepdims=True))
        dx = dxo_ref[...] + _norm_bwd(dhv, xh, r, ng_ref[...])
        dx_ref[...] = dx
        yv = y_ref[...]
        ry = _rstd(yv)
        n = yv * ry
        _acc_rows(dpg_ref, i, jnp.sum(dx * n, axis=0, keepdims=True))
        dy_ref[...] = _norm_bwd(dx, n, ry, pg_ref[...]).astype(BF16)

    return pl.pallas_call(
        body, name=name, grid=(t // tr,),
        in_specs=[_row_spec(tr, d), _row_spec(tr, d), _row_spec(tr, d), _vec_spec(d), _row_spec(tr, d), _vec_spec(d)],
        out_specs=[_row_spec(tr, d), _row_spec(tr, d), _vec_spec(d), _vec_spec(d)],
        out_shape=[jax.ShapeDtypeStruct((t, d), F32), jax.ShapeDtypeStruct((t, d), BF16),
                   jax.ShapeDtypeStruct((1, d), F32), jax.ShapeDtypeStruct((1, d), F32)],
        compiler_params=_cparams(("arbitrary",)),
    )(dx_out, dh, x, pre_gain, y_prev, post_gain_prev)


def _first_bwd(dx_out, dh, x, pre_gain, *, name, tr=256):
    t, d = x.shape
    tr = _pick(t, tr, 8)

    def body(dxo_ref, dh_ref, x_ref, ng_ref, dx_ref, dng_ref):
        i = pl.program_id(0)
        xv = x_ref[...]
        r = _rstd(xv)
        xh = xv * r
        dhv = dh_ref[...]
        _acc_rows(dng_ref, i, jnp.sum(dhv * xh, axis=0, keepdims=True))
        dx_ref[...] = dxo_ref[...] + _norm_bwd(dhv, xh, r, ng_ref[...])

    return pl.pallas_call(
        body, name=name, grid=(t // tr,),
        in_specs=[_row_spec(tr, d), _row_spec(tr, d), _row_spec(tr, d), _vec_spec(d)],
        out_specs=[_row_spec(tr, d), _vec_spec(d)],
        out_shape=[jax.ShapeDtypeStruct((t, d), F32), jax.ShapeDtypeStruct((1, d), F32)],
        compiler_params=_cparams(("arbitrary",)),
    )(dx_out, dh, x, pre_gain)


def _sigmoid(x):
    return 1.0 / (1.0 + jnp.exp(-x))


def _log_sigmoid(x):
    return jnp.minimum(x, 0.0) - jnp.log(1.0 + jnp.exp(-jnp.abs(x)))


_GELU_C = math.sqrt(2.0 / math.pi)


_GELU_A = 0.044715


def _gelu_parts(x, with_grad=True):
    x2 = x * x
    h = 0.5 * jnp.tanh(x * (_GELU_C + (_GELU_C * _GELU_A) * x2)) + 0.5
    val = x * h
    if not with_grad:
        return val, None
    return val, h * (1.0 + (1.0 - h) * (x * (2.0 * _GELU_C + (6.0 * _GELU_C * _GELU_A) * x2)))


def _split3(x):
    hi = x.astype(BF16)
    r1 = x - hi.astype(F32)
    mid = r1.astype(BF16)
    lo = (r1 - mid.astype(F32)).astype(BF16)
    return hi, mid, lo


def _tri_matmul(tri_bf16, x):
    hi, mid, lo = _split3(x)
    return _dot_nn(tri_bf16, hi) + _dot_nn(tri_bf16, mid) + _dot_nn(tri_bf16, lo)


def _gla_dims(d):
    dk, dv = d // 2, d
    return dk, dv, dk // GLA_HEADS, dv // GLA_HEADS


def _col_pieces(a, b, lay):
    ws, wp = lay
    out = []
    while a < b:
        j = a // ws
        end = min(b, (j + 1) * ws)
        out.append((j * wp + a - j * ws, end - a))
        a = end
    return out


def _load_cols(ref, a, b, lay):
    parts = [ref[:, s:s + n] for s, n in _col_pieces(a, b, lay)]
    return parts[0] if len(parts) == 1 else jnp.concatenate(parts, axis=1)


def _store_cols(ref, a, val, lay):
    off = 0
    for s, n in _col_pieces(a, a + val.shape[1], lay):
        ref[:, s:s + n] = val[:, off:off + n]
        off += n


def _gate_window(c_r, lay):
    (start, _), = _col_pieces(c_r, c_r + GLA_GATE_RANK, lay)
    assert (start % lay[1]) + LANES <= lay[1]
    return slice(start, start + LANES)


def _gla_gates(glr, k, w2_ref, b_ref):
    z = _dot_nn(glr.astype(BF16), w2_ref[...].astype(BF16)) + b_ref[...]
    la = _log_sigmoid(z) * (1.0 / GLA_TAU)
    row = lax.broadcasted_iota(jnp.int32, (CHUNK, CHUNK), 0)
    col = lax.broadcasted_iota(jnp.int32, (CHUNK, CHUNK), 1)
    incl = (row >= col).astype(BF16)
    bcum = _tri_matmul(incl, la)
    b_end = bcum[CHUNK - 1:CHUNK, :]
    e_rest = jnp.exp(b_end - bcum)
    return z, e_rest, k * e_rest, jnp.exp(b_end)


def _gla_fwd(proj, w2p, b_gate, o_gain, lay, *, name):
    t, wcols = proj.shape
    d = o_gain.shape[1]
    dk, dv, dkh, dvh = _gla_dims(d)
    nc = t // CHUNK
    c_k, c_v, c_g, c_r = dk, 2 * dk, 2 * dk + dv, 2 * dk + 2 * dv
    scale = dkh ** -0.5

    def body(p_ref, w2_ref, b_ref, og_ref, o_ref, a_ref, sb_ref, sfin_ref, s_ref):
        i = pl.program_id(0)

        @pl.when(i == 0)
        def _():
            s_ref[...] = jnp.zeros_like(s_ref)

        q = _load_cols(p_ref, 0, dk, lay) * scale
        k = _load_cols(p_ref, c_k, c_k + dk, lay)
        glr = p_ref[:, _gate_window(c_r, lay)]
        _, _, kdec, decay = _gla_gates(glr, k, w2_ref, b_ref)
        for h in range(GLA_HEADS):
            ks = slice(h * dkh, (h + 1) * dkh)
            vs = slice(h * dvh, (h + 1) * dvh)
            v_h = _load_cols(p_ref, c_v + h * dvh, c_v + (h + 1) * dvh, lay)
            g_h = _load_cols(p_ref, c_g + h * dvh, c_g + (h + 1) * dvh, lay)
            s_old = s_ref[h]
            sb_ref[0, h] = s_old
            s_new = s_old * decay[:, ks] + _dot_tn(v_h.astype(BF16), kdec[:, ks].astype(BF16))
            s_ref[h] = s_new
            o_h = _dot_nt(q[:, ks].astype(BF16), s_new.astype(BF16))
            o_ref[:, vs] = o_h
            on = o_h * _rstd(o_h)
            a_ref[:, vs] = (on * og_ref[:, vs] * (g_h * _sigmoid(g_h))).astype(BF16)

        @pl.when(i == nc - 1)
        def _():
            sfin_ref[...] = s_ref[...]

    full = lambda *shape: pl.BlockSpec(shape, lambda i: (0,) * len(shape))
    return pl.pallas_call(
        body, name=name, grid=(nc,),
        in_specs=[pl.BlockSpec((CHUNK, wcols), lambda i: (i, 0)), full(LANES, dk), full(1, dk), full(1, dv)],
        out_specs=[pl.BlockSpec((CHUNK, dv), lambda i: (i, 0)), pl.BlockSpec((CHUNK, dv), lambda i: (i, 0)),
                   pl.BlockSpec((1, GLA_HEADS, dvh, dkh), lambda i: (i, 0, 0, 0)), full(GLA_HEADS, dvh, dkh)],
        out_shape=[jax.ShapeDtypeStruct((t, dv), F32), jax.ShapeDtypeStruct((t, dv), BF16),
                   jax.ShapeDtypeStruct((nc, GLA_HEADS, dvh, dkh), F32),
                   jax.ShapeDtypeStruct((GLA_HEADS, dvh, dkh), F32)],
        scratch_shapes=[pltpu.VMEM((GLA_HEADS, dvh, dkh), F32)],
        compiler_params=_cparams(("arbitrary",)),
    )(proj, w2p, b_gate, o_gain)


def _gla_bwd(da, o, proj, w2p, b_gate, o_gain, s_before, s_final, lay, *, name):
    t, wcols = proj.shape
    d = o_gain.shape[1]
    dk, dv, dkh, dvh = _gla_dims(d)
    nc = t // CHUNK
    c_k, c_v, c_g, c_r = dk, 2 * dk, 2 * dk + dv, 2 * dk + 2 * dv
    scale = dkh ** -0.5

    def body(da_ref, o_ref, p_ref, w2_ref, b_ref, og_ref, sb_ref, sfin_ref,
             dp_ref, dog_ref, db_ref, dw2_ref, s_ref, gc_ref, dkd_ref):
        i = pl.program_id(0)

        @pl.when(i == 0)
        def _():
            s_ref[...] = sfin_ref[...]
            gc_ref[...] = jnp.zeros_like(gc_ref)

        ws, wp = lay
        for j in range(N_CHIPS):
            dp_ref[:, j * wp + ws:(j + 1) * wp] = jnp.zeros((CHUNK, wp - ws), BF16)
        q = _load_cols(p_ref, 0, dk, lay) * scale
        k = _load_cols(p_ref, c_k, c_k + dk, lay)
        glr = p_ref[:, _gate_window(c_r, lay)]
        z, e_rest, kdec, decay = _gla_gates(glr, k, w2_ref, b_ref)
        ddecay = []
        for h in range(GLA_HEADS):
            ks = slice(h * dkh, (h + 1) * dkh)
            vs = slice(h * dvh, (h + 1) * dvh)
            v_h = _load_cols(p_ref, c_v + h * dvh, c_v + (h + 1) * dvh, lay)
            g_h = _load_cols(p_ref, c_g + h * dvh, c_g + (h + 1) * dvh, lay)
            da_h = da_ref[:, vs]
            o_h = o_ref[:, vs]
            og_h = og_ref[:, vs]
            r = _rstd(o_h)
            on = o_h * r
            sg = _sigmoid(g_h)
            silu = g_h * sg
            _acc_rows(dog_ref, i, jnp.sum(da_h * silu * on, axis=0, keepdims=True), vs)
            _store_cols(dp_ref, c_g + h * dvh, (da_h * (on * og_h) * (sg * (1.0 + g_h * (1.0 - sg)))).astype(BF16),
                        lay)
            don = da_h * silu * og_h
            do_h = (r * (don - on * jnp.mean(don * on, axis=-1, keepdims=True))).astype(BF16)
            s_cur = s_ref[h]
            _store_cols(dp_ref, h * dkh, (_dot_nn(do_h, s_cur.astype(BF16)) * scale).astype(BF16), lay)
            g_tot = gc_ref[h] + _dot_tn(do_h, q[:, ks].astype(BF16))
            g_bf = g_tot.astype(BF16)
            dkd_ref[:, ks] = _dot_nn(v_h.astype(BF16), g_bf)
            _store_cols(dp_ref, c_v + h * dvh, _dot_nt(kdec[:, ks].astype(BF16), g_bf).astype(BF16), lay)
            s_prev = sb_ref[0, h]
            ddecay.append(jnp.sum(g_tot * s_prev, axis=0, keepdims=True))
            gc_ref[h] = g_tot * decay[:, ks]
            s_ref[h] = s_prev
        dkdec = dkd_ref[...]
        _store_cols(dp_ref, c_k, (dkdec * e_rest).astype(BF16), lay)
        d_e = dkdec * kdec
        row = lax.broadcasted_iota(jnp.int32, (CHUNK, CHUNK), 0)
        col = lax.broadcasted_iota(jnp.int32, (CHUNK, CHUNK), 1)
        excl = (row > col).astype(BF16)
        dla = jnp.concatenate(ddecay, axis=1) * decay + _tri_matmul(excl, d_e)
        dz = dla * (1.0 / GLA_TAU) * (1.0 - _sigmoid(z))
        _acc_rows(db_ref, i, jnp.sum(dz, axis=0, keepdims=True))
        dz_bf = dz.astype(BF16)
        dw2 = _dot_tn(glr.astype(BF16), dz_bf)

        @pl.when(i == 0)
        def _():
            dw2_ref[...] = dw2

        @pl.when(i > 0)
        def _():
            dw2_ref[...] += dw2

        dp_ref[:, _gate_window(c_r, lay)] = _dot_nt(dz_bf, w2_ref[...].astype(BF16)).astype(BF16)

    rev = lambda i: (nc - 1 - i, 0)
    full = lambda *shape: pl.BlockSpec(shape, lambda i: (0,) * len(shape))
    return pl.pallas_call(
        body, name=name, grid=(nc,),
        in_specs=[pl.BlockSpec((CHUNK, dv), rev), pl.BlockSpec((CHUNK, dv), rev), pl.BlockSpec((CHUNK, wcols), rev),
                  full(LANES, dk), full(1, dk), full(1, dv),
                  pl.BlockSpec((1, GLA_HEADS, dvh, dkh), lambda i: (nc - 1 - i, 0, 0, 0)), full(GLA_HEADS, dvh, dkh)],
        out_specs=[pl.BlockSpec((CHUNK, wcols), rev), full(1, dv), full(1, dk), full(LANES, dk)],
        out_shape=[jax.ShapeDtypeStruct((t, wcols), BF16), jax.ShapeDtypeStruct((1, dv), F32),
                   jax.ShapeDtypeStruct((1, dk), F32), jax.ShapeDtypeStruct((LANES, dk), F32)],
        scratch_shapes=[pltpu.VMEM((GLA_HEADS, dvh, dkh), F32), pltpu.VMEM((GLA_HEADS, dvh, dkh), F32),
                        pltpu.VMEM((CHUNK, dk), F32)],
        compiler_params=_cparams(("arbitrary",)),
    )(da, o, proj, w2p, b_gate, o_gain, s_before, s_final)


def _sgu_mid(p_ref, lg_ref, lb_ref, ws_ref, bst_ref, w, with_grad=True):
    gd = w // SGU_GROUPS
    u_act, du_fac = _gelu_parts(p_ref[:, 0:w], with_grad)
    vf, dv_fac = _gelu_parts(p_ref[:, w:2 * w], with_grad)
    mu = jnp.mean(vf, axis=-1, keepdims=True)
    cen = vf - mu
    rstd = lax.rsqrt(jnp.mean(cen * cen, axis=-1, keepdims=True) + EPS)
    xh = cen * rstd
    vn = (xh * lg_ref[...] + lb_ref[...]).astype(BF16)
    vs = [_dot_nn(ws_ref[g].astype(BF16), vn[:, g * gd:(g + 1) * gd]) + bst_ref[:, g:g + 1]
          for g in range(SGU_GROUPS)]
    return u_act, du_fac, dv_fac, rstd, xh, vn, vs


def _sgu_fwd(proj, ln_gain, ln_bias, ws_masked, bs_t, *, name):
    t, w3 = proj.shape
    w = w3 // 3
    gd = w // SGU_GROUPS
    nb = t // SGU_BLOCK

    def body(p_ref, lg_ref, lb_ref, ws_ref, bst_ref, a_ref):
        u_act, _, _, _, _, _, vs = _sgu_mid(p_ref, lg_ref, lb_ref, ws_ref, bst_ref, w, with_grad=False)
        for g in range(SGU_GROUPS):
            cs = slice(g * gd, (g + 1) * gd)
            gate = p_ref[:, 2 * w + g * gd:2 * w + (g + 1) * gd]
            a_ref[:, cs] = (u_act[:, cs] * vs[g] * (gate * _sigmoid(gate))).astype(BF16)

    full = lambda *shape: pl.BlockSpec(shape, lambda i: (0,) * len(shape))
    return pl.pallas_call(
        body, name=name, grid=(nb,),
        in_specs=[pl.BlockSpec((SGU_BLOCK, w3), lambda i: (i, 0)), full(1, w), full(1, w),
                  full(SGU_GROUPS, SGU_BLOCK, SGU_BLOCK), full(SGU_BLOCK, SGU_GROUPS)],
        out_specs=pl.BlockSpec((SGU_BLOCK, w), lambda i: (i, 0)),
        out_shape=jax.ShapeDtypeStruct((t, w), BF16),
        compiler_params=_cparams(("parallel",)),
    )(proj, ln_gain, ln_bias, ws_masked, bs_t)


def _sgu_bwd(da, proj, ln_gain, ln_bias, ws_masked, ws_masked_t, bs_t, *, name):
    t, w3 = proj.shape
    w = w3 // 3
    gd = w // SGU_GROUPS
    nb = t // SGU_BLOCK

    def body(da_ref, p_ref, lg_ref, lb_ref, ws_ref, wst_ref, bst_ref, dp_ref, dws_ref, dbst_ref, dlg_ref, dlb_ref,
             dvn_ref):
        i = pl.program_id(0)
        u_act, du_fac, dv_fac, rstd, xh, vn, vs = _sgu_mid(p_ref, lg_ref, lb_ref, ws_ref, bst_ref, w)
        for g in range(SGU_GROUPS):
            cs = slice(g * gd, (g + 1) * gd)
            gate = p_ref[:, 2 * w + g * gd:2 * w + (g + 1) * gd]
            sg = _sigmoid(gate)
            silu = gate * sg
            da_g = da_ref[:, cs]
            ua_g = u_act[:, cs]
            dp_ref[:, cs] = (da_g * vs[g] * silu * du_fac[:, cs]).astype(BF16)
            dp_ref[:, 2 * w + g * gd:2 * w + (g + 1) * gd] = (
                da_g * ua_g * vs[g] * (sg * (1.0 + gate * (1.0 - sg)))).astype(BF16)
            dvs = da_g * ua_g * silu
            dvs_bf = dvs.astype(BF16)
            dvn_ref[:, cs] = _dot_nn(wst_ref[g].astype(BF16), dvs_bf)
            dws = _dot_nt(dvs_bf, vn[:, cs])
            dbs = jnp.sum(dvs, axis=1, keepdims=True)

            @pl.when(i == 0)
            def _():
                dws_ref[g] = dws
                dbst_ref[:, g:g + 1] = dbs

            @pl.when(i > 0)
            def _():
                dws_ref[g] += dws
                dbst_ref[:, g:g + 1] += dbs

        dvn = dvn_ref[...]
        _acc_rows(dlg_ref, i, jnp.sum(dvn * xh, axis=0, keepdims=True))
        _acc_rows(dlb_ref, i, jnp.sum(dvn, axis=0, keepdims=True))
        dxh = dvn * lg_ref[...]
        dvf = rstd * (dxh - jnp.mean(dxh, axis=-1, keepdims=True)
                      - xh * jnp.mean(dxh * xh, axis=-1, keepdims=True))
        dp_ref[:, w:2 * w] = (dvf * dv_fac).astype(BF16)

    full = lambda *shape: pl.BlockSpec(shape, lambda i: (0,) * len(shape))
    return pl.pallas_call(
        body, name=name, grid=(nb,),
        in_specs=[pl.BlockSpec((SGU_BLOCK, w), lambda i: (i, 0)), pl.BlockSpec((SGU_BLOCK, w3), lambda i: (i, 0)),
                  full(1, w), full(1, w), full(SGU_GROUPS, SGU_BLOCK, SGU_BLOCK),
                  full(SGU_GROUPS, SGU_BLOCK, SGU_BLOCK), full(SGU_BLOCK, SGU_GROUPS)],
        out_specs=[pl.BlockSpec((SGU_BLOCK, w3), lambda i: (i, 0)), full(SGU_GROUPS, SGU_BLOCK, SGU_BLOCK),
                   full(SGU_BLOCK, SGU_GROUPS), full(1, w), full(1, w)],
        out_shape=[jax.ShapeDtypeStruct((t, w3), BF16), jax.ShapeDtypeStruct((SGU_GROUPS, SGU_BLOCK, SGU_BLOCK), F32),
                   jax.ShapeDtypeStruct((SGU_BLOCK, SGU_GROUPS), F32), jax.ShapeDtypeStruct((1, w), F32),
                   jax.ShapeDtypeStruct((1, w), F32)],
        scratch_shapes=[pltpu.VMEM((SGU_BLOCK, w), F32)],
        compiler_params=_cparams(("arbitrary",)),
    )(da, proj, ln_gain, ln_bias, ws_masked, ws_masked_t, bs_t)


def _tile2d(rows, cols, block_bytes, row_unit):
    if rows % row_unit == 0:
        return _pick(rows, max(row_unit, block_bytes // (4 * cols)), row_unit), cols
    return rows, _pick(cols, max(LANES, block_bytes // (4 * rows)))


def _adamw(w, g, m, v, *, name, block_bytes=ELEMENTWISE_BLOCK_BYTES, after=None):
    rows, cols = w.shape
    tr, tc = _tile2d(rows, cols, block_bytes, 8)
    g_rows = g.shape[0]
    assert g_rows == rows or tr == rows
    extra_specs, extra_args = ([], []) if after is None else ([pl.BlockSpec(memory_space=pl.ANY)], [after])

    def body(w_ref, g_ref, m_ref, v_ref, *rest):
        go_ref, d_ref, mo_ref, vo_ref = rest[len(extra_args):]
        gv = g_ref[0:tr, :]
        go_ref[...] = gv
        mn = ADAM_B1 * m_ref[...] + (1.0 - ADAM_B1) * gv
        vn = ADAM_B2 * v_ref[...] + (1.0 - ADAM_B2) * (gv * gv)
        m_hat = mn / (1.0 - ADAM_B1 ** ADAM_STEP)
        v_hat = vn / (1.0 - ADAM_B2 ** ADAM_STEP)
        d_ref[...] = -ADAM_LR * (m_hat / (jnp.sqrt(v_hat) + ADAM_EPS) + ADAM_WD * w_ref[...])
        mo_ref[...] = mn
        vo_ref[...] = vn

    spec = pl.BlockSpec((tr, tc), lambda i, j: (i, j))
    g_spec = spec if g_rows == rows else pl.BlockSpec((g_rows, tc), lambda i, j: (0, j))
    return pl.pallas_call(
        body, name=name, grid=(rows // tr, cols // tc), in_specs=[spec, g_spec, spec, spec] + extra_specs,
        out_specs=[spec] * 4, out_shape=[jax.ShapeDtypeStruct((rows, cols), F32)] * 4,
        compiler_params=_cparams(("parallel", "parallel")),
    )(w, g, m, v, *extra_args)


def _matmul_dw_pair(a_me, a_sib, b_me, b_sib, core_idx, *, shards_on, name, after=None, part=(0, 1)):
    T, M = a_me.shape
    N = b_me.shape[1]
    if shards_on == "rows":
        p, count = part
        tm, hc = M // N_CHIPS, N // 2
        hp = hc // count
        tn = _pick(hp, 1024)
        per = hp // tn
        grid = (N_CHIPS, per)
        a_spec = pl.BlockSpec((T, tm), lambda i, n, h: (0, i))
        b_me_spec = pl.BlockSpec((T, tn), lambda i, n, h: (0, (h[0] * count + p) * per + n))
        b_sib_spec = pl.BlockSpec((T, tn), lambda i, n, h: (0, p * per + n))
        out_spec = pl.BlockSpec((None, tm, tn), lambda i, n, h: (i, 0, n))
        out_shape = jax.ShapeDtypeStruct((N_CHIPS, tm, hp), BF16)
    else:
        tm, hc = _pick(M, 1024), N // N_CHIPS // 2
        grid = (M // tm, N_CHIPS)
        a_spec = pl.BlockSpec((T, tm), lambda i, j, h: (0, i))
        b_me_spec = pl.BlockSpec((T, hc), lambda i, j, h: (0, 2 * j + h[0]))
        b_sib_spec = pl.BlockSpec((T, hc), lambda i, j, h: (0, j))
        out_spec = pl.BlockSpec((None, tm, hc), lambda i, j, h: (j, i, 0))
        out_shape = jax.ShapeDtypeStruct((N_CHIPS, M, hc), BF16)
    extra_specs, extra_args = ([], []) if after is None else ([pl.BlockSpec(memory_space=pl.ANY)], [after])

    def body(h_ref, am_ref, as_ref, bm_ref, bs_ref, *rest):
        o_ref = rest[len(extra_args)]
        o_ref[...] = (_dot_tn(am_ref[...], bm_ref[...]) + _dot_tn(as_ref[...], bs_ref[...])).astype(BF16)

    grid_spec = pltpu.PrefetchScalarGridSpec(
        num_scalar_prefetch=1, grid=grid, in_specs=[a_spec, a_spec, b_me_spec, b_sib_spec] + extra_specs,
        out_specs=out_spec)
    return pl.pallas_call(
        body, name=name, grid_spec=grid_spec, out_shape=out_shape, compiler_params=_cparams(("parallel", "parallel")),
    )(core_idx, a_me, a_sib, b_me, b_sib, *extra_args)


def _chip_sum(pair, landed, slots, *, name, block_bytes=ELEMENTWISE_BLOCK_BYTES, part=(0, 1), into=None):
    p, count = part
    _, r, hp = pair.shape
    tr, tc = _tile2d(r, hp, block_bytes, 16)
    ncb = hp // tc
    extra_specs, extra_args = ([], []) if into is None else ([pl.BlockSpec(memory_space=pl.ANY)], [into])

    def body(s_ref, own_ref, l0_ref, l1_ref, l2_ref, *rest):
        rest[-1][...] = ((own_ref[...].astype(F32) + l0_ref[...].astype(F32)) + l1_ref[...].astype(F32)
                         ) + l2_ref[...].astype(F32)

    def slab(which):
        return pl.BlockSpec((None, tr, tc), lambda i, k, s: (s[which], i, k))

    grid_spec = pltpu.PrefetchScalarGridSpec(
        num_scalar_prefetch=1, grid=(r // tr, ncb),
        in_specs=[slab(0), slab(1), slab(2), slab(3)] + extra_specs,
        out_specs=pl.BlockSpec((tr, tc), lambda i, k, s: (i, (s[4] * count + p) * ncb + k)))
    return pl.pallas_call(
        body, name=name, grid_spec=grid_spec, out_shape=jax.ShapeDtypeStruct((r, 2 * hp * count), F32),
        input_output_aliases={} if into is None else {5: 0},
        compiler_params=_cparams(("parallel", "parallel")),
    )(slots, pair, landed, landed, landed, *extra_args)


def _stack_sum(x, *, name, out_dtype=F32, block_bytes=ELEMENTWISE_BLOCK_BYTES):
    s, r, c = x.shape
    tr = _pick(r, max(8, block_bytes // (4 * c)), 16) if r % 16 == 0 else r

    def body(x_ref, o_ref):
        acc = x_ref[0].astype(F32)
        for j in range(1, s):
            acc = acc + x_ref[j].astype(F32)
        o_ref[...] = acc.astype(out_dtype)

    return pl.pallas_call(
        body, name=name, grid=(r // tr,),
        in_specs=[pl.BlockSpec((s, tr, c), lambda i: (0, i, 0))], out_specs=pl.BlockSpec((tr, c), lambda i: (i, 0)),
        out_shape=jax.ShapeDtypeStruct((r, c), out_dtype), compiler_params=_cparams(("parallel",)),
    )(x)


HBM = pl.BlockSpec(memory_space=pltpu.HBM)


def _place():
    x, y, c = lax.axis_index("x"), lax.axis_index("y"), lax.axis_index("c")
    other_chips = [(1 - x, y), (x, 1 - y), (1 - x, 1 - y)]
    return x, y, c, other_chips


def _handshake(peers):
    barrier = pltpu.get_barrier_semaphore()
    for peer in peers:
        pl.semaphore_signal(barrier, inc=1, device_id=peer, device_id_type=MESH)
    pl.semaphore_wait(barrier, len(peers))


def _sibling():
    x, y, c, _ = _place()
    return [(x, y, 1 - c)]


def _same_core_chips():
    x, y, c, chips = _place()
    return [(cx, cy, c) for cx, cy in chips]


def _same_core_neighbours():
    x, y, c, _ = _place()
    return [(1 - x, y, c), (x, 1 - y, c)]


def _split_params(cid):
    return pltpu.CompilerParams(has_side_effects=SIDE_EFFECT, collective_id=cid)


def _half_cols(cols, which):
    hc = cols // 2
    return pl.ds(pl.multiple_of(which * hc, LANES), hc)


SEM = pl.BlockSpec(memory_space=pltpu.SEMAPHORE)
ANY = pl.BlockSpec(memory_space=pl.ANY)
SIDE_EFFECT = pltpu.SideEffectType.DATAFLOW_SIDE_EFFECTING
TOKEN_SHAPE = (8, LANES)


def _hbm(shape, dtype):
    return pltpu.HBM(shape, dtype)


def _in_hbm(a):
    return pltpu.with_memory_space_constraint(a, pltpu.HBM)


def _gather_copy(src_ref, land_ref, ssem, rsem, k, chip_of_block, to, c):
    cols = src_ref.shape[1]
    return pltpu.make_async_remote_copy(
        src_ref=src_ref.at[:, _half_cols(cols, c)], dst_ref=land_ref.at[chip_of_block, :, _half_cols(cols, c)],
        send_sem=ssem.at[k], recv_sem=rsem.at[k], device_id=to, device_id_type=MESH)


NEIGHBOURS = (0, 1)
ALL_CHIPS = (0, 1, 2)


def _gather_start(shards, *, name, cid, after=(), relayed=(), own_slab=None):
    n = len(shards)
    after = list(after)

    def body(*refs):
        srcs, lands = refs[:n], refs[n:2 * n]
        outs = refs[2 * n + len(after):]
        token = outs[-1]
        _handshake(_same_core_chips())
        x, y, c, chips = _place()
        me = 2 * x + y
        for a in range(n):
            ssem, rsem = outs[4 * a], outs[4 * a + 1]
            for k in NEIGHBOURS if a in relayed else ALL_CHIPS:
                cx, cy = chips[k]
                _gather_copy(srcs[a], lands[a], ssem, rsem, k, me, (cx, cy, c), c).start()
        token[...] = jnp.zeros_like(token)

    out_shape, out_specs, aliases = [], [], {}
    for a, s in enumerate(shards):
        out_shape += [pltpu.SemaphoreType.DMA((3,)), pltpu.SemaphoreType.DMA((3,)), _hbm(s.shape, s.dtype),
                      _hbm((N_CHIPS,) + s.shape, s.dtype)]
        out_specs += [SEM, SEM, HBM, HBM]
        aliases[a] = 4 * a + 2
        aliases[n + a] = 4 * a + 3
    out_shape.append(jax.ShapeDtypeStruct(TOKEN_SHAPE, F32))
    out_specs.append(pl.BlockSpec(memory_space=pltpu.VMEM))
    lands = [lax.empty((N_CHIPS,) + s.shape, s.dtype) for s in shards]
    if own_slab is not None:
        lands = [lax.dynamic_update_slice(land, s[None], (own_slab, 0, 0)) for land, s in zip(lands, shards)]
    lands = [_in_hbm(land) for land in lands]
    res = pl.pallas_call(
        body, name=name, in_specs=[HBM] * (2 * n) + [ANY] * len(after), out_specs=out_specs, out_shape=out_shape,
        input_output_aliases=aliases, compiler_params=_split_params(cid),
    )(*[_in_hbm(s) for s in shards], *lands, *after)
    return [tuple(res[4 * a:4 * a + 4]) for a in range(n)], res[-1]


def _wait_call(wait_fn, parts, after, *, name):
    ssem, rsem, src, land = parts
    after = list(after) if isinstance(after, (list, tuple)) else [after]

    def body(src_ref, land_ref, ssem_ref, rsem_ref, *rest):
        wait_fn(src_ref, land_ref, ssem_ref, rsem_ref)

    return pl.pallas_call(
        body, name=name, in_specs=[HBM, HBM, SEM, SEM] + [ANY] * len(after), out_specs=[HBM, HBM],
        out_shape=[_hbm(src.shape, src.dtype), _hbm(land.shape, land.dtype)], input_output_aliases={0: 0, 1: 1},
        compiler_params=pltpu.CompilerParams(has_side_effects=SIDE_EFFECT),
    )(src, land, ssem, rsem, *after)


def _gather_wait(parts, after, *, name, ks=ALL_CHIPS):
    def wait(src_ref, land_ref, ssem_ref, rsem_ref):
        x, y, c, chips = _place()
        for k in ks:
            cx, cy = chips[k]
            cp = _gather_copy(src_ref, land_ref, ssem_ref, rsem_ref, k, 2 * cx + cy, (x, y, c), c)
            cp.wait_send()
            cp.wait_recv()

    return _wait_call(wait, parts, after, name=name)


def _relay_copy(buf_ref, ssem, rsem, k, slab, to, c):
    hr = buf_ref.shape[1] // 2
    part = buf_ref.at[slab, pl.ds(k * hr, hr), _half_cols(buf_ref.shape[2], c)]
    return pltpu.make_async_remote_copy(
        src_ref=part, dst_ref=part, send_sem=ssem.at[k], recv_sem=rsem.at[k], device_id=to, device_id_type=MESH)


def _relay_start(land, *, name, cid):
    def body(buf_ref, ssem, rsem, buf_out, token):
        _handshake(_same_core_neighbours())
        x, y, c, _ = _place()
        _relay_copy(buf_ref, ssem, rsem, 0, 2 * (1 - x) + y, (x, 1 - y, c), c).start()
        _relay_copy(buf_ref, ssem, rsem, 1, 2 * x + 1 - y, (1 - x, y, c), c).start()
        token[...] = jnp.zeros_like(token)

    res = pl.pallas_call(
        body, name=name, in_specs=[HBM], out_specs=[SEM, SEM, HBM, pl.BlockSpec(memory_space=pltpu.VMEM)],
        out_shape=[pltpu.SemaphoreType.DMA((2,)), pltpu.SemaphoreType.DMA((2,)), _hbm(land.shape, land.dtype),
                   jax.ShapeDtypeStruct(TOKEN_SHAPE, F32)],
        input_output_aliases={0: 2}, compiler_params=_split_params(cid),
    )(land)
    return tuple(res[:3]), res[3]


def _relay_wait(parts, after, *, name):
    ssem, rsem, buf = parts
    after = list(after) if isinstance(after, (list, tuple)) else [after]

    def body(buf_ref, ssem_ref, rsem_ref, *rest):
        x, y, c, _ = _place()
        diagonal = 2 * (1 - x) + 1 - y
        _relay_copy(buf_ref, ssem_ref, rsem_ref, 0, 2 * (1 - x) + y, (x, y, c), c).wait_send()
        _relay_copy(buf_ref, ssem_ref, rsem_ref, 1, 2 * x + 1 - y, (x, y, c), c).wait_send()
        _relay_copy(buf_ref, ssem_ref, rsem_ref, 0, diagonal, (x, y, c), c).wait_recv()
        _relay_copy(buf_ref, ssem_ref, rsem_ref, 1, diagonal, (x, y, c), c).wait_recv()

    return pl.pallas_call(
        body, name=name, in_specs=[HBM, SEM, SEM] + [ANY] * len(after), out_specs=HBM,
        out_shape=_hbm(buf.shape, buf.dtype), input_output_aliases={0: 0},
        compiler_params=pltpu.CompilerParams(has_side_effects=SIDE_EFFECT),
    )(buf, ssem, rsem, *after)


def _forward_copy(buf_ref, ssem, rsem, k, slab, which, to):
    part = buf_ref.at[slab, :, _half_cols(buf_ref.shape[2], which)]
    return pltpu.make_async_remote_copy(
        src_ref=part, dst_ref=part, send_sem=ssem.at[k], recv_sem=rsem.at[k], device_id=to, device_id_type=MESH)


def _sibling_forward(land, *, name, cid, ks=ALL_CHIPS):
    def body(_, buf, send_sems, recv_sems):
        _handshake(_sibling())
        x, y, c, chips = _place()
        copies = []
        for k in ks:
            cx, cy = chips[k]
            cp = _forward_copy(buf, send_sems, recv_sems, k, 2 * cx + cy, c, (x, y, 1 - c))
            cp.start()
            copies.append(cp)
        for k in ks:
            cx, cy = chips[k]
            _forward_copy(buf, send_sems, recv_sems, k, 2 * cx + cy, 1 - c, (x, y, c)).wait_recv()
        for cp in copies:
            cp.wait_send()

    return pl.pallas_call(
        body, name=name, in_specs=[HBM], out_specs=HBM, out_shape=jax.ShapeDtypeStruct(land.shape, land.dtype),
        input_output_aliases={0: 0},
        scratch_shapes=[pltpu.SemaphoreType.DMA((3,)), pltpu.SemaphoreType.DMA((3,))],
        compiler_params=pltpu.CompilerParams(collective_id=cid),
    )(land)


def _forward_start(land, *, name, cid, ks=ALL_CHIPS):
    def body(buf_ref, ssem, rsem, buf_out, token):
        _handshake(_sibling())
        x, y, c, chips = _place()
        for k in ks:
            cx, cy = chips[k]
            _forward_copy(buf_ref, ssem, rsem, k, 2 * cx + cy, c, (x, y, 1 - c)).start()
        token[...] = jnp.zeros_like(token)

    res = pl.pallas_call(
        body, name=name, in_specs=[HBM], out_specs=[SEM, SEM, HBM, pl.BlockSpec(memory_space=pltpu.VMEM)],
        out_shape=[pltpu.SemaphoreType.DMA((3,)), pltpu.SemaphoreType.DMA((3,)), _hbm(land.shape, land.dtype),
                   jax.ShapeDtypeStruct(TOKEN_SHAPE, F32)],
        input_output_aliases={0: 2}, compiler_params=_split_params(cid),
    )(land)
    return tuple(res[:3]), res[3]


def _forward_wait(parts, after, *, name, ks=ALL_CHIPS):
    ssem, rsem, buf = parts
    after = list(after) if isinstance(after, (list, tuple)) else [after]

    def body(buf_ref, ssem_ref, rsem_ref, *rest):
        x, y, c, chips = _place()
        for k in ks:
            cx, cy = chips[k]
            _forward_copy(buf_ref, ssem_ref, rsem_ref, k, 2 * cx + cy, c, (x, y, c)).wait_send()
            _forward_copy(buf_ref, ssem_ref, rsem_ref, k, 2 * cx + cy, 1 - c, (x, y, c)).wait_recv()

    return pl.pallas_call(
        body, name=name, in_specs=[HBM, SEM, SEM] + [ANY] * len(after), out_specs=HBM,
        out_shape=_hbm(buf.shape, buf.dtype), input_output_aliases={0: 0},
        compiler_params=pltpu.CompilerParams(has_side_effects=SIDE_EFFECT),
    )(buf, ssem, rsem, *after)


def _share_copy(buf_ref, ssem, rsem, a, which, to):
    part = buf_ref.at[:, _half_cols(buf_ref.shape[1], which)]
    return pltpu.make_async_remote_copy(
        src_ref=part, dst_ref=part, send_sem=ssem.at[a], recv_sem=rsem.at[a], device_id=to, device_id_type=MESH)


def _share_start(arrays, *, name, cid):
    n = len(arrays)

    def body(*refs):
        bufs, ssem, rsem, token = refs[:n], refs[n], refs[n + 1], refs[-1]
        _handshake(_sibling())
        x, y, c, _ = _place()
        for a in range(n):
            _share_copy(bufs[a], ssem, rsem, a, c, (x, y, 1 - c)).start()
        token[...] = jnp.zeros_like(token)

    res = pl.pallas_call(
        body, name=name, in_specs=[HBM] * n,
        out_specs=[SEM, SEM] + [HBM] * n + [pl.BlockSpec(memory_space=pltpu.VMEM)],
        out_shape=[pltpu.SemaphoreType.DMA((n,)), pltpu.SemaphoreType.DMA((n,))]
        + [_hbm(b.shape, b.dtype) for b in arrays] + [jax.ShapeDtypeStruct(TOKEN_SHAPE, F32)],
        input_output_aliases={a: 2 + a for a in range(n)}, compiler_params=_split_params(cid),
    )(*[_in_hbm(b) for b in arrays])
    return (res[0], res[1], list(res[2:2 + n])), res[-1]


def _share_wait(parts, after, *, name):
    ssem, rsem, bufs = parts
    n = len(bufs)
    after = list(after) if isinstance(after, (list, tuple)) else [after]

    def body(*refs):
        buf_refs, ssem_ref, rsem_ref = refs[:n], refs[n], refs[n + 1]
        x, y, c, _ = _place()
        for a in range(n):
            _share_copy(buf_refs[a], ssem_ref, rsem_ref, a, c, (x, y, c)).wait_send()
            _share_copy(buf_refs[a], ssem_ref, rsem_ref, a, 1 - c, (x, y, c)).wait_recv()

    return pl.pallas_call(
        body, name=name, in_specs=[HBM] * n + [SEM, SEM] + [ANY] * len(after), out_specs=[HBM] * n,
        out_shape=[_hbm(b.shape, b.dtype) for b in bufs], input_output_aliases={a: a for a in range(n)},
        compiler_params=pltpu.CompilerParams(has_side_effects=SIDE_EFFECT),
    )(*bufs, ssem, rsem, *after)


def _scatter_copy(src_ref, land_ref, ssem, rsem, k, src_slab, dst_slab, to):
    return pltpu.make_async_remote_copy(
        src_ref=src_ref.at[src_slab], dst_ref=land_ref.at[dst_slab], send_sem=ssem.at[k], recv_sem=rsem.at[k],
        device_id=to, device_id_type=MESH)


def _scatter_start(part, *, name, cid):
    def start(src_ref, land_ref, ssem, rsem):
        x, y, c, chips = _place()
        me = 2 * x + y
        for k, (cx, cy) in enumerate(chips):
            _scatter_copy(src_ref, land_ref, ssem, rsem, k, 2 * cx + cy, me, (cx, cy, c)).start()

    return _split_start(start, _same_core_chips, part, part.shape, N_CHIPS - 1, name=name, cid=cid)


def _scatter_wait(parts, after, *, name):
    def wait(src_ref, land_ref, ssem_ref, rsem_ref):
        x, y, c, chips = _place()
        for k, (cx, cy) in enumerate(chips):
            idx = 2 * cx + cy
            cp = _scatter_copy(src_ref, land_ref, ssem_ref, rsem_ref, k, idx, idx, (x, y, c))
            cp.wait_send()
            cp.wait_recv()

    return _wait_call(wait, parts, after, name=name)


def _split_start(start_fn, peers_fn, src, land_shape, n_sems, *, name, cid):
    def body(src_ref, land_ref, ssem, rsem, src_out, land_out, token):
        _handshake(peers_fn())
        start_fn(src_ref, land_ref, ssem, rsem)
        token[...] = jnp.zeros_like(token)

    res = pl.pallas_call(
        body, name=name, in_specs=[HBM, HBM], out_specs=[SEM, SEM, HBM, HBM, pl.BlockSpec(memory_space=pltpu.VMEM)],
        out_shape=[pltpu.SemaphoreType.DMA((n_sems,)), pltpu.SemaphoreType.DMA((n_sems,)), _hbm(src.shape, src.dtype),
                   _hbm(land_shape, src.dtype), jax.ShapeDtypeStruct(TOKEN_SHAPE, F32)],
        input_output_aliases={0: 2, 1: 3}, compiler_params=_split_params(cid),
    )(_in_hbm(src), _in_hbm(lax.empty(land_shape, src.dtype)))
    return tuple(res[:4]), res[4]


def _sibling_copies(src_ref, land_ref, ssem, rsem, k0, groups, which, to):
    def copy(k, src, dst):
        return pltpu.make_async_remote_copy(
            src_ref=src, dst_ref=dst, send_sem=ssem.at[k], recv_sem=rsem.at[k], device_id=to, device_id_type=MESH)

    if groups == 0:
        return [copy(k0, src_ref, land_ref)]
    hw = src_ref.shape[1] // groups // 2
    return [copy(k0 + j, src_ref.at[:, pl.ds(pl.multiple_of((2 * j + which) * hw, LANES), hw)],
                 land_ref.at[:, j * hw:(j + 1) * hw]) for j in range(groups)]


def _to_sibling_start(items, *, name, cid):
    n = len(items)
    shapes = [a.shape if g == 0 else (a.shape[0], a.shape[1] // 2) for a, g in items]
    first = [sum(max(g, 1) for _, g in items[:k]) for k in range(n + 1)]

    def body(*refs):
        srcs, lands, ssem, rsem, token = refs[:n], refs[n:2 * n], refs[2 * n], refs[2 * n + 1], refs[-1]
        _handshake(_sibling())
        x, y, c, _ = _place()
        for k, (_, g) in enumerate(items):
            for cp in _sibling_copies(srcs[k], lands[k], ssem, rsem, first[k], g, 1 - c, (x, y, 1 - c)):
                cp.start()
        token[...] = jnp.zeros_like(token)

    res = pl.pallas_call(
        body, name=name, in_specs=[HBM] * (2 * n),
        out_specs=[SEM, SEM] + [HBM] * (2 * n) + [pl.BlockSpec(memory_space=pltpu.VMEM)],
        out_shape=[pltpu.SemaphoreType.DMA((first[n],)), pltpu.SemaphoreType.DMA((first[n],))]
        + [_hbm(a.shape, a.dtype) for a, _ in items] + [_hbm(s, a.dtype) for s, (a, _) in zip(shapes, items)]
        + [jax.ShapeDtypeStruct(TOKEN_SHAPE, F32)],
        input_output_aliases={k: 2 + k for k in range(2 * n)}, compiler_params=_split_params(cid),
    )(*[_in_hbm(a) for a, _ in items], *[_in_hbm(lax.empty(s, a.dtype)) for s, (a, _) in zip(shapes, items)])
    return [(res[0], res[1], first[k], g, res[2 + k], res[2 + n + k]) for k, (_, g) in enumerate(items)], res[-1]


def _from_sibling(flight, after, *, name):
    ssem, rsem, k0, groups, src, land = flight

    def wait(src_ref, land_ref, ssem_ref, rsem_ref):
        x, y, c, _ = _place()
        for cp in _sibling_copies(src_ref, land_ref, ssem_ref, rsem_ref, k0, groups, 1 - c, (x, y, c)):
            cp.wait_send()
            cp.wait_recv()

    return _wait_call(wait, (ssem, rsem, src, land), after, name=name)


def _dev_peers(x, y, c, chips):
    return [(x, y, 1 - c)] + [(cx, cy, c) for cx, cy in chips] + [(cx, cy, 1 - c) for cx, cy in chips]


def _dev_gather_start(part, *, name, cid):
    def start(src_ref, land_ref, ssem, rsem):
        x, y, c, chips = _place()
        for k, to in enumerate(_dev_peers(x, y, c, chips)):
            pltpu.make_async_remote_copy(
                src_ref=src_ref, dst_ref=land_ref.at[4 * x + 2 * y + c], send_sem=ssem.at[k], recv_sem=rsem.at[k],
                device_id=to, device_id_type=MESH).start()

    return _split_start(start, lambda: _dev_peers(*_place()), part, (N_DEV,) + part.shape, N_DEV - 1, name=name,
                        cid=cid)


def _dev_gather_wait(parts, after, *, name):
    def wait(src_ref, land_ref, ssem_ref, rsem_ref):
        x, y, c, chips = _place()
        for k, (px, py, pc) in enumerate(_dev_peers(x, y, c, chips)):
            cp = pltpu.make_async_remote_copy(
                src_ref=src_ref, dst_ref=land_ref.at[4 * px + 2 * py + pc], send_sem=ssem_ref.at[k],
                recv_sem=rsem_ref.at[k], device_id=(x, y, c), device_id_type=MESH)
            cp.wait_send()
            cp.wait_recv()

    return _wait_call(wait, parts, after, name=name)[1]


def _sibling_share_halves(arrays, *, name, cid):
    n = len(arrays)

    def body(*refs):
        bufs = refs[n:2 * n]
        send_sems, recv_sems = refs[2 * n:]
        _handshake(_sibling())
        x, y, c, _ = _place()
        copies = []
        for a in range(n):
            mine = bufs[a].at[:, _half_cols(bufs[a].shape[1], c)]
            cp = pltpu.make_async_remote_copy(
                src_ref=mine, dst_ref=mine, send_sem=send_sems.at[a], recv_sem=recv_sems.at[a],
                device_id=(x, y, 1 - c), device_id_type=MESH)
            cp.start()
            copies.append(cp)
        for a in range(n):
            theirs = bufs[a].at[:, _half_cols(bufs[a].shape[1], 1 - c)]
            pltpu.make_async_remote_copy(
                src_ref=theirs, dst_ref=theirs, send_sem=send_sems.at[a], recv_sem=recv_sems.at[a],
                device_id=(x, y, c), device_id_type=MESH).wait_recv()
        for cp in copies:
            cp.wait_send()

    return pl.pallas_call(
        body, name=name, in_specs=[HBM] * n, out_specs=[HBM] * n,
        out_shape=[jax.ShapeDtypeStruct(h.shape, h.dtype) for h in arrays],
        input_output_aliases={a: a for a in range(n)},
        scratch_shapes=[pltpu.SemaphoreType.DMA((n,)), pltpu.SemaphoreType.DMA((n,))],
        compiler_params=pltpu.CompilerParams(collective_id=cid),
    )(*arrays)


def _pack(arrays, rows_multiple=16, width=LANES):
    flat = jnp.concatenate([a.astype(F32).reshape(-1) for a in arrays])
    total = flat.shape[0]
    rows = -(-total // width)
    rows = -(-rows // rows_multiple) * rows_multiple
    return jnp.pad(flat, (0, rows * width - total)).reshape(rows, width)


def _unpack(buf, shapes):
    flat = buf.reshape(-1)
    out, off = [], 0
    for s in shapes:
        n = math.prod(s)
        out.append(flat[off:off + n].reshape(s))
        off += n
    return out


def kernel(x, norm_pre, norm_post, gla_w_in, gla_w_gate2, gla_b_gate, gla_o_gain, gla_w_out, sgu_w_in, sgu_ln_gain, sgu_ln_bias, sgu_w_spatial, sgu_b_spatial, sgu_w_out, loss_target, m_norm_pre, m_norm_post, m_gla_w_in, m_gla_w_gate2, m_gla_b_gate, m_gla_o_gain, m_gla_w_out, m_sgu_w_in, m_sgu_ln_gain, m_sgu_ln_bias, m_sgu_w_spatial, m_sgu_b_spatial, m_sgu_w_out, v_norm_pre, v_norm_post, v_gla_w_in, v_gla_w_gate2, v_gla_b_gate, v_gla_o_gain, v_gla_w_out, v_sgu_w_in, v_sgu_ln_gain, v_sgu_ln_bias, v_sgu_w_spatial, v_sgu_b_spatial, v_sgu_w_out):
    _, t, d = x.shape
    dk = d // 2
    ws = gla_w_in.shape[2]
    wp = -(-ws // LANES) * LANES
    lay = (ws, wp)
    chip =2 * lax.axis_index("x") + lax.axis_index("y")
    core = lax.axis_index("c")
    core_idx = core.astype(jnp.int32).reshape(1)
    others = jnp.arange(N_CHIPS - 1, dtype=jnp.int32)
    others = others + (others >= chip).astype(jnp.int32)
    slots = jnp.concatenate([chip.astype(jnp.int32).reshape(1), others, core_idx])

    x0 = x[0]
    target = loss_target[0]

    wt_in_g, mt_in_g, vt_in_g = gla_w_in[0].T, m_gla_w_in[0].T, v_gla_w_in[0].T

    small_shard = _pack([gla_w_gate2[0], sgu_ln_gain[0], sgu_ln_bias[0]], rows_multiple=8, width=2 * LANES)
    own = [small_shard, jnp.pad(wt_in_g.astype(BF16), ((0, wp - ws), (0, 0)))]
    in_flight, token = _gather_start(own, name="gather_start_a", cid=0, relayed=(1,))

    def with_sibling_and_own(mine, land, name, cid):
        return lax.dynamic_update_slice(_sibling_forward(land, name=name + "_share", cid=cid), mine[None],
                                        (chip, 0, 0))

    h0 = _norm_pre(x0, norm_pre[0:1] + token[0:1, 0:1], name="pre0")
    g_small = with_sibling_and_own(*_gather_wait(in_flight[0], h0, name="w_small_wait"), "w_small", 12)
    mine, land = _gather_wait(in_flight[1], [g_small, wt_in_g, mt_in_g, vt_in_g], name="w_gla_in_wait", ks=NEIGHBOURS)
    relay, token = _relay_start(land, name="w_gla_in_relay", cid=11)
    crossing, token = _forward_start(relay[2], name="w_gla_in_share_near", cid=22, ks=NEIGHBOURS)
    own_later = [(p[0] + token[0, 0]).astype(BF16) for p in (gla_w_out, sgu_w_in, sgu_w_out)]
    in_flight_later, token = _gather_start(own_later, name="gather_start_b", cid=1, after=[token], own_slab=chip)
    in_flight = in_flight + in_flight_later
    land = _relay_wait((relay[0], relay[1], crossing[2]), token, name="w_gla_in_relay_wait")
    land = _forward_wait((crossing[0], crossing[1], land), token, name="w_gla_in_share_near_wait", ks=NEIGHBOURS)
    land = _sibling_forward(land, name="w_gla_in_share_far", cid=13, ks=(2,))
    wt_g = lax.dynamic_update_slice(land, mine[None], (chip, 0, 0)).reshape(N_CHIPS * wp, d)

    def behind(small, token):
        return small + token[0:1, 0:1]

    def arriving(i, after, name):
        mine, land = _gather_wait(in_flight[i], after, name=name + "_wait")
        crossing, token = _forward_start(land, name=name + "_share", cid=i)
        return (mine, crossing), token

    def arrived(pending, after, name):
        _, crossing = pending
        return _forward_wait(crossing, after, name=name + "_share_wait")

    shard_shapes = [gla_w_gate2.shape[1:], sgu_ln_gain.shape[1:], sgu_ln_bias.shape[1:]]
    per_chip = [_unpack(g_small[j], shard_shapes) for j in range(N_CHIPS)]
    w2_full = jnp.concatenate([p[0] for p in per_chip], axis=1)
    ln_gain = jnp.concatenate([p[1] for p in per_chip], axis=0)[None, :]
    ln_bias = jnp.concatenate([p[2] for p in per_chip], axis=0)[None, :]
    w2p = jnp.pad(w2_full, ((0, LANES - GLA_GATE_RANK), (0, 0)))

    pos_chunk = jnp.arange(SGU_BLOCK) // CHUNK
    mask = pos_chunk[:, None] >= pos_chunk[None, :]
    ws_masked = jnp.where(mask[None], sgu_w_spatial[0], 0.0)
    ws_masked_t = ws_masked.transpose(0, 2, 1)
    bs_t = sgu_b_spatial[0].T

    proj0 = _matmul(h0, wt_g, mode="nt", out_dtype=F32, name="gla_in", tn=wp)
    pending, tok = arriving(2, proj0, "w_gla_out")
    o0, a0, s_before, s_final = _gla_fwd(proj0, w2p, behind(gla_b_gate, tok), gla_o_gain, lay, name="gla_scan")
    w_out_g = arrived(pending, a0, "w_gla_out").reshape(d, d)
    y0 = _matmul(a0, w_out_g, mode="nn", out_dtype=F32, name="gla_out", tn=1024)
    pending, tok = arriving(3, y0, "w_sgu_in")
    x1, h1 = _post_then_pre(x0, y0, behind(norm_post[0:1], tok), norm_pre[1:2], name="post0_pre1")
    g_wi_s = arrived(pending, h1, "w_sgu_in")
    pending, tok = arriving(4, g_wi_s, "w_sgu_out")
    proj1 = _matmul(h1, g_wi_s, mode="nn", out_dtype=F32, name="sgu_in", b_shards=True, after=tok, tn=768)
    a1 = _sgu_fwd(proj1, ln_gain, ln_bias, ws_masked, bs_t, name="sgu_gate")
    w_out_s = arrived(pending, a1, "w_sgu_out").reshape(d, d)
    acts, tok = _to_sibling_start([(a1, 0), (a0, 0), (h1, 0), (h0, 1)], name="acts_to_sibling", cid=5)
    a1, a0, h1, h0 = [f[4] for f in acts]
    y1 = _matmul(a1, w_out_s, mode="nn", out_dtype=F32, name="sgu_out", after=tok, tn=1024)
    loss_part, dx2, dy1, d_post1 = _loss_head(x1, y1, norm_post[1:2], target, name="loss_head")

    def pair_gradient(a_sent, b_sent, after, shards_on, name, cid):
        a_me, a_sib = _from_sibling(a_sent, after, name=name + "_a_wait")
        b_me, b_sib = _from_sibling(b_sent, [a_sib] + list(after), name=name + "_b_wait")
        pair = _matmul_dw_pair(a_me, a_sib, b_me, b_sib, core_idx, shards_on=shards_on,
                               name=name + "_pair")
        return _scatter_start(pair, name=name + "_start", cid=cid)

    def reduced(flight, after, name):
        pair, landed = _scatter_wait(flight, after, name=name + "_wait")
        return _chip_sum(pair, landed, slots, name=name + "_sum")

    (dy1_sent,), tok = _to_sibling_start([(dy1, 1)], name="dy1_to_sibling", cid=6)
    dy1 = dy1_sent[4]
    da1 = _matmul(dy1, w_out_s, mode="nt", out_dtype=F32, name="d_sgu_act", after=tok, tn=1024)
    fl_wo_s, tok = pair_gradient(acts[0], dy1_sent, [da1], "rows", "g_sgu_out", 15)
    dproj1, d_ws, d_bs_t, d_lg, d_lb = _sgu_bwd(da1, proj1, ln_gain, behind(ln_bias, tok), ws_masked, ws_masked_t,
                                                bs_t, name="sgu_gate_bwd")
    (dp1_sent,), tok = _to_sibling_start([(dproj1, N_CHIPS)], name="dproj1_to_sibling", cid=7)
    dproj1 = dp1_sent[4]
    dh1 = _matmul_nt_shards(dproj1, g_wi_s, out_dtype=F32, name="d_sgu_h", after=tok)
    fl_wi_s, tok = pair_gradient(acts[2], dp1_sent, [dh1], "cols", "g_sgu_in", 16)
    dx1, dy0, d_pre1, d_post0 = _mid_bwd(dx2, dh1, x1, behind(norm_pre[1:2], tok), y0, norm_post[0:1],
                                         name="pre1_post0_bwd")
    (dy0_sent,), tok = _to_sibling_start([(dy0, 1)], name="dy0_to_sibling", cid=8)
    dy0 = dy0_sent[4]
    da0 = _matmul(dy0, w_out_g, mode="nt", out_dtype=F32, name="d_gla_act", after=tok, tn=1024)
    fl_wo_g, tok = pair_gradient(acts[1], dy0_sent, [da0], "rows", "g_gla_out", 17)
    dproj0, d_og, d_bg, d_w2p = _gla_bwd(da0, o0, proj0, w2p, behind(gla_b_gate, tok), gla_o_gain, s_before, s_final,
                                         lay, name="gla_scan_bwd")
    early_shapes = [norm_post.shape, gla_b_gate.shape, gla_o_gain.shape, sgu_w_spatial.shape, sgu_b_spatial.shape,
                    (1, GLA_GATE_RANK, dk), (1, d), (1, d), (1, LANES)]
    early_part = _pack([jnp.concatenate([d_post0, d_post1], axis=0), d_bg, d_og, jnp.where(mask[None], d_ws, 0.0)[None],
                        d_bs_t.T[None], d_w2p[:GLA_GATE_RANK][None], d_lg, d_lb, loss_part])
    early_flight, tok = _dev_gather_start(early_part, name="small_early_start", cid=20)
    (dp0_sent,), tok_sent = _to_sibling_start([(dproj0, 0)], name="dproj0_to_sibling", cid=9)
    dproj0 = dp0_sent[4]
    dh0 = _matmul(dproj0, wt_g, mode="nn", out_dtype=F32, name="d_gla_h", after=tok_sent)
    a_me, a_sib = _from_sibling(dp0_sent, [dh0, tok], name="g_gla_in_a_wait")
    b_me, b_sib = _from_sibling(acts[3], [a_sib, dh0], name="g_gla_in_b_wait")
    fl_wi_g, tok_scatter = [], None
    for p in range(2):
        pair = _matmul_dw_pair(a_me, a_sib, b_me, b_sib, core_idx, shards_on="rows", part=(p, 2),
                               name=f"g_gla_in_pair{p}", after=tok_scatter)
        flight, tok_scatter = _scatter_start(pair, name=f"g_gla_in_start{p}", cid=18 + p)
        fl_wi_g.append(flight)
    r_wo_s = reduced(fl_wo_s, tok_scatter, "g_sgu_out")
    r_wi_s = reduced(fl_wi_s, r_wo_s, "g_sgu_in")
    r_wo_g = reduced(fl_wo_g, r_wi_s, "g_gla_out")
    sharing, tok = _share_start([r_wo_s, r_wi_s, r_wo_g], name="grads_share_a", cid=10)
    grad_x, d_pre0 = _first_bwd(dx1, dh0, x0, behind(norm_pre[0:1], tok), name="pre0_bwd")

    late_part = _pack([jnp.concatenate([d_pre0, d_pre1], axis=0)])
    late_flight, tok = _dev_gather_start(late_part, name="small_late_start", cid=21)

    def big_update(w, g, m, v, name, after=None):
        return [u[None] for u in _adamw(w[0], g, m[0], v[0], name=name, after=after)]

    g_wo_sgu, g_wi_sgu, g_wo_gla = _share_wait(sharing, [grad_x, tok], name="grads_share_a_wait")
    u_wi_sgu = big_update(sgu_w_in, g_wi_sgu, m_sgu_w_in, v_sgu_w_in, "adamw_sgu_w_in")

    r_wi_g, behind_this = None, u_wi_sgu[1]
    for p, flight in enumerate(fl_wi_g):
        pair, landed = _scatter_wait(flight, behind_this, name=f"g_gla_in_wait{p}")
        r_wi_g = behind_this = _chip_sum(pair, landed, slots, part=(p, 2), into=r_wi_g, name=f"g_gla_in_sum{p}")
    gt_wi_gla, = _sibling_share_halves([r_wi_g], name="grads_share_b", cid=14)
    u_wi_gla_t = _adamw(wt_in_g, gt_wi_gla, mt_in_g, vt_in_g, name="adamw_gla_w_in")
    u_wi_gla = [u.T[None] for u in u_wi_gla_t]
    u_wo_gla = big_update(gla_w_out, g_wo_gla, m_gla_w_out, v_gla_w_out, "adamw_gla_w_out", after=u_wi_gla_t[1])
    u_wo_sgu = big_update(sgu_w_out, g_wo_sgu, m_sgu_w_out, v_sgu_w_out, "adamw_sgu_w_out", after=u_wo_gla[1])

    def summed_over_devices(part, flight, after, shapes, name):
        land = _dev_gather_wait(flight, after, name=name + "_wait")
        every = lax.dynamic_update_slice(land, part[None], (2 * chip + core, 0, 0))
        return _unpack(_stack_sum(every, name=name + "_sum"), shapes)

    (g_post, g_bg, g_og, g_wsp, g_bsp, g_w2_full, g_lg_full, g_lb_full, loss_vec) = summed_over_devices(
        early_part, early_flight, u_wo_sgu[1], early_shapes, "small_early")
    g_pre, = summed_over_devices(late_part, late_flight, loss_vec, [norm_pre.shape], "small_late")
    loss = loss_vec[0, 0]
    g_w2 = lax.dynamic_slice_in_dim(g_w2_full, chip * (dk // N_CHIPS), dk // N_CHIPS, axis=2)
    g_lg = lax.dynamic_slice_in_dim(g_lg_full, chip * (d // N_CHIPS), d // N_CHIPS, axis=1)
    g_lb = lax.dynamic_slice_in_dim(g_lb_full, chip * (d // N_CHIPS), d // N_CHIPS, axis=1)

    small_w = [norm_pre, norm_post, gla_b_gate, gla_o_gain, sgu_w_spatial, sgu_b_spatial, gla_w_gate2, sgu_ln_gain,
               sgu_ln_bias]
    small_g = [g_pre, g_post, g_bg, g_og, g_wsp, g_bsp, g_w2, g_lg, g_lb]
    small_m = [m_norm_pre, m_norm_post, m_gla_b_gate, m_gla_o_gain, m_sgu_w_spatial, m_sgu_b_spatial, m_gla_w_gate2,
               m_sgu_ln_gain, m_sgu_ln_bias]
    small_v = [v_norm_pre, v_norm_post, v_gla_b_gate, v_gla_o_gain, v_sgu_w_spatial, v_sgu_b_spatial, v_gla_w_gate2,
               v_sgu_ln_gain, v_sgu_ln_bias]
    own_shapes = [w.shape for w in small_w]
    _, s_dl, s_m, s_v = _adamw(_pack(small_w), _pack(small_g), _pack(small_m), _pack(small_v), name="adamw_small")
    dl_s, m_s, v_s = _unpack(s_dl, own_shapes), _unpack(s_m, own_shapes), _unpack(s_v, own_shapes)

    def ordered(small, kind):
        pre, post, bg, og, wsp, bsp, w2, lg, lb = small
        return [pre, post, u_wi_gla[kind], w2, bg, og, u_wo_gla[kind], u_wi_sgu[kind], lg, lb, wsp, bsp, u_wo_sgu[kind]]

    return (loss, grad_x[None], *ordered(small_g, 0), *ordered(dl_s, 1), *ordered(m_s, 2), *ordered(v_s, 3))
```

```python
import math

import jax
import jax.numpy as jnp
from jax import lax
from jax.experimental import pallas as pl
from jax.experimental.pallas import tpu as pltpu

F32 = jnp.float32
BF16 = jnp.bfloat16
MESH = pl.DeviceIdType.MESH

EPS = 1e-6
CHUNK = 64
GLA_HEADS = 4
GLA_GATE_RANK = 16
GLA_TAU = 16.0
SGU_BLOCK = 128
SGU_GROUPS = 8
N_CHIPS = 4
N_DEV = 8
LANES = 128

ADAM_LR = 0.001
ADAM_B1 = 0.9
ADAM_B2 = 0.999
ADAM_EPS = 1e-08
ADAM_WD = 0.01
ADAM_STEP = 10

VMEM_LIMIT = 56 * 1024 * 1024
ELEMENTWISE_BLOCK_BYTES = 2 << 20


def _cparams(sem=None):
    return pltpu.CompilerParams(dimension_semantics=sem, vmem_limit_bytes=VMEM_LIMIT)


def _pick(n, cap, unit=LANES):
    best = None
    for t in range(unit, min(n, cap) + 1, unit):
        if n % t == 0:
            best = t
    assert best is not None, (n, cap, unit)
    return best


def _dot(a, b, dims):
    return lax.dot_general(a, b, (dims, ((), ())), preferred_element_type=F32)


def _dot_nn(a, b):
    return _dot(a, b, ((1,), (0,)))


def _dot_nt(a, b):
    return _dot(a, b, ((1,), (1,)))


def _dot_tn(a, b):
    return _dot(a, b, ((0,), (0,)))


def _matmul(a, b, *, mode, out_dtype, name, tm=1024, tn=512, b_shards=False, after=None):
    M, K = a.shape
    if b_shards:
        ns, Kb, bc = b.shape
        N, tn = ns * bc, _pick(bc, tn)
        per = bc // tn
        b_spec = pl.BlockSpec((None, K, tn), lambda i, j: (j // per, 0, j % per))
    elif mode == "nt":
        N, Kb = b.shape
        tn = _pick(N, tn)
        b_spec = pl.BlockSpec((tn, K), lambda i, j: (j, 0))
    else:
        Kb, N = b.shape
        tn = _pick(N, tn)
        b_spec = pl.BlockSpec((K, tn), lambda i, j: (0, j))
    assert K == Kb and a.dtype == b.dtype == BF16, (a.shape, b.shape, mode)
    tm = _pick(M, tm)
    dims = ((1,), (1,)) if mode == "nt" else ((1,), (0,))
    extra_specs, extra_args = ([], []) if after is None else ([pl.BlockSpec(memory_space=pl.ANY)], [after])

    def body(a_ref, b_ref, *rest):
        rest[-1][...] = _dot(a_ref[...], b_ref[...], dims).astype(out_dtype)

    return pl.pallas_call(
        body, name=name, grid=(M // tm, N // tn),
        in_specs=[pl.BlockSpec((tm, K), lambda i, j: (i, 0)), b_spec] + extra_specs,
        out_specs=pl.BlockSpec((tm, tn), lambda i, j: (i, j)), out_shape=jax.ShapeDtypeStruct((M, N), out_dtype),
        compiler_params=_cparams(("parallel", "parallel")),
    )(a, b, *extra_args)


def _matmul_nt_shards(a, b, *, out_dtype, name, tm=1024, tn=512, after=None):
    M, K = a.shape
    ns, N, kc = b.shape
    assert K == ns * kc
    tm, tn = _pick(M, tm), _pick(N, tn)

    def body(a_ref, *rest):
        b_refs, o_ref = rest[:ns], rest[ns + (after is not None)]
        acc = _dot_nt(a_ref[:, 0:kc], b_refs[0][...])
        for j in range(1, ns):
            acc += _dot_nt(a_ref[:, j * kc:(j + 1) * kc], b_refs[j][...])
        o_ref[...] = acc.astype(out_dtype)

    def shard(j):
        return pl.BlockSpec((None, tn, kc), lambda i, n: (j, n, 0))

    extra_specs, extra_args = ([], []) if after is None else ([pl.BlockSpec(memory_space=pl.ANY)], [after])
    return pl.pallas_call(
        body, name=name, grid=(M // tm, N // tn),
        in_specs=[pl.BlockSpec((tm, K), lambda i, n: (i, 0))] + [shard(j) for j in range(ns)] + extra_specs,
        out_specs=pl.BlockSpec((tm, tn), lambda i, n: (i, n)), out_shape=jax.ShapeDtypeStruct((M, N), out_dtype),
        compiler_params=_cparams(("parallel", "parallel")),
    )(a, *([b] * ns), *extra_args)


def _rstd(x):
    return lax.rsqrt(jnp.mean(x * x, axis=-1, keepdims=True) + EPS)


def _row_spec(tr, d):
    return pl.BlockSpec((tr, d), lambda i: (i, 0))


def _vec_spec(d):
    return pl.BlockSpec((1, d), lambda i: (0, 0))


def _acc_rows(ref, i, val, cols=slice(None)):
    @pl.when(i == 0)
    def _():
        ref[:, cols] = val

    @pl.when(i > 0)
    def _():
        ref[:, cols] += val


def _norm_pre(x, gain, *, name, tr=256):
    t, d = x.shape
    tr = _pick(t, tr, 8)

    def body(x_ref, g_ref, h_ref):
        xv = x_ref[...]
        h_ref[...] = (xv * _rstd(xv) * g_ref[...]).astype(BF16)

    return pl.pallas_call(
        body, name=name, grid=(t // tr,), in_specs=[_row_spec(tr, d), _vec_spec(d)], out_specs=_row_spec(tr, d),
        out_shape=jax.ShapeDtypeStruct((t, d), BF16), compiler_params=_cparams(("parallel",)),
    )(x, gain)


def _post_then_pre(x, y, post_gain, pre_gain, *, name, tr=256):
    t, d = x.shape
    tr = _pick(t, tr, 8)

    def body(x_ref, y_ref, pg_ref, ng_ref, xn_ref, h_ref):
        yv = y_ref[...]
        xn = x_ref[...] + yv * _rstd(yv) * pg_ref[...]
        xn_ref[...] = xn
        h_ref[...] = (xn * _rstd(xn) * ng_ref[...]).astype(BF16)

    return pl.pallas_call(
        body, name=name, grid=(t // tr,),
        in_specs=[_row_spec(tr, d), _row_spec(tr, d), _vec_spec(d), _vec_spec(d)],
        out_specs=[_row_spec(tr, d), _row_spec(tr, d)],
        out_shape=[jax.ShapeDtypeStruct((t, d), F32), jax.ShapeDtypeStruct((t, d), BF16)],
        compiler_params=_cparams(("parallel",)),
    )(x, y, post_gain, pre_gain)


def _norm_bwd(dy, n, r, gain):
    dn = dy * gain
    return r * (dn - n * jnp.mean(dn * n, axis=-1, keepdims=True))


def _loss_head(x, y, post_gain, target, *, name, tr=256):
    t, d = x.shape
    tr = _pick(t, tr, 8)

    def body(x_ref, y_ref, pg_ref, t_ref, loss_ref, dx_ref, dy_ref, dpg_ref):
        i = pl.program_id(0)
        yv = y_ref[...]
        r = _rstd(yv)
        n = yv * r
        err = x_ref[...] + n * pg_ref[...] - t_ref[...]
        dx = err * (1.0 / d)
        dx_ref[...] = dx
        part = 0.5 * jnp.sum(jnp.mean(err * err, axis=-1, keepdims=True), axis=0, keepdims=True)
        _acc_rows(loss_ref, i, jnp.broadcast_to(part, (1, LANES)))
        _acc_rows(dpg_ref, i, jnp.sum(dx * n, axis=0, keepdims=True))
        dy_ref[...] = _norm_bwd(dx, n, r, pg_ref[...]).astype(BF16)

    return pl.pallas_call(
        body, name=name, grid=(t // tr,),
        in_specs=[_row_spec(tr, d), _row_spec(tr, d), _vec_spec(d), _row_spec(tr, d)],
        out_specs=[_vec_spec(LANES), _row_spec(tr, d), _row_spec(tr, d), _vec_spec(d)],
        out_shape=[jax.ShapeDtypeStruct((1, LANES), F32), jax.ShapeDtypeStruct((t, d), F32),
                   jax.ShapeDtypeStruct((t, d), BF16), jax.ShapeDtypeStruct((1, d), F32)],
        compiler_params=_cparams(("arbitrary",)),
    )(x, y, post_gain, target)


def _mid_bwd(dx_out, dh, x, pre_gain, y_prev, post_gain_prev, *, name, tr=256):
    t, d = x.shape
    tr = _pick(t, tr, 8)

    def body(dxo_ref, dh_ref, x_ref, ng_ref, y_ref, pg_ref, dx_ref, dy_ref, dng_ref, dpg_ref):
        i = pl.program_id(0)
        xv = x_ref[...]
        r = _rstd(xv)
        xh = xv * r
        dhv = dh_ref[...]
        _acc_rows(dng_ref, i, jnp.sum(dhv * xh, axis=0, keepdims=True))
        dx = dxo_ref[...] + _norm_bwd(dhv, xh, r, ng_ref[...])
        dx_ref[...] = dx
        yv = y_ref[...]
        ry = _rstd(yv)
        n = yv * ry
        _acc_rows(dpg_ref, i, jnp.sum(dx * n, axis=0, keepdims=True))
        dy_ref[...] = _norm_bwd(dx, n, ry, pg_ref[...]).astype(BF16)

    return pl.pallas_call(
        body, name=name, grid=(t // tr,),
        in_specs=[_row_spec(tr, d), _row_spec(tr, d), _row_spec(tr, d), _vec_spec(d), _row_spec(tr, d), _vec_spec(d)],
        out_specs=[_row_spec(tr, d), _row_spec(tr, d), _vec_spec(d), _vec_spec(d)],
        out_shape=[jax.ShapeDtypeStruct((t, d), F32), jax.ShapeDtypeStruct((t, d), BF16),
                   jax.ShapeDtypeStruct((1, d), F32), jax.ShapeDtypeStruct((1, d), F32)],
        compiler_params=_cparams(("arbitrary",)),
    )(dx_out, dh, x, pre_gain, y_prev, post_gain_prev)


def _first_bwd(dx_out, dh, x, pre_gain, *, name, tr=256):
    t, d = x.shape
    tr = _pick(t, tr, 8)

    def body(dxo_ref, dh_ref, x_ref, ng_ref, dx_ref, dng_ref):
        i = pl.program_id(0)
        xv = x_ref[...]
        r = _rstd(xv)
        xh = xv * r
        dhv = dh_ref[...]
        _acc_rows(dng_ref, i, jnp.sum(dhv * xh, axis=0, keepdims=True))
        dx_ref[...] = dxo_ref[...] + _norm_bwd(dhv, xh, r, ng_ref[...])

    return pl.pallas_call(
        body, name=name, grid=(t // tr,),
        in_specs=[_row_spec(tr, d), _row_spec(tr, d), _row_spec(tr, d), _vec_spec(d)],
        out_specs=[_row_spec(tr, d), _vec_spec(d)],
        out_shape=[jax.ShapeDtypeStruct((t, d), F32), jax.ShapeDtypeStruct((1, d), F32)],
        compiler_params=_cparams(("arbitrary",)),
    )(dx_out, dh, x, pre_gain)


def _sigmoid(x):
    return 1.0 / (1.0 + jnp.exp(-x))


def _log_sigmoid(x):
    return jnp.minimum(x, 0.0) - jnp.log(1.0 + jnp.exp(-jnp.abs(x)))


_GELU_C = math.sqrt(2.0 / math.pi)


_GELU_A = 0.044715


def _gelu_parts(x, with_grad=True):
    x2 = x * x
    h = 0.5 * jnp.tanh(x * (_GELU_C + (_GELU_C * _GELU_A) * x2)) + 0.5
    val = x * h
    if not with_grad:
        return val, None
    return val, h * (1.0 + (1.0 - h) * (x * (2.0 * _GELU_C + (6.0 * _GELU_C * _GELU_A) * x2)))


def _split3(x):
    hi = x.astype(BF16)
    r1 = x - hi.astype(F32)
    mid = r1.astype(BF16)
    lo = (r1 - mid.astype(F32)).astype(BF16)
    return hi, mid, lo


def _tri_matmul(tri_bf16, x):
    hi, mid, lo = _split3(x)
    return _dot_nn(tri_bf16, hi) + _dot_nn(tri_bf16, mid) + _dot_nn(tri_bf16, lo)


def _gla_dims(d):
    dk, dv = d // 2, d
    return dk, dv, dk // GLA_HEADS, dv // GLA_HEADS


def _col_pieces(a, b, lay):
    ws, wp = lay
    out = []
    while a < b:
        j = a // ws
        end = min(b, (j + 1) * ws)
        out.append((j * wp + a - j * ws, end - a))
        a = end
    return out


def _load_cols(ref, a, b, lay):
    parts = [ref[:, s:s + n] for s, n in _col_pieces(a, b, lay)]
    return parts[0] if len(parts) == 1 else jnp.concatenate(parts, axis=1)


def _store_cols(ref, a, val, lay):
    off = 0
    for s, n in _col_pieces(a, a + val.shape[1], lay):
        ref[:, s:s + n] = val[:, off:off + n]
        off += n


def _gate_window(c_r, lay):
    (start, _), = _col_pieces(c_r, c_r + GLA_GATE_RANK, lay)
    assert (start % lay[1]) + LANES <= lay[1]
    return slice(start, start + LANES)


def _gla_gates(glr, k, w2_ref, b_ref):
    z = _dot_nn(glr.astype(BF16), w2_ref[...].astype(BF16)) + b_ref[...]
    la = _log_sigmoid(z) * (1.0 / GLA_TAU)
    row = lax.broadcasted_iota(jnp.int32, (CHUNK, CHUNK), 0)
    col = lax.broadcasted_iota(jnp.int32, (CHUNK, CHUNK), 1)
    incl = (row >= col).astype(BF16)
    bcum = _tri_matmul(incl, la)
    b_end = bcum[CHUNK - 1:CHUNK, :]
    e_rest = jnp.exp(b_end - bcum)
    return z, e_rest, k * e_rest, jnp.exp(b_end)


def _gla_fwd(proj, w2p, b_gate, o_gain, lay, *, name):
    t, wcols = proj.shape
    d = o_gain.shape[1]
    dk, dv, dkh, dvh = _gla_dims(d)
    nc = t // CHUNK
    c_k, c_v, c_g, c_r = dk, 2 * dk, 2 * dk + dv, 2 * dk + 2 * dv
    scale = dkh ** -0.5

    def body(p_ref, w2_ref, b_ref, og_ref, o_ref, a_ref, sb_ref, sfin_ref, s_ref):
        i = pl.program_id(0)

        @pl.when(i == 0)
        def _():
            s_ref[...] = jnp.zeros_like(s_ref)

        q = _load_cols(p_ref, 0, dk, lay) * scale
        k = _load_cols(p_ref, c_k, c_k + dk, lay)
        glr = p_ref[:, _gate_window(c_r, lay)]
        _, _, kdec, decay = _gla_gates(glr, k, w2_ref, b_ref)
        for h in range(GLA_HEADS):
            ks = slice(h * dkh, (h + 1) * dkh)
            vs = slice(h * dvh, (h + 1) * dvh)
            v_h = _load_cols(p_ref, c_v + h * dvh, c_v + (h + 1) * dvh, lay)
            g_h = _load_cols(p_ref, c_g + h * dvh, c_g + (h + 1) * dvh, lay)
            s_old = s_ref[h]
            sb_ref[0, h] = s_old
            s_new = s_old * decay[:, ks] + _dot_tn(v_h.astype(BF16), kdec[:, ks].astype(BF16))
            s_ref[h] = s_new
            o_h = _dot_nt(q[:, ks].astype(BF16), s_new.astype(BF16))
            o_ref[:, vs] = o_h
            on = o_h * _rstd(o_h)
            a_ref[:, vs] = (on * og_ref[:, vs] * (g_h * _sigmoid(g_h))).astype(BF16)

        @pl.when(i == nc - 1)
        def _():
            sfin_ref[...] = s_ref[...]

    full = lambda *shape: pl.BlockSpec(shape, lambda i: (0,) * len(shape))
    return pl.pallas_call(
        body, name=name, grid=(nc,),
        in_specs=[pl.BlockSpec((CHUNK, wcols), lambda i: (i, 0)), full(LANES, dk), full(1, dk), full(1, dv)],
        out_specs=[pl.BlockSpec((CHUNK, dv), lambda i: (i, 0)), pl.BlockSpec((CHUNK, dv), lambda i: (i, 0)),
                   pl.BlockSpec((1, GLA_HEADS, dvh, dkh), lambda i: (i, 0, 0, 0)), full(GLA_HEADS, dvh, dkh)],
        out_shape=[jax.ShapeDtypeStruct((t, dv), F32), jax.ShapeDtypeStruct((t, dv), BF16),
                   jax.ShapeDtypeStruct((nc, GLA_HEADS, dvh, dkh), F32),
                   jax.ShapeDtypeStruct((GLA_HEADS, dvh, dkh), F32)],
        scratch_shapes=[pltpu.VMEM((GLA_HEADS, dvh, dkh), F32)],
        compiler_params=_cparams(("arbitrary",)),
    )(proj, w2p, b_gate, o_gain)


def _gla_bwd(da, o, proj, w2p, b_gate, o_gain, s_before, s_final, lay, *, name):
    t, wcols = proj.shape
    d = o_gain.shape[1]
    dk, dv, dkh, dvh = _gla_dims(d)
    nc = t // CHUNK
    c_k, c_v, c_g, c_r = dk, 2 * dk, 2 * dk + dv, 2 * dk + 2 * dv
    scale = dkh ** -0.5

    def body(da_ref, o_ref, p_ref, w2_ref, b_ref, og_ref, sb_ref, sfin_ref,
             dp_ref, dog_ref, db_ref, dw2_ref, s_ref, gc_ref, dkd_ref):
        i = pl.program_id(0)

        @pl.when(i == 0)
        def _():
            s_ref[...] = sfin_ref[...]
            gc_ref[...] = jnp.zeros_like(gc_ref)

        ws, wp = lay
        for j in range(N_CHIPS):
            dp_ref[:, j * wp + ws:(j + 1) * wp] = jnp.zeros((CHUNK, wp - ws), BF16)
        q = _load_cols(p_ref, 0, dk, lay) * scale
        k = _load_cols(p_ref, c_k, c_k + dk, lay)
        glr = p_ref[:, _gate_window(c_r, lay)]
        z, e_rest, kdec, decay = _gla_gates(glr, k, w2_ref, b_ref)
        ddecay = []
        for h in range(GLA_HEADS):
            ks = slice(h * dkh, (h + 1) * dkh)
            vs = slice(h * dvh, (h + 1) * dvh)
            v_h = _load_cols(p_ref, c_v + h * dvh, c_v + (h + 1) * dvh, lay)
            g_h = _load_cols(p_ref, c_g + h * dvh, c_g + (h + 1) * dvh, lay)
            da_h = da_ref[:, vs]
            o_h = o_ref[:, vs]
            og_h = og_ref[:, vs]
            r = _rstd(o_h)
            on = o_h * r
            sg = _sigmoid(g_h)
            silu = g_h * sg
            _acc_rows(dog_ref, i, jnp.sum(da_h * silu * on, axis=0, keepdims=True), vs)
            _store_cols(dp_ref, c_g + h * dvh, (da_h * (on * og_h) * (sg * (1.0 + g_h * (1.0 - sg)))).astype(BF16),
                        lay)
            don = da_h * silu * og_h
            do_h = (r * (don - on * jnp.mean(don * on, axis=-1, keepdims=True))).astype(BF16)
            s_cur = s_ref[h]
            _store_cols(dp_ref, h * dkh, (_dot_nn(do_h, s_cur.astype(BF16)) * scale).astype(BF16), lay)
            g_tot = gc_ref[h] + _dot_tn(do_h, q[:, ks].astype(BF16))
            g_bf = g_tot.astype(BF16)
            dkd_ref[:, ks] = _dot_nn(v_h.astype(BF16), g_bf)
            _store_cols(dp_ref, c_v + h * dvh, _dot_nt(kdec[:, ks].astype(BF16), g_bf).astype(BF16), lay)
            s_prev = sb_ref[0, h]
            ddecay.append(jnp.sum(g_tot * s_prev, axis=0, keepdims=True))
            gc_ref[h] = g_tot * decay[:, ks]
            s_ref[h] = s_prev
        dkdec = dkd_ref[...]
        _store_cols(dp_ref, c_k, (dkdec * e_rest).astype(BF16), lay)
        d_e = dkdec * kdec
        row = lax.broadcasted_iota(jnp.int32, (CHUNK, CHUNK), 0)
        col = lax.broadcasted_iota(jnp.int32, (CHUNK, CHUNK), 1)
        excl = (row > col).astype(BF16)
        dla = jnp.concatenate(ddecay, axis=1) * decay + _tri_matmul(excl, d_e)
        dz = dla * (1.0 / GLA_TAU) * (1.0 - _sigmoid(z))
        _acc_rows(db_ref, i, jnp.sum(dz, axis=0, keepdims=True))
        dz_bf = dz.astype(BF16)
        dw2 = _dot_tn(glr.astype(BF16), dz_bf)

        @pl.when(i == 0)
        def _():
            dw2_ref[...] = dw2

        @pl.when(i > 0)
        def _():
            dw2_ref[...] += dw2

        dp_ref[:, _gate_window(c_r, lay)] = _dot_nt(dz_bf, w2_ref[...].astype(BF16)).astype(BF16)

    rev = lambda i: (nc - 1 - i, 0)
    full = lambda *shape: pl.BlockSpec(shape, lambda i: (0,) * len(shape))
    return pl.pallas_call(
        body, name=name, grid=(nc,),
        in_specs=[pl.BlockSpec((CHUNK, dv), rev), pl.BlockSpec((CHUNK, dv), rev), pl.BlockSpec((CHUNK, wcols), rev),
                  full(LANES, dk), full(1, dk), full(1, dv),
                  pl.BlockSpec((1, GLA_HEADS, dvh, dkh), lambda i: (nc - 1 - i, 0, 0, 0)), full(GLA_HEADS, dvh, dkh)],
        out_specs=[pl.BlockSpec((CHUNK, wcols), rev), full(1, dv), full(1, dk), full(LANES, dk)],
        out_shape=[jax.ShapeDtypeStruct((t, wcols), BF16), jax.ShapeDtypeStruct((1, dv), F32),
                   jax.ShapeDtypeStruct((1, dk), F32), jax.ShapeDtypeStruct((LANES, dk), F32)],
        scratch_shapes=[pltpu.VMEM((GLA_HEADS, dvh, dkh), F32), pltpu.VMEM((GLA_HEADS, dvh, dkh), F32),
                        pltpu.VMEM((CHUNK, dk), F32)],
        compiler_params=_cparams(("arbitrary",)),
    )(da, o, proj, w2p, b_gate, o_gain, s_before, s_final)


def _sgu_mid(p_ref, lg_ref, lb_ref, ws_ref, bst_ref, w, with_grad=True):
    gd = w // SGU_GROUPS
    u_act, du_fac = _gelu_parts(p_ref[:, 0:w], with_grad)
    vf, dv_fac = _gelu_parts(p_ref[:, w:2 * w], with_grad)
    mu = jnp.mean(vf, axis=-1, keepdims=True)
    cen = vf - mu
    rstd = lax.rsqrt(jnp.mean(cen * cen, axis=-1, keepdims=True) + EPS)
    xh = cen * rstd
    vn = (xh * lg_ref[...] + lb_ref[...]).astype(BF16)
    vs = [_dot_nn(ws_ref[g].astype(BF16), vn[:, g * gd:(g + 1) * gd]) + bst_ref[:, g:g + 1]
          for g in range(SGU_GROUPS)]
    return u_act, du_fac, dv_fac, rstd, xh, vn, vs


def _sgu_fwd(proj, ln_gain, ln_bias, ws_masked, bs_t, *, name):
    t, w3 = proj.shape
    w = w3 // 3
    gd = w // SGU_GROUPS
    nb = t // SGU_BLOCK

    def body(p_ref, lg_ref, lb_ref, ws_ref, bst_ref, a_ref):
        u_act, _, _, _, _, _, vs = _sgu_mid(p_ref, lg_ref, lb_ref, ws_ref, bst_ref, w, with_grad=False)
        for g in range(SGU_GROUPS):
            cs = slice(g * gd, (g + 1) * gd)
            gate = p_ref[:, 2 * w + g * gd:2 * w + (g + 1) * gd]
            a_ref[:, cs] = (u_act[:, cs] * vs[g] * (gate * _sigmoid(gate))).astype(BF16)

    full = lambda *shape: pl.BlockSpec(shape, lambda i: (0,) * len(shape))
    return pl.pallas_call(
        body, name=name, grid=(nb,),
        in_specs=[pl.BlockSpec((SGU_BLOCK, w3), lambda i: (i, 0)), full(1, w), full(1, w),
                  full(SGU_GROUPS, SGU_BLOCK, SGU_BLOCK), full(SGU_BLOCK, SGU_GROUPS)],
        out_specs=pl.BlockSpec((SGU_BLOCK, w), lambda i: (i, 0)),
        out_shape=jax.ShapeDtypeStruct((t, w), BF16),
        compiler_params=_cparams(("parallel",)),
    )(proj, ln_gain, ln_bias, ws_masked, bs_t)


def _sgu_bwd(da, proj, ln_gain, ln_bias, ws_masked, ws_masked_t, bs_t, *, name):
    t, w3 = proj.shape
    w = w3 // 3
    gd = w // SGU_GROUPS
    nb = t // SGU_BLOCK

    def body(da_ref, p_ref, lg_ref, lb_ref, ws_ref, wst_ref, bst_ref, dp_ref, dws_ref, dbst_ref, dlg_ref, dlb_ref,
             dvn_ref):
        i = pl.program_id(0)
        u_act, du_fac, dv_fac, rstd, xh, vn, vs = _sgu_mid(p_ref, lg_ref, lb_ref, ws_ref, bst_ref, w)
        for g in range(SGU_GROUPS):
            cs = slice(g * gd, (g + 1) * gd)
            gate = p_ref[:, 2 * w + g * gd:2 * w + (g + 1) * gd]
            sg = _sigmoid(gate)
            silu = gate * sg
            da_g = da_ref[:, cs]
            ua_g = u_act[:, cs]
            dp_ref[:, cs] = (da_g * vs[g] * silu * du_fac[:, cs]).astype(BF16)
            dp_ref[:, 2 * w + g * gd:2 * w + (g + 1) * gd] = (
                da_g * ua_g * vs[g] * (sg * (1.0 + gate * (1.0 - sg)))).astype(BF16)
            dvs = da_g * ua_g * silu
            dvs_bf = dvs.astype(BF16)
            dvn_ref[:, cs] = _dot_nn(wst_ref[g].astype(BF16), dvs_bf)
            dws = _dot_nt(dvs_bf, vn[:, cs])
            dbs = jnp.sum(dvs, axis=1, keepdims=True)

            @pl.when(i == 0)
            def _():
                dws_ref[g] = dws
                dbst_ref[:, g:g + 1] = dbs

            @pl.when(i > 0)
            def _():
                dws_ref[g] += dws
                dbst_ref[:, g:g + 1] += dbs

        dvn = dvn_ref[...]
        _acc_rows(dlg_ref, i, jnp.sum(dvn * xh, axis=0, keepdims=True))
        _acc_rows(dlb_ref, i, jnp.sum(dvn, axis=0, keepdims=True))
        dxh = dvn * lg_ref[...]
        dvf = rstd * (dxh - jnp.mean(dxh, axis=-1, keepdims=True)
                      - xh * jnp.mean(dxh * xh, axis=-1, keepdims=True))
        dp_ref[:, w:2 * w] = (dvf * dv_fac).astype(BF16)

    full = lambda *shape: pl.BlockSpec(shape, lambda i: (0,) * len(shape))
    return pl.pallas_call(
        body, name=name, grid=(nb,),
        in_specs=[pl.BlockSpec((SGU_BLOCK, w), lambda i: (i, 0)), pl.BlockSpec((SGU_BLOCK, w3), lambda i: (i, 0)),
                  full(1, w), full(1, w), full(SGU_GROUPS, SGU_BLOCK, SGU_BLOCK),
                  full(SGU_GROUPS, SGU_BLOCK, SGU_BLOCK), full(SGU_BLOCK, SGU_GROUPS)],
        out_specs=[pl.BlockSpec((SGU_BLOCK, w3), lambda i: (i, 0)), full(SGU_GROUPS, SGU_BLOCK, SGU_BLOCK),
                   full(SGU_BLOCK, SGU_GROUPS), full(1, w), full(1, w)],
        out_shape=[jax.ShapeDtypeStruct((t, w3), BF16), jax.ShapeDtypeStruct((SGU_GROUPS, SGU_BLOCK, SGU_BLOCK), F32),
                   jax.ShapeDtypeStruct((SGU_BLOCK, SGU_GROUPS), F32), jax.ShapeDtypeStruct((1, w), F32),
                   jax.ShapeDtypeStruct((1, w), F32)],
        scratch_shapes=[pltpu.VMEM((SGU_BLOCK, w), F32)],
        compiler_params=_cparams(("arbitrary",)),
    )(da, proj, ln_gain, ln_bias, ws_masked, ws_masked_t, bs_t)


def _tile2d(rows, cols, block_bytes, row_unit):
    if rows % row_unit == 0:
        return _pick(rows, max(row_unit, block_bytes // (4 * cols)), row_unit), cols
    return rows, _pick(cols, max(LANES, block_bytes // (4 * rows)))


def _adamw(w, g, m, v, *, name, block_bytes=ELEMENTWISE_BLOCK_BYTES, after=None):
    rows, cols = w.shape
    tr, tc = _tile2d(rows, cols, block_bytes, 8)
    g_rows = g.shape[0]
    assert g_rows == rows or tr == rows
    extra_specs, extra_args = ([], []) if after is None else ([pl.BlockSpec(memory_space=pl.ANY)], [after])

    def body(w_ref, g_ref, m_ref, v_ref, *rest):
        go_ref, d_ref, mo_ref, vo_ref = rest[len(extra_args):]
        gv = g_ref[0:tr, :]
        go_ref[...] = gv
        mn = ADAM_B1 * m_ref[...] + (1.0 - ADAM_B1) * gv
        vn = ADAM_B2 * v_ref[...] + (1.0 - ADAM_B2) * (gv * gv)
        m_hat = mn / (1.0 - ADAM_B1 ** ADAM_STEP)
        v_hat = vn / (1.0 - ADAM_B2 ** ADAM_STEP)
        d_ref[...] = -ADAM_LR * (m_hat / (jnp.sqrt(v_hat) + ADAM_EPS) + ADAM_WD * w_ref[...])
        mo_ref[...] = mn
        vo_ref[...] = vn

    spec = pl.BlockSpec((tr, tc), lambda i, j: (i, j))
    g_spec = spec if g_rows == rows else pl.BlockSpec((g_rows, tc), lambda i, j: (0, j))
    return pl.pallas_call(
        body, name=name, grid=(rows // tr, cols // tc), in_specs=[spec, g_spec, spec, spec] + extra_specs,
        out_specs=[spec] * 4, out_shape=[jax.ShapeDtypeStruct((rows, cols), F32)] * 4,
        compiler_params=_cparams(("parallel", "parallel")),
    )(w, g, m, v, *extra_args)


def _adamw_in_two(w, g, m, v, *, name, block_bytes=ELEMENTWISE_BLOCK_BYTES):
    rows, cols = w.shape
    tr, tc = _tile2d(rows, cols, block_bytes, 8)
    g_rows = g.shape[0]
    assert g_rows == rows or tr == rows

    def moment_body(g_ref, m_ref, go_ref, mo_ref):
        gv = g_ref[0:tr, :]
        go_ref[...] = gv
        mo_ref[...] = ADAM_B1 * m_ref[...] + (1.0 - ADAM_B1) * gv

    def rest_body(w_ref, g_ref, mn_ref, v_ref, d_ref, vo_ref):
        gv = g_ref[...]
        vn = ADAM_B2 * v_ref[...] + (1.0 - ADAM_B2) * (gv * gv)
        m_hat = mn_ref[...] / (1.0 - ADAM_B1 ** ADAM_STEP)
        v_hat = vn / (1.0 - ADAM_B2 ** ADAM_STEP)
        d_ref[...] = -ADAM_LR * (m_hat / (jnp.sqrt(v_hat) + ADAM_EPS) + ADAM_WD * w_ref[...])
        vo_ref[...] = vn

    spec = pl.BlockSpec((tr, tc), lambda i, j: (i, j))
    g_spec = spec if g_rows == rows else pl.BlockSpec((g_rows, tc), lambda i, j: (0, j))
    grid, shape = (rows // tr, cols // tc), jax.ShapeDtypeStruct((rows, cols), F32)
    g_own, mn = pl.pallas_call(
        moment_body, name=name + "_moment", grid=grid, in_specs=[g_spec, spec], out_specs=[spec] * 2,
        out_shape=[shape] * 2, compiler_params=_cparams(("parallel", "parallel")),
    )(g, m)
    delta, vn = pl.pallas_call(
        rest_body, name=name, grid=grid, in_specs=[spec] * 4, out_specs=[spec] * 2,
        out_shape=[shape] * 2, compiler_params=_cparams(("parallel", "parallel")),
    )(w, g_own, mn, v)
    return g_own, delta, mn, vn


def _matmul_dw_pair(a_me, a_sib, b_me, b_sib, core_idx, *, shards_on, name, after=None, part=(0, 1)):
    T, M = a_me.shape
    N = b_me.shape[1]
    if shards_on == "rows":
        p, count = part
        tm, hc = M // N_CHIPS, N // 2
        hp = hc // count
        tn = _pick(hp, 1024)
        per = hp // tn
        grid = (N_CHIPS, per)
        a_spec = pl.BlockSpec((T, tm), lambda i, n, h: (0, i))
        b_me_spec = pl.BlockSpec((T, tn), lambda i, n, h: (0, (h[0] * count + p) * per + n))
        b_sib_spec = pl.BlockSpec((T, tn), lambda i, n, h: (0, p * per + n))
        out_spec = pl.BlockSpec((None, tm, tn), lambda i, n, h: (i, 0, n))
        out_shape = jax.ShapeDtypeStruct((N_CHIPS, tm, hp), BF16)
    else:
        tm, hc = _pick(M, 1024), N // N_CHIPS // 2
        grid = (M // tm, N_CHIPS)
        a_spec = pl.BlockSpec((T, tm), lambda i, j, h: (0, i))
        b_me_spec = pl.BlockSpec((T, hc), lambda i, j, h: (0, 2 * j + h[0]))
        b_sib_spec = pl.BlockSpec((T, hc), lambda i, j, h: (0, j))
        out_spec = pl.BlockSpec((None, tm, hc), lambda i, j, h: (j, i, 0))
        out_shape = jax.ShapeDtypeStruct((N_CHIPS, M, hc), BF16)
    extra_specs, extra_args = ([], []) if after is None else ([pl.BlockSpec(memory_space=pl.ANY)], [after])

    def body(h_ref, am_ref, as_ref, bm_ref, bs_ref, *rest):
        o_ref = rest[len(extra_args)]
        o_ref[...] = (_dot_tn(am_ref[...], bm_ref[...]) + _dot_tn(as_ref[...], bs_ref[...])).astype(BF16)

    grid_spec = pltpu.PrefetchScalarGridSpec(
        num_scalar_prefetch=1, grid=grid, in_specs=[a_spec, a_spec, b_me_spec, b_sib_spec] + extra_specs,
        out_specs=out_spec)
    return pl.pallas_call(
        body, name=name, grid_spec=grid_spec, out_shape=out_shape, compiler_params=_cparams(("parallel", "parallel")),
    )(core_idx, a_me, a_sib, b_me, b_sib, *extra_args)


def _chip_sum(pair, landed, slots, *, name, block_bytes=ELEMENTWISE_BLOCK_BYTES, part=(0, 1), into=None):
    p, count = part
    _, r, hp = pair.shape
    tr, tc = _tile2d(r, hp, block_bytes, 16)
    ncb = hp // tc
    extra_specs, extra_args = ([], []) if into is None else ([pl.BlockSpec(memory_space=pl.ANY)], [into])

    def body(s_ref, own_ref, l0_ref, l1_ref, l2_ref, *rest):
        rest[-1][...] = ((own_ref[...].astype(F32) + l0_ref[...].astype(F32)) + l1_ref[...].astype(F32)
                         ) + l2_ref[...].astype(F32)

    def slab(which):
        return pl.BlockSpec((None, tr, tc), lambda i, k, s: (s[which], i, k))

    grid_spec = pltpu.PrefetchScalarGridSpec(
        num_scalar_prefetch=1, grid=(r // tr, ncb),
        in_specs=[slab(0), slab(1), slab(2), slab(3)] + extra_specs,
        out_specs=pl.BlockSpec((tr, tc), lambda i, k, s: (i, (s[4] * count + p) * ncb + k)))
    return pl.pallas_call(
        body, name=name, grid_spec=grid_spec, out_shape=jax.ShapeDtypeStruct((r, 2 * hp * count), F32),
        input_output_aliases={} if into is None else {5: 0},
        compiler_params=_cparams(("parallel", "parallel")),
    )(slots, pair, landed, landed, landed, *extra_args)


def _stack_sum(x, *, name, out_dtype=F32, block_bytes=ELEMENTWISE_BLOCK_BYTES):
    s, r, c = x.shape
    tr = _pick(r, max(8, block_bytes // (4 * c)), 16) if r % 16 == 0 else r

    def body(x_ref, o_ref):
        acc = x_ref[0].astype(F32)
        for j in range(1, s):
            acc = acc + x_ref[j].astype(F32)
        o_ref[...] = acc.astype(out_dtype)

    return pl.pallas_call(
        body, name=name, grid=(r // tr,),
        in_specs=[pl.BlockSpec((s, tr, c), lambda i: (0, i, 0))], out_specs=pl.BlockSpec((tr, c), lambda i: (i, 0)),
        out_shape=jax.ShapeDtypeStruct((r, c), out_dtype), compiler_params=_cparams(("parallel",)),
    )(x)


HBM = pl.BlockSpec(memory_space=pltpu.HBM)


def _place():
    x, y, c = lax.axis_index("x"), lax.axis_index("y"), lax.axis_index("c")
    other_chips = [(1 - x, y), (x, 1 - y), (1 - x, 1 - y)]
    return x, y, c, other_chips


def _handshake(peers):
    barrier = pltpu.get_barrier_semaphore()
    for peer in peers:
        pl.semaphore_signal(barrier, inc=1, device_id=peer, device_id_type=MESH)
    pl.semaphore_wait(barrier, len(peers))


def _sibling():
    x, y, c, _ = _place()
    return [(x, y, 1 - c)]


def _same_core_chips():
    x, y, c, chips = _place()
    return [(cx, cy, c) for cx, cy in chips]


def _same_core_neighbours():
    x, y, c, _ = _place()
    return [(1 - x, y, c), (x, 1 - y, c)]


def _split_params(cid):
    return pltpu.CompilerParams(has_side_effects=SIDE_EFFECT, collective_id=cid)


def _half_cols(cols, which):
    hc = cols // 2
    return pl.ds(pl.multiple_of(which * hc, LANES), hc)


SEM = pl.BlockSpec(memory_space=pltpu.SEMAPHORE)
ANY = pl.BlockSpec(memory_space=pl.ANY)
SIDE_EFFECT = pltpu.SideEffectType.DATAFLOW_SIDE_EFFECTING
TOKEN_SHAPE = (8, LANES)


def _hbm(shape, dtype):
    return pltpu.HBM(shape, dtype)


def _in_hbm(a):
    return pltpu.with_memory_space_constraint(a, pltpu.HBM)


def _gather_copy(src_ref, land_ref, ssem, rsem, k, chip_of_block, to, c):
    cols = src_ref.shape[1]
    return pltpu.make_async_remote_copy(
        src_ref=src_ref.at[:, _half_cols(cols, c)], dst_ref=land_ref.at[chip_of_block, :, _half_cols(cols, c)],
        send_sem=ssem.at[k], recv_sem=rsem.at[k], device_id=to, device_id_type=MESH)


NEIGHBOURS = (0, 1)
ALL_CHIPS = (0, 1, 2)


def _gather_start(shards, *, name, cid, after=(), relayed=(), own_slab=None):
    n = len(shards)
    after = list(after)

    def body(*refs):
        srcs, lands = refs[:n], refs[n:2 * n]
        outs = refs[2 * n + len(after):]
        token = outs[-1]
        _handshake(_same_core_chips())
        x, y, c, chips = _place()
        me = 2 * x + y
        for a in range(n):
            ssem, rsem = outs[4 * a], outs[4 * a + 1]
            for k in NEIGHBOURS if a in relayed else ALL_CHIPS:
                cx, cy = chips[k]
                _gather_copy(srcs[a], lands[a], ssem, rsem, k, me, (cx, cy, c), c).start()
        token[...] = jnp.zeros_like(token)

    out_shape, out_specs, aliases = [], [], {}
    for a, s in enumerate(shards):
        out_shape += [pltpu.SemaphoreType.DMA((3,)), pltpu.SemaphoreType.DMA((3,)), _hbm(s.shape, s.dtype),
                      _hbm((N_CHIPS,) + s.shape, s.dtype)]
        out_specs += [SEM, SEM, HBM, HBM]
        aliases[a] = 4 * a + 2
        aliases[n + a] = 4 * a + 3
    out_shape.append(jax.ShapeDtypeStruct(TOKEN_SHAPE, F32))
    out_specs.append(pl.BlockSpec(memory_space=pltpu.VMEM))
    lands = [lax.empty((N_CHIPS,) + s.shape, s.dtype) for s in shards]
    if own_slab is not None:
        lands = [lax.dynamic_update_slice(land, s[None], (own_slab, 0, 0)) for land, s in zip(lands, shards)]
    lands = [_in_hbm(land) for land in lands]
    res = pl.pallas_call(
        body, name=name, in_specs=[HBM] * (2 * n) + [ANY] * len(after), out_specs=out_specs, out_shape=out_shape,
        input_output_aliases=aliases, compiler_params=_split_params(cid),
    )(*[_in_hbm(s) for s in shards], *lands, *after)
    return [tuple(res[4 * a:4 * a + 4]) for a in range(n)], res[-1]


def _wait_call(wait_fn, parts, after, *, name):
    ssem, rsem, src, land = parts
    after = list(after) if isinstance(after, (list, tuple)) else [after]

    def body(src_ref, land_ref, ssem_ref, rsem_ref, *rest):
        wait_fn(src_ref, land_ref, ssem_ref, rsem_ref)

    return pl.pallas_call(
        body, name=name, in_specs=[HBM, HBM, SEM, SEM] + [ANY] * len(after), out_specs=[HBM, HBM],
        out_shape=[_hbm(src.shape, src.dtype), _hbm(land.shape, land.dtype)], input_output_aliases={0: 0, 1: 1},
        compiler_params=pltpu.CompilerParams(has_side_effects=SIDE_EFFECT),
    )(src, land, ssem, rsem, *after)


def _gather_wait(parts, after, *, name, ks=ALL_CHIPS):
    def wait(src_ref, land_ref, ssem_ref, rsem_ref):
        x, y, c, chips = _place()
        for k in ks:
            cx, cy = chips[k]
            cp = _gather_copy(src_ref, land_ref, ssem_ref, rsem_ref, k, 2 * cx + cy, (x, y, c), c)
            cp.wait_send()
            cp.wait_recv()

    return _wait_call(wait, parts, after, name=name)


def _relay_copy(buf_ref, ssem, rsem, k, slab, to, c):
    hr = buf_ref.shape[1] // 2
    part = buf_ref.at[slab, pl.ds(k * hr, hr), _half_cols(buf_ref.shape[2], c)]
    return pltpu.make_async_remote_copy(
        src_ref=part, dst_ref=part, send_sem=ssem.at[k], recv_sem=rsem.at[k], device_id=to, device_id_type=MESH)


def _relay_start(land, *, name, cid):
    def body(buf_ref, ssem, rsem, buf_out, token):
        _handshake(_same_core_neighbours())
        x, y, c, _ = _place()
        _relay_copy(buf_ref, ssem, rsem, 0, 2 * (1 - x) + y, (x, 1 - y, c), c).start()
        _relay_copy(buf_ref, ssem, rsem, 1, 2 * x + 1 - y, (1 - x, y, c), c).start()
        token[...] = jnp.zeros_like(token)

    res = pl.pallas_call(
        body, name=name, in_specs=[HBM], out_specs=[SEM, SEM, HBM, pl.BlockSpec(memory_space=pltpu.VMEM)],
        out_shape=[pltpu.SemaphoreType.DMA((2,)), pltpu.SemaphoreType.DMA((2,)), _hbm(land.shape, land.dtype),
                   jax.ShapeDtypeStruct(TOKEN_SHAPE, F32)],
        input_output_aliases={0: 2}, compiler_params=_split_params(cid),
    )(land)
    return tuple(res[:3]), res[3]


def _relay_wait(parts, after, *, name):
    ssem, rsem, buf = parts
    after = list(after) if isinstance(after, (list, tuple)) else [after]

    def body(buf_ref, ssem_ref, rsem_ref, *rest):
        x, y, c, _ = _place()
        diagonal = 2 * (1 - x) + 1 - y
        _relay_copy(buf_ref, ssem_ref, rsem_ref, 0, 2 * (1 - x) + y, (x, y, c), c).wait_send()
        _relay_copy(buf_ref, ssem_ref, rsem_ref, 1, 2 * x + 1 - y, (x, y, c), c).wait_send()
        _relay_copy(buf_ref, ssem_ref, rsem_ref, 0, diagonal, (x, y, c), c).wait_recv()
        _relay_copy(buf_ref, ssem_ref, rsem_ref, 1, diagonal, (x, y, c), c).wait_recv()

    return pl.pallas_call(
        body, name=name, in_specs=[HBM, SEM, SEM] + [ANY] * len(after), out_specs=HBM,
        out_shape=_hbm(buf.shape, buf.dtype), input_output_aliases={0: 0},
        compiler_params=pltpu.CompilerParams(has_side_effects=SIDE_EFFECT),
    )(buf, ssem, rsem, *after)


def _forward_copy(buf_ref, ssem, rsem, k, slab, which, to):
    part = buf_ref.at[slab, :, _half_cols(buf_ref.shape[2], which)]
    return pltpu.make_async_remote_copy(
        src_ref=part, dst_ref=part, send_sem=ssem.at[k], recv_sem=rsem.at[k], device_id=to, device_id_type=MESH)


def _sibling_forward(land, *, name, cid, ks=ALL_CHIPS):
    def body(_, buf, send_sems, recv_sems):
        _handshake(_sibling())
        x, y, c, chips = _place()
        copies = []
        for k in ks:
            cx, cy = chips[k]
            cp = _forward_copy(buf, send_sems, recv_sems, k, 2 * cx + cy, c, (x, y, 1 - c))
            cp.start()
            copies.append(cp)
        for k in ks:
            cx, cy = chips[k]
            _forward_copy(buf, send_sems, recv_sems, k, 2 * cx + cy, 1 - c, (x, y, c)).wait_recv()
        for cp in copies:
            cp.wait_send()

    return pl.pallas_call(
        body, name=name, in_specs=[HBM], out_specs=HBM, out_shape=jax.ShapeDtypeStruct(land.shape, land.dtype),
        input_output_aliases={0: 0},
        scratch_shapes=[pltpu.SemaphoreType.DMA((3,)), pltpu.SemaphoreType.DMA((3,))],
        compiler_params=pltpu.CompilerParams(collective_id=cid),
    )(land)


def _forward_start(land, *, name, cid, ks=ALL_CHIPS):
    def body(buf_ref, ssem, rsem, buf_out, token):
        _handshake(_sibling())
        x, y, c, chips = _place()
        for k in ks:
            cx, cy = chips[k]
            _forward_copy(buf_ref, ssem, rsem, k, 2 * cx + cy, c, (x, y, 1 - c)).start()
        token[...] = jnp.zeros_like(token)

    res = pl.pallas_call(
        body, name=name, in_specs=[HBM], out_specs=[SEM, SEM, HBM, pl.BlockSpec(memory_space=pltpu.VMEM)],
        out_shape=[pltpu.SemaphoreType.DMA((3,)), pltpu.SemaphoreType.DMA((3,)), _hbm(land.shape, land.dtype),
                   jax.ShapeDtypeStruct(TOKEN_SHAPE, F32)],
        input_output_aliases={0: 2}, compiler_params=_split_params(cid),
    )(land)
    return tuple(res[:3]), res[3]


def _forward_wait(parts, after, *, name, ks=ALL_CHIPS):
    ssem, rsem, buf = parts
    after = list(after) if isinstance(after, (list, tuple)) else [after]

    def body(buf_ref, ssem_ref, rsem_ref, *rest):
        x, y, c, chips = _place()
        for k in ks:
            cx, cy = chips[k]
            _forward_copy(buf_ref, ssem_ref, rsem_ref, k, 2 * cx + cy, c, (x, y, c)).wait_send()
            _forward_copy(buf_ref, ssem_ref, rsem_ref, k, 2 * cx + cy, 1 - c, (x, y, c)).wait_recv()

    return pl.pallas_call(
        body, name=name, in_specs=[HBM, SEM, SEM] + [ANY] * len(after), out_specs=HBM,
        out_shape=_hbm(buf.shape, buf.dtype), input_output_aliases={0: 0},
        compiler_params=pltpu.CompilerParams(has_side_effects=SIDE_EFFECT),
    )(buf, ssem, rsem, *after)


def _share_copy(buf_ref, ssem, rsem, a, which, to):
    part = buf_ref.at[:, _half_cols(buf_ref.shape[1], which)]
    return pltpu.make_async_remote_copy(
        src_ref=part, dst_ref=part, send_sem=ssem.at[a], recv_sem=rsem.at[a], device_id=to, device_id_type=MESH)


def _share_start(arrays, *, name, cid):
    n = len(arrays)

    def body(*refs):
        bufs, ssem, rsem, token = refs[:n], refs[n], refs[n + 1], refs[-1]
        _handshake(_sibling())
        x, y, c, _ = _place()
        for a in range(n):
            _share_copy(bufs[a], ssem, rsem, a, c, (x, y, 1 - c)).start()
        token[...] = jnp.zeros_like(token)

    res = pl.pallas_call(
        body, name=name, in_specs=[HBM] * n,
        out_specs=[SEM, SEM] + [HBM] * n + [pl.BlockSpec(memory_space=pltpu.VMEM)],
        out_shape=[pltpu.SemaphoreType.DMA((n,)), pltpu.SemaphoreType.DMA((n,))]
        + [_hbm(b.shape, b.dtype) for b in arrays] + [jax.ShapeDtypeStruct(TOKEN_SHAPE, F32)],
        input_output_aliases={a: 2 + a for a in range(n)}, compiler_params=_split_params(cid),
    )(*[_in_hbm(b) for b in arrays])
    return (res[0], res[1], list(res[2:2 + n])), res[-1]


def _share_wait(parts, after, *, name):
    ssem, rsem, bufs = parts
    n = len(bufs)
    after = list(after) if isinstance(after, (list, tuple)) else [after]

    def body(*refs):
        buf_refs, ssem_ref, rsem_ref = refs[:n], refs[n], refs[n + 1]
        x, y, c, _ = _place()
        for a in range(n):
            _share_copy(buf_refs[a], ssem_ref, rsem_ref, a, c, (x, y, c)).wait_send()
            _share_copy(buf_refs[a], ssem_ref, rsem_ref, a, 1 - c, (x, y, c)).wait_recv()

    return pl.pallas_call(
        body, name=name, in_specs=[HBM] * n + [SEM, SEM] + [ANY] * len(after), out_specs=[HBM] * n,
        out_shape=[_hbm(b.shape, b.dtype) for b in bufs], input_output_aliases={a: a for a in range(n)},
        compiler_params=pltpu.CompilerParams(has_side_effects=SIDE_EFFECT),
    )(*bufs, ssem, rsem, *after)


def _scatter_copy(src_ref, land_ref, ssem, rsem, k, src_slab, dst_slab, to):
    return pltpu.make_async_remote_copy(
        src_ref=src_ref.at[src_slab], dst_ref=land_ref.at[dst_slab], send_sem=ssem.at[k], recv_sem=rsem.at[k],
        device_id=to, device_id_type=MESH)


def _scatter_start(part, *, name, cid):
    def start(src_ref, land_ref, ssem, rsem):
        x, y, c, chips = _place()
        me = 2 * x + y
        for k, (cx, cy) in enumerate(chips):
            _scatter_copy(src_ref, land_ref, ssem, rsem, k, 2 * cx + cy, me, (cx, cy, c)).start()

    return _split_start(start, _same_core_chips, part, part.shape, N_CHIPS - 1, name=name, cid=cid)


def _scatter_wait(parts, after, *, name):
    def wait(src_ref, land_ref, ssem_ref, rsem_ref):
        x, y, c, chips = _place()
        for k, (cx, cy) in enumerate(chips):
            idx = 2 * cx + cy
            cp = _scatter_copy(src_ref, land_ref, ssem_ref, rsem_ref, k, idx, idx, (x, y, c))
            cp.wait_send()
            cp.wait_recv()

    return _wait_call(wait, parts, after, name=name)


def _split_start(start_fn, peers_fn, src, land_shape, n_sems, *, name, cid):
    def body(src_ref, land_ref, ssem, rsem, src_out, land_out, token):
        _handshake(peers_fn())
        start_fn(src_ref, land_ref, ssem, rsem)
        token[...] = jnp.zeros_like(token)

    res = pl.pallas_call(
        body, name=name, in_specs=[HBM, HBM], out_specs=[SEM, SEM, HBM, HBM, pl.BlockSpec(memory_space=pltpu.VMEM)],
        out_shape=[pltpu.SemaphoreType.DMA((n_sems,)), pltpu.SemaphoreType.DMA((n_sems,)), _hbm(src.shape, src.dtype),
                   _hbm(land_shape, src.dtype), jax.ShapeDtypeStruct(TOKEN_SHAPE, F32)],
        input_output_aliases={0: 2, 1: 3}, compiler_params=_split_params(cid),
    )(_in_hbm(src), _in_hbm(lax.empty(land_shape, src.dtype)))
    return tuple(res[:4]), res[4]


def _sibling_copies(src_ref, land_ref, ssem, rsem, k0, groups, which, to):
    def copy(k, src, dst):
        return pltpu.make_async_remote_copy(
            src_ref=src, dst_ref=dst, send_sem=ssem.at[k], recv_sem=rsem.at[k], device_id=to, device_id_type=MESH)

    if groups == 0:
        return [copy(k0, src_ref, land_ref)]
    hw = src_ref.shape[1] // groups // 2
    return [copy(k0 + j, src_ref.at[:, pl.ds(pl.multiple_of((2 * j + which) * hw, LANES), hw)],
                 land_ref.at[:, j * hw:(j + 1) * hw]) for j in range(groups)]


def _to_sibling_start(items, *, name, cid):
    n = len(items)
    shapes = [a.shape if g == 0 else (a.shape[0], a.shape[1] // 2) for a, g in items]
    first = [sum(max(g, 1) for _, g in items[:k]) for k in range(n + 1)]

    def body(*refs):
        srcs, lands, ssem, rsem, token = refs[:n], refs[n:2 * n], refs[2 * n], refs[2 * n + 1], refs[-1]
        _handshake(_sibling())
        x, y, c, _ = _place()
        for k, (_, g) in enumerate(items):
            for cp in _sibling_copies(srcs[k], lands[k], ssem, rsem, first[k], g, 1 - c, (x, y, 1 - c)):
                cp.start()
        token[...] = jnp.zeros_like(token)

    res = pl.pallas_call(
        body, name=name, in_specs=[HBM] * (2 * n),
        out_specs=[SEM, SEM] + [HBM] * (2 * n) + [pl.BlockSpec(memory_space=pltpu.VMEM)],
        out_shape=[pltpu.SemaphoreType.DMA((first[n],)), pltpu.SemaphoreType.DMA((first[n],))]
        + [_hbm(a.shape, a.dtype) for a, _ in items] + [_hbm(s, a.dtype) for s, (a, _) in zip(shapes, items)]
        + [jax.ShapeDtypeStruct(TOKEN_SHAPE, F32)],
        input_output_aliases={k: 2 + k for k in range(2 * n)}, compiler_params=_split_params(cid),
    )(*[_in_hbm(a) for a, _ in items], *[_in_hbm(lax.empty(s, a.dtype)) for s, (a, _) in zip(shapes, items)])
    return [(res[0], res[1], first[k], g, res[2 + k], res[2 + n + k]) for k, (_, g) in enumerate(items)], res[-1]


def _from_sibling(flight, after, *, name):
    ssem, rsem, k0, groups, src, land = flight

    def wait(src_ref, land_ref, ssem_ref, rsem_ref):
        x, y, c, _ = _place()
        for cp in _sibling_copies(src_ref, land_ref, ssem_ref, rsem_ref, k0, groups, 1 - c, (x, y, c)):
            cp.wait_send()
            cp.wait_recv()

    return _wait_call(wait, (ssem, rsem, src, land), after, name=name)


def _dev_peers(x, y, c, chips):
    return [(x, y, 1 - c)] + [(cx, cy, c) for cx, cy in chips] + [(cx, cy, 1 - c) for cx, cy in chips]


def _dev_gather_start(part, *, name, cid):
    def start(src_ref, land_ref, ssem, rsem):
        x, y, c, chips = _place()
        for k, to in enumerate(_dev_peers(x, y, c, chips)):
            pltpu.make_async_remote_copy(
                src_ref=src_ref, dst_ref=land_ref.at[4 * x + 2 * y + c], send_sem=ssem.at[k], recv_sem=rsem.at[k],
                device_id=to, device_id_type=MESH).start()

    return _split_start(start, lambda: _dev_peers(*_place()), part, (N_DEV,) + part.shape, N_DEV - 1, name=name,
                        cid=cid)


def _dev_gather_wait(parts, after, *, name):
    def wait(src_ref, land_ref, ssem_ref, rsem_ref):
        x, y, c, chips = _place()
        for k, (px, py, pc) in enumerate(_dev_peers(x, y, c, chips)):
            cp = pltpu.make_async_remote_copy(
                src_ref=src_ref, dst_ref=land_ref.at[4 * px + 2 * py + pc], send_sem=ssem_ref.at[k],
                recv_sem=rsem_ref.at[k], device_id=(x, y, c), device_id_type=MESH)
            cp.wait_send()
            cp.wait_recv()

    return _wait_call(wait, parts, after, name=name)[1]


def _sibling_share_halves(arrays, *, name, cid):
    n = len(arrays)

    def body(*refs):
        bufs = refs[n:2 * n]
        send_sems, recv_sems = refs[2 * n:]
        _handshake(_sibling())
        x, y, c, _ = _place()
        copies = []
        for a in range(n):
            mine = bufs[a].at[:, _half_cols(bufs[a].shape[1], c)]
            cp = pltpu.make_async_remote_copy(
                src_ref=mine, dst_ref=mine, send_sem=send_sems.at[a], recv_sem=recv_sems.at[a],
                device_id=(x, y, 1 - c), device_id_type=MESH)
            cp.start()
            copies.append(cp)
        for a in range(n):
            theirs = bufs[a].at[:, _half_cols(bufs[a].shape[1], 1 - c)]
            pltpu.make_async_remote_copy(
                src_ref=theirs, dst_ref=theirs, send_sem=send_sems.at[a], recv_sem=recv_sems.at[a],
                device_id=(x, y, c), device_id_type=MESH).wait_recv()
        for cp in copies:
            cp.wait_send()

    return pl.pallas_call(
        body, name=name, in_specs=[HBM] * n, out_specs=[HBM] * n,
        out_shape=[jax.ShapeDtypeStruct(h.shape, h.dtype) for h in arrays],
        input_output_aliases={a: a for a in range(n)},
        scratch_shapes=[pltpu.SemaphoreType.DMA((n,)), pltpu.SemaphoreType.DMA((n,))],
        compiler_params=pltpu.CompilerParams(collective_id=cid),
    )(*arrays)


def _pack(arrays, rows_multiple=16, width=LANES):
    flat = jnp.concatenate([a.astype(F32).reshape(-1) for a in arrays])
    total = flat.shape[0]
    rows = -(-total // width)
    rows = -(-rows // rows_multiple) * rows_multiple
    return jnp.pad(flat, (0, rows * width - total)).reshape(rows, width)


def _unpack(buf, shapes):
    flat = buf.reshape(-1)
    out, off = [], 0
    for s in shapes:
        n = math.prod(s)
        out.append(flat[off:off + n].reshape(s))
        off += n
    return out


def kernel(x, norm_pre, norm_post, gla_w_in, gla_w_gate2, gla_b_gate, gla_o_gain, gla_w_out, sgu_w_in, sgu_ln_gain, sgu_ln_bias, sgu_w_spatial, sgu_b_spatial, sgu_w_out, loss_target, m_norm_pre, m_norm_post, m_gla_w_in, m_gla_w_gate2, m_gla_b_gate, m_gla_o_gain, m_gla_w_out, m_sgu_w_in, m_sgu_ln_gain, m_sgu_ln_bias, m_sgu_w_spatial, m_sgu_b_spatial, m_sgu_w_out, v_norm_pre, v_norm_post, v_gla_w_in, v_gla_w_gate2, v_gla_b_gate, v_gla_o_gain, v_gla_w_out, v_sgu_w_in, v_sgu_ln_gain, v_sgu_ln_bias, v_sgu_w_spatial, v_sgu_b_spatial, v_sgu_w_out):
    _, t, d = x.shape
    dk = d // 2
    ws = gla_w_in.shape[2]
    wp = -(-ws // LANES) * LANES
    lay = (ws, wp)
    chip =2 * lax.axis_index("x") + lax.axis_index("y")
    core = lax.axis_index("c")
    core_idx = core.astype(jnp.int32).reshape(1)
    others = jnp.arange(N_CHIPS - 1, dtype=jnp.int32)
    others = others + (others >= chip).astype(jnp.int32)
    slots = jnp.concatenate([chip.astype(jnp.int32).reshape(1), others, core_idx])

    x0 = x[0]
    target = loss_target[0]

    wt_in_g, mt_in_g, vt_in_g = gla_w_in[0].T, m_gla_w_in[0].T, v_gla_w_in[0].T

    small_shard = _pack([gla_w_gate2[0], sgu_ln_gain[0], sgu_ln_bias[0]], rows_multiple=8, width=2 * LANES)
    own = [small_shard, jnp.pad(wt_in_g.astype(BF16), ((0, wp - ws), (0, 0)))]
    in_flight, token = _gather_start(own, name="gather_start_a", cid=0, relayed=(1,))

    def with_sibling_and_own(mine, land, name, cid):
        return lax.dynamic_update_slice(_sibling_forward(land, name=name + "_share", cid=cid), mine[None],
                                        (chip, 0, 0))

    h0 = _norm_pre(x0, norm_pre[0:1] + token[0:1, 0:1], name="pre0")
    g_small = with_sibling_and_own(*_gather_wait(in_flight[0], h0, name="w_small_wait"), "w_small", 12)
    mine, land = _gather_wait(in_flight[1], [g_small, wt_in_g, mt_in_g, vt_in_g], name="w_gla_in_wait", ks=NEIGHBOURS)
    relay, token = _relay_start(land, name="w_gla_in_relay", cid=11)
    crossing, token = _forward_start(relay[2], name="w_gla_in_share_near", cid=22, ks=NEIGHBOURS)
    own_later = [(p[0] + token[0, 0]).astype(BF16) for p in (gla_w_out, sgu_w_in, sgu_w_out)]
    in_flight_later, token = _gather_start(own_later, name="gather_start_b", cid=1, after=[token], own_slab=chip)
    in_flight = in_flight + in_flight_later
    land = _relay_wait((relay[0], relay[1], crossing[2]), token, name="w_gla_in_relay_wait")
    land = _forward_wait((crossing[0], crossing[1], land), token, name="w_gla_in_share_near_wait", ks=NEIGHBOURS)
    land = _sibling_forward(land, name="w_gla_in_share_far", cid=13, ks=(2,))
    wt_g = lax.dynamic_update_slice(land, mine[None], (chip, 0, 0)).reshape(N_CHIPS * wp, d)

    def behind(small, token):
        return small + token[0:1, 0:1]

    def arriving(i, after, name):
        mine, land = _gather_wait(in_flight[i], after, name=name + "_wait")
        crossing, token = _forward_start(land, name=name + "_share", cid=i)
        return (mine, crossing), token

    def arrived(pending, after, name):
        _, crossing = pending
        return _forward_wait(crossing, after, name=name + "_share_wait")

    shard_shapes = [gla_w_gate2.shape[1:], sgu_ln_gain.shape[1:], sgu_ln_bias.shape[1:]]
    per_chip = [_unpack(g_small[j], shard_shapes) for j in range(N_CHIPS)]
    w2_full = jnp.concatenate([p[0] for p in per_chip], axis=1)
    ln_gain = jnp.concatenate([p[1] for p in per_chip], axis=0)[None, :]
    ln_bias = jnp.concatenate([p[2] for p in per_chip], axis=0)[None, :]
    w2p = jnp.pad(w2_full, ((0, LANES - GLA_GATE_RANK), (0, 0)))

    pos_chunk = jnp.arange(SGU_BLOCK) // CHUNK
    mask = pos_chunk[:, None] >= pos_chunk[None, :]
    ws_masked = jnp.where(mask[None], sgu_w_spatial[0], 0.0)
    ws_masked_t = ws_masked.transpose(0, 2, 1)
    bs_t = sgu_b_spatial[0].T

    proj0 = _matmul(h0, wt_g, mode="nt", out_dtype=F32, name="gla_in", tn=wp)
    pending, tok = arriving(2, proj0, "w_gla_out")
    o0, a0, s_before, s_final = _gla_fwd(proj0, w2p, behind(gla_b_gate, tok), gla_o_gain, lay, name="gla_scan")
    w_out_g = arrived(pending, a0, "w_gla_out").reshape(d, d)
    y0 = _matmul(a0, w_out_g, mode="nn", out_dtype=F32, name="gla_out", tn=1024)
    pending, tok = arriving(3, y0, "w_sgu_in")
    x1, h1 = _post_then_pre(x0, y0, behind(norm_post[0:1], tok), norm_pre[1:2], name="post0_pre1")
    g_wi_s = arrived(pending, h1, "w_sgu_in")
    pending, tok = arriving(4, g_wi_s, "w_sgu_out")
    proj1 = _matmul(h1, g_wi_s, mode="nn", out_dtype=F32, name="sgu_in", b_shards=True, after=tok, tn=768)
    a1 = _sgu_fwd(proj1, ln_gain, ln_bias, ws_masked, bs_t, name="sgu_gate")
    w_out_s = arrived(pending, a1, "w_sgu_out").reshape(d, d)
    acts, tok = _to_sibling_start([(a1, 0), (a0, 0), (h1, 0), (h0, 1)], name="acts_to_sibling", cid=5)
    a1, a0, h1, h0 = [f[4] for f in acts]
    y1 = _matmul(a1, w_out_s, mode="nn", out_dtype=F32, name="sgu_out", after=tok, tn=1024)
    loss_part, dx2, dy1, d_post1 = _loss_head(x1, y1, norm_post[1:2], target, name="loss_head")

    def pair_gradient(a_sent, b_sent, after, shards_on, name, cid):
        a_me, a_sib = _from_sibling(a_sent, after, name=name + "_a_wait")
        b_me, b_sib = _from_sibling(b_sent, [a_sib] + list(after), name=name + "_b_wait")
        pair = _matmul_dw_pair(a_me, a_sib, b_me, b_sib, core_idx, shards_on=shards_on,
                               name=name + "_pair")
        return _scatter_start(pair, name=name + "_start", cid=cid)

    def reduced(flight, after, name):
        pair, landed = _scatter_wait(flight, after, name=name + "_wait")
        return _chip_sum(pair, landed, slots, name=name + "_sum")

    (dy1_sent,), tok = _to_sibling_start([(dy1, 1)], name="dy1_to_sibling", cid=6)
    dy1 = dy1_sent[4]
    da1 = _matmul(dy1, w_out_s, mode="nt", out_dtype=F32, name="d_sgu_act", after=tok, tn=1024)
    fl_wo_s, tok = pair_gradient(acts[0], dy1_sent, [da1], "rows", "g_sgu_out", 15)
    dproj1, d_ws, d_bs_t, d_lg, d_lb = _sgu_bwd(da1, proj1, ln_gain, behind(ln_bias, tok), ws_masked, ws_masked_t,
                                                bs_t, name="sgu_gate_bwd")
    (dp1_sent,), tok = _to_sibling_start([(dproj1, N_CHIPS)], name="dproj1_to_sibling", cid=7)
    dproj1 = dp1_sent[4]
    dh1 = _matmul_nt_shards(dproj1, g_wi_s, out_dtype=F32, name="d_sgu_h", after=tok)
    fl_wi_s, tok = pair_gradient(acts[2], dp1_sent, [dh1], "cols", "g_sgu_in", 16)
    dx1, dy0, d_pre1, d_post0 = _mid_bwd(dx2, dh1, x1, behind(norm_pre[1:2], tok), y0, norm_post[0:1],
                                         name="pre1_post0_bwd")
    (dy0_sent,), tok = _to_sibling_start([(dy0, 1)], name="dy0_to_sibling", cid=8)
    dy0 = dy0_sent[4]
    da0 = _matmul(dy0, w_out_g, mode="nt", out_dtype=F32, name="d_gla_act", after=tok, tn=1024)
    fl_wo_g, tok = pair_gradient(acts[1], dy0_sent, [da0], "rows", "g_gla_out", 17)
    dproj0, d_og, d_bg, d_w2p = _gla_bwd(da0, o0, proj0, w2p, behind(gla_b_gate, tok), gla_o_gain, s_before, s_final,
                                         lay, name="gla_scan_bwd")
    early_shapes = [norm_post.shape, gla_b_gate.shape, gla_o_gain.shape, sgu_w_spatial.shape, sgu_b_spatial.shape,
                    (1, GLA_GATE_RANK, dk), (1, d), (1, d), (1, LANES)]
    early_part = _pack([jnp.concatenate([d_post0, d_post1], axis=0), d_bg, d_og, jnp.where(mask[None], d_ws, 0.0)[None],
                        d_bs_t.T[None], d_w2p[:GLA_GATE_RANK][None], d_lg, d_lb, loss_part])
    early_flight, tok = _dev_gather_start(early_part, name="small_early_start", cid=20)
    (dp0_sent,), tok_sent = _to_sibling_start([(dproj0, 0)], name="dproj0_to_sibling", cid=9)
    dproj0 = dp0_sent[4]
    dh0 = _matmul(dproj0, wt_g, mode="nn", out_dtype=F32, name="d_gla_h", after=tok_sent)
    a_me, a_sib = _from_sibling(dp0_sent, [dh0, tok], name="g_gla_in_a_wait")
    b_me, b_sib = _from_sibling(acts[3], [a_sib, dh0], name="g_gla_in_b_wait")
    fl_wi_g, tok_scatter = [], None
    for p in range(2):
        pair = _matmul_dw_pair(a_me, a_sib, b_me, b_sib, core_idx, shards_on="rows", part=(p, 2),
                               name=f"g_gla_in_pair{p}", after=tok_scatter)
        flight, tok_scatter = _scatter_start(pair, name=f"g_gla_in_start{p}", cid=18 + p)
        fl_wi_g.append(flight)
    r_wo_s = reduced(fl_wo_s, tok_scatter, "g_sgu_out")
    r_wi_s = reduced(fl_wi_s, r_wo_s, "g_sgu_in")
    r_wo_g = reduced(fl_wo_g, r_wi_s, "g_gla_out")
    sharing, tok = _share_start([r_wo_s, r_wi_s, r_wo_g], name="grads_share_a", cid=10)
    grad_x, d_pre0 = _first_bwd(dx1, dh0, x0, behind(norm_pre[0:1], tok), name="pre0_bwd")

    late_part = _pack([jnp.concatenate([d_pre0, d_pre1], axis=0)])
    late_flight, tok = _dev_gather_start(late_part, name="small_late_start", cid=21)

    def big_update(w, g, m, v, name, after=None):
        return [u[None] for u in _adamw(w[0], g, m[0], v[0], name=name, after=after)]

    g_wo_sgu, g_wi_sgu, g_wo_gla = _share_wait(sharing, [grad_x, tok], name="grads_share_a_wait")
    u_wi_sgu = big_update(sgu_w_in, g_wi_sgu, m_sgu_w_in, v_sgu_w_in, "adamw_sgu_w_in")
    u_wo_gla = big_update(gla_w_out, g_wo_gla, m_gla_w_out, v_gla_w_out, "adamw_gla_w_out", after=u_wi_sgu[1])

    r_wi_g, behind_this = None, u_wo_gla[1]
    for p, flight in enumerate(fl_wi_g):
        pair, landed = _scatter_wait(flight, behind_this, name=f"g_gla_in_wait{p}")
        r_wi_g = behind_this = _chip_sum(pair, landed, slots, part=(p, 2), into=r_wi_g, name=f"g_gla_in_sum{p}")
    gt_wi_gla, = _sibling_share_halves([r_wi_g], name="grads_share_b", cid=14)
    u_wi_gla_t = _adamw_in_two(wt_in_g, gt_wi_gla, mt_in_g, vt_in_g, name="adamw_gla_w_in")
    u_wi_gla = [u.T[None] for u in u_wi_gla_t]
    u_wo_sgu = big_update(sgu_w_out, g_wo_sgu, m_sgu_w_out, v_sgu_w_out, "adamw_sgu_w_out", after=u_wi_gla_t[1])

    def summed_over_devices(part, flight, after, shapes, name):
        land = _dev_gather_wait(flight, after, name=name + "_wait")
        every = lax.dynamic_update_slice(land, part[None], (2 * chip + core, 0, 0))
        return _unpack(_stack_sum(every, name=name + "_sum"), shapes)

    (g_post, g_bg, g_og, g_wsp, g_bsp, g_w2_full, g_lg_full, g_lb_full, loss_vec) = summed_over_devices(
        early_part, early_flight, u_wo_sgu[1], early_shapes, "small_early")
    g_pre, = summed_over_devices(late_part, late_flight, loss_vec, [norm_pre.shape], "small_late")
    loss = loss_vec[0, 0]
    g_w2 = lax.dynamic_slice_in_dim(g_w2_full, chip * (dk // N_CHIPS), dk // N_CHIPS, axis=2)
    g_lg = lax.dynamic_slice_in_dim(g_lg_full, chip * (d // N_CHIPS), d // N_CHIPS, axis=1)
    g_lb = lax.dynamic_slice_in_dim(g_lb_full, chip * (d // N_CHIPS), d // N_CHIPS, axis=1)

    small_w = [norm_pre, norm_post, gla_b_gate, gla_o_gain, sgu_w_spatial, sgu_b_spatial, gla_w_gate2, sgu_ln_gain,
               sgu_ln_bias]
    small_g = [g_pre, g_post, g_bg, g_og, g_wsp, g_bsp, g_w2, g_lg, g_lb]
    small_m = [m_norm_pre, m_norm_post, m_gla_b_gate, m_gla_o_gain, m_sgu_w_spatial, m_sgu_b_spatial, m_gla_w_gate2,
               m_sgu_ln_gain, m_sgu_ln_bias]
    small_v = [v_norm_pre, v_norm_post, v_gla_b_gate, v_gla_o_gain, v_sgu_w_spatial, v_sgu_b_spatial, v_gla_w_gate2,
               v_sgu_ln_gain, v_sgu_ln_bias]
    own_shapes = [w.shape for w in small_w]
    _, s_dl, s_m, s_v = _adamw(_pack(small_w), _pack(small_g), _pack(small_m), _pack(small_v), name="adamw_small")
    dl_s, m_s, v_s = _unpack(s_dl, own_shapes), _unpack(s_m, own_shapes), _unpack(s_v, own_shapes)

    def ordered(small, kind):
        pre, post, bg, og, wsp, bsp, w2, lg, lb = small
        return [pre, post, u_wi_gla[kind], w2, bg, og, u_wo_gla[kind], u_wi_sgu[kind], lg, lb, wsp, bsp, u_wo_sgu[kind]]

    return (loss, grad_x[None], *ordered(small_g, 0), *ordered(dl_s, 1), *ordered(m_s, 2), *ordered(v_s, 3))
```

```python
import math

import jax
import jax.numpy as jnp
from jax import lax
from jax.experimental import pallas as pl
from jax.experimental.pallas import tpu as pltpu

F32 = jnp.float32
BF16 = jnp.bfloat16
MESH = pl.DeviceIdType.MESH

EPS = 1e-6
CHUNK = 64
GLA_HEADS = 4
GLA_GATE_RANK = 16
GLA_TAU = 16.0
SGU_BLOCK = 128
SGU_GROUPS = 8
N_CHIPS = 4
N_DEV = 8
LANES = 128

ADAM_LR = 0.001
ADAM_B1 = 0.9
ADAM_B2 = 0.999
ADAM_EPS = 1e-08
ADAM_WD = 0.01
ADAM_STEP = 10

VMEM_LIMIT = 56 * 1024 * 1024
ELEMENTWISE_BLOCK_BYTES = 2 << 20


def _cparams(sem=None):
    return pltpu.CompilerParams(dimension_semantics=sem, vmem_limit_bytes=VMEM_LIMIT)


def _pick(n, cap, unit=LANES):
    best = None
    for t in range(unit, min(n, cap) + 1, unit):
        if n % t == 0:
            best = t
    assert best is not None, (n, cap, unit)
    return best


def _dot(a, b, dims):
    return lax.dot_general(a, b, (dims, ((), ())), preferred_element_type=F32)


def _dot_nn(a, b):
    return _dot(a, b, ((1,), (0,)))


def _dot_nt(a, b):
    return _dot(a, b, ((1,), (1,)))


def _dot_tn(a, b):
    return _dot(a, b, ((0,), (0,)))


def _matmul(a, b, *, mode, out_dtype, name, tm=1024, tn=512, b_shards=False, after=None):
    M, K = a.shape
    if b_shards:
        ns, Kb, bc = b.shape
        N, tn = ns * bc, _pick(bc, tn)
        per = bc // tn
        b_spec = pl.BlockSpec((None, K, tn), lambda i, j: (j // per, 0, j % per))
    elif mode == "nt":
        N, Kb = b.shape
        tn = _pick(N, tn)
        b_spec = pl.BlockSpec((tn, K), lambda i, j: (j, 0))
    else:
        Kb, N = b.shape
        tn = _pick(N, tn)
        b_spec = pl.BlockSpec((K, tn), lambda i, j: (0, j))
    assert K == Kb and a.dtype == b.dtype == BF16, (a.shape, b.shape, mode)
    tm = _pick(M, tm)
    dims = ((1,), (1,)) if mode == "nt" else ((1,), (0,))
    extra_specs, extra_args = ([], []) if after is None else ([pl.BlockSpec(memory_space=pl.ANY)], [after])

    def body(a_ref, b_ref, *rest):
        rest[-1][...] = _dot(a_ref[...], b_ref[...], dims).astype(out_dtype)

    return pl.pallas_call(
        body, name=name, grid=(M // tm, N // tn),
        in_specs=[pl.BlockSpec((tm, K), lambda i, j: (i, 0)), b_spec] + extra_specs,
        out_specs=pl.BlockSpec((tm, tn), lambda i, j: (i, j)), out_shape=jax.ShapeDtypeStruct((M, N), out_dtype),
        compiler_params=_cparams(("parallel", "parallel")),
    )(a, b, *extra_args)


def _matmul_nt_shards(a, b, *, out_dtype, name, tm=1024, tn=512, after=None):
    M, K = a.shape
    ns, N, kc = b.shape
    assert K == ns * kc
    tm, tn = _pick(M, tm), _pick(N, tn)

    def body(a_ref, *rest):
        b_refs, o_ref = rest[:ns], rest[ns + (after is not None)]
        acc = _dot_nt(a_ref[:, 0:kc], b_refs[0][...])
        for j in range(1, ns):
            acc += _dot_nt(a_ref[:, j * kc:(j + 1) * kc], b_refs[j][...])
        o_ref[...] = acc.astype(out_dtype)

    def shard(j):
        return pl.BlockSpec((None, tn, kc), lambda i, n: (j, n, 0))

    extra_specs, extra_args = ([], []) if after is None else ([pl.BlockSpec(memory_space=pl.ANY)], [after])
    return pl.pallas_call(
        body, name=name, grid=(M // tm, N // tn),
        in_specs=[pl.BlockSpec((tm, K), lambda i, n: (i, 0))] + [shard(j) for j in range(ns)] + extra_specs,
        out_specs=pl.BlockSpec((tm, tn), lambda i, n: (i, n)), out_shape=jax.ShapeDtypeStruct((M, N), out_dtype),
        compiler_params=_cparams(("parallel", "parallel")),
    )(a, *([b] * ns), *extra_args)


def _rstd(x):
    return lax.rsqrt(jnp.mean(x * x, axis=-1, keepdims=True) + EPS)


def _row_spec(tr, d):
    return pl.BlockSpec((tr, d), lambda i: (i, 0))


def _vec_spec(d):
    return pl.BlockSpec((1, d), lambda i: (0, 0))


def _acc_rows(ref, i, val, cols=slice(None)):
    @pl.when(i == 0)
    def _():
        ref[:, cols] = val

    @pl.when(i > 0)
    def _():
        ref[:, cols] += val


def _norm_pre(x, gain, *, name, tr=256):
    t, d = x.shape
    tr = _pick(t, tr, 8)

    def body(x_ref, g_ref, h_ref):
        xv = x_ref[...]
        h_ref[...] = (xv * _rstd(xv) * g_ref[...]).astype(BF16)

    return pl.pallas_call(
        body, name=name, grid=(t // tr,), in_specs=[_row_spec(tr, d), _vec_spec(d)], out_specs=_row_spec(tr, d),
        out_shape=jax.ShapeDtypeStruct((t, d), BF16), compiler_params=_cparams(("parallel",)),
    )(x, gain)


def _post_then_pre(x, y, post_gain, pre_gain, *, name, tr=256):
    t, d = x.shape
    tr = _pick(t, tr, 8)

    def body(x_ref, y_ref, pg_ref, ng_ref, xn_ref, h_ref):
        yv = y_ref[...]
        xn = x_ref[...] + yv * _rstd(yv) * pg_ref[...]
        xn_ref[...] = xn
        h_ref[...] = (xn * _rstd(xn) * ng_ref[...]).astype(BF16)

    return pl.pallas_call(
        body, name=name, grid=(t // tr,),
        in_specs=[_row_spec(tr, d), _row_spec(tr, d), _vec_spec(d), _vec_spec(d)],
        out_specs=[_row_spec(tr, d), _row_spec(tr, d)],
        out_shape=[jax.ShapeDtypeStruct((t, d), F32), jax.ShapeDtypeStruct((t, d), BF16)],
        compiler_params=_cparams(("parallel",)),
    )(x, y, post_gain, pre_gain)


def _norm_bwd(dy, n, r, gain):
    dn = dy * gain
    return r * (dn - n * jnp.mean(dn * n, axis=-1, keepdims=True))


def _loss_head(x, y, post_gain, target, *, name, tr=256):
    t, d = x.shape
    tr = _pick(t, tr, 8)

    def body(x_ref, y_ref, pg_ref, t_ref, loss_ref, dx_ref, dy_ref, dpg_ref):
        i = pl.program_id(0)
        yv = y_ref[...]
        r = _rstd(yv)
        n = yv * r
        err = x_ref[...] + n * pg_ref[...] - t_ref[...]
        dx = err * (1.0 / d)
        dx_ref[...] = dx
        part = 0.5 * jnp.sum(jnp.mean(err * err, axis=-1, keepdims=True), axis=0, keepdims=True)
        _acc_rows(loss_ref, i, jnp.broadcast_to(part, (1, LANES)))
        _acc_rows(dpg_ref, i, jnp.sum(dx * n, axis=0, keepdims=True))
        dy_ref[...] = _norm_bwd(dx, n, r, pg_ref[...]).astype(BF16)

    return pl.pallas_call(
        body, name=name, grid=(t // tr,),
        in_specs=[_row_spec(tr, d), _row_spec(tr, d), _vec_spec(d), _row_spec(tr, d)],
        out_specs=[_vec_spec(LANES), _row_spec(tr, d), _row_spec(tr, d), _vec_spec(d)],
        out_shape=[jax.ShapeDtypeStruct((1, LANES), F32), jax.ShapeDtypeStruct((t, d), F32),
                   jax.ShapeDtypeStruct((t, d), BF16), jax.ShapeDtypeStruct((1, d), F32)],
        compiler_params=_cparams(("arbitrary",)),
    )(x, y, post_gain, target)


def _mid_bwd(dx_out, dh, x, pre_gain, y_prev, post_gain_prev, *, name, tr=256):
    t, d = x.shape
    tr = _pick(t, tr, 8)

    def body(dxo_ref, dh_ref, x_ref, ng_ref, y_ref, pg_ref, dx_ref, dy_ref, dng_ref, dpg_ref):
        i = pl.program_id(0)
        xv = x_ref[...]
        r = _rstd(xv)
        xh = xv * r
        dhv = dh_ref[...]
        _acc_rows(dng_ref, i, jnp.sum(dhv * xh, axis=0, keepdims=True))
        dx = dxo_ref[...] + _norm_bwd(dhv, xh, r, ng_ref[...])
        dx_ref[...] = dx
        yv = y_ref[...]
        ry = _rstd(yv)
        n = yv * ry
        _acc_rows(dpg_ref, i, jnp.sum(dx * n, axis=0, keepdims=True))
        dy_ref[...] = _norm_bwd(dx, n, ry, pg_ref[...]).astype(BF16)

    return pl.pallas_call(
        body, name=name, grid=(t // tr,),
        in_specs=[_row_spec(tr, d), _row_spec(tr, d), _row_spec(tr, d), _vec_spec(d), _row_spec(tr, d), _vec_spec(d)],
        out_specs=[_row_spec(tr, d), _row_spec(tr, d), _vec_spec(d), _vec_spec(d)],
        out_shape=[jax.ShapeDtypeStruct((t, d), F32), jax.ShapeDtypeStruct((t, d), BF16),
                   jax.ShapeDtypeStruct((1, d), F32), jax.ShapeDtypeStruct((1, d), F32)],
        compiler_params=_cparams(("arbitrary",)),
    )(dx_out, dh, x, pre_gain, y_prev, post_gain_prev)


def _first_bwd(dx_out, dh, x, pre_gain, *, name, tr=256):
    t, d = x.shape
    tr = _pick(t, tr, 8)

    def body(dxo_ref, dh_ref, x_ref, ng_ref, dx_ref, dng_ref):
        i = pl.program_id(0)
        xv = x_ref[...]
        r = _rstd(xv)
        xh = xv * r
        dhv = dh_ref[...]
        _acc_rows(dng_ref, i, jnp.sum(dhv * xh, axis=0, keepdims=True))
        dx_ref[...] = dxo_ref[...] + _norm_bwd(dhv, xh, r, ng_ref[...])

    return pl.pallas_call(
        body, name=name, grid=(t // tr,),
        in_specs=[_row_spec(tr, d), _row_spec(tr, d), _row_spec(tr, d), _vec_spec(d)],
        out_specs=[_row_spec(tr, d), _vec_spec(d)],
        out_shape=[jax.ShapeDtypeStruct((t, d), F32), jax.ShapeDtypeStruct((1, d), F32)],
        compiler_params=_cparams(("arbitrary",)),
    )(dx_out, dh, x, pre_gain)


def _sigmoid(x):
    return 1.0 / (1.0 + jnp.exp(-x))


def _log_sigmoid(x):
    return jnp.minimum(x, 0.0) - jnp.log(1.0 + jnp.exp(-jnp.abs(x)))


_GELU_C = math.sqrt(2.0 / math.pi)


_GELU_A = 0.044715


def _gelu_parts(x, with_grad=True):
    x2 = x * x
    h = 0.5 * jnp.tanh(x * (_GELU_C + (_GELU_C * _GELU_A) * x2)) + 0.5
    val = x * h
    if not with_grad:
        return val, None
    return val, h * (1.0 + (1.0 - h) * (x * (2.0 * _GELU_C + (6.0 * _GELU_C * _GELU_A) * x2)))


def _split3(x):
    hi = x.astype(BF16)
    r1 = x - hi.astype(F32)
    mid = r1.astype(BF16)
    lo = (r1 - mid.astype(F32)).astype(BF16)
    return hi, mid, lo


def _tri_matmul(tri_bf16, x):
    hi, mid, lo = _split3(x)
    return _dot_nn(tri_bf16, hi) + _dot_nn(tri_bf16, mid) + _dot_nn(tri_bf16, lo)


def _gla_dims(d):
    dk, dv = d // 2, d
    return dk, dv, dk // GLA_HEADS, dv // GLA_HEADS


def _col_pieces(a, b, lay):
    ws, wp = lay
    out = []
    while a < b:
        j = a // ws
        end = min(b, (j + 1) * ws)
        out.append((j * wp + a - j * ws, end - a))
        a = end
    return out


def _load_cols(ref, a, b, lay):
    parts = [ref[:, s:s + n] for s, n in _col_pieces(a, b, lay)]
    return parts[0] if len(parts) == 1 else jnp.concatenate(parts, axis=1)


def _store_cols(ref, a, val, lay):
    off = 0
    for s, n in _col_pieces(a, a + val.shape[1], lay):
        ref[:, s:s + n] = val[:, off:off + n]
        off += n


def _gate_window(c_r, lay):
    (start, _), = _col_pieces(c_r, c_r + GLA_GATE_RANK, lay)
    assert (start % lay[1]) + LANES <= lay[1]
    return slice(start, start + LANES)


def _gla_gates(glr, k, w2_ref, b_ref):
    z = _dot_nn(glr.astype(BF16), w2_ref[...].astype(BF16)) + b_ref[...]
    la = _log_sigmoid(z) * (1.0 / GLA_TAU)
    row = lax.broadcasted_iota(jnp.int32, (CHUNK, CHUNK), 0)
    col = lax.broadcasted_iota(jnp.int32, (CHUNK, CHUNK), 1)
    incl = (row >= col).astype(BF16)
    bcum = _tri_matmul(incl, la)
    b_end = bcum[CHUNK - 1:CHUNK, :]
    e_rest = jnp.exp(b_end - bcum)
    return z, e_rest, k * e_rest, jnp.exp(b_end)


def _gla_fwd(proj, w2p, b_gate, o_gain, lay, *, name):
    t, wcols = proj.shape
    d = o_gain.shape[1]
    dk, dv, dkh, dvh = _gla_dims(d)
    nc = t // CHUNK
    c_k, c_v, c_g, c_r = dk, 2 * dk, 2 * dk + dv, 2 * dk + 2 * dv
    scale = dkh ** -0.5

    def body(p_ref, w2_ref, b_ref, og_ref, o_ref, a_ref, sb_ref, sfin_ref, s_ref):
        i = pl.program_id(0)

        @pl.when(i == 0)
        def _():
            s_ref[...] = jnp.zeros_like(s_ref)

        q = _load_cols(p_ref, 0, dk, lay) * scale
        k = _load_cols(p_ref, c_k, c_k + dk, lay)
        glr = p_ref[:, _gate_window(c_r, lay)]
        _, _, kdec, decay = _gla_gates(glr, k, w2_ref, b_ref)
        for h in range(GLA_HEADS):
            ks = slice(h * dkh, (h + 1) * dkh)
            vs = slice(h * dvh, (h + 1) * dvh)
            v_h = _load_cols(p_ref, c_v + h * dvh, c_v + (h + 1) * dvh, lay)
            g_h = _load_cols(p_ref, c_g + h * dvh, c_g + (h + 1) * dvh, lay)
            s_old = s_ref[h]
            sb_ref[0, h] = s_old
            s_new = s_old * decay[:, ks] + _dot_tn(v_h.astype(BF16), kdec[:, ks].astype(BF16))
            s_ref[h] = s_new
            o_h = _dot_nt(q[:, ks].astype(BF16), s_new.astype(BF16))
            o_ref[:, vs] = o_h
            on = o_h * _rstd(o_h)
            a_ref[:, vs] = (on * og_ref[:, vs] * (g_h * _sigmoid(g_h))).astype(BF16)

        @pl.when(i == nc - 1)
        def _():
            sfin_ref[...] = s_ref[...]

    full = lambda *shape: pl.BlockSpec(shape, lambda i: (0,) * len(shape))
    return pl.pallas_call(
        body, name=name, grid=(nc,),
        in_specs=[pl.BlockSpec((CHUNK, wcols), lambda i: (i, 0)), full(LANES, dk), full(1, dk), full(1, dv)],
        out_specs=[pl.BlockSpec((CHUNK, dv), lambda i: (i, 0)), pl.BlockSpec((CHUNK, dv), lambda i: (i, 0)),
                   pl.BlockSpec((1, GLA_HEADS, dvh, dkh), lambda i: (i, 0, 0, 0)), full(GLA_HEADS, dvh, dkh)],
        out_shape=[jax.ShapeDtypeStruct((t, dv), F32), jax.ShapeDtypeStruct((t, dv), BF16),
                   jax.ShapeDtypeStruct((nc, GLA_HEADS, dvh, dkh), F32),
                   jax.ShapeDtypeStruct((GLA_HEADS, dvh, dkh), F32)],
        scratch_shapes=[pltpu.VMEM((GLA_HEADS, dvh, dkh), F32)],
        compiler_params=_cparams(("arbitrary",)),
    )(proj, w2p, b_gate, o_gain)


def _gla_bwd(da, o, proj, w2p, b_gate, o_gain, s_before, s_final, lay, *, name):
    t, wcols = proj.shape
    d = o_gain.shape[1]
    dk, dv, dkh, dvh = _gla_dims(d)
    nc = t // CHUNK
    c_k, c_v, c_g, c_r = dk, 2 * dk, 2 * dk + dv, 2 * dk + 2 * dv
    scale = dkh ** -0.5

    def body(da_ref, o_ref, p_ref, w2_ref, b_ref, og_ref, sb_ref, sfin_ref,
             dp_ref, dog_ref, db_ref, dw2_ref, s_ref, gc_ref, dkd_ref):
        i = pl.program_id(0)

        @pl.when(i == 0)
        def _():
            s_ref[...] = sfin_ref[...]
            gc_ref[...] = jnp.zeros_like(gc_ref)

        ws, wp = lay
        for j in range(N_CHIPS):
            dp_ref[:, j * wp + ws:(j + 1) * wp] = jnp.zeros((CHUNK, wp - ws), BF16)
        q = _load_cols(p_ref, 0, dk, lay) * scale
        k = _load_cols(p_ref, c_k, c_k + dk, lay)
        glr = p_ref[:, _gate_window(c_r, lay)]
        z, e_rest, kdec, decay = _gla_gates(glr, k, w2_ref, b_ref)
        ddecay = []
        for h in range(GLA_HEADS):
            ks = slice(h * dkh, (h + 1) * dkh)
            vs = slice(h * dvh, (h + 1) * dvh)
            v_h = _load_cols(p_ref, c_v + h * dvh, c_v + (h + 1) * dvh, lay)
            g_h = _load_cols(p_ref, c_g + h * dvh, c_g + (h + 1) * dvh, lay)
            da_h = da_ref[:, vs]
            o_h = o_ref[:, vs]
            og_h = og_ref[:, vs]
            r = _rstd(o_h)
            on = o_h * r
            sg = _sigmoid(g_h)
            silu = g_h * sg
            _acc_rows(dog_ref, i, jnp.sum(da_h * silu * on, axis=0, keepdims=True), vs)
            _store_cols(dp_ref, c_g + h * dvh, (da_h * (on * og_h) * (sg * (1.0 + g_h * (1.0 - sg)))).astype(BF16),
                        lay)
            don = da_h * silu * og_h
            do_h = (r * (don - on * jnp.mean(don * on, axis=-1, keepdims=True))).astype(BF16)
            s_cur = s_ref[h]
            _store_cols(dp_ref, h * dkh, (_dot_nn(do_h, s_cur.astype(BF16)) * scale).astype(BF16), lay)
            g_tot = gc_ref[h] + _dot_tn(do_h, q[:, ks].astype(BF16))
            g_bf = g_tot.astype(BF16)
            dkd_ref[:, ks] = _dot_nn(v_h.astype(BF16), g_bf)
            _store_cols(dp_ref, c_v + h * dvh, _dot_nt(kdec[:, ks].astype(BF16), g_bf).astype(BF16), lay)
            s_prev = sb_ref[0, h]
            ddecay.append(jnp.sum(g_tot * s_prev, axis=0, keepdims=True))
            gc_ref[h] = g_tot * decay[:, ks]
            s_ref[h] = s_prev
        dkdec = dkd_ref[...]
        _store_cols(dp_ref, c_k, (dkdec * e_rest).astype(BF16), lay)
        d_e = dkdec * kdec
        row = lax.broadcasted_iota(jnp.int32, (CHUNK, CHUNK), 0)
        col = lax.broadcasted_iota(jnp.int32, (CHUNK, CHUNK), 1)
        excl = (row > col).astype(BF16)
        dla = jnp.concatenate(ddecay, axis=1) * decay + _tri_matmul(excl, d_e)
        dz = dla * (1.0 / GLA_TAU) * (1.0 - _sigmoid(z))
        _acc_rows(db_ref, i, jnp.sum(dz, axis=0, keepdims=True))
        dz_bf = dz.astype(BF16)
        dw2 = _dot_tn(glr.astype(BF16), dz_bf)

        @pl.when(i == 0)
        def _():
            dw2_ref[...] = dw2

        @pl.when(i > 0)
        def _():
            dw2_ref[...] += dw2

        dp_ref[:, _gate_window(c_r, lay)] = _dot_nt(dz_bf, w2_ref[...].astype(BF16)).astype(BF16)

    rev = lambda i: (nc - 1 - i, 0)
    full = lambda *shape: pl.BlockSpec(shape, lambda i: (0,) * len(shape))
    return pl.pallas_call(
        body, name=name, grid=(nc,),
        in_specs=[pl.BlockSpec((CHUNK, dv), rev), pl.BlockSpec((CHUNK, dv), rev), pl.BlockSpec((CHUNK, wcols), rev),
                  full(LANES, dk), full(1, dk), full(1, dv),
                  pl.BlockSpec((1, GLA_HEADS, dvh, dkh), lambda i: (nc - 1 - i, 0, 0, 0)), full(GLA_HEADS, dvh, dkh)],
        out_specs=[pl.BlockSpec((CHUNK, wcols), rev), full(1, dv), full(1, dk), full(LANES, dk)],
        out_shape=[jax.ShapeDtypeStruct((t, wcols), BF16), jax.ShapeDtypeStruct((1, dv), F32),
                   jax.ShapeDtypeStruct((1, dk), F32), jax.ShapeDtypeStruct((LANES, dk), F32)],
        scratch_shapes=[pltpu.VMEM((GLA_HEADS, dvh, dkh), F32), pltpu.VMEM((GLA_HEADS, dvh, dkh), F32),
                        pltpu.VMEM((CHUNK, dk), F32)],
        compiler_params=_cparams(("arbitrary",)),
    )(da, o, proj, w2p, b_gate, o_gain, s_before, s_final)


def _sgu_mid(p_ref, lg_ref, lb_ref, ws_ref, bst_ref, w, with_grad=True):
    gd = w // SGU_GROUPS
    u_act, du_fac = _gelu_parts(p_ref[:, 0:w], with_grad)
    vf, dv_fac = _gelu_parts(p_ref[:, w:2 * w], with_grad)
    mu = jnp.mean(vf, axis=-1, keepdims=True)
    cen = vf - mu
    rstd = lax.rsqrt(jnp.mean(cen * cen, axis=-1, keepdims=True) + EPS)
    xh = cen * rstd
    vn = (xh * lg_ref[...] + lb_ref[...]).astype(BF16)
    vs = [_dot_nn(ws_ref[g].astype(BF16), vn[:, g * gd:(g + 1) * gd]) + bst_ref[:, g:g + 1]
          for g in range(SGU_GROUPS)]
    return u_act, du_fac, dv_fac, rstd, xh, vn, vs


def _sgu_fwd(proj, ln_gain, ln_bias, ws_masked, bs_t, *, name):
    t, w3 = proj.shape
    w = w3 // 3
    gd = w // SGU_GROUPS
    nb = t // SGU_BLOCK

    def body(p_ref, lg_ref, lb_ref, ws_ref, bst_ref, a_ref):
        u_act, _, _, _, _, _, vs = _sgu_mid(p_ref, lg_ref, lb_ref, ws_ref, bst_ref, w, with_grad=False)
        for g in range(SGU_GROUPS):
            cs = slice(g * gd, (g + 1) * gd)
            gate = p_ref[:, 2 * w + g * gd:2 * w + (g + 1) * gd]
            a_ref[:, cs] = (u_act[:, cs] * vs[g] * (gate * _sigmoid(gate))).astype(BF16)

    full = lambda *shape: pl.BlockSpec(shape, lambda i: (0,) * len(shape))
    return pl.pallas_call(
        body, name=name, grid=(nb,),
        in_specs=[pl.BlockSpec((SGU_BLOCK, w3), lambda i: (i, 0)), full(1, w), full(1, w),
                  full(SGU_GROUPS, SGU_BLOCK, SGU_BLOCK), full(SGU_BLOCK, SGU_GROUPS)],
        out_specs=pl.BlockSpec((SGU_BLOCK, w), lambda i: (i, 0)),
        out_shape=jax.ShapeDtypeStruct((t, w), BF16),
        compiler_params=_cparams(("parallel",)),
    )(proj, ln_gain, ln_bias, ws_masked, bs_t)


def _sgu_bwd(da, proj, ln_gain, ln_bias, ws_masked, ws_masked_t, bs_t, *, name):
    t, w3 = proj.shape
    w = w3 // 3
    gd = w // SGU_GROUPS
    nb = t // SGU_BLOCK

    def body(da_ref, p_ref, lg_ref, lb_ref, ws_ref, wst_ref, bst_ref, dp_ref, dws_ref, dbst_ref, dlg_ref, dlb_ref,
             dvn_ref):
        i = pl.program_id(0)
        u_act, du_fac, dv_fac, rstd, xh, vn, vs = _sgu_mid(p_ref, lg_ref, lb_ref, ws_ref, bst_ref, w)
        for g in range(SGU_GROUPS):
            cs = slice(g * gd, (g + 1) * gd)
            gate = p_ref[:, 2 * w + g * gd:2 * w + (g + 1) * gd]
            sg = _sigmoid(gate)
            silu = gate * sg
            da_g = da_ref[:, cs]
            ua_g = u_act[:, cs]
            dp_ref[:, cs] = (da_g * vs[g] * silu * du_fac[:, cs]).astype(BF16)
            dp_ref[:, 2 * w + g * gd:2 * w + (g + 1) * gd] = (
                da_g * ua_g * vs[g] * (sg * (1.0 + gate * (1.0 - sg)))).astype(BF16)
            dvs = da_g * ua_g * silu
            dvs_bf = dvs.astype(BF16)
            dvn_ref[:, cs] = _dot_nn(wst_ref[g].astype(BF16), dvs_bf)
            dws = _dot_nt(dvs_bf, vn[:, cs])
            dbs = jnp.sum(dvs, axis=1, keepdims=True)

            @pl.when(i == 0)
            def _():
                dws_ref[g] = dws
                dbst_ref[:, g:g + 1] = dbs

            @pl.when(i > 0)
            def _():
                dws_ref[g] += dws
                dbst_ref[:, g:g + 1] += dbs

        dvn = dvn_ref[...]
        _acc_rows(dlg_ref, i, jnp.sum(dvn * xh, axis=0, keepdims=True))
        _acc_rows(dlb_ref, i, jnp.sum(dvn, axis=0, keepdims=True))
        dxh = dvn * lg_ref[...]
        dvf = rstd * (dxh - jnp.mean(dxh, axis=-1, keepdims=True)
                      - xh * jnp.mean(dxh * xh, axis=-1, keepdims=True))
        dp_ref[:, w:2 * w] = (dvf * dv_fac).astype(BF16)

    full = lambda *shape: pl.BlockSpec(shape, lambda i: (0,) * len(shape))
    return pl.pallas_call(
        body, name=name, grid=(nb,),
        in_specs=[pl.BlockSpec((SGU_BLOCK, w), lambda i: (i, 0)), pl.BlockSpec((SGU_BLOCK, w3), lambda i: (i, 0)),
                  full(1, w), full(1, w), full(SGU_GROUPS, SGU_BLOCK, SGU_BLOCK),
                  full(SGU_GROUPS, SGU_BLOCK, SGU_BLOCK), full(SGU_BLOCK, SGU_GROUPS)],
        out_specs=[pl.BlockSpec((SGU_BLOCK, w3), lambda i: (i, 0)), full(SGU_GROUPS, SGU_BLOCK, SGU_BLOCK),
                   full(SGU_BLOCK, SGU_GROUPS), full(1, w), full(1, w)],
        out_shape=[jax.ShapeDtypeStruct((t, w3), BF16), jax.ShapeDtypeStruct((SGU_GROUPS, SGU_BLOCK, SGU_BLOCK), F32),
                   jax.ShapeDtypeStruct((SGU_BLOCK, SGU_GROUPS), F32), jax.ShapeDtypeStruct((1, w), F32),
                   jax.ShapeDtypeStruct((1, w), F32)],
        scratch_shapes=[pltpu.VMEM((SGU_BLOCK, w), F32)],
        compiler_params=_cparams(("arbitrary",)),
    )(da, proj, ln_gain, ln_bias, ws_masked, ws_masked_t, bs_t)


def _tile2d(rows, cols, block_bytes, row_unit):
    if rows % row_unit == 0:
        return _pick(rows, max(row_unit, block_bytes // (4 * cols)), row_unit), cols
    return rows, _pick(cols, max(LANES, block_bytes // (4 * rows)))


def _adamw(w, g, m, v, *, name, block_bytes=ELEMENTWISE_BLOCK_BYTES, after=None):
    rows, cols = w.shape
    tr, tc = _tile2d(rows, cols, block_bytes, 8)
    g_rows = g.shape[0]
    assert g_rows == rows or tr == rows
    extra_specs, extra_args = ([], []) if after is None else ([pl.BlockSpec(memory_space=pl.ANY)], [after])

    def body(w_ref, g_ref, m_ref, v_ref, *rest):
        go_ref, d_ref, mo_ref, vo_ref = rest[len(extra_args):]
        gv = g_ref[0:tr, :]
        go_ref[...] = gv
        mn = ADAM_B1 * m_ref[...] + (1.0 - ADAM_B1) * gv
        vn = ADAM_B2 * v_ref[...] + (1.0 - ADAM_B2) * (gv * gv)
        m_hat = mn / (1.0 - ADAM_B1 ** ADAM_STEP)
        v_hat = vn / (1.0 - ADAM_B2 ** ADAM_STEP)
        d_ref[...] = -ADAM_LR * (m_hat / (jnp.sqrt(v_hat) + ADAM_EPS) + ADAM_WD * w_ref[...])
        mo_ref[...] = mn
        vo_ref[...] = vn

    spec = pl.BlockSpec((tr, tc), lambda i, j: (i, j))
    g_spec = spec if g_rows == rows else pl.BlockSpec((g_rows, tc), lambda i, j: (0, j))
    return pl.pallas_call(
        body, name=name, grid=(rows // tr, cols // tc), in_specs=[spec, g_spec, spec, spec] + extra_specs,
        out_specs=[spec] * 4, out_shape=[jax.ShapeDtypeStruct((rows, cols), F32)] * 4,
        compiler_params=_cparams(("parallel", "parallel")),
    )(w, g, m, v, *extra_args)


def _matmul_dw_pair(a_me, a_sib, b_me, b_sib, core_idx, *, shards_on, name, after=None, part=(0, 1)):
    T, M = a_me.shape
    N = b_me.shape[1]
    if shards_on == "rows":
        p, count = part
        tm, hc = M // N_CHIPS, N // 2
        hp = hc // count
        tn = _pick(hp, 1024)
        per = hp // tn
        grid = (N_CHIPS, per)
        a_spec = pl.BlockSpec((T, tm), lambda i, n, h: (0, i))
        b_me_spec = pl.BlockSpec((T, tn), lambda i, n, h: (0, (h[0] * count + p) * per + n))
        b_sib_spec = pl.BlockSpec((T, tn), lambda i, n, h: (0, p * per + n))
        out_spec = pl.BlockSpec((None, tm, tn), lambda i, n, h: (i, 0, n))
        out_shape = jax.ShapeDtypeStruct((N_CHIPS, tm, hp), BF16)
    else:
        tm, hc = _pick(M, 1024), N // N_CHIPS // 2
        grid = (M // tm, N_CHIPS)
        a_spec = pl.BlockSpec((T, tm), lambda i, j, h: (0, i))
        b_me_spec = pl.BlockSpec((T, hc), lambda i, j, h: (0, 2 * j + h[0]))
        b_sib_spec = pl.BlockSpec((T, hc), lambda i, j, h: (0, j))
        out_spec = pl.BlockSpec((None, tm, hc), lambda i, j, h: (j, i, 0))
        out_shape = jax.ShapeDtypeStruct((N_CHIPS, M, hc), BF16)
    extra_specs, extra_args = ([], []) if after is None else ([pl.BlockSpec(memory_space=pl.ANY)], [after])

    def body(h_ref, am_ref, as_ref, bm_ref, bs_ref, *rest):
        o_ref = rest[len(extra_args)]
        o_ref[...] = (_dot_tn(am_ref[...], bm_ref[...]) + _dot_tn(as_ref[...], bs_ref[...])).astype(BF16)

    grid_spec = pltpu.PrefetchScalarGridSpec(
        num_scalar_prefetch=1, grid=grid, in_specs=[a_spec, a_spec, b_me_spec, b_sib_spec] + extra_specs,
        out_specs=out_spec)
    return pl.pallas_call(
        body, name=name, grid_spec=grid_spec, out_shape=out_shape, compiler_params=_cparams(("parallel", "parallel")),
    )(core_idx, a_me, a_sib, b_me, b_sib, *extra_args)


def _chip_sum(pair, landed, slots, *, name, block_bytes=ELEMENTWISE_BLOCK_BYTES, part=(0, 1), into=None):
    p, count = part
    _, r, hp = pair.shape
    tr, tc = _tile2d(r, hp, block_bytes, 16)
    ncb = hp // tc
    extra_specs, extra_args = ([], []) if into is None else ([pl.BlockSpec(memory_space=pl.ANY)], [into])

    def body(s_ref, own_ref, l0_ref, l1_ref, l2_ref, *rest):
        rest[-1][...] = ((own_ref[...].astype(F32) + l0_ref[...].astype(F32)) + l1_ref[...].astype(F32)
                         ) + l2_ref[...].astype(F32)

    def slab(which):
        return pl.BlockSpec((None, tr, tc), lambda i, k, s: (s[which], i, k))

    grid_spec = pltpu.PrefetchScalarGridSpec(
        num_scalar_prefetch=1, grid=(r // tr, ncb),
        in_specs=[slab(0), slab(1), slab(2), slab(3)] + extra_specs,
        out_specs=pl.BlockSpec((tr, tc), lambda i, k, s: (i, (s[4] * count + p) * ncb + k)))
    return pl.pallas_call(
        body, name=name, grid_spec=grid_spec, out_shape=jax.ShapeDtypeStruct((r, 2 * hp * count), F32),
        input_output_aliases={} if into is None else {5: 0},
        compiler_params=_cparams(("parallel", "parallel")),
    )(slots, pair, landed, landed, landed, *extra_args)


def _stack_sum(x, *, name, out_dtype=F32, block_bytes=ELEMENTWISE_BLOCK_BYTES):
    s, r, c = x.shape
    tr = _pick(r, max(8, block_bytes // (4 * c)), 16) if r % 16 == 0 else r

    def body(x_ref, o_ref):
        acc = x_ref[0].astype(F32)
        for j in range(1, s):
            acc = acc + x_ref[j].astype(F32)
        o_ref[...] = acc.astype(out_dtype)

    return pl.pallas_call(
        body, name=name, grid=(r // tr,),
        in_specs=[pl.BlockSpec((s, tr, c), lambda i: (0, i, 0))], out_specs=pl.BlockSpec((tr, c), lambda i: (i, 0)),
        out_shape=jax.ShapeDtypeStruct((r, c), out_dtype), compiler_params=_cparams(("parallel",)),
    )(x)


HBM = pl.BlockSpec(memory_space=pltpu.HBM)


def _place():
    x, y, c = lax.axis_index("x"), lax.axis_index("y"), lax.axis_index("c")
    other_chips = [(1 - x, y), (x, 1 - y), (1 - x, 1 - y)]
    return x, y, c, other_chips


def _handshake(peers):
    barrier = pltpu.get_barrier_semaphore()
    for peer in peers:
        pl.semaphore_signal(barrier, inc=1, device_id=peer, device_id_type=MESH)
    pl.semaphore_wait(barrier, len(peers))


def _sibling():
    x, y, c, _ = _place()
    return [(x, y, 1 - c)]


def _same_core_chips():
    x, y, c, chips = _place()
    return [(cx, cy, c) for cx, cy in chips]


def _same_core_neighbours():
    x, y, c, _ = _place()
    return [(1 - x, y, c), (x, 1 - y, c)]


def _split_params(cid):
    return pltpu.CompilerParams(has_side_effects=SIDE_EFFECT, collective_id=cid)


def _half_cols(cols, which):
    hc = cols // 2
    return pl.ds(pl.multiple_of(which * hc, LANES), hc)


SEM = pl.BlockSpec(memory_space=pltpu.SEMAPHORE)
ANY = pl.BlockSpec(memory_space=pl.ANY)
SIDE_EFFECT = pltpu.SideEffectType.DATAFLOW_SIDE_EFFECTING
TOKEN_SHAPE = (8, LANES)


def _hbm(shape, dtype):
    return pltpu.HBM(shape, dtype)


def _in_hbm(a):
    return pltpu.with_memory_space_constraint(a, pltpu.HBM)


def _gather_copy(src_ref, land_ref, ssem, rsem, k, chip_of_block, to, c):
    cols = src_ref.shape[1]
    return pltpu.make_async_remote_copy(
        src_ref=src_ref.at[:, _half_cols(cols, c)], dst_ref=land_ref.at[chip_of_block, :, _half_cols(cols, c)],
        send_sem=ssem.at[k], recv_sem=rsem.at[k], device_id=to, device_id_type=MESH)


NEIGHBOURS = (0, 1)
ALL_CHIPS = (0, 1, 2)


def _gather_start(shards, *, name, cid, after=(), relayed=(), own_slab=None):
    n = len(shards)
    after = list(after)

    def body(*refs):
        srcs, lands = refs[:n], refs[n:2 * n]
        outs = refs[2 * n + len(after):]
        token = outs[-1]
        _handshake(_same_core_chips())
        x, y, c, chips = _place()
        me = 2 * x + y
        for a in range(n):
            ssem, rsem = outs[4 * a], outs[4 * a + 1]
            for k in NEIGHBOURS if a in relayed else ALL_CHIPS:
                cx, cy = chips[k]
                _gather_copy(srcs[a], lands[a], ssem, rsem, k, me, (cx, cy, c), c).start()
        token[...] = jnp.zeros_like(token)

    out_shape, out_specs, aliases = [], [], {}
    for a, s in enumerate(shards):
        out_shape += [pltpu.SemaphoreType.DMA((3,)), pltpu.SemaphoreType.DMA((3,)), _hbm(s.shape, s.dtype),
                      _hbm((N_CHIPS,) + s.shape, s.dtype)]
        out_specs += [SEM, SEM, HBM, HBM]
        aliases[a] = 4 * a + 2
        aliases[n + a] = 4 * a + 3
    out_shape.append(jax.ShapeDtypeStruct(TOKEN_SHAPE, F32))
    out_specs.append(pl.BlockSpec(memory_space=pltpu.VMEM))
    lands = [lax.empty((N_CHIPS,) + s.shape, s.dtype) for s in shards]
    if own_slab is not None:
        lands = [lax.dynamic_update_slice(land, s[None], (own_slab, 0, 0)) for land, s in zip(lands, shards)]
    lands = [_in_hbm(land) for land in lands]
    res = pl.pallas_call(
        body, name=name, in_specs=[HBM] * (2 * n) + [ANY] * len(after), out_specs=out_specs, out_shape=out_shape,
        input_output_aliases=aliases, compiler_params=_split_params(cid),
    )(*[_in_hbm(s) for s in shards], *lands, *after)
    return [tuple(res[4 * a:4 * a + 4]) for a in range(n)], res[-1]


def _wait_call(wait_fn, parts, after, *, name):
    ssem, rsem, src, land = parts
    after = list(after) if isinstance(after, (list, tuple)) else [after]

    def body(src_ref, land_ref, ssem_ref, rsem_ref, *rest):
        wait_fn(src_ref, land_ref, ssem_ref, rsem_ref)

    return pl.pallas_call(
        body, name=name, in_specs=[HBM, HBM, SEM, SEM] + [ANY] * len(after), out_specs=[HBM, HBM],
        out_shape=[_hbm(src.shape, src.dtype), _hbm(land.shape, land.dtype)], input_output_aliases={0: 0, 1: 1},
        compiler_params=pltpu.CompilerParams(has_side_effects=SIDE_EFFECT),
    )(src, land, ssem, rsem, *after)


def _gather_wait(parts, after, *, name, ks=ALL_CHIPS):
    def wait(src_ref, land_ref, ssem_ref, rsem_ref):
        x, y, c, chips = _place()
        for k in ks:
            cx, cy = chips[k]
            cp = _gather_copy(src_ref, land_ref, ssem_ref, rsem_ref, k, 2 * cx + cy, (x, y, c), c)
            cp.wait_send()
            cp.wait_recv()

    return _wait_call(wait, parts, after, name=name)


def _relay_copy(buf_ref, ssem, rsem, k, slab, to, c):
    hr = buf_ref.shape[1] // 2
    part = buf_ref.at[slab, pl.ds(k * hr, hr), _half_cols(buf_ref.shape[2], c)]
    return pltpu.make_async_remote_copy(
        src_ref=part, dst_ref=part, send_sem=ssem.at[k], recv_sem=rsem.at[k], device_id=to, device_id_type=MESH)


def _relay_start(land, *, name, cid):
    def body(buf_ref, ssem, rsem, buf_out, token):
        _handshake(_same_core_neighbours())
        x, y, c, _ = _place()
        _relay_copy(buf_ref, ssem, rsem, 0, 2 * (1 - x) + y, (x, 1 - y, c), c).start()
        _relay_copy(buf_ref, ssem, rsem, 1, 2 * x + 1 - y, (1 - x, y, c), c).start()
        token[...] = jnp.zeros_like(token)

    res = pl.pallas_call(
        body, name=name, in_specs=[HBM], out_specs=[SEM, SEM, HBM, pl.BlockSpec(memory_space=pltpu.VMEM)],
        out_shape=[pltpu.SemaphoreType.DMA((2,)), pltpu.SemaphoreType.DMA((2,)), _hbm(land.shape, land.dtype),
                   jax.ShapeDtypeStruct(TOKEN_SHAPE, F32)],
        input_output_aliases={0: 2}, compiler_params=_split_params(cid),
    )(land)
    return tuple(res[:3]), res[3]


def _relay_wait(parts, after, *, name):
    ssem, rsem, buf = parts
    after = list(after) if isinstance(after, (list, tuple)) else [after]

    def body(buf_ref, ssem_ref, rsem_ref, *rest):
        x, y, c, _ = _place()
        diagonal = 2 * (1 - x) + 1 - y
        _relay_copy(buf_ref, ssem_ref, rsem_ref, 0, 2 * (1 - x) + y, (x, y, c), c).wait_send()
        _relay_copy(buf_ref, ssem_ref, rsem_ref, 1, 2 * x + 1 - y, (x, y, c), c).wait_send()
        _relay_copy(buf_ref, ssem_ref, rsem_ref, 0, diagonal, (x, y, c), c).wait_recv()
        _relay_copy(buf_ref, ssem_ref, rsem_ref, 1, diagonal, (x, y, c), c).wait_recv()

    return pl.pallas_call(
        body, name=name, in_specs=[HBM, SEM, SEM] + [ANY] * len(after), out_specs=HBM,
        out_shape=_hbm(buf.shape, buf.dtype), input_output_aliases={0: 0},
        compiler_params=pltpu.CompilerParams(has_side_effects=SIDE_EFFECT),
    )(buf, ssem, rsem, *after)


def _forward_copy(buf_ref, ssem, rsem, k, slab, which, to):
    part = buf_ref.at[slab, :, _half_cols(buf_ref.shape[2], which)]
    return pltpu.make_async_remote_copy(
        src_ref=part, dst_ref=part, send_sem=ssem.at[k], recv_sem=rsem.at[k], device_id=to, device_id_type=MESH)


def _sibling_forward(land, *, name, cid, ks=ALL_CHIPS):
    def body(_, buf, send_sems, recv_sems):
        _handshake(_sibling())
        x, y, c, chips = _place()
        copies = []
        for k in ks:
            cx, cy = chips[k]
            cp = _forward_copy(buf, send_sems, recv_sems, k, 2 * cx + cy, c, (x, y, 1 - c))
            cp.start()
            copies.append(cp)
        for k in ks:
            cx, cy = chips[k]
            _forward_copy(buf, send_sems, recv_sems, k, 2 * cx + cy, 1 - c, (x, y, c)).wait_recv()
        for cp in copies:
            cp.wait_send()

    return pl.pallas_call(
        body, name=name, in_specs=[HBM], out_specs=HBM, out_shape=jax.ShapeDtypeStruct(land.shape, land.dtype),
        input_output_aliases={0: 0},
        scratch_shapes=[pltpu.SemaphoreType.DMA((3,)), pltpu.SemaphoreType.DMA((3,))],
        compiler_params=pltpu.CompilerParams(collective_id=cid),
    )(land)


def _forward_start(land, *, name, cid, ks=ALL_CHIPS):
    def body(buf_ref, ssem, rsem, buf_out, token):
        _handshake(_sibling())
        x, y, c, chips = _place()
        for k in ks:
            cx, cy = chips[k]
            _forward_copy(buf_ref, ssem, rsem, k, 2 * cx + cy, c, (x, y, 1 - c)).start()
        token[...] = jnp.zeros_like(token)

    res = pl.pallas_call(
        body, name=name, in_specs=[HBM], out_specs=[SEM, SEM, HBM, pl.BlockSpec(memory_space=pltpu.VMEM)],
        out_shape=[pltpu.SemaphoreType.DMA((3,)), pltpu.SemaphoreType.DMA((3,)), _hbm(land.shape, land.dtype),
                   jax.ShapeDtypeStruct(TOKEN_SHAPE, F32)],
        input_output_aliases={0: 2}, compiler_params=_split_params(cid),
    )(land)
    return tuple(res[:3]), res[3]


def _forward_wait(parts, after, *, name, ks=ALL_CHIPS):
    ssem, rsem, buf = parts
    after = list(after) if isinstance(after, (list, tuple)) else [after]

    def body(buf_ref, ssem_ref, rsem_ref, *rest):
        x, y, c, chips = _place()
        for k in ks:
            cx, cy = chips[k]
            _forward_copy(buf_ref, ssem_ref, rsem_ref, k, 2 * cx + cy, c, (x, y, c)).wait_send()
            _forward_copy(buf_ref, ssem_ref, rsem_ref, k, 2 * cx + cy, 1 - c, (x, y, c)).wait_recv()

    return pl.pallas_call(
        body, name=name, in_specs=[HBM, SEM, SEM] + [ANY] * len(after), out_specs=HBM,
        out_shape=_hbm(buf.shape, buf.dtype), input_output_aliases={0: 0},
        compiler_params=pltpu.CompilerParams(has_side_effects=SIDE_EFFECT),
    )(buf, ssem, rsem, *after)


def _share_copy(buf_ref, ssem, rsem, a, which, to):
    part = buf_ref.at[:, _half_cols(buf_ref.shape[1], which)]
    return pltpu.make_async_remote_copy(
        src_ref=part, dst_ref=part, send_sem=ssem.at[a], recv_sem=rsem.at[a], device_id=to, device_id_type=MESH)


def _share_start(arrays, *, name, cid):
    n = len(arrays)

    def body(*refs):
        bufs, ssem, rsem, token = refs[:n], refs[n], refs[n + 1], refs[-1]
        _handshake(_sibling())
        x, y, c, _ = _place()
        for a in range(n):
            _share_copy(bufs[a], ssem, rsem, a, c, (x, y, 1 - c)).start()
        token[...] = jnp.zeros_like(token)

    res = pl.pallas_call(
        body, name=name, in_specs=[HBM] * n,
        out_specs=[SEM, SEM] + [HBM] * n + [pl.BlockSpec(memory_space=pltpu.VMEM)],
        out_shape=[pltpu.SemaphoreType.DMA((n,)), pltpu.SemaphoreType.DMA((n,))]
        + [_hbm(b.shape, b.dtype) for b in arrays] + [jax.ShapeDtypeStruct(TOKEN_SHAPE, F32)],
        input_output_aliases={a: 2 + a for a in range(n)}, compiler_params=_split_params(cid),
    )(*[_in_hbm(b) for b in arrays])
    return (res[0], res[1], list(res[2:2 + n])), res[-1]


def _share_wait(parts, after, *, name):
    ssem, rsem, bufs = parts
    n = len(bufs)
    after = list(after) if isinstance(after, (list, tuple)) else [after]

    def body(*refs):
        buf_refs, ssem_ref, rsem_ref = refs[:n], refs[n], refs[n + 1]
        x, y, c, _ = _place()
        for a in range(n):
            _share_copy(buf_refs[a], ssem_ref, rsem_ref, a, c, (x, y, c)).wait_send()
            _share_copy(buf_refs[a], ssem_ref, rsem_ref, a, 1 - c, (x, y, c)).wait_recv()

    return pl.pallas_call(
        body, name=name, in_specs=[HBM] * n + [SEM, SEM] + [ANY] * len(after), out_specs=[HBM] * n,
        out_shape=[_hbm(b.shape, b.dtype) for b in bufs], input_output_aliases={a: a for a in range(n)},
        compiler_params=pltpu.CompilerParams(has_side_effects=SIDE_EFFECT),
    )(*bufs, ssem, rsem, *after)


def _scatter_copy(src_ref, land_ref, ssem, rsem, k, src_slab, dst_slab, to):
    return pltpu.make_async_remote_copy(
        src_ref=src_ref.at[src_slab], dst_ref=land_ref.at[dst_slab], send_sem=ssem.at[k], recv_sem=rsem.at[k],
        device_id=to, device_id_type=MESH)


def _scatter_start(part, *, name, cid):
    def start(src_ref, land_ref, ssem, rsem):
        x, y, c, chips = _place()
        me = 2 * x + y
        for k, (cx, cy) in enumerate(chips):
            _scatter_copy(src_ref, land_ref, ssem, rsem, k, 2 * cx + cy, me, (cx, cy, c)).start()

    return _split_start(start, _same_core_chips, part, part.shape, N_CHIPS - 1, name=name, cid=cid)


def _scatter_wait(parts, after, *, name):
    def wait(src_ref, land_ref, ssem_ref, rsem_ref):
        x, y, c, chips = _place()
        for k, (cx, cy) in enumerate(chips):
            idx = 2 * cx + cy
            cp = _scatter_copy(src_ref, land_ref, ssem_ref, rsem_ref, k, idx, idx, (x, y, c))
            cp.wait_send()
            cp.wait_recv()

    return _wait_call(wait, parts, after, name=name)


def _split_start(start_fn, peers_fn, src, land_shape, n_sems, *, name, cid):
    def body(src_ref, land_ref, ssem, rsem, src_out, land_out, token):
        _handshake(peers_fn())
        start_fn(src_ref, land_ref, ssem, rsem)
        token[...] = jnp.zeros_like(token)

    res = pl.pallas_call(
        body, name=name, in_specs=[HBM, HBM], out_specs=[SEM, SEM, HBM, HBM, pl.BlockSpec(memory_space=pltpu.VMEM)],
        out_shape=[pltpu.SemaphoreType.DMA((n_sems,)), pltpu.SemaphoreType.DMA((n_sems,)), _hbm(src.shape, src.dtype),
                   _hbm(land_shape, src.dtype), jax.ShapeDtypeStruct(TOKEN_SHAPE, F32)],
        input_output_aliases={0: 2, 1: 3}, compiler_params=_split_params(cid),
    )(_in_hbm(src), _in_hbm(lax.empty(land_shape, src.dtype)))
    return tuple(res[:4]), res[4]


def _sibling_copies(src_ref, land_ref, ssem, rsem, k0, groups, which, to):
    def copy(k, src, dst):
        return pltpu.make_async_remote_copy(
            src_ref=src, dst_ref=dst, send_sem=ssem.at[k], recv_sem=rsem.at[k], device_id=to, device_id_type=MESH)

    if groups == 0:
        return [copy(k0, src_ref, land_ref)]
    hw = src_ref.shape[1] // groups // 2
    return [copy(k0 + j, src_ref.at[:, pl.ds(pl.multiple_of((2 * j + which) * hw, LANES), hw)],
                 land_ref.at[:, j * hw:(j + 1) * hw]) for j in range(groups)]


def _to_sibling_start(items, *, name, cid):
    n = len(items)
    shapes = [a.shape if g == 0 else (a.shape[0], a.shape[1] // 2) for a, g in items]
    first = [sum(max(g, 1) for _, g in items[:k]) for k in range(n + 1)]

    def body(*refs):
        srcs, lands, ssem, rsem, token = refs[:n], refs[n:2 * n], refs[2 * n], refs[2 * n + 1], refs[-1]
        _handshake(_sibling())
        x, y, c, _ = _place()
        for k, (_, g) in enumerate(items):
            for cp in _sibling_copies(srcs[k], lands[k], ssem, rsem, first[k], g, 1 - c, (x, y, 1 - c)):
                cp.start()
        token[...] = jnp.zeros_like(token)

    res = pl.pallas_call(
        body, name=name, in_specs=[HBM] * (2 * n),
        out_specs=[SEM, SEM] + [HBM] * (2 * n) + [pl.BlockSpec(memory_space=pltpu.VMEM)],
        out_shape=[pltpu.SemaphoreType.DMA((first[n],)), pltpu.SemaphoreType.DMA((first[n],))]
        + [_hbm(a.shape, a.dtype) for a, _ in items] + [_hbm(s, a.dtype) for s, (a, _) in zip(shapes, items)]
        + [jax.ShapeDtypeStruct(TOKEN_SHAPE, F32)],
        input_output_aliases={k: 2 + k for k in range(2 * n)}, compiler_params=_split_params(cid),
    )(*[_in_hbm(a) for a, _ in items], *[_in_hbm(lax.empty(s, a.dtype)) for s, (a, _) in zip(shapes, items)])
    return [(res[0], res[1], first[k], g, res[2 + k], res[2 + n + k]) for k, (_, g) in enumerate(items)], res[-1]


def _from_sibling(flight, after, *, name):
    ssem, rsem, k0, groups, src, land = flight

    def wait(src_ref, land_ref, ssem_ref, rsem_ref):
        x, y, c, _ = _place()
        for cp in _sibling_copies(src_ref, land_ref, ssem_ref, rsem_ref, k0, groups, 1 - c, (x, y, c)):
            cp.wait_send()
            cp.wait_recv()

    return _wait_call(wait, (ssem, rsem, src, land), after, name=name)


def _dev_peers(x, y, c, chips):
    return [(x, y, 1 - c)] + [(cx, cy, c) for cx, cy in chips] + [(cx, cy, 1 - c) for cx, cy in chips]


def _dev_gather_start(part, *, name, cid):
    def start(src_ref, land_ref, ssem, rsem):
        x, y, c, chips = _place()
        for k, to in enumerate(_dev_peers(x, y, c, chips)):
            pltpu.make_async_remote_copy(
                src_ref=src_ref, dst_ref=land_ref.at[4 * x + 2 * y + c], send_sem=ssem.at[k], recv_sem=rsem.at[k],
                device_id=to, device_id_type=MESH).start()

    return _split_start(start, lambda: _dev_peers(*_place()), part, (N_DEV,) + part.shape, N_DEV - 1, name=name,
                        cid=cid)


def _dev_gather_wait(parts, after, *, name):
    def wait(src_ref, land_ref, ssem_ref, rsem_ref):
        x, y, c, chips = _place()
        for k, (px, py, pc) in enumerate(_dev_peers(x, y, c, chips)):
            cp = pltpu.make_async_remote_copy(
                src_ref=src_ref, dst_ref=land_ref.at[4 * px + 2 * py + pc], send_sem=ssem_ref.at[k],
                recv_sem=rsem_ref.at[k], device_id=(x, y, c), device_id_type=MESH)
            cp.wait_send()
            cp.wait_recv()

    return _wait_call(wait, parts, after, name=name)[1]


def _sibling_share_halves(arrays, *, name, cid):
    n = len(arrays)

    def body(*refs):
        bufs = refs[n:2 * n]
        send_sems, recv_sems = refs[2 * n:]
        _handshake(_sibling())
        x, y, c, _ = _place()
        copies = []
        for a in range(n):
            mine = bufs[a].at[:, _half_cols(bufs[a].shape[1], c)]
            cp = pltpu.make_async_remote_copy(
                src_ref=mine, dst_ref=mine, send_sem=send_sems.at[a], recv_sem=recv_sems.at[a],
                device_id=(x, y, 1 - c), device_id_type=MESH)
            cp.start()
            copies.append(cp)
        for a in range(n):
            theirs = bufs[a].at[:, _half_cols(bufs[a].shape[1], 1 - c)]
            pltpu.make_async_remote_copy(
                src_ref=theirs, dst_ref=theirs, send_sem=send_sems.at[a], recv_sem=recv_sems.at[a],
                device_id=(x, y, c), device_id_type=MESH).wait_recv()
        for cp in copies:
            cp.wait_send()

    return pl.pallas_call(
        body, name=name, in_specs=[HBM] * n, out_specs=[HBM] * n,
        out_shape=[jax.ShapeDtypeStruct(h.shape, h.dtype) for h in arrays],
        input_output_aliases={a: a for a in range(n)},
        scratch_shapes=[pltpu.SemaphoreType.DMA((n,)), pltpu.SemaphoreType.DMA((n,))],
        compiler_params=pltpu.CompilerParams(collective_id=cid),
    )(*arrays)


def _chip_sum_share(pair, landed, slots, into, *, part, name, cid, block_bytes=ELEMENTWISE_BLOCK_BYTES // 4):
    p, count = part
    assert p == count - 1 and p > 0
    _, r, hp = pair.shape
    tr, tc = _tile2d(r, hp, block_bytes, 16)
    n = r // tr
    assert tc == hp and n >= 2

    def body(s_ref, own_ref, l0_ref, l1_ref, l2_ref, _, out_ref, buf, here_sems, send_sems, recv_sems):
        i = pl.program_id(0)
        x, y, c, _chips = _place()
        me, sibling = (x, y, c), (x, y, 1 - c)

        def cols(which, first, width):
            return pl.ds(pl.multiple_of((which * count + first) * hp, LANES), width)

        def earlier(which, to):
            done = out_ref.at[:, cols(which, 0, p * hp)]
            return pltpu.make_async_remote_copy(src_ref=done, dst_ref=done, send_sem=send_sems.at[2],
                                                recv_sem=recv_sems.at[n], device_id=to, device_id_type=MESH)

        def block_copies(j, which, to):
            dst = out_ref.at[pl.ds(j * tr, tr), cols(which, p, hp)]
            here = pltpu.make_async_copy(buf.at[j % 2], dst, here_sems.at[j % 2])
            there = pltpu.make_async_remote_copy(src_ref=buf.at[j % 2], dst_ref=dst, send_sem=send_sems.at[j % 2],
                                                 recv_sem=recv_sems.at[j], device_id=to, device_id_type=MESH)
            return here, there

        def finish(j):
            here, there = block_copies(j, c, sibling)
            here.wait()
            there.wait_send()

        @pl.when(i == 0)
        def _():
            _handshake([sibling])
            earlier(c, sibling).start()

        @pl.when(i >= 2)
        def _():
            finish(i - 2)

        buf[i % 2] = ((own_ref[...].astype(F32) + l0_ref[...].astype(F32)) + l1_ref[...].astype(F32)
                      ) + l2_ref[...].astype(F32)
        here, there = block_copies(i, c, sibling)
        here.start()
        there.start()

        @pl.when(i == n - 1)
        def _():
            finish(n - 2)
            finish(n - 1)
            for j in range(n):
                block_copies(j, 1 - c, me)[1].wait_recv()
            earlier(1 - c, me).wait_recv()
            earlier(c, sibling).wait_send()

    def slab(which):
        return pl.BlockSpec((None, tr, hp), lambda i, s: (s[which], i, 0))

    grid_spec = pltpu.PrefetchScalarGridSpec(
        num_scalar_prefetch=1, grid=(n,),
        in_specs=[slab(0), slab(1), slab(2), slab(3), pl.BlockSpec(memory_space=pl.ANY)],
        out_specs=pl.BlockSpec(memory_space=pl.ANY),
        scratch_shapes=[pltpu.VMEM((2, tr, hp), F32), pltpu.SemaphoreType.DMA((2,)), pltpu.SemaphoreType.DMA((3,)),
                        pltpu.SemaphoreType.DMA((n + 1,))])
    return pl.pallas_call(
        body, name=name, grid_spec=grid_spec, out_shape=jax.ShapeDtypeStruct(into.shape, F32),
        input_output_aliases={5: 0},
        compiler_params=pltpu.CompilerParams(dimension_semantics=("arbitrary",), collective_id=cid,
                                             vmem_limit_bytes=VMEM_LIMIT),
    )(slots, pair, landed, landed, landed, into)


def _pack(arrays, rows_multiple=16, width=LANES):
    flat = jnp.concatenate([a.astype(F32).reshape(-1) for a in arrays])
    total = flat.shape[0]
    rows = -(-total // width)
    rows = -(-rows // rows_multiple) * rows_multiple
    return jnp.pad(flat, (0, rows * width - total)).reshape(rows, width)


def _unpack(buf, shapes):
    flat = buf.reshape(-1)
    out, off = [], 0
    for s in shapes:
        n = math.prod(s)
        out.append(flat[off:off + n].reshape(s))
        off += n
    return out


def kernel(x, norm_pre, norm_post, gla_w_in, gla_w_gate2, gla_b_gate, gla_o_gain, gla_w_out, sgu_w_in, sgu_ln_gain, sgu_ln_bias, sgu_w_spatial, sgu_b_spatial, sgu_w_out, loss_target, m_norm_pre, m_norm_post, m_gla_w_in, m_gla_w_gate2, m_gla_b_gate, m_gla_o_gain, m_gla_w_out, m_sgu_w_in, m_sgu_ln_gain, m_sgu_ln_bias, m_sgu_w_spatial, m_sgu_b_spatial, m_sgu_w_out, v_norm_pre, v_norm_post, v_gla_w_in, v_gla_w_gate2, v_gla_b_gate, v_gla_o_gain, v_gla_w_out, v_sgu_w_in, v_sgu_ln_gain, v_sgu_ln_bias, v_sgu_w_spatial, v_sgu_b_spatial, v_sgu_w_out):
    _, t, d = x.shape
    dk = d // 2
    ws = gla_w_in.shape[2]
    wp = -(-ws // LANES) * LANES
    lay = (ws, wp)
    chip =2 * lax.axis_index("x") + lax.axis_index("y")
    core = lax.axis_index("c")
    core_idx = core.astype(jnp.int32).reshape(1)
    others = jnp.arange(N_CHIPS - 1, dtype=jnp.int32)
    others = others + (others >= chip).astype(jnp.int32)
    slots = jnp.concatenate([chip.astype(jnp.int32).reshape(1), others, core_idx])

    x0 = x[0]
    target = loss_target[0]

    wt_in_g, mt_in_g, vt_in_g = gla_w_in[0].T, m_gla_w_in[0].T, v_gla_w_in[0].T

    small_shard = _pack([gla_w_gate2[0], sgu_ln_gain[0], sgu_ln_bias[0]], rows_multiple=8, width=2 * LANES)
    own = [small_shard, jnp.pad(wt_in_g.astype(BF16), ((0, wp - ws), (0, 0)))]
    in_flight, token = _gather_start(own, name="gather_start_a", cid=0, relayed=(1,))

    def with_sibling_and_own(mine, land, name, cid):
        return lax.dynamic_update_slice(_sibling_forward(land, name=name + "_share", cid=cid), mine[None],
                                        (chip, 0, 0))

    h0 = _norm_pre(x0, norm_pre[0:1] + token[0:1, 0:1], name="pre0")
    g_small = with_sibling_and_own(*_gather_wait(in_flight[0], h0, name="w_small_wait"), "w_small", 12)
    mine, land = _gather_wait(in_flight[1], [g_small, wt_in_g, mt_in_g, vt_in_g], name="w_gla_in_wait", ks=NEIGHBOURS)
    relay, token = _relay_start(land, name="w_gla_in_relay", cid=11)
    crossing, token = _forward_start(relay[2], name="w_gla_in_share_near", cid=22, ks=NEIGHBOURS)
    own_later = [(p[0] + token[0, 0]).astype(BF16) for p in (gla_w_out, sgu_w_in, sgu_w_out)]
    in_flight_later, token = _gather_start(own_later, name="gather_start_b", cid=1, after=[token], own_slab=chip)
    in_flight = in_flight + in_flight_later
    land = _relay_wait((relay[0], relay[1], crossing[2]), token, name="w_gla_in_relay_wait")
    land = _forward_wait((crossing[0], crossing[1], land), token, name="w_gla_in_share_near_wait", ks=NEIGHBOURS)
    land = _sibling_forward(land, name="w_gla_in_share_far", cid=13, ks=(2,))
    wt_g = lax.dynamic_update_slice(land, mine[None], (chip, 0, 0)).reshape(N_CHIPS * wp, d)

    def behind(small, token):
        return small + token[0:1, 0:1]

    def arriving(i, after, name):
        mine, land = _gather_wait(in_flight[i], after, name=name + "_wait")
        crossing, token = _forward_start(land, name=name + "_share", cid=i)
        return (mine, crossing), token

    def arrived(pending, after, name):
        _, crossing = pending
        return _forward_wait(crossing, after, name=name + "_share_wait")

    shard_shapes = [gla_w_gate2.shape[1:], sgu_ln_gain.shape[1:], sgu_ln_bias.shape[1:]]
    per_chip = [_unpack(g_small[j], shard_shapes) for j in range(N_CHIPS)]
    w2_full = jnp.concatenate([p[0] for p in per_chip], axis=1)
    ln_gain = jnp.concatenate([p[1] for p in per_chip], axis=0)[None, :]
    ln_bias = jnp.concatenate([p[2] for p in per_chip], axis=0)[None, :]
    w2p = jnp.pad(w2_full, ((0, LANES - GLA_GATE_RANK), (0, 0)))

    pos_chunk = jnp.arange(SGU_BLOCK) // CHUNK
    mask = pos_chunk[:, None] >= pos_chunk[None, :]
    ws_masked = jnp.where(mask[None], sgu_w_spatial[0], 0.0)
    ws_masked_t = ws_masked.transpose(0, 2, 1)
    bs_t = sgu_b_spatial[0].T

    proj0 = _matmul(h0, wt_g, mode="nt", out_dtype=F32, name="gla_in", tn=wp)
    pending, tok = arriving(2, proj0, "w_gla_out")
    o0, a0, s_before, s_final = _gla_fwd(proj0, w2p, behind(gla_b_gate, tok), gla_o_gain, lay, name="gla_scan")
    w_out_g = arrived(pending, a0, "w_gla_out").reshape(d, d)
    y0 = _matmul(a0, w_out_g, mode="nn", out_dtype=F32, name="gla_out", tn=1024)
    pending, tok = arriving(3, y0, "w_sgu_in")
    x1, h1 = _post_then_pre(x0, y0, behind(norm_post[0:1], tok), norm_pre[1:2], name="post0_pre1")
    g_wi_s = arrived(pending, h1, "w_sgu_in")
    pending, tok = arriving(4, g_wi_s, "w_sgu_out")
    proj1 = _matmul(h1, g_wi_s, mode="nn", out_dtype=F32, name="sgu_in", b_shards=True, after=tok, tn=768)
    a1 = _sgu_fwd(proj1, ln_gain, ln_bias, ws_masked, bs_t, name="sgu_gate")
    w_out_s = arrived(pending, a1, "w_sgu_out").reshape(d, d)
    acts, tok = _to_sibling_start([(a1, 0), (a0, 0), (h1, 0), (h0, 1)], name="acts_to_sibling", cid=5)
    a1, a0, h1, h0 = [f[4] for f in acts]
    y1 = _matmul(a1, w_out_s, mode="nn", out_dtype=F32, name="sgu_out", after=tok, tn=1024)
    loss_part, dx2, dy1, d_post1 = _loss_head(x1, y1, norm_post[1:2], target, name="loss_head")

    def pair_gradient(a_sent, b_sent, after, shards_on, name, cid):
        a_me, a_sib = _from_sibling(a_sent, after, name=name + "_a_wait")
        b_me, b_sib = _from_sibling(b_sent, [a_sib] + list(after), name=name + "_b_wait")
        pair = _matmul_dw_pair(a_me, a_sib, b_me, b_sib, core_idx, shards_on=shards_on,
                               name=name + "_pair")
        return _scatter_start(pair, name=name + "_start", cid=cid)

    def reduced(flight, after, name):
        pair, landed = _scatter_wait(flight, after, name=name + "_wait")
        return _chip_sum(pair, landed, slots, name=name + "_sum")

    (dy1_sent,), tok = _to_sibling_start([(dy1, 1)], name="dy1_to_sibling", cid=6)
    dy1 = dy1_sent[4]
    da1 = _matmul(dy1, w_out_s, mode="nt", out_dtype=F32, name="d_sgu_act", after=tok, tn=1024)
    fl_wo_s, tok = pair_gradient(acts[0], dy1_sent, [da1], "rows", "g_sgu_out", 15)
    dproj1, d_ws, d_bs_t, d_lg, d_lb = _sgu_bwd(da1, proj1, ln_gain, behind(ln_bias, tok), ws_masked, ws_masked_t,
                                                bs_t, name="sgu_gate_bwd")
    (dp1_sent,), tok = _to_sibling_start([(dproj1, N_CHIPS)], name="dproj1_to_sibling", cid=7)
    dproj1 = dp1_sent[4]
    dh1 = _matmul_nt_shards(dproj1, g_wi_s, out_dtype=F32, name="d_sgu_h", after=tok)
    fl_wi_s, tok = pair_gradient(acts[2], dp1_sent, [dh1], "cols", "g_sgu_in", 16)
    dx1, dy0, d_pre1, d_post0 = _mid_bwd(dx2, dh1, x1, behind(norm_pre[1:2], tok), y0, norm_post[0:1],
                                         name="pre1_post0_bwd")
    (dy0_sent,), tok = _to_sibling_start([(dy0, 1)], name="dy0_to_sibling", cid=8)
    dy0 = dy0_sent[4]
    da0 = _matmul(dy0, w_out_g, mode="nt", out_dtype=F32, name="d_gla_act", after=tok, tn=1024)
    fl_wo_g, tok = pair_gradient(acts[1], dy0_sent, [da0], "rows", "g_gla_out", 17)
    dproj0, d_og, d_bg, d_w2p = _gla_bwd(da0, o0, proj0, w2p, behind(gla_b_gate, tok), gla_o_gain, s_before, s_final,
                                         lay, name="gla_scan_bwd")
    early_shapes = [norm_post.shape, gla_b_gate.shape, gla_o_gain.shape, sgu_w_spatial.shape, sgu_b_spatial.shape,
                    (1, GLA_GATE_RANK, dk), (1, d), (1, d), (1, LANES)]
    early_part = _pack([jnp.concatenate([d_post0, d_post1], axis=0), d_bg, d_og, jnp.where(mask[None], d_ws, 0.0)[None],
                        d_bs_t.T[None], d_w2p[:GLA_GATE_RANK][None], d_lg, d_lb, loss_part])
    early_flight, tok = _dev_gather_start(early_part, name="small_early_start", cid=20)
    (dp0_sent,), tok_sent = _to_sibling_start([(dproj0, 0)], name="dproj0_to_sibling", cid=9)
    dproj0 = dp0_sent[4]
    dh0 = _matmul(dproj0, wt_g, mode="nn", out_dtype=F32, name="d_gla_h", after=tok_sent)
    a_me, a_sib = _from_sibling(dp0_sent, [dh0, tok], name="g_gla_in_a_wait")
    b_me, b_sib = _from_sibling(acts[3], [a_sib, dh0], name="g_gla_in_b_wait")
    fl_wi_g, tok_scatter = [], None
    for p in range(2):
        pair = _matmul_dw_pair(a_me, a_sib, b_me, b_sib, core_idx, shards_on="rows", part=(p, 2),
                               name=f"g_gla_in_pair{p}", after=tok_scatter)
        flight, tok_scatter = _scatter_start(pair, name=f"g_gla_in_start{p}", cid=18 + p)
        fl_wi_g.append(flight)
    r_wo_s = reduced(fl_wo_s, tok_scatter, "g_sgu_out")
    r_wi_s = reduced(fl_wi_s, r_wo_s, "g_sgu_in")
    r_wo_g = reduced(fl_wo_g, r_wi_s, "g_gla_out")
    sharing, tok = _share_start([r_wo_s, r_wi_s, r_wo_g], name="grads_share_a", cid=10)
    grad_x, d_pre0 = _first_bwd(dx1, dh0, x0, behind(norm_pre[0:1], tok), name="pre0_bwd")

    late_part = _pack([jnp.concatenate([d_pre0, d_pre1], axis=0)])
    late_flight, tok = _dev_gather_start(late_part, name="small_late_start", cid=21)

    def big_update(w, g, m, v, name, after=None):
        return [u[None] for u in _adamw(w[0], g, m[0], v[0], name=name, after=after)]

    g_wo_sgu, g_wi_sgu, g_wo_gla = _share_wait(sharing, [grad_x, tok], name="grads_share_a_wait")
    u_wi_sgu = big_update(sgu_w_in, g_wi_sgu, m_sgu_w_in, v_sgu_w_in, "adamw_sgu_w_in")
    u_wo_gla = big_update(gla_w_out, g_wo_gla, m_gla_w_out, v_gla_w_out, "adamw_gla_w_out", after=u_wi_sgu[1])

    r_wi_g, behind_this = None, u_wo_gla[1]
    for p, flight in enumerate(fl_wi_g):
        pair, landed = _scatter_wait(flight, behind_this, name=f"g_gla_in_wait{p}")
        if p == 0:
            r_wi_g = behind_this = _chip_sum(pair, landed, slots, part=(p, 2), name=f"g_gla_in_sum{p}")
        else:
            gt_wi_gla = _chip_sum_share(pair, landed, slots, r_wi_g, part=(p, 2), name="g_gla_in_sum_share", cid=14)
    u_wi_gla_t = _adamw(wt_in_g, gt_wi_gla, mt_in_g, vt_in_g, name="adamw_gla_w_in")
    u_wi_gla = [u.T[None] for u in u_wi_gla_t]
    u_wo_sgu = big_update(sgu_w_out, g_wo_sgu, m_sgu_w_out, v_sgu_w_out, "adamw_sgu_w_out", after=u_wi_gla_t[1])

    def summed_over_devices(part, flight, after, shapes, name):
        land = _dev_gather_wait(flight, after, name=name + "_wait")
        every = lax.dynamic_update_slice(land, part[None], (2 * chip + core, 0, 0))
        return _unpack(_stack_sum(every, name=name + "_sum"), shapes)

    (g_post, g_bg, g_og, g_wsp, g_bsp, g_w2_full, g_lg_full, g_lb_full, loss_vec) = summed_over_devices(
        early_part, early_flight, u_wo_sgu[1], early_shapes, "small_early")
    g_pre, = summed_over_devices(late_part, late_flight, loss_vec, [norm_pre.shape], "small_late")
    loss = loss_vec[0, 0]
    g_w2 = lax.dynamic_slice_in_dim(g_w2_full, chip * (dk // N_CHIPS), dk // N_CHIPS, axis=2)
    g_lg = lax.dynamic_slice_in_dim(g_lg_full, chip * (d // N_CHIPS), d // N_CHIPS, axis=1)
    g_lb = lax.dynamic_slice_in_dim(g_lb_full, chip * (d // N_CHIPS), d // N_CHIPS, axis=1)

    small_w = [norm_pre, norm_post, gla_b_gate, gla_o_gain, sgu_w_spatial, sgu_b_spatial, gla_w_gate2, sgu_ln_gain,
               sgu_ln_bias]
    small_g = [g_pre, g_post, g_bg, g_og, g_wsp, g_bsp, g_w2, g_lg, g_lb]
    small_m = [m_norm_pre, m_norm_post, m_gla_b_gate, m_gla_o_gain, m_sgu_w_spatial, m_sgu_b_spatial, m_gla_w_gate2,
               m_sgu_ln_gain, m_sgu_ln_bias]
    small_v = [v_norm_pre, v_norm_post, v_gla_b_gate, v_gla_o_gain, v_sgu_w_spatial, v_sgu_b_spatial, v_gla_w_gate2,
               v_sgu_ln_gain, v_sgu_ln_bias]
    own_shapes = [w.shape for w in small_w]
    _, s_dl, s_m, s_v = _adamw(_pack(small_w), _pack(small_g), _pack(small_m), _pack(small_v), name="adamw_small")
    dl_s, m_s, v_s = _unpack(s_dl, own_shapes), _unpack(s_m, own_shapes), _unpack(s_v, own_shapes)

    def ordered(small, kind):
        pre, post, bg, og, wsp, bsp, w2, lg, lb = small
        return [pre, post, u_wi_gla[kind], w2, bg, og, u_wo_gla[kind], u_wi_sgu[kind], lg, lb, wsp, bsp, u_wo_sgu[kind]]

    return (loss, grad_x[None], *ordered(small_g, 0), *ordered(dl_s, 1), *ordered(m_s, 2), *ordered(v_s, 3))
```

```python
import math

import jax
import jax.numpy as jnp
from jax import lax
from jax.experimental import pallas as pl
from jax.experimental.pallas import tpu as pltpu

F32 = jnp.float32
BF16 = jnp.bfloat16
MESH = pl.DeviceIdType.MESH

EPS = 1e-6
CHUNK = 64
GLA_HEADS = 4
GLA_GATE_RANK = 16
GLA_TAU = 16.0
SGU_BLOCK = 128
SGU_GROUPS = 8
N_CHIPS = 4
N_DEV = 8
LANES = 128

ADAM_LR = 0.001
ADAM_B1 = 0.9
ADAM_B2 = 0.999
ADAM_EPS = 1e-08
ADAM_WD = 0.01
ADAM_STEP = 10

VMEM_LIMIT = 56 * 1024 * 1024
ELEMENTWISE_BLOCK_BYTES = 2 << 20


def _cparams(sem=None):
    return pltpu.CompilerParams(dimension_semantics=sem, vmem_limit_bytes=VMEM_LIMIT)


def _pick(n, cap, unit=LANES):
    best = None
    for t in range(unit, min(n, cap) + 1, unit):
        if n % t == 0:
            best = t
    assert best is not None, (n, cap, unit)
    return best


def _dot(a, b, dims):
    return lax.dot_general(a, b, (dims, ((), ())), preferred_element_type=F32)


def _dot_nn(a, b):
    return _dot(a, b, ((1,), (0,)))


def _dot_nt(a, b):
    return _dot(a, b, ((1,), (1,)))


def _dot_tn(a, b):
    return _dot(a, b, ((0,), (0,)))


def _matmul(a, b, *, mode, out_dtype, name, tm=1024, tn=512, b_shards=False, after=None):
    M, K = a.shape
    if b_shards:
        ns, Kb, bc = b.shape
        N, tn = ns * bc, _pick(bc, tn)
        per = bc // tn
        b_spec = pl.BlockSpec((None, K, tn), lambda i, j: (j // per, 0, j % per))
    elif mode == "nt":
        N, Kb = b.shape
        tn = _pick(N, tn)
        b_spec = pl.BlockSpec((tn, K), lambda i, j: (j, 0))
    else:
        Kb, N = b.shape
        tn = _pick(N, tn)
        b_spec = pl.BlockSpec((K, tn), lambda i, j: (0, j))
    assert K == Kb and a.dtype == b.dtype == BF16, (a.shape, b.shape, mode)
    tm = _pick(M, tm)
    dims = ((1,), (1,)) if mode == "nt" else ((1,), (0,))
    extra_specs, extra_args = ([], []) if after is None else ([pl.BlockSpec(memory_space=pl.ANY)], [after])

    def body(a_ref, b_ref, *rest):
        rest[-1][...] = _dot(a_ref[...], b_ref[...], dims).astype(out_dtype)

    return pl.pallas_call(
        body, name=name, grid=(M // tm, N // tn),
        in_specs=[pl.BlockSpec((tm, K), lambda i, j: (i, 0)), b_spec] + extra_specs,
        out_specs=pl.BlockSpec((tm, tn), lambda i, j: (i, j)), out_shape=jax.ShapeDtypeStruct((M, N), out_dtype),
        compiler_params=_cparams(("parallel", "parallel")),
    )(a, b, *extra_args)


def _matmul_nt_shards(a, b, *, out_dtype, name, tm=1024, tn=512, after=None):
    M, K = a.shape
    ns, N, kc = b.shape
    assert K == ns * kc
    tm, tn = _pick(M, tm), _pick(N, tn)

    def body(a_ref, *rest):
        b_refs, o_ref = rest[:ns], rest[ns + (after is not None)]
        acc = _dot_nt(a_ref[:, 0:kc], b_refs[0][...])
        for j in range(1, ns):
            acc += _dot_nt(a_ref[:, j * kc:(j + 1) * kc], b_refs[j][...])
        o_ref[...] = acc.astype(out_dtype)

    def shard(j):
        return pl.BlockSpec((None, tn, kc), lambda i, n: (j, n, 0))

    extra_specs, extra_args = ([], []) if after is None else ([pl.BlockSpec(memory_space=pl.ANY)], [after])
    return pl.pallas_call(
        body, name=name, grid=(M // tm, N // tn),
        in_specs=[pl.BlockSpec((tm, K), lambda i, n: (i, 0))] + [shard(j) for j in range(ns)] + extra_specs,
        out_specs=pl.BlockSpec((tm, tn), lambda i, n: (i, n)), out_shape=jax.ShapeDtypeStruct((M, N), out_dtype),
        compiler_params=_cparams(("parallel", "parallel")),
    )(a, *([b] * ns), *extra_args)


def _rstd(x):
    return lax.rsqrt(jnp.mean(x * x, axis=-1, keepdims=True) + EPS)


def _row_spec(tr, d):
    return pl.BlockSpec((tr, d), lambda i: (i, 0))


def _vec_spec(d):
    return pl.BlockSpec((1, d), lambda i: (0, 0))


def _acc_rows(ref, i, val, cols=slice(None)):
    @pl.when(i == 0)
    def _():
        ref[:, cols] = val

    @pl.when(i > 0)
    def _():
        ref[:, cols] += val


def _norm_pre(x, gain, *, name, tr=256):
    t, d = x.shape
    tr = _pick(t, tr, 8)

    def body(x_ref, g_ref, h_ref):
        xv = x_ref[...]
        h_ref[...] = (xv * _rstd(xv) * g_ref[...]).astype(BF16)

    return pl.pallas_call(
        body, name=name, grid=(t // tr,), in_specs=[_row_spec(tr, d), _vec_spec(d)], out_specs=_row_spec(tr, d),
        out_shape=jax.ShapeDtypeStruct((t, d), BF16), compiler_params=_cparams(("parallel",)),
    )(x, gain)


def _post_then_pre(x, y, post_gain, pre_gain, *, name, tr=256):
    t, d = x.shape
    tr = _pick(t, tr, 8)

    def body(x_ref, y_ref, pg_ref, ng_ref, xn_ref, h_ref):
        yv = y_ref[...]
        xn = x_ref[...] + yv * _rstd(yv) * pg_ref[...]
        xn_ref[...] = xn
        h_ref[...] = (xn * _rstd(xn) * ng_ref[...]).astype(BF16)

    return pl.pallas_call(
        body, name=name, grid=(t // tr,),
        in_specs=[_row_spec(tr, d), _row_spec(tr, d), _vec_spec(d), _vec_spec(d)],
        out_specs=[_row_spec(tr, d), _row_spec(tr, d)],
        out_shape=[jax.ShapeDtypeStruct((t, d), F32), jax.ShapeDtypeStruct((t, d), BF16)],
        compiler_params=_cparams(("parallel",)),
    )(x, y, post_gain, pre_gain)


def _norm_bwd(dy, n, r, gain):
    dn = dy * gain
    return r * (dn - n * jnp.mean(dn * n, axis=-1, keepdims=True))


def _loss_head(x, y, post_gain, target, *, name, tr=256):
    t, d = x.shape
    tr = _pick(t, tr, 8)

    def body(x_ref, y_ref, pg_ref, t_ref, loss_ref, dx_ref, dy_ref, dpg_ref):
        i = pl.program_id(0)
        yv = y_ref[...]
        r = _rstd(yv)
        n = yv * r
        err = x_ref[...] + n * pg_ref[...] - t_ref[...]
        dx = err * (1.0 / d)
        dx_ref[...] = dx
        part = 0.5 * jnp.sum(jnp.mean(err * err, axis=-1, keepdims=True), axis=0, keepdims=True)
        _acc_rows(loss_ref, i, jnp.broadcast_to(part, (1, LANES)))
        _acc_rows(dpg_ref, i, jnp.sum(dx * n, axis=0, keepdims=True))
        dy_ref[...] = _norm_bwd(dx, n, r, pg_ref[...]).astype(BF16)

    return pl.pallas_call(
        body, name=name, grid=(t // tr,),
        in_specs=[_row_spec(tr, d), _row_spec(tr, d), _vec_spec(d), _row_spec(tr, d)],
        out_specs=[_vec_spec(LANES), _row_spec(tr, d), _row_spec(tr, d), _vec_spec(d)],
        out_shape=[jax.ShapeDtypeStruct((1, LANES), F32), jax.ShapeDtypeStruct((t, d), F32),
                   jax.ShapeDtypeStruct((t, d), BF16), jax.ShapeDtypeStruct((1, d), F32)],
        compiler_params=_cparams(("arbitrary",)),
    )(x, y, post_gain, target)


def _mid_bwd(dx_out, dh, x, pre_gain, y_prev, post_gain_prev, *, name, tr=256):
    t, d = x.shape
    tr = _pick(t, tr, 8)

    def body(dxo_ref, dh_ref, x_ref, ng_ref, y_ref, pg_ref, dx_ref, dy_ref, dng_ref, dpg_ref):
        i = pl.program_id(0)
        xv = x_ref[...]
        r = _rstd(xv)
        xh = xv * r
        dhv = dh_ref[...]
        _acc_rows(dng_ref, i, jnp.sum(dhv * xh, axis=0, keepdims=True))
        dx = dxo_ref[...] + _norm_bwd(dhv, xh, r, ng_ref[...])
        dx_ref[...] = dx
        yv = y_ref[...]
        ry = _rstd(yv)
        n = yv * ry
        _acc_rows(dpg_ref, i, jnp.sum(dx * n, axis=0, keepdims=True))
        dy_ref[...] = _norm_bwd(dx, n, ry, pg_ref[...]).astype(BF16)

    return pl.pallas_call(
        body, name=name, grid=(t // tr,),
        in_specs=[_row_spec(tr, d), _row_spec(tr, d), _row_spec(tr, d), _vec_spec(d), _row_spec(tr, d), _vec_spec(d)],
        out_specs=[_row_spec(tr, d), _row_spec(tr, d), _vec_spec(d), _vec_spec(d)],
        out_shape=[jax.ShapeDtypeStruct((t, d), F32), jax.ShapeDtypeStruct((t, d), BF16),
                   jax.ShapeDtypeStruct((1, d), F32), jax.ShapeDtypeStruct((1, d), F32)],
        compiler_params=_cparams(("arbitrary",)),
    )(dx_out, dh, x, pre_gain, y_prev, post_gain_prev)


def _first_bwd(dx_out, dh, x, pre_gain, *, name, tr=256):
    t, d = x.shape
    tr = _pick(t, tr, 8)

    def body(dxo_ref, dh_ref, x_ref, ng_ref, dx_ref, dng_ref):
        i = pl.program_id(0)
        xv = x_ref[...]
        r = _rstd(xv)
        xh = xv * r
        dhv = dh_ref[...]
        _acc_rows(dng_ref, i, jnp.sum(dhv * xh, axis=0, keepdims=True))
        dx_ref[...] = dxo_ref[...] + _norm_bwd(dhv, xh, r, ng_ref[...])

    return pl.pallas_call(
        body, name=name, grid=(t // tr,),
        in_specs=[_row_spec(tr, d), _row_spec(tr, d), _row_spec(tr, d), _vec_spec(d)],
        out_specs=[_row_spec(tr, d), _vec_spec(d)],
        out_shape=[jax.ShapeDtypeStruct((t, d), F32), jax.ShapeDtypeStruct((1, d), F32)],
        compiler_params=_cparams(("arbitrary",)),
    )(dx_out, dh, x, pre_gain)


def _sigmoid(x):
    return 1.0 / (1.0 + jnp.exp(-x))


def _log_sigmoid(x):
    return jnp.minimum(x, 0.0) - jnp.log(1.0 + jnp.exp(-jnp.abs(x)))


_GELU_C = math.sqrt(2.0 / math.pi)


_GELU_A = 0.044715


def _gelu_parts(x, with_grad=True):
    x2 = x * x
    h = 0.5 * jnp.tanh(x * (_GELU_C + (_GELU_C * _GELU_A) * x2)) + 0.5
    val = x * h
    if not with_grad:
        return val, None
    return val, h * (1.0 + (1.0 - h) * (x * (2.0 * _GELU_C + (6.0 * _GELU_C * _GELU_A) * x2)))


def _split3(x):
    hi = x.astype(BF16)
    r1 = x - hi.astype(F32)
    mid = r1.astype(BF16)
    lo = (r1 - mid.astype(F32)).astype(BF16)
    return hi, mid, lo


def _tri_matmul(tri_bf16, x):
    hi, mid, lo = _split3(x)
    return _dot_nn(tri_bf16, hi) + _dot_nn(tri_bf16, mid) + _dot_nn(tri_bf16, lo)


def _gla_dims(d):
    dk, dv = d // 2, d
    return dk, dv, dk // GLA_HEADS, dv // GLA_HEADS


def _col_pieces(a, b, lay):
    ws, wp = lay
    out = []
    while a < b:
        j = a // ws
        end = min(b, (j + 1) * ws)
        out.append((j * wp + a - j * ws, end - a))
        a = end
    return out


def _load_cols(ref, a, b, lay):
    parts = [ref[:, s:s + n] for s, n in _col_pieces(a, b, lay)]
    return parts[0] if len(parts) == 1 else jnp.concatenate(parts, axis=1)


def _store_cols(ref, a, val, lay):
    off = 0
    for s, n in _col_pieces(a, a + val.shape[1], lay):
        ref[:, s:s + n] = val[:, off:off + n]
        off += n


def _gate_window(c_r, lay):
    (start, _), = _col_pieces(c_r, c_r + GLA_GATE_RANK, lay)
    assert (start % lay[1]) + LANES <= lay[1]
    return slice(start, start + LANES)


def _gla_gates(glr, k, w2_ref, b_ref):
    z = _dot_nn(glr.astype(BF16), w2_ref[...].astype(BF16)) + b_ref[...]
    la = _log_sigmoid(z) * (1.0 / GLA_TAU)
    row = lax.broadcasted_iota(jnp.int32, (CHUNK, CHUNK), 0)
    col = lax.broadcasted_iota(jnp.int32, (CHUNK, CHUNK), 1)
    incl = (row >= col).astype(BF16)
    bcum = _tri_matmul(incl, la)
    b_end = bcum[CHUNK - 1:CHUNK, :]
    e_rest = jnp.exp(b_end - bcum)
    return z, e_rest, k * e_rest, jnp.exp(b_end)


def _gla_fwd(proj, w2p, b_gate, o_gain, lay, *, name):
    t, wcols = proj.shape
    d = o_gain.shape[1]
    dk, dv, dkh, dvh = _gla_dims(d)
    nc = t // CHUNK
    c_k, c_v, c_g, c_r = dk, 2 * dk, 2 * dk + dv, 2 * dk + 2 * dv
    scale = dkh ** -0.5

    def body(p_ref, w2_ref, b_ref, og_ref, o_ref, a_ref, sb_ref, sfin_ref, s_ref):
        i = pl.program_id(0)

        @pl.when(i == 0)
        def _():
            s_ref[...] = jnp.zeros_like(s_ref)

        q = _load_cols(p_ref, 0, dk, lay) * scale
        k = _load_cols(p_ref, c_k, c_k + dk, lay)
        glr = p_ref[:, _gate_window(c_r, lay)]
        _, _, kdec, decay = _gla_gates(glr, k, w2_ref, b_ref)
        for h in range(GLA_HEADS):
            ks = slice(h * dkh, (h + 1) * dkh)
            vs = slice(h * dvh, (h + 1) * dvh)
            v_h = _load_cols(p_ref, c_v + h * dvh, c_v + (h + 1) * dvh, lay)
            g_h = _load_cols(p_ref, c_g + h * dvh, c_g + (h + 1) * dvh, lay)
            s_old = s_ref[h]
            sb_ref[0, h] = s_old
            s_new = s_old * decay[:, ks] + _dot_tn(v_h.astype(BF16), kdec[:, ks].astype(BF16))
            s_ref[h] = s_new
            o_h = _dot_nt(q[:, ks].astype(BF16), s_new.astype(BF16))
            o_ref[:, vs] = o_h
            on = o_h * _rstd(o_h)
            a_ref[:, vs] = (on * og_ref[:, vs] * (g_h * _sigmoid(g_h))).astype(BF16)

        @pl.when(i == nc - 1)
        def _():
            sfin_ref[...] = s_ref[...]

    full = lambda *shape: pl.BlockSpec(shape, lambda i: (0,) * len(shape))
    return pl.pallas_call(
        body, name=name, grid=(nc,),
        in_specs=[pl.BlockSpec((CHUNK, wcols), lambda i: (i, 0)), full(LANES, dk), full(1, dk), full(1, dv)],
        out_specs=[pl.BlockSpec((CHUNK, dv), lambda i: (i, 0)), pl.BlockSpec((CHUNK, dv), lambda i: (i, 0)),
                   pl.BlockSpec((1, GLA_HEADS, dvh, dkh), lambda i: (i, 0, 0, 0)), full(GLA_HEADS, dvh, dkh)],
        out_shape=[jax.ShapeDtypeStruct((t, dv), F32), jax.ShapeDtypeStruct((t, dv), BF16),
                   jax.ShapeDtypeStruct((nc, GLA_HEADS, dvh, dkh), F32),
                   jax.ShapeDtypeStruct((GLA_HEADS, dvh, dkh), F32)],
        scratch_shapes=[pltpu.VMEM((GLA_HEADS, dvh, dkh), F32)],
        compiler_params=_cparams(("arbitrary",)),
    )(proj, w2p, b_gate, o_gain)


def _gla_bwd(da, o, proj, w2p, b_gate, o_gain, s_before, s_final, lay, *, name):
    t, wcols = proj.shape
    d = o_gain.shape[1]
    dk, dv, dkh, dvh = _gla_dims(d)
    nc = t // CHUNK
    c_k, c_v, c_g, c_r = dk, 2 * dk, 2 * dk + dv, 2 * dk + 2 * dv
    scale = dkh ** -0.5

    def body(da_ref, o_ref, p_ref, w2_ref, b_ref, og_ref, sb_ref, sfin_ref,
             dp_ref, dog_ref, db_ref, dw2_ref, s_ref, gc_ref, dkd_ref):
        i = pl.program_id(0)

        @pl.when(i == 0)
        def _():
            s_ref[...] = sfin_ref[...]
            gc_ref[...] = jnp.zeros_like(gc_ref)

        ws, wp = lay
        for j in range(N_CHIPS):
            dp_ref[:, j * wp + ws:(j + 1) * wp] = jnp.zeros((CHUNK, wp - ws), BF16)
        q = _load_cols(p_ref, 0, dk, lay) * scale
        k = _load_cols(p_ref, c_k, c_k + dk, lay)
        glr = p_ref[:, _gate_window(c_r, lay)]
        z, e_rest, kdec, decay = _gla_gates(glr, k, w2_ref, b_ref)
        ddecay = []
        for h in range(GLA_HEADS):
            ks = slice(h * dkh, (h + 1) * dkh)
            vs = slice(h * dvh, (h + 1) * dvh)
            v_h = _load_cols(p_ref, c_v + h * dvh, c_v + (h + 1) * dvh, lay)
            g_h = _load_cols(p_ref, c_g + h * dvh, c_g + (h + 1) * dvh, lay)
            da_h = da_ref[:, vs]
            o_h = o_ref[:, vs]
            og_h = og_ref[:, vs]
            r = _rstd(o_h)
            on = o_h * r
            sg = _sigmoid(g_h)
            silu = g_h * sg
            _acc_rows(dog_ref, i, jnp.sum(da_h * silu * on, axis=0, keepdims=True), vs)
            _store_cols(dp_ref, c_g + h * dvh, (da_h * (on * og_h) * (sg * (1.0 + g_h * (1.0 - sg)))).astype(BF16),
                        lay)
            don = da_h * silu * og_h
            do_h = (r * (don - on * jnp.mean(don * on, axis=-1, keepdims=True))).astype(BF16)
            s_cur = s_ref[h]
            _store_cols(dp_ref, h * dkh, (_dot_nn(do_h, s_cur.astype(BF16)) * scale).astype(BF16), lay)
            g_tot = gc_ref[h] + _dot_tn(do_h, q[:, ks].astype(BF16))
            g_bf = g_tot.astype(BF16)
            dkd_ref[:, ks] = _dot_nn(v_h.astype(BF16), g_bf)
            _store_cols(dp_ref, c_v + h * dvh, _dot_nt(kdec[:, ks].astype(BF16), g_bf).astype(BF16), lay)
            s_prev = sb_ref[0, h]
            ddecay.append(jnp.sum(g_tot * s_prev, axis=0, keepdims=True))
            gc_ref[h] = g_tot * decay[:, ks]
            s_ref[h] = s_prev
        dkdec = dkd_ref[...]
        _store_cols(dp_ref, c_k, (dkdec * e_rest).astype(BF16), lay)
        d_e = dkdec * kdec
        row = lax.broadcasted_iota(jnp.int32, (CHUNK, CHUNK), 0)
        col = lax.broadcasted_iota(jnp.int32, (CHUNK, CHUNK), 1)
        excl = (row > col).astype(BF16)
        dla = jnp.concatenate(ddecay, axis=1) * decay + _tri_matmul(excl, d_e)
        dz = dla * (1.0 / GLA_TAU) * (1.0 - _sigmoid(z))
        _acc_rows(db_ref, i, jnp.sum(dz, axis=0, keepdims=True))
        dz_bf = dz.astype(BF16)
        dw2 = _dot_tn(glr.astype(BF16), dz_bf)

        @pl.when(i == 0)
        def _():
            dw2_ref[...] = dw2

        @pl.when(i > 0)
        def _():
            dw2_ref[...] += dw2

        dp_ref[:, _gate_window(c_r, lay)] = _dot_nt(dz_bf, w2_ref[...].astype(BF16)).astype(BF16)

    rev = lambda i: (nc - 1 - i, 0)
    full = lambda *shape: pl.BlockSpec(shape, lambda i: (0,) * len(shape))
    return pl.pallas_call(
        body, name=name, grid=(nc,),
        in_specs=[pl.BlockSpec((CHUNK, dv), rev), pl.BlockSpec((CHUNK, dv), rev), pl.BlockSpec((CHUNK, wcols), rev),
                  full(LANES, dk), full(1, dk), full(1, dv),
                  pl.BlockSpec((1, GLA_HEADS, dvh, dkh), lambda i: (nc - 1 - i, 0, 0, 0)), full(GLA_HEADS, dvh, dkh)],
        out_specs=[pl.BlockSpec((CHUNK, wcols), rev), full(1, dv), full(1, dk), full(LANES, dk)],
        out_shape=[jax.ShapeDtypeStruct((t, wcols), BF16), jax.ShapeDtypeStruct((1, dv), F32),
                   jax.ShapeDtypeStruct((1, dk), F32), jax.ShapeDtypeStruct((LANES, dk), F32)],
        scratch_shapes=[pltpu.VMEM((GLA_HEADS, dvh, dkh), F32), pltpu.VMEM((GLA_HEADS, dvh, dkh), F32),
                        pltpu.VMEM((CHUNK, dk), F32)],
        compiler_params=_cparams(("arbitrary",)),
    )(da, o, proj, w2p, b_gate, o_gain, s_before, s_final)


def _sgu_mid(p_ref, lg_ref, lb_ref, ws_ref, bst_ref, w, with_grad=True):
    gd = w // SGU_GROUPS
    u_act, du_fac = _gelu_parts(p_ref[:, 0:w], with_grad)
    vf, dv_fac = _gelu_parts(p_ref[:, w:2 * w], with_grad)
    mu = jnp.mean(vf, axis=-1, keepdims=True)
    cen = vf - mu
    rstd = lax.rsqrt(jnp.mean(cen * cen, axis=-1, keepdims=True) + EPS)
    xh = cen * rstd
    vn = (xh * lg_ref[...] + lb_ref[...]).astype(BF16)
    vs = [_dot_nn(ws_ref[g].astype(BF16), vn[:, g * gd:(g + 1) * gd]) + bst_ref[:, g:g + 1]
          for g in range(SGU_GROUPS)]
    return u_act, du_fac, dv_fac, rstd, xh, vn, vs


def _sgu_fwd(proj, ln_gain, ln_bias, ws_masked, bs_t, *, name):
    t, w3 = proj.shape
    w = w3 // 3
    gd = w // SGU_GROUPS
    nb = t // SGU_BLOCK

    def body(p_ref, lg_ref, lb_ref, ws_ref, bst_ref, a_ref):
        u_act, _, _, _, _, _, vs = _sgu_mid(p_ref, lg_ref, lb_ref, ws_ref, bst_ref, w, with_grad=False)
        for g in range(SGU_GROUPS):
            cs = slice(g * gd, (g + 1) * gd)
            gate = p_ref[:, 2 * w + g * gd:2 * w + (g + 1) * gd]
            a_ref[:, cs] = (u_act[:, cs] * vs[g] * (gate * _sigmoid(gate))).astype(BF16)

    full = lambda *shape: pl.BlockSpec(shape, lambda i: (0,) * len(shape))
    return pl.pallas_call(
        body, name=name, grid=(nb,),
        in_specs=[pl.BlockSpec((SGU_BLOCK, w3), lambda i: (i, 0)), full(1, w), full(1, w),
                  full(SGU_GROUPS, SGU_BLOCK, SGU_BLOCK), full(SGU_BLOCK, SGU_GROUPS)],
        out_specs=pl.BlockSpec((SGU_BLOCK, w), lambda i: (i, 0)),
        out_shape=jax.ShapeDtypeStruct((t, w), BF16),
        compiler_params=_cparams(("parallel",)),
    )(proj, ln_gain, ln_bias, ws_masked, bs_t)


def _sgu_bwd(da, proj, ln_gain, ln_bias, ws_masked, ws_masked_t, bs_t, *, name):
    t, w3 = proj.shape
    w = w3 // 3
    gd = w // SGU_GROUPS
    nb = t // SGU_BLOCK

    def body(da_ref, p_ref, lg_ref, lb_ref, ws_ref, wst_ref, bst_ref, dp_ref, dws_ref, dbst_ref, dlg_ref, dlb_ref,
             dvn_ref):
        i = pl.program_id(0)
        u_act, du_fac, dv_fac, rstd, xh, vn, vs = _sgu_mid(p_ref, lg_ref, lb_ref, ws_ref, bst_ref, w)
        for g in range(SGU_GROUPS):
            cs = slice(g * gd, (g + 1) * gd)
            gate = p_ref[:, 2 * w + g * gd:2 * w + (g + 1) * gd]
            sg = _sigmoid(gate)
            silu = gate * sg
            da_g = da_ref[:, cs]
            ua_g = u_act[:, cs]
            dp_ref[:, cs] = (da_g * vs[g] * silu * du_fac[:, cs]).astype(BF16)
            dp_ref[:, 2 * w + g * gd:2 * w + (g + 1) * gd] = (
                da_g * ua_g * vs[g] * (sg * (1.0 + gate * (1.0 - sg)))).astype(BF16)
            dvs = da_g * ua_g * silu
            dvs_bf = dvs.astype(BF16)
            dvn_ref[:, cs] = _dot_nn(wst_ref[g].astype(BF16), dvs_bf)
            dws = _dot_nt(dvs_bf, vn[:, cs])
            dbs = jnp.sum(dvs, axis=1, keepdims=True)

            @pl.when(i == 0)
            def _():
                dws_ref[g] = dws
                dbst_ref[:, g:g + 1] = dbs

            @pl.when(i > 0)
            def _():
                dws_ref[g] += dws
                dbst_ref[:, g:g + 1] += dbs

        dvn = dvn_ref[...]
        _acc_rows(dlg_ref, i, jnp.sum(dvn * xh, axis=0, keepdims=True))
        _acc_rows(dlb_ref, i, jnp.sum(dvn, axis=0, keepdims=True))
        dxh = dvn * lg_ref[...]
        dvf = rstd * (dxh - jnp.mean(dxh, axis=-1, keepdims=True)
                      - xh * jnp.mean(dxh * xh, axis=-1, keepdims=True))
        dp_ref[:, w:2 * w] = (dvf * dv_fac).astype(BF16)

    full = lambda *shape: pl.BlockSpec(shape, lambda i: (0,) * len(shape))
    return pl.pallas_call(
        body, name=name, grid=(nb,),
        in_specs=[pl.BlockSpec((SGU_BLOCK, w), lambda i: (i, 0)), pl.BlockSpec((SGU_BLOCK, w3), lambda i: (i, 0)),
                  full(1, w), full(1, w), full(SGU_GROUPS, SGU_BLOCK, SGU_BLOCK),
                  full(SGU_GROUPS, SGU_BLOCK, SGU_BLOCK), full(SGU_BLOCK, SGU_GROUPS)],
        out_specs=[pl.BlockSpec((SGU_BLOCK, w3), lambda i: (i, 0)), full(SGU_GROUPS, SGU_BLOCK, SGU_BLOCK),
                   full(SGU_BLOCK, SGU_GROUPS), full(1, w), full(1, w)],
        out_shape=[jax.ShapeDtypeStruct((t, w3), BF16), jax.ShapeDtypeStruct((SGU_GROUPS, SGU_BLOCK, SGU_BLOCK), F32),
                   jax.ShapeDtypeStruct((SGU_BLOCK, SGU_GROUPS), F32), jax.ShapeDtypeStruct((1, w), F32),
                   jax.ShapeDtypeStruct((1, w), F32)],
        scratch_shapes=[pltpu.VMEM((SGU_BLOCK, w), F32)],
        compiler_params=_cparams(("arbitrary",)),
    )(da, proj, ln_gain, ln_bias, ws_masked, ws_masked_t, bs_t)


def _tile2d(rows, cols, block_bytes, row_unit):
    if rows % row_unit == 0:
        return _pick(rows, max(row_unit, block_bytes // (4 * cols)), row_unit), cols
    return rows, _pick(cols, max(LANES, block_bytes // (4 * rows)))


def _adamw(w, g, m, v, *, name, block_bytes=ELEMENTWISE_BLOCK_BYTES, after=None):
    rows, cols = w.shape
    tr, tc = _tile2d(rows, cols, block_bytes, 8)
    g_rows = g.shape[0]
    assert g_rows == rows or tr == rows
    extra_specs, extra_args = ([], []) if after is None else ([pl.BlockSpec(memory_space=pl.ANY)], [after])

    def body(w_ref, g_ref, m_ref, v_ref, *rest):
        go_ref, d_ref, mo_ref, vo_ref = rest[len(extra_args):]
        gv = g_ref[0:tr, :]
        go_ref[...] = gv
        mn = ADAM_B1 * m_ref[...] + (1.0 - ADAM_B1) * gv
        vn = ADAM_B2 * v_ref[...] + (1.0 - ADAM_B2) * (gv * gv)
        m_hat = mn / (1.0 - ADAM_B1 ** ADAM_STEP)
        v_hat = vn / (1.0 - ADAM_B2 ** ADAM_STEP)
        d_ref[...] = -ADAM_LR * (m_hat / (jnp.sqrt(v_hat) + ADAM_EPS) + ADAM_WD * w_ref[...])
        mo_ref[...] = mn
        vo_ref[...] = vn

    spec = pl.BlockSpec((tr, tc), lambda i, j: (i, j))
    g_spec = spec if g_rows == rows else pl.BlockSpec((g_rows, tc), lambda i, j: (0, j))
    return pl.pallas_call(
        body, name=name, grid=(rows // tr, cols // tc), in_specs=[spec, g_spec, spec, spec] + extra_specs,
        out_specs=[spec] * 4, out_shape=[jax.ShapeDtypeStruct((rows, cols), F32)] * 4,
        compiler_params=_cparams(("parallel", "parallel")),
    )(w, g, m, v, *extra_args)


def _matmul_dw_pair(a_me, a_sib, b_me, b_sib, core_idx, *, shards_on, name, after=None, part=(0, 1)):
    T, M = a_me.shape
    N = b_me.shape[1]
    if shards_on == "rows":
        p, count = part
        tm, hc = M // N_CHIPS, N // 2
        hp = hc // count
        tn = _pick(hp, 1024)
        per = hp // tn
        grid = (N_CHIPS, per)
        a_spec = pl.BlockSpec((T, tm), lambda i, n, h: (0, i))
        b_me_spec = pl.BlockSpec((T, tn), lambda i, n, h: (0, (h[0] * count + p) * per + n))
        b_sib_spec = pl.BlockSpec((T, tn), lambda i, n, h: (0, p * per + n))
        out_spec = pl.BlockSpec((None, tm, tn), lambda i, n, h: (i, 0, n))
        out_shape = jax.ShapeDtypeStruct((N_CHIPS, tm, hp), BF16)
    else:
        tm, hc = _pick(M, 1024), N // N_CHIPS // 2
        grid = (M // tm, N_CHIPS)
        a_spec = pl.BlockSpec((T, tm), lambda i, j, h: (0, i))
        b_me_spec = pl.BlockSpec((T, hc), lambda i, j, h: (0, 2 * j + h[0]))
        b_sib_spec = pl.BlockSpec((T, hc), lambda i, j, h: (0, j))
        out_spec = pl.BlockSpec((None, tm, hc), lambda i, j, h: (j, i, 0))
        out_shape = jax.ShapeDtypeStruct((N_CHIPS, M, hc), BF16)
    extra_specs, extra_args = ([], []) if after is None else ([pl.BlockSpec(memory_space=pl.ANY)], [after])

    def body(h_ref, am_ref, as_ref, bm_ref, bs_ref, *rest):
        o_ref = rest[len(extra_args)]
        o_ref[...] = (_dot_tn(am_ref[...], bm_ref[...]) + _dot_tn(as_ref[...], bs_ref[...])).astype(BF16)

    grid_spec = pltpu.PrefetchScalarGridSpec(
        num_scalar_prefetch=1, grid=grid, in_specs=[a_spec, a_spec, b_me_spec, b_sib_spec] + extra_specs,
        out_specs=out_spec)
    return pl.pallas_call(
        body, name=name, grid_spec=grid_spec, out_shape=out_shape, compiler_params=_cparams(("parallel", "parallel")),
    )(core_idx, a_me, a_sib, b_me, b_sib, *extra_args)


def _chip_sum(pair, landed, slots, *, name, block_bytes=ELEMENTWISE_BLOCK_BYTES, part=(0, 1), into=None):
    p, count = part
    _, r, hp = pair.shape
    tr, tc = _tile2d(r, hp, block_bytes, 16)
    ncb = hp // tc
    extra_specs, extra_args = ([], []) if into is None else ([pl.BlockSpec(memory_space=pl.ANY)], [into])

    def body(s_ref, own_ref, l0_ref, l1_ref, l2_ref, *rest):
        rest[-1][...] = ((own_ref[...].astype(F32) + l0_ref[...].astype(F32)) + l1_ref[...].astype(F32)
                         ) + l2_ref[...].astype(F32)

    def slab(which):
        return pl.BlockSpec((None, tr, tc), lambda i, k, s: (s[which], i, k))

    grid_spec = pltpu.PrefetchScalarGridSpec(
        num_scalar_prefetch=1, grid=(r // tr, ncb),
        in_specs=[slab(0), slab(1), slab(2), slab(3)] + extra_specs,
        out_specs=pl.BlockSpec((tr, tc), lambda i, k, s: (i, (s[4] * count + p) * ncb + k)))
    return pl.pallas_call(
        body, name=name, grid_spec=grid_spec, out_shape=jax.ShapeDtypeStruct((r, 2 * hp * count), F32),
        input_output_aliases={} if into is None else {5: 0},
        compiler_params=_cparams(("parallel", "parallel")),
    )(slots, pair, landed, landed, landed, *extra_args)


def _stack_sum(x, *, name, out_dtype=F32, block_bytes=ELEMENTWISE_BLOCK_BYTES):
    s, r, c = x.shape
    tr = _pick(r, max(8, block_bytes // (4 * c)), 16) if r % 16 == 0 else r

    def body(x_ref, o_ref):
        acc = x_ref[0].astype(F32)
        for j in range(1, s):
            acc = acc + x_ref[j].astype(F32)
        o_ref[...] = acc.astype(out_dtype)

    return pl.pallas_call(
        body, name=name, grid=(r // tr,),
        in_specs=[pl.BlockSpec((s, tr, c), lambda i: (0, i, 0))], out_specs=pl.BlockSpec((tr, c), lambda i: (i, 0)),
        out_shape=jax.ShapeDtypeStruct((r, c), out_dtype), compiler_params=_cparams(("parallel",)),
    )(x)


HBM = pl.BlockSpec(memory_space=pltpu.HBM)


def _place():
    x, y, c = lax.axis_index("x"), lax.axis_index("y"), lax.axis_index("c")
    other_chips = [(1 - x, y), (x, 1 - y), (1 - x, 1 - y)]
    return x, y, c, other_chips


def _handshake(peers):
    barrier = pltpu.get_barrier_semaphore()
    for peer in peers:
        pl.semaphore_signal(barrier, inc=1, device_id=peer, device_id_type=MESH)
    pl.semaphore_wait(barrier, len(peers))


def _sibling():
    x, y, c, _ = _place()
    return [(x, y, 1 - c)]


def _same_core_chips():
    x, y, c, chips = _place()
    return [(cx, cy, c) for cx, cy in chips]


def _same_core_neighbours():
    x, y, c, _ = _place()
    return [(1 - x, y, c), (x, 1 - y, c)]


def _split_params(cid):
    return pltpu.CompilerParams(has_side_effects=SIDE_EFFECT, collective_id=cid)


def _half_cols(cols, which):
    hc = cols // 2
    return pl.ds(pl.multiple_of(which * hc, LANES), hc)


SEM = pl.BlockSpec(memory_space=pltpu.SEMAPHORE)
ANY = pl.BlockSpec(memory_space=pl.ANY)
SIDE_EFFECT = pltpu.SideEffectType.DATAFLOW_SIDE_EFFECTING
TOKEN_SHAPE = (8, LANES)


def _hbm(shape, dtype):
    return pltpu.HBM(shape, dtype)


def _in_hbm(a):
    return pltpu.with_memory_space_constraint(a, pltpu.HBM)


def _gather_copy(src_ref, land_ref, ssem, rsem, k, chip_of_block, to, c):
    cols = src_ref.shape[1]
    return pltpu.make_async_remote_copy(
        src_ref=src_ref.at[:, _half_cols(cols, c)], dst_ref=land_ref.at[chip_of_block, :, _half_cols(cols, c)],
        send_sem=ssem.at[k], recv_sem=rsem.at[k], device_id=to, device_id_type=MESH)


NEIGHBOURS = (0, 1)
ALL_CHIPS = (0, 1, 2)


def _gather_start(shards, *, name, cid, after=(), relayed=(), own_slab=None):
    n = len(shards)
    after = list(after)

    def body(*refs):
        srcs, lands = refs[:n], refs[n:2 * n]
        outs = refs[2 * n + len(after):]
        token = outs[-1]
        _handshake(_same_core_chips())
        x, y, c, chips = _place()
        me = 2 * x + y
        for a in range(n):
            ssem, rsem = outs[4 * a], outs[4 * a + 1]
            for k in NEIGHBOURS if a in relayed else ALL_CHIPS:
                cx, cy = chips[k]
                _gather_copy(srcs[a], lands[a], ssem, rsem, k, me, (cx, cy, c), c).start()
        token[...] = jnp.zeros_like(token)

    out_shape, out_specs, aliases = [], [], {}
    for a, s in enumerate(shards):
        out_shape += [pltpu.SemaphoreType.DMA((3,)), pltpu.SemaphoreType.DMA((3,)), _hbm(s.shape, s.dtype),
                      _hbm((N_CHIPS,) + s.shape, s.dtype)]
        out_specs += [SEM, SEM, HBM, HBM]
        aliases[a] = 4 * a + 2
        aliases[n + a] = 4 * a + 3
    out_shape.append(jax.ShapeDtypeStruct(TOKEN_SHAPE, F32))
    out_specs.append(pl.BlockSpec(memory_space=pltpu.VMEM))
    lands = [lax.empty((N_CHIPS,) + s.shape, s.dtype) for s in shards]
    if own_slab is not None:
        lands = [lax.dynamic_update_slice(land, s[None], (own_slab, 0, 0)) for land, s in zip(lands, shards)]
    lands = [_in_hbm(land) for land in lands]
    res = pl.pallas_call(
        body, name=name, in_specs=[HBM] * (2 * n) + [ANY] * len(after), out_specs=out_specs, out_shape=out_shape,
        input_output_aliases=aliases, compiler_params=_split_params(cid),
    )(*[_in_hbm(s) for s in shards], *lands, *after)
    return [tuple(res[4 * a:4 * a + 4]) for a in range(n)], res[-1]


def _wait_call(wait_fn, parts, after, *, name):
    ssem, rsem, src, land = parts
    after = list(after) if isinstance(after, (list, tuple)) else [after]

    def body(src_ref, land_ref, ssem_ref, rsem_ref, *rest):
        wait_fn(src_ref, land_ref, ssem_ref, rsem_ref)

    return pl.pallas_call(
        body, name=name, in_specs=[HBM, HBM, SEM, SEM] + [ANY] * len(after), out_specs=[HBM, HBM],
        out_shape=[_hbm(src.shape, src.dtype), _hbm(land.shape, land.dtype)], input_output_aliases={0: 0, 1: 1},
        compiler_params=pltpu.CompilerParams(has_side_effects=SIDE_EFFECT),
    )(src, land, ssem, rsem, *after)


def _gather_wait(parts, after, *, name, ks=ALL_CHIPS):
    def wait(src_ref, land_ref, ssem_ref, rsem_ref):
        x, y, c, chips = _place()
        for k in ks:
            cx, cy = chips[k]
            cp = _gather_copy(src_ref, land_ref, ssem_ref, rsem_ref, k, 2 * cx + cy, (x, y, c), c)
            cp.wait_send()
            cp.wait_recv()

    return _wait_call(wait, parts, after, name=name)


def _relay_copy(buf_ref, ssem, rsem, k, slab, to, c):
    hr = buf_ref.shape[1] // 2
    part = buf_ref.at[slab, pl.ds(k * hr, hr), _half_cols(buf_ref.shape[2], c)]
    return pltpu.make_async_remote_copy(
        src_ref=part, dst_ref=part, send_sem=ssem.at[k], recv_sem=rsem.at[k], device_id=to, device_id_type=MESH)


def _relay_start(land, *, name, cid):
    def body(buf_ref, ssem, rsem, buf_out, token):
        _handshake(_same_core_neighbours())
        x, y, c, _ = _place()
        _relay_copy(buf_ref, ssem, rsem, 0, 2 * (1 - x) + y, (x, 1 - y, c), c).start()
        _relay_copy(buf_ref, ssem, rsem, 1, 2 * x + 1 - y, (1 - x, y, c), c).start()
        token[...] = jnp.zeros_like(token)

    res = pl.pallas_call(
        body, name=name, in_specs=[HBM], out_specs=[SEM, SEM, HBM, pl.BlockSpec(memory_space=pltpu.VMEM)],
        out_shape=[pltpu.SemaphoreType.DMA((2,)), pltpu.SemaphoreType.DMA((2,)), _hbm(land.shape, land.dtype),
                   jax.ShapeDtypeStruct(TOKEN_SHAPE, F32)],
        input_output_aliases={0: 2}, compiler_params=_split_params(cid),
    )(land)
    return tuple(res[:3]), res[3]


def _relay_wait(parts, after, *, name):
    ssem, rsem, buf = parts
    after = list(after) if isinstance(after, (list, tuple)) else [after]

    def body(buf_ref, ssem_ref, rsem_ref, *rest):
        x, y, c, _ = _place()
        diagonal = 2 * (1 - x) + 1 - y
        _relay_copy(buf_ref, ssem_ref, rsem_ref, 0, 2 * (1 - x) + y, (x, y, c), c).wait_send()
        _relay_copy(buf_ref, ssem_ref, rsem_ref, 1, 2 * x + 1 - y, (x, y, c), c).wait_send()
        _relay_copy(buf_ref, ssem_ref, rsem_ref, 0, diagonal, (x, y, c), c).wait_recv()
        _relay_copy(buf_ref, ssem_ref, rsem_ref, 1, diagonal, (x, y, c), c).wait_recv()

    return pl.pallas_call(
        body, name=name, in_specs=[HBM, SEM, SEM] + [ANY] * len(after), out_specs=HBM,
        out_shape=_hbm(buf.shape, buf.dtype), input_output_aliases={0: 0},
        compiler_params=pltpu.CompilerParams(has_side_effects=SIDE_EFFECT),
    )(buf, ssem, rsem, *after)


def _forward_copy(buf_ref, ssem, rsem, k, slab, which, to):
    part = buf_ref.at[slab, :, _half_cols(buf_ref.shape[2], which)]
    return pltpu.make_async_remote_copy(
        src_ref=part, dst_ref=part, send_sem=ssem.at[k], recv_sem=rsem.at[k], device_id=to, device_id_type=MESH)


def _sibling_forward(land, *, name, cid, ks=ALL_CHIPS):
    def body(_, buf, send_sems, recv_sems):
        _handshake(_sibling())
        x, y, c, chips = _place()
        copies = []
        for k in ks:
            cx, cy = chips[k]
            cp = _forward_copy(buf, send_sems, recv_sems, k, 2 * cx + cy, c, (x, y, 1 - c))
            cp.start()
            copies.append(cp)
        for k in ks:
            cx, cy = chips[k]
            _forward_copy(buf, send_sems, recv_sems, k, 2 * cx + cy, 1 - c, (x, y, c)).wait_recv()
        for cp in copies:
            cp.wait_send()

    return pl.pallas_call(
        body, name=name, in_specs=[HBM], out_specs=HBM, out_shape=jax.ShapeDtypeStruct(land.shape, land.dtype),
        input_output_aliases={0: 0},
        scratch_shapes=[pltpu.SemaphoreType.DMA((3,)), pltpu.SemaphoreType.DMA((3,))],
        compiler_params=pltpu.CompilerParams(collective_id=cid),
    )(land)


def _forward_start(land, *, name, cid, ks=ALL_CHIPS):
    def body(buf_ref, ssem, rsem, buf_out, token):
        _handshake(_sibling())
        x, y, c, chips = _place()
        for k in ks:
            cx, cy = chips[k]
            _forward_copy(buf_ref, ssem, rsem, k, 2 * cx + cy, c, (x, y, 1 - c)).start()
        token[...] = jnp.zeros_like(token)

    res = pl.pallas_call(
        body, name=name, in_specs=[HBM], out_specs=[SEM, SEM, HBM, pl.BlockSpec(memory_space=pltpu.VMEM)],
        out_shape=[pltpu.SemaphoreType.DMA((3,)), pltpu.SemaphoreType.DMA((3,)), _hbm(land.shape, land.dtype),
                   jax.ShapeDtypeStruct(TOKEN_SHAPE, F32)],
        input_output_aliases={0: 2}, compiler_params=_split_params(cid),
    )(land)
    return tuple(res[:3]), res[3]


def _forward_wait(parts, after, *, name, ks=ALL_CHIPS):
    ssem, rsem, buf = parts
    after = list(after) if isinstance(after, (list, tuple)) else [after]

    def body(buf_ref, ssem_ref, rsem_ref, *rest):
        x, y, c, chips = _place()
        for k in ks:
            cx, cy = chips[k]
            _forward_copy(buf_ref, ssem_ref, rsem_ref, k, 2 * cx + cy, c, (x, y, c)).wait_send()
            _forward_copy(buf_ref, ssem_ref, rsem_ref, k, 2 * cx + cy, 1 - c, (x, y, c)).wait_recv()

    return pl.pallas_call(
        body, name=name, in_specs=[HBM, SEM, SEM] + [ANY] * len(after), out_specs=HBM,
        out_shape=_hbm(buf.shape, buf.dtype), input_output_aliases={0: 0},
        compiler_params=pltpu.CompilerParams(has_side_effects=SIDE_EFFECT),
    )(buf, ssem, rsem, *after)


def _share_copy(buf_ref, ssem, rsem, a, which, to):
    part = buf_ref.at[:, _half_cols(buf_ref.shape[1], which)]
    return pltpu.make_async_remote_copy(
        src_ref=part, dst_ref=part, send_sem=ssem.at[a], recv_sem=rsem.at[a], device_id=to, device_id_type=MESH)


def _share_start(arrays, *, name, cid):
    n = len(arrays)

    def body(*refs):
        bufs, ssem, rsem, token = refs[:n], refs[n], refs[n + 1], refs[-1]
        _handshake(_sibling())
        x, y, c, _ = _place()
        for a in range(n):
            _share_copy(bufs[a], ssem, rsem, a, c, (x, y, 1 - c)).start()
        token[...] = jnp.zeros_like(token)

    res = pl.pallas_call(
        body, name=name, in_specs=[HBM] * n,
        out_specs=[SEM, SEM] + [HBM] * n + [pl.BlockSpec(memory_space=pltpu.VMEM)],
        out_shape=[pltpu.SemaphoreType.DMA((n,)), pltpu.SemaphoreType.DMA((n,))]
        + [_hbm(b.shape, b.dtype) for b in arrays] + [jax.ShapeDtypeStruct(TOKEN_SHAPE, F32)],
        input_output_aliases={a: 2 + a for a in range(n)}, compiler_params=_split_params(cid),
    )(*[_in_hbm(b) for b in arrays])
    return (res[0], res[1], list(res[2:2 + n])), res[-1]


def _share_wait(parts, after, *, name):
    ssem, rsem, bufs = parts
    n = len(bufs)
    after = list(after) if isinstance(after, (list, tuple)) else [after]

    def body(*refs):
        buf_refs, ssem_ref, rsem_ref = refs[:n], refs[n], refs[n + 1]
        x, y, c, _ = _place()
        for a in range(n):
            _share_copy(buf_refs[a], ssem_ref, rsem_ref, a, c, (x, y, c)).wait_send()
            _share_copy(buf_refs[a], ssem_ref, rsem_ref, a, 1 - c, (x, y, c)).wait_recv()

    return pl.pallas_call(
        body, name=name, in_specs=[HBM] * n + [SEM, SEM] + [ANY] * len(after), out_specs=[HBM] * n,
        out_shape=[_hbm(b.shape, b.dtype) for b in bufs], input_output_aliases={a: a for a in range(n)},
        compiler_params=pltpu.CompilerParams(has_side_effects=SIDE_EFFECT),
    )(*bufs, ssem, rsem, *after)


def _scatter_copy(src_ref, land_ref, ssem, rsem, k, src_slab, dst_slab, to):
    return pltpu.make_async_remote_copy(
        src_ref=src_ref.at[src_slab], dst_ref=land_ref.at[dst_slab], send_sem=ssem.at[k], recv_sem=rsem.at[k],
        device_id=to, device_id_type=MESH)


def _scatter_start(part, *, name, cid):
    def start(src_ref, land_ref, ssem, rsem):
        x, y, c, chips = _place()
        me = 2 * x + y
        for k, (cx, cy) in enumerate(chips):
            _scatter_copy(src_ref, land_ref, ssem, rsem, k, 2 * cx + cy, me, (cx, cy, c)).start()

    return _split_start(start, _same_core_chips, part, part.shape, N_CHIPS - 1, name=name, cid=cid)


def _scatter_wait(parts, after, *, name):
    def wait(src_ref, land_ref, ssem_ref, rsem_ref):
        x, y, c, chips = _place()
        for k, (cx, cy) in enumerate(chips):
            idx = 2 * cx + cy
            cp = _scatter_copy(src_ref, land_ref, ssem_ref, rsem_ref, k, idx, idx, (x, y, c))
            cp.wait_send()
            cp.wait_recv()

    return _wait_call(wait, parts, after, name=name)


def _split_start(start_fn, peers_fn, src, land_shape, n_sems, *, name, cid):
    def body(src_ref, land_ref, ssem, rsem, src_out, land_out, token):
        _handshake(peers_fn())
        start_fn(src_ref, land_ref, ssem, rsem)
        token[...] = jnp.zeros_like(token)

    res = pl.pallas_call(
        body, name=name, in_specs=[HBM, HBM], out_specs=[SEM, SEM, HBM, HBM, pl.BlockSpec(memory_space=pltpu.VMEM)],
        out_shape=[pltpu.SemaphoreType.DMA((n_sems,)), pltpu.SemaphoreType.DMA((n_sems,)), _hbm(src.shape, src.dtype),
                   _hbm(land_shape, src.dtype), jax.ShapeDtypeStruct(TOKEN_SHAPE, F32)],
        input_output_aliases={0: 2, 1: 3}, compiler_params=_split_params(cid),
    )(_in_hbm(src), _in_hbm(lax.empty(land_shape, src.dtype)))
    return tuple(res[:4]), res[4]


def _sibling_copies(src_ref, land_ref, ssem, rsem, k0, groups, which, to):
    def copy(k, src, dst):
        return pltpu.make_async_remote_copy(
            src_ref=src, dst_ref=dst, send_sem=ssem.at[k], recv_sem=rsem.at[k], device_id=to, device_id_type=MESH)

    if groups == 0:
        return [copy(k0, src_ref, land_ref)]
    hw = src_ref.shape[1] // groups // 2
    return [copy(k0 + j, src_ref.at[:, pl.ds(pl.multiple_of((2 * j + which) * hw, LANES), hw)],
                 land_ref.at[:, j * hw:(j + 1) * hw]) for j in range(groups)]


def _to_sibling_start(items, *, name, cid):
    n = len(items)
    shapes = [a.shape if g == 0 else (a.shape[0], a.shape[1] // 2) for a, g in items]
    first = [sum(max(g, 1) for _, g in items[:k]) for k in range(n + 1)]

    def body(*refs):
        srcs, lands, ssem, rsem, token = refs[:n], refs[n:2 * n], refs[2 * n], refs[2 * n + 1], refs[-1]
        _handshake(_sibling())
        x, y, c, _ = _place()
        for k, (_, g) in enumerate(items):
            for cp in _sibling_copies(srcs[k], lands[k], ssem, rsem, first[k], g, 1 - c, (x, y, 1 - c)):
                cp.start()
        token[...] = jnp.zeros_like(token)

    res = pl.pallas_call(
        body, name=name, in_specs=[HBM] * (2 * n),
        out_specs=[SEM, SEM] + [HBM] * (2 * n) + [pl.BlockSpec(memory_space=pltpu.VMEM)],
        out_shape=[pltpu.SemaphoreType.DMA((first[n],)), pltpu.SemaphoreType.DMA((first[n],))]
        + [_hbm(a.shape, a.dtype) for a, _ in items] + [_hbm(s, a.dtype) for s, (a, _) in zip(shapes, items)]
        + [jax.ShapeDtypeStruct(TOKEN_SHAPE, F32)],
        input_output_aliases={k: 2 + k for k in range(2 * n)}, compiler_params=_split_params(cid),
    )(*[_in_hbm(a) for a, _ in items], *[_in_hbm(lax.empty(s, a.dtype)) for s, (a, _) in zip(shapes, items)])
    return [(res[0], res[1], first[k], g, res[2 + k], res[2 + n + k]) for k, (_, g) in enumerate(items)], res[-1]


def _from_sibling(flight, after, *, name):
    ssem, rsem, k0, groups, src, land = flight

    def wait(src_ref, land_ref, ssem_ref, rsem_ref):
        x, y, c, _ = _place()
        for cp in _sibling_copies(src_ref, land_ref, ssem_ref, rsem_ref, k0, groups, 1 - c, (x, y, c)):
            cp.wait_send()
            cp.wait_recv()

    return _wait_call(wait, (ssem, rsem, src, land), after, name=name)


def _dev_peers(x, y, c, chips):
    return [(x, y, 1 - c)] + [(cx, cy, c) for cx, cy in chips] + [(cx, cy, 1 - c) for cx, cy in chips]


def _dev_gather_start(part, *, name, cid):
    def start(src_ref, land_ref, ssem, rsem):
        x, y, c, chips = _place()
        for k, to in enumerate(_dev_peers(x, y, c, chips)):
            pltpu.make_async_remote_copy(
                src_ref=src_ref, dst_ref=land_ref.at[4 * x + 2 * y + c], send_sem=ssem.at[k], recv_sem=rsem.at[k],
                device_id=to, device_id_type=MESH).start()

    return _split_start(start, lambda: _dev_peers(*_place()), part, (N_DEV,) + part.shape, N_DEV - 1, name=name,
                        cid=cid)


def _dev_gather_wait(parts, after, *, name):
    def wait(src_ref, land_ref, ssem_ref, rsem_ref):
        x, y, c, chips = _place()
        for k, (px, py, pc) in enumerate(_dev_peers(x, y, c, chips)):
            cp = pltpu.make_async_remote_copy(
                src_ref=src_ref, dst_ref=land_ref.at[4 * px + 2 * py + pc], send_sem=ssem_ref.at[k],
                recv_sem=rsem_ref.at[k], device_id=(x, y, c), device_id_type=MESH)
            cp.wait_send()
            cp.wait_recv()

    return _wait_call(wait, parts, after, name=name)[1]


def _sibling_share_halves(arrays, *, name, cid):
    n = len(arrays)

    def body(*refs):
        bufs = refs[n:2 * n]
        send_sems, recv_sems = refs[2 * n:]
        _handshake(_sibling())
        x, y, c, _ = _place()
        copies = []
        for a in range(n):
            mine = bufs[a].at[:, _half_cols(bufs[a].shape[1], c)]
            cp = pltpu.make_async_remote_copy(
                src_ref=mine, dst_ref=mine, send_sem=send_sems.at[a], recv_sem=recv_sems.at[a],
                device_id=(x, y, 1 - c), device_id_type=MESH)
            cp.start()
            copies.append(cp)
        for a in range(n):
            theirs = bufs[a].at[:, _half_cols(bufs[a].shape[1], 1 - c)]
            pltpu.make_async_remote_copy(
                src_ref=theirs, dst_ref=theirs, send_sem=send_sems.at[a], recv_sem=recv_sems.at[a],
                device_id=(x, y, c), device_id_type=MESH).wait_recv()
        for cp in copies:
            cp.wait_send()

    return pl.pallas_call(
        body, name=name, in_specs=[HBM] * n, out_specs=[HBM] * n,
        out_shape=[jax.ShapeDtypeStruct(h.shape, h.dtype) for h in arrays],
        input_output_aliases={a: a for a in range(n)},
        scratch_shapes=[pltpu.SemaphoreType.DMA((n,)), pltpu.SemaphoreType.DMA((n,))],
        compiler_params=pltpu.CompilerParams(collective_id=cid),
    )(*arrays)


def _chip_sum_share(pair, landed, slots, into, *, part, name, cid, block_bytes=ELEMENTWISE_BLOCK_BYTES // 4):
    p, count = part
    _, r, hp = pair.shape
    tr, tc = _tile2d(r, hp, block_bytes, 16)
    n = r // tr
    assert tc == hp and n >= 2
    extra_specs, extra_args = ([], []) if into is None else ([pl.BlockSpec(memory_space=pl.ANY)], [into])

    def body(s_ref, own_ref, l0_ref, l1_ref, l2_ref, *rest):
        out_ref, buf, here_sems, send_sems, recv_sems = rest[-5:]
        i = pl.program_id(0)
        x, y, c, _chips = _place()
        me, sibling = (x, y, c), (x, y, 1 - c)

        def cols(which, first, width):
            return pl.ds(pl.multiple_of((which * count + first) * hp, LANES), width)

        def block_copies(j, which, to):
            dst = out_ref.at[pl.ds(j * tr, tr), cols(which, p, hp)]
            here = pltpu.make_async_copy(buf.at[j % 2], dst, here_sems.at[j % 2])
            there = pltpu.make_async_remote_copy(src_ref=buf.at[j % 2], dst_ref=dst, send_sem=send_sems.at[j % 2],
                                                 recv_sem=recv_sems.at[j], device_id=to, device_id_type=MESH)
            return here, there

        def finish(j):
            here, there = block_copies(j, c, sibling)
            here.wait()
            there.wait_send()

        @pl.when(i == 0)
        def _():
            _handshake([sibling])

        @pl.when(i >= 2)
        def _():
            finish(i - 2)

        buf[i % 2] = ((own_ref[...].astype(F32) + l0_ref[...].astype(F32)) + l1_ref[...].astype(F32)
                      ) + l2_ref[...].astype(F32)
        here, there = block_copies(i, c, sibling)
        here.start()
        there.start()

        @pl.when(i == n - 1)
        def _():
            finish(n - 2)
            finish(n - 1)
            for j in range(n):
                block_copies(j, 1 - c, me)[1].wait_recv()

    def slab(which):
        return pl.BlockSpec((None, tr, hp), lambda i, s: (s[which], i, 0))

    grid_spec = pltpu.PrefetchScalarGridSpec(
        num_scalar_prefetch=1, grid=(n,),
        in_specs=[slab(0), slab(1), slab(2), slab(3)] + extra_specs,
        out_specs=pl.BlockSpec(memory_space=pl.ANY),
        scratch_shapes=[pltpu.VMEM((2, tr, hp), F32), pltpu.SemaphoreType.DMA((2,)), pltpu.SemaphoreType.DMA((2,)),
                        pltpu.SemaphoreType.DMA((n,))])
    return pl.pallas_call(
        body, name=name, grid_spec=grid_spec, out_shape=jax.ShapeDtypeStruct((r, 2 * hp * count), F32),
        input_output_aliases={} if into is None else {5: 0},
        compiler_params=pltpu.CompilerParams(dimension_semantics=("arbitrary",), collective_id=cid,
                                             vmem_limit_bytes=VMEM_LIMIT),
    )(slots, pair, landed, landed, landed, *extra_args)


def _pack(arrays, rows_multiple=16, width=LANES):
    flat = jnp.concatenate([a.astype(F32).reshape(-1) for a in arrays])
    total = flat.shape[0]
    rows = -(-total // width)
    rows = -(-rows // rows_multiple) * rows_multiple
    return jnp.pad(flat, (0, rows * width - total)).reshape(rows, width)


def _unpack(buf, shapes):
    flat = buf.reshape(-1)
    out, off = [], 0
    for s in shapes:
        n = math.prod(s)
        out.append(flat[off:off + n].reshape(s))
        off += n
    return out


def kernel(x, norm_pre, norm_post, gla_w_in, gla_w_gate2, gla_b_gate, gla_o_gain, gla_w_out, sgu_w_in, sgu_ln_gain, sgu_ln_bias, sgu_w_spatial, sgu_b_spatial, sgu_w_out, loss_target, m_norm_pre, m_norm_post, m_gla_w_in, m_gla_w_gate2, m_gla_b_gate, m_gla_o_gain, m_gla_w_out, m_sgu_w_in, m_sgu_ln_gain, m_sgu_ln_bias, m_sgu_w_spatial, m_sgu_b_spatial, m_sgu_w_out, v_norm_pre, v_norm_post, v_gla_w_in, v_gla_w_gate2, v_gla_b_gate, v_gla_o_gain, v_gla_w_out, v_sgu_w_in, v_sgu_ln_gain, v_sgu_ln_bias, v_sgu_w_spatial, v_sgu_b_spatial, v_sgu_w_out):
    _, t, d = x.shape
    dk = d // 2
    ws = gla_w_in.shape[2]
    wp = -(-ws // LANES) * LANES
    lay = (ws, wp)
    chip =2 * lax.axis_index("x") + lax.axis_index("y")
    core = lax.axis_index("c")
    core_idx = core.astype(jnp.int32).reshape(1)
    others = jnp.arange(N_CHIPS - 1, dtype=jnp.int32)
    others = others + (others >= chip).astype(jnp.int32)
    slots = jnp.concatenate([chip.astype(jnp.int32).reshape(1), others, core_idx])

    x0 = x[0]
    target = loss_target[0]

    wt_in_g, mt_in_g, vt_in_g = gla_w_in[0].T, m_gla_w_in[0].T, v_gla_w_in[0].T

    small_shard = _pack([gla_w_gate2[0], sgu_ln_gain[0], sgu_ln_bias[0]], rows_multiple=8, width=2 * LANES)
    own = [small_shard, jnp.pad(wt_in_g.astype(BF16), ((0, wp - ws), (0, 0)))]
    in_flight, token = _gather_start(own, name="gather_start_a", cid=0, relayed=(1,))

    def with_sibling_and_own(mine, land, name, cid):
        return lax.dynamic_update_slice(_sibling_forward(land, name=name + "_share", cid=cid), mine[None],
                                        (chip, 0, 0))

    h0 = _norm_pre(x0, norm_pre[0:1] + token[0:1, 0:1], name="pre0")
    g_small = with_sibling_and_own(*_gather_wait(in_flight[0], h0, name="w_small_wait"), "w_small", 12)
    mine, land = _gather_wait(in_flight[1], [g_small, wt_in_g, mt_in_g, vt_in_g], name="w_gla_in_wait", ks=NEIGHBOURS)
    relay, token = _relay_start(land, name="w_gla_in_relay", cid=11)
    crossing, token = _forward_start(relay[2], name="w_gla_in_share_near", cid=22, ks=NEIGHBOURS)
    own_later = [(p[0] + token[0, 0]).astype(BF16) for p in (gla_w_out, sgu_w_in, sgu_w_out)]
    in_flight_later, token = _gather_start(own_later, name="gather_start_b", cid=1, after=[token], own_slab=chip)
    in_flight = in_flight + in_flight_later
    land = _relay_wait((relay[0], relay[1], crossing[2]), token, name="w_gla_in_relay_wait")
    land = _forward_wait((crossing[0], crossing[1], land), token, name="w_gla_in_share_near_wait", ks=NEIGHBOURS)
    land = _sibling_forward(land, name="w_gla_in_share_far", cid=13, ks=(2,))
    wt_g = lax.dynamic_update_slice(land, mine[None], (chip, 0, 0)).reshape(N_CHIPS * wp, d)

    def behind(small, token):
        return small + token[0:1, 0:1]

    def arriving(i, after, name):
        mine, land = _gather_wait(in_flight[i], after, name=name + "_wait")
        crossing, token = _forward_start(land, name=name + "_share", cid=i)
        return (mine, crossing), token

    def arrived(pending, after, name):
        _, crossing = pending
        return _forward_wait(crossing, after, name=name + "_share_wait")

    shard_shapes = [gla_w_gate2.shape[1:], sgu_ln_gain.shape[1:], sgu_ln_bias.shape[1:]]
    per_chip = [_unpack(g_small[j], shard_shapes) for j in range(N_CHIPS)]
    w2_full = jnp.concatenate([p[0] for p in per_chip], axis=1)
    ln_gain = jnp.concatenate([p[1] for p in per_chip], axis=0)[None, :]
    ln_bias = jnp.concatenate([p[2] for p in per_chip], axis=0)[None, :]
    w2p = jnp.pad(w2_full, ((0, LANES - GLA_GATE_RANK), (0, 0)))

    pos_chunk = jnp.arange(SGU_BLOCK) // CHUNK
    mask = pos_chunk[:, None] >= pos_chunk[None, :]
    ws_masked = jnp.where(mask[None], sgu_w_spatial[0], 0.0)
    ws_masked_t = ws_masked.transpose(0, 2, 1)
    bs_t = sgu_b_spatial[0].T

    proj0 = _matmul(h0, wt_g, mode="nt", out_dtype=F32, name="gla_in", tn=wp)
    pending, tok = arriving(2, proj0, "w_gla_out")
    o0, a0, s_before, s_final = _gla_fwd(proj0, w2p, behind(gla_b_gate, tok), gla_o_gain, lay, name="gla_scan")
    w_out_g = arrived(pending, a0, "w_gla_out").reshape(d, d)
    y0 = _matmul(a0, w_out_g, mode="nn", out_dtype=F32, name="gla_out", tn=1024)
    pending, tok = arriving(3, y0, "w_sgu_in")
    x1, h1 = _post_then_pre(x0, y0, behind(norm_post[0:1], tok), norm_pre[1:2], name="post0_pre1")
    g_wi_s = arrived(pending, h1, "w_sgu_in")
    pending, tok = arriving(4, g_wi_s, "w_sgu_out")
    proj1 = _matmul(h1, g_wi_s, mode="nn", out_dtype=F32, name="sgu_in", b_shards=True, after=tok, tn=768)
    a1 = _sgu_fwd(proj1, ln_gain, ln_bias, ws_masked, bs_t, name="sgu_gate")
    w_out_s = arrived(pending, a1, "w_sgu_out").reshape(d, d)
    acts, tok = _to_sibling_start([(a1, 0), (a0, 0), (h1, 0), (h0, 1)], name="acts_to_sibling", cid=5)
    a1, a0, h1, h0 = [f[4] for f in acts]
    y1 = _matmul(a1, w_out_s, mode="nn", out_dtype=F32, name="sgu_out", after=tok, tn=1024)
    loss_part, dx2, dy1, d_post1 = _loss_head(x1, y1, norm_post[1:2], target, name="loss_head")

    def pair_gradient(a_sent, b_sent, after, shards_on, name, cid):
        a_me, a_sib = _from_sibling(a_sent, after, name=name + "_a_wait")
        b_me, b_sib = _from_sibling(b_sent, [a_sib] + list(after), name=name + "_b_wait")
        pair = _matmul_dw_pair(a_me, a_sib, b_me, b_sib, core_idx, shards_on=shards_on,
                               name=name + "_pair")
        return _scatter_start(pair, name=name + "_start", cid=cid)

    def reduced(flight, after, name):
        pair, landed = _scatter_wait(flight, after, name=name + "_wait")
        return _chip_sum(pair, landed, slots, name=name + "_sum")

    (dy1_sent,), tok = _to_sibling_start([(dy1, 1)], name="dy1_to_sibling", cid=6)
    dy1 = dy1_sent[4]
    da1 = _matmul(dy1, w_out_s, mode="nt", out_dtype=F32, name="d_sgu_act", after=tok, tn=1024)
    fl_wo_s, tok = pair_gradient(acts[0], dy1_sent, [da1], "rows", "g_sgu_out", 15)
    dproj1, d_ws, d_bs_t, d_lg, d_lb = _sgu_bwd(da1, proj1, ln_gain, behind(ln_bias, tok), ws_masked, ws_masked_t,
                                                bs_t, name="sgu_gate_bwd")
    (dp1_sent,), tok = _to_sibling_start([(dproj1, N_CHIPS)], name="dproj1_to_sibling", cid=7)
    dproj1 = dp1_sent[4]
    dh1 = _matmul_nt_shards(dproj1, g_wi_s, out_dtype=F32, name="d_sgu_h", after=tok)
    fl_wi_s, tok = pair_gradient(acts[2], dp1_sent, [dh1], "cols", "g_sgu_in", 16)
    dx1, dy0, d_pre1, d_post0 = _mid_bwd(dx2, dh1, x1, behind(norm_pre[1:2], tok), y0, norm_post[0:1],
                                         name="pre1_post0_bwd")
    (dy0_sent,), tok = _to_sibling_start([(dy0, 1)], name="dy0_to_sibling", cid=8)
    dy0 = dy0_sent[4]
    da0 = _matmul(dy0, w_out_g, mode="nt", out_dtype=F32, name="d_gla_act", after=tok, tn=1024)
    fl_wo_g, tok = pair_gradient(acts[1], dy0_sent, [da0], "rows", "g_gla_out", 17)
    dproj0, d_og, d_bg, d_w2p = _gla_bwd(da0, o0, proj0, w2p, behind(gla_b_gate, tok), gla_o_gain, s_before, s_final,
                                         lay, name="gla_scan_bwd")
    early_shapes = [norm_post.shape, gla_b_gate.shape, gla_o_gain.shape, sgu_w_spatial.shape, sgu_b_spatial.shape,
                    (1, GLA_GATE_RANK, dk), (1, d), (1, d), (1, LANES)]
    early_part = _pack([jnp.concatenate([d_post0, d_post1], axis=0), d_bg, d_og, jnp.where(mask[None], d_ws, 0.0)[None],
                        d_bs_t.T[None], d_w2p[:GLA_GATE_RANK][None], d_lg, d_lb, loss_part])
    early_flight, tok = _dev_gather_start(early_part, name="small_early_start", cid=20)
    (dp0_sent,), tok_sent = _to_sibling_start([(dproj0, 0)], name="dproj0_to_sibling", cid=9)
    dproj0 = dp0_sent[4]
    dh0 = _matmul(dproj0, wt_g, mode="nn", out_dtype=F32, name="d_gla_h", after=tok_sent)
    a_me, a_sib = _from_sibling(dp0_sent, [dh0, tok], name="g_gla_in_a_wait")
    b_me, b_sib = _from_sibling(acts[3], [a_sib, dh0], name="g_gla_in_b_wait")
    fl_wi_g, tok_scatter = [], None
    for p in range(2):
        pair = _matmul_dw_pair(a_me, a_sib, b_me, b_sib, core_idx, shards_on="rows", part=(p, 2),
                               name=f"g_gla_in_pair{p}", after=tok_scatter)
        flight, tok_scatter = _scatter_start(pair, name=f"g_gla_in_start{p}", cid=18 + p)
        fl_wi_g.append(flight)
    r_wo_s = reduced(fl_wo_s, tok_scatter, "g_sgu_out")
    r_wi_s = reduced(fl_wi_s, r_wo_s, "g_sgu_in")
    r_wo_g = reduced(fl_wo_g, r_wi_s, "g_gla_out")
    sharing, tok = _share_start([r_wo_s, r_wi_s, r_wo_g], name="grads_share_a", cid=10)
    grad_x, d_pre0 = _first_bwd(dx1, dh0, x0, behind(norm_pre[0:1], tok), name="pre0_bwd")

    late_part = _pack([jnp.concatenate([d_pre0, d_pre1], axis=0)])
    late_flight, tok = _dev_gather_start(late_part, name="small_late_start", cid=21)

    def big_update(w, g, m, v, name, after=None):
        return [u[None] for u in _adamw(w[0], g, m[0], v[0], name=name, after=after)]

    g_wo_sgu, g_wi_sgu, g_wo_gla = _share_wait(sharing, [grad_x, tok], name="grads_share_a_wait")
    u_wi_sgu = big_update(sgu_w_in, g_wi_sgu, m_sgu_w_in, v_sgu_w_in, "adamw_sgu_w_in")
    u_wo_gla = big_update(gla_w_out, g_wo_gla, m_gla_w_out, v_gla_w_out, "adamw_gla_w_out", after=u_wi_sgu[1])

    r_wi_g, behind_this = None, u_wo_gla[1]
    for p, flight in enumerate(fl_wi_g):
        pair, landed = _scatter_wait(flight, behind_this, name=f"g_gla_in_wait{p}")
        r_wi_g = behind_this = _chip_sum_share(pair, landed, slots, r_wi_g, part=(p, 2),
                                               name=f"g_gla_in_sum_share{p}", cid=(23, 14)[p])
    gt_wi_gla = r_wi_g
    u_wi_gla_t = _adamw(wt_in_g, gt_wi_gla, mt_in_g, vt_in_g, name="adamw_gla_w_in")
    u_wi_gla = [u.T[None] for u in u_wi_gla_t]
    u_wo_sgu = big_update(sgu_w_out, g_wo_sgu, m_sgu_w_out, v_sgu_w_out, "adamw_sgu_w_out", after=u_wi_gla_t[1])

    def summed_over_devices(part, flight, after, shapes, name):
        land = _dev_gather_wait(flight, after, name=name + "_wait")
        every = lax.dynamic_update_slice(land, part[None], (2 * chip + core, 0, 0))
        return _unpack(_stack_sum(every, name=name + "_sum"), shapes)

    (g_post, g_bg, g_og, g_wsp, g_bsp, g_w2_full, g_lg_full, g_lb_full, loss_vec) = summed_over_devices(
        early_part, early_flight, u_wo_sgu[1], early_shapes, "small_early")
    g_pre, = summed_over_devices(late_part, late_flight, loss_vec, [norm_pre.shape], "small_late")
    loss = loss_vec[0, 0]
    g_w2 = lax.dynamic_slice_in_dim(g_w2_full, chip * (dk // N_CHIPS), dk // N_CHIPS, axis=2)
    g_lg = lax.dynamic_slice_in_dim(g_lg_full, chip * (d // N_CHIPS), d // N_CHIPS, axis=1)
    g_lb = lax.dynamic_slice_in_dim(g_lb_full, chip * (d // N_CHIPS), d // N_CHIPS, axis=1)

    small_w = [norm_pre, norm_post, gla_b_gate, gla_o_gain, sgu_w_spatial, sgu_b_spatial, gla_w_gate2, sgu_ln_gain,
               sgu_ln_bias]
    small_g = [g_pre, g_post, g_bg, g_og, g_wsp, g_bsp, g_w2, g_lg, g_lb]
    small_m = [m_norm_pre, m_norm_post, m_gla_b_gate, m_gla_o_gain, m_sgu_w_spatial, m_sgu_b_spatial, m_gla_w_gate2,
               m_sgu_ln_gain, m_sgu_ln_bias]
    small_v = [v_norm_pre, v_norm_post, v_gla_b_gate, v_gla_o_gain, v_sgu_w_spatial, v_sgu_b_spatial, v_gla_w_gate2,
               v_sgu_ln_gain, v_sgu_ln_bias]
    own_shapes = [w.shape for w in small_w]
    _, s_dl, s_m, s_v = _adamw(_pack(small_w), _pack(small_g), _pack(small_m), _pack(small_v), name="adamw_small")
    dl_s, m_s, v_s = _unpack(s_dl, own_shapes), _unpack(s_m, own_shapes), _unpack(s_v, own_shapes)

    def ordered(small, kind):
        pre, post, bg, og, wsp, bsp, w2, lg, lb = small
        return [pre, post, u_wi_gla[kind], w2, bg, og, u_wo_gla[kind], u_wi_sgu[kind], lg, lb, wsp, bsp, u_wo_sgu[kind]]

    return (loss, grad_x[None], *ordered(small_g, 0), *ordered(dl_s, 1), *ordered(m_s, 2), *ordered(v_s, 3))
```
